```python
import math
import jax, jax.numpy as jnp
from jax import lax
import numpy as np

D_MODEL = 1024
BATCH = 8
SEQ = 4096
DEPTH = 1

N_META = 16
CHUNK = 128
Q_BLOCK = 128
PAD = CHUNK - N_META
D_SSD = 2 * D_MODEL
SSD_HEAD_DIM = 64
H_SSD = D_SSD // SSD_HEAD_DIM
SSD_GROUPS = 4
D_STATE = 128
CONV_K = 4
CONV_DIM = D_SSD + 2 * SSD_GROUPS * D_STATE
H_ATT = 16
ATT_HEAD_DIM = 64
D_ATT = H_ATT * ATT_HEAD_DIM
N_COLS = D_SSD + CONV_DIM + H_SSD + D_ATT + 3 * D_ATT + H_ATT + 2 * D_MODEL
EPS = 1e-6

kernel_name = "hybrid_ssd_fox_gated_merge"


def rmsnorm(x, g):
    xf = x.astype(jnp.float32)
    y = xf * lax.rsqrt(jnp.mean(xf * xf, axis=-1, keepdims=True) + EPS)
    return (y * g.astype(jnp.float32)).astype(x.dtype)


def gated_group_rmsnorm(y, z, g):
    u = (y * jax.nn.silu(z)).astype(jnp.float32)
    shp = u.shape
    u = u.reshape(shp[:-1] + (SSD_GROUPS, shp[-1] // SSD_GROUPS))
    u = u * lax.rsqrt(jnp.mean(u * u, axis=-1, keepdims=True) + EPS)
    return (u.reshape(shp) * g.astype(jnp.float32)).astype(y.dtype)


def causal_depthwise_conv(u, w, b):
    C = u.shape[-1]
    out = lax.conv_general_dilated(u, w[:, None, :].astype(u.dtype), window_strides=(1,),
                                   padding=[(CONV_K - 1, 0)],
                                   dimension_numbers=("NWC", "WIO", "NWC"),
                                   feature_group_count=C)
    return out + b


def ssd_chunked(xh, dt, a, bmat, cmat):
    Bsz, Lp, H, P = xh.shape
    G, N = bmat.shape[-2:]
    R = H // G
    nc = Lp // CHUNK
    xdt = (xh.astype(jnp.float32) * dt[..., None]).reshape(Bsz, nc, CHUNK, G, R, P)
    adt = (dt * a).reshape(Bsz, nc, CHUNK, G, R).transpose(0, 1, 3, 4, 2)
    a_cs = jnp.cumsum(adt, axis=-1)
    bm = bmat.astype(jnp.float32).reshape(Bsz, nc, CHUNK, G, N)
    cm = cmat.astype(jnp.float32).reshape(Bsz, nc, CHUNK, G, N)
    causal = jnp.tril(jnp.ones((CHUNK, CHUNK), dtype=bool))
    seg = a_cs[..., :, None] - a_cs[..., None, :]
    decay = jnp.exp(jnp.where(causal, seg, -jnp.inf))
    cb = jnp.einsum("bclgn,bcsgn->bcgls", cm, bm)
    y_diag = jnp.einsum("bcgls,bcgrls,bcsgrp->bclgrp", cb, decay, xdt)
    decay_states = jnp.exp(a_cs[..., -1:] - a_cs)
    states = jnp.einsum("bclgn,bcgrl,bclgrp->bcgrpn", bm, decay_states, xdt)
    chunk_decay = jnp.exp(a_cs[..., -1])

    def step(h, inp):
        s, d = inp
        return h * d[..., None, None] + s, h

    h0 = jnp.zeros((Bsz, G, R, P, N), jnp.float32)
    _, h_in = lax.scan(step, h0, (jnp.swapaxes(states, 0, 1), jnp.swapaxes(chunk_decay, 0, 1)))
    h_in = jnp.swapaxes(h_in, 0, 1)
    y_off = jnp.einsum("bclgn,bcgrpn,bcgrl->bclgrp", cm, h_in, jnp.exp(a_cs))
    return (y_diag + y_off).reshape(Bsz, Lp, H, P)


def forgetting_attention(q, k, v, logf):
    Bsz, Lp, H, Dh = q.shape
    scale = 1.0 / math.sqrt(Dh)
    c = jnp.cumsum(logf, axis=1)
    c_k = jnp.transpose(c, (0, 2, 1))
    nb = Lp // Q_BLOCK
    qb = q.reshape(Bsz, nb, Q_BLOCK, H, Dh).transpose(1, 0, 2, 3, 4)
    cqb = c.reshape(Bsz, nb, Q_BLOCK, H).transpose(1, 0, 3, 2)
    kpos = jnp.arange(Lp)

    def block(args):
        i, qi, cqi = args
        qpos = i * Q_BLOCK + jnp.arange(Q_BLOCK)
        s = jnp.einsum("bthd,bshd->bhts", qi, k).astype(jnp.float32) * scale
        s = s + (cqi[..., :, None] - c_k[..., None, :])
        s = jnp.where(kpos[None, :] <= qpos[:, None], s, -jnp.inf)
        p = jax.nn.softmax(s, axis=-1).astype(v.dtype)
        return jnp.einsum("bhts,bshd->bthd", p, v)

    out = lax.map(block, (jnp.arange(nb), qb, cqb))
    return out.transpose(1, 0, 2, 3, 4).reshape(Bsz, Lp, H, Dh)


def hybrid_layer(h, norm_pre, w_in, conv_w, conv_b, dt_bias, a_log, d_skip, ssd_norm,
                 fgate_bias, gate_bias, w_proj_ssd, w_proj_att, w_out, norm_post):
    Bsz, L, _ = h.shape
    Lp = L + PAD
    u = rmsnorm(h, norm_pre)
    proj = u @ w_in
    cuts = [D_SSD, D_SSD + CONV_DIM, D_SSD + CONV_DIM + H_SSD,
            D_SSD + CONV_DIM + H_SSD + D_ATT,
            D_SSD + CONV_DIM + H_SSD + 2 * D_ATT,
            D_SSD + CONV_DIM + H_SSD + 3 * D_ATT,
            D_SSD + CONV_DIM + H_SSD + 4 * D_ATT,
            D_SSD + CONV_DIM + H_SSD + 4 * D_ATT + H_ATT]
    z_ssd, xbc, dt_raw, z_att, q, k, v, f_raw, g_raw = jnp.split(proj, cuts, axis=-1)

    xbc = jax.nn.silu(causal_depthwise_conv(xbc, conv_w, conv_b))
    xs, bm, cm = jnp.split(xbc, [D_SSD, D_SSD + SSD_GROUPS * D_STATE], axis=-1)
    dt = jax.nn.softplus(dt_raw.astype(jnp.float32) + dt_bias.astype(jnp.float32))
    a = -jnp.exp(a_log.astype(jnp.float32))
    front = ((0, 0), (PAD, 0), (0, 0))
    xs_p = jnp.pad(xs, front).reshape(Bsz, Lp, H_SSD, SSD_HEAD_DIM)
    bm_p = jnp.pad(bm, front).reshape(Bsz, Lp, SSD_GROUPS, D_STATE)
    cm_p = jnp.pad(cm, front).reshape(Bsz, Lp, SSD_GROUPS, D_STATE)
    dt_p = jnp.pad(dt, front)
    y = ssd_chunked(xs_p, dt_p, a, bm_p, cm_p)
    y = y + d_skip.astype(jnp.float32)[:, None] * xs_p.astype(jnp.float32)
    y = y[:, PAD:].reshape(Bsz, L, D_SSD).astype(h.dtype)
    y_ssd = gated_group_rmsnorm(y, z_ssd, ssd_norm)

    logf = jax.nn.log_sigmoid(f_raw.astype(jnp.float32) + fgate_bias.astype(jnp.float32))
    back = ((0, 0), (0, PAD), (0, 0))
    qh = jnp.pad(q, back).reshape(Bsz, Lp, H_ATT, ATT_HEAD_DIM)
    kh = jnp.pad(k, back).reshape(Bsz, Lp, H_ATT, ATT_HEAD_DIM)
    vh = jnp.pad(v, back).reshape(Bsz, Lp, H_ATT, ATT_HEAD_DIM)
    o = forgetting_attention(qh, kh, vh, jnp.pad(logf, back))
    o = o[:, :L].reshape(Bsz, L, D_ATT)
    y_att = o * jax.nn.silu(z_att)

    gates = jax.nn.sigmoid(g_raw + gate_bias)
    g_ssd, g_att = jnp.split(gates, 2, axis=-1)
    merged = g_ssd * (y_ssd @ w_proj_ssd) + g_att * (y_att @ w_proj_att)
    return h + rmsnorm(merged @ w_out, norm_post)


def _fwd_setup_inputs(seed: int = 0) -> dict:
    key = jax.random.key(seed)
    ks = jax.random.split(key, 16)
    D = D_MODEL
    nrm = jax.random.normal
    x = nrm(ks[0], (BATCH, SEQ, D), jnp.float32)
    meta_tokens = nrm(ks[1], (N_META, D), jnp.float32)
    norm_pre = 1.0 + 0.05 * nrm(ks[2], (DEPTH, D), jnp.float32)
    w_in = nrm(ks[3], (DEPTH, D, N_COLS), jnp.float32) * D ** -0.5
    conv_w = jax.random.uniform(ks[4], (DEPTH, CONV_K, CONV_DIM), jnp.float32, -0.5, 0.5)
    conv_b = 0.05 * nrm(ks[5], (DEPTH, CONV_DIM), jnp.float32)
    dt0 = jnp.exp(jax.random.uniform(ks[6], (DEPTH, H_SSD), jnp.float32,
                                     math.log(1e-3), math.log(1e-1)))
    dt_bias = dt0 + jnp.log(-jnp.expm1(-dt0))
    a_log = jnp.log(jax.random.uniform(ks[7], (DEPTH, H_SSD), jnp.float32, 1.0, 16.0))
    d_skip = 1.0 + 0.1 * nrm(ks[8], (DEPTH, H_SSD), jnp.float32)
    ssd_norm = 1.0 + 0.05 * nrm(ks[9], (DEPTH, D_SSD), jnp.float32)
    fgate_bias = jax.random.uniform(ks[10], (DEPTH, H_ATT), jnp.float32, 1.0, 6.0)
    gate_bias = 0.1 * nrm(ks[11], (DEPTH, 2 * D), jnp.float32)
    w_proj_ssd = nrm(ks[12], (DEPTH, D_SSD, D), jnp.float32) * D_SSD ** -0.5
    w_proj_att = nrm(ks[13], (DEPTH, D_ATT, D), jnp.float32) * D_ATT ** -0.5
    w_out = nrm(ks[14], (DEPTH, D, D), jnp.float32) * D ** -0.5
    norm_post = 1.0 + 0.05 * nrm(ks[15], (DEPTH, D), jnp.float32)
    return {"x": x, "meta_tokens": meta_tokens, "norm_pre": norm_pre, "w_in": w_in,
            "conv_w": conv_w, "conv_b": conv_b, "dt_bias": dt_bias, "a_log": a_log,
            "d_skip": d_skip, "ssd_norm": ssd_norm, "fgate_bias": fgate_bias,
            "gate_bias": gate_bias, "w_proj_ssd": w_proj_ssd, "w_proj_att": w_proj_att,
            "w_out": w_out, "norm_post": norm_post}


def _fwd_reference(x, meta_tokens, norm_pre, w_in, conv_w, conv_b, dt_bias, a_log, d_skip,
              ssd_norm, fgate_bias, gate_bias, w_proj_ssd, w_proj_att, w_out, norm_post):
    Bsz = x.shape[0]
    meta = jnp.broadcast_to(meta_tokens[None].astype(x.dtype), (Bsz, N_META, D_MODEL))
    h = jnp.concatenate([meta, x], axis=1)
    for i in range(DEPTH):
        h = hybrid_layer(h, norm_pre[i], w_in[i], conv_w[i], conv_b[i], dt_bias[i], a_log[i],
                         d_skip[i], ssd_norm[i], fgate_bias[i], gate_bias[i], w_proj_ssd[i],
                         w_proj_att[i], w_out[i], norm_post[i])
    return h[:, N_META:]


import jax as _jax
import jax.numpy as _jnp

TWIN_FORMAT = 'train_step'
FWD_PARAMS = ['x', 'meta_tokens', 'norm_pre', 'w_in', 'conv_w', 'conv_b', 'dt_bias', 'a_log', 'd_skip', 'ssd_norm', 'fgate_bias', 'gate_bias', 'w_proj_ssd', 'w_proj_att', 'w_out', 'norm_post']
TWIN_WEIGHTS = ['meta_tokens', 'norm_pre', 'w_in', 'conv_w', 'conv_b', 'dt_bias', 'a_log', 'd_skip', 'ssd_norm', 'fgate_bias', 'gate_bias', 'w_proj_ssd', 'w_proj_att', 'w_out', 'norm_post']
TWIN_DIFF_INPUT = 'x'
TWIN_INPUTS = ['x', 'meta_tokens', 'norm_pre', 'w_in', 'conv_w', 'conv_b', 'dt_bias', 'a_log', 'd_skip', 'ssd_norm', 'fgate_bias', 'gate_bias', 'w_proj_ssd', 'w_proj_att', 'w_out', 'norm_post', 'loss_target', 'm_meta_tokens', 'm_norm_pre', 'm_w_in', 'm_conv_w', 'm_conv_b', 'm_dt_bias', 'm_a_log', 'm_d_skip', 'm_ssd_norm', 'm_fgate_bias', 'm_gate_bias', 'm_w_proj_ssd', 'm_w_proj_att', 'm_w_out', 'm_norm_post', 'v_meta_tokens', 'v_norm_pre', 'v_w_in', 'v_conv_w', 'v_conv_b', 'v_dt_bias', 'v_a_log', 'v_d_skip', 'v_ssd_norm', 'v_fgate_bias', 'v_gate_bias', 'v_w_proj_ssd', 'v_w_proj_att', 'v_w_out', 'v_norm_post']
TWIN_OUTPUTS = ['loss', 'grad_x', 'grad_meta_tokens', 'grad_norm_pre', 'grad_w_in', 'grad_conv_w', 'grad_conv_b', 'grad_dt_bias', 'grad_a_log', 'grad_d_skip', 'grad_ssd_norm', 'grad_fgate_bias', 'grad_gate_bias', 'grad_w_proj_ssd', 'grad_w_proj_att', 'grad_w_out', 'grad_norm_post', 'delta_meta_tokens', 'delta_norm_pre', 'delta_w_in', 'delta_conv_w', 'delta_conv_b', 'delta_dt_bias', 'delta_a_log', 'delta_d_skip', 'delta_ssd_norm', 'delta_fgate_bias', 'delta_gate_bias', 'delta_w_proj_ssd', 'delta_w_proj_att', 'delta_w_out', 'delta_norm_post', 'new_m_meta_tokens', 'new_m_norm_pre', 'new_m_w_in', 'new_m_conv_w', 'new_m_conv_b', 'new_m_dt_bias', 'new_m_a_log', 'new_m_d_skip', 'new_m_ssd_norm', 'new_m_fgate_bias', 'new_m_gate_bias', 'new_m_w_proj_ssd', 'new_m_w_proj_att', 'new_m_w_out', 'new_m_norm_post', 'new_v_meta_tokens', 'new_v_norm_pre', 'new_v_w_in', 'new_v_conv_w', 'new_v_conv_b', 'new_v_dt_bias', 'new_v_a_log', 'new_v_d_skip', 'new_v_ssd_norm', 'new_v_fgate_bias', 'new_v_gate_bias', 'new_v_w_proj_ssd', 'new_v_w_proj_att', 'new_v_w_out', 'new_v_norm_post']
TWIN_LEAF_KINDS = {'loss': 'loss', 'grad_x': 'grad_x', 'grad_meta_tokens': 'grad_w', 'grad_norm_pre': 'grad_w', 'grad_w_in': 'grad_w', 'grad_conv_w': 'grad_w', 'grad_conv_b': 'grad_w', 'grad_dt_bias': 'grad_w', 'grad_a_log': 'grad_w', 'grad_d_skip': 'grad_w', 'grad_ssd_norm': 'grad_w', 'grad_fgate_bias': 'grad_w', 'grad_gate_bias': 'grad_w', 'grad_w_proj_ssd': 'grad_w', 'grad_w_proj_att': 'grad_w', 'grad_w_out': 'grad_w', 'grad_norm_post': 'grad_w', 'delta_meta_tokens': 'delta_w', 'delta_norm_pre': 'delta_w', 'delta_w_in': 'delta_w', 'delta_conv_w': 'delta_w', 'delta_conv_b': 'delta_w', 'delta_dt_bias': 'delta_w', 'delta_a_log': 'delta_w', 'delta_d_skip': 'delta_w', 'delta_ssd_norm': 'delta_w', 'delta_fgate_bias': 'delta_w', 'delta_gate_bias': 'delta_w', 'delta_w_proj_ssd': 'delta_w', 'delta_w_proj_att': 'delta_w', 'delta_w_out': 'delta_w', 'delta_norm_post': 'delta_w', 'new_m_meta_tokens': 'new_m', 'new_m_norm_pre': 'new_m', 'new_m_w_in': 'new_m', 'new_m_conv_w': 'new_m', 'new_m_conv_b': 'new_m', 'new_m_dt_bias': 'new_m', 'new_m_a_log': 'new_m', 'new_m_d_skip': 'new_m', 'new_m_ssd_norm': 'new_m', 'new_m_fgate_bias': 'new_m', 'new_m_gate_bias': 'new_m', 'new_m_w_proj_ssd': 'new_m', 'new_m_w_proj_att': 'new_m', 'new_m_w_out': 'new_m', 'new_m_norm_post': 'new_m', 'new_v_meta_tokens': 'new_v', 'new_v_norm_pre': 'new_v', 'new_v_w_in': 'new_v', 'new_v_conv_w': 'new_v', 'new_v_conv_b': 'new_v', 'new_v_dt_bias': 'new_v', 'new_v_a_log': 'new_v', 'new_v_d_skip': 'new_v', 'new_v_ssd_norm': 'new_v', 'new_v_fgate_bias': 'new_v', 'new_v_gate_bias': 'new_v', 'new_v_w_proj_ssd': 'new_v', 'new_v_w_proj_att': 'new_v', 'new_v_w_out': 'new_v', 'new_v_norm_post': 'new_v'}


def _forward(args):
    return _fwd_reference(*[args[k] for k in FWD_PARAMS])


def _output_shape():
    out = _jax.eval_shape(lambda: _forward(_fwd_setup_inputs(0)))
    return out.shape, out.dtype

N_MICROBATCH = 1
ADAM_LR = 0.001
ADAM_B1 = 0.9
ADAM_B2 = 0.999
ADAM_EPS = 1e-08
ADAM_WD = 0.01
ADAM_STEP = 10
PER_EXAMPLE_BATCH_AXIS = {'x': 0, 'loss_target': 0}
SHARED_INPUTS = []
_WEIGHT_DTYPES = {'meta_tokens': _jnp.float32, 'norm_pre': _jnp.float32, 'w_in': _jnp.float32, 'conv_w': _jnp.float32, 'conv_b': _jnp.float32, 'dt_bias': _jnp.float32, 'a_log': _jnp.float32, 'd_skip': _jnp.float32, 'ssd_norm': _jnp.float32, 'fgate_bias': _jnp.float32, 'gate_bias': _jnp.float32, 'w_proj_ssd': _jnp.float32, 'w_proj_att': _jnp.float32, 'w_out': _jnp.float32, 'norm_post': _jnp.float32}
MOMENT_SCALE = {'meta_tokens': 6.052413e-03, 'norm_pre': 3.856263e-01, 'w_in': 1.239238e-01, 'conv_w': 2.867880e-01, 'conv_b': 5.780744e-01, 'dt_bias': 1.962016e-01, 'a_log': 2.203945e-01, 'd_skip': 1.217739e+00, 'ssd_norm': 2.043672e-01, 'fgate_bias': 3.217772e-01, 'gate_bias': 8.775647e-02, 'w_proj_ssd': 2.915138e-01, 'w_proj_att': 4.448441e-02, 'w_out': 2.996008e-01, 'norm_post': 3.200987e+01}


def _to_microbatches(a, axis):
    t = _jnp.moveaxis(a, axis, 0)
    t = t.reshape((N_MICROBATCH, t.shape[0] // N_MICROBATCH) + t.shape[1:])
    return _jnp.moveaxis(t, 1, axis + 1)


def setup_inputs(seed: int = 0) -> dict:
    inp = _fwd_setup_inputs(seed)
    key = _jax.random.fold_in(_jax.random.key(seed), 7919)
    shape, _ = _output_shape()
    out = dict(inp)
    out["loss_target"] = _jax.random.normal(_jax.random.fold_in(key, 0), shape, _jnp.float32)
    for i, name in enumerate(TWIN_WEIGHTS):
        w = inp[name].astype(_jnp.float32)
        if MOMENT_SCALE is None:
            s = _jnp.sqrt(_jnp.mean(_jnp.square(w)) + 1e-30)
        else:
            s = MOMENT_SCALE[name]
        km, kv = _jax.random.split(_jax.random.fold_in(key, i + 1))
        out[name] = w
        out["m_" + name] = s * _jax.random.normal(km, w.shape, _jnp.float32)
        out["v_" + name] = (s * s) * _jax.random.uniform(kv, w.shape, _jnp.float32, 0.5, 1.5)
    if N_MICROBATCH > 1:
        for name, axis in PER_EXAMPLE_BATCH_AXIS.items():
            out[name] = _to_microbatches(out[name], axis)
    return {'x': out['x'], 'meta_tokens': out['meta_tokens'], 'norm_pre': out['norm_pre'], 'w_in': out['w_in'], 'conv_w': out['conv_w'], 'conv_b': out['conv_b'], 'dt_bias': out['dt_bias'], 'a_log': out['a_log'], 'd_skip': out['d_skip'], 'ssd_norm': out['ssd_norm'], 'fgate_bias': out['fgate_bias'], 'gate_bias': out['gate_bias'], 'w_proj_ssd': out['w_proj_ssd'], 'w_proj_att': out['w_proj_att'], 'w_out': out['w_out'], 'norm_post': out['norm_post'], 'loss_target': out['loss_target'], 'm_meta_tokens': out['m_meta_tokens'], 'm_norm_pre': out['m_norm_pre'], 'm_w_in': out['m_w_in'], 'm_conv_w': out['m_conv_w'], 'm_conv_b': out['m_conv_b'], 'm_dt_bias': out['m_dt_bias'], 'm_a_log': out['m_a_log'], 'm_d_skip': out['m_d_skip'], 'm_ssd_norm': out['m_ssd_norm'], 'm_fgate_bias': out['m_fgate_bias'], 'm_gate_bias': out['m_gate_bias'], 'm_w_proj_ssd': out['m_w_proj_ssd'], 'm_w_proj_att': out['m_w_proj_att'], 'm_w_out': out['m_w_out'], 'm_norm_post': out['m_norm_post'], 'v_meta_tokens': out['v_meta_tokens'], 'v_norm_pre': out['v_norm_pre'], 'v_w_in': out['v_w_in'], 'v_conv_w': out['v_conv_w'], 'v_conv_b': out['v_conv_b'], 'v_dt_bias': out['v_dt_bias'], 'v_a_log': out['v_a_log'], 'v_d_skip': out['v_d_skip'], 'v_ssd_norm': out['v_ssd_norm'], 'v_fgate_bias': out['v_fgate_bias'], 'v_gate_bias': out['v_gate_bias'], 'v_w_proj_ssd': out['v_w_proj_ssd'], 'v_w_proj_att': out['v_w_proj_att'], 'v_w_out': out['v_w_out'], 'v_norm_post': out['v_norm_post']}


def _loss(weights, diff, rest, loss_target):
    with _jax.named_scope("forward"):
        args = {**rest, TWIN_DIFF_INPUT: diff, **{k: w.astype(_WEIGHT_DTYPES[k]) for k, w in weights.items()}}
        y = _forward(args)
    with _jax.named_scope("loss_head"):
        err = _jnp.square(y.astype(_jnp.float32) - loss_target)
        return 0.5 * _jnp.sum(_jnp.mean(err, axis=-1)) if err.ndim else 0.5 * err


def _adamw(w, g, m, v):
    m = ADAM_B1 * m + (1.0 - ADAM_B1) * g
    v = ADAM_B2 * v + (1.0 - ADAM_B2) * _jnp.square(g)
    m_hat = m / (1.0 - ADAM_B1 ** ADAM_STEP)
    v_hat = v / (1.0 - ADAM_B2 ** ADAM_STEP)
    delta = -ADAM_LR * (m_hat / (_jnp.sqrt(v_hat) + ADAM_EPS) + ADAM_WD * w)
    return delta, m, v


def reference(x, meta_tokens, norm_pre, w_in, conv_w, conv_b, dt_bias, a_log, d_skip, ssd_norm, fgate_bias, gate_bias, w_proj_ssd, w_proj_att, w_out, norm_post, loss_target, m_meta_tokens, m_norm_pre, m_w_in, m_conv_w, m_conv_b, m_dt_bias, m_a_log, m_d_skip, m_ssd_norm, m_fgate_bias, m_gate_bias, m_w_proj_ssd, m_w_proj_att, m_w_out, m_norm_post, v_meta_tokens, v_norm_pre, v_w_in, v_conv_w, v_conv_b, v_dt_bias, v_a_log, v_d_skip, v_ssd_norm, v_fgate_bias, v_gate_bias, v_w_proj_ssd, v_w_proj_att, v_w_out, v_norm_post):
    given = dict(x=x, meta_tokens=meta_tokens, norm_pre=norm_pre, w_in=w_in, conv_w=conv_w, conv_b=conv_b, dt_bias=dt_bias, a_log=a_log, d_skip=d_skip, ssd_norm=ssd_norm, fgate_bias=fgate_bias, gate_bias=gate_bias, w_proj_ssd=w_proj_ssd, w_proj_att=w_proj_att, w_out=w_out, norm_post=norm_post, loss_target=loss_target, m_meta_tokens=m_meta_tokens, m_norm_pre=m_norm_pre, m_w_in=m_w_in, m_conv_w=m_conv_w, m_conv_b=m_conv_b, m_dt_bias=m_dt_bias, m_a_log=m_a_log, m_d_skip=m_d_skip, m_ssd_norm=m_ssd_norm, m_fgate_bias=m_fgate_bias, m_gate_bias=m_gate_bias, m_w_proj_ssd=m_w_proj_ssd, m_w_proj_att=m_w_proj_att, m_w_out=m_w_out, m_norm_post=m_norm_post, v_meta_tokens=v_meta_tokens, v_norm_pre=v_norm_pre, v_w_in=v_w_in, v_conv_w=v_conv_w, v_conv_b=v_conv_b, v_dt_bias=v_dt_bias, v_a_log=v_a_log, v_d_skip=v_d_skip, v_ssd_norm=v_ssd_norm, v_fgate_bias=v_fgate_bias, v_gate_bias=v_gate_bias, v_w_proj_ssd=v_w_proj_ssd, v_w_proj_att=v_w_proj_att, v_w_out=v_w_out, v_norm_post=v_norm_post)
    weights = {n: given[n] for n in TWIN_WEIGHTS}
    shared = {n: given[n] for n in SHARED_INPUTS}
    per_example = {n: given[n] for n in ['x']}
    grad_fn = _jax.value_and_grad(_loss, argnums=(0, 1))

    def one_microbatch(ex, loss_target):
        ex = dict(ex)
        diff = ex.pop(TWIN_DIFF_INPUT)
        return grad_fn(weights, diff, {**shared, **ex}, loss_target)

    if N_MICROBATCH == 1:
        loss, (grad_w, grad_x) = one_microbatch(per_example, given["loss_target"])
    else:
        def body(carry, xs):
            loss_sum, grad_sum = carry
            l_k, (gw_k, gx_k) = one_microbatch(xs[0], xs[1])
            with _jax.named_scope("update"):
                return (loss_sum + l_k, _jax.tree.map(_jnp.add, grad_sum, gw_k)), gx_k

        init = (_jnp.zeros((), _jnp.float32), _jax.tree.map(_jnp.zeros_like, weights))
        (loss, grad_w), grad_x = _jax.lax.scan(body, init, (per_example, given["loss_target"]))
    with _jax.named_scope("update"):
        delta_w, new_m, new_v = {}, {}, {}
        for n in TWIN_WEIGHTS:
            delta_w[n], new_m[n], new_v[n] = _adamw(weights[n], grad_w[n], given["m_" + n], given["v_" + n])
    return (loss, grad_x, *[grad_w[n] for n in TWIN_WEIGHTS], *[delta_w[n] for n in TWIN_WEIGHTS],
            *[new_m[n] for n in TWIN_WEIGHTS], *[new_v[n] for n in TWIN_WEIGHTS])
```

```python
import functools
import math

import jax
import jax.numpy as jnp
from jax import lax
from jax.experimental import pallas as pl
from jax.experimental.pallas import tpu as pltpu

F32 = jnp.float32
BF16 = jnp.bfloat16

N_DEV = 8
N_META = 16
CHUNK = 128
PADN = CHUNK - N_META
HEAD_DIM = 64
SSD_GROUPS = 4
CONV_K = 4
EPS = 1e-6
NEG = -1e30
LANES = 128
HALO = 16

ADAM_LR = 0.001
ADAM_B1 = 0.9
ADAM_B2 = 0.999
ADAM_EPS = 1e-08
ADAM_WD = 0.01
ADAM_STEP = 10

VMEM_LIMIT = 56 * 1024 * 1024

NN = (((1,), (0,)), ((), ()))
NT = (((1,), (1,)), ((), ()))
TN = (((0,), (0,)), ((), ()))
MESH = pl.DeviceIdType.MESH


def _dot(a, b, dims=NN):
    return lax.dot_general(a, b, dims, preferred_element_type=F32)


def _split2(x):
    hi = x.astype(BF16)
    lo = (x - hi.astype(F32)).astype(BF16)
    return hi, lo


def _dot_sel(x, sel):
    hi, lo = _split2(x)
    return _dot(hi, sel) + _dot(lo, sel)


def _dot_tri(tri, x):
    h1 = x.astype(BF16)
    r1 = x - h1.astype(F32)
    h2 = r1.astype(BF16)
    h3 = (r1 - h2.astype(F32)).astype(BF16)
    return _dot(tri, h1) + _dot(tri, h2) + _dot(tri, h3)


def _sigmoid(x):
    return 1.0 / (1.0 + jnp.exp(-x))


def _softplus(x):
    return jnp.maximum(x, 0.0) + jnp.log(1.0 + jnp.exp(-jnp.abs(x)))


def _cparams(sem=None, vmem=VMEM_LIMIT):
    kw = {"vmem_limit_bytes": vmem}
    if sem is not None:
        kw["dimension_semantics"] = sem
    return pltpu.CompilerParams(**kw)


def _full(shape):
    nd = len(shape)
    return pl.BlockSpec(shape, lambda *_: (0,) * nd)


def _att_block(p):
    return 384 if p % 384 == 0 else CHUNK


def _my_pos():
    return lax.axis_index("x"), lax.axis_index("y"), lax.axis_index("c")


def _dev_index(x, y, c):
    return 4 * x + 2 * y + c


FLIPS = [(fx, fy, fc) for fx in (0, 1) for fy in (0, 1) for fc in (0, 1)][1:]


def _flip(pos, f):
    return tuple((1 - p) if fi else p for p, fi in zip(pos, f))


def _all_gather(bufs, name):
    nb = len(bufs)

    def body(*refs):
        ins, outs = refs[:nb], refs[nb:2 * nb]
        send_sems, recv_sems, local_sems = refs[2 * nb:]
        x, y, c = _my_pos()
        me = _dev_index(x, y, c)
        sibling = (x, y, 1 - c)
        chips = [(1 - x, y), (x, 1 - y), (1 - x, 1 - y)]

        def copy(b, k, block_idx, to, src=None):
            dst = outs[b].at[block_idx]
            return pltpu.make_async_remote_copy(
                src_ref=dst if src is None else src, dst_ref=dst,
                send_sem=send_sems.at[b, k], recv_sem=recv_sems.at[b, k],
                device_id=to, device_id_type=MESH)

        started = []
        for b in range(nb):
            mine = pltpu.make_async_copy(ins[b], outs[b].at[me], local_sems.at[b])
            mine.start()
            started.append(mine)
        first = []
        for b in range(nb):
            first.append(copy(b, 0, me, sibling, src=ins[b]))
            for j, chip in enumerate(chips):
                first.append(copy(b, 1 + j, me, (chip[0], chip[1], c), src=ins[b]))
        for cp in first:
            cp.start()
        passed = []
        for j, chip in enumerate(chips):
            blk = _dev_index(chip[0], chip[1], c)
            for b in range(nb):
                copy(b, 1 + j, blk, (x, y, c)).wait_recv()
                fwd = copy(b, 4 + j, blk, sibling)
                fwd.start()
                passed.append(fwd)
        for b in range(nb):
            copy(b, 0, _dev_index(x, y, 1 - c), (x, y, c)).wait_recv()
        for j, chip in enumerate(chips):
            blk = _dev_index(chip[0], chip[1], 1 - c)
            for b in range(nb):
                copy(b, 4 + j, blk, (x, y, c)).wait_recv()
        for cp in first + passed:
            cp.wait_send()
        for mine in started:
            mine.wait()

    any_spec = pl.BlockSpec(memory_space=pl.ANY)
    return pl.pallas_call(
        body, name=name,
        out_shape=[jax.ShapeDtypeStruct((N_DEV,) + b.shape, b.dtype) for b in bufs],
        in_specs=[any_spec] * nb, out_specs=[any_spec] * nb,
        scratch_shapes=[pltpu.SemaphoreType.DMA((nb, 7)), pltpu.SemaphoreType.DMA((nb, 7)),
                        pltpu.SemaphoreType.DMA((nb,))],
    )(*bufs)


def _all_to_all(bufs, name):
    nb = len(bufs)

    def body(*refs):
        ins, outs = refs[:nb], refs[nb:2 * nb]
        send_sems, recv_sems, local_sems = refs[2 * nb:]
        pos = _my_pos()
        me = _dev_index(*pos)

        def copy(b, k, to):
            return pltpu.make_async_remote_copy(
                src_ref=ins[b].at[_dev_index(*to)], dst_ref=outs[b].at[me],
                send_sem=send_sems.at[b, k], recv_sem=recv_sems.at[b, k],
                device_id=to, device_id_type=MESH)

        local = []
        for b in range(nb):
            cp = pltpu.make_async_copy(ins[b].at[me], outs[b].at[me], local_sems.at[b])
            cp.start()
            local.append(cp)
        sends = []
        for b in range(nb):
            for k, f in enumerate(FLIPS):
                cp = copy(b, k, _flip(pos, f))
                cp.start()
                sends.append(cp)
        for b in range(nb):
            for k, f in enumerate(FLIPS):
                peer = _flip(pos, f)
                pltpu.make_async_remote_copy(
                    src_ref=ins[b].at[me], dst_ref=outs[b].at[_dev_index(*peer)],
                    send_sem=send_sems.at[b, k], recv_sem=recv_sems.at[b, k],
                    device_id=peer, device_id_type=MESH).wait_recv()
        for cp in sends:
            cp.wait_send()
        for cp in local:
            cp.wait()

    any_spec = pl.BlockSpec(memory_space=pl.ANY)
    return pl.pallas_call(
        body, name=name,
        out_shape=[jax.ShapeDtypeStruct(b.shape, b.dtype) for b in bufs],
        in_specs=[any_spec] * nb, out_specs=[any_spec] * nb,
        scratch_shapes=[pltpu.SemaphoreType.DMA((nb, 7)), pltpu.SemaphoreType.DMA((nb, 7)),
                        pltpu.SemaphoreType.DMA((nb,))],
    )(*bufs)


def _all_reduce_small(v, name):
    r, cdim = v.shape

    def body(x_ref, out_ref, slots, send_sems, recv_sems):
        pos = _my_pos()
        me = _dev_index(*pos)
        slots[me] = x_ref[...]
        sends = []
        for k, f in enumerate(FLIPS):
            cp = pltpu.make_async_remote_copy(
                src_ref=x_ref, dst_ref=slots.at[me], send_sem=send_sems.at[k], recv_sem=recv_sems.at[k],
                device_id=_flip(pos, f), device_id_type=MESH)
            cp.start()
            sends.append(cp)
        for k, f in enumerate(FLIPS):
            peer = _flip(pos, f)
            pltpu.make_async_remote_copy(
                src_ref=x_ref, dst_ref=slots.at[_dev_index(*peer)], send_sem=send_sems.at[k],
                recv_sem=recv_sems.at[k], device_id=peer, device_id_type=MESH).wait_recv()
        for cp in sends:
            cp.wait_send()
        acc = slots[0]
        for s in range(1, N_DEV):
            acc = acc + slots[s]
        out_ref[...] = acc

    vm = pl.BlockSpec(memory_space=pltpu.VMEM)
    return pl.pallas_call(
        body, name=name, out_shape=jax.ShapeDtypeStruct(v.shape, F32),
        in_specs=[vm], out_specs=vm,
        scratch_shapes=[pltpu.VMEM((N_DEV, r, cdim), F32), pltpu.SemaphoreType.DMA((7,)),
                        pltpu.SemaphoreType.DMA((7,))],
    )(v)


def _mm(a, b, dims, out_dtype, tm, tn, name):
    if dims == "nn":
        (m, k), (_, n) = a.shape, b.shape
        a_spec = pl.BlockSpec((tm, k), lambda j, i: (i, 0))
        b_spec = pl.BlockSpec((k, tn), lambda j, i: (0, j))
        dn = NN
    elif dims == "nt":
        (m, k), (n, _) = a.shape, b.shape
        a_spec = pl.BlockSpec((tm, k), lambda j, i: (i, 0))
        b_spec = pl.BlockSpec((tn, k), lambda j, i: (j, 0))
        dn = NT
    else:
        (k, m), (_, n) = a.shape, b.shape
        a_spec = pl.BlockSpec((k, tm), lambda j, i: (0, i))
        b_spec = pl.BlockSpec((k, tn), lambda j, i: (0, j))
        dn = TN
    assert m % tm == 0 and n % tn == 0, (m, tm, n, tn)

    def body(a_ref, b_ref, o_ref):
        o_ref[...] = _dot(a_ref[...], b_ref[...], dn).astype(o_ref.dtype)

    return pl.pallas_call(
        body, name=name, grid=(n // tn, m // tm),
        in_specs=[a_spec, b_spec], out_specs=pl.BlockSpec((tm, tn), lambda j, i: (i, j)),
        out_shape=jax.ShapeDtypeStruct((m, n), out_dtype),
        compiler_params=_cparams(("parallel", "parallel")),
    )(a, b)


def _tile(n, prefs):
    for t in prefs:
        if n % t == 0:
            return t
    return n


def _prenorm_fwd(head, x2, w):
    p, d = x2.shape[0] + CHUNK, x2.shape[1]

    def body(head_ref, x_ref, w_ref, u_ref):
        i = pl.program_id(0)
        h = jnp.where(i == 0, head_ref[...], x_ref[...])
        ms = jnp.mean(h * h, axis=-1, keepdims=True)
        u_ref[...] = (h * lax.rsqrt(ms + EPS) * w_ref[...]).astype(BF16)

    return pl.pallas_call(
        body, name="prenorm_fwd", grid=(p // CHUNK,),
        in_specs=[_full((CHUNK, d)), pl.BlockSpec((CHUNK, d), lambda i: (jnp.maximum(i - 1, 0), 0)), _full((1, d))],
        out_specs=pl.BlockSpec((CHUNK, d), lambda i: (i, 0)),
        out_shape=jax.ShapeDtypeStruct((p, d), BF16),
        compiler_params=_cparams(("arbitrary",)),
    )(head, x2, w)


def _prenorm_bwd(head, x2, w, du, dout):
    p, d = x2.shape[0] + CHUNK, x2.shape[1]

    def body(head_ref, x_ref, w_ref, du_ref, dout_ref, gx_ref, ghead_ref, gw_ref):
        i = pl.program_id(0)
        h = jnp.where(i == 0, head_ref[...], x_ref[...])
        rstd = lax.rsqrt(jnp.mean(h * h, axis=-1, keepdims=True) + EPS)
        xhat = h * rstd
        dub = du_ref[...]
        dxh = dub * w_ref[...]
        dh = rstd * (dxh - xhat * jnp.mean(dxh * xhat, axis=-1, keepdims=True)) + dout_ref[...]

        @pl.when(i == 0)
        def _():
            ghead_ref[...] = dh
            gw_ref[...] = jnp.zeros_like(gw_ref)

        gx_ref[...] = dh
        gw_ref[0:1, :] += jnp.sum(dub * xhat, axis=0, keepdims=True)

    return pl.pallas_call(
        body, name="prenorm_bwd", grid=(p // CHUNK,),
        in_specs=[_full((CHUNK, d)), pl.BlockSpec((CHUNK, d), lambda i: (jnp.maximum(i - 1, 0), 0)), _full((1, d)),
                  pl.BlockSpec((CHUNK, d), lambda i: (i, 0)), pl.BlockSpec((CHUNK, d), lambda i: (i, 0))],
        out_specs=[pl.BlockSpec((CHUNK, d), lambda i: (jnp.maximum(i - 1, 0), 0)), _full((CHUNK, d)), _full((8, d))],
        out_shape=[jax.ShapeDtypeStruct(x2.shape, F32), jax.ShapeDtypeStruct((CHUNK, d), F32),
                   jax.ShapeDtypeStruct((8, d), F32)],
        compiler_params=_cparams(("arbitrary",)),
    )(head, x2, w, du, dout)


def _conv_pre(xr, halo128, cw_ref, cb_ref, rows):
    pre = cb_ref[...] + cw_ref[CONV_K - 1:CONV_K, :] * xr
    shifted = []
    for j in range(1, CONV_K):
        sh = jnp.where(rows >= j, pltpu.roll(xr, j, 0), pltpu.roll(halo128, j, 0))
        shifted.append(sh)
        pre = pre + cw_ref[CONV_K - 1 - j:CONV_K - j, :] * sh
    return pre, shifted


def _ssd_scalars(dtf_ref, brow_ref, alog_ref, rowmask, hs, ha, tri):
    lane = lax.broadcasted_iota(jnp.int32, (1, LANES), 1)
    is_dt = lane < hs
    is_f = (lane >= hs) & (lane < hs + ha)
    dtr = dtf_ref[...] + brow_ref[...]
    sp = _softplus(dtr)
    dt = jnp.where(is_dt, sp, 0.0) * rowmask
    logf = jnp.where(is_f, jnp.minimum(dtr, 0.0) - jnp.log(1.0 + jnp.exp(-jnp.abs(dtr))), 0.0) * rowmask
    a_row = jnp.where(is_dt, -jnp.exp(alog_ref[...]), 0.0)
    run = _dot_tri(tri, dt * a_row + logf)
    return dtr, dt, a_row, run, is_dt, is_f


def _tri_mats():
    r = lax.broadcasted_iota(jnp.int32, (CHUNK, CHUNK), 0)
    c = lax.broadcasted_iota(jnp.int32, (CHUNK, CHUNK), 1)
    return r, c


def _ssd_fwd(xbc, z, dtf, conv_w, conv_b, brow, alog, dskip_l, ssd_norm, sel_t, hs, ha):
    p, cd = xbc.shape
    ds = z.shape[1]
    ns = (cd - ds) // (2 * SSD_GROUPS)
    gw = ds // SSD_GROUPS
    nch = p // CHUNK
    hpg = hs // SSD_GROUPS

    def body(xbc_ref, halo_ref, z_ref, dtf_ref, cw_ref, cb_ref, brow_ref, alog_ref, dsk_ref, nrm_ref, selt_ref,
             y_ref, yssd_ref, hin_ref, cf_ref, st_ref, carry_ref, yacc_ref):
        c = pl.program_id(0)

        @pl.when(c == 0)
        def _():
            st_ref[...] = jnp.zeros_like(st_ref)
            carry_ref[...] = jnp.zeros_like(carry_ref)

        rows = lax.broadcasted_iota(jnp.int32, (CHUNK, 1), 0)
        rowmask = jnp.where((rows >= PADN) | (c > 0), 1.0, 0.0)
        ri, ci = _tri_mats()
        causal = ri >= ci
        tri = jnp.where(causal, 1.0, 0.0).astype(BF16)

        xr = xbc_ref[...].astype(F32)
        halo = halo_ref[...].astype(F32) * jnp.where(c > 0, 1.0, 0.0)
        halo128 = jnp.concatenate([jnp.zeros((CHUNK - HALO, cd), F32), halo], axis=0)
        pre, _ = _conv_pre(xr, halo128, cw_ref, cb_ref, rows)
        xc = pre * _sigmoid(pre) * rowmask

        dtr, dt, a_row, run, is_dt, is_f = _ssd_scalars(dtf_ref, brow_ref, alog_ref, rowmask, hs, ha, tri)
        cf = run + carry_ref[...]
        cf_ref[...] = cf
        carry_ref[...] = jnp.where(is_f, cf[CHUNK - 1:CHUNK, :], 0.0)
        cs = jnp.where(is_dt, run, 0.0)
        cl = cs[CHUNK - 1:CHUNK, :]
        selt = selt_ref[...]
        dt_x = _dot_sel(dt, selt)
        e_x = _dot_sel(jnp.exp(cs), selt)
        w_x = _dot_sel(jnp.exp(cl - cs), selt)
        cdec_x = _dot_sel(jnp.broadcast_to(jnp.exp(cl), (8, LANES)), selt)[0:1, :]
        cs_t = cs.T

        xs = xc[:, :ds]
        xdt = xs * dt_x
        xdt_b = xdt.astype(BF16)
        xw_b = (xdt * w_x).astype(BF16)
        lane = lax.broadcasted_iota(jnp.int32, (1, LANES), 1)
        half0 = lane < HEAD_DIM
        for g in range(SSD_GROUPS):
            bg = xc[:, ds + g * ns: ds + (g + 1) * ns].astype(BF16)
            cg = xc[:, ds + SSD_GROUPS * ns + g * ns: ds + SSD_GROUPS * ns + (g + 1) * ns].astype(BF16)
            gm = _dot(cg, bg, NT)
            gs = slice(g * gw, (g + 1) * gw)
            stg = st_ref[:, gs]
            stg_b = stg.astype(BF16)
            hin_ref[0, :, gs] = stg_b
            yoff = _dot(cg, stg_b) * e_x[:, gs]
            for pr in range(gw // LANES):
                sl = slice(g * gw + pr * LANES, g * gw + (pr + 1) * LANES)
                xp = xdt_b[:, sl]
                yd = jnp.zeros((CHUNK, LANES), F32)
                for j in range(2):
                    h = g * hpg + 2 * pr + j
                    seg = cs[:, h:h + 1] - cs_t[h:h + 1, :]
                    m = jnp.where(causal, gm * jnp.exp(jnp.minimum(seg, 0.0)), 0.0).astype(BF16)
                    sel = half0 if j == 0 else jnp.logical_not(half0)
                    yd = yd + _dot(m, jnp.where(sel, xp, jnp.zeros_like(xp)))
                yacc_ref[:, sl] = yd + yoff[:, pr * LANES:(pr + 1) * LANES] + dsk_ref[:, sl] * xs[:, sl]
            st_ref[:, gs] = stg * cdec_x[:, gs] + _dot(bg, xw_b[:, gs], TN)

        y = yacc_ref[...]
        y_ref[...] = y.astype(BF16)
        zf = z_ref[...].astype(F32)
        u = y * zf * _sigmoid(zf)
        for g in range(SSD_GROUPS):
            gs = slice(g * gw, (g + 1) * gw)
            ug = u[:, gs]
            ms = jnp.mean(ug * ug, axis=-1, keepdims=True)
            yssd_ref[:, gs] = (ug * lax.rsqrt(ms + EPS) * nrm_ref[:, gs]).astype(BF16)

    rb = CHUNK // HALO
    return pl.pallas_call(
        body, name="ssd_fwd", grid=(nch,),
        in_specs=[pl.BlockSpec((CHUNK, cd), lambda c: (c, 0)),
                  pl.BlockSpec((HALO, cd), lambda c: (jnp.maximum(c * rb - 1, 0), 0)),
                  pl.BlockSpec((CHUNK, ds), lambda c: (c, 0)),
                  pl.BlockSpec((CHUNK, LANES), lambda c: (c, 0)),
                  _full((CONV_K, cd)), _full((1, cd)), _full((1, LANES)), _full((1, LANES)),
                  _full((1, ds)), _full((1, ds)), _full((LANES, ds))],
        out_specs=[pl.BlockSpec((CHUNK, ds), lambda c: (c, 0)), pl.BlockSpec((CHUNK, ds), lambda c: (c, 0)),
                   pl.BlockSpec((1, ns, ds), lambda c: (c, 0, 0)), pl.BlockSpec((CHUNK, LANES), lambda c: (c, 0))],
        out_shape=[jax.ShapeDtypeStruct((p, ds), BF16), jax.ShapeDtypeStruct((p, ds), BF16),
                   jax.ShapeDtypeStruct((nch, ns, ds), BF16), jax.ShapeDtypeStruct((p, LANES), F32)],
        scratch_shapes=[pltpu.VMEM((ns, ds), F32), pltpu.VMEM((1, LANES), F32), pltpu.VMEM((CHUNK, ds), F32)],
        compiler_params=_cparams(("arbitrary",)),
    )(xbc, xbc, z, dtf, conv_w, conv_b, brow, alog, dskip_l, ssd_norm, sel_t)


def _ssd_bwd(dyssd, y, z, xbc, dtf, hin, dcf, conv_w, conv_b, brow, alog, dskip_l, ssd_norm, sel_t, sel, hs, ha):
    p, cd = xbc.shape
    ds = z.shape[1]
    ns = (cd - ds) // (2 * SSD_GROUPS)
    gw = ds // SSD_GROUPS
    nch = p // CHUNK
    hpg = hs // SSD_GROUPS
    rb = CHUNK // HALO

    def body(dyssd_ref, y_ref, z_ref, xbc_ref, halo_ref, dtf_ref, hin_ref, dcf_ref, cw_ref, cb_ref, brow_ref,
             alog_ref, dsk_ref, nrm_ref, selt_ref, sel_ref,
             dxbc_ref, dz_ref, ddtf_ref, gcw_ref, gcb_ref, gnrm_ref, gsm_ref,
             dst_ref, nxt_ref, fcar_ref, gdsk_ref, dxc_ref):
        step = pl.program_id(0)
        c = nch - 1 - step

        @pl.when(step == 0)
        def _():
            dst_ref[...] = jnp.zeros_like(dst_ref)
            nxt_ref[...] = jnp.zeros_like(nxt_ref)
            fcar_ref[...] = jnp.zeros_like(fcar_ref)
            gdsk_ref[...] = jnp.zeros_like(gdsk_ref)
            gcw_ref[...] = jnp.zeros_like(gcw_ref)
            gcb_ref[...] = jnp.zeros_like(gcb_ref)
            gnrm_ref[...] = jnp.zeros_like(gnrm_ref)
            gsm_ref[...] = jnp.zeros_like(gsm_ref)

        rows = lax.broadcasted_iota(jnp.int32, (CHUNK, 1), 0)
        rowmask = jnp.where((rows >= PADN) | (c > 0), 1.0, 0.0)
        ri, ci = _tri_mats()
        causal = ri >= ci
        anti = ci >= ri
        tri = jnp.where(causal, 1.0, 0.0).astype(BF16)
        rtri = jnp.where(anti, 1.0, 0.0).astype(BF16)

        xr = xbc_ref[...].astype(F32)
        halo = halo_ref[...].astype(F32) * jnp.where(c > 0, 1.0, 0.0)
        halo128 = jnp.concatenate([jnp.zeros((CHUNK - HALO, cd), F32), halo], axis=0)
        pre, shifted = _conv_pre(xr, halo128, cw_ref, cb_ref, rows)
        sg = _sigmoid(pre)
        xc = pre * sg * rowmask
        dsilu = sg * (1.0 + pre * (1.0 - sg)) * rowmask

        dtr, dt, a_row, run, is_dt, is_f = _ssd_scalars(dtf_ref, brow_ref, alog_ref, rowmask, hs, ha, tri)
        cs = jnp.where(is_dt, run, 0.0)
        cl = cs[CHUNK - 1:CHUNK, :]
        selt = selt_ref[...]
        selm = sel_ref[...]
        dt_x = _dot_sel(dt, selt)
        e_x = _dot_sel(jnp.exp(cs), selt)
        w_x = _dot_sel(jnp.exp(cl - cs), selt)
        cdec = jnp.exp(cl)
        cdec_x = _dot_sel(jnp.broadcast_to(cdec, (8, LANES)), selt)[0:1, :]
        cs_t = cs.T
        xs = xc[:, :ds]
        xdt = xs * dt_x
        xdt_b = xdt.astype(BF16)
        xw_b = (xdt * w_x).astype(BF16)

        yv = y_ref[...].astype(F32)
        zf = z_ref[...].astype(F32)
        sz = _sigmoid(zf)
        u = yv * zf * sz
        dyo = dyssd_ref[...].astype(F32)
        du_parts = []
        for g in range(SSD_GROUPS):
            gs = slice(g * gw, (g + 1) * gw)
            ug = u[:, gs]
            rstd = lax.rsqrt(jnp.mean(ug * ug, axis=-1, keepdims=True) + EPS)
            yhat = ug * rstd
            dyg = dyo[:, gs]
            gnrm_ref[0:1, gs] += jnp.sum(dyg * yhat, axis=0, keepdims=True)
            dyh = dyg * nrm_ref[:, gs]
            du_parts.append(rstd * (dyh - yhat * jnp.mean(dyh * yhat, axis=-1, keepdims=True)))
        du = jnp.concatenate(du_parts, axis=1)
        dy = du * zf * sz
        dz_ref[...] = (du * yv * sz * (1.0 + zf * (1.0 - sz))).astype(BF16)

        dsk = dsk_ref[...]
        gdsk_ref[...] += jnp.sum(dy * xs, axis=0, keepdims=True)
        dy_b = dy.astype(BF16)
        dye_b = (dy * e_x).astype(BF16)
        lane = lax.broadcasted_iota(jnp.int32, (1, LANES), 1)
        half0 = lane < HEAD_DIM
        x_parts, yo_parts, t4_parts = [], [], []
        dcs = jnp.zeros((CHUNK, LANES), F32)
        for g in range(SSD_GROUPS):
            gs = slice(g * gw, (g + 1) * gw)
            bsl = slice(ds + g * ns, ds + (g + 1) * ns)
            csl = slice(ds + SSD_GROUPS * ns + g * ns, ds + SSD_GROUPS * ns + (g + 1) * ns)
            bg = xc[:, bsl].astype(BF16)
            cg = xc[:, csl].astype(BF16)
            gm = _dot(cg, bg, NT)
            gm_t = _dot(bg, cg, NT)
            stg_b = hin_ref[0, :, gs]
            dstg = dst_ref[:, gs]
            dstg_b = dstg.astype(BF16)
            t4_parts.append(jnp.sum(dstg * stg_b.astype(F32), axis=0, keepdims=True))
            zst = _dot(bg, dstg_b) * w_x[:, gs]
            x_parts.append(xdt[:, gs] * zst)
            yo_parts.append(dy[:, gs] * (_dot(cg, stg_b) * e_x[:, gs]))
            dgsum = jnp.zeros((CHUNK, CHUNK), F32)
            dgtsum = jnp.zeros((CHUNK, CHUNK), F32)
            for pr in range(gw // LANES):
                sl = slice(g * gw + pr * LANES, g * gw + (pr + 1) * LANES)
                xp = xdt_b[:, sl]
                dyp = dy_b[:, sl]
                dxd = zst[:, pr * LANES:(pr + 1) * LANES]
                for j in range(2):
                    h = g * hpg + 2 * pr + j
                    sel_l = half0 if j == 0 else jnp.logical_not(half0)
                    seg = cs[:, h:h + 1] - cs_t[h:h + 1, :]
                    lm = jnp.where(causal, jnp.exp(jnp.minimum(seg, 0.0)), 0.0)
                    lmt = jnp.where(anti, jnp.exp(jnp.minimum(-seg, 0.0)), 0.0)
                    dyp_m = jnp.where(sel_l, dyp, jnp.zeros_like(dyp))
                    xp_m = jnp.where(sel_l, xp, jnp.zeros_like(xp))
                    dxd = dxd + _dot((gm_t * lmt).astype(BF16), dyp_m)
                    dg = _dot(dyp_m, xp, NT) * lm
                    dgt = _dot(xp_m, dyp, NT) * lmt
                    dgsum = dgsum + dg
                    dgtsum = dgtsum + dgt
                    qrow = (jnp.sum(dg * gm, axis=1, keepdims=True) - jnp.sum(dgt * gm_t, axis=1, keepdims=True))
                    dcs = dcs + jnp.where(lane == h, qrow, 0.0)
                dxc_ref[:, sl] = dxd
            dxc_ref[:, csl] = _dot(dgsum.astype(BF16), bg) + _dot(dye_b[:, gs], stg_b, NT)
            dxc_ref[:, bsl] = _dot(dgtsum.astype(BF16), cg) + _dot(xw_b[:, gs], dstg_b, NT)
            dst_ref[:, gs] = dstg * cdec_x[:, gs] + _dot(cg, dye_b[:, gs], TN)

        dxdt = dxc_ref[:, :ds]
        xst = _dot_sel(jnp.concatenate(x_parts, axis=1), selm)
        yo = _dot_sel(jnp.concatenate(yo_parts, axis=1), selm)
        t4 = _dot_sel(jnp.concatenate([jnp.concatenate(t4_parts, axis=1), jnp.zeros((7, ds), F32)], axis=0), selm)
        dcl = jnp.sum(xst, axis=0, keepdims=True) + cdec * t4[0:1, :]
        dcs = dcs + yo - xst + jnp.where(rows == CHUNK - 1, dcl, 0.0)
        da_ = _dot_tri(rtri, dcs)
        ddt = _dot_sel(dxdt * xs, selm) + da_ * a_row
        dcf_blk = dcf_ref[...]
        dlogf = _dot_tri(rtri, dcf_blk) + fcar_ref[...]
        fcar_ref[...] += jnp.sum(dcf_blk, axis=0, keepdims=True)
        sgd = _sigmoid(dtr)
        ddtf = (jnp.where(is_dt, ddt * sgd, 0.0) + jnp.where(is_f, dlogf * (1.0 - sgd), 0.0)) * rowmask
        ddtf_ref[...] = ddtf
        gsm_ref[0:1, :] += jnp.sum(ddtf, axis=0, keepdims=True)
        gsm_ref[1:2, :] += jnp.sum(da_ * dt, axis=0, keepdims=True) * a_row

        dxc_ref[:, :ds] = dxdt * dt_x + dsk * dy
        dpre = dxc_ref[...] * dsilu
        gcb_ref[0:1, :] += jnp.sum(dpre, axis=0, keepdims=True)
        gcw_ref[CONV_K - 1:CONV_K, :] += jnp.sum(dpre * xr, axis=0, keepdims=True)
        nxt128 = jnp.concatenate([nxt_ref[...], jnp.zeros((CHUNK - 8, cd), F32)], axis=0)
        dxr = cw_ref[CONV_K - 1:CONV_K, :] * dpre
        for j in range(1, CONV_K):
            gcw_ref[CONV_K - 1 - j:CONV_K - j, :] += jnp.sum(dpre * shifted[j - 1], axis=0, keepdims=True)
            up = jnp.where(rows < CHUNK - j, pltpu.roll(dpre, CHUNK - j, 0), pltpu.roll(nxt128, CHUNK - j, 0))
            dxr = dxr + cw_ref[CONV_K - 1 - j:CONV_K - j, :] * up
        nxt_ref[...] = dpre[0:8, :]
        dxbc_ref[...] = dxr.astype(BF16)

        @pl.when(step == nch - 1)
        def _():
            gsm_ref[2:3, :] = _dot_sel(jnp.broadcast_to(gdsk_ref[...], (8, ds)), selm)[0:1, :]

    rev = lambda s: nch - 1 - s
    blk = lambda w: pl.BlockSpec((CHUNK, w), lambda s: (rev(s), 0))
    return pl.pallas_call(
        body, name="ssd_bwd", grid=(nch,),
        in_specs=[blk(ds), blk(ds), blk(ds), blk(cd),
                  pl.BlockSpec((HALO, cd), lambda s: (jnp.maximum(rev(s) * rb - 1, 0), 0)),
                  blk(LANES), pl.BlockSpec((1, ns, ds), lambda s: (rev(s), 0, 0)), blk(LANES),
                  _full((CONV_K, cd)), _full((1, cd)), _full((1, LANES)), _full((1, LANES)),
                  _full((1, ds)), _full((1, ds)), _full((LANES, ds)), _full((ds, LANES))],
        out_specs=[blk(cd), blk(ds), blk(LANES), _full((8, cd)), _full((8, cd)), _full((8, ds)), _full((8, LANES))],
        out_shape=[jax.ShapeDtypeStruct((p, cd), BF16), jax.ShapeDtypeStruct((p, ds), BF16),
                   jax.ShapeDtypeStruct((p, LANES), F32), jax.ShapeDtypeStruct((8, cd), F32),
                   jax.ShapeDtypeStruct((8, cd), F32), jax.ShapeDtypeStruct((8, ds), F32),
                   jax.ShapeDtypeStruct((8, LANES), F32)],
        scratch_shapes=[pltpu.VMEM((ns, ds), F32), pltpu.VMEM((8, cd), F32), pltpu.VMEM((1, LANES), F32),
                        pltpu.VMEM((1, ds), F32), pltpu.VMEM((CHUNK, cd), F32)],
        compiler_params=_cparams(("arbitrary",)),
    )(dyssd, y, z, xbc, xbc, dtf, hin, dcf, conv_w, conv_b, brow, alog, dskip_l, ssd_norm, sel_t, sel)


def _attn_fwd(q, k, v, cq_rep, ck, blk):
    p, da = q.shape
    npair, nkb = ck.shape[0], ck.shape[1]
    scale = 1.0 / math.sqrt(HEAD_DIM)

    def body(q_ref, k_ref, v_ref, cq_ref, ck_ref, o_ref, lse_ref):
        i = pl.program_id(1)
        qb = q_ref[...]
        lane = lax.broadcasted_iota(jnp.int32, (1, LANES), 1)
        half0 = lane < HEAD_DIM
        qpos = i * blk + lax.broadcasted_iota(jnp.int32, (blk, 1), 0)
        kiota = lax.broadcasted_iota(jnp.int32, (1, blk), 1)
        o_tot = jnp.zeros((blk, LANES), F32)
        lse_tot = jnp.zeros((blk, LANES), F32)
        for j in range(2):
            sel = half0 if j == 0 else jnp.logical_not(half0)
            qm = jnp.where(sel, qb, jnp.zeros_like(qb))
            cqc = cq_ref[:, HEAD_DIM * j:HEAD_DIM * j + 1]

            def kb_body(kb, carry):
                m, l, acc = carry
                r0 = pl.multiple_of(kb * blk, blk)
                ks = k_ref[pl.ds(r0, blk), :]
                vs = v_ref[pl.ds(r0, blk), :]
                s = _dot(qm, ks, NT) * scale + cqc - ck_ref[0, kb, j:j + 1, :]
                kpos = kb * blk + kiota
                s = jnp.where((kpos <= qpos) & (kpos >= PADN), s, NEG)
                mn = jnp.maximum(m, jnp.max(s, axis=-1, keepdims=True))
                pr = jnp.exp(s - mn)
                alpha = jnp.exp(m - mn)
                l = alpha * l + jnp.sum(pr, axis=-1, keepdims=True)
                acc = alpha * acc + _dot(pr.astype(BF16), jnp.where(sel, vs, jnp.zeros_like(vs)))
                return mn, l, acc

            m, l, acc = lax.fori_loop(0, i + 1, kb_body,
                                      (jnp.full((blk, 1), NEG, F32), jnp.zeros((blk, 1), F32),
                                       jnp.zeros((blk, LANES), F32)))
            o_tot = o_tot + acc / l
            lse_tot = lse_tot + jnp.where(sel, m + jnp.log(l), 0.0)
        o_ref[...] = o_tot.astype(BF16)
        lse_ref[...] = lse_tot

    return pl.pallas_call(
        body, name="attn_fwd", grid=(npair, p // blk),
        in_specs=[pl.BlockSpec((blk, LANES), lambda h, i: (i, h)),
                  pl.BlockSpec((p, LANES), lambda h, i: (0, h)), pl.BlockSpec((p, LANES), lambda h, i: (0, h)),
                  pl.BlockSpec((blk, LANES), lambda h, i: (i, h)),
                  pl.BlockSpec((1, nkb, 8, blk), lambda h, i: (h, 0, 0, 0))],
        out_specs=[pl.BlockSpec((blk, LANES), lambda h, i: (i, h)), pl.BlockSpec((blk, LANES), lambda h, i: (i, h))],
        out_shape=[jax.ShapeDtypeStruct((p, da), BF16), jax.ShapeDtypeStruct((p, da), F32)],
        compiler_params=_cparams(("parallel", "arbitrary")),
    )(q, k, v, cq_rep, ck)


def _attn_bwd(q, k, v, o, do, lse_rep, cq_rep, ck, blk):
    p, da = q.shape
    npair, nkb = ck.shape[0], ck.shape[1]
    nq = p // blk
    scale = 1.0 / math.sqrt(HEAD_DIM)

    def body(k_ref, v_ref, q_ref, do_ref, o_ref, lse_ref, cq_ref, ck_ref, dk_ref, dv_ref, dq_ref, dck_ref, dcq_ref):
        jb = pl.program_id(1)

        @pl.when(jb == 0)
        def _():
            dq_ref[...] = jnp.zeros_like(dq_ref)
            dcq_ref[...] = jnp.zeros_like(dcq_ref)

        ks = k_ref[...]
        vs = v_ref[...]
        lane = lax.broadcasted_iota(jnp.int32, (1, LANES), 1)
        half0 = lane < HEAD_DIM
        kpos = jb * blk + lax.broadcasted_iota(jnp.int32, (1, blk), 1)
        qiota = lax.broadcasted_iota(jnp.int32, (blk, 1), 0)
        dk_tot = jnp.zeros((blk, LANES), F32)
        dv_tot = jnp.zeros((blk, LANES), F32)
        dck_rows = []
        for j in range(2):
            sel = half0 if j == 0 else jnp.logical_not(half0)
            ckr = ck_ref[0, 0, j:j + 1, :]
            km = jnp.where(sel, ks, jnp.zeros_like(ks))

            def ib_body(ib, carry):
                dk, dv, dck = carry
                r0 = pl.multiple_of(ib * blk, blk)
                qb = q_ref[pl.ds(r0, blk), :]
                dob = do_ref[pl.ds(r0, blk), :]
                ob = o_ref[pl.ds(r0, blk), :]
                lse = lse_ref[pl.ds(r0, blk), HEAD_DIM * j:HEAD_DIM * j + 1]
                cqc = cq_ref[pl.ds(r0, blk), HEAD_DIM * j:HEAD_DIM * j + 1]
                qm = jnp.where(sel, qb, jnp.zeros_like(qb))
                dom = jnp.where(sel, dob, jnp.zeros_like(dob))
                dlt = jnp.sum(dom.astype(F32) * ob.astype(F32), axis=-1, keepdims=True)
                s = _dot(qm, ks, NT) * scale + cqc - ckr
                qpos = ib * blk + qiota
                pm = jnp.where((kpos <= qpos) & (kpos >= PADN), jnp.exp(jnp.minimum(s - lse, 0.0)), 0.0)
                dp = _dot(dom, vs, NT)
                dsm = pm * (dp - dlt)
                ds_b = dsm.astype(BF16)
                dv = dv + _dot(pm.astype(BF16), dom, TN)
                dk = dk + _dot(ds_b, qm, TN) * scale
                dq_ref[pl.ds(r0, blk), :] += _dot(ds_b, km) * scale
                dck = dck - jnp.sum(dsm, axis=0, keepdims=True)
                dcq_ref[pl.ds(r0, blk), :] += jnp.where(sel, jnp.sum(dsm, axis=1, keepdims=True), 0.0)
                return dk, dv, dck

            dk_j, dv_j, dck_j = lax.fori_loop(jb, nq, ib_body,
                                              (jnp.zeros((blk, LANES), F32), jnp.zeros((blk, LANES), F32),
                                               jnp.zeros((1, blk), F32)))
            dk_tot = dk_tot + dk_j
            dv_tot = dv_tot + dv_j
            dck_rows.append(dck_j)
        dk_ref[...] = dk_tot.astype(BF16)
        dv_ref[...] = dv_tot.astype(BF16)
        dck_ref[0, 0] = jnp.concatenate(dck_rows + [jnp.zeros((6, blk), F32)], axis=0)

    colblk = pl.BlockSpec((blk, LANES), lambda h, j: (j, h))
    colfull = pl.BlockSpec((p, LANES), lambda h, j: (0, h))
    ckspec = pl.BlockSpec((1, 1, 8, blk), lambda h, j: (h, j, 0, 0))
    return pl.pallas_call(
        body, name="attn_bwd", grid=(npair, nkb),
        in_specs=[colblk, colblk, colfull, colfull, colfull, colfull, colfull, ckspec],
        out_specs=[colblk, colblk, colfull, ckspec, colfull],
        out_shape=[jax.ShapeDtypeStruct((p, da), BF16), jax.ShapeDtypeStruct((p, da), BF16),
                   jax.ShapeDtypeStruct((p, da), F32), jax.ShapeDtypeStruct((npair, nkb, 8, blk), F32),
                   jax.ShapeDtypeStruct((p, da), F32)],
        compiler_params=_cparams(("parallel", "arbitrary")),
    )(k, v, q, do, o, lse_rep, cq_rep, ck)


def _tail(yssd, o, zatt, graw, head, x2, tgt2, wps, wpa, wout, gate_bias, norm_post):
    p, ds = yssd.shape
    da = o.shape[1]
    d = x2.shape[1]

    def body(yssd_ref, o_ref, zatt_ref, g_ref, head_ref, x_ref, tgt_ref, wps_ref, wpa_ref, wout_ref, gb_ref, np_ref,
             dyssd_ref, do_ref, dzatt_ref, dg_ref, dzo_ref, mrg_ref, da_ref, db_ref, yatt_ref, dout_ref, red_ref):
        i = pl.program_id(0)

        @pl.when(i == 0)
        def _():
            red_ref[...] = jnp.zeros_like(red_ref)

        h = jnp.where(i == 0, head_ref[...], x_ref[...])
        valid = jnp.where(i > 0, 1.0, 0.0)
        ob = o_ref[...].astype(F32)
        za = zatt_ref[...].astype(F32)
        sza = _sigmoid(za)
        silu = za * sza
        yatt_b = (ob * silu).astype(BF16)
        yatt_ref[...] = yatt_b
        wps_v, wpa_v, wout_v = wps_ref[...], wpa_ref[...], wout_ref[...]
        a = _dot(yssd_ref[...], wps_v)
        b = _dot(yatt_b, wpa_v)
        gr = g_ref[...].astype(F32) + gb_ref[...]
        gs = _sigmoid(gr[:, :d])
        ga = _sigmoid(gr[:, d:])
        mrg_b = (gs * a + ga * b).astype(BF16)
        mrg_ref[...] = mrg_b
        zo = _dot(mrg_b, wout_v)
        rstd = lax.rsqrt(jnp.mean(zo * zo, axis=-1, keepdims=True) + EPS)
        zh = zo * rstd
        npw = np_ref[...]
        err = (h + zh * npw - tgt_ref[...]) * valid
        dout = err * (1.0 / d)
        dout_ref[...] = dout
        dzh = dout * npw
        dzo_b = (rstd * (dzh - zh * jnp.mean(dzh * zh, axis=-1, keepdims=True))).astype(BF16)
        dzo_ref[...] = dzo_b
        dm = _dot(dzo_b, wout_v, NT)
        da_b = (gs * dm).astype(BF16)
        db_b = (ga * dm).astype(BF16)
        da_ref[...] = da_b
        db_ref[...] = db_b
        dgs = dm * a * gs * (1.0 - gs)
        dga = dm * b * ga * (1.0 - ga)
        dg_ref[:, :d] = dgs.astype(BF16)
        dg_ref[:, d:] = dga.astype(BF16)
        dyssd_ref[...] = _dot(da_b, wps_v, NT).astype(BF16)
        dya = _dot(db_b, wpa_v, NT)
        do_ref[...] = (dya * silu).astype(BF16)
        dzatt_ref[...] = (dya * ob * sza * (1.0 + za * (1.0 - sza))).astype(BF16)
        red_ref[0:1, :d] += jnp.sum(dout * zh, axis=0, keepdims=True)
        red_ref[1:2, :d] += jnp.sum(dgs, axis=0, keepdims=True)
        red_ref[1:2, d:] += jnp.sum(dga, axis=0, keepdims=True)
        red_ref[2:3, 0:1] += jnp.sum(jnp.sum(err * err, axis=1, keepdims=True), axis=0, keepdims=True) * (0.5 / d)

    row = lambda w: pl.BlockSpec((CHUNK, w), lambda i: (i, 0))
    shifted = lambda w: pl.BlockSpec((CHUNK, w), lambda i: (jnp.maximum(i - 1, 0), 0))
    sd = jax.ShapeDtypeStruct
    return pl.pallas_call(
        body, name="tail", grid=(p // CHUNK,),
        in_specs=[row(ds), row(da), row(da), row(2 * d), _full((CHUNK, d)), shifted(d), shifted(d),
                  _full((ds, d)), _full((da, d)), _full((d, d)), _full((1, 2 * d)), _full((1, d))],
        out_specs=[row(ds), row(da), row(da), row(2 * d), row(d), row(d), row(d), row(d), row(da), row(d),
                   _full((8, 2 * d))],
        out_shape=[sd((p, ds), BF16), sd((p, da), BF16), sd((p, da), BF16), sd((p, 2 * d), BF16), sd((p, d), BF16),
                   sd((p, d), BF16), sd((p, d), BF16), sd((p, d), BF16), sd((p, da), BF16), sd((p, d), F32),
                   sd((8, 2 * d), F32)],
        compiler_params=_cparams(("arbitrary",)),
    )(yssd, o, zatt, graw, head, x2, tgt2, wps, wpa, wout, gate_bias, norm_post)


def _adamw_math(w, g, m, v):
    m2 = ADAM_B1 * m + (1.0 - ADAM_B1) * g
    v2 = ADAM_B2 * v + (1.0 - ADAM_B2) * (g * g)
    m_hat = m2 / (1.0 - ADAM_B1 ** ADAM_STEP)
    v_hat = v2 / (1.0 - ADAM_B2 ** ADAM_STEP)
    delta = -ADAM_LR * (m_hat / (jnp.sqrt(v_hat) + ADAM_EPS) + ADAM_WD * w)
    return delta, m2, v2


def _adamw(w, g, m, v, name, parts=False):
    r, cdim = w.shape
    tr = CHUNK if r % CHUNK == 0 else r

    def body(w_ref, g_ref, m_ref, v_ref, go_ref, d_ref, mo_ref, vo_ref):
        if parts:
            g = g_ref[0].astype(F32)
            for s in range(1, N_DEV):
                g = g + g_ref[s].astype(F32)
        else:
            g = g_ref[...]
        delta, m2, v2 = _adamw_math(w_ref[...], g, m_ref[...], v_ref[...])
        go_ref[...] = g
        d_ref[...] = delta
        mo_ref[...] = m2
        vo_ref[...] = v2

    blk = pl.BlockSpec((tr, cdim), lambda i: (i, 0))
    gspec = pl.BlockSpec((N_DEV, tr, cdim), lambda i: (0, i, 0)) if parts else blk
    return pl.pallas_call(
        body, name=name, grid=(r // tr,),
        in_specs=[blk, gspec, blk, blk], out_specs=[blk] * 4,
        out_shape=[jax.ShapeDtypeStruct((r, cdim), F32)] * 4,
        compiler_params=_cparams(("parallel",)),
    )(w, g, m, v)


def _pad_cols(a, width):
    return jnp.pad(a, ((0, 0), (0, width - a.shape[1])))


def _pack_small_shard(conv_w_sh, meta_sh, width):
    return jnp.concatenate([_pad_cols(conv_w_sh, width), jnp.zeros((4, width), F32), _pad_cols(meta_sh, width)], axis=0)


def _pack_small_rep(norm_pre, norm_post, gate_bias, ssd_norm, conv_b, misc, width):
    rows = [norm_pre, norm_post, gate_bias, ssd_norm, conv_b, misc]
    return jnp.concatenate([_pad_cols(r, width) for r in rows] + [jnp.zeros((2, width), F32)], axis=0)


def _misc_row(dt_bias, fgate_bias, a_log, d_skip, extra):
    hs, ha = dt_bias.shape[1], fgate_bias.shape[1]
    return jnp.concatenate([dt_bias, fgate_bias, jnp.zeros((1, LANES - hs - ha), F32), _pad_cols(a_log, LANES),
                            _pad_cols(d_skip, LANES), _pad_cols(extra, LANES)], axis=1)


def kernel(x, meta_tokens, norm_pre, w_in, conv_w, conv_b, dt_bias, a_log, d_skip, ssd_norm, fgate_bias, gate_bias, w_proj_ssd, w_proj_att, w_out, norm_post, loss_target, m_meta_tokens, m_norm_pre, m_w_in, m_conv_w, m_conv_b, m_dt_bias, m_a_log, m_d_skip, m_ssd_norm, m_fgate_bias, m_gate_bias, m_w_proj_ssd, m_w_proj_att, m_w_out, m_norm_post, v_meta_tokens, v_norm_pre, v_w_in, v_conv_w, v_conv_b, v_dt_bias, v_a_log, v_d_skip, v_ssd_norm, v_fgate_bias, v_gate_bias, v_w_proj_ssd, v_w_proj_att, v_w_out, v_norm_post):
    seq, d = x.shape[1], x.shape[2]
    p = seq + CHUNK
    hs, ha = dt_bias.shape[1], fgate_bias.shape[1]
    ds, cd = ssd_norm.shape[1], conv_b.shape[1]
    da = ha * HEAD_DIM
    nc8 = w_in.shape[2]
    cws = cd // N_DEV
    msh = d // N_DEV
    r1, r2, r3 = ds // N_DEV, da // N_DEV, d // N_DEV
    me = _dev_index(*_my_pos())
    x2, tgt2 = x[0], loss_target[0]

    win_sh = w_in[0].astype(BF16)
    rows_sh = jnp.concatenate([w_proj_ssd[0], w_proj_att[0], w_out[0]], axis=0).astype(BF16)
    small_sh = _pack_small_shard(conv_w[0], meta_tokens, cws)
    win_all, rows_all, small_all = _all_gather([win_sh, rows_sh, small_sh], "gather_weights")
    w_full = jnp.transpose(win_all, (1, 0, 2)).reshape(d, N_DEV * nc8)
    cuts = [0, ds, ds + cd, ds + cd + hs, ds + cd + hs + da, ds + cd + hs + 2 * da, ds + cd + hs + 3 * da,
            ds + cd + hs + 4 * da, ds + cd + hs + 4 * da + ha, ds + cd + hs + 4 * da + ha + 2 * d]
    w_z, w_xbc, w_dt, w_zatt, w_q, w_k, w_v, w_f, w_g = [w_full[:, cuts[i]:cuts[i + 1]] for i in range(9)]
    w_dtf = jnp.concatenate([w_dt, w_f, jnp.zeros((d, LANES - hs - ha), BF16)], axis=1)
    wps = rows_all[:, :r1].reshape(ds, d)
    wpa = rows_all[:, r1:r1 + r2].reshape(da, d)
    wout = rows_all[:, r1 + r2:].reshape(d, d)
    conv_w_full = jnp.transpose(small_all[:, 0:CONV_K, :], (1, 0, 2)).reshape(CONV_K, cd)
    meta_full = jnp.transpose(small_all[:, 8:8 + N_META, :msh], (1, 0, 2)).reshape(N_META, d)
    head = jnp.concatenate([jnp.zeros((PADN, d), F32), meta_full], axis=0)

    u = _prenorm_fwd(head, x2, norm_pre)
    tm = _att_block(p)
    seg_w = [w_z, w_xbc, w_zatt, w_q, w_k, w_v, w_g]
    zs, xbc, zatt, q, k, v, graw = [
        _mm(u, w, "nn", BF16, tm, _tile(w.shape[1], (1024, 512, 256, 128)), "inproj_%d" % i) for i, w in enumerate(seg_w)]
    dtf = _mm(u, w_dtf, "nn", F32, tm, LANES, "inproj_dtf")

    brow = jnp.concatenate([dt_bias, fgate_bias, jnp.zeros((1, LANES - hs - ha), F32)], axis=1)
    alog_row = _pad_cols(a_log, LANES)
    dskip_l = jnp.repeat(d_skip, HEAD_DIM, axis=1)
    sel_t = (lax.broadcasted_iota(jnp.int32, (LANES, ds), 1) // HEAD_DIM
             == lax.broadcasted_iota(jnp.int32, (LANES, ds), 0)).astype(BF16)
    sel = sel_t.T
    y, yssd, hin, cf = _ssd_fwd(xbc, zs, dtf, conv_w_full, conv_b, brow, alog_row, dskip_l, ssd_norm, sel_t, hs, ha)

    blk = _att_block(p)
    nkb, npair = p // blk, ha // 2
    cum = cf[:, hs:hs + ha]
    cq_rep = jnp.repeat(cum, HEAD_DIM, axis=1)
    ck = jnp.transpose(cum.T.reshape(npair, 2, nkb, blk), (0, 2, 1, 3))
    ck = jnp.pad(ck, ((0, 0), (0, 0), (0, 6), (0, 0)))
    o, lse_rep = _attn_fwd(q, k, v, cq_rep, ck, blk)

    (dyssd, d_o, dzatt, dgraw, dzo, mrg, da_, db_, yatt, dout, red_tail) = _tail(
        yssd, o, zatt, graw, head, x2, tgt2, wps, wpa, wout, gate_bias, norm_post)

    tw = _tile(d, (512, 256, 128))
    g_wout = _mm(mrg, dzo, "tn", BF16, tw, tw, "wgrad_out")
    g_wps = _mm(yssd, da_, "tn", BF16, _tile(ds, (512, 256, 128)), tw, "wgrad_ps")
    g_wpa = _mm(yatt, db_, "tn", BF16, _tile(da, (512, 256, 128)), tw, "wgrad_pa")

    dk, dv, dq32, dck, dcq = _attn_bwd(q, k, v, o, d_o, lse_rep, cq_rep, ck, blk)
    dq = dq32.astype(BF16)
    dcum = jnp.transpose(dck[:, :, 0:2, :], (0, 2, 1, 3)).reshape(ha, p).T + dcq[:, ::HEAD_DIM]
    dcf = jnp.pad(dcum, ((0, 0), (hs, LANES - hs - ha)))
    dxbc, dzs, ddtf, gcw, gcb, gnrm, gsm = _ssd_bwd(
        dyssd, y, zs, xbc, dtf, hin, dcf, conv_w_full, conv_b, brow, alog_row, dskip_l, ssd_norm, sel_t, sel, hs, ha)
    ddtf_b = ddtf.astype(BF16)

    dsegs = [dzs, dxbc, dzatt, dq, dk, dv, dgraw, ddtf_b]
    dproj = jnp.concatenate(dsegs, axis=1)
    w_re = jnp.concatenate(seg_w + [w_dtf], axis=1)
    du = _mm(dproj, w_re, "nt", F32, tm, _tile(d, (256, 128)), "dgrad_in")
    gx, ghead, gnp = _prenorm_bwd(head, x2, norm_pre, du, dout)

    gsegs = [_mm(u, dsg, "tn", BF16, tw, _tile(dsg.shape[1], (512, 256, 128)), "wgrad_in_%d" % i)
             for i, dsg in enumerate(dsegs)]
    g_z, g_xbc, g_zatt, g_q, g_k, g_v, g_g, g_dtf = gsegs
    gw_full = jnp.concatenate([g_z, g_xbc, g_dtf[:, :hs], g_zatt, g_q, g_k, g_v, g_dtf[:, hs:hs + ha], g_g], axis=1)
    gwin_parts = jnp.transpose(gw_full.reshape(d, N_DEV, nc8), (1, 0, 2))
    grows_parts = jnp.concatenate([g_wps.reshape(N_DEV, r1, d), g_wpa.reshape(N_DEV, r2, d),
                                   g_wout.reshape(N_DEV, r3, d)], axis=1)

    recv_win, recv_rows = _all_to_all([gwin_parts, grows_parts], "scatter_grads")
    gmisc = jnp.concatenate([gsm[0:1], gsm[1:2], gsm[2:3], _pad_cols(red_tail[2:3, 0:1], LANES)], axis=1)
    small_g = jnp.concatenate([
        _pack_small_rep(gnp[0:1], red_tail[0:1, :d], red_tail[1:2], gnrm[0:1], gcb[0:1], gmisc, cd),
        _pad_cols(gcw[0:CONV_K], cd), jnp.zeros((4, cd), F32), _pad_cols(ghead[PADN:], cd)], axis=0)
    red = _all_reduce_small(small_g, "reduce_small")

    loss = red[5, 3 * LANES]
    g_small_sh = _pack_small_shard(lax.dynamic_slice_in_dim(red[8:8 + CONV_K], me * cws, cws, axis=1),
                                   lax.dynamic_slice_in_dim(red[16:16 + N_META, :d], me * msh, msh, axis=1), cws)

    zero1 = jnp.zeros((1, 1), F32)
    upd_in = _adamw(w_in[0], recv_win, m_w_in[0], v_w_in[0], "adamw_w_in", parts=True)
    cat3 = lambda a, b, c: jnp.concatenate([a[0], b[0], c[0]], axis=0)
    upd_rows = _adamw(cat3(w_proj_ssd, w_proj_att, w_out), recv_rows, cat3(m_w_proj_ssd, m_w_proj_att, m_w_out),
                      cat3(v_w_proj_ssd, v_w_proj_att, v_w_out), "adamw_rows", parts=True)
    rep = lambda a, b, c, e, f, g1, g2, g3, g4: _pack_small_rep(a, b, c, e, f, _misc_row(g1, g2, g3, g4, zero1), cd)
    upd_rep = _adamw(rep(norm_pre, norm_post, gate_bias, ssd_norm, conv_b, dt_bias, fgate_bias, a_log, d_skip),
                     red[0:8],
                     rep(m_norm_pre, m_norm_post, m_gate_bias, m_ssd_norm, m_conv_b, m_dt_bias, m_fgate_bias, m_a_log, m_d_skip),
                     rep(v_norm_pre, v_norm_post, v_gate_bias, v_ssd_norm, v_conv_b, v_dt_bias, v_fgate_bias, v_a_log, v_d_skip),
                     "adamw_rep")
    upd_sh = _adamw(small_sh, g_small_sh, _pack_small_shard(m_conv_w[0], m_meta_tokens, cws),
                    _pack_small_shard(v_conv_w[0], v_meta_tokens, cws), "adamw_small_shard")

    def leaves(i):
        a_in, a_rows, a_rep, a_sh = upd_in[i], upd_rows[i], upd_rep[i], upd_sh[i]
        misc = a_rep[5:6]
        return [a_sh[8:8 + N_META, :msh], a_rep[0:1, :d], a_in[None], a_sh[0:CONV_K][None], a_rep[4:5, :cd],
                misc[:, :hs], misc[:, LANES:LANES + hs], misc[:, 2 * LANES:2 * LANES + hs], a_rep[3:4, :ds],
                misc[:, hs:hs + ha], a_rep[2:3, :2 * d], a_rows[:r1][None], a_rows[r1:r1 + r2][None],
                a_rows[r1 + r2:][None], a_rep[1:2, :d]]

    return tuple([loss, gx[None]] + leaves(0) + leaves(1) + leaves(2) + leaves(3))
```

```python
import functools
import math

import jax
import jax.numpy as jnp
from jax import lax
from jax.experimental import pallas as pl
from jax.experimental.pallas import tpu as pltpu

F32 = jnp.float32
BF16 = jnp.bfloat16

N_DEV = 8
N_META = 16
CHUNK = 128
PADN = CHUNK - N_META
HEAD_DIM = 64
SSD_GROUPS = 4
CONV_K = 4
EPS = 1e-6
NEG = -1e30
LANES = 128
HALO = 16

ADAM_LR = 0.001
ADAM_B1 = 0.9
ADAM_B2 = 0.999
ADAM_EPS = 1e-08
ADAM_WD = 0.01
ADAM_STEP = 10

VMEM_LIMIT = 56 * 1024 * 1024

NN = (((1,), (0,)), ((), ()))
NT = (((1,), (1,)), ((), ()))
TN = (((0,), (0,)), ((), ()))
MESH = pl.DeviceIdType.MESH


def _dot(a, b, dims=NN):
    return lax.dot_general(a, b, dims, preferred_element_type=F32)


def _split2(x):
    hi = x.astype(BF16)
    lo = (x - hi.astype(F32)).astype(BF16)
    return hi, lo


def _dot_sel(x, sel):
    hi, lo = _split2(x)
    return _dot(hi, sel) + _dot(lo, sel)


def _dot_tri(tri, x):
    h1 = x.astype(BF16)
    r1 = x - h1.astype(F32)
    h2 = r1.astype(BF16)
    h3 = (r1 - h2.astype(F32)).astype(BF16)
    return _dot(tri, h1) + _dot(tri, h2) + _dot(tri, h3)


def _sigmoid(x):
    return 1.0 / (1.0 + jnp.exp(-x))


def _softplus(x):
    return jnp.maximum(x, 0.0) + jnp.log(1.0 + jnp.exp(-jnp.abs(x)))


def _cparams(sem=None, vmem=VMEM_LIMIT):
    kw = {"vmem_limit_bytes": vmem}
    if sem is not None:
        kw["dimension_semantics"] = sem
    return pltpu.CompilerParams(**kw)


def _full(shape):
    nd = len(shape)
    return pl.BlockSpec(shape, lambda *_: (0,) * nd)


def _att_block(p):
    return 384 if p % 384 == 0 else CHUNK


def _my_pos():
    return lax.axis_index("x"), lax.axis_index("y"), lax.axis_index("c")


def _dev_index(x, y, c):
    return 4 * x + 2 * y + c


FLIPS = [(fx, fy, fc) for fx in (0, 1) for fy in (0, 1) for fc in (0, 1)][1:]


def _flip(pos, f):
    return tuple((1 - p) if fi else p for p, fi in zip(pos, f))


def _all_gather(bufs, name):
    nb = len(bufs)

    def body(*refs):
        ins, outs = refs[:nb], refs[nb:2 * nb]
        send_sems, recv_sems, local_sems = refs[2 * nb:]
        x, y, c = _my_pos()
        me = _dev_index(x, y, c)
        sibling = (x, y, 1 - c)
        chips = [(1 - x, y), (x, 1 - y), (1 - x, 1 - y)]

        def copy(b, k, block_idx, to, src=None):
            dst = outs[b].at[block_idx]
            return pltpu.make_async_remote_copy(
                src_ref=dst if src is None else src, dst_ref=dst,
                send_sem=send_sems.at[b, k], recv_sem=recv_sems.at[b, k],
                device_id=to, device_id_type=MESH)

        started = []
        for b in range(nb):
            mine = pltpu.make_async_copy(ins[b], outs[b].at[me], local_sems.at[b])
            mine.start()
            started.append(mine)
        first = []
        for b in range(nb):
            first.append(copy(b, 0, me, sibling, src=ins[b]))
            for j, chip in enumerate(chips):
                first.append(copy(b, 1 + j, me, (chip[0], chip[1], c), src=ins[b]))
        for cp in first:
            cp.start()
        passed = []
        for j, chip in enumerate(chips):
            blk = _dev_index(chip[0], chip[1], c)
            for b in range(nb):
                copy(b, 1 + j, blk, (x, y, c)).wait_recv()
                fwd = copy(b, 4 + j, blk, sibling)
                fwd.start()
                passed.append(fwd)
        for b in range(nb):
            copy(b, 0, _dev_index(x, y, 1 - c), (x, y, c)).wait_recv()
        for j, chip in enumerate(chips):
            blk = _dev_index(chip[0], chip[1], 1 - c)
            for b in range(nb):
                copy(b, 4 + j, blk, (x, y, c)).wait_recv()
        for cp in first + passed:
            cp.wait_send()
        for mine in started:
            mine.wait()

    any_spec = pl.BlockSpec(memory_space=pl.ANY)
    return pl.pallas_call(
        body, name=name,
        out_shape=[jax.ShapeDtypeStruct((N_DEV,) + b.shape, b.dtype) for b in bufs],
        in_specs=[any_spec] * nb, out_specs=[any_spec] * nb,
        scratch_shapes=[pltpu.SemaphoreType.DMA((nb, 7)), pltpu.SemaphoreType.DMA((nb, 7)),
                        pltpu.SemaphoreType.DMA((nb,))],
    )(*bufs)


def _all_to_all(bufs, name):
    nb = len(bufs)

    def body(*refs):
        ins, outs = refs[:nb], refs[nb:2 * nb]
        send_sems, recv_sems, local_sems = refs[2 * nb:]
        pos = _my_pos()
        me = _dev_index(*pos)

        def copy(b, k, to):
            return pltpu.make_async_remote_copy(
                src_ref=ins[b].at[_dev_index(*to)], dst_ref=outs[b].at[me],
                send_sem=send_sems.at[b, k], recv_sem=recv_sems.at[b, k],
                device_id=to, device_id_type=MESH)

        local = []
        for b in range(nb):
            cp = pltpu.make_async_copy(ins[b].at[me], outs[b].at[me], local_sems.at[b])
            cp.start()
            local.append(cp)
        sends = []
        for b in range(nb):
            for k, f in enumerate(FLIPS):
                cp = copy(b, k, _flip(pos, f))
                cp.start()
                sends.append(cp)
        for b in range(nb):
            for k, f in enumerate(FLIPS):
                peer = _flip(pos, f)
                pltpu.make_async_remote_copy(
                    src_ref=ins[b].at[me], dst_ref=outs[b].at[_dev_index(*peer)],
                    send_sem=send_sems.at[b, k], recv_sem=recv_sems.at[b, k],
                    device_id=peer, device_id_type=MESH).wait_recv()
        for cp in sends:
            cp.wait_send()
        for cp in local:
            cp.wait()

    any_spec = pl.BlockSpec(memory_space=pl.ANY)
    return pl.pallas_call(
        body, name=name,
        out_shape=[jax.ShapeDtypeStruct(b.shape, b.dtype) for b in bufs],
        in_specs=[any_spec] * nb, out_specs=[any_spec] * nb,
        scratch_shapes=[pltpu.SemaphoreType.DMA((nb, 7)), pltpu.SemaphoreType.DMA((nb, 7)),
                        pltpu.SemaphoreType.DMA((nb,))],
    )(*bufs)


def _all_reduce_small(v, name):
    r, cdim = v.shape

    def body(x_ref, out_ref, slots, send_sems, recv_sems):
        pos = _my_pos()
        me = _dev_index(*pos)
        slots[me] = x_ref[...]
        sends = []
        for k, f in enumerate(FLIPS):
            cp = pltpu.make_async_remote_copy(
                src_ref=x_ref, dst_ref=slots.at[me], send_sem=send_sems.at[k], recv_sem=recv_sems.at[k],
                device_id=_flip(pos, f), device_id_type=MESH)
            cp.start()
            sends.append(cp)
        for k, f in enumerate(FLIPS):
            peer = _flip(pos, f)
            pltpu.make_async_remote_copy(
                src_ref=x_ref, dst_ref=slots.at[_dev_index(*peer)], send_sem=send_sems.at[k],
                recv_sem=recv_sems.at[k], device_id=peer, device_id_type=MESH).wait_recv()
        for cp in sends:
            cp.wait_send()
        acc = slots[0]
        for s in range(1, N_DEV):
            acc = acc + slots[s]
        out_ref[...] = acc

    vm = pl.BlockSpec(memory_space=pltpu.VMEM)
    return pl.pallas_call(
        body, name=name, out_shape=jax.ShapeDtypeStruct(v.shape, F32),
        in_specs=[vm], out_specs=vm,
        scratch_shapes=[pltpu.VMEM((N_DEV, r, cdim), F32), pltpu.SemaphoreType.DMA((7,)),
                        pltpu.SemaphoreType.DMA((7,))],
    )(v)


def _mm(a, b, dims, out_dtype, tm, tn, name):
    if dims == "nn":
        (m, k), (_, n) = a.shape, b.shape
        a_spec = pl.BlockSpec((tm, k), lambda j, i: (i, 0))
        b_spec = pl.BlockSpec((k, tn), lambda j, i: (0, j))
        dn = NN
    elif dims == "nt":
        (m, k), (n, _) = a.shape, b.shape
        a_spec = pl.BlockSpec((tm, k), lambda j, i: (i, 0))
        b_spec = pl.BlockSpec((tn, k), lambda j, i: (j, 0))
        dn = NT
    else:
        (k, m), (_, n) = a.shape, b.shape
        a_spec = pl.BlockSpec((k, tm), lambda j, i: (0, i))
        b_spec = pl.BlockSpec((k, tn), lambda j, i: (0, j))
        dn = TN
    assert m % tm == 0 and n % tn == 0, (m, tm, n, tn)

    def body(a_ref, b_ref, o_ref):
        o_ref[...] = _dot(a_ref[...], b_ref[...], dn).astype(o_ref.dtype)

    return pl.pallas_call(
        body, name=name, grid=(n // tn, m // tm),
        in_specs=[a_spec, b_spec], out_specs=pl.BlockSpec((tm, tn), lambda j, i: (i, j)),
        out_shape=jax.ShapeDtypeStruct((m, n), out_dtype),
        compiler_params=_cparams(("parallel", "parallel")),
    )(a, b)


def _tile(n, prefs):
    for t in prefs:
        if n % t == 0:
            return t
    return n


def _prenorm_fwd(head, x2, w):
    p, d = x2.shape[0] + CHUNK, x2.shape[1]

    def body(head_ref, x_ref, w_ref, u_ref):
        i = pl.program_id(0)
        h = jnp.where(i == 0, head_ref[...], x_ref[...])
        ms = jnp.mean(h * h, axis=-1, keepdims=True)
        u_ref[...] = (h * lax.rsqrt(ms + EPS) * w_ref[...]).astype(BF16)

    return pl.pallas_call(
        body, name="prenorm_fwd", grid=(p // CHUNK,),
        in_specs=[_full((CHUNK, d)), pl.BlockSpec((CHUNK, d), lambda i: (jnp.maximum(i - 1, 0), 0)), _full((1, d))],
        out_specs=pl.BlockSpec((CHUNK, d), lambda i: (i, 0)),
        out_shape=jax.ShapeDtypeStruct((p, d), BF16),
        compiler_params=_cparams(("arbitrary",)),
    )(head, x2, w)


def _prenorm_bwd(head, x2, w, du, dout):
    p, d = x2.shape[0] + CHUNK, x2.shape[1]

    def body(head_ref, x_ref, w_ref, du_ref, dout_ref, gx_ref, ghead_ref, gw_ref):
        i = pl.program_id(0)
        h = jnp.where(i == 0, head_ref[...], x_ref[...])
        rstd = lax.rsqrt(jnp.mean(h * h, axis=-1, keepdims=True) + EPS)
        xhat = h * rstd
        dub = du_ref[...]
        dxh = dub * w_ref[...]
        dh = rstd * (dxh - xhat * jnp.mean(dxh * xhat, axis=-1, keepdims=True)) + dout_ref[...]

        @pl.when(i == 0)
        def _():
            ghead_ref[...] = dh
            gw_ref[...] = jnp.zeros_like(gw_ref)

        gx_ref[...] = dh
        gw_ref[0:1, :] += jnp.sum(dub * xhat, axis=0, keepdims=True)

    return pl.pallas_call(
        body, name="prenorm_bwd", grid=(p // CHUNK,),
        in_specs=[_full((CHUNK, d)), pl.BlockSpec((CHUNK, d), lambda i: (jnp.maximum(i - 1, 0), 0)), _full((1, d)),
                  pl.BlockSpec((CHUNK, d), lambda i: (i, 0)), pl.BlockSpec((CHUNK, d), lambda i: (i, 0))],
        out_specs=[pl.BlockSpec((CHUNK, d), lambda i: (jnp.maximum(i - 1, 0), 0)), _full((CHUNK, d)), _full((8, d))],
        out_shape=[jax.ShapeDtypeStruct(x2.shape, F32), jax.ShapeDtypeStruct((CHUNK, d), F32),
                   jax.ShapeDtypeStruct((8, d), F32)],
        compiler_params=_cparams(("arbitrary",)),
    )(head, x2, w, du, dout)


def _conv_pre(xr, halo128, cw_ref, cb_ref, rows):
    pre = cb_ref[...] + cw_ref[CONV_K - 1:CONV_K, :] * xr
    shifted = []
    for j in range(1, CONV_K):
        sh = jnp.where(rows >= j, pltpu.roll(xr, j, 0), pltpu.roll(halo128, j, 0))
        shifted.append(sh)
        pre = pre + cw_ref[CONV_K - 1 - j:CONV_K - j, :] * sh
    return pre, shifted


def _ssd_scalars(dtf_ref, brow_ref, alog_ref, rowmask, hs, ha, tri):
    lane = lax.broadcasted_iota(jnp.int32, (1, LANES), 1)
    is_dt = lane < hs
    is_f = (lane >= hs) & (lane < hs + ha)
    dtr = dtf_ref[...] + brow_ref[...]
    sp = _softplus(dtr)
    dt = jnp.where(is_dt, sp, 0.0) * rowmask
    logf = jnp.where(is_f, jnp.minimum(dtr, 0.0) - jnp.log(1.0 + jnp.exp(-jnp.abs(dtr))), 0.0) * rowmask
    a_row = jnp.where(is_dt, -jnp.exp(alog_ref[...]), 0.0)
    run = _dot_tri(tri, dt * a_row + logf)
    return dtr, dt, a_row, run, is_dt, is_f


def _tri_mats():
    r = lax.broadcasted_iota(jnp.int32, (CHUNK, CHUNK), 0)
    c = lax.broadcasted_iota(jnp.int32, (CHUNK, CHUNK), 1)
    return r, c


def _ssd_fwd(xbc, z, dtf, conv_w, conv_b, brow, alog, dskip_l, ssd_norm, sel_t, hs, ha):
    p, cd = xbc.shape
    ds = z.shape[1]
    ns = (cd - ds) // (2 * SSD_GROUPS)
    gw = ds // SSD_GROUPS
    nch = p // CHUNK
    hpg = hs // SSD_GROUPS

    def body(xbc_ref, halo_ref, z_ref, dtf_ref, cw_ref, cb_ref, brow_ref, alog_ref, dsk_ref, nrm_ref, selt_ref,
             y_ref, yssd_ref, hin_ref, cf_ref, st_ref, carry_ref, yacc_ref):
        c = pl.program_id(0)

        @pl.when(c == 0)
        def _():
            st_ref[...] = jnp.zeros_like(st_ref)
            carry_ref[...] = jnp.zeros_like(carry_ref)

        rows = lax.broadcasted_iota(jnp.int32, (CHUNK, 1), 0)
        rowmask = jnp.where((rows >= PADN) | (c > 0), 1.0, 0.0)
        ri, ci = _tri_mats()
        causal = ri >= ci
        tri = jnp.where(causal, 1.0, 0.0).astype(BF16)

        xr = xbc_ref[...].astype(F32)
        halo = halo_ref[...].astype(F32) * jnp.where(c > 0, 1.0, 0.0)
        halo128 = jnp.concatenate([jnp.zeros((CHUNK - HALO, cd), F32), halo], axis=0)
        pre, _ = _conv_pre(xr, halo128, cw_ref, cb_ref, rows)
        xc = pre * _sigmoid(pre) * rowmask

        dtr, dt, a_row, run, is_dt, is_f = _ssd_scalars(dtf_ref, brow_ref, alog_ref, rowmask, hs, ha, tri)
        cf = run + carry_ref[...]
        cf_ref[...] = cf
        carry_ref[...] = jnp.where(is_f, cf[CHUNK - 1:CHUNK, :], 0.0)
        cs = jnp.where(is_dt, run, 0.0)
        cl = cs[CHUNK - 1:CHUNK, :]
        selt = selt_ref[...]
        dt_x = _dot_sel(dt, selt)
        e_x = _dot_sel(jnp.exp(cs), selt)
        w_x = _dot_sel(jnp.exp(cl - cs), selt)
        cdec_x = _dot_sel(jnp.broadcast_to(jnp.exp(cl), (8, LANES)), selt)[0:1, :]
        cs_t = cs.T

        xs = xc[:, :ds]
        xdt = xs * dt_x
        xdt_b = xdt.astype(BF16)
        xw_b = (xdt * w_x).astype(BF16)
        lane = lax.broadcasted_iota(jnp.int32, (1, LANES), 1)
        half0 = lane < HEAD_DIM
        for g in range(SSD_GROUPS):
            bg = xc[:, ds + g * ns: ds + (g + 1) * ns].astype(BF16)
            cg = xc[:, ds + SSD_GROUPS * ns + g * ns: ds + SSD_GROUPS * ns + (g + 1) * ns].astype(BF16)
            gm = _dot(cg, bg, NT)
            gs = slice(g * gw, (g + 1) * gw)
            stg = st_ref[:, gs]
            stg_b = stg.astype(BF16)
            hin_ref[0, :, gs] = stg_b
            yoff = _dot(cg, stg_b) * e_x[:, gs]
            for pr in range(gw // LANES):
                sl = slice(g * gw + pr * LANES, g * gw + (pr + 1) * LANES)
                xp = xdt_b[:, sl]
                yd = jnp.zeros((CHUNK, LANES), F32)
                for j in range(2):
                    h = g * hpg + 2 * pr + j
                    seg = cs[:, h:h + 1] - cs_t[h:h + 1, :]
                    m = jnp.where(causal, gm * jnp.exp(jnp.minimum(seg, 0.0)), 0.0).astype(BF16)
                    sel = half0 if j == 0 else jnp.logical_not(half0)
                    yd = yd + _dot(m, jnp.where(sel, xp, jnp.zeros_like(xp)))
                yacc_ref[:, sl] = yd + yoff[:, pr * LANES:(pr + 1) * LANES] + dsk_ref[:, sl] * xs[:, sl]
            st_ref[:, gs] = stg * cdec_x[:, gs] + _dot(bg, xw_b[:, gs], TN)

        y = yacc_ref[...]
        y_ref[...] = y.astype(BF16)
        zf = z_ref[...].astype(F32)
        u = y * zf * _sigmoid(zf)
        for g in range(SSD_GROUPS):
            gs = slice(g * gw, (g + 1) * gw)
            ug = u[:, gs]
            ms = jnp.mean(ug * ug, axis=-1, keepdims=True)
            yssd_ref[:, gs] = (ug * lax.rsqrt(ms + EPS) * nrm_ref[:, gs]).astype(BF16)

    rb = CHUNK // HALO
    return pl.pallas_call(
        body, name="ssd_fwd", grid=(nch,),
        in_specs=[pl.BlockSpec((CHUNK, cd), lambda c: (c, 0)),
                  pl.BlockSpec((HALO, cd), lambda c: (jnp.maximum(c * rb - 1, 0), 0)),
                  pl.BlockSpec((CHUNK, ds), lambda c: (c, 0)),
                  pl.BlockSpec((CHUNK, LANES), lambda c: (c, 0)),
                  _full((CONV_K, cd)), _full((1, cd)), _full((1, LANES)), _full((1, LANES)),
                  _full((1, ds)), _full((1, ds)), _full((LANES, ds))],
        out_specs=[pl.BlockSpec((CHUNK, ds), lambda c: (c, 0)), pl.BlockSpec((CHUNK, ds), lambda c: (c, 0)),
                   pl.BlockSpec((1, ns, ds), lambda c: (c, 0, 0)), pl.BlockSpec((CHUNK, LANES), lambda c: (c, 0))],
        out_shape=[jax.ShapeDtypeStruct((p, ds), BF16), jax.ShapeDtypeStruct((p, ds), BF16),
                   jax.ShapeDtypeStruct((nch, ns, ds), BF16), jax.ShapeDtypeStruct((p, LANES), F32)],
        scratch_shapes=[pltpu.VMEM((ns, ds), F32), pltpu.VMEM((1, LANES), F32), pltpu.VMEM((CHUNK, ds), F32)],
        compiler_params=_cparams(("arbitrary",)),
    )(xbc, xbc, z, dtf, conv_w, conv_b, brow, alog, dskip_l, ssd_norm, sel_t)


def _ssd_bwd(dyssd, y, z, xbc, dtf, hin, dcf, conv_w, conv_b, brow, alog, dskip_l, ssd_norm, sel_t, sel, hs, ha):
    p, cd = xbc.shape
    ds = z.shape[1]
    ns = (cd - ds) // (2 * SSD_GROUPS)
    gw = ds // SSD_GROUPS
    nch = p // CHUNK
    hpg = hs // SSD_GROUPS
    rb = CHUNK // HALO

    def body(dyssd_ref, y_ref, z_ref, xbc_ref, halo_ref, dtf_ref, hin_ref, dcf_ref, cw_ref, cb_ref, brow_ref,
             alog_ref, dsk_ref, nrm_ref, selt_ref, sel_ref,
             dxbc_ref, dz_ref, ddtf_ref, gcw_ref, gcb_ref, gnrm_ref, gsm_ref,
             dst_ref, nxt_ref, fcar_ref, gdsk_ref, dxc_ref):
        step = pl.program_id(0)
        c = nch - 1 - step

        @pl.when(step == 0)
        def _():
            dst_ref[...] = jnp.zeros_like(dst_ref)
            nxt_ref[...] = jnp.zeros_like(nxt_ref)
            fcar_ref[...] = jnp.zeros_like(fcar_ref)
            gdsk_ref[...] = jnp.zeros_like(gdsk_ref)
            gcw_ref[...] = jnp.zeros_like(gcw_ref)
            gcb_ref[...] = jnp.zeros_like(gcb_ref)
            gnrm_ref[...] = jnp.zeros_like(gnrm_ref)
            gsm_ref[...] = jnp.zeros_like(gsm_ref)

        rows = lax.broadcasted_iota(jnp.int32, (CHUNK, 1), 0)
        rowmask = jnp.where((rows >= PADN) | (c > 0), 1.0, 0.0)
        ri, ci = _tri_mats()
        causal = ri >= ci
        anti = ci >= ri
        tri = jnp.where(causal, 1.0, 0.0).astype(BF16)
        rtri = jnp.where(anti, 1.0, 0.0).astype(BF16)

        xr = xbc_ref[...].astype(F32)
        halo = halo_ref[...].astype(F32) * jnp.where(c > 0, 1.0, 0.0)
        halo128 = jnp.concatenate([jnp.zeros((CHUNK - HALO, cd), F32), halo], axis=0)
        pre, shifted = _conv_pre(xr, halo128, cw_ref, cb_ref, rows)
        sg = _sigmoid(pre)
        xc = pre * sg * rowmask
        dsilu = sg * (1.0 + pre * (1.0 - sg)) * rowmask

        dtr, dt, a_row, run, is_dt, is_f = _ssd_scalars(dtf_ref, brow_ref, alog_ref, rowmask, hs, ha, tri)
        cs = jnp.where(is_dt, run, 0.0)
        cl = cs[CHUNK - 1:CHUNK, :]
        selt = selt_ref[...]
        selm = sel_ref[...]
        dt_x = _dot_sel(dt, selt)
        e_x = _dot_sel(jnp.exp(cs), selt)
        w_x = _dot_sel(jnp.exp(cl - cs), selt)
        cdec = jnp.exp(cl)
        cdec_x = _dot_sel(jnp.broadcast_to(cdec, (8, LANES)), selt)[0:1, :]
        cs_t = cs.T
        xs = xc[:, :ds]
        xdt = xs * dt_x
        xdt_b = xdt.astype(BF16)
        xw_b = (xdt * w_x).astype(BF16)

        yv = y_ref[...].astype(F32)
        zf = z_ref[...].astype(F32)
        sz = _sigmoid(zf)
        u = yv * zf * sz
        dyo = dyssd_ref[...].astype(F32)
        du_parts = []
        for g in range(SSD_GROUPS):
            gs = slice(g * gw, (g + 1) * gw)
            ug = u[:, gs]
            rstd = lax.rsqrt(jnp.mean(ug * ug, axis=-1, keepdims=True) + EPS)
            yhat = ug * rstd
            dyg = dyo[:, gs]
            gnrm_ref[0:1, gs] += jnp.sum(dyg * yhat, axis=0, keepdims=True)
            dyh = dyg * nrm_ref[:, gs]
            du_parts.append(rstd * (dyh - yhat * jnp.mean(dyh * yhat, axis=-1, keepdims=True)))
        du = jnp.concatenate(du_parts, axis=1)
        dy = du * zf * sz
        dz_ref[...] = (du * yv * sz * (1.0 + zf * (1.0 - sz))).astype(BF16)

        dsk = dsk_ref[...]
        gdsk_ref[...] += jnp.sum(dy * xs, axis=0, keepdims=True)
        dy_b = dy.astype(BF16)
        dye_b = (dy * e_x).astype(BF16)
        lane = lax.broadcasted_iota(jnp.int32, (1, LANES), 1)
        half0 = lane < HEAD_DIM
        x_parts, yo_parts, t4_parts = [], [], []
        dcs = jnp.zeros((CHUNK, LANES), F32)
        for g in range(SSD_GROUPS):
            gs = slice(g * gw, (g + 1) * gw)
            bsl = slice(ds + g * ns, ds + (g + 1) * ns)
            csl = slice(ds + SSD_GROUPS * ns + g * ns, ds + SSD_GROUPS * ns + (g + 1) * ns)
            bg = xc[:, bsl].astype(BF16)
            cg = xc[:, csl].astype(BF16)
            gm = _dot(cg, bg, NT)
            gm_t = _dot(bg, cg, NT)
            stg_b = hin_ref[0, :, gs]
            dstg = dst_ref[:, gs]
            dstg_b = dstg.astype(BF16)
            t4_parts.append(jnp.sum(dstg * stg_b.astype(F32), axis=0, keepdims=True))
            zst = _dot(bg, dstg_b) * w_x[:, gs]
            x_parts.append(xdt[:, gs] * zst)
            yo_parts.append(dy[:, gs] * (_dot(cg, stg_b) * e_x[:, gs]))
            dgsum = jnp.zeros((CHUNK, CHUNK), F32)
            dgtsum = jnp.zeros((CHUNK, CHUNK), F32)
            for pr in range(gw // LANES):
                sl = slice(g * gw + pr * LANES, g * gw + (pr + 1) * LANES)
                xp = xdt_b[:, sl]
                dyp = dy_b[:, sl]
                dxd = zst[:, pr * LANES:(pr + 1) * LANES]
                for j in range(2):
                    h = g * hpg + 2 * pr + j
                    sel_l = half0 if j == 0 else jnp.logical_not(half0)
                    seg = cs[:, h:h + 1] - cs_t[h:h + 1, :]
                    lm = jnp.where(causal, jnp.exp(jnp.minimum(seg, 0.0)), 0.0)
                    lmt = jnp.where(anti, jnp.exp(jnp.minimum(-seg, 0.0)), 0.0)
                    dyp_m = jnp.where(sel_l, dyp, jnp.zeros_like(dyp))
                    xp_m = jnp.where(sel_l, xp, jnp.zeros_like(xp))
                    dxd = dxd + _dot((gm_t * lmt).astype(BF16), dyp_m)
                    dg = _dot(dyp_m, xp, NT) * lm
                    dgt = _dot(xp_m, dyp, NT) * lmt
                    dgsum = dgsum + dg
                    dgtsum = dgtsum + dgt
                    qrow = (jnp.sum(dg * gm, axis=1, keepdims=True) - jnp.sum(dgt * gm_t, axis=1, keepdims=True))
                    dcs = dcs + jnp.where(lane == h, qrow, 0.0)
                dxc_ref[:, sl] = dxd
            dxc_ref[:, csl] = _dot(dgsum.astype(BF16), bg) + _dot(dye_b[:, gs], stg_b, NT)
            dxc_ref[:, bsl] = _dot(dgtsum.astype(BF16), cg) + _dot(xw_b[:, gs], dstg_b, NT)
            dst_ref[:, gs] = dstg * cdec_x[:, gs] + _dot(cg, dye_b[:, gs], TN)

        dxdt = dxc_ref[:, :ds]
        xst = _dot_sel(jnp.concatenate(x_parts, axis=1), selm)
        yo = _dot_sel(jnp.concatenate(yo_parts, axis=1), selm)
        t4 = _dot_sel(jnp.concatenate([jnp.concatenate(t4_parts, axis=1), jnp.zeros((7, ds), F32)], axis=0), selm)
        dcl = jnp.sum(xst, axis=0, keepdims=True) + cdec * t4[0:1, :]
        dcs = dcs + yo - xst + jnp.where(rows == CHUNK - 1, dcl, 0.0)
        da_ = _dot_tri(rtri, dcs)
        ddt = _dot_sel(dxdt * xs, selm) + da_ * a_row
        dcf_blk = dcf_ref[...]
        dlogf = _dot_tri(rtri, dcf_blk) + fcar_ref[...]
        fcar_ref[...] += jnp.sum(dcf_blk, axis=0, keepdims=True)
        sgd = _sigmoid(dtr)
        ddtf = (jnp.where(is_dt, ddt * sgd, 0.0) + jnp.where(is_f, dlogf * (1.0 - sgd), 0.0)) * rowmask
        ddtf_ref[...] = ddtf
        gsm_ref[0:1, :] += jnp.sum(ddtf, axis=0, keepdims=True)
        gsm_ref[1:2, :] += jnp.sum(da_ * dt, axis=0, keepdims=True) * a_row

        dxc_ref[:, :ds] = dxdt * dt_x + dsk * dy
        dpre = dxc_ref[...] * dsilu
        gcb_ref[0:1, :] += jnp.sum(dpre, axis=0, keepdims=True)
        gcw_ref[CONV_K - 1:CONV_K, :] += jnp.sum(dpre * xr, axis=0, keepdims=True)
        nxt128 = jnp.concatenate([nxt_ref[...], jnp.zeros((CHUNK - 8, cd), F32)], axis=0)
        dxr = cw_ref[CONV_K - 1:CONV_K, :] * dpre
        for j in range(1, CONV_K):
            gcw_ref[CONV_K - 1 - j:CONV_K - j, :] += jnp.sum(dpre * shifted[j - 1], axis=0, keepdims=True)
            up = jnp.where(rows < CHUNK - j, pltpu.roll(dpre, CHUNK - j, 0), pltpu.roll(nxt128, CHUNK - j, 0))
            dxr = dxr + cw_ref[CONV_K - 1 - j:CONV_K - j, :] * up
        nxt_ref[...] = dpre[0:8, :]
        dxbc_ref[...] = dxr.astype(BF16)

        @pl.when(step == nch - 1)
        def _():
            gsm_ref[2:3, :] = _dot_sel(jnp.broadcast_to(gdsk_ref[...], (8, ds)), selm)[0:1, :]

    rev = lambda s: nch - 1 - s
    blk = lambda w: pl.BlockSpec((CHUNK, w), lambda s: (rev(s), 0))
    return pl.pallas_call(
        body, name="ssd_bwd", grid=(nch,),
        in_specs=[blk(ds), blk(ds), blk(ds), blk(cd),
                  pl.BlockSpec((HALO, cd), lambda s: (jnp.maximum(rev(s) * rb - 1, 0), 0)),
                  blk(LANES), pl.BlockSpec((1, ns, ds), lambda s: (rev(s), 0, 0)), blk(LANES),
                  _full((CONV_K, cd)), _full((1, cd)), _full((1, LANES)), _full((1, LANES)),
                  _full((1, ds)), _full((1, ds)), _full((LANES, ds)), _full((ds, LANES))],
        out_specs=[blk(cd), blk(ds), blk(LANES), _full((8, cd)), _full((8, cd)), _full((8, ds)), _full((8, LANES))],
        out_shape=[jax.ShapeDtypeStruct((p, cd), BF16), jax.ShapeDtypeStruct((p, ds), BF16),
                   jax.ShapeDtypeStruct((p, LANES), F32), jax.ShapeDtypeStruct((8, cd), F32),
                   jax.ShapeDtypeStruct((8, cd), F32), jax.ShapeDtypeStruct((8, ds), F32),
                   jax.ShapeDtypeStruct((8, LANES), F32)],
        scratch_shapes=[pltpu.VMEM((ns, ds), F32), pltpu.VMEM((8, cd), F32), pltpu.VMEM((1, LANES), F32),
                        pltpu.VMEM((1, ds), F32), pltpu.VMEM((CHUNK, cd), F32)],
        compiler_params=_cparams(("arbitrary",)),
    )(dyssd, y, z, xbc, xbc, dtf, hin, dcf, conv_w, conv_b, brow, alog, dskip_l, ssd_norm, sel_t, sel)


def _attn_fwd(q, k, v, ck, blk):
    p, da = q.shape
    npair, nkb = ck.shape[0], ck.shape[1]
    scale = 1.0 / math.sqrt(HEAD_DIM)

    def body(q_ref, k_ref, v_ref, ck_ref, o_ref, lse_ref):
        i = pl.program_id(1)
        lane = lax.broadcasted_iota(jnp.int32, (1, LANES), 1)
        sels = [lane < HEAD_DIM, lane >= HEAD_DIM]
        ones = [jnp.where(lane == HEAD_DIM, 1.0, 0.0).astype(BF16), jnp.where(lane == 0, 1.0, 0.0).astype(BF16)]
        qb = q_ref[...] * scale
        qms = [jnp.where(sel, qb, jnp.zeros_like(qb)) for sel in sels]
        cmask = (lax.broadcasted_iota(jnp.int32, (blk, blk), 1) <= lax.broadcasted_iota(jnp.int32, (blk, blk), 0))

        def step(kb, carry, masked):
            r0 = pl.multiple_of(kb * blk, blk)
            ks = k_ref[pl.ds(r0, blk), :]
            vs = v_ref[pl.ds(r0, blk), :]
            out = []
            for j in range(2):
                m, acc = carry[2 * j], carry[2 * j + 1]
                s = _dot(qms[j], ks, NT) - ck_ref[0, kb, j:j + 1, :]
                if masked:
                    s = jnp.where(cmask, s, NEG)
                mn = jnp.maximum(m, jnp.max(s, axis=-1, keepdims=True))
                pr = jnp.exp(s - mn).astype(BF16)
                acc = jnp.exp(m - mn) * acc + _dot(pr, jnp.where(sels[j], vs, ones[j]))
                out += [mn, acc]
            return tuple(out)

        init = (jnp.full((blk, 1), NEG, F32), jnp.zeros((blk, LANES), F32)) * 2
        carry = lax.fori_loop(0, i, lambda kb, c: step(kb, c, False), init)
        m0, a0, m1, a1 = step(i, carry, True)
        l0 = a0[:, HEAD_DIM:HEAD_DIM + 1]
        l1 = a1[:, 0:1]
        o_ref[...] = jnp.where(sels[0], a0 / l0, a1 / l1).astype(BF16)
        lse_ref[...] = jnp.where(sels[0], m0 + jnp.log(l0), m1 + jnp.log(l1))

    return pl.pallas_call(
        body, name="attn_fwd", grid=(npair, p // blk),
        in_specs=[pl.BlockSpec((blk, LANES), lambda h, i: (i, h)),
                  pl.BlockSpec((p, LANES), lambda h, i: (0, h)), pl.BlockSpec((p, LANES), lambda h, i: (0, h)),
                  pl.BlockSpec((1, nkb, 8, blk), lambda h, i: (h, 0, 0, 0))],
        out_specs=[pl.BlockSpec((blk, LANES), lambda h, i: (i, h)), pl.BlockSpec((blk, LANES), lambda h, i: (i, h))],
        out_shape=[jax.ShapeDtypeStruct((p, da), BF16), jax.ShapeDtypeStruct((p, da), F32)],
        compiler_params=_cparams(("parallel", "arbitrary")),
    )(q, k, v, ck)


def _attn_bwd(q, k, v, o, do, lse_rep, ck, blk):
    p, da = q.shape
    npair, nkb = ck.shape[0], ck.shape[1]
    nq = p // blk
    scale = 1.0 / math.sqrt(HEAD_DIM)

    def body(k_ref, v_ref, q_ref, do_ref, o_ref, lse_ref, ck_ref, dk_ref, dv_ref, dq_ref, dcs_ref, rsum_ref, dq_acc):
        jb = pl.program_id(1)

        @pl.when(jb == 0)
        def _():
            dq_acc[...] = jnp.zeros_like(dq_acc)

        ks = k_ref[...]
        vs = v_ref[...]
        lane = lax.broadcasted_iota(jnp.int32, (1, LANES), 1)
        sels = [lane < HEAD_DIM, lane >= HEAD_DIM]
        ones = [jnp.where(lane == HEAD_DIM, 1.0, 0.0).astype(BF16), jnp.where(lane == 0, 1.0, 0.0).astype(BF16)]
        kss = ks * scale
        kmo = [jnp.where(sels[j], kss, ones[j]) for j in range(2)]
        cmask = (lax.broadcasted_iota(jnp.int32, (blk, blk), 1) <= lax.broadcasted_iota(jnp.int32, (blk, blk), 0))

        def step(ib, carry, masked):
            r0 = pl.multiple_of(ib * blk, blk)
            qb = q_ref[pl.ds(r0, blk), :] * scale
            dob = do_ref[pl.ds(r0, blk), :]
            prod = dob.astype(F32) * o_ref[pl.ds(r0, blk), :].astype(F32)
            out = []
            for j in range(2):
                dk, dv = carry[2 * j], carry[2 * j + 1]
                qm = jnp.where(sels[j], qb, jnp.zeros_like(qb))
                dom = jnp.where(sels[j], dob, jnp.zeros_like(dob))
                lse = lse_ref[pl.ds(r0, blk), HEAD_DIM * j:HEAD_DIM * j + 1]
                dlt = jnp.sum(jnp.where(sels[j], prod, 0.0), axis=-1, keepdims=True)
                s = _dot(qm, ks, NT) - ck_ref[0, 0, j:j + 1, :] - lse
                pm = jnp.exp(jnp.minimum(s, 0.0))
                if masked:
                    pm = jnp.where(cmask, pm, 0.0)
                ds_b = (pm * (_dot(dom, vs, NT) - dlt)).astype(BF16)
                dv = dv + _dot(pm.astype(BF16), dom, TN)
                dk = dk + _dot(ds_b, jnp.where(sels[j], qb, ones[j]), TN)
                dq_acc[pl.ds(r0, blk), LANES * j:LANES * (j + 1)] += _dot(ds_b, kmo[j])
                out += [dk, dv]
            return tuple(out)

        zero = jnp.zeros((blk, LANES), F32)
        carry = step(jb, (zero, zero, zero, zero), True)
        dk0, dv0, dk1, dv1 = lax.fori_loop(jb + 1, nq, lambda ib, c: step(ib, c, False), carry)
        dk_ref[...] = jnp.where(sels[0], dk0, dk1).astype(BF16)
        dv_ref[...] = (dv0 + dv1).astype(BF16)
        dcs_ref[...] = jnp.where(sels[0], dk0[:, HEAD_DIM:HEAD_DIM + 1], dk1[:, 0:1])

        @pl.when(jb == nkb - 1)
        def _():
            a0 = dq_acc[:, :LANES]
            a1 = dq_acc[:, LANES:]
            dq_ref[...] = jnp.where(sels[0], a0, a1).astype(BF16)
            rsum_ref[...] = jnp.where(sels[0], a0[:, HEAD_DIM:HEAD_DIM + 1], a1[:, 0:1])

    colblk = pl.BlockSpec((blk, LANES), lambda h, j: (j, h))
    colfull = pl.BlockSpec((p, LANES), lambda h, j: (0, h))
    ckspec = pl.BlockSpec((1, 1, 8, blk), lambda h, j: (h, j, 0, 0))
    return pl.pallas_call(
        body, name="attn_bwd", grid=(npair, nkb),
        in_specs=[colblk, colblk, colfull, colfull, colfull, colfull, ckspec],
        out_specs=[colblk, colblk, colfull, colblk, colfull],
        out_shape=[jax.ShapeDtypeStruct((p, da), BF16), jax.ShapeDtypeStruct((p, da), BF16),
                   jax.ShapeDtypeStruct((p, da), BF16), jax.ShapeDtypeStruct((p, da), F32),
                   jax.ShapeDtypeStruct((p, da), F32)],
        scratch_shapes=[pltpu.VMEM((p, 2 * LANES), F32)],
        compiler_params=_cparams(("parallel", "arbitrary")),
    )(k, v, q, do, o, lse_rep, ck)


def _tail(yssd, o, zatt, graw, head, x2, tgt2, wps, wpa, wout, gate_bias, norm_post):
    p, ds = yssd.shape
    da = o.shape[1]
    d = x2.shape[1]

    def body(yssd_ref, o_ref, zatt_ref, g_ref, head_ref, x_ref, tgt_ref, wps_ref, wpa_ref, wout_ref, gb_ref, np_ref,
             dyssd_ref, do_ref, dzatt_ref, dg_ref, dzo_ref, mrg_ref, da_ref, db_ref, yatt_ref, dout_ref, red_ref):
        i = pl.program_id(0)

        @pl.when(i == 0)
        def _():
            red_ref[...] = jnp.zeros_like(red_ref)

        h = jnp.where(i == 0, head_ref[...], x_ref[...])
        valid = jnp.where(i > 0, 1.0, 0.0)
        ob = o_ref[...].astype(F32)
        za = zatt_ref[...].astype(F32)
        sza = _sigmoid(za)
        silu = za * sza
        yatt_b = (ob * silu).astype(BF16)
        yatt_ref[...] = yatt_b
        wps_v, wpa_v, wout_v = wps_ref[...], wpa_ref[...], wout_ref[...]
        a = _dot(yssd_ref[...], wps_v)
        b = _dot(yatt_b, wpa_v)
        gr = g_ref[...].astype(F32) + gb_ref[...]
        gs = _sigmoid(gr[:, :d])
        ga = _sigmoid(gr[:, d:])
        mrg_b = (gs * a + ga * b).astype(BF16)
        mrg_ref[...] = mrg_b
        zo = _dot(mrg_b, wout_v)
        rstd = lax.rsqrt(jnp.mean(zo * zo, axis=-1, keepdims=True) + EPS)
        zh = zo * rstd
        npw = np_ref[...]
        err = (h + zh * npw - tgt_ref[...]) * valid
        dout = err * (1.0 / d)
        dout_ref[...] = dout
        dzh = dout * npw
        dzo_b = (rstd * (dzh - zh * jnp.mean(dzh * zh, axis=-1, keepdims=True))).astype(BF16)
        dzo_ref[...] = dzo_b
        dm = _dot(dzo_b, wout_v, NT)
        da_b = (gs * dm).astype(BF16)
        db_b = (ga * dm).astype(BF16)
        da_ref[...] = da_b
        db_ref[...] = db_b
        dgs = dm * a * gs * (1.0 - gs)
        dga = dm * b * ga * (1.0 - ga)
        dg_ref[:, :d] = dgs.astype(BF16)
        dg_ref[:, d:] = dga.astype(BF16)
        dyssd_ref[...] = _dot(da_b, wps_v, NT).astype(BF16)
        dya = _dot(db_b, wpa_v, NT)
        do_ref[...] = (dya * silu).astype(BF16)
        dzatt_ref[...] = (dya * ob * sza * (1.0 + za * (1.0 - sza))).astype(BF16)
        red_ref[0:1, :d] += jnp.sum(dout * zh, axis=0, keepdims=True)
        red_ref[1:2, :d] += jnp.sum(dgs, axis=0, keepdims=True)
        red_ref[1:2, d:] += jnp.sum(dga, axis=0, keepdims=True)
        red_ref[2:3, 0:1] += jnp.sum(jnp.sum(err * err, axis=1, keepdims=True), axis=0, keepdims=True) * (0.5 / d)

    row = lambda w: pl.BlockSpec((CHUNK, w), lambda i: (i, 0))
    shifted = lambda w: pl.BlockSpec((CHUNK, w), lambda i: (jnp.maximum(i - 1, 0), 0))
    sd = jax.ShapeDtypeStruct
    return pl.pallas_call(
        body, name="tail", grid=(p // CHUNK,),
        in_specs=[row(ds), row(da), row(da), row(2 * d), _full((CHUNK, d)), shifted(d), shifted(d),
                  _full((ds, d)), _full((da, d)), _full((d, d)), _full((1, 2 * d)), _full((1, d))],
        out_specs=[row(ds), row(da), row(da), row(2 * d), row(d), row(d), row(d), row(d), row(da), row(d),
                   _full((8, 2 * d))],
        out_shape=[sd((p, ds), BF16), sd((p, da), BF16), sd((p, da), BF16), sd((p, 2 * d), BF16), sd((p, d), BF16),
                   sd((p, d), BF16), sd((p, d), BF16), sd((p, d), BF16), sd((p, da), BF16), sd((p, d), F32),
                   sd((8, 2 * d), F32)],
        compiler_params=_cparams(("arbitrary",)),
    )(yssd, o, zatt, graw, head, x2, tgt2, wps, wpa, wout, gate_bias, norm_post)


def _adamw_math(w, g, m, v):
    m2 = ADAM_B1 * m + (1.0 - ADAM_B1) * g
    v2 = ADAM_B2 * v + (1.0 - ADAM_B2) * (g * g)
    m_hat = m2 / (1.0 - ADAM_B1 ** ADAM_STEP)
    v_hat = v2 / (1.0 - ADAM_B2 ** ADAM_STEP)
    delta = -ADAM_LR * (m_hat / (jnp.sqrt(v_hat) + ADAM_EPS) + ADAM_WD * w)
    return delta, m2, v2


def _adamw(w, g, m, v, name, parts=False):
    r, cdim = w.shape
    tr = CHUNK if r % CHUNK == 0 else r

    def body(w_ref, g_ref, m_ref, v_ref, go_ref, d_ref, mo_ref, vo_ref):
        if parts:
            g = g_ref[0].astype(F32)
            for s in range(1, N_DEV):
                g = g + g_ref[s].astype(F32)
        else:
            g = g_ref[...]
        delta, m2, v2 = _adamw_math(w_ref[...], g, m_ref[...], v_ref[...])
        go_ref[...] = g
        d_ref[...] = delta
        mo_ref[...] = m2
        vo_ref[...] = v2

    blk = pl.BlockSpec((tr, cdim), lambda i: (i, 0))
    gspec = pl.BlockSpec((N_DEV, tr, cdim), lambda i: (0, i, 0)) if parts else blk
    return pl.pallas_call(
        body, name=name, grid=(r // tr,),
        in_specs=[blk, gspec, blk, blk], out_specs=[blk] * 4,
        out_shape=[jax.ShapeDtypeStruct((r, cdim), F32)] * 4,
        compiler_params=_cparams(("parallel",)),
    )(w, g, m, v)


def _pad_cols(a, width):
    return jnp.pad(a, ((0, 0), (0, width - a.shape[1])))


def _pack_small_shard(conv_w_sh, meta_sh, width):
    return jnp.concatenate([_pad_cols(conv_w_sh, width), jnp.zeros((4, width), F32), _pad_cols(meta_sh, width)], axis=0)


def _pack_small_rep(norm_pre, norm_post, gate_bias, ssd_norm, conv_b, misc, width):
    rows = [norm_pre, norm_post, gate_bias, ssd_norm, conv_b, misc]
    return jnp.concatenate([_pad_cols(r, width) for r in rows] + [jnp.zeros((2, width), F32)], axis=0)


def _misc_row(dt_bias, fgate_bias, a_log, d_skip, extra):
    hs, ha = dt_bias.shape[1], fgate_bias.shape[1]
    return jnp.concatenate([dt_bias, fgate_bias, jnp.zeros((1, LANES - hs - ha), F32), _pad_cols(a_log, LANES),
                            _pad_cols(d_skip, LANES), _pad_cols(extra, LANES)], axis=1)


def kernel(x, meta_tokens, norm_pre, w_in, conv_w, conv_b, dt_bias, a_log, d_skip, ssd_norm, fgate_bias, gate_bias, w_proj_ssd, w_proj_att, w_out, norm_post, loss_target, m_meta_tokens, m_norm_pre, m_w_in, m_conv_w, m_conv_b, m_dt_bias, m_a_log, m_d_skip, m_ssd_norm, m_fgate_bias, m_gate_bias, m_w_proj_ssd, m_w_proj_att, m_w_out, m_norm_post, v_meta_tokens, v_norm_pre, v_w_in, v_conv_w, v_conv_b, v_dt_bias, v_a_log, v_d_skip, v_ssd_norm, v_fgate_bias, v_gate_bias, v_w_proj_ssd, v_w_proj_att, v_w_out, v_norm_post):
    seq, d = x.shape[1], x.shape[2]
    p = seq + CHUNK
    hs, ha = dt_bias.shape[1], fgate_bias.shape[1]
    ds, cd = ssd_norm.shape[1], conv_b.shape[1]
    da = ha * HEAD_DIM
    nc8 = w_in.shape[2]
    cws = cd // N_DEV
    msh = d // N_DEV
    r1, r2, r3 = ds // N_DEV, da // N_DEV, d // N_DEV
    me = _dev_index(*_my_pos())
    x2, tgt2 = x[0], loss_target[0]

    win_sh = w_in[0].astype(BF16)
    rows_sh = jnp.concatenate([w_proj_ssd[0], w_proj_att[0], w_out[0]], axis=0).astype(BF16)
    small_sh = _pack_small_shard(conv_w[0], meta_tokens, cws)
    win_all, rows_all, small_all = _all_gather([win_sh, rows_sh, small_sh], "gather_weights")
    w_full = jnp.transpose(win_all, (1, 0, 2)).reshape(d, N_DEV * nc8)
    cuts = [0, ds, ds + cd, ds + cd + hs, ds + cd + hs + da, ds + cd + hs + 2 * da, ds + cd + hs + 3 * da,
            ds + cd + hs + 4 * da, ds + cd + hs + 4 * da + ha, ds + cd + hs + 4 * da + ha + 2 * d]
    w_z, w_xbc, w_dt, w_zatt, w_q, w_k, w_v, w_f, w_g = [w_full[:, cuts[i]:cuts[i + 1]] for i in range(9)]
    w_dtf = jnp.concatenate([w_dt, w_f, jnp.zeros((d, LANES - hs - ha), BF16)], axis=1)
    wps = rows_all[:, :r1].reshape(ds, d)
    wpa = rows_all[:, r1:r1 + r2].reshape(da, d)
    wout = rows_all[:, r1 + r2:].reshape(d, d)
    conv_w_full = jnp.transpose(small_all[:, 0:CONV_K, :], (1, 0, 2)).reshape(CONV_K, cd)
    meta_full = jnp.transpose(small_all[:, 8:8 + N_META, :msh], (1, 0, 2)).reshape(N_META, d)
    head = jnp.concatenate([jnp.zeros((PADN, d), F32), meta_full], axis=0)

    u = _prenorm_fwd(head, x2, norm_pre)
    tm = _att_block(p)
    seg_w = [w_z, w_xbc, w_zatt, w_q, w_k, w_v, w_g]
    zs, xbc, zatt, q, k, v, graw = [
        _mm(u, w, "nn", BF16, tm, _tile(w.shape[1], (1024, 512, 256, 128)), "inproj_%d" % i) for i, w in enumerate(seg_w)]
    dtf = _mm(u, w_dtf, "nn", F32, tm, LANES, "inproj_dtf")

    brow = jnp.concatenate([dt_bias, fgate_bias, jnp.zeros((1, LANES - hs - ha), F32)], axis=1)
    alog_row = _pad_cols(a_log, LANES)
    dskip_l = jnp.repeat(d_skip, HEAD_DIM, axis=1)
    sel_t = (lax.broadcasted_iota(jnp.int32, (LANES, ds), 1) // HEAD_DIM
             == lax.broadcasted_iota(jnp.int32, (LANES, ds), 0)).astype(BF16)
    sel = sel_t.T
    y, yssd, hin, cf = _ssd_fwd(xbc, zs, dtf, conv_w_full, conv_b, brow, alog_row, dskip_l, ssd_norm, sel_t, hs, ha)

    blk = _att_block(p)
    nkb, npair = p // blk, ha // 2
    cum = jnp.where(lax.broadcasted_iota(jnp.int32, (p, 1), 0) < PADN, -NEG, cf[:, hs:hs + ha])
    ck = jnp.transpose(cum.T.reshape(npair, 2, nkb, blk), (0, 2, 1, 3))
    ck = jnp.pad(ck, ((0, 0), (0, 0), (0, 6), (0, 0)))
    o, lse_rep = _attn_fwd(q, k, v, ck, blk)

    (dyssd, d_o, dzatt, dgraw, dzo, mrg, da_, db_, yatt, dout, red_tail) = _tail(
        yssd, o, zatt, graw, head, x2, tgt2, wps, wpa, wout, gate_bias, norm_post)

    tw = _tile(d, (512, 256, 128))
    g_wout = _mm(mrg, dzo, "tn", BF16, tw, tw, "wgrad_out")
    g_wps = _mm(yssd, da_, "tn", BF16, _tile(ds, (512, 256, 128)), tw, "wgrad_ps")
    g_wpa = _mm(yatt, db_, "tn", BF16, _tile(da, (512, 256, 128)), tw, "wgrad_pa")

    dk, dv, dq, dcs, rsum = _attn_bwd(q, k, v, o, d_o, lse_rep, ck, blk)
    dcum = (rsum - dcs)[:, ::HEAD_DIM]
    dcf = jnp.pad(dcum, ((0, 0), (hs, LANES - hs - ha)))
    dxbc, dzs, ddtf, gcw, gcb, gnrm, gsm = _ssd_bwd(
        dyssd, y, zs, xbc, dtf, hin, dcf, conv_w_full, conv_b, brow, alog_row, dskip_l, ssd_norm, sel_t, sel, hs, ha)
    ddtf_b = ddtf.astype(BF16)

    dsegs = [dzs, dxbc, dzatt, dq, dk, dv, dgraw, ddtf_b]
    dproj = jnp.concatenate(dsegs, axis=1)
    w_re = jnp.concatenate(seg_w + [w_dtf], axis=1)
    du = _mm(dproj, w_re, "nt", F32, tm, _tile(d, (256, 128)), "dgrad_in")
    gx, ghead, gnp = _prenorm_bwd(head, x2, norm_pre, du, dout)

    gsegs = [_mm(u, dsg, "tn", BF16, tw, _tile(dsg.shape[1], (512, 256, 128)), "wgrad_in_%d" % i)
             for i, dsg in enumerate(dsegs)]
    g_z, g_xbc, g_zatt, g_q, g_k, g_v, g_g, g_dtf = gsegs
    gw_full = jnp.concatenate([g_z, g_xbc, g_dtf[:, :hs], g_zatt, g_q, g_k, g_v, g_dtf[:, hs:hs + ha], g_g], axis=1)
    gwin_parts = jnp.transpose(gw_full.reshape(d, N_DEV, nc8), (1, 0, 2))
    grows_parts = jnp.concatenate([g_wps.reshape(N_DEV, r1, d), g_wpa.reshape(N_DEV, r2, d),
                                   g_wout.reshape(N_DEV, r3, d)], axis=1)

    recv_win, recv_rows = _all_to_all([gwin_parts, grows_parts], "scatter_grads")
    gmisc = jnp.concatenate([gsm[0:1], gsm[1:2], gsm[2:3], _pad_cols(red_tail[2:3, 0:1], LANES)], axis=1)
    small_g = jnp.concatenate([
        _pack_small_rep(gnp[0:1], red_tail[0:1, :d], red_tail[1:2], gnrm[0:1], gcb[0:1], gmisc, cd),
        _pad_cols(gcw[0:CONV_K], cd), jnp.zeros((4, cd), F32), _pad_cols(ghead[PADN:], cd)], axis=0)
    red = _all_reduce_small(small_g, "reduce_small")

    loss = red[5, 3 * LANES]
    g_small_sh = _pack_small_shard(lax.dynamic_slice_in_dim(red[8:8 + CONV_K], me * cws, cws, axis=1),
                                   lax.dynamic_slice_in_dim(red[16:16 + N_META, :d], me * msh, msh, axis=1), cws)

    zero1 = jnp.zeros((1, 1), F32)
    upd_in = _adamw(w_in[0], recv_win, m_w_in[0], v_w_in[0], "adamw_w_in", parts=True)
    cat3 = lambda a, b, c: jnp.concatenate([a[0], b[0], c[0]], axis=0)
    upd_rows = _adamw(cat3(w_proj_ssd, w_proj_att, w_out), recv_rows, cat3(m_w_proj_ssd, m_w_proj_att, m_w_out),
                      cat3(v_w_proj_ssd, v_w_proj_att, v_w_out), "adamw_rows", parts=True)
    rep = lambda a, b, c, e, f, g1, g2, g3, g4: _pack_small_rep(a, b, c, e, f, _misc_row(g1, g2, g3, g4, zero1), cd)
    upd_rep = _adamw(rep(norm_pre, norm_post, gate_bias, ssd_norm, conv_b, dt_bias, fgate_bias, a_log, d_skip),
                     red[0:8],
                     rep(m_norm_pre, m_norm_post, m_gate_bias, m_ssd_norm, m_conv_b, m_dt_bias, m_fgate_bias, m_a_log, m_d_skip),
                     rep(v_norm_pre, v_norm_post, v_gate_bias, v_ssd_norm, v_conv_b, v_dt_bias, v_fgate_bias, v_a_log, v_d_skip),
                     "adamw_rep")
    upd_sh = _adamw(small_sh, g_small_sh, _pack_small_shard(m_conv_w[0], m_meta_tokens, cws),
                    _pack_small_shard(v_conv_w[0], v_meta_tokens, cws), "adamw_small_shard")

    def leaves(i):
        a_in, a_rows, a_rep, a_sh = upd_in[i], upd_rows[i], upd_rep[i], upd_sh[i]
        misc = a_rep[5:6]
        return [a_sh[8:8 + N_META, :msh], a_rep[0:1, :d], a_in[None], a_sh[0:CONV_K][None], a_rep[4:5, :cd],
                misc[:, :hs], misc[:, LANES:LANES + hs], misc[:, 2 * LANES:2 * LANES + hs], a_rep[3:4, :ds],
                misc[:, hs:hs + ha], a_rep[2:3, :2 * d], a_rows[:r1][None], a_rows[r1:r1 + r2][None],
                a_rows[r1 + r2:][None], a_rep[1:2, :d]]

    return tuple([loss, gx[None]] + leaves(0) + leaves(1) + leaves(2) + leaves(3))
```

```python
import functools
import math

import jax
import jax.numpy as jnp
from jax import lax
from jax.experimental import pallas as pl
from jax.experimental.pallas import tpu as pltpu

F32 = jnp.float32
BF16 = jnp.bfloat16

N_DEV = 8
N_META = 16
CHUNK = 128
PADN = CHUNK - N_META
HEAD_DIM = 64
SSD_GROUPS = 4
CONV_K = 4
EPS = 1e-6
NEG = -1e30
LANES = 128
HALO = 16

ADAM_LR = 0.001
ADAM_B1 = 0.9
ADAM_B2 = 0.999
ADAM_EPS = 1e-08
ADAM_WD = 0.01
ADAM_STEP = 10

VMEM_LIMIT = 56 * 1024 * 1024

NN = (((1,), (0,)), ((), ()))
NT = (((1,), (1,)), ((), ()))
TN = (((0,), (0,)), ((), ()))
MESH = pl.DeviceIdType.MESH


def _dot(a, b, dims=NN):
    return lax.dot_general(a, b, dims, preferred_element_type=F32)


def _split2(x):
    hi = x.astype(BF16)
    lo = (x - hi.astype(F32)).astype(BF16)
    return hi, lo


def _dot_sel(x, sel):
    hi, lo = _split2(x)
    return _dot(hi, sel) + _dot(lo, sel)


def _dot_tri(tri, x):
    h1 = x.astype(BF16)
    r1 = x - h1.astype(F32)
    h2 = r1.astype(BF16)
    h3 = (r1 - h2.astype(F32)).astype(BF16)
    return _dot(tri, h1) + _dot(tri, h2) + _dot(tri, h3)


def _sigmoid(x):
    return 1.0 / (1.0 + jnp.exp(-x))


def _softplus(x):
    return jnp.maximum(x, 0.0) + jnp.log(1.0 + jnp.exp(-jnp.abs(x)))


def _cparams(sem=None, vmem=VMEM_LIMIT):
    kw = {"vmem_limit_bytes": vmem}
    if sem is not None:
        kw["dimension_semantics"] = sem
    return pltpu.CompilerParams(**kw)


def _full(shape):
    nd = len(shape)
    return pl.BlockSpec(shape, lambda *_: (0,) * nd)


def _att_block(p):
    return 384 if p % 384 == 0 else CHUNK


def _my_pos():
    return lax.axis_index("x"), lax.axis_index("y"), lax.axis_index("c")


def _dev_index(x, y, c):
    return 4 * x + 2 * y + c


FLIPS = [(fx, fy, fc) for fx in (0, 1) for fy in (0, 1) for fc in (0, 1)][1:]


def _flip(pos, f):
    return tuple((1 - p) if fi else p for p, fi in zip(pos, f))


def _all_gather(bufs, name):
    nb = len(bufs)

    def body(*refs):
        ins, outs = refs[:nb], refs[nb:2 * nb]
        send_sems, recv_sems, local_sems = refs[2 * nb:]
        x, y, c = _my_pos()
        me = _dev_index(x, y, c)
        sibling = (x, y, 1 - c)
        chips = [(1 - x, y), (x, 1 - y), (1 - x, 1 - y)]

        def copy(b, k, block_idx, to, src=None):
            dst = outs[b].at[block_idx]
            return pltpu.make_async_remote_copy(
                src_ref=dst if src is None else src, dst_ref=dst,
                send_sem=send_sems.at[b, k], recv_sem=recv_sems.at[b, k],
                device_id=to, device_id_type=MESH)

        started = []
        for b in range(nb):
            mine = pltpu.make_async_copy(ins[b], outs[b].at[me], local_sems.at[b])
            mine.start()
            started.append(mine)
        first = []
        for b in range(nb):
            first.append(copy(b, 0, me, sibling, src=ins[b]))
            for j, chip in enumerate(chips):
                first.append(copy(b, 1 + j, me, (chip[0], chip[1], c), src=ins[b]))
        for cp in first:
            cp.start()
        passed = []
        for j, chip in enumerate(chips):
            blk = _dev_index(chip[0], chip[1], c)
            for b in range(nb):
                copy(b, 1 + j, blk, (x, y, c)).wait_recv()
                fwd = copy(b, 4 + j, blk, sibling)
                fwd.start()
                passed.append(fwd)
        for b in range(nb):
            copy(b, 0, _dev_index(x, y, 1 - c), (x, y, c)).wait_recv()
        for j, chip in enumerate(chips):
            blk = _dev_index(chip[0], chip[1], 1 - c)
            for b in range(nb):
                copy(b, 4 + j, blk, (x, y, c)).wait_recv()
        for cp in first + passed:
            cp.wait_send()
        for mine in started:
            mine.wait()

    any_spec = pl.BlockSpec(memory_space=pl.ANY)
    return pl.pallas_call(
        body, name=name,
        out_shape=[jax.ShapeDtypeStruct((N_DEV,) + b.shape, b.dtype) for b in bufs],
        in_specs=[any_spec] * nb, out_specs=[any_spec] * nb,
        scratch_shapes=[pltpu.SemaphoreType.DMA((nb, 7)), pltpu.SemaphoreType.DMA((nb, 7)),
                        pltpu.SemaphoreType.DMA((nb,))],
    )(*bufs)


N_CHIP = 4
CHIP_FLIPS = [(1, 0), (0, 1), (1, 1)]


def _exchange_sibling(bufs, name):
    nb = len(bufs)

    def body(*refs):
        ins, outs = refs[:nb], refs[nb:2 * nb]
        send_sems, recv_sems = refs[2 * nb:]
        x, y, c = _my_pos()

        def copy(b, k):
            return pltpu.make_async_remote_copy(
                src_ref=ins[b].at[2 * k + (1 - c)], dst_ref=outs[b].at[k],
                send_sem=send_sems.at[b, k], recv_sem=recv_sems.at[b, k],
                device_id=(x, y, 1 - c), device_id_type=MESH)

        cps = [copy(b, k) for b in range(nb) for k in range(N_CHIP)]
        for cp in cps:
            cp.start()
        for cp in cps:
            cp.wait()

    any_spec = pl.BlockSpec(memory_space=pl.ANY)
    return pl.pallas_call(
        body, name=name,
        out_shape=[jax.ShapeDtypeStruct((N_CHIP,) + b.shape[1:], b.dtype) for b in bufs],
        in_specs=[any_spec] * nb, out_specs=[any_spec] * nb,
        scratch_shapes=[pltpu.SemaphoreType.DMA((nb, N_CHIP)), pltpu.SemaphoreType.DMA((nb, N_CHIP))],
    )(*bufs)


def _pair_add(mine, recv, core, name):
    _, r, cdim = mine.shape
    tr = _tile(r, (256, 128))

    def body(core_ref, a_ref, b_ref, o_ref):
        o_ref[0] = (a_ref[0, 0].astype(F32) + b_ref[0].astype(F32)).astype(o_ref.dtype)

    return pl.pallas_call(
        body, name=name,
        grid_spec=pltpu.PrefetchScalarGridSpec(
            num_scalar_prefetch=1, grid=(N_CHIP, r // tr),
            in_specs=[pl.BlockSpec((1, 1, tr, cdim), lambda k, i, core_ref: (k, core_ref[0], i, 0)),
                      pl.BlockSpec((1, tr, cdim), lambda k, i, core_ref: (k, i, 0))],
            out_specs=pl.BlockSpec((1, tr, cdim), lambda k, i, core_ref: (k, i, 0))),
        out_shape=jax.ShapeDtypeStruct((N_CHIP, r, cdim), mine.dtype),
        compiler_params=_cparams(("parallel", "parallel")),
    )(core, mine.reshape(N_CHIP, 2, r, cdim), recv)


def _exchange_chips(bufs, name):
    nb = len(bufs)

    def body(*refs):
        ins, outs = refs[:nb], refs[nb:2 * nb]
        send_sems, recv_sems, local_sems = refs[2 * nb:]
        x, y, c = _my_pos()
        mine = 2 * x + y

        def copy(b, j, f):
            px, py = (1 - x) if f[0] else x, (1 - y) if f[1] else y
            return pltpu.make_async_remote_copy(
                src_ref=ins[b].at[2 * px + py], dst_ref=outs[b].at[mine],
                send_sem=send_sems.at[b, j], recv_sem=recv_sems.at[b, j],
                device_id=(px, py, c), device_id_type=MESH)

        local = [pltpu.make_async_copy(ins[b].at[mine], outs[b].at[mine], local_sems.at[b]) for b in range(nb)]
        for cp in local:
            cp.start()
        sends = [copy(b, j, f) for b in range(nb) for j, f in enumerate(CHIP_FLIPS)]
        for cp in sends:
            cp.start()
        for b in range(nb):
            for j, f in enumerate(CHIP_FLIPS):
                px, py = (1 - x) if f[0] else x, (1 - y) if f[1] else y
                pltpu.make_async_remote_copy(
                    src_ref=ins[b].at[mine], dst_ref=outs[b].at[2 * px + py],
                    send_sem=send_sems.at[b, j], recv_sem=recv_sems.at[b, j],
                    device_id=(px, py, c), device_id_type=MESH).wait_recv()
        for cp in sends:
            cp.wait_send()
        for cp in local:
            cp.wait()

    any_spec = pl.BlockSpec(memory_space=pl.ANY)
    return pl.pallas_call(
        body, name=name,
        out_shape=[jax.ShapeDtypeStruct(b.shape, b.dtype) for b in bufs],
        in_specs=[any_spec] * nb, out_specs=[any_spec] * nb,
        scratch_shapes=[pltpu.SemaphoreType.DMA((nb, 3)), pltpu.SemaphoreType.DMA((nb, 3)),
                        pltpu.SemaphoreType.DMA((nb,))],
    )(*bufs)


def _all_reduce_small(v, name):
    r, cdim = v.shape

    def body(x_ref, out_ref, slots, send_sems, recv_sems):
        pos = _my_pos()
        me = _dev_index(*pos)
        slots[me] = x_ref[...]
        sends = []
        for k, f in enumerate(FLIPS):
            cp = pltpu.make_async_remote_copy(
                src_ref=x_ref, dst_ref=slots.at[me], send_sem=send_sems.at[k], recv_sem=recv_sems.at[k],
                device_id=_flip(pos, f), device_id_type=MESH)
            cp.start()
            sends.append(cp)
        for k, f in enumerate(FLIPS):
            peer = _flip(pos, f)
            pltpu.make_async_remote_copy(
                src_ref=x_ref, dst_ref=slots.at[_dev_index(*peer)], send_sem=send_sems.at[k],
                recv_sem=recv_sems.at[k], device_id=peer, device_id_type=MESH).wait_recv()
        for cp in sends:
            cp.wait_send()
        acc = slots[0]
        for s in range(1, N_DEV):
            acc = acc + slots[s]
        out_ref[...] = acc

    vm = pl.BlockSpec(memory_space=pltpu.VMEM)
    return pl.pallas_call(
        body, name=name, out_shape=jax.ShapeDtypeStruct(v.shape, F32),
        in_specs=[vm], out_specs=vm,
        scratch_shapes=[pltpu.VMEM((N_DEV, r, cdim), F32), pltpu.SemaphoreType.DMA((7,)),
                        pltpu.SemaphoreType.DMA((7,))],
    )(v)


def _mm(a, b, dims, out_dtype, tm, tn, name):
    if dims == "nn":
        (m, k), (_, n) = a.shape, b.shape
        a_spec = pl.BlockSpec((tm, k), lambda j, i: (i, 0))
        b_spec = pl.BlockSpec((k, tn), lambda j, i: (0, j))
        dn = NN
    elif dims == "nt":
        (m, k), (n, _) = a.shape, b.shape
        a_spec = pl.BlockSpec((tm, k), lambda j, i: (i, 0))
        b_spec = pl.BlockSpec((tn, k), lambda j, i: (j, 0))
        dn = NT
    else:
        (k, m), (_, n) = a.shape, b.shape
        a_spec = pl.BlockSpec((k, tm), lambda j, i: (0, i))
        b_spec = pl.BlockSpec((k, tn), lambda j, i: (0, j))
        dn = TN
    assert m % tm == 0 and n % tn == 0, (m, tm, n, tn)

    def body(a_ref, b_ref, o_ref):
        o_ref[...] = _dot(a_ref[...], b_ref[...], dn).astype(o_ref.dtype)

    return pl.pallas_call(
        body, name=name, grid=(n // tn, m // tm),
        in_specs=[a_spec, b_spec], out_specs=pl.BlockSpec((tm, tn), lambda j, i: (i, j)),
        out_shape=jax.ShapeDtypeStruct((m, n), out_dtype),
        compiler_params=_cparams(("parallel", "parallel")),
    )(a, b)


def _mm_sum_nt(a_list, b_list, tm, tn, name):
    n_op = len(a_list)
    m, n = a_list[0].shape[0], b_list[0].shape[0]

    def body(*refs):
        acc = _dot(refs[0][...], refs[n_op][...], NT)
        for i in range(1, n_op):
            acc = acc + _dot(refs[i][...], refs[n_op + i][...], NT)
        refs[2 * n_op][...] = acc

    return pl.pallas_call(
        body, name=name, grid=(n // tn, m // tm),
        in_specs=([pl.BlockSpec((tm, a.shape[1]), lambda j, i: (i, 0)) for a in a_list]
                  + [pl.BlockSpec((tn, b.shape[1]), lambda j, i: (j, 0)) for b in b_list]),
        out_specs=pl.BlockSpec((tm, tn), lambda j, i: (i, j)),
        out_shape=jax.ShapeDtypeStruct((m, n), F32),
        compiler_params=_cparams(("parallel", "parallel")),
    )(*a_list, *b_list)


def _tile(n, prefs):
    for t in prefs:
        if n % t == 0:
            return t
    return n


def _prenorm_fwd(head, x2, w):
    p, d = x2.shape[0] + CHUNK, x2.shape[1]

    def body(head_ref, x_ref, w_ref, u_ref):
        i = pl.program_id(0)
        h = jnp.where(i == 0, head_ref[...], x_ref[...])
        ms = jnp.mean(h * h, axis=-1, keepdims=True)
        u_ref[...] = (h * lax.rsqrt(ms + EPS) * w_ref[...]).astype(BF16)

    return pl.pallas_call(
        body, name="prenorm_fwd", grid=(p // CHUNK,),
        in_specs=[_full((CHUNK, d)), pl.BlockSpec((CHUNK, d), lambda i: (jnp.maximum(i - 1, 0), 0)), _full((1, d))],
        out_specs=pl.BlockSpec((CHUNK, d), lambda i: (i, 0)),
        out_shape=jax.ShapeDtypeStruct((p, d), BF16),
        compiler_params=_cparams(("arbitrary",)),
    )(head, x2, w)


def _prenorm_bwd(head, x2, w, du, dout):
    p, d = x2.shape[0] + CHUNK, x2.shape[1]

    def body(head_ref, x_ref, w_ref, du_ref, dout_ref, gx_ref, ghead_ref, gw_ref):
        i = pl.program_id(0)
        h = jnp.where(i == 0, head_ref[...], x_ref[...])
        rstd = lax.rsqrt(jnp.mean(h * h, axis=-1, keepdims=True) + EPS)
        xhat = h * rstd
        dub = du_ref[...]
        dxh = dub * w_ref[...]
        dh = rstd * (dxh - xhat * jnp.mean(dxh * xhat, axis=-1, keepdims=True)) + dout_ref[...]

        @pl.when(i == 0)
        def _():
            ghead_ref[...] = dh
            gw_ref[...] = jnp.zeros_like(gw_ref)

        gx_ref[...] = dh
        gw_ref[0:1, :] += jnp.sum(dub * xhat, axis=0, keepdims=True)

    return pl.pallas_call(
        body, name="prenorm_bwd", grid=(p // CHUNK,),
        in_specs=[_full((CHUNK, d)), pl.BlockSpec((CHUNK, d), lambda i: (jnp.maximum(i - 1, 0), 0)), _full((1, d)),
                  pl.BlockSpec((CHUNK, d), lambda i: (i, 0)), pl.BlockSpec((CHUNK, d), lambda i: (i, 0))],
        out_specs=[pl.BlockSpec((CHUNK, d), lambda i: (jnp.maximum(i - 1, 0), 0)), _full((CHUNK, d)), _full((8, d))],
        out_shape=[jax.ShapeDtypeStruct(x2.shape, F32), jax.ShapeDtypeStruct((CHUNK, d), F32),
                   jax.ShapeDtypeStruct((8, d), F32)],
        compiler_params=_cparams(("arbitrary",)),
    )(head, x2, w, du, dout)


def _conv_pre(xr, halo128, cw_ref, cb_ref, rows):
    pre = cb_ref[...] + cw_ref[CONV_K - 1:CONV_K, :] * xr
    shifted = []
    for j in range(1, CONV_K):
        sh = jnp.where(rows >= j, pltpu.roll(xr, j, 0), pltpu.roll(halo128, j, 0))
        shifted.append(sh)
        pre = pre + cw_ref[CONV_K - 1 - j:CONV_K - j, :] * sh
    return pre, shifted


def _ssd_scalars(dtf_ref, brow_ref, alog_ref, rowmask, hs, ha, tri):
    lane = lax.broadcasted_iota(jnp.int32, (1, LANES), 1)
    is_dt = lane < hs
    is_f = (lane >= hs) & (lane < hs + ha)
    dtr = dtf_ref[...] + brow_ref[...]
    sp = _softplus(dtr)
    dt = jnp.where(is_dt, sp, 0.0) * rowmask
    logf = jnp.where(is_f, jnp.minimum(dtr, 0.0) - jnp.log(1.0 + jnp.exp(-jnp.abs(dtr))), 0.0) * rowmask
    a_row = jnp.where(is_dt, -jnp.exp(alog_ref[...]), 0.0)
    run = _dot_tri(tri, dt * a_row + logf)
    return dtr, dt, a_row, run, is_dt, is_f


def _tri_mats():
    r = lax.broadcasted_iota(jnp.int32, (CHUNK, CHUNK), 0)
    c = lax.broadcasted_iota(jnp.int32, (CHUNK, CHUNK), 1)
    return r, c


def _ssd_fwd(xbc, z, dtf, conv_w, conv_b, brow, alog, dskip_l, ssd_norm, sel_t, hs, ha):
    p, cd = xbc.shape
    ds = z.shape[1]
    ns = (cd - ds) // (2 * SSD_GROUPS)
    gw = ds // SSD_GROUPS
    nch = p // CHUNK
    hpg = hs // SSD_GROUPS

    def body(xbc_ref, halo_ref, z_ref, dtf_ref, cw_ref, cb_ref, brow_ref, alog_ref, dsk_ref, nrm_ref, selt_ref,
             y_ref, yssd_ref, hin_ref, cf_ref, st_ref, carry_ref, yacc_ref):
        c = pl.program_id(0)

        @pl.when(c == 0)
        def _():
            st_ref[...] = jnp.zeros_like(st_ref)
            carry_ref[...] = jnp.zeros_like(carry_ref)

        rows = lax.broadcasted_iota(jnp.int32, (CHUNK, 1), 0)
        rowmask = jnp.where((rows >= PADN) | (c > 0), 1.0, 0.0)
        ri, ci = _tri_mats()
        causal = ri >= ci
        tri = jnp.where(causal, 1.0, 0.0).astype(BF16)

        xr = xbc_ref[...].astype(F32)
        halo = halo_ref[...].astype(F32) * jnp.where(c > 0, 1.0, 0.0)
        halo128 = jnp.concatenate([jnp.zeros((CHUNK - HALO, cd), F32), halo], axis=0)
        pre, _ = _conv_pre(xr, halo128, cw_ref, cb_ref, rows)
        xc = pre * _sigmoid(pre) * rowmask

        dtr, dt, a_row, run, is_dt, is_f = _ssd_scalars(dtf_ref, brow_ref, alog_ref, rowmask, hs, ha, tri)
        cf = run + carry_ref[...]
        cf_ref[...] = cf
        carry_ref[...] = jnp.where(is_f, cf[CHUNK - 1:CHUNK, :], 0.0)
        cs = jnp.where(is_dt, run, 0.0)
        cl = cs[CHUNK - 1:CHUNK, :]
        selt = selt_ref[...]
        dt_x = _dot_sel(dt, selt)
        e_x = _dot_sel(jnp.exp(cs), selt)
        w_x = _dot_sel(jnp.exp(cl - cs), selt)
        cdec_x = _dot_sel(jnp.broadcast_to(jnp.exp(cl), (8, LANES)), selt)[0:1, :]
        cs_t = cs.T

        xs = xc[:, :ds]
        xdt = xs * dt_x
        xdt_b = xdt.astype(BF16)
        xw_b = (xdt * w_x).astype(BF16)
        lane = lax.broadcasted_iota(jnp.int32, (1, LANES), 1)
        half0 = lane < HEAD_DIM
        for g in range(SSD_GROUPS):
            bg = xc[:, ds + g * ns: ds + (g + 1) * ns].astype(BF16)
            cg = xc[:, ds + SSD_GROUPS * ns + g * ns: ds + SSD_GROUPS * ns + (g + 1) * ns].astype(BF16)
            gm = _dot(cg, bg, NT)
            gs = slice(g * gw, (g + 1) * gw)
            stg = st_ref[:, gs]
            stg_b = stg.astype(BF16)
            hin_ref[0, :, gs] = stg_b
            yoff = _dot(cg, stg_b) * e_x[:, gs]
            for pr in range(gw // LANES):
                sl = slice(g * gw + pr * LANES, g * gw + (pr + 1) * LANES)
                xp = xdt_b[:, sl]
                yd = jnp.zeros((CHUNK, LANES), F32)
                for j in range(2):
                    h = g * hpg + 2 * pr + j
                    seg = cs[:, h:h + 1] - cs_t[h:h + 1, :]
                    m = jnp.where(causal, gm * jnp.exp(jnp.minimum(seg, 0.0)), 0.0).astype(BF16)
                    sel = half0 if j == 0 else jnp.logical_not(half0)
                    yd = yd + _dot(m, jnp.where(sel, xp, jnp.zeros_like(xp)))
                yacc_ref[:, sl] = yd + yoff[:, pr * LANES:(pr + 1) * LANES] + dsk_ref[:, sl] * xs[:, sl]
            st_ref[:, gs] = stg * cdec_x[:, gs] + _dot(bg, xw_b[:, gs], TN)

        y = yacc_ref[...]
        y_ref[...] = y.astype(BF16)
        zf = z_ref[...].astype(F32)
        u = y * zf * _sigmoid(zf)
        for g in range(SSD_GROUPS):
            gs = slice(g * gw, (g + 1) * gw)
            ug = u[:, gs]
            ms = jnp.mean(ug * ug, axis=-1, keepdims=True)
            yssd_ref[:, gs] = (ug * lax.rsqrt(ms + EPS) * nrm_ref[:, gs]).astype(BF16)

    rb = CHUNK // HALO
    return pl.pallas_call(
        body, name="ssd_fwd", grid=(nch,),
        in_specs=[pl.BlockSpec((CHUNK, cd), lambda c: (c, 0)),
                  pl.BlockSpec((HALO, cd), lambda c: (jnp.maximum(c * rb - 1, 0), 0)),
                  pl.BlockSpec((CHUNK, ds), lambda c: (c, 0)),
                  pl.BlockSpec((CHUNK, LANES), lambda c: (c, 0)),
                  _full((CONV_K, cd)), _full((1, cd)), _full((1, LANES)), _full((1, LANES)),
                  _full((1, ds)), _full((1, ds)), _full((LANES, ds))],
        out_specs=[pl.BlockSpec((CHUNK, ds), lambda c: (c, 0)), pl.BlockSpec((CHUNK, ds), lambda c: (c, 0)),
                   pl.BlockSpec((1, ns, ds), lambda c: (c, 0, 0)), pl.BlockSpec((CHUNK, LANES), lambda c: (c, 0))],
        out_shape=[jax.ShapeDtypeStruct((p, ds), BF16), jax.ShapeDtypeStruct((p, ds), BF16),
                   jax.ShapeDtypeStruct((nch, ns, ds), BF16), jax.ShapeDtypeStruct((p, LANES), F32)],
        scratch_shapes=[pltpu.VMEM((ns, ds), F32), pltpu.VMEM((1, LANES), F32), pltpu.VMEM((CHUNK, ds), F32)],
        compiler_params=_cparams(("arbitrary",)),
    )(xbc, xbc, z, dtf, conv_w, conv_b, brow, alog, dskip_l, ssd_norm, sel_t)


def _ssd_bwd(dyssd, y, z, xbc, dtf, hin, dcf, conv_w, conv_b, brow, alog, dskip_l, ssd_norm, sel_t, sel, hs, ha):
    p, cd = xbc.shape
    ds = z.shape[1]
    ns = (cd - ds) // (2 * SSD_GROUPS)
    gw = ds // SSD_GROUPS
    nch = p // CHUNK
    hpg = hs // SSD_GROUPS
    rb = CHUNK // HALO

    def body(dyssd_ref, y_ref, z_ref, xbc_ref, halo_ref, dtf_ref, hin_ref, dcf_ref, cw_ref, cb_ref, brow_ref,
             alog_ref, dsk_ref, nrm_ref, selt_ref, sel_ref,
             dxbc_ref, dz_ref, ddtf_ref, gcw_ref, gcb_ref, gnrm_ref, gsm_ref,
             dst_ref, nxt_ref, fcar_ref, gdsk_ref, dxc_ref):
        step = pl.program_id(0)
        c = nch - 1 - step

        @pl.when(step == 0)
        def _():
            dst_ref[...] = jnp.zeros_like(dst_ref)
            nxt_ref[...] = jnp.zeros_like(nxt_ref)
            fcar_ref[...] = jnp.zeros_like(fcar_ref)
            gdsk_ref[...] = jnp.zeros_like(gdsk_ref)
            gcw_ref[...] = jnp.zeros_like(gcw_ref)
            gcb_ref[...] = jnp.zeros_like(gcb_ref)
            gnrm_ref[...] = jnp.zeros_like(gnrm_ref)
            gsm_ref[...] = jnp.zeros_like(gsm_ref)

        rows = lax.broadcasted_iota(jnp.int32, (CHUNK, 1), 0)
        rowmask = jnp.where((rows >= PADN) | (c > 0), 1.0, 0.0)
        ri, ci = _tri_mats()
        causal = ri >= ci
        anti = ci >= ri
        tri = jnp.where(causal, 1.0, 0.0).astype(BF16)
        rtri = jnp.where(anti, 1.0, 0.0).astype(BF16)

        xr = xbc_ref[...].astype(F32)
        halo = halo_ref[...].astype(F32) * jnp.where(c > 0, 1.0, 0.0)
        halo128 = jnp.concatenate([jnp.zeros((CHUNK - HALO, cd), F32), halo], axis=0)
        pre, shifted = _conv_pre(xr, halo128, cw_ref, cb_ref, rows)
        sg = _sigmoid(pre)
        xc = pre * sg * rowmask
        dsilu = sg * (1.0 + pre * (1.0 - sg)) * rowmask

        dtr, dt, a_row, run, is_dt, is_f = _ssd_scalars(dtf_ref, brow_ref, alog_ref, rowmask, hs, ha, tri)
        cs = jnp.where(is_dt, run, 0.0)
        cl = cs[CHUNK - 1:CHUNK, :]
        selt = selt_ref[...]
        selm = sel_ref[...]
        dt_x = _dot_sel(dt, selt)
        e_x = _dot_sel(jnp.exp(cs), selt)
        w_x = _dot_sel(jnp.exp(cl - cs), selt)
        cdec = jnp.exp(cl)
        cdec_x = _dot_sel(jnp.broadcast_to(cdec, (8, LANES)), selt)[0:1, :]
        cs_t = cs.T
        xs = xc[:, :ds]
        xdt = xs * dt_x
        xdt_b = xdt.astype(BF16)
        xw_b = (xdt * w_x).astype(BF16)

        yv = y_ref[...].astype(F32)
        zf = z_ref[...].astype(F32)
        sz = _sigmoid(zf)
        u = yv * zf * sz
        dyo = dyssd_ref[...].astype(F32)
        du_parts = []
        for g in range(SSD_GROUPS):
            gs = slice(g * gw, (g + 1) * gw)
            ug = u[:, gs]
            rstd = lax.rsqrt(jnp.mean(ug * ug, axis=-1, keepdims=True) + EPS)
            yhat = ug * rstd
            dyg = dyo[:, gs]
            gnrm_ref[0:1, gs] += jnp.sum(dyg * yhat, axis=0, keepdims=True)
            dyh = dyg * nrm_ref[:, gs]
            du_parts.append(rstd * (dyh - yhat * jnp.mean(dyh * yhat, axis=-1, keepdims=True)))
        du = jnp.concatenate(du_parts, axis=1)
        dy = du * zf * sz
        dz_ref[...] = (du * yv * sz * (1.0 + zf * (1.0 - sz))).astype(BF16)

        dsk = dsk_ref[...]
        gdsk_ref[...] += jnp.sum(dy * xs, axis=0, keepdims=True)
        dy_b = dy.astype(BF16)
        dye_b = (dy * e_x).astype(BF16)
        lane = lax.broadcasted_iota(jnp.int32, (1, LANES), 1)
        half0 = lane < HEAD_DIM
        x_parts, yo_parts, t4_parts = [], [], []
        dcs = jnp.zeros((CHUNK, LANES), F32)
        for g in range(SSD_GROUPS):
            gs = slice(g * gw, (g + 1) * gw)
            bsl = slice(ds + g * ns, ds + (g + 1) * ns)
            csl = slice(ds + SSD_GROUPS * ns + g * ns, ds + SSD_GROUPS * ns + (g + 1) * ns)
            bg = xc[:, bsl].astype(BF16)
            cg = xc[:, csl].astype(BF16)
            gm = _dot(cg, bg, NT)
            gm_t = _dot(bg, cg, NT)
            stg_b = hin_ref[0, :, gs]
            dstg = dst_ref[:, gs]
            dstg_b = dstg.astype(BF16)
            t4_parts.append(jnp.sum(dstg * stg_b.astype(F32), axis=0, keepdims=True))
            zst = _dot(bg, dstg_b) * w_x[:, gs]
            x_parts.append(xdt[:, gs] * zst)
            yo_parts.append(dy[:, gs] * (_dot(cg, stg_b) * e_x[:, gs]))
            dgsum = jnp.zeros((CHUNK, CHUNK), F32)
            dgtsum = jnp.zeros((CHUNK, CHUNK), F32)
            for pr in range(gw // LANES):
                sl = slice(g * gw + pr * LANES, g * gw + (pr + 1) * LANES)
                xp = xdt_b[:, sl]
                dyp = dy_b[:, sl]
                dxd = zst[:, pr * LANES:(pr + 1) * LANES]
                for j in range(2):
                    h = g * hpg + 2 * pr + j
                    sel_l = half0 if j == 0 else jnp.logical_not(half0)
                    seg = cs[:, h:h + 1] - cs_t[h:h + 1, :]
                    lm = jnp.where(causal, jnp.exp(jnp.minimum(seg, 0.0)), 0.0)
                    lmt = jnp.where(anti, jnp.exp(jnp.minimum(-seg, 0.0)), 0.0)
                    dyp_m = jnp.where(sel_l, dyp, jnp.zeros_like(dyp))
                    xp_m = jnp.where(sel_l, xp, jnp.zeros_like(xp))
                    dxd = dxd + _dot((gm_t * lmt).astype(BF16), dyp_m)
                    dg = _dot(dyp_m, xp, NT) * lm
                    dgt = _dot(xp_m, dyp, NT) * lmt
                    dgsum = dgsum + dg
                    dgtsum = dgtsum + dgt
                    qrow = (jnp.sum(dg * gm, axis=1, keepdims=True) - jnp.sum(dgt * gm_t, axis=1, keepdims=True))
                    dcs = dcs + jnp.where(lane == h, qrow, 0.0)
                dxc_ref[:, sl] = dxd
            dxc_ref[:, csl] = _dot(dgsum.astype(BF16), bg) + _dot(dye_b[:, gs], stg_b, NT)
            dxc_ref[:, bsl] = _dot(dgtsum.astype(BF16), cg) + _dot(xw_b[:, gs], dstg_b, NT)
            dst_ref[:, gs] = dstg * cdec_x[:, gs] + _dot(cg, dye_b[:, gs], TN)

        dxdt = dxc_ref[:, :ds]
        xst = _dot_sel(jnp.concatenate(x_parts, axis=1), selm)
        yo = _dot_sel(jnp.concatenate(yo_parts, axis=1), selm)
        t4 = _dot_sel(jnp.concatenate([jnp.concatenate(t4_parts, axis=1), jnp.zeros((7, ds), F32)], axis=0), selm)
        dcl = jnp.sum(xst, axis=0, keepdims=True) + cdec * t4[0:1, :]
        dcs = dcs + yo - xst + jnp.where(rows == CHUNK - 1, dcl, 0.0)
        da_ = _dot_tri(rtri, dcs)
        ddt = _dot_sel(dxdt * xs, selm) + da_ * a_row
        dcf_blk = dcf_ref[...]
        dlogf = _dot_tri(rtri, dcf_blk) + fcar_ref[...]
        fcar_ref[...] += jnp.sum(dcf_blk, axis=0, keepdims=True)
        sgd = _sigmoid(dtr)
        ddtf = (jnp.where(is_dt, ddt * sgd, 0.0) + jnp.where(is_f, dlogf * (1.0 - sgd), 0.0)) * rowmask
        ddtf_ref[...] = ddtf
        gsm_ref[0:1, :] += jnp.sum(ddtf, axis=0, keepdims=True)
        gsm_ref[1:2, :] += jnp.sum(da_ * dt, axis=0, keepdims=True) * a_row

        dxc_ref[:, :ds] = dxdt * dt_x + dsk * dy
        dpre = dxc_ref[...] * dsilu
        gcb_ref[0:1, :] += jnp.sum(dpre, axis=0, keepdims=True)
        gcw_ref[CONV_K - 1:CONV_K, :] += jnp.sum(dpre * xr, axis=0, keepdims=True)
        nxt128 = jnp.concatenate([nxt_ref[...], jnp.zeros((CHUNK - 8, cd), F32)], axis=0)
        dxr = cw_ref[CONV_K - 1:CONV_K, :] * dpre
        for j in range(1, CONV_K):
            gcw_ref[CONV_K - 1 - j:CONV_K - j, :] += jnp.sum(dpre * shifted[j - 1], axis=0, keepdims=True)
            up = jnp.where(rows < CHUNK - j, pltpu.roll(dpre, CHUNK - j, 0), pltpu.roll(nxt128, CHUNK - j, 0))
            dxr = dxr + cw_ref[CONV_K - 1 - j:CONV_K - j, :] * up
        nxt_ref[...] = dpre[0:8, :]
        dxbc_ref[...] = dxr.astype(BF16)

        @pl.when(step == nch - 1)
        def _():
            gsm_ref[2:3, :] = _dot_sel(jnp.broadcast_to(gdsk_ref[...], (8, ds)), selm)[0:1, :]

    rev = lambda s: nch - 1 - s
    blk = lambda w: pl.BlockSpec((CHUNK, w), lambda s: (rev(s), 0))
    return pl.pallas_call(
        body, name="ssd_bwd", grid=(nch,),
        in_specs=[blk(ds), blk(ds), blk(ds), blk(cd),
                  pl.BlockSpec((HALO, cd), lambda s: (jnp.maximum(rev(s) * rb - 1, 0), 0)),
                  blk(LANES), pl.BlockSpec((1, ns, ds), lambda s: (rev(s), 0, 0)), blk(LANES),
                  _full((CONV_K, cd)), _full((1, cd)), _full((1, LANES)), _full((1, LANES)),
                  _full((1, ds)), _full((1, ds)), _full((LANES, ds)), _full((ds, LANES))],
        out_specs=[blk(cd), blk(ds), blk(LANES), _full((8, cd)), _full((8, cd)), _full((8, ds)), _full((8, LANES))],
        out_shape=[jax.ShapeDtypeStruct((p, cd), BF16), jax.ShapeDtypeStruct((p, ds), BF16),
                   jax.ShapeDtypeStruct((p, LANES), F32), jax.ShapeDtypeStruct((8, cd), F32),
                   jax.ShapeDtypeStruct((8, cd), F32), jax.ShapeDtypeStruct((8, ds), F32),
                   jax.ShapeDtypeStruct((8, LANES), F32)],
        scratch_shapes=[pltpu.VMEM((ns, ds), F32), pltpu.VMEM((8, cd), F32), pltpu.VMEM((1, LANES), F32),
                        pltpu.VMEM((1, ds), F32), pltpu.VMEM((CHUNK, cd), F32)],
        compiler_params=_cparams(("arbitrary",)),
    )(dyssd, y, z, xbc, xbc, dtf, hin, dcf, conv_w, conv_b, brow, alog, dskip_l, ssd_norm, sel_t, sel)


def _attn_fwd(q, k, v, ck, blk):
    p, da = q.shape
    npair, nkb = ck.shape[0], ck.shape[1]
    scale = 1.0 / math.sqrt(HEAD_DIM)

    def body(q_ref, k_ref, v_ref, ck_ref, o_ref, lse_ref):
        i = pl.program_id(1)
        lane = lax.broadcasted_iota(jnp.int32, (1, LANES), 1)
        sels = [lane < HEAD_DIM, lane >= HEAD_DIM]
        ones = [jnp.where(lane == HEAD_DIM, 1.0, 0.0).astype(BF16), jnp.where(lane == 0, 1.0, 0.0).astype(BF16)]
        qb = q_ref[...] * scale
        qms = [jnp.where(sel, qb, jnp.zeros_like(qb)) for sel in sels]
        cmask = (lax.broadcasted_iota(jnp.int32, (blk, blk), 1) <= lax.broadcasted_iota(jnp.int32, (blk, blk), 0))

        def step(kb, carry, masked):
            r0 = pl.multiple_of(kb * blk, blk)
            ks = k_ref[pl.ds(r0, blk), :]
            vs = v_ref[pl.ds(r0, blk), :]
            out = []
            for j in range(2):
                m, acc = carry[2 * j], carry[2 * j + 1]
                s = _dot(qms[j], ks, NT) - ck_ref[0, kb, j:j + 1, :]
                if masked:
                    s = jnp.where(cmask, s, NEG)
                mn = jnp.maximum(m, jnp.max(s, axis=-1, keepdims=True))
                pr = jnp.exp(s - mn).astype(BF16)
                acc = jnp.exp(m - mn) * acc + _dot(pr, jnp.where(sels[j], vs, ones[j]))
                out += [mn, acc]
            return tuple(out)

        init = (jnp.full((blk, 1), NEG, F32), jnp.zeros((blk, LANES), F32)) * 2
        carry = lax.fori_loop(0, i, lambda kb, c: step(kb, c, False), init)
        m0, a0, m1, a1 = step(i, carry, True)
        l0 = a0[:, HEAD_DIM:HEAD_DIM + 1]
        l1 = a1[:, 0:1]
        o_ref[...] = jnp.where(sels[0], a0 / l0, a1 / l1).astype(BF16)
        lse_ref[...] = jnp.where(sels[0], m0 + jnp.log(l0), m1 + jnp.log(l1))

    return pl.pallas_call(
        body, name="attn_fwd", grid=(npair, p // blk),
        in_specs=[pl.BlockSpec((blk, LANES), lambda h, i: (i, h)),
                  pl.BlockSpec((p, LANES), lambda h, i: (0, h)), pl.BlockSpec((p, LANES), lambda h, i: (0, h)),
                  pl.BlockSpec((1, nkb, 8, blk), lambda h, i: (h, 0, 0, 0))],
        out_specs=[pl.BlockSpec((blk, LANES), lambda h, i: (i, h)), pl.BlockSpec((blk, LANES), lambda h, i: (i, h))],
        out_shape=[jax.ShapeDtypeStruct((p, da), BF16), jax.ShapeDtypeStruct((p, da), F32)],
        compiler_params=_cparams(("parallel", "arbitrary")),
    )(q, k, v, ck)


def _attn_bwd(q, k, v, o, do, lse_rep, ck, blk):
    p, da = q.shape
    npair, nkb = ck.shape[0], ck.shape[1]
    nq = p // blk
    scale = 1.0 / math.sqrt(HEAD_DIM)

    def body(k_ref, v_ref, q_ref, do_ref, o_ref, lse_ref, ck_ref, dk_ref, dv_ref, dq_ref, dcs_ref, rsum_ref, dq_acc):
        jb = pl.program_id(1)

        @pl.when(jb == 0)
        def _():
            dq_acc[...] = jnp.zeros_like(dq_acc)

        ks = k_ref[...]
        vs = v_ref[...]
        lane = lax.broadcasted_iota(jnp.int32, (1, LANES), 1)
        sels = [lane < HEAD_DIM, lane >= HEAD_DIM]
        ones = [jnp.where(lane == HEAD_DIM, 1.0, 0.0).astype(BF16), jnp.where(lane == 0, 1.0, 0.0).astype(BF16)]
        kss = ks * scale
        kmo = [jnp.where(sels[j], kss, ones[j]) for j in range(2)]
        cmask = (lax.broadcasted_iota(jnp.int32, (blk, blk), 1) <= lax.broadcasted_iota(jnp.int32, (blk, blk), 0))

        def step(ib, carry, masked):
            r0 = pl.multiple_of(ib * blk, blk)
            qb = q_ref[pl.ds(r0, blk), :] * scale
            dob = do_ref[pl.ds(r0, blk), :]
            prod = dob.astype(F32) * o_ref[pl.ds(r0, blk), :].astype(F32)
            out = []
            for j in range(2):
                dk, dv = carry[2 * j], carry[2 * j + 1]
                qm = jnp.where(sels[j], qb, jnp.zeros_like(qb))
                dom = jnp.where(sels[j], dob, jnp.zeros_like(dob))
                lse = lse_ref[pl.ds(r0, blk), HEAD_DIM * j:HEAD_DIM * j + 1]
                dlt = jnp.sum(jnp.where(sels[j], prod, 0.0), axis=-1, keepdims=True)
                s = _dot(qm, ks, NT) - ck_ref[0, 0, j:j + 1, :] - lse
                pm = jnp.exp(jnp.minimum(s, 0.0))
                if masked:
                    pm = jnp.where(cmask, pm, 0.0)
                ds_b = (pm * (_dot(dom, vs, NT) - dlt)).astype(BF16)
                dv = dv + _dot(pm.astype(BF16), dom, TN)
                dk = dk + _dot(ds_b, jnp.where(sels[j], qb, ones[j]), TN)
                dq_acc[pl.ds(r0, blk), LANES * j:LANES * (j + 1)] += _dot(ds_b, kmo[j])
                out += [dk, dv]
            return tuple(out)

        zero = jnp.zeros((blk, LANES), F32)
        carry = step(jb, (zero, zero, zero, zero), True)
        dk0, dv0, dk1, dv1 = lax.fori_loop(jb + 1, nq, lambda ib, c: step(ib, c, False), carry)
        dk_ref[...] = jnp.where(sels[0], dk0, dk1).astype(BF16)
        dv_ref[...] = (dv0 + dv1).astype(BF16)
        lane8 = lax.broadcasted_iota(jnp.int32, (1, 8), 1)
        pair8 = lambda c0, c1: jnp.where(lane8 == 0, c0, jnp.where(lane8 == 1, c1, 0.0))
        dcs_ref[0] = pair8(dk0[:, HEAD_DIM:HEAD_DIM + 1], dk1[:, 0:1])

        @pl.when(jb == nkb - 1)
        def _():
            a0 = dq_acc[:, :LANES]
            a1 = dq_acc[:, LANES:]
            dq_ref[...] = jnp.where(sels[0], a0, a1).astype(BF16)
            rsum_ref[0] = pair8(a0[:, HEAD_DIM:HEAD_DIM + 1], a1[:, 0:1])

    colblk = pl.BlockSpec((blk, LANES), lambda h, j: (j, h))
    colfull = pl.BlockSpec((p, LANES), lambda h, j: (0, h))
    ckspec = pl.BlockSpec((1, 1, 8, blk), lambda h, j: (h, j, 0, 0))
    return pl.pallas_call(
        body, name="attn_bwd", grid=(npair, nkb),
        in_specs=[colblk, colblk, colfull, colfull, colfull, colfull, ckspec],
        out_specs=[colblk, colblk, colfull, pl.BlockSpec((1, blk, 8), lambda h, j: (h, j, 0)),
                   pl.BlockSpec((1, p, 8), lambda h, j: (h, 0, 0))],
        out_shape=[jax.ShapeDtypeStruct((p, da), BF16), jax.ShapeDtypeStruct((p, da), BF16),
                   jax.ShapeDtypeStruct((p, da), BF16), jax.ShapeDtypeStruct((npair, p, 8), F32),
                   jax.ShapeDtypeStruct((npair, p, 8), F32)],
        scratch_shapes=[pltpu.VMEM((p, 2 * LANES), F32)],
        compiler_params=_cparams(("parallel", "arbitrary")),
    )(k, v, q, do, o, lse_rep, ck)


def _tail(yssd, o, zatt, graw, head, x2, tgt2, wps, wpa, wout, gate_bias, norm_post):
    p, ds = yssd.shape
    da = o.shape[1]
    d = x2.shape[1]

    def body(yssd_ref, o_ref, zatt_ref, g_ref, head_ref, x_ref, tgt_ref, wps_ref, wpa_ref, wout_ref, gb_ref, np_ref,
             dyssd_ref, do_ref, dzatt_ref, dg_ref, dzo_ref, mrg_ref, da_ref, db_ref, yatt_ref, dout_ref, red_ref):
        i = pl.program_id(0)

        @pl.when(i == 0)
        def _():
            red_ref[...] = jnp.zeros_like(red_ref)

        h = jnp.where(i == 0, head_ref[...], x_ref[...])
        valid = jnp.where(i > 0, 1.0, 0.0)
        ob = o_ref[...].astype(F32)
        za = zatt_ref[...].astype(F32)
        sza = _sigmoid(za)
        silu = za * sza
        yatt_b = (ob * silu).astype(BF16)
        yatt_ref[...] = yatt_b
        wps_v, wpa_v, wout_v = wps_ref[...], wpa_ref[...], wout_ref[...]
        a = _dot(yssd_ref[...], wps_v)
        b = _dot(yatt_b, wpa_v)
        gr = g_ref[...].astype(F32) + gb_ref[...]
        gs = _sigmoid(gr[:, :d])
        ga = _sigmoid(gr[:, d:])
        mrg_b = (gs * a + ga * b).astype(BF16)
        mrg_ref[...] = mrg_b
        zo = _dot(mrg_b, wout_v)
        rstd = lax.rsqrt(jnp.mean(zo * zo, axis=-1, keepdims=True) + EPS)
        zh = zo * rstd
        npw = np_ref[...]
        err = (h + zh * npw - tgt_ref[...]) * valid
        dout = err * (1.0 / d)
        dout_ref[...] = dout
        dzh = dout * npw
        dzo_b = (rstd * (dzh - zh * jnp.mean(dzh * zh, axis=-1, keepdims=True))).astype(BF16)
        dzo_ref[...] = dzo_b
        dm = _dot(dzo_b, wout_v, NT)
        da_b = (gs * dm).astype(BF16)
        db_b = (ga * dm).astype(BF16)
        da_ref[...] = da_b
        db_ref[...] = db_b
        dgs = dm * a * gs * (1.0 - gs)
        dga = dm * b * ga * (1.0 - ga)
        dg_ref[:, :d] = dgs.astype(BF16)
        dg_ref[:, d:] = dga.astype(BF16)
        dyssd_ref[...] = _dot(da_b, wps_v, NT).astype(BF16)
        dya = _dot(db_b, wpa_v, NT)
        do_ref[...] = (dya * silu).astype(BF16)
        dzatt_ref[...] = (dya * ob * sza * (1.0 + za * (1.0 - sza))).astype(BF16)
        red_ref[0:1, :d] += jnp.sum(dout * zh, axis=0, keepdims=True)
        red_ref[1:2, :d] += jnp.sum(dgs, axis=0, keepdims=True)
        red_ref[1:2, d:] += jnp.sum(dga, axis=0, keepdims=True)
        red_ref[2:3, 0:1] += jnp.sum(jnp.sum(err * err, axis=1, keepdims=True), axis=0, keepdims=True) * (0.5 / d)

    row = lambda w: pl.BlockSpec((CHUNK, w), lambda i: (i, 0))
    shifted = lambda w: pl.BlockSpec((CHUNK, w), lambda i: (jnp.maximum(i - 1, 0), 0))
    sd = jax.ShapeDtypeStruct
    return pl.pallas_call(
        body, name="tail", grid=(p // CHUNK,),
        in_specs=[row(ds), row(da), row(da), row(2 * d), _full((CHUNK, d)), shifted(d), shifted(d),
                  _full((ds, d)), _full((da, d)), _full((d, d)), _full((1, 2 * d)), _full((1, d))],
        out_specs=[row(ds), row(da), row(da), row(2 * d), row(d), row(d), row(d), row(d), row(da), row(d),
                   _full((8, 2 * d))],
        out_shape=[sd((p, ds), BF16), sd((p, da), BF16), sd((p, da), BF16), sd((p, 2 * d), BF16), sd((p, d), BF16),
                   sd((p, d), BF16), sd((p, d), BF16), sd((p, d), BF16), sd((p, da), BF16), sd((p, d), F32),
                   sd((8, 2 * d), F32)],
        compiler_params=_cparams(("arbitrary",)),
    )(yssd, o, zatt, graw, head, x2, tgt2, wps, wpa, wout, gate_bias, norm_post)


def _adamw_math(w, g, m, v):
    m2 = ADAM_B1 * m + (1.0 - ADAM_B1) * g
    v2 = ADAM_B2 * v + (1.0 - ADAM_B2) * (g * g)
    m_hat = m2 / (1.0 - ADAM_B1 ** ADAM_STEP)
    v_hat = v2 / (1.0 - ADAM_B2 ** ADAM_STEP)
    delta = -ADAM_LR * (m_hat / (jnp.sqrt(v_hat) + ADAM_EPS) + ADAM_WD * w)
    return delta, m2, v2


def _adamw(w, g, m, v, name, parts=False):
    lead = w.ndim == 3
    r, cdim = w.shape[-2:]
    tr = CHUNK if r % CHUNK == 0 else r
    at = (lambda ref: ref.at[0]) if lead else (lambda ref: ref)

    def body(w_ref, g_ref, m_ref, v_ref, go_ref, d_ref, mo_ref, vo_ref):
        if parts:
            g = g_ref[0].astype(F32)
            for s in range(1, g_ref.shape[0]):
                g = g + g_ref[s].astype(F32)
        else:
            g = g_ref[...]
        delta, m2, v2 = _adamw_math(at(w_ref)[...], g, at(m_ref)[...], at(v_ref)[...])
        at(go_ref)[...] = g
        at(d_ref)[...] = delta
        at(mo_ref)[...] = m2
        at(vo_ref)[...] = v2

    blk = (pl.BlockSpec((1, tr, cdim), lambda i: (0, i, 0)) if lead else pl.BlockSpec((tr, cdim), lambda i: (i, 0)))
    gspec = pl.BlockSpec((g.shape[0], tr, cdim), lambda i: (0, i, 0)) if parts else blk
    return pl.pallas_call(
        body, name=name, grid=(r // tr,),
        in_specs=[blk, gspec, blk, blk], out_specs=[blk] * 4,
        out_shape=[jax.ShapeDtypeStruct(w.shape, F32)] * 4,
        compiler_params=_cparams(("parallel",)),
    )(w, g, m, v)


def _pad_cols(a, width):
    return jnp.pad(a, ((0, 0), (0, width - a.shape[1])))


def _pack_small_shard(conv_w_sh, meta_sh, width):
    return jnp.concatenate([_pad_cols(conv_w_sh, width), jnp.zeros((4, width), F32), _pad_cols(meta_sh, width)], axis=0)


def _pack_small_rep(norm_pre, norm_post, gate_bias, ssd_norm, conv_b, misc, width):
    rows = [norm_pre, norm_post, gate_bias, ssd_norm, conv_b, misc]
    return jnp.concatenate([_pad_cols(r, width) for r in rows] + [jnp.zeros((2, width), F32)], axis=0)


def _misc_row(dt_bias, fgate_bias, a_log, d_skip, extra):
    hs, ha = dt_bias.shape[1], fgate_bias.shape[1]
    return jnp.concatenate([dt_bias, fgate_bias, jnp.zeros((1, LANES - hs - ha), F32), _pad_cols(a_log, LANES),
                            _pad_cols(d_skip, LANES), _pad_cols(extra, LANES)], axis=1)


def kernel(x, meta_tokens, norm_pre, w_in, conv_w, conv_b, dt_bias, a_log, d_skip, ssd_norm, fgate_bias, gate_bias, w_proj_ssd, w_proj_att, w_out, norm_post, loss_target, m_meta_tokens, m_norm_pre, m_w_in, m_conv_w, m_conv_b, m_dt_bias, m_a_log, m_d_skip, m_ssd_norm, m_fgate_bias, m_gate_bias, m_w_proj_ssd, m_w_proj_att, m_w_out, m_norm_post, v_meta_tokens, v_norm_pre, v_w_in, v_conv_w, v_conv_b, v_dt_bias, v_a_log, v_d_skip, v_ssd_norm, v_fgate_bias, v_gate_bias, v_w_proj_ssd, v_w_proj_att, v_w_out, v_norm_post):
    seq, d = x.shape[1], x.shape[2]
    p = seq + CHUNK
    hs, ha = dt_bias.shape[1], fgate_bias.shape[1]
    ds, cd = ssd_norm.shape[1], conv_b.shape[1]
    da = ha * HEAD_DIM
    nc8 = w_in.shape[2]
    cws = cd // N_DEV
    msh = d // N_DEV
    r1, r2, r3 = ds // N_DEV, da // N_DEV, d // N_DEV
    me = _dev_index(*_my_pos())
    x2, tgt2 = x[0], loss_target[0]

    win_sh = w_in[0].astype(BF16)
    rows_sh = jnp.concatenate([w_proj_ssd[0], w_proj_att[0], w_out[0]], axis=0).astype(BF16)
    small_sh = _pack_small_shard(conv_w[0], meta_tokens, cws)
    win_all, rows_all, small_all = _all_gather([win_sh, rows_sh, small_sh], "gather_weights")
    w_full = jnp.transpose(win_all, (1, 0, 2)).reshape(d, N_DEV * nc8)
    cuts = [0, ds, ds + cd, ds + cd + hs, ds + cd + hs + da, ds + cd + hs + 2 * da, ds + cd + hs + 3 * da,
            ds + cd + hs + 4 * da, ds + cd + hs + 4 * da + ha, ds + cd + hs + 4 * da + ha + 2 * d]
    w_z, w_xbc, w_dt, w_zatt, w_q, w_k, w_v, w_f, w_g = [w_full[:, cuts[i]:cuts[i + 1]] for i in range(9)]
    w_dtf = jnp.concatenate([w_dt, w_f, jnp.zeros((d, LANES - hs - ha), BF16)], axis=1)
    wps = rows_all[:, :r1].reshape(ds, d)
    wpa = rows_all[:, r1:r1 + r2].reshape(da, d)
    wout = rows_all[:, r1 + r2:].reshape(d, d)
    conv_w_full = jnp.transpose(small_all[:, 0:CONV_K, :], (1, 0, 2)).reshape(CONV_K, cd)
    meta_full = jnp.transpose(small_all[:, 8:8 + N_META, :msh], (1, 0, 2)).reshape(N_META, d)
    head = jnp.concatenate([jnp.zeros((PADN, d), F32), meta_full], axis=0)

    u = _prenorm_fwd(head, x2, norm_pre)
    tm = _att_block(p)
    seg_w = [w_z, w_xbc, w_zatt, w_q, w_k, w_v, w_g]
    zs, xbc, zatt, q, k, v, graw = [
        _mm(u, w, "nn", BF16, tm, _tile(w.shape[1], (1024, 512, 256, 128)), "inproj_%d" % i) for i, w in enumerate(seg_w)]
    dtf = _mm(u, w_dtf, "nn", F32, tm, LANES, "inproj_dtf")

    brow = jnp.concatenate([dt_bias, fgate_bias, jnp.zeros((1, LANES - hs - ha), F32)], axis=1)
    alog_row = _pad_cols(a_log, LANES)
    dskip_l = jnp.repeat(d_skip, HEAD_DIM, axis=1)
    sel_t = (lax.broadcasted_iota(jnp.int32, (LANES, ds), 1) // HEAD_DIM
             == lax.broadcasted_iota(jnp.int32, (LANES, ds), 0)).astype(BF16)
    sel = sel_t.T
    y, yssd, hin, cf = _ssd_fwd(xbc, zs, dtf, conv_w_full, conv_b, brow, alog_row, dskip_l, ssd_norm, sel_t, hs, ha)

    blk = _att_block(p)
    nkb, npair = p // blk, ha // 2
    cum = jnp.where(lax.broadcasted_iota(jnp.int32, (p, 1), 0) < PADN, -NEG, cf[:, hs:hs + ha])
    ck = jnp.transpose(cum.T.reshape(npair, 2, nkb, blk), (0, 2, 1, 3))
    ck = jnp.pad(ck, ((0, 0), (0, 0), (0, 6), (0, 0)))
    o, lse_rep = _attn_fwd(q, k, v, ck, blk)

    (dyssd, d_o, dzatt, dgraw, dzo, mrg, da_, db_, yatt, dout, red_tail) = _tail(
        yssd, o, zatt, graw, head, x2, tgt2, wps, wpa, wout, gate_bias, norm_post)

    tw = _tile(d, (512, 256, 128))
    g_wout = _mm(mrg, dzo, "tn", BF16, tw, tw, "wgrad_out")
    g_wps = _mm(yssd, da_, "tn", BF16, _tile(ds, (512, 256, 128)), tw, "wgrad_ps")
    g_wpa = _mm(yatt, db_, "tn", BF16, _tile(da, (512, 256, 128)), tw, "wgrad_pa")

    dk, dv, dq, dcs, rsum = _attn_bwd(q, k, v, o, d_o, lse_rep, ck, blk)
    dcum = jnp.transpose((rsum - dcs)[:, :, 0:2], (1, 0, 2)).reshape(p, ha)
    dcf = jnp.pad(dcum, ((0, 0), (hs, LANES - hs - ha)))
    dxbc, dzs, ddtf, gcw, gcb, gnrm, gsm = _ssd_bwd(
        dyssd, y, zs, xbc, dtf, hin, dcf, conv_w_full, conv_b, brow, alog_row, dskip_l, ssd_norm, sel_t, sel, hs, ha)
    ddtf_b = ddtf.astype(BF16)

    dsegs = [dzs, dxbc, dzatt, dq, dk, dv, dgraw, ddtf_b]
    du = _mm_sum_nt(dsegs, seg_w + [w_dtf], tm, _tile(d, (256, 128)), "dgrad_in")
    gx, ghead, gnp = _prenorm_bwd(head, x2, norm_pre, du, dout)

    gsegs = [_mm(u, dsg, "tn", BF16, tw, _tile(dsg.shape[1], (512, 256, 128)), "wgrad_in_%d" % i)
             for i, dsg in enumerate(dsegs)]
    g_z, g_xbc, g_zatt, g_q, g_k, g_v, g_g, g_dtf = gsegs
    gw_full = jnp.concatenate([g_z, g_xbc, g_dtf[:, :hs], g_zatt, g_q, g_k, g_v, g_dtf[:, hs:hs + ha], g_g], axis=1)
    gwin_parts = jnp.transpose(gw_full.reshape(d, N_DEV, nc8), (1, 0, 2))
    grows_parts = jnp.concatenate([g_wps.reshape(N_DEV, r1, d), g_wpa.reshape(N_DEV, r2, d),
                                   g_wout.reshape(N_DEV, r3, d)], axis=1)

    core = lax.axis_index("c").astype(jnp.int32).reshape(1)
    sib_win, sib_rows = _exchange_sibling([gwin_parts, grows_parts], "scatter_grads_sibling")
    chip_win = _pair_add(gwin_parts, sib_win, core, "pair_add_w_in")
    chip_rows = _pair_add(grows_parts, sib_rows, core, "pair_add_rows")
    recv_win, recv_rows = _exchange_chips([chip_win, chip_rows], "scatter_grads_chips")
    gmisc = jnp.concatenate([gsm[0:1], gsm[1:2], gsm[2:3], _pad_cols(red_tail[2:3, 0:1], LANES)], axis=1)
    small_g = jnp.concatenate([
        _pack_small_rep(gnp[0:1], red_tail[0:1, :d], red_tail[1:2], gnrm[0:1], gcb[0:1], gmisc, cd),
        _pad_cols(gcw[0:CONV_K], cd), jnp.zeros((4, cd), F32), _pad_cols(ghead[PADN:], cd)], axis=0)
    red = _all_reduce_small(small_g, "reduce_small")

    loss = red[5, 3 * LANES]
    g_small_sh = _pack_small_shard(lax.dynamic_slice_in_dim(red[8:8 + CONV_K], me * cws, cws, axis=1),
                                   lax.dynamic_slice_in_dim(red[16:16 + N_META, :d], me * msh, msh, axis=1), cws)

    zero1 = jnp.zeros((1, 1), F32)
    upd_in = _adamw(w_in, recv_win, m_w_in, v_w_in, "adamw_w_in", parts=True)
    cat3 = lambda a, b, c: jnp.concatenate([a[0], b[0], c[0]], axis=0)
    upd_rows = _adamw(cat3(w_proj_ssd, w_proj_att, w_out), recv_rows, cat3(m_w_proj_ssd, m_w_proj_att, m_w_out),
                      cat3(v_w_proj_ssd, v_w_proj_att, v_w_out), "adamw_rows", parts=True)
    rep = lambda a, b, c, e, f, g1, g2, g3, g4: _pack_small_rep(a, b, c, e, f, _misc_row(g1, g2, g3, g4, zero1), cd)
    upd_rep = _adamw(rep(norm_pre, norm_post, gate_bias, ssd_norm, conv_b, dt_bias, fgate_bias, a_log, d_skip),
                     red[0:8],
                     rep(m_norm_pre, m_norm_post, m_gate_bias, m_ssd_norm, m_conv_b, m_dt_bias, m_fgate_bias, m_a_log, m_d_skip),
                     rep(v_norm_pre, v_norm_post, v_gate_bias, v_ssd_norm, v_conv_b, v_dt_bias, v_fgate_bias, v_a_log, v_d_skip),
                     "adamw_rep")
    upd_sh = _adamw(small_sh, g_small_sh, _pack_small_shard(m_conv_w[0], m_meta_tokens, cws),
                    _pack_small_shard(v_conv_w[0], v_meta_tokens, cws), "adamw_small_shard")

    def leaves(i):
        a_in, a_rows, a_rep, a_sh = upd_in[i], upd_rows[i], upd_rep[i], upd_sh[i]
        misc = a_rep[5:6]
        return [a_sh[8:8 + N_META, :msh], a_rep[0:1, :d], a_in, a_sh[0:CONV_K][None], a_rep[4:5, :cd],
                misc[:, :hs], misc[:, LANES:LANES + hs], misc[:, 2 * LANES:2 * LANES + hs], a_rep[3:4, :ds],
                misc[:, hs:hs + ha], a_rep[2:3, :2 * d], a_rows[:r1][None], a_rows[r1:r1 + r2][None],
                a_rows[r1 + r2:][None], a_rep[1:2, :d]]

    return tuple([loss, gx[None]] + leaves(0) + leaves(1) + leaves(2) + leaves(3))
```

```python
import functools
import math

import jax
import jax.numpy as jnp
from jax import lax
from jax.experimental import pallas as pl
from jax.experimental.pallas import tpu as pltpu

F32 = jnp.float32
BF16 = jnp.bfloat16

N_DEV = 8
N_META = 16
CHUNK = 128
PADN = CHUNK - N_META
HEAD_DIM = 64
SSD_GROUPS = 4
CONV_K = 4
EPS = 1e-6
NEG = -1e30
LANES = 128
HALO = 16

ADAM_LR = 0.001
ADAM_B1 = 0.9
ADAM_B2 = 0.999
ADAM_EPS = 1e-08
ADAM_WD = 0.01
ADAM_STEP = 10

VMEM_LIMIT = 56 * 1024 * 1024

NN = (((1,), (0,)), ((), ()))
NT = (((1,), (1,)), ((), ()))
TN = (((0,), (0,)), ((), ()))
MESH = pl.DeviceIdType.MESH


def _dot(a, b, dims=NN):
    return lax.dot_general(a, b, dims, preferred_element_type=F32)


def _split2(x):
    hi = x.astype(BF16)
    lo = (x - hi.astype(F32)).astype(BF16)
    return hi, lo


def _dot_sel(x, sel):
    hi, lo = _split2(x)
    return _dot(hi, sel) + _dot(lo, sel)


def _dot_tri(tri, x):
    h1 = x.astype(BF16)
    r1 = x - h1.astype(F32)
    h2 = r1.astype(BF16)
    h3 = (r1 - h2.astype(F32)).astype(BF16)
    return _dot(tri, h1) + _dot(tri, h2) + _dot(tri, h3)


def _sigmoid(x):
    return 1.0 / (1.0 + jnp.exp(-x))


def _softplus(x):
    return jnp.maximum(x, 0.0) + jnp.log(1.0 + jnp.exp(-jnp.abs(x)))


def _cparams(sem=None, vmem=VMEM_LIMIT):
    kw = {"vmem_limit_bytes": vmem}
    if sem is not None:
        kw["dimension_semantics"] = sem
    return pltpu.CompilerParams(**kw)


def _full(shape):
    nd = len(shape)
    return pl.BlockSpec(shape, lambda *_: (0,) * nd)


def _att_block(p):
    return 384 if p % 384 == 0 else CHUNK


def _my_pos():
    return lax.axis_index("x"), lax.axis_index("y"), lax.axis_index("c")


def _dev_index(x, y, c):
    return 4 * x + 2 * y + c


FLIPS = [(fx, fy, fc) for fx in (0, 1) for fy in (0, 1) for fc in (0, 1)][1:]


def _flip(pos, f):
    return tuple((1 - p) if fi else p for p, fi in zip(pos, f))


def _all_gather(bufs, name):
    nb = len(bufs)

    def body(*refs):
        ins, outs = refs[:nb], refs[nb:2 * nb]
        send_sems, recv_sems, local_sems = refs[2 * nb:]
        x, y, c = _my_pos()
        me = _dev_index(x, y, c)
        sibling = (x, y, 1 - c)
        chips = [(1 - x, y), (x, 1 - y), (1 - x, 1 - y)]

        def copy(b, k, block_idx, to, src=None):
            dst = outs[b].at[block_idx]
            return pltpu.make_async_remote_copy(
                src_ref=dst if src is None else src, dst_ref=dst,
                send_sem=send_sems.at[b, k], recv_sem=recv_sems.at[b, k],
                device_id=to, device_id_type=MESH)

        started = []
        for b in range(nb):
            mine = pltpu.make_async_copy(ins[b], outs[b].at[me], local_sems.at[b])
            mine.start()
            started.append(mine)
        first = []
        for b in range(nb):
            first.append(copy(b, 0, me, sibling, src=ins[b]))
            for j, chip in enumerate(chips):
                first.append(copy(b, 1 + j, me, (chip[0], chip[1], c), src=ins[b]))
        for cp in first:
            cp.start()
        passed = []
        for j, chip in enumerate(chips):
            blk = _dev_index(chip[0], chip[1], c)
            for b in range(nb):
                copy(b, 1 + j, blk, (x, y, c)).wait_recv()
                fwd = copy(b, 4 + j, blk, sibling)
                fwd.start()
                passed.append(fwd)
        for b in range(nb):
            copy(b, 0, _dev_index(x, y, 1 - c), (x, y, c)).wait_recv()
        for j, chip in enumerate(chips):
            blk = _dev_index(chip[0], chip[1], 1 - c)
            for b in range(nb):
                copy(b, 4 + j, blk, (x, y, c)).wait_recv()
        for cp in first + passed:
            cp.wait_send()
        for mine in started:
            mine.wait()

    any_spec = pl.BlockSpec(memory_space=pl.ANY)
    return pl.pallas_call(
        body, name=name,
        out_shape=[jax.ShapeDtypeStruct((N_DEV,) + b.shape, b.dtype) for b in bufs],
        in_specs=[any_spec] * nb, out_specs=[any_spec] * nb,
        scratch_shapes=[pltpu.SemaphoreType.DMA((nb, 7)), pltpu.SemaphoreType.DMA((nb, 7)),
                        pltpu.SemaphoreType.DMA((nb,))],
    )(*bufs)


N_CHIP = 4
CHIP_FLIPS = [(1, 0), (0, 1), (1, 1)]


def _exchange_sibling(bufs, name):
    nb = len(bufs)

    def body(*refs):
        ins, outs = refs[:nb], refs[nb:2 * nb]
        send_sems, recv_sems = refs[2 * nb:]
        x, y, c = _my_pos()

        def copy(b, k):
            return pltpu.make_async_remote_copy(
                src_ref=ins[b].at[2 * k + (1 - c)], dst_ref=outs[b].at[k],
                send_sem=send_sems.at[b, k], recv_sem=recv_sems.at[b, k],
                device_id=(x, y, 1 - c), device_id_type=MESH)

        cps = [copy(b, k) for b in range(nb) for k in range(N_CHIP)]
        for cp in cps:
            cp.start()
        for cp in cps:
            cp.wait()

    any_spec = pl.BlockSpec(memory_space=pl.ANY)
    return pl.pallas_call(
        body, name=name,
        out_shape=[jax.ShapeDtypeStruct((N_CHIP,) + b.shape[1:], b.dtype) for b in bufs],
        in_specs=[any_spec] * nb, out_specs=[any_spec] * nb,
        scratch_shapes=[pltpu.SemaphoreType.DMA((nb, N_CHIP)), pltpu.SemaphoreType.DMA((nb, N_CHIP))],
    )(*bufs)


def _pair_add(mine, recv, core, name):
    _, r, cdim = mine.shape
    tr, tc, by_rows = _tiles_2d(r, cdim)
    pick = (lambda i: (i, 0)) if by_rows else (lambda i: (0, i))

    def body(core_ref, a_ref, b_ref, o_ref):
        o_ref[0] = (a_ref[0, 0].astype(F32) + b_ref[0].astype(F32)).astype(o_ref.dtype)

    return pl.pallas_call(
        body, name=name,
        grid_spec=pltpu.PrefetchScalarGridSpec(
            num_scalar_prefetch=1, grid=(N_CHIP, (r // tr) * (cdim // tc)),
            in_specs=[pl.BlockSpec((1, 1, tr, tc), lambda k, i, core_ref: (k, core_ref[0]) + pick(i)),
                      pl.BlockSpec((1, tr, tc), lambda k, i, core_ref: (k,) + pick(i))],
            out_specs=pl.BlockSpec((1, tr, tc), lambda k, i, core_ref: (k,) + pick(i))),
        out_shape=jax.ShapeDtypeStruct((N_CHIP, r, cdim), mine.dtype),
        compiler_params=_cparams(("parallel", "parallel")),
    )(core, mine.reshape(N_CHIP, 2, r, cdim), recv)


def _exchange_chips(bufs, name):
    nb = len(bufs)

    def body(*refs):
        ins, outs = refs[:nb], refs[nb:2 * nb]
        send_sems, recv_sems, local_sems = refs[2 * nb:]
        x, y, c = _my_pos()
        mine = 2 * x + y

        def copy(b, j, f):
            px, py = (1 - x) if f[0] else x, (1 - y) if f[1] else y
            return pltpu.make_async_remote_copy(
                src_ref=ins[b].at[2 * px + py], dst_ref=outs[b].at[mine],
                send_sem=send_sems.at[b, j], recv_sem=recv_sems.at[b, j],
                device_id=(px, py, c), device_id_type=MESH)

        local = [pltpu.make_async_copy(ins[b].at[mine], outs[b].at[mine], local_sems.at[b]) for b in range(nb)]
        for cp in local:
            cp.start()
        sends = [copy(b, j, f) for b in range(nb) for j, f in enumerate(CHIP_FLIPS)]
        for cp in sends:
            cp.start()
        for b in range(nb):
            for j, f in enumerate(CHIP_FLIPS):
                px, py = (1 - x) if f[0] else x, (1 - y) if f[1] else y
                pltpu.make_async_remote_copy(
                    src_ref=ins[b].at[mine], dst_ref=outs[b].at[2 * px + py],
                    send_sem=send_sems.at[b, j], recv_sem=recv_sems.at[b, j],
                    device_id=(px, py, c), device_id_type=MESH).wait_recv()
        for cp in sends:
            cp.wait_send()
        for cp in local:
            cp.wait()

    any_spec = pl.BlockSpec(memory_space=pl.ANY)
    return pl.pallas_call(
        body, name=name,
        out_shape=[jax.ShapeDtypeStruct(b.shape, b.dtype) for b in bufs],
        in_specs=[any_spec] * nb, out_specs=[any_spec] * nb,
        scratch_shapes=[pltpu.SemaphoreType.DMA((nb, 3)), pltpu.SemaphoreType.DMA((nb, 3)),
                        pltpu.SemaphoreType.DMA((nb,))],
    )(*bufs)


def _all_reduce_small(v, name):
    r, cdim = v.shape

    def body(x_ref, out_ref, slots, send_sems, recv_sems):
        pos = _my_pos()
        me = _dev_index(*pos)
        slots[me] = x_ref[...]
        sends = []
        for k, f in enumerate(FLIPS):
            cp = pltpu.make_async_remote_copy(
                src_ref=x_ref, dst_ref=slots.at[me], send_sem=send_sems.at[k], recv_sem=recv_sems.at[k],
                device_id=_flip(pos, f), device_id_type=MESH)
            cp.start()
            sends.append(cp)
        for k, f in enumerate(FLIPS):
            peer = _flip(pos, f)
            pltpu.make_async_remote_copy(
                src_ref=x_ref, dst_ref=slots.at[_dev_index(*peer)], send_sem=send_sems.at[k],
                recv_sem=recv_sems.at[k], device_id=peer, device_id_type=MESH).wait_recv()
        for cp in sends:
            cp.wait_send()
        acc = slots[0]
        for s in range(1, N_DEV):
            acc = acc + slots[s]
        out_ref[...] = acc

    vm = pl.BlockSpec(memory_space=pltpu.VMEM)
    return pl.pallas_call(
        body, name=name, out_shape=jax.ShapeDtypeStruct(v.shape, F32),
        in_specs=[vm], out_specs=vm,
        scratch_shapes=[pltpu.VMEM((N_DEV, r, cdim), F32), pltpu.SemaphoreType.DMA((7,)),
                        pltpu.SemaphoreType.DMA((7,))],
    )(v)


def _mm(a, b, dims, out_dtype, tm, tn, name):
    if dims == "nn":
        (m, k), (_, n) = a.shape, b.shape
        a_spec = pl.BlockSpec((tm, k), lambda j, i: (i, 0))
        b_spec = pl.BlockSpec((k, tn), lambda j, i: (0, j))
        dn = NN
    elif dims == "nt":
        (m, k), (n, _) = a.shape, b.shape
        a_spec = pl.BlockSpec((tm, k), lambda j, i: (i, 0))
        b_spec = pl.BlockSpec((tn, k), lambda j, i: (j, 0))
        dn = NT
    else:
        (k, m), (_, n) = a.shape, b.shape
        a_spec = pl.BlockSpec((k, tm), lambda j, i: (0, i))
        b_spec = pl.BlockSpec((k, tn), lambda j, i: (0, j))
        dn = TN
    assert m % tm == 0 and n % tn == 0, (m, tm, n, tn)

    def body(a_ref, b_ref, o_ref):
        o_ref[...] = _dot(a_ref[...], b_ref[...], dn).astype(o_ref.dtype)

    return pl.pallas_call(
        body, name=name, grid=(n // tn, m // tm),
        in_specs=[a_spec, b_spec], out_specs=pl.BlockSpec((tm, tn), lambda j, i: (i, j)),
        out_shape=jax.ShapeDtypeStruct((m, n), out_dtype),
        compiler_params=_cparams(("parallel", "parallel")),
    )(a, b)


def _tiles_2d(r, cdim):
    if r % CHUNK == 0:
        return CHUNK, cdim, True
    return r, _tile(cdim, (256, 128)), False


def _mm_sum_nn(a_list, b_list, tm, tn, name):
    n_op = len(a_list)
    m, n = a_list[0].shape[0], b_list[0].shape[1]

    def body(*refs):
        acc = _dot(refs[0][...], refs[n_op][...])
        for i in range(1, n_op):
            acc = acc + _dot(refs[i][...], refs[n_op + i][...])
        refs[2 * n_op][...] = acc

    return pl.pallas_call(
        body, name=name, grid=(n // tn, m // tm),
        in_specs=([pl.BlockSpec((tm, a.shape[1]), lambda j, i: (i, 0)) for a in a_list]
                  + [pl.BlockSpec((b.shape[0], tn), lambda j, i: (0, j)) for b in b_list]),
        out_specs=pl.BlockSpec((tm, tn), lambda j, i: (i, j)),
        out_shape=jax.ShapeDtypeStruct((m, n), F32),
        compiler_params=_cparams(("parallel", "parallel")),
    )(*a_list, *b_list)


def _tile(n, prefs):
    for t in prefs:
        if n % t == 0:
            return t
    return n


def _prenorm_fwd(head, x2, w):
    p, d = x2.shape[0] + CHUNK, x2.shape[1]

    def body(head_ref, x_ref, w_ref, u_ref):
        i = pl.program_id(0)
        h = jnp.where(i == 0, head_ref[...], x_ref[...])
        ms = jnp.mean(h * h, axis=-1, keepdims=True)
        u_ref[...] = (h * lax.rsqrt(ms + EPS) * w_ref[...]).astype(BF16)

    return pl.pallas_call(
        body, name="prenorm_fwd", grid=(p // CHUNK,),
        in_specs=[_full((CHUNK, d)), pl.BlockSpec((CHUNK, d), lambda i: (jnp.maximum(i - 1, 0), 0)), _full((1, d))],
        out_specs=pl.BlockSpec((CHUNK, d), lambda i: (i, 0)),
        out_shape=jax.ShapeDtypeStruct((p, d), BF16),
        compiler_params=_cparams(("arbitrary",)),
    )(head, x2, w)


def _prenorm_bwd(head, x2, w, du, dout):
    p, d = x2.shape[0] + CHUNK, x2.shape[1]

    def body(head_ref, x_ref, w_ref, du_ref, dout_ref, gx_ref, ghead_ref, gw_ref):
        i = pl.program_id(0)
        h = jnp.where(i == 0, head_ref[...], x_ref[...])
        rstd = lax.rsqrt(jnp.mean(h * h, axis=-1, keepdims=True) + EPS)
        xhat = h * rstd
        dub = du_ref[...]
        dxh = dub * w_ref[...]
        dh = rstd * (dxh - xhat * jnp.mean(dxh * xhat, axis=-1, keepdims=True)) + dout_ref[...]

        @pl.when(i == 0)
        def _():
            ghead_ref[...] = dh
            gw_ref[...] = jnp.zeros_like(gw_ref)

        gx_ref[...] = dh
        gw_ref[0:1, :] += jnp.sum(dub * xhat, axis=0, keepdims=True)

    return pl.pallas_call(
        body, name="prenorm_bwd", grid=(p // CHUNK,),
        in_specs=[_full((CHUNK, d)), pl.BlockSpec((CHUNK, d), lambda i: (jnp.maximum(i - 1, 0), 0)), _full((1, d)),
                  pl.BlockSpec((CHUNK, d), lambda i: (i, 0)), pl.BlockSpec((CHUNK, d), lambda i: (i, 0))],
        out_specs=[pl.BlockSpec((CHUNK, d), lambda i: (jnp.maximum(i - 1, 0), 0)), _full((CHUNK, d)), _full((8, d))],
        out_shape=[jax.ShapeDtypeStruct(x2.shape, F32), jax.ShapeDtypeStruct((CHUNK, d), F32),
                   jax.ShapeDtypeStruct((8, d), F32)],
        compiler_params=_cparams(("arbitrary",)),
    )(head, x2, w, du, dout)


def _conv_pre(xr, halo128, cw_ref, cb_ref, rows):
    pre = cb_ref[...] + cw_ref[CONV_K - 1:CONV_K, :] * xr
    shifted = []
    for j in range(1, CONV_K):
        sh = jnp.where(rows >= j, pltpu.roll(xr, j, 0), pltpu.roll(halo128, j, 0))
        shifted.append(sh)
        pre = pre + cw_ref[CONV_K - 1 - j:CONV_K - j, :] * sh
    return pre, shifted


def _ssd_scalars(dtf_ref, brow_ref, alog_ref, rowmask, hs, ha, tri):
    lane = lax.broadcasted_iota(jnp.int32, (1, LANES), 1)
    is_dt = lane < hs
    is_f = (lane >= hs) & (lane < hs + ha)
    dtr = dtf_ref[...] + brow_ref[...]
    sp = _softplus(dtr)
    dt = jnp.where(is_dt, sp, 0.0) * rowmask
    logf = jnp.where(is_f, jnp.minimum(dtr, 0.0) - jnp.log(1.0 + jnp.exp(-jnp.abs(dtr))), 0.0) * rowmask
    a_row = jnp.where(is_dt, -jnp.exp(alog_ref[...]), 0.0)
    run = _dot_tri(tri, dt * a_row + logf)
    return dtr, dt, a_row, run, is_dt, is_f


def _tri_mats():
    r = lax.broadcasted_iota(jnp.int32, (CHUNK, CHUNK), 0)
    c = lax.broadcasted_iota(jnp.int32, (CHUNK, CHUNK), 1)
    return r, c


def _ssd_fwd(xbc, z, dtf, conv_w, conv_b, brow, alog, dskip_l, ssd_norm, sel_t, hs, ha):
    p, cd = xbc.shape
    ds = z.shape[1]
    ns = (cd - ds) // (2 * SSD_GROUPS)
    gw = ds // SSD_GROUPS
    nch = p // CHUNK
    hpg = hs // SSD_GROUPS

    def body(xbc_ref, halo_ref, z_ref, dtf_ref, cw_ref, cb_ref, brow_ref, alog_ref, dsk_ref, nrm_ref, selt_ref,
             y_ref, yssd_ref, hin_ref, cf_ref, st_ref, carry_ref, yacc_ref):
        c = pl.program_id(0)

        @pl.when(c == 0)
        def _():
            st_ref[...] = jnp.zeros_like(st_ref)
            carry_ref[...] = jnp.zeros_like(carry_ref)

        rows = lax.broadcasted_iota(jnp.int32, (CHUNK, 1), 0)
        rowmask = jnp.where((rows >= PADN) | (c > 0), 1.0, 0.0)
        ri, ci = _tri_mats()
        causal = ri >= ci
        tri = jnp.where(causal, 1.0, 0.0).astype(BF16)

        xr = xbc_ref[...].astype(F32)
        halo = halo_ref[...].astype(F32) * jnp.where(c > 0, 1.0, 0.0)
        halo128 = jnp.concatenate([jnp.zeros((CHUNK - HALO, cd), F32), halo], axis=0)
        pre, _ = _conv_pre(xr, halo128, cw_ref, cb_ref, rows)
        xc = pre * _sigmoid(pre) * rowmask

        dtr, dt, a_row, run, is_dt, is_f = _ssd_scalars(dtf_ref, brow_ref, alog_ref, rowmask, hs, ha, tri)
        cf = run + carry_ref[...]
        cf_ref[...] = cf
        carry_ref[...] = jnp.where(is_f, cf[CHUNK - 1:CHUNK, :], 0.0)
        cs = jnp.where(is_dt, run, 0.0)
        cl = cs[CHUNK - 1:CHUNK, :]
        selt = selt_ref[...]
        dt_x = _dot_sel(dt, selt)
        e_x = _dot_sel(jnp.exp(cs), selt)
        w_x = _dot_sel(jnp.exp(cl - cs), selt)
        cdec_x = _dot_sel(jnp.broadcast_to(jnp.exp(cl), (8, LANES)), selt)[0:1, :]
        cs_t = cs.T

        xs = xc[:, :ds]
        xdt = xs * dt_x
        xdt_b = xdt.astype(BF16)
        xw_b = (xdt * w_x).astype(BF16)
        lane = lax.broadcasted_iota(jnp.int32, (1, LANES), 1)
        half0 = lane < HEAD_DIM
        for g in range(SSD_GROUPS):
            bg = xc[:, ds + g * ns: ds + (g + 1) * ns].astype(BF16)
            cg = xc[:, ds + SSD_GROUPS * ns + g * ns: ds + SSD_GROUPS * ns + (g + 1) * ns].astype(BF16)
            gm = _dot(cg, bg, NT)
            gs = slice(g * gw, (g + 1) * gw)
            stg = st_ref[:, gs]
            stg_b = stg.astype(BF16)
            hin_ref[0, :, gs] = stg_b
            yoff = _dot(cg, stg_b) * e_x[:, gs]
            for pr in range(gw // LANES):
                sl = slice(g * gw + pr * LANES, g * gw + (pr + 1) * LANES)
                xp = xdt_b[:, sl]
                yd = jnp.zeros((CHUNK, LANES), F32)
                for j in range(2):
                    h = g * hpg + 2 * pr + j
                    seg = cs[:, h:h + 1] - cs_t[h:h + 1, :]
                    m = jnp.where(causal, gm * jnp.exp(jnp.minimum(seg, 0.0)), 0.0).astype(BF16)
                    sel = half0 if j == 0 else jnp.logical_not(half0)
                    yd = yd + _dot(m, jnp.where(sel, xp, jnp.zeros_like(xp)))
                yacc_ref[:, sl] = yd + yoff[:, pr * LANES:(pr + 1) * LANES] + dsk_ref[:, sl] * xs[:, sl]
            st_ref[:, gs] = stg * cdec_x[:, gs] + _dot(bg, xw_b[:, gs], TN)

        y = yacc_ref[...]
        y_ref[...] = y.astype(BF16)
        zf = z_ref[...].astype(F32)
        u = y * zf * _sigmoid(zf)
        for g in range(SSD_GROUPS):
            gs = slice(g * gw, (g + 1) * gw)
            ug = u[:, gs]
            ms = jnp.mean(ug * ug, axis=-1, keepdims=True)
            yssd_ref[:, gs] = (ug * lax.rsqrt(ms + EPS) * nrm_ref[:, gs]).astype(BF16)

    rb = CHUNK // HALO
    return pl.pallas_call(
        body, name="ssd_fwd", grid=(nch,),
        in_specs=[pl.BlockSpec((CHUNK, cd), lambda c: (c, 0)),
                  pl.BlockSpec((HALO, cd), lambda c: (jnp.maximum(c * rb - 1, 0), 0)),
                  pl.BlockSpec((CHUNK, ds), lambda c: (c, 0)),
                  pl.BlockSpec((CHUNK, LANES), lambda c: (c, 0)),
                  _full((CONV_K, cd)), _full((1, cd)), _full((1, LANES)), _full((1, LANES)),
                  _full((1, ds)), _full((1, ds)), _full((LANES, ds))],
        out_specs=[pl.BlockSpec((CHUNK, ds), lambda c: (c, 0)), pl.BlockSpec((CHUNK, ds), lambda c: (c, 0)),
                   pl.BlockSpec((1, ns, ds), lambda c: (c, 0, 0)), pl.BlockSpec((CHUNK, LANES), lambda c: (c, 0))],
        out_shape=[jax.ShapeDtypeStruct((p, ds), BF16), jax.ShapeDtypeStruct((p, ds), BF16),
                   jax.ShapeDtypeStruct((nch, ns, ds), BF16), jax.ShapeDtypeStruct((p, LANES), F32)],
        scratch_shapes=[pltpu.VMEM((ns, ds), F32), pltpu.VMEM((1, LANES), F32), pltpu.VMEM((CHUNK, ds), F32)],
        compiler_params=_cparams(("arbitrary",)),
    )(xbc, xbc, z, dtf, conv_w, conv_b, brow, alog, dskip_l, ssd_norm, sel_t)


def _ssd_bwd(dyssd, y, z, xbc, dtf, hin, dcf, conv_w, conv_b, brow, alog, dskip_l, ssd_norm, sel_t, sel, hs, ha):
    p, cd = xbc.shape
    ds = z.shape[1]
    ns = (cd - ds) // (2 * SSD_GROUPS)
    gw = ds // SSD_GROUPS
    nch = p // CHUNK
    hpg = hs // SSD_GROUPS
    rb = CHUNK // HALO

    def body(dyssd_ref, y_ref, z_ref, xbc_ref, halo_ref, dtf_ref, hin_ref, dcf_ref, cw_ref, cb_ref, brow_ref,
             alog_ref, dsk_ref, nrm_ref, selt_ref, sel_ref,
             dxbc_ref, dz_ref, ddtf_ref, gcw_ref, gcb_ref, gnrm_ref, gsm_ref,
             dst_ref, nxt_ref, fcar_ref, gdsk_ref, dxc_ref):
        step = pl.program_id(0)
        c = nch - 1 - step

        @pl.when(step == 0)
        def _():
            dst_ref[...] = jnp.zeros_like(dst_ref)
            nxt_ref[...] = jnp.zeros_like(nxt_ref)
            fcar_ref[...] = jnp.zeros_like(fcar_ref)
            gdsk_ref[...] = jnp.zeros_like(gdsk_ref)
            gcw_ref[...] = jnp.zeros_like(gcw_ref)
            gcb_ref[...] = jnp.zeros_like(gcb_ref)
            gnrm_ref[...] = jnp.zeros_like(gnrm_ref)
            gsm_ref[...] = jnp.zeros_like(gsm_ref)

        rows = lax.broadcasted_iota(jnp.int32, (CHUNK, 1), 0)
        rowmask = jnp.where((rows >= PADN) | (c > 0), 1.0, 0.0)
        ri, ci = _tri_mats()
        causal = ri >= ci
        anti = ci >= ri
        tri = jnp.where(causal, 1.0, 0.0).astype(BF16)
        rtri = jnp.where(anti, 1.0, 0.0).astype(BF16)

        xr = xbc_ref[...].astype(F32)
        halo = halo_ref[...].astype(F32) * jnp.where(c > 0, 1.0, 0.0)
        halo128 = jnp.concatenate([jnp.zeros((CHUNK - HALO, cd), F32), halo], axis=0)
        pre, shifted = _conv_pre(xr, halo128, cw_ref, cb_ref, rows)
        sg = _sigmoid(pre)
        xc = pre * sg * rowmask
        dsilu = sg * (1.0 + pre * (1.0 - sg)) * rowmask

        dtr, dt, a_row, run, is_dt, is_f = _ssd_scalars(dtf_ref, brow_ref, alog_ref, rowmask, hs, ha, tri)
        cs = jnp.where(is_dt, run, 0.0)
        cl = cs[CHUNK - 1:CHUNK, :]
        selt = selt_ref[...]
        selm = sel_ref[...]
        dt_x = _dot_sel(dt, selt)
        e_x = _dot_sel(jnp.exp(cs), selt)
        w_x = _dot_sel(jnp.exp(cl - cs), selt)
        cdec = jnp.exp(cl)
        cdec_x = _dot_sel(jnp.broadcast_to(cdec, (8, LANES)), selt)[0:1, :]
        cs_t = cs.T
        xs = xc[:, :ds]
        xdt = xs * dt_x
        xdt_b = xdt.astype(BF16)
        xw_b = (xdt * w_x).astype(BF16)

        yv = y_ref[...].astype(F32)
        zf = z_ref[...].astype(F32)
        sz = _sigmoid(zf)
        u = yv * zf * sz
        dyo = dyssd_ref[...].astype(F32)
        du_parts = []
        for g in range(SSD_GROUPS):
            gs = slice(g * gw, (g + 1) * gw)
            ug = u[:, gs]
            rstd = lax.rsqrt(jnp.mean(ug * ug, axis=-1, keepdims=True) + EPS)
            yhat = ug * rstd
            dyg = dyo[:, gs]
            gnrm_ref[0:1, gs] += jnp.sum(dyg * yhat, axis=0, keepdims=True)
            dyh = dyg * nrm_ref[:, gs]
            du_parts.append(rstd * (dyh - yhat * jnp.mean(dyh * yhat, axis=-1, keepdims=True)))
        du = jnp.concatenate(du_parts, axis=1)
        dy = du * zf * sz
        dz_ref[...] = (du * yv * sz * (1.0 + zf * (1.0 - sz))).astype(BF16)

        dsk = dsk_ref[...]
        gdsk_ref[...] += jnp.sum(dy * xs, axis=0, keepdims=True)
        dy_b = dy.astype(BF16)
        dye_b = (dy * e_x).astype(BF16)
        lane = lax.broadcasted_iota(jnp.int32, (1, LANES), 1)
        half0 = lane < HEAD_DIM
        x_parts, yo_parts, t4_parts = [], [], []
        dcs = jnp.zeros((CHUNK, LANES), F32)
        for g in range(SSD_GROUPS):
            gs = slice(g * gw, (g + 1) * gw)
            bsl = slice(ds + g * ns, ds + (g + 1) * ns)
            csl = slice(ds + SSD_GROUPS * ns + g * ns, ds + SSD_GROUPS * ns + (g + 1) * ns)
            bg = xc[:, bsl].astype(BF16)
            cg = xc[:, csl].astype(BF16)
            gm = _dot(cg, bg, NT)
            gm_t = _dot(bg, cg, NT)
            stg_b = hin_ref[0, :, gs]
            dstg = dst_ref[:, gs]
            dstg_b = dstg.astype(BF16)
            t4_parts.append(jnp.sum(dstg * stg_b.astype(F32), axis=0, keepdims=True))
            zst = _dot(bg, dstg_b) * w_x[:, gs]
            x_parts.append(xdt[:, gs] * zst)
            yo_parts.append(dy[:, gs] * (_dot(cg, stg_b) * e_x[:, gs]))
            dgsum = jnp.zeros((CHUNK, CHUNK), F32)
            dgtsum = jnp.zeros((CHUNK, CHUNK), F32)
            for pr in range(gw // LANES):
                sl = slice(g * gw + pr * LANES, g * gw + (pr + 1) * LANES)
                xp = xdt_b[:, sl]
                dyp = dy_b[:, sl]
                dxd = zst[:, pr * LANES:(pr + 1) * LANES]
                for j in range(2):
                    h = g * hpg + 2 * pr + j
                    sel_l = half0 if j == 0 else jnp.logical_not(half0)
                    seg = cs[:, h:h + 1] - cs_t[h:h + 1, :]
                    lm = jnp.where(causal, jnp.exp(jnp.minimum(seg, 0.0)), 0.0)
                    lmt = jnp.where(anti, jnp.exp(jnp.minimum(-seg, 0.0)), 0.0)
                    dyp_m = jnp.where(sel_l, dyp, jnp.zeros_like(dyp))
                    xp_m = jnp.where(sel_l, xp, jnp.zeros_like(xp))
                    dxd = dxd + _dot((gm_t * lmt).astype(BF16), dyp_m)
                    dg = _dot(dyp_m, xp, NT) * lm
                    dgt = _dot(xp_m, dyp, NT) * lmt
                    dgsum = dgsum + dg
                    dgtsum = dgtsum + dgt
                    qrow = (jnp.sum(dg * gm, axis=1, keepdims=True) - jnp.sum(dgt * gm_t, axis=1, keepdims=True))
                    dcs = dcs + jnp.where(lane == h, qrow, 0.0)
                dxc_ref[:, sl] = dxd
            dxc_ref[:, csl] = _dot(dgsum.astype(BF16), bg) + _dot(dye_b[:, gs], stg_b, NT)
            dxc_ref[:, bsl] = _dot(dgtsum.astype(BF16), cg) + _dot(xw_b[:, gs], dstg_b, NT)
            dst_ref[:, gs] = dstg * cdec_x[:, gs] + _dot(cg, dye_b[:, gs], TN)

        dxdt = dxc_ref[:, :ds]
        xst = _dot_sel(jnp.concatenate(x_parts, axis=1), selm)
        yo = _dot_sel(jnp.concatenate(yo_parts, axis=1), selm)
        t4 = _dot_sel(jnp.concatenate([jnp.concatenate(t4_parts, axis=1), jnp.zeros((7, ds), F32)], axis=0), selm)
        dcl = jnp.sum(xst, axis=0, keepdims=True) + cdec * t4[0:1, :]
        dcs = dcs + yo - xst + jnp.where(rows == CHUNK - 1, dcl, 0.0)
        da_ = _dot_tri(rtri, dcs)
        ddt = _dot_sel(dxdt * xs, selm) + da_ * a_row
        dcf_blk = dcf_ref[...]
        dlogf = _dot_tri(rtri, dcf_blk) + fcar_ref[...]
        fcar_ref[...] += jnp.sum(dcf_blk, axis=0, keepdims=True)
        sgd = _sigmoid(dtr)
        ddtf = (jnp.where(is_dt, ddt * sgd, 0.0) + jnp.where(is_f, dlogf * (1.0 - sgd), 0.0)) * rowmask
        ddtf_ref[...] = ddtf
        gsm_ref[0:1, :] += jnp.sum(ddtf, axis=0, keepdims=True)
        gsm_ref[1:2, :] += jnp.sum(da_ * dt, axis=0, keepdims=True) * a_row

        dxc_ref[:, :ds] = dxdt * dt_x + dsk * dy
        dpre = dxc_ref[...] * dsilu
        gcb_ref[0:1, :] += jnp.sum(dpre, axis=0, keepdims=True)
        gcw_ref[CONV_K - 1:CONV_K, :] += jnp.sum(dpre * xr, axis=0, keepdims=True)
        nxt128 = jnp.concatenate([nxt_ref[...], jnp.zeros((CHUNK - 8, cd), F32)], axis=0)
        dxr = cw_ref[CONV_K - 1:CONV_K, :] * dpre
        for j in range(1, CONV_K):
            gcw_ref[CONV_K - 1 - j:CONV_K - j, :] += jnp.sum(dpre * shifted[j - 1], axis=0, keepdims=True)
            up = jnp.where(rows < CHUNK - j, pltpu.roll(dpre, CHUNK - j, 0), pltpu.roll(nxt128, CHUNK - j, 0))
            dxr = dxr + cw_ref[CONV_K - 1 - j:CONV_K - j, :] * up
        nxt_ref[...] = dpre[0:8, :]
        dxbc_ref[...] = dxr.astype(BF16)

        @pl.when(step == nch - 1)
        def _():
            gsm_ref[2:3, :] = _dot_sel(jnp.broadcast_to(gdsk_ref[...], (8, ds)), selm)[0:1, :]

    rev = lambda s: nch - 1 - s
    blk = lambda w: pl.BlockSpec((CHUNK, w), lambda s: (rev(s), 0))
    return pl.pallas_call(
        body, name="ssd_bwd", grid=(nch,),
        in_specs=[blk(ds), blk(ds), blk(ds), blk(cd),
                  pl.BlockSpec((HALO, cd), lambda s: (jnp.maximum(rev(s) * rb - 1, 0), 0)),
                  blk(LANES), pl.BlockSpec((1, ns, ds), lambda s: (rev(s), 0, 0)), blk(LANES),
                  _full((CONV_K, cd)), _full((1, cd)), _full((1, LANES)), _full((1, LANES)),
                  _full((1, ds)), _full((1, ds)), _full((LANES, ds)), _full((ds, LANES))],
        out_specs=[blk(cd), blk(ds), blk(LANES), _full((8, cd)), _full((8, cd)), _full((8, ds)), _full((8, LANES))],
        out_shape=[jax.ShapeDtypeStruct((p, cd), BF16), jax.ShapeDtypeStruct((p, ds), BF16),
                   jax.ShapeDtypeStruct((p, LANES), F32), jax.ShapeDtypeStruct((8, cd), F32),
                   jax.ShapeDtypeStruct((8, cd), F32), jax.ShapeDtypeStruct((8, ds), F32),
                   jax.ShapeDtypeStruct((8, LANES), F32)],
        scratch_shapes=[pltpu.VMEM((ns, ds), F32), pltpu.VMEM((8, cd), F32), pltpu.VMEM((1, LANES), F32),
                        pltpu.VMEM((1, ds), F32), pltpu.VMEM((CHUNK, cd), F32)],
        compiler_params=_cparams(("arbitrary",)),
    )(dyssd, y, z, xbc, xbc, dtf, hin, dcf, conv_w, conv_b, brow, alog, dskip_l, ssd_norm, sel_t, sel)


def _attn_fwd(q, k, v, ck, blk):
    p, da = q.shape
    npair, nkb = ck.shape[0], ck.shape[1]
    scale = 1.0 / math.sqrt(HEAD_DIM)

    def body(q_ref, k_ref, v_ref, ck_ref, o_ref, lse_ref):
        i = pl.program_id(1)
        lane = lax.broadcasted_iota(jnp.int32, (1, LANES), 1)
        sels = [lane < HEAD_DIM, lane >= HEAD_DIM]
        ones = [jnp.where(lane == HEAD_DIM, 1.0, 0.0).astype(BF16), jnp.where(lane == 0, 1.0, 0.0).astype(BF16)]
        qb = q_ref[...] * scale
        qms = [jnp.where(sel, qb, jnp.zeros_like(qb)) for sel in sels]
        cmask = (lax.broadcasted_iota(jnp.int32, (blk, blk), 1) <= lax.broadcasted_iota(jnp.int32, (blk, blk), 0))

        def step(kb, carry, masked):
            r0 = pl.multiple_of(kb * blk, blk)
            ks = k_ref[pl.ds(r0, blk), :]
            vs = v_ref[pl.ds(r0, blk), :]
            out = []
            for j in range(2):
                m, acc = carry[2 * j], carry[2 * j + 1]
                s = _dot(qms[j], ks, NT) - ck_ref[0, kb, j:j + 1, :]
                if masked:
                    s = jnp.where(cmask, s, NEG)
                mn = jnp.maximum(m, jnp.max(s, axis=-1, keepdims=True))
                pr = jnp.exp(s - mn).astype(BF16)
                acc = jnp.exp(m - mn) * acc + _dot(pr, jnp.where(sels[j], vs, ones[j]))
                out += [mn, acc]
            return tuple(out)

        init = (jnp.full((blk, 1), NEG, F32), jnp.zeros((blk, LANES), F32)) * 2
        carry = lax.fori_loop(0, i, lambda kb, c: step(kb, c, False), init)
        m0, a0, m1, a1 = step(i, carry, True)
        l0 = a0[:, HEAD_DIM:HEAD_DIM + 1]
        l1 = a1[:, 0:1]
        o_ref[...] = jnp.where(sels[0], a0 / l0, a1 / l1).astype(BF16)
        lse_ref[...] = jnp.where(sels[0], m0 + jnp.log(l0), m1 + jnp.log(l1))

    return pl.pallas_call(
        body, name="attn_fwd", grid=(npair, p // blk),
        in_specs=[pl.BlockSpec((blk, LANES), lambda h, i: (i, h)),
                  pl.BlockSpec((p, LANES), lambda h, i: (0, h)), pl.BlockSpec((p, LANES), lambda h, i: (0, h)),
                  pl.BlockSpec((1, nkb, 8, blk), lambda h, i: (h, 0, 0, 0))],
        out_specs=[pl.BlockSpec((blk, LANES), lambda h, i: (i, h)), pl.BlockSpec((blk, LANES), lambda h, i: (i, h))],
        out_shape=[jax.ShapeDtypeStruct((p, da), BF16), jax.ShapeDtypeStruct((p, da), F32)],
        compiler_params=_cparams(("parallel", "arbitrary")),
    )(q, k, v, ck)


def _attn_bwd(q, k, v, o, do, lse_rep, ck, blk):
    p, da = q.shape
    npair, nkb = ck.shape[0], ck.shape[1]
    nq = p // blk
    scale = 1.0 / math.sqrt(HEAD_DIM)

    def body(k_ref, v_ref, q_ref, do_ref, o_ref, lse_ref, ck_ref, dk_ref, dv_ref, dq_ref, dcs_ref, rsum_ref, dq_acc):
        jb = pl.program_id(1)

        @pl.when(jb == 0)
        def _():
            dq_acc[...] = jnp.zeros_like(dq_acc)

        ks = k_ref[...]
        vs = v_ref[...]
        lane = lax.broadcasted_iota(jnp.int32, (1, LANES), 1)
        sels = [lane < HEAD_DIM, lane >= HEAD_DIM]
        ones = [jnp.where(lane == HEAD_DIM, 1.0, 0.0).astype(BF16), jnp.where(lane == 0, 1.0, 0.0).astype(BF16)]
        kss = ks * scale
        kmo = [jnp.where(sels[j], kss, ones[j]) for j in range(2)]
        cmask = (lax.broadcasted_iota(jnp.int32, (blk, blk), 1) <= lax.broadcasted_iota(jnp.int32, (blk, blk), 0))

        def step(ib, carry, masked):
            r0 = pl.multiple_of(ib * blk, blk)
            qb = q_ref[pl.ds(r0, blk), :] * scale
            dob = do_ref[pl.ds(r0, blk), :]
            prod = dob.astype(F32) * o_ref[pl.ds(r0, blk), :].astype(F32)
            out = []
            for j in range(2):
                dk, dv = carry[2 * j], carry[2 * j + 1]
                qm = jnp.where(sels[j], qb, jnp.zeros_like(qb))
                dom = jnp.where(sels[j], dob, jnp.zeros_like(dob))
                lse = lse_ref[pl.ds(r0, blk), HEAD_DIM * j:HEAD_DIM * j + 1]
                dlt = jnp.sum(jnp.where(sels[j], prod, 0.0), axis=-1, keepdims=True)
                s = _dot(qm, ks, NT) - ck_ref[0, 0, j:j + 1, :] - lse
                pm = jnp.exp(jnp.minimum(s, 0.0))
                if masked:
                    pm = jnp.where(cmask, pm, 0.0)
                ds_b = (pm * (_dot(dom, vs, NT) - dlt)).astype(BF16)
                dv = dv + _dot(pm.astype(BF16), dom, TN)
                dk = dk + _dot(ds_b, jnp.where(sels[j], qb, ones[j]), TN)
                dq_acc[pl.ds(r0, blk), LANES * j:LANES * (j + 1)] += _dot(ds_b, kmo[j])
                out += [dk, dv]
            return tuple(out)

        zero = jnp.zeros((blk, LANES), F32)
        carry = step(jb, (zero, zero, zero, zero), True)
        dk0, dv0, dk1, dv1 = lax.fori_loop(jb + 1, nq, lambda ib, c: step(ib, c, False), carry)
        dk_ref[...] = jnp.where(sels[0], dk0, dk1).astype(BF16)
        dv_ref[...] = (dv0 + dv1).astype(BF16)
        lane8 = lax.broadcasted_iota(jnp.int32, (1, 8), 1)
        pair8 = lambda c0, c1: jnp.where(lane8 == 0, c0, jnp.where(lane8 == 1, c1, 0.0))
        dcs_ref[0] = pair8(dk0[:, HEAD_DIM:HEAD_DIM + 1], dk1[:, 0:1])

        @pl.when(jb == nkb - 1)
        def _():
            a0 = dq_acc[:, :LANES]
            a1 = dq_acc[:, LANES:]
            dq_ref[...] = jnp.where(sels[0], a0, a1).astype(BF16)
            rsum_ref[0] = pair8(a0[:, HEAD_DIM:HEAD_DIM + 1], a1[:, 0:1])

    colblk = pl.BlockSpec((blk, LANES), lambda h, j: (j, h))
    colfull = pl.BlockSpec((p, LANES), lambda h, j: (0, h))
    ckspec = pl.BlockSpec((1, 1, 8, blk), lambda h, j: (h, j, 0, 0))
    return pl.pallas_call(
        body, name="attn_bwd", grid=(npair, nkb),
        in_specs=[colblk, colblk, colfull, colfull, colfull, colfull, ckspec],
        out_specs=[colblk, colblk, colfull, pl.BlockSpec((1, blk, 8), lambda h, j: (h, j, 0)),
                   pl.BlockSpec((1, p, 8), lambda h, j: (h, 0, 0))],
        out_shape=[jax.ShapeDtypeStruct((p, da), BF16), jax.ShapeDtypeStruct((p, da), BF16),
                   jax.ShapeDtypeStruct((p, da), BF16), jax.ShapeDtypeStruct((npair, p, 8), F32),
                   jax.ShapeDtypeStruct((npair, p, 8), F32)],
        scratch_shapes=[pltpu.VMEM((p, 2 * LANES), F32)],
        compiler_params=_cparams(("parallel", "arbitrary")),
    )(k, v, q, do, o, lse_rep, ck)


def _tail(yssd, o, zatt, graw, head, x2, tgt2, wps, wpa, wout, gate_bias, norm_post):
    p, ds = yssd.shape
    da = o.shape[1]
    d = x2.shape[1]

    def body(yssd_ref, o_ref, zatt_ref, g_ref, head_ref, x_ref, tgt_ref, wps_ref, wpa_ref, wout_ref, gb_ref, np_ref,
             dyssd_ref, do_ref, dzatt_ref, dg_ref, dzo_ref, mrg_ref, da_ref, db_ref, yatt_ref, dout_ref, red_ref):
        i = pl.program_id(0)

        @pl.when(i == 0)
        def _():
            red_ref[...] = jnp.zeros_like(red_ref)

        h = jnp.where(i == 0, head_ref[...], x_ref[...])
        valid = jnp.where(i > 0, 1.0, 0.0)
        ob = o_ref[...].astype(F32)
        za = zatt_ref[...].astype(F32)
        sza = _sigmoid(za)
        silu = za * sza
        yatt_b = (ob * silu).astype(BF16)
        yatt_ref[...] = yatt_b
        wps_v, wpa_v, wout_v = wps_ref[...], wpa_ref[...], wout_ref[...]
        a = _dot(yssd_ref[...], wps_v)
        b = _dot(yatt_b, wpa_v)
        gr = g_ref[...].astype(F32) + gb_ref[...]
        gs = _sigmoid(gr[:, :d])
        ga = _sigmoid(gr[:, d:])
        mrg_b = (gs * a + ga * b).astype(BF16)
        mrg_ref[...] = mrg_b
        zo = _dot(mrg_b, wout_v)
        rstd = lax.rsqrt(jnp.mean(zo * zo, axis=-1, keepdims=True) + EPS)
        zh = zo * rstd
        npw = np_ref[...]
        err = (h + zh * npw - tgt_ref[...]) * valid
        dout = err * (1.0 / d)
        dout_ref[...] = dout
        dzh = dout * npw
        dzo_b = (rstd * (dzh - zh * jnp.mean(dzh * zh, axis=-1, keepdims=True))).astype(BF16)
        dzo_ref[...] = dzo_b
        dm = _dot(dzo_b, wout_v, NT)
        da_b = (gs * dm).astype(BF16)
        db_b = (ga * dm).astype(BF16)
        da_ref[...] = da_b
        db_ref[...] = db_b
        dgs = dm * a * gs * (1.0 - gs)
        dga = dm * b * ga * (1.0 - ga)
        dg_ref[:, :d] = dgs.astype(BF16)
        dg_ref[:, d:] = dga.astype(BF16)
        dyssd_ref[...] = _dot(da_b, wps_v, NT).astype(BF16)
        dya = _dot(db_b, wpa_v, NT)
        do_ref[...] = (dya * silu).astype(BF16)
        dzatt_ref[...] = (dya * ob * sza * (1.0 + za * (1.0 - sza))).astype(BF16)
        red_ref[0:1, :d] += jnp.sum(dout * zh, axis=0, keepdims=True)
        red_ref[1:2, :d] += jnp.sum(dgs, axis=0, keepdims=True)
        red_ref[1:2, d:] += jnp.sum(dga, axis=0, keepdims=True)
        red_ref[2:3, 0:1] += jnp.sum(jnp.sum(err * err, axis=1, keepdims=True), axis=0, keepdims=True) * (0.5 / d)

    row = lambda w: pl.BlockSpec((CHUNK, w), lambda i: (i, 0))
    shifted = lambda w: pl.BlockSpec((CHUNK, w), lambda i: (jnp.maximum(i - 1, 0), 0))
    sd = jax.ShapeDtypeStruct
    return pl.pallas_call(
        body, name="tail", grid=(p // CHUNK,),
        in_specs=[row(ds), row(da), row(da), row(2 * d), _full((CHUNK, d)), shifted(d), shifted(d),
                  _full((ds, d)), _full((da, d)), _full((d, d)), _full((1, 2 * d)), _full((1, d))],
        out_specs=[row(ds), row(da), row(da), row(2 * d), row(d), row(d), row(d), row(d), row(da), row(d),
                   _full((8, 2 * d))],
        out_shape=[sd((p, ds), BF16), sd((p, da), BF16), sd((p, da), BF16), sd((p, 2 * d), BF16), sd((p, d), BF16),
                   sd((p, d), BF16), sd((p, d), BF16), sd((p, d), BF16), sd((p, da), BF16), sd((p, d), F32),
                   sd((8, 2 * d), F32)],
        compiler_params=_cparams(("arbitrary",)),
    )(yssd, o, zatt, graw, head, x2, tgt2, wps, wpa, wout, gate_bias, norm_post)


def _adamw_math(w, g, m, v):
    m2 = ADAM_B1 * m + (1.0 - ADAM_B1) * g
    v2 = ADAM_B2 * v + (1.0 - ADAM_B2) * (g * g)
    m_hat = m2 / (1.0 - ADAM_B1 ** ADAM_STEP)
    v_hat = v2 / (1.0 - ADAM_B2 ** ADAM_STEP)
    delta = -ADAM_LR * (m_hat / (jnp.sqrt(v_hat) + ADAM_EPS) + ADAM_WD * w)
    return delta, m2, v2


def _adamw(w, g, m, v, name, parts=False):
    r, cdim = w.shape
    tr, tc, by_rows = _tiles_2d(r, cdim)
    pick = (lambda i: (i, 0)) if by_rows else (lambda i: (0, i))

    def body(w_ref, g_ref, m_ref, v_ref, go_ref, d_ref, mo_ref, vo_ref):
        if parts:
            g = g_ref[0].astype(F32)
            for s in range(1, g_ref.shape[0]):
                g = g + g_ref[s].astype(F32)
        else:
            g = g_ref[...]
        delta, m2, v2 = _adamw_math(w_ref[...], g, m_ref[...], v_ref[...])
        go_ref[...] = g
        d_ref[...] = delta
        mo_ref[...] = m2
        vo_ref[...] = v2

    blk = pl.BlockSpec((tr, tc), pick)
    gspec = pl.BlockSpec((g.shape[0], tr, tc), lambda i: (0,) + pick(i)) if parts else blk
    return pl.pallas_call(
        body, name=name, grid=((r // tr) * (cdim // tc),),
        in_specs=[blk, gspec, blk, blk], out_specs=[blk] * 4,
        out_shape=[jax.ShapeDtypeStruct((r, cdim), F32)] * 4,
        compiler_params=_cparams(("parallel",)),
    )(w, g, m, v)


def _pad_cols(a, width):
    return jnp.pad(a, ((0, 0), (0, width - a.shape[1])))


def _pack_small_shard(conv_w_sh, meta_sh, width):
    return jnp.concatenate([_pad_cols(conv_w_sh, width), jnp.zeros((4, width), F32), _pad_cols(meta_sh, width)], axis=0)


def _pack_small_rep(norm_pre, norm_post, gate_bias, ssd_norm, conv_b, misc, width):
    rows = [norm_pre, norm_post, gate_bias, ssd_norm, conv_b, misc]
    return jnp.concatenate([_pad_cols(r, width) for r in rows] + [jnp.zeros((2, width), F32)], axis=0)


def _misc_row(dt_bias, fgate_bias, a_log, d_skip, extra):
    hs, ha = dt_bias.shape[1], fgate_bias.shape[1]
    return jnp.concatenate([dt_bias, fgate_bias, jnp.zeros((1, LANES - hs - ha), F32), _pad_cols(a_log, LANES),
                            _pad_cols(d_skip, LANES), _pad_cols(extra, LANES)], axis=1)


def kernel(x, meta_tokens, norm_pre, w_in, conv_w, conv_b, dt_bias, a_log, d_skip, ssd_norm, fgate_bias, gate_bias, w_proj_ssd, w_proj_att, w_out, norm_post, loss_target, m_meta_tokens, m_norm_pre, m_w_in, m_conv_w, m_conv_b, m_dt_bias, m_a_log, m_d_skip, m_ssd_norm, m_fgate_bias, m_gate_bias, m_w_proj_ssd, m_w_proj_att, m_w_out, m_norm_post, v_meta_tokens, v_norm_pre, v_w_in, v_conv_w, v_conv_b, v_dt_bias, v_a_log, v_d_skip, v_ssd_norm, v_fgate_bias, v_gate_bias, v_w_proj_ssd, v_w_proj_att, v_w_out, v_norm_post):
    seq, d = x.shape[1], x.shape[2]
    p = seq + CHUNK
    hs, ha = dt_bias.shape[1], fgate_bias.shape[1]
    ds, cd = ssd_norm.shape[1], conv_b.shape[1]
    da = ha * HEAD_DIM
    nc8 = w_in.shape[2]
    cws = cd // N_DEV
    msh = d // N_DEV
    r1, r2, r3 = ds // N_DEV, da // N_DEV, d // N_DEV
    me = _dev_index(*_my_pos())
    x2, tgt2 = x[0], loss_target[0]

    win_sh = jnp.transpose(w_in[0]).astype(BF16)
    rows_sh = jnp.concatenate([w_proj_ssd[0], w_proj_att[0], w_out[0]], axis=0).astype(BF16)
    small_sh = _pack_small_shard(conv_w[0], meta_tokens, cws)
    win_all, rows_all, small_all = _all_gather([win_sh, rows_sh, small_sh], "gather_weights")
    w_full = win_all.reshape(N_DEV * nc8, d)
    cuts = [0, ds, ds + cd, ds + cd + hs, ds + cd + hs + da, ds + cd + hs + 2 * da, ds + cd + hs + 3 * da,
            ds + cd + hs + 4 * da, ds + cd + hs + 4 * da + ha, ds + cd + hs + 4 * da + ha + 2 * d]
    w_z, w_xbc, w_dt, w_zatt, w_q, w_k, w_v, w_f, w_g = [w_full[cuts[i]:cuts[i + 1]] for i in range(9)]
    w_dtf = jnp.concatenate([w_dt, w_f, jnp.zeros((LANES - hs - ha, d), BF16)], axis=0)
    wps = rows_all[:, :r1].reshape(ds, d)
    wpa = rows_all[:, r1:r1 + r2].reshape(da, d)
    wout = rows_all[:, r1 + r2:].reshape(d, d)
    conv_w_full = jnp.transpose(small_all[:, 0:CONV_K, :], (1, 0, 2)).reshape(CONV_K, cd)
    meta_full = jnp.transpose(small_all[:, 8:8 + N_META, :msh], (1, 0, 2)).reshape(N_META, d)
    head = jnp.concatenate([jnp.zeros((PADN, d), F32), meta_full], axis=0)

    u = _prenorm_fwd(head, x2, norm_pre)
    tm = _att_block(p)
    seg_w = [w_z, w_xbc, w_zatt, w_q, w_k, w_v, w_g]
    zs, xbc, zatt, q, k, v, graw = [
        _mm(u, w, "nt", BF16, tm, _tile(w.shape[0], (1024, 512, 256, 128)), "inproj_%d" % i) for i, w in enumerate(seg_w)]
    dtf = _mm(u, w_dtf, "nt", F32, tm, LANES, "inproj_dtf")

    brow = jnp.concatenate([dt_bias, fgate_bias, jnp.zeros((1, LANES - hs - ha), F32)], axis=1)
    alog_row = _pad_cols(a_log, LANES)
    dskip_l = jnp.repeat(d_skip, HEAD_DIM, axis=1)
    sel_t = (lax.broadcasted_iota(jnp.int32, (LANES, ds), 1) // HEAD_DIM
             == lax.broadcasted_iota(jnp.int32, (LANES, ds), 0)).astype(BF16)
    sel = sel_t.T
    y, yssd, hin, cf = _ssd_fwd(xbc, zs, dtf, conv_w_full, conv_b, brow, alog_row, dskip_l, ssd_norm, sel_t, hs, ha)

    blk = _att_block(p)
    nkb, npair = p // blk, ha // 2
    cum = jnp.where(lax.broadcasted_iota(jnp.int32, (p, 1), 0) < PADN, -NEG, cf[:, hs:hs + ha])
    ck = jnp.transpose(cum.T.reshape(npair, 2, nkb, blk), (0, 2, 1, 3))
    ck = jnp.pad(ck, ((0, 0), (0, 0), (0, 6), (0, 0)))
    o, lse_rep = _attn_fwd(q, k, v, ck, blk)

    (dyssd, d_o, dzatt, dgraw, dzo, mrg, da_, db_, yatt, dout, red_tail) = _tail(
        yssd, o, zatt, graw, head, x2, tgt2, wps, wpa, wout, gate_bias, norm_post)

    tw = _tile(d, (512, 256, 128))
    g_wout = _mm(mrg, dzo, "tn", BF16, tw, tw, "wgrad_out")
    g_wps = _mm(yssd, da_, "tn", BF16, _tile(ds, (512, 256, 128)), tw, "wgrad_ps")
    g_wpa = _mm(yatt, db_, "tn", BF16, _tile(da, (512, 256, 128)), tw, "wgrad_pa")

    dk, dv, dq, dcs, rsum = _attn_bwd(q, k, v, o, d_o, lse_rep, ck, blk)
    dcum = jnp.transpose((rsum - dcs)[:, :, 0:2], (1, 0, 2)).reshape(p, ha)
    dcf = jnp.pad(dcum, ((0, 0), (hs, LANES - hs - ha)))
    dxbc, dzs, ddtf, gcw, gcb, gnrm, gsm = _ssd_bwd(
        dyssd, y, zs, xbc, dtf, hin, dcf, conv_w_full, conv_b, brow, alog_row, dskip_l, ssd_norm, sel_t, sel, hs, ha)
    ddtf_b = ddtf.astype(BF16)

    dsegs = [dzs, dxbc, dzatt, dq, dk, dv, dgraw, ddtf_b]
    du = _mm_sum_nn(dsegs, seg_w + [w_dtf], tm, _tile(d, (256, 128)), "dgrad_in")
    gx, ghead, gnp = _prenorm_bwd(head, x2, norm_pre, du, dout)

    gsegs = [_mm(dsg, u, "tn", BF16, _tile(dsg.shape[1], (512, 256, 128)), tw, "wgrad_in_%d" % i)
             for i, dsg in enumerate(dsegs)]
    g_z, g_xbc, g_zatt, g_q, g_k, g_v, g_g, g_dtf = gsegs
    gw_full = jnp.concatenate([g_z, g_xbc, g_dtf[:hs], g_zatt, g_q, g_k, g_v, g_dtf[hs:hs + ha], g_g], axis=0)
    gwin_parts = gw_full.reshape(N_DEV, nc8, d)
    grows_parts = jnp.concatenate([g_wps.reshape(N_DEV, r1, d), g_wpa.reshape(N_DEV, r2, d),
                                   g_wout.reshape(N_DEV, r3, d)], axis=1)

    core = lax.axis_index("c").astype(jnp.int32).reshape(1)
    sib_win, sib_rows = _exchange_sibling([gwin_parts, grows_parts], "scatter_grads_sibling")
    chip_win = _pair_add(gwin_parts, sib_win, core, "pair_add_w_in")
    chip_rows = _pair_add(grows_parts, sib_rows, core, "pair_add_rows")
    recv_win, recv_rows = _exchange_chips([chip_win, chip_rows], "scatter_grads_chips")
    gmisc = jnp.concatenate([gsm[0:1], gsm[1:2], gsm[2:3], _pad_cols(red_tail[2:3, 0:1], LANES)], axis=1)
    small_g = jnp.concatenate([
        _pack_small_rep(gnp[0:1], red_tail[0:1, :d], red_tail[1:2], gnrm[0:1], gcb[0:1], gmisc, cd),
        _pad_cols(gcw[0:CONV_K], cd), jnp.zeros((4, cd), F32), _pad_cols(ghead[PADN:], cd)], axis=0)
    red = _all_reduce_small(small_g, "reduce_small")

    loss = red[5, 3 * LANES]
    g_small_sh = _pack_small_shard(lax.dynamic_slice_in_dim(red[8:8 + CONV_K], me * cws, cws, axis=1),
                                   lax.dynamic_slice_in_dim(red[16:16 + N_META, :d], me * msh, msh, axis=1), cws)

    zero1 = jnp.zeros((1, 1), F32)
    upd_in = _adamw(jnp.transpose(w_in[0]), recv_win, jnp.transpose(m_w_in[0]), jnp.transpose(v_w_in[0]),
                    "adamw_w_in", parts=True)
    cat3 = lambda a, b, c: jnp.concatenate([a[0], b[0], c[0]], axis=0)
    upd_rows = _adamw(cat3(w_proj_ssd, w_proj_att, w_out), recv_rows, cat3(m_w_proj_ssd, m_w_proj_att, m_w_out),
                      cat3(v_w_proj_ssd, v_w_proj_att, v_w_out), "adamw_rows", parts=True)
    rep = lambda a, b, c, e, f, g1, g2, g3, g4: _pack_small_rep(a, b, c, e, f, _misc_row(g1, g2, g3, g4, zero1), cd)
    upd_rep = _adamw(rep(norm_pre, norm_post, gate_bias, ssd_norm, conv_b, dt_bias, fgate_bias, a_log, d_skip),
                     red[0:8],
                     rep(m_norm_pre, m_norm_post, m_gate_bias, m_ssd_norm, m_conv_b, m_dt_bias, m_fgate_bias, m_a_log, m_d_skip),
                     rep(v_norm_pre, v_norm_post, v_gate_bias, v_ssd_norm, v_conv_b, v_dt_bias, v_fgate_bias, v_a_log, v_d_skip),
                     "adamw_rep")
    upd_sh = _adamw(small_sh, g_small_sh, _pack_small_shard(m_conv_w[0], m_meta_tokens, cws),
                    _pack_small_shard(v_conv_w[0], v_meta_tokens, cws), "adamw_small_shard")

    def leaves(i):
        a_in, a_rows, a_rep, a_sh = upd_in[i], upd_rows[i], upd_rep[i], upd_sh[i]
        misc = a_rep[5:6]
        return [a_sh[8:8 + N_META, :msh], a_rep[0:1, :d], jnp.transpose(a_in)[None], a_sh[0:CONV_K][None], a_rep[4:5, :cd],
                misc[:, :hs], misc[:, LANES:LANES + hs], misc[:, 2 * LANES:2 * LANES + hs], a_rep[3:4, :ds],
                misc[:, hs:hs + ha], a_rep[2:3, :2 * d], a_rows[:r1][None], a_rows[r1:r1 + r2][None],
                a_rows[r1 + r2:][None], a_rep[1:2, :d]]

    return tuple([loss, gx[None]] + leaves(0) + leaves(1) + leaves(2) + leaves(3))
```

```python
import functools
import math

import jax
import jax.numpy as jnp
from jax import lax
from jax.experimental import pallas as pl
from jax.experimental.pallas import tpu as pltpu

F32 = jnp.float32
BF16 = jnp.bfloat16

N_DEV = 8
N_META = 16
CHUNK = 128
PADN = CHUNK - N_META
HEAD_DIM = 64
SSD_GROUPS = 4
CONV_K = 4
EPS = 1e-6
NEG = -1e30
LANES = 128
HALO = 16

ADAM_LR = 0.001
ADAM_B1 = 0.9
ADAM_B2 = 0.999
ADAM_EPS = 1e-08
ADAM_WD = 0.01
ADAM_STEP = 10

VMEM_LIMIT = 56 * 1024 * 1024

NN = (((1,), (0,)), ((), ()))
NT = (((1,), (1,)), ((), ()))
TN = (((0,), (0,)), ((), ()))
MESH = pl.DeviceIdType.MESH


def _dot(a, b, dims=NN):
    return lax.dot_general(a, b, dims, preferred_element_type=F32)


def _split2(x):
    hi = x.astype(BF16)
    lo = (x - hi.astype(F32)).astype(BF16)
    return hi, lo


def _dot_sel(x, sel):
    hi, lo = _split2(x)
    return _dot(hi, sel) + _dot(lo, sel)


def _dot_tri(tri, x):
    h1 = x.astype(BF16)
    r1 = x - h1.astype(F32)
    h2 = r1.astype(BF16)
    h3 = (r1 - h2.astype(F32)).astype(BF16)
    return _dot(tri, h1) + _dot(tri, h2) + _dot(tri, h3)


def _sigmoid(x):
    return 1.0 / (1.0 + jnp.exp(-x))


def _softplus(x):
    return jnp.maximum(x, 0.0) + jnp.log(1.0 + jnp.exp(-jnp.abs(x)))


def _cparams(sem=None, vmem=VMEM_LIMIT):
    kw = {"vmem_limit_bytes": vmem}
    if sem is not None:
        kw["dimension_semantics"] = sem
    return pltpu.CompilerParams(**kw)


def _full(shape):
    nd = len(shape)
    return pl.BlockSpec(shape, lambda *_: (0,) * nd)


def _att_block(p):
    return 384 if p % 384 == 0 else CHUNK


def _my_pos():
    return lax.axis_index("x"), lax.axis_index("y"), lax.axis_index("c")


def _dev_index(x, y, c):
    return 4 * x + 2 * y + c


FLIPS = [(fx, fy, fc) for fx in (0, 1) for fy in (0, 1) for fc in (0, 1)][1:]


def _flip(pos, f):
    return tuple((1 - p) if fi else p for p, fi in zip(pos, f))


def _all_gather(bufs, name):
    nb = len(bufs)

    def body(*refs):
        ins, outs = refs[:nb], refs[nb:2 * nb]
        send_sems, recv_sems, local_sems = refs[2 * nb:]
        x, y, c = _my_pos()
        me = _dev_index(x, y, c)
        sibling = (x, y, 1 - c)
        chips = [(1 - x, y), (x, 1 - y), (1 - x, 1 - y)]

        def copy(b, k, block_idx, to, src=None):
            dst = outs[b].at[block_idx]
            return pltpu.make_async_remote_copy(
                src_ref=dst if src is None else src, dst_ref=dst,
                send_sem=send_sems.at[b, k], recv_sem=recv_sems.at[b, k],
                device_id=to, device_id_type=MESH)

        started = []
        for b in range(nb):
            mine = pltpu.make_async_copy(ins[b], outs[b].at[me], local_sems.at[b])
            mine.start()
            started.append(mine)
        first = []
        for b in range(nb):
            first.append(copy(b, 0, me, sibling, src=ins[b]))
            for j, chip in enumerate(chips):
                first.append(copy(b, 1 + j, me, (chip[0], chip[1], c), src=ins[b]))
        for cp in first:
            cp.start()
        passed = []
        for j, chip in enumerate(chips):
            blk = _dev_index(chip[0], chip[1], c)
            for b in range(nb):
                copy(b, 1 + j, blk, (x, y, c)).wait_recv()
                fwd = copy(b, 4 + j, blk, sibling)
                fwd.start()
                passed.append(fwd)
        for b in range(nb):
            copy(b, 0, _dev_index(x, y, 1 - c), (x, y, c)).wait_recv()
        for j, chip in enumerate(chips):
            blk = _dev_index(chip[0], chip[1], 1 - c)
            for b in range(nb):
                copy(b, 4 + j, blk, (x, y, c)).wait_recv()
        for cp in first + passed:
            cp.wait_send()
        for mine in started:
            mine.wait()

    any_spec = pl.BlockSpec(memory_space=pl.ANY)
    return pl.pallas_call(
        body, name=name,
        out_shape=[jax.ShapeDtypeStruct((N_DEV,) + b.shape, b.dtype) for b in bufs],
        in_specs=[any_spec] * nb, out_specs=[any_spec] * nb,
        scratch_shapes=[pltpu.SemaphoreType.DMA((nb, 7)), pltpu.SemaphoreType.DMA((nb, 7)),
                        pltpu.SemaphoreType.DMA((nb,))],
    )(*bufs)


N_CHIP = 4
CHIP_FLIPS = [(1, 0), (0, 1), (1, 1)]


def _exchange_sibling(bufs, name):
    nb = len(bufs)

    def body(*refs):
        ins, outs = refs[:nb], refs[nb:2 * nb]
        send_sems, recv_sems = refs[2 * nb:]
        x, y, c = _my_pos()

        def copy(b, k):
            return pltpu.make_async_remote_copy(
                src_ref=ins[b].at[2 * k + (1 - c)], dst_ref=outs[b].at[k],
                send_sem=send_sems.at[b, k], recv_sem=recv_sems.at[b, k],
                device_id=(x, y, 1 - c), device_id_type=MESH)

        cps = [copy(b, k) for b in range(nb) for k in range(N_CHIP)]
        for cp in cps:
            cp.start()
        for cp in cps:
            cp.wait()

    any_spec = pl.BlockSpec(memory_space=pl.ANY)
    return pl.pallas_call(
        body, name=name,
        out_shape=[jax.ShapeDtypeStruct((N_CHIP,) + b.shape[1:], b.dtype) for b in bufs],
        in_specs=[any_spec] * nb, out_specs=[any_spec] * nb,
        scratch_shapes=[pltpu.SemaphoreType.DMA((nb, N_CHIP)), pltpu.SemaphoreType.DMA((nb, N_CHIP))],
    )(*bufs)


def _pair_add(mine, recv, core, name):
    _, r, cdim = mine.shape
    tr, tc, by_rows = _tiles_2d(r, cdim)
    pick = (lambda i: (i, 0)) if by_rows else (lambda i: (0, i))

    def body(core_ref, a_ref, b_ref, o_ref):
        o_ref[0] = (a_ref[0, 0].astype(F32) + b_ref[0].astype(F32)).astype(o_ref.dtype)

    return pl.pallas_call(
        body, name=name,
        grid_spec=pltpu.PrefetchScalarGridSpec(
            num_scalar_prefetch=1, grid=(N_CHIP, (r // tr) * (cdim // tc)),
            in_specs=[pl.BlockSpec((1, 1, tr, tc), lambda k, i, core_ref: (k, core_ref[0]) + pick(i)),
                      pl.BlockSpec((1, tr, tc), lambda k, i, core_ref: (k,) + pick(i))],
            out_specs=pl.BlockSpec((1, tr, tc), lambda k, i, core_ref: (k,) + pick(i))),
        out_shape=jax.ShapeDtypeStruct((N_CHIP, r, cdim), mine.dtype),
        compiler_params=_cparams(("parallel", "parallel")),
    )(core, mine.reshape(N_CHIP, 2, r, cdim), recv)


def _exchange_chips(bufs, name):
    nb = len(bufs)

    def body(*refs):
        ins, outs = refs[:nb], refs[nb:2 * nb]
        send_sems, recv_sems, local_sems = refs[2 * nb:]
        x, y, c = _my_pos()
        mine = 2 * x + y

        def copy(b, j, f):
            px, py = (1 - x) if f[0] else x, (1 - y) if f[1] else y
            return pltpu.make_async_remote_copy(
                src_ref=ins[b].at[2 * px + py], dst_ref=outs[b].at[mine],
                send_sem=send_sems.at[b, j], recv_sem=recv_sems.at[b, j],
                device_id=(px, py, c), device_id_type=MESH)

        local = [pltpu.make_async_copy(ins[b].at[mine], outs[b].at[mine], local_sems.at[b]) for b in range(nb)]
        for cp in local:
            cp.start()
        sends = [copy(b, j, f) for b in range(nb) for j, f in enumerate(CHIP_FLIPS)]
        for cp in sends:
            cp.start()
        for b in range(nb):
            for j, f in enumerate(CHIP_FLIPS):
                px, py = (1 - x) if f[0] else x, (1 - y) if f[1] else y
                pltpu.make_async_remote_copy(
                    src_ref=ins[b].at[mine], dst_ref=outs[b].at[2 * px + py],
                    send_sem=send_sems.at[b, j], recv_sem=recv_sems.at[b, j],
                    device_id=(px, py, c), device_id_type=MESH).wait_recv()
        for cp in sends:
            cp.wait_send()
        for cp in local:
            cp.wait()

    any_spec = pl.BlockSpec(memory_space=pl.ANY)
    return pl.pallas_call(
        body, name=name,
        out_shape=[jax.ShapeDtypeStruct(b.shape, b.dtype) for b in bufs],
        in_specs=[any_spec] * nb, out_specs=[any_spec] * nb,
        scratch_shapes=[pltpu.SemaphoreType.DMA((nb, 3)), pltpu.SemaphoreType.DMA((nb, 3)),
                        pltpu.SemaphoreType.DMA((nb,))],
    )(*bufs)


def _chip_peer(x, y, f):
    return ((1 - x) if f[0] else x), ((1 - y) if f[1] else y)


def _exchange_chips_start(bufs, name):
    nb = len(bufs)
    nsem = 2 * 3 * nb

    def body(*refs):
        ins, lands = refs[:nb], refs[nb:2 * nb]
        sems = refs[2 * nb:2 * nb + nsem]
        token = refs[-1]
        x, y, c = _my_pos()
        for b in range(nb):
            for j, f in enumerate(CHIP_FLIPS):
                px, py = _chip_peer(x, y, f)
                pltpu.make_async_remote_copy(
                    src_ref=ins[b].at[2 * px + py], dst_ref=lands[b].at[2 * x + y],
                    send_sem=sems[2 * (3 * b + j)], recv_sem=sems[2 * (3 * b + j) + 1],
                    device_id=(px, py, c), device_id_type=MESH).start()
        token[...] = jnp.zeros_like(token)

    hbm = pl.BlockSpec(memory_space=pltpu.HBM)
    sem = pl.BlockSpec(memory_space=pltpu.SEMAPHORE)
    out = pl.pallas_call(
        body, name=name,
        out_shape=(*([pltpu.SemaphoreType.DMA(())] * nsem),
                   *[pltpu.HBM(b.shape, b.dtype) for b in bufs], *[pltpu.HBM(b.shape, b.dtype) for b in bufs],
                   jax.ShapeDtypeStruct((8, LANES), F32)),
        in_specs=[hbm] * (2 * nb),
        out_specs=(*([sem] * nsem), *([hbm] * (2 * nb)), pl.BlockSpec(memory_space=pltpu.VMEM)),
        input_output_aliases={i: nsem + i for i in range(2 * nb)},
        compiler_params=pltpu.CompilerParams(has_side_effects=pltpu.SideEffectType.DATAFLOW_SIDE_EFFECTING),
    )(*[pltpu.with_memory_space_constraint(b, pltpu.HBM) for b in bufs],
      *[pltpu.with_memory_space_constraint(lax.empty(b.shape, b.dtype), pltpu.HBM) for b in bufs])
    return out[:nsem], out[nsem:nsem + nb], out[nsem + nb:nsem + 2 * nb], out[-1]


def _exchange_chips_wait(sems, thru, lands, after, name):
    nb = len(thru)
    nsem = len(sems)

    def body(*refs):
        ins, lnd = refs[:nb], refs[nb:2 * nb]
        sem_refs = refs[2 * nb:2 * nb + nsem]
        x, y, c = _my_pos()
        for b in range(nb):
            for j, f in enumerate(CHIP_FLIPS):
                px, py = _chip_peer(x, y, f)
                cp = pltpu.make_async_remote_copy(
                    src_ref=ins[b].at[2 * px + py], dst_ref=lnd[b].at[2 * px + py],
                    send_sem=sem_refs[2 * (3 * b + j)], recv_sem=sem_refs[2 * (3 * b + j) + 1],
                    device_id=(px, py, c), device_id_type=MESH)
                cp.wait_send()
                cp.wait_recv()

    hbm = pl.BlockSpec(memory_space=pltpu.HBM)
    sem = pl.BlockSpec(memory_space=pltpu.SEMAPHORE)
    out = pl.pallas_call(
        body, name=name,
        out_shape=tuple([pltpu.HBM(b.shape, b.dtype) for b in thru] + [pltpu.HBM(b.shape, b.dtype) for b in lands]),
        in_specs=[hbm] * (2 * nb) + [sem] * nsem + [pl.BlockSpec(memory_space=pl.ANY)],
        out_specs=tuple([hbm] * (2 * nb)),
        input_output_aliases={i: i for i in range(2 * nb)},
        compiler_params=pltpu.CompilerParams(has_side_effects=pltpu.SideEffectType.DATAFLOW_SIDE_EFFECTING),
    )(*thru, *lands, *sems, after)
    return out[:nb], out[nb:]


def _all_reduce_small(v, name):
    r, cdim = v.shape

    def body(x_ref, out_ref, slots, send_sems, recv_sems):
        pos = _my_pos()
        me = _dev_index(*pos)
        slots[me] = x_ref[...]
        sends = []
        for k, f in enumerate(FLIPS):
            cp = pltpu.make_async_remote_copy(
                src_ref=x_ref, dst_ref=slots.at[me], send_sem=send_sems.at[k], recv_sem=recv_sems.at[k],
                device_id=_flip(pos, f), device_id_type=MESH)
            cp.start()
            sends.append(cp)
        for k, f in enumerate(FLIPS):
            peer = _flip(pos, f)
            pltpu.make_async_remote_copy(
                src_ref=x_ref, dst_ref=slots.at[_dev_index(*peer)], send_sem=send_sems.at[k],
                recv_sem=recv_sems.at[k], device_id=peer, device_id_type=MESH).wait_recv()
        for cp in sends:
            cp.wait_send()
        acc = slots[0]
        for s in range(1, N_DEV):
            acc = acc + slots[s]
        out_ref[...] = acc

    vm = pl.BlockSpec(memory_space=pltpu.VMEM)
    return pl.pallas_call(
        body, name=name, out_shape=jax.ShapeDtypeStruct(v.shape, F32),
        in_specs=[vm], out_specs=vm,
        scratch_shapes=[pltpu.VMEM((N_DEV, r, cdim), F32), pltpu.SemaphoreType.DMA((7,)),
                        pltpu.SemaphoreType.DMA((7,))],
    )(v)


def _mm(a, b, dims, out_dtype, tm, tn, name):
    if dims == "nn":
        (m, k), (_, n) = a.shape, b.shape
        a_spec = pl.BlockSpec((tm, k), lambda j, i: (i, 0))
        b_spec = pl.BlockSpec((k, tn), lambda j, i: (0, j))
        dn = NN
    elif dims == "nt":
        (m, k), (n, _) = a.shape, b.shape
        a_spec = pl.BlockSpec((tm, k), lambda j, i: (i, 0))
        b_spec = pl.BlockSpec((tn, k), lambda j, i: (j, 0))
        dn = NT
    else:
        (k, m), (_, n) = a.shape, b.shape
        a_spec = pl.BlockSpec((k, tm), lambda j, i: (0, i))
        b_spec = pl.BlockSpec((k, tn), lambda j, i: (0, j))
        dn = TN
    assert m % tm == 0 and n % tn == 0, (m, tm, n, tn)

    def body(a_ref, b_ref, o_ref):
        o_ref[...] = _dot(a_ref[...], b_ref[...], dn).astype(o_ref.dtype)

    return pl.pallas_call(
        body, name=name, grid=(n // tn, m // tm),
        in_specs=[a_spec, b_spec], out_specs=pl.BlockSpec((tm, tn), lambda j, i: (i, j)),
        out_shape=jax.ShapeDtypeStruct((m, n), out_dtype),
        compiler_params=_cparams(("parallel", "parallel")),
    )(a, b)


def _tiles_2d(r, cdim):
    if r % CHUNK == 0:
        return CHUNK, cdim, True
    return r, _tile(cdim, (256, 128)), False


def _mm_sum_nn(a_list, b_list, tm, tn, name):
    n_op = len(a_list)
    m, n = a_list[0].shape[0], b_list[0].shape[1]

    def body(*refs):
        acc = _dot(refs[0][...], refs[n_op][...])
        for i in range(1, n_op):
            acc = acc + _dot(refs[i][...], refs[n_op + i][...])
        refs[2 * n_op][...] = acc

    return pl.pallas_call(
        body, name=name, grid=(n // tn, m // tm),
        in_specs=([pl.BlockSpec((tm, a.shape[1]), lambda j, i: (i, 0)) for a in a_list]
                  + [pl.BlockSpec((b.shape[0], tn), lambda j, i: (0, j)) for b in b_list]),
        out_specs=pl.BlockSpec((tm, tn), lambda j, i: (i, j)),
        out_shape=jax.ShapeDtypeStruct((m, n), F32),
        compiler_params=_cparams(("parallel", "parallel")),
    )(*a_list, *b_list)


def _tile(n, prefs):
    for t in prefs:
        if n % t == 0:
            return t
    return n


def _prenorm_fwd(head, x2, w):
    p, d = x2.shape[0] + CHUNK, x2.shape[1]

    def body(head_ref, x_ref, w_ref, u_ref):
        i = pl.program_id(0)
        h = jnp.where(i == 0, head_ref[...], x_ref[...])
        ms = jnp.mean(h * h, axis=-1, keepdims=True)
        u_ref[...] = (h * lax.rsqrt(ms + EPS) * w_ref[...]).astype(BF16)

    return pl.pallas_call(
        body, name="prenorm_fwd", grid=(p // CHUNK,),
        in_specs=[_full((CHUNK, d)), pl.BlockSpec((CHUNK, d), lambda i: (jnp.maximum(i - 1, 0), 0)), _full((1, d))],
        out_specs=pl.BlockSpec((CHUNK, d), lambda i: (i, 0)),
        out_shape=jax.ShapeDtypeStruct((p, d), BF16),
        compiler_params=_cparams(("arbitrary",)),
    )(head, x2, w)


def _prenorm_bwd(head, x2, w, du, dout):
    p, d = x2.shape[0] + CHUNK, x2.shape[1]

    def body(head_ref, x_ref, w_ref, du_ref, dout_ref, gx_ref, ghead_ref, gw_ref):
        i = pl.program_id(0)
        h = jnp.where(i == 0, head_ref[...], x_ref[...])
        rstd = lax.rsqrt(jnp.mean(h * h, axis=-1, keepdims=True) + EPS)
        xhat = h * rstd
        dub = du_ref[...]
        dxh = dub * w_ref[...]
        dh = rstd * (dxh - xhat * jnp.mean(dxh * xhat, axis=-1, keepdims=True)) + dout_ref[...]

        @pl.when(i == 0)
        def _():
            ghead_ref[...] = dh
            gw_ref[...] = jnp.zeros_like(gw_ref)

        gx_ref[...] = dh
        gw_ref[0:1, :] += jnp.sum(dub * xhat, axis=0, keepdims=True)

    return pl.pallas_call(
        body, name="prenorm_bwd", grid=(p // CHUNK,),
        in_specs=[_full((CHUNK, d)), pl.BlockSpec((CHUNK, d), lambda i: (jnp.maximum(i - 1, 0), 0)), _full((1, d)),
                  pl.BlockSpec((CHUNK, d), lambda i: (i, 0)), pl.BlockSpec((CHUNK, d), lambda i: (i, 0))],
        out_specs=[pl.BlockSpec((CHUNK, d), lambda i: (jnp.maximum(i - 1, 0), 0)), _full((CHUNK, d)), _full((8, d))],
        out_shape=[jax.ShapeDtypeStruct(x2.shape, F32), jax.ShapeDtypeStruct((CHUNK, d), F32),
                   jax.ShapeDtypeStruct((8, d), F32)],
        compiler_params=_cparams(("arbitrary",)),
    )(head, x2, w, du, dout)


def _conv_pre(xr, halo128, cw_ref, cb_ref, rows):
    pre = cb_ref[...] + cw_ref[CONV_K - 1:CONV_K, :] * xr
    shifted = []
    for j in range(1, CONV_K):
        sh = jnp.where(rows >= j, pltpu.roll(xr, j, 0), pltpu.roll(halo128, j, 0))
        shifted.append(sh)
        pre = pre + cw_ref[CONV_K - 1 - j:CONV_K - j, :] * sh
    return pre, shifted


def _ssd_scalars(dtf_ref, brow_ref, alog_ref, rowmask, hs, ha, tri):
    lane = lax.broadcasted_iota(jnp.int32, (1, LANES), 1)
    is_dt = lane < hs
    is_f = (lane >= hs) & (lane < hs + ha)
    dtr = dtf_ref[...] + brow_ref[...]
    sp = _softplus(dtr)
    dt = jnp.where(is_dt, sp, 0.0) * rowmask
    logf = jnp.where(is_f, jnp.minimum(dtr, 0.0) - jnp.log(1.0 + jnp.exp(-jnp.abs(dtr))), 0.0) * rowmask
    a_row = jnp.where(is_dt, -jnp.exp(alog_ref[...]), 0.0)
    run = _dot_tri(tri, dt * a_row + logf)
    return dtr, dt, a_row, run, is_dt, is_f


def _tri_mats():
    r = lax.broadcasted_iota(jnp.int32, (CHUNK, CHUNK), 0)
    c = lax.broadcasted_iota(jnp.int32, (CHUNK, CHUNK), 1)
    return r, c


def _ssd_fwd(xbc, z, dtf, conv_w, conv_b, brow, alog, dskip_l, ssd_norm, sel_t, hs, ha):
    p, cd = xbc.shape
    ds = z.shape[1]
    ns = (cd - ds) // (2 * SSD_GROUPS)
    gw = ds // SSD_GROUPS
    nch = p // CHUNK
    hpg = hs // SSD_GROUPS

    def body(xbc_ref, halo_ref, z_ref, dtf_ref, cw_ref, cb_ref, brow_ref, alog_ref, dsk_ref, nrm_ref, selt_ref,
             y_ref, yssd_ref, hin_ref, cf_ref, st_ref, carry_ref, yacc_ref):
        c = pl.program_id(0)

        @pl.when(c == 0)
        def _():
            st_ref[...] = jnp.zeros_like(st_ref)
            carry_ref[...] = jnp.zeros_like(carry_ref)

        rows = lax.broadcasted_iota(jnp.int32, (CHUNK, 1), 0)
        rowmask = jnp.where((rows >= PADN) | (c > 0), 1.0, 0.0)
        ri, ci = _tri_mats()
        causal = ri >= ci
        tri = jnp.where(causal, 1.0, 0.0).astype(BF16)

        xr = xbc_ref[...].astype(F32)
        halo = halo_ref[...].astype(F32) * jnp.where(c > 0, 1.0, 0.0)
        halo128 = jnp.concatenate([jnp.zeros((CHUNK - HALO, cd), F32), halo], axis=0)
        pre, _ = _conv_pre(xr, halo128, cw_ref, cb_ref, rows)
        xc = pre * _sigmoid(pre) * rowmask

        dtr, dt, a_row, run, is_dt, is_f = _ssd_scalars(dtf_ref, brow_ref, alog_ref, rowmask, hs, ha, tri)
        cf = run + carry_ref[...]
        cf_ref[...] = cf
        carry_ref[...] = jnp.where(is_f, cf[CHUNK - 1:CHUNK, :], 0.0)
        cs = jnp.where(is_dt, run, 0.0)
        cl = cs[CHUNK - 1:CHUNK, :]
        selt = selt_ref[...]
        dt_x = _dot_sel(dt, selt)
        e_x = _dot_sel(jnp.exp(cs), selt)
        w_x = _dot_sel(jnp.exp(cl - cs), selt)
        cdec_x = _dot_sel(jnp.broadcast_to(jnp.exp(cl), (8, LANES)), selt)[0:1, :]
        cs_t = cs.T

        xs = xc[:, :ds]
        xdt = xs * dt_x
        xdt_b = xdt.astype(BF16)
        xw_b = (xdt * w_x).astype(BF16)
        lane = lax.broadcasted_iota(jnp.int32, (1, LANES), 1)
        half0 = lane < HEAD_DIM
        for g in range(SSD_GROUPS):
            bg = xc[:, ds + g * ns: ds + (g + 1) * ns].astype(BF16)
            cg = xc[:, ds + SSD_GROUPS * ns + g * ns: ds + SSD_GROUPS * ns + (g + 1) * ns].astype(BF16)
            gm = _dot(cg, bg, NT)
            gs = slice(g * gw, (g + 1) * gw)
            stg = st_ref[:, gs]
            stg_b = stg.astype(BF16)
            hin_ref[0, :, gs] = stg_b
            yoff = _dot(cg, stg_b) * e_x[:, gs]
            for pr in range(gw // LANES):
                sl = slice(g * gw + pr * LANES, g * gw + (pr + 1) * LANES)
                xp = xdt_b[:, sl]
                yd = jnp.zeros((CHUNK, LANES), F32)
                for j in range(2):
                    h = g * hpg + 2 * pr + j
                    seg = cs[:, h:h + 1] - cs_t[h:h + 1, :]
                    m = jnp.where(causal, gm * jnp.exp(jnp.minimum(seg, 0.0)), 0.0).astype(BF16)
                    sel = half0 if j == 0 else jnp.logical_not(half0)
                    yd = yd + _dot(m, jnp.where(sel, xp, jnp.zeros_like(xp)))
                yacc_ref[:, sl] = yd + yoff[:, pr * LANES:(pr + 1) * LANES] + dsk_ref[:, sl] * xs[:, sl]
            st_ref[:, gs] = stg * cdec_x[:, gs] + _dot(bg, xw_b[:, gs], TN)

        y = yacc_ref[...]
        y_ref[...] = y.astype(BF16)
        zf = z_ref[...].astype(F32)
        u = y * zf * _sigmoid(zf)
        for g in range(SSD_GROUPS):
            gs = slice(g * gw, (g + 1) * gw)
            ug = u[:, gs]
            ms = jnp.mean(ug * ug, axis=-1, keepdims=True)
            yssd_ref[:, gs] = (ug * lax.rsqrt(ms + EPS) * nrm_ref[:, gs]).astype(BF16)

    rb = CHUNK // HALO
    return pl.pallas_call(
        body, name="ssd_fwd", grid=(nch,),
        in_specs=[pl.BlockSpec((CHUNK, cd), lambda c: (c, 0)),
                  pl.BlockSpec((HALO, cd), lambda c: (jnp.maximum(c * rb - 1, 0), 0)),
                  pl.BlockSpec((CHUNK, ds), lambda c: (c, 0)),
                  pl.BlockSpec((CHUNK, LANES), lambda c: (c, 0)),
                  _full((CONV_K, cd)), _full((1, cd)), _full((1, LANES)), _full((1, LANES)),
                  _full((1, ds)), _full((1, ds)), _full((LANES, ds))],
        out_specs=[pl.BlockSpec((CHUNK, ds), lambda c: (c, 0)), pl.BlockSpec((CHUNK, ds), lambda c: (c, 0)),
                   pl.BlockSpec((1, ns, ds), lambda c: (c, 0, 0)), pl.BlockSpec((CHUNK, LANES), lambda c: (c, 0))],
        out_shape=[jax.ShapeDtypeStruct((p, ds), BF16), jax.ShapeDtypeStruct((p, ds), BF16),
                   jax.ShapeDtypeStruct((nch, ns, ds), BF16), jax.ShapeDtypeStruct((p, LANES), F32)],
        scratch_shapes=[pltpu.VMEM((ns, ds), F32), pltpu.VMEM((1, LANES), F32), pltpu.VMEM((CHUNK, ds), F32)],
        compiler_params=_cparams(("arbitrary",)),
    )(xbc, xbc, z, dtf, conv_w, conv_b, brow, alog, dskip_l, ssd_norm, sel_t)


def _ssd_bwd(dyssd, y, z, xbc, dtf, hin, dcf, conv_w, conv_b, brow, alog, dskip_l, ssd_norm, sel_t, sel, hs, ha):
    p, cd = xbc.shape
    ds = z.shape[1]
    ns = (cd - ds) // (2 * SSD_GROUPS)
    gw = ds // SSD_GROUPS
    nch = p // CHUNK
    hpg = hs // SSD_GROUPS
    rb = CHUNK // HALO

    def body(dyssd_ref, y_ref, z_ref, xbc_ref, halo_ref, dtf_ref, hin_ref, dcf_ref, cw_ref, cb_ref, brow_ref,
             alog_ref, dsk_ref, nrm_ref, selt_ref, sel_ref,
             dxbc_ref, dz_ref, ddtf_ref, gcw_ref, gcb_ref, gnrm_ref, gsm_ref,
             dst_ref, nxt_ref, fcar_ref, gdsk_ref, dxc_ref):
        step = pl.program_id(0)
        c = nch - 1 - step

        @pl.when(step == 0)
        def _():
            dst_ref[...] = jnp.zeros_like(dst_ref)
            nxt_ref[...] = jnp.zeros_like(nxt_ref)
            fcar_ref[...] = jnp.zeros_like(fcar_ref)
            gdsk_ref[...] = jnp.zeros_like(gdsk_ref)
            gcw_ref[...] = jnp.zeros_like(gcw_ref)
            gcb_ref[...] = jnp.zeros_like(gcb_ref)
            gnrm_ref[...] = jnp.zeros_like(gnrm_ref)
            gsm_ref[...] = jnp.zeros_like(gsm_ref)

        rows = lax.broadcasted_iota(jnp.int32, (CHUNK, 1), 0)
        rowmask = jnp.where((rows >= PADN) | (c > 0), 1.0, 0.0)
        ri, ci = _tri_mats()
        causal = ri >= ci
        anti = ci >= ri
        tri = jnp.where(causal, 1.0, 0.0).astype(BF16)
        rtri = jnp.where(anti, 1.0, 0.0).astype(BF16)

        xr = xbc_ref[...].astype(F32)
        halo = halo_ref[...].astype(F32) * jnp.where(c > 0, 1.0, 0.0)
        halo128 = jnp.concatenate([jnp.zeros((CHUNK - HALO, cd), F32), halo], axis=0)
        pre, shifted = _conv_pre(xr, halo128, cw_ref, cb_ref, rows)
        sg = _sigmoid(pre)
        xc = pre * sg * rowmask
        dsilu = sg * (1.0 + pre * (1.0 - sg)) * rowmask

        dtr, dt, a_row, run, is_dt, is_f = _ssd_scalars(dtf_ref, brow_ref, alog_ref, rowmask, hs, ha, tri)
        cs = jnp.where(is_dt, run, 0.0)
        cl = cs[CHUNK - 1:CHUNK, :]
        selt = selt_ref[...]
        selm = sel_ref[...]
        dt_x = _dot_sel(dt, selt)
        e_x = _dot_sel(jnp.exp(cs), selt)
        w_x = _dot_sel(jnp.exp(cl - cs), selt)
        cdec = jnp.exp(cl)
        cdec_x = _dot_sel(jnp.broadcast_to(cdec, (8, LANES)), selt)[0:1, :]
        cs_t = cs.T
        xs = xc[:, :ds]
        xdt = xs * dt_x
        xdt_b = xdt.astype(BF16)
        xw_b = (xdt * w_x).astype(BF16)

        yv = y_ref[...].astype(F32)
        zf = z_ref[...].astype(F32)
        sz = _sigmoid(zf)
        u = yv * zf * sz
        dyo = dyssd_ref[...].astype(F32)
        du_parts = []
        for g in range(SSD_GROUPS):
            gs = slice(g * gw, (g + 1) * gw)
            ug = u[:, gs]
            rstd = lax.rsqrt(jnp.mean(ug * ug, axis=-1, keepdims=True) + EPS)
            yhat = ug * rstd
            dyg = dyo[:, gs]
            gnrm_ref[0:1, gs] += jnp.sum(dyg * yhat, axis=0, keepdims=True)
            dyh = dyg * nrm_ref[:, gs]
            du_parts.append(rstd * (dyh - yhat * jnp.mean(dyh * yhat, axis=-1, keepdims=True)))
        du = jnp.concatenate(du_parts, axis=1)
        dy = du * zf * sz
        dz_ref[...] = (du * yv * sz * (1.0 + zf * (1.0 - sz))).astype(BF16)

        dsk = dsk_ref[...]
        gdsk_ref[...] += jnp.sum(dy * xs, axis=0, keepdims=True)
        dy_b = dy.astype(BF16)
        dye_b = (dy * e_x).astype(BF16)
        lane = lax.broadcasted_iota(jnp.int32, (1, LANES), 1)
        half0 = lane < HEAD_DIM
        x_parts, yo_parts, t4_parts = [], [], []
        dcs = jnp.zeros((CHUNK, LANES), F32)
        for g in range(SSD_GROUPS):
            gs = slice(g * gw, (g + 1) * gw)
            bsl = slice(ds + g * ns, ds + (g + 1) * ns)
            csl = slice(ds + SSD_GROUPS * ns + g * ns, ds + SSD_GROUPS * ns + (g + 1) * ns)
            bg = xc[:, bsl].astype(BF16)
            cg = xc[:, csl].astype(BF16)
            gm = _dot(cg, bg, NT)
            gm_t = _dot(bg, cg, NT)
            stg_b = hin_ref[0, :, gs]
            dstg = dst_ref[:, gs]
            dstg_b = dstg.astype(BF16)
            t4_parts.append(jnp.sum(dstg * stg_b.astype(F32), axis=0, keepdims=True))
            zst = _dot(bg, dstg_b) * w_x[:, gs]
            x_parts.append(xdt[:, gs] * zst)
            yo_parts.append(dy[:, gs] * (_dot(cg, stg_b) * e_x[:, gs]))
            dgsum = jnp.zeros((CHUNK, CHUNK), F32)
            dgtsum = jnp.zeros((CHUNK, CHUNK), F32)
            for pr in range(gw // LANES):
                sl = slice(g * gw + pr * LANES, g * gw + (pr + 1) * LANES)
                xp = xdt_b[:, sl]
                dyp = dy_b[:, sl]
                dxd = zst[:, pr * LANES:(pr + 1) * LANES]
                for j in range(2):
                    h = g * hpg + 2 * pr + j
                    sel_l = half0 if j == 0 else jnp.logical_not(half0)
                    seg = cs[:, h:h + 1] - cs_t[h:h + 1, :]
                    lm = jnp.where(causal, jnp.exp(jnp.minimum(seg, 0.0)), 0.0)
                    lmt = jnp.where(anti, jnp.exp(jnp.minimum(-seg, 0.0)), 0.0)
                    dyp_m = jnp.where(sel_l, dyp, jnp.zeros_like(dyp))
                    xp_m = jnp.where(sel_l, xp, jnp.zeros_like(xp))
                    dxd = dxd + _dot((gm_t * lmt).astype(BF16), dyp_m)
                    dg = _dot(dyp_m, xp, NT) * lm
                    dgt = _dot(xp_m, dyp, NT) * lmt
                    dgsum = dgsum + dg
                    dgtsum = dgtsum + dgt
                    qrow = (jnp.sum(dg * gm, axis=1, keepdims=True) - jnp.sum(dgt * gm_t, axis=1, keepdims=True))
                    dcs = dcs + jnp.where(lane == h, qrow, 0.0)
                dxc_ref[:, sl] = dxd
            dxc_ref[:, csl] = _dot(dgsum.astype(BF16), bg) + _dot(dye_b[:, gs], stg_b, NT)
            dxc_ref[:, bsl] = _dot(dgtsum.astype(BF16), cg) + _dot(xw_b[:, gs], dstg_b, NT)
            dst_ref[:, gs] = dstg * cdec_x[:, gs] + _dot(cg, dye_b[:, gs], TN)

        dxdt = dxc_ref[:, :ds]
        xst = _dot_sel(jnp.concatenate(x_parts, axis=1), selm)
        yo = _dot_sel(jnp.concatenate(yo_parts, axis=1), selm)
        t4 = _dot_sel(jnp.concatenate([jnp.concatenate(t4_parts, axis=1), jnp.zeros((7, ds), F32)], axis=0), selm)
        dcl = jnp.sum(xst, axis=0, keepdims=True) + cdec * t4[0:1, :]
        dcs = dcs + yo - xst + jnp.where(rows == CHUNK - 1, dcl, 0.0)
        da_ = _dot_tri(rtri, dcs)
        ddt = _dot_sel(dxdt * xs, selm) + da_ * a_row
        dcf_blk = dcf_ref[...]
        dlogf = _dot_tri(rtri, dcf_blk) + fcar_ref[...]
        fcar_ref[...] += jnp.sum(dcf_blk, axis=0, keepdims=True)
        sgd = _sigmoid(dtr)
        ddtf = (jnp.where(is_dt, ddt * sgd, 0.0) + jnp.where(is_f, dlogf * (1.0 - sgd), 0.0)) * rowmask
        ddtf_ref[...] = ddtf
        gsm_ref[0:1, :] += jnp.sum(ddtf, axis=0, keepdims=True)
        gsm_ref[1:2, :] += jnp.sum(da_ * dt, axis=0, keepdims=True) * a_row

        dxc_ref[:, :ds] = dxdt * dt_x + dsk * dy
        dpre = dxc_ref[...] * dsilu
        gcb_ref[0:1, :] += jnp.sum(dpre, axis=0, keepdims=True)
        gcw_ref[CONV_K - 1:CONV_K, :] += jnp.sum(dpre * xr, axis=0, keepdims=True)
        nxt128 = jnp.concatenate([nxt_ref[...], jnp.zeros((CHUNK - 8, cd), F32)], axis=0)
        dxr = cw_ref[CONV_K - 1:CONV_K, :] * dpre
        for j in range(1, CONV_K):
            gcw_ref[CONV_K - 1 - j:CONV_K - j, :] += jnp.sum(dpre * shifted[j - 1], axis=0, keepdims=True)
            up = jnp.where(rows < CHUNK - j, pltpu.roll(dpre, CHUNK - j, 0), pltpu.roll(nxt128, CHUNK - j, 0))
            dxr = dxr + cw_ref[CONV_K - 1 - j:CONV_K - j, :] * up
        nxt_ref[...] = dpre[0:8, :]
        dxbc_ref[...] = dxr.astype(BF16)

        @pl.when(step == nch - 1)
        def _():
            gsm_ref[2:3, :] = _dot_sel(jnp.broadcast_to(gdsk_ref[...], (8, ds)), selm)[0:1, :]

    rev = lambda s: nch - 1 - s
    blk = lambda w: pl.BlockSpec((CHUNK, w), lambda s: (rev(s), 0))
    return pl.pallas_call(
        body, name="ssd_bwd", grid=(nch,),
        in_specs=[blk(ds), blk(ds), blk(ds), blk(cd),
                  pl.BlockSpec((HALO, cd), lambda s: (jnp.maximum(rev(s) * rb - 1, 0), 0)),
                  blk(LANES), pl.BlockSpec((1, ns, ds), lambda s: (rev(s), 0, 0)), blk(LANES),
                  _full((CONV_K, cd)), _full((1, cd)), _full((1, LANES)), _full((1, LANES)),
                  _full((1, ds)), _full((1, ds)), _full((LANES, ds)), _full((ds, LANES))],
        out_specs=[blk(cd), blk(ds), blk(LANES), _full((8, cd)), _full((8, cd)), _full((8, ds)), _full((8, LANES))],
        out_shape=[jax.ShapeDtypeStruct((p, cd), BF16), jax.ShapeDtypeStruct((p, ds), BF16),
                   jax.ShapeDtypeStruct((p, LANES), F32), jax.ShapeDtypeStruct((8, cd), F32),
                   jax.ShapeDtypeStruct((8, cd), F32), jax.ShapeDtypeStruct((8, ds), F32),
                   jax.ShapeDtypeStruct((8, LANES), F32)],
        scratch_shapes=[pltpu.VMEM((ns, ds), F32), pltpu.VMEM((8, cd), F32), pltpu.VMEM((1, LANES), F32),
                        pltpu.VMEM((1, ds), F32), pltpu.VMEM((CHUNK, cd), F32)],
        compiler_params=_cparams(("arbitrary",)),
    )(dyssd, y, z, xbc, xbc, dtf, hin, dcf, conv_w, conv_b, brow, alog, dskip_l, ssd_norm, sel_t, sel)


def _attn_fwd(q, k, v, ck, blk):
    p, da = q.shape
    npair, nkb = ck.shape[0], ck.shape[1]
    scale = 1.0 / math.sqrt(HEAD_DIM)

    def body(q_ref, k_ref, v_ref, ck_ref, o_ref, lse_ref):
        i = pl.program_id(1)
        lane = lax.broadcasted_iota(jnp.int32, (1, LANES), 1)
        sels = [lane < HEAD_DIM, lane >= HEAD_DIM]
        ones = [jnp.where(lane == HEAD_DIM, 1.0, 0.0).astype(BF16), jnp.where(lane == 0, 1.0, 0.0).astype(BF16)]
        qb = q_ref[...] * scale
        qms = [jnp.where(sel, qb, jnp.zeros_like(qb)) for sel in sels]
        cmask = (lax.broadcasted_iota(jnp.int32, (blk, blk), 1) <= lax.broadcasted_iota(jnp.int32, (blk, blk), 0))

        def step(kb, carry, masked):
            r0 = pl.multiple_of(kb * blk, blk)
            ks = k_ref[pl.ds(r0, blk), :]
            vs = v_ref[pl.ds(r0, blk), :]
            out = []
            for j in range(2):
                m, acc = carry[2 * j], carry[2 * j + 1]
                s = _dot(qms[j], ks, NT) - ck_ref[0, kb, j:j + 1, :]
                if masked:
                    s = jnp.where(cmask, s, NEG)
                mn = jnp.maximum(m, jnp.max(s, axis=-1, keepdims=True))
                pr = jnp.exp(s - mn).astype(BF16)
                acc = jnp.exp(m - mn) * acc + _dot(pr, jnp.where(sels[j], vs, ones[j]))
                out += [mn, acc]
            return tuple(out)

        init = (jnp.full((blk, 1), NEG, F32), jnp.zeros((blk, LANES), F32)) * 2
        carry = lax.fori_loop(0, i, lambda kb, c: step(kb, c, False), init)
        m0, a0, m1, a1 = step(i, carry, True)
        l0 = a0[:, HEAD_DIM:HEAD_DIM + 1]
        l1 = a1[:, 0:1]
        o_ref[...] = jnp.where(sels[0], a0 / l0, a1 / l1).astype(BF16)
        lse_ref[...] = jnp.where(sels[0], m0 + jnp.log(l0), m1 + jnp.log(l1))

    return pl.pallas_call(
        body, name="attn_fwd", grid=(npair, p // blk),
        in_specs=[pl.BlockSpec((blk, LANES), lambda h, i: (i, h)),
                  pl.BlockSpec((p, LANES), lambda h, i: (0, h)), pl.BlockSpec((p, LANES), lambda h, i: (0, h)),
                  pl.BlockSpec((1, nkb, 8, blk), lambda h, i: (h, 0, 0, 0))],
        out_specs=[pl.BlockSpec((blk, LANES), lambda h, i: (i, h)), pl.BlockSpec((blk, LANES), lambda h, i: (i, h))],
        out_shape=[jax.ShapeDtypeStruct((p, da), BF16), jax.ShapeDtypeStruct((p, da), F32)],
        compiler_params=_cparams(("parallel", "arbitrary")),
    )(q, k, v, ck)


def _attn_bwd(q, k, v, o, do, lse_rep, ck, blk):
    p, da = q.shape
    npair, nkb = ck.shape[0], ck.shape[1]
    nq = p // blk
    scale = 1.0 / math.sqrt(HEAD_DIM)

    def body(k_ref, v_ref, q_ref, do_ref, o_ref, lse_ref, ck_ref, dk_ref, dv_ref, dq_ref, dcs_ref, rsum_ref, dq_acc):
        jb = pl.program_id(1)

        @pl.when(jb == 0)
        def _():
            dq_acc[...] = jnp.zeros_like(dq_acc)

        ks = k_ref[...]
        vs = v_ref[...]
        lane = lax.broadcasted_iota(jnp.int32, (1, LANES), 1)
        sels = [lane < HEAD_DIM, lane >= HEAD_DIM]
        ones = [jnp.where(lane == HEAD_DIM, 1.0, 0.0).astype(BF16), jnp.where(lane == 0, 1.0, 0.0).astype(BF16)]
        kss = ks * scale
        kmo = [jnp.where(sels[j], kss, ones[j]) for j in range(2)]
        cmask = (lax.broadcasted_iota(jnp.int32, (blk, blk), 1) <= lax.broadcasted_iota(jnp.int32, (blk, blk), 0))

        def step(ib, carry, masked):
            r0 = pl.multiple_of(ib * blk, blk)
            qb = q_ref[pl.ds(r0, blk), :] * scale
            dob = do_ref[pl.ds(r0, blk), :]
            prod = dob.astype(F32) * o_ref[pl.ds(r0, blk), :].astype(F32)
            out = []
            for j in range(2):
                dk, dv = carry[2 * j], carry[2 * j + 1]
                qm = jnp.where(sels[j], qb, jnp.zeros_like(qb))
                dom = jnp.where(sels[j], dob, jnp.zeros_like(dob))
                lse = lse_ref[pl.ds(r0, blk), HEAD_DIM * j:HEAD_DIM * j + 1]
                dlt = jnp.sum(jnp.where(sels[j], prod, 0.0), axis=-1, keepdims=True)
                s = _dot(qm, ks, NT) - ck_ref[0, 0, j:j + 1, :] - lse
                pm = jnp.exp(jnp.minimum(s, 0.0))
                if masked:
                    pm = jnp.where(cmask, pm, 0.0)
                ds_b = (pm * (_dot(dom, vs, NT) - dlt)).astype(BF16)
                dv = dv + _dot(pm.astype(BF16), dom, TN)
                dk = dk + _dot(ds_b, jnp.where(sels[j], qb, ones[j]), TN)
                dq_acc[pl.ds(r0, blk), LANES * j:LANES * (j + 1)] += _dot(ds_b, kmo[j])
                out += [dk, dv]
            return tuple(out)

        zero = jnp.zeros((blk, LANES), F32)
        carry = step(jb, (zero, zero, zero, zero), True)
        dk0, dv0, dk1, dv1 = lax.fori_loop(jb + 1, nq, lambda ib, c: step(ib, c, False), carry)
        dk_ref[...] = jnp.where(sels[0], dk0, dk1).astype(BF16)
        dv_ref[...] = (dv0 + dv1).astype(BF16)
        lane8 = lax.broadcasted_iota(jnp.int32, (1, 8), 1)
        pair8 = lambda c0, c1: jnp.where(lane8 == 0, c0, jnp.where(lane8 == 1, c1, 0.0))
        dcs_ref[0] = pair8(dk0[:, HEAD_DIM:HEAD_DIM + 1], dk1[:, 0:1])

        @pl.when(jb == nkb - 1)
        def _():
            a0 = dq_acc[:, :LANES]
            a1 = dq_acc[:, LANES:]
            dq_ref[...] = jnp.where(sels[0], a0, a1).astype(BF16)
            rsum_ref[0] = pair8(a0[:, HEAD_DIM:HEAD_DIM + 1], a1[:, 0:1])

    colblk = pl.BlockSpec((blk, LANES), lambda h, j: (j, h))
    colfull = pl.BlockSpec((p, LANES), lambda h, j: (0, h))
    ckspec = pl.BlockSpec((1, 1, 8, blk), lambda h, j: (h, j, 0, 0))
    return pl.pallas_call(
        body, name="attn_bwd", grid=(npair, nkb),
        in_specs=[colblk, colblk, colfull, colfull, colfull, colfull, ckspec],
        out_specs=[colblk, colblk, colfull, pl.BlockSpec((1, blk, 8), lambda h, j: (h, j, 0)),
                   pl.BlockSpec((1, p, 8), lambda h, j: (h, 0, 0))],
        out_shape=[jax.ShapeDtypeStruct((p, da), BF16), jax.ShapeDtypeStruct((p, da), BF16),
                   jax.ShapeDtypeStruct((p, da), BF16), jax.ShapeDtypeStruct((npair, p, 8), F32),
                   jax.ShapeDtypeStruct((npair, p, 8), F32)],
        scratch_shapes=[pltpu.VMEM((p, 2 * LANES), F32)],
        compiler_params=_cparams(("parallel", "arbitrary")),
    )(k, v, q, do, o, lse_rep, ck)


def _tail(yssd, o, zatt, graw, head, x2, tgt2, wps, wpa, wout, gate_bias, norm_post):
    p, ds = yssd.shape
    da = o.shape[1]
    d = x2.shape[1]

    def body(yssd_ref, o_ref, zatt_ref, g_ref, head_ref, x_ref, tgt_ref, wps_ref, wpa_ref, wout_ref, gb_ref, np_ref,
             dyssd_ref, do_ref, dzatt_ref, dg_ref, dzo_ref, mrg_ref, da_ref, db_ref, yatt_ref, dout_ref, red_ref):
        i = pl.program_id(0)

        @pl.when(i == 0)
        def _():
            red_ref[...] = jnp.zeros_like(red_ref)

        h = jnp.where(i == 0, head_ref[...], x_ref[...])
        valid = jnp.where(i > 0, 1.0, 0.0)
        ob = o_ref[...].astype(F32)
        za = zatt_ref[...].astype(F32)
        sza = _sigmoid(za)
        silu = za * sza
        yatt_b = (ob * silu).astype(BF16)
        yatt_ref[...] = yatt_b
        wps_v, wpa_v, wout_v = wps_ref[...], wpa_ref[...], wout_ref[...]
        a = _dot(yssd_ref[...], wps_v)
        b = _dot(yatt_b, wpa_v)
        gr = g_ref[...].astype(F32) + gb_ref[...]
        gs = _sigmoid(gr[:, :d])
        ga = _sigmoid(gr[:, d:])
        mrg_b = (gs * a + ga * b).astype(BF16)
        mrg_ref[...] = mrg_b
        zo = _dot(mrg_b, wout_v)
        rstd = lax.rsqrt(jnp.mean(zo * zo, axis=-1, keepdims=True) + EPS)
        zh = zo * rstd
        npw = np_ref[...]
        err = (h + zh * npw - tgt_ref[...]) * valid
        dout = err * (1.0 / d)
        dout_ref[...] = dout
        dzh = dout * npw
        dzo_b = (rstd * (dzh - zh * jnp.mean(dzh * zh, axis=-1, keepdims=True))).astype(BF16)
        dzo_ref[...] = dzo_b
        dm = _dot(dzo_b, wout_v, NT)
        da_b = (gs * dm).astype(BF16)
        db_b = (ga * dm).astype(BF16)
        da_ref[...] = da_b
        db_ref[...] = db_b
        dgs = dm * a * gs * (1.0 - gs)
        dga = dm * b * ga * (1.0 - ga)
        dg_ref[:, :d] = dgs.astype(BF16)
        dg_ref[:, d:] = dga.astype(BF16)
        dyssd_ref[...] = _dot(da_b, wps_v, NT).astype(BF16)
        dya = _dot(db_b, wpa_v, NT)
        do_ref[...] = (dya * silu).astype(BF16)
        dzatt_ref[...] = (dya * ob * sza * (1.0 + za * (1.0 - sza))).astype(BF16)
        red_ref[0:1, :d] += jnp.sum(dout * zh, axis=0, keepdims=True)
        red_ref[1:2, :d] += jnp.sum(dgs, axis=0, keepdims=True)
        red_ref[1:2, d:] += jnp.sum(dga, axis=0, keepdims=True)
        red_ref[2:3, 0:1] += jnp.sum(jnp.sum(err * err, axis=1, keepdims=True), axis=0, keepdims=True) * (0.5 / d)

    row = lambda w: pl.BlockSpec((CHUNK, w), lambda i: (i, 0))
    shifted = lambda w: pl.BlockSpec((CHUNK, w), lambda i: (jnp.maximum(i - 1, 0), 0))
    sd = jax.ShapeDtypeStruct
    return pl.pallas_call(
        body, name="tail", grid=(p // CHUNK,),
        in_specs=[row(ds), row(da), row(da), row(2 * d), _full((CHUNK, d)), shifted(d), shifted(d),
                  _full((ds, d)), _full((da, d)), _full((d, d)), _full((1, 2 * d)), _full((1, d))],
        out_specs=[row(ds), row(da), row(da), row(2 * d), row(d), row(d), row(d), row(d), row(da), row(d),
                   _full((8, 2 * d))],
        out_shape=[sd((p, ds), BF16), sd((p, da), BF16), sd((p, da), BF16), sd((p, 2 * d), BF16), sd((p, d), BF16),
                   sd((p, d), BF16), sd((p, d), BF16), sd((p, d), BF16), sd((p, da), BF16), sd((p, d), F32),
                   sd((8, 2 * d), F32)],
        compiler_params=_cparams(("arbitrary",)),
    )(yssd, o, zatt, graw, head, x2, tgt2, wps, wpa, wout, gate_bias, norm_post)


def _adamw_math(w, g, m, v):
    m2 = ADAM_B1 * m + (1.0 - ADAM_B1) * g
    v2 = ADAM_B2 * v + (1.0 - ADAM_B2) * (g * g)
    m_hat = m2 / (1.0 - ADAM_B1 ** ADAM_STEP)
    v_hat = v2 / (1.0 - ADAM_B2 ** ADAM_STEP)
    delta = -ADAM_LR * (m_hat / (jnp.sqrt(v_hat) + ADAM_EPS) + ADAM_WD * w)
    return delta, m2, v2


def _adamw(w, g, m, v, name, parts=False):
    r, cdim = w.shape
    tr, tc, by_rows = _tiles_2d(r, cdim)
    pick = (lambda i: (i, 0)) if by_rows else (lambda i: (0, i))

    def body(w_ref, g_ref, m_ref, v_ref, go_ref, d_ref, mo_ref, vo_ref):
        if parts:
            g = g_ref[0].astype(F32)
            for s in range(1, g_ref.shape[0]):
                g = g + g_ref[s].astype(F32)
        else:
            g = g_ref[...]
        delta, m2, v2 = _adamw_math(w_ref[...], g, m_ref[...], v_ref[...])
        go_ref[...] = g
        d_ref[...] = delta
        mo_ref[...] = m2
        vo_ref[...] = v2

    blk = pl.BlockSpec((tr, tc), pick)
    gspec = pl.BlockSpec((g.shape[0], tr, tc), lambda i: (0,) + pick(i)) if parts else blk
    return pl.pallas_call(
        body, name=name, grid=((r // tr) * (cdim // tc),),
        in_specs=[blk, gspec, blk, blk], out_specs=[blk] * 4,
        out_shape=[jax.ShapeDtypeStruct((r, cdim), F32)] * 4,
        compiler_params=_cparams(("parallel",)),
    )(w, g, m, v)


def _pad_cols(a, width):
    return jnp.pad(a, ((0, 0), (0, width - a.shape[1])))


def _pack_small_shard(conv_w_sh, meta_sh, width):
    return jnp.concatenate([_pad_cols(conv_w_sh, width), jnp.zeros((4, width), F32), _pad_cols(meta_sh, width)], axis=0)


def _pack_small_rep(norm_pre, norm_post, gate_bias, ssd_norm, conv_b, misc, width):
    rows = [norm_pre, norm_post, gate_bias, ssd_norm, conv_b, misc]
    return jnp.concatenate([_pad_cols(r, width) for r in rows] + [jnp.zeros((2, width), F32)], axis=0)


def _misc_row(dt_bias, fgate_bias, a_log, d_skip, extra):
    hs, ha = dt_bias.shape[1], fgate_bias.shape[1]
    return jnp.concatenate([dt_bias, fgate_bias, jnp.zeros((1, LANES - hs - ha), F32), _pad_cols(a_log, LANES),
                            _pad_cols(d_skip, LANES), _pad_cols(extra, LANES)], axis=1)


def kernel(x, meta_tokens, norm_pre, w_in, conv_w, conv_b, dt_bias, a_log, d_skip, ssd_norm, fgate_bias, gate_bias, w_proj_ssd, w_proj_att, w_out, norm_post, loss_target, m_meta_tokens, m_norm_pre, m_w_in, m_conv_w, m_conv_b, m_dt_bias, m_a_log, m_d_skip, m_ssd_norm, m_fgate_bias, m_gate_bias, m_w_proj_ssd, m_w_proj_att, m_w_out, m_norm_post, v_meta_tokens, v_norm_pre, v_w_in, v_conv_w, v_conv_b, v_dt_bias, v_a_log, v_d_skip, v_ssd_norm, v_fgate_bias, v_gate_bias, v_w_proj_ssd, v_w_proj_att, v_w_out, v_norm_post):
    seq, d = x.shape[1], x.shape[2]
    p = seq + CHUNK
    hs, ha = dt_bias.shape[1], fgate_bias.shape[1]
    ds, cd = ssd_norm.shape[1], conv_b.shape[1]
    da = ha * HEAD_DIM
    nc8 = w_in.shape[2]
    cws = cd // N_DEV
    msh = d // N_DEV
    r1, r2, r3 = ds // N_DEV, da // N_DEV, d // N_DEV
    me = _dev_index(*_my_pos())
    x2, tgt2 = x[0], loss_target[0]

    win_sh = jnp.transpose(w_in[0]).astype(BF16)
    rows_sh = jnp.concatenate([w_proj_ssd[0], w_proj_att[0], w_out[0]], axis=0).astype(BF16)
    small_sh = _pack_small_shard(conv_w[0], meta_tokens, cws)
    win_all, rows_all, small_all = _all_gather([win_sh, rows_sh, small_sh], "gather_weights")
    w_full = win_all.reshape(N_DEV * nc8, d)
    cuts = [0, ds, ds + cd, ds + cd + hs, ds + cd + hs + da, ds + cd + hs + 2 * da, ds + cd + hs + 3 * da,
            ds + cd + hs + 4 * da, ds + cd + hs + 4 * da + ha, ds + cd + hs + 4 * da + ha + 2 * d]
    w_z, w_xbc, w_dt, w_zatt, w_q, w_k, w_v, w_f, w_g = [w_full[cuts[i]:cuts[i + 1]] for i in range(9)]
    w_dtf = jnp.concatenate([w_dt, w_f, jnp.zeros((LANES - hs - ha, d), BF16)], axis=0)
    wps = rows_all[:, :r1].reshape(ds, d)
    wpa = rows_all[:, r1:r1 + r2].reshape(da, d)
    wout = rows_all[:, r1 + r2:].reshape(d, d)
    conv_w_full = jnp.transpose(small_all[:, 0:CONV_K, :], (1, 0, 2)).reshape(CONV_K, cd)
    meta_full = jnp.transpose(small_all[:, 8:8 + N_META, :msh], (1, 0, 2)).reshape(N_META, d)
    head = jnp.concatenate([jnp.zeros((PADN, d), F32), meta_full], axis=0)

    u = _prenorm_fwd(head, x2, norm_pre)
    tm = _att_block(p)
    seg_w = [w_z, w_xbc, w_zatt, w_q, w_k, w_v, w_g]
    zs, xbc, zatt, q, k, v, graw = [
        _mm(u, w, "nt", BF16, tm, _tile(w.shape[0], (1024, 512, 256, 128)), "inproj_%d" % i) for i, w in enumerate(seg_w)]
    dtf = _mm(u, w_dtf, "nt", F32, tm, LANES, "inproj_dtf")

    brow = jnp.concatenate([dt_bias, fgate_bias, jnp.zeros((1, LANES - hs - ha), F32)], axis=1)
    alog_row = _pad_cols(a_log, LANES)
    dskip_l = jnp.repeat(d_skip, HEAD_DIM, axis=1)
    sel_t = (lax.broadcasted_iota(jnp.int32, (LANES, ds), 1) // HEAD_DIM
             == lax.broadcasted_iota(jnp.int32, (LANES, ds), 0)).astype(BF16)
    sel = sel_t.T
    y, yssd, hin, cf = _ssd_fwd(xbc, zs, dtf, conv_w_full, conv_b, brow, alog_row, dskip_l, ssd_norm, sel_t, hs, ha)

    blk = _att_block(p)
    nkb, npair = p // blk, ha // 2
    cum = jnp.where(lax.broadcasted_iota(jnp.int32, (p, 1), 0) < PADN, -NEG, cf[:, hs:hs + ha])
    ck = jnp.transpose(cum.T.reshape(npair, 2, nkb, blk), (0, 2, 1, 3))
    ck = jnp.pad(ck, ((0, 0), (0, 0), (0, 6), (0, 0)))
    o, lse_rep = _attn_fwd(q, k, v, ck, blk)

    (dyssd, d_o, dzatt, dgraw, dzo, mrg, da_, db_, yatt, dout, red_tail) = _tail(
        yssd, o, zatt, graw, head, x2, tgt2, wps, wpa, wout, gate_bias, norm_post)

    tw = _tile(d, (512, 256, 128))
    g_wout = _mm(mrg, dzo, "tn", BF16, tw, tw, "wgrad_out")
    g_wps = _mm(yssd, da_, "tn", BF16, _tile(ds, (512, 256, 128)), tw, "wgrad_ps")
    g_wpa = _mm(yatt, db_, "tn", BF16, _tile(da, (512, 256, 128)), tw, "wgrad_pa")

    dk, dv, dq, dcs, rsum = _attn_bwd(q, k, v, o, d_o, lse_rep, ck, blk)
    dcum = jnp.transpose((rsum - dcs)[:, :, 0:2], (1, 0, 2)).reshape(p, ha)
    dcf = jnp.pad(dcum, ((0, 0), (hs, LANES - hs - ha)))
    dxbc, dzs, ddtf, gcw, gcb, gnrm, gsm = _ssd_bwd(
        dyssd, y, zs, xbc, dtf, hin, dcf, conv_w_full, conv_b, brow, alog_row, dskip_l, ssd_norm, sel_t, sel, hs, ha)
    ddtf_b = ddtf.astype(BF16)

    dsegs = [dzs, dxbc, dzatt, dq, dk, dv, dgraw, ddtf_b]
    gsegs = [_mm(dsg, u, "tn", BF16, _tile(dsg.shape[1], (512, 256, 128)), tw, "wgrad_in_%d" % i)
             for i, dsg in enumerate(dsegs)]
    g_z, g_xbc, g_zatt, g_q, g_k, g_v, g_g, g_dtf = gsegs
    gw_full = jnp.concatenate([g_z, g_xbc, g_dtf[:hs], g_zatt, g_q, g_k, g_v, g_dtf[hs:hs + ha], g_g], axis=0)
    gwin_parts = gw_full.reshape(N_DEV, nc8, d)
    grows_parts = jnp.concatenate([g_wps.reshape(N_DEV, r1, d), g_wpa.reshape(N_DEV, r2, d),
                                   g_wout.reshape(N_DEV, r3, d)], axis=1)

    core = lax.axis_index("c").astype(jnp.int32).reshape(1)
    sib_win, sib_rows = _exchange_sibling([gwin_parts, grows_parts], "scatter_grads_sibling")
    chip_win = _pair_add(gwin_parts, sib_win, core, "pair_add_w_in")
    chip_rows = _pair_add(grows_parts, sib_rows, core, "pair_add_rows")
    sems, thru, lands, token = _exchange_chips_start([chip_win, chip_rows], "scatter_grads_start")
    dsegs_after = dsegs[:-1] + [ddtf_b + token[0:1, 0:1].astype(BF16)]
    du = _mm_sum_nn(dsegs_after, seg_w + [w_dtf], tm, _tile(d, (256, 128)), "dgrad_in")
    gx, ghead, gnp = _prenorm_bwd(head, x2, norm_pre, du, dout)
    sent, got = _exchange_chips_wait(sems, thru, lands, gnp, "scatter_grads_wait")
    chip = me // 2
    recv_win, recv_rows = [lax.dynamic_update_slice_in_dim(g, lax.dynamic_slice_in_dim(s, chip, 1, axis=0), chip, axis=0)
                           for g, s in zip(got, sent)]
    gmisc = jnp.concatenate([gsm[0:1], gsm[1:2], gsm[2:3], _pad_cols(red_tail[2:3, 0:1], LANES)], axis=1)
    small_g = jnp.concatenate([
        _pack_small_rep(gnp[0:1], red_tail[0:1, :d], red_tail[1:2], gnrm[0:1], gcb[0:1], gmisc, cd),
        _pad_cols(gcw[0:CONV_K], cd), jnp.zeros((4, cd), F32), _pad_cols(ghead[PADN:], cd)], axis=0)
    red = _all_reduce_small(small_g, "reduce_small")

    loss = red[5, 3 * LANES]
    g_small_sh = _pack_small_shard(lax.dynamic_slice_in_dim(red[8:8 + CONV_K], me * cws, cws, axis=1),
                                   lax.dynamic_slice_in_dim(red[16:16 + N_META, :d], me * msh, msh, axis=1), cws)

    zero1 = jnp.zeros((1, 1), F32)
    upd_in = _adamw(jnp.transpose(w_in[0]), recv_win, jnp.transpose(m_w_in[0]), jnp.transpose(v_w_in[0]),
                    "adamw_w_in", parts=True)
    cat3 = lambda a, b, c: jnp.concatenate([a[0], b[0], c[0]], axis=0)
    upd_rows = _adamw(cat3(w_proj_ssd, w_proj_att, w_out), recv_rows, cat3(m_w_proj_ssd, m_w_proj_att, m_w_out),
                      cat3(v_w_proj_ssd, v_w_proj_att, v_w_out), "adamw_rows", parts=True)
    rep = lambda a, b, c, e, f, g1, g2, g3, g4: _pack_small_rep(a, b, c, e, f, _misc_row(g1, g2, g3, g4, zero1), cd)
    upd_rep = _adamw(rep(norm_pre, norm_post, gate_bias, ssd_norm, conv_b, dt_bias, fgate_bias, a_log, d_skip),
                     red[0:8],
                     rep(m_norm_pre, m_norm_post, m_gate_bias, m_ssd_norm, m_conv_b, m_dt_bias, m_fgate_bias, m_a_log, m_d_skip),
                     rep(v_norm_pre, v_norm_post, v_gate_bias, v_ssd_norm, v_conv_b, v_dt_bias, v_fgate_bias, v_a_log, v_d_skip),
                     "adamw_rep")
    upd_sh = _adamw(small_sh, g_small_sh, _pack_small_shard(m_conv_w[0], m_meta_tokens, cws),
                    _pack_small_shard(v_conv_w[0], v_meta_tokens, cws), "adamw_small_shard")

    def leaves(i):
        a_in, a_rows, a_rep, a_sh = upd_in[i], upd_rows[i], upd_rep[i], upd_sh[i]
        misc = a_rep[5:6]
        return [a_sh[8:8 + N_META, :msh], a_rep[0:1, :d], jnp.transpose(a_in)[None], a_sh[0:CONV_K][None], a_rep[4:5, :cd],
                misc[:, :hs], misc[:, LANES:LANES + hs], misc[:, 2 * LANES:2 * LANES + hs], a_rep[3:4, :ds],
                misc[:, hs:hs + ha], a_rep[2:3, :2 * d], a_rows[:r1][None], a_rows[r1:r1 + r2][None],
                a_rows[r1 + r2:][None], a_rep[1:2, :d]]

    return tuple([loss, gx[None]] + leaves(0) + leaves(1) + leaves(2) + leaves(3))
```

```python
import functools
import math

import jax
import jax.numpy as jnp
from jax import lax
from jax.experimental import pallas as pl
from jax.experimental.pallas import tpu as pltpu

F32 = jnp.float32
BF16 = jnp.bfloat16

N_DEV = 8
N_META = 16
CHUNK = 128
PADN = CHUNK - N_META
HEAD_DIM = 64
SSD_GROUPS = 4
CONV_K = 4
EPS = 1e-6
NEG = -1e30
LANES = 128
HALO = 16

ADAM_LR = 0.001
ADAM_B1 = 0.9
ADAM_B2 = 0.999
ADAM_EPS = 1e-08
ADAM_WD = 0.01
ADAM_STEP = 10

VMEM_LIMIT = 56 * 1024 * 1024

NN = (((1,), (0,)), ((), ()))
NT = (((1,), (1,)), ((), ()))
TN = (((0,), (0,)), ((), ()))
MESH = pl.DeviceIdType.MESH


def _dot(a, b, dims=NN):
    return lax.dot_general(a, b, dims, preferred_element_type=F32)


def _split2(x):
    hi = x.astype(BF16)
    lo = (x - hi.astype(F32)).astype(BF16)
    return hi, lo


def _dot_sel(x, sel):
    hi, lo = _split2(x)
    return _dot(hi, sel) + _dot(lo, sel)


def _dot_tri(tri, x):
    h1 = x.astype(BF16)
    r1 = x - h1.astype(F32)
    h2 = r1.astype(BF16)
    h3 = (r1 - h2.astype(F32)).astype(BF16)
    return _dot(tri, h1) + _dot(tri, h2) + _dot(tri, h3)


def _sigmoid(x):
    return 1.0 / (1.0 + jnp.exp(-x))


def _softplus(x):
    return jnp.maximum(x, 0.0) + jnp.log(1.0 + jnp.exp(-jnp.abs(x)))


def _cparams(sem=None, vmem=VMEM_LIMIT):
    kw = {"vmem_limit_bytes": vmem}
    if sem is not None:
        kw["dimension_semantics"] = sem
    return pltpu.CompilerParams(**kw)


def _full(shape):
    nd = len(shape)
    return pl.BlockSpec(shape, lambda *_: (0,) * nd)


def _att_block(p):
    return 384 if p % 384 == 0 else CHUNK


def _my_pos():
    return lax.axis_index("x"), lax.axis_index("y"), lax.axis_index("c")


def _dev_index(x, y, c):
    return 4 * x + 2 * y + c


FLIPS = [(fx, fy, fc) for fx in (0, 1) for fy in (0, 1) for fc in (0, 1)][1:]


def _flip(pos, f):
    return tuple((1 - p) if fi else p for p, fi in zip(pos, f))


def _all_gather(bufs, name):
    nb = len(bufs)

    def body(*refs):
        ins, outs = refs[:nb], refs[nb:2 * nb]
        send_sems, recv_sems, local_sems = refs[2 * nb:]
        x, y, c = _my_pos()
        me = _dev_index(x, y, c)
        sibling = (x, y, 1 - c)
        chips = [(1 - x, y), (x, 1 - y), (1 - x, 1 - y)]

        def copy(b, k, block_idx, to, src=None):
            dst = outs[b].at[block_idx]
            return pltpu.make_async_remote_copy(
                src_ref=dst if src is None else src, dst_ref=dst,
                send_sem=send_sems.at[b, k], recv_sem=recv_sems.at[b, k],
                device_id=to, device_id_type=MESH)

        started = []
        for b in range(nb):
            mine = pltpu.make_async_copy(ins[b], outs[b].at[me], local_sems.at[b])
            mine.start()
            started.append(mine)
        first = []
        for b in range(nb):
            first.append(copy(b, 0, me, sibling, src=ins[b]))
            for j, chip in enumerate(chips):
                first.append(copy(b, 1 + j, me, (chip[0], chip[1], c), src=ins[b]))
        for cp in first:
            cp.start()
        passed = []
        for j, chip in enumerate(chips):
            blk = _dev_index(chip[0], chip[1], c)
            for b in range(nb):
                copy(b, 1 + j, blk, (x, y, c)).wait_recv()
                fwd = copy(b, 4 + j, blk, sibling)
                fwd.start()
                passed.append(fwd)
        for b in range(nb):
            copy(b, 0, _dev_index(x, y, 1 - c), (x, y, c)).wait_recv()
        for j, chip in enumerate(chips):
            blk = _dev_index(chip[0], chip[1], 1 - c)
            for b in range(nb):
                copy(b, 4 + j, blk, (x, y, c)).wait_recv()
        for cp in first + passed:
            cp.wait_send()
        for mine in started:
            mine.wait()

    any_spec = pl.BlockSpec(memory_space=pl.ANY)
    return pl.pallas_call(
        body, name=name,
        out_shape=[jax.ShapeDtypeStruct((N_DEV,) + b.shape, b.dtype) for b in bufs],
        in_specs=[any_spec] * nb, out_specs=[any_spec] * nb,
        scratch_shapes=[pltpu.SemaphoreType.DMA((nb, 7)), pltpu.SemaphoreType.DMA((nb, 7)),
                        pltpu.SemaphoreType.DMA((nb,))],
    )(*bufs)


N_CHIP = 4
CHIP_FLIPS = [(1, 0), (0, 1), (1, 1)]


def _exchange_sibling(bufs, name):
    nb = len(bufs)

    def body(*refs):
        ins, outs = refs[:nb], refs[nb:2 * nb]
        send_sems, recv_sems = refs[2 * nb:]
        x, y, c = _my_pos()

        def copy(b, k):
            return pltpu.make_async_remote_copy(
                src_ref=ins[b].at[2 * k + (1 - c)], dst_ref=outs[b].at[k],
                send_sem=send_sems.at[b, k], recv_sem=recv_sems.at[b, k],
                device_id=(x, y, 1 - c), device_id_type=MESH)

        cps = [copy(b, k) for b in range(nb) for k in range(N_CHIP)]
        for cp in cps:
            cp.start()
        for cp in cps:
            cp.wait()

    any_spec = pl.BlockSpec(memory_space=pl.ANY)
    return pl.pallas_call(
        body, name=name,
        out_shape=[jax.ShapeDtypeStruct((N_CHIP,) + b.shape[1:], b.dtype) for b in bufs],
        in_specs=[any_spec] * nb, out_specs=[any_spec] * nb,
        scratch_shapes=[pltpu.SemaphoreType.DMA((nb, N_CHIP)), pltpu.SemaphoreType.DMA((nb, N_CHIP))],
    )(*bufs)


def _pair_add(mine, recv, core, name):
    _, r, cdim = mine.shape
    tr, tc, by_rows = _tiles_2d(r, cdim)
    pick = (lambda i: (i, 0)) if by_rows else (lambda i: (0, i))

    def body(core_ref, a_ref, b_ref, o_ref):
        o_ref[0] = (a_ref[0, 0].astype(F32) + b_ref[0].astype(F32)).astype(o_ref.dtype)

    return pl.pallas_call(
        body, name=name,
        grid_spec=pltpu.PrefetchScalarGridSpec(
            num_scalar_prefetch=1, grid=(N_CHIP, (r // tr) * (cdim // tc)),
            in_specs=[pl.BlockSpec((1, 1, tr, tc), lambda k, i, core_ref: (k, core_ref[0]) + pick(i)),
                      pl.BlockSpec((1, tr, tc), lambda k, i, core_ref: (k,) + pick(i))],
            out_specs=pl.BlockSpec((1, tr, tc), lambda k, i, core_ref: (k,) + pick(i))),
        out_shape=jax.ShapeDtypeStruct((N_CHIP, r, cdim), mine.dtype),
        compiler_params=_cparams(("parallel", "parallel")),
    )(core, mine.reshape(N_CHIP, 2, r, cdim), recv)


def _chip_peer(x, y, f):
    return ((1 - x) if f[0] else x), ((1 - y) if f[1] else y)


def _exchange_chips_start(bufs, name):
    nb = len(bufs)
    nsem = 2 * 3 * nb

    def body(*refs):
        ins, lands = refs[:nb], refs[nb:2 * nb]
        sems = refs[2 * nb:2 * nb + nsem]
        token = refs[-1]
        x, y, c = _my_pos()
        for b in range(nb):
            for j, f in enumerate(CHIP_FLIPS):
                px, py = _chip_peer(x, y, f)
                pltpu.make_async_remote_copy(
                    src_ref=ins[b].at[2 * px + py], dst_ref=lands[b].at[2 * x + y],
                    send_sem=sems[2 * (3 * b + j)], recv_sem=sems[2 * (3 * b + j) + 1],
                    device_id=(px, py, c), device_id_type=MESH).start()
        token[...] = jnp.zeros_like(token)

    hbm = pl.BlockSpec(memory_space=pltpu.HBM)
    sem = pl.BlockSpec(memory_space=pltpu.SEMAPHORE)
    out = pl.pallas_call(
        body, name=name,
        out_shape=(*([pltpu.SemaphoreType.DMA(())] * nsem),
                   *[pltpu.HBM(b.shape, b.dtype) for b in bufs], *[pltpu.HBM(b.shape, b.dtype) for b in bufs],
                   jax.ShapeDtypeStruct((8, LANES), F32)),
        in_specs=[hbm] * (2 * nb),
        out_specs=(*([sem] * nsem), *([hbm] * (2 * nb)), pl.BlockSpec(memory_space=pltpu.VMEM)),
        input_output_aliases={i: nsem + i for i in range(2 * nb)},
        compiler_params=pltpu.CompilerParams(has_side_effects=pltpu.SideEffectType.DATAFLOW_SIDE_EFFECTING),
    )(*[pltpu.with_memory_space_constraint(b, pltpu.HBM) for b in bufs],
      *[pltpu.with_memory_space_constraint(lax.empty(b.shape, b.dtype), pltpu.HBM) for b in bufs])
    return out[:nsem], out[nsem:nsem + nb], out[nsem + nb:nsem + 2 * nb], out[-1]


def _exchange_chips_wait(sems, thru, lands, after, name):
    nb = len(thru)
    nsem = len(sems)

    def body(*refs):
        ins, lnd = refs[:nb], refs[nb:2 * nb]
        sem_refs = refs[2 * nb:2 * nb + nsem]
        x, y, c = _my_pos()
        for b in range(nb):
            for j, f in enumerate(CHIP_FLIPS):
                px, py = _chip_peer(x, y, f)
                cp = pltpu.make_async_remote_copy(
                    src_ref=ins[b].at[2 * px + py], dst_ref=lnd[b].at[2 * px + py],
                    send_sem=sem_refs[2 * (3 * b + j)], recv_sem=sem_refs[2 * (3 * b + j) + 1],
                    device_id=(px, py, c), device_id_type=MESH)
                cp.wait_send()
                cp.wait_recv()

    hbm = pl.BlockSpec(memory_space=pltpu.HBM)
    sem = pl.BlockSpec(memory_space=pltpu.SEMAPHORE)
    out = pl.pallas_call(
        body, name=name,
        out_shape=tuple([pltpu.HBM(b.shape, b.dtype) for b in thru] + [pltpu.HBM(b.shape, b.dtype) for b in lands]),
        in_specs=[hbm] * (2 * nb) + [sem] * nsem + [pl.BlockSpec(memory_space=pl.ANY)],
        out_specs=tuple([hbm] * (2 * nb)),
        input_output_aliases={i: i for i in range(2 * nb)},
        compiler_params=pltpu.CompilerParams(has_side_effects=pltpu.SideEffectType.DATAFLOW_SIDE_EFFECTING),
    )(*thru, *lands, *sems, after)
    return out[:nb], out[nb:]


def _bcast_start(buf, name):
    nsem = 2 * len(FLIPS)

    def body(src, land, *rest):
        sems, token = rest[:nsem], rest[-1]
        pos = _my_pos()
        for k, f in enumerate(FLIPS):
            pltpu.make_async_remote_copy(
                src_ref=src, dst_ref=land.at[_dev_index(*pos)], send_sem=sems[2 * k], recv_sem=sems[2 * k + 1],
                device_id=_flip(pos, f), device_id_type=MESH).start()
        token[...] = jnp.zeros_like(token)

    hbm = pl.BlockSpec(memory_space=pltpu.HBM)
    sem = pl.BlockSpec(memory_space=pltpu.SEMAPHORE)
    land_shape = (N_DEV,) + buf.shape
    out = pl.pallas_call(
        body, name=name,
        out_shape=(*([pltpu.SemaphoreType.DMA(())] * nsem), pltpu.HBM(buf.shape, buf.dtype),
                   pltpu.HBM(land_shape, buf.dtype), jax.ShapeDtypeStruct((8, LANES), F32)),
        in_specs=[hbm, hbm],
        out_specs=(*([sem] * nsem), hbm, hbm, pl.BlockSpec(memory_space=pltpu.VMEM)),
        input_output_aliases={0: nsem, 1: nsem + 1},
        compiler_params=pltpu.CompilerParams(has_side_effects=pltpu.SideEffectType.DATAFLOW_SIDE_EFFECTING),
    )(pltpu.with_memory_space_constraint(buf, pltpu.HBM),
      pltpu.with_memory_space_constraint(lax.empty(land_shape, buf.dtype), pltpu.HBM))
    return out[:nsem], out[nsem], out[nsem + 1], out[-1]


def _bcast_wait(sems, thru, land, after, name):
    nsem = len(sems)

    def body(src, lnd, *rest):
        sem_refs = rest[:nsem]
        pos = _my_pos()
        for k, f in enumerate(FLIPS):
            peer = _flip(pos, f)
            cp = pltpu.make_async_remote_copy(
                src_ref=src, dst_ref=lnd.at[_dev_index(*peer)], send_sem=sem_refs[2 * k],
                recv_sem=sem_refs[2 * k + 1], device_id=peer, device_id_type=MESH)
            cp.wait_send()
            cp.wait_recv()

    hbm = pl.BlockSpec(memory_space=pltpu.HBM)
    sem = pl.BlockSpec(memory_space=pltpu.SEMAPHORE)
    sent, got = pl.pallas_call(
        body, name=name,
        out_shape=(pltpu.HBM(thru.shape, thru.dtype), pltpu.HBM(land.shape, land.dtype)),
        in_specs=[hbm, hbm] + [sem] * nsem + [pl.BlockSpec(memory_space=pl.ANY)],
        out_specs=(hbm, hbm), input_output_aliases={0: 0, 1: 1},
        compiler_params=pltpu.CompilerParams(has_side_effects=pltpu.SideEffectType.DATAFLOW_SIDE_EFFECTING),
    )(thru, land, *sems, after)
    return lax.dynamic_update_slice_in_dim(got, sent[None], _dev_index(*_my_pos()), axis=0)


def _sum_slots(v, name):
    _, r, cdim = v.shape

    def body(v_ref, o_ref):
        acc = v_ref[0]
        for s in range(1, N_DEV):
            acc = acc + v_ref[s]
        o_ref[...] = acc

    return pl.pallas_call(
        body, name=name, out_shape=jax.ShapeDtypeStruct((r, cdim), F32),
        in_specs=[_full((N_DEV, r, cdim))], out_specs=_full((r, cdim)), grid=(1,),
        compiler_params=_cparams(("arbitrary",)),
    )(v)


def _mm(a, b, dims, out_dtype, tm, tn, name):
    if dims == "nn":
        (m, k), (_, n) = a.shape, b.shape
        a_spec = pl.BlockSpec((tm, k), lambda j, i: (i, 0))
        b_spec = pl.BlockSpec((k, tn), lambda j, i: (0, j))
        dn = NN
    elif dims == "nt":
        (m, k), (n, _) = a.shape, b.shape
        a_spec = pl.BlockSpec((tm, k), lambda j, i: (i, 0))
        b_spec = pl.BlockSpec((tn, k), lambda j, i: (j, 0))
        dn = NT
    else:
        (k, m), (_, n) = a.shape, b.shape
        a_spec = pl.BlockSpec((k, tm), lambda j, i: (0, i))
        b_spec = pl.BlockSpec((k, tn), lambda j, i: (0, j))
        dn = TN
    assert m % tm == 0 and n % tn == 0, (m, tm, n, tn)

    def body(a_ref, b_ref, o_ref):
        o_ref[...] = _dot(a_ref[...], b_ref[...], dn).astype(o_ref.dtype)

    return pl.pallas_call(
        body, name=name, grid=(n // tn, m // tm),
        in_specs=[a_spec, b_spec], out_specs=pl.BlockSpec((tm, tn), lambda j, i: (i, j)),
        out_shape=jax.ShapeDtypeStruct((m, n), out_dtype),
        compiler_params=_cparams(("parallel", "parallel")),
    )(a, b)


def _tiles_2d(r, cdim):
    if r % CHUNK == 0:
        return CHUNK, cdim, True
    return r, _tile(cdim, (256, 128)), False


def _mm_sum_nn(a_list, b_list, tm, tn, name):
    n_op = len(a_list)
    m, n = a_list[0].shape[0], b_list[0].shape[1]

    def body(*refs):
        acc = _dot(refs[0][...], refs[n_op][...])
        for i in range(1, n_op):
            acc = acc + _dot(refs[i][...], refs[n_op + i][...])
        refs[2 * n_op][...] = acc

    return pl.pallas_call(
        body, name=name, grid=(n // tn, m // tm),
        in_specs=([pl.BlockSpec((tm, a.shape[1]), lambda j, i: (i, 0)) for a in a_list]
                  + [pl.BlockSpec((b.shape[0], tn), lambda j, i: (0, j)) for b in b_list]),
        out_specs=pl.BlockSpec((tm, tn), lambda j, i: (i, j)),
        out_shape=jax.ShapeDtypeStruct((m, n), F32),
        compiler_params=_cparams(("parallel", "parallel")),
    )(*a_list, *b_list)


def _tile(n, prefs):
    for t in prefs:
        if n % t == 0:
            return t
    return n


def _prenorm_fwd(head, x2, w):
    p, d = x2.shape[0] + CHUNK, x2.shape[1]

    def body(head_ref, x_ref, w_ref, u_ref):
        i = pl.program_id(0)
        h = jnp.where(i == 0, head_ref[...], x_ref[...])
        ms = jnp.mean(h * h, axis=-1, keepdims=True)
        u_ref[...] = (h * lax.rsqrt(ms + EPS) * w_ref[...]).astype(BF16)

    return pl.pallas_call(
        body, name="prenorm_fwd", grid=(p // CHUNK,),
        in_specs=[_full((CHUNK, d)), pl.BlockSpec((CHUNK, d), lambda i: (jnp.maximum(i - 1, 0), 0)), _full((1, d))],
        out_specs=pl.BlockSpec((CHUNK, d), lambda i: (i, 0)),
        out_shape=jax.ShapeDtypeStruct((p, d), BF16),
        compiler_params=_cparams(("arbitrary",)),
    )(head, x2, w)


def _prenorm_bwd(head, x2, w, du, dout):
    p, d = x2.shape[0] + CHUNK, x2.shape[1]

    def body(head_ref, x_ref, w_ref, du_ref, dout_ref, gx_ref, ghead_ref, gw_ref):
        i = pl.program_id(0)
        h = jnp.where(i == 0, head_ref[...], x_ref[...])
        rstd = lax.rsqrt(jnp.mean(h * h, axis=-1, keepdims=True) + EPS)
        xhat = h * rstd
        dub = du_ref[...]
        dxh = dub * w_ref[...]
        dh = rstd * (dxh - xhat * jnp.mean(dxh * xhat, axis=-1, keepdims=True)) + dout_ref[...]

        @pl.when(i == 0)
        def _():
            ghead_ref[...] = dh
            gw_ref[...] = jnp.zeros_like(gw_ref)

        gx_ref[...] = dh
        gw_ref[0:1, :] += jnp.sum(dub * xhat, axis=0, keepdims=True)

    return pl.pallas_call(
        body, name="prenorm_bwd", grid=(p // CHUNK,),
        in_specs=[_full((CHUNK, d)), pl.BlockSpec((CHUNK, d), lambda i: (jnp.maximum(i - 1, 0), 0)), _full((1, d)),
                  pl.BlockSpec((CHUNK, d), lambda i: (i, 0)), pl.BlockSpec((CHUNK, d), lambda i: (i, 0))],
        out_specs=[pl.BlockSpec((CHUNK, d), lambda i: (jnp.maximum(i - 1, 0), 0)), _full((CHUNK, d)), _full((8, d))],
        out_shape=[jax.ShapeDtypeStruct(x2.shape, F32), jax.ShapeDtypeStruct((CHUNK, d), F32),
                   jax.ShapeDtypeStruct((8, d), F32)],
        compiler_params=_cparams(("arbitrary",)),
    )(head, x2, w, du, dout)


def _conv_pre(xr, halo128, cw_ref, cb_ref, rows):
    pre = cb_ref[...] + cw_ref[CONV_K - 1:CONV_K, :] * xr
    shifted = []
    for j in range(1, CONV_K):
        sh = jnp.where(rows >= j, pltpu.roll(xr, j, 0), pltpu.roll(halo128, j, 0))
        shifted.append(sh)
        pre = pre + cw_ref[CONV_K - 1 - j:CONV_K - j, :] * sh
    return pre, shifted


def _ssd_scalars(dtf_ref, brow_ref, alog_ref, rowmask, hs, ha, tri):
    lane = lax.broadcasted_iota(jnp.int32, (1, LANES), 1)
    is_dt = lane < hs
    is_f = (lane >= hs) & (lane < hs + ha)
    dtr = dtf_ref[...] + brow_ref[...]
    sp = _softplus(dtr)
    dt = jnp.where(is_dt, sp, 0.0) * rowmask
    logf = jnp.where(is_f, jnp.minimum(dtr, 0.0) - jnp.log(1.0 + jnp.exp(-jnp.abs(dtr))), 0.0) * rowmask
    a_row = jnp.where(is_dt, -jnp.exp(alog_ref[...]), 0.0)
    run = _dot_tri(tri, dt * a_row + logf)
    return dtr, dt, a_row, run, is_dt, is_f


def _tri_mats():
    r = lax.broadcasted_iota(jnp.int32, (CHUNK, CHUNK), 0)
    c = lax.broadcasted_iota(jnp.int32, (CHUNK, CHUNK), 1)
    return r, c


def _ssd_fwd(xbc, z, dtf, conv_w, conv_b, brow, alog, dskip_l, ssd_norm, sel_t, hs, ha):
    p, cd = xbc.shape
    ds = z.shape[1]
    ns = (cd - ds) // (2 * SSD_GROUPS)
    gw = ds // SSD_GROUPS
    nch = p // CHUNK
    hpg = hs // SSD_GROUPS

    def body(xbc_ref, halo_ref, z_ref, dtf_ref, cw_ref, cb_ref, brow_ref, alog_ref, dsk_ref, nrm_ref, selt_ref,
             y_ref, yssd_ref, hin_ref, cf_ref, st_ref, carry_ref, yacc_ref):
        c = pl.program_id(0)

        @pl.when(c == 0)
        def _():
            st_ref[...] = jnp.zeros_like(st_ref)
            carry_ref[...] = jnp.zeros_like(carry_ref)

        rows = lax.broadcasted_iota(jnp.int32, (CHUNK, 1), 0)
        rowmask = jnp.where((rows >= PADN) | (c > 0), 1.0, 0.0)
        ri, ci = _tri_mats()
        causal = ri >= ci
        tri = jnp.where(causal, 1.0, 0.0).astype(BF16)

        xr = xbc_ref[...].astype(F32)
        halo = halo_ref[...].astype(F32) * jnp.where(c > 0, 1.0, 0.0)
        halo128 = jnp.concatenate([jnp.zeros((CHUNK - HALO, cd), F32), halo], axis=0)
        pre, _ = _conv_pre(xr, halo128, cw_ref, cb_ref, rows)
        xc = pre * _sigmoid(pre) * rowmask

        dtr, dt, a_row, run, is_dt, is_f = _ssd_scalars(dtf_ref, brow_ref, alog_ref, rowmask, hs, ha, tri)
        cf = run + carry_ref[...]
        cf_ref[...] = cf
        carry_ref[...] = jnp.where(is_f, cf[CHUNK - 1:CHUNK, :], 0.0)
        cs = jnp.where(is_dt, run, 0.0)
        cl = cs[CHUNK - 1:CHUNK, :]
        selt = selt_ref[...]
        dt_x = _dot_sel(dt, selt)
        e_x = _dot_sel(jnp.exp(cs), selt)
        w_x = _dot_sel(jnp.exp(cl - cs), selt)
        cdec_x = _dot_sel(jnp.broadcast_to(jnp.exp(cl), (8, LANES)), selt)[0:1, :]
        cs_t = cs.T

        xs = xc[:, :ds]
        xdt = xs * dt_x
        xdt_b = xdt.astype(BF16)
        xw_b = (xdt * w_x).astype(BF16)
        lane = lax.broadcasted_iota(jnp.int32, (1, LANES), 1)
        half0 = lane < HEAD_DIM
        for g in range(SSD_GROUPS):
            bg = xc[:, ds + g * ns: ds + (g + 1) * ns].astype(BF16)
            cg = xc[:, ds + SSD_GROUPS * ns + g * ns: ds + SSD_GROUPS * ns + (g + 1) * ns].astype(BF16)
            gm = _dot(cg, bg, NT)
            gs = slice(g * gw, (g + 1) * gw)
            stg = st_ref[:, gs]
            stg_b = stg.astype(BF16)
            hin_ref[0, :, gs] = stg_b
            yoff = _dot(cg, stg_b) * e_x[:, gs]
            for pr in range(gw // LANES):
                sl = slice(g * gw + pr * LANES, g * gw + (pr + 1) * LANES)
                xp = xdt_b[:, sl]
                yd = jnp.zeros((CHUNK, LANES), F32)
                for j in range(2):
                    h = g * hpg + 2 * pr + j
                    seg = cs[:, h:h + 1] - cs_t[h:h + 1, :]
                    m = jnp.where(causal, gm * jnp.exp(jnp.minimum(seg, 0.0)), 0.0).astype(BF16)
                    sel = half0 if j == 0 else jnp.logical_not(half0)
                    yd = yd + _dot(m, jnp.where(sel, xp, jnp.zeros_like(xp)))
                yacc_ref[:, sl] = yd + yoff[:, pr * LANES:(pr + 1) * LANES] + dsk_ref[:, sl] * xs[:, sl]
            st_ref[:, gs] = stg * cdec_x[:, gs] + _dot(bg, xw_b[:, gs], TN)

        y = yacc_ref[...]
        y_ref[...] = y.astype(BF16)
        zf = z_ref[...].astype(F32)
        u = y * zf * _sigmoid(zf)
        for g in range(SSD_GROUPS):
            gs = slice(g * gw, (g + 1) * gw)
            ug = u[:, gs]
            ms = jnp.mean(ug * ug, axis=-1, keepdims=True)
            yssd_ref[:, gs] = (ug * lax.rsqrt(ms + EPS) * nrm_ref[:, gs]).astype(BF16)

    rb = CHUNK // HALO
    return pl.pallas_call(
        body, name="ssd_fwd", grid=(nch,),
        in_specs=[pl.BlockSpec((CHUNK, cd), lambda c: (c, 0)),
                  pl.BlockSpec((HALO, cd), lambda c: (jnp.maximum(c * rb - 1, 0), 0)),
                  pl.BlockSpec((CHUNK, ds), lambda c: (c, 0)),
                  pl.BlockSpec((CHUNK, LANES), lambda c: (c, 0)),
                  _full((CONV_K, cd)), _full((1, cd)), _full((1, LANES)), _full((1, LANES)),
                  _full((1, ds)), _full((1, ds)), _full((LANES, ds))],
        out_specs=[pl.BlockSpec((CHUNK, ds), lambda c: (c, 0)), pl.BlockSpec((CHUNK, ds), lambda c: (c, 0)),
                   pl.BlockSpec((1, ns, ds), lambda c: (c, 0, 0)), pl.BlockSpec((CHUNK, LANES), lambda c: (c, 0))],
        out_shape=[jax.ShapeDtypeStruct((p, ds), BF16), jax.ShapeDtypeStruct((p, ds), BF16),
                   jax.ShapeDtypeStruct((nch, ns, ds), BF16), jax.ShapeDtypeStruct((p, LANES), F32)],
        scratch_shapes=[pltpu.VMEM((ns, ds), F32), pltpu.VMEM((1, LANES), F32), pltpu.VMEM((CHUNK, ds), F32)],
        compiler_params=_cparams(("arbitrary",)),
    )(xbc, xbc, z, dtf, conv_w, conv_b, brow, alog, dskip_l, ssd_norm, sel_t)


def _ssd_bwd(dyssd, y, z, xbc, dtf, hin, dcf, conv_w, conv_b, brow, alog, dskip_l, ssd_norm, sel_t, sel, hs, ha):
    p, cd = xbc.shape
    ds = z.shape[1]
    ns = (cd - ds) // (2 * SSD_GROUPS)
    gw = ds // SSD_GROUPS
    nch = p // CHUNK
    hpg = hs // SSD_GROUPS
    rb = CHUNK // HALO

    def body(dyssd_ref, y_ref, z_ref, xbc_ref, halo_ref, dtf_ref, hin_ref, dcf_ref, cw_ref, cb_ref, brow_ref,
             alog_ref, dsk_ref, nrm_ref, selt_ref, sel_ref,
             dxbc_ref, dz_ref, ddtf_ref, gcw_ref, gcb_ref, gnrm_ref, gsm_ref,
             dst_ref, nxt_ref, fcar_ref, gdsk_ref, dxc_ref):
        step = pl.program_id(0)
        c = nch - 1 - step

        @pl.when(step == 0)
        def _():
            dst_ref[...] = jnp.zeros_like(dst_ref)
            nxt_ref[...] = jnp.zeros_like(nxt_ref)
            fcar_ref[...] = jnp.zeros_like(fcar_ref)
            gdsk_ref[...] = jnp.zeros_like(gdsk_ref)
            gcw_ref[...] = jnp.zeros_like(gcw_ref)
            gcb_ref[...] = jnp.zeros_like(gcb_ref)
            gnrm_ref[...] = jnp.zeros_like(gnrm_ref)
            gsm_ref[...] = jnp.zeros_like(gsm_ref)

        rows = lax.broadcasted_iota(jnp.int32, (CHUNK, 1), 0)
        rowmask = jnp.where((rows >= PADN) | (c > 0), 1.0, 0.0)
        ri, ci = _tri_mats()
        causal = ri >= ci
        anti = ci >= ri
        tri = jnp.where(causal, 1.0, 0.0).astype(BF16)
        rtri = jnp.where(anti, 1.0, 0.0).astype(BF16)

        xr = xbc_ref[...].astype(F32)
        halo = halo_ref[...].astype(F32) * jnp.where(c > 0, 1.0, 0.0)
        halo128 = jnp.concatenate([jnp.zeros((CHUNK - HALO, cd), F32), halo], axis=0)
        pre, shifted = _conv_pre(xr, halo128, cw_ref, cb_ref, rows)
        sg = _sigmoid(pre)
        xc = pre * sg * rowmask
        dsilu = sg * (1.0 + pre * (1.0 - sg)) * rowmask

        dtr, dt, a_row, run, is_dt, is_f = _ssd_scalars(dtf_ref, brow_ref, alog_ref, rowmask, hs, ha, tri)
        cs = jnp.where(is_dt, run, 0.0)
        cl = cs[CHUNK - 1:CHUNK, :]
        selt = selt_ref[...]
        selm = sel_ref[...]
        dt_x = _dot_sel(dt, selt)
        e_x = _dot_sel(jnp.exp(cs), selt)
        w_x = _dot_sel(jnp.exp(cl - cs), selt)
        cdec = jnp.exp(cl)
        cdec_x = _dot_sel(jnp.broadcast_to(cdec, (8, LANES)), selt)[0:1, :]
        cs_t = cs.T
        xs = xc[:, :ds]
        xdt = xs * dt_x
        xdt_b = xdt.astype(BF16)
        xw_b = (xdt * w_x).astype(BF16)

        yv = y_ref[...].astype(F32)
        zf = z_ref[...].astype(F32)
        sz = _sigmoid(zf)
        u = yv * zf * sz
        dyo = dyssd_ref[...].astype(F32)
        du_parts = []
        for g in range(SSD_GROUPS):
            gs = slice(g * gw, (g + 1) * gw)
            ug = u[:, gs]
            rstd = lax.rsqrt(jnp.mean(ug * ug, axis=-1, keepdims=True) + EPS)
            yhat = ug * rstd
            dyg = dyo[:, gs]
            gnrm_ref[0:1, gs] += jnp.sum(dyg * yhat, axis=0, keepdims=True)
            dyh = dyg * nrm_ref[:, gs]
            du_parts.append(rstd * (dyh - yhat * jnp.mean(dyh * yhat, axis=-1, keepdims=True)))
        du = jnp.concatenate(du_parts, axis=1)
        dy = du * zf * sz
        dz_ref[...] = (du * yv * sz * (1.0 + zf * (1.0 - sz))).astype(BF16)

        dsk = dsk_ref[...]
        gdsk_ref[...] += jnp.sum(dy * xs, axis=0, keepdims=True)
        dy_b = dy.astype(BF16)
        dye_b = (dy * e_x).astype(BF16)
        lane = lax.broadcasted_iota(jnp.int32, (1, LANES), 1)
        half0 = lane < HEAD_DIM
        x_parts, yo_parts, t4_parts = [], [], []
        dcs = jnp.zeros((CHUNK, LANES), F32)
        for g in range(SSD_GROUPS):
            gs = slice(g * gw, (g + 1) * gw)
            bsl = slice(ds + g * ns, ds + (g + 1) * ns)
            csl = slice(ds + SSD_GROUPS * ns + g * ns, ds + SSD_GROUPS * ns + (g + 1) * ns)
            bg = xc[:, bsl].astype(BF16)
            cg = xc[:, csl].astype(BF16)
            gm = _dot(cg, bg, NT)
            gm_t = _dot(bg, cg, NT)
            stg_b = hin_ref[0, :, gs]
            dstg = dst_ref[:, gs]
            dstg_b = dstg.astype(BF16)
            t4_parts.append(jnp.sum(dstg * stg_b.astype(F32), axis=0, keepdims=True))
            zst = _dot(bg, dstg_b) * w_x[:, gs]
            x_parts.append(xdt[:, gs] * zst)
            yo_parts.append(dy[:, gs] * (_dot(cg, stg_b) * e_x[:, gs]))
            dgsum = jnp.zeros((CHUNK, CHUNK), F32)
            dgtsum = jnp.zeros((CHUNK, CHUNK), F32)
            for pr in range(gw // LANES):
                sl = slice(g * gw + pr * LANES, g * gw + (pr + 1) * LANES)
                xp = xdt_b[:, sl]
                dyp = dy_b[:, sl]
                dxd = zst[:, pr * LANES:(pr + 1) * LANES]
                for j in range(2):
                    h = g * hpg + 2 * pr + j
                    sel_l = half0 if j == 0 else jnp.logical_not(half0)
                    seg = cs[:, h:h + 1] - cs_t[h:h + 1, :]
                    lm = jnp.where(causal, jnp.exp(jnp.minimum(seg, 0.0)), 0.0)
                    lmt = jnp.where(anti, jnp.exp(jnp.minimum(-seg, 0.0)), 0.0)
                    dyp_m = jnp.where(sel_l, dyp, jnp.zeros_like(dyp))
                    xp_m = jnp.where(sel_l, xp, jnp.zeros_like(xp))
                    dxd = dxd + _dot((gm_t * lmt).astype(BF16), dyp_m)
                    dg = _dot(dyp_m, xp, NT) * lm
                    dgt = _dot(xp_m, dyp, NT) * lmt
                    dgsum = dgsum + dg
                    dgtsum = dgtsum + dgt
                    qrow = (jnp.sum(dg * gm, axis=1, keepdims=True) - jnp.sum(dgt * gm_t, axis=1, keepdims=True))
                    dcs = dcs + jnp.where(lane == h, qrow, 0.0)
                dxc_ref[:, sl] = dxd
            dxc_ref[:, csl] = _dot(dgsum.astype(BF16), bg) + _dot(dye_b[:, gs], stg_b, NT)
            dxc_ref[:, bsl] = _dot(dgtsum.astype(BF16), cg) + _dot(xw_b[:, gs], dstg_b, NT)
            dst_ref[:, gs] = dstg * cdec_x[:, gs] + _dot(cg, dye_b[:, gs], TN)

        dxdt = dxc_ref[:, :ds]
        xst = _dot_sel(jnp.concatenate(x_parts, axis=1), selm)
        yo = _dot_sel(jnp.concatenate(yo_parts, axis=1), selm)
        t4 = _dot_sel(jnp.concatenate([jnp.concatenate(t4_parts, axis=1), jnp.zeros((7, ds), F32)], axis=0), selm)
        dcl = jnp.sum(xst, axis=0, keepdims=True) + cdec * t4[0:1, :]
        dcs = dcs + yo - xst + jnp.where(rows == CHUNK - 1, dcl, 0.0)
        da_ = _dot_tri(rtri, dcs)
        ddt = _dot_sel(dxdt * xs, selm) + da_ * a_row
        dcf_blk = dcf_ref[...]
        dlogf = _dot_tri(rtri, dcf_blk) + fcar_ref[...]
        fcar_ref[...] += jnp.sum(dcf_blk, axis=0, keepdims=True)
        sgd = _sigmoid(dtr)
        ddtf = (jnp.where(is_dt, ddt * sgd, 0.0) + jnp.where(is_f, dlogf * (1.0 - sgd), 0.0)) * rowmask
        ddtf_ref[...] = ddtf
        gsm_ref[0:1, :] += jnp.sum(ddtf, axis=0, keepdims=True)
        gsm_ref[1:2, :] += jnp.sum(da_ * dt, axis=0, keepdims=True) * a_row

        dxc_ref[:, :ds] = dxdt * dt_x + dsk * dy
        dpre = dxc_ref[...] * dsilu
        gcb_ref[0:1, :] += jnp.sum(dpre, axis=0, keepdims=True)
        gcw_ref[CONV_K - 1:CONV_K, :] += jnp.sum(dpre * xr, axis=0, keepdims=True)
        nxt128 = jnp.concatenate([nxt_ref[...], jnp.zeros((CHUNK - 8, cd), F32)], axis=0)
        dxr = cw_ref[CONV_K - 1:CONV_K, :] * dpre
        for j in range(1, CONV_K):
            gcw_ref[CONV_K - 1 - j:CONV_K - j, :] += jnp.sum(dpre * shifted[j - 1], axis=0, keepdims=True)
            up = jnp.where(rows < CHUNK - j, pltpu.roll(dpre, CHUNK - j, 0), pltpu.roll(nxt128, CHUNK - j, 0))
            dxr = dxr + cw_ref[CONV_K - 1 - j:CONV_K - j, :] * up
        nxt_ref[...] = dpre[0:8, :]
        dxbc_ref[...] = dxr.astype(BF16)

        @pl.when(step == nch - 1)
        def _():
            gsm_ref[2:3, :] = _dot_sel(jnp.broadcast_to(gdsk_ref[...], (8, ds)), selm)[0:1, :]

    rev = lambda s: nch - 1 - s
    blk = lambda w: pl.BlockSpec((CHUNK, w), lambda s: (rev(s), 0))
    return pl.pallas_call(
        body, name="ssd_bwd", grid=(nch,),
        in_specs=[blk(ds), blk(ds), blk(ds), blk(cd),
                  pl.BlockSpec((HALO, cd), lambda s: (jnp.maximum(rev(s) * rb - 1, 0), 0)),
                  blk(LANES), pl.BlockSpec((1, ns, ds), lambda s: (rev(s), 0, 0)), blk(LANES),
                  _full((CONV_K, cd)), _full((1, cd)), _full((1, LANES)), _full((1, LANES)),
                  _full((1, ds)), _full((1, ds)), _full((LANES, ds)), _full((ds, LANES))],
        out_specs=[blk(cd), blk(ds), blk(LANES), _full((8, cd)), _full((8, cd)), _full((8, ds)), _full((8, LANES))],
        out_shape=[jax.ShapeDtypeStruct((p, cd), BF16), jax.ShapeDtypeStruct((p, ds), BF16),
                   jax.ShapeDtypeStruct((p, LANES), F32), jax.ShapeDtypeStruct((8, cd), F32),
                   jax.ShapeDtypeStruct((8, cd), F32), jax.ShapeDtypeStruct((8, ds), F32),
                   jax.ShapeDtypeStruct((8, LANES), F32)],
        scratch_shapes=[pltpu.VMEM((ns, ds), F32), pltpu.VMEM((8, cd), F32), pltpu.VMEM((1, LANES), F32),
                        pltpu.VMEM((1, ds), F32), pltpu.VMEM((CHUNK, cd), F32)],
        compiler_params=_cparams(("arbitrary",)),
    )(dyssd, y, z, xbc, xbc, dtf, hin, dcf, conv_w, conv_b, brow, alog, dskip_l, ssd_norm, sel_t, sel)


def _attn_fwd(q, k, v, ck, blk):
    p, da = q.shape
    npair, nkb = ck.shape[0], ck.shape[1]
    scale = 1.0 / math.sqrt(HEAD_DIM)

    def body(q_ref, k_ref, v_ref, ck_ref, o_ref, lse_ref):
        i = pl.program_id(1)
        lane = lax.broadcasted_iota(jnp.int32, (1, LANES), 1)
        sels = [lane < HEAD_DIM, lane >= HEAD_DIM]
        ones = [jnp.where(lane == HEAD_DIM, 1.0, 0.0).astype(BF16), jnp.where(lane == 0, 1.0, 0.0).astype(BF16)]
        qb = q_ref[...] * scale
        qms = [jnp.where(sel, qb, jnp.zeros_like(qb)) for sel in sels]
        cmask = (lax.broadcasted_iota(jnp.int32, (blk, blk), 1) <= lax.broadcasted_iota(jnp.int32, (blk, blk), 0))

        def step(kb, carry, masked):
            r0 = pl.multiple_of(kb * blk, blk)
            ks = k_ref[pl.ds(r0, blk), :]
            vs = v_ref[pl.ds(r0, blk), :]
            out = []
            for j in range(2):
                m, acc = carry[2 * j], carry[2 * j + 1]
                s = _dot(qms[j], ks, NT) - ck_ref[0, kb, j:j + 1, :]
                if masked:
                    s = jnp.where(cmask, s, NEG)
                mn = jnp.maximum(m, jnp.max(s, axis=-1, keepdims=True))
                pr = jnp.exp(s - mn).astype(BF16)
                acc = jnp.exp(m - mn) * acc + _dot(pr, jnp.where(sels[j], vs, ones[j]))
                out += [mn, acc]
            return tuple(out)

        init = (jnp.full((blk, 1), NEG, F32), jnp.zeros((blk, LANES), F32)) * 2
        carry = lax.fori_loop(0, i, lambda kb, c: step(kb, c, False), init)
        m0, a0, m1, a1 = step(i, carry, True)
        l0 = a0[:, HEAD_DIM:HEAD_DIM + 1]
        l1 = a1[:, 0:1]
        o_ref[...] = jnp.where(sels[0], a0 / l0, a1 / l1).astype(BF16)
        lse_ref[...] = jnp.where(sels[0], m0 + jnp.log(l0), m1 + jnp.log(l1))

    return pl.pallas_call(
        body, name="attn_fwd", grid=(npair, p // blk),
        in_specs=[pl.BlockSpec((blk, LANES), lambda h, i: (i, h)),
                  pl.BlockSpec((p, LANES), lambda h, i: (0, h)), pl.BlockSpec((p, LANES), lambda h, i: (0, h)),
                  pl.BlockSpec((1, nkb, 8, blk), lambda h, i: (h, 0, 0, 0))],
        out_specs=[pl.BlockSpec((blk, LANES), lambda h, i: (i, h)), pl.BlockSpec((blk, LANES), lambda h, i: (i, h))],
        out_shape=[jax.ShapeDtypeStruct((p, da), BF16), jax.ShapeDtypeStruct((p, da), F32)],
        compiler_params=_cparams(("parallel", "arbitrary")),
    )(q, k, v, ck)


def _attn_bwd(q, k, v, o, do, lse_rep, ck, blk):
    p, da = q.shape
    npair, nkb = ck.shape[0], ck.shape[1]
    nq = p // blk
    scale = 1.0 / math.sqrt(HEAD_DIM)

    def body(k_ref, v_ref, q_ref, do_ref, o_ref, lse_ref, ck_ref, dk_ref, dv_ref, dq_ref, dcs_ref, rsum_ref, dq_acc):
        jb = pl.program_id(1)

        @pl.when(jb == 0)
        def _():
            dq_acc[...] = jnp.zeros_like(dq_acc)

        ks = k_ref[...]
        vs = v_ref[...]
        lane = lax.broadcasted_iota(jnp.int32, (1, LANES), 1)
        sels = [lane < HEAD_DIM, lane >= HEAD_DIM]
        ones = [jnp.where(lane == HEAD_DIM, 1.0, 0.0).astype(BF16), jnp.where(lane == 0, 1.0, 0.0).astype(BF16)]
        kss = ks * scale
        kmo = [jnp.where(sels[j], kss, ones[j]) for j in range(2)]
        cmask = (lax.broadcasted_iota(jnp.int32, (blk, blk), 1) <= lax.broadcasted_iota(jnp.int32, (blk, blk), 0))

        def step(ib, carry, masked):
            r0 = pl.multiple_of(ib * blk, blk)
            qb = q_ref[pl.ds(r0, blk), :] * scale
            dob = do_ref[pl.ds(r0, blk), :]
            prod = dob.astype(F32) * o_ref[pl.ds(r0, blk), :].astype(F32)
            out = []
            for j in range(2):
                dk, dv = carry[2 * j], carry[2 * j + 1]
                qm = jnp.where(sels[j], qb, jnp.zeros_like(qb))
                dom = jnp.where(sels[j], dob, jnp.zeros_like(dob))
                lse = lse_ref[pl.ds(r0, blk), HEAD_DIM * j:HEAD_DIM * j + 1]
                dlt = jnp.sum(jnp.where(sels[j], prod, 0.0), axis=-1, keepdims=True)
                s = _dot(qm, ks, NT) - ck_ref[0, 0, j:j + 1, :] - lse
                pm = jnp.exp(jnp.minimum(s, 0.0))
                if masked:
                    pm = jnp.where(cmask, pm, 0.0)
                ds_b = (pm * (_dot(dom, vs, NT) - dlt)).astype(BF16)
                dv = dv + _dot(pm.astype(BF16), dom, TN)
                dk = dk + _dot(ds_b, jnp.where(sels[j], qb, ones[j]), TN)
                dq_acc[pl.ds(r0, blk), LANES * j:LANES * (j + 1)] += _dot(ds_b, kmo[j])
                out += [dk, dv]
            return tuple(out)

        zero = jnp.zeros((blk, LANES), F32)
        carry = step(jb, (zero, zero, zero, zero), True)
        dk0, dv0, dk1, dv1 = lax.fori_loop(jb + 1, nq, lambda ib, c: step(ib, c, False), carry)
        dk_ref[...] = jnp.where(sels[0], dk0, dk1).astype(BF16)
        dv_ref[...] = (dv0 + dv1).astype(BF16)
        lane8 = lax.broadcasted_iota(jnp.int32, (1, 8), 1)
        pair8 = lambda c0, c1: jnp.where(lane8 == 0, c0, jnp.where(lane8 == 1, c1, 0.0))
        dcs_ref[0] = pair8(dk0[:, HEAD_DIM:HEAD_DIM + 1], dk1[:, 0:1])

        @pl.when(jb == nkb - 1)
        def _():
            a0 = dq_acc[:, :LANES]
            a1 = dq_acc[:, LANES:]
            dq_ref[...] = jnp.where(sels[0], a0, a1).astype(BF16)
            rsum_ref[0] = pair8(a0[:, HEAD_DIM:HEAD_DIM + 1], a1[:, 0:1])

    colblk = pl.BlockSpec((blk, LANES), lambda h, j: (j, h))
    colfull = pl.BlockSpec((p, LANES), lambda h, j: (0, h))
    ckspec = pl.BlockSpec((1, 1, 8, blk), lambda h, j: (h, j, 0, 0))
    return pl.pallas_call(
        body, name="attn_bwd", grid=(npair, nkb),
        in_specs=[colblk, colblk, colfull, colfull, colfull, colfull, ckspec],
        out_specs=[colblk, colblk, colfull, pl.BlockSpec((1, blk, 8), lambda h, j: (h, j, 0)),
                   pl.BlockSpec((1, p, 8), lambda h, j: (h, 0, 0))],
        out_shape=[jax.ShapeDtypeStruct((p, da), BF16), jax.ShapeDtypeStruct((p, da), BF16),
                   jax.ShapeDtypeStruct((p, da), BF16), jax.ShapeDtypeStruct((npair, p, 8), F32),
                   jax.ShapeDtypeStruct((npair, p, 8), F32)],
        scratch_shapes=[pltpu.VMEM((p, 2 * LANES), F32)],
        compiler_params=_cparams(("parallel", "arbitrary")),
    )(k, v, q, do, o, lse_rep, ck)


def _tail(yssd, o, zatt, graw, head, x2, tgt2, wps, wpa, wout, gate_bias, norm_post):
    p, ds = yssd.shape
    da = o.shape[1]
    d = x2.shape[1]

    def body(yssd_ref, o_ref, zatt_ref, g_ref, head_ref, x_ref, tgt_ref, wps_ref, wpa_ref, wout_ref, gb_ref, np_ref,
             dyssd_ref, do_ref, dzatt_ref, dg_ref, dzo_ref, mrg_ref, da_ref, db_ref, yatt_ref, dout_ref, red_ref):
        i = pl.program_id(0)

        @pl.when(i == 0)
        def _():
            red_ref[...] = jnp.zeros_like(red_ref)

        h = jnp.where(i == 0, head_ref[...], x_ref[...])
        valid = jnp.where(i > 0, 1.0, 0.0)
        ob = o_ref[...].astype(F32)
        za = zatt_ref[...].astype(F32)
        sza = _sigmoid(za)
        silu = za * sza
        yatt_b = (ob * silu).astype(BF16)
        yatt_ref[...] = yatt_b
        wps_v, wpa_v, wout_v = wps_ref[...], wpa_ref[...], wout_ref[...]
        a = _dot(yssd_ref[...], wps_v)
        b = _dot(yatt_b, wpa_v)
        gr = g_ref[...].astype(F32) + gb_ref[...]
        gs = _sigmoid(gr[:, :d])
        ga = _sigmoid(gr[:, d:])
        mrg_b = (gs * a + ga * b).astype(BF16)
        mrg_ref[...] = mrg_b
        zo = _dot(mrg_b, wout_v)
        rstd = lax.rsqrt(jnp.mean(zo * zo, axis=-1, keepdims=True) + EPS)
        zh = zo * rstd
        npw = np_ref[...]
        err = (h + zh * npw - tgt_ref[...]) * valid
        dout = err * (1.0 / d)
        dout_ref[...] = dout
        dzh = dout * npw
        dzo_b = (rstd * (dzh - zh * jnp.mean(dzh * zh, axis=-1, keepdims=True))).astype(BF16)
        dzo_ref[...] = dzo_b
        dm = _dot(dzo_b, wout_v, NT)
        da_b = (gs * dm).astype(BF16)
        db_b = (ga * dm).astype(BF16)
        da_ref[...] = da_b
        db_ref[...] = db_b
        dgs = dm * a * gs * (1.0 - gs)
        dga = dm * b * ga * (1.0 - ga)
        dg_ref[:, :d] = dgs.astype(BF16)
        dg_ref[:, d:] = dga.astype(BF16)
        dyssd_ref[...] = _dot(da_b, wps_v, NT).astype(BF16)
        dya = _dot(db_b, wpa_v, NT)
        do_ref[...] = (dya * silu).astype(BF16)
        dzatt_ref[...] = (dya * ob * sza * (1.0 + za * (1.0 - sza))).astype(BF16)
        red_ref[0:1, :d] += jnp.sum(dout * zh, axis=0, keepdims=True)
        red_ref[1:2, :d] += jnp.sum(dgs, axis=0, keepdims=True)
        red_ref[1:2, d:] += jnp.sum(dga, axis=0, keepdims=True)
        red_ref[2:3, 0:1] += jnp.sum(jnp.sum(err * err, axis=1, keepdims=True), axis=0, keepdims=True) * (0.5 / d)

    row = lambda w: pl.BlockSpec((CHUNK, w), lambda i: (i, 0))
    shifted = lambda w: pl.BlockSpec((CHUNK, w), lambda i: (jnp.maximum(i - 1, 0), 0))
    sd = jax.ShapeDtypeStruct
    return pl.pallas_call(
        body, name="tail", grid=(p // CHUNK,),
        in_specs=[row(ds), row(da), row(da), row(2 * d), _full((CHUNK, d)), shifted(d), shifted(d),
                  _full((ds, d)), _full((da, d)), _full((d, d)), _full((1, 2 * d)), _full((1, d))],
        out_specs=[row(ds), row(da), row(da), row(2 * d), row(d), row(d), row(d), row(d), row(da), row(d),
                   _full((8, 2 * d))],
        out_shape=[sd((p, ds), BF16), sd((p, da), BF16), sd((p, da), BF16), sd((p, 2 * d), BF16), sd((p, d), BF16),
                   sd((p, d), BF16), sd((p, d), BF16), sd((p, d), BF16), sd((p, da), BF16), sd((p, d), F32),
                   sd((8, 2 * d), F32)],
        compiler_params=_cparams(("arbitrary",)),
    )(yssd, o, zatt, graw, head, x2, tgt2, wps, wpa, wout, gate_bias, norm_post)


def _adamw_math(w, g, m, v):
    m2 = ADAM_B1 * m + (1.0 - ADAM_B1) * g
    v2 = ADAM_B2 * v + (1.0 - ADAM_B2) * (g * g)
    m_hat = m2 / (1.0 - ADAM_B1 ** ADAM_STEP)
    v_hat = v2 / (1.0 - ADAM_B2 ** ADAM_STEP)
    delta = -ADAM_LR * (m_hat / (jnp.sqrt(v_hat) + ADAM_EPS) + ADAM_WD * w)
    return delta, m2, v2


def _adamw(w, g, m, v, name, parts=False):
    r, cdim = w.shape
    tr, tc, by_rows = _tiles_2d(r, cdim)
    pick = (lambda i: (i, 0)) if by_rows else (lambda i: (0, i))

    def body(w_ref, g_ref, m_ref, v_ref, go_ref, d_ref, mo_ref, vo_ref):
        if parts:
            g = g_ref[0].astype(F32)
            for s in range(1, g_ref.shape[0]):
                g = g + g_ref[s].astype(F32)
        else:
            g = g_ref[...]
        delta, m2, v2 = _adamw_math(w_ref[...], g, m_ref[...], v_ref[...])
        go_ref[...] = g
        d_ref[...] = delta
        mo_ref[...] = m2
        vo_ref[...] = v2

    blk = pl.BlockSpec((tr, tc), pick)
    gspec = pl.BlockSpec((g.shape[0], tr, tc), lambda i: (0,) + pick(i)) if parts else blk
    return pl.pallas_call(
        body, name=name, grid=((r // tr) * (cdim // tc),),
        in_specs=[blk, gspec, blk, blk], out_specs=[blk] * 4,
        out_shape=[jax.ShapeDtypeStruct((r, cdim), F32)] * 4,
        compiler_params=_cparams(("parallel",)),
    )(w, g, m, v)


def _pad_cols(a, width):
    return jnp.pad(a, ((0, 0), (0, width - a.shape[1])))


def _pack_small_shard(conv_w_sh, meta_sh, width):
    return jnp.concatenate([_pad_cols(conv_w_sh, width), jnp.zeros((4, width), F32), _pad_cols(meta_sh, width)], axis=0)


def _pack_small_rep(norm_pre, norm_post, gate_bias, ssd_norm, conv_b, misc, width):
    rows = [norm_pre, norm_post, gate_bias, ssd_norm, conv_b, misc]
    return jnp.concatenate([_pad_cols(r, width) for r in rows] + [jnp.zeros((2, width), F32)], axis=0)


def _misc_row(dt_bias, fgate_bias, a_log, d_skip, extra):
    hs, ha = dt_bias.shape[1], fgate_bias.shape[1]
    return jnp.concatenate([dt_bias, fgate_bias, jnp.zeros((1, LANES - hs - ha), F32), _pad_cols(a_log, LANES),
                            _pad_cols(d_skip, LANES), _pad_cols(extra, LANES)], axis=1)


def kernel(x, meta_tokens, norm_pre, w_in, conv_w, conv_b, dt_bias, a_log, d_skip, ssd_norm, fgate_bias, gate_bias, w_proj_ssd, w_proj_att, w_out, norm_post, loss_target, m_meta_tokens, m_norm_pre, m_w_in, m_conv_w, m_conv_b, m_dt_bias, m_a_log, m_d_skip, m_ssd_norm, m_fgate_bias, m_gate_bias, m_w_proj_ssd, m_w_proj_att, m_w_out, m_norm_post, v_meta_tokens, v_norm_pre, v_w_in, v_conv_w, v_conv_b, v_dt_bias, v_a_log, v_d_skip, v_ssd_norm, v_fgate_bias, v_gate_bias, v_w_proj_ssd, v_w_proj_att, v_w_out, v_norm_post):
    seq, d = x.shape[1], x.shape[2]
    p = seq + CHUNK
    hs, ha = dt_bias.shape[1], fgate_bias.shape[1]
    ds, cd = ssd_norm.shape[1], conv_b.shape[1]
    da = ha * HEAD_DIM
    nc8 = w_in.shape[2]
    cws = cd // N_DEV
    msh = d // N_DEV
    r1, r2, r3 = ds // N_DEV, da // N_DEV, d // N_DEV
    me = _dev_index(*_my_pos())
    x2, tgt2 = x[0], loss_target[0]

    win_sh = jnp.transpose(w_in[0]).astype(BF16)
    rows_sh = jnp.concatenate([w_proj_ssd[0], w_proj_att[0], w_out[0]], axis=0).astype(BF16)
    small_sh = _pack_small_shard(conv_w[0], meta_tokens, cws)
    win_all, small_all = _all_gather([win_sh, small_sh], "gather_weights")
    rows_sh, win_all = lax.optimization_barrier((rows_sh, win_all))
    rows_sems, rows_thru, rows_land, rows_token = _bcast_start(rows_sh, "gather_rows_start")
    w_full = win_all.reshape(N_DEV * nc8, d)
    cuts = [0, ds, ds + cd, ds + cd + hs, ds + cd + hs + da, ds + cd + hs + 2 * da, ds + cd + hs + 3 * da,
            ds + cd + hs + 4 * da, ds + cd + hs + 4 * da + ha, ds + cd + hs + 4 * da + ha + 2 * d]
    w_z, w_xbc, w_dt, w_zatt, w_q, w_k, w_v, w_f, w_g = [w_full[cuts[i]:cuts[i + 1]] for i in range(9)]
    w_dtf = jnp.concatenate([w_dt, w_f, jnp.zeros((LANES - hs - ha, d), BF16)], axis=0)
    conv_w_full = jnp.transpose(small_all[:, 0:CONV_K, :], (1, 0, 2)).reshape(CONV_K, cd)
    meta_full = jnp.transpose(small_all[:, 8:8 + N_META, :msh], (1, 0, 2)).reshape(N_META, d)
    head = jnp.concatenate([jnp.zeros((PADN, d), F32), meta_full + rows_token[0:1, 0:1]], axis=0)

    u = _prenorm_fwd(head, x2, norm_pre)
    tm = _att_block(p)
    seg_w = [w_z, w_xbc, w_zatt, w_q, w_k, w_v, w_g]
    zs, xbc, zatt, q, k, v, graw = [
        _mm(u, w, "nt", BF16, tm, _tile(w.shape[0], (1024, 512, 256, 128)), "inproj_%d" % i) for i, w in enumerate(seg_w)]
    dtf = _mm(u, w_dtf, "nt", F32, tm, LANES, "inproj_dtf")

    brow = jnp.concatenate([dt_bias, fgate_bias, jnp.zeros((1, LANES - hs - ha), F32)], axis=1)
    alog_row = _pad_cols(a_log, LANES)
    dskip_l = jnp.repeat(d_skip, HEAD_DIM, axis=1)
    sel_t = (lax.broadcasted_iota(jnp.int32, (LANES, ds), 1) // HEAD_DIM
             == lax.broadcasted_iota(jnp.int32, (LANES, ds), 0)).astype(BF16)
    sel = sel_t.T
    y, yssd, hin, cf = _ssd_fwd(xbc, zs, dtf, conv_w_full, conv_b, brow, alog_row, dskip_l, ssd_norm, sel_t, hs, ha)

    blk = _att_block(p)
    nkb, npair = p // blk, ha // 2
    cum = jnp.where(lax.broadcasted_iota(jnp.int32, (p, 1), 0) < PADN, -NEG, cf[:, hs:hs + ha])
    ck = jnp.transpose(cum.T.reshape(npair, 2, nkb, blk), (0, 2, 1, 3))
    ck = jnp.pad(ck, ((0, 0), (0, 0), (0, 6), (0, 0)))
    o, lse_rep = _attn_fwd(q, k, v, ck, blk)

    rows_all = _bcast_wait(rows_sems, rows_thru, rows_land, lse_rep, "gather_rows_wait")
    wps = rows_all[:, :r1].reshape(ds, d)
    wpa = rows_all[:, r1:r1 + r2].reshape(da, d)
    wout = rows_all[:, r1 + r2:].reshape(d, d)

    (dyssd, d_o, dzatt, dgraw, dzo, mrg, da_, db_, yatt, dout, red_tail) = _tail(
        yssd, o, zatt, graw, head, x2, tgt2, wps, wpa, wout, gate_bias, norm_post)

    tw = _tile(d, (512, 256, 128))
    g_wout = _mm(mrg, dzo, "tn", BF16, tw, tw, "wgrad_out")
    g_wps = _mm(yssd, da_, "tn", BF16, _tile(ds, (512, 256, 128)), tw, "wgrad_ps")
    g_wpa = _mm(yatt, db_, "tn", BF16, _tile(da, (512, 256, 128)), tw, "wgrad_pa")

    dk, dv, dq, dcs, rsum = _attn_bwd(q, k, v, o, d_o, lse_rep, ck, blk)
    dcum = jnp.transpose((rsum - dcs)[:, :, 0:2], (1, 0, 2)).reshape(p, ha)
    dcf = jnp.pad(dcum, ((0, 0), (hs, LANES - hs - ha)))
    dxbc, dzs, ddtf, gcw, gcb, gnrm, gsm = _ssd_bwd(
        dyssd, y, zs, xbc, dtf, hin, dcf, conv_w_full, conv_b, brow, alog_row, dskip_l, ssd_norm, sel_t, sel, hs, ha)
    ddtf_b = ddtf.astype(BF16)

    dsegs = [dzs, dxbc, dzatt, dq, dk, dv, dgraw, ddtf_b]
    gsegs = [_mm(dsg, u, "tn", BF16, _tile(dsg.shape[1], (512, 256, 128)), tw, "wgrad_in_%d" % i)
             for i, dsg in enumerate(dsegs)]
    g_z, g_xbc, g_zatt, g_q, g_k, g_v, g_g, g_dtf = gsegs
    gw_full = jnp.concatenate([g_z, g_xbc, g_dtf[:hs], g_zatt, g_q, g_k, g_v, g_dtf[hs:hs + ha], g_g], axis=0)
    gwin_parts = gw_full.reshape(N_DEV, nc8, d)
    grows_parts = jnp.concatenate([g_wps.reshape(N_DEV, r1, d), g_wpa.reshape(N_DEV, r2, d),
                                   g_wout.reshape(N_DEV, r3, d)], axis=1)

    core = lax.axis_index("c").astype(jnp.int32).reshape(1)
    sib_win, sib_rows = _exchange_sibling([gwin_parts, grows_parts], "scatter_grads_sibling")
    chip_win = _pair_add(gwin_parts, sib_win, core, "pair_add_w_in")
    chip_rows = _pair_add(grows_parts, sib_rows, core, "pair_add_rows")
    sems, thru, lands, token = _exchange_chips_start([chip_win, chip_rows], "scatter_grads_start")
    dsegs_after = dsegs[:-1] + [ddtf_b + token[0:1, 0:1].astype(BF16)]
    du = _mm_sum_nn(dsegs_after, seg_w + [w_dtf], tm, _tile(d, (256, 128)), "dgrad_in")
    gx, ghead, gnp = _prenorm_bwd(head, x2, norm_pre, du, dout)
    sent, got = _exchange_chips_wait(sems, thru, lands, gnp, "scatter_grads_wait")
    chip = me // 2
    recv_win, recv_rows = [lax.dynamic_update_slice_in_dim(g, lax.dynamic_slice_in_dim(s, chip, 1, axis=0), chip, axis=0)
                           for g, s in zip(got, sent)]
    gmisc = jnp.concatenate([gsm[0:1], gsm[1:2], gsm[2:3], _pad_cols(red_tail[2:3, 0:1], LANES)], axis=1)
    small_g = jnp.concatenate([
        _pack_small_rep(gnp[0:1], red_tail[0:1, :d], red_tail[1:2], gnrm[0:1], gcb[0:1], gmisc, cd),
        _pad_cols(gcw[0:CONV_K], cd), jnp.zeros((4, cd), F32), _pad_cols(ghead[PADN:], cd)], axis=0)
    sg_sems, sg_thru, sg_land, sg_token = _bcast_start(small_g, "reduce_small_start")

    zero1 = jnp.zeros((1, 1), F32)
    upd_in = _adamw(jnp.transpose(w_in[0]) + sg_token[0:1, 0:1], recv_win, jnp.transpose(m_w_in[0]),
                    jnp.transpose(v_w_in[0]), "adamw_w_in", parts=True)
    cat3 = lambda a, b, c: jnp.concatenate([a[0], b[0], c[0]], axis=0)
    upd_rows = _adamw(cat3(w_proj_ssd, w_proj_att, w_out) + sg_token[0:1, 0:1], recv_rows,
                      cat3(m_w_proj_ssd, m_w_proj_att, m_w_out),
                      cat3(v_w_proj_ssd, v_w_proj_att, v_w_out), "adamw_rows", parts=True)
    both_done = upd_in[1][0:8, 0:LANES] + upd_rows[1][0:8, 0:LANES]
    red = _sum_slots(_bcast_wait(sg_sems, sg_thru, sg_land, both_done, "reduce_small_wait"), "reduce_small_sum")
    loss = red[5, 3 * LANES]
    g_small_sh = _pack_small_shard(lax.dynamic_slice_in_dim(red[8:8 + CONV_K], me * cws, cws, axis=1),
                                   lax.dynamic_slice_in_dim(red[16:16 + N_META, :d], me * msh, msh, axis=1), cws)
    rep = lambda a, b, c, e, f, g1, g2, g3, g4: _pack_small_rep(a, b, c, e, f, _misc_row(g1, g2, g3, g4, zero1), cd)
    upd_rep = _adamw(rep(norm_pre, norm_post, gate_bias, ssd_norm, conv_b, dt_bias, fgate_bias, a_log, d_skip),
                     red[0:8],
                     rep(m_norm_pre, m_norm_post, m_gate_bias, m_ssd_norm, m_conv_b, m_dt_bias, m_fgate_bias, m_a_log, m_d_skip),
                     rep(v_norm_pre, v_norm_post, v_gate_bias, v_ssd_norm, v_conv_b, v_dt_bias, v_fgate_bias, v_a_log, v_d_skip),
                     "adamw_rep")
    upd_sh = _adamw(small_sh, g_small_sh, _pack_small_shard(m_conv_w[0], m_meta_tokens, cws),
                    _pack_small_shard(v_conv_w[0], v_meta_tokens, cws), "adamw_small_shard")

    def leaves(i):
        a_in, a_rows, a_rep, a_sh = upd_in[i], upd_rows[i], upd_rep[i], upd_sh[i]
        misc = a_rep[5:6]
        return [a_sh[8:8 + N_META, :msh], a_rep[0:1, :d], jnp.transpose(a_in)[None], a_sh[0:CONV_K][None], a_rep[4:5, :cd],
                misc[:, :hs], misc[:, LANES:LANES + hs], misc[:, 2 * LANES:2 * LANES + hs], a_rep[3:4, :ds],
                misc[:, hs:hs + ha], a_rep[2:3, :2 * d], a_rows[:r1][None], a_rows[r1:r1 + r2][None],
                a_rows[r1 + r2:][None], a_rep[1:2, :d]]

    return tuple([loss, gx[None]] + leaves(0) + leaves(1) + leaves(2) + leaves(3))
```

```python
import functools
import math

import jax
import jax.numpy as jnp
from jax import lax
from jax.experimental import pallas as pl
from jax.experimental.pallas import tpu as pltpu

F32 = jnp.float32
BF16 = jnp.bfloat16

N_DEV = 8
N_META = 16
CHUNK = 128
PADN = CHUNK - N_META
HEAD_DIM = 64
SSD_GROUPS = 4
CONV_K = 4
EPS = 1e-6
NEG = -1e30
LANES = 128
HALO = 16

ADAM_LR = 0.001
ADAM_B1 = 0.9
ADAM_B2 = 0.999
ADAM_EPS = 1e-08
ADAM_WD = 0.01
ADAM_STEP = 10

VMEM_LIMIT = 56 * 1024 * 1024

NN = (((1,), (0,)), ((), ()))
NT = (((1,), (1,)), ((), ()))
TN = (((0,), (0,)), ((), ()))
MESH = pl.DeviceIdType.MESH


def _dot(a, b, dims=NN):
    return lax.dot_general(a, b, dims, preferred_element_type=F32)


def _split2(x):
    hi = x.astype(BF16)
    lo = (x - hi.astype(F32)).astype(BF16)
    return hi, lo


def _dot_sel(x, sel):
    hi, lo = _split2(x)
    return _dot(hi, sel) + _dot(lo, sel)


def _dot_tri(tri, x):
    h1 = x.astype(BF16)
    r1 = x - h1.astype(F32)
    h2 = r1.astype(BF16)
    h3 = (r1 - h2.astype(F32)).astype(BF16)
    return _dot(tri, h1) + _dot(tri, h2) + _dot(tri, h3)


def _sigmoid(x):
    return 1.0 / (1.0 + jnp.exp(-x))


def _softplus(x):
    return jnp.maximum(x, 0.0) + jnp.log(1.0 + jnp.exp(-jnp.abs(x)))


def _cparams(sem=None, vmem=VMEM_LIMIT):
    kw = {"vmem_limit_bytes": vmem}
    if sem is not None:
        kw["dimension_semantics"] = sem
    return pltpu.CompilerParams(**kw)


def _full(shape):
    nd = len(shape)
    return pl.BlockSpec(shape, lambda *_: (0,) * nd)


def _att_block(p):
    return 384 if p % 384 == 0 else CHUNK


def _my_pos():
    return lax.axis_index("x"), lax.axis_index("y"), lax.axis_index("c")


def _dev_index(x, y, c):
    return 4 * x + 2 * y + c


FLIPS = [(fx, fy, fc) for fx in (0, 1) for fy in (0, 1) for fc in (0, 1)][1:]


def _flip(pos, f):
    return tuple((1 - p) if fi else p for p, fi in zip(pos, f))


def _all_gather(bufs, name):
    nb = len(bufs)

    def body(*refs):
        ins, outs = refs[:nb], refs[nb:2 * nb]
        send_sems, recv_sems, local_sems = refs[2 * nb:]
        x, y, c = _my_pos()
        me = _dev_index(x, y, c)
        sibling = (x, y, 1 - c)
        chips = [(1 - x, y), (x, 1 - y), (1 - x, 1 - y)]

        def copy(b, k, block_idx, to, src=None):
            dst = outs[b].at[block_idx]
            return pltpu.make_async_remote_copy(
                src_ref=dst if src is None else src, dst_ref=dst,
                send_sem=send_sems.at[b, k], recv_sem=recv_sems.at[b, k],
                device_id=to, device_id_type=MESH)

        started = []
        for b in range(nb):
            mine = pltpu.make_async_copy(ins[b], outs[b].at[me], local_sems.at[b])
            mine.start()
            started.append(mine)
        first = []
        for b in range(nb):
            first.append(copy(b, 0, me, sibling, src=ins[b]))
            for j, chip in enumerate(chips):
                first.append(copy(b, 1 + j, me, (chip[0], chip[1], c), src=ins[b]))
        for cp in first:
            cp.start()
        passed = []
        for j, chip in enumerate(chips):
            blk = _dev_index(chip[0], chip[1], c)
            for b in range(nb):
                copy(b, 1 + j, blk, (x, y, c)).wait_recv()
                fwd = copy(b, 4 + j, blk, sibling)
                fwd.start()
                passed.append(fwd)
        for b in range(nb):
            copy(b, 0, _dev_index(x, y, 1 - c), (x, y, c)).wait_recv()
        for j, chip in enumerate(chips):
            blk = _dev_index(chip[0], chip[1], 1 - c)
            for b in range(nb):
                copy(b, 4 + j, blk, (x, y, c)).wait_recv()
        for cp in first + passed:
            cp.wait_send()
        for mine in started:
            mine.wait()

    any_spec = pl.BlockSpec(memory_space=pl.ANY)
    return pl.pallas_call(
        body, name=name,
        out_shape=[jax.ShapeDtypeStruct((N_DEV,) + b.shape, b.dtype) for b in bufs],
        in_specs=[any_spec] * nb, out_specs=[any_spec] * nb,
        scratch_shapes=[pltpu.SemaphoreType.DMA((nb, 7)), pltpu.SemaphoreType.DMA((nb, 7)),
                        pltpu.SemaphoreType.DMA((nb,))],
    )(*bufs)


N_CHIP = 4
CHIP_FLIPS = [(1, 0), (0, 1), (1, 1)]


def _exchange_sibling(bufs, name):
    nb = len(bufs)

    def body(*refs):
        ins, outs = refs[:nb], refs[nb:2 * nb]
        send_sems, recv_sems = refs[2 * nb:]
        x, y, c = _my_pos()

        def copy(b, k):
            return pltpu.make_async_remote_copy(
                src_ref=ins[b].at[2 * k + (1 - c)], dst_ref=outs[b].at[k],
                send_sem=send_sems.at[b, k], recv_sem=recv_sems.at[b, k],
                device_id=(x, y, 1 - c), device_id_type=MESH)

        cps = [copy(b, k) for b in range(nb) for k in range(N_CHIP)]
        for cp in cps:
            cp.start()
        for cp in cps:
            cp.wait()

    any_spec = pl.BlockSpec(memory_space=pl.ANY)
    return pl.pallas_call(
        body, name=name,
        out_shape=[jax.ShapeDtypeStruct((N_CHIP,) + b.shape[1:], b.dtype) for b in bufs],
        in_specs=[any_spec] * nb, out_specs=[any_spec] * nb,
        scratch_shapes=[pltpu.SemaphoreType.DMA((nb, N_CHIP)), pltpu.SemaphoreType.DMA((nb, N_CHIP))],
    )(*bufs)


def _pair_add(mine, recv, core, name):
    _, r, cdim = mine.shape
    tr, tc, by_rows = _tiles_2d(r, cdim)
    pick = (lambda i: (i, 0)) if by_rows else (lambda i: (0, i))

    def body(core_ref, a_ref, b_ref, o_ref):
        o_ref[0] = (a_ref[0, 0].astype(F32) + b_ref[0].astype(F32)).astype(o_ref.dtype)

    return pl.pallas_call(
        body, name=name,
        grid_spec=pltpu.PrefetchScalarGridSpec(
            num_scalar_prefetch=1, grid=(N_CHIP, (r // tr) * (cdim // tc)),
            in_specs=[pl.BlockSpec((1, 1, tr, tc), lambda k, i, core_ref: (k, core_ref[0]) + pick(i)),
                      pl.BlockSpec((1, tr, tc), lambda k, i, core_ref: (k,) + pick(i))],
            out_specs=pl.BlockSpec((1, tr, tc), lambda k, i, core_ref: (k,) + pick(i))),
        out_shape=jax.ShapeDtypeStruct((N_CHIP, r, cdim), mine.dtype),
        compiler_params=_cparams(("parallel", "parallel")),
    )(core, mine.reshape(N_CHIP, 2, r, cdim), recv)


def _chip_peer(x, y, f):
    return ((1 - x) if f[0] else x), ((1 - y) if f[1] else y)


def _exchange_chips_start(bufs, name):
    nb = len(bufs)
    nsem = 2 * 3 * nb

    def body(*refs):
        ins, lands = refs[:nb], refs[nb:2 * nb]
        sems = refs[2 * nb:2 * nb + nsem]
        token = refs[-1]
        x, y, c = _my_pos()
        for b in range(nb):
            for j, f in enumerate(CHIP_FLIPS):
                px, py = _chip_peer(x, y, f)
                pltpu.make_async_remote_copy(
                    src_ref=ins[b].at[2 * px + py], dst_ref=lands[b].at[2 * x + y],
                    send_sem=sems[2 * (3 * b + j)], recv_sem=sems[2 * (3 * b + j) + 1],
                    device_id=(px, py, c), device_id_type=MESH).start()
        token[...] = jnp.zeros_like(token)

    hbm = pl.BlockSpec(memory_space=pltpu.HBM)
    sem = pl.BlockSpec(memory_space=pltpu.SEMAPHORE)
    out = pl.pallas_call(
        body, name=name,
        out_shape=(*([pltpu.SemaphoreType.DMA(())] * nsem),
                   *[pltpu.HBM(b.shape, b.dtype) for b in bufs], *[pltpu.HBM(b.shape, b.dtype) for b in bufs],
                   jax.ShapeDtypeStruct((8, LANES), F32)),
        in_specs=[hbm] * (2 * nb),
        out_specs=(*([sem] * nsem), *([hbm] * (2 * nb)), pl.BlockSpec(memory_space=pltpu.VMEM)),
        input_output_aliases={i: nsem + i for i in range(2 * nb)},
        compiler_params=pltpu.CompilerParams(has_side_effects=pltpu.SideEffectType.DATAFLOW_SIDE_EFFECTING),
    )(*[pltpu.with_memory_space_constraint(b, pltpu.HBM) for b in bufs],
      *[pltpu.with_memory_space_constraint(lax.empty(b.shape, b.dtype), pltpu.HBM) for b in bufs])
    return out[:nsem], out[nsem:nsem + nb], out[nsem + nb:nsem + 2 * nb], out[-1]


def _exchange_chips_wait(sems, thru, lands, after, name):
    nb = len(thru)
    nsem = len(sems)

    def body(*refs):
        ins, lnd = refs[:nb], refs[nb:2 * nb]
        sem_refs = refs[2 * nb:2 * nb + nsem]
        x, y, c = _my_pos()
        for b in range(nb):
            for j, f in enumerate(CHIP_FLIPS):
                px, py = _chip_peer(x, y, f)
                cp = pltpu.make_async_remote_copy(
                    src_ref=ins[b].at[2 * px + py], dst_ref=lnd[b].at[2 * px + py],
                    send_sem=sem_refs[2 * (3 * b + j)], recv_sem=sem_refs[2 * (3 * b + j) + 1],
                    device_id=(px, py, c), device_id_type=MESH)
                cp.wait_send()
                cp.wait_recv()

    hbm = pl.BlockSpec(memory_space=pltpu.HBM)
    sem = pl.BlockSpec(memory_space=pltpu.SEMAPHORE)
    out = pl.pallas_call(
        body, name=name,
        out_shape=tuple([pltpu.HBM(b.shape, b.dtype) for b in thru] + [pltpu.HBM(b.shape, b.dtype) for b in lands]),
        in_specs=[hbm] * (2 * nb) + [sem] * nsem + [pl.BlockSpec(memory_space=pl.ANY)],
        out_specs=tuple([hbm] * (2 * nb)),
        input_output_aliases={i: i for i in range(2 * nb)},
        compiler_params=pltpu.CompilerParams(has_side_effects=pltpu.SideEffectType.DATAFLOW_SIDE_EFFECTING),
    )(*thru, *lands, *sems, after)
    return out[:nb], out[nb:]


def _bcast_start(buf, name):
    nsem = 2 * len(FLIPS)

    def body(src, land, *rest):
        sems, token = rest[:nsem], rest[-1]
        pos = _my_pos()
        for k, f in enumerate(FLIPS):
            pltpu.make_async_remote_copy(
                src_ref=src, dst_ref=land.at[_dev_index(*pos)], send_sem=sems[2 * k], recv_sem=sems[2 * k + 1],
                device_id=_flip(pos, f), device_id_type=MESH).start()
        token[...] = jnp.zeros_like(token)

    hbm = pl.BlockSpec(memory_space=pltpu.HBM)
    sem = pl.BlockSpec(memory_space=pltpu.SEMAPHORE)
    land_shape = (N_DEV,) + buf.shape
    out = pl.pallas_call(
        body, name=name,
        out_shape=(*([pltpu.SemaphoreType.DMA(())] * nsem), pltpu.HBM(buf.shape, buf.dtype),
                   pltpu.HBM(land_shape, buf.dtype), jax.ShapeDtypeStruct((8, LANES), F32)),
        in_specs=[hbm, hbm],
        out_specs=(*([sem] * nsem), hbm, hbm, pl.BlockSpec(memory_space=pltpu.VMEM)),
        input_output_aliases={0: nsem, 1: nsem + 1},
        compiler_params=pltpu.CompilerParams(has_side_effects=pltpu.SideEffectType.DATAFLOW_SIDE_EFFECTING),
    )(pltpu.with_memory_space_constraint(buf, pltpu.HBM),
      pltpu.with_memory_space_constraint(lax.empty(land_shape, buf.dtype), pltpu.HBM))
    return out[:nsem], out[nsem], out[nsem + 1], out[-1]


def _bcast_wait(sems, thru, land, after, name):
    nsem = len(sems)

    def body(src, lnd, *rest):
        sem_refs = rest[:nsem]
        pos = _my_pos()
        for k, f in enumerate(FLIPS):
            peer = _flip(pos, f)
            cp = pltpu.make_async_remote_copy(
                src_ref=src, dst_ref=lnd.at[_dev_index(*peer)], send_sem=sem_refs[2 * k],
                recv_sem=sem_refs[2 * k + 1], device_id=peer, device_id_type=MESH)
            cp.wait_send()
            cp.wait_recv()

    hbm = pl.BlockSpec(memory_space=pltpu.HBM)
    sem = pl.BlockSpec(memory_space=pltpu.SEMAPHORE)
    sent, got = pl.pallas_call(
        body, name=name,
        out_shape=(pltpu.HBM(thru.shape, thru.dtype), pltpu.HBM(land.shape, land.dtype)),
        in_specs=[hbm, hbm] + [sem] * nsem + [pl.BlockSpec(memory_space=pl.ANY)],
        out_specs=(hbm, hbm), input_output_aliases={0: 0, 1: 1},
        compiler_params=pltpu.CompilerParams(has_side_effects=pltpu.SideEffectType.DATAFLOW_SIDE_EFFECTING),
    )(thru, land, *sems, after)
    return lax.dynamic_update_slice_in_dim(got, sent[None], _dev_index(*_my_pos()), axis=0)


def _sum_slots(v, name):
    _, r, cdim = v.shape

    def body(v_ref, o_ref):
        acc = v_ref[0]
        for s in range(1, N_DEV):
            acc = acc + v_ref[s]
        o_ref[...] = acc

    return pl.pallas_call(
        body, name=name, out_shape=jax.ShapeDtypeStruct((r, cdim), F32),
        in_specs=[_full((N_DEV, r, cdim))], out_specs=_full((r, cdim)), grid=(1,),
        compiler_params=_cparams(("arbitrary",)),
    )(v)


def _mm(a, b, dims, out_dtype, tm, tn, name):
    if dims == "nn":
        (m, k), (_, n) = a.shape, b.shape
        a_spec = pl.BlockSpec((tm, k), lambda j, i: (i, 0))
        b_spec = pl.BlockSpec((k, tn), lambda j, i: (0, j))
        dn = NN
    elif dims == "nt":
        (m, k), (n, _) = a.shape, b.shape
        a_spec = pl.BlockSpec((tm, k), lambda j, i: (i, 0))
        b_spec = pl.BlockSpec((tn, k), lambda j, i: (j, 0))
        dn = NT
    else:
        (k, m), (_, n) = a.shape, b.shape
        a_spec = pl.BlockSpec((k, tm), lambda j, i: (0, i))
        b_spec = pl.BlockSpec((k, tn), lambda j, i: (0, j))
        dn = TN
    assert m % tm == 0 and n % tn == 0, (m, tm, n, tn)

    def body(a_ref, b_ref, o_ref):
        o_ref[...] = _dot(a_ref[...], b_ref[...], dn).astype(o_ref.dtype)

    return pl.pallas_call(
        body, name=name, grid=(n // tn, m // tm),
        in_specs=[a_spec, b_spec], out_specs=pl.BlockSpec((tm, tn), lambda j, i: (i, j)),
        out_shape=jax.ShapeDtypeStruct((m, n), out_dtype),
        compiler_params=_cparams(("parallel", "parallel")),
    )(a, b)


def _tiles_2d(r, cdim):
    if r % CHUNK == 0:
        return CHUNK, cdim, True
    return r, _tile(cdim, (256, 128)), False


def _mm_sum_nn(a_list, b_list, tm, tn, name):
    n_op = len(a_list)
    m, n = a_list[0].shape[0], b_list[0].shape[1]

    def body(*refs):
        acc = _dot(refs[0][...], refs[n_op][...])
        for i in range(1, n_op):
            acc = acc + _dot(refs[i][...], refs[n_op + i][...])
        refs[2 * n_op][...] = acc

    return pl.pallas_call(
        body, name=name, grid=(n // tn, m // tm),
        in_specs=([pl.BlockSpec((tm, a.shape[1]), lambda j, i: (i, 0)) for a in a_list]
                  + [pl.BlockSpec((b.shape[0], tn), lambda j, i: (0, j)) for b in b_list]),
        out_specs=pl.BlockSpec((tm, tn), lambda j, i: (i, j)),
        out_shape=jax.ShapeDtypeStruct((m, n), F32),
        compiler_params=_cparams(("parallel", "parallel")),
    )(*a_list, *b_list)


def _tile(n, prefs):
    for t in prefs:
        if n % t == 0:
            return t
    return n


def _prenorm_fwd(head, x2, w):
    p, d = x2.shape[0] + CHUNK, x2.shape[1]

    def body(head_ref, x_ref, w_ref, u_ref):
        i = pl.program_id(0)
        h = jnp.where(i == 0, head_ref[...], x_ref[...])
        ms = jnp.mean(h * h, axis=-1, keepdims=True)
        u_ref[...] = (h * lax.rsqrt(ms + EPS) * w_ref[...]).astype(BF16)

    return pl.pallas_call(
        body, name="prenorm_fwd", grid=(p // CHUNK,),
        in_specs=[_full((CHUNK, d)), pl.BlockSpec((CHUNK, d), lambda i: (jnp.maximum(i - 1, 0), 0)), _full((1, d))],
        out_specs=pl.BlockSpec((CHUNK, d), lambda i: (i, 0)),
        out_shape=jax.ShapeDtypeStruct((p, d), BF16),
        compiler_params=_cparams(("arbitrary",)),
    )(head, x2, w)


def _prenorm_bwd(head, x2, w, du, dout):
    p, d = x2.shape[0] + CHUNK, x2.shape[1]

    def body(head_ref, x_ref, w_ref, du_ref, dout_ref, gx_ref, ghead_ref, gw_ref):
        i = pl.program_id(0)
        h = jnp.where(i == 0, head_ref[...], x_ref[...])
        rstd = lax.rsqrt(jnp.mean(h * h, axis=-1, keepdims=True) + EPS)
        xhat = h * rstd
        dub = du_ref[...]
        dxh = dub * w_ref[...]
        dh = rstd * (dxh - xhat * jnp.mean(dxh * xhat, axis=-1, keepdims=True)) + dout_ref[...]

        @pl.when(i == 0)
        def _():
            ghead_ref[...] = dh
            gw_ref[...] = jnp.zeros_like(gw_ref)

        gx_ref[...] = dh
        gw_ref[0:1, :] += jnp.sum(dub * xhat, axis=0, keepdims=True)

    return pl.pallas_call(
        body, name="prenorm_bwd", grid=(p // CHUNK,),
        in_specs=[_full((CHUNK, d)), pl.BlockSpec((CHUNK, d), lambda i: (jnp.maximum(i - 1, 0), 0)), _full((1, d)),
                  pl.BlockSpec((CHUNK, d), lambda i: (i, 0)), pl.BlockSpec((CHUNK, d), lambda i: (i, 0))],
        out_specs=[pl.BlockSpec((CHUNK, d), lambda i: (jnp.maximum(i - 1, 0), 0)), _full((CHUNK, d)), _full((8, d))],
        out_shape=[jax.ShapeDtypeStruct(x2.shape, F32), jax.ShapeDtypeStruct((CHUNK, d), F32),
                   jax.ShapeDtypeStruct((8, d), F32)],
        compiler_params=_cparams(("arbitrary",)),
    )(head, x2, w, du, dout)


def _conv_pre(ext_ref, cw_ref, cb_ref):
    pre = cb_ref[...] + cw_ref[CONV_K - 1:CONV_K, :] * ext_ref[8:8 + CHUNK, :]
    for j in range(1, CONV_K):
        pre = pre + cw_ref[CONV_K - 1 - j:CONV_K - j, :] * ext_ref[8 - j:8 - j + CHUNK, :]
    return pre


def _ssd_scalars(dtf_ref, brow_ref, alog_ref, rowmask, hs, ha, tri):
    lane = lax.broadcasted_iota(jnp.int32, (1, LANES), 1)
    is_dt = lane < hs
    is_f = (lane >= hs) & (lane < hs + ha)
    dtr = dtf_ref[...] + brow_ref[...]
    sp = _softplus(dtr)
    dt = jnp.where(is_dt, sp, 0.0) * rowmask
    logf = jnp.where(is_f, jnp.minimum(dtr, 0.0) - jnp.log(1.0 + jnp.exp(-jnp.abs(dtr))), 0.0) * rowmask
    a_row = jnp.where(is_dt, -jnp.exp(alog_ref[...]), 0.0)
    run = _dot_tri(tri, dt * a_row + logf)
    return dtr, dt, a_row, run, is_dt, is_f


def _tri_mats():
    r = lax.broadcasted_iota(jnp.int32, (CHUNK, CHUNK), 0)
    c = lax.broadcasted_iota(jnp.int32, (CHUNK, CHUNK), 1)
    return r, c


def _ssd_fwd(xbc, z, dtf, conv_w, conv_b, brow, alog, dskip_l, ssd_norm, sel_t, hs, ha):
    p, cd = xbc.shape
    ds = z.shape[1]
    ns = (cd - ds) // (2 * SSD_GROUPS)
    gw = ds // SSD_GROUPS
    nch = p // CHUNK
    hpg = hs // SSD_GROUPS

    def body(xbc_ref, halo_ref, z_ref, dtf_ref, cw_ref, cb_ref, brow_ref, alog_ref, dsk_ref, nrm_ref, selt_ref,
             y_ref, yssd_ref, hin_ref, cf_ref, pre_ref, st_ref, carry_ref, yacc_ref, xc_s, ex_s, xdtb_s, xwb_s, ext_s):
        c = pl.program_id(0)

        @pl.when(c == 0)
        def _():
            st_ref[...] = jnp.zeros_like(st_ref)
            carry_ref[...] = jnp.zeros_like(carry_ref)

        rows = lax.broadcasted_iota(jnp.int32, (CHUNK, 1), 0)
        rowmask = jnp.where((rows >= PADN) | (c > 0), 1.0, 0.0)
        ri, ci = _tri_mats()
        causal = ri >= ci
        tri = jnp.where(causal, 1.0, 0.0).astype(BF16)

        ext_s[0:8, :] = halo_ref[...].astype(F32)[HALO - 8:, :] * jnp.where(c > 0, 1.0, 0.0)
        ext_s[8:, :] = xbc_ref[...].astype(F32)
        pre = _conv_pre(ext_s, cw_ref, cb_ref)
        pre_ref[...] = pre.astype(BF16)
        xc_s[...] = pre * _sigmoid(pre) * rowmask

        dtr, dt, a_row, run, is_dt, is_f = _ssd_scalars(dtf_ref, brow_ref, alog_ref, rowmask, hs, ha, tri)
        cf = run + carry_ref[...]
        cf_ref[...] = cf
        carry_ref[...] = jnp.where(is_f, cf[CHUNK - 1:CHUNK, :], 0.0)
        cs = jnp.where(is_dt, run, 0.0)
        cl = cs[CHUNK - 1:CHUNK, :]
        selt = selt_ref[...]
        ex_s[...] = _dot_sel(jnp.exp(cs), selt)
        cdec_x = _dot_sel(jnp.broadcast_to(jnp.exp(cl), (8, LANES)), selt)[0:1, :]
        cs_t = cs.T
        xdt = xc_s[:, :ds] * _dot_sel(dt, selt)
        xdtb_s[...] = xdt.astype(BF16)
        xwb_s[...] = (xdt * _dot_sel(jnp.exp(cl - cs), selt)).astype(BF16)

        lane = lax.broadcasted_iota(jnp.int32, (1, LANES), 1)
        half0 = lane < HEAD_DIM
        for g in range(SSD_GROUPS):
            bg = xc_s[:, ds + g * ns: ds + (g + 1) * ns].astype(BF16)
            cg = xc_s[:, ds + SSD_GROUPS * ns + g * ns: ds + SSD_GROUPS * ns + (g + 1) * ns].astype(BF16)
            gm = _dot(cg, bg, NT)
            gs = slice(g * gw, (g + 1) * gw)
            stg = st_ref[:, gs]
            stg_b = stg.astype(BF16)
            hin_ref[0, :, gs] = stg_b
            yoff = _dot(cg, stg_b) * ex_s[:, gs]
            for pr in range(gw // LANES):
                sl = slice(g * gw + pr * LANES, g * gw + (pr + 1) * LANES)
                xp = xdtb_s[:, sl]
                yd = jnp.zeros((CHUNK, LANES), F32)
                for j in range(2):
                    h = g * hpg + 2 * pr + j
                    seg = cs[:, h:h + 1] - cs_t[h:h + 1, :]
                    m = jnp.where(causal, gm * jnp.exp(jnp.minimum(seg, 0.0)), 0.0).astype(BF16)
                    sel = half0 if j == 0 else jnp.logical_not(half0)
                    yd = yd + _dot(m, jnp.where(sel, xp, jnp.zeros_like(xp)))
                yacc_ref[:, sl] = yd + yoff[:, pr * LANES:(pr + 1) * LANES] + dsk_ref[:, sl] * xc_s[:, sl]
            st_ref[:, gs] = stg * cdec_x[:, gs] + _dot(bg, xwb_s[:, gs], TN)

        y = yacc_ref[...]
        y_ref[...] = y.astype(BF16)
        zf = z_ref[...].astype(F32)
        u = y * zf * _sigmoid(zf)
        for g in range(SSD_GROUPS):
            gs = slice(g * gw, (g + 1) * gw)
            ug = u[:, gs]
            ms = jnp.mean(ug * ug, axis=-1, keepdims=True)
            yssd_ref[:, gs] = (ug * lax.rsqrt(ms + EPS) * nrm_ref[:, gs]).astype(BF16)

    rb = CHUNK // HALO
    return pl.pallas_call(
        body, name="ssd_fwd", grid=(nch,),
        in_specs=[pl.BlockSpec((CHUNK, cd), lambda c: (c, 0)),
                  pl.BlockSpec((HALO, cd), lambda c: (jnp.maximum(c * rb - 1, 0), 0)),
                  pl.BlockSpec((CHUNK, ds), lambda c: (c, 0)),
                  pl.BlockSpec((CHUNK, LANES), lambda c: (c, 0)),
                  _full((CONV_K, cd)), _full((1, cd)), _full((1, LANES)), _full((1, LANES)),
                  _full((1, ds)), _full((1, ds)), _full((LANES, ds))],
        out_specs=[pl.BlockSpec((CHUNK, ds), lambda c: (c, 0)), pl.BlockSpec((CHUNK, ds), lambda c: (c, 0)),
                   pl.BlockSpec((1, ns, ds), lambda c: (c, 0, 0)), pl.BlockSpec((CHUNK, LANES), lambda c: (c, 0)),
                   pl.BlockSpec((CHUNK, cd), lambda c: (c, 0))],
        out_shape=[jax.ShapeDtypeStruct((p, ds), BF16), jax.ShapeDtypeStruct((p, ds), BF16),
                   jax.ShapeDtypeStruct((nch, ns, ds), BF16), jax.ShapeDtypeStruct((p, LANES), F32),
                   jax.ShapeDtypeStruct((p, cd), BF16)],
        scratch_shapes=[pltpu.VMEM((ns, ds), F32), pltpu.VMEM((1, LANES), F32), pltpu.VMEM((CHUNK, ds), F32),
                        pltpu.VMEM((CHUNK, cd), F32), pltpu.VMEM((CHUNK, ds), F32),
                        pltpu.VMEM((CHUNK, ds), BF16), pltpu.VMEM((CHUNK, ds), BF16),
                        pltpu.VMEM((8 + CHUNK, cd), F32)],
        compiler_params=_cparams(("arbitrary",)),
    )(xbc, xbc, z, dtf, conv_w, conv_b, brow, alog, dskip_l, ssd_norm, sel_t)


def _ssd_bwd(dyssd, y, z, xbc, pre, dtf, hin, dcf, conv_w, brow, alog, dskip_l, ssd_norm, sel_t, sel, hs, ha):
    p, cd = xbc.shape
    ds = z.shape[1]
    ns = (cd - ds) // (2 * SSD_GROUPS)
    gw = ds // SSD_GROUPS
    nch = p // CHUNK
    hpg = hs // SSD_GROUPS

    def body(dyssd_ref, y_ref, z_ref, xbc_ref, pre_ref, dtf_ref, hin_ref, dcf_ref, cw_ref, brow_ref,
             alog_ref, dsk_ref, nrm_ref, selt_ref, sel_ref,
             dxbc_ref, dz_ref, ddtf_ref, gcw_ref, gcb_ref, gnrm_ref, gsm_ref,
             dst_ref, nxt_ref, fcar_ref, gdsk_ref, dxc_ref, xc_s, dsl_s, dtx_s, ex_s, wx_s, dy_s, xdtb_s, xwb_s,
             dyb_s, dyeb_s):
        step = pl.program_id(0)
        c = nch - 1 - step

        @pl.when(step == 0)
        def _():
            dst_ref[...] = jnp.zeros_like(dst_ref)
            nxt_ref[...] = jnp.zeros_like(nxt_ref)
            fcar_ref[...] = jnp.zeros_like(fcar_ref)
            gdsk_ref[...] = jnp.zeros_like(gdsk_ref)
            gcw_ref[...] = jnp.zeros_like(gcw_ref)
            gcb_ref[...] = jnp.zeros_like(gcb_ref)
            gnrm_ref[...] = jnp.zeros_like(gnrm_ref)
            gsm_ref[...] = jnp.zeros_like(gsm_ref)

        rows = lax.broadcasted_iota(jnp.int32, (CHUNK, 1), 0)
        rowmask = jnp.where((rows >= PADN) | (c > 0), 1.0, 0.0)
        ri, ci = _tri_mats()
        causal = ri >= ci
        anti = ci >= ri
        tri = jnp.where(causal, 1.0, 0.0).astype(BF16)
        rtri = jnp.where(anti, 1.0, 0.0).astype(BF16)

        pre = pre_ref[...].astype(F32)
        sg = _sigmoid(pre)
        xc_s[...] = pre * sg * rowmask
        dsl_s[...] = sg * (1.0 + pre * (1.0 - sg)) * rowmask

        dtr, dt, a_row, run, is_dt, is_f = _ssd_scalars(dtf_ref, brow_ref, alog_ref, rowmask, hs, ha, tri)
        cs = jnp.where(is_dt, run, 0.0)
        cl = cs[CHUNK - 1:CHUNK, :]
        selt = selt_ref[...]
        selm = sel_ref[...]
        dtx_s[...] = _dot_sel(dt, selt)
        ex_s[...] = _dot_sel(jnp.exp(cs), selt)
        wx_s[...] = _dot_sel(jnp.exp(cl - cs), selt)
        cdec = jnp.exp(cl)
        cdec_x = _dot_sel(jnp.broadcast_to(cdec, (8, LANES)), selt)[0:1, :]
        cs_t = cs.T
        xdt = xc_s[:, :ds] * dtx_s[...]
        xdtb_s[...] = xdt.astype(BF16)
        xwb_s[...] = (xdt * wx_s[...]).astype(BF16)

        yv = y_ref[...].astype(F32)
        zf = z_ref[...].astype(F32)
        sz = _sigmoid(zf)
        u = yv * zf * sz
        dyo = dyssd_ref[...].astype(F32)
        du_parts = []
        for g in range(SSD_GROUPS):
            gs = slice(g * gw, (g + 1) * gw)
            ug = u[:, gs]
            rstd = lax.rsqrt(jnp.mean(ug * ug, axis=-1, keepdims=True) + EPS)
            yhat = ug * rstd
            dyg = dyo[:, gs]
            gnrm_ref[0:1, gs] += jnp.sum(dyg * yhat, axis=0, keepdims=True)
            dyh = dyg * nrm_ref[:, gs]
            du_parts.append(rstd * (dyh - yhat * jnp.mean(dyh * yhat, axis=-1, keepdims=True)))
        du = jnp.concatenate(du_parts, axis=1)
        dy = du * zf * sz
        dz_ref[...] = (du * yv * sz * (1.0 + zf * (1.0 - sz))).astype(BF16)
        dy_s[...] = dy
        dyb_s[...] = dy.astype(BF16)
        dyeb_s[...] = (dy * ex_s[...]).astype(BF16)
        gdsk_ref[...] += jnp.sum(dy * xc_s[:, :ds], axis=0, keepdims=True)
        lane = lax.broadcasted_iota(jnp.int32, (1, LANES), 1)
        half0 = lane < HEAD_DIM
        x_parts, yo_parts, t4_parts = [], [], []
        dcs = jnp.zeros((CHUNK, LANES), F32)
        for g in range(SSD_GROUPS):
            gs = slice(g * gw, (g + 1) * gw)
            bsl = slice(ds + g * ns, ds + (g + 1) * ns)
            csl = slice(ds + SSD_GROUPS * ns + g * ns, ds + SSD_GROUPS * ns + (g + 1) * ns)
            bg = xc_s[:, bsl].astype(BF16)
            cg = xc_s[:, csl].astype(BF16)
            gm = _dot(cg, bg, NT)
            gm_t = _dot(bg, cg, NT)
            stg_b = hin_ref[0, :, gs]
            dstg = dst_ref[:, gs]
            dstg_b = dstg.astype(BF16)
            t4_parts.append(jnp.sum(dstg * stg_b.astype(F32), axis=0, keepdims=True))
            zst = _dot(bg, dstg_b) * wx_s[:, gs]
            x_parts.append(xc_s[:, gs] * dtx_s[:, gs] * zst)
            yo_parts.append(dy_s[:, gs] * (_dot(cg, stg_b) * ex_s[:, gs]))
            dgsum = jnp.zeros((CHUNK, CHUNK), F32)
            dgtsum = jnp.zeros((CHUNK, CHUNK), F32)
            for pr in range(gw // LANES):
                sl = slice(g * gw + pr * LANES, g * gw + (pr + 1) * LANES)
                xp = xdtb_s[:, sl]
                dyp = dyb_s[:, sl]
                dxd = zst[:, pr * LANES:(pr + 1) * LANES]
                for j in range(2):
                    h = g * hpg + 2 * pr + j
                    sel_l = half0 if j == 0 else jnp.logical_not(half0)
                    seg = cs[:, h:h + 1] - cs_t[h:h + 1, :]
                    lm = jnp.where(causal, jnp.exp(jnp.minimum(seg, 0.0)), 0.0)
                    lmt = jnp.where(anti, jnp.exp(jnp.minimum(-seg, 0.0)), 0.0)
                    dyp_m = jnp.where(sel_l, dyp, jnp.zeros_like(dyp))
                    xp_m = jnp.where(sel_l, xp, jnp.zeros_like(xp))
                    dxd = dxd + _dot((gm_t * lmt).astype(BF16), dyp_m)
                    dg = _dot(dyp_m, xp, NT) * lm
                    dgt = _dot(xp_m, dyp, NT) * lmt
                    dgsum = dgsum + dg
                    dgtsum = dgtsum + dgt
                    qrow = (jnp.sum(dg * gm, axis=1, keepdims=True) - jnp.sum(dgt * gm_t, axis=1, keepdims=True))
                    dcs = dcs + jnp.where(lane == h, qrow, 0.0)
                dxc_ref[:, sl] = dxd
            dxc_ref[:, csl] = _dot(dgsum.astype(BF16), bg) + _dot(dyeb_s[:, gs], stg_b, NT)
            dxc_ref[:, bsl] = _dot(dgtsum.astype(BF16), cg) + _dot(xwb_s[:, gs], dstg_b, NT)
            dst_ref[:, gs] = dstg * cdec_x[:, gs] + _dot(cg, dyeb_s[:, gs], TN)

        dxdt = dxc_ref[:, :ds]
        xst = _dot_sel(jnp.concatenate(x_parts, axis=1), selm)
        yo = _dot_sel(jnp.concatenate(yo_parts, axis=1), selm)
        t4 = _dot_sel(jnp.concatenate([jnp.concatenate(t4_parts, axis=1), jnp.zeros((7, ds), F32)], axis=0), selm)
        dcl = jnp.sum(xst, axis=0, keepdims=True) + cdec * t4[0:1, :]
        dcs = dcs + yo - xst + jnp.where(rows == CHUNK - 1, dcl, 0.0)
        da_ = _dot_tri(rtri, dcs)
        ddt = _dot_sel(dxdt * xc_s[:, :ds], selm) + da_ * a_row
        dcf_blk = dcf_ref[...]
        dlogf = _dot_tri(rtri, dcf_blk) + fcar_ref[...]
        fcar_ref[...] += jnp.sum(dcf_blk, axis=0, keepdims=True)
        sgd = _sigmoid(dtr)
        ddtf = (jnp.where(is_dt, ddt * sgd, 0.0) + jnp.where(is_f, dlogf * (1.0 - sgd), 0.0)) * rowmask
        ddtf_ref[...] = ddtf
        gsm_ref[0:1, :] += jnp.sum(ddtf, axis=0, keepdims=True)
        gsm_ref[1:2, :] += jnp.sum(da_ * dt, axis=0, keepdims=True) * a_row

        dxc_ref[:, :ds] = dxdt * dtx_s[...] + dsk_ref[...] * dy_s[...]
        dpre = dxc_ref[...] * dsl_s[...]
        nxt_ref[0:CHUNK, :] = dpre
        gcb_ref[0:1, :] += jnp.sum(dpre, axis=0, keepdims=True)
        xr = xbc_ref[...].astype(F32)
        gcw_ref[CONV_K - 1:CONV_K, :] += jnp.sum(dpre * xr, axis=0, keepdims=True)
        dxr = cw_ref[CONV_K - 1:CONV_K, :] * dpre
        for j in range(1, CONV_K):
            up = nxt_ref[j:j + CHUNK, :]
            gcw_ref[CONV_K - 1 - j:CONV_K - j, :] += jnp.sum(up * xr, axis=0, keepdims=True)
            dxr = dxr + cw_ref[CONV_K - 1 - j:CONV_K - j, :] * up
        nxt_ref[CHUNK:, :] = dpre[0:8, :]
        dxbc_ref[...] = dxr.astype(BF16)

        @pl.when(step == nch - 1)
        def _():
            gsm_ref[2:3, :] = _dot_sel(jnp.broadcast_to(gdsk_ref[...], (8, ds)), selm)[0:1, :]

    rev = lambda s: nch - 1 - s
    blk = lambda w: pl.BlockSpec((CHUNK, w), lambda s: (rev(s), 0))
    return pl.pallas_call(
        body, name="ssd_bwd", grid=(nch,),
        in_specs=[blk(ds), blk(ds), blk(ds), blk(cd), blk(cd),
                  blk(LANES), pl.BlockSpec((1, ns, ds), lambda s: (rev(s), 0, 0)), blk(LANES),
                  _full((CONV_K, cd)), _full((1, LANES)), _full((1, LANES)),
                  _full((1, ds)), _full((1, ds)), _full((LANES, ds)), _full((ds, LANES))],
        out_specs=[blk(cd), blk(ds), blk(LANES), _full((8, cd)), _full((8, cd)), _full((8, ds)), _full((8, LANES))],
        out_shape=[jax.ShapeDtypeStruct((p, cd), BF16), jax.ShapeDtypeStruct((p, ds), BF16),
                   jax.ShapeDtypeStruct((p, LANES), F32), jax.ShapeDtypeStruct((8, cd), F32),
                   jax.ShapeDtypeStruct((8, cd), F32), jax.ShapeDtypeStruct((8, ds), F32),
                   jax.ShapeDtypeStruct((8, LANES), F32)],
        scratch_shapes=[pltpu.VMEM((ns, ds), F32), pltpu.VMEM((CHUNK + 8, cd), F32), pltpu.VMEM((1, LANES), F32),
                        pltpu.VMEM((1, ds), F32), pltpu.VMEM((CHUNK, cd), F32),
                        pltpu.VMEM((CHUNK, cd), F32), pltpu.VMEM((CHUNK, cd), F32),
                        pltpu.VMEM((CHUNK, ds), F32), pltpu.VMEM((CHUNK, ds), F32), pltpu.VMEM((CHUNK, ds), F32),
                        pltpu.VMEM((CHUNK, ds), F32), pltpu.VMEM((CHUNK, ds), BF16), pltpu.VMEM((CHUNK, ds), BF16),
                        pltpu.VMEM((CHUNK, ds), BF16), pltpu.VMEM((CHUNK, ds), BF16)],
        compiler_params=_cparams(("arbitrary",)),
    )(dyssd, y, z, xbc, pre, dtf, hin, dcf, conv_w, brow, alog, dskip_l, ssd_norm, sel_t, sel)


def _attn_fwd(q, k, v, ck, blk):
    p, da = q.shape
    npair, nkb = ck.shape[0], ck.shape[1]
    scale = 1.0 / math.sqrt(HEAD_DIM)

    def body(q_ref, k_ref, v_ref, ck_ref, o_ref, lse_ref):
        i = pl.program_id(1)
        lane = lax.broadcasted_iota(jnp.int32, (1, LANES), 1)
        sels = [lane < HEAD_DIM, lane >= HEAD_DIM]
        ones = [jnp.where(lane == HEAD_DIM, 1.0, 0.0).astype(BF16), jnp.where(lane == 0, 1.0, 0.0).astype(BF16)]
        qb = q_ref[...] * scale
        qms = [jnp.where(sel, qb, jnp.zeros_like(qb)) for sel in sels]
        cmask = (lax.broadcasted_iota(jnp.int32, (blk, blk), 1) <= lax.broadcasted_iota(jnp.int32, (blk, blk), 0))

        def step(kb, carry, masked, nk=1):
            r0 = pl.multiple_of(kb * blk, blk)
            ks = k_ref[pl.ds(r0, nk * blk), :]
            vs = v_ref[pl.ds(r0, nk * blk), :]
            out = []
            for j in range(2):
                m, acc = carry[2 * j], carry[2 * j + 1]
                ckr = jnp.concatenate([ck_ref[0, kb + t, j:j + 1, :] for t in range(nk)], axis=1)
                s = _dot(qms[j], ks, NT) - ckr
                if masked:
                    s = jnp.where(cmask, s, NEG)
                mn = jnp.maximum(m, jnp.max(s, axis=-1, keepdims=True))
                pr = jnp.exp(s - mn).astype(BF16)
                acc = jnp.exp(m - mn) * acc + _dot(pr, jnp.where(sels[j], vs, ones[j]))
                out += [mn, acc]
            return tuple(out)

        init = (jnp.full((blk, 1), NEG, F32), jnp.zeros((blk, LANES), F32)) * 2
        n4 = i // 4
        n2 = (i - 4 * n4) // 2
        carry = lax.fori_loop(0, n4, lambda t, c: step(4 * t, c, False, 4), init)
        carry = lax.fori_loop(0, n2, lambda t, c: step(4 * n4 + 2 * t, c, False, 2), carry)
        carry = lax.fori_loop(4 * n4 + 2 * n2, i, lambda kb, c: step(kb, c, False), carry)
        m0, a0, m1, a1 = step(i, carry, True)
        l0 = a0[:, HEAD_DIM:HEAD_DIM + 1]
        l1 = a1[:, 0:1]
        o_ref[...] = jnp.where(sels[0], a0 / l0, a1 / l1).astype(BF16)
        lse_ref[...] = jnp.where(sels[0], m0 + jnp.log(l0), m1 + jnp.log(l1))

    return pl.pallas_call(
        body, name="attn_fwd", grid=(npair, p // blk),
        in_specs=[pl.BlockSpec((blk, LANES), lambda h, i: (i, h)),
                  pl.BlockSpec((p, LANES), lambda h, i: (0, h)), pl.BlockSpec((p, LANES), lambda h, i: (0, h)),
                  pl.BlockSpec((1, nkb, 8, blk), lambda h, i: (h, 0, 0, 0))],
        out_specs=[pl.BlockSpec((blk, LANES), lambda h, i: (i, h)), pl.BlockSpec((blk, LANES), lambda h, i: (i, h))],
        out_shape=[jax.ShapeDtypeStruct((p, da), BF16), jax.ShapeDtypeStruct((p, da), F32)],
        compiler_params=_cparams(("parallel", "arbitrary")),
    )(q, k, v, ck)


def _attn_bwd(q, k, v, o, do, lse_rep, ck, blk):
    p, da = q.shape
    npair, nkb = ck.shape[0], ck.shape[1]
    nq = p // blk
    scale = 1.0 / math.sqrt(HEAD_DIM)

    def body(k_ref, v_ref, q_ref, do_ref, o_ref, lse_ref, ck_ref, dk_ref, dv_ref, dq_ref, dcs_ref, rsum_ref, dq_acc):
        jb = pl.program_id(1)

        @pl.when(jb == 0)
        def _():
            dq_acc[...] = jnp.zeros_like(dq_acc)

        ks = k_ref[...]
        vs = v_ref[...]
        lane = lax.broadcasted_iota(jnp.int32, (1, LANES), 1)
        sels = [lane < HEAD_DIM, lane >= HEAD_DIM]
        ones = [jnp.where(lane == HEAD_DIM, 1.0, 0.0).astype(BF16), jnp.where(lane == 0, 1.0, 0.0).astype(BF16)]
        kss = ks * scale
        kmo = [jnp.where(sels[j], kss, ones[j]) for j in range(2)]
        cmask = (lax.broadcasted_iota(jnp.int32, (blk, blk), 1) <= lax.broadcasted_iota(jnp.int32, (blk, blk), 0))

        def step(ib, carry, masked, nb=1):
            rows = nb * blk
            r0 = pl.multiple_of(ib * blk, blk)
            qb = q_ref[pl.ds(r0, rows), :] * scale
            dob = do_ref[pl.ds(r0, rows), :]
            prod = dob.astype(F32) * o_ref[pl.ds(r0, rows), :].astype(F32)
            out = []
            for j in range(2):
                dk, dv = carry[2 * j], carry[2 * j + 1]
                qm = jnp.where(sels[j], qb, jnp.zeros_like(qb))
                dom = jnp.where(sels[j], dob, jnp.zeros_like(dob))
                lse = lse_ref[pl.ds(r0, rows), HEAD_DIM * j:HEAD_DIM * j + 1]
                dlt = jnp.sum(jnp.where(sels[j], prod, 0.0), axis=-1, keepdims=True)
                s = _dot(qm, ks, NT) - ck_ref[0, 0, j:j + 1, :] - lse
                pm = jnp.exp(jnp.minimum(s, 0.0))
                if masked:
                    pm = jnp.where(cmask, pm, 0.0)
                ds_b = (pm * (_dot(dom, vs, NT) - dlt)).astype(BF16)
                dv = dv + _dot(pm.astype(BF16), dom, TN)
                dk = dk + _dot(ds_b, jnp.where(sels[j], qb, ones[j]), TN)
                dq_acc[pl.ds(r0, rows), LANES * j:LANES * (j + 1)] += _dot(ds_b, kmo[j])
                out += [dk, dv]
            return tuple(out)

        zero = jnp.zeros((blk, LANES), F32)
        carry = step(jb, (zero, zero, zero, zero), True)
        n4 = (nq - 1 - jb) // 4
        n2 = (nq - 1 - jb - 4 * n4) // 2
        carry = lax.fori_loop(0, n4, lambda t, c: step(jb + 1 + 4 * t, c, False, 4), carry)
        carry = lax.fori_loop(0, n2, lambda t, c: step(jb + 1 + 4 * n4 + 2 * t, c, False, 2), carry)
        dk0, dv0, dk1, dv1 = lax.fori_loop(jb + 1 + 4 * n4 + 2 * n2, nq, lambda ib, c: step(ib, c, False), carry)
        dk_ref[...] = jnp.where(sels[0], dk0, dk1).astype(BF16)
        dv_ref[...] = (dv0 + dv1).astype(BF16)
        lane8 = lax.broadcasted_iota(jnp.int32, (1, 8), 1)
        pair8 = lambda c0, c1: jnp.where(lane8 == 0, c0, jnp.where(lane8 == 1, c1, 0.0))
        dcs_ref[0] = pair8(dk0[:, HEAD_DIM:HEAD_DIM + 1], dk1[:, 0:1])

        @pl.when(jb == nkb - 1)
        def _():
            a0 = dq_acc[:, :LANES]
            a1 = dq_acc[:, LANES:]
            dq_ref[...] = jnp.where(sels[0], a0, a1).astype(BF16)
            rsum_ref[0] = pair8(a0[:, HEAD_DIM:HEAD_DIM + 1], a1[:, 0:1])

    colblk = pl.BlockSpec((blk, LANES), lambda h, j: (j, h))
    colfull = pl.BlockSpec((p, LANES), lambda h, j: (0, h))
    ckspec = pl.BlockSpec((1, 1, 8, blk), lambda h, j: (h, j, 0, 0))
    return pl.pallas_call(
        body, name="attn_bwd", grid=(npair, nkb),
        in_specs=[colblk, colblk, colfull, colfull, colfull, colfull, ckspec],
        out_specs=[colblk, colblk, colfull, pl.BlockSpec((1, blk, 8), lambda h, j: (h, j, 0)),
                   pl.BlockSpec((1, p, 8), lambda h, j: (h, 0, 0))],
        out_shape=[jax.ShapeDtypeStruct((p, da), BF16), jax.ShapeDtypeStruct((p, da), BF16),
                   jax.ShapeDtypeStruct((p, da), BF16), jax.ShapeDtypeStruct((npair, p, 8), F32),
                   jax.ShapeDtypeStruct((npair, p, 8), F32)],
        scratch_shapes=[pltpu.VMEM((p, 2 * LANES), F32)],
        compiler_params=_cparams(("parallel", "arbitrary")),
    )(k, v, q, do, o, lse_rep, ck)


def _tail(yssd, o, zatt, graw, head, x2, tgt2, wps, wpa, wout, gate_bias, norm_post):
    p, ds = yssd.shape
    da = o.shape[1]
    d = x2.shape[1]

    def body(yssd_ref, o_ref, zatt_ref, g_ref, head_ref, x_ref, tgt_ref, wps_ref, wpa_ref, wout_ref, gb_ref, np_ref,
             dyssd_ref, do_ref, dzatt_ref, dg_ref, dzo_ref, mrg_ref, da_ref, db_ref, yatt_ref, dout_ref, red_ref):
        i = pl.program_id(0)

        @pl.when(i == 0)
        def _():
            red_ref[...] = jnp.zeros_like(red_ref)

        h = jnp.where(i == 0, head_ref[...], x_ref[...])
        valid = jnp.where(i > 0, 1.0, 0.0)
        ob = o_ref[...].astype(F32)
        za = zatt_ref[...].astype(F32)
        sza = _sigmoid(za)
        silu = za * sza
        yatt_b = (ob * silu).astype(BF16)
        yatt_ref[...] = yatt_b
        wps_v, wpa_v, wout_v = wps_ref[...], wpa_ref[...], wout_ref[...]
        a = _dot(yssd_ref[...], wps_v)
        b = _dot(yatt_b, wpa_v)
        gr = g_ref[...].astype(F32) + gb_ref[...]
        gs = _sigmoid(gr[:, :d])
        ga = _sigmoid(gr[:, d:])
        mrg_b = (gs * a + ga * b).astype(BF16)
        mrg_ref[...] = mrg_b
        zo = _dot(mrg_b, wout_v)
        rstd = lax.rsqrt(jnp.mean(zo * zo, axis=-1, keepdims=True) + EPS)
        zh = zo * rstd
        npw = np_ref[...]
        err = (h + zh * npw - tgt_ref[...]) * valid
        dout = err * (1.0 / d)
        dout_ref[...] = dout
        dzh = dout * npw
        dzo_b = (rstd * (dzh - zh * jnp.mean(dzh * zh, axis=-1, keepdims=True))).astype(BF16)
        dzo_ref[...] = dzo_b
        dm = _dot(dzo_b, wout_v, NT)
        da_b = (gs * dm).astype(BF16)
        db_b = (ga * dm).astype(BF16)
        da_ref[...] = da_b
        db_ref[...] = db_b
        dgs = dm * a * gs * (1.0 - gs)
        dga = dm * b * ga * (1.0 - ga)
        dg_ref[:, :d] = dgs.astype(BF16)
        dg_ref[:, d:] = dga.astype(BF16)
        dyssd_ref[...] = _dot(da_b, wps_v, NT).astype(BF16)
        dya = _dot(db_b, wpa_v, NT)
        do_ref[...] = (dya * silu).astype(BF16)
        dzatt_ref[...] = (dya * ob * sza * (1.0 + za * (1.0 - sza))).astype(BF16)
        red_ref[0:1, :d] += jnp.sum(dout * zh, axis=0, keepdims=True)
        red_ref[1:2, :d] += jnp.sum(dgs, axis=0, keepdims=True)
        red_ref[1:2, d:] += jnp.sum(dga, axis=0, keepdims=True)
        red_ref[2:3, 0:1] += jnp.sum(jnp.sum(err * err, axis=1, keepdims=True), axis=0, keepdims=True) * (0.5 / d)

    row = lambda w: pl.BlockSpec((CHUNK, w), lambda i: (i, 0))
    shifted = lambda w: pl.BlockSpec((CHUNK, w), lambda i: (jnp.maximum(i - 1, 0), 0))
    sd = jax.ShapeDtypeStruct
    return pl.pallas_call(
        body, name="tail", grid=(p // CHUNK,),
        in_specs=[row(ds), row(da), row(da), row(2 * d), _full((CHUNK, d)), shifted(d), shifted(d),
                  _full((ds, d)), _full((da, d)), _full((d, d)), _full((1, 2 * d)), _full((1, d))],
        out_specs=[row(ds), row(da), row(da), row(2 * d), row(d), row(d), row(d), row(d), row(da), row(d),
                   _full((8, 2 * d))],
        out_shape=[sd((p, ds), BF16), sd((p, da), BF16), sd((p, da), BF16), sd((p, 2 * d), BF16), sd((p, d), BF16),
                   sd((p, d), BF16), sd((p, d), BF16), sd((p, d), BF16), sd((p, da), BF16), sd((p, d), F32),
                   sd((8, 2 * d), F32)],
        compiler_params=_cparams(("arbitrary",)),
    )(yssd, o, zatt, graw, head, x2, tgt2, wps, wpa, wout, gate_bias, norm_post)


def _adamw_math(w, g, m, v):
    m2 = ADAM_B1 * m + (1.0 - ADAM_B1) * g
    v2 = ADAM_B2 * v + (1.0 - ADAM_B2) * (g * g)
    m_hat = m2 / (1.0 - ADAM_B1 ** ADAM_STEP)
    v_hat = v2 / (1.0 - ADAM_B2 ** ADAM_STEP)
    delta = -ADAM_LR * (m_hat / (jnp.sqrt(v_hat) + ADAM_EPS) + ADAM_WD * w)
    return delta, m2, v2


def _adamw(w, g, m, v, name, parts=False):
    r, cdim = w.shape
    tr, tc, by_rows = _tiles_2d(r, cdim)
    pick = (lambda i: (i, 0)) if by_rows else (lambda i: (0, i))

    def body(w_ref, g_ref, m_ref, v_ref, go_ref, d_ref, mo_ref, vo_ref):
        if parts:
            g = g_ref[0].astype(F32)
            for s in range(1, g_ref.shape[0]):
                g = g + g_ref[s].astype(F32)
        else:
            g = g_ref[...]
        delta, m2, v2 = _adamw_math(w_ref[...], g, m_ref[...], v_ref[...])
        go_ref[...] = g
        d_ref[...] = delta
        mo_ref[...] = m2
        vo_ref[...] = v2

    blk = pl.BlockSpec((tr, tc), pick)
    gspec = pl.BlockSpec((g.shape[0], tr, tc), lambda i: (0,) + pick(i)) if parts else blk
    return pl.pallas_call(
        body, name=name, grid=((r // tr) * (cdim // tc),),
        in_specs=[blk, gspec, blk, blk], out_specs=[blk] * 4,
        out_shape=[jax.ShapeDtypeStruct((r, cdim), F32)] * 4,
        compiler_params=_cparams(("parallel",)),
    )(w, g, m, v)


def _pad_cols(a, width):
    return jnp.pad(a, ((0, 0), (0, width - a.shape[1])))


def _pack_small_shard(conv_w_sh, meta_sh, width):
    return jnp.concatenate([_pad_cols(conv_w_sh, width), jnp.zeros((4, width), F32), _pad_cols(meta_sh, width)], axis=0)


def _pack_small_rep(norm_pre, norm_post, gate_bias, ssd_norm, conv_b, misc, width):
    rows = [norm_pre, norm_post, gate_bias, ssd_norm, conv_b, misc]
    return jnp.concatenate([_pad_cols(r, width) for r in rows] + [jnp.zeros((2, width), F32)], axis=0)


def _misc_row(dt_bias, fgate_bias, a_log, d_skip, extra):
    hs, ha = dt_bias.shape[1], fgate_bias.shape[1]
    return jnp.concatenate([dt_bias, fgate_bias, jnp.zeros((1, LANES - hs - ha), F32), _pad_cols(a_log, LANES),
                            _pad_cols(d_skip, LANES), _pad_cols(extra, LANES)], axis=1)


def kernel(x, meta_tokens, norm_pre, w_in, conv_w, conv_b, dt_bias, a_log, d_skip, ssd_norm, fgate_bias, gate_bias, w_proj_ssd, w_proj_att, w_out, norm_post, loss_target, m_meta_tokens, m_norm_pre, m_w_in, m_conv_w, m_conv_b, m_dt_bias, m_a_log, m_d_skip, m_ssd_norm, m_fgate_bias, m_gate_bias, m_w_proj_ssd, m_w_proj_att, m_w_out, m_norm_post, v_meta_tokens, v_norm_pre, v_w_in, v_conv_w, v_conv_b, v_dt_bias, v_a_log, v_d_skip, v_ssd_norm, v_fgate_bias, v_gate_bias, v_w_proj_ssd, v_w_proj_att, v_w_out, v_norm_post):
    seq, d = x.shape[1], x.shape[2]
    p = seq + CHUNK
    hs, ha = dt_bias.shape[1], fgate_bias.shape[1]
    ds, cd = ssd_norm.shape[1], conv_b.shape[1]
    da = ha * HEAD_DIM
    nc8 = w_in.shape[2]
    cws = cd // N_DEV
    msh = d // N_DEV
    r1, r2, r3 = ds // N_DEV, da // N_DEV, d // N_DEV
    me = _dev_index(*_my_pos())
    x2, tgt2 = x[0], loss_target[0]

    win_sh = jnp.transpose(w_in[0]).astype(BF16)
    rows_sh = jnp.concatenate([w_proj_ssd[0], w_proj_att[0], w_out[0]], axis=0).astype(BF16)
    small_sh = _pack_small_shard(conv_w[0], meta_tokens, cws)
    win_all, small_all = _all_gather([win_sh, small_sh], "gather_weights")
    rows_sh, win_all = lax.optimization_barrier((rows_sh, win_all))
    rows_sems, rows_thru, rows_land, rows_token = _bcast_start(rows_sh, "gather_rows_start")
    w_full = win_all.reshape(N_DEV * nc8, d)
    cuts = [0, ds, ds + cd, ds + cd + hs, ds + cd + hs + da, ds + cd + hs + 2 * da, ds + cd + hs + 3 * da,
            ds + cd + hs + 4 * da, ds + cd + hs + 4 * da + ha, ds + cd + hs + 4 * da + ha + 2 * d]
    w_z, w_xbc, w_dt, w_zatt, w_q, w_k, w_v, w_f, w_g = [w_full[cuts[i]:cuts[i + 1]] for i in range(9)]
    w_dtf = jnp.concatenate([w_dt, w_f, jnp.zeros((LANES - hs - ha, d), BF16)], axis=0)
    conv_w_full = jnp.transpose(small_all[:, 0:CONV_K, :], (1, 0, 2)).reshape(CONV_K, cd)
    meta_full = jnp.transpose(small_all[:, 8:8 + N_META, :msh], (1, 0, 2)).reshape(N_META, d)
    head = jnp.concatenate([jnp.zeros((PADN, d), F32), meta_full + rows_token[0:1, 0:1]], axis=0)

    u = _prenorm_fwd(head, x2, norm_pre)
    tm = _att_block(p)
    seg_w = [w_z, w_xbc, w_zatt, w_q, w_k, w_v, w_g]
    zs, xbc, zatt, q, k, v, graw = [
        _mm(u, w, "nt", BF16, tm, _tile(w.shape[0], (1024, 512, 256, 128)), "inproj_%d" % i) for i, w in enumerate(seg_w)]
    dtf = _mm(u, w_dtf, "nt", F32, tm, LANES, "inproj_dtf")

    brow = jnp.concatenate([dt_bias, fgate_bias, jnp.zeros((1, LANES - hs - ha), F32)], axis=1)
    alog_row = _pad_cols(a_log, LANES)
    dskip_l = jnp.repeat(d_skip, HEAD_DIM, axis=1)
    sel_t = (lax.broadcasted_iota(jnp.int32, (LANES, ds), 1) // HEAD_DIM
             == lax.broadcasted_iota(jnp.int32, (LANES, ds), 0)).astype(BF16)
    sel = sel_t.T
    y, yssd, hin, cf, pre = _ssd_fwd(xbc, zs, dtf, conv_w_full, conv_b, brow, alog_row, dskip_l, ssd_norm, sel_t, hs, ha)

    blk = _att_block(p)
    nkb, npair = p // blk, ha // 2
    cum = jnp.where(lax.broadcasted_iota(jnp.int32, (p, 1), 0) < PADN, -NEG, cf[:, hs:hs + ha])
    ck = jnp.transpose(cum.T.reshape(npair, 2, nkb, blk), (0, 2, 1, 3))
    ck = jnp.pad(ck, ((0, 0), (0, 0), (0, 6), (0, 0)))
    o, lse_rep = _attn_fwd(q, k, v, ck, blk)

    rows_all = _bcast_wait(rows_sems, rows_thru, rows_land, lse_rep, "gather_rows_wait")
    wps = rows_all[:, :r1].reshape(ds, d)
    wpa = rows_all[:, r1:r1 + r2].reshape(da, d)
    wout = rows_all[:, r1 + r2:].reshape(d, d)

    (dyssd, d_o, dzatt, dgraw, dzo, mrg, da_, db_, yatt, dout, red_tail) = _tail(
        yssd, o, zatt, graw, head, x2, tgt2, wps, wpa, wout, gate_bias, norm_post)

    tw = _tile(d, (512, 256, 128))
    g_wout = _mm(mrg, dzo, "tn", BF16, tw, tw, "wgrad_out")
    g_wps = _mm(yssd, da_, "tn", BF16, _tile(ds, (512, 256, 128)), tw, "wgrad_ps")
    g_wpa = _mm(yatt, db_, "tn", BF16, _tile(da, (512, 256, 128)), tw, "wgrad_pa")

    dk, dv, dq, dcs, rsum = _attn_bwd(q, k, v, o, d_o, lse_rep, ck, blk)
    dcum = jnp.transpose((rsum - dcs)[:, :, 0:2], (1, 0, 2)).reshape(p, ha)
    dcf = jnp.pad(dcum, ((0, 0), (hs, LANES - hs - ha)))
    dxbc, dzs, ddtf, gcw, gcb, gnrm, gsm = _ssd_bwd(
        dyssd, y, zs, xbc, pre, dtf, hin, dcf, conv_w_full, brow, alog_row, dskip_l, ssd_norm, sel_t, sel, hs, ha)
    ddtf_b = ddtf.astype(BF16)

    dsegs = [dzs, dxbc, dzatt, dq, dk, dv, dgraw, ddtf_b]
    gsegs = [_mm(dsg, u, "tn", BF16, _tile(dsg.shape[1], (512, 256, 128)), tw, "wgrad_in_%d" % i)
             for i, dsg in enumerate(dsegs)]
    g_z, g_xbc, g_zatt, g_q, g_k, g_v, g_g, g_dtf = gsegs
    gw_full = jnp.concatenate([g_z, g_xbc, g_dtf[:hs], g_zatt, g_q, g_k, g_v, g_dtf[hs:hs + ha], g_g], axis=0)
    gwin_parts = gw_full.reshape(N_DEV, nc8, d)
    grows_parts = jnp.concatenate([g_wps.reshape(N_DEV, r1, d), g_wpa.reshape(N_DEV, r2, d),
                                   g_wout.reshape(N_DEV, r3, d)], axis=1)

    core = lax.axis_index("c").astype(jnp.int32).reshape(1)
    sib_win, sib_rows = _exchange_sibling([gwin_parts, grows_parts], "scatter_grads_sibling")
    chip_win = _pair_add(gwin_parts, sib_win, core, "pair_add_w_in")
    chip_rows = _pair_add(grows_parts, sib_rows, core, "pair_add_rows")
    sems, thru, lands, token = _exchange_chips_start([chip_win, chip_rows], "scatter_grads_start")
    dsegs_after = dsegs[:-1] + [ddtf_b + token[0:1, 0:1].astype(BF16)]
    du = _mm_sum_nn(dsegs_after, seg_w + [w_dtf], tm, _tile(d, (256, 128)), "dgrad_in")
    gx, ghead, gnp = _prenorm_bwd(head, x2, norm_pre, du, dout)
    sent, got = _exchange_chips_wait(sems, thru, lands, gnp, "scatter_grads_wait")
    chip = me // 2
    recv_win, recv_rows = [lax.dynamic_update_slice_in_dim(g, lax.dynamic_slice_in_dim(s, chip, 1, axis=0), chip, axis=0)
                           for g, s in zip(got, sent)]
    gmisc = jnp.concatenate([gsm[0:1], gsm[1:2], gsm[2:3], _pad_cols(red_tail[2:3, 0:1], LANES)], axis=1)
    small_g = jnp.concatenate([
        _pack_small_rep(gnp[0:1], red_tail[0:1, :d], red_tail[1:2], gnrm[0:1], gcb[0:1], gmisc, cd),
        _pad_cols(gcw[0:CONV_K], cd), jnp.zeros((4, cd), F32), _pad_cols(ghead[PADN:], cd)], axis=0)
    sg_sems, sg_thru, sg_land, sg_token = _bcast_start(small_g, "reduce_small_start")

    zero1 = jnp.zeros((1, 1), F32)
    upd_in = _adamw(jnp.transpose(w_in[0]) + sg_token[0:1, 0:1], recv_win, jnp.transpose(m_w_in[0]),
                    jnp.transpose(v_w_in[0]), "adamw_w_in", parts=True)
    cat3 = lambda a, b, c: jnp.concatenate([a[0], b[0], c[0]], axis=0)
    upd_rows = _adamw(cat3(w_proj_ssd, w_proj_att, w_out) + sg_token[0:1, 0:1], recv_rows,
                      cat3(m_w_proj_ssd, m_w_proj_att, m_w_out),
                      cat3(v_w_proj_ssd, v_w_proj_att, v_w_out), "adamw_rows", parts=True)
    both_done = upd_in[1][0:8, 0:LANES] + upd_rows[1][0:8, 0:LANES]
    red = _sum_slots(_bcast_wait(sg_sems, sg_thru, sg_land, both_done, "reduce_small_wait"), "reduce_small_sum")
    loss = red[5, 3 * LANES]
    g_small_sh = _pack_small_shard(lax.dynamic_slice_in_dim(red[8:8 + CONV_K], me * cws, cws, axis=1),
                                   lax.dynamic_slice_in_dim(red[16:16 + N_META, :d], me * msh, msh, axis=1), cws)
    rep = lambda a, b, c, e, f, g1, g2, g3, g4: _pack_small_rep(a, b, c, e, f, _misc_row(g1, g2, g3, g4, zero1), cd)
    upd_rep = _adamw(rep(norm_pre, norm_post, gate_bias, ssd_norm, conv_b, dt_bias, fgate_bias, a_log, d_skip),
                     red[0:8],
                     rep(m_norm_pre, m_norm_post, m_gate_bias, m_ssd_norm, m_conv_b, m_dt_bias, m_fgate_bias, m_a_log, m_d_skip),
                     rep(v_norm_pre, v_norm_post, v_gate_bias, v_ssd_norm, v_conv_b, v_dt_bias, v_fgate_bias, v_a_log, v_d_skip),
                     "adamw_rep")
    upd_sh = _adamw(small_sh, g_small_sh, _pack_small_shard(m_conv_w[0], m_meta_tokens, cws),
                    _pack_small_shard(v_conv_w[0], v_meta_tokens, cws), "adamw_small_shard")

    def leaves(i):
        a_in, a_rows, a_rep, a_sh = upd_in[i], upd_rows[i], upd_rep[i], upd_sh[i]
        misc = a_rep[5:6]
        return [a_sh[8:8 + N_META, :msh], a_rep[0:1, :d], jnp.transpose(a_in)[None], a_sh[0:CONV_K][None], a_rep[4:5, :cd],
                misc[:, :hs], misc[:, LANES:LANES + hs], misc[:, 2 * LANES:2 * LANES + hs], a_rep[3:4, :ds],
                misc[:, hs:hs + ha], a_rep[2:3, :2 * d], a_rows[:r1][None], a_rows[r1:r1 + r2][None],
                a_rows[r1 + r2:][None], a_rep[1:2, :d]]

    return tuple([loss, gx[None]] + leaves(0) + leaves(1) + leaves(2) + leaves(3))
```

```python
import functools
import math

import jax
import jax.numpy as jnp
from jax import lax
from jax.experimental import pallas as pl
from jax.experimental.pallas import tpu as pltpu

F32 = jnp.float32
BF16 = jnp.bfloat16

N_DEV = 8
N_META = 16
CHUNK = 128
PADN = CHUNK - N_META
HEAD_DIM = 64
SSD_GROUPS = 4
CONV_K = 4
EPS = 1e-6
NEG = -1e30
LANES = 128
HALO = 16

ADAM_LR = 0.001
ADAM_B1 = 0.9
ADAM_B2 = 0.999
ADAM_EPS = 1e-08
ADAM_WD = 0.01
ADAM_STEP = 10

VMEM_LIMIT = 56 * 1024 * 1024

NN = (((1,), (0,)), ((), ()))
NT = (((1,), (1,)), ((), ()))
TN = (((0,), (0,)), ((), ()))
MESH = pl.DeviceIdType.MESH


def _dot(a, b, dims=NN):
    return lax.dot_general(a, b, dims, preferred_element_type=F32)


def _split2(x):
    hi = x.astype(BF16)
    lo = (x - hi.astype(F32)).astype(BF16)
    return hi, lo


def _dot_sel(x, sel):
    hi, lo = _split2(x)
    return _dot(hi, sel) + _dot(lo, sel)


def _dot_tri(tri, x):
    h1 = x.astype(BF16)
    r1 = x - h1.astype(F32)
    h2 = r1.astype(BF16)
    h3 = (r1 - h2.astype(F32)).astype(BF16)
    return _dot(tri, h1) + _dot(tri, h2) + _dot(tri, h3)


def _sigmoid(x):
    return 1.0 / (1.0 + jnp.exp(-x))


def _softplus(x):
    return jnp.maximum(x, 0.0) + jnp.log(1.0 + jnp.exp(-jnp.abs(x)))


def _cparams(sem=None, vmem=VMEM_LIMIT):
    kw = {"vmem_limit_bytes": vmem}
    if sem is not None:
        kw["dimension_semantics"] = sem
    return pltpu.CompilerParams(**kw)


def _full(shape):
    nd = len(shape)
    return pl.BlockSpec(shape, lambda *_: (0,) * nd)


def _att_block(p):
    return 384 if p % 384 == 0 else CHUNK


def _my_pos():
    return lax.axis_index("x"), lax.axis_index("y"), lax.axis_index("c")


def _dev_index(x, y, c):
    return 4 * x + 2 * y + c


FLIPS = [(fx, fy, fc) for fx in (0, 1) for fy in (0, 1) for fc in (0, 1)][1:]


def _flip(pos, f):
    return tuple((1 - p) if fi else p for p, fi in zip(pos, f))


def _all_gather(bufs, name):
    nb = len(bufs)

    def body(*refs):
        ins, outs = refs[:nb], refs[nb:2 * nb]
        send_sems, recv_sems, local_sems = refs[2 * nb:]
        x, y, c = _my_pos()
        me = _dev_index(x, y, c)
        sibling = (x, y, 1 - c)
        chips = [(1 - x, y), (x, 1 - y), (1 - x, 1 - y)]

        def copy(b, k, block_idx, to, src=None):
            dst = outs[b].at[block_idx]
            return pltpu.make_async_remote_copy(
                src_ref=dst if src is None else src, dst_ref=dst,
                send_sem=send_sems.at[b, k], recv_sem=recv_sems.at[b, k],
                device_id=to, device_id_type=MESH)

        started = []
        for b in range(nb):
            mine = pltpu.make_async_copy(ins[b], outs[b].at[me], local_sems.at[b])
            mine.start()
            started.append(mine)
        first = []
        for b in range(nb):
            first.append(copy(b, 0, me, sibling, src=ins[b]))
            for j, chip in enumerate(chips):
                first.append(copy(b, 1 + j, me, (chip[0], chip[1], c), src=ins[b]))
        for cp in first:
            cp.start()
        passed = []
        for j, chip in enumerate(chips):
            blk = _dev_index(chip[0], chip[1], c)
            for b in range(nb):
                copy(b, 1 + j, blk, (x, y, c)).wait_recv()
                fwd = copy(b, 4 + j, blk, sibling)
                fwd.start()
                passed.append(fwd)
        for b in range(nb):
            copy(b, 0, _dev_index(x, y, 1 - c), (x, y, c)).wait_recv()
        for j, chip in enumerate(chips):
            blk = _dev_index(chip[0], chip[1], 1 - c)
            for b in range(nb):
                copy(b, 4 + j, blk, (x, y, c)).wait_recv()
        for cp in first + passed:
            cp.wait_send()
        for mine in started:
            mine.wait()

    any_spec = pl.BlockSpec(memory_space=pl.ANY)
    return pl.pallas_call(
        body, name=name,
        out_shape=[jax.ShapeDtypeStruct((N_DEV,) + b.shape, b.dtype) for b in bufs],
        in_specs=[any_spec] * nb, out_specs=[any_spec] * nb,
        scratch_shapes=[pltpu.SemaphoreType.DMA((nb, 7)), pltpu.SemaphoreType.DMA((nb, 7)),
                        pltpu.SemaphoreType.DMA((nb,))],
    )(*bufs)


N_CHIP = 4
CHIP_FLIPS = [(1, 0), (0, 1), (1, 1)]


def _exchange_sibling(bufs, name):
    nb = len(bufs)

    def body(*refs):
        ins, outs = refs[:nb], refs[nb:2 * nb]
        send_sems, recv_sems = refs[2 * nb:]
        x, y, c = _my_pos()

        def copy(b, k):
            return pltpu.make_async_remote_copy(
                src_ref=ins[b].at[2 * k + (1 - c)], dst_ref=outs[b].at[k],
                send_sem=send_sems.at[b, k], recv_sem=recv_sems.at[b, k],
                device_id=(x, y, 1 - c), device_id_type=MESH)

        cps = [copy(b, k) for b in range(nb) for k in range(N_CHIP)]
        for cp in cps:
            cp.start()
        for cp in cps:
            cp.wait()

    any_spec = pl.BlockSpec(memory_space=pl.ANY)
    return pl.pallas_call(
        body, name=name,
        out_shape=[jax.ShapeDtypeStruct((N_CHIP,) + b.shape[1:], b.dtype) for b in bufs],
        in_specs=[any_spec] * nb, out_specs=[any_spec] * nb,
        scratch_shapes=[pltpu.SemaphoreType.DMA((nb, N_CHIP)), pltpu.SemaphoreType.DMA((nb, N_CHIP))],
    )(*bufs)


def _pair_add(mine, recv, core, name):
    _, r, cdim = mine.shape
    tr, tc, by_rows = _tiles_2d(r, cdim)
    pick = (lambda i: (i, 0)) if by_rows else (lambda i: (0, i))

    def body(core_ref, a_ref, b_ref, o_ref):
        o_ref[0] = (a_ref[0, 0].astype(F32) + b_ref[0].astype(F32)).astype(o_ref.dtype)

    return pl.pallas_call(
        body, name=name,
        grid_spec=pltpu.PrefetchScalarGridSpec(
            num_scalar_prefetch=1, grid=(N_CHIP, (r // tr) * (cdim // tc)),
            in_specs=[pl.BlockSpec((1, 1, tr, tc), lambda k, i, core_ref: (k, core_ref[0]) + pick(i)),
                      pl.BlockSpec((1, tr, tc), lambda k, i, core_ref: (k,) + pick(i))],
            out_specs=pl.BlockSpec((1, tr, tc), lambda k, i, core_ref: (k,) + pick(i))),
        out_shape=jax.ShapeDtypeStruct((N_CHIP, r, cdim), mine.dtype),
        compiler_params=_cparams(("parallel", "parallel")),
    )(core, mine.reshape(N_CHIP, 2, r, cdim), recv)


def _chip_peer(x, y, f):
    return ((1 - x) if f[0] else x), ((1 - y) if f[1] else y)


def _exchange_chips_start(bufs, name):
    nb = len(bufs)
    nsem = 2 * 3 * nb

    def body(*refs):
        ins, lands = refs[:nb], refs[nb:2 * nb]
        sems = refs[2 * nb:2 * nb + nsem]
        token = refs[-1]
        x, y, c = _my_pos()
        for b in range(nb):
            for j, f in enumerate(CHIP_FLIPS):
                px, py = _chip_peer(x, y, f)
                pltpu.make_async_remote_copy(
                    src_ref=ins[b].at[2 * px + py], dst_ref=lands[b].at[2 * x + y],
                    send_sem=sems[2 * (3 * b + j)], recv_sem=sems[2 * (3 * b + j) + 1],
                    device_id=(px, py, c), device_id_type=MESH).start()
        token[...] = jnp.zeros_like(token)

    hbm = pl.BlockSpec(memory_space=pltpu.HBM)
    sem = pl.BlockSpec(memory_space=pltpu.SEMAPHORE)
    out = pl.pallas_call(
        body, name=name,
        out_shape=(*([pltpu.SemaphoreType.DMA(())] * nsem),
                   *[pltpu.HBM(b.shape, b.dtype) for b in bufs], *[pltpu.HBM(b.shape, b.dtype) for b in bufs],
                   jax.ShapeDtypeStruct((8, LANES), F32)),
        in_specs=[hbm] * (2 * nb),
        out_specs=(*([sem] * nsem), *([hbm] * (2 * nb)), pl.BlockSpec(memory_space=pltpu.VMEM)),
        input_output_aliases={i: nsem + i for i in range(2 * nb)},
        compiler_params=pltpu.CompilerParams(has_side_effects=pltpu.SideEffectType.DATAFLOW_SIDE_EFFECTING),
    )(*[pltpu.with_memory_space_constraint(b, pltpu.HBM) for b in bufs],
      *[pltpu.with_memory_space_constraint(lax.empty(b.shape, b.dtype), pltpu.HBM) for b in bufs])
    return out[:nsem], out[nsem:nsem + nb], out[nsem + nb:nsem + 2 * nb], out[-1]


def _exchange_chips_wait(sems, thru, lands, after, name):
    nb = len(thru)
    nsem = len(sems)

    def body(*refs):
        ins, lnd = refs[:nb], refs[nb:2 * nb]
        sem_refs = refs[2 * nb:2 * nb + nsem]
        x, y, c = _my_pos()
        for b in range(nb):
            for j, f in enumerate(CHIP_FLIPS):
                px, py = _chip_peer(x, y, f)
                cp = pltpu.make_async_remote_copy(
                    src_ref=ins[b].at[2 * px + py], dst_ref=lnd[b].at[2 * px + py],
                    send_sem=sem_refs[2 * (3 * b + j)], recv_sem=sem_refs[2 * (3 * b + j) + 1],
                    device_id=(px, py, c), device_id_type=MESH)
                cp.wait_send()
                cp.wait_recv()

    hbm = pl.BlockSpec(memory_space=pltpu.HBM)
    sem = pl.BlockSpec(memory_space=pltpu.SEMAPHORE)
    out = pl.pallas_call(
        body, name=name,
        out_shape=tuple([pltpu.HBM(b.shape, b.dtype) for b in thru] + [pltpu.HBM(b.shape, b.dtype) for b in lands]),
        in_specs=[hbm] * (2 * nb) + [sem] * nsem + [pl.BlockSpec(memory_space=pl.ANY)],
        out_specs=tuple([hbm] * (2 * nb)),
        input_output_aliases={i: i for i in range(2 * nb)},
        compiler_params=pltpu.CompilerParams(has_side_effects=pltpu.SideEffectType.DATAFLOW_SIDE_EFFECTING),
    )(*thru, *lands, *sems, after)
    return out[:nb], out[nb:]


def _bcast_start(buf, name):
    nsem = 2 * len(FLIPS)

    def body(src, land, *rest):
        sems, token = rest[:nsem], rest[-1]
        pos = _my_pos()
        for k, f in enumerate(FLIPS):
            pltpu.make_async_remote_copy(
                src_ref=src, dst_ref=land.at[_dev_index(*pos)], send_sem=sems[2 * k], recv_sem=sems[2 * k + 1],
                device_id=_flip(pos, f), device_id_type=MESH).start()
        token[...] = jnp.zeros_like(token)

    hbm = pl.BlockSpec(memory_space=pltpu.HBM)
    sem = pl.BlockSpec(memory_space=pltpu.SEMAPHORE)
    land_shape = (N_DEV,) + buf.shape
    out = pl.pallas_call(
        body, name=name,
        out_shape=(*([pltpu.SemaphoreType.DMA(())] * nsem), pltpu.HBM(buf.shape, buf.dtype),
                   pltpu.HBM(land_shape, buf.dtype), jax.ShapeDtypeStruct((8, LANES), F32)),
        in_specs=[hbm, hbm],
        out_specs=(*([sem] * nsem), hbm, hbm, pl.BlockSpec(memory_space=pltpu.VMEM)),
        input_output_aliases={0: nsem, 1: nsem + 1},
        compiler_params=pltpu.CompilerParams(has_side_effects=pltpu.SideEffectType.DATAFLOW_SIDE_EFFECTING),
    )(pltpu.with_memory_space_constraint(buf, pltpu.HBM),
      pltpu.with_memory_space_constraint(lax.empty(land_shape, buf.dtype), pltpu.HBM))
    return out[:nsem], out[nsem], out[nsem + 1], out[-1]


def _bcast_wait(sems, thru, land, after, name):
    nsem = len(sems)

    def body(src, lnd, *rest):
        sem_refs = rest[:nsem]
        pos = _my_pos()
        for k, f in enumerate(FLIPS):
            peer = _flip(pos, f)
            cp = pltpu.make_async_remote_copy(
                src_ref=src, dst_ref=lnd.at[_dev_index(*peer)], send_sem=sem_refs[2 * k],
                recv_sem=sem_refs[2 * k + 1], device_id=peer, device_id_type=MESH)
            cp.wait_send()
            cp.wait_recv()

    hbm = pl.BlockSpec(memory_space=pltpu.HBM)
    sem = pl.BlockSpec(memory_space=pltpu.SEMAPHORE)
    sent, got = pl.pallas_call(
        body, name=name,
        out_shape=(pltpu.HBM(thru.shape, thru.dtype), pltpu.HBM(land.shape, land.dtype)),
        in_specs=[hbm, hbm] + [sem] * nsem + [pl.BlockSpec(memory_space=pl.ANY)],
        out_specs=(hbm, hbm), input_output_aliases={0: 0, 1: 1},
        compiler_params=pltpu.CompilerParams(has_side_effects=pltpu.SideEffectType.DATAFLOW_SIDE_EFFECTING),
    )(thru, land, *sems, after)
    return lax.dynamic_update_slice_in_dim(got, sent[None], _dev_index(*_my_pos()), axis=0)


def _sum_slots(v, name):
    _, r, cdim = v.shape

    def body(v_ref, o_ref):
        acc = v_ref[0]
        for s in range(1, N_DEV):
            acc = acc + v_ref[s]
        o_ref[...] = acc

    return pl.pallas_call(
        body, name=name, out_shape=jax.ShapeDtypeStruct((r, cdim), F32),
        in_specs=[_full((N_DEV, r, cdim))], out_specs=_full((r, cdim)), grid=(1,),
        compiler_params=_cparams(("arbitrary",)),
    )(v)


def _mm(a, b, dims, out_dtype, tm, tn, name):
    if dims == "nn":
        (m, k), (_, n) = a.shape, b.shape
        a_spec = pl.BlockSpec((tm, k), lambda j, i: (i, 0))
        b_spec = pl.BlockSpec((k, tn), lambda j, i: (0, j))
        dn = NN
    elif dims == "nt":
        (m, k), (n, _) = a.shape, b.shape
        a_spec = pl.BlockSpec((tm, k), lambda j, i: (i, 0))
        b_spec = pl.BlockSpec((tn, k), lambda j, i: (j, 0))
        dn = NT
    else:
        (k, m), (_, n) = a.shape, b.shape
        a_spec = pl.BlockSpec((k, tm), lambda j, i: (0, i))
        b_spec = pl.BlockSpec((k, tn), lambda j, i: (0, j))
        dn = TN
    assert m % tm == 0 and n % tn == 0, (m, tm, n, tn)

    def body(a_ref, b_ref, o_ref):
        o_ref[...] = _dot(a_ref[...], b_ref[...], dn).astype(o_ref.dtype)

    return pl.pallas_call(
        body, name=name, grid=(n // tn, m // tm),
        in_specs=[a_spec, b_spec], out_specs=pl.BlockSpec((tm, tn), lambda j, i: (i, j)),
        out_shape=jax.ShapeDtypeStruct((m, n), out_dtype),
        compiler_params=_cparams(("parallel", "parallel")),
    )(a, b)


def _tiles_2d(r, cdim):
    if r % CHUNK == 0:
        return CHUNK, cdim, True
    return r, _tile(cdim, (256, 128)), False


def _mm_sum_nn(a_list, b_list, tm, tn, name):
    n_op = len(a_list)
    m, n = a_list[0].shape[0], b_list[0].shape[1]

    def body(*refs):
        acc = _dot(refs[0][...], refs[n_op][...])
        for i in range(1, n_op):
            acc = acc + _dot(refs[i][...], refs[n_op + i][...])
        refs[2 * n_op][...] = acc

    return pl.pallas_call(
        body, name=name, grid=(n // tn, m // tm),
        in_specs=([pl.BlockSpec((tm, a.shape[1]), lambda j, i: (i, 0)) for a in a_list]
                  + [pl.BlockSpec((b.shape[0], tn), lambda j, i: (0, j)) for b in b_list]),
        out_specs=pl.BlockSpec((tm, tn), lambda j, i: (i, j)),
        out_shape=jax.ShapeDtypeStruct((m, n), F32),
        compiler_params=_cparams(("parallel", "parallel")),
    )(*a_list, *b_list)


def _tile(n, prefs):
    for t in prefs:
        if n % t == 0:
            return t
    return n


def _prenorm_fwd(head, x2, w):
    p, d = x2.shape[0] + CHUNK, x2.shape[1]

    def body(head_ref, x_ref, w_ref, u_ref):
        i = pl.program_id(0)
        h = jnp.where(i == 0, head_ref[...], x_ref[...])
        ms = jnp.mean(h * h, axis=-1, keepdims=True)
        u_ref[...] = (h * lax.rsqrt(ms + EPS) * w_ref[...]).astype(BF16)

    return pl.pallas_call(
        body, name="prenorm_fwd", grid=(p // CHUNK,),
        in_specs=[_full((CHUNK, d)), pl.BlockSpec((CHUNK, d), lambda i: (jnp.maximum(i - 1, 0), 0)), _full((1, d))],
        out_specs=pl.BlockSpec((CHUNK, d), lambda i: (i, 0)),
        out_shape=jax.ShapeDtypeStruct((p, d), BF16),
        compiler_params=_cparams(("arbitrary",)),
    )(head, x2, w)


def _prenorm_bwd(head, x2, w, du, dout):
    p, d = x2.shape[0] + CHUNK, x2.shape[1]

    def body(head_ref, x_ref, w_ref, du_ref, dout_ref, gx_ref, ghead_ref, gw_ref):
        i = pl.program_id(0)
        h = jnp.where(i == 0, head_ref[...], x_ref[...])
        rstd = lax.rsqrt(jnp.mean(h * h, axis=-1, keepdims=True) + EPS)
        xhat = h * rstd
        dub = du_ref[...]
        dxh = dub * w_ref[...]
        dh = rstd * (dxh - xhat * jnp.mean(dxh * xhat, axis=-1, keepdims=True)) + dout_ref[...]

        @pl.when(i == 0)
        def _():
            ghead_ref[...] = dh
            gw_ref[...] = jnp.zeros_like(gw_ref)

        gx_ref[...] = dh
        gw_ref[0:1, :] += jnp.sum(dub * xhat, axis=0, keepdims=True)

    return pl.pallas_call(
        body, name="prenorm_bwd", grid=(p // CHUNK,),
        in_specs=[_full((CHUNK, d)), pl.BlockSpec((CHUNK, d), lambda i: (jnp.maximum(i - 1, 0), 0)), _full((1, d)),
                  pl.BlockSpec((CHUNK, d), lambda i: (i, 0)), pl.BlockSpec((CHUNK, d), lambda i: (i, 0))],
        out_specs=[pl.BlockSpec((CHUNK, d), lambda i: (jnp.maximum(i - 1, 0), 0)), _full((CHUNK, d)), _full((8, d))],
        out_shape=[jax.ShapeDtypeStruct(x2.shape, F32), jax.ShapeDtypeStruct((CHUNK, d), F32),
                   jax.ShapeDtypeStruct((8, d), F32)],
        compiler_params=_cparams(("arbitrary",)),
    )(head, x2, w, du, dout)


def _conv_pre(ext_ref, cw_ref, cb_ref):
    pre = cb_ref[...] + cw_ref[CONV_K - 1:CONV_K, :] * ext_ref[8:8 + CHUNK, :]
    for j in range(1, CONV_K):
        pre = pre + cw_ref[CONV_K - 1 - j:CONV_K - j, :] * ext_ref[8 - j:8 - j + CHUNK, :]
    return pre


def _ssd_scalars(dtf_ref, brow_ref, alog_ref, rowmask, hs, ha, tri):
    lane = lax.broadcasted_iota(jnp.int32, (1, LANES), 1)
    is_dt = lane < hs
    is_f = (lane >= hs) & (lane < hs + ha)
    dtr = dtf_ref[...] + brow_ref[...]
    sp = _softplus(dtr)
    dt = jnp.where(is_dt, sp, 0.0) * rowmask
    logf = jnp.where(is_f, jnp.minimum(dtr, 0.0) - jnp.log(1.0 + jnp.exp(-jnp.abs(dtr))), 0.0) * rowmask
    a_row = jnp.where(is_dt, -jnp.exp(alog_ref[...]), 0.0)
    run = _dot_tri(tri, dt * a_row + logf)
    return dtr, dt, a_row, run, is_dt, is_f


def _tri_mats():
    r = lax.broadcasted_iota(jnp.int32, (CHUNK, CHUNK), 0)
    c = lax.broadcasted_iota(jnp.int32, (CHUNK, CHUNK), 1)
    return r, c


def _ssd_fwd(xbc, z, dtf, conv_w, conv_b, brow, alog, dskip_l, ssd_norm, sel_t, hs, ha):
    p, cd = xbc.shape
    ds = z.shape[1]
    ns = (cd - ds) // (2 * SSD_GROUPS)
    gw = ds // SSD_GROUPS
    nch = p // CHUNK
    hpg = hs // SSD_GROUPS

    def body(xbc_ref, halo_ref, z_ref, dtf_ref, cw_ref, cb_ref, brow_ref, alog_ref, dsk_ref, nrm_ref, selt_ref,
             y_ref, yssd_ref, hin_ref, cf_ref, pre_ref, st_ref, carry_ref, yacc_ref, xc_s, ex_s, xdtb_s, xwb_s, ext_s):
        c = pl.program_id(0)

        @pl.when(c == 0)
        def _():
            st_ref[...] = jnp.zeros_like(st_ref)
            carry_ref[...] = jnp.zeros_like(carry_ref)

        rows = lax.broadcasted_iota(jnp.int32, (CHUNK, 1), 0)
        rowmask = jnp.where((rows >= PADN) | (c > 0), 1.0, 0.0)
        ri, ci = _tri_mats()
        causal = ri >= ci
        tri = jnp.where(causal, 1.0, 0.0).astype(BF16)

        ext_s[0:8, :] = halo_ref[...].astype(F32)[HALO - 8:, :] * jnp.where(c > 0, 1.0, 0.0)
        ext_s[8:, :] = xbc_ref[...].astype(F32)
        pre = _conv_pre(ext_s, cw_ref, cb_ref)
        pre_ref[...] = pre.astype(BF16)
        xc_s[...] = pre * _sigmoid(pre) * rowmask

        dtr, dt, a_row, run, is_dt, is_f = _ssd_scalars(dtf_ref, brow_ref, alog_ref, rowmask, hs, ha, tri)
        cf = run + carry_ref[...]
        cf_ref[...] = cf
        carry_ref[...] = jnp.where(is_f, cf[CHUNK - 1:CHUNK, :], 0.0)
        cs = jnp.where(is_dt, run, 0.0)
        cl = cs[CHUNK - 1:CHUNK, :]
        selt = selt_ref[...]
        ex_s[...] = _dot_sel(jnp.exp(cs), selt)
        cdec_x = _dot_sel(jnp.broadcast_to(jnp.exp(cl), (8, LANES)), selt)[0:1, :]
        cs_t = cs.T
        xdt = xc_s[:, :ds] * _dot_sel(dt, selt)
        xdtb_s[...] = xdt.astype(BF16)
        xwb_s[...] = (xdt * _dot_sel(jnp.exp(cl - cs), selt)).astype(BF16)

        lane = lax.broadcasted_iota(jnp.int32, (1, LANES), 1)
        half0 = lane < HEAD_DIM
        for g in range(SSD_GROUPS):
            bg = xc_s[:, ds + g * ns: ds + (g + 1) * ns].astype(BF16)
            cg = xc_s[:, ds + SSD_GROUPS * ns + g * ns: ds + SSD_GROUPS * ns + (g + 1) * ns].astype(BF16)
            gm = _dot(cg, bg, NT)
            gs = slice(g * gw, (g + 1) * gw)
            stg = st_ref[:, gs]
            stg_b = stg.astype(BF16)
            hin_ref[0, :, gs] = stg_b
            yoff = _dot(cg, stg_b) * ex_s[:, gs]
            for pr in range(gw // LANES):
                sl = slice(g * gw + pr * LANES, g * gw + (pr + 1) * LANES)
                xp = xdtb_s[:, sl]
                yd = jnp.zeros((CHUNK, LANES), F32)
                for j in range(2):
                    h = g * hpg + 2 * pr + j
                    seg = cs[:, h:h + 1] - cs_t[h:h + 1, :]
                    m = jnp.where(causal, gm * jnp.exp(jnp.minimum(seg, 0.0)), 0.0).astype(BF16)
                    sel = half0 if j == 0 else jnp.logical_not(half0)
                    yd = yd + _dot(m, jnp.where(sel, xp, jnp.zeros_like(xp)))
                yacc_ref[:, sl] = yd + yoff[:, pr * LANES:(pr + 1) * LANES] + dsk_ref[:, sl] * xc_s[:, sl]
            st_ref[:, gs] = stg * cdec_x[:, gs] + _dot(bg, xwb_s[:, gs], TN)

        y = yacc_ref[...]
        y_ref[...] = y.astype(BF16)
        zf = z_ref[...].astype(F32)
        u = y * zf * _sigmoid(zf)
        for g in range(SSD_GROUPS):
            gs = slice(g * gw, (g + 1) * gw)
            ug = u[:, gs]
            ms = jnp.mean(ug * ug, axis=-1, keepdims=True)
            yssd_ref[:, gs] = (ug * lax.rsqrt(ms + EPS) * nrm_ref[:, gs]).astype(BF16)

    rb = CHUNK // HALO
    return pl.pallas_call(
        body, name="ssd_fwd", grid=(nch,),
        in_specs=[pl.BlockSpec((CHUNK, cd), lambda c: (c, 0)),
                  pl.BlockSpec((HALO, cd), lambda c: (jnp.maximum(c * rb - 1, 0), 0)),
                  pl.BlockSpec((CHUNK, ds), lambda c: (c, 0)),
                  pl.BlockSpec((CHUNK, LANES), lambda c: (c, 0)),
                  _full((CONV_K, cd)), _full((1, cd)), _full((1, LANES)), _full((1, LANES)),
                  _full((1, ds)), _full((1, ds)), _full((LANES, ds))],
        out_specs=[pl.BlockSpec((CHUNK, ds), lambda c: (c, 0)), pl.BlockSpec((CHUNK, ds), lambda c: (c, 0)),
                   pl.BlockSpec((1, ns, ds), lambda c: (c, 0, 0)), pl.BlockSpec((CHUNK, LANES), lambda c: (c, 0)),
                   pl.BlockSpec((CHUNK, cd), lambda c: (c, 0))],
        out_shape=[jax.ShapeDtypeStruct((p, ds), BF16), jax.ShapeDtypeStruct((p, ds), BF16),
                   jax.ShapeDtypeStruct((nch, ns, ds), BF16), jax.ShapeDtypeStruct((p, LANES), F32),
                   jax.ShapeDtypeStruct((p, cd), BF16)],
        scratch_shapes=[pltpu.VMEM((ns, ds), F32), pltpu.VMEM((1, LANES), F32), pltpu.VMEM((CHUNK, ds), F32),
                        pltpu.VMEM((CHUNK, cd), F32), pltpu.VMEM((CHUNK, ds), F32),
                        pltpu.VMEM((CHUNK, ds), BF16), pltpu.VMEM((CHUNK, ds), BF16),
                        pltpu.VMEM((8 + CHUNK, cd), F32)],
        compiler_params=_cparams(("arbitrary",)),
    )(xbc, xbc, z, dtf, conv_w, conv_b, brow, alog, dskip_l, ssd_norm, sel_t)


def _ssd_bwd(dyssd, y, z, xbc, pre, dtf, hin, dcf, conv_w, brow, alog, dskip_l, ssd_norm, sel_t, sel, hs, ha):
    p, cd = xbc.shape
    ds = z.shape[1]
    ns = (cd - ds) // (2 * SSD_GROUPS)
    gw = ds // SSD_GROUPS
    nch = p // CHUNK
    hpg = hs // SSD_GROUPS

    def body(dyssd_ref, y_ref, z_ref, xbc_ref, pre_ref, dtf_ref, hin_ref, dcf_ref, cw_ref, brow_ref,
             alog_ref, dsk_ref, nrm_ref, selt_ref, sel_ref,
             dxbc_ref, dz_ref, ddtf_ref, gcw_ref, gcb_ref, gnrm_ref, gsm_ref,
             dst_ref, nxt_ref, fcar_ref, gdsk_ref, dxc_ref, xc_s, dsl_s, dtx_s, ex_s, wx_s, dy_s, xdtb_s, xwb_s,
             dyb_s, dyeb_s):
        step = pl.program_id(0)
        c = nch - 1 - step

        @pl.when(step == 0)
        def _():
            dst_ref[...] = jnp.zeros_like(dst_ref)
            nxt_ref[...] = jnp.zeros_like(nxt_ref)
            fcar_ref[...] = jnp.zeros_like(fcar_ref)
            gdsk_ref[...] = jnp.zeros_like(gdsk_ref)
            gcw_ref[...] = jnp.zeros_like(gcw_ref)
            gcb_ref[...] = jnp.zeros_like(gcb_ref)
            gnrm_ref[...] = jnp.zeros_like(gnrm_ref)
            gsm_ref[...] = jnp.zeros_like(gsm_ref)

        rows = lax.broadcasted_iota(jnp.int32, (CHUNK, 1), 0)
        rowmask = jnp.where((rows >= PADN) | (c > 0), 1.0, 0.0)
        ri, ci = _tri_mats()
        causal = ri >= ci
        anti = ci >= ri
        tri = jnp.where(causal, 1.0, 0.0).astype(BF16)
        rtri = jnp.where(anti, 1.0, 0.0).astype(BF16)

        pre = pre_ref[...].astype(F32)
        sg = _sigmoid(pre)
        xc_s[...] = pre * sg * rowmask
        dsl_s[...] = sg * (1.0 + pre * (1.0 - sg)) * rowmask

        dtr, dt, a_row, run, is_dt, is_f = _ssd_scalars(dtf_ref, brow_ref, alog_ref, rowmask, hs, ha, tri)
        cs = jnp.where(is_dt, run, 0.0)
        cl = cs[CHUNK - 1:CHUNK, :]
        selt = selt_ref[...]
        selm = sel_ref[...]
        dtx_s[...] = _dot_sel(dt, selt)
        ex_s[...] = _dot_sel(jnp.exp(cs), selt)
        wx_s[...] = _dot_sel(jnp.exp(cl - cs), selt)
        cdec = jnp.exp(cl)
        cdec_x = _dot_sel(jnp.broadcast_to(cdec, (8, LANES)), selt)[0:1, :]
        cs_t = cs.T
        xdt = xc_s[:, :ds] * dtx_s[...]
        xdtb_s[...] = xdt.astype(BF16)
        xwb_s[...] = (xdt * wx_s[...]).astype(BF16)

        yv = y_ref[...].astype(F32)
        zf = z_ref[...].astype(F32)
        sz = _sigmoid(zf)
        u = yv * zf * sz
        dyo = dyssd_ref[...].astype(F32)
        du_parts = []
        for g in range(SSD_GROUPS):
            gs = slice(g * gw, (g + 1) * gw)
            ug = u[:, gs]
            rstd = lax.rsqrt(jnp.mean(ug * ug, axis=-1, keepdims=True) + EPS)
            yhat = ug * rstd
            dyg = dyo[:, gs]
            gnrm_ref[0:1, gs] += jnp.sum(dyg * yhat, axis=0, keepdims=True)
            dyh = dyg * nrm_ref[:, gs]
            du_parts.append(rstd * (dyh - yhat * jnp.mean(dyh * yhat, axis=-1, keepdims=True)))
        du = jnp.concatenate(du_parts, axis=1)
        dy = du * zf * sz
        dz_ref[...] = (du * yv * sz * (1.0 + zf * (1.0 - sz))).astype(BF16)
        dy_s[...] = dy
        dyb_s[...] = dy.astype(BF16)
        dyeb_s[...] = (dy * ex_s[...]).astype(BF16)
        gdsk_ref[...] += jnp.sum(dy * xc_s[:, :ds], axis=0, keepdims=True)
        lane = lax.broadcasted_iota(jnp.int32, (1, LANES), 1)
        half0 = lane < HEAD_DIM
        x_parts, yo_parts, t4_parts = [], [], []
        dcs = jnp.zeros((CHUNK, LANES), F32)
        for g in range(SSD_GROUPS):
            gs = slice(g * gw, (g + 1) * gw)
            bsl = slice(ds + g * ns, ds + (g + 1) * ns)
            csl = slice(ds + SSD_GROUPS * ns + g * ns, ds + SSD_GROUPS * ns + (g + 1) * ns)
            bg = xc_s[:, bsl].astype(BF16)
            cg = xc_s[:, csl].astype(BF16)
            gm = _dot(cg, bg, NT)
            gm_t = _dot(bg, cg, NT)
            stg_b = hin_ref[0, :, gs]
            dstg = dst_ref[:, gs]
            dstg_b = dstg.astype(BF16)
            t4_parts.append(jnp.sum(dstg * stg_b.astype(F32), axis=0, keepdims=True))
            zst = _dot(bg, dstg_b) * wx_s[:, gs]
            x_parts.append(xc_s[:, gs] * dtx_s[:, gs] * zst)
            yo_parts.append(dy_s[:, gs] * (_dot(cg, stg_b) * ex_s[:, gs]))
            dgsum = jnp.zeros((CHUNK, CHUNK), F32)
            dgtsum = jnp.zeros((CHUNK, CHUNK), F32)
            for pr in range(gw // LANES):
                sl = slice(g * gw + pr * LANES, g * gw + (pr + 1) * LANES)
                xp = xdtb_s[:, sl]
                dyp = dyb_s[:, sl]
                dxd = zst[:, pr * LANES:(pr + 1) * LANES]
                for j in range(2):
                    h = g * hpg + 2 * pr + j
                    sel_l = half0 if j == 0 else jnp.logical_not(half0)
                    seg = cs[:, h:h + 1] - cs_t[h:h + 1, :]
                    lm = jnp.where(causal, jnp.exp(jnp.minimum(seg, 0.0)), 0.0)
                    lmt = jnp.where(anti, jnp.exp(jnp.minimum(-seg, 0.0)), 0.0)
                    dyp_m = jnp.where(sel_l, dyp, jnp.zeros_like(dyp))
                    xp_m = jnp.where(sel_l, xp, jnp.zeros_like(xp))
                    dxd = dxd + _dot((gm_t * lmt).astype(BF16), dyp_m)
                    dg = _dot(dyp_m, xp, NT) * lm
                    dgt = _dot(xp_m, dyp, NT) * lmt
                    dgsum = dgsum + dg
                    dgtsum = dgtsum + dgt
                    qrow = (jnp.sum(dg * gm, axis=1, keepdims=True) - jnp.sum(dgt * gm_t, axis=1, keepdims=True))
                    dcs = dcs + jnp.where(lane == h, qrow, 0.0)
                dxc_ref[:, sl] = dxd
            dxc_ref[:, csl] = _dot(dgsum.astype(BF16), bg) + _dot(dyeb_s[:, gs], stg_b, NT)
            dxc_ref[:, bsl] = _dot(dgtsum.astype(BF16), cg) + _dot(xwb_s[:, gs], dstg_b, NT)
            dst_ref[:, gs] = dstg * cdec_x[:, gs] + _dot(cg, dyeb_s[:, gs], TN)

        dxdt = dxc_ref[:, :ds]
        xst = _dot_sel(jnp.concatenate(x_parts, axis=1), selm)
        yo = _dot_sel(jnp.concatenate(yo_parts, axis=1), selm)
        t4 = _dot_sel(jnp.concatenate([jnp.concatenate(t4_parts, axis=1), jnp.zeros((7, ds), F32)], axis=0), selm)
        dcl = jnp.sum(xst, axis=0, keepdims=True) + cdec * t4[0:1, :]
        dcs = dcs + yo - xst + jnp.where(rows == CHUNK - 1, dcl, 0.0)
        da_ = _dot_tri(rtri, dcs)
        ddt = _dot_sel(dxdt * xc_s[:, :ds], selm) + da_ * a_row
        dcf_blk = dcf_ref[...]
        dlogf = _dot_tri(rtri, dcf_blk) + fcar_ref[...]
        fcar_ref[...] += jnp.sum(dcf_blk, axis=0, keepdims=True)
        sgd = _sigmoid(dtr)
        ddtf = (jnp.where(is_dt, ddt * sgd, 0.0) + jnp.where(is_f, dlogf * (1.0 - sgd), 0.0)) * rowmask
        ddtf_ref[...] = ddtf
        gsm_ref[0:1, :] += jnp.sum(ddtf, axis=0, keepdims=True)
        gsm_ref[1:2, :] += jnp.sum(da_ * dt, axis=0, keepdims=True) * a_row

        dxc_ref[:, :ds] = dxdt * dtx_s[...] + dsk_ref[...] * dy_s[...]
        dpre = dxc_ref[...] * dsl_s[...]
        nxt_ref[0:CHUNK, :] = dpre
        gcb_ref[0:1, :] += jnp.sum(dpre, axis=0, keepdims=True)
        xr = xbc_ref[...].astype(F32)
        gcw_ref[CONV_K - 1:CONV_K, :] += jnp.sum(dpre * xr, axis=0, keepdims=True)
        dxr = cw_ref[CONV_K - 1:CONV_K, :] * dpre
        for j in range(1, CONV_K):
            up = nxt_ref[j:j + CHUNK, :]
            gcw_ref[CONV_K - 1 - j:CONV_K - j, :] += jnp.sum(up * xr, axis=0, keepdims=True)
            dxr = dxr + cw_ref[CONV_K - 1 - j:CONV_K - j, :] * up
        nxt_ref[CHUNK:, :] = dpre[0:8, :]
        dxbc_ref[...] = dxr.astype(BF16)

        @pl.when(step == nch - 1)
        def _():
            gsm_ref[2:3, :] = _dot_sel(jnp.broadcast_to(gdsk_ref[...], (8, ds)), selm)[0:1, :]

    rev = lambda s: nch - 1 - s
    blk = lambda w: pl.BlockSpec((CHUNK, w), lambda s: (rev(s), 0))
    return pl.pallas_call(
        body, name="ssd_bwd", grid=(nch,),
        in_specs=[blk(ds), blk(ds), blk(ds), blk(cd), blk(cd),
                  blk(LANES), pl.BlockSpec((1, ns, ds), lambda s: (rev(s), 0, 0)), blk(LANES),
                  _full((CONV_K, cd)), _full((1, LANES)), _full((1, LANES)),
                  _full((1, ds)), _full((1, ds)), _full((LANES, ds)), _full((ds, LANES))],
        out_specs=[blk(cd), blk(ds), blk(LANES), _full((8, cd)), _full((8, cd)), _full((8, ds)), _full((8, LANES))],
        out_shape=[jax.ShapeDtypeStruct((p, cd), BF16), jax.ShapeDtypeStruct((p, ds), BF16),
                   jax.ShapeDtypeStruct((p, LANES), F32), jax.ShapeDtypeStruct((8, cd), F32),
                   jax.ShapeDtypeStruct((8, cd), F32), jax.ShapeDtypeStruct((8, ds), F32),
                   jax.ShapeDtypeStruct((8, LANES), F32)],
        scratch_shapes=[pltpu.VMEM((ns, ds), F32), pltpu.VMEM((CHUNK + 8, cd), F32), pltpu.VMEM((1, LANES), F32),
                        pltpu.VMEM((1, ds), F32), pltpu.VMEM((CHUNK, cd), F32),
                        pltpu.VMEM((CHUNK, cd), F32), pltpu.VMEM((CHUNK, cd), F32),
                        pltpu.VMEM((CHUNK, ds), F32), pltpu.VMEM((CHUNK, ds), F32), pltpu.VMEM((CHUNK, ds), F32),
                        pltpu.VMEM((CHUNK, ds), F32), pltpu.VMEM((CHUNK, ds), BF16), pltpu.VMEM((CHUNK, ds), BF16),
                        pltpu.VMEM((CHUNK, ds), BF16), pltpu.VMEM((CHUNK, ds), BF16)],
        compiler_params=_cparams(("arbitrary",)),
    )(dyssd, y, z, xbc, pre, dtf, hin, dcf, conv_w, brow, alog, dskip_l, ssd_norm, sel_t, sel)


def _attn_fwd(q, k, v, ck, blk):
    p, da = q.shape
    npair, nkb = ck.shape[0], ck.shape[1]
    scale = 1.0 / math.sqrt(HEAD_DIM)

    def body(q_ref, k_ref, v_ref, ck_ref, o_ref, lse_ref):
        i = pl.program_id(1)
        lane = lax.broadcasted_iota(jnp.int32, (1, LANES), 1)
        sels = [lane < HEAD_DIM, lane >= HEAD_DIM]
        ones = [jnp.where(lane == HEAD_DIM, 1.0, 0.0).astype(BF16), jnp.where(lane == 0, 1.0, 0.0).astype(BF16)]
        qb = q_ref[...] * scale
        qms = [jnp.where(sel, qb, jnp.zeros_like(qb)) for sel in sels]
        cmask = (lax.broadcasted_iota(jnp.int32, (blk, blk), 1) <= lax.broadcasted_iota(jnp.int32, (blk, blk), 0))

        def step(kb, carry, masked, nk=1):
            r0 = pl.multiple_of(kb * blk, blk)
            ks = k_ref[pl.ds(r0, nk * blk), :]
            vs = v_ref[pl.ds(r0, nk * blk), :]
            out = []
            for j in range(2):
                m, acc = carry[2 * j], carry[2 * j + 1]
                ckr = jnp.concatenate([ck_ref[0, kb + t, j:j + 1, :] for t in range(nk)], axis=1)
                s = _dot(qms[j], ks, NT) - ckr
                if masked:
                    s = jnp.where(cmask, s, NEG)
                mn = jnp.maximum(m, jnp.max(s, axis=-1, keepdims=True))
                pr = jnp.exp(s - mn).astype(BF16)
                acc = jnp.exp(m - mn) * acc + _dot(pr, jnp.where(sels[j], vs, ones[j]))
                out += [mn, acc]
            return tuple(out)

        init = (jnp.full((blk, 1), NEG, F32), jnp.zeros((blk, LANES), F32)) * 2
        n4 = i // 4
        n2 = (i - 4 * n4) // 2
        carry = lax.fori_loop(0, n4, lambda t, c: step(4 * t, c, False, 4), init)
        carry = lax.fori_loop(0, n2, lambda t, c: step(4 * n4 + 2 * t, c, False, 2), carry)
        carry = lax.fori_loop(4 * n4 + 2 * n2, i, lambda kb, c: step(kb, c, False), carry)
        m0, a0, m1, a1 = step(i, carry, True)
        l0 = a0[:, HEAD_DIM:HEAD_DIM + 1]
        l1 = a1[:, 0:1]
        o_ref[...] = jnp.where(sels[0], a0 / l0, a1 / l1).astype(BF16)
        lse_ref[...] = jnp.where(sels[0], m0 + jnp.log(l0), m1 + jnp.log(l1))

    return pl.pallas_call(
        body, name="attn_fwd", grid=(npair, p // blk),
        in_specs=[pl.BlockSpec((blk, LANES), lambda h, i: (i, h)),
                  pl.BlockSpec((p, LANES), lambda h, i: (0, h)), pl.BlockSpec((p, LANES), lambda h, i: (0, h)),
                  pl.BlockSpec((1, nkb, 8, blk), lambda h, i: (h, 0, 0, 0))],
        out_specs=[pl.BlockSpec((blk, LANES), lambda h, i: (i, h)), pl.BlockSpec((blk, LANES), lambda h, i: (i, h))],
        out_shape=[jax.ShapeDtypeStruct((p, da), BF16), jax.ShapeDtypeStruct((p, da), F32)],
        compiler_params=_cparams(("parallel", "arbitrary")),
    )(q, k, v, ck)


def _attn_bwd(q, k, v, o, do, lse_rep, ck, blk):
    p, da = q.shape
    npair, nkb = ck.shape[0], ck.shape[1]
    nq = p // blk
    scale = 1.0 / math.sqrt(HEAD_DIM)

    def body(k_ref, v_ref, q_ref, do_ref, o_ref, lse_ref, ck_ref, dk_ref, dv_ref, dq_ref, dcs_ref, rsum_ref, dq_acc):
        jb = pl.program_id(1)

        @pl.when(jb == 0)
        def _():
            dq_acc[...] = jnp.zeros_like(dq_acc)

        ks = k_ref[...]
        vs = v_ref[...]
        lane = lax.broadcasted_iota(jnp.int32, (1, LANES), 1)
        sels = [lane < HEAD_DIM, lane >= HEAD_DIM]
        ones = [jnp.where(lane == HEAD_DIM, 1.0, 0.0).astype(BF16), jnp.where(lane == 0, 1.0, 0.0).astype(BF16)]
        kss = ks * scale
        kmo = [jnp.where(sels[j], kss, ones[j]) for j in range(2)]
        cmask = (lax.broadcasted_iota(jnp.int32, (blk, blk), 1) <= lax.broadcasted_iota(jnp.int32, (blk, blk), 0))

        def step(ib, carry, masked, nb=1):
            rows = nb * blk
            r0 = pl.multiple_of(ib * blk, blk)
            qb = q_ref[pl.ds(r0, rows), :] * scale
            dob = do_ref[pl.ds(r0, rows), :]
            prod = dob.astype(F32) * o_ref[pl.ds(r0, rows), :].astype(F32)
            out = []
            for j in range(2):
                dk, dv = carry[2 * j], carry[2 * j + 1]
                qm = jnp.where(sels[j], qb, jnp.zeros_like(qb))
                dom = jnp.where(sels[j], dob, jnp.zeros_like(dob))
                lse = lse_ref[pl.ds(r0, rows), HEAD_DIM * j:HEAD_DIM * j + 1]
                dlt = jnp.sum(jnp.where(sels[j], prod, 0.0), axis=-1, keepdims=True)
                s = _dot(qm, ks, NT) - ck_ref[0, 0, j:j + 1, :] - lse
                pm = jnp.exp(jnp.minimum(s, 0.0))
                if masked:
                    pm = jnp.where(cmask, pm, 0.0)
                ds_b = (pm * (_dot(dom, vs, NT) - dlt)).astype(BF16)
                dv = dv + _dot(pm.astype(BF16), dom, TN)
                dk = dk + _dot(ds_b, jnp.where(sels[j], qb, ones[j]), TN)
                dq_acc[pl.ds(r0, rows), LANES * j:LANES * (j + 1)] += _dot(ds_b, kmo[j])
                out += [dk, dv]
            return tuple(out)

        zero = jnp.zeros((blk, LANES), F32)
        carry = step(jb, (zero, zero, zero, zero), True)
        n4 = (nq - 1 - jb) // 4
        n2 = (nq - 1 - jb - 4 * n4) // 2
        carry = lax.fori_loop(0, n4, lambda t, c: step(jb + 1 + 4 * t, c, False, 4), carry)
        carry = lax.fori_loop(0, n2, lambda t, c: step(jb + 1 + 4 * n4 + 2 * t, c, False, 2), carry)
        dk0, dv0, dk1, dv1 = lax.fori_loop(jb + 1 + 4 * n4 + 2 * n2, nq, lambda ib, c: step(ib, c, False), carry)
        dk_ref[...] = jnp.where(sels[0], dk0, dk1).astype(BF16)
        dv_ref[...] = (dv0 + dv1).astype(BF16)
        lane8 = lax.broadcasted_iota(jnp.int32, (1, 8), 1)
        pair8 = lambda c0, c1: jnp.where(lane8 == 0, c0, jnp.where(lane8 == 1, c1, 0.0))
        dcs_ref[0] = pair8(dk0[:, HEAD_DIM:HEAD_DIM + 1], dk1[:, 0:1])

        @pl.when(jb == nkb - 1)
        def _():
            a0 = dq_acc[:, :LANES]
            a1 = dq_acc[:, LANES:]
            dq_ref[...] = jnp.where(sels[0], a0, a1).astype(BF16)
            rsum_ref[0] = pair8(a0[:, HEAD_DIM:HEAD_DIM + 1], a1[:, 0:1])

    colblk = pl.BlockSpec((blk, LANES), lambda h, j: (j, h))
    colfull = pl.BlockSpec((p, LANES), lambda h, j: (0, h))
    ckspec = pl.BlockSpec((1, 1, 8, blk), lambda h, j: (h, j, 0, 0))
    return pl.pallas_call(
        body, name="attn_bwd", grid=(npair, nkb),
        in_specs=[colblk, colblk, colfull, colfull, colfull, colfull, ckspec],
        out_specs=[colblk, colblk, colfull, pl.BlockSpec((1, blk, 8), lambda h, j: (h, j, 0)),
                   pl.BlockSpec((1, p, 8), lambda h, j: (h, 0, 0))],
        out_shape=[jax.ShapeDtypeStruct((p, da), BF16), jax.ShapeDtypeStruct((p, da), BF16),
                   jax.ShapeDtypeStruct((p, da), BF16), jax.ShapeDtypeStruct((npair, p, 8), F32),
                   jax.ShapeDtypeStruct((npair, p, 8), F32)],
        scratch_shapes=[pltpu.VMEM((p, 2 * LANES), F32)],
        compiler_params=_cparams(("parallel", "arbitrary")),
    )(k, v, q, do, o, lse_rep, ck)


def _rows3(i):
    return jnp.maximum(3 * i - 1, 0), 3 * i, 3 * i + 1


def _tail_fwd(yssd, o, zatt, graw, head, x2, tgt2, wps, wpa, wout, gate_bias, norm_post, tm):
    p, ds = yssd.shape
    da = o.shape[1]
    d = x2.shape[1]
    nsub = tm // CHUNK

    def body(yssd_ref, o_ref, zatt_ref, g_ref, head_ref, *rest):
        x_refs, t_refs = rest[:nsub], rest[nsub:2 * nsub]
        (wps_ref, wpa_ref, wout_ref, gb_ref, np_ref,
         yatt_ref, mrg_ref, a_ref, b_ref, dzo_ref, dout_ref, red_ref) = rest[2 * nsub:]
        i = pl.program_id(0)

        @pl.when(i == 0)
        def _():
            red_ref[...] = jnp.zeros_like(red_ref)

        first = jnp.where(i == 0, head_ref[...], x_refs[0][...])
        h = jnp.concatenate([first] + [r[...] for r in x_refs[1:]], axis=0)
        tgt = jnp.concatenate([r[...] for r in t_refs], axis=0)
        rows = lax.broadcasted_iota(jnp.int32, (tm, 1), 0)
        valid = jnp.where((i > 0) | (rows >= CHUNK), 1.0, 0.0)
        ob = o_ref[...].astype(F32)
        za = zatt_ref[...].astype(F32)
        yatt_b = (ob * za * _sigmoid(za)).astype(BF16)
        yatt_ref[...] = yatt_b
        a = _dot(yssd_ref[...], wps_ref[...])
        b = _dot(yatt_b, wpa_ref[...])
        a_ref[...] = a.astype(BF16)
        b_ref[...] = b.astype(BF16)
        gr = g_ref[...].astype(F32) + gb_ref[...]
        mrg_b = (_sigmoid(gr[:, :d]) * a + _sigmoid(gr[:, d:]) * b).astype(BF16)
        mrg_ref[...] = mrg_b
        zo = _dot(mrg_b, wout_ref[...])
        rstd = lax.rsqrt(jnp.mean(zo * zo, axis=-1, keepdims=True) + EPS)
        zh = zo * rstd
        npw = np_ref[...]
        err = (h + zh * npw - tgt) * valid
        dout = err * (1.0 / d)
        dout_ref[...] = dout
        dzh = dout * npw
        dzo_ref[...] = (rstd * (dzh - zh * jnp.mean(dzh * zh, axis=-1, keepdims=True))).astype(BF16)
        red_ref[0:1, :] += jnp.sum(dout * zh, axis=0, keepdims=True)
        red_ref[1:2, 0:1] += jnp.sum(jnp.sum(err * err, axis=1, keepdims=True), axis=0, keepdims=True) * (0.5 / d)

    row = lambda w: pl.BlockSpec((tm, w), lambda i: (i, 0))
    once = lambda shape: pl.BlockSpec(shape, lambda i: (0,) * len(shape), pipeline_mode=pl.Buffered(1))
    if nsub == 1:
        subs = [pl.BlockSpec((CHUNK, d), lambda i: (jnp.maximum(i - 1, 0), 0))]
    else:
        subs = [pl.BlockSpec((CHUNK, d), functools.partial(lambda i, k: (_rows3(i)[k], 0), k=k)) for k in range(3)]
    sd = jax.ShapeDtypeStruct
    return pl.pallas_call(
        body, name="tail_fwd", grid=(p // tm,),
        in_specs=[row(ds), row(da), row(da), row(2 * d), _full((CHUNK, d))] + subs + subs
                 + [once((ds, d)), once((da, d)), once((d, d)), _full((1, 2 * d)), _full((1, d))],
        out_specs=[row(da), row(d), row(d), row(d), row(d), row(d), _full((8, d))],
        out_shape=[sd((p, da), BF16), sd((p, d), BF16), sd((p, d), BF16), sd((p, d), BF16), sd((p, d), BF16),
                   sd((p, d), F32), sd((8, d), F32)],
        compiler_params=_cparams(("arbitrary",)),
    )(yssd, o, zatt, graw, head, *([x2] * nsub), *([tgt2] * nsub), wps, wpa, wout, gate_bias, norm_post)


def _tail_bwd(dzo, a_b, b_b, graw, o, zatt, wps, wpa, wout, gate_bias, tm):
    p, d = dzo.shape
    ds, da = wps.shape[0], wpa.shape[0]

    def body(dzo_ref, a_ref, b_ref, g_ref, o_ref, zatt_ref, wps_ref, wpa_ref, wout_ref, gb_ref,
             da_ref, db_ref, dg_ref, dyssd_ref, do_ref, dzatt_ref, red_ref):
        i = pl.program_id(0)

        @pl.when(i == 0)
        def _():
            red_ref[...] = jnp.zeros_like(red_ref)

        gr = g_ref[...].astype(F32) + gb_ref[...]
        gs = _sigmoid(gr[:, :d])
        ga = _sigmoid(gr[:, d:])
        dm = _dot(dzo_ref[...], wout_ref[...], NT)
        da_b = (gs * dm).astype(BF16)
        db_b = (ga * dm).astype(BF16)
        da_ref[...] = da_b
        db_ref[...] = db_b
        dgs = dm * a_ref[...].astype(F32) * gs * (1.0 - gs)
        dga = dm * b_ref[...].astype(F32) * ga * (1.0 - ga)
        dg_ref[:, :d] = dgs.astype(BF16)
        dg_ref[:, d:] = dga.astype(BF16)
        red_ref[0:1, :d] += jnp.sum(dgs, axis=0, keepdims=True)
        red_ref[0:1, d:] += jnp.sum(dga, axis=0, keepdims=True)
        dyssd_ref[...] = _dot(da_b, wps_ref[...], NT).astype(BF16)
        dya = _dot(db_b, wpa_ref[...], NT)
        ob = o_ref[...].astype(F32)
        za = zatt_ref[...].astype(F32)
        sza = _sigmoid(za)
        do_ref[...] = (dya * za * sza).astype(BF16)
        dzatt_ref[...] = (dya * ob * sza * (1.0 + za * (1.0 - sza))).astype(BF16)

    row = lambda w: pl.BlockSpec((tm, w), lambda i: (i, 0))
    once = lambda shape: pl.BlockSpec(shape, lambda i: (0,) * len(shape), pipeline_mode=pl.Buffered(1))
    sd = jax.ShapeDtypeStruct
    return pl.pallas_call(
        body, name="tail_bwd", grid=(p // tm,),
        in_specs=[row(d), row(d), row(d), row(2 * d), row(da), row(da),
                  once((ds, d)), once((da, d)), once((d, d)), _full((1, 2 * d))],
        out_specs=[row(d), row(d), row(2 * d), row(ds), row(da), row(da), _full((8, 2 * d))],
        out_shape=[sd((p, d), BF16), sd((p, d), BF16), sd((p, 2 * d), BF16), sd((p, ds), BF16), sd((p, da), BF16),
                   sd((p, da), BF16), sd((8, 2 * d), F32)],
        compiler_params=_cparams(("arbitrary",)),
    )(dzo, a_b, b_b, graw, o, zatt, wps, wpa, wout, gate_bias)


def _adamw_math(w, g, m, v):
    m2 = ADAM_B1 * m + (1.0 - ADAM_B1) * g
    v2 = ADAM_B2 * v + (1.0 - ADAM_B2) * (g * g)
    m_hat = m2 / (1.0 - ADAM_B1 ** ADAM_STEP)
    v_hat = v2 / (1.0 - ADAM_B2 ** ADAM_STEP)
    delta = -ADAM_LR * (m_hat / (jnp.sqrt(v_hat) + ADAM_EPS) + ADAM_WD * w)
    return delta, m2, v2


def _adamw(w, g, m, v, name, parts=False):
    r, cdim = w.shape
    tr, tc, by_rows = _tiles_2d(r, cdim)
    pick = (lambda i: (i, 0)) if by_rows else (lambda i: (0, i))

    def body(w_ref, g_ref, m_ref, v_ref, go_ref, d_ref, mo_ref, vo_ref):
        if parts:
            g = g_ref[0].astype(F32)
            for s in range(1, g_ref.shape[0]):
                g = g + g_ref[s].astype(F32)
        else:
            g = g_ref[...]
        delta, m2, v2 = _adamw_math(w_ref[...], g, m_ref[...], v_ref[...])
        go_ref[...] = g
        d_ref[...] = delta
        mo_ref[...] = m2
        vo_ref[...] = v2

    blk = pl.BlockSpec((tr, tc), pick)
    gspec = pl.BlockSpec((g.shape[0], tr, tc), lambda i: (0,) + pick(i)) if parts else blk
    return pl.pallas_call(
        body, name=name, grid=((r // tr) * (cdim // tc),),
        in_specs=[blk, gspec, blk, blk], out_specs=[blk] * 4,
        out_shape=[jax.ShapeDtypeStruct((r, cdim), F32)] * 4,
        compiler_params=_cparams(("parallel",)),
    )(w, g, m, v)


def _pad_cols(a, width):
    return jnp.pad(a, ((0, 0), (0, width - a.shape[1])))


def _pack_small_shard(conv_w_sh, meta_sh, width):
    return jnp.concatenate([_pad_cols(conv_w_sh, width), jnp.zeros((4, width), F32), _pad_cols(meta_sh, width)], axis=0)


def _pack_small_rep(norm_pre, norm_post, gate_bias, ssd_norm, conv_b, misc, width):
    rows = [norm_pre, norm_post, gate_bias, ssd_norm, conv_b, misc]
    return jnp.concatenate([_pad_cols(r, width) for r in rows] + [jnp.zeros((2, width), F32)], axis=0)


def _misc_row(dt_bias, fgate_bias, a_log, d_skip, extra):
    hs, ha = dt_bias.shape[1], fgate_bias.shape[1]
    return jnp.concatenate([dt_bias, fgate_bias, jnp.zeros((1, LANES - hs - ha), F32), _pad_cols(a_log, LANES),
                            _pad_cols(d_skip, LANES), _pad_cols(extra, LANES)], axis=1)


def kernel(x, meta_tokens, norm_pre, w_in, conv_w, conv_b, dt_bias, a_log, d_skip, ssd_norm, fgate_bias, gate_bias, w_proj_ssd, w_proj_att, w_out, norm_post, loss_target, m_meta_tokens, m_norm_pre, m_w_in, m_conv_w, m_conv_b, m_dt_bias, m_a_log, m_d_skip, m_ssd_norm, m_fgate_bias, m_gate_bias, m_w_proj_ssd, m_w_proj_att, m_w_out, m_norm_post, v_meta_tokens, v_norm_pre, v_w_in, v_conv_w, v_conv_b, v_dt_bias, v_a_log, v_d_skip, v_ssd_norm, v_fgate_bias, v_gate_bias, v_w_proj_ssd, v_w_proj_att, v_w_out, v_norm_post):
    seq, d = x.shape[1], x.shape[2]
    p = seq + CHUNK
    hs, ha = dt_bias.shape[1], fgate_bias.shape[1]
    ds, cd = ssd_norm.shape[1], conv_b.shape[1]
    da = ha * HEAD_DIM
    nc8 = w_in.shape[2]
    cws = cd // N_DEV
    msh = d // N_DEV
    r1, r2, r3 = ds // N_DEV, da // N_DEV, d // N_DEV
    me = _dev_index(*_my_pos())
    x2, tgt2 = x[0], loss_target[0]

    win_sh = jnp.transpose(w_in[0]).astype(BF16)
    rows_sh = jnp.concatenate([w_proj_ssd[0], w_proj_att[0], w_out[0]], axis=0).astype(BF16)
    small_sh = _pack_small_shard(conv_w[0], meta_tokens, cws)
    win_all, small_all = _all_gather([win_sh, small_sh], "gather_weights")
    rows_sh, win_all = lax.optimization_barrier((rows_sh, win_all))
    rows_sems, rows_thru, rows_land, rows_token = _bcast_start(rows_sh, "gather_rows_start")
    w_full = win_all.reshape(N_DEV * nc8, d)
    cuts = [0, ds, ds + cd, ds + cd + hs, ds + cd + hs + da, ds + cd + hs + 2 * da, ds + cd + hs + 3 * da,
            ds + cd + hs + 4 * da, ds + cd + hs + 4 * da + ha, ds + cd + hs + 4 * da + ha + 2 * d]
    w_z, w_xbc, w_dt, w_zatt, w_q, w_k, w_v, w_f, w_g = [w_full[cuts[i]:cuts[i + 1]] for i in range(9)]
    w_dtf = jnp.concatenate([w_dt, w_f, jnp.zeros((LANES - hs - ha, d), BF16)], axis=0)
    conv_w_full = jnp.transpose(small_all[:, 0:CONV_K, :], (1, 0, 2)).reshape(CONV_K, cd)
    meta_full = jnp.transpose(small_all[:, 8:8 + N_META, :msh], (1, 0, 2)).reshape(N_META, d)
    head = jnp.concatenate([jnp.zeros((PADN, d), F32), meta_full + rows_token[0:1, 0:1]], axis=0)

    u = _prenorm_fwd(head, x2, norm_pre)
    tm = _att_block(p)
    seg_w = [w_z, w_xbc, w_zatt, w_q, w_k, w_v, w_g]
    zs, xbc, zatt, q, k, v, graw = [
        _mm(u, w, "nt", BF16, _tile(p, (1408, tm)), _tile(w.shape[0], (1024, 512, 256, 128)), "inproj_%d" % i)
        for i, w in enumerate(seg_w)]
    dtf = _mm(u, w_dtf, "nt", F32, _tile(p, (1408, tm)), LANES, "inproj_dtf")

    brow = jnp.concatenate([dt_bias, fgate_bias, jnp.zeros((1, LANES - hs - ha), F32)], axis=1)
    alog_row = _pad_cols(a_log, LANES)
    dskip_l = jnp.repeat(d_skip, HEAD_DIM, axis=1)
    sel_t = (lax.broadcasted_iota(jnp.int32, (LANES, ds), 1) // HEAD_DIM
             == lax.broadcasted_iota(jnp.int32, (LANES, ds), 0)).astype(BF16)
    sel = sel_t.T
    y, yssd, hin, cf, pre = _ssd_fwd(xbc, zs, dtf, conv_w_full, conv_b, brow, alog_row, dskip_l, ssd_norm, sel_t, hs, ha)

    blk = _att_block(p)
    nkb, npair = p // blk, ha // 2
    cum = jnp.where(lax.broadcasted_iota(jnp.int32, (p, 1), 0) < PADN, -NEG, cf[:, hs:hs + ha])
    ck = jnp.transpose(cum.T.reshape(npair, 2, nkb, blk), (0, 2, 1, 3))
    ck = jnp.pad(ck, ((0, 0), (0, 0), (0, 6), (0, 0)))
    o, lse_rep = _attn_fwd(q, k, v, ck, blk)

    rows_all = _bcast_wait(rows_sems, rows_thru, rows_land, lse_rep, "gather_rows_wait")
    wps = rows_all[:, :r1].reshape(ds, d)
    wpa = rows_all[:, r1:r1 + r2].reshape(da, d)
    wout = rows_all[:, r1 + r2:].reshape(d, d)

    yatt, mrg, a_b, b_b, dzo, dout, red_fwd = _tail_fwd(
        yssd, o, zatt, graw, head, x2, tgt2, wps, wpa, wout, gate_bias, norm_post, tm)
    da_, db_, dgraw, dyssd, d_o, dzatt, red_bwd = _tail_bwd(dzo, a_b, b_b, graw, o, zatt, wps, wpa, wout, gate_bias, tm)

    tw = _tile(d, (512, 256, 128))
    g_wout = _mm(mrg, dzo, "tn", BF16, tw, tw, "wgrad_out")
    g_wps = _mm(yssd, da_, "tn", BF16, _tile(ds, (512, 256, 128)), tw, "wgrad_ps")
    g_wpa = _mm(yatt, db_, "tn", BF16, _tile(da, (512, 256, 128)), tw, "wgrad_pa")

    dk, dv, dq, dcs, rsum = _attn_bwd(q, k, v, o, d_o, lse_rep, ck, blk)
    dcum = jnp.transpose((rsum - dcs)[:, :, 0:2], (1, 0, 2)).reshape(p, ha)
    dcf = jnp.pad(dcum, ((0, 0), (hs, LANES - hs - ha)))
    dxbc, dzs, ddtf, gcw, gcb, gnrm, gsm = _ssd_bwd(
        dyssd, y, zs, xbc, pre, dtf, hin, dcf, conv_w_full, brow, alog_row, dskip_l, ssd_norm, sel_t, sel, hs, ha)
    ddtf_b = ddtf.astype(BF16)

    dsegs = [dzs, dxbc, dzatt, dq, dk, dv, dgraw, ddtf_b]
    gsegs = [_mm(dsg, u, "tn", BF16, _tile(dsg.shape[1], (512, 256, 128)), tw, "wgrad_in_%d" % i)
             for i, dsg in enumerate(dsegs)]
    g_z, g_xbc, g_zatt, g_q, g_k, g_v, g_g, g_dtf = gsegs
    gw_full = jnp.concatenate([g_z, g_xbc, g_dtf[:hs], g_zatt, g_q, g_k, g_v, g_dtf[hs:hs + ha], g_g], axis=0)
    gwin_parts = gw_full.reshape(N_DEV, nc8, d)
    grows_parts = jnp.concatenate([g_wps.reshape(N_DEV, r1, d), g_wpa.reshape(N_DEV, r2, d),
                                   g_wout.reshape(N_DEV, r3, d)], axis=1)

    core = lax.axis_index("c").astype(jnp.int32).reshape(1)
    sib_win, sib_rows = _exchange_sibling([gwin_parts, grows_parts], "scatter_grads_sibling")
    chip_win = _pair_add(gwin_parts, sib_win, core, "pair_add_w_in")
    chip_rows = _pair_add(grows_parts, sib_rows, core, "pair_add_rows")
    sems, thru, lands, token = _exchange_chips_start([chip_win, chip_rows], "scatter_grads_start")
    dsegs_after = dsegs[:-1] + [ddtf_b + token[0:1, 0:1].astype(BF16)]
    du = _mm_sum_nn(dsegs_after, seg_w + [w_dtf], tm, _tile(d, (256, 128)), "dgrad_in")
    gx, ghead, gnp = _prenorm_bwd(head, x2, norm_pre, du, dout)
    sent, got = _exchange_chips_wait(sems, thru, lands, gnp, "scatter_grads_wait")
    chip = me // 2
    recv_win, recv_rows = [lax.dynamic_update_slice_in_dim(g, lax.dynamic_slice_in_dim(s, chip, 1, axis=0), chip, axis=0)
                           for g, s in zip(got, sent)]
    gmisc = jnp.concatenate([gsm[0:1], gsm[1:2], gsm[2:3], _pad_cols(red_fwd[1:2, 0:1], LANES)], axis=1)
    small_g = jnp.concatenate([
        _pack_small_rep(gnp[0:1], red_fwd[0:1], red_bwd[0:1], gnrm[0:1], gcb[0:1], gmisc, cd),
        _pad_cols(gcw[0:CONV_K], cd), jnp.zeros((4, cd), F32), _pad_cols(ghead[PADN:], cd)], axis=0)
    sg_sems, sg_thru, sg_land, sg_token = _bcast_start(small_g, "reduce_small_start")

    zero1 = jnp.zeros((1, 1), F32)
    upd_in = _adamw(jnp.transpose(w_in[0]) + sg_token[0:1, 0:1], recv_win, jnp.transpose(m_w_in[0]),
                    jnp.transpose(v_w_in[0]), "adamw_w_in", parts=True)
    cat3 = lambda a, b, c: jnp.concatenate([a[0], b[0], c[0]], axis=0)
    upd_rows = _adamw(cat3(w_proj_ssd, w_proj_att, w_out) + sg_token[0:1, 0:1], recv_rows,
                      cat3(m_w_proj_ssd, m_w_proj_att, m_w_out),
                      cat3(v_w_proj_ssd, v_w_proj_att, v_w_out), "adamw_rows", parts=True)
    both_done = upd_in[1][0:8, 0:LANES] + upd_rows[1][0:8, 0:LANES]
    red = _sum_slots(_bcast_wait(sg_sems, sg_thru, sg_land, both_done, "reduce_small_wait"), "reduce_small_sum")
    loss = red[5, 3 * LANES]
    g_small_sh = _pack_small_shard(lax.dynamic_slice_in_dim(red[8:8 + CONV_K], me * cws, cws, axis=1),
                                   lax.dynamic_slice_in_dim(red[16:16 + N_META, :d], me * msh, msh, axis=1), cws)
    rep = lambda a, b, c, e, f, g1, g2, g3, g4: _pack_small_rep(a, b, c, e, f, _misc_row(g1, g2, g3, g4, zero1), cd)
    upd_rep = _adamw(rep(norm_pre, norm_post, gate_bias, ssd_norm, conv_b, dt_bias, fgate_bias, a_log, d_skip),
                     red[0:8],
                     rep(m_norm_pre, m_norm_post, m_gate_bias, m_ssd_norm, m_conv_b, m_dt_bias, m_fgate_bias, m_a_log, m_d_skip),
                     rep(v_norm_pre, v_norm_post, v_gate_bias, v_ssd_norm, v_conv_b, v_dt_bias, v_fgate_bias, v_a_log, v_d_skip),
                     "adamw_rep")
    upd_sh = _adamw(small_sh, g_small_sh, _pack_small_shard(m_conv_w[0], m_meta_tokens, cws),
                    _pack_small_shard(v_conv_w[0], v_meta_tokens, cws), "adamw_small_shard")

    def leaves(i):
        a_in, a_rows, a_rep, a_sh = upd_in[i], upd_rows[i], upd_rep[i], upd_sh[i]
        misc = a_rep[5:6]
        return [a_sh[8:8 + N_META, :msh], a_rep[0:1, :d], jnp.transpose(a_in)[None], a_sh[0:CONV_K][None], a_rep[4:5, :cd],
                misc[:, :hs], misc[:, LANES:LANES + hs], misc[:, 2 * LANES:2 * LANES + hs], a_rep[3:4, :ds],
                misc[:, hs:hs + ha], a_rep[2:3, :2 * d], a_rows[:r1][None], a_rows[r1:r1 + r2][None],
                a_rows[r1 + r2:][None], a_rep[1:2, :d]]

    return tuple([loss, gx[None]] + leaves(0) + leaves(1) + leaves(2) + leaves(3))
```

```python
import functools
import math

import jax
import jax.numpy as jnp
from jax import lax
from jax.experimental import pallas as pl
from jax.experimental.pallas import tpu as pltpu

F32 = jnp.float32
BF16 = jnp.bfloat16

N_DEV = 8
N_META = 16
CHUNK = 128
PADN = CHUNK - N_META
HEAD_DIM = 64
SSD_GROUPS = 4
CONV_K = 4
EPS = 1e-6
NEG = -1e30
LANES = 128
HALO = 16

ADAM_LR = 0.001
ADAM_B1 = 0.9
ADAM_B2 = 0.999
ADAM_EPS = 1e-08
ADAM_WD = 0.01
ADAM_STEP = 10

VMEM_LIMIT = 56 * 1024 * 1024

NN = (((1,), (0,)), ((), ()))
NT = (((1,), (1,)), ((), ()))
TN = (((0,), (0,)), ((), ()))
MESH = pl.DeviceIdType.MESH


def _dot(a, b, dims=NN):
    return lax.dot_general(a, b, dims, preferred_element_type=F32)


def _split2(x):
    hi = x.astype(BF16)
    lo = (x - hi.astype(F32)).astype(BF16)
    return hi, lo


def _dot_sel(x, sel):
    hi, lo = _split2(x)
    return _dot(hi, sel) + _dot(lo, sel)


def _dot_tri(tri, x):
    h1 = x.astype(BF16)
    r1 = x - h1.astype(F32)
    h2 = r1.astype(BF16)
    h3 = (r1 - h2.astype(F32)).astype(BF16)
    return _dot(tri, h1) + _dot(tri, h2) + _dot(tri, h3)


def _sigmoid(x):
    return 1.0 / (1.0 + jnp.exp(-x))


def _softplus(x):
    return jnp.maximum(x, 0.0) + jnp.log(1.0 + jnp.exp(-jnp.abs(x)))


def _cparams(sem=None, vmem=VMEM_LIMIT):
    kw = {"vmem_limit_bytes": vmem}
    if sem is not None:
        kw["dimension_semantics"] = sem
    return pltpu.CompilerParams(**kw)


def _full(shape):
    nd = len(shape)
    return pl.BlockSpec(shape, lambda *_: (0,) * nd)


def _att_block(p):
    return 384 if p % 384 == 0 else CHUNK


def _my_pos():
    return lax.axis_index("x"), lax.axis_index("y"), lax.axis_index("c")


def _dev_index(x, y, c):
    return 4 * x + 2 * y + c


FLIPS = [(fx, fy, fc) for fx in (0, 1) for fy in (0, 1) for fc in (0, 1)][1:]


def _flip(pos, f):
    return tuple((1 - p) if fi else p for p, fi in zip(pos, f))


def _all_gather(bufs, name):
    nb = len(bufs)

    def body(*refs):
        ins, outs = refs[:nb], refs[nb:2 * nb]
        send_sems, recv_sems, local_sems = refs[2 * nb:]
        x, y, c = _my_pos()
        me = _dev_index(x, y, c)
        sibling = (x, y, 1 - c)
        chips = [(1 - x, y), (x, 1 - y), (1 - x, 1 - y)]

        def copy(b, k, block_idx, to, src=None):
            dst = outs[b].at[block_idx]
            return pltpu.make_async_remote_copy(
                src_ref=dst if src is None else src, dst_ref=dst,
                send_sem=send_sems.at[b, k], recv_sem=recv_sems.at[b, k],
                device_id=to, device_id_type=MESH)

        started = []
        for b in range(nb):
            mine = pltpu.make_async_copy(ins[b], outs[b].at[me], local_sems.at[b])
            mine.start()
            started.append(mine)
        first = []
        for b in range(nb):
            first.append(copy(b, 0, me, sibling, src=ins[b]))
            for j, chip in enumerate(chips):
                first.append(copy(b, 1 + j, me, (chip[0], chip[1], c), src=ins[b]))
        for cp in first:
            cp.start()
        passed = []
        for j, chip in enumerate(chips):
            blk = _dev_index(chip[0], chip[1], c)
            for b in range(nb):
                copy(b, 1 + j, blk, (x, y, c)).wait_recv()
                fwd = copy(b, 4 + j, blk, sibling)
                fwd.start()
                passed.append(fwd)
        for b in range(nb):
            copy(b, 0, _dev_index(x, y, 1 - c), (x, y, c)).wait_recv()
        for j, chip in enumerate(chips):
            blk = _dev_index(chip[0], chip[1], 1 - c)
            for b in range(nb):
                copy(b, 4 + j, blk, (x, y, c)).wait_recv()
        for cp in first + passed:
            cp.wait_send()
        for mine in started:
            mine.wait()

    any_spec = pl.BlockSpec(memory_space=pl.ANY)
    return pl.pallas_call(
        body, name=name,
        out_shape=[jax.ShapeDtypeStruct((N_DEV,) + b.shape, b.dtype) for b in bufs],
        in_specs=[any_spec] * nb, out_specs=[any_spec] * nb,
        scratch_shapes=[pltpu.SemaphoreType.DMA((nb, 7)), pltpu.SemaphoreType.DMA((nb, 7)),
                        pltpu.SemaphoreType.DMA((nb,))],
    )(*bufs)


N_CHIP = 4
CHIP_FLIPS = [(1, 0), (0, 1), (1, 1)]


def _exchange_sibling(bufs, name):
    nb = len(bufs)

    def body(*refs):
        ins, outs = refs[:nb], refs[nb:2 * nb]
        send_sems, recv_sems = refs[2 * nb:]
        x, y, c = _my_pos()

        def copy(b, k):
            return pltpu.make_async_remote_copy(
                src_ref=ins[b].at[2 * k + (1 - c)], dst_ref=outs[b].at[k],
                send_sem=send_sems.at[b, k], recv_sem=recv_sems.at[b, k],
                device_id=(x, y, 1 - c), device_id_type=MESH)

        cps = [copy(b, k) for b in range(nb) for k in range(N_CHIP)]
        for cp in cps:
            cp.start()
        for cp in cps:
            cp.wait()

    any_spec = pl.BlockSpec(memory_space=pl.ANY)
    return pl.pallas_call(
        body, name=name,
        out_shape=[jax.ShapeDtypeStruct((N_CHIP,) + b.shape[1:], b.dtype) for b in bufs],
        in_specs=[any_spec] * nb, out_specs=[any_spec] * nb,
        scratch_shapes=[pltpu.SemaphoreType.DMA((nb, N_CHIP)), pltpu.SemaphoreType.DMA((nb, N_CHIP))],
    )(*bufs)


def _pair_add(mine, recv, core, name):
    _, r, cdim = mine.shape
    tr, tc, by_rows = _tiles_2d(r, cdim)
    pick = (lambda i: (i, 0)) if by_rows else (lambda i: (0, i))

    def body(core_ref, a_ref, b_ref, o_ref):
        o_ref[0] = (a_ref[0, 0].astype(F32) + b_ref[0].astype(F32)).astype(o_ref.dtype)

    return pl.pallas_call(
        body, name=name,
        grid_spec=pltpu.PrefetchScalarGridSpec(
            num_scalar_prefetch=1, grid=(N_CHIP, (r // tr) * (cdim // tc)),
            in_specs=[pl.BlockSpec((1, 1, tr, tc), lambda k, i, core_ref: (k, core_ref[0]) + pick(i)),
                      pl.BlockSpec((1, tr, tc), lambda k, i, core_ref: (k,) + pick(i))],
            out_specs=pl.BlockSpec((1, tr, tc), lambda k, i, core_ref: (k,) + pick(i))),
        out_shape=jax.ShapeDtypeStruct((N_CHIP, r, cdim), mine.dtype),
        compiler_params=_cparams(("parallel", "parallel")),
    )(core, mine.reshape(N_CHIP, 2, r, cdim), recv)


def _chip_peer(x, y, f):
    return ((1 - x) if f[0] else x), ((1 - y) if f[1] else y)


def _exchange_chips_start(bufs, name):
    nb = len(bufs)
    nsem = 2 * 3 * nb

    def body(*refs):
        ins, lands = refs[:nb], refs[nb:2 * nb]
        sems = refs[2 * nb:2 * nb + nsem]
        token = refs[-1]
        x, y, c = _my_pos()
        for b in range(nb):
            for j, f in enumerate(CHIP_FLIPS):
                px, py = _chip_peer(x, y, f)
                pltpu.make_async_remote_copy(
                    src_ref=ins[b].at[2 * px + py], dst_ref=lands[b].at[2 * x + y],
                    send_sem=sems[2 * (3 * b + j)], recv_sem=sems[2 * (3 * b + j) + 1],
                    device_id=(px, py, c), device_id_type=MESH).start()
        token[...] = jnp.zeros_like(token)

    hbm = pl.BlockSpec(memory_space=pltpu.HBM)
    sem = pl.BlockSpec(memory_space=pltpu.SEMAPHORE)
    out = pl.pallas_call(
        body, name=name,
        out_shape=(*([pltpu.SemaphoreType.DMA(())] * nsem),
                   *[pltpu.HBM(b.shape, b.dtype) for b in bufs], *[pltpu.HBM(b.shape, b.dtype) for b in bufs],
                   jax.ShapeDtypeStruct((8, LANES), F32)),
        in_specs=[hbm] * (2 * nb),
        out_specs=(*([sem] * nsem), *([hbm] * (2 * nb)), pl.BlockSpec(memory_space=pltpu.VMEM)),
        input_output_aliases={i: nsem + i for i in range(2 * nb)},
        compiler_params=pltpu.CompilerParams(has_side_effects=pltpu.SideEffectType.DATAFLOW_SIDE_EFFECTING),
    )(*[pltpu.with_memory_space_constraint(b, pltpu.HBM) for b in bufs],
      *[pltpu.with_memory_space_constraint(lax.empty(b.shape, b.dtype), pltpu.HBM) for b in bufs])
    return out[:nsem], out[nsem:nsem + nb], out[nsem + nb:nsem + 2 * nb], out[-1]


def _exchange_chips_wait(sems, thru, lands, after, name):
    nb = len(thru)
    nsem = len(sems)

    def body(*refs):
        ins, lnd = refs[:nb], refs[nb:2 * nb]
        sem_refs = refs[2 * nb:2 * nb + nsem]
        x, y, c = _my_pos()
        for b in range(nb):
            for j, f in enumerate(CHIP_FLIPS):
                px, py = _chip_peer(x, y, f)
                cp = pltpu.make_async_remote_copy(
                    src_ref=ins[b].at[2 * px + py], dst_ref=lnd[b].at[2 * px + py],
                    send_sem=sem_refs[2 * (3 * b + j)], recv_sem=sem_refs[2 * (3 * b + j) + 1],
                    device_id=(px, py, c), device_id_type=MESH)
                cp.wait_send()
                cp.wait_recv()

    hbm = pl.BlockSpec(memory_space=pltpu.HBM)
    sem = pl.BlockSpec(memory_space=pltpu.SEMAPHORE)
    out = pl.pallas_call(
        body, name=name,
        out_shape=tuple([pltpu.HBM(b.shape, b.dtype) for b in thru] + [pltpu.HBM(b.shape, b.dtype) for b in lands]),
        in_specs=[hbm] * (2 * nb) + [sem] * nsem + [pl.BlockSpec(memory_space=pl.ANY)],
        out_specs=tuple([hbm] * (2 * nb)),
        input_output_aliases={i: i for i in range(2 * nb)},
        compiler_params=pltpu.CompilerParams(has_side_effects=pltpu.SideEffectType.DATAFLOW_SIDE_EFFECTING),
    )(*thru, *lands, *sems, after)
    return out[:nb], out[nb:]


def _bcast_start(buf, name):
    nsem = 2 * len(FLIPS)

    def body(src, land, *rest):
        sems, token = rest[:nsem], rest[-1]
        pos = _my_pos()
        for k, f in enumerate(FLIPS):
            pltpu.make_async_remote_copy(
                src_ref=src, dst_ref=land.at[_dev_index(*pos)], send_sem=sems[2 * k], recv_sem=sems[2 * k + 1],
                device_id=_flip(pos, f), device_id_type=MESH).start()
        token[...] = jnp.zeros_like(token)

    hbm = pl.BlockSpec(memory_space=pltpu.HBM)
    sem = pl.BlockSpec(memory_space=pltpu.SEMAPHORE)
    land_shape = (N_DEV,) + buf.shape
    out = pl.pallas_call(
        body, name=name,
        out_shape=(*([pltpu.SemaphoreType.DMA(())] * nsem), pltpu.HBM(buf.shape, buf.dtype),
                   pltpu.HBM(land_shape, buf.dtype), jax.ShapeDtypeStruct((8, LANES), F32)),
        in_specs=[hbm, hbm],
        out_specs=(*([sem] * nsem), hbm, hbm, pl.BlockSpec(memory_space=pltpu.VMEM)),
        input_output_aliases={0: nsem, 1: nsem + 1},
        compiler_params=pltpu.CompilerParams(has_side_effects=pltpu.SideEffectType.DATAFLOW_SIDE_EFFECTING),
    )(pltpu.with_memory_space_constraint(buf, pltpu.HBM),
      pltpu.with_memory_space_constraint(lax.empty(land_shape, buf.dtype), pltpu.HBM))
    return out[:nsem], out[nsem], out[nsem + 1], out[-1]


def _bcast_wait(sems, thru, land, after, name):
    nsem = len(sems)

    def body(src, lnd, *rest):
        sem_refs = rest[:nsem]
        pos = _my_pos()
        for k, f in enumerate(FLIPS):
            peer = _flip(pos, f)
            cp = pltpu.make_async_remote_copy(
                src_ref=src, dst_ref=lnd.at[_dev_index(*peer)], send_sem=sem_refs[2 * k],
                recv_sem=sem_refs[2 * k + 1], device_id=peer, device_id_type=MESH)
            cp.wait_send()
            cp.wait_recv()

    hbm = pl.BlockSpec(memory_space=pltpu.HBM)
    sem = pl.BlockSpec(memory_space=pltpu.SEMAPHORE)
    sent, got = pl.pallas_call(
        body, name=name,
        out_shape=(pltpu.HBM(thru.shape, thru.dtype), pltpu.HBM(land.shape, land.dtype)),
        in_specs=[hbm, hbm] + [sem] * nsem + [pl.BlockSpec(memory_space=pl.ANY)],
        out_specs=(hbm, hbm), input_output_aliases={0: 0, 1: 1},
        compiler_params=pltpu.CompilerParams(has_side_effects=pltpu.SideEffectType.DATAFLOW_SIDE_EFFECTING),
    )(thru, land, *sems, after)
    return lax.dynamic_update_slice_in_dim(got, sent[None], _dev_index(*_my_pos()), axis=0)


def _sum_slots(v, name):
    _, r, cdim = v.shape

    def body(v_ref, o_ref):
        acc = v_ref[0]
        for s in range(1, N_DEV):
            acc = acc + v_ref[s]
        o_ref[...] = acc

    return pl.pallas_call(
        body, name=name, out_shape=jax.ShapeDtypeStruct((r, cdim), F32),
        in_specs=[_full((N_DEV, r, cdim))], out_specs=_full((r, cdim)), grid=(1,),
        compiler_params=_cparams(("arbitrary",)),
    )(v)


def _mm(a, b, dims, out_dtype, tm, tn, name):
    if dims == "nn":
        (m, k), (_, n) = a.shape, b.shape
        a_spec = pl.BlockSpec((tm, k), lambda j, i: (i, 0))
        b_spec = pl.BlockSpec((k, tn), lambda j, i: (0, j))
        dn = NN
    elif dims == "nt":
        (m, k), (n, _) = a.shape, b.shape
        a_spec = pl.BlockSpec((tm, k), lambda j, i: (i, 0))
        b_spec = pl.BlockSpec((tn, k), lambda j, i: (j, 0))
        dn = NT
    else:
        (k, m), (_, n) = a.shape, b.shape
        a_spec = pl.BlockSpec((k, tm), lambda j, i: (0, i))
        b_spec = pl.BlockSpec((k, tn), lambda j, i: (0, j))
        dn = TN
    assert m % tm == 0 and n % tn == 0, (m, tm, n, tn)

    def body(a_ref, b_ref, o_ref):
        o_ref[...] = _dot(a_ref[...], b_ref[...], dn).astype(o_ref.dtype)

    return pl.pallas_call(
        body, name=name, grid=(n // tn, m // tm),
        in_specs=[a_spec, b_spec], out_specs=pl.BlockSpec((tm, tn), lambda j, i: (i, j)),
        out_shape=jax.ShapeDtypeStruct((m, n), out_dtype),
        compiler_params=_cparams(("parallel", "parallel")),
    )(a, b)


def _tiles_2d(r, cdim):
    if r % CHUNK == 0:
        return CHUNK, cdim, True
    return r, _tile(cdim, (256, 128)), False


def _mm_sum_nn(a_list, b_list, tm, tn, name):
    n_op = len(a_list)
    m, n = a_list[0].shape[0], b_list[0].shape[1]

    def body(*refs):
        acc = _dot(refs[0][...], refs[n_op][...])
        for i in range(1, n_op):
            acc = acc + _dot(refs[i][...], refs[n_op + i][...])
        refs[2 * n_op][...] = acc

    return pl.pallas_call(
        body, name=name, grid=(n // tn, m // tm),
        in_specs=([pl.BlockSpec((tm, a.shape[1]), lambda j, i: (i, 0)) for a in a_list]
                  + [pl.BlockSpec((b.shape[0], tn), lambda j, i: (0, j)) for b in b_list]),
        out_specs=pl.BlockSpec((tm, tn), lambda j, i: (i, j)),
        out_shape=jax.ShapeDtypeStruct((m, n), F32),
        compiler_params=_cparams(("parallel", "parallel")),
    )(*a_list, *b_list)


def _tile(n, prefs):
    for t in prefs:
        if n % t == 0:
            return t
    return n


def _prenorm_fwd(head, x2, w):
    p, d = x2.shape[0] + CHUNK, x2.shape[1]

    def body(head_ref, x_ref, w_ref, u_ref):
        i = pl.program_id(0)
        h = jnp.where(i == 0, head_ref[...], x_ref[...])
        ms = jnp.mean(h * h, axis=-1, keepdims=True)
        u_ref[...] = (h * lax.rsqrt(ms + EPS) * w_ref[...]).astype(BF16)

    return pl.pallas_call(
        body, name="prenorm_fwd", grid=(p // CHUNK,),
        in_specs=[_full((CHUNK, d)), pl.BlockSpec((CHUNK, d), lambda i: (jnp.maximum(i - 1, 0), 0)), _full((1, d))],
        out_specs=pl.BlockSpec((CHUNK, d), lambda i: (i, 0)),
        out_shape=jax.ShapeDtypeStruct((p, d), BF16),
        compiler_params=_cparams(("arbitrary",)),
    )(head, x2, w)


def _prenorm_bwd(head, x2, w, du, dout):
    p, d = x2.shape[0] + CHUNK, x2.shape[1]

    def body(head_ref, x_ref, w_ref, du_ref, dout_ref, gx_ref, ghead_ref, gw_ref):
        i = pl.program_id(0)
        h = jnp.where(i == 0, head_ref[...], x_ref[...])
        rstd = lax.rsqrt(jnp.mean(h * h, axis=-1, keepdims=True) + EPS)
        xhat = h * rstd
        dub = du_ref[...]
        dxh = dub * w_ref[...]
        dh = rstd * (dxh - xhat * jnp.mean(dxh * xhat, axis=-1, keepdims=True)) + dout_ref[...]

        @pl.when(i == 0)
        def _():
            ghead_ref[...] = dh
            gw_ref[...] = jnp.zeros_like(gw_ref)

        gx_ref[...] = dh
        gw_ref[0:1, :] += jnp.sum(dub * xhat, axis=0, keepdims=True)

    return pl.pallas_call(
        body, name="prenorm_bwd", grid=(p // CHUNK,),
        in_specs=[_full((CHUNK, d)), pl.BlockSpec((CHUNK, d), lambda i: (jnp.maximum(i - 1, 0), 0)), _full((1, d)),
                  pl.BlockSpec((CHUNK, d), lambda i: (i, 0)), pl.BlockSpec((CHUNK, d), lambda i: (i, 0))],
        out_specs=[pl.BlockSpec((CHUNK, d), lambda i: (jnp.maximum(i - 1, 0), 0)), _full((CHUNK, d)), _full((8, d))],
        out_shape=[jax.ShapeDtypeStruct(x2.shape, F32), jax.ShapeDtypeStruct((CHUNK, d), F32),
                   jax.ShapeDtypeStruct((8, d), F32)],
        compiler_params=_cparams(("arbitrary",)),
    )(head, x2, w, du, dout)


def _conv_pre(ext_ref, cw_ref, cb_ref):
    pre = cb_ref[...] + cw_ref[CONV_K - 1:CONV_K, :] * ext_ref[8:8 + CHUNK, :]
    for j in range(1, CONV_K):
        pre = pre + cw_ref[CONV_K - 1 - j:CONV_K - j, :] * ext_ref[8 - j:8 - j + CHUNK, :]
    return pre


def _ssd_scalars(dtf_ref, brow_ref, alog_ref, rowmask, hs, ha, tri):
    lane = lax.broadcasted_iota(jnp.int32, (1, LANES), 1)
    is_dt = lane < hs
    is_f = (lane >= hs) & (lane < hs + ha)
    dtr = dtf_ref[...] + brow_ref[...]
    sp = _softplus(dtr)
    dt = jnp.where(is_dt, sp, 0.0) * rowmask
    logf = jnp.where(is_f, jnp.minimum(dtr, 0.0) - jnp.log(1.0 + jnp.exp(-jnp.abs(dtr))), 0.0) * rowmask
    a_row = jnp.where(is_dt, -jnp.exp(alog_ref[...]), 0.0)
    run = _dot_tri(tri, dt * a_row + logf)
    return dtr, dt, a_row, run, is_dt, is_f


def _tri_mats():
    r = lax.broadcasted_iota(jnp.int32, (CHUNK, CHUNK), 0)
    c = lax.broadcasted_iota(jnp.int32, (CHUNK, CHUNK), 1)
    return r, c


def _ssd_fwd(xbc, z, dtf, conv_w, conv_b, brow, alog, dskip_l, ssd_norm, sel_t, hs, ha):
    p, cd = xbc.shape
    ds = z.shape[1]
    ns = (cd - ds) // (2 * SSD_GROUPS)
    gw = ds // SSD_GROUPS
    nch = p // CHUNK
    hpg = hs // SSD_GROUPS

    def body(xbc_ref, halo_ref, z_ref, dtf_ref, cw_ref, cb_ref, brow_ref, alog_ref, dsk_ref, nrm_ref, selt_ref,
             y_ref, yssd_ref, hin_ref, cf_ref, pre_ref, st_ref, carry_ref, yacc_ref, xc_s, ex_s, xdtb_s, xwb_s, ext_s):
        c = pl.program_id(0)

        @pl.when(c == 0)
        def _():
            st_ref[...] = jnp.zeros_like(st_ref)
            carry_ref[...] = jnp.zeros_like(carry_ref)

        rows = lax.broadcasted_iota(jnp.int32, (CHUNK, 1), 0)
        rowmask = jnp.where((rows >= PADN) | (c > 0), 1.0, 0.0)
        ri, ci = _tri_mats()
        causal = ri >= ci
        tri = jnp.where(causal, 1.0, 0.0).astype(BF16)

        ext_s[0:8, :] = halo_ref[...].astype(F32)[HALO - 8:, :] * jnp.where(c > 0, 1.0, 0.0)
        ext_s[8:, :] = xbc_ref[...].astype(F32)
        pre = _conv_pre(ext_s, cw_ref, cb_ref)
        pre_ref[...] = pre.astype(BF16)
        xc_s[...] = pre * _sigmoid(pre) * rowmask

        dtr, dt, a_row, run, is_dt, is_f = _ssd_scalars(dtf_ref, brow_ref, alog_ref, rowmask, hs, ha, tri)
        cf = run + carry_ref[...]
        cf_ref[...] = cf
        carry_ref[...] = jnp.where(is_f, cf[CHUNK - 1:CHUNK, :], 0.0)
        cs = jnp.where(is_dt, run, 0.0)
        cl = cs[CHUNK - 1:CHUNK, :]
        selt = selt_ref[...]
        ex_s[...] = _dot_sel(jnp.exp(cs), selt)
        cdec_x = _dot_sel(jnp.broadcast_to(jnp.exp(cl), (8, LANES)), selt)[0:1, :]
        cs_t = cs.T
        xdt = xc_s[:, :ds] * _dot_sel(dt, selt)
        xdtb_s[...] = xdt.astype(BF16)
        xwb_s[...] = (xdt * _dot_sel(jnp.exp(cl - cs), selt)).astype(BF16)

        lane = lax.broadcasted_iota(jnp.int32, (1, LANES), 1)
        half0 = lane < HEAD_DIM
        for g in range(SSD_GROUPS):
            bg = xc_s[:, ds + g * ns: ds + (g + 1) * ns].astype(BF16)
            cg = xc_s[:, ds + SSD_GROUPS * ns + g * ns: ds + SSD_GROUPS * ns + (g + 1) * ns].astype(BF16)
            gm = _dot(cg, bg, NT)
            gs = slice(g * gw, (g + 1) * gw)
            stg = st_ref[:, gs]
            stg_b = stg.astype(BF16)
            hin_ref[0, :, gs] = stg_b
            yoff = _dot(cg, stg_b) * ex_s[:, gs]
            for pr in range(gw // LANES):
                sl = slice(g * gw + pr * LANES, g * gw + (pr + 1) * LANES)
                xp = xdtb_s[:, sl]
                yd = jnp.zeros((CHUNK, LANES), F32)
                for j in range(2):
                    h = g * hpg + 2 * pr + j
                    seg = cs[:, h:h + 1] - cs_t[h:h + 1, :]
                    m = jnp.where(causal, gm * jnp.exp(jnp.minimum(seg, 0.0)), 0.0).astype(BF16)
                    sel = half0 if j == 0 else jnp.logical_not(half0)
                    yd = yd + _dot(m, jnp.where(sel, xp, jnp.zeros_like(xp)))
                yacc_ref[:, sl] = yd + yoff[:, pr * LANES:(pr + 1) * LANES] + dsk_ref[:, sl] * xc_s[:, sl]
            st_ref[:, gs] = stg * cdec_x[:, gs] + _dot(bg, xwb_s[:, gs], TN)

        y = yacc_ref[...]
        y_ref[...] = y.astype(BF16)
        zf = z_ref[...].astype(F32)
        u = y * zf * _sigmoid(zf)
        for g in range(SSD_GROUPS):
            gs = slice(g * gw, (g + 1) * gw)
            ug = u[:, gs]
            ms = jnp.mean(ug * ug, axis=-1, keepdims=True)
            yssd_ref[:, gs] = (ug * lax.rsqrt(ms + EPS) * nrm_ref[:, gs]).astype(BF16)

    rb = CHUNK // HALO
    return pl.pallas_call(
        body, name="ssd_fwd", grid=(nch,),
        in_specs=[pl.BlockSpec((CHUNK, cd), lambda c: (c, 0)),
                  pl.BlockSpec((HALO, cd), lambda c: (jnp.maximum(c * rb - 1, 0), 0)),
                  pl.BlockSpec((CHUNK, ds), lambda c: (c, 0)),
                  pl.BlockSpec((CHUNK, LANES), lambda c: (c, 0)),
                  _full((CONV_K, cd)), _full((1, cd)), _full((1, LANES)), _full((1, LANES)),
                  _full((1, ds)), _full((1, ds)), _full((LANES, ds))],
        out_specs=[pl.BlockSpec((CHUNK, ds), lambda c: (c, 0)), pl.BlockSpec((CHUNK, ds), lambda c: (c, 0)),
                   pl.BlockSpec((1, ns, ds), lambda c: (c, 0, 0)), pl.BlockSpec((CHUNK, LANES), lambda c: (c, 0)),
                   pl.BlockSpec((CHUNK, cd), lambda c: (c, 0))],
        out_shape=[jax.ShapeDtypeStruct((p, ds), BF16), jax.ShapeDtypeStruct((p, ds), BF16),
                   jax.ShapeDtypeStruct((nch, ns, ds), BF16), jax.ShapeDtypeStruct((p, LANES), F32),
                   jax.ShapeDtypeStruct((p, cd), BF16)],
        scratch_shapes=[pltpu.VMEM((ns, ds), F32), pltpu.VMEM((1, LANES), F32), pltpu.VMEM((CHUNK, ds), F32),
                        pltpu.VMEM((CHUNK, cd), F32), pltpu.VMEM((CHUNK, ds), F32),
                        pltpu.VMEM((CHUNK, ds), BF16), pltpu.VMEM((CHUNK, ds), BF16),
                        pltpu.VMEM((8 + CHUNK, cd), F32)],
        compiler_params=_cparams(("arbitrary",)),
    )(xbc, xbc, z, dtf, conv_w, conv_b, brow, alog, dskip_l, ssd_norm, sel_t)


def _ssd_bwd(dyssd, y, z, xbc, pre, dtf, hin, dcf, conv_w, brow, alog, dskip_l, ssd_norm, sel_t, sel, hs, ha):
    p, cd = xbc.shape
    ds = z.shape[1]
    ns = (cd - ds) // (2 * SSD_GROUPS)
    gw = ds // SSD_GROUPS
    nch = p // CHUNK
    hpg = hs // SSD_GROUPS

    def body(dyssd_ref, y_ref, z_ref, xbc_ref, pre_ref, dtf_ref, hin_ref, dcf_ref, cw_ref, brow_ref,
             alog_ref, dsk_ref, nrm_ref, selt_ref, sel_ref,
             dxbc_ref, dz_ref, ddtf_ref, gcw_ref, gcb_ref, gnrm_ref, gsm_ref,
             dst_ref, nxt_ref, fcar_ref, gdsk_ref, dxc_ref, xc_s, dsl_s, dtx_s, ex_s, wx_s, dy_s, xdtb_s, xwb_s,
             dyb_s, dyeb_s):
        step = pl.program_id(0)
        c = nch - 1 - step

        @pl.when(step == 0)
        def _():
            dst_ref[...] = jnp.zeros_like(dst_ref)
            nxt_ref[...] = jnp.zeros_like(nxt_ref)
            fcar_ref[...] = jnp.zeros_like(fcar_ref)
            gdsk_ref[...] = jnp.zeros_like(gdsk_ref)
            gcw_ref[...] = jnp.zeros_like(gcw_ref)
            gcb_ref[...] = jnp.zeros_like(gcb_ref)
            gnrm_ref[...] = jnp.zeros_like(gnrm_ref)
            gsm_ref[...] = jnp.zeros_like(gsm_ref)

        rows = lax.broadcasted_iota(jnp.int32, (CHUNK, 1), 0)
        rowmask = jnp.where((rows >= PADN) | (c > 0), 1.0, 0.0)
        ri, ci = _tri_mats()
        causal = ri >= ci
        anti = ci >= ri
        tri = jnp.where(causal, 1.0, 0.0).astype(BF16)
        rtri = jnp.where(anti, 1.0, 0.0).astype(BF16)

        pre = pre_ref[...].astype(F32)
        sg = _sigmoid(pre)
        xc_s[...] = pre * sg * rowmask
        dsl_s[...] = sg * (1.0 + pre * (1.0 - sg)) * rowmask

        dtr, dt, a_row, run, is_dt, is_f = _ssd_scalars(dtf_ref, brow_ref, alog_ref, rowmask, hs, ha, tri)
        cs = jnp.where(is_dt, run, 0.0)
        cl = cs[CHUNK - 1:CHUNK, :]
        selt = selt_ref[...]
        selm = sel_ref[...]
        dtx_s[...] = _dot_sel(dt, selt)
        ex_s[...] = _dot_sel(jnp.exp(cs), selt)
        wx_s[...] = _dot_sel(jnp.exp(cl - cs), selt)
        cdec = jnp.exp(cl)
        cdec_x = _dot_sel(jnp.broadcast_to(cdec, (8, LANES)), selt)[0:1, :]
        cs_t = cs.T
        xdt = xc_s[:, :ds] * dtx_s[...]
        xdtb_s[...] = xdt.astype(BF16)
        xwb_s[...] = (xdt * wx_s[...]).astype(BF16)

        yv = y_ref[...].astype(F32)
        zf = z_ref[...].astype(F32)
        sz = _sigmoid(zf)
        u = yv * zf * sz
        dyo = dyssd_ref[...].astype(F32)
        du_parts = []
        for g in range(SSD_GROUPS):
            gs = slice(g * gw, (g + 1) * gw)
            ug = u[:, gs]
            rstd = lax.rsqrt(jnp.mean(ug * ug, axis=-1, keepdims=True) + EPS)
            yhat = ug * rstd
            dyg = dyo[:, gs]
            gnrm_ref[0:1, gs] += jnp.sum(dyg * yhat, axis=0, keepdims=True)
            dyh = dyg * nrm_ref[:, gs]
            du_parts.append(rstd * (dyh - yhat * jnp.mean(dyh * yhat, axis=-1, keepdims=True)))
        du = jnp.concatenate(du_parts, axis=1)
        dy = du * zf * sz
        dz_ref[...] = (du * yv * sz * (1.0 + zf * (1.0 - sz))).astype(BF16)
        dy_s[...] = dy
        dyb_s[...] = dy.astype(BF16)
        dyeb_s[...] = (dy * ex_s[...]).astype(BF16)
        gdsk_ref[...] += jnp.sum(dy * xc_s[:, :ds], axis=0, keepdims=True)
        lane = lax.broadcasted_iota(jnp.int32, (1, LANES), 1)
        half0 = lane < HEAD_DIM
        x_parts, yo_parts, t4_parts = [], [], []
        dcs = jnp.zeros((CHUNK, LANES), F32)
        for g in range(SSD_GROUPS):
            gs = slice(g * gw, (g + 1) * gw)
            bsl = slice(ds + g * ns, ds + (g + 1) * ns)
            csl = slice(ds + SSD_GROUPS * ns + g * ns, ds + SSD_GROUPS * ns + (g + 1) * ns)
            bg = xc_s[:, bsl].astype(BF16)
            cg = xc_s[:, csl].astype(BF16)
            gm = _dot(cg, bg, NT)
            gm_t = _dot(bg, cg, NT)
            stg_b = hin_ref[0, :, gs]
            dstg = dst_ref[:, gs]
            dstg_b = dstg.astype(BF16)
            t4_parts.append(jnp.sum(dstg * stg_b.astype(F32), axis=0, keepdims=True))
            zst = _dot(bg, dstg_b) * wx_s[:, gs]
            x_parts.append(xc_s[:, gs] * dtx_s[:, gs] * zst)
            yo_parts.append(dy_s[:, gs] * (_dot(cg, stg_b) * ex_s[:, gs]))
            dgsum = jnp.zeros((CHUNK, CHUNK), F32)
            dgtsum = jnp.zeros((CHUNK, CHUNK), F32)
            for pr in range(gw // LANES):
                sl = slice(g * gw + pr * LANES, g * gw + (pr + 1) * LANES)
                xp = xdtb_s[:, sl]
                dyp = dyb_s[:, sl]
                dxd = zst[:, pr * LANES:(pr + 1) * LANES]
                for j in range(2):
                    h = g * hpg + 2 * pr + j
                    sel_l = half0 if j == 0 else jnp.logical_not(half0)
                    seg = cs[:, h:h + 1] - cs_t[h:h + 1, :]
                    lm = jnp.where(causal, jnp.exp(jnp.minimum(seg, 0.0)), 0.0)
                    lmt = jnp.where(anti, jnp.exp(jnp.minimum(-seg, 0.0)), 0.0)
                    dyp_m = jnp.where(sel_l, dyp, jnp.zeros_like(dyp))
                    xp_m = jnp.where(sel_l, xp, jnp.zeros_like(xp))
                    dxd = dxd + _dot((gm_t * lmt).astype(BF16), dyp_m)
                    dg = _dot(dyp_m, xp, NT) * lm
                    dgt = _dot(xp_m, dyp, NT) * lmt
                    dgsum = dgsum + dg
                    dgtsum = dgtsum + dgt
                    qrow = (jnp.sum(dg * gm, axis=1, keepdims=True) - jnp.sum(dgt * gm_t, axis=1, keepdims=True))
                    dcs = dcs + jnp.where(lane == h, qrow, 0.0)
                dxc_ref[:, sl] = dxd
            dxc_ref[:, csl] = _dot(dgsum.astype(BF16), bg) + _dot(dyeb_s[:, gs], stg_b, NT)
            dxc_ref[:, bsl] = _dot(dgtsum.astype(BF16), cg) + _dot(xwb_s[:, gs], dstg_b, NT)
            dst_ref[:, gs] = dstg * cdec_x[:, gs] + _dot(cg, dyeb_s[:, gs], TN)

        dxdt = dxc_ref[:, :ds]
        xst = _dot_sel(jnp.concatenate(x_parts, axis=1), selm)
        yo = _dot_sel(jnp.concatenate(yo_parts, axis=1), selm)
        t4 = _dot_sel(jnp.concatenate([jnp.concatenate(t4_parts, axis=1), jnp.zeros((7, ds), F32)], axis=0), selm)
        dcl = jnp.sum(xst, axis=0, keepdims=True) + cdec * t4[0:1, :]
        dcs = dcs + yo - xst + jnp.where(rows == CHUNK - 1, dcl, 0.0)
        da_ = _dot_tri(rtri, dcs)
        ddt = _dot_sel(dxdt * xc_s[:, :ds], selm) + da_ * a_row
        dcf_blk = dcf_ref[...]
        dlogf = _dot_tri(rtri, dcf_blk) + fcar_ref[...]
        fcar_ref[...] += jnp.sum(dcf_blk, axis=0, keepdims=True)
        sgd = _sigmoid(dtr)
        ddtf = (jnp.where(is_dt, ddt * sgd, 0.0) + jnp.where(is_f, dlogf * (1.0 - sgd), 0.0)) * rowmask
        ddtf_ref[...] = ddtf
        gsm_ref[0:1, :] += jnp.sum(ddtf, axis=0, keepdims=True)
        gsm_ref[1:2, :] += jnp.sum(da_ * dt, axis=0, keepdims=True) * a_row

        dxc_ref[:, :ds] = dxdt * dtx_s[...] + dsk_ref[...] * dy_s[...]
        dpre = dxc_ref[...] * dsl_s[...]
        nxt_ref[0:CHUNK, :] = dpre
        gcb_ref[0:1, :] += jnp.sum(dpre, axis=0, keepdims=True)
        xr = xbc_ref[...].astype(F32)
        gcw_ref[CONV_K - 1:CONV_K, :] += jnp.sum(dpre * xr, axis=0, keepdims=True)
        dxr = cw_ref[CONV_K - 1:CONV_K, :] * dpre
        for j in range(1, CONV_K):
            up = nxt_ref[j:j + CHUNK, :]
            gcw_ref[CONV_K - 1 - j:CONV_K - j, :] += jnp.sum(up * xr, axis=0, keepdims=True)
            dxr = dxr + cw_ref[CONV_K - 1 - j:CONV_K - j, :] * up
        nxt_ref[CHUNK:, :] = dpre[0:8, :]
        dxbc_ref[...] = dxr.astype(BF16)

        @pl.when(step == nch - 1)
        def _():
            gsm_ref[2:3, :] = _dot_sel(jnp.broadcast_to(gdsk_ref[...], (8, ds)), selm)[0:1, :]

    rev = lambda s: nch - 1 - s
    blk = lambda w: pl.BlockSpec((CHUNK, w), lambda s: (rev(s), 0))
    return pl.pallas_call(
        body, name="ssd_bwd", grid=(nch,),
        in_specs=[blk(ds), blk(ds), blk(ds), blk(cd), blk(cd),
                  blk(LANES), pl.BlockSpec((1, ns, ds), lambda s: (rev(s), 0, 0)), blk(LANES),
                  _full((CONV_K, cd)), _full((1, LANES)), _full((1, LANES)),
                  _full((1, ds)), _full((1, ds)), _full((LANES, ds)), _full((ds, LANES))],
        out_specs=[blk(cd), blk(ds), blk(LANES), _full((8, cd)), _full((8, cd)), _full((8, ds)), _full((8, LANES))],
        out_shape=[jax.ShapeDtypeStruct((p, cd), BF16), jax.ShapeDtypeStruct((p, ds), BF16),
                   jax.ShapeDtypeStruct((p, LANES), F32), jax.ShapeDtypeStruct((8, cd), F32),
                   jax.ShapeDtypeStruct((8, cd), F32), jax.ShapeDtypeStruct((8, ds), F32),
                   jax.ShapeDtypeStruct((8, LANES), F32)],
        scratch_shapes=[pltpu.VMEM((ns, ds), F32), pltpu.VMEM((CHUNK + 8, cd), F32), pltpu.VMEM((1, LANES), F32),
                        pltpu.VMEM((1, ds), F32), pltpu.VMEM((CHUNK, cd), F32),
                        pltpu.VMEM((CHUNK, cd), F32), pltpu.VMEM((CHUNK, cd), F32),
                        pltpu.VMEM((CHUNK, ds), F32), pltpu.VMEM((CHUNK, ds), F32), pltpu.VMEM((CHUNK, ds), F32),
                        pltpu.VMEM((CHUNK, ds), F32), pltpu.VMEM((CHUNK, ds), BF16), pltpu.VMEM((CHUNK, ds), BF16),
                        pltpu.VMEM((CHUNK, ds), BF16), pltpu.VMEM((CHUNK, ds), BF16)],
        compiler_params=_cparams(("arbitrary",)),
    )(dyssd, y, z, xbc, pre, dtf, hin, dcf, conv_w, brow, alog, dskip_l, ssd_norm, sel_t, sel)


def _attn_fwd(q, k, v, ck, blk):
    p, da = q.shape
    npair, nkb = ck.shape[0], ck.shape[1]
    scale = 1.0 / math.sqrt(HEAD_DIM)

    def body(q_ref, k_ref, v_ref, ck_ref, o_ref, lse_ref):
        i = pl.program_id(1)
        lane = lax.broadcasted_iota(jnp.int32, (1, LANES), 1)
        sels = [lane < HEAD_DIM, lane >= HEAD_DIM]
        ones = [jnp.where(lane == HEAD_DIM, 1.0, 0.0).astype(BF16), jnp.where(lane == 0, 1.0, 0.0).astype(BF16)]
        qb = q_ref[...] * scale
        qms = [jnp.where(sel, qb, jnp.zeros_like(qb)) for sel in sels]
        cmask = (lax.broadcasted_iota(jnp.int32, (blk, blk), 1) <= lax.broadcasted_iota(jnp.int32, (blk, blk), 0))

        def step(kb, carry, masked, nk=1):
            r0 = pl.multiple_of(kb * blk, blk)
            ks = k_ref[pl.ds(r0, nk * blk), :]
            vs = v_ref[pl.ds(r0, nk * blk), :]
            out = []
            for j in range(2):
                m, acc = carry[2 * j], carry[2 * j + 1]
                ckr = jnp.concatenate([ck_ref[0, kb + t, j:j + 1, :] for t in range(nk)], axis=1)
                s = _dot(qms[j], ks, NT) - ckr
                if masked:
                    s = jnp.where(cmask, s, NEG)
                mn = jnp.maximum(m, jnp.max(s, axis=-1, keepdims=True))
                pr = jnp.exp(s - mn).astype(BF16)
                acc = jnp.exp(m - mn) * acc + _dot(pr, jnp.where(sels[j], vs, ones[j]))
                out += [mn, acc]
            return tuple(out)

        init = (jnp.full((blk, 1), NEG, F32), jnp.zeros((blk, LANES), F32)) * 2
        n4 = i // 4
        n2 = (i - 4 * n4) // 2
        carry = lax.fori_loop(0, n4, lambda t, c: step(4 * t, c, False, 4), init)
        carry = lax.fori_loop(0, n2, lambda t, c: step(4 * n4 + 2 * t, c, False, 2), carry)
        carry = lax.fori_loop(4 * n4 + 2 * n2, i, lambda kb, c: step(kb, c, False), carry)
        m0, a0, m1, a1 = step(i, carry, True)
        l0 = a0[:, HEAD_DIM:HEAD_DIM + 1]
        l1 = a1[:, 0:1]
        o_ref[...] = jnp.where(sels[0], a0 / l0, a1 / l1).astype(BF16)
        lse_ref[...] = jnp.where(sels[0], m0 + jnp.log(l0), m1 + jnp.log(l1))

    return pl.pallas_call(
        body, name="attn_fwd", grid=(npair, p // blk),
        in_specs=[pl.BlockSpec((blk, LANES), lambda h, i: (i, h)),
                  pl.BlockSpec((p, LANES), lambda h, i: (0, h)), pl.BlockSpec((p, LANES), lambda h, i: (0, h)),
                  pl.BlockSpec((1, nkb, 8, blk), lambda h, i: (h, 0, 0, 0))],
        out_specs=[pl.BlockSpec((blk, LANES), lambda h, i: (i, h)), pl.BlockSpec((blk, LANES), lambda h, i: (i, h))],
        out_shape=[jax.ShapeDtypeStruct((p, da), BF16), jax.ShapeDtypeStruct((p, da), F32)],
        compiler_params=_cparams(("parallel", "arbitrary")),
    )(q, k, v, ck)


def _attn_bwd(q, k, v, o, do, lse_rep, ck, blk):
    p, da = q.shape
    npair, nkb = ck.shape[0], ck.shape[1]
    nq = p // blk
    scale = 1.0 / math.sqrt(HEAD_DIM)

    def body(k_ref, v_ref, q_ref, do_ref, o_ref, lse_ref, ck_ref, dk_ref, dv_ref, dq_ref, dcs_ref, rsum_ref, dq_acc):
        jb = pl.program_id(1)

        @pl.when(jb == 0)
        def _():
            dq_acc[...] = jnp.zeros_like(dq_acc)

        ks = k_ref[...]
        vs = v_ref[...]
        lane = lax.broadcasted_iota(jnp.int32, (1, LANES), 1)
        sels = [lane < HEAD_DIM, lane >= HEAD_DIM]
        ones = [jnp.where(lane == HEAD_DIM, 1.0, 0.0).astype(BF16), jnp.where(lane == 0, 1.0, 0.0).astype(BF16)]
        kss = ks * scale
        kmo = [jnp.where(sels[j], kss, ones[j]) for j in range(2)]
        cmask = (lax.broadcasted_iota(jnp.int32, (blk, blk), 1) <= lax.broadcasted_iota(jnp.int32, (blk, blk), 0))

        def step(ib, carry, masked, nb=1):
            rows = nb * blk
            r0 = pl.multiple_of(ib * blk, blk)
            qb = q_ref[pl.ds(r0, rows), :] * scale
            dob = do_ref[pl.ds(r0, rows), :]
            prod = dob.astype(F32) * o_ref[pl.ds(r0, rows), :].astype(F32)
            out = []
            for j in range(2):
                dk, dv = carry[2 * j], carry[2 * j + 1]
                qm = jnp.where(sels[j], qb, jnp.zeros_like(qb))
                dom = jnp.where(sels[j], dob, jnp.zeros_like(dob))
                lse = lse_ref[pl.ds(r0, rows), HEAD_DIM * j:HEAD_DIM * j + 1]
                dlt = jnp.sum(jnp.where(sels[j], prod, 0.0), axis=-1, keepdims=True)
                s = _dot(qm, ks, NT) - ck_ref[0, 0, j:j + 1, :] - lse
                pm = jnp.exp(jnp.minimum(s, 0.0))
                if masked:
                    pm = jnp.where(cmask, pm, 0.0)
                ds_b = (pm * (_dot(dom, vs, NT) - dlt)).astype(BF16)
                dv = dv + _dot(pm.astype(BF16), dom, TN)
                dk = dk + _dot(ds_b, jnp.where(sels[j], qb, ones[j]), TN)
                dq_acc[pl.ds(r0, rows), LANES * j:LANES * (j + 1)] += _dot(ds_b, kmo[j])
                out += [dk, dv]
            return tuple(out)

        zero = jnp.zeros((blk, LANES), F32)
        carry = step(jb, (zero, zero, zero, zero), True)
        n4 = (nq - 1 - jb) // 4
        n2 = (nq - 1 - jb - 4 * n4) // 2
        carry = lax.fori_loop(0, n4, lambda t, c: step(jb + 1 + 4 * t, c, False, 4), carry)
        carry = lax.fori_loop(0, n2, lambda t, c: step(jb + 1 + 4 * n4 + 2 * t, c, False, 2), carry)
        dk0, dv0, dk1, dv1 = lax.fori_loop(jb + 1 + 4 * n4 + 2 * n2, nq, lambda ib, c: step(ib, c, False), carry)
        dk_ref[...] = jnp.where(sels[0], dk0, dk1).astype(BF16)
        dv_ref[...] = (dv0 + dv1).astype(BF16)
        lane8 = lax.broadcasted_iota(jnp.int32, (1, 8), 1)
        pair8 = lambda c0, c1: jnp.where(lane8 == 0, c0, jnp.where(lane8 == 1, c1, 0.0))
        dcs_ref[0] = pair8(dk0[:, HEAD_DIM:HEAD_DIM + 1], dk1[:, 0:1])

        @pl.when(jb == nkb - 1)
        def _():
            a0 = dq_acc[:, :LANES]
            a1 = dq_acc[:, LANES:]
            dq_ref[...] = jnp.where(sels[0], a0, a1).astype(BF16)
            rsum_ref[0] = pair8(a0[:, HEAD_DIM:HEAD_DIM + 1], a1[:, 0:1])

    colblk = pl.BlockSpec((blk, LANES), lambda h, j: (j, h))
    colfull = pl.BlockSpec((p, LANES), lambda h, j: (0, h))
    ckspec = pl.BlockSpec((1, 1, 8, blk), lambda h, j: (h, j, 0, 0))
    return pl.pallas_call(
        body, name="attn_bwd", grid=(npair, nkb),
        in_specs=[colblk, colblk, colfull, colfull, colfull, colfull, ckspec],
        out_specs=[colblk, colblk, colfull, pl.BlockSpec((1, blk, 8), lambda h, j: (h, j, 0)),
                   pl.BlockSpec((1, p, 8), lambda h, j: (h, 0, 0))],
        out_shape=[jax.ShapeDtypeStruct((p, da), BF16), jax.ShapeDtypeStruct((p, da), BF16),
                   jax.ShapeDtypeStruct((p, da), BF16), jax.ShapeDtypeStruct((npair, p, 8), F32),
                   jax.ShapeDtypeStruct((npair, p, 8), F32)],
        scratch_shapes=[pltpu.VMEM((p, 2 * LANES), F32)],
        compiler_params=_cparams(("parallel", "arbitrary")),
    )(k, v, q, do, o, lse_rep, ck)


def _rows3(i):
    return jnp.maximum(3 * i - 1, 0), 3 * i, 3 * i + 1


def _tail_fwd(yssd, o, zatt, graw, head, x2, tgt2, wps, wpa, wout, gate_bias, norm_post, tm):
    p, ds = yssd.shape
    da = o.shape[1]
    d = x2.shape[1]
    nsub = tm // CHUNK

    def body(yssd_ref, o_ref, zatt_ref, g_ref, head_ref, *rest):
        x_refs, t_refs = rest[:nsub], rest[nsub:2 * nsub]
        (wps_ref, wpa_ref, wout_ref, gb_ref, np_ref,
         yatt_ref, mrg_ref, a_ref, b_ref, dzo_ref, dout_ref, red_ref) = rest[2 * nsub:]
        i = pl.program_id(0)

        @pl.when(i == 0)
        def _():
            red_ref[...] = jnp.zeros_like(red_ref)

        first = jnp.where(i == 0, head_ref[...], x_refs[0][...])
        h = jnp.concatenate([first] + [r[...] for r in x_refs[1:]], axis=0)
        tgt = jnp.concatenate([r[...] for r in t_refs], axis=0)
        rows = lax.broadcasted_iota(jnp.int32, (tm, 1), 0)
        valid = jnp.where((i > 0) | (rows >= CHUNK), 1.0, 0.0)
        ob = o_ref[...].astype(F32)
        za = zatt_ref[...].astype(F32)
        yatt_b = (ob * za * _sigmoid(za)).astype(BF16)
        yatt_ref[...] = yatt_b
        a = _dot(yssd_ref[...], wps_ref[...])
        b = _dot(yatt_b, wpa_ref[...])
        a_ref[...] = a.astype(BF16)
        b_ref[...] = b.astype(BF16)
        gr = g_ref[...].astype(F32) + gb_ref[...]
        mrg_b = (_sigmoid(gr[:, :d]) * a + _sigmoid(gr[:, d:]) * b).astype(BF16)
        mrg_ref[...] = mrg_b
        zo = _dot(mrg_b, wout_ref[...])
        rstd = lax.rsqrt(jnp.mean(zo * zo, axis=-1, keepdims=True) + EPS)
        zh = zo * rstd
        npw = np_ref[...]
        err = (h + zh * npw - tgt) * valid
        dout = err * (1.0 / d)
        dout_ref[...] = dout
        dzh = dout * npw
        dzo_ref[...] = (rstd * (dzh - zh * jnp.mean(dzh * zh, axis=-1, keepdims=True))).astype(BF16)
        red_ref[0:1, :] += jnp.sum(dout * zh, axis=0, keepdims=True)
        red_ref[1:2, 0:1] += jnp.sum(jnp.sum(err * err, axis=1, keepdims=True), axis=0, keepdims=True) * (0.5 / d)

    row = lambda w: pl.BlockSpec((tm, w), lambda i: (i, 0))
    once = lambda shape: pl.BlockSpec(shape, lambda i: (0,) * len(shape), pipeline_mode=pl.Buffered(1))
    if nsub == 1:
        subs = [pl.BlockSpec((CHUNK, d), lambda i: (jnp.maximum(i - 1, 0), 0))]
    else:
        subs = [pl.BlockSpec((CHUNK, d), functools.partial(lambda i, k: (_rows3(i)[k], 0), k=k)) for k in range(3)]
    sd = jax.ShapeDtypeStruct
    return pl.pallas_call(
        body, name="tail_fwd", grid=(p // tm,),
        in_specs=[row(ds), row(da), row(da), row(2 * d), _full((CHUNK, d))] + subs + subs
                 + [once((ds, d)), once((da, d)), once((d, d)), _full((1, 2 * d)), _full((1, d))],
        out_specs=[row(da), row(d), row(d), row(d), row(d), row(d), _full((8, d))],
        out_shape=[sd((p, da), BF16), sd((p, d), BF16), sd((p, d), BF16), sd((p, d), BF16), sd((p, d), BF16),
                   sd((p, d), F32), sd((8, d), F32)],
        compiler_params=_cparams(("arbitrary",)),
    )(yssd, o, zatt, graw, head, *([x2] * nsub), *([tgt2] * nsub), wps, wpa, wout, gate_bias, norm_post)


def _tail_bwd(dzo, a_b, b_b, graw, o, zatt, wps, wpa, wout, gate_bias, tm):
    p, d = dzo.shape
    ds, da = wps.shape[0], wpa.shape[0]

    def body(dzo_ref, a_ref, b_ref, g_ref, o_ref, zatt_ref, wps_ref, wpa_ref, wout_ref, gb_ref,
             da_ref, db_ref, dg_ref, dyssd_ref, do_ref, dzatt_ref, red_ref):
        i = pl.program_id(0)

        @pl.when(i == 0)
        def _():
            red_ref[...] = jnp.zeros_like(red_ref)

        gr = g_ref[...].astype(F32) + gb_ref[...]
        gs = _sigmoid(gr[:, :d])
        ga = _sigmoid(gr[:, d:])
        dm = _dot(dzo_ref[...], wout_ref[...], NT)
        da_b = (gs * dm).astype(BF16)
        db_b = (ga * dm).astype(BF16)
        da_ref[...] = da_b
        db_ref[...] = db_b
        dgs = dm * a_ref[...].astype(F32) * gs * (1.0 - gs)
        dga = dm * b_ref[...].astype(F32) * ga * (1.0 - ga)
        dg_ref[:, :d] = dgs.astype(BF16)
        dg_ref[:, d:] = dga.astype(BF16)
        red_ref[0:1, :d] += jnp.sum(dgs, axis=0, keepdims=True)
        red_ref[0:1, d:] += jnp.sum(dga, axis=0, keepdims=True)
        dyssd_ref[...] = _dot(da_b, wps_ref[...], NT).astype(BF16)
        dya = _dot(db_b, wpa_ref[...], NT)
        ob = o_ref[...].astype(F32)
        za = zatt_ref[...].astype(F32)
        sza = _sigmoid(za)
        do_ref[...] = (dya * za * sza).astype(BF16)
        dzatt_ref[...] = (dya * ob * sza * (1.0 + za * (1.0 - sza))).astype(BF16)

    row = lambda w: pl.BlockSpec((tm, w), lambda i: (i, 0))
    once = lambda shape: pl.BlockSpec(shape, lambda i: (0,) * len(shape), pipeline_mode=pl.Buffered(1))
    sd = jax.ShapeDtypeStruct
    return pl.pallas_call(
        body, name="tail_bwd", grid=(p // tm,),
        in_specs=[row(d), row(d), row(d), row(2 * d), row(da), row(da),
                  once((ds, d)), once((da, d)), once((d, d)), _full((1, 2 * d))],
        out_specs=[row(d), row(d), row(2 * d), row(ds), row(da), row(da), _full((8, 2 * d))],
        out_shape=[sd((p, d), BF16), sd((p, d), BF16), sd((p, 2 * d), BF16), sd((p, ds), BF16), sd((p, da), BF16),
                   sd((p, da), BF16), sd((8, 2 * d), F32)],
        compiler_params=_cparams(("arbitrary",)),
    )(dzo, a_b, b_b, graw, o, zatt, wps, wpa, wout, gate_bias)


def _adamw_math(w, g, m, v):
    m2 = ADAM_B1 * m + (1.0 - ADAM_B1) * g
    v2 = ADAM_B2 * v + (1.0 - ADAM_B2) * (g * g)
    m_hat = m2 / (1.0 - ADAM_B1 ** ADAM_STEP)
    v_hat = v2 / (1.0 - ADAM_B2 ** ADAM_STEP)
    delta = -ADAM_LR * (m_hat / (jnp.sqrt(v_hat) + ADAM_EPS) + ADAM_WD * w)
    return delta, m2, v2


def _adamw_small(params, red, name):
    names = list(params)
    n = len(names)
    extra = [params[k][3] for k in names if not isinstance(params[k][3], tuple)]

    def body(*refs):
        w_refs, m_refs, v_refs = refs[:n], refs[n:2 * n], refs[2 * n:3 * n]
        red_ref = refs[3 * n]
        g_refs = iter(refs[3 * n + 1:3 * n + 1 + len(extra)])
        outs = refs[3 * n + 1 + len(extra):]
        for i, k in enumerate(names):
            where = params[k][3]
            rows, cols = w_refs[i].shape
            if isinstance(where, tuple):
                g = red_ref[where[0]:where[0] + rows, where[1]:where[1] + cols]
            else:
                g = next(g_refs)[...]
            delta, m2, v2 = _adamw_math(w_refs[i][...], g, m_refs[i][...], v_refs[i][...])
            for o, val in zip(outs[4 * i:4 * i + 4], (g, delta, m2, v2)):
                o[...] = val

    vm = pl.BlockSpec(memory_space=pltpu.VMEM)
    ws, ms, vs = ([params[k][j] for k in names] for j in range(3))
    out = pl.pallas_call(
        body, name=name,
        out_shape=[jax.ShapeDtypeStruct(w.shape, F32) for w in ws for _ in range(4)],
        in_specs=[vm] * (3 * n + 1 + len(extra)), out_specs=[vm] * (4 * n),
    )(*ws, *ms, *vs, red, *extra)
    return {k: tuple(out[4 * i:4 * i + 4]) for i, k in enumerate(names)}


def _adamw(w, g, m, v, name, parts=False, part_row0=0):
    r, cdim = w.shape
    tr, tc, by_rows = _tiles_2d(r, cdim)
    pick = (lambda i: (i, 0)) if by_rows else (lambda i: (0, i))
    assert part_row0 % tr == 0
    gpick = (lambda i: (i + part_row0 // tr, 0)) if by_rows else (lambda i: (part_row0 // tr, i))

    def body(w_ref, g_ref, m_ref, v_ref, go_ref, d_ref, mo_ref, vo_ref):
        if parts:
            g = g_ref[0].astype(F32)
            for s in range(1, g_ref.shape[0]):
                g = g + g_ref[s].astype(F32)
        else:
            g = g_ref[...]
        delta, m2, v2 = _adamw_math(w_ref[...], g, m_ref[...], v_ref[...])
        go_ref[...] = g
        d_ref[...] = delta
        mo_ref[...] = m2
        vo_ref[...] = v2

    blk = pl.BlockSpec((tr, tc), pick)
    gspec = pl.BlockSpec((g.shape[0], tr, tc), lambda i: (0,) + gpick(i)) if parts else blk
    return pl.pallas_call(
        body, name=name, grid=((r // tr) * (cdim // tc),),
        in_specs=[blk, gspec, blk, blk], out_specs=[blk] * 4,
        out_shape=[jax.ShapeDtypeStruct((r, cdim), F32)] * 4,
        compiler_params=_cparams(("parallel",)),
    )(w, g, m, v)


def _pad_cols(a, width):
    return jnp.pad(a, ((0, 0), (0, width - a.shape[1])))


def _pack_small_shard(conv_w_sh, meta_sh, width):
    return jnp.concatenate([_pad_cols(conv_w_sh, width), jnp.zeros((4, width), F32), _pad_cols(meta_sh, width)], axis=0)


def _pack_small_rep(norm_pre, norm_post, gate_bias, ssd_norm, conv_b, misc, width):
    rows = [norm_pre, norm_post, gate_bias, ssd_norm, conv_b, misc]
    return jnp.concatenate([_pad_cols(r, width) for r in rows] + [jnp.zeros((2, width), F32)], axis=0)


def kernel(x, meta_tokens, norm_pre, w_in, conv_w, conv_b, dt_bias, a_log, d_skip, ssd_norm, fgate_bias, gate_bias, w_proj_ssd, w_proj_att, w_out, norm_post, loss_target, m_meta_tokens, m_norm_pre, m_w_in, m_conv_w, m_conv_b, m_dt_bias, m_a_log, m_d_skip, m_ssd_norm, m_fgate_bias, m_gate_bias, m_w_proj_ssd, m_w_proj_att, m_w_out, m_norm_post, v_meta_tokens, v_norm_pre, v_w_in, v_conv_w, v_conv_b, v_dt_bias, v_a_log, v_d_skip, v_ssd_norm, v_fgate_bias, v_gate_bias, v_w_proj_ssd, v_w_proj_att, v_w_out, v_norm_post):
    seq, d = x.shape[1], x.shape[2]
    p = seq + CHUNK
    hs, ha = dt_bias.shape[1], fgate_bias.shape[1]
    ds, cd = ssd_norm.shape[1], conv_b.shape[1]
    da = ha * HEAD_DIM
    nc8 = w_in.shape[2]
    cws = cd // N_DEV
    msh = d // N_DEV
    r1, r2, r3 = ds // N_DEV, da // N_DEV, d // N_DEV
    me = _dev_index(*_my_pos())
    x2, tgt2 = x[0], loss_target[0]

    win_sh = jnp.transpose(w_in[0]).astype(BF16)
    rows_sh = jnp.concatenate([w_proj_ssd[0], w_proj_att[0], w_out[0]], axis=0).astype(BF16)
    small_sh = _pack_small_shard(conv_w[0], meta_tokens, cws)
    win_all, small_all = _all_gather([win_sh, small_sh], "gather_weights")
    rows_sh, win_all = lax.optimization_barrier((rows_sh, win_all))
    rows_sems, rows_thru, rows_land, rows_token = _bcast_start(rows_sh, "gather_rows_start")
    w_full = win_all.reshape(N_DEV * nc8, d)
    cuts = [0, ds, ds + cd, ds + cd + hs, ds + cd + hs + da, ds + cd + hs + 2 * da, ds + cd + hs + 3 * da,
            ds + cd + hs + 4 * da, ds + cd + hs + 4 * da + ha, ds + cd + hs + 4 * da + ha + 2 * d]
    w_z, w_xbc, w_dt, w_zatt, w_q, w_k, w_v, w_f, w_g = [w_full[cuts[i]:cuts[i + 1]] for i in range(9)]
    w_dtf = jnp.concatenate([w_dt, w_f, jnp.zeros((LANES - hs - ha, d), BF16)], axis=0)
    conv_w_full = jnp.transpose(small_all[:, 0:CONV_K, :], (1, 0, 2)).reshape(CONV_K, cd)
    meta_full = jnp.transpose(small_all[:, 8:8 + N_META, :msh], (1, 0, 2)).reshape(N_META, d)
    head = jnp.concatenate([jnp.zeros((PADN, d), F32), meta_full + rows_token[0:1, 0:1]], axis=0)

    u = _prenorm_fwd(head, x2, norm_pre)
    tm = _att_block(p)
    seg_w = [w_z, w_xbc, w_zatt, w_q, w_k, w_v, w_g]
    zs, xbc, zatt, q, k, v, graw = [
        _mm(u, w, "nt", BF16, _tile(p, (1408, tm)), _tile(w.shape[0], (1024, 512, 256, 128)), "inproj_%d" % i)
        for i, w in enumerate(seg_w)]
    dtf = _mm(u, w_dtf, "nt", F32, _tile(p, (1408, tm)), LANES, "inproj_dtf")

    brow = jnp.concatenate([dt_bias, fgate_bias, jnp.zeros((1, LANES - hs - ha), F32)], axis=1)
    alog_row = _pad_cols(a_log, LANES)
    dskip_l = jnp.repeat(d_skip, HEAD_DIM, axis=1)
    sel_t = (lax.broadcasted_iota(jnp.int32, (LANES, ds), 1) // HEAD_DIM
             == lax.broadcasted_iota(jnp.int32, (LANES, ds), 0)).astype(BF16)
    sel = sel_t.T
    y, yssd, hin, cf, pre = _ssd_fwd(xbc, zs, dtf, conv_w_full, conv_b, brow, alog_row, dskip_l, ssd_norm, sel_t, hs, ha)

    blk = _att_block(p)
    nkb, npair = p // blk, ha // 2
    cum = jnp.where(lax.broadcasted_iota(jnp.int32, (p, 1), 0) < PADN, -NEG, cf[:, hs:hs + ha])
    ck = jnp.transpose(cum.T.reshape(npair, 2, nkb, blk), (0, 2, 1, 3))
    ck = jnp.pad(ck, ((0, 0), (0, 0), (0, 6), (0, 0)))
    o, lse_rep = _attn_fwd(q, k, v, ck, blk)

    rows_all = _bcast_wait(rows_sems, rows_thru, rows_land, lse_rep, "gather_rows_wait")
    wps = rows_all[:, :r1].reshape(ds, d)
    wpa = rows_all[:, r1:r1 + r2].reshape(da, d)
    wout = rows_all[:, r1 + r2:].reshape(d, d)

    yatt, mrg, a_b, b_b, dzo, dout, red_fwd = _tail_fwd(
        yssd, o, zatt, graw, head, x2, tgt2, wps, wpa, wout, gate_bias, norm_post, tm)
    da_, db_, dgraw, dyssd, d_o, dzatt, red_bwd = _tail_bwd(dzo, a_b, b_b, graw, o, zatt, wps, wpa, wout, gate_bias, tm)

    tw = _tile(d, (512, 256, 128))
    g_wout = _mm(mrg, dzo, "tn", BF16, tw, tw, "wgrad_out")
    g_wps = _mm(yssd, da_, "tn", BF16, _tile(ds, (512, 256, 128)), tw, "wgrad_ps")
    g_wpa = _mm(yatt, db_, "tn", BF16, _tile(da, (512, 256, 128)), tw, "wgrad_pa")

    dk, dv, dq, dcs, rsum = _attn_bwd(q, k, v, o, d_o, lse_rep, ck, blk)
    dcum = jnp.transpose((rsum - dcs)[:, :, 0:2], (1, 0, 2)).reshape(p, ha)
    dcf = jnp.pad(dcum, ((0, 0), (hs, LANES - hs - ha)))
    dxbc, dzs, ddtf, gcw, gcb, gnrm, gsm = _ssd_bwd(
        dyssd, y, zs, xbc, pre, dtf, hin, dcf, conv_w_full, brow, alog_row, dskip_l, ssd_norm, sel_t, sel, hs, ha)
    ddtf_b = ddtf.astype(BF16)

    dsegs = [dzs, dxbc, dzatt, dq, dk, dv, dgraw, ddtf_b]
    gsegs = [_mm(dsg, u, "tn", BF16, _tile(dsg.shape[1], (512, 256, 128)), tw, "wgrad_in_%d" % i)
             for i, dsg in enumerate(dsegs)]
    g_z, g_xbc, g_zatt, g_q, g_k, g_v, g_g, g_dtf = gsegs
    gw_full = jnp.concatenate([g_z, g_xbc, g_dtf[:hs], g_zatt, g_q, g_k, g_v, g_dtf[hs:hs + ha], g_g], axis=0)
    gwin_parts = gw_full.reshape(N_DEV, nc8, d)
    grows_parts = jnp.concatenate([g_wps.reshape(N_DEV, r1, d), g_wpa.reshape(N_DEV, r2, d),
                                   g_wout.reshape(N_DEV, r3, d)], axis=1)

    core = lax.axis_index("c").astype(jnp.int32).reshape(1)
    sib_win, sib_rows = _exchange_sibling([gwin_parts, grows_parts], "scatter_grads_sibling")
    chip_win = _pair_add(gwin_parts, sib_win, core, "pair_add_w_in")
    chip_rows = _pair_add(grows_parts, sib_rows, core, "pair_add_rows")
    sems, thru, lands, token = _exchange_chips_start([chip_win, chip_rows], "scatter_grads_start")
    dsegs_after = dsegs[:-1] + [ddtf_b + token[0:1, 0:1].astype(BF16)]
    du = _mm_sum_nn(dsegs_after, seg_w + [w_dtf], tm, _tile(d, (256, 128)), "dgrad_in")
    gx, ghead, gnp = _prenorm_bwd(head, x2, norm_pre, du, dout)
    sent, got = _exchange_chips_wait(sems, thru, lands, gnp, "scatter_grads_wait")
    chip = me // 2
    recv_win, recv_rows = [lax.dynamic_update_slice_in_dim(g, lax.dynamic_slice_in_dim(s, chip, 1, axis=0), chip, axis=0)
                           for g, s in zip(got, sent)]
    gmisc = jnp.concatenate([gsm[0:1], gsm[1:2], gsm[2:3], _pad_cols(red_fwd[1:2, 0:1], LANES)], axis=1)
    small_g = jnp.concatenate([
        _pack_small_rep(gnp[0:1], red_fwd[0:1], red_bwd[0:1], gnrm[0:1], gcb[0:1], gmisc, cd),
        _pad_cols(gcw[0:CONV_K], cd), jnp.zeros((4, cd), F32), _pad_cols(ghead[PADN:], cd)], axis=0)
    sg_sems, sg_thru, sg_land, sg_token = _bcast_start(small_g, "reduce_small_start")

    upd_in = _adamw(jnp.transpose(w_in[0]) + sg_token[0:1, 0:1], recv_win, jnp.transpose(m_w_in[0]),
                    jnp.transpose(v_w_in[0]), "adamw_w_in", parts=True)
    upd_ps = _adamw(w_proj_ssd[0] + sg_token[0:1, 0:1], recv_rows, m_w_proj_ssd[0], v_w_proj_ssd[0],
                    "adamw_w_proj_ssd", parts=True, part_row0=0)
    upd_pa = _adamw(w_proj_att[0], recv_rows, m_w_proj_att[0], v_w_proj_att[0], "adamw_w_proj_att", parts=True,
                    part_row0=r1)
    upd_out = _adamw(w_out[0], recv_rows, m_w_out[0], v_w_out[0], "adamw_w_out", parts=True, part_row0=r1 + r2)
    all_done = upd_in[1][0:8, 0:LANES] + upd_ps[1][0:8, 0:LANES] + upd_pa[1][0:8, 0:LANES] + upd_out[1][0:8, 0:LANES]
    red = _sum_slots(_bcast_wait(sg_sems, sg_thru, sg_land, all_done, "reduce_small_wait"), "reduce_small_sum")
    loss = red[5, 3 * LANES]
    g_conv_w = lax.dynamic_slice_in_dim(red[8:8 + CONV_K], me * cws, cws, axis=1)
    g_meta = lax.dynamic_slice_in_dim(red[16:16 + N_META, :d], me * msh, msh, axis=1)
    small = {
        "meta_tokens": (meta_tokens, m_meta_tokens, v_meta_tokens, g_meta),
        "norm_pre": (norm_pre, m_norm_pre, v_norm_pre, (0, 0)),
        "conv_w": (conv_w[0], m_conv_w[0], v_conv_w[0], g_conv_w),
        "conv_b": (conv_b, m_conv_b, v_conv_b, (4, 0)),
        "dt_bias": (dt_bias, m_dt_bias, v_dt_bias, (5, 0)),
        "a_log": (a_log, m_a_log, v_a_log, (5, LANES)),
        "d_skip": (d_skip, m_d_skip, v_d_skip, (5, 2 * LANES)),
        "ssd_norm": (ssd_norm, m_ssd_norm, v_ssd_norm, (3, 0)),
        "fgate_bias": (fgate_bias, m_fgate_bias, v_fgate_bias, (5, hs)),
        "gate_bias": (gate_bias, m_gate_bias, v_gate_bias, (2, 0)),
        "norm_post": (norm_post, m_norm_post, v_norm_post, (1, 0)),
    }
    upd_small = _adamw_small(small, red, "adamw_small")

    def leaves(i):
        sm = {k: v[i] for k, v in upd_small.items()}
        return [sm["meta_tokens"], sm["norm_pre"], jnp.transpose(upd_in[i])[None], sm["conv_w"][None], sm["conv_b"],
                sm["dt_bias"], sm["a_log"], sm["d_skip"], sm["ssd_norm"], sm["fgate_bias"], sm["gate_bias"],
                upd_ps[i][None], upd_pa[i][None], upd_out[i][None], sm["norm_post"]]

    return tuple([loss, gx[None]] + leaves(0) + leaves(1) + leaves(2) + leaves(3))
```

```python
import functools
import math

import jax
import jax.numpy as jnp
from jax import lax
from jax.experimental import pallas as pl
from jax.experimental.pallas import tpu as pltpu

F32 = jnp.float32
BF16 = jnp.bfloat16

N_DEV = 8
N_META = 16
CHUNK = 128
PADN = CHUNK - N_META
HEAD_DIM = 64
SSD_GROUPS = 4
CONV_K = 4
EPS = 1e-6
NEG = -1e30
LANES = 128
HALO = 16

ADAM_LR = 0.001
ADAM_B1 = 0.9
ADAM_B2 = 0.999
ADAM_EPS = 1e-08
ADAM_WD = 0.01
ADAM_STEP = 10

VMEM_LIMIT = 56 * 1024 * 1024

NN = (((1,), (0,)), ((), ()))
NT = (((1,), (1,)), ((), ()))
TN = (((0,), (0,)), ((), ()))
MESH = pl.DeviceIdType.MESH


def _dot(a, b, dims=NN):
    return lax.dot_general(a, b, dims, preferred_element_type=F32)


def _split2(x):
    hi = x.astype(BF16)
    lo = (x - hi.astype(F32)).astype(BF16)
    return hi, lo


def _dot_sel(x, sel):
    hi, lo = _split2(x)
    return _dot(hi, sel) + _dot(lo, sel)


def _dot_tri(tri, x):
    h1 = x.astype(BF16)
    r1 = x - h1.astype(F32)
    h2 = r1.astype(BF16)
    h3 = (r1 - h2.astype(F32)).astype(BF16)
    return _dot(tri, h1) + _dot(tri, h2) + _dot(tri, h3)


def _sigmoid(x):
    return 1.0 / (1.0 + jnp.exp(-x))


def _softplus(x):
    return jnp.maximum(x, 0.0) + jnp.log(1.0 + jnp.exp(-jnp.abs(x)))


def _cparams(sem=None, vmem=VMEM_LIMIT):
    kw = {"vmem_limit_bytes": vmem}
    if sem is not None:
        kw["dimension_semantics"] = sem
    return pltpu.CompilerParams(**kw)


def _full(shape):
    nd = len(shape)
    return pl.BlockSpec(shape, lambda *_: (0,) * nd)


def _att_block(p):
    return 384 if p % 384 == 0 else CHUNK


def _my_pos():
    return lax.axis_index("x"), lax.axis_index("y"), lax.axis_index("c")


def _dev_index(x, y, c):
    return 4 * x + 2 * y + c


FLIPS = [(fx, fy, fc) for fx in (0, 1) for fy in (0, 1) for fc in (0, 1)][1:]


def _flip(pos, f):
    return tuple((1 - p) if fi else p for p, fi in zip(pos, f))


def _all_gather(bufs, name):
    nb = len(bufs)

    def body(*refs):
        ins, outs = refs[:nb], refs[nb:2 * nb]
        send_sems, recv_sems, local_sems = refs[2 * nb:]
        x, y, c = _my_pos()
        me = _dev_index(x, y, c)
        sibling = (x, y, 1 - c)
        chips = [(1 - x, y), (x, 1 - y), (1 - x, 1 - y)]

        def copy(b, k, block_idx, to, src=None):
            dst = outs[b].at[block_idx]
            return pltpu.make_async_remote_copy(
                src_ref=dst if src is None else src, dst_ref=dst,
                send_sem=send_sems.at[b, k], recv_sem=recv_sems.at[b, k],
                device_id=to, device_id_type=MESH)

        started = []
        for b in range(nb):
            mine = pltpu.make_async_copy(ins[b], outs[b].at[me], local_sems.at[b])
            mine.start()
            started.append(mine)
        first = []
        for b in range(nb):
            first.append(copy(b, 0, me, sibling, src=ins[b]))
            for j, chip in enumerate(chips):
                first.append(copy(b, 1 + j, me, (chip[0], chip[1], c), src=ins[b]))
        for cp in first:
            cp.start()
        passed = []
        for j, chip in enumerate(chips):
            blk = _dev_index(chip[0], chip[1], c)
            for b in range(nb):
                copy(b, 1 + j, blk, (x, y, c)).wait_recv()
                fwd = copy(b, 4 + j, blk, sibling)
                fwd.start()
                passed.append(fwd)
        for b in range(nb):
            copy(b, 0, _dev_index(x, y, 1 - c), (x, y, c)).wait_recv()
        for j, chip in enumerate(chips):
            blk = _dev_index(chip[0], chip[1], 1 - c)
            for b in range(nb):
                copy(b, 4 + j, blk, (x, y, c)).wait_recv()
        for cp in first + passed:
            cp.wait_send()
        for mine in started:
            mine.wait()

    any_spec = pl.BlockSpec(memory_space=pl.ANY)
    return pl.pallas_call(
        body, name=name,
        out_shape=[jax.ShapeDtypeStruct((N_DEV,) + b.shape, b.dtype) for b in bufs],
        in_specs=[any_spec] * nb, out_specs=[any_spec] * nb,
        scratch_shapes=[pltpu.SemaphoreType.DMA((nb, 7)), pltpu.SemaphoreType.DMA((nb, 7)),
                        pltpu.SemaphoreType.DMA((nb,))],
    )(*bufs)


N_CHIP = 4
CHIP_FLIPS = [(1, 0), (0, 1), (1, 1)]


def _exchange_sibling(bufs, name):
    nb = len(bufs)

    def body(*refs):
        ins, outs = refs[:nb], refs[nb:2 * nb]
        send_sems, recv_sems = refs[2 * nb:]
        x, y, c = _my_pos()

        def copy(b, k):
            return pltpu.make_async_remote_copy(
                src_ref=ins[b].at[2 * k + (1 - c)], dst_ref=outs[b].at[k],
                send_sem=send_sems.at[b, k], recv_sem=recv_sems.at[b, k],
                device_id=(x, y, 1 - c), device_id_type=MESH)

        cps = [copy(b, k) for b in range(nb) for k in range(N_CHIP)]
        for cp in cps:
            cp.start()
        for cp in cps:
            cp.wait()

    any_spec = pl.BlockSpec(memory_space=pl.ANY)
    return pl.pallas_call(
        body, name=name,
        out_shape=[jax.ShapeDtypeStruct((N_CHIP,) + b.shape[1:], b.dtype) for b in bufs],
        in_specs=[any_spec] * nb, out_specs=[any_spec] * nb,
        scratch_shapes=[pltpu.SemaphoreType.DMA((nb, N_CHIP)), pltpu.SemaphoreType.DMA((nb, N_CHIP))],
    )(*bufs)


def _pair_add(mine, recv, core, name):
    _, r, cdim = mine.shape
    tr, tc, by_rows = _tiles_2d(r, cdim)
    pick = (lambda i: (i, 0)) if by_rows else (lambda i: (0, i))

    def body(core_ref, a_ref, b_ref, o_ref):
        o_ref[0] = (a_ref[0].astype(F32) + b_ref[0].astype(F32)).astype(o_ref.dtype)

    return pl.pallas_call(
        body, name=name,
        grid_spec=pltpu.PrefetchScalarGridSpec(
            num_scalar_prefetch=1, grid=(N_CHIP, (r // tr) * (cdim // tc)),
            in_specs=[pl.BlockSpec((1, tr, tc), lambda k, i, core_ref: (2 * k + core_ref[0],) + pick(i)),
                      pl.BlockSpec((1, tr, tc), lambda k, i, core_ref: (k,) + pick(i))],
            out_specs=pl.BlockSpec((1, tr, tc), lambda k, i, core_ref: (k,) + pick(i))),
        out_shape=jax.ShapeDtypeStruct((N_CHIP, r, cdim), mine.dtype),
        compiler_params=_cparams(("parallel", "parallel")),
    )(core, mine, recv)


def _chip_peer(x, y, f):
    return ((1 - x) if f[0] else x), ((1 - y) if f[1] else y)


def _exchange_chips_start(bufs, name):
    nb = len(bufs)
    nsem = 2 * 3 * nb

    def body(*refs):
        ins, lands = refs[:nb], refs[nb:2 * nb]
        sems = refs[2 * nb:2 * nb + nsem]
        token = refs[-1]
        x, y, c = _my_pos()
        for b in range(nb):
            for j, f in enumerate(CHIP_FLIPS):
                px, py = _chip_peer(x, y, f)
                pltpu.make_async_remote_copy(
                    src_ref=ins[b].at[2 * px + py], dst_ref=lands[b].at[2 * x + y],
                    send_sem=sems[2 * (3 * b + j)], recv_sem=sems[2 * (3 * b + j) + 1],
                    device_id=(px, py, c), device_id_type=MESH).start()
        token[...] = jnp.zeros_like(token)

    hbm = pl.BlockSpec(memory_space=pltpu.HBM)
    sem = pl.BlockSpec(memory_space=pltpu.SEMAPHORE)
    out = pl.pallas_call(
        body, name=name,
        out_shape=(*([pltpu.SemaphoreType.DMA(())] * nsem),
                   *[pltpu.HBM(b.shape, b.dtype) for b in bufs], *[pltpu.HBM(b.shape, b.dtype) for b in bufs],
                   jax.ShapeDtypeStruct((8, LANES), F32)),
        in_specs=[hbm] * (2 * nb),
        out_specs=(*([sem] * nsem), *([hbm] * (2 * nb)), pl.BlockSpec(memory_space=pltpu.VMEM)),
        input_output_aliases={i: nsem + i for i in range(2 * nb)},
        compiler_params=pltpu.CompilerParams(has_side_effects=pltpu.SideEffectType.DATAFLOW_SIDE_EFFECTING),
    )(*[pltpu.with_memory_space_constraint(b, pltpu.HBM) for b in bufs],
      *[pltpu.with_memory_space_constraint(lax.empty(b.shape, b.dtype), pltpu.HBM) for b in bufs])
    return out[:nsem], out[nsem:nsem + nb], out[nsem + nb:nsem + 2 * nb], out[-1]


def _exchange_chips_wait(sems, thru, lands, after, name):
    nb = len(thru)
    nsem = len(sems)

    def body(*refs):
        ins, lnd = refs[:nb], refs[nb:2 * nb]
        sem_refs = refs[2 * nb:2 * nb + nsem]
        x, y, c = _my_pos()
        for b in range(nb):
            for j, f in enumerate(CHIP_FLIPS):
                px, py = _chip_peer(x, y, f)
                cp = pltpu.make_async_remote_copy(
                    src_ref=ins[b].at[2 * px + py], dst_ref=lnd[b].at[2 * px + py],
                    send_sem=sem_refs[2 * (3 * b + j)], recv_sem=sem_refs[2 * (3 * b + j) + 1],
                    device_id=(px, py, c), device_id_type=MESH)
                cp.wait_send()
                cp.wait_recv()

    hbm = pl.BlockSpec(memory_space=pltpu.HBM)
    sem = pl.BlockSpec(memory_space=pltpu.SEMAPHORE)
    out = pl.pallas_call(
        body, name=name,
        out_shape=tuple([pltpu.HBM(b.shape, b.dtype) for b in thru] + [pltpu.HBM(b.shape, b.dtype) for b in lands]),
        in_specs=[hbm] * (2 * nb) + [sem] * nsem + [pl.BlockSpec(memory_space=pl.ANY)],
        out_specs=tuple([hbm] * (2 * nb)),
        input_output_aliases={i: i for i in range(2 * nb)},
        compiler_params=pltpu.CompilerParams(has_side_effects=pltpu.SideEffectType.DATAFLOW_SIDE_EFFECTING),
    )(*thru, *lands, *sems, after)
    return out[:nb], out[nb:]


def _bcast_start(buf, name):
    nsem = 2 * len(FLIPS)

    def body(src, land, *rest):
        sems, token = rest[:nsem], rest[-1]
        pos = _my_pos()
        for k, f in enumerate(FLIPS):
            pltpu.make_async_remote_copy(
                src_ref=src, dst_ref=land.at[_dev_index(*pos)], send_sem=sems[2 * k], recv_sem=sems[2 * k + 1],
                device_id=_flip(pos, f), device_id_type=MESH).start()
        token[...] = jnp.zeros_like(token)

    hbm = pl.BlockSpec(memory_space=pltpu.HBM)
    sem = pl.BlockSpec(memory_space=pltpu.SEMAPHORE)
    land_shape = (N_DEV,) + buf.shape
    out = pl.pallas_call(
        body, name=name,
        out_shape=(*([pltpu.SemaphoreType.DMA(())] * nsem), pltpu.HBM(buf.shape, buf.dtype),
                   pltpu.HBM(land_shape, buf.dtype), jax.ShapeDtypeStruct((8, LANES), F32)),
        in_specs=[hbm, hbm],
        out_specs=(*([sem] * nsem), hbm, hbm, pl.BlockSpec(memory_space=pltpu.VMEM)),
        input_output_aliases={0: nsem, 1: nsem + 1},
        compiler_params=pltpu.CompilerParams(has_side_effects=pltpu.SideEffectType.DATAFLOW_SIDE_EFFECTING),
    )(pltpu.with_memory_space_constraint(buf, pltpu.HBM),
      pltpu.with_memory_space_constraint(lax.empty(land_shape, buf.dtype), pltpu.HBM))
    return out[:nsem], out[nsem], out[nsem + 1], out[-1]


def _bcast_wait(sems, thru, land, after, name):
    nsem = len(sems)

    def body(src, lnd, *rest):
        sem_refs = rest[:nsem]
        pos = _my_pos()
        for k, f in enumerate(FLIPS):
            peer = _flip(pos, f)
            cp = pltpu.make_async_remote_copy(
                src_ref=src, dst_ref=lnd.at[_dev_index(*peer)], send_sem=sem_refs[2 * k],
                recv_sem=sem_refs[2 * k + 1], device_id=peer, device_id_type=MESH)
            cp.wait_send()
            cp.wait_recv()

    hbm = pl.BlockSpec(memory_space=pltpu.HBM)
    sem = pl.BlockSpec(memory_space=pltpu.SEMAPHORE)
    sent, got = pl.pallas_call(
        body, name=name,
        out_shape=(pltpu.HBM(thru.shape, thru.dtype), pltpu.HBM(land.shape, land.dtype)),
        in_specs=[hbm, hbm] + [sem] * nsem + [pl.BlockSpec(memory_space=pl.ANY)],
        out_specs=(hbm, hbm), input_output_aliases={0: 0, 1: 1},
        compiler_params=pltpu.CompilerParams(has_side_effects=pltpu.SideEffectType.DATAFLOW_SIDE_EFFECTING),
    )(thru, land, *sems, after)
    return lax.dynamic_update_slice_in_dim(got, sent[None], _dev_index(*_my_pos()), axis=0)


def _sum_slots(v, name):
    _, r, cdim = v.shape

    def body(v_ref, o_ref):
        acc = v_ref[0]
        for s in range(1, N_DEV):
            acc = acc + v_ref[s]
        o_ref[...] = acc

    return pl.pallas_call(
        body, name=name, out_shape=jax.ShapeDtypeStruct((r, cdim), F32),
        in_specs=[_full((N_DEV, r, cdim))], out_specs=_full((r, cdim)), grid=(1,),
        compiler_params=_cparams(("arbitrary",)),
    )(v)


def _mm(a, b, dims, out_dtype, tm, tn, name):
    if dims == "nn":
        (m, k), (_, n) = a.shape, b.shape
        a_spec = pl.BlockSpec((tm, k), lambda j, i: (i, 0))
        b_spec = pl.BlockSpec((k, tn), lambda j, i: (0, j))
        dn = NN
    elif dims == "nt":
        (m, k), (n, _) = a.shape, b.shape
        a_spec = pl.BlockSpec((tm, k), lambda j, i: (i, 0))
        b_spec = pl.BlockSpec((tn, k), lambda j, i: (j, 0))
        dn = NT
    else:
        (k, m), (_, n) = a.shape, b.shape
        a_spec = pl.BlockSpec((k, tm), lambda j, i: (0, i))
        b_spec = pl.BlockSpec((k, tn), lambda j, i: (0, j))
        dn = TN
    assert m % tm == 0 and n % tn == 0, (m, tm, n, tn)

    def body(a_ref, b_ref, o_ref):
        o_ref[...] = _dot(a_ref[...], b_ref[...], dn).astype(o_ref.dtype)

    return pl.pallas_call(
        body, name=name, grid=(n // tn, m // tm),
        in_specs=[a_spec, b_spec], out_specs=pl.BlockSpec((tm, tn), lambda j, i: (i, j)),
        out_shape=jax.ShapeDtypeStruct((m, n), out_dtype),
        compiler_params=_cparams(("parallel", "parallel")),
    )(a, b)


def _tiles_2d(r, cdim):
    if r % CHUNK == 0:
        return CHUNK, cdim, True
    return r, _tile(cdim, (256, 128)), False


def _mm_sum_nn(a_list, b_list, tm, tn, name):
    n_op = len(a_list)
    m, n = a_list[0].shape[0], b_list[0].shape[1]

    def body(*refs):
        acc = _dot(refs[0][...], refs[n_op][...])
        for i in range(1, n_op):
            acc = acc + _dot(refs[i][...], refs[n_op + i][...])
        refs[2 * n_op][...] = acc

    return pl.pallas_call(
        body, name=name, grid=(n // tn, m // tm),
        in_specs=([pl.BlockSpec((tm, a.shape[1]), lambda j, i: (i, 0)) for a in a_list]
                  + [pl.BlockSpec((b.shape[0], tn), lambda j, i: (0, j)) for b in b_list]),
        out_specs=pl.BlockSpec((tm, tn), lambda j, i: (i, j)),
        out_shape=jax.ShapeDtypeStruct((m, n), F32),
        compiler_params=_cparams(("parallel", "parallel")),
    )(*a_list, *b_list)


def _tile(n, prefs):
    for t in prefs:
        if n % t == 0:
            return t
    return n


def _prenorm_fwd(head, x2, w):
    p, d = x2.shape[0] + CHUNK, x2.shape[1]

    def body(head_ref, x_ref, w_ref, u_ref):
        i = pl.program_id(0)
        h = jnp.where(i == 0, head_ref[...], x_ref[...])
        ms = jnp.mean(h * h, axis=-1, keepdims=True)
        u_ref[...] = (h * lax.rsqrt(ms + EPS) * w_ref[...]).astype(BF16)

    return pl.pallas_call(
        body, name="prenorm_fwd", grid=(p // CHUNK,),
        in_specs=[_full((CHUNK, d)), pl.BlockSpec((CHUNK, d), lambda i: (jnp.maximum(i - 1, 0), 0)), _full((1, d))],
        out_specs=pl.BlockSpec((CHUNK, d), lambda i: (i, 0)),
        out_shape=jax.ShapeDtypeStruct((p, d), BF16),
        compiler_params=_cparams(("arbitrary",)),
    )(head, x2, w)


def _prenorm_bwd(head, x2, w, du, dout):
    p, d = x2.shape[0] + CHUNK, x2.shape[1]

    def body(head_ref, x_ref, w_ref, du_ref, dout_ref, gx_ref, ghead_ref, gw_ref):
        i = pl.program_id(0)
        h = jnp.where(i == 0, head_ref[...], x_ref[...])
        rstd = lax.rsqrt(jnp.mean(h * h, axis=-1, keepdims=True) + EPS)
        xhat = h * rstd
        dub = du_ref[...]
        dxh = dub * w_ref[...]
        dh = rstd * (dxh - xhat * jnp.mean(dxh * xhat, axis=-1, keepdims=True)) + dout_ref[...]

        @pl.when(i == 0)
        def _():
            ghead_ref[...] = dh
            gw_ref[...] = jnp.zeros_like(gw_ref)

        gx_ref[...] = dh
        gw_ref[0:1, :] += jnp.sum(dub * xhat, axis=0, keepdims=True)

    return pl.pallas_call(
        body, name="prenorm_bwd", grid=(p // CHUNK,),
        in_specs=[_full((CHUNK, d)), pl.BlockSpec((CHUNK, d), lambda i: (jnp.maximum(i - 1, 0), 0)), _full((1, d)),
                  pl.BlockSpec((CHUNK, d), lambda i: (i, 0)), pl.BlockSpec((CHUNK, d), lambda i: (i, 0))],
        out_specs=[pl.BlockSpec((CHUNK, d), lambda i: (jnp.maximum(i - 1, 0), 0)), _full((CHUNK, d)), _full((8, d))],
        out_shape=[jax.ShapeDtypeStruct(x2.shape, F32), jax.ShapeDtypeStruct((CHUNK, d), F32),
                   jax.ShapeDtypeStruct((8, d), F32)],
        compiler_params=_cparams(("arbitrary",)),
    )(head, x2, w, du, dout)


def _conv_pre(ext_ref, cw_ref, cb_ref):
    pre = cb_ref[...] + cw_ref[CONV_K - 1:CONV_K, :] * ext_ref[8:8 + CHUNK, :]
    for j in range(1, CONV_K):
        pre = pre + cw_ref[CONV_K - 1 - j:CONV_K - j, :] * ext_ref[8 - j:8 - j + CHUNK, :]
    return pre


def _ssd_scalars(dtf_ref, brow_ref, alog_ref, rowmask, hs, ha, tri):
    lane = lax.broadcasted_iota(jnp.int32, (1, LANES), 1)
    is_dt = lane < hs
    is_f = (lane >= hs) & (lane < hs + ha)
    dtr = dtf_ref[...] + brow_ref[...]
    sp = _softplus(dtr)
    dt = jnp.where(is_dt, sp, 0.0) * rowmask
    logf = jnp.where(is_f, jnp.minimum(dtr, 0.0) - jnp.log(1.0 + jnp.exp(-jnp.abs(dtr))), 0.0) * rowmask
    a_row = jnp.where(is_dt, -jnp.exp(alog_ref[...]), 0.0)
    run = _dot_tri(tri, dt * a_row + logf)
    return dtr, dt, a_row, run, is_dt, is_f


def _tri_mats():
    r = lax.broadcasted_iota(jnp.int32, (CHUNK, CHUNK), 0)
    c = lax.broadcasted_iota(jnp.int32, (CHUNK, CHUNK), 1)
    return r, c


def _ssd_fwd(xbc, z, dtf, conv_w, conv_b, brow, alog, dskip_l, ssd_norm, sel_t, hs, ha):
    p, cd = xbc.shape
    ds = z.shape[1]
    ns = (cd - ds) // (2 * SSD_GROUPS)
    gw = ds // SSD_GROUPS
    nch = p // CHUNK
    hpg = hs // SSD_GROUPS

    def body(xbc_ref, halo_ref, z_ref, dtf_ref, cw_ref, cb_ref, brow_ref, alog_ref, dsk_ref, nrm_ref, selt_ref,
             y_ref, yssd_ref, hin_ref, cf_ref, pre_ref, st_ref, carry_ref, yacc_ref, xc_s, ex_s, xdtb_s, xwb_s, ext_s):
        c = pl.program_id(0)

        @pl.when(c == 0)
        def _():
            st_ref[...] = jnp.zeros_like(st_ref)
            carry_ref[...] = jnp.zeros_like(carry_ref)

        rows = lax.broadcasted_iota(jnp.int32, (CHUNK, 1), 0)
        rowmask = jnp.where((rows >= PADN) | (c > 0), 1.0, 0.0)
        ri, ci = _tri_mats()
        causal = ri >= ci
        tri = jnp.where(causal, 1.0, 0.0).astype(BF16)

        ext_s[0:8, :] = halo_ref[...].astype(F32)[HALO - 8:, :] * jnp.where(c > 0, 1.0, 0.0)
        ext_s[8:, :] = xbc_ref[...].astype(F32)
        pre = _conv_pre(ext_s, cw_ref, cb_ref)
        pre_ref[...] = pre.astype(BF16)
        xc_s[...] = pre * _sigmoid(pre) * rowmask

        dtr, dt, a_row, run, is_dt, is_f = _ssd_scalars(dtf_ref, brow_ref, alog_ref, rowmask, hs, ha, tri)
        cf = run + carry_ref[...]
        cf_ref[...] = cf
        carry_ref[...] = jnp.where(is_f, cf[CHUNK - 1:CHUNK, :], 0.0)
        cs = jnp.where(is_dt, run, 0.0)
        cl = cs[CHUNK - 1:CHUNK, :]
        selt = selt_ref[...]
        ex_s[...] = _dot_sel(jnp.exp(cs), selt)
        cdec_x = _dot_sel(jnp.broadcast_to(jnp.exp(cl), (8, LANES)), selt)[0:1, :]
        cs_t = cs.T
        xdt = xc_s[:, :ds] * _dot_sel(dt, selt)
        xdtb_s[...] = xdt.astype(BF16)
        xwb_s[...] = (xdt * _dot_sel(jnp.exp(cl - cs), selt)).astype(BF16)

        lane = lax.broadcasted_iota(jnp.int32, (1, LANES), 1)
        half0 = lane < HEAD_DIM
        for g in range(SSD_GROUPS):
            bg = xc_s[:, ds + g * ns: ds + (g + 1) * ns].astype(BF16)
            cg = xc_s[:, ds + SSD_GROUPS * ns + g * ns: ds + SSD_GROUPS * ns + (g + 1) * ns].astype(BF16)
            gm = _dot(cg, bg, NT)
            gs = slice(g * gw, (g + 1) * gw)
            stg = st_ref[:, gs]
            stg_b = stg.astype(BF16)
            hin_ref[0, :, gs] = stg_b
            yoff = _dot(cg, stg_b) * ex_s[:, gs]
            for pr in range(gw // LANES):
                sl = slice(g * gw + pr * LANES, g * gw + (pr + 1) * LANES)
                xp = xdtb_s[:, sl]
                yd = jnp.zeros((CHUNK, LANES), F32)
                for j in range(2):
                    h = g * hpg + 2 * pr + j
                    seg = cs[:, h:h + 1] - cs_t[h:h + 1, :]
                    m = jnp.where(causal, gm * jnp.exp(jnp.minimum(seg, 0.0)), 0.0).astype(BF16)
                    sel = half0 if j == 0 else jnp.logical_not(half0)
                    yd = yd + _dot(m, jnp.where(sel, xp, jnp.zeros_like(xp)))
                yacc_ref[:, sl] = yd + yoff[:, pr * LANES:(pr + 1) * LANES] + dsk_ref[:, sl] * xc_s[:, sl]
            st_ref[:, gs] = stg * cdec_x[:, gs] + _dot(bg, xwb_s[:, gs], TN)

        y = yacc_ref[...]
        y_ref[...] = y.astype(BF16)
        zf = z_ref[...].astype(F32)
        u = y * zf * _sigmoid(zf)
        for g in range(SSD_GROUPS):
            gs = slice(g * gw, (g + 1) * gw)
            ug = u[:, gs]
            ms = jnp.mean(ug * ug, axis=-1, keepdims=True)
            yssd_ref[:, gs] = (ug * lax.rsqrt(ms + EPS) * nrm_ref[:, gs]).astype(BF16)

    rb = CHUNK // HALO
    return pl.pallas_call(
        body, name="ssd_fwd", grid=(nch,),
        in_specs=[pl.BlockSpec((CHUNK, cd), lambda c: (c, 0)),
                  pl.BlockSpec((HALO, cd), lambda c: (jnp.maximum(c * rb - 1, 0), 0)),
                  pl.BlockSpec((CHUNK, ds), lambda c: (c, 0)),
                  pl.BlockSpec((CHUNK, LANES), lambda c: (c, 0)),
                  _full((CONV_K, cd)), _full((1, cd)), _full((1, LANES)), _full((1, LANES)),
                  _full((1, ds)), _full((1, ds)), _full((LANES, ds))],
        out_specs=[pl.BlockSpec((CHUNK, ds), lambda c: (c, 0)), pl.BlockSpec((CHUNK, ds), lambda c: (c, 0)),
                   pl.BlockSpec((1, ns, ds), lambda c: (c, 0, 0)), pl.BlockSpec((CHUNK, LANES), lambda c: (c, 0)),
                   pl.BlockSpec((CHUNK, cd), lambda c: (c, 0))],
        out_shape=[jax.ShapeDtypeStruct((p, ds), BF16), jax.ShapeDtypeStruct((p, ds), BF16),
                   jax.ShapeDtypeStruct((nch, ns, ds), BF16), jax.ShapeDtypeStruct((p, LANES), F32),
                   jax.ShapeDtypeStruct((p, cd), BF16)],
        scratch_shapes=[pltpu.VMEM((ns, ds), F32), pltpu.VMEM((1, LANES), F32), pltpu.VMEM((CHUNK, ds), F32),
                        pltpu.VMEM((CHUNK, cd), F32), pltpu.VMEM((CHUNK, ds), F32),
                        pltpu.VMEM((CHUNK, ds), BF16), pltpu.VMEM((CHUNK, ds), BF16),
                        pltpu.VMEM((8 + CHUNK, cd), F32)],
        compiler_params=_cparams(("arbitrary",)),
    )(xbc, xbc, z, dtf, conv_w, conv_b, brow, alog, dskip_l, ssd_norm, sel_t)


def _ssd_bwd(dyssd, y, z, xbc, pre, dtf, hin, dcf, conv_w, brow, alog, dskip_l, ssd_norm, sel_t, sel, hs, ha):
    p, cd = xbc.shape
    ds = z.shape[1]
    ns = (cd - ds) // (2 * SSD_GROUPS)
    gw = ds // SSD_GROUPS
    nch = p // CHUNK
    hpg = hs // SSD_GROUPS

    def body(dyssd_ref, y_ref, z_ref, xbc_ref, pre_ref, dtf_ref, hin_ref, dcf_ref, cw_ref, brow_ref,
             alog_ref, dsk_ref, nrm_ref, selt_ref, sel_ref,
             dxbc_ref, dz_ref, ddtf_ref, gcw_ref, gcb_ref, gnrm_ref, gsm_ref,
             dst_ref, nxt_ref, fcar_ref, gdsk_ref, dxc_ref, xc_s, dsl_s, dtx_s, ex_s, wx_s, dy_s, xdtb_s, xwb_s,
             dyb_s, dyeb_s):
        step = pl.program_id(0)
        c = nch - 1 - step

        @pl.when(step == 0)
        def _():
            dst_ref[...] = jnp.zeros_like(dst_ref)
            nxt_ref[...] = jnp.zeros_like(nxt_ref)
            fcar_ref[...] = jnp.zeros_like(fcar_ref)
            gdsk_ref[...] = jnp.zeros_like(gdsk_ref)
            gcw_ref[...] = jnp.zeros_like(gcw_ref)
            gcb_ref[...] = jnp.zeros_like(gcb_ref)
            gnrm_ref[...] = jnp.zeros_like(gnrm_ref)
            gsm_ref[...] = jnp.zeros_like(gsm_ref)

        rows = lax.broadcasted_iota(jnp.int32, (CHUNK, 1), 0)
        rowmask = jnp.where((rows >= PADN) | (c > 0), 1.0, 0.0)
        ri, ci = _tri_mats()
        causal = ri >= ci
        anti = ci >= ri
        tri = jnp.where(causal, 1.0, 0.0).astype(BF16)
        rtri = jnp.where(anti, 1.0, 0.0).astype(BF16)

        pre = pre_ref[...].astype(F32)
        sg = _sigmoid(pre)
        xc_s[...] = pre * sg * rowmask
        dsl_s[...] = sg * (1.0 + pre * (1.0 - sg)) * rowmask

        dtr, dt, a_row, run, is_dt, is_f = _ssd_scalars(dtf_ref, brow_ref, alog_ref, rowmask, hs, ha, tri)
        cs = jnp.where(is_dt, run, 0.0)
        cl = cs[CHUNK - 1:CHUNK, :]
        selt = selt_ref[...]
        selm = sel_ref[...]
        dtx_s[...] = _dot_sel(dt, selt)
        ex_s[...] = _dot_sel(jnp.exp(cs), selt)
        wx_s[...] = _dot_sel(jnp.exp(cl - cs), selt)
        cdec = jnp.exp(cl)
        cdec_x = _dot_sel(jnp.broadcast_to(cdec, (8, LANES)), selt)[0:1, :]
        cs_t = cs.T
        xdt = xc_s[:, :ds] * dtx_s[...]
        xdtb_s[...] = xdt.astype(BF16)
        xwb_s[...] = (xdt * wx_s[...]).astype(BF16)

        yv = y_ref[...].astype(F32)
        zf = z_ref[...].astype(F32)
        sz = _sigmoid(zf)
        u = yv * zf * sz
        dyo = dyssd_ref[...].astype(F32)
        du_parts = []
        for g in range(SSD_GROUPS):
            gs = slice(g * gw, (g + 1) * gw)
            ug = u[:, gs]
            rstd = lax.rsqrt(jnp.mean(ug * ug, axis=-1, keepdims=True) + EPS)
            yhat = ug * rstd
            dyg = dyo[:, gs]
            gnrm_ref[0:1, gs] += jnp.sum(dyg * yhat, axis=0, keepdims=True)
            dyh = dyg * nrm_ref[:, gs]
            du_parts.append(rstd * (dyh - yhat * jnp.mean(dyh * yhat, axis=-1, keepdims=True)))
        du = jnp.concatenate(du_parts, axis=1)
        dy = du * zf * sz
        dz_ref[...] = (du * yv * sz * (1.0 + zf * (1.0 - sz))).astype(BF16)
        dy_s[...] = dy
        dyb_s[...] = dy.astype(BF16)
        dyeb_s[...] = (dy * ex_s[...]).astype(BF16)
        gdsk_ref[...] += jnp.sum(dy * xc_s[:, :ds], axis=0, keepdims=True)
        lane = lax.broadcasted_iota(jnp.int32, (1, LANES), 1)
        half0 = lane < HEAD_DIM
        x_parts, yo_parts, t4_parts = [], [], []
        dcs = jnp.zeros((CHUNK, LANES), F32)
        for g in range(SSD_GROUPS):
            gs = slice(g * gw, (g + 1) * gw)
            bsl = slice(ds + g * ns, ds + (g + 1) * ns)
            csl = slice(ds + SSD_GROUPS * ns + g * ns, ds + SSD_GROUPS * ns + (g + 1) * ns)
            bg = xc_s[:, bsl].astype(BF16)
            cg = xc_s[:, csl].astype(BF16)
            gm = _dot(cg, bg, NT)
            gm_t = _dot(bg, cg, NT)
            stg_b = hin_ref[0, :, gs]
            dstg = dst_ref[:, gs]
            dstg_b = dstg.astype(BF16)
            t4_parts.append(jnp.sum(dstg * stg_b.astype(F32), axis=0, keepdims=True))
            zst = _dot(bg, dstg_b) * wx_s[:, gs]
            x_parts.append(xc_s[:, gs] * dtx_s[:, gs] * zst)
            yo_parts.append(dy_s[:, gs] * (_dot(cg, stg_b) * ex_s[:, gs]))
            dgsum = jnp.zeros((CHUNK, CHUNK), F32)
            dgtsum = jnp.zeros((CHUNK, CHUNK), F32)
            for pr in range(gw // LANES):
                sl = slice(g * gw + pr * LANES, g * gw + (pr + 1) * LANES)
                xp = xdtb_s[:, sl]
                dyp = dyb_s[:, sl]
                dxd = zst[:, pr * LANES:(pr + 1) * LANES]
                for j in range(2):
                    h = g * hpg + 2 * pr + j
                    sel_l = half0 if j == 0 else jnp.logical_not(half0)
                    seg = cs[:, h:h + 1] - cs_t[h:h + 1, :]
                    lm = jnp.where(causal, jnp.exp(jnp.minimum(seg, 0.0)), 0.0)
                    lmt = jnp.where(anti, jnp.exp(jnp.minimum(-seg, 0.0)), 0.0)
                    dyp_m = jnp.where(sel_l, dyp, jnp.zeros_like(dyp))
                    xp_m = jnp.where(sel_l, xp, jnp.zeros_like(xp))
                    dxd = dxd + _dot((gm_t * lmt).astype(BF16), dyp_m)
                    dg = _dot(dyp_m, xp, NT) * lm
                    dgt = _dot(xp_m, dyp, NT) * lmt
                    dgsum = dgsum + dg
                    dgtsum = dgtsum + dgt
                    qrow = (jnp.sum(dg * gm, axis=1, keepdims=True) - jnp.sum(dgt * gm_t, axis=1, keepdims=True))
                    dcs = dcs + jnp.where(lane == h, qrow, 0.0)
                dxc_ref[:, sl] = dxd
            dxc_ref[:, csl] = _dot(dgsum.astype(BF16), bg) + _dot(dyeb_s[:, gs], stg_b, NT)
            dxc_ref[:, bsl] = _dot(dgtsum.astype(BF16), cg) + _dot(xwb_s[:, gs], dstg_b, NT)
            dst_ref[:, gs] = dstg * cdec_x[:, gs] + _dot(cg, dyeb_s[:, gs], TN)

        dxdt = dxc_ref[:, :ds]
        xst = _dot_sel(jnp.concatenate(x_parts, axis=1), selm)
        yo = _dot_sel(jnp.concatenate(yo_parts, axis=1), selm)
        t4 = _dot_sel(jnp.concatenate([jnp.concatenate(t4_parts, axis=1), jnp.zeros((7, ds), F32)], axis=0), selm)
        dcl = jnp.sum(xst, axis=0, keepdims=True) + cdec * t4[0:1, :]
        dcs = dcs + yo - xst + jnp.where(rows == CHUNK - 1, dcl, 0.0)
        da_ = _dot_tri(rtri, dcs)
        ddt = _dot_sel(dxdt * xc_s[:, :ds], selm) + da_ * a_row
        dcf_blk = dcf_ref[...]
        dlogf = _dot_tri(rtri, dcf_blk) + fcar_ref[...]
        fcar_ref[...] += jnp.sum(dcf_blk, axis=0, keepdims=True)
        sgd = _sigmoid(dtr)
        ddtf = (jnp.where(is_dt, ddt * sgd, 0.0) + jnp.where(is_f, dlogf * (1.0 - sgd), 0.0)) * rowmask
        ddtf_ref[...] = ddtf
        gsm_ref[0:1, :] += jnp.sum(ddtf, axis=0, keepdims=True)
        gsm_ref[1:2, :] += jnp.sum(da_ * dt, axis=0, keepdims=True) * a_row

        dxc_ref[:, :ds] = dxdt * dtx_s[...] + dsk_ref[...] * dy_s[...]
        dpre = dxc_ref[...] * dsl_s[...]
        nxt_ref[0:CHUNK, :] = dpre
        gcb_ref[0:1, :] += jnp.sum(dpre, axis=0, keepdims=True)
        xr = xbc_ref[...].astype(F32)
        gcw_ref[CONV_K - 1:CONV_K, :] += jnp.sum(dpre * xr, axis=0, keepdims=True)
        dxr = cw_ref[CONV_K - 1:CONV_K, :] * dpre
        for j in range(1, CONV_K):
            up = nxt_ref[j:j + CHUNK, :]
            gcw_ref[CONV_K - 1 - j:CONV_K - j, :] += jnp.sum(up * xr, axis=0, keepdims=True)
            dxr = dxr + cw_ref[CONV_K - 1 - j:CONV_K - j, :] * up
        nxt_ref[CHUNK:, :] = dpre[0:8, :]
        dxbc_ref[...] = dxr.astype(BF16)

        @pl.when(step == nch - 1)
        def _():
            gsm_ref[2:3, :] = _dot_sel(jnp.broadcast_to(gdsk_ref[...], (8, ds)), selm)[0:1, :]

    rev = lambda s: nch - 1 - s
    blk = lambda w: pl.BlockSpec((CHUNK, w), lambda s: (rev(s), 0))
    return pl.pallas_call(
        body, name="ssd_bwd", grid=(nch,),
        in_specs=[blk(ds), blk(ds), blk(ds), blk(cd), blk(cd),
                  blk(LANES), pl.BlockSpec((1, ns, ds), lambda s: (rev(s), 0, 0)), blk(LANES),
                  _full((CONV_K, cd)), _full((1, LANES)), _full((1, LANES)),
                  _full((1, ds)), _full((1, ds)), _full((LANES, ds)), _full((ds, LANES))],
        out_specs=[blk(cd), blk(ds), blk(LANES), _full((8, cd)), _full((8, cd)), _full((8, ds)), _full((8, LANES))],
        out_shape=[jax.ShapeDtypeStruct((p, cd), BF16), jax.ShapeDtypeStruct((p, ds), BF16),
                   jax.ShapeDtypeStruct((p, LANES), F32), jax.ShapeDtypeStruct((8, cd), F32),
                   jax.ShapeDtypeStruct((8, cd), F32), jax.ShapeDtypeStruct((8, ds), F32),
                   jax.ShapeDtypeStruct((8, LANES), F32)],
        scratch_shapes=[pltpu.VMEM((ns, ds), F32), pltpu.VMEM((CHUNK + 8, cd), F32), pltpu.VMEM((1, LANES), F32),
                        pltpu.VMEM((1, ds), F32), pltpu.VMEM((CHUNK, cd), F32),
                        pltpu.VMEM((CHUNK, cd), F32), pltpu.VMEM((CHUNK, cd), F32),
                        pltpu.VMEM((CHUNK, ds), F32), pltpu.VMEM((CHUNK, ds), F32), pltpu.VMEM((CHUNK, ds), F32),
                        pltpu.VMEM((CHUNK, ds), F32), pltpu.VMEM((CHUNK, ds), BF16), pltpu.VMEM((CHUNK, ds), BF16),
                        pltpu.VMEM((CHUNK, ds), BF16), pltpu.VMEM((CHUNK, ds), BF16)],
        compiler_params=_cparams(("arbitrary",)),
    )(dyssd, y, z, xbc, pre, dtf, hin, dcf, conv_w, brow, alog, dskip_l, ssd_norm, sel_t, sel)


def _attn_fwd(q, k, v, ck, blk):
    p, da = q.shape
    npair, nkb = ck.shape[0], ck.shape[1]
    scale = 1.0 / math.sqrt(HEAD_DIM)

    def body(q_ref, k_ref, v_ref, ck_ref, o_ref, lse_ref):
        i = pl.program_id(1)
        lane = lax.broadcasted_iota(jnp.int32, (1, LANES), 1)
        sels = [lane < HEAD_DIM, lane >= HEAD_DIM]
        ones = [jnp.where(lane == HEAD_DIM, 1.0, 0.0).astype(BF16), jnp.where(lane == 0, 1.0, 0.0).astype(BF16)]
        qb = q_ref[...] * scale
        qms = [jnp.where(sel, qb, jnp.zeros_like(qb)) for sel in sels]
        cmask = (lax.broadcasted_iota(jnp.int32, (blk, blk), 1) <= lax.broadcasted_iota(jnp.int32, (blk, blk), 0))

        def step(kb, carry, masked, nk=1):
            r0 = pl.multiple_of(kb * blk, blk)
            ks = k_ref[pl.ds(r0, nk * blk), :]
            vs = v_ref[pl.ds(r0, nk * blk), :]
            out = []
            for j in range(2):
                m, acc = carry[2 * j], carry[2 * j + 1]
                ckr = jnp.concatenate([ck_ref[0, kb + t, j:j + 1, :] for t in range(nk)], axis=1)
                s = _dot(qms[j], ks, NT) - ckr
                if masked:
                    s = jnp.where(cmask, s, NEG)
                mn = jnp.maximum(m, jnp.max(s, axis=-1, keepdims=True))
                pr = jnp.exp(s - mn).astype(BF16)
                acc = jnp.exp(m - mn) * acc + _dot(pr, jnp.where(sels[j], vs, ones[j]))
                out += [mn, acc]
            return tuple(out)

        init = (jnp.full((blk, 1), NEG, F32), jnp.zeros((blk, LANES), F32)) * 2
        n4 = i // 4
        n2 = (i - 4 * n4) // 2
        carry = lax.fori_loop(0, n4, lambda t, c: step(4 * t, c, False, 4), init)
        carry = lax.fori_loop(0, n2, lambda t, c: step(4 * n4 + 2 * t, c, False, 2), carry)
        carry = lax.fori_loop(4 * n4 + 2 * n2, i, lambda kb, c: step(kb, c, False), carry)
        m0, a0, m1, a1 = step(i, carry, True)
        l0 = a0[:, HEAD_DIM:HEAD_DIM + 1]
        l1 = a1[:, 0:1]
        o_ref[...] = jnp.where(sels[0], a0 / l0, a1 / l1).astype(BF16)
        lse_ref[...] = jnp.where(sels[0], m0 + jnp.log(l0), m1 + jnp.log(l1))

    return pl.pallas_call(
        body, name="attn_fwd", grid=(npair, p // blk),
        in_specs=[pl.BlockSpec((blk, LANES), lambda h, i: (i, h)),
                  pl.BlockSpec((p, LANES), lambda h, i: (0, h)), pl.BlockSpec((p, LANES), lambda h, i: (0, h)),
                  pl.BlockSpec((1, nkb, 8, blk), lambda h, i: (h, 0, 0, 0))],
        out_specs=[pl.BlockSpec((blk, LANES), lambda h, i: (i, h)), pl.BlockSpec((blk, LANES), lambda h, i: (i, h))],
        out_shape=[jax.ShapeDtypeStruct((p, da), BF16), jax.ShapeDtypeStruct((p, da), F32)],
        compiler_params=_cparams(("parallel", "arbitrary")),
    )(q, k, v, ck)


def _attn_bwd(q, k, v, o, do, lse_rep, ck, blk):
    p, da = q.shape
    npair, nkb = ck.shape[0], ck.shape[1]
    nq = p // blk
    scale = 1.0 / math.sqrt(HEAD_DIM)

    def body(k_ref, v_ref, q_ref, do_ref, o_ref, lse_ref, ck_ref, dk_ref, dv_ref, dq_ref, dcs_ref, rsum_ref, dq_acc):
        jb = pl.program_id(1)

        @pl.when(jb == 0)
        def _():
            dq_acc[...] = jnp.zeros_like(dq_acc)

        ks = k_ref[...]
        vs = v_ref[...]
        lane = lax.broadcasted_iota(jnp.int32, (1, LANES), 1)
        sels = [lane < HEAD_DIM, lane >= HEAD_DIM]
        ones = [jnp.where(lane == HEAD_DIM, 1.0, 0.0).astype(BF16), jnp.where(lane == 0, 1.0, 0.0).astype(BF16)]
        kss = ks * scale
        kmo = [jnp.where(sels[j], kss, ones[j]) for j in range(2)]
        cmask = (lax.broadcasted_iota(jnp.int32, (blk, blk), 1) <= lax.broadcasted_iota(jnp.int32, (blk, blk), 0))

        def step(ib, carry, masked, nb=1):
            rows = nb * blk
            r0 = pl.multiple_of(ib * blk, blk)
            qb = q_ref[pl.ds(r0, rows), :] * scale
            dob = do_ref[pl.ds(r0, rows), :]
            prod = dob.astype(F32) * o_ref[pl.ds(r0, rows), :].astype(F32)
            out = []
            for j in range(2):
                dk, dv = carry[2 * j], carry[2 * j + 1]
                qm = jnp.where(sels[j], qb, jnp.zeros_like(qb))
                dom = jnp.where(sels[j], dob, jnp.zeros_like(dob))
                lse = lse_ref[pl.ds(r0, rows), HEAD_DIM * j:HEAD_DIM * j + 1]
                dlt = jnp.sum(jnp.where(sels[j], prod, 0.0), axis=-1, keepdims=True)
                s = _dot(qm, ks, NT) - ck_ref[0, 0, j:j + 1, :] - lse
                pm = jnp.exp(jnp.minimum(s, 0.0))
                if masked:
                    pm = jnp.where(cmask, pm, 0.0)
                ds_b = (pm * (_dot(dom, vs, NT) - dlt)).astype(BF16)
                dv = dv + _dot(pm.astype(BF16), dom, TN)
                dk = dk + _dot(ds_b, jnp.where(sels[j], qb, ones[j]), TN)
                dq_acc[pl.ds(r0, rows), LANES * j:LANES * (j + 1)] += _dot(ds_b, kmo[j])
                out += [dk, dv]
            return tuple(out)

        zero = jnp.zeros((blk, LANES), F32)
        carry = step(jb, (zero, zero, zero, zero), True)
        n4 = (nq - 1 - jb) // 4
        n2 = (nq - 1 - jb - 4 * n4) // 2
        carry = lax.fori_loop(0, n4, lambda t, c: step(jb + 1 + 4 * t, c, False, 4), carry)
        carry = lax.fori_loop(0, n2, lambda t, c: step(jb + 1 + 4 * n4 + 2 * t, c, False, 2), carry)
        dk0, dv0, dk1, dv1 = lax.fori_loop(jb + 1 + 4 * n4 + 2 * n2, nq, lambda ib, c: step(ib, c, False), carry)
        dk_ref[...] = jnp.where(sels[0], dk0, dk1).astype(BF16)
        dv_ref[...] = (dv0 + dv1).astype(BF16)
        lane8 = lax.broadcasted_iota(jnp.int32, (1, 8), 1)
        pair8 = lambda c0, c1: jnp.where(lane8 == 0, c0, jnp.where(lane8 == 1, c1, 0.0))
        dcs_ref[0] = pair8(dk0[:, HEAD_DIM:HEAD_DIM + 1], dk1[:, 0:1])

        @pl.when(jb == nkb - 1)
        def _():
            a0 = dq_acc[:, :LANES]
            a1 = dq_acc[:, LANES:]
            dq_ref[...] = jnp.where(sels[0], a0, a1).astype(BF16)
            rsum_ref[0] = pair8(a0[:, HEAD_DIM:HEAD_DIM + 1], a1[:, 0:1])

    colblk = pl.BlockSpec((blk, LANES), lambda h, j: (j, h))
    colfull = pl.BlockSpec((p, LANES), lambda h, j: (0, h))
    ckspec = pl.BlockSpec((1, 1, 8, blk), lambda h, j: (h, j, 0, 0))
    return pl.pallas_call(
        body, name="attn_bwd", grid=(npair, nkb),
        in_specs=[colblk, colblk, colfull, colfull, colfull, colfull, ckspec],
        out_specs=[colblk, colblk, colfull, pl.BlockSpec((1, blk, 8), lambda h, j: (h, j, 0)),
                   pl.BlockSpec((1, p, 8), lambda h, j: (h, 0, 0))],
        out_shape=[jax.ShapeDtypeStruct((p, da), BF16), jax.ShapeDtypeStruct((p, da), BF16),
                   jax.ShapeDtypeStruct((p, da), BF16), jax.ShapeDtypeStruct((npair, p, 8), F32),
                   jax.ShapeDtypeStruct((npair, p, 8), F32)],
        scratch_shapes=[pltpu.VMEM((p, 2 * LANES), F32)],
        compiler_params=_cparams(("parallel", "arbitrary")),
    )(k, v, q, do, o, lse_rep, ck)


def _rows3(i):
    return jnp.maximum(3 * i - 1, 0), 3 * i, 3 * i + 1


def _tail_fwd(yssd, o, zatt, graw, head, x2, tgt2, wps, wpa, wout, gate_bias, norm_post, tm):
    p, ds = yssd.shape
    da = o.shape[1]
    d = x2.shape[1]
    nsub = tm // CHUNK

    def body(yssd_ref, o_ref, zatt_ref, g_ref, head_ref, *rest):
        x_refs, t_refs = rest[:nsub], rest[nsub:2 * nsub]
        (wps_ref, wpa_ref, wout_ref, gb_ref, np_ref,
         yatt_ref, mrg_ref, a_ref, b_ref, dzo_ref, dout_ref, red_ref) = rest[2 * nsub:]
        i = pl.program_id(0)

        @pl.when(i == 0)
        def _():
            red_ref[...] = jnp.zeros_like(red_ref)

        first = jnp.where(i == 0, head_ref[...], x_refs[0][...])
        h = jnp.concatenate([first] + [r[...] for r in x_refs[1:]], axis=0)
        tgt = jnp.concatenate([r[...] for r in t_refs], axis=0)
        rows = lax.broadcasted_iota(jnp.int32, (tm, 1), 0)
        valid = jnp.where((i > 0) | (rows >= CHUNK), 1.0, 0.0)
        ob = o_ref[...].astype(F32)
        za = zatt_ref[...].astype(F32)
        yatt_b = (ob * za * _sigmoid(za)).astype(BF16)
        yatt_ref[...] = yatt_b
        a = _dot(yssd_ref[...], wps_ref[...])
        b = _dot(yatt_b, wpa_ref[...])
        a_ref[...] = a.astype(BF16)
        b_ref[...] = b.astype(BF16)
        gr = g_ref[...].astype(F32) + gb_ref[...]
        mrg_b = (_sigmoid(gr[:, :d]) * a + _sigmoid(gr[:, d:]) * b).astype(BF16)
        mrg_ref[...] = mrg_b
        zo = _dot(mrg_b, wout_ref[...])
        rstd = lax.rsqrt(jnp.mean(zo * zo, axis=-1, keepdims=True) + EPS)
        zh = zo * rstd
        npw = np_ref[...]
        err = (h + zh * npw - tgt) * valid
        dout = err * (1.0 / d)
        dout_ref[...] = dout
        dzh = dout * npw
        dzo_ref[...] = (rstd * (dzh - zh * jnp.mean(dzh * zh, axis=-1, keepdims=True))).astype(BF16)
        red_ref[0:1, :] += jnp.sum(dout * zh, axis=0, keepdims=True)
        red_ref[1:2, 0:1] += jnp.sum(jnp.sum(err * err, axis=1, keepdims=True), axis=0, keepdims=True) * (0.5 / d)

    row = lambda w: pl.BlockSpec((tm, w), lambda i: (i, 0))
    once = lambda shape: pl.BlockSpec(shape, lambda i: (0,) * len(shape), pipeline_mode=pl.Buffered(1))
    if nsub == 1:
        subs = [pl.BlockSpec((CHUNK, d), lambda i: (jnp.maximum(i - 1, 0), 0))]
    else:
        subs = [pl.BlockSpec((CHUNK, d), functools.partial(lambda i, k: (_rows3(i)[k], 0), k=k)) for k in range(3)]
    sd = jax.ShapeDtypeStruct
    return pl.pallas_call(
        body, name="tail_fwd", grid=(p // tm,),
        in_specs=[row(ds), row(da), row(da), row(2 * d), _full((CHUNK, d))] + subs + subs
                 + [once((ds, d)), once((da, d)), once((d, d)), _full((1, 2 * d)), _full((1, d))],
        out_specs=[row(da), row(d), row(d), row(d), row(d), row(d), _full((8, d))],
        out_shape=[sd((p, da), BF16), sd((p, d), BF16), sd((p, d), BF16), sd((p, d), BF16), sd((p, d), BF16),
                   sd((p, d), F32), sd((8, d), F32)],
        compiler_params=_cparams(("arbitrary",)),
    )(yssd, o, zatt, graw, head, *([x2] * nsub), *([tgt2] * nsub), wps, wpa, wout, gate_bias, norm_post)


def _tail_bwd(dzo, a_b, b_b, graw, o, zatt, wps, wpa, wout, gate_bias, tm):
    p, d = dzo.shape
    ds, da = wps.shape[0], wpa.shape[0]

    def body(dzo_ref, a_ref, b_ref, g_ref, o_ref, zatt_ref, wps_ref, wpa_ref, wout_ref, gb_ref,
             da_ref, db_ref, dg_ref, dyssd_ref, do_ref, dzatt_ref, red_ref):
        i = pl.program_id(0)

        @pl.when(i == 0)
        def _():
            red_ref[...] = jnp.zeros_like(red_ref)

        gr = g_ref[...].astype(F32) + gb_ref[...]
        gs = _sigmoid(gr[:, :d])
        ga = _sigmoid(gr[:, d:])
        dm = _dot(dzo_ref[...], wout_ref[...], NT)
        da_b = (gs * dm).astype(BF16)
        db_b = (ga * dm).astype(BF16)
        da_ref[...] = da_b
        db_ref[...] = db_b
        dgs = dm * a_ref[...].astype(F32) * gs * (1.0 - gs)
        dga = dm * b_ref[...].astype(F32) * ga * (1.0 - ga)
        dg_ref[:, :d] = dgs.astype(BF16)
        dg_ref[:, d:] = dga.astype(BF16)
        red_ref[0:1, :d] += jnp.sum(dgs, axis=0, keepdims=True)
        red_ref[0:1, d:] += jnp.sum(dga, axis=0, keepdims=True)
        dyssd_ref[...] = _dot(da_b, wps_ref[...], NT).astype(BF16)
        dya = _dot(db_b, wpa_ref[...], NT)
        ob = o_ref[...].astype(F32)
        za = zatt_ref[...].astype(F32)
        sza = _sigmoid(za)
        do_ref[...] = (dya * za * sza).astype(BF16)
        dzatt_ref[...] = (dya * ob * sza * (1.0 + za * (1.0 - sza))).astype(BF16)

    row = lambda w: pl.BlockSpec((tm, w), lambda i: (i, 0))
    once = lambda shape: pl.BlockSpec(shape, lambda i: (0,) * len(shape), pipeline_mode=pl.Buffered(1))
    sd = jax.ShapeDtypeStruct
    return pl.pallas_call(
        body, name="tail_bwd", grid=(p // tm,),
        in_specs=[row(d), row(d), row(d), row(2 * d), row(da), row(da),
                  once((ds, d)), once((da, d)), once((d, d)), _full((1, 2 * d))],
        out_specs=[row(d), row(d), row(2 * d), row(ds), row(da), row(da), _full((8, 2 * d))],
        out_shape=[sd((p, d), BF16), sd((p, d), BF16), sd((p, 2 * d), BF16), sd((p, ds), BF16), sd((p, da), BF16),
                   sd((p, da), BF16), sd((8, 2 * d), F32)],
        compiler_params=_cparams(("arbitrary",)),
    )(dzo, a_b, b_b, graw, o, zatt, wps, wpa, wout, gate_bias)


def _adamw_math(w, g, m, v):
    m2 = ADAM_B1 * m + (1.0 - ADAM_B1) * g
    v2 = ADAM_B2 * v + (1.0 - ADAM_B2) * (g * g)
    m_hat = m2 / (1.0 - ADAM_B1 ** ADAM_STEP)
    v_hat = v2 / (1.0 - ADAM_B2 ** ADAM_STEP)
    delta = -ADAM_LR * (m_hat / (jnp.sqrt(v_hat) + ADAM_EPS) + ADAM_WD * w)
    return delta, m2, v2


def _adamw_small(params, red, name):
    names = list(params)
    n = len(names)
    extra = [params[k][3] for k in names if not isinstance(params[k][3], tuple)]

    def body(*refs):
        w_refs, m_refs, v_refs = refs[:n], refs[n:2 * n], refs[2 * n:3 * n]
        red_ref = refs[3 * n]
        g_refs = iter(refs[3 * n + 1:3 * n + 1 + len(extra)])
        outs = refs[3 * n + 1 + len(extra):]
        for i, k in enumerate(names):
            where = params[k][3]
            rows, cols = w_refs[i].shape
            if isinstance(where, tuple):
                g = red_ref[where[0]:where[0] + rows, where[1]:where[1] + cols]
            else:
                g = next(g_refs)[...]
            delta, m2, v2 = _adamw_math(w_refs[i][...], g, m_refs[i][...], v_refs[i][...])
            for o, val in zip(outs[4 * i:4 * i + 4], (g, delta, m2, v2)):
                o[...] = val

    vm = pl.BlockSpec(memory_space=pltpu.VMEM)
    ws, ms, vs = ([params[k][j] for k in names] for j in range(3))
    out = pl.pallas_call(
        body, name=name,
        out_shape=[jax.ShapeDtypeStruct(w.shape, F32) for w in ws for _ in range(4)],
        in_specs=[vm] * (3 * n + 1 + len(extra)), out_specs=[vm] * (4 * n),
    )(*ws, *ms, *vs, red, *extra)
    return {k: tuple(out[4 * i:4 * i + 4]) for i, k in enumerate(names)}


def _adamw(w, g, m, v, name, parts=False, part_row0=0):
    r, cdim = w.shape
    tr, tc, by_rows = _tiles_2d(r, cdim)
    pick = (lambda i: (i, 0)) if by_rows else (lambda i: (0, i))
    assert part_row0 % tr == 0
    gpick = (lambda i: (i + part_row0 // tr, 0)) if by_rows else (lambda i: (part_row0 // tr, i))

    def body(w_ref, g_ref, m_ref, v_ref, go_ref, d_ref, mo_ref, vo_ref):
        if parts:
            g = g_ref[0].astype(F32)
            for s in range(1, g_ref.shape[0]):
                g = g + g_ref[s].astype(F32)
        else:
            g = g_ref[...]
        delta, m2, v2 = _adamw_math(w_ref[...], g, m_ref[...], v_ref[...])
        go_ref[...] = g
        d_ref[...] = delta
        mo_ref[...] = m2
        vo_ref[...] = v2

    blk = pl.BlockSpec((tr, tc), pick)
    gspec = pl.BlockSpec((g.shape[0], tr, tc), lambda i: (0,) + gpick(i)) if parts else blk
    return pl.pallas_call(
        body, name=name, grid=((r // tr) * (cdim // tc),),
        in_specs=[blk, gspec, blk, blk], out_specs=[blk] * 4,
        out_shape=[jax.ShapeDtypeStruct((r, cdim), F32)] * 4,
        compiler_params=_cparams(("parallel",)),
    )(w, g, m, v)


def _pad_cols(a, width):
    return jnp.pad(a, ((0, 0), (0, width - a.shape[1])))


def _pack_small_shard(conv_w_sh, meta_sh, width):
    return jnp.concatenate([_pad_cols(conv_w_sh, width), jnp.zeros((4, width), F32), _pad_cols(meta_sh, width)], axis=0)


def _pack_small_rep(norm_pre, norm_post, gate_bias, ssd_norm, conv_b, misc, width):
    rows = [norm_pre, norm_post, gate_bias, ssd_norm, conv_b, misc]
    return jnp.concatenate([_pad_cols(r, width) for r in rows] + [jnp.zeros((2, width), F32)], axis=0)


def kernel(x, meta_tokens, norm_pre, w_in, conv_w, conv_b, dt_bias, a_log, d_skip, ssd_norm, fgate_bias, gate_bias, w_proj_ssd, w_proj_att, w_out, norm_post, loss_target, m_meta_tokens, m_norm_pre, m_w_in, m_conv_w, m_conv_b, m_dt_bias, m_a_log, m_d_skip, m_ssd_norm, m_fgate_bias, m_gate_bias, m_w_proj_ssd, m_w_proj_att, m_w_out, m_norm_post, v_meta_tokens, v_norm_pre, v_w_in, v_conv_w, v_conv_b, v_dt_bias, v_a_log, v_d_skip, v_ssd_norm, v_fgate_bias, v_gate_bias, v_w_proj_ssd, v_w_proj_att, v_w_out, v_norm_post):
    seq, d = x.shape[1], x.shape[2]
    p = seq + CHUNK
    hs, ha = dt_bias.shape[1], fgate_bias.shape[1]
    ds, cd = ssd_norm.shape[1], conv_b.shape[1]
    da = ha * HEAD_DIM
    nc8 = w_in.shape[2]
    cws = cd // N_DEV
    msh = d // N_DEV
    r1, r2, r3 = ds // N_DEV, da // N_DEV, d // N_DEV
    me = _dev_index(*_my_pos())
    x2, tgt2 = x[0], loss_target[0]

    win_sh = jnp.transpose(w_in[0]).astype(BF16)
    rows_sh = jnp.concatenate([w_proj_ssd[0], w_proj_att[0], w_out[0]], axis=0).astype(BF16)
    small_sh = _pack_small_shard(conv_w[0], meta_tokens, cws)
    win_all, small_all = _all_gather([win_sh, small_sh], "gather_weights")
    rows_sh, win_all = lax.optimization_barrier((rows_sh, win_all))
    rows_sems, rows_thru, rows_land, rows_token = _bcast_start(rows_sh, "gather_rows_start")
    cuts = [0, ds, ds + cd, ds + cd + hs, ds + cd + hs + da, ds + cd + hs + 2 * da, ds + cd + hs + 3 * da,
            ds + cd + hs + 4 * da, ds + cd + hs + 4 * da + ha, ds + cd + hs + 4 * da + ha + 2 * d]

    def piece_rows(r0, r1):
        parts = [win_all[s, max(r0, s * nc8) - s * nc8:min(r1, (s + 1) * nc8) - s * nc8]
                 for s in range(N_DEV) if max(r0, s * nc8) < min(r1, (s + 1) * nc8)]
        return parts[0] if len(parts) == 1 else jnp.concatenate(parts, axis=0)

    w_z, w_xbc, w_dt, w_zatt, w_q, w_k, w_v, w_f, w_g = [piece_rows(cuts[i], cuts[i + 1]) for i in range(9)]
    w_dtf = jnp.concatenate([w_dt, w_f, jnp.zeros((LANES - hs - ha, d), BF16)], axis=0)
    conv_w_full = jnp.transpose(small_all[:, 0:CONV_K, :], (1, 0, 2)).reshape(CONV_K, cd)
    meta_full = jnp.transpose(small_all[:, 8:8 + N_META, :msh], (1, 0, 2)).reshape(N_META, d)
    head = jnp.concatenate([jnp.zeros((PADN, d), F32), meta_full + rows_token[0:1, 0:1]], axis=0)

    u = _prenorm_fwd(head, x2, norm_pre)
    tm = _att_block(p)
    seg_w = [w_z, w_xbc, w_zatt, w_q, w_k, w_v, w_g]
    zs, xbc, zatt, q, k, v, graw = [
        _mm(u, w, "nt", BF16, _tile(p, (1408, tm)), _tile(w.shape[0], (1024, 512, 256, 128)), "inproj_%d" % i)
        for i, w in enumerate(seg_w)]
    dtf = _mm(u, w_dtf, "nt", F32, _tile(p, (1408, tm)), LANES, "inproj_dtf")

    brow = jnp.concatenate([dt_bias, fgate_bias, jnp.zeros((1, LANES - hs - ha), F32)], axis=1)
    alog_row = _pad_cols(a_log, LANES)
    dskip_l = jnp.repeat(d_skip, HEAD_DIM, axis=1)
    sel_t = (lax.broadcasted_iota(jnp.int32, (LANES, ds), 1) // HEAD_DIM
             == lax.broadcasted_iota(jnp.int32, (LANES, ds), 0)).astype(BF16)
    sel = sel_t.T
    y, yssd, hin, cf, pre = _ssd_fwd(xbc, zs, dtf, conv_w_full, conv_b, brow, alog_row, dskip_l, ssd_norm, sel_t, hs, ha)

    blk = _att_block(p)
    nkb, npair = p // blk, ha // 2
    cum = jnp.where(lax.broadcasted_iota(jnp.int32, (p, 1), 0) < PADN, -NEG, cf[:, hs:hs + ha])
    ck = jnp.transpose(cum.T.reshape(npair, 2, nkb, blk), (0, 2, 1, 3))
    ck = jnp.pad(ck, ((0, 0), (0, 0), (0, 6), (0, 0)))
    o, lse_rep = _attn_fwd(q, k, v, ck, blk)

    rows_all = _bcast_wait(rows_sems, rows_thru, rows_land, lse_rep, "gather_rows_wait")
    wps = rows_all[:, :r1].reshape(ds, d)
    wpa = rows_all[:, r1:r1 + r2].reshape(da, d)
    wout = rows_all[:, r1 + r2:].reshape(d, d)

    yatt, mrg, a_b, b_b, dzo, dout, red_fwd = _tail_fwd(
        yssd, o, zatt, graw, head, x2, tgt2, wps, wpa, wout, gate_bias, norm_post, tm)
    da_, db_, dgraw, dyssd, d_o, dzatt, red_bwd = _tail_bwd(dzo, a_b, b_b, graw, o, zatt, wps, wpa, wout, gate_bias, tm)

    tw = _tile(d, (512, 256, 128))
    g_wout = _mm(mrg, dzo, "tn", BF16, tw, tw, "wgrad_out")
    g_wps = _mm(yssd, da_, "tn", BF16, _tile(ds, (512, 256, 128)), tw, "wgrad_ps")
    g_wpa = _mm(yatt, db_, "tn", BF16, _tile(da, (512, 256, 128)), tw, "wgrad_pa")

    dk, dv, dq, dcs, rsum = _attn_bwd(q, k, v, o, d_o, lse_rep, ck, blk)
    dcum = jnp.transpose((rsum - dcs)[:, :, 0:2], (1, 0, 2)).reshape(p, ha)
    dcf = jnp.pad(dcum, ((0, 0), (hs, LANES - hs - ha)))
    dxbc, dzs, ddtf, gcw, gcb, gnrm, gsm = _ssd_bwd(
        dyssd, y, zs, xbc, pre, dtf, hin, dcf, conv_w_full, brow, alog_row, dskip_l, ssd_norm, sel_t, sel, hs, ha)
    ddtf_b = ddtf.astype(BF16)

    dsegs = [dzs, dxbc, dzatt, dq, dk, dv, dgraw, ddtf_b]
    gsegs = [_mm(dsg, u, "tn", BF16, _tile(dsg.shape[1], (512, 256, 128)), tw, "wgrad_in_%d" % i)
             for i, dsg in enumerate(dsegs)]
    g_z, g_xbc, g_zatt, g_q, g_k, g_v, g_g, g_dtf = gsegs
    gw_full = jnp.concatenate([g_z, g_xbc, g_dtf[:hs], g_zatt, g_q, g_k, g_v, g_dtf[hs:hs + ha], g_g], axis=0)
    gwin_parts = gw_full.reshape(N_DEV, nc8, d)
    grows_parts = jnp.concatenate([g_wps.reshape(N_DEV, r1, d), g_wpa.reshape(N_DEV, r2, d),
                                   g_wout.reshape(N_DEV, r3, d)], axis=1)

    core = lax.axis_index("c").astype(jnp.int32).reshape(1)
    sib_win, sib_rows = _exchange_sibling([gwin_parts, grows_parts], "scatter_grads_sibling")
    chip_win = _pair_add(gwin_parts, sib_win, core, "pair_add_w_in")
    chip_rows = _pair_add(grows_parts, sib_rows, core, "pair_add_rows")
    sems, thru, lands, token = _exchange_chips_start([chip_win, chip_rows], "scatter_grads_start")
    dsegs_after = dsegs[:-1] + [ddtf_b + token[0:1, 0:1].astype(BF16)]
    du = _mm_sum_nn(dsegs_after, seg_w + [w_dtf], tm, _tile(d, (256, 128)), "dgrad_in")
    gx, ghead, gnp = _prenorm_bwd(head, x2, norm_pre, du, dout)
    sent, got = _exchange_chips_wait(sems, thru, lands, gnp, "scatter_grads_wait")
    chip = me // 2
    recv_win, recv_rows = [lax.dynamic_update_slice_in_dim(g, lax.dynamic_slice_in_dim(s, chip, 1, axis=0), chip, axis=0)
                           for g, s in zip(got, sent)]
    gmisc = jnp.concatenate([gsm[0:1], gsm[1:2], gsm[2:3], _pad_cols(red_fwd[1:2, 0:1], LANES)], axis=1)
    small_g = jnp.concatenate([
        _pack_small_rep(gnp[0:1], red_fwd[0:1], red_bwd[0:1], gnrm[0:1], gcb[0:1], gmisc, cd),
        _pad_cols(gcw[0:CONV_K], cd), jnp.zeros((4, cd), F32), _pad_cols(ghead[PADN:], cd)], axis=0)
    sg_sems, sg_thru, sg_land, sg_token = _bcast_start(small_g, "reduce_small_start")

    upd_in = _adamw(jnp.transpose(w_in[0]) + sg_token[0:1, 0:1], recv_win, jnp.transpose(m_w_in[0]),
                    jnp.transpose(v_w_in[0]), "adamw_w_in", parts=True)
    upd_ps = _adamw(w_proj_ssd[0] + sg_token[0:1, 0:1], recv_rows, m_w_proj_ssd[0], v_w_proj_ssd[0],
                    "adamw_w_proj_ssd", parts=True, part_row0=0)
    upd_pa = _adamw(w_proj_att[0], recv_rows, m_w_proj_att[0], v_w_proj_att[0], "adamw_w_proj_att", parts=True,
                    part_row0=r1)
    upd_out = _adamw(w_out[0], recv_rows, m_w_out[0], v_w_out[0], "adamw_w_out", parts=True, part_row0=r1 + r2)
    all_done = upd_in[1][0:8, 0:LANES] + upd_ps[1][0:8, 0:LANES] + upd_pa[1][0:8, 0:LANES] + upd_out[1][0:8, 0:LANES]
    red = _sum_slots(_bcast_wait(sg_sems, sg_thru, sg_land, all_done, "reduce_small_wait"), "reduce_small_sum")
    loss = red[5, 3 * LANES]
    g_conv_w = lax.dynamic_slice_in_dim(red[8:8 + CONV_K], me * cws, cws, axis=1)
    g_meta = lax.dynamic_slice_in_dim(red[16:16 + N_META, :d], me * msh, msh, axis=1)
    small = {
        "meta_tokens": (meta_tokens, m_meta_tokens, v_meta_tokens, g_meta),
        "norm_pre": (norm_pre, m_norm_pre, v_norm_pre, (0, 0)),
        "conv_w": (conv_w[0], m_conv_w[0], v_conv_w[0], g_conv_w),
        "conv_b": (conv_b, m_conv_b, v_conv_b, (4, 0)),
        "dt_bias": (dt_bias, m_dt_bias, v_dt_bias, (5, 0)),
        "a_log": (a_log, m_a_log, v_a_log, (5, LANES)),
        "d_skip": (d_skip, m_d_skip, v_d_skip, (5, 2 * LANES)),
        "ssd_norm": (ssd_norm, m_ssd_norm, v_ssd_norm, (3, 0)),
        "fgate_bias": (fgate_bias, m_fgate_bias, v_fgate_bias, (5, hs)),
        "gate_bias": (gate_bias, m_gate_bias, v_gate_bias, (2, 0)),
        "norm_post": (norm_post, m_norm_post, v_norm_post, (1, 0)),
    }
    upd_small = _adamw_small(small, red, "adamw_small")

    def leaves(i):
        sm = {k: v[i] for k, v in upd_small.items()}
        return [sm["meta_tokens"], sm["norm_pre"], jnp.transpose(upd_in[i])[None], sm["conv_w"][None], sm["conv_b"],
                sm["dt_bias"], sm["a_log"], sm["d_skip"], sm["ssd_norm"], sm["fgate_bias"], sm["gate_bias"],
                upd_ps[i][None], upd_pa[i][None], upd_out[i][None], sm["norm_post"]]

    return tuple([loss, gx[None]] + leaves(0) + leaves(1) + leaves(2) + leaves(3))
```

```python
import functools
import math

import jax
import jax.numpy as jnp
from jax import lax
from jax.experimental import pallas as pl
from jax.experimental.pallas import tpu as pltpu

F32 = jnp.float32
BF16 = jnp.bfloat16

N_DEV = 8
N_META = 16
CHUNK = 128
PADN = CHUNK - N_META
HEAD_DIM = 64
SSD_GROUPS = 4
CONV_K = 4
EPS = 1e-6
NEG = -1e30
LANES = 128
HALO = 16

ADAM_LR = 0.001
ADAM_B1 = 0.9
ADAM_B2 = 0.999
ADAM_EPS = 1e-08
ADAM_WD = 0.01
ADAM_STEP = 10

VMEM_LIMIT = 56 * 1024 * 1024

NN = (((1,), (0,)), ((), ()))
NT = (((1,), (1,)), ((), ()))
TN = (((0,), (0,)), ((), ()))
MESH = pl.DeviceIdType.MESH


def _dot(a, b, dims=NN):
    return lax.dot_general(a, b, dims, preferred_element_type=F32)


def _split2(x):
    hi = x.astype(BF16)
    lo = (x - hi.astype(F32)).astype(BF16)
    return hi, lo


def _dot_sel(x, sel):
    hi, lo = _split2(x)
    return _dot(hi, sel) + _dot(lo, sel)


def _dot_tri(tri, x):
    h1 = x.astype(BF16)
    r1 = x - h1.astype(F32)
    h2 = r1.astype(BF16)
    h3 = (r1 - h2.astype(F32)).astype(BF16)
    return _dot(tri, h1) + _dot(tri, h2) + _dot(tri, h3)


def _sigmoid(x):
    return 1.0 / (1.0 + jnp.exp(-x))


def _softplus(x):
    return jnp.maximum(x, 0.0) + jnp.log(1.0 + jnp.exp(-jnp.abs(x)))


def _cparams(sem=None, vmem=VMEM_LIMIT):
    kw = {"vmem_limit_bytes": vmem}
    if sem is not None:
        kw["dimension_semantics"] = sem
    return pltpu.CompilerParams(**kw)


def _full(shape):
    nd = len(shape)
    return pl.BlockSpec(shape, lambda *_: (0,) * nd)


def _att_block(p):
    return 384 if p % 384 == 0 else CHUNK


def _my_pos():
    return lax.axis_index("x"), lax.axis_index("y"), lax.axis_index("c")


def _dev_index(x, y, c):
    return 4 * x + 2 * y + c


FLIPS = [(fx, fy, fc) for fx in (0, 1) for fy in (0, 1) for fc in (0, 1)][1:]


def _flip(pos, f):
    return tuple((1 - p) if fi else p for p, fi in zip(pos, f))


def _all_gather(bufs, name):
    nb = len(bufs)

    def body(*refs):
        ins, outs = refs[:nb], refs[nb:2 * nb]
        send_sems, recv_sems, local_sems = refs[2 * nb:]
        x, y, c = _my_pos()
        me = _dev_index(x, y, c)
        sibling = (x, y, 1 - c)
        chips = [(1 - x, y), (x, 1 - y), (1 - x, 1 - y)]

        def copy(b, k, block_idx, to, src=None):
            dst = outs[b].at[block_idx]
            return pltpu.make_async_remote_copy(
                src_ref=dst if src is None else src, dst_ref=dst,
                send_sem=send_sems.at[b, k], recv_sem=recv_sems.at[b, k],
                device_id=to, device_id_type=MESH)

        started = []
        for b in range(nb):
            mine = pltpu.make_async_copy(ins[b], outs[b].at[me], local_sems.at[b])
            mine.start()
            started.append(mine)
        first = []
        for b in range(nb):
            first.append(copy(b, 0, me, sibling, src=ins[b]))
            for j, chip in enumerate(chips):
                first.append(copy(b, 1 + j, me, (chip[0], chip[1], c), src=ins[b]))
        for cp in first:
            cp.start()
        passed = []
        for j, chip in enumerate(chips):
            blk = _dev_index(chip[0], chip[1], c)
            for b in range(nb):
                copy(b, 1 + j, blk, (x, y, c)).wait_recv()
                fwd = copy(b, 4 + j, blk, sibling)
                fwd.start()
                passed.append(fwd)
        for b in range(nb):
            copy(b, 0, _dev_index(x, y, 1 - c), (x, y, c)).wait_recv()
        for j, chip in enumerate(chips):
            blk = _dev_index(chip[0], chip[1], 1 - c)
            for b in range(nb):
                copy(b, 4 + j, blk, (x, y, c)).wait_recv()
        for cp in first + passed:
            cp.wait_send()
        for mine in started:
            mine.wait()

    any_spec = pl.BlockSpec(memory_space=pl.ANY)
    return pl.pallas_call(
        body, name=name,
        out_shape=[jax.ShapeDtypeStruct((N_DEV,) + b.shape, b.dtype) for b in bufs],
        in_specs=[any_spec] * nb, out_specs=[any_spec] * nb,
        scratch_shapes=[pltpu.SemaphoreType.DMA((nb, 7)), pltpu.SemaphoreType.DMA((nb, 7)),
                        pltpu.SemaphoreType.DMA((nb,))],
    )(*bufs)


N_CHIP = 4
CHIP_FLIPS = [(1, 0), (0, 1), (1, 1)]


def _exchange_sibling(bufs, name):
    nb = len(bufs)

    def body(*refs):
        ins, outs = refs[:nb], refs[nb:2 * nb]
        send_sems, recv_sems = refs[2 * nb:]
        x, y, c = _my_pos()

        def copy(b, k):
            return pltpu.make_async_remote_copy(
                src_ref=ins[b].at[2 * k + (1 - c)], dst_ref=outs[b].at[k],
                send_sem=send_sems.at[b, k], recv_sem=recv_sems.at[b, k],
                device_id=(x, y, 1 - c), device_id_type=MESH)

        cps = [copy(b, k) for b in range(nb) for k in range(N_CHIP)]
        for cp in cps:
            cp.start()
        for cp in cps:
            cp.wait()

    any_spec = pl.BlockSpec(memory_space=pl.ANY)
    return pl.pallas_call(
        body, name=name,
        out_shape=[jax.ShapeDtypeStruct((N_CHIP,) + b.shape[1:], b.dtype) for b in bufs],
        in_specs=[any_spec] * nb, out_specs=[any_spec] * nb,
        scratch_shapes=[pltpu.SemaphoreType.DMA((nb, N_CHIP)), pltpu.SemaphoreType.DMA((nb, N_CHIP))],
    )(*bufs)


def _pair_add(mine, recv, core, name):
    _, r, cdim = mine.shape
    tr, tc, by_rows = _tiles_2d(r, cdim)
    pick = (lambda i: (i, 0)) if by_rows else (lambda i: (0, i))

    def body(core_ref, a_ref, b_ref, o_ref):
        o_ref[0] = (a_ref[0].astype(F32) + b_ref[0].astype(F32)).astype(o_ref.dtype)

    return pl.pallas_call(
        body, name=name,
        grid_spec=pltpu.PrefetchScalarGridSpec(
            num_scalar_prefetch=1, grid=(N_CHIP, (r // tr) * (cdim // tc)),
            in_specs=[pl.BlockSpec((1, tr, tc), lambda k, i, core_ref: (2 * k + core_ref[0],) + pick(i)),
                      pl.BlockSpec((1, tr, tc), lambda k, i, core_ref: (k,) + pick(i))],
            out_specs=pl.BlockSpec((1, tr, tc), lambda k, i, core_ref: (k,) + pick(i))),
        out_shape=jax.ShapeDtypeStruct((N_CHIP, r, cdim), mine.dtype),
        compiler_params=_cparams(("parallel", "parallel")),
    )(core, mine, recv)


def _chip_peer(x, y, f):
    return ((1 - x) if f[0] else x), ((1 - y) if f[1] else y)


def _exchange_chips_start(bufs, name):
    nb = len(bufs)
    nsem = 2 * 3 * nb

    def body(*refs):
        ins, lands = refs[:nb], refs[nb:2 * nb]
        sems = refs[2 * nb:2 * nb + nsem]
        token = refs[-1]
        x, y, c = _my_pos()
        for b in range(nb):
            for j, f in enumerate(CHIP_FLIPS):
                px, py = _chip_peer(x, y, f)
                pltpu.make_async_remote_copy(
                    src_ref=ins[b].at[2 * px + py], dst_ref=lands[b].at[2 * x + y],
                    send_sem=sems[2 * (3 * b + j)], recv_sem=sems[2 * (3 * b + j) + 1],
                    device_id=(px, py, c), device_id_type=MESH).start()
        token[...] = jnp.zeros_like(token)

    hbm = pl.BlockSpec(memory_space=pltpu.HBM)
    sem = pl.BlockSpec(memory_space=pltpu.SEMAPHORE)
    out = pl.pallas_call(
        body, name=name,
        out_shape=(*([pltpu.SemaphoreType.DMA(())] * nsem),
                   *[pltpu.HBM(b.shape, b.dtype) for b in bufs], *[pltpu.HBM(b.shape, b.dtype) for b in bufs],
                   jax.ShapeDtypeStruct((8, LANES), F32)),
        in_specs=[hbm] * (2 * nb),
        out_specs=(*([sem] * nsem), *([hbm] * (2 * nb)), pl.BlockSpec(memory_space=pltpu.VMEM)),
        input_output_aliases={i: nsem + i for i in range(2 * nb)},
        compiler_params=pltpu.CompilerParams(has_side_effects=pltpu.SideEffectType.DATAFLOW_SIDE_EFFECTING),
    )(*[pltpu.with_memory_space_constraint(b, pltpu.HBM) for b in bufs],
      *[pltpu.with_memory_space_constraint(lax.empty(b.shape, b.dtype), pltpu.HBM) for b in bufs])
    return out[:nsem], out[nsem:nsem + nb], out[nsem + nb:nsem + 2 * nb], out[-1]


def _exchange_chips_wait(sems, thru, lands, after, name):
    nb = len(thru)
    nsem = len(sems)

    def body(*refs):
        ins, lnd = refs[:nb], refs[nb:2 * nb]
        sem_refs = refs[2 * nb:2 * nb + nsem]
        x, y, c = _my_pos()
        for b in range(nb):
            for j, f in enumerate(CHIP_FLIPS):
                px, py = _chip_peer(x, y, f)
                cp = pltpu.make_async_remote_copy(
                    src_ref=ins[b].at[2 * px + py], dst_ref=lnd[b].at[2 * px + py],
                    send_sem=sem_refs[2 * (3 * b + j)], recv_sem=sem_refs[2 * (3 * b + j) + 1],
                    device_id=(px, py, c), device_id_type=MESH)
                cp.wait_send()
                cp.wait_recv()

    hbm = pl.BlockSpec(memory_space=pltpu.HBM)
    sem = pl.BlockSpec(memory_space=pltpu.SEMAPHORE)
    out = pl.pallas_call(
        body, name=name,
        out_shape=tuple([pltpu.HBM(b.shape, b.dtype) for b in thru] + [pltpu.HBM(b.shape, b.dtype) for b in lands]),
        in_specs=[hbm] * (2 * nb) + [sem] * nsem + [pl.BlockSpec(memory_space=pl.ANY)],
        out_specs=tuple([hbm] * (2 * nb)),
        input_output_aliases={i: i for i in range(2 * nb)},
        compiler_params=pltpu.CompilerParams(has_side_effects=pltpu.SideEffectType.DATAFLOW_SIDE_EFFECTING),
    )(*thru, *lands, *sems, after)
    return out[:nb], out[nb:]


def _bcast_start(buf, name):
    nsem = 2 * len(FLIPS)

    def body(src, land, *rest):
        sems, token = rest[:nsem], rest[-1]
        pos = _my_pos()
        for k, f in enumerate(FLIPS):
            pltpu.make_async_remote_copy(
                src_ref=src, dst_ref=land.at[_dev_index(*pos)], send_sem=sems[2 * k], recv_sem=sems[2 * k + 1],
                device_id=_flip(pos, f), device_id_type=MESH).start()
        token[...] = jnp.zeros_like(token)

    hbm = pl.BlockSpec(memory_space=pltpu.HBM)
    sem = pl.BlockSpec(memory_space=pltpu.SEMAPHORE)
    land_shape = (N_DEV,) + buf.shape
    out = pl.pallas_call(
        body, name=name,
        out_shape=(*([pltpu.SemaphoreType.DMA(())] * nsem), pltpu.HBM(buf.shape, buf.dtype),
                   pltpu.HBM(land_shape, buf.dtype), jax.ShapeDtypeStruct((8, LANES), F32)),
        in_specs=[hbm, hbm],
        out_specs=(*([sem] * nsem), hbm, hbm, pl.BlockSpec(memory_space=pltpu.VMEM)),
        input_output_aliases={0: nsem, 1: nsem + 1},
        compiler_params=pltpu.CompilerParams(has_side_effects=pltpu.SideEffectType.DATAFLOW_SIDE_EFFECTING),
    )(pltpu.with_memory_space_constraint(buf, pltpu.HBM),
      pltpu.with_memory_space_constraint(lax.empty(land_shape, buf.dtype), pltpu.HBM))
    return out[:nsem], out[nsem], out[nsem + 1], out[-1]


def _bcast_wait(sems, thru, land, after, name):
    nsem = len(sems)

    def body(src, lnd, *rest):
        sem_refs = rest[:nsem]
        pos = _my_pos()
        for k, f in enumerate(FLIPS):
            peer = _flip(pos, f)
            cp = pltpu.make_async_remote_copy(
                src_ref=src, dst_ref=lnd.at[_dev_index(*peer)], send_sem=sem_refs[2 * k],
                recv_sem=sem_refs[2 * k + 1], device_id=peer, device_id_type=MESH)
            cp.wait_send()
            cp.wait_recv()

    hbm = pl.BlockSpec(memory_space=pltpu.HBM)
    sem = pl.BlockSpec(memory_space=pltpu.SEMAPHORE)
    sent, got = pl.pallas_call(
        body, name=name,
        out_shape=(pltpu.HBM(thru.shape, thru.dtype), pltpu.HBM(land.shape, land.dtype)),
        in_specs=[hbm, hbm] + [sem] * nsem + [pl.BlockSpec(memory_space=pl.ANY)],
        out_specs=(hbm, hbm), input_output_aliases={0: 0, 1: 1},
        compiler_params=pltpu.CompilerParams(has_side_effects=pltpu.SideEffectType.DATAFLOW_SIDE_EFFECTING),
    )(thru, land, *sems, after)
    return lax.dynamic_update_slice_in_dim(got, sent[None], _dev_index(*_my_pos()), axis=0)


def _sum_slots(v, name):
    _, r, cdim = v.shape

    def body(v_ref, o_ref):
        acc = v_ref[0]
        for s in range(1, N_DEV):
            acc = acc + v_ref[s]
        o_ref[...] = acc

    return pl.pallas_call(
        body, name=name, out_shape=jax.ShapeDtypeStruct((r, cdim), F32),
        in_specs=[_full((N_DEV, r, cdim))], out_specs=_full((r, cdim)), grid=(1,),
        compiler_params=_cparams(("arbitrary",)),
    )(v)


def _mm(a, b, dims, out_dtype, tm, tn, name):
    if dims == "nn":
        (m, k), (_, n) = a.shape, b.shape
        a_spec = pl.BlockSpec((tm, k), lambda j, i: (i, 0))
        b_spec = pl.BlockSpec((k, tn), lambda j, i: (0, j))
        dn = NN
    elif dims == "nt":
        (m, k), (n, _) = a.shape, b.shape
        a_spec = pl.BlockSpec((tm, k), lambda j, i: (i, 0))
        b_spec = pl.BlockSpec((tn, k), lambda j, i: (j, 0))
        dn = NT
    else:
        (k, m), (_, n) = a.shape, b.shape
        a_spec = pl.BlockSpec((k, tm), lambda j, i: (0, i))
        b_spec = pl.BlockSpec((k, tn), lambda j, i: (0, j))
        dn = TN
    assert m % tm == 0 and n % tn == 0, (m, tm, n, tn)

    def body(a_ref, b_ref, o_ref):
        o_ref[...] = _dot(a_ref[...], b_ref[...], dn).astype(o_ref.dtype)

    return pl.pallas_call(
        body, name=name, grid=(n // tn, m // tm),
        in_specs=[a_spec, b_spec], out_specs=pl.BlockSpec((tm, tn), lambda j, i: (i, j)),
        out_shape=jax.ShapeDtypeStruct((m, n), out_dtype),
        compiler_params=_cparams(("parallel", "parallel")),
    )(a, b)


def _tiles_2d(r, cdim):
    if r % CHUNK == 0:
        return CHUNK, cdim, True
    return r, _tile(cdim, (256, 128)), False


def _mm_sum_nn(a_list, b_list, tm, tn, name):
    n_op = len(a_list)
    m, n = a_list[0].shape[0], b_list[0].shape[1]

    def body(*refs):
        acc = _dot(refs[0][...], refs[n_op][...])
        for i in range(1, n_op):
            acc = acc + _dot(refs[i][...], refs[n_op + i][...])
        refs[2 * n_op][...] = acc

    return pl.pallas_call(
        body, name=name, grid=(n // tn, m // tm),
        in_specs=([pl.BlockSpec((tm, a.shape[1]), lambda j, i: (i, 0)) for a in a_list]
                  + [pl.BlockSpec((b.shape[0], tn), lambda j, i: (0, j)) for b in b_list]),
        out_specs=pl.BlockSpec((tm, tn), lambda j, i: (i, j)),
        out_shape=jax.ShapeDtypeStruct((m, n), F32),
        compiler_params=_cparams(("parallel", "parallel")),
    )(*a_list, *b_list)


def _tile(n, prefs):
    for t in prefs:
        if n % t == 0:
            return t
    return n


def _prenorm_fwd(head, x2, w):
    p, d = x2.shape[0] + CHUNK, x2.shape[1]

    def body(head_ref, x_ref, w_ref, u_ref):
        i = pl.program_id(0)
        h = jnp.where(i == 0, head_ref[...], x_ref[...])
        ms = jnp.mean(h * h, axis=-1, keepdims=True)
        u_ref[...] = (h * lax.rsqrt(ms + EPS) * w_ref[...]).astype(BF16)

    return pl.pallas_call(
        body, name="prenorm_fwd", grid=(p // CHUNK,),
        in_specs=[_full((CHUNK, d)), pl.BlockSpec((CHUNK, d), lambda i: (jnp.maximum(i - 1, 0), 0)), _full((1, d))],
        out_specs=pl.BlockSpec((CHUNK, d), lambda i: (i, 0)),
        out_shape=jax.ShapeDtypeStruct((p, d), BF16),
        compiler_params=_cparams(("arbitrary",)),
    )(head, x2, w)


def _prenorm_bwd(head, x2, w, du, dout):
    p, d = x2.shape[0] + CHUNK, x2.shape[1]

    def body(head_ref, x_ref, w_ref, du_ref, dout_ref, gx_ref, ghead_ref, gw_ref):
        i = pl.program_id(0)
        h = jnp.where(i == 0, head_ref[...], x_ref[...])
        rstd = lax.rsqrt(jnp.mean(h * h, axis=-1, keepdims=True) + EPS)
        xhat = h * rstd
        dub = du_ref[...]
        dxh = dub * w_ref[...]
        dh = rstd * (dxh - xhat * jnp.mean(dxh * xhat, axis=-1, keepdims=True)) + dout_ref[...]

        @pl.when(i == 0)
        def _():
            ghead_ref[...] = dh
            gw_ref[...] = jnp.zeros_like(gw_ref)

        gx_ref[...] = dh
        gw_ref[0:1, :] += jnp.sum(dub * xhat, axis=0, keepdims=True)

    return pl.pallas_call(
        body, name="prenorm_bwd", grid=(p // CHUNK,),
        in_specs=[_full((CHUNK, d)), pl.BlockSpec((CHUNK, d), lambda i: (jnp.maximum(i - 1, 0), 0)), _full((1, d)),
                  pl.BlockSpec((CHUNK, d), lambda i: (i, 0)), pl.BlockSpec((CHUNK, d), lambda i: (i, 0))],
        out_specs=[pl.BlockSpec((CHUNK, d), lambda i: (jnp.maximum(i - 1, 0), 0)), _full((CHUNK, d)), _full((8, d))],
        out_shape=[jax.ShapeDtypeStruct(x2.shape, F32), jax.ShapeDtypeStruct((CHUNK, d), F32),
                   jax.ShapeDtypeStruct((8, d), F32)],
        compiler_params=_cparams(("arbitrary",)),
    )(head, x2, w, du, dout)


def _conv_pre(ext_ref, cw_ref, cb_ref):
    pre = cb_ref[...] + cw_ref[CONV_K - 1:CONV_K, :] * ext_ref[8:8 + CHUNK, :]
    for j in range(1, CONV_K):
        pre = pre + cw_ref[CONV_K - 1 - j:CONV_K - j, :] * ext_ref[8 - j:8 - j + CHUNK, :]
    return pre


def _ssd_scalars(dtf_ref, brow_ref, alog_ref, rowmask, hs, ha, tri):
    lane = lax.broadcasted_iota(jnp.int32, (1, LANES), 1)
    is_dt = lane < hs
    is_f = (lane >= hs) & (lane < hs + ha)
    dtr = dtf_ref[...] + brow_ref[...]
    sp = _softplus(dtr)
    dt = jnp.where(is_dt, sp, 0.0) * rowmask
    logf = jnp.where(is_f, jnp.minimum(dtr, 0.0) - jnp.log(1.0 + jnp.exp(-jnp.abs(dtr))), 0.0) * rowmask
    a_row = jnp.where(is_dt, -jnp.exp(alog_ref[...]), 0.0)
    run = _dot_tri(tri, dt * a_row + logf)
    return dtr, dt, a_row, run, is_dt, is_f


def _tri_mats():
    r = lax.broadcasted_iota(jnp.int32, (CHUNK, CHUNK), 0)
    c = lax.broadcasted_iota(jnp.int32, (CHUNK, CHUNK), 1)
    return r, c


def _ssd_fwd(xbc, z, dtf, conv_w, conv_b, brow, alog, dskip_l, ssd_norm, sel_t, hs, ha):
    p, cd = xbc.shape
    ds = z.shape[1]
    ns = (cd - ds) // (2 * SSD_GROUPS)
    gw = ds // SSD_GROUPS
    nch = p // CHUNK
    hpg = hs // SSD_GROUPS

    def body(xbc_ref, halo_ref, z_ref, dtf_ref, cw_ref, cb_ref, brow_ref, alog_ref, dsk_ref, nrm_ref, selt_ref,
             y_ref, yssd_ref, hin_ref, cf_ref, pre_ref, st_ref, carry_ref, yacc_ref, xc_s, ex_s, xdtb_s, xwb_s, ext_s):
        c = pl.program_id(0)

        @pl.when(c == 0)
        def _():
            st_ref[...] = jnp.zeros_like(st_ref)
            carry_ref[...] = jnp.zeros_like(carry_ref)

        rows = lax.broadcasted_iota(jnp.int32, (CHUNK, 1), 0)
        rowmask = jnp.where((rows >= PADN) | (c > 0), 1.0, 0.0)
        ri, ci = _tri_mats()
        causal = ri >= ci
        tri = jnp.where(causal, 1.0, 0.0).astype(BF16)

        ext_s[0:8, :] = halo_ref[...].astype(F32)[HALO - 8:, :] * jnp.where(c > 0, 1.0, 0.0)
        ext_s[8:, :] = xbc_ref[...].astype(F32)
        pre = _conv_pre(ext_s, cw_ref, cb_ref)
        pre_ref[...] = pre.astype(BF16)
        xc_s[...] = pre * _sigmoid(pre) * rowmask

        dtr, dt, a_row, run, is_dt, is_f = _ssd_scalars(dtf_ref, brow_ref, alog_ref, rowmask, hs, ha, tri)
        cf = run + carry_ref[...]
        cf_ref[...] = cf
        carry_ref[...] = jnp.where(is_f, cf[CHUNK - 1:CHUNK, :], 0.0)
        cs = jnp.where(is_dt, run, 0.0)
        cl = cs[CHUNK - 1:CHUNK, :]
        selt = selt_ref[...]
        ex_s[...] = _dot_sel(jnp.exp(cs), selt)
        cdec_x = _dot_sel(jnp.broadcast_to(jnp.exp(cl), (8, LANES)), selt)[0:1, :]
        cs_t = cs.T
        xdt = xc_s[:, :ds] * _dot_sel(dt, selt)
        xdtb_s[...] = xdt.astype(BF16)
        xwb_s[...] = (xdt * _dot_sel(jnp.exp(cl - cs), selt)).astype(BF16)

        lane = lax.broadcasted_iota(jnp.int32, (1, LANES), 1)
        half0 = lane < HEAD_DIM
        for g in range(SSD_GROUPS):
            bg = xc_s[:, ds + g * ns: ds + (g + 1) * ns].astype(BF16)
            cg = xc_s[:, ds + SSD_GROUPS * ns + g * ns: ds + SSD_GROUPS * ns + (g + 1) * ns].astype(BF16)
            gm = _dot(cg, bg, NT)
            gs = slice(g * gw, (g + 1) * gw)
            stg = st_ref[:, gs]
            stg_b = stg.astype(BF16)
            hin_ref[0, :, gs] = stg_b
            yoff = _dot(cg, stg_b) * ex_s[:, gs]
            for pr in range(gw // LANES):
                sl = slice(g * gw + pr * LANES, g * gw + (pr + 1) * LANES)
                xp = xdtb_s[:, sl]
                yd = jnp.zeros((CHUNK, LANES), F32)
                for j in range(2):
                    h = g * hpg + 2 * pr + j
                    seg = cs[:, h:h + 1] - cs_t[h:h + 1, :]
                    m = jnp.where(causal, gm * jnp.exp(jnp.minimum(seg, 0.0)), 0.0).astype(BF16)
                    sel = half0 if j == 0 else jnp.logical_not(half0)
                    yd = yd + _dot(m, jnp.where(sel, xp, jnp.zeros_like(xp)))
                yacc_ref[:, sl] = yd + yoff[:, pr * LANES:(pr + 1) * LANES] + dsk_ref[:, sl] * xc_s[:, sl]
            st_ref[:, gs] = stg * cdec_x[:, gs] + _dot(bg, xwb_s[:, gs], TN)

        y = yacc_ref[...]
        y_ref[...] = y.astype(BF16)
        zf = z_ref[...].astype(F32)
        u = y * zf * _sigmoid(zf)
        for g in range(SSD_GROUPS):
            gs = slice(g * gw, (g + 1) * gw)
            ug = u[:, gs]
            ms = jnp.mean(ug * ug, axis=-1, keepdims=True)
            yssd_ref[:, gs] = (ug * lax.rsqrt(ms + EPS) * nrm_ref[:, gs]).astype(BF16)

    rb = CHUNK // HALO
    return pl.pallas_call(
        body, name="ssd_fwd", grid=(nch,),
        in_specs=[pl.BlockSpec((CHUNK, cd), lambda c: (c, 0)),
                  pl.BlockSpec((HALO, cd), lambda c: (jnp.maximum(c * rb - 1, 0), 0)),
                  pl.BlockSpec((CHUNK, ds), lambda c: (c, 0)),
                  pl.BlockSpec((CHUNK, LANES), lambda c: (c, 0)),
                  _full((CONV_K, cd)), _full((1, cd)), _full((1, LANES)), _full((1, LANES)),
                  _full((1, ds)), _full((1, ds)), _full((LANES, ds))],
        out_specs=[pl.BlockSpec((CHUNK, ds), lambda c: (c, 0)), pl.BlockSpec((CHUNK, ds), lambda c: (c, 0)),
                   pl.BlockSpec((1, ns, ds), lambda c: (c, 0, 0)), pl.BlockSpec((CHUNK, LANES), lambda c: (c, 0)),
                   pl.BlockSpec((CHUNK, cd), lambda c: (c, 0))],
        out_shape=[jax.ShapeDtypeStruct((p, ds), BF16), jax.ShapeDtypeStruct((p, ds), BF16),
                   jax.ShapeDtypeStruct((nch, ns, ds), BF16), jax.ShapeDtypeStruct((p, LANES), F32),
                   jax.ShapeDtypeStruct((p, cd), BF16)],
        scratch_shapes=[pltpu.VMEM((ns, ds), F32), pltpu.VMEM((1, LANES), F32), pltpu.VMEM((CHUNK, ds), F32),
                        pltpu.VMEM((CHUNK, cd), F32), pltpu.VMEM((CHUNK, ds), F32),
                        pltpu.VMEM((CHUNK, ds), BF16), pltpu.VMEM((CHUNK, ds), BF16),
                        pltpu.VMEM((8 + CHUNK, cd), F32)],
        compiler_params=_cparams(("arbitrary",)),
    )(xbc, xbc, z, dtf, conv_w, conv_b, brow, alog, dskip_l, ssd_norm, sel_t)


def _ssd_bwd(dyssd, y, z, xbc, pre, dtf, hin, dcf, conv_w, brow, alog, dskip_l, ssd_norm, sel_t, sel, hs, ha):
    p, cd = xbc.shape
    ds = z.shape[1]
    ns = (cd - ds) // (2 * SSD_GROUPS)
    gw = ds // SSD_GROUPS
    nch = p // CHUNK
    hpg = hs // SSD_GROUPS

    def body(dyssd_ref, y_ref, z_ref, xbc_ref, pre_ref, dtf_ref, hin_ref, dcf_ref, cw_ref, brow_ref,
             alog_ref, dsk_ref, nrm_ref, selt_ref, sel_ref,
             dxbc_ref, dz_ref, ddtf_ref, gcw_ref, gcb_ref, gnrm_ref, gsm_ref,
             dst_ref, nxt_ref, fcar_ref, gdsk_ref, dxc_ref, xc_s, dsl_s, dtx_s, ex_s, wx_s, dy_s, xdtb_s, xwb_s,
             dyb_s, dyeb_s):
        step = pl.program_id(0)
        c = nch - 1 - step

        @pl.when(step == 0)
        def _():
            dst_ref[...] = jnp.zeros_like(dst_ref)
            nxt_ref[...] = jnp.zeros_like(nxt_ref)
            fcar_ref[...] = jnp.zeros_like(fcar_ref)
            gdsk_ref[...] = jnp.zeros_like(gdsk_ref)
            gcw_ref[...] = jnp.zeros_like(gcw_ref)
            gcb_ref[...] = jnp.zeros_like(gcb_ref)
            gnrm_ref[...] = jnp.zeros_like(gnrm_ref)
            gsm_ref[...] = jnp.zeros_like(gsm_ref)

        rows = lax.broadcasted_iota(jnp.int32, (CHUNK, 1), 0)
        rowmask = jnp.where((rows >= PADN) | (c > 0), 1.0, 0.0)
        ri, ci = _tri_mats()
        causal = ri >= ci
        anti = ci >= ri
        tri = jnp.where(causal, 1.0, 0.0).astype(BF16)
        rtri = jnp.where(anti, 1.0, 0.0).astype(BF16)

        pre = pre_ref[...].astype(F32)
        sg = _sigmoid(pre)
        xc_s[...] = pre * sg * rowmask
        dsl_s[...] = sg * (1.0 + pre * (1.0 - sg)) * rowmask

        dtr, dt, a_row, run, is_dt, is_f = _ssd_scalars(dtf_ref, brow_ref, alog_ref, rowmask, hs, ha, tri)
        cs = jnp.where(is_dt, run, 0.0)
        cl = cs[CHUNK - 1:CHUNK, :]
        selt = selt_ref[...]
        selm = sel_ref[...]
        dtx_s[...] = _dot_sel(dt, selt)
        ex_s[...] = _dot_sel(jnp.exp(cs), selt)
        wx_s[...] = _dot_sel(jnp.exp(cl - cs), selt)
        cdec = jnp.exp(cl)
        cdec_x = _dot_sel(jnp.broadcast_to(cdec, (8, LANES)), selt)[0:1, :]
        cs_t = cs.T
        xdt = xc_s[:, :ds] * dtx_s[...]
        xdtb_s[...] = xdt.astype(BF16)
        xwb_s[...] = (xdt * wx_s[...]).astype(BF16)

        yv = y_ref[...].astype(F32)
        zf = z_ref[...].astype(F32)
        sz = _sigmoid(zf)
        u = yv * zf * sz
        dyo = dyssd_ref[...].astype(F32)
        du_parts = []
        for g in range(SSD_GROUPS):
            gs = slice(g * gw, (g + 1) * gw)
            ug = u[:, gs]
            rstd = lax.rsqrt(jnp.mean(ug * ug, axis=-1, keepdims=True) + EPS)
            yhat = ug * rstd
            dyg = dyo[:, gs]
            gnrm_ref[0:1, gs] += jnp.sum(dyg * yhat, axis=0, keepdims=True)
            dyh = dyg * nrm_ref[:, gs]
            du_parts.append(rstd * (dyh - yhat * jnp.mean(dyh * yhat, axis=-1, keepdims=True)))
        du = jnp.concatenate(du_parts, axis=1)
        dy = du * zf * sz
        dz_ref[...] = (du * yv * sz * (1.0 + zf * (1.0 - sz))).astype(BF16)
        dy_s[...] = dy
        dyb_s[...] = dy.astype(BF16)
        dyeb_s[...] = (dy * ex_s[...]).astype(BF16)
        gdsk_ref[...] += jnp.sum(dy * xc_s[:, :ds], axis=0, keepdims=True)
        lane = lax.broadcasted_iota(jnp.int32, (1, LANES), 1)
        half0 = lane < HEAD_DIM
        x_parts, yo_parts, t4_parts = [], [], []
        dcs = jnp.zeros((CHUNK, LANES), F32)
        for g in range(SSD_GROUPS):
            gs = slice(g * gw, (g + 1) * gw)
            bsl = slice(ds + g * ns, ds + (g + 1) * ns)
            csl = slice(ds + SSD_GROUPS * ns + g * ns, ds + SSD_GROUPS * ns + (g + 1) * ns)
            bg = xc_s[:, bsl].astype(BF16)
            cg = xc_s[:, csl].astype(BF16)
            gm = _dot(cg, bg, NT)
            gm_t = _dot(bg, cg, NT)
            stg_b = hin_ref[0, :, gs]
            dstg = dst_ref[:, gs]
            dstg_b = dstg.astype(BF16)
            t4_parts.append(jnp.sum(dstg * stg_b.astype(F32), axis=0, keepdims=True))
            zst = _dot(bg, dstg_b) * wx_s[:, gs]
            x_parts.append(xc_s[:, gs] * dtx_s[:, gs] * zst)
            yo_parts.append(dy_s[:, gs] * (_dot(cg, stg_b) * ex_s[:, gs]))
            dgsum = jnp.zeros((CHUNK, CHUNK), F32)
            dgtsum = jnp.zeros((CHUNK, CHUNK), F32)
            for pr in range(gw // LANES):
                sl = slice(g * gw + pr * LANES, g * gw + (pr + 1) * LANES)
                xp = xdtb_s[:, sl]
                dyp = dyb_s[:, sl]
                dxd = zst[:, pr * LANES:(pr + 1) * LANES]
                for j in range(2):
                    h = g * hpg + 2 * pr + j
                    sel_l = half0 if j == 0 else jnp.logical_not(half0)
                    seg = cs[:, h:h + 1] - cs_t[h:h + 1, :]
                    lm = jnp.where(causal, jnp.exp(jnp.minimum(seg, 0.0)), 0.0)
                    lmt = jnp.where(anti, jnp.exp(jnp.minimum(-seg, 0.0)), 0.0)
                    dyp_m = jnp.where(sel_l, dyp, jnp.zeros_like(dyp))
                    xp_m = jnp.where(sel_l, xp, jnp.zeros_like(xp))
                    dxd = dxd + _dot((gm_t * lmt).astype(BF16), dyp_m)
                    dg = _dot(dyp_m, xp, NT) * lm
                    dgt = _dot(xp_m, dyp, NT) * lmt
                    dgsum = dgsum + dg
                    dgtsum = dgtsum + dgt
                    qrow = (jnp.sum(dg * gm, axis=1, keepdims=True) - jnp.sum(dgt * gm_t, axis=1, keepdims=True))
                    dcs = dcs + jnp.where(lane == h, qrow, 0.0)
                dxc_ref[:, sl] = dxd
            dxc_ref[:, csl] = _dot(dgsum.astype(BF16), bg) + _dot(dyeb_s[:, gs], stg_b, NT)
            dxc_ref[:, bsl] = _dot(dgtsum.astype(BF16), cg) + _dot(xwb_s[:, gs], dstg_b, NT)
            dst_ref[:, gs] = dstg * cdec_x[:, gs] + _dot(cg, dyeb_s[:, gs], TN)

        dxdt = dxc_ref[:, :ds]
        xst = _dot_sel(jnp.concatenate(x_parts, axis=1), selm)
        yo = _dot_sel(jnp.concatenate(yo_parts, axis=1), selm)
        t4 = _dot_sel(jnp.concatenate([jnp.concatenate(t4_parts, axis=1), jnp.zeros((7, ds), F32)], axis=0), selm)
        dcl = jnp.sum(xst, axis=0, keepdims=True) + cdec * t4[0:1, :]
        dcs = dcs + yo - xst + jnp.where(rows == CHUNK - 1, dcl, 0.0)
        da_ = _dot_tri(rtri, dcs)
        ddt = _dot_sel(dxdt * xc_s[:, :ds], selm) + da_ * a_row
        dcf_blk = dcf_ref[...]
        dlogf = _dot_tri(rtri, dcf_blk) + fcar_ref[...]
        fcar_ref[...] += jnp.sum(dcf_blk, axis=0, keepdims=True)
        sgd = _sigmoid(dtr)
        ddtf = (jnp.where(is_dt, ddt * sgd, 0.0) + jnp.where(is_f, dlogf * (1.0 - sgd), 0.0)) * rowmask
        ddtf_ref[...] = ddtf
        gsm_ref[0:1, :] += jnp.sum(ddtf, axis=0, keepdims=True)
        gsm_ref[1:2, :] += jnp.sum(da_ * dt, axis=0, keepdims=True) * a_row

        dxc_ref[:, :ds] = dxdt * dtx_s[...] + dsk_ref[...] * dy_s[...]
        dpre = dxc_ref[...] * dsl_s[...]
        nxt_ref[0:CHUNK, :] = dpre
        gcb_ref[0:1, :] += jnp.sum(dpre, axis=0, keepdims=True)
        xr = xbc_ref[...].astype(F32)
        gcw_ref[CONV_K - 1:CONV_K, :] += jnp.sum(dpre * xr, axis=0, keepdims=True)
        dxr = cw_ref[CONV_K - 1:CONV_K, :] * dpre
        for j in range(1, CONV_K):
            up = nxt_ref[j:j + CHUNK, :]
            gcw_ref[CONV_K - 1 - j:CONV_K - j, :] += jnp.sum(up * xr, axis=0, keepdims=True)
            dxr = dxr + cw_ref[CONV_K - 1 - j:CONV_K - j, :] * up
        nxt_ref[CHUNK:, :] = dpre[0:8, :]
        dxbc_ref[...] = dxr.astype(BF16)

        @pl.when(step == nch - 1)
        def _():
            gsm_ref[2:3, :] = _dot_sel(jnp.broadcast_to(gdsk_ref[...], (8, ds)), selm)[0:1, :]

    rev = lambda s: nch - 1 - s
    blk = lambda w: pl.BlockSpec((CHUNK, w), lambda s: (rev(s), 0))
    return pl.pallas_call(
        body, name="ssd_bwd", grid=(nch,),
        in_specs=[blk(ds), blk(ds), blk(ds), blk(cd), blk(cd),
                  blk(LANES), pl.BlockSpec((1, ns, ds), lambda s: (rev(s), 0, 0)), blk(LANES),
                  _full((CONV_K, cd)), _full((1, LANES)), _full((1, LANES)),
                  _full((1, ds)), _full((1, ds)), _full((LANES, ds)), _full((ds, LANES))],
        out_specs=[blk(cd), blk(ds), blk(LANES), _full((8, cd)), _full((8, cd)), _full((8, ds)), _full((8, LANES))],
        out_shape=[jax.ShapeDtypeStruct((p, cd), BF16), jax.ShapeDtypeStruct((p, ds), BF16),
                   jax.ShapeDtypeStruct((p, LANES), F32), jax.ShapeDtypeStruct((8, cd), F32),
                   jax.ShapeDtypeStruct((8, cd), F32), jax.ShapeDtypeStruct((8, ds), F32),
                   jax.ShapeDtypeStruct((8, LANES), F32)],
        scratch_shapes=[pltpu.VMEM((ns, ds), F32), pltpu.VMEM((CHUNK + 8, cd), F32), pltpu.VMEM((1, LANES), F32),
                        pltpu.VMEM((1, ds), F32), pltpu.VMEM((CHUNK, cd), F32),
                        pltpu.VMEM((CHUNK, cd), F32), pltpu.VMEM((CHUNK, cd), F32),
                        pltpu.VMEM((CHUNK, ds), F32), pltpu.VMEM((CHUNK, ds), F32), pltpu.VMEM((CHUNK, ds), F32),
                        pltpu.VMEM((CHUNK, ds), F32), pltpu.VMEM((CHUNK, ds), BF16), pltpu.VMEM((CHUNK, ds), BF16),
                        pltpu.VMEM((CHUNK, ds), BF16), pltpu.VMEM((CHUNK, ds), BF16)],
        compiler_params=_cparams(("arbitrary",)),
    )(dyssd, y, z, xbc, pre, dtf, hin, dcf, conv_w, brow, alog, dskip_l, ssd_norm, sel_t, sel)


def _attn_fwd(q, k, v, ck, blk):
    p, da = q.shape
    npair, nkb = ck.shape[0], ck.shape[1]
    scale = 1.0 / math.sqrt(HEAD_DIM)

    def body(q_ref, k_ref, v_ref, ck_ref, o_ref, lse_ref):
        i = pl.program_id(1)
        lane = lax.broadcasted_iota(jnp.int32, (1, LANES), 1)
        sels = [lane < HEAD_DIM, lane >= HEAD_DIM]
        ones = [jnp.where(lane == HEAD_DIM, 1.0, 0.0).astype(BF16), jnp.where(lane == 0, 1.0, 0.0).astype(BF16)]
        qb = q_ref[...] * scale
        qms = [jnp.where(sel, qb, jnp.zeros_like(qb)) for sel in sels]
        cmask = (lax.broadcasted_iota(jnp.int32, (blk, blk), 1) <= lax.broadcasted_iota(jnp.int32, (blk, blk), 0))

        def step(kb, carry, masked, nk=1):
            r0 = pl.multiple_of(kb * blk, blk)
            ks = k_ref[pl.ds(r0, nk * blk), :]
            vs = v_ref[pl.ds(r0, nk * blk), :]
            out = []
            for j in range(2):
                m, acc = carry[2 * j], carry[2 * j + 1]
                ckr = jnp.concatenate([ck_ref[0, kb + t, j:j + 1, :] for t in range(nk)], axis=1)
                s = _dot(qms[j], ks, NT) - ckr
                if masked:
                    s = jnp.where(cmask, s, NEG)
                mn = jnp.maximum(m, jnp.max(s, axis=-1, keepdims=True))
                pr = jnp.exp(s - mn).astype(BF16)
                acc = jnp.exp(m - mn) * acc + _dot(pr, jnp.where(sels[j], vs, ones[j]))
                out += [mn, acc]
            return tuple(out)

        init = (jnp.full((blk, 1), NEG, F32), jnp.zeros((blk, LANES), F32)) * 2
        n4 = i // 4
        n2 = (i - 4 * n4) // 2
        carry = lax.fori_loop(0, n4, lambda t, c: step(4 * t, c, False, 4), init)
        carry = lax.fori_loop(0, n2, lambda t, c: step(4 * n4 + 2 * t, c, False, 2), carry)
        carry = lax.fori_loop(4 * n4 + 2 * n2, i, lambda kb, c: step(kb, c, False), carry)
        m0, a0, m1, a1 = step(i, carry, True)
        l0 = a0[:, HEAD_DIM:HEAD_DIM + 1]
        l1 = a1[:, 0:1]
        o_ref[...] = jnp.where(sels[0], a0 / l0, a1 / l1).astype(BF16)
        lse_ref[...] = jnp.where(sels[0], m0 + jnp.log(l0), m1 + jnp.log(l1))

    return pl.pallas_call(
        body, name="attn_fwd", grid=(npair, p // blk),
        in_specs=[pl.BlockSpec((blk, LANES), lambda h, i: (i, h)),
                  pl.BlockSpec((p, LANES), lambda h, i: (0, h)), pl.BlockSpec((p, LANES), lambda h, i: (0, h)),
                  pl.BlockSpec((1, nkb, 8, blk), lambda h, i: (h, 0, 0, 0))],
        out_specs=[pl.BlockSpec((blk, LANES), lambda h, i: (i, h)), pl.BlockSpec((blk, LANES), lambda h, i: (i, h))],
        out_shape=[jax.ShapeDtypeStruct((p, da), BF16), jax.ShapeDtypeStruct((p, da), F32)],
        compiler_params=_cparams(("parallel", "arbitrary")),
    )(q, k, v, ck)


def _attn_bwd(q, k, v, o, do, lse_rep, ck, blk):
    p, da = q.shape
    npair, nkb = ck.shape[0], ck.shape[1]
    nq = p // blk
    scale = 1.0 / math.sqrt(HEAD_DIM)

    def body(k_ref, v_ref, q_ref, do_ref, o_ref, lse_ref, ck_ref, dk_ref, dv_ref, dq_ref, dcs_ref, rsum_ref, dq_acc):
        jb = pl.program_id(1)

        @pl.when(jb == 0)
        def _():
            dq_acc[...] = jnp.zeros_like(dq_acc)

        ks = k_ref[...]
        vs = v_ref[...]
        lane = lax.broadcasted_iota(jnp.int32, (1, LANES), 1)
        sels = [lane < HEAD_DIM, lane >= HEAD_DIM]
        ones = [jnp.where(lane == HEAD_DIM, 1.0, 0.0).astype(BF16), jnp.where(lane == 0, 1.0, 0.0).astype(BF16)]
        kss = ks * scale
        kmo = [jnp.where(sels[j], kss, ones[j]) for j in range(2)]
        cmask = (lax.broadcasted_iota(jnp.int32, (blk, blk), 1) <= lax.broadcasted_iota(jnp.int32, (blk, blk), 0))

        def step(ib, carry, masked, nb=1):
            rows = nb * blk
            r0 = pl.multiple_of(ib * blk, blk)
            qb = q_ref[pl.ds(r0, rows), :] * scale
            dob = do_ref[pl.ds(r0, rows), :]
            prod = dob.astype(F32) * o_ref[pl.ds(r0, rows), :].astype(F32)
            out = []
            for j in range(2):
                dk, dv = carry[2 * j], carry[2 * j + 1]
                qm = jnp.where(sels[j], qb, jnp.zeros_like(qb))
                dom = jnp.where(sels[j], dob, jnp.zeros_like(dob))
                lse = lse_ref[pl.ds(r0, rows), HEAD_DIM * j:HEAD_DIM * j + 1]
                dlt = jnp.sum(jnp.where(sels[j], prod, 0.0), axis=-1, keepdims=True)
                s = _dot(qm, ks, NT) - ck_ref[0, 0, j:j + 1, :] - lse
                pm = jnp.exp(jnp.minimum(s, 0.0))
                if masked:
                    pm = jnp.where(cmask, pm, 0.0)
                ds_b = (pm * (_dot(dom, vs, NT) - dlt)).astype(BF16)
                dv = dv + _dot(pm.astype(BF16), dom, TN)
                dk = dk + _dot(ds_b, jnp.where(sels[j], qb, ones[j]), TN)
                dq_acc[pl.ds(r0, rows), LANES * j:LANES * (j + 1)] += _dot(ds_b, kmo[j])
                out += [dk, dv]
            return tuple(out)

        zero = jnp.zeros((blk, LANES), F32)
        carry = step(jb, (zero, zero, zero, zero), True)
        n4 = (nq - 1 - jb) // 4
        n2 = (nq - 1 - jb - 4 * n4) // 2
        carry = lax.fori_loop(0, n4, lambda t, c: step(jb + 1 + 4 * t, c, False, 4), carry)
        carry = lax.fori_loop(0, n2, lambda t, c: step(jb + 1 + 4 * n4 + 2 * t, c, False, 2), carry)
        dk0, dv0, dk1, dv1 = lax.fori_loop(jb + 1 + 4 * n4 + 2 * n2, nq, lambda ib, c: step(ib, c, False), carry)
        dk_ref[...] = jnp.where(sels[0], dk0, dk1).astype(BF16)
        dv_ref[...] = (dv0 + dv1).astype(BF16)
        pair8 = lambda c0, c1: jnp.where(lane == 0, c0, jnp.where(lane == 1, c1, 0.0)).T[0:8]
        dcs_ref[0] = pair8(dk0[:, HEAD_DIM:HEAD_DIM + 1], dk1[:, 0:1])

        @pl.when(jb == nkb - 1)
        def _():
            a0 = dq_acc[:, :LANES]
            a1 = dq_acc[:, LANES:]
            dq_ref[...] = jnp.where(sels[0], a0, a1).astype(BF16)
            rsum_ref[0] = pair8(a0[:, HEAD_DIM:HEAD_DIM + 1], a1[:, 0:1])

    colblk = pl.BlockSpec((blk, LANES), lambda h, j: (j, h))
    colfull = pl.BlockSpec((p, LANES), lambda h, j: (0, h))
    ckspec = pl.BlockSpec((1, 1, 8, blk), lambda h, j: (h, j, 0, 0))
    return pl.pallas_call(
        body, name="attn_bwd", grid=(npair, nkb),
        in_specs=[colblk, colblk, colfull, colfull, colfull, colfull, ckspec],
        out_specs=[colblk, colblk, colfull, pl.BlockSpec((1, 8, blk), lambda h, j: (h, 0, j)),
                   pl.BlockSpec((1, 8, p), lambda h, j: (h, 0, 0))],
        out_shape=[jax.ShapeDtypeStruct((p, da), BF16), jax.ShapeDtypeStruct((p, da), BF16),
                   jax.ShapeDtypeStruct((p, da), BF16), jax.ShapeDtypeStruct((npair, 8, p), F32),
                   jax.ShapeDtypeStruct((npair, 8, p), F32)],
        scratch_shapes=[pltpu.VMEM((p, 2 * LANES), F32)],
        compiler_params=_cparams(("parallel", "arbitrary")),
    )(k, v, q, do, o, lse_rep, ck)


def _rows3(i):
    return jnp.maximum(3 * i - 1, 0), 3 * i, 3 * i + 1


def _tail_fwd(yssd, o, zatt, graw, head, x2, tgt2, wps, wpa, wout, gate_bias, norm_post, tm):
    p, ds = yssd.shape
    da = o.shape[1]
    d = x2.shape[1]
    nsub = tm // CHUNK

    def body(yssd_ref, o_ref, zatt_ref, g_ref, head_ref, *rest):
        x_refs, t_refs = rest[:nsub], rest[nsub:2 * nsub]
        (wps_ref, wpa_ref, wout_ref, gb_ref, np_ref,
         yatt_ref, mrg_ref, a_ref, b_ref, dzo_ref, dout_ref, red_ref) = rest[2 * nsub:]
        i = pl.program_id(0)

        @pl.when(i == 0)
        def _():
            red_ref[...] = jnp.zeros_like(red_ref)

        first = jnp.where(i == 0, head_ref[...], x_refs[0][...])
        h = jnp.concatenate([first] + [r[...] for r in x_refs[1:]], axis=0)
        tgt = jnp.concatenate([r[...] for r in t_refs], axis=0)
        rows = lax.broadcasted_iota(jnp.int32, (tm, 1), 0)
        valid = jnp.where((i > 0) | (rows >= CHUNK), 1.0, 0.0)
        ob = o_ref[...].astype(F32)
        za = zatt_ref[...].astype(F32)
        yatt_b = (ob * za * _sigmoid(za)).astype(BF16)
        yatt_ref[...] = yatt_b
        a = _dot(yssd_ref[...], wps_ref[...])
        b = _dot(yatt_b, wpa_ref[...])
        a_ref[...] = a.astype(BF16)
        b_ref[...] = b.astype(BF16)
        gr = g_ref[...].astype(F32) + gb_ref[...]
        mrg_b = (_sigmoid(gr[:, :d]) * a + _sigmoid(gr[:, d:]) * b).astype(BF16)
        mrg_ref[...] = mrg_b
        zo = _dot(mrg_b, wout_ref[...])
        rstd = lax.rsqrt(jnp.mean(zo * zo, axis=-1, keepdims=True) + EPS)
        zh = zo * rstd
        npw = np_ref[...]
        err = (h + zh * npw - tgt) * valid
        dout = err * (1.0 / d)
        dout_ref[...] = dout
        dzh = dout * npw
        dzo_ref[...] = (rstd * (dzh - zh * jnp.mean(dzh * zh, axis=-1, keepdims=True))).astype(BF16)
        red_ref[0:1, :] += jnp.sum(dout * zh, axis=0, keepdims=True)
        red_ref[1:2, 0:1] += jnp.sum(jnp.sum(err * err, axis=1, keepdims=True), axis=0, keepdims=True) * (0.5 / d)

    row = lambda w: pl.BlockSpec((tm, w), lambda i: (i, 0))
    once = lambda shape: pl.BlockSpec(shape, lambda i: (0,) * len(shape), pipeline_mode=pl.Buffered(1))
    if nsub == 1:
        subs = [pl.BlockSpec((CHUNK, d), lambda i: (jnp.maximum(i - 1, 0), 0))]
    else:
        subs = [pl.BlockSpec((CHUNK, d), functools.partial(lambda i, k: (_rows3(i)[k], 0), k=k)) for k in range(3)]
    sd = jax.ShapeDtypeStruct
    return pl.pallas_call(
        body, name="tail_fwd", grid=(p // tm,),
        in_specs=[row(ds), row(da), row(da), row(2 * d), _full((CHUNK, d))] + subs + subs
                 + [once((ds, d)), once((da, d)), once((d, d)), _full((1, 2 * d)), _full((1, d))],
        out_specs=[row(da), row(d), row(d), row(d), row(d), row(d), _full((8, d))],
        out_shape=[sd((p, da), BF16), sd((p, d), BF16), sd((p, d), BF16), sd((p, d), BF16), sd((p, d), BF16),
                   sd((p, d), F32), sd((8, d), F32)],
        compiler_params=_cparams(("arbitrary",)),
    )(yssd, o, zatt, graw, head, *([x2] * nsub), *([tgt2] * nsub), wps, wpa, wout, gate_bias, norm_post)


def _tail_bwd(dzo, a_b, b_b, graw, o, zatt, wps, wpa, wout, gate_bias, tm):
    p, d = dzo.shape
    ds, da = wps.shape[0], wpa.shape[0]

    def body(dzo_ref, a_ref, b_ref, g_ref, o_ref, zatt_ref, wps_ref, wpa_ref, wout_ref, gb_ref,
             da_ref, db_ref, dg_ref, dyssd_ref, do_ref, dzatt_ref, red_ref):
        i = pl.program_id(0)

        @pl.when(i == 0)
        def _():
            red_ref[...] = jnp.zeros_like(red_ref)

        gr = g_ref[...].astype(F32) + gb_ref[...]
        gs = _sigmoid(gr[:, :d])
        ga = _sigmoid(gr[:, d:])
        dm = _dot(dzo_ref[...], wout_ref[...], NT)
        da_b = (gs * dm).astype(BF16)
        db_b = (ga * dm).astype(BF16)
        da_ref[...] = da_b
        db_ref[...] = db_b
        dgs = dm * a_ref[...].astype(F32) * gs * (1.0 - gs)
        dga = dm * b_ref[...].astype(F32) * ga * (1.0 - ga)
        dg_ref[:, :d] = dgs.astype(BF16)
        dg_ref[:, d:] = dga.astype(BF16)
        red_ref[0:1, :d] += jnp.sum(dgs, axis=0, keepdims=True)
        red_ref[0:1, d:] += jnp.sum(dga, axis=0, keepdims=True)
        dyssd_ref[...] = _dot(da_b, wps_ref[...], NT).astype(BF16)
        dya = _dot(db_b, wpa_ref[...], NT)
        ob = o_ref[...].astype(F32)
        za = zatt_ref[...].astype(F32)
        sza = _sigmoid(za)
        do_ref[...] = (dya * za * sza).astype(BF16)
        dzatt_ref[...] = (dya * ob * sza * (1.0 + za * (1.0 - sza))).astype(BF16)

    row = lambda w: pl.BlockSpec((tm, w), lambda i: (i, 0))
    once = lambda shape: pl.BlockSpec(shape, lambda i: (0,) * len(shape), pipeline_mode=pl.Buffered(1))
    sd = jax.ShapeDtypeStruct
    return pl.pallas_call(
        body, name="tail_bwd", grid=(p // tm,),
        in_specs=[row(d), row(d), row(d), row(2 * d), row(da), row(da),
                  once((ds, d)), once((da, d)), once((d, d)), _full((1, 2 * d))],
        out_specs=[row(d), row(d), row(2 * d), row(ds), row(da), row(da), _full((8, 2 * d))],
        out_shape=[sd((p, d), BF16), sd((p, d), BF16), sd((p, 2 * d), BF16), sd((p, ds), BF16), sd((p, da), BF16),
                   sd((p, da), BF16), sd((8, 2 * d), F32)],
        compiler_params=_cparams(("arbitrary",)),
    )(dzo, a_b, b_b, graw, o, zatt, wps, wpa, wout, gate_bias)


def _adamw_math(w, g, m, v):
    m2 = ADAM_B1 * m + (1.0 - ADAM_B1) * g
    v2 = ADAM_B2 * v + (1.0 - ADAM_B2) * (g * g)
    m_hat = m2 / (1.0 - ADAM_B1 ** ADAM_STEP)
    v_hat = v2 / (1.0 - ADAM_B2 ** ADAM_STEP)
    delta = -ADAM_LR * (m_hat / (jnp.sqrt(v_hat) + ADAM_EPS) + ADAM_WD * w)
    return delta, m2, v2


def _adamw_small(params, red, name):
    names = list(params)
    n = len(names)
    extra = [params[k][3] for k in names if not isinstance(params[k][3], tuple)]

    def body(*refs):
        w_refs, m_refs, v_refs = refs[:n], refs[n:2 * n], refs[2 * n:3 * n]
        red_ref = refs[3 * n]
        g_refs = iter(refs[3 * n + 1:3 * n + 1 + len(extra)])
        outs = refs[3 * n + 1 + len(extra):]
        for i, k in enumerate(names):
            where = params[k][3]
            rows, cols = w_refs[i].shape
            if isinstance(where, tuple):
                g = red_ref[where[0]:where[0] + rows, where[1]:where[1] + cols]
            else:
                g = next(g_refs)[...]
            delta, m2, v2 = _adamw_math(w_refs[i][...], g, m_refs[i][...], v_refs[i][...])
            for o, val in zip(outs[4 * i:4 * i + 4], (g, delta, m2, v2)):
                o[...] = val

    vm = pl.BlockSpec(memory_space=pltpu.VMEM)
    ws, ms, vs = ([params[k][j] for k in names] for j in range(3))
    out = pl.pallas_call(
        body, name=name,
        out_shape=[jax.ShapeDtypeStruct(w.shape, F32) for w in ws for _ in range(4)],
        in_specs=[vm] * (3 * n + 1 + len(extra)), out_specs=[vm] * (4 * n),
    )(*ws, *ms, *vs, red, *extra)
    return {k: tuple(out[4 * i:4 * i + 4]) for i, k in enumerate(names)}


def _adamw(w, g, m, v, name, parts=False, part_row0=0):
    r, cdim = w.shape
    tr, tc, by_rows = _tiles_2d(r, cdim)
    pick = (lambda i: (i, 0)) if by_rows else (lambda i: (0, i))
    assert part_row0 % tr == 0
    gpick = (lambda i: (i + part_row0 // tr, 0)) if by_rows else (lambda i: (part_row0 // tr, i))

    def body(w_ref, g_ref, m_ref, v_ref, go_ref, d_ref, mo_ref, vo_ref):
        if parts:
            g = g_ref[0].astype(F32)
            for s in range(1, g_ref.shape[0]):
                g = g + g_ref[s].astype(F32)
        else:
            g = g_ref[...]
        delta, m2, v2 = _adamw_math(w_ref[...], g, m_ref[...], v_ref[...])
        go_ref[...] = g
        d_ref[...] = delta
        mo_ref[...] = m2
        vo_ref[...] = v2

    blk = pl.BlockSpec((tr, tc), pick)
    gspec = pl.BlockSpec((g.shape[0], tr, tc), lambda i: (0,) + gpick(i)) if parts else blk
    return pl.pallas_call(
        body, name=name, grid=((r // tr) * (cdim // tc),),
        in_specs=[blk, gspec, blk, blk], out_specs=[blk] * 4,
        out_shape=[jax.ShapeDtypeStruct((r, cdim), F32)] * 4,
        compiler_params=_cparams(("parallel",)),
    )(w, g, m, v)


def _pad_cols(a, width):
    return jnp.pad(a, ((0, 0), (0, width - a.shape[1])))


def _pack_small_shard(conv_w_sh, meta_sh, width):
    return jnp.concatenate([_pad_cols(conv_w_sh, width), jnp.zeros((4, width), F32), _pad_cols(meta_sh, width)], axis=0)


def _pack_small_rep(norm_pre, norm_post, gate_bias, ssd_norm, conv_b, misc, width):
    rows = [norm_pre, norm_post, gate_bias, ssd_norm, conv_b, misc]
    return jnp.concatenate([_pad_cols(r, width) for r in rows] + [jnp.zeros((2, width), F32)], axis=0)


def kernel(x, meta_tokens, norm_pre, w_in, conv_w, conv_b, dt_bias, a_log, d_skip, ssd_norm, fgate_bias, gate_bias, w_proj_ssd, w_proj_att, w_out, norm_post, loss_target, m_meta_tokens, m_norm_pre, m_w_in, m_conv_w, m_conv_b, m_dt_bias, m_a_log, m_d_skip, m_ssd_norm, m_fgate_bias, m_gate_bias, m_w_proj_ssd, m_w_proj_att, m_w_out, m_norm_post, v_meta_tokens, v_norm_pre, v_w_in, v_conv_w, v_conv_b, v_dt_bias, v_a_log, v_d_skip, v_ssd_norm, v_fgate_bias, v_gate_bias, v_w_proj_ssd, v_w_proj_att, v_w_out, v_norm_post):
    seq, d = x.shape[1], x.shape[2]
    p = seq + CHUNK
    hs, ha = dt_bias.shape[1], fgate_bias.shape[1]
    ds, cd = ssd_norm.shape[1], conv_b.shape[1]
    da = ha * HEAD_DIM
    nc8 = w_in.shape[2]
    cws = cd // N_DEV
    msh = d // N_DEV
    r1, r2, r3 = ds // N_DEV, da // N_DEV, d // N_DEV
    me = _dev_index(*_my_pos())
    x2, tgt2 = x[0], loss_target[0]

    win_sh = jnp.transpose(w_in[0]).astype(BF16)
    rows_sh = jnp.concatenate([w_proj_ssd[0], w_proj_att[0], w_out[0]], axis=0).astype(BF16)
    small_sh = _pack_small_shard(conv_w[0], meta_tokens, cws)
    win_all, small_all = _all_gather([win_sh, small_sh], "gather_weights")
    rows_sh, win_all = lax.optimization_barrier((rows_sh, win_all))
    rows_sems, rows_thru, rows_land, rows_token = _bcast_start(rows_sh, "gather_rows_start")
    cuts = [0, ds, ds + cd, ds + cd + hs, ds + cd + hs + da, ds + cd + hs + 2 * da, ds + cd + hs + 3 * da,
            ds + cd + hs + 4 * da, ds + cd + hs + 4 * da + ha, ds + cd + hs + 4 * da + ha + 2 * d]

    def piece_rows(r0, r1):
        parts = [win_all[s, max(r0, s * nc8) - s * nc8:min(r1, (s + 1) * nc8) - s * nc8]
                 for s in range(N_DEV) if max(r0, s * nc8) < min(r1, (s + 1) * nc8)]
        return parts[0] if len(parts) == 1 else jnp.concatenate(parts, axis=0)

    w_z, w_xbc, w_dt, w_zatt, w_q, w_k, w_v, w_f, w_g = [piece_rows(cuts[i], cuts[i + 1]) for i in range(9)]
    w_dtf = jnp.concatenate([w_dt, w_f, jnp.zeros((LANES - hs - ha, d), BF16)], axis=0)
    conv_w_full = jnp.transpose(small_all[:, 0:CONV_K, :], (1, 0, 2)).reshape(CONV_K, cd)
    meta_full = jnp.transpose(small_all[:, 8:8 + N_META, :msh], (1, 0, 2)).reshape(N_META, d)
    head = jnp.concatenate([jnp.zeros((PADN, d), F32), meta_full + rows_token[0:1, 0:1]], axis=0)

    u = _prenorm_fwd(head, x2, norm_pre)
    tm = _att_block(p)
    seg_w = [w_z, w_xbc, w_zatt, w_q, w_k, w_v, w_g]
    zs, xbc, zatt, q, k, v, graw = [
        _mm(u, w, "nt", BF16, _tile(p, (1408, tm)), _tile(w.shape[0], (1024, 512, 256, 128)), "inproj_%d" % i)
        for i, w in enumerate(seg_w)]
    dtf = _mm(u, w_dtf, "nt", F32, _tile(p, (1408, tm)), LANES, "inproj_dtf")

    brow = jnp.concatenate([dt_bias, fgate_bias, jnp.zeros((1, LANES - hs - ha), F32)], axis=1)
    alog_row = _pad_cols(a_log, LANES)
    dskip_l = jnp.repeat(d_skip, HEAD_DIM, axis=1)
    sel_t = (lax.broadcasted_iota(jnp.int32, (LANES, ds), 1) // HEAD_DIM
             == lax.broadcasted_iota(jnp.int32, (LANES, ds), 0)).astype(BF16)
    sel = sel_t.T
    y, yssd, hin, cf, pre = _ssd_fwd(xbc, zs, dtf, conv_w_full, conv_b, brow, alog_row, dskip_l, ssd_norm, sel_t, hs, ha)

    blk = _att_block(p)
    nkb, npair = p // blk, ha // 2
    cum = jnp.where(lax.broadcasted_iota(jnp.int32, (p, 1), 0) < PADN, -NEG, cf[:, hs:hs + ha])
    ck = jnp.transpose(cum.T.reshape(npair, 2, nkb, blk), (0, 2, 1, 3))
    ck = jnp.pad(ck, ((0, 0), (0, 0), (0, 6), (0, 0)))
    o, lse_rep = _attn_fwd(q, k, v, ck, blk)

    rows_all = _bcast_wait(rows_sems, rows_thru, rows_land, lse_rep, "gather_rows_wait")
    wps = rows_all[:, :r1].reshape(ds, d)
    wpa = rows_all[:, r1:r1 + r2].reshape(da, d)
    wout = rows_all[:, r1 + r2:].reshape(d, d)

    yatt, mrg, a_b, b_b, dzo, dout, red_fwd = _tail_fwd(
        yssd, o, zatt, graw, head, x2, tgt2, wps, wpa, wout, gate_bias, norm_post, tm)
    da_, db_, dgraw, dyssd, d_o, dzatt, red_bwd = _tail_bwd(dzo, a_b, b_b, graw, o, zatt, wps, wpa, wout, gate_bias, tm)

    tw = _tile(d, (512, 256, 128))
    g_wout = _mm(mrg, dzo, "tn", BF16, tw, tw, "wgrad_out")
    g_wps = _mm(yssd, da_, "tn", BF16, _tile(ds, (512, 256, 128)), tw, "wgrad_ps")
    g_wpa = _mm(yatt, db_, "tn", BF16, _tile(da, (512, 256, 128)), tw, "wgrad_pa")

    dk, dv, dq, dcs, rsum = _attn_bwd(q, k, v, o, d_o, lse_rep, ck, blk)
    dcum = (rsum - dcs)[:, 0:2, :].reshape(ha, p).T
    dcf = jnp.pad(dcum, ((0, 0), (hs, LANES - hs - ha)))
    dxbc, dzs, ddtf, gcw, gcb, gnrm, gsm = _ssd_bwd(
        dyssd, y, zs, xbc, pre, dtf, hin, dcf, conv_w_full, brow, alog_row, dskip_l, ssd_norm, sel_t, sel, hs, ha)
    ddtf_b = ddtf.astype(BF16)

    dsegs = [dzs, dxbc, dzatt, dq, dk, dv, dgraw, ddtf_b]
    gsegs = [_mm(dsg, u, "tn", BF16, _tile(dsg.shape[1], (512, 256, 128)), tw, "wgrad_in_%d" % i)
             for i, dsg in enumerate(dsegs)]
    g_z, g_xbc, g_zatt, g_q, g_k, g_v, g_g, g_dtf = gsegs
    gw_full = jnp.concatenate([g_z, g_xbc, g_dtf[:hs], g_zatt, g_q, g_k, g_v, g_dtf[hs:hs + ha], g_g], axis=0)
    gwin_parts = gw_full.reshape(N_DEV, nc8, d)
    grows_parts = jnp.concatenate([g_wps.reshape(N_DEV, r1, d), g_wpa.reshape(N_DEV, r2, d),
                                   g_wout.reshape(N_DEV, r3, d)], axis=1)

    core = lax.axis_index("c").astype(jnp.int32).reshape(1)
    sib_win, sib_rows = _exchange_sibling([gwin_parts, grows_parts], "scatter_grads_sibling")
    chip_win = _pair_add(gwin_parts, sib_win, core, "pair_add_w_in")
    chip_rows = _pair_add(grows_parts, sib_rows, core, "pair_add_rows")
    sems, thru, lands, token = _exchange_chips_start([chip_win, chip_rows], "scatter_grads_start")
    dsegs_after = dsegs[:-1] + [ddtf_b + token[0:1, 0:1].astype(BF16)]
    du = _mm_sum_nn(dsegs_after, seg_w + [w_dtf], tm, _tile(d, (512, 256, 128)), "dgrad_in")
    gx, ghead, gnp = _prenorm_bwd(head, x2, norm_pre, du, dout)
    sent, got = _exchange_chips_wait(sems, thru, lands, gnp, "scatter_grads_wait")
    chip = me // 2
    recv_win, recv_rows = [lax.dynamic_update_slice_in_dim(g, lax.dynamic_slice_in_dim(s, chip, 1, axis=0), chip, axis=0)
                           for g, s in zip(got, sent)]
    gmisc = jnp.concatenate([gsm[0:1], gsm[1:2], gsm[2:3], _pad_cols(red_fwd[1:2, 0:1], LANES)], axis=1)
    small_g = jnp.concatenate([
        _pack_small_rep(gnp[0:1], red_fwd[0:1], red_bwd[0:1], gnrm[0:1], gcb[0:1], gmisc, cd),
        _pad_cols(gcw[0:CONV_K], cd), jnp.zeros((4, cd), F32), _pad_cols(ghead[PADN:], cd)], axis=0)
    sg_sems, sg_thru, sg_land, sg_token = _bcast_start(small_g, "reduce_small_start")

    upd_in = _adamw(jnp.transpose(w_in[0]) + sg_token[0:1, 0:1], recv_win, jnp.transpose(m_w_in[0]),
                    jnp.transpose(v_w_in[0]), "adamw_w_in", parts=True)
    upd_ps = _adamw(w_proj_ssd[0] + sg_token[0:1, 0:1], recv_rows, m_w_proj_ssd[0], v_w_proj_ssd[0],
                    "adamw_w_proj_ssd", parts=True, part_row0=0)
    upd_pa = _adamw(w_proj_att[0], recv_rows, m_w_proj_att[0], v_w_proj_att[0], "adamw_w_proj_att", parts=True,
                    part_row0=r1)
    upd_out = _adamw(w_out[0], recv_rows, m_w_out[0], v_w_out[0], "adamw_w_out", parts=True, part_row0=r1 + r2)
    all_done = upd_in[1][0:8, 0:LANES] + upd_ps[1][0:8, 0:LANES] + upd_pa[1][0:8, 0:LANES] + upd_out[1][0:8, 0:LANES]
    red = _sum_slots(_bcast_wait(sg_sems, sg_thru, sg_land, all_done, "reduce_small_wait"), "reduce_small_sum")
    loss = red[5, 3 * LANES]
    g_conv_w = lax.dynamic_slice_in_dim(red[8:8 + CONV_K], me * cws, cws, axis=1)
    g_meta = lax.dynamic_slice_in_dim(red[16:16 + N_META, :d], me * msh, msh, axis=1)
    small = {
        "meta_tokens": (meta_tokens, m_meta_tokens, v_meta_tokens, g_meta),
        "norm_pre": (norm_pre, m_norm_pre, v_norm_pre, (0, 0)),
        "conv_w": (conv_w[0], m_conv_w[0], v_conv_w[0], g_conv_w),
        "conv_b": (conv_b, m_conv_b, v_conv_b, (4, 0)),
        "dt_bias": (dt_bias, m_dt_bias, v_dt_bias, (5, 0)),
        "a_log": (a_log, m_a_log, v_a_log, (5, LANES)),
        "d_skip": (d_skip, m_d_skip, v_d_skip, (5, 2 * LANES)),
        "ssd_norm": (ssd_norm, m_ssd_norm, v_ssd_norm, (3, 0)),
        "fgate_bias": (fgate_bias, m_fgate_bias, v_fgate_bias, (5, hs)),
        "gate_bias": (gate_bias, m_gate_bias, v_gate_bias, (2, 0)),
        "norm_post": (norm_post, m_norm_post, v_norm_post, (1, 0)),
    }
    upd_small = _adamw_small(small, red, "adamw_small")

    def leaves(i):
        sm = {k: v[i] for k, v in upd_small.items()}
        return [sm["meta_tokens"], sm["norm_pre"], jnp.transpose(upd_in[i])[None], sm["conv_w"][None], sm["conv_b"],
                sm["dt_bias"], sm["a_log"], sm["d_skip"], sm["ssd_norm"], sm["fgate_bias"], sm["gate_bias"],
                upd_ps[i][None], upd_pa[i][None], upd_out[i][None], sm["norm_post"]]

    return tuple([loss, gx[None]] + leaves(0) + leaves(1) + leaves(2) + leaves(3))
```

```python
import functools
import math

import jax
import jax.numpy as jnp
from jax import lax
from jax.experimental import pallas as pl
from jax.experimental.pallas import tpu as pltpu

F32 = jnp.float32
BF16 = jnp.bfloat16

N_DEV = 8
N_META = 16
CHUNK = 128
PADN = CHUNK - N_META
HEAD_DIM = 64
SSD_GROUPS = 4
CONV_K = 4
EPS = 1e-6
NEG = -1e30
LANES = 128
HALO = 16

ADAM_LR = 0.001
ADAM_B1 = 0.9
ADAM_B2 = 0.999
ADAM_EPS = 1e-08
ADAM_WD = 0.01
ADAM_STEP = 10

VMEM_LIMIT = 56 * 1024 * 1024

NN = (((1,), (0,)), ((), ()))
NT = (((1,), (1,)), ((), ()))
TN = (((0,), (0,)), ((), ()))
MESH = pl.DeviceIdType.MESH


def _dot(a, b, dims=NN):
    return lax.dot_general(a, b, dims, preferred_element_type=F32)


def _split2(x):
    hi = x.astype(BF16)
    lo = (x - hi.astype(F32)).astype(BF16)
    return hi, lo


def _dot_sel(x, sel):
    hi, lo = _split2(x)
    return _dot(hi, sel) + _dot(lo, sel)


def _dot_tri(tri, x):
    h1 = x.astype(BF16)
    r1 = x - h1.astype(F32)
    h2 = r1.astype(BF16)
    h3 = (r1 - h2.astype(F32)).astype(BF16)
    return _dot(tri, h1) + _dot(tri, h2) + _dot(tri, h3)


def _sigmoid(x):
    return 1.0 / (1.0 + jnp.exp(-x))


def _softplus(x):
    return jnp.maximum(x, 0.0) + jnp.log(1.0 + jnp.exp(-jnp.abs(x)))


def _cparams(sem=None, vmem=VMEM_LIMIT):
    kw = {"vmem_limit_bytes": vmem}
    if sem is not None:
        kw["dimension_semantics"] = sem
    return pltpu.CompilerParams(**kw)


def _full(shape):
    nd = len(shape)
    return pl.BlockSpec(shape, lambda *_: (0,) * nd)


def _att_block(p):
    return 384 if p % 384 == 0 else CHUNK


def _my_pos():
    return lax.axis_index("x"), lax.axis_index("y"), lax.axis_index("c")


def _dev_index(x, y, c):
    return 4 * x + 2 * y + c


FLIPS = [(fx, fy, fc) for fx in (0, 1) for fy in (0, 1) for fc in (0, 1)][1:]


def _flip(pos, f):
    return tuple((1 - p) if fi else p for p, fi in zip(pos, f))


def _all_gather(bufs, name):
    nb = len(bufs)

    def body(*refs):
        ins, outs = refs[:nb], refs[nb:2 * nb]
        send_sems, recv_sems, local_sems = refs[2 * nb:]
        x, y, c = _my_pos()
        me = _dev_index(x, y, c)
        sibling = (x, y, 1 - c)
        chips = [(1 - x, y), (x, 1 - y), (1 - x, 1 - y)]

        def copy(b, k, block_idx, to, src=None):
            dst = outs[b].at[block_idx]
            return pltpu.make_async_remote_copy(
                src_ref=dst if src is None else src, dst_ref=dst,
                send_sem=send_sems.at[b, k], recv_sem=recv_sems.at[b, k],
                device_id=to, device_id_type=MESH)

        started = []
        for b in range(nb):
            mine = pltpu.make_async_copy(ins[b], outs[b].at[me], local_sems.at[b])
            mine.start()
            started.append(mine)
        first = []
        for b in range(nb):
            first.append(copy(b, 0, me, sibling, src=ins[b]))
            for j, chip in enumerate(chips):
                first.append(copy(b, 1 + j, me, (chip[0], chip[1], c), src=ins[b]))
        for cp in first:
            cp.start()
        passed = []
        for j, chip in enumerate(chips):
            blk = _dev_index(chip[0], chip[1], c)
            for b in range(nb):
                copy(b, 1 + j, blk, (x, y, c)).wait_recv()
                fwd = copy(b, 4 + j, blk, sibling)
                fwd.start()
                passed.append(fwd)
        for b in range(nb):
            copy(b, 0, _dev_index(x, y, 1 - c), (x, y, c)).wait_recv()
        for j, chip in enumerate(chips):
            blk = _dev_index(chip[0], chip[1], 1 - c)
            for b in range(nb):
                copy(b, 4 + j, blk, (x, y, c)).wait_recv()
        for cp in first + passed:
            cp.wait_send()
        for mine in started:
            mine.wait()

    any_spec = pl.BlockSpec(memory_space=pl.ANY)
    return pl.pallas_call(
        body, name=name,
        out_shape=[jax.ShapeDtypeStruct((N_DEV,) + b.shape, b.dtype) for b in bufs],
        in_specs=[any_spec] * nb, out_specs=[any_spec] * nb,
        scratch_shapes=[pltpu.SemaphoreType.DMA((nb, 7)), pltpu.SemaphoreType.DMA((nb, 7)),
                        pltpu.SemaphoreType.DMA((nb,))],
    )(*bufs)


N_CHIP = 4
CHIP_FLIPS = [(1, 0), (0, 1), (1, 1)]


def _exchange_sibling(bufs, name):
    nb = len(bufs)

    def body(*refs):
        ins, outs = refs[:nb], refs[nb:2 * nb]
        send_sems, recv_sems = refs[2 * nb:]
        x, y, c = _my_pos()

        def copy(b, k):
            return pltpu.make_async_remote_copy(
                src_ref=ins[b].at[2 * k + (1 - c)], dst_ref=outs[b].at[k],
                send_sem=send_sems.at[b, k], recv_sem=recv_sems.at[b, k],
                device_id=(x, y, 1 - c), device_id_type=MESH)

        cps = [copy(b, k) for b in range(nb) for k in range(N_CHIP)]
        for cp in cps:
            cp.start()
        for cp in cps:
            cp.wait()

    any_spec = pl.BlockSpec(memory_space=pl.ANY)
    return pl.pallas_call(
        body, name=name,
        out_shape=[jax.ShapeDtypeStruct((N_CHIP,) + b.shape[1:], b.dtype) for b in bufs],
        in_specs=[any_spec] * nb, out_specs=[any_spec] * nb,
        scratch_shapes=[pltpu.SemaphoreType.DMA((nb, N_CHIP)), pltpu.SemaphoreType.DMA((nb, N_CHIP))],
    )(*bufs)


def _pair_add(mine, recv, core, name):
    _, r, cdim = mine.shape
    tr, tc, by_rows = _tiles_2d(r, cdim)
    pick = (lambda i: (i, 0)) if by_rows else (lambda i: (0, i))

    def body(core_ref, a_ref, b_ref, o_ref):
        o_ref[0] = (a_ref[0].astype(F32) + b_ref[0].astype(F32)).astype(o_ref.dtype)

    return pl.pallas_call(
        body, name=name,
        grid_spec=pltpu.PrefetchScalarGridSpec(
            num_scalar_prefetch=1, grid=(N_CHIP, (r // tr) * (cdim // tc)),
            in_specs=[pl.BlockSpec((1, tr, tc), lambda k, i, core_ref: (2 * k + core_ref[0],) + pick(i)),
                      pl.BlockSpec((1, tr, tc), lambda k, i, core_ref: (k,) + pick(i))],
            out_specs=pl.BlockSpec((1, tr, tc), lambda k, i, core_ref: (k,) + pick(i))),
        out_shape=jax.ShapeDtypeStruct((N_CHIP, r, cdim), mine.dtype),
        compiler_params=_cparams(("parallel", "parallel")),
    )(core, mine, recv)


def _chip_peer(x, y, f):
    return ((1 - x) if f[0] else x), ((1 - y) if f[1] else y)


def _exchange_chips_start(bufs, name):
    nb = len(bufs)
    nsem = 2 * 3 * nb

    def body(*refs):
        ins, lands = refs[:nb], refs[nb:2 * nb]
        sems = refs[2 * nb:2 * nb + nsem]
        token = refs[-1]
        x, y, c = _my_pos()
        for b in range(nb):
            for j, f in enumerate(CHIP_FLIPS):
                px, py = _chip_peer(x, y, f)
                pltpu.make_async_remote_copy(
                    src_ref=ins[b].at[2 * px + py], dst_ref=lands[b].at[2 * x + y],
                    send_sem=sems[2 * (3 * b + j)], recv_sem=sems[2 * (3 * b + j) + 1],
                    device_id=(px, py, c), device_id_type=MESH).start()
        token[...] = jnp.zeros_like(token)

    hbm = pl.BlockSpec(memory_space=pltpu.HBM)
    sem = pl.BlockSpec(memory_space=pltpu.SEMAPHORE)
    out = pl.pallas_call(
        body, name=name,
        out_shape=(*([pltpu.SemaphoreType.DMA(())] * nsem),
                   *[pltpu.HBM(b.shape, b.dtype) for b in bufs], *[pltpu.HBM(b.shape, b.dtype) for b in bufs],
                   jax.ShapeDtypeStruct((8, LANES), F32)),
        in_specs=[hbm] * (2 * nb),
        out_specs=(*([sem] * nsem), *([hbm] * (2 * nb)), pl.BlockSpec(memory_space=pltpu.VMEM)),
        input_output_aliases={i: nsem + i for i in range(2 * nb)},
        compiler_params=pltpu.CompilerParams(has_side_effects=pltpu.SideEffectType.DATAFLOW_SIDE_EFFECTING),
    )(*[pltpu.with_memory_space_constraint(b, pltpu.HBM) for b in bufs],
      *[pltpu.with_memory_space_constraint(lax.empty(b.shape, b.dtype), pltpu.HBM) for b in bufs])
    return out[:nsem], out[nsem:nsem + nb], out[nsem + nb:nsem + 2 * nb], out[-1]


def _exchange_chips_wait(sems, thru, lands, after, name):
    nb = len(thru)
    nsem = len(sems)

    def body(*refs):
        ins, lnd = refs[:nb], refs[nb:2 * nb]
        sem_refs = refs[2 * nb:2 * nb + nsem]
        x, y, c = _my_pos()
        for b in range(nb):
            for j, f in enumerate(CHIP_FLIPS):
                px, py = _chip_peer(x, y, f)
                cp = pltpu.make_async_remote_copy(
                    src_ref=ins[b].at[2 * px + py], dst_ref=lnd[b].at[2 * px + py],
                    send_sem=sem_refs[2 * (3 * b + j)], recv_sem=sem_refs[2 * (3 * b + j) + 1],
                    device_id=(px, py, c), device_id_type=MESH)
                cp.wait_send()
                cp.wait_recv()

    hbm = pl.BlockSpec(memory_space=pltpu.HBM)
    sem = pl.BlockSpec(memory_space=pltpu.SEMAPHORE)
    out = pl.pallas_call(
        body, name=name,
        out_shape=tuple([pltpu.HBM(b.shape, b.dtype) for b in thru] + [pltpu.HBM(b.shape, b.dtype) for b in lands]),
        in_specs=[hbm] * (2 * nb) + [sem] * nsem + [pl.BlockSpec(memory_space=pl.ANY)],
        out_specs=tuple([hbm] * (2 * nb)),
        input_output_aliases={i: i for i in range(2 * nb)},
        compiler_params=pltpu.CompilerParams(has_side_effects=pltpu.SideEffectType.DATAFLOW_SIDE_EFFECTING),
    )(*thru, *lands, *sems, after)
    return out[:nb], out[nb:]


def _bcast_start(buf, name):
    nsem = 2 * len(FLIPS)

    def body(src, land, *rest):
        sems, token = rest[:nsem], rest[-1]
        pos = _my_pos()
        for k, f in enumerate(FLIPS):
            pltpu.make_async_remote_copy(
                src_ref=src, dst_ref=land.at[_dev_index(*pos)], send_sem=sems[2 * k], recv_sem=sems[2 * k + 1],
                device_id=_flip(pos, f), device_id_type=MESH).start()
        token[...] = jnp.zeros_like(token)

    hbm = pl.BlockSpec(memory_space=pltpu.HBM)
    sem = pl.BlockSpec(memory_space=pltpu.SEMAPHORE)
    land_shape = (N_DEV,) + buf.shape
    out = pl.pallas_call(
        body, name=name,
        out_shape=(*([pltpu.SemaphoreType.DMA(())] * nsem), pltpu.HBM(buf.shape, buf.dtype),
                   pltpu.HBM(land_shape, buf.dtype), jax.ShapeDtypeStruct((8, LANES), F32)),
        in_specs=[hbm, hbm],
        out_specs=(*([sem] * nsem), hbm, hbm, pl.BlockSpec(memory_space=pltpu.VMEM)),
        input_output_aliases={0: nsem, 1: nsem + 1},
        compiler_params=pltpu.CompilerParams(has_side_effects=pltpu.SideEffectType.DATAFLOW_SIDE_EFFECTING),
    )(pltpu.with_memory_space_constraint(buf, pltpu.HBM),
      pltpu.with_memory_space_constraint(lax.empty(land_shape, buf.dtype), pltpu.HBM))
    return out[:nsem], out[nsem], out[nsem + 1], out[-1]


def _bcast_wait(sems, thru, land, after, name):
    nsem = len(sems)

    def body(src, lnd, *rest):
        sem_refs = rest[:nsem]
        pos = _my_pos()
        for k, f in enumerate(FLIPS):
            peer = _flip(pos, f)
            cp = pltpu.make_async_remote_copy(
                src_ref=src, dst_ref=lnd.at[_dev_index(*peer)], send_sem=sem_refs[2 * k],
                recv_sem=sem_refs[2 * k + 1], device_id=peer, device_id_type=MESH)
            cp.wait_send()
            cp.wait_recv()

    hbm = pl.BlockSpec(memory_space=pltpu.HBM)
    sem = pl.BlockSpec(memory_space=pltpu.SEMAPHORE)
    sent, got = pl.pallas_call(
        body, name=name,
        out_shape=(pltpu.HBM(thru.shape, thru.dtype), pltpu.HBM(land.shape, land.dtype)),
        in_specs=[hbm, hbm] + [sem] * nsem + [pl.BlockSpec(memory_space=pl.ANY)],
        out_specs=(hbm, hbm), input_output_aliases={0: 0, 1: 1},
        compiler_params=pltpu.CompilerParams(has_side_effects=pltpu.SideEffectType.DATAFLOW_SIDE_EFFECTING),
    )(thru, land, *sems, after)
    return lax.dynamic_update_slice_in_dim(got, sent[None], _dev_index(*_my_pos()), axis=0)


def _sum_slots(v, name):
    _, r, cdim = v.shape

    def body(v_ref, o_ref):
        acc = v_ref[0]
        for s in range(1, N_DEV):
            acc = acc + v_ref[s]
        o_ref[...] = acc

    return pl.pallas_call(
        body, name=name, out_shape=jax.ShapeDtypeStruct((r, cdim), F32),
        in_specs=[_full((N_DEV, r, cdim))], out_specs=_full((r, cdim)), grid=(1,),
        compiler_params=_cparams(("arbitrary",)),
    )(v)


def _mm(a, b, dims, out_dtype, tm, tn, name):
    if dims == "nn":
        (m, k), (_, n) = a.shape, b.shape
        a_spec = pl.BlockSpec((tm, k), lambda j, i: (i, 0))
        b_spec = pl.BlockSpec((k, tn), lambda j, i: (0, j))
        dn = NN
    elif dims == "nt":
        (m, k), (n, _) = a.shape, b.shape
        a_spec = pl.BlockSpec((tm, k), lambda j, i: (i, 0))
        b_spec = pl.BlockSpec((tn, k), lambda j, i: (j, 0))
        dn = NT
    else:
        (k, m), (_, n) = a.shape, b.shape
        a_spec = pl.BlockSpec((k, tm), lambda j, i: (0, i))
        b_spec = pl.BlockSpec((k, tn), lambda j, i: (0, j))
        dn = TN
    assert m % tm == 0 and n % tn == 0, (m, tm, n, tn)

    def body(a_ref, b_ref, o_ref):
        o_ref[...] = _dot(a_ref[...], b_ref[...], dn).astype(o_ref.dtype)

    return pl.pallas_call(
        body, name=name, grid=(n // tn, m // tm),
        in_specs=[a_spec, b_spec], out_specs=pl.BlockSpec((tm, tn), lambda j, i: (i, j)),
        out_shape=jax.ShapeDtypeStruct((m, n), out_dtype),
        compiler_params=_cparams(("parallel", "parallel")),
    )(a, b)


def _tiles_2d(r, cdim):
    if r % CHUNK == 0:
        return CHUNK, cdim, True
    return r, _tile(cdim, (256, 128)), False


def _mm_sum_nn(a_list, b_list, tm, tn, name):
    n_op = len(a_list)
    m, n = a_list[0].shape[0], b_list[0].shape[1]

    def body(*refs):
        acc = _dot(refs[0][...], refs[n_op][...])
        for i in range(1, n_op):
            acc = acc + _dot(refs[i][...], refs[n_op + i][...])
        refs[2 * n_op][...] = acc

    return pl.pallas_call(
        body, name=name, grid=(n // tn, m // tm),
        in_specs=([pl.BlockSpec((tm, a.shape[1]), lambda j, i: (i, 0)) for a in a_list]
                  + [pl.BlockSpec((b.shape[0], tn), lambda j, i: (0, j)) for b in b_list]),
        out_specs=pl.BlockSpec((tm, tn), lambda j, i: (i, j)),
        out_shape=jax.ShapeDtypeStruct((m, n), F32),
        compiler_params=_cparams(("parallel", "parallel")),
    )(*a_list, *b_list)


def _tile(n, prefs):
    for t in prefs:
        if n % t == 0:
            return t
    return n


def _prenorm_fwd(head, x2, w):
    p, d = x2.shape[0] + CHUNK, x2.shape[1]

    def body(head_ref, x_ref, w_ref, u_ref):
        i = pl.program_id(0)
        h = jnp.where(i == 0, head_ref[...], x_ref[...])
        ms = jnp.mean(h * h, axis=-1, keepdims=True)
        u_ref[...] = (h * lax.rsqrt(ms + EPS) * w_ref[...]).astype(BF16)

    return pl.pallas_call(
        body, name="prenorm_fwd", grid=(p // CHUNK,),
        in_specs=[_full((CHUNK, d)), pl.BlockSpec((CHUNK, d), lambda i: (jnp.maximum(i - 1, 0), 0)), _full((1, d))],
        out_specs=pl.BlockSpec((CHUNK, d), lambda i: (i, 0)),
        out_shape=jax.ShapeDtypeStruct((p, d), BF16),
        compiler_params=_cparams(("arbitrary",)),
    )(head, x2, w)


def _prenorm_bwd(head, x2, w, du, dout):
    p, d = x2.shape[0] + CHUNK, x2.shape[1]

    def body(head_ref, x_ref, w_ref, du_ref, dout_ref, gx_ref, ghead_ref, gw_ref):
        i = pl.program_id(0)
        h = jnp.where(i == 0, head_ref[...], x_ref[...])
        rstd = lax.rsqrt(jnp.mean(h * h, axis=-1, keepdims=True) + EPS)
        xhat = h * rstd
        dub = du_ref[...]
        dxh = dub * w_ref[...]
        dh = rstd * (dxh - xhat * jnp.mean(dxh * xhat, axis=-1, keepdims=True)) + dout_ref[...]

        @pl.when(i == 0)
        def _():
            ghead_ref[...] = dh
            gw_ref[...] = jnp.zeros_like(gw_ref)

        gx_ref[...] = dh
        gw_ref[0:1, :] += jnp.sum(dub * xhat, axis=0, keepdims=True)

    return pl.pallas_call(
        body, name="prenorm_bwd", grid=(p // CHUNK,),
        in_specs=[_full((CHUNK, d)), pl.BlockSpec((CHUNK, d), lambda i: (jnp.maximum(i - 1, 0), 0)), _full((1, d)),
                  pl.BlockSpec((CHUNK, d), lambda i: (i, 0)), pl.BlockSpec((CHUNK, d), lambda i: (i, 0))],
        out_specs=[pl.BlockSpec((CHUNK, d), lambda i: (jnp.maximum(i - 1, 0), 0)), _full((CHUNK, d)), _full((8, d))],
        out_shape=[jax.ShapeDtypeStruct(x2.shape, F32), jax.ShapeDtypeStruct((CHUNK, d), F32),
                   jax.ShapeDtypeStruct((8, d), F32)],
        compiler_params=_cparams(("arbitrary",)),
    )(head, x2, w, du, dout)


def _conv_pre(ext_ref, cw_ref, cb_ref):
    pre = cb_ref[...] + cw_ref[CONV_K - 1:CONV_K, :] * ext_ref[8:8 + CHUNK, :]
    for j in range(1, CONV_K):
        pre = pre + cw_ref[CONV_K - 1 - j:CONV_K - j, :] * ext_ref[8 - j:8 - j + CHUNK, :]
    return pre


def _ssd_scalars(dtf_ref, brow_ref, alog_ref, rowmask, hs, ha, tri):
    lane = lax.broadcasted_iota(jnp.int32, (1, LANES), 1)
    is_dt = lane < hs
    is_f = (lane >= hs) & (lane < hs + ha)
    dtr = dtf_ref[...] + brow_ref[...]
    sp = _softplus(dtr)
    dt = jnp.where(is_dt, sp, 0.0) * rowmask
    logf = jnp.where(is_f, jnp.minimum(dtr, 0.0) - jnp.log(1.0 + jnp.exp(-jnp.abs(dtr))), 0.0) * rowmask
    a_row = jnp.where(is_dt, -jnp.exp(alog_ref[...]), 0.0)
    run = _dot_tri(tri, dt * a_row + logf)
    return dtr, dt, a_row, run, is_dt, is_f


def _tri_mats():
    r = lax.broadcasted_iota(jnp.int32, (CHUNK, CHUNK), 0)
    c = lax.broadcasted_iota(jnp.int32, (CHUNK, CHUNK), 1)
    return r, c


def _ssd_fwd(xbc, z, dtf, conv_w, conv_b, brow, alog, dskip_l, ssd_norm, sel_t, hs, ha):
    p, cd = xbc.shape
    ds = z.shape[1]
    ns = (cd - ds) // (2 * SSD_GROUPS)
    gw = ds // SSD_GROUPS
    nch = p // CHUNK
    hpg = hs // SSD_GROUPS

    def body(xbc_ref, halo_ref, z_ref, dtf_ref, cw_ref, cb_ref, brow_ref, alog_ref, dsk_ref, nrm_ref, selt_ref,
             y_ref, yssd_ref, hin_ref, cf_ref, pre_ref, st_ref, carry_ref, yacc_ref, xc_s, ex_s, xdtb_s, xwb_s, ext_s):
        c = pl.program_id(0)

        @pl.when(c == 0)
        def _():
            st_ref[...] = jnp.zeros_like(st_ref)
            carry_ref[...] = jnp.zeros_like(carry_ref)

        rows = lax.broadcasted_iota(jnp.int32, (CHUNK, 1), 0)
        rowmask = jnp.where((rows >= PADN) | (c > 0), 1.0, 0.0)
        ri, ci = _tri_mats()
        causal = ri >= ci
        tri = jnp.where(causal, 1.0, 0.0).astype(BF16)

        ext_s[0:8, :] = halo_ref[...].astype(F32)[HALO - 8:, :] * jnp.where(c > 0, 1.0, 0.0)
        ext_s[8:, :] = xbc_ref[...].astype(F32)
        pre = _conv_pre(ext_s, cw_ref, cb_ref)
        pre_ref[...] = pre.astype(BF16)
        xc_s[...] = pre * _sigmoid(pre) * rowmask

        dtr, dt, a_row, run, is_dt, is_f = _ssd_scalars(dtf_ref, brow_ref, alog_ref, rowmask, hs, ha, tri)
        cf = run + carry_ref[...]
        cf_ref[...] = cf
        carry_ref[...] = jnp.where(is_f, cf[CHUNK - 1:CHUNK, :], 0.0)
        cs = jnp.where(is_dt, run, 0.0)
        cl = cs[CHUNK - 1:CHUNK, :]
        selt = selt_ref[...]
        ex_s[...] = _dot_sel(jnp.exp(cs), selt)
        cdec_x = _dot_sel(jnp.broadcast_to(jnp.exp(cl), (8, LANES)), selt)[0:1, :]
        cs_t = cs.T
        xdt = xc_s[:, :ds] * _dot_sel(dt, selt)
        xdtb_s[...] = xdt.astype(BF16)
        xwb_s[...] = (xdt * _dot_sel(jnp.exp(cl - cs), selt)).astype(BF16)

        lane = lax.broadcasted_iota(jnp.int32, (1, LANES), 1)
        half0 = lane < HEAD_DIM
        for g in range(SSD_GROUPS):
            bg = xc_s[:, ds + g * ns: ds + (g + 1) * ns].astype(BF16)
            cg = xc_s[:, ds + SSD_GROUPS * ns + g * ns: ds + SSD_GROUPS * ns + (g + 1) * ns].astype(BF16)
            gm = _dot(cg, bg, NT)
            gs = slice(g * gw, (g + 1) * gw)
            stg = st_ref[:, gs]
            stg_b = stg.astype(BF16)
            hin_ref[0, :, gs] = stg_b
            yoff = _dot(cg, stg_b) * ex_s[:, gs]
            for pr in range(gw // LANES):
                sl = slice(g * gw + pr * LANES, g * gw + (pr + 1) * LANES)
                xp = xdtb_s[:, sl]
                yd = jnp.zeros((CHUNK, LANES), F32)
                for j in range(2):
                    h = g * hpg + 2 * pr + j
                    seg = cs[:, h:h + 1] - cs_t[h:h + 1, :]
                    m = jnp.where(causal, gm * jnp.exp(jnp.minimum(seg, 0.0)), 0.0).astype(BF16)
                    sel = half0 if j == 0 else jnp.logical_not(half0)
                    yd = yd + _dot(m, jnp.where(sel, xp, jnp.zeros_like(xp)))
                yacc_ref[:, sl] = yd + yoff[:, pr * LANES:(pr + 1) * LANES] + dsk_ref[:, sl] * xc_s[:, sl]
            st_ref[:, gs] = stg * cdec_x[:, gs] + _dot(bg, xwb_s[:, gs], TN)

        y = yacc_ref[...]
        y_ref[...] = y.astype(BF16)
        zf = z_ref[...].astype(F32)
        u = y * zf * _sigmoid(zf)
        for g in range(SSD_GROUPS):
            gs = slice(g * gw, (g + 1) * gw)
            ug = u[:, gs]
            ms = jnp.mean(ug * ug, axis=-1, keepdims=True)
            yssd_ref[:, gs] = (ug * lax.rsqrt(ms + EPS) * nrm_ref[:, gs]).astype(BF16)

    rb = CHUNK // HALO
    return pl.pallas_call(
        body, name="ssd_fwd", grid=(nch,),
        in_specs=[pl.BlockSpec((CHUNK, cd), lambda c: (c, 0)),
                  pl.BlockSpec((HALO, cd), lambda c: (jnp.maximum(c * rb - 1, 0), 0)),
                  pl.BlockSpec((CHUNK, ds), lambda c: (c, 0)),
                  pl.BlockSpec((CHUNK, LANES), lambda c: (c, 0)),
                  _full((CONV_K, cd)), _full((1, cd)), _full((1, LANES)), _full((1, LANES)),
                  _full((1, ds)), _full((1, ds)), _full((LANES, ds))],
        out_specs=[pl.BlockSpec((CHUNK, ds), lambda c: (c, 0)), pl.BlockSpec((CHUNK, ds), lambda c: (c, 0)),
                   pl.BlockSpec((1, ns, ds), lambda c: (c, 0, 0)), pl.BlockSpec((CHUNK, LANES), lambda c: (c, 0)),
                   pl.BlockSpec((CHUNK, cd), lambda c: (c, 0))],
        out_shape=[jax.ShapeDtypeStruct((p, ds), BF16), jax.ShapeDtypeStruct((p, ds), BF16),
                   jax.ShapeDtypeStruct((nch, ns, ds), BF16), jax.ShapeDtypeStruct((p, LANES), F32),
                   jax.ShapeDtypeStruct((p, cd), BF16)],
        scratch_shapes=[pltpu.VMEM((ns, ds), F32), pltpu.VMEM((1, LANES), F32), pltpu.VMEM((CHUNK, ds), F32),
                        pltpu.VMEM((CHUNK, cd), F32), pltpu.VMEM((CHUNK, ds), F32),
                        pltpu.VMEM((CHUNK, ds), BF16), pltpu.VMEM((CHUNK, ds), BF16),
                        pltpu.VMEM((8 + CHUNK, cd), F32)],
        compiler_params=_cparams(("arbitrary",)),
    )(xbc, xbc, z, dtf, conv_w, conv_b, brow, alog, dskip_l, ssd_norm, sel_t)


def _ssd_bwd(dyssd, y, z, xbc, pre, dtf, hin, dcf, conv_w, brow, alog, dskip_l, ssd_norm, sel_t, sel, hs, ha):
    p, cd = xbc.shape
    ds = z.shape[1]
    ns = (cd - ds) // (2 * SSD_GROUPS)
    gw = ds // SSD_GROUPS
    nch = p // CHUNK
    hpg = hs // SSD_GROUPS

    def body(dyssd_ref, y_ref, z_ref, xbc_ref, pre_ref, dtf_ref, hin_ref, dcf_ref, cw_ref, brow_ref,
             alog_ref, dsk_ref, nrm_ref, selt_ref, sel_ref,
             dxbc_ref, dz_ref, ddtf_ref, gcw_ref, gcb_ref, gnrm_ref, gsm_ref,
             dst_ref, nxt_ref, fcar_ref, gdsk_ref, dxc_ref, xc_s, dsl_s, dtx_s, ex_s, wx_s, dy_s, xdtb_s, xwb_s,
             dyb_s, dyeb_s):
        step = pl.program_id(0)
        c = nch - 1 - step

        @pl.when(step == 0)
        def _():
            dst_ref[...] = jnp.zeros_like(dst_ref)
            nxt_ref[...] = jnp.zeros_like(nxt_ref)
            fcar_ref[...] = jnp.zeros_like(fcar_ref)
            gdsk_ref[...] = jnp.zeros_like(gdsk_ref)
            gcw_ref[...] = jnp.zeros_like(gcw_ref)
            gcb_ref[...] = jnp.zeros_like(gcb_ref)
            gnrm_ref[...] = jnp.zeros_like(gnrm_ref)
            gsm_ref[...] = jnp.zeros_like(gsm_ref)

        rows = lax.broadcasted_iota(jnp.int32, (CHUNK, 1), 0)
        rowmask = jnp.where((rows >= PADN) | (c > 0), 1.0, 0.0)
        ri, ci = _tri_mats()
        causal = ri >= ci
        anti = ci >= ri
        tri = jnp.where(causal, 1.0, 0.0).astype(BF16)
        rtri = jnp.where(anti, 1.0, 0.0).astype(BF16)

        pre = pre_ref[...].astype(F32)
        sg = _sigmoid(pre)
        xc_s[...] = pre * sg * rowmask
        dsl_s[...] = sg * (1.0 + pre * (1.0 - sg)) * rowmask

        dtr, dt, a_row, run, is_dt, is_f = _ssd_scalars(dtf_ref, brow_ref, alog_ref, rowmask, hs, ha, tri)
        cs = jnp.where(is_dt, run, 0.0)
        cl = cs[CHUNK - 1:CHUNK, :]
        selt = selt_ref[...]
        selm = sel_ref[...]
        dtx_s[...] = _dot_sel(dt, selt)
        ex_s[...] = _dot_sel(jnp.exp(cs), selt)
        wx_s[...] = _dot_sel(jnp.exp(cl - cs), selt)
        cdec = jnp.exp(cl)
        cdec_x = _dot_sel(jnp.broadcast_to(cdec, (8, LANES)), selt)[0:1, :]
        cs_t = cs.T
        xdt = xc_s[:, :ds] * dtx_s[...]
        xdtb_s[...] = xdt.astype(BF16)
        xwb_s[...] = (xdt * wx_s[...]).astype(BF16)

        yv = y_ref[...].astype(F32)
        zf = z_ref[...].astype(F32)
        sz = _sigmoid(zf)
        u = yv * zf * sz
        dyo = dyssd_ref[...].astype(F32)
        du_parts = []
        for g in range(SSD_GROUPS):
            gs = slice(g * gw, (g + 1) * gw)
            ug = u[:, gs]
            rstd = lax.rsqrt(jnp.mean(ug * ug, axis=-1, keepdims=True) + EPS)
            yhat = ug * rstd
            dyg = dyo[:, gs]
            gnrm_ref[0:1, gs] += jnp.sum(dyg * yhat, axis=0, keepdims=True)
            dyh = dyg * nrm_ref[:, gs]
            du_parts.append(rstd * (dyh - yhat * jnp.mean(dyh * yhat, axis=-1, keepdims=True)))
        du = jnp.concatenate(du_parts, axis=1)
        dy = du * zf * sz
        dz_ref[...] = (du * yv * sz * (1.0 + zf * (1.0 - sz))).astype(BF16)
        dy_s[...] = dy
        dyb_s[...] = dy.astype(BF16)
        dyeb_s[...] = (dy * ex_s[...]).astype(BF16)
        gdsk_ref[...] += jnp.sum(dy * xc_s[:, :ds], axis=0, keepdims=True)
        lane = lax.broadcasted_iota(jnp.int32, (1, LANES), 1)
        half0 = lane < HEAD_DIM
        x_parts, yo_parts, t4_parts = [], [], []
        dcs = jnp.zeros((CHUNK, LANES), F32)
        for g in range(SSD_GROUPS):
            gs = slice(g * gw, (g + 1) * gw)
            bsl = slice(ds + g * ns, ds + (g + 1) * ns)
            csl = slice(ds + SSD_GROUPS * ns + g * ns, ds + SSD_GROUPS * ns + (g + 1) * ns)
            bg = xc_s[:, bsl].astype(BF16)
            cg = xc_s[:, csl].astype(BF16)
            gm = _dot(cg, bg, NT)
            gm_t = _dot(bg, cg, NT)
            stg_b = hin_ref[0, :, gs]
            dstg = dst_ref[:, gs]
            dstg_b = dstg.astype(BF16)
            t4_parts.append(jnp.sum(dstg * stg_b.astype(F32), axis=0, keepdims=True))
            zst = _dot(bg, dstg_b) * wx_s[:, gs]
            x_parts.append(xc_s[:, gs] * dtx_s[:, gs] * zst)
            yo_parts.append(dy_s[:, gs] * (_dot(cg, stg_b) * ex_s[:, gs]))
            dgsum = jnp.zeros((CHUNK, CHUNK), F32)
            dgtsum = jnp.zeros((CHUNK, CHUNK), F32)
            for pr in range(gw // LANES):
                sl = slice(g * gw + pr * LANES, g * gw + (pr + 1) * LANES)
                xp = xdtb_s[:, sl]
                dyp = dyb_s[:, sl]
                dxd = zst[:, pr * LANES:(pr + 1) * LANES]
                for j in range(2):
                    h = g * hpg + 2 * pr + j
                    sel_l = half0 if j == 0 else jnp.logical_not(half0)
                    seg = cs[:, h:h + 1] - cs_t[h:h + 1, :]
                    lm = jnp.where(causal, jnp.exp(jnp.minimum(seg, 0.0)), 0.0)
                    lmt = jnp.where(anti, jnp.exp(jnp.minimum(-seg, 0.0)), 0.0)
                    dyp_m = jnp.where(sel_l, dyp, jnp.zeros_like(dyp))
                    xp_m = jnp.where(sel_l, xp, jnp.zeros_like(xp))
                    dxd = dxd + _dot((gm_t * lmt).astype(BF16), dyp_m)
                    dg = _dot(dyp_m, xp, NT) * lm
                    dgt = _dot(xp_m, dyp, NT) * lmt
                    dgsum = dgsum + dg
                    dgtsum = dgtsum + dgt
                    qrow = (jnp.sum(dg * gm, axis=1, keepdims=True) - jnp.sum(dgt * gm_t, axis=1, keepdims=True))
                    dcs = dcs + jnp.where(lane == h, qrow, 0.0)
                dxc_ref[:, sl] = dxd
            dxc_ref[:, csl] = _dot(dgsum.astype(BF16), bg) + _dot(dyeb_s[:, gs], stg_b, NT)
            dxc_ref[:, bsl] = _dot(dgtsum.astype(BF16), cg) + _dot(xwb_s[:, gs], dstg_b, NT)
            dst_ref[:, gs] = dstg * cdec_x[:, gs] + _dot(cg, dyeb_s[:, gs], TN)

        dxdt = dxc_ref[:, :ds]
        xst = _dot_sel(jnp.concatenate(x_parts, axis=1), selm)
        yo = _dot_sel(jnp.concatenate(yo_parts, axis=1), selm)
        t4 = _dot_sel(jnp.concatenate([jnp.concatenate(t4_parts, axis=1), jnp.zeros((7, ds), F32)], axis=0), selm)
        dcl = jnp.sum(xst, axis=0, keepdims=True) + cdec * t4[0:1, :]
        dcs = dcs + yo - xst + jnp.where(rows == CHUNK - 1, dcl, 0.0)
        da_ = _dot_tri(rtri, dcs)
        ddt = _dot_sel(dxdt * xc_s[:, :ds], selm) + da_ * a_row
        dcf_blk = dcf_ref[...]
        dlogf = _dot_tri(rtri, dcf_blk) + fcar_ref[...]
        fcar_ref[...] += jnp.sum(dcf_blk, axis=0, keepdims=True)
        sgd = _sigmoid(dtr)
        ddtf = (jnp.where(is_dt, ddt * sgd, 0.0) + jnp.where(is_f, dlogf * (1.0 - sgd), 0.0)) * rowmask
        ddtf_ref[...] = ddtf
        gsm_ref[0:1, :] += jnp.sum(ddtf, axis=0, keepdims=True)
        gsm_ref[1:2, :] += jnp.sum(da_ * dt, axis=0, keepdims=True) * a_row

        dxc_ref[:, :ds] = dxdt * dtx_s[...] + dsk_ref[...] * dy_s[...]
        dpre = dxc_ref[...] * dsl_s[...]
        nxt_ref[0:CHUNK, :] = dpre
        gcb_ref[0:1, :] += jnp.sum(dpre, axis=0, keepdims=True)
        xr = xbc_ref[...].astype(F32)
        gcw_ref[CONV_K - 1:CONV_K, :] += jnp.sum(dpre * xr, axis=0, keepdims=True)
        dxr = cw_ref[CONV_K - 1:CONV_K, :] * dpre
        for j in range(1, CONV_K):
            up = nxt_ref[j:j + CHUNK, :]
            gcw_ref[CONV_K - 1 - j:CONV_K - j, :] += jnp.sum(up * xr, axis=0, keepdims=True)
            dxr = dxr + cw_ref[CONV_K - 1 - j:CONV_K - j, :] * up
        nxt_ref[CHUNK:, :] = dpre[0:8, :]
        dxbc_ref[...] = dxr.astype(BF16)

        @pl.when(step == nch - 1)
        def _():
            gsm_ref[2:3, :] = _dot_sel(jnp.broadcast_to(gdsk_ref[...], (8, ds)), selm)[0:1, :]

    rev = lambda s: nch - 1 - s
    blk = lambda w: pl.BlockSpec((CHUNK, w), lambda s: (rev(s), 0))
    return pl.pallas_call(
        body, name="ssd_bwd", grid=(nch,),
        in_specs=[blk(ds), blk(ds), blk(ds), blk(cd), blk(cd),
                  blk(LANES), pl.BlockSpec((1, ns, ds), lambda s: (rev(s), 0, 0)), blk(LANES),
                  _full((CONV_K, cd)), _full((1, LANES)), _full((1, LANES)),
                  _full((1, ds)), _full((1, ds)), _full((LANES, ds)), _full((ds, LANES))],
        out_specs=[blk(cd), blk(ds), blk(LANES), _full((8, cd)), _full((8, cd)), _full((8, ds)), _full((8, LANES))],
        out_shape=[jax.ShapeDtypeStruct((p, cd), BF16), jax.ShapeDtypeStruct((p, ds), BF16),
                   jax.ShapeDtypeStruct((p, LANES), F32), jax.ShapeDtypeStruct((8, cd), F32),
                   jax.ShapeDtypeStruct((8, cd), F32), jax.ShapeDtypeStruct((8, ds), F32),
                   jax.ShapeDtypeStruct((8, LANES), F32)],
        scratch_shapes=[pltpu.VMEM((ns, ds), F32), pltpu.VMEM((CHUNK + 8, cd), F32), pltpu.VMEM((1, LANES), F32),
                        pltpu.VMEM((1, ds), F32), pltpu.VMEM((CHUNK, cd), F32),
                        pltpu.VMEM((CHUNK, cd), F32), pltpu.VMEM((CHUNK, cd), F32),
                        pltpu.VMEM((CHUNK, ds), F32), pltpu.VMEM((CHUNK, ds), F32), pltpu.VMEM((CHUNK, ds), F32),
                        pltpu.VMEM((CHUNK, ds), F32), pltpu.VMEM((CHUNK, ds), BF16), pltpu.VMEM((CHUNK, ds), BF16),
                        pltpu.VMEM((CHUNK, ds), BF16), pltpu.VMEM((CHUNK, ds), BF16)],
        compiler_params=_cparams(("arbitrary",)),
    )(dyssd, y, z, xbc, pre, dtf, hin, dcf, conv_w, brow, alog, dskip_l, ssd_norm, sel_t, sel)


def _attn_fwd(q, k, v, ck, blk):
    p, da = q.shape
    npair, nkb = ck.shape[0], ck.shape[1]
    scale = 1.0 / math.sqrt(HEAD_DIM)

    def body(q_ref, k_ref, v_ref, ck_ref, o_ref, lse_ref):
        i = pl.program_id(1)
        lane = lax.broadcasted_iota(jnp.int32, (1, LANES), 1)
        sels = [lane < HEAD_DIM, lane >= HEAD_DIM]
        ones = [jnp.where(lane == HEAD_DIM, 1.0, 0.0).astype(BF16), jnp.where(lane == 0, 1.0, 0.0).astype(BF16)]
        qb = q_ref[...] * scale
        qms = [jnp.where(sel, qb, jnp.zeros_like(qb)) for sel in sels]
        cmask = (lax.broadcasted_iota(jnp.int32, (blk, blk), 1) <= lax.broadcasted_iota(jnp.int32, (blk, blk), 0))

        def step(kb, carry, masked, nk=1):
            r0 = pl.multiple_of(kb * blk, blk)
            ks = k_ref[pl.ds(r0, nk * blk), :]
            vs = v_ref[pl.ds(r0, nk * blk), :]
            out = []
            for j in range(2):
                m, acc = carry[2 * j], carry[2 * j + 1]
                ckr = jnp.concatenate([ck_ref[0, kb + t, j:j + 1, :] for t in range(nk)], axis=1)
                s = _dot(qms[j], ks, NT) - ckr
                if masked:
                    s = jnp.where(cmask, s, NEG)
                mn = jnp.maximum(m, jnp.max(s, axis=-1, keepdims=True))
                pr = jnp.exp(s - mn).astype(BF16)
                acc = jnp.exp(m - mn) * acc + _dot(pr, jnp.where(sels[j], vs, ones[j]))
                out += [mn, acc]
            return tuple(out)

        init = (jnp.full((blk, 1), NEG, F32), jnp.zeros((blk, LANES), F32)) * 2
        n4 = i // 4
        n2 = (i - 4 * n4) // 2
        carry = lax.fori_loop(0, n4, lambda t, c: step(4 * t, c, False, 4), init)
        carry = lax.fori_loop(0, n2, lambda t, c: step(4 * n4 + 2 * t, c, False, 2), carry)
        carry = lax.fori_loop(4 * n4 + 2 * n2, i, lambda kb, c: step(kb, c, False), carry)
        m0, a0, m1, a1 = step(i, carry, True)
        l0 = a0[:, HEAD_DIM:HEAD_DIM + 1]
        l1 = a1[:, 0:1]
        o_ref[...] = jnp.where(sels[0], a0 / l0, a1 / l1).astype(BF16)
        lse_ref[...] = jnp.where(sels[0], m0 + jnp.log(l0), m1 + jnp.log(l1))

    return pl.pallas_call(
        body, name="attn_fwd", grid=(npair, p // blk),
        in_specs=[pl.BlockSpec((blk, LANES), lambda h, i: (i, h)),
                  pl.BlockSpec((p, LANES), lambda h, i: (0, h)), pl.BlockSpec((p, LANES), lambda h, i: (0, h)),
                  pl.BlockSpec((1, nkb, 8, blk), lambda h, i: (h, 0, 0, 0))],
        out_specs=[pl.BlockSpec((blk, LANES), lambda h, i: (i, h)), pl.BlockSpec((blk, LANES), lambda h, i: (i, h))],
        out_shape=[jax.ShapeDtypeStruct((p, da), BF16), jax.ShapeDtypeStruct((p, da), F32)],
        compiler_params=_cparams(("parallel", "arbitrary")),
    )(q, k, v, ck)


def _attn_bwd(q, k, v, o, do, lse_rep, ck, blk):
    p, da = q.shape
    npair, nkb = ck.shape[0], ck.shape[1]
    nq = p // blk
    scale = 1.0 / math.sqrt(HEAD_DIM)

    def body(k_ref, v_ref, q_ref, do_ref, o_ref, lse_ref, ck_ref, dk_ref, dv_ref, dq_ref, dcs_ref, rsum_ref, dq_acc):
        jb = pl.program_id(1)

        @pl.when(jb == 0)
        def _():
            dq_acc[...] = jnp.zeros_like(dq_acc)

        ks = k_ref[...]
        vs = v_ref[...]
        lane = lax.broadcasted_iota(jnp.int32, (1, LANES), 1)
        sels = [lane < HEAD_DIM, lane >= HEAD_DIM]
        ones = [jnp.where(lane == HEAD_DIM, 1.0, 0.0).astype(BF16), jnp.where(lane == 0, 1.0, 0.0).astype(BF16)]
        kss = ks * scale
        kmo = [jnp.where(sels[j], kss, ones[j]) for j in range(2)]
        cmask = (lax.broadcasted_iota(jnp.int32, (blk, blk), 1) <= lax.broadcasted_iota(jnp.int32, (blk, blk), 0))

        def step(ib, carry, masked, nb=1):
            rows = nb * blk
            r0 = pl.multiple_of(ib * blk, blk)
            qb = q_ref[pl.ds(r0, rows), :] * scale
            dob = do_ref[pl.ds(r0, rows), :]
            prod = dob.astype(F32) * o_ref[pl.ds(r0, rows), :].astype(F32)
            out = []
            for j in range(2):
                dk, dv = carry[2 * j], carry[2 * j + 1]
                qm = jnp.where(sels[j], qb, jnp.zeros_like(qb))
                dom = jnp.where(sels[j], dob, jnp.zeros_like(dob))
                lse = lse_ref[pl.ds(r0, rows), HEAD_DIM * j:HEAD_DIM * j + 1]
                dlt = jnp.sum(jnp.where(sels[j], prod, 0.0), axis=-1, keepdims=True)
                s = _dot(qm, ks, NT) - ck_ref[0, 0, j:j + 1, :] - lse
                pm = jnp.exp(jnp.minimum(s, 0.0))
                if masked:
                    pm = jnp.where(cmask, pm, 0.0)
                ds_b = (pm * (_dot(dom, vs, NT) - dlt)).astype(BF16)
                dv = dv + _dot(pm.astype(BF16), dom, TN)
                dk = dk + _dot(ds_b, jnp.where(sels[j], qb, ones[j]), TN)
                dq_acc[pl.ds(r0, rows), LANES * j:LANES * (j + 1)] += _dot(ds_b, kmo[j])
                out += [dk, dv]
            return tuple(out)

        zero = jnp.zeros((blk, LANES), F32)
        carry = step(jb, (zero, zero, zero, zero), True)
        n4 = (nq - 1 - jb) // 4
        n2 = (nq - 1 - jb - 4 * n4) // 2
        carry = lax.fori_loop(0, n4, lambda t, c: step(jb + 1 + 4 * t, c, False, 4), carry)
        carry = lax.fori_loop(0, n2, lambda t, c: step(jb + 1 + 4 * n4 + 2 * t, c, False, 2), carry)
        dk0, dv0, dk1, dv1 = lax.fori_loop(jb + 1 + 4 * n4 + 2 * n2, nq, lambda ib, c: step(ib, c, False), carry)
        dk_ref[...] = jnp.where(sels[0], dk0, dk1).astype(BF16)
        dv_ref[...] = (dv0 + dv1).astype(BF16)
        pair8 = lambda c0, c1: jnp.where(lane == 0, c0, jnp.where(lane == 1, c1, 0.0)).T[0:8]
        dcs_ref[0] = pair8(dk0[:, HEAD_DIM:HEAD_DIM + 1], dk1[:, 0:1])

        @pl.when(jb == nkb - 1)
        def _():
            a0 = dq_acc[:, :LANES]
            a1 = dq_acc[:, LANES:]
            dq_ref[...] = jnp.where(sels[0], a0, a1).astype(BF16)
            rsum_ref[0] = pair8(a0[:, HEAD_DIM:HEAD_DIM + 1], a1[:, 0:1])

    colblk = pl.BlockSpec((blk, LANES), lambda h, j: (j, h))
    colfull = pl.BlockSpec((p, LANES), lambda h, j: (0, h))
    ckspec = pl.BlockSpec((1, 1, 8, blk), lambda h, j: (h, j, 0, 0))
    return pl.pallas_call(
        body, name="attn_bwd", grid=(npair, nkb),
        in_specs=[colblk, colblk, colfull, colfull, colfull, colfull, ckspec],
        out_specs=[colblk, colblk, colfull, pl.BlockSpec((1, 8, blk), lambda h, j: (h, 0, j)),
                   pl.BlockSpec((1, 8, p), lambda h, j: (h, 0, 0))],
        out_shape=[jax.ShapeDtypeStruct((p, da), BF16), jax.ShapeDtypeStruct((p, da), BF16),
                   jax.ShapeDtypeStruct((p, da), BF16), jax.ShapeDtypeStruct((npair, 8, p), F32),
                   jax.ShapeDtypeStruct((npair, 8, p), F32)],
        scratch_shapes=[pltpu.VMEM((p, 2 * LANES), F32)],
        compiler_params=_cparams(("parallel", "arbitrary")),
    )(k, v, q, do, o, lse_rep, ck)


def _rows3(i):
    return jnp.maximum(3 * i - 1, 0), 3 * i, 3 * i + 1


def _tail_fwd(yssd, o, zatt, graw, head, x2, tgt2, wps, wpa, wout, gate_bias, norm_post, tm):
    p, ds = yssd.shape
    da = o.shape[1]
    d = x2.shape[1]
    nsub = tm // CHUNK

    def body(yssd_ref, o_ref, zatt_ref, g_ref, head_ref, *rest):
        x_refs, t_refs = rest[:nsub], rest[nsub:2 * nsub]
        (wps_ref, wpa_ref, wout_ref, gb_ref, np_ref,
         yatt_ref, mrg_ref, a_ref, b_ref, dzo_ref, dout_ref, red_ref) = rest[2 * nsub:]
        i = pl.program_id(0)

        @pl.when(i == 0)
        def _():
            red_ref[...] = jnp.zeros_like(red_ref)

        first = jnp.where(i == 0, head_ref[...], x_refs[0][...])
        h = jnp.concatenate([first] + [r[...] for r in x_refs[1:]], axis=0)
        tgt = jnp.concatenate([r[...] for r in t_refs], axis=0)
        rows = lax.broadcasted_iota(jnp.int32, (tm, 1), 0)
        valid = jnp.where((i > 0) | (rows >= CHUNK), 1.0, 0.0)
        ob = o_ref[...].astype(F32)
        za = zatt_ref[...].astype(F32)
        yatt_b = (ob * za * _sigmoid(za)).astype(BF16)
        yatt_ref[...] = yatt_b
        a = _dot(yssd_ref[...], wps_ref[...])
        b = _dot(yatt_b, wpa_ref[...])
        a_ref[...] = a.astype(BF16)
        b_ref[...] = b.astype(BF16)
        gr = g_ref[...].astype(F32) + gb_ref[...]
        mrg_b = (_sigmoid(gr[:, :d]) * a + _sigmoid(gr[:, d:]) * b).astype(BF16)
        mrg_ref[...] = mrg_b
        zo = _dot(mrg_b, wout_ref[...])
        rstd = lax.rsqrt(jnp.mean(zo * zo, axis=-1, keepdims=True) + EPS)
        zh = zo * rstd
        npw = np_ref[...]
        err = (h + zh * npw - tgt) * valid
        dout = err * (1.0 / d)
        dout_ref[...] = dout
        dzh = dout * npw
        dzo_ref[...] = (rstd * (dzh - zh * jnp.mean(dzh * zh, axis=-1, keepdims=True))).astype(BF16)
        red_ref[0:1, :] += jnp.sum(dout * zh, axis=0, keepdims=True)
        red_ref[1:2, 0:1] += jnp.sum(jnp.sum(err * err, axis=1, keepdims=True), axis=0, keepdims=True) * (0.5 / d)

    row = lambda w: pl.BlockSpec((tm, w), lambda i: (i, 0))
    once = lambda shape: pl.BlockSpec(shape, lambda i: (0,) * len(shape), pipeline_mode=pl.Buffered(1))
    if nsub == 1:
        subs = [pl.BlockSpec((CHUNK, d), lambda i: (jnp.maximum(i - 1, 0), 0))]
    else:
        subs = [pl.BlockSpec((CHUNK, d), functools.partial(lambda i, k: (_rows3(i)[k], 0), k=k)) for k in range(3)]
    sd = jax.ShapeDtypeStruct
    return pl.pallas_call(
        body, name="tail_fwd", grid=(p // tm,),
        in_specs=[row(ds), row(da), row(da), row(2 * d), _full((CHUNK, d))] + subs + subs
                 + [once((ds, d)), once((da, d)), once((d, d)), _full((1, 2 * d)), _full((1, d))],
        out_specs=[row(da), row(d), row(d), row(d), row(d), row(d), _full((8, d))],
        out_shape=[sd((p, da), BF16), sd((p, d), BF16), sd((p, d), BF16), sd((p, d), BF16), sd((p, d), BF16),
                   sd((p, d), F32), sd((8, d), F32)],
        compiler_params=_cparams(("arbitrary",)),
    )(yssd, o, zatt, graw, head, *([x2] * nsub), *([tgt2] * nsub), wps, wpa, wout, gate_bias, norm_post)


def _tail_bwd(dzo, a_b, b_b, graw, o, zatt, wps, wpa, wout, gate_bias, tm):
    p, d = dzo.shape
    ds, da = wps.shape[0], wpa.shape[0]

    def body(dzo_ref, a_ref, b_ref, g_ref, o_ref, zatt_ref, wps_ref, wpa_ref, wout_ref, gb_ref,
             da_ref, db_ref, dg_ref, dyssd_ref, do_ref, dzatt_ref, red_ref):
        i = pl.program_id(0)

        @pl.when(i == 0)
        def _():
            red_ref[...] = jnp.zeros_like(red_ref)

        gr = g_ref[...].astype(F32) + gb_ref[...]
        gs = _sigmoid(gr[:, :d])
        ga = _sigmoid(gr[:, d:])
        dm = _dot(dzo_ref[...], wout_ref[...], NT)
        da_b = (gs * dm).astype(BF16)
        db_b = (ga * dm).astype(BF16)
        da_ref[...] = da_b
        db_ref[...] = db_b
        dgs = dm * a_ref[...].astype(F32) * gs * (1.0 - gs)
        dga = dm * b_ref[...].astype(F32) * ga * (1.0 - ga)
        dg_ref[:, :d] = dgs.astype(BF16)
        dg_ref[:, d:] = dga.astype(BF16)
        red_ref[0:1, :d] += jnp.sum(dgs, axis=0, keepdims=True)
        red_ref[0:1, d:] += jnp.sum(dga, axis=0, keepdims=True)
        dyssd_ref[...] = _dot(da_b, wps_ref[...], NT).astype(BF16)
        dya = _dot(db_b, wpa_ref[...], NT)
        ob = o_ref[...].astype(F32)
        za = zatt_ref[...].astype(F32)
        sza = _sigmoid(za)
        do_ref[...] = (dya * za * sza).astype(BF16)
        dzatt_ref[...] = (dya * ob * sza * (1.0 + za * (1.0 - sza))).astype(BF16)

    row = lambda w: pl.BlockSpec((tm, w), lambda i: (i, 0))
    once = lambda shape: pl.BlockSpec(shape, lambda i: (0,) * len(shape), pipeline_mode=pl.Buffered(1))
    sd = jax.ShapeDtypeStruct
    return pl.pallas_call(
        body, name="tail_bwd", grid=(p // tm,),
        in_specs=[row(d), row(d), row(d), row(2 * d), row(da), row(da),
                  once((ds, d)), once((da, d)), once((d, d)), _full((1, 2 * d))],
        out_specs=[row(d), row(d), row(2 * d), row(ds), row(da), row(da), _full((8, 2 * d))],
        out_shape=[sd((p, d), BF16), sd((p, d), BF16), sd((p, 2 * d), BF16), sd((p, ds), BF16), sd((p, da), BF16),
                   sd((p, da), BF16), sd((8, 2 * d), F32)],
        compiler_params=_cparams(("arbitrary",)),
    )(dzo, a_b, b_b, graw, o, zatt, wps, wpa, wout, gate_bias)


def _adamw_math(w, g, m, v):
    m2 = ADAM_B1 * m + (1.0 - ADAM_B1) * g
    v2 = ADAM_B2 * v + (1.0 - ADAM_B2) * (g * g)
    m_hat = m2 / (1.0 - ADAM_B1 ** ADAM_STEP)
    v_hat = v2 / (1.0 - ADAM_B2 ** ADAM_STEP)
    delta = -ADAM_LR * (m_hat / (jnp.sqrt(v_hat) + ADAM_EPS) + ADAM_WD * w)
    return delta, m2, v2


def _adamw_small(params, red, name):
    names = list(params)
    n = len(names)
    extra = [params[k][3] for k in names if not isinstance(params[k][3], tuple)]

    def body(*refs):
        w_refs, m_refs, v_refs = refs[:n], refs[n:2 * n], refs[2 * n:3 * n]
        red_ref = refs[3 * n]
        g_refs = iter(refs[3 * n + 1:3 * n + 1 + len(extra)])
        outs = refs[3 * n + 1 + len(extra):]
        for i, k in enumerate(names):
            where = params[k][3]
            rows, cols = w_refs[i].shape
            if isinstance(where, tuple):
                g = red_ref[where[0]:where[0] + rows, where[1]:where[1] + cols]
            else:
                g = next(g_refs)[...]
            delta, m2, v2 = _adamw_math(w_refs[i][...], g, m_refs[i][...], v_refs[i][...])
            for o, val in zip(outs[4 * i:4 * i + 4], (g, delta, m2, v2)):
                o[...] = val

    vm = pl.BlockSpec(memory_space=pltpu.VMEM)
    ws, ms, vs = ([params[k][j] for k in names] for j in range(3))
    out = pl.pallas_call(
        body, name=name,
        out_shape=[jax.ShapeDtypeStruct(w.shape, F32) for w in ws for _ in range(4)],
        in_specs=[vm] * (3 * n + 1 + len(extra)), out_specs=[vm] * (4 * n),
    )(*ws, *ms, *vs, red, *extra)
    return {k: tuple(out[4 * i:4 * i + 4]) for i, k in enumerate(names)}


def _adamw(w, g, m, v, name, parts=False, part_row0=0):
    r, cdim = w.shape
    tr, tc, by_rows = _tiles_2d(r, cdim)
    pick = (lambda i: (i, 0)) if by_rows else (lambda i: (0, i))
    assert part_row0 % tr == 0
    gpick = (lambda i: (i + part_row0 // tr, 0)) if by_rows else (lambda i: (part_row0 // tr, i))

    def body(w_ref, g_ref, m_ref, v_ref, go_ref, d_ref, mo_ref, vo_ref):
        if parts:
            g = g_ref[0].astype(F32)
            for s in range(1, g_ref.shape[0]):
                g = g + g_ref[s].astype(F32)
        else:
            g = g_ref[...]
        delta, m2, v2 = _adamw_math(w_ref[...], g, m_ref[...], v_ref[...])
        go_ref[...] = g
        d_ref[...] = delta
        mo_ref[...] = m2
        vo_ref[...] = v2

    blk = pl.BlockSpec((tr, tc), pick)
    gspec = pl.BlockSpec((g.shape[0], tr, tc), lambda i: (0,) + gpick(i)) if parts else blk
    return pl.pallas_call(
        body, name=name, grid=((r // tr) * (cdim // tc),),
        in_specs=[blk, gspec, blk, blk], out_specs=[blk] * 4,
        out_shape=[jax.ShapeDtypeStruct((r, cdim), F32)] * 4,
        compiler_params=_cparams(("parallel",)),
    )(w, g, m, v)


def _pad_cols(a, width):
    return jnp.pad(a, ((0, 0), (0, width - a.shape[1])))


def _pack_small_shard(conv_w_sh, meta_sh, width):
    return jnp.concatenate([_pad_cols(conv_w_sh, width), jnp.zeros((4, width), F32), _pad_cols(meta_sh, width)], axis=0)


def _pack_small_rep(norm_pre, norm_post, gate_bias, ssd_norm, conv_b, misc, width):
    rows = [norm_pre, norm_post, gate_bias, ssd_norm, conv_b, misc]
    return jnp.concatenate([_pad_cols(r, width) for r in rows] + [jnp.zeros((2, width), F32)], axis=0)


def kernel(x, meta_tokens, norm_pre, w_in, conv_w, conv_b, dt_bias, a_log, d_skip, ssd_norm, fgate_bias, gate_bias, w_proj_ssd, w_proj_att, w_out, norm_post, loss_target, m_meta_tokens, m_norm_pre, m_w_in, m_conv_w, m_conv_b, m_dt_bias, m_a_log, m_d_skip, m_ssd_norm, m_fgate_bias, m_gate_bias, m_w_proj_ssd, m_w_proj_att, m_w_out, m_norm_post, v_meta_tokens, v_norm_pre, v_w_in, v_conv_w, v_conv_b, v_dt_bias, v_a_log, v_d_skip, v_ssd_norm, v_fgate_bias, v_gate_bias, v_w_proj_ssd, v_w_proj_att, v_w_out, v_norm_post):
    seq, d = x.shape[1], x.shape[2]
    p = seq + CHUNK
    hs, ha = dt_bias.shape[1], fgate_bias.shape[1]
    ds, cd = ssd_norm.shape[1], conv_b.shape[1]
    da = ha * HEAD_DIM
    nc8 = w_in.shape[2]
    cws = cd // N_DEV
    msh = d // N_DEV
    r1, r2, r3 = ds // N_DEV, da // N_DEV, d // N_DEV
    me = _dev_index(*_my_pos())
    x2, tgt2 = x[0], loss_target[0]

    win_sh = jnp.transpose(w_in[0]).astype(BF16)
    rows_sh = jnp.concatenate([w_proj_ssd[0], w_proj_att[0], w_out[0]], axis=0).astype(BF16)
    small_sh = _pack_small_shard(conv_w[0], meta_tokens, cws)
    win_all, small_all = _all_gather([win_sh, small_sh], "gather_weights")
    rows_sh, win_all = lax.optimization_barrier((rows_sh, win_all))
    rows_sems, rows_thru, rows_land, rows_token = _bcast_start(rows_sh, "gather_rows_start")
    cuts = [0, ds, ds + cd, ds + cd + hs, ds + cd + hs + da, ds + cd + hs + 2 * da, ds + cd + hs + 3 * da,
            ds + cd + hs + 4 * da, ds + cd + hs + 4 * da + ha, ds + cd + hs + 4 * da + ha + 2 * d]

    def piece_rows(r0, r1):
        parts = [win_all[s, max(r0, s * nc8) - s * nc8:min(r1, (s + 1) * nc8) - s * nc8]
                 for s in range(N_DEV) if max(r0, s * nc8) < min(r1, (s + 1) * nc8)]
        return parts[0] if len(parts) == 1 else jnp.concatenate(parts, axis=0)

    w_z, w_xbc, w_dt, w_zatt, w_q, w_k, w_v, w_f, w_g = [piece_rows(cuts[i], cuts[i + 1]) for i in range(9)]
    w_dtf = jnp.concatenate([w_dt, w_f, jnp.zeros((LANES - hs - ha, d), BF16)], axis=0)
    conv_w_full = jnp.transpose(small_all[:, 0:CONV_K, :], (1, 0, 2)).reshape(CONV_K, cd)
    meta_full = jnp.transpose(small_all[:, 8:8 + N_META, :msh], (1, 0, 2)).reshape(N_META, d)
    head = jnp.concatenate([jnp.zeros((PADN, d), F32), meta_full + rows_token[0:1, 0:1]], axis=0)

    u = _prenorm_fwd(head, x2, norm_pre)
    tm = _att_block(p)
    seg_w = [w_z, w_xbc, w_zatt, w_q, w_k, w_v, w_g]
    zs, xbc, zatt, q, k, v, graw = [
        _mm(u, w, "nt", BF16, _tile(p, (1408, tm)), _tile(w.shape[0], (1024, 512, 256, 128)), "inproj_%d" % i)
        for i, w in enumerate(seg_w)]
    dtf = _mm(u, w_dtf, "nt", F32, _tile(p, (1408, tm)), LANES, "inproj_dtf")

    brow = jnp.concatenate([dt_bias, fgate_bias, jnp.zeros((1, LANES - hs - ha), F32)], axis=1)
    alog_row = _pad_cols(a_log, LANES)
    dskip_l = jnp.repeat(d_skip, HEAD_DIM, axis=1)
    sel_t = (lax.broadcasted_iota(jnp.int32, (LANES, ds), 1) // HEAD_DIM
             == lax.broadcasted_iota(jnp.int32, (LANES, ds), 0)).astype(BF16)
    sel = sel_t.T
    y, yssd, hin, cf, pre = _ssd_fwd(xbc, zs, dtf, conv_w_full, conv_b, brow, alog_row, dskip_l, ssd_norm, sel_t, hs, ha)

    blk = _att_block(p)
    nkb, npair = p // blk, ha // 2
    cum = jnp.where(lax.broadcasted_iota(jnp.int32, (p, 1), 0) < PADN, -NEG, cf[:, hs:hs + ha])
    ck = jnp.transpose(cum.T.reshape(npair, 2, nkb, blk), (0, 2, 1, 3))
    ck = jnp.pad(ck, ((0, 0), (0, 0), (0, 6), (0, 0)))
    o, lse_rep = _attn_fwd(q, k, v, ck, blk)

    rows_all = _bcast_wait(rows_sems, rows_thru, rows_land, lse_rep, "gather_rows_wait")
    wps = rows_all[:, :r1].reshape(ds, d)
    wpa = rows_all[:, r1:r1 + r2].reshape(da, d)
    wout = rows_all[:, r1 + r2:].reshape(d, d)

    yatt, mrg, a_b, b_b, dzo, dout, red_fwd = _tail_fwd(
        yssd, o, zatt, graw, head, x2, tgt2, wps, wpa, wout, gate_bias, norm_post, tm)
    da_, db_, dgraw, dyssd, d_o, dzatt, red_bwd = _tail_bwd(dzo, a_b, b_b, graw, o, zatt, wps, wpa, wout, gate_bias, tm)

    tw = _tile(d, (512, 256, 128))
    g_wout = _mm(mrg, dzo, "tn", BF16, tw, d, "wgrad_out")
    g_wps = _mm(yssd, da_, "tn", BF16, _tile(ds, (512, 256, 128)), d, "wgrad_ps")
    g_wpa = _mm(yatt, db_, "tn", BF16, _tile(da, (512, 256, 128)), d, "wgrad_pa")

    dk, dv, dq, dcs, rsum = _attn_bwd(q, k, v, o, d_o, lse_rep, ck, blk)
    dcum = (rsum - dcs)[:, 0:2, :].reshape(ha, p).T
    dcf = jnp.pad(dcum, ((0, 0), (hs, LANES - hs - ha)))
    dxbc, dzs, ddtf, gcw, gcb, gnrm, gsm = _ssd_bwd(
        dyssd, y, zs, xbc, pre, dtf, hin, dcf, conv_w_full, brow, alog_row, dskip_l, ssd_norm, sel_t, sel, hs, ha)
    ddtf_b = ddtf.astype(BF16)

    dsegs = [dzs, dxbc, dzatt, dq, dk, dv, dgraw, ddtf_b]
    gsegs = [_mm(dsg, u, "tn", BF16, _tile(dsg.shape[1], (512, 256, 128)), d, "wgrad_in_%d" % i)
             for i, dsg in enumerate(dsegs)]
    g_z, g_xbc, g_zatt, g_q, g_k, g_v, g_g, g_dtf = gsegs
    gw_full = jnp.concatenate([g_z, g_xbc, g_dtf[:hs], g_zatt, g_q, g_k, g_v, g_dtf[hs:hs + ha], g_g], axis=0)
    gwin_parts = gw_full.reshape(N_DEV, nc8, d)
    grows_parts = jnp.concatenate([g_wps.reshape(N_DEV, r1, d), g_wpa.reshape(N_DEV, r2, d),
                                   g_wout.reshape(N_DEV, r3, d)], axis=1)

    core = lax.axis_index("c").astype(jnp.int32).reshape(1)
    sib_win, sib_rows = _exchange_sibling([gwin_parts, grows_parts], "scatter_grads_sibling")
    chip_win = _pair_add(gwin_parts, sib_win, core, "pair_add_w_in")
    chip_rows = _pair_add(grows_parts, sib_rows, core, "pair_add_rows")
    sems, thru, lands, token = _exchange_chips_start([chip_win, chip_rows], "scatter_grads_start")
    dsegs_after = dsegs[:-1] + [ddtf_b + token[0:1, 0:1].astype(BF16)]
    du = _mm_sum_nn(dsegs_after, seg_w + [w_dtf], tm, _tile(d, (512, 256, 128)), "dgrad_in")
    gx, ghead, gnp = _prenorm_bwd(head, x2, norm_pre, du, dout)
    sent, got = _exchange_chips_wait(sems, thru, lands, gnp, "scatter_grads_wait")
    chip = me // 2
    recv_win, recv_rows = [lax.dynamic_update_slice_in_dim(g, lax.dynamic_slice_in_dim(s, chip, 1, axis=0), chip, axis=0)
                           for g, s in zip(got, sent)]
    gmisc = jnp.concatenate([gsm[0:1], gsm[1:2], gsm[2:3], _pad_cols(red_fwd[1:2, 0:1], LANES)], axis=1)
    small_g = jnp.concatenate([
        _pack_small_rep(gnp[0:1], red_fwd[0:1], red_bwd[0:1], gnrm[0:1], gcb[0:1], gmisc, cd),
        _pad_cols(gcw[0:CONV_K], cd), jnp.zeros((4, cd), F32), _pad_cols(ghead[PADN:], cd)], axis=0)
    sg_sems, sg_thru, sg_land, sg_token = _bcast_start(small_g, "reduce_small_start")

    upd_in = _adamw(jnp.transpose(w_in[0]) + sg_token[0:1, 0:1], recv_win, jnp.transpose(m_w_in[0]),
                    jnp.transpose(v_w_in[0]), "adamw_w_in", parts=True)
    upd_ps = _adamw(w_proj_ssd[0] + sg_token[0:1, 0:1], recv_rows, m_w_proj_ssd[0], v_w_proj_ssd[0],
                    "adamw_w_proj_ssd", parts=True, part_row0=0)
    upd_pa = _adamw(w_proj_att[0], recv_rows, m_w_proj_att[0], v_w_proj_att[0], "adamw_w_proj_att", parts=True,
                    part_row0=r1)
    upd_out = _adamw(w_out[0], recv_rows, m_w_out[0], v_w_out[0], "adamw_w_out", parts=True, part_row0=r1 + r2)
    all_done = upd_in[1][0:8, 0:LANES] + upd_ps[1][0:8, 0:LANES] + upd_pa[1][0:8, 0:LANES] + upd_out[1][0:8, 0:LANES]
    red = _sum_slots(_bcast_wait(sg_sems, sg_thru, sg_land, all_done, "reduce_small_wait"), "reduce_small_sum")
    loss = red[5, 3 * LANES]
    g_conv_w = lax.dynamic_slice_in_dim(red[8:8 + CONV_K], me * cws, cws, axis=1)
    g_meta = lax.dynamic_slice_in_dim(red[16:16 + N_META, :d], me * msh, msh, axis=1)
    small = {
        "meta_tokens": (meta_tokens, m_meta_tokens, v_meta_tokens, g_meta),
        "norm_pre": (norm_pre, m_norm_pre, v_norm_pre, (0, 0)),
        "conv_w": (conv_w[0], m_conv_w[0], v_conv_w[0], g_conv_w),
        "conv_b": (conv_b, m_conv_b, v_conv_b, (4, 0)),
        "dt_bias": (dt_bias, m_dt_bias, v_dt_bias, (5, 0)),
        "a_log": (a_log, m_a_log, v_a_log, (5, LANES)),
        "d_skip": (d_skip, m_d_skip, v_d_skip, (5, 2 * LANES)),
        "ssd_norm": (ssd_norm, m_ssd_norm, v_ssd_norm, (3, 0)),
        "fgate_bias": (fgate_bias, m_fgate_bias, v_fgate_bias, (5, hs)),
        "gate_bias": (gate_bias, m_gate_bias, v_gate_bias, (2, 0)),
        "norm_post": (norm_post, m_norm_post, v_norm_post, (1, 0)),
    }
    upd_small = _adamw_small(small, red, "adamw_small")

    def leaves(i):
        sm = {k: v[i] for k, v in upd_small.items()}
        return [sm["meta_tokens"], sm["norm_pre"], jnp.transpose(upd_in[i])[None], sm["conv_w"][None], sm["conv_b"],
                sm["dt_bias"], sm["a_log"], sm["d_skip"], sm["ssd_norm"], sm["fgate_bias"], sm["gate_bias"],
                upd_ps[i][None], upd_pa[i][None], upd_out[i][None], sm["norm_post"]]

    return tuple([loss, gx[None]] + leaves(0) + leaves(1) + leaves(2) + leaves(3))
```

```python
import functools
import math

import jax
import jax.numpy as jnp
from jax import lax
from jax.experimental import pallas as pl
from jax.experimental.pallas import tpu as pltpu

F32 = jnp.float32
BF16 = jnp.bfloat16

N_DEV = 8
N_META = 16
CHUNK = 128
PADN = CHUNK - N_META
HEAD_DIM = 64
SSD_GROUPS = 4
CONV_K = 4
EPS = 1e-6
NEG = -1e30
LANES = 128
HALO = 16

ADAM_LR = 0.001
ADAM_B1 = 0.9
ADAM_B2 = 0.999
ADAM_EPS = 1e-08
ADAM_WD = 0.01
ADAM_STEP = 10

VMEM_LIMIT = 56 * 1024 * 1024

NN = (((1,), (0,)), ((), ()))
NT = (((1,), (1,)), ((), ()))
TN = (((0,), (0,)), ((), ()))
MESH = pl.DeviceIdType.MESH


def _dot(a, b, dims=NN):
    return lax.dot_general(a, b, dims, preferred_element_type=F32)


def _split2(x):
    hi = x.astype(BF16)
    lo = (x - hi.astype(F32)).astype(BF16)
    return hi, lo


def _dot_sel(x, sel):
    hi, lo = _split2(x)
    return _dot(hi, sel) + _dot(lo, sel)


def _dot_tri(tri, x):
    h1 = x.astype(BF16)
    r1 = x - h1.astype(F32)
    h2 = r1.astype(BF16)
    h3 = (r1 - h2.astype(F32)).astype(BF16)
    return _dot(tri, h1) + _dot(tri, h2) + _dot(tri, h3)


def _sigmoid(x):
    return 1.0 / (1.0 + jnp.exp(-x))


def _softplus(x):
    return jnp.maximum(x, 0.0) + jnp.log(1.0 + jnp.exp(-jnp.abs(x)))


def _cparams(sem=None, vmem=VMEM_LIMIT):
    kw = {"vmem_limit_bytes": vmem}
    if sem is not None:
        kw["dimension_semantics"] = sem
    return pltpu.CompilerParams(**kw)


def _full(shape):
    nd = len(shape)
    return pl.BlockSpec(shape, lambda *_: (0,) * nd)


def _att_block(p):
    return 384 if p % 384 == 0 else CHUNK


def _my_pos():
    return lax.axis_index("x"), lax.axis_index("y"), lax.axis_index("c")


def _dev_index(x, y, c):
    return 4 * x + 2 * y + c


FLIPS = [(fx, fy, fc) for fx in (0, 1) for fy in (0, 1) for fc in (0, 1)][1:]


def _flip(pos, f):
    return tuple((1 - p) if fi else p for p, fi in zip(pos, f))


def _all_gather(bufs, name):
    nb = len(bufs)

    def body(*refs):
        ins, outs = refs[:nb], refs[nb:2 * nb]
        send_sems, recv_sems, local_sems = refs[2 * nb:]
        x, y, c = _my_pos()
        me = _dev_index(x, y, c)
        sibling = (x, y, 1 - c)
        chips = [(1 - x, y), (x, 1 - y), (1 - x, 1 - y)]

        def copy(b, k, block_idx, to, src=None):
            dst = outs[b].at[block_idx]
            return pltpu.make_async_remote_copy(
                src_ref=dst if src is None else src, dst_ref=dst,
                send_sem=send_sems.at[b, k], recv_sem=recv_sems.at[b, k],
                device_id=to, device_id_type=MESH)

        started = []
        for b in range(nb):
            mine = pltpu.make_async_copy(ins[b], outs[b].at[me], local_sems.at[b])
            mine.start()
            started.append(mine)
        first = []
        for b in range(nb):
            first.append(copy(b, 0, me, sibling, src=ins[b]))
            for j, chip in enumerate(chips):
                first.append(copy(b, 1 + j, me, (chip[0], chip[1], c), src=ins[b]))
        for cp in first:
            cp.start()
        passed = []
        for j, chip in enumerate(chips):
            blk = _dev_index(chip[0], chip[1], c)
            for b in range(nb):
                copy(b, 1 + j, blk, (x, y, c)).wait_recv()
                fwd = copy(b, 4 + j, blk, sibling)
                fwd.start()
                passed.append(fwd)
        for b in range(nb):
            copy(b, 0, _dev_index(x, y, 1 - c), (x, y, c)).wait_recv()
        for j, chip in enumerate(chips):
            blk = _dev_index(chip[0], chip[1], 1 - c)
            for b in range(nb):
                copy(b, 4 + j, blk, (x, y, c)).wait_recv()
        for cp in first + passed:
            cp.wait_send()
        for mine in started:
            mine.wait()

    any_spec = pl.BlockSpec(memory_space=pl.ANY)
    return pl.pallas_call(
        body, name=name,
        out_shape=[jax.ShapeDtypeStruct((N_DEV,) + b.shape, b.dtype) for b in bufs],
        in_specs=[any_spec] * nb, out_specs=[any_spec] * nb,
        scratch_shapes=[pltpu.SemaphoreType.DMA((nb, 7)), pltpu.SemaphoreType.DMA((nb, 7)),
                        pltpu.SemaphoreType.DMA((nb,))],
    )(*bufs)


N_CHIP = 4
CHIP_FLIPS = [(1, 0), (0, 1), (1, 1)]


def _exchange_sibling(bufs, name):
    nb = len(bufs)

    def body(*refs):
        ins, outs = refs[:nb], refs[nb:2 * nb]
        send_sems, recv_sems = refs[2 * nb:]
        x, y, c = _my_pos()

        def copy(b, k):
            return pltpu.make_async_remote_copy(
                src_ref=ins[b].at[2 * k + (1 - c)], dst_ref=outs[b].at[k],
                send_sem=send_sems.at[b, k], recv_sem=recv_sems.at[b, k],
                device_id=(x, y, 1 - c), device_id_type=MESH)

        cps = [copy(b, k) for b in range(nb) for k in range(N_CHIP)]
        for cp in cps:
            cp.start()
        for cp in cps:
            cp.wait()

    any_spec = pl.BlockSpec(memory_space=pl.ANY)
    return pl.pallas_call(
        body, name=name,
        out_shape=[jax.ShapeDtypeStruct((N_CHIP,) + b.shape[1:], b.dtype) for b in bufs],
        in_specs=[any_spec] * nb, out_specs=[any_spec] * nb,
        scratch_shapes=[pltpu.SemaphoreType.DMA((nb, N_CHIP)), pltpu.SemaphoreType.DMA((nb, N_CHIP))],
    )(*bufs)


def _pair_add(mine, recv, core, name):
    _, r, cdim = mine.shape
    tr, tc, by_rows = _tiles_2d(r, cdim)
    pick = (lambda i: (i, 0)) if by_rows else (lambda i: (0, i))

    def body(core_ref, a_ref, b_ref, o_ref):
        o_ref[0] = (a_ref[0].astype(F32) + b_ref[0].astype(F32)).astype(o_ref.dtype)

    return pl.pallas_call(
        body, name=name,
        grid_spec=pltpu.PrefetchScalarGridSpec(
            num_scalar_prefetch=1, grid=(N_CHIP, (r // tr) * (cdim // tc)),
            in_specs=[pl.BlockSpec((1, tr, tc), lambda k, i, core_ref: (2 * k + core_ref[0],) + pick(i)),
                      pl.BlockSpec((1, tr, tc), lambda k, i, core_ref: (k,) + pick(i))],
            out_specs=pl.BlockSpec((1, tr, tc), lambda k, i, core_ref: (k,) + pick(i))),
        out_shape=jax.ShapeDtypeStruct((N_CHIP, r, cdim), mine.dtype),
        compiler_params=_cparams(("parallel", "parallel")),
    )(core, mine, recv)


def _chip_peer(x, y, f):
    return ((1 - x) if f[0] else x), ((1 - y) if f[1] else y)


def _exchange_chips_start(bufs, name):
    nb = len(bufs)
    nsem = 2 * 3 * nb

    def body(*refs):
        ins, lands = refs[:nb], refs[nb:2 * nb]
        sems = refs[2 * nb:2 * nb + nsem]
        token = refs[-1]
        x, y, c = _my_pos()
        for b in range(nb):
            for j, f in enumerate(CHIP_FLIPS):
                px, py = _chip_peer(x, y, f)
                pltpu.make_async_remote_copy(
                    src_ref=ins[b].at[2 * px + py], dst_ref=lands[b].at[2 * x + y],
                    send_sem=sems[2 * (3 * b + j)], recv_sem=sems[2 * (3 * b + j) + 1],
                    device_id=(px, py, c), device_id_type=MESH).start()
        token[...] = jnp.zeros_like(token)

    hbm = pl.BlockSpec(memory_space=pltpu.HBM)
    sem = pl.BlockSpec(memory_space=pltpu.SEMAPHORE)
    out = pl.pallas_call(
        body, name=name,
        out_shape=(*([pltpu.SemaphoreType.DMA(())] * nsem),
                   *[pltpu.HBM(b.shape, b.dtype) for b in bufs], *[pltpu.HBM(b.shape, b.dtype) for b in bufs],
                   jax.ShapeDtypeStruct((8, LANES), F32)),
        in_specs=[hbm] * (2 * nb),
        out_specs=(*([sem] * nsem), *([hbm] * (2 * nb)), pl.BlockSpec(memory_space=pltpu.VMEM)),
        input_output_aliases={i: nsem + i for i in range(2 * nb)},
        compiler_params=pltpu.CompilerParams(has_side_effects=pltpu.SideEffectType.DATAFLOW_SIDE_EFFECTING),
    )(*[pltpu.with_memory_space_constraint(b, pltpu.HBM) for b in bufs],
      *[pltpu.with_memory_space_constraint(lax.empty(b.shape, b.dtype), pltpu.HBM) for b in bufs])
    return out[:nsem], out[nsem:nsem + nb], out[nsem + nb:nsem + 2 * nb], out[-1]


def _exchange_chips_wait(sems, thru, lands, after, name):
    nb = len(thru)
    nsem = len(sems)

    def body(*refs):
        ins, lnd = refs[:nb], refs[nb:2 * nb]
        sem_refs = refs[2 * nb:2 * nb + nsem]
        x, y, c = _my_pos()
        for b in range(nb):
            for j, f in enumerate(CHIP_FLIPS):
                px, py = _chip_peer(x, y, f)
                cp = pltpu.make_async_remote_copy(
                    src_ref=ins[b].at[2 * px + py], dst_ref=lnd[b].at[2 * px + py],
                    send_sem=sem_refs[2 * (3 * b + j)], recv_sem=sem_refs[2 * (3 * b + j) + 1],
                    device_id=(px, py, c), device_id_type=MESH)
                cp.wait_send()
                cp.wait_recv()

    hbm = pl.BlockSpec(memory_space=pltpu.HBM)
    sem = pl.BlockSpec(memory_space=pltpu.SEMAPHORE)
    out = pl.pallas_call(
        body, name=name,
        out_shape=tuple([pltpu.HBM(b.shape, b.dtype) for b in thru] + [pltpu.HBM(b.shape, b.dtype) for b in lands]),
        in_specs=[hbm] * (2 * nb) + [sem] * nsem + [pl.BlockSpec(memory_space=pl.ANY)],
        out_specs=tuple([hbm] * (2 * nb)),
        input_output_aliases={i: i for i in range(2 * nb)},
        compiler_params=pltpu.CompilerParams(has_side_effects=pltpu.SideEffectType.DATAFLOW_SIDE_EFFECTING),
    )(*thru, *lands, *sems, after)
    return out[:nb], out[nb:]


def _bcast_start(buf, name):
    nsem = 2 * len(FLIPS)

    def body(src, land, *rest):
        sems, token = rest[:nsem], rest[-1]
        pos = _my_pos()
        for k, f in enumerate(FLIPS):
            pltpu.make_async_remote_copy(
                src_ref=src, dst_ref=land.at[_dev_index(*pos)], send_sem=sems[2 * k], recv_sem=sems[2 * k + 1],
                device_id=_flip(pos, f), device_id_type=MESH).start()
        token[...] = jnp.zeros_like(token)

    hbm = pl.BlockSpec(memory_space=pltpu.HBM)
    sem = pl.BlockSpec(memory_space=pltpu.SEMAPHORE)
    land_shape = (N_DEV,) + buf.shape
    out = pl.pallas_call(
        body, name=name,
        out_shape=(*([pltpu.SemaphoreType.DMA(())] * nsem), pltpu.HBM(buf.shape, buf.dtype),
                   pltpu.HBM(land_shape, buf.dtype), jax.ShapeDtypeStruct((8, LANES), F32)),
        in_specs=[hbm, hbm],
        out_specs=(*([sem] * nsem), hbm, hbm, pl.BlockSpec(memory_space=pltpu.VMEM)),
        input_output_aliases={0: nsem, 1: nsem + 1},
        compiler_params=pltpu.CompilerParams(has_side_effects=pltpu.SideEffectType.DATAFLOW_SIDE_EFFECTING),
    )(pltpu.with_memory_space_constraint(buf, pltpu.HBM),
      pltpu.with_memory_space_constraint(lax.empty(land_shape, buf.dtype), pltpu.HBM))
    return out[:nsem], out[nsem], out[nsem + 1], out[-1]


def _bcast_wait(sems, thru, land, after, name):
    nsem = len(sems)

    def body(src, lnd, *rest):
        sem_refs = rest[:nsem]
        pos = _my_pos()
        for k, f in enumerate(FLIPS):
            peer = _flip(pos, f)
            cp = pltpu.make_async_remote_copy(
                src_ref=src, dst_ref=lnd.at[_dev_index(*peer)], send_sem=sem_refs[2 * k],
                recv_sem=sem_refs[2 * k + 1], device_id=peer, device_id_type=MESH)
            cp.wait_send()
            cp.wait_recv()

    hbm = pl.BlockSpec(memory_space=pltpu.HBM)
    sem = pl.BlockSpec(memory_space=pltpu.SEMAPHORE)
    sent, got = pl.pallas_call(
        body, name=name,
        out_shape=(pltpu.HBM(thru.shape, thru.dtype), pltpu.HBM(land.shape, land.dtype)),
        in_specs=[hbm, hbm] + [sem] * nsem + [pl.BlockSpec(memory_space=pl.ANY)],
        out_specs=(hbm, hbm), input_output_aliases={0: 0, 1: 1},
        compiler_params=pltpu.CompilerParams(has_side_effects=pltpu.SideEffectType.DATAFLOW_SIDE_EFFECTING),
    )(thru, land, *sems, after)
    return lax.dynamic_update_slice_in_dim(got, sent[None], _dev_index(*_my_pos()), axis=0)


def _sum_slots(v, name):
    _, r, cdim = v.shape

    def body(v_ref, o_ref):
        acc = v_ref[0]
        for s in range(1, N_DEV):
            acc = acc + v_ref[s]
        o_ref[...] = acc

    return pl.pallas_call(
        body, name=name, out_shape=jax.ShapeDtypeStruct((r, cdim), F32),
        in_specs=[_full((N_DEV, r, cdim))], out_specs=_full((r, cdim)), grid=(1,),
        compiler_params=_cparams(("arbitrary",)),
    )(v)


def _mm(a, b, dims, out_dtype, tm, tn, name):
    if dims == "nn":
        (m, k), (_, n) = a.shape, b.shape
        a_spec = pl.BlockSpec((tm, k), lambda j, i: (i, 0))
        b_spec = pl.BlockSpec((k, tn), lambda j, i: (0, j))
        dn = NN
    elif dims == "nt":
        (m, k), (n, _) = a.shape, b.shape
        a_spec = pl.BlockSpec((tm, k), lambda j, i: (i, 0))
        b_spec = pl.BlockSpec((tn, k), lambda j, i: (j, 0))
        dn = NT
    else:
        (k, m), (_, n) = a.shape, b.shape
        a_spec = pl.BlockSpec((k, tm), lambda j, i: (0, i))
        b_spec = pl.BlockSpec((k, tn), lambda j, i: (0, j))
        dn = TN
    assert m % tm == 0 and n % tn == 0, (m, tm, n, tn)

    def body(a_ref, b_ref, o_ref):
        o_ref[...] = _dot(a_ref[...], b_ref[...], dn).astype(o_ref.dtype)

    return pl.pallas_call(
        body, name=name, grid=(n // tn, m // tm),
        in_specs=[a_spec, b_spec], out_specs=pl.BlockSpec((tm, tn), lambda j, i: (i, j)),
        out_shape=jax.ShapeDtypeStruct((m, n), out_dtype),
        compiler_params=_cparams(("parallel", "parallel")),
    )(a, b)


def _tiles_2d(r, cdim):
    if r % CHUNK == 0:
        return CHUNK, cdim, True
    return r, _tile(cdim, (256, 128)), False


def _mm_sum_nn(a_list, b_list, tm, tn, name):
    n_op = len(a_list)
    m, n = a_list[0].shape[0], b_list[0].shape[1]

    def body(*refs):
        acc = _dot(refs[0][...], refs[n_op][...])
        for i in range(1, n_op):
            acc = acc + _dot(refs[i][...], refs[n_op + i][...])
        refs[2 * n_op][...] = acc

    return pl.pallas_call(
        body, name=name, grid=(n // tn, m // tm),
        in_specs=([pl.BlockSpec((tm, a.shape[1]), lambda j, i: (i, 0)) for a in a_list]
                  + [pl.BlockSpec((b.shape[0], tn), lambda j, i: (0, j)) for b in b_list]),
        out_specs=pl.BlockSpec((tm, tn), lambda j, i: (i, j)),
        out_shape=jax.ShapeDtypeStruct((m, n), F32),
        compiler_params=_cparams(("parallel", "parallel")),
    )(*a_list, *b_list)


def _tile(n, prefs):
    for t in prefs:
        if n % t == 0:
            return t
    return n


def _prenorm_fwd(head, x2, w):
    p, d = x2.shape[0] + CHUNK, x2.shape[1]

    def body(head_ref, x_ref, w_ref, u_ref):
        i = pl.program_id(0)
        h = jnp.where(i == 0, head_ref[...], x_ref[...])
        ms = jnp.mean(h * h, axis=-1, keepdims=True)
        u_ref[...] = (h * lax.rsqrt(ms + EPS) * w_ref[...]).astype(BF16)

    return pl.pallas_call(
        body, name="prenorm_fwd", grid=(p // CHUNK,),
        in_specs=[_full((CHUNK, d)), pl.BlockSpec((CHUNK, d), lambda i: (jnp.maximum(i - 1, 0), 0)), _full((1, d))],
        out_specs=pl.BlockSpec((CHUNK, d), lambda i: (i, 0)),
        out_shape=jax.ShapeDtypeStruct((p, d), BF16),
        compiler_params=_cparams(("arbitrary",)),
    )(head, x2, w)


def _prenorm_bwd(head, x2, w, du, dout):
    p, d = x2.shape[0] + CHUNK, x2.shape[1]

    def body(head_ref, x_ref, w_ref, du_ref, dout_ref, gx_ref, ghead_ref, gw_ref):
        i = pl.program_id(0)
        h = jnp.where(i == 0, head_ref[...], x_ref[...])
        rstd = lax.rsqrt(jnp.mean(h * h, axis=-1, keepdims=True) + EPS)
        xhat = h * rstd
        dub = du_ref[...]
        dxh = dub * w_ref[...]
        dh = rstd * (dxh - xhat * jnp.mean(dxh * xhat, axis=-1, keepdims=True)) + dout_ref[...]

        @pl.when(i == 0)
        def _():
            ghead_ref[...] = dh
            gw_ref[...] = jnp.zeros_like(gw_ref)

        gx_ref[...] = dh
        gw_ref[0:1, :] += jnp.sum(dub * xhat, axis=0, keepdims=True)

    return pl.pallas_call(
        body, name="prenorm_bwd", grid=(p // CHUNK,),
        in_specs=[_full((CHUNK, d)), pl.BlockSpec((CHUNK, d), lambda i: (jnp.maximum(i - 1, 0), 0)), _full((1, d)),
                  pl.BlockSpec((CHUNK, d), lambda i: (i, 0)), pl.BlockSpec((CHUNK, d), lambda i: (i, 0))],
        out_specs=[pl.BlockSpec((CHUNK, d), lambda i: (jnp.maximum(i - 1, 0), 0)), _full((CHUNK, d)), _full((8, d))],
        out_shape=[jax.ShapeDtypeStruct(x2.shape, F32), jax.ShapeDtypeStruct((CHUNK, d), F32),
                   jax.ShapeDtypeStruct((8, d), F32)],
        compiler_params=_cparams(("arbitrary",)),
    )(head, x2, w, du, dout)


def _conv_pre(ext_ref, cw_ref, cb_ref):
    pre = cb_ref[...] + cw_ref[CONV_K - 1:CONV_K, :] * ext_ref[8:8 + CHUNK, :]
    for j in range(1, CONV_K):
        pre = pre + cw_ref[CONV_K - 1 - j:CONV_K - j, :] * ext_ref[8 - j:8 - j + CHUNK, :]
    return pre


def _ssd_scalars(dtf_ref, brow_ref, alog_ref, rowmask, hs, ha, tri):
    lane = lax.broadcasted_iota(jnp.int32, (1, LANES), 1)
    is_dt = lane < hs
    is_f = (lane >= hs) & (lane < hs + ha)
    dtr = dtf_ref[...] + brow_ref[...]
    sp = _softplus(dtr)
    dt = jnp.where(is_dt, sp, 0.0) * rowmask
    logf = jnp.where(is_f, jnp.minimum(dtr, 0.0) - jnp.log(1.0 + jnp.exp(-jnp.abs(dtr))), 0.0) * rowmask
    a_row = jnp.where(is_dt, -jnp.exp(alog_ref[...]), 0.0)
    run = _dot_tri(tri, dt * a_row + logf)
    return dtr, dt, a_row, run, is_dt, is_f


def _tri_mats():
    r = lax.broadcasted_iota(jnp.int32, (CHUNK, CHUNK), 0)
    c = lax.broadcasted_iota(jnp.int32, (CHUNK, CHUNK), 1)
    return r, c


def _ssd_fwd(xbc, z, dtf, conv_w, conv_b, brow, alog, dskip_l, ssd_norm, sel_t, hs, ha):
    p, cd = xbc.shape
    ds = z.shape[1]
    ns = (cd - ds) // (2 * SSD_GROUPS)
    gw = ds // SSD_GROUPS
    nch = p // CHUNK
    hpg = hs // SSD_GROUPS

    def body(xbc_ref, halo_ref, z_ref, dtf_ref, cw_ref, cb_ref, brow_ref, alog_ref, dsk_ref, nrm_ref, selt_ref,
             y_ref, yssd_ref, hin_ref, cf_ref, pre_ref, st_ref, carry_ref, yacc_ref, xc_s, ex_s, xdtb_s, xwb_s, ext_s):
        c = pl.program_id(0)

        @pl.when(c == 0)
        def _():
            st_ref[...] = jnp.zeros_like(st_ref)
            carry_ref[...] = jnp.zeros_like(carry_ref)

        rows = lax.broadcasted_iota(jnp.int32, (CHUNK, 1), 0)
        rowmask = jnp.where((rows >= PADN) | (c > 0), 1.0, 0.0)
        ri, ci = _tri_mats()
        causal = ri >= ci
        tri = jnp.where(causal, 1.0, 0.0).astype(BF16)

        ext_s[0:8, :] = halo_ref[...].astype(F32)[HALO - 8:, :] * jnp.where(c > 0, 1.0, 0.0)
        ext_s[8:, :] = xbc_ref[...].astype(F32)
        pre = _conv_pre(ext_s, cw_ref, cb_ref)
        pre_ref[...] = pre.astype(BF16)
        xc_s[...] = pre * _sigmoid(pre) * rowmask

        dtr, dt, a_row, run, is_dt, is_f = _ssd_scalars(dtf_ref, brow_ref, alog_ref, rowmask, hs, ha, tri)
        cf = run + carry_ref[...]
        cf_ref[...] = cf
        carry_ref[...] = jnp.where(is_f, cf[CHUNK - 1:CHUNK, :], 0.0)
        cs = jnp.where(is_dt, run, 0.0)
        cl = cs[CHUNK - 1:CHUNK, :]
        selt = selt_ref[...]
        ex_s[...] = _dot_sel(jnp.exp(cs), selt)
        cdec_x = _dot_sel(jnp.broadcast_to(jnp.exp(cl), (8, LANES)), selt)[0:1, :]
        cs_t = cs.T
        xdt = xc_s[:, :ds] * _dot_sel(dt, selt)
        xdtb_s[...] = xdt.astype(BF16)
        xwb_s[...] = (xdt * _dot_sel(jnp.exp(cl - cs), selt)).astype(BF16)

        lane = lax.broadcasted_iota(jnp.int32, (1, LANES), 1)
        half0 = lane < HEAD_DIM
        for g in range(SSD_GROUPS):
            bg = xc_s[:, ds + g * ns: ds + (g + 1) * ns].astype(BF16)
            cg = xc_s[:, ds + SSD_GROUPS * ns + g * ns: ds + SSD_GROUPS * ns + (g + 1) * ns].astype(BF16)
            gm = _dot(cg, bg, NT)
            gs = slice(g * gw, (g + 1) * gw)
            stg = st_ref[:, gs]
            stg_b = stg.astype(BF16)
            hin_ref[0, :, gs] = stg_b
            yoff = _dot(cg, stg_b) * ex_s[:, gs]
            for pr in range(gw // LANES):
                sl = slice(g * gw + pr * LANES, g * gw + (pr + 1) * LANES)
                xp = xdtb_s[:, sl]
                yd = jnp.zeros((CHUNK, LANES), F32)
                for j in range(2):
                    h = g * hpg + 2 * pr + j
                    seg = cs[:, h:h + 1] - cs_t[h:h + 1, :]
                    m = jnp.where(causal, gm * jnp.exp(jnp.minimum(seg, 0.0)), 0.0).astype(BF16)
                    sel = half0 if j == 0 else jnp.logical_not(half0)
                    yd = yd + _dot(m, jnp.where(sel, xp, jnp.zeros_like(xp)))
                yacc_ref[:, sl] = yd + yoff[:, pr * LANES:(pr + 1) * LANES] + dsk_ref[:, sl] * xc_s[:, sl]
            st_ref[:, gs] = stg * cdec_x[:, gs] + _dot(bg, xwb_s[:, gs], TN)

        y = yacc_ref[...]
        y_ref[...] = y.astype(BF16)
        zf = z_ref[...].astype(F32)
        u = y * zf * _sigmoid(zf)
        for g in range(SSD_GROUPS):
            gs = slice(g * gw, (g + 1) * gw)
            ug = u[:, gs]
            ms = jnp.mean(ug * ug, axis=-1, keepdims=True)
            yssd_ref[:, gs] = (ug * lax.rsqrt(ms + EPS) * nrm_ref[:, gs]).astype(BF16)

    rb = CHUNK // HALO
    return pl.pallas_call(
        body, name="ssd_fwd", grid=(nch,),
        in_specs=[pl.BlockSpec((CHUNK, cd), lambda c: (c, 0)),
                  pl.BlockSpec((HALO, cd), lambda c: (jnp.maximum(c * rb - 1, 0), 0)),
                  pl.BlockSpec((CHUNK, ds), lambda c: (c, 0)),
                  pl.BlockSpec((CHUNK, LANES), lambda c: (c, 0)),
                  _full((CONV_K, cd)), _full((1, cd)), _full((1, LANES)), _full((1, LANES)),
                  _full((1, ds)), _full((1, ds)), _full((LANES, ds))],
        out_specs=[pl.BlockSpec((CHUNK, ds), lambda c: (c, 0)), pl.BlockSpec((CHUNK, ds), lambda c: (c, 0)),
                   pl.BlockSpec((1, ns, ds), lambda c: (c, 0, 0)), pl.BlockSpec((CHUNK, LANES), lambda c: (c, 0)),
                   pl.BlockSpec((CHUNK, cd), lambda c: (c, 0))],
        out_shape=[jax.ShapeDtypeStruct((p, ds), BF16), jax.ShapeDtypeStruct((p, ds), BF16),
                   jax.ShapeDtypeStruct((nch, ns, ds), BF16), jax.ShapeDtypeStruct((p, LANES), F32),
                   jax.ShapeDtypeStruct((p, cd), BF16)],
        scratch_shapes=[pltpu.VMEM((ns, ds), F32), pltpu.VMEM((1, LANES), F32), pltpu.VMEM((CHUNK, ds), F32),
                        pltpu.VMEM((CHUNK, cd), F32), pltpu.VMEM((CHUNK, ds), F32),
                        pltpu.VMEM((CHUNK, ds), BF16), pltpu.VMEM((CHUNK, ds), BF16),
                        pltpu.VMEM((8 + CHUNK, cd), F32)],
        compiler_params=_cparams(("arbitrary",)),
    )(xbc, xbc, z, dtf, conv_w, conv_b, brow, alog, dskip_l, ssd_norm, sel_t)


def _ssd_bwd(dyssd, y, z, xbc, pre, dtf, hin, dcf, conv_w, brow, alog, dskip_l, ssd_norm, sel_t, sel, hs, ha):
    p, cd = xbc.shape
    ds = z.shape[1]
    ns = (cd - ds) // (2 * SSD_GROUPS)
    gw = ds // SSD_GROUPS
    nch = p // CHUNK
    hpg = hs // SSD_GROUPS

    def body(dyssd_ref, y_ref, z_ref, xbc_ref, pre_ref, dtf_ref, hin_ref, dcf_ref, cw_ref, brow_ref,
             alog_ref, dsk_ref, nrm_ref, selt_ref, sel_ref,
             dxbc_ref, dz_ref, ddtf_ref, gcw_ref, gcb_ref, gnrm_ref, gsm_ref,
             dst_ref, nxt_ref, fcar_ref, gdsk_ref, dxc_ref, xc_s, dsl_s, dtx_s, ex_s, wx_s, dy_s, xdtb_s, xwb_s,
             dyb_s, dyeb_s):
        step = pl.program_id(0)
        c = nch - 1 - step

        @pl.when(step == 0)
        def _():
            dst_ref[...] = jnp.zeros_like(dst_ref)
            nxt_ref[...] = jnp.zeros_like(nxt_ref)
            fcar_ref[...] = jnp.zeros_like(fcar_ref)
            gdsk_ref[...] = jnp.zeros_like(gdsk_ref)
            gcw_ref[...] = jnp.zeros_like(gcw_ref)
            gcb_ref[...] = jnp.zeros_like(gcb_ref)
            gnrm_ref[...] = jnp.zeros_like(gnrm_ref)
            gsm_ref[...] = jnp.zeros_like(gsm_ref)

        rows = lax.broadcasted_iota(jnp.int32, (CHUNK, 1), 0)
        rowmask = jnp.where((rows >= PADN) | (c > 0), 1.0, 0.0)
        ri, ci = _tri_mats()
        causal = ri >= ci
        anti = ci >= ri
        tri = jnp.where(causal, 1.0, 0.0).astype(BF16)
        rtri = jnp.where(anti, 1.0, 0.0).astype(BF16)

        pre = pre_ref[...].astype(F32)
        sg = _sigmoid(pre)
        xc_s[...] = pre * sg * rowmask
        dsl_s[...] = sg * (1.0 + pre * (1.0 - sg)) * rowmask

        dtr, dt, a_row, run, is_dt, is_f = _ssd_scalars(dtf_ref, brow_ref, alog_ref, rowmask, hs, ha, tri)
        cs = jnp.where(is_dt, run, 0.0)
        cl = cs[CHUNK - 1:CHUNK, :]
        selt = selt_ref[...]
        selm = sel_ref[...]
        dtx_s[...] = _dot_sel(dt, selt)
        ex_s[...] = _dot_sel(jnp.exp(cs), selt)
        wx_s[...] = _dot_sel(jnp.exp(cl - cs), selt)
        cdec = jnp.exp(cl)
        cdec_x = _dot_sel(jnp.broadcast_to(cdec, (8, LANES)), selt)[0:1, :]
        cs_t = cs.T
        xdt = xc_s[:, :ds] * dtx_s[...]
        xdtb_s[...] = xdt.astype(BF16)
        xwb_s[...] = (xdt * wx_s[...]).astype(BF16)

        yv = y_ref[...].astype(F32)
        zf = z_ref[...].astype(F32)
        sz = _sigmoid(zf)
        u = yv * zf * sz
        dyo = dyssd_ref[...].astype(F32)
        du_parts = []
        for g in range(SSD_GROUPS):
            gs = slice(g * gw, (g + 1) * gw)
            ug = u[:, gs]
            rstd = lax.rsqrt(jnp.mean(ug * ug, axis=-1, keepdims=True) + EPS)
            yhat = ug * rstd
            dyg = dyo[:, gs]
            gnrm_ref[0:1, gs] += jnp.sum(dyg * yhat, axis=0, keepdims=True)
            dyh = dyg * nrm_ref[:, gs]
            du_parts.append(rstd * (dyh - yhat * jnp.mean(dyh * yhat, axis=-1, keepdims=True)))
        du = jnp.concatenate(du_parts, axis=1)
        dy = du * zf * sz
        dz_ref[...] = (du * yv * sz * (1.0 + zf * (1.0 - sz))).astype(BF16)
        dy_s[...] = dy
        dyb_s[...] = dy.astype(BF16)
        dyeb_s[...] = (dy * ex_s[...]).astype(BF16)
        gdsk_ref[...] += jnp.sum(dy * xc_s[:, :ds], axis=0, keepdims=True)
        lane = lax.broadcasted_iota(jnp.int32, (1, LANES), 1)
        half0 = lane < HEAD_DIM
        x_parts, yo_parts, t4_parts = [], [], []
        dcs = jnp.zeros((CHUNK, LANES), F32)
        for g in range(SSD_GROUPS):
            gs = slice(g * gw, (g + 1) * gw)
            bsl = slice(ds + g * ns, ds + (g + 1) * ns)
            csl = slice(ds + SSD_GROUPS * ns + g * ns, ds + SSD_GROUPS * ns + (g + 1) * ns)
            bg = xc_s[:, bsl].astype(BF16)
            cg = xc_s[:, csl].astype(BF16)
            gm = _dot(cg, bg, NT)
            gm_t = _dot(bg, cg, NT)
            stg_b = hin_ref[0, :, gs]
            dstg = dst_ref[:, gs]
            dstg_b = dstg.astype(BF16)
            t4_parts.append(jnp.sum(dstg * stg_b.astype(F32), axis=0, keepdims=True))
            zst = _dot(bg, dstg_b) * wx_s[:, gs]
            x_parts.append(xc_s[:, gs] * dtx_s[:, gs] * zst)
            yo_parts.append(dy_s[:, gs] * (_dot(cg, stg_b) * ex_s[:, gs]))
            dgsum = jnp.zeros((CHUNK, CHUNK), F32)
            dgtsum = jnp.zeros((CHUNK, CHUNK), F32)
            for pr in range(gw // LANES):
                sl = slice(g * gw + pr * LANES, g * gw + (pr + 1) * LANES)
                xp = xdtb_s[:, sl]
                dyp = dyb_s[:, sl]
                dxd = zst[:, pr * LANES:(pr + 1) * LANES]
                for j in range(2):
                    h = g * hpg + 2 * pr + j
                    sel_l = half0 if j == 0 else jnp.logical_not(half0)
                    seg = cs[:, h:h + 1] - cs_t[h:h + 1, :]
                    lm = jnp.where(causal, jnp.exp(jnp.minimum(seg, 0.0)), 0.0)
                    lmt = jnp.where(anti, jnp.exp(jnp.minimum(-seg, 0.0)), 0.0)
                    dyp_m = jnp.where(sel_l, dyp, jnp.zeros_like(dyp))
                    xp_m = jnp.where(sel_l, xp, jnp.zeros_like(xp))
                    dxd = dxd + _dot((gm_t * lmt).astype(BF16), dyp_m)
                    dg = _dot(dyp_m, xp, NT) * lm
                    dgt = _dot(xp_m, dyp, NT) * lmt
                    dgsum = dgsum + dg
                    dgtsum = dgtsum + dgt
                    qrow = (jnp.sum(dg * gm, axis=1, keepdims=True) - jnp.sum(dgt * gm_t, axis=1, keepdims=True))
                    dcs = dcs + jnp.where(lane == h, qrow, 0.0)
                dxc_ref[:, sl] = dxd
            dxc_ref[:, csl] = _dot(dgsum.astype(BF16), bg) + _dot(dyeb_s[:, gs], stg_b, NT)
            dxc_ref[:, bsl] = _dot(dgtsum.astype(BF16), cg) + _dot(xwb_s[:, gs], dstg_b, NT)
            dst_ref[:, gs] = dstg * cdec_x[:, gs] + _dot(cg, dyeb_s[:, gs], TN)

        dxdt = dxc_ref[:, :ds]
        xst = _dot_sel(jnp.concatenate(x_parts, axis=1), selm)
        yo = _dot_sel(jnp.concatenate(yo_parts, axis=1), selm)
        t4 = _dot_sel(jnp.concatenate([jnp.concatenate(t4_parts, axis=1), jnp.zeros((7, ds), F32)], axis=0), selm)
        dcl = jnp.sum(xst, axis=0, keepdims=True) + cdec * t4[0:1, :]
        dcs = dcs + yo - xst + jnp.where(rows == CHUNK - 1, dcl, 0.0)
        da_ = _dot_tri(rtri, dcs)
        ddt = _dot_sel(dxdt * xc_s[:, :ds], selm) + da_ * a_row
        dcf_blk = dcf_ref[...]
        dlogf = _dot_tri(rtri, dcf_blk) + fcar_ref[...]
        fcar_ref[...] += jnp.sum(dcf_blk, axis=0, keepdims=True)
        sgd = _sigmoid(dtr)
        ddtf = (jnp.where(is_dt, ddt * sgd, 0.0) + jnp.where(is_f, dlogf * (1.0 - sgd), 0.0)) * rowmask
        ddtf_ref[...] = ddtf
        gsm_ref[0:1, :] += jnp.sum(ddtf, axis=0, keepdims=True)
        gsm_ref[1:2, :] += jnp.sum(da_ * dt, axis=0, keepdims=True) * a_row

        dxc_ref[:, :ds] = dxdt * dtx_s[...] + dsk_ref[...] * dy_s[...]
        dpre = dxc_ref[...] * dsl_s[...]
        nxt_ref[0:CHUNK, :] = dpre
        gcb_ref[0:1, :] += jnp.sum(dpre, axis=0, keepdims=True)
        xr = xbc_ref[...].astype(F32)
        gcw_ref[CONV_K - 1:CONV_K, :] += jnp.sum(dpre * xr, axis=0, keepdims=True)
        dxr = cw_ref[CONV_K - 1:CONV_K, :] * dpre
        for j in range(1, CONV_K):
            up = nxt_ref[j:j + CHUNK, :]
            gcw_ref[CONV_K - 1 - j:CONV_K - j, :] += jnp.sum(up * xr, axis=0, keepdims=True)
            dxr = dxr + cw_ref[CONV_K - 1 - j:CONV_K - j, :] * up
        nxt_ref[CHUNK:, :] = dpre[0:8, :]
        dxbc_ref[...] = dxr.astype(BF16)

        @pl.when(step == nch - 1)
        def _():
            gsm_ref[2:3, :] = _dot_sel(jnp.broadcast_to(gdsk_ref[...], (8, ds)), selm)[0:1, :]

    rev = lambda s: nch - 1 - s
    blk = lambda w: pl.BlockSpec((CHUNK, w), lambda s: (rev(s), 0))
    return pl.pallas_call(
        body, name="ssd_bwd", grid=(nch,),
        in_specs=[blk(ds), blk(ds), blk(ds), blk(cd), blk(cd),
                  blk(LANES), pl.BlockSpec((1, ns, ds), lambda s: (rev(s), 0, 0)), blk(LANES),
                  _full((CONV_K, cd)), _full((1, LANES)), _full((1, LANES)),
                  _full((1, ds)), _full((1, ds)), _full((LANES, ds)), _full((ds, LANES))],
        out_specs=[blk(cd), blk(ds), blk(LANES), _full((8, cd)), _full((8, cd)), _full((8, ds)), _full((8, LANES))],
        out_shape=[jax.ShapeDtypeStruct((p, cd), BF16), jax.ShapeDtypeStruct((p, ds), BF16),
                   jax.ShapeDtypeStruct((p, LANES), F32), jax.ShapeDtypeStruct((8, cd), F32),
                   jax.ShapeDtypeStruct((8, cd), F32), jax.ShapeDtypeStruct((8, ds), F32),
                   jax.ShapeDtypeStruct((8, LANES), F32)],
        scratch_shapes=[pltpu.VMEM((ns, ds), F32), pltpu.VMEM((CHUNK + 8, cd), F32), pltpu.VMEM((1, LANES), F32),
                        pltpu.VMEM((1, ds), F32), pltpu.VMEM((CHUNK, cd), F32),
                        pltpu.VMEM((CHUNK, cd), F32), pltpu.VMEM((CHUNK, cd), F32),
                        pltpu.VMEM((CHUNK, ds), F32), pltpu.VMEM((CHUNK, ds), F32), pltpu.VMEM((CHUNK, ds), F32),
                        pltpu.VMEM((CHUNK, ds), F32), pltpu.VMEM((CHUNK, ds), BF16), pltpu.VMEM((CHUNK, ds), BF16),
                        pltpu.VMEM((CHUNK, ds), BF16), pltpu.VMEM((CHUNK, ds), BF16)],
        compiler_params=_cparams(("arbitrary",)),
    )(dyssd, y, z, xbc, pre, dtf, hin, dcf, conv_w, brow, alog, dskip_l, ssd_norm, sel_t, sel)


def _attn_fwd(q, k, v, ck, blk):
    p, da = q.shape
    npair, nkb = ck.shape[0], ck.shape[1]
    scale = 1.0 / math.sqrt(HEAD_DIM)

    def body(q_ref, k_ref, v_ref, ck_ref, o_ref, lse_ref):
        i = pl.program_id(1)
        lane = lax.broadcasted_iota(jnp.int32, (1, LANES), 1)
        sels = [lane < HEAD_DIM, lane >= HEAD_DIM]
        ones = [jnp.where(lane == HEAD_DIM, 1.0, 0.0).astype(BF16), jnp.where(lane == 0, 1.0, 0.0).astype(BF16)]
        qb = q_ref[...] * scale
        cmask = (lax.broadcasted_iota(jnp.int32, (blk, blk), 1) <= lax.broadcasted_iota(jnp.int32, (blk, blk), 0))

        def step(kb, carry, masked, nk=1):
            r0 = pl.multiple_of(kb * blk, blk)
            ks = k_ref[pl.ds(r0, nk * blk), :]
            vs = v_ref[pl.ds(r0, nk * blk), :]
            kk = jnp.concatenate([jnp.where(sel, ks, jnp.zeros_like(ks)) for sel in sels], axis=0)
            s_both = _dot(qb, kk, NT)
            out = []
            for j in range(2):
                m, acc = carry[2 * j], carry[2 * j + 1]
                ckr = jnp.concatenate([ck_ref[0, kb + t, j:j + 1, :] for t in range(nk)], axis=1)
                s = s_both[:, j * nk * blk:(j + 1) * nk * blk] - ckr
                if masked:
                    s = jnp.where(cmask, s, NEG)
                mn = jnp.maximum(m, jnp.max(s, axis=-1, keepdims=True))
                pr = jnp.exp(s - mn).astype(BF16)
                acc = jnp.exp(m - mn) * acc + _dot(pr, jnp.where(sels[j], vs, ones[j]))
                out += [mn, acc]
            return tuple(out)

        init = (jnp.full((blk, 1), NEG, F32), jnp.zeros((blk, LANES), F32)) * 2
        n4 = i // 4
        n2 = (i - 4 * n4) // 2
        carry = lax.fori_loop(0, n4, lambda t, c: step(4 * t, c, False, 4), init)
        carry = lax.fori_loop(0, n2, lambda t, c: step(4 * n4 + 2 * t, c, False, 2), carry)
        carry = lax.fori_loop(4 * n4 + 2 * n2, i, lambda kb, c: step(kb, c, False), carry)
        m0, a0, m1, a1 = step(i, carry, True)
        l0 = a0[:, HEAD_DIM:HEAD_DIM + 1]
        l1 = a1[:, 0:1]
        o_ref[...] = jnp.where(sels[0], a0 / l0, a1 / l1).astype(BF16)
        lse_ref[...] = jnp.where(sels[0], m0 + jnp.log(l0), m1 + jnp.log(l1))

    return pl.pallas_call(
        body, name="attn_fwd", grid=(npair, p // blk),
        in_specs=[pl.BlockSpec((blk, LANES), lambda h, i: (i, h)),
                  pl.BlockSpec((p, LANES), lambda h, i: (0, h)), pl.BlockSpec((p, LANES), lambda h, i: (0, h)),
                  pl.BlockSpec((1, nkb, 8, blk), lambda h, i: (h, 0, 0, 0))],
        out_specs=[pl.BlockSpec((blk, LANES), lambda h, i: (i, h)), pl.BlockSpec((blk, LANES), lambda h, i: (i, h))],
        out_shape=[jax.ShapeDtypeStruct((p, da), BF16), jax.ShapeDtypeStruct((p, da), F32)],
        compiler_params=_cparams(("parallel", "arbitrary")),
    )(q, k, v, ck)


def _attn_bwd(q, k, v, o, do, lse_rep, ck, blk):
    p, da = q.shape
    npair, nkb = ck.shape[0], ck.shape[1]
    nq = p // blk
    scale = 1.0 / math.sqrt(HEAD_DIM)

    def body(k_ref, v_ref, q_ref, do_ref, o_ref, lse_ref, ck_ref, dk_ref, dv_ref, dq_ref, dcs_ref, rsum_ref, dq_acc):
        jb = pl.program_id(1)

        @pl.when(jb == 0)
        def _():
            dq_acc[...] = jnp.zeros_like(dq_acc)

        ks = k_ref[...]
        vs = v_ref[...]
        lane = lax.broadcasted_iota(jnp.int32, (1, LANES), 1)
        sels = [lane < HEAD_DIM, lane >= HEAD_DIM]
        ones = [jnp.where(lane == HEAD_DIM, 1.0, 0.0).astype(BF16), jnp.where(lane == 0, 1.0, 0.0).astype(BF16)]
        kss = ks * scale
        kmo = [jnp.where(sels[j], kss, ones[j]) for j in range(2)]
        cmask = (lax.broadcasted_iota(jnp.int32, (blk, blk), 1) <= lax.broadcasted_iota(jnp.int32, (blk, blk), 0))

        def step(ib, carry, masked, nb=1):
            rows = nb * blk
            r0 = pl.multiple_of(ib * blk, blk)
            qb = q_ref[pl.ds(r0, rows), :] * scale
            dob = do_ref[pl.ds(r0, rows), :]
            prod = dob.astype(F32) * o_ref[pl.ds(r0, rows), :].astype(F32)
            out = []
            for j in range(2):
                dk, dv = carry[2 * j], carry[2 * j + 1]
                qm = jnp.where(sels[j], qb, jnp.zeros_like(qb))
                dom = jnp.where(sels[j], dob, jnp.zeros_like(dob))
                lse = lse_ref[pl.ds(r0, rows), HEAD_DIM * j:HEAD_DIM * j + 1]
                dlt = jnp.sum(jnp.where(sels[j], prod, 0.0), axis=-1, keepdims=True)
                s = _dot(qm, ks, NT) - ck_ref[0, 0, j:j + 1, :] - lse
                pm = jnp.exp(jnp.minimum(s, 0.0))
                if masked:
                    pm = jnp.where(cmask, pm, 0.0)
                ds_b = (pm * (_dot(dom, vs, NT) - dlt)).astype(BF16)
                dv = dv + _dot(pm.astype(BF16), dom, TN)
                dk = dk + _dot(ds_b, jnp.where(sels[j], qb, ones[j]), TN)
                dq_acc[pl.ds(r0, rows), LANES * j:LANES * (j + 1)] += _dot(ds_b, kmo[j])
                out += [dk, dv]
            return tuple(out)

        zero = jnp.zeros((blk, LANES), F32)
        carry = step(jb, (zero, zero, zero, zero), True)
        n4 = (nq - 1 - jb) // 4
        n2 = (nq - 1 - jb - 4 * n4) // 2
        carry = lax.fori_loop(0, n4, lambda t, c: step(jb + 1 + 4 * t, c, False, 4), carry)
        carry = lax.fori_loop(0, n2, lambda t, c: step(jb + 1 + 4 * n4 + 2 * t, c, False, 2), carry)
        dk0, dv0, dk1, dv1 = lax.fori_loop(jb + 1 + 4 * n4 + 2 * n2, nq, lambda ib, c: step(ib, c, False), carry)
        dk_ref[...] = jnp.where(sels[0], dk0, dk1).astype(BF16)
        dv_ref[...] = (dv0 + dv1).astype(BF16)
        pair8 = lambda c0, c1: jnp.where(lane == 0, c0, jnp.where(lane == 1, c1, 0.0)).T[0:8]
        dcs_ref[0] = pair8(dk0[:, HEAD_DIM:HEAD_DIM + 1], dk1[:, 0:1])

        @pl.when(jb == nkb - 1)
        def _():
            a0 = dq_acc[:, :LANES]
            a1 = dq_acc[:, LANES:]
            dq_ref[...] = jnp.where(sels[0], a0, a1).astype(BF16)
            rsum_ref[0] = pair8(a0[:, HEAD_DIM:HEAD_DIM + 1], a1[:, 0:1])

    colblk = pl.BlockSpec((blk, LANES), lambda h, j: (j, h))
    colfull = pl.BlockSpec((p, LANES), lambda h, j: (0, h))
    ckspec = pl.BlockSpec((1, 1, 8, blk), lambda h, j: (h, j, 0, 0))
    return pl.pallas_call(
        body, name="attn_bwd", grid=(npair, nkb),
        in_specs=[colblk, colblk, colfull, colfull, colfull, colfull, ckspec],
        out_specs=[colblk, colblk, colfull, pl.BlockSpec((1, 8, blk), lambda h, j: (h, 0, j)),
                   pl.BlockSpec((1, 8, p), lambda h, j: (h, 0, 0))],
        out_shape=[jax.ShapeDtypeStruct((p, da), BF16), jax.ShapeDtypeStruct((p, da), BF16),
                   jax.ShapeDtypeStruct((p, da), BF16), jax.ShapeDtypeStruct((npair, 8, p), F32),
                   jax.ShapeDtypeStruct((npair, 8, p), F32)],
        scratch_shapes=[pltpu.VMEM((p, 2 * LANES), F32)],
        compiler_params=_cparams(("parallel", "arbitrary")),
    )(k, v, q, do, o, lse_rep, ck)


def _rows3(i):
    return jnp.maximum(3 * i - 1, 0), 3 * i, 3 * i + 1


def _tail_fwd(yssd, o, zatt, graw, head, x2, tgt2, wps, wpa, wout, gate_bias, norm_post, tm):
    p, ds = yssd.shape
    da = o.shape[1]
    d = x2.shape[1]
    nsub = tm // CHUNK

    def body(yssd_ref, o_ref, zatt_ref, g_ref, head_ref, *rest):
        x_refs, t_refs = rest[:nsub], rest[nsub:2 * nsub]
        (wps_ref, wpa_ref, wout_ref, gb_ref, np_ref,
         yatt_ref, mrg_ref, a_ref, b_ref, dzo_ref, dout_ref, red_ref) = rest[2 * nsub:]
        i = pl.program_id(0)

        @pl.when(i == 0)
        def _():
            red_ref[...] = jnp.zeros_like(red_ref)

        first = jnp.where(i == 0, head_ref[...], x_refs[0][...])
        h = jnp.concatenate([first] + [r[...] for r in x_refs[1:]], axis=0)
        tgt = jnp.concatenate([r[...] for r in t_refs], axis=0)
        rows = lax.broadcasted_iota(jnp.int32, (tm, 1), 0)
        valid = jnp.where((i > 0) | (rows >= CHUNK), 1.0, 0.0)
        ob = o_ref[...].astype(F32)
        za = zatt_ref[...].astype(F32)
        yatt_b = (ob * za * _sigmoid(za)).astype(BF16)
        yatt_ref[...] = yatt_b
        a = _dot(yssd_ref[...], wps_ref[...])
        b = _dot(yatt_b, wpa_ref[...])
        a_ref[...] = a.astype(BF16)
        b_ref[...] = b.astype(BF16)
        gr = g_ref[...].astype(F32) + gb_ref[...]
        mrg_b = (_sigmoid(gr[:, :d]) * a + _sigmoid(gr[:, d:]) * b).astype(BF16)
        mrg_ref[...] = mrg_b
        zo = _dot(mrg_b, wout_ref[...])
        rstd = lax.rsqrt(jnp.mean(zo * zo, axis=-1, keepdims=True) + EPS)
        zh = zo * rstd
        npw = np_ref[...]
        err = (h + zh * npw - tgt) * valid
        dout = err * (1.0 / d)
        dout_ref[...] = dout
        dzh = dout * npw
        dzo_ref[...] = (rstd * (dzh - zh * jnp.mean(dzh * zh, axis=-1, keepdims=True))).astype(BF16)
        red_ref[0:1, :] += jnp.sum(dout * zh, axis=0, keepdims=True)
        red_ref[1:2, 0:1] += jnp.sum(jnp.sum(err * err, axis=1, keepdims=True), axis=0, keepdims=True) * (0.5 / d)

    row = lambda w: pl.BlockSpec((tm, w), lambda i: (i, 0))
    once = lambda shape: pl.BlockSpec(shape, lambda i: (0,) * len(shape), pipeline_mode=pl.Buffered(1))
    if nsub == 1:
        subs = [pl.BlockSpec((CHUNK, d), lambda i: (jnp.maximum(i - 1, 0), 0))]
    else:
        subs = [pl.BlockSpec((CHUNK, d), functools.partial(lambda i, k: (_rows3(i)[k], 0), k=k)) for k in range(3)]
    sd = jax.ShapeDtypeStruct
    return pl.pallas_call(
        body, name="tail_fwd", grid=(p // tm,),
        in_specs=[row(ds), row(da), row(da), row(2 * d), _full((CHUNK, d))] + subs + subs
                 + [once((ds, d)), once((da, d)), once((d, d)), _full((1, 2 * d)), _full((1, d))],
        out_specs=[row(da), row(d), row(d), row(d), row(d), row(d), _full((8, d))],
        out_shape=[sd((p, da), BF16), sd((p, d), BF16), sd((p, d), BF16), sd((p, d), BF16), sd((p, d), BF16),
                   sd((p, d), F32), sd((8, d), F32)],
        compiler_params=_cparams(("arbitrary",)),
    )(yssd, o, zatt, graw, head, *([x2] * nsub), *([tgt2] * nsub), wps, wpa, wout, gate_bias, norm_post)


def _tail_bwd(dzo, a_b, b_b, graw, o, zatt, wps, wpa, wout, gate_bias, tm):
    p, d = dzo.shape
    ds, da = wps.shape[0], wpa.shape[0]

    def body(dzo_ref, a_ref, b_ref, g_ref, o_ref, zatt_ref, wps_ref, wpa_ref, wout_ref, gb_ref,
             da_ref, db_ref, dg_ref, dyssd_ref, do_ref, dzatt_ref, red_ref):
        i = pl.program_id(0)

        @pl.when(i == 0)
        def _():
            red_ref[...] = jnp.zeros_like(red_ref)

        gr = g_ref[...].astype(F32) + gb_ref[...]
        gs = _sigmoid(gr[:, :d])
        ga = _sigmoid(gr[:, d:])
        dm = _dot(dzo_ref[...], wout_ref[...], NT)
        da_b = (gs * dm).astype(BF16)
        db_b = (ga * dm).astype(BF16)
        da_ref[...] = da_b
        db_ref[...] = db_b
        dgs = dm * a_ref[...].astype(F32) * gs * (1.0 - gs)
        dga = dm * b_ref[...].astype(F32) * ga * (1.0 - ga)
        dg_ref[:, :d] = dgs.astype(BF16)
        dg_ref[:, d:] = dga.astype(BF16)
        red_ref[0:1, :d] += jnp.sum(dgs, axis=0, keepdims=True)
        red_ref[0:1, d:] += jnp.sum(dga, axis=0, keepdims=True)
        dyssd_ref[...] = _dot(da_b, wps_ref[...], NT).astype(BF16)
        dya = _dot(db_b, wpa_ref[...], NT)
        ob = o_ref[...].astype(F32)
        za = zatt_ref[...].astype(F32)
        sza = _sigmoid(za)
        do_ref[...] = (dya * za * sza).astype(BF16)
        dzatt_ref[...] = (dya * ob * sza * (1.0 + za * (1.0 - sza))).astype(BF16)

    row = lambda w: pl.BlockSpec((tm, w), lambda i: (i, 0))
    once = lambda shape: pl.BlockSpec(shape, lambda i: (0,) * len(shape), pipeline_mode=pl.Buffered(1))
    sd = jax.ShapeDtypeStruct
    return pl.pallas_call(
        body, name="tail_bwd", grid=(p // tm,),
        in_specs=[row(d), row(d), row(d), row(2 * d), row(da), row(da),
                  once((ds, d)), once((da, d)), once((d, d)), _full((1, 2 * d))],
        out_specs=[row(d), row(d), row(2 * d), row(ds), row(da), row(da), _full((8, 2 * d))],
        out_shape=[sd((p, d), BF16), sd((p, d), BF16), sd((p, 2 * d), BF16), sd((p, ds), BF16), sd((p, da), BF16),
                   sd((p, da), BF16), sd((8, 2 * d), F32)],
        compiler_params=_cparams(("arbitrary",)),
    )(dzo, a_b, b_b, graw, o, zatt, wps, wpa, wout, gate_bias)


def _adamw_math(w, g, m, v):
    m2 = ADAM_B1 * m + (1.0 - ADAM_B1) * g
    v2 = ADAM_B2 * v + (1.0 - ADAM_B2) * (g * g)
    m_hat = m2 / (1.0 - ADAM_B1 ** ADAM_STEP)
    v_hat = v2 / (1.0 - ADAM_B2 ** ADAM_STEP)
    delta = -ADAM_LR * (m_hat / (jnp.sqrt(v_hat) + ADAM_EPS) + ADAM_WD * w)
    return delta, m2, v2


def _adamw_small(params, red, name):
    names = list(params)
    n = len(names)
    extra = [params[k][3] for k in names if not isinstance(params[k][3], tuple)]

    def body(*refs):
        w_refs, m_refs, v_refs = refs[:n], refs[n:2 * n], refs[2 * n:3 * n]
        red_ref = refs[3 * n]
        g_refs = iter(refs[3 * n + 1:3 * n + 1 + len(extra)])
        outs = refs[3 * n + 1 + len(extra):]
        for i, k in enumerate(names):
            where = params[k][3]
            rows, cols = w_refs[i].shape
            if isinstance(where, tuple):
                g = red_ref[where[0]:where[0] + rows, where[1]:where[1] + cols]
            else:
                g = next(g_refs)[...]
            delta, m2, v2 = _adamw_math(w_refs[i][...], g, m_refs[i][...], v_refs[i][...])
            for o, val in zip(outs[4 * i:4 * i + 4], (g, delta, m2, v2)):
                o[...] = val

    vm = pl.BlockSpec(memory_space=pltpu.VMEM)
    ws, ms, vs = ([params[k][j] for k in names] for j in range(3))
    out = pl.pallas_call(
        body, name=name,
        out_shape=[jax.ShapeDtypeStruct(w.shape, F32) for w in ws for _ in range(4)],
        in_specs=[vm] * (3 * n + 1 + len(extra)), out_specs=[vm] * (4 * n),
    )(*ws, *ms, *vs, red, *extra)
    return {k: tuple(out[4 * i:4 * i + 4]) for i, k in enumerate(names)}


def _adamw(w, g, m, v, name, parts=False, part_row0=0):
    r, cdim = w.shape
    tr, tc, by_rows = _tiles_2d(r, cdim)
    pick = (lambda i: (i, 0)) if by_rows else (lambda i: (0, i))
    assert part_row0 % tr == 0
    gpick = (lambda i: (i + part_row0 // tr, 0)) if by_rows else (lambda i: (part_row0 // tr, i))

    def body(w_ref, g_ref, m_ref, v_ref, go_ref, d_ref, mo_ref, vo_ref):
        if parts:
            g = g_ref[0].astype(F32)
            for s in range(1, g_ref.shape[0]):
                g = g + g_ref[s].astype(F32)
        else:
            g = g_ref[...]
        delta, m2, v2 = _adamw_math(w_ref[...], g, m_ref[...], v_ref[...])
        go_ref[...] = g
        d_ref[...] = delta
        mo_ref[...] = m2
        vo_ref[...] = v2

    blk = pl.BlockSpec((tr, tc), pick)
    gspec = pl.BlockSpec((g.shape[0], tr, tc), lambda i: (0,) + gpick(i)) if parts else blk
    return pl.pallas_call(
        body, name=name, grid=((r // tr) * (cdim // tc),),
        in_specs=[blk, gspec, blk, blk], out_specs=[blk] * 4,
        out_shape=[jax.ShapeDtypeStruct((r, cdim), F32)] * 4,
        compiler_params=_cparams(("parallel",)),
    )(w, g, m, v)


def _pad_cols(a, width):
    return jnp.pad(a, ((0, 0), (0, width - a.shape[1])))


def _pack_small_shard(conv_w_sh, meta_sh, width):
    return jnp.concatenate([_pad_cols(conv_w_sh, width), jnp.zeros((4, width), F32), _pad_cols(meta_sh, width)], axis=0)


def _pack_small_rep(norm_pre, norm_post, gate_bias, ssd_norm, conv_b, misc, width):
    rows = [norm_pre, norm_post, gate_bias, ssd_norm, conv_b, misc]
    return jnp.concatenate([_pad_cols(r, width) for r in rows] + [jnp.zeros((2, width), F32)], axis=0)


def kernel(x, meta_tokens, norm_pre, w_in, conv_w, conv_b, dt_bias, a_log, d_skip, ssd_norm, fgate_bias, gate_bias, w_proj_ssd, w_proj_att, w_out, norm_post, loss_target, m_meta_tokens, m_norm_pre, m_w_in, m_conv_w, m_conv_b, m_dt_bias, m_a_log, m_d_skip, m_ssd_norm, m_fgate_bias, m_gate_bias, m_w_proj_ssd, m_w_proj_att, m_w_out, m_norm_post, v_meta_tokens, v_norm_pre, v_w_in, v_conv_w, v_conv_b, v_dt_bias, v_a_log, v_d_skip, v_ssd_norm, v_fgate_bias, v_gate_bias, v_w_proj_ssd, v_w_proj_att, v_w_out, v_norm_post):
    seq, d = x.shape[1], x.shape[2]
    p = seq + CHUNK
    hs, ha = dt_bias.shape[1], fgate_bias.shape[1]
    ds, cd = ssd_norm.shape[1], conv_b.shape[1]
    da = ha * HEAD_DIM
    nc8 = w_in.shape[2]
    cws = cd // N_DEV
    msh = d // N_DEV
    r1, r2, r3 = ds // N_DEV, da // N_DEV, d // N_DEV
    me = _dev_index(*_my_pos())
    x2, tgt2 = x[0], loss_target[0]

    win_sh = jnp.transpose(w_in[0]).astype(BF16)
    rows_sh = jnp.concatenate([w_proj_ssd[0], w_proj_att[0], w_out[0]], axis=0).astype(BF16)
    small_sh = _pack_small_shard(conv_w[0], meta_tokens, cws)
    win_all, small_all = _all_gather([win_sh, small_sh], "gather_weights")
    rows_sh, win_all = lax.optimization_barrier((rows_sh, win_all))
    rows_sems, rows_thru, rows_land, rows_token = _bcast_start(rows_sh, "gather_rows_start")
    cuts = [0, ds, ds + cd, ds + cd + hs, ds + cd + hs + da, ds + cd + hs + 2 * da, ds + cd + hs + 3 * da,
            ds + cd + hs + 4 * da, ds + cd + hs + 4 * da + ha, ds + cd + hs + 4 * da + ha + 2 * d]

    def piece_rows(r0, r1):
        parts = [win_all[s, max(r0, s * nc8) - s * nc8:min(r1, (s + 1) * nc8) - s * nc8]
                 for s in range(N_DEV) if max(r0, s * nc8) < min(r1, (s + 1) * nc8)]
        return parts[0] if len(parts) == 1 else jnp.concatenate(parts, axis=0)

    w_z, w_xbc, w_dt, w_zatt, w_q, w_k, w_v, w_f, w_g = [piece_rows(cuts[i], cuts[i + 1]) for i in range(9)]
    w_dtf = jnp.concatenate([w_dt, w_f, jnp.zeros((LANES - hs - ha, d), BF16)], axis=0)
    conv_w_full = jnp.transpose(small_all[:, 0:CONV_K, :], (1, 0, 2)).reshape(CONV_K, cd)
    meta_full = jnp.transpose(small_all[:, 8:8 + N_META, :msh], (1, 0, 2)).reshape(N_META, d)
    head = jnp.concatenate([jnp.zeros((PADN, d), F32), meta_full + rows_token[0:1, 0:1]], axis=0)

    u = _prenorm_fwd(head, x2, norm_pre)
    tm = _att_block(p)
    seg_w = [w_z, w_xbc, w_zatt, w_q, w_k, w_v, w_g]
    zs, xbc, zatt, q, k, v, graw = [
        _mm(u, w, "nt", BF16, _tile(p, (1408, tm)), _tile(w.shape[0], (1024, 512, 256, 128)), "inproj_%d" % i)
        for i, w in enumerate(seg_w)]
    dtf = _mm(u, w_dtf, "nt", F32, _tile(p, (1408, tm)), LANES, "inproj_dtf")

    brow = jnp.concatenate([dt_bias, fgate_bias, jnp.zeros((1, LANES - hs - ha), F32)], axis=1)
    alog_row = _pad_cols(a_log, LANES)
    dskip_l = jnp.repeat(d_skip, HEAD_DIM, axis=1)
    sel_t = (lax.broadcasted_iota(jnp.int32, (LANES, ds), 1) // HEAD_DIM
             == lax.broadcasted_iota(jnp.int32, (LANES, ds), 0)).astype(BF16)
    sel = sel_t.T
    y, yssd, hin, cf, pre = _ssd_fwd(xbc, zs, dtf, conv_w_full, conv_b, brow, alog_row, dskip_l, ssd_norm, sel_t, hs, ha)

    blk = _att_block(p)
    nkb, npair = p // blk, ha // 2
    cum = jnp.where(lax.broadcasted_iota(jnp.int32, (p, 1), 0) < PADN, -NEG, cf[:, hs:hs + ha])
    ck = jnp.transpose(cum.T.reshape(npair, 2, nkb, blk), (0, 2, 1, 3))
    ck = jnp.pad(ck, ((0, 0), (0, 0), (0, 6), (0, 0)))
    o, lse_rep = _attn_fwd(q, k, v, ck, blk)

    rows_all = _bcast_wait(rows_sems, rows_thru, rows_land, lse_rep, "gather_rows_wait")
    wps = rows_all[:, :r1].reshape(ds, d)
    wpa = rows_all[:, r1:r1 + r2].reshape(da, d)
    wout = rows_all[:, r1 + r2:].reshape(d, d)

    yatt, mrg, a_b, b_b, dzo, dout, red_fwd = _tail_fwd(
        yssd, o, zatt, graw, head, x2, tgt2, wps, wpa, wout, gate_bias, norm_post, tm)
    da_, db_, dgraw, dyssd, d_o, dzatt, red_bwd = _tail_bwd(dzo, a_b, b_b, graw, o, zatt, wps, wpa, wout, gate_bias, tm)

    tw = _tile(d, (512, 256, 128))
    g_wout = _mm(mrg, dzo, "tn", BF16, tw, d, "wgrad_out")
    g_wps = _mm(yssd, da_, "tn", BF16, _tile(ds, (512, 256, 128)), d, "wgrad_ps")
    g_wpa = _mm(yatt, db_, "tn", BF16, _tile(da, (512, 256, 128)), d, "wgrad_pa")

    dk, dv, dq, dcs, rsum = _attn_bwd(q, k, v, o, d_o, lse_rep, ck, blk)
    dcum = (rsum - dcs)[:, 0:2, :].reshape(ha, p).T
    dcf = jnp.pad(dcum, ((0, 0), (hs, LANES - hs - ha)))
    dxbc, dzs, ddtf, gcw, gcb, gnrm, gsm = _ssd_bwd(
        dyssd, y, zs, xbc, pre, dtf, hin, dcf, conv_w_full, brow, alog_row, dskip_l, ssd_norm, sel_t, sel, hs, ha)
    ddtf_b = ddtf.astype(BF16)

    dsegs = [dzs, dxbc, dzatt, dq, dk, dv, dgraw, ddtf_b]
    gsegs = [_mm(dsg, u, "tn", BF16, _tile(dsg.shape[1], (512, 256, 128)), d, "wgrad_in_%d" % i)
             for i, dsg in enumerate(dsegs)]
    g_z, g_xbc, g_zatt, g_q, g_k, g_v, g_g, g_dtf = gsegs
    gw_full = jnp.concatenate([g_z, g_xbc, g_dtf[:hs], g_zatt, g_q, g_k, g_v, g_dtf[hs:hs + ha], g_g], axis=0)
    gwin_parts = gw_full.reshape(N_DEV, nc8, d)
    grows_parts = jnp.concatenate([g_wps.reshape(N_DEV, r1, d), g_wpa.reshape(N_DEV, r2, d),
                                   g_wout.reshape(N_DEV, r3, d)], axis=1)

    core = lax.axis_index("c").astype(jnp.int32).reshape(1)
    sib_win, sib_rows = _exchange_sibling([gwin_parts, grows_parts], "scatter_grads_sibling")
    chip_win = _pair_add(gwin_parts, sib_win, core, "pair_add_w_in")
    chip_rows = _pair_add(grows_parts, sib_rows, core, "pair_add_rows")
    sems, thru, lands, token = _exchange_chips_start([chip_win, chip_rows], "scatter_grads_start")
    dsegs_after = dsegs[:-1] + [ddtf_b + token[0:1, 0:1].astype(BF16)]
    du = _mm_sum_nn(dsegs_after, seg_w + [w_dtf], tm, _tile(d, (512, 256, 128)), "dgrad_in")
    gx, ghead, gnp = _prenorm_bwd(head, x2, norm_pre, du, dout)
    sent, got = _exchange_chips_wait(sems, thru, lands, gnp, "scatter_grads_wait")
    chip = me // 2
    recv_win, recv_rows = [lax.dynamic_update_slice_in_dim(g, lax.dynamic_slice_in_dim(s, chip, 1, axis=0), chip, axis=0)
                           for g, s in zip(got, sent)]
    gmisc = jnp.concatenate([gsm[0:1], gsm[1:2], gsm[2:3], _pad_cols(red_fwd[1:2, 0:1], LANES)], axis=1)
    small_g = jnp.concatenate([
        _pack_small_rep(gnp[0:1], red_fwd[0:1], red_bwd[0:1], gnrm[0:1], gcb[0:1], gmisc, cd),
        _pad_cols(gcw[0:CONV_K], cd), jnp.zeros((4, cd), F32), _pad_cols(ghead[PADN:], cd)], axis=0)
    sg_sems, sg_thru, sg_land, sg_token = _bcast_start(small_g, "reduce_small_start")

    upd_in = _adamw(jnp.transpose(w_in[0]) + sg_token[0:1, 0:1], recv_win, jnp.transpose(m_w_in[0]),
                    jnp.transpose(v_w_in[0]), "adamw_w_in", parts=True)
    upd_ps = _adamw(w_proj_ssd[0] + sg_token[0:1, 0:1], recv_rows, m_w_proj_ssd[0], v_w_proj_ssd[0],
                    "adamw_w_proj_ssd", parts=True, part_row0=0)
    upd_pa = _adamw(w_proj_att[0], recv_rows, m_w_proj_att[0], v_w_proj_att[0], "adamw_w_proj_att", parts=True,
                    part_row0=r1)
    upd_out = _adamw(w_out[0], recv_rows, m_w_out[0], v_w_out[0], "adamw_w_out", parts=True, part_row0=r1 + r2)
    all_done = upd_in[1][0:8, 0:LANES] + upd_ps[1][0:8, 0:LANES] + upd_pa[1][0:8, 0:LANES] + upd_out[1][0:8, 0:LANES]
    red = _sum_slots(_bcast_wait(sg_sems, sg_thru, sg_land, all_done, "reduce_small_wait"), "reduce_small_sum")
    loss = red[5, 3 * LANES]
    g_conv_w = lax.dynamic_slice_in_dim(red[8:8 + CONV_K], me * cws, cws, axis=1)
    g_meta = lax.dynamic_slice_in_dim(red[16:16 + N_META, :d], me * msh, msh, axis=1)
    small = {
        "meta_tokens": (meta_tokens, m_meta_tokens, v_meta_tokens, g_meta),
        "norm_pre": (norm_pre, m_norm_pre, v_norm_pre, (0, 0)),
        "conv_w": (conv_w[0], m_conv_w[0], v_conv_w[0], g_conv_w),
        "conv_b": (conv_b, m_conv_b, v_conv_b, (4, 0)),
        "dt_bias": (dt_bias, m_dt_bias, v_dt_bias, (5, 0)),
        "a_log": (a_log, m_a_log, v_a_log, (5, LANES)),
        "d_skip": (d_skip, m_d_skip, v_d_skip, (5, 2 * LANES)),
        "ssd_norm": (ssd_norm, m_ssd_norm, v_ssd_norm, (3, 0)),
        "fgate_bias": (fgate_bias, m_fgate_bias, v_fgate_bias, (5, hs)),
        "gate_bias": (gate_bias, m_gate_bias, v_gate_bias, (2, 0)),
        "norm_post": (norm_post, m_norm_post, v_norm_post, (1, 0)),
    }
    upd_small = _adamw_small(small, red, "adamw_small")

    def leaves(i):
        sm = {k: v[i] for k, v in upd_small.items()}
        return [sm["meta_tokens"], sm["norm_pre"], jnp.transpose(upd_in[i])[None], sm["conv_w"][None], sm["conv_b"],
                sm["dt_bias"], sm["a_log"], sm["d_skip"], sm["ssd_norm"], sm["fgate_bias"], sm["gate_bias"],
                upd_ps[i][None], upd_pa[i][None], upd_out[i][None], sm["norm_post"]]

    return tuple([loss, gx[None]] + leaves(0) + leaves(1) + leaves(2) + leaves(3))
```

```python
import functools
import math

import jax
import jax.numpy as jnp
from jax import lax
from jax.experimental import pallas as pl
from jax.experimental.pallas import tpu as pltpu

F32 = jnp.float32
BF16 = jnp.bfloat16

N_DEV = 8
N_META = 16
CHUNK = 128
PADN = CHUNK - N_META
HEAD_DIM = 64
SSD_GROUPS = 4
CONV_K = 4
EPS = 1e-6
NEG = -1e30
LANES = 128
HALO = 16

ADAM_LR = 0.001
ADAM_B1 = 0.9
ADAM_B2 = 0.999
ADAM_EPS = 1e-08
ADAM_WD = 0.01
ADAM_STEP = 10

VMEM_LIMIT = 56 * 1024 * 1024

NN = (((1,), (0,)), ((), ()))
NT = (((1,), (1,)), ((), ()))
TN = (((0,), (0,)), ((), ()))
MESH = pl.DeviceIdType.MESH


def _dot(a, b, dims=NN):
    return lax.dot_general(a, b, dims, preferred_element_type=F32)


def _split2(x):
    hi = x.astype(BF16)
    lo = (x - hi.astype(F32)).astype(BF16)
    return hi, lo


def _dot_sel(x, sel):
    hi, lo = _split2(x)
    return _dot(hi, sel) + _dot(lo, sel)


def _dot_tri(tri, x):
    h1 = x.astype(BF16)
    r1 = x - h1.astype(F32)
    h2 = r1.astype(BF16)
    h3 = (r1 - h2.astype(F32)).astype(BF16)
    return _dot(tri, h1) + _dot(tri, h2) + _dot(tri, h3)


def _sigmoid(x):
    return 1.0 / (1.0 + jnp.exp(-x))


def _softplus(x):
    return jnp.maximum(x, 0.0) + jnp.log(1.0 + jnp.exp(-jnp.abs(x)))


def _cparams(sem=None, vmem=VMEM_LIMIT):
    kw = {"vmem_limit_bytes": vmem}
    if sem is not None:
        kw["dimension_semantics"] = sem
    return pltpu.CompilerParams(**kw)


def _full(shape):
    nd = len(shape)
    return pl.BlockSpec(shape, lambda *_: (0,) * nd)


def _att_block(p):
    return 384 if p % 384 == 0 else CHUNK


def _my_pos():
    return lax.axis_index("x"), lax.axis_index("y"), lax.axis_index("c")


def _dev_index(x, y, c):
    return 4 * x + 2 * y + c


FLIPS = [(fx, fy, fc) for fx in (0, 1) for fy in (0, 1) for fc in (0, 1)][1:]


def _flip(pos, f):
    return tuple((1 - p) if fi else p for p, fi in zip(pos, f))


def _all_gather(bufs, name):
    nb = len(bufs)

    def body(*refs):
        ins, outs = refs[:nb], refs[nb:2 * nb]
        send_sems, recv_sems, local_sems = refs[2 * nb:]
        x, y, c = _my_pos()
        me = _dev_index(x, y, c)
        sibling = (x, y, 1 - c)
        near = [(1 - x, y), (x, 1 - y)]
        far = (1 - x, 1 - y)
        relay_from = (c * (1 - x) + (1 - c) * x, c * y + (1 - c) * (1 - y))
        relay_to = (c * x + (1 - c) * (1 - x), c * (1 - y) + (1 - c) * y)

        def copy(b, k, block_idx, to, src=None):
            dst = outs[b].at[block_idx]
            return pltpu.make_async_remote_copy(
                src_ref=dst if src is None else src, dst_ref=dst,
                send_sem=send_sems.at[b, k], recv_sem=recv_sems.at[b, k],
                device_id=to, device_id_type=MESH)

        started = []
        for b in range(nb):
            mine = pltpu.make_async_copy(ins[b], outs[b].at[me], local_sems.at[b])
            mine.start()
            started.append(mine)
        sent = []
        for b in range(nb):
            sent.append(copy(b, 0, me, sibling, src=ins[b]))
            for j, chip in enumerate(near):
                sent.append(copy(b, 1 + j, me, (chip[0], chip[1], c), src=ins[b]))
        for cp in sent:
            cp.start()
        for j, chip in enumerate(near):
            blk = _dev_index(chip[0], chip[1], c)
            for b in range(nb):
                copy(b, 1 + j, blk, (x, y, c)).wait_recv()
                sent.append(copy(b, 4 + j, blk, sibling))
                sent[-1].start()
        for b in range(nb):
            sent.append(copy(b, 3, _dev_index(relay_from[0], relay_from[1], c), (relay_to[0], relay_to[1], c)))
            sent[-1].start()
        blk = _dev_index(far[0], far[1], c)
        for b in range(nb):
            copy(b, 3, blk, (x, y, c)).wait_recv()
            sent.append(copy(b, 6, blk, sibling))
            sent[-1].start()
        for b in range(nb):
            copy(b, 0, _dev_index(x, y, 1 - c), (x, y, c)).wait_recv()
        for j, chip in enumerate(near + [far]):
            blk = _dev_index(chip[0], chip[1], 1 - c)
            for b in range(nb):
                copy(b, 4 + j, blk, (x, y, c)).wait_recv()
        for cp in sent:
            cp.wait_send()
        for mine in started:
            mine.wait()

    any_spec = pl.BlockSpec(memory_space=pl.ANY)
    return pl.pallas_call(
        body, name=name,
        out_shape=[jax.ShapeDtypeStruct((N_DEV,) + b.shape, b.dtype) for b in bufs],
        in_specs=[any_spec] * nb, out_specs=[any_spec] * nb,
        scratch_shapes=[pltpu.SemaphoreType.DMA((nb, 7)), pltpu.SemaphoreType.DMA((nb, 7)),
                        pltpu.SemaphoreType.DMA((nb,))],
    )(*bufs)


N_CHIP = 4
CHIP_FLIPS = [(1, 0), (0, 1), (1, 1)]


def _exchange_sibling(bufs, name):
    nb = len(bufs)

    def body(*refs):
        ins, outs = refs[:nb], refs[nb:2 * nb]
        send_sems, recv_sems = refs[2 * nb:]
        x, y, c = _my_pos()

        def copy(b, k):
            return pltpu.make_async_remote_copy(
                src_ref=ins[b].at[2 * k + (1 - c)], dst_ref=outs[b].at[k],
                send_sem=send_sems.at[b, k], recv_sem=recv_sems.at[b, k],
                device_id=(x, y, 1 - c), device_id_type=MESH)

        cps = [copy(b, k) for b in range(nb) for k in range(N_CHIP)]
        for cp in cps:
            cp.start()
        for cp in cps:
            cp.wait()

    any_spec = pl.BlockSpec(memory_space=pl.ANY)
    return pl.pallas_call(
        body, name=name,
        out_shape=[jax.ShapeDtypeStruct((N_CHIP,) + b.shape[1:], b.dtype) for b in bufs],
        in_specs=[any_spec] * nb, out_specs=[any_spec] * nb,
        scratch_shapes=[pltpu.SemaphoreType.DMA((nb, N_CHIP)), pltpu.SemaphoreType.DMA((nb, N_CHIP))],
    )(*bufs)


def _pair_add(mine, recv, core, name):
    _, r, cdim = mine.shape
    tr, tc, by_rows = _tiles_2d(r, cdim)
    pick = (lambda i: (i, 0)) if by_rows else (lambda i: (0, i))

    def body(core_ref, a_ref, b_ref, o_ref):
        o_ref[0] = (a_ref[0].astype(F32) + b_ref[0].astype(F32)).astype(o_ref.dtype)

    return pl.pallas_call(
        body, name=name,
        grid_spec=pltpu.PrefetchScalarGridSpec(
            num_scalar_prefetch=1, grid=(N_CHIP, (r // tr) * (cdim // tc)),
            in_specs=[pl.BlockSpec((1, tr, tc), lambda k, i, core_ref: (2 * k + core_ref[0],) + pick(i)),
                      pl.BlockSpec((1, tr, tc), lambda k, i, core_ref: (k,) + pick(i))],
            out_specs=pl.BlockSpec((1, tr, tc), lambda k, i, core_ref: (k,) + pick(i))),
        out_shape=jax.ShapeDtypeStruct((N_CHIP, r, cdim), mine.dtype),
        compiler_params=_cparams(("parallel", "parallel")),
    )(core, mine, recv)


def _chip_peer(x, y, f):
    return ((1 - x) if f[0] else x), ((1 - y) if f[1] else y)


def _exchange_chips_start(bufs, name):
    nb = len(bufs)
    nsem = 2 * 3 * nb

    def body(*refs):
        ins, lands = refs[:nb], refs[nb:2 * nb]
        sems = refs[2 * nb:2 * nb + nsem]
        token = refs[-1]
        x, y, c = _my_pos()
        for b in range(nb):
            for j, f in enumerate(CHIP_FLIPS):
                px, py = _chip_peer(x, y, f)
                pltpu.make_async_remote_copy(
                    src_ref=ins[b].at[2 * px + py], dst_ref=lands[b].at[2 * x + y],
                    send_sem=sems[2 * (3 * b + j)], recv_sem=sems[2 * (3 * b + j) + 1],
                    device_id=(px, py, c), device_id_type=MESH).start()
        token[...] = jnp.zeros_like(token)

    hbm = pl.BlockSpec(memory_space=pltpu.HBM)
    sem = pl.BlockSpec(memory_space=pltpu.SEMAPHORE)
    out = pl.pallas_call(
        body, name=name,
        out_shape=(*([pltpu.SemaphoreType.DMA(())] * nsem),
                   *[pltpu.HBM(b.shape, b.dtype) for b in bufs], *[pltpu.HBM(b.shape, b.dtype) for b in bufs],
                   jax.ShapeDtypeStruct((8, LANES), F32)),
        in_specs=[hbm] * (2 * nb),
        out_specs=(*([sem] * nsem), *([hbm] * (2 * nb)), pl.BlockSpec(memory_space=pltpu.VMEM)),
        input_output_aliases={i: nsem + i for i in range(2 * nb)},
        compiler_params=pltpu.CompilerParams(has_side_effects=pltpu.SideEffectType.DATAFLOW_SIDE_EFFECTING),
    )(*[pltpu.with_memory_space_constraint(b, pltpu.HBM) for b in bufs],
      *[pltpu.with_memory_space_constraint(lax.empty(b.shape, b.dtype), pltpu.HBM) for b in bufs])
    return out[:nsem], out[nsem:nsem + nb], out[nsem + nb:nsem + 2 * nb], out[-1]


def _exchange_chips_wait(sems, thru, lands, after, name):
    nb = len(thru)
    nsem = len(sems)

    def body(*refs):
        ins, lnd = refs[:nb], refs[nb:2 * nb]
        sem_refs = refs[2 * nb:2 * nb + nsem]
        x, y, c = _my_pos()
        for b in range(nb):
            for j, f in enumerate(CHIP_FLIPS):
                px, py = _chip_peer(x, y, f)
                cp = pltpu.make_async_remote_copy(
                    src_ref=ins[b].at[2 * px + py], dst_ref=lnd[b].at[2 * px + py],
                    send_sem=sem_refs[2 * (3 * b + j)], recv_sem=sem_refs[2 * (3 * b + j) + 1],
                    device_id=(px, py, c), device_id_type=MESH)
                cp.wait_send()
                cp.wait_recv()

    hbm = pl.BlockSpec(memory_space=pltpu.HBM)
    sem = pl.BlockSpec(memory_space=pltpu.SEMAPHORE)
    out = pl.pallas_call(
        body, name=name,
        out_shape=tuple([pltpu.HBM(b.shape, b.dtype) for b in thru] + [pltpu.HBM(b.shape, b.dtype) for b in lands]),
        in_specs=[hbm] * (2 * nb) + [sem] * nsem + [pl.BlockSpec(memory_space=pl.ANY)],
        out_specs=tuple([hbm] * (2 * nb)),
        input_output_aliases={i: i for i in range(2 * nb)},
        compiler_params=pltpu.CompilerParams(has_side_effects=pltpu.SideEffectType.DATAFLOW_SIDE_EFFECTING),
    )(*thru, *lands, *sems, after)
    return out[:nb], out[nb:]


def _bcast_start(buf, name):
    nsem = 2 * len(FLIPS)

    def body(src, land, *rest):
        sems, token = rest[:nsem], rest[-1]
        pos = _my_pos()
        for k, f in enumerate(FLIPS):
            pltpu.make_async_remote_copy(
                src_ref=src, dst_ref=land.at[_dev_index(*pos)], send_sem=sems[2 * k], recv_sem=sems[2 * k + 1],
                device_id=_flip(pos, f), device_id_type=MESH).start()
        token[...] = jnp.zeros_like(token)

    hbm = pl.BlockSpec(memory_space=pltpu.HBM)
    sem = pl.BlockSpec(memory_space=pltpu.SEMAPHORE)
    land_shape = (N_DEV,) + buf.shape
    out = pl.pallas_call(
        body, name=name,
        out_shape=(*([pltpu.SemaphoreType.DMA(())] * nsem), pltpu.HBM(buf.shape, buf.dtype),
                   pltpu.HBM(land_shape, buf.dtype), jax.ShapeDtypeStruct((8, LANES), F32)),
        in_specs=[hbm, hbm],
        out_specs=(*([sem] * nsem), hbm, hbm, pl.BlockSpec(memory_space=pltpu.VMEM)),
        input_output_aliases={0: nsem, 1: nsem + 1},
        compiler_params=pltpu.CompilerParams(has_side_effects=pltpu.SideEffectType.DATAFLOW_SIDE_EFFECTING),
    )(pltpu.with_memory_space_constraint(buf, pltpu.HBM),
      pltpu.with_memory_space_constraint(lax.empty(land_shape, buf.dtype), pltpu.HBM))
    return out[:nsem], out[nsem], out[nsem + 1], out[-1]


def _bcast_wait(sems, thru, land, after, name):
    nsem = len(sems)

    def body(src, lnd, *rest):
        sem_refs = rest[:nsem]
        pos = _my_pos()
        for k, f in enumerate(FLIPS):
            peer = _flip(pos, f)
            cp = pltpu.make_async_remote_copy(
                src_ref=src, dst_ref=lnd.at[_dev_index(*peer)], send_sem=sem_refs[2 * k],
                recv_sem=sem_refs[2 * k + 1], device_id=peer, device_id_type=MESH)
            cp.wait_send()
            cp.wait_recv()

    hbm = pl.BlockSpec(memory_space=pltpu.HBM)
    sem = pl.BlockSpec(memory_space=pltpu.SEMAPHORE)
    sent, got = pl.pallas_call(
        body, name=name,
        out_shape=(pltpu.HBM(thru.shape, thru.dtype), pltpu.HBM(land.shape, land.dtype)),
        in_specs=[hbm, hbm] + [sem] * nsem + [pl.BlockSpec(memory_space=pl.ANY)],
        out_specs=(hbm, hbm), input_output_aliases={0: 0, 1: 1},
        compiler_params=pltpu.CompilerParams(has_side_effects=pltpu.SideEffectType.DATAFLOW_SIDE_EFFECTING),
    )(thru, land, *sems, after)
    return lax.dynamic_update_slice_in_dim(got, sent[None], _dev_index(*_my_pos()), axis=0)


def _sum_slots(v, name):
    _, r, cdim = v.shape

    def body(v_ref, o_ref):
        acc = v_ref[0]
        for s in range(1, N_DEV):
            acc = acc + v_ref[s]
        o_ref[...] = acc

    return pl.pallas_call(
        body, name=name, out_shape=jax.ShapeDtypeStruct((r, cdim), F32),
        in_specs=[_full((N_DEV, r, cdim))], out_specs=_full((r, cdim)), grid=(1,),
        compiler_params=_cparams(("arbitrary",)),
    )(v)


def _mm(a, b, dims, out_dtype, tm, tn, name):
    if dims == "nn":
        (m, k), (_, n) = a.shape, b.shape
        a_spec = pl.BlockSpec((tm, k), lambda j, i: (i, 0))
        b_spec = pl.BlockSpec((k, tn), lambda j, i: (0, j))
        dn = NN
    elif dims == "nt":
        (m, k), (n, _) = a.shape, b.shape
        a_spec = pl.BlockSpec((tm, k), lambda j, i: (i, 0))
        b_spec = pl.BlockSpec((tn, k), lambda j, i: (j, 0))
        dn = NT
    else:
        (k, m), (_, n) = a.shape, b.shape
        a_spec = pl.BlockSpec((k, tm), lambda j, i: (0, i))
        b_spec = pl.BlockSpec((k, tn), lambda j, i: (0, j))
        dn = TN
    assert m % tm == 0 and n % tn == 0, (m, tm, n, tn)

    def body(a_ref, b_ref, o_ref):
        o_ref[...] = _dot(a_ref[...], b_ref[...], dn).astype(o_ref.dtype)

    return pl.pallas_call(
        body, name=name, grid=(n // tn, m // tm),
        in_specs=[a_spec, b_spec], out_specs=pl.BlockSpec((tm, tn), lambda j, i: (i, j)),
        out_shape=jax.ShapeDtypeStruct((m, n), out_dtype),
        compiler_params=_cparams(("parallel", "parallel")),
    )(a, b)


def _tiles_2d(r, cdim):
    if r % CHUNK == 0:
        return CHUNK, cdim, True
    return r, _tile(cdim, (256, 128)), False


def _mm_sum_nn(a_list, b_list, tm, tn, name):
    n_op = len(a_list)
    m, n = a_list[0].shape[0], b_list[0].shape[1]

    def body(*refs):
        acc = _dot(refs[0][...], refs[n_op][...])
        for i in range(1, n_op):
            acc = acc + _dot(refs[i][...], refs[n_op + i][...])
        refs[2 * n_op][...] = acc

    return pl.pallas_call(
        body, name=name, grid=(n // tn, m // tm),
        in_specs=([pl.BlockSpec((tm, a.shape[1]), lambda j, i: (i, 0)) for a in a_list]
                  + [pl.BlockSpec((b.shape[0], tn), lambda j, i: (0, j)) for b in b_list]),
        out_specs=pl.BlockSpec((tm, tn), lambda j, i: (i, j)),
        out_shape=jax.ShapeDtypeStruct((m, n), F32),
        compiler_params=_cparams(("parallel", "parallel")),
    )(*a_list, *b_list)


def _tile(n, prefs):
    for t in prefs:
        if n % t == 0:
            return t
    return n


def _prenorm_fwd(head, x2, w):
    p, d = x2.shape[0] + CHUNK, x2.shape[1]

    def body(head_ref, x_ref, w_ref, u_ref):
        i = pl.program_id(0)
        h = jnp.where(i == 0, head_ref[...], x_ref[...])
        ms = jnp.mean(h * h, axis=-1, keepdims=True)
        u_ref[...] = (h * lax.rsqrt(ms + EPS) * w_ref[...]).astype(BF16)

    return pl.pallas_call(
        body, name="prenorm_fwd", grid=(p // CHUNK,),
        in_specs=[_full((CHUNK, d)), pl.BlockSpec((CHUNK, d), lambda i: (jnp.maximum(i - 1, 0), 0)), _full((1, d))],
        out_specs=pl.BlockSpec((CHUNK, d), lambda i: (i, 0)),
        out_shape=jax.ShapeDtypeStruct((p, d), BF16),
        compiler_params=_cparams(("arbitrary",)),
    )(head, x2, w)


def _prenorm_bwd(head, x2, w, du, dout):
    p, d = x2.shape[0] + CHUNK, x2.shape[1]

    def body(head_ref, x_ref, w_ref, du_ref, dout_ref, gx_ref, ghead_ref, gw_ref):
        i = pl.program_id(0)
        h = jnp.where(i == 0, head_ref[...], x_ref[...])
        rstd = lax.rsqrt(jnp.mean(h * h, axis=-1, keepdims=True) + EPS)
        xhat = h * rstd
        dub = du_ref[...]
        dxh = dub * w_ref[...]
        dh = rstd * (dxh - xhat * jnp.mean(dxh * xhat, axis=-1, keepdims=True)) + dout_ref[...]

        @pl.when(i == 0)
        def _():
            ghead_ref[...] = dh
            gw_ref[...] = jnp.zeros_like(gw_ref)

        gx_ref[...] = dh
        gw_ref[0:1, :] += jnp.sum(dub * xhat, axis=0, keepdims=True)

    return pl.pallas_call(
        body, name="prenorm_bwd", grid=(p // CHUNK,),
        in_specs=[_full((CHUNK, d)), pl.BlockSpec((CHUNK, d), lambda i: (jnp.maximum(i - 1, 0), 0)), _full((1, d)),
                  pl.BlockSpec((CHUNK, d), lambda i: (i, 0)), pl.BlockSpec((CHUNK, d), lambda i: (i, 0))],
        out_specs=[pl.BlockSpec((CHUNK, d), lambda i: (jnp.maximum(i - 1, 0), 0)), _full((CHUNK, d)), _full((8, d))],
        out_shape=[jax.ShapeDtypeStruct(x2.shape, F32), jax.ShapeDtypeStruct((CHUNK, d), F32),
                   jax.ShapeDtypeStruct((8, d), F32)],
        compiler_params=_cparams(("arbitrary",)),
    )(head, x2, w, du, dout)


def _conv_pre(ext_ref, cw_ref, cb_ref):
    pre = cb_ref[...] + cw_ref[CONV_K - 1:CONV_K, :] * ext_ref[8:8 + CHUNK, :]
    for j in range(1, CONV_K):
        pre = pre + cw_ref[CONV_K - 1 - j:CONV_K - j, :] * ext_ref[8 - j:8 - j + CHUNK, :]
    return pre


def _ssd_scalars(dtf_ref, brow_ref, alog_ref, rowmask, hs, ha, tri):
    lane = lax.broadcasted_iota(jnp.int32, (1, LANES), 1)
    is_dt = lane < hs
    is_f = (lane >= hs) & (lane < hs + ha)
    dtr = dtf_ref[...] + brow_ref[...]
    sp = _softplus(dtr)
    dt = jnp.where(is_dt, sp, 0.0) * rowmask
    logf = jnp.where(is_f, jnp.minimum(dtr, 0.0) - jnp.log(1.0 + jnp.exp(-jnp.abs(dtr))), 0.0) * rowmask
    a_row = jnp.where(is_dt, -jnp.exp(alog_ref[...]), 0.0)
    run = _dot_tri(tri, dt * a_row + logf)
    return dtr, dt, a_row, run, is_dt, is_f


def _tri_mats():
    r = lax.broadcasted_iota(jnp.int32, (CHUNK, CHUNK), 0)
    c = lax.broadcasted_iota(jnp.int32, (CHUNK, CHUNK), 1)
    return r, c


def _ssd_fwd(xbc, z, dtf, conv_w, conv_b, brow, alog, dskip_l, ssd_norm, sel_t, hs, ha):
    p, cd = xbc.shape
    ds = z.shape[1]
    ns = (cd - ds) // (2 * SSD_GROUPS)
    gw = ds // SSD_GROUPS
    nch = p // CHUNK
    hpg = hs // SSD_GROUPS

    def body(xbc_ref, halo_ref, z_ref, dtf_ref, cw_ref, cb_ref, brow_ref, alog_ref, dsk_ref, nrm_ref, selt_ref,
             y_ref, yssd_ref, hin_ref, cf_ref, pre_ref, st_ref, carry_ref, yacc_ref, xc_s, ex_s, xdtb_s, xwb_s, ext_s):
        c = pl.program_id(0)

        @pl.when(c == 0)
        def _():
            st_ref[...] = jnp.zeros_like(st_ref)
            carry_ref[...] = jnp.zeros_like(carry_ref)

        rows = lax.broadcasted_iota(jnp.int32, (CHUNK, 1), 0)
        rowmask = jnp.where((rows >= PADN) | (c > 0), 1.0, 0.0)
        ri, ci = _tri_mats()
        causal = ri >= ci
        tri = jnp.where(causal, 1.0, 0.0).astype(BF16)

        ext_s[0:8, :] = halo_ref[...].astype(F32)[HALO - 8:, :] * jnp.where(c > 0, 1.0, 0.0)
        ext_s[8:, :] = xbc_ref[...].astype(F32)
        pre = _conv_pre(ext_s, cw_ref, cb_ref)
        pre_ref[...] = pre.astype(BF16)
        xc_s[...] = pre * _sigmoid(pre) * rowmask

        dtr, dt, a_row, run, is_dt, is_f = _ssd_scalars(dtf_ref, brow_ref, alog_ref, rowmask, hs, ha, tri)
        cf = run + carry_ref[...]
        cf_ref[...] = cf
        carry_ref[...] = jnp.where(is_f, cf[CHUNK - 1:CHUNK, :], 0.0)
        cs = jnp.where(is_dt, run, 0.0)
        cl = cs[CHUNK - 1:CHUNK, :]
        selt = selt_ref[...]
        ex_s[...] = _dot_sel(jnp.exp(cs), selt)
        cdec_x = _dot_sel(jnp.broadcast_to(jnp.exp(cl), (8, LANES)), selt)[0:1, :]
        cs_t = cs.T
        xdt = xc_s[:, :ds] * _dot_sel(dt, selt)
        xdtb_s[...] = xdt.astype(BF16)
        xwb_s[...] = (xdt * _dot_sel(jnp.exp(cl - cs), selt)).astype(BF16)

        lane = lax.broadcasted_iota(jnp.int32, (1, LANES), 1)
        half0 = lane < HEAD_DIM
        for g in range(SSD_GROUPS):
            bg = xc_s[:, ds + g * ns: ds + (g + 1) * ns].astype(BF16)
            cg = xc_s[:, ds + SSD_GROUPS * ns + g * ns: ds + SSD_GROUPS * ns + (g + 1) * ns].astype(BF16)
            gm = _dot(cg, bg, NT)
            gs = slice(g * gw, (g + 1) * gw)
            stg = st_ref[:, gs]
            stg_b = stg.astype(BF16)
            hin_ref[0, :, gs] = stg_b
            yoff = _dot(cg, stg_b) * ex_s[:, gs]
            for pr in range(gw // LANES):
                sl = slice(g * gw + pr * LANES, g * gw + (pr + 1) * LANES)
                xp = xdtb_s[:, sl]
                yd = jnp.zeros((CHUNK, LANES), F32)
                for j in range(2):
                    h = g * hpg + 2 * pr + j
                    seg = cs[:, h:h + 1] - cs_t[h:h + 1, :]
                    m = jnp.where(causal, gm * jnp.exp(jnp.minimum(seg, 0.0)), 0.0).astype(BF16)
                    sel = half0 if j == 0 else jnp.logical_not(half0)
                    yd = yd + _dot(m, jnp.where(sel, xp, jnp.zeros_like(xp)))
                yacc_ref[:, sl] = yd + yoff[:, pr * LANES:(pr + 1) * LANES] + dsk_ref[:, sl] * xc_s[:, sl]
            st_ref[:, gs] = stg * cdec_x[:, gs] + _dot(bg, xwb_s[:, gs], TN)

        y = yacc_ref[...]
        y_ref[...] = y.astype(BF16)
        zf = z_ref[...].astype(F32)
        u = y * zf * _sigmoid(zf)
        for g in range(SSD_GROUPS):
            gs = slice(g * gw, (g + 1) * gw)
            ug = u[:, gs]
            ms = jnp.mean(ug * ug, axis=-1, keepdims=True)
            yssd_ref[:, gs] = (ug * lax.rsqrt(ms + EPS) * nrm_ref[:, gs]).astype(BF16)

    rb = CHUNK // HALO
    return pl.pallas_call(
        body, name="ssd_fwd", grid=(nch,),
        in_specs=[pl.BlockSpec((CHUNK, cd), lambda c: (c, 0)),
                  pl.BlockSpec((HALO, cd), lambda c: (jnp.maximum(c * rb - 1, 0), 0)),
                  pl.BlockSpec((CHUNK, ds), lambda c: (c, 0)),
                  pl.BlockSpec((CHUNK, LANES), lambda c: (c, 0)),
                  _full((CONV_K, cd)), _full((1, cd)), _full((1, LANES)), _full((1, LANES)),
                  _full((1, ds)), _full((1, ds)), _full((LANES, ds))],
        out_specs=[pl.BlockSpec((CHUNK, ds), lambda c: (c, 0)), pl.BlockSpec((CHUNK, ds), lambda c: (c, 0)),
                   pl.BlockSpec((1, ns, ds), lambda c: (c, 0, 0)), pl.BlockSpec((CHUNK, LANES), lambda c: (c, 0)),
                   pl.BlockSpec((CHUNK, cd), lambda c: (c, 0))],
        out_shape=[jax.ShapeDtypeStruct((p, ds), BF16), jax.ShapeDtypeStruct((p, ds), BF16),
                   jax.ShapeDtypeStruct((nch, ns, ds), BF16), jax.ShapeDtypeStruct((p, LANES), F32),
                   jax.ShapeDtypeStruct((p, cd), BF16)],
        scratch_shapes=[pltpu.VMEM((ns, ds), F32), pltpu.VMEM((1, LANES), F32), pltpu.VMEM((CHUNK, ds), F32),
                        pltpu.VMEM((CHUNK, cd), F32), pltpu.VMEM((CHUNK, ds), F32),
                        pltpu.VMEM((CHUNK, ds), BF16), pltpu.VMEM((CHUNK, ds), BF16),
                        pltpu.VMEM((8 + CHUNK, cd), F32)],
        compiler_params=_cparams(("arbitrary",)),
    )(xbc, xbc, z, dtf, conv_w, conv_b, brow, alog, dskip_l, ssd_norm, sel_t)


def _ssd_bwd(dyssd, y, z, xbc, pre, dtf, hin, dcf, conv_w, brow, alog, dskip_l, ssd_norm, sel_t, sel, hs, ha):
    p, cd = xbc.shape
    ds = z.shape[1]
    ns = (cd - ds) // (2 * SSD_GROUPS)
    gw = ds // SSD_GROUPS
    nch = p // CHUNK
    hpg = hs // SSD_GROUPS

    def body(dyssd_ref, y_ref, z_ref, xbc_ref, pre_ref, dtf_ref, hin_ref, dcf_ref, cw_ref, brow_ref,
             alog_ref, dsk_ref, nrm_ref, selt_ref, sel_ref,
             dxbc_ref, dz_ref, ddtf_ref, gcw_ref, gcb_ref, gnrm_ref, gsm_ref,
             dst_ref, nxt_ref, fcar_ref, gdsk_ref, dxc_ref, xc_s, dsl_s, dtx_s, ex_s, wx_s, dy_s, xdtb_s, xwb_s,
             dyb_s, dyeb_s):
        step = pl.program_id(0)
        c = nch - 1 - step

        @pl.when(step == 0)
        def _():
            dst_ref[...] = jnp.zeros_like(dst_ref)
            nxt_ref[...] = jnp.zeros_like(nxt_ref)
            fcar_ref[...] = jnp.zeros_like(fcar_ref)
            gdsk_ref[...] = jnp.zeros_like(gdsk_ref)
            gcw_ref[...] = jnp.zeros_like(gcw_ref)
            gcb_ref[...] = jnp.zeros_like(gcb_ref)
            gnrm_ref[...] = jnp.zeros_like(gnrm_ref)
            gsm_ref[...] = jnp.zeros_like(gsm_ref)

        rows = lax.broadcasted_iota(jnp.int32, (CHUNK, 1), 0)
        rowmask = jnp.where((rows >= PADN) | (c > 0), 1.0, 0.0)
        ri, ci = _tri_mats()
        causal = ri >= ci
        anti = ci >= ri
        tri = jnp.where(causal, 1.0, 0.0).astype(BF16)
        rtri = jnp.where(anti, 1.0, 0.0).astype(BF16)

        pre = pre_ref[...].astype(F32)
        sg = _sigmoid(pre)
        xc_s[...] = pre * sg * rowmask
        dsl_s[...] = sg * (1.0 + pre * (1.0 - sg)) * rowmask

        dtr, dt, a_row, run, is_dt, is_f = _ssd_scalars(dtf_ref, brow_ref, alog_ref, rowmask, hs, ha, tri)
        cs = jnp.where(is_dt, run, 0.0)
        cl = cs[CHUNK - 1:CHUNK, :]
        selt = selt_ref[...]
        selm = sel_ref[...]
        dtx_s[...] = _dot_sel(dt, selt)
        ex_s[...] = _dot_sel(jnp.exp(cs), selt)
        wx_s[...] = _dot_sel(jnp.exp(cl - cs), selt)
        cdec = jnp.exp(cl)
        cdec_x = _dot_sel(jnp.broadcast_to(cdec, (8, LANES)), selt)[0:1, :]
        cs_t = cs.T
        xdt = xc_s[:, :ds] * dtx_s[...]
        xdtb_s[...] = xdt.astype(BF16)
        xwb_s[...] = (xdt * wx_s[...]).astype(BF16)

        yv = y_ref[...].astype(F32)
        zf = z_ref[...].astype(F32)
        sz = _sigmoid(zf)
        u = yv * zf * sz
        dyo = dyssd_ref[...].astype(F32)
        du_parts = []
        for g in range(SSD_GROUPS):
            gs = slice(g * gw, (g + 1) * gw)
            ug = u[:, gs]
            rstd = lax.rsqrt(jnp.mean(ug * ug, axis=-1, keepdims=True) + EPS)
            yhat = ug * rstd
            dyg = dyo[:, gs]
            gnrm_ref[0:1, gs] += jnp.sum(dyg * yhat, axis=0, keepdims=True)
            dyh = dyg * nrm_ref[:, gs]
            du_parts.append(rstd * (dyh - yhat * jnp.mean(dyh * yhat, axis=-1, keepdims=True)))
        du = jnp.concatenate(du_parts, axis=1)
        dy = du * zf * sz
        dz_ref[...] = (du * yv * sz * (1.0 + zf * (1.0 - sz))).astype(BF16)
        dy_s[...] = dy
        dyb_s[...] = dy.astype(BF16)
        dyeb_s[...] = (dy * ex_s[...]).astype(BF16)
        gdsk_ref[...] += jnp.sum(dy * xc_s[:, :ds], axis=0, keepdims=True)
        lane = lax.broadcasted_iota(jnp.int32, (1, LANES), 1)
        half0 = lane < HEAD_DIM
        x_parts, yo_parts, t4_parts = [], [], []
        dcs = jnp.zeros((CHUNK, LANES), F32)
        for g in range(SSD_GROUPS):
            gs = slice(g * gw, (g + 1) * gw)
            bsl = slice(ds + g * ns, ds + (g + 1) * ns)
            csl = slice(ds + SSD_GROUPS * ns + g * ns, ds + SSD_GROUPS * ns + (g + 1) * ns)
            bg = xc_s[:, bsl].astype(BF16)
            cg = xc_s[:, csl].astype(BF16)
            gm = _dot(cg, bg, NT)
            gm_t = _dot(bg, cg, NT)
            stg_b = hin_ref[0, :, gs]
            dstg = dst_ref[:, gs]
            dstg_b = dstg.astype(BF16)
            t4_parts.append(jnp.sum(dstg * stg_b.astype(F32), axis=0, keepdims=True))
            zst = _dot(bg, dstg_b) * wx_s[:, gs]
            x_parts.append(xc_s[:, gs] * dtx_s[:, gs] * zst)
            yo_parts.append(dy_s[:, gs] * (_dot(cg, stg_b) * ex_s[:, gs]))
            dgsum = jnp.zeros((CHUNK, CHUNK), F32)
            dgtsum = jnp.zeros((CHUNK, CHUNK), F32)
            for pr in range(gw // LANES):
                sl = slice(g * gw + pr * LANES, g * gw + (pr + 1) * LANES)
                xp = xdtb_s[:, sl]
                dyp = dyb_s[:, sl]
                dxd = zst[:, pr * LANES:(pr + 1) * LANES]
                for j in range(2):
                    h = g * hpg + 2 * pr + j
                    sel_l = half0 if j == 0 else jnp.logical_not(half0)
                    seg = cs[:, h:h + 1] - cs_t[h:h + 1, :]
                    lm = jnp.where(causal, jnp.exp(jnp.minimum(seg, 0.0)), 0.0)
                    lmt = jnp.where(anti, jnp.exp(jnp.minimum(-seg, 0.0)), 0.0)
                    dyp_m = jnp.where(sel_l, dyp, jnp.zeros_like(dyp))
                    xp_m = jnp.where(sel_l, xp, jnp.zeros_like(xp))
                    dxd = dxd + _dot((gm_t * lmt).astype(BF16), dyp_m)
                    dg = _dot(dyp_m, xp, NT) * lm
                    dgt = _dot(xp_m, dyp, NT) * lmt
                    dgsum = dgsum + dg
                    dgtsum = dgtsum + dgt
                    qrow = (jnp.sum(dg * gm, axis=1, keepdims=True) - jnp.sum(dgt * gm_t, axis=1, keepdims=True))
                    dcs = dcs + jnp.where(lane == h, qrow, 0.0)
                dxc_ref[:, sl] = dxd
            dxc_ref[:, csl] = _dot(dgsum.astype(BF16), bg) + _dot(dyeb_s[:, gs], stg_b, NT)
            dxc_ref[:, bsl] = _dot(dgtsum.astype(BF16), cg) + _dot(xwb_s[:, gs], dstg_b, NT)
            dst_ref[:, gs] = dstg * cdec_x[:, gs] + _dot(cg, dyeb_s[:, gs], TN)

        dxdt = dxc_ref[:, :ds]
        xst = _dot_sel(jnp.concatenate(x_parts, axis=1), selm)
        yo = _dot_sel(jnp.concatenate(yo_parts, axis=1), selm)
        t4 = _dot_sel(jnp.concatenate([jnp.concatenate(t4_parts, axis=1), jnp.zeros((7, ds), F32)], axis=0), selm)
        dcl = jnp.sum(xst, axis=0, keepdims=True) + cdec * t4[0:1, :]
        dcs = dcs + yo - xst + jnp.where(rows == CHUNK - 1, dcl, 0.0)
        da_ = _dot_tri(rtri, dcs)
        ddt = _dot_sel(dxdt * xc_s[:, :ds], selm) + da_ * a_row
        dcf_blk = dcf_ref[...]
        dlogf = _dot_tri(rtri, dcf_blk) + fcar_ref[...]
        fcar_ref[...] += jnp.sum(dcf_blk, axis=0, keepdims=True)
        sgd = _sigmoid(dtr)
        ddtf = (jnp.where(is_dt, ddt * sgd, 0.0) + jnp.where(is_f, dlogf * (1.0 - sgd), 0.0)) * rowmask
        ddtf_ref[...] = ddtf
        gsm_ref[0:1, :] += jnp.sum(ddtf, axis=0, keepdims=True)
        gsm_ref[1:2, :] += jnp.sum(da_ * dt, axis=0, keepdims=True) * a_row

        dxc_ref[:, :ds] = dxdt * dtx_s[...] + dsk_ref[...] * dy_s[...]
        dpre = dxc_ref[...] * dsl_s[...]
        nxt_ref[0:CHUNK, :] = dpre
        gcb_ref[0:1, :] += jnp.sum(dpre, axis=0, keepdims=True)
        xr = xbc_ref[...].astype(F32)
        gcw_ref[CONV_K - 1:CONV_K, :] += jnp.sum(dpre * xr, axis=0, keepdims=True)
        dxr = cw_ref[CONV_K - 1:CONV_K, :] * dpre
        for j in range(1, CONV_K):
            up = nxt_ref[j:j + CHUNK, :]
            gcw_ref[CONV_K - 1 - j:CONV_K - j, :] += jnp.sum(up * xr, axis=0, keepdims=True)
            dxr = dxr + cw_ref[CONV_K - 1 - j:CONV_K - j, :] * up
        nxt_ref[CHUNK:, :] = dpre[0:8, :]
        dxbc_ref[...] = dxr.astype(BF16)

        @pl.when(step == nch - 1)
        def _():
            gsm_ref[2:3, :] = _dot_sel(jnp.broadcast_to(gdsk_ref[...], (8, ds)), selm)[0:1, :]

    rev = lambda s: nch - 1 - s
    blk = lambda w: pl.BlockSpec((CHUNK, w), lambda s: (rev(s), 0))
    return pl.pallas_call(
        body, name="ssd_bwd", grid=(nch,),
        in_specs=[blk(ds), blk(ds), blk(ds), blk(cd), blk(cd),
                  blk(LANES), pl.BlockSpec((1, ns, ds), lambda s: (rev(s), 0, 0)), blk(LANES),
                  _full((CONV_K, cd)), _full((1, LANES)), _full((1, LANES)),
                  _full((1, ds)), _full((1, ds)), _full((LANES, ds)), _full((ds, LANES))],
        out_specs=[blk(cd), blk(ds), blk(LANES), _full((8, cd)), _full((8, cd)), _full((8, ds)), _full((8, LANES))],
        out_shape=[jax.ShapeDtypeStruct((p, cd), BF16), jax.ShapeDtypeStruct((p, ds), BF16),
                   jax.ShapeDtypeStruct((p, LANES), F32), jax.ShapeDtypeStruct((8, cd), F32),
                   jax.ShapeDtypeStruct((8, cd), F32), jax.ShapeDtypeStruct((8, ds), F32),
                   jax.ShapeDtypeStruct((8, LANES), F32)],
        scratch_shapes=[pltpu.VMEM((ns, ds), F32), pltpu.VMEM((CHUNK + 8, cd), F32), pltpu.VMEM((1, LANES), F32),
                        pltpu.VMEM((1, ds), F32), pltpu.VMEM((CHUNK, cd), F32),
                        pltpu.VMEM((CHUNK, cd), F32), pltpu.VMEM((CHUNK, cd), F32),
                        pltpu.VMEM((CHUNK, ds), F32), pltpu.VMEM((CHUNK, ds), F32), pltpu.VMEM((CHUNK, ds), F32),
                        pltpu.VMEM((CHUNK, ds), F32), pltpu.VMEM((CHUNK, ds), BF16), pltpu.VMEM((CHUNK, ds), BF16),
                        pltpu.VMEM((CHUNK, ds), BF16), pltpu.VMEM((CHUNK, ds), BF16)],
        compiler_params=_cparams(("arbitrary",)),
    )(dyssd, y, z, xbc, pre, dtf, hin, dcf, conv_w, brow, alog, dskip_l, ssd_norm, sel_t, sel)


def _attn_fwd(q, k, v, ck, blk):
    p, da = q.shape
    npair, nkb = ck.shape[0], ck.shape[1]
    scale = 1.0 / math.sqrt(HEAD_DIM)

    def body(q_ref, k_ref, v_ref, ck_ref, o_ref, lse_ref):
        i = pl.program_id(1)
        lane = lax.broadcasted_iota(jnp.int32, (1, LANES), 1)
        sels = [lane < HEAD_DIM, lane >= HEAD_DIM]
        ones = [jnp.where(lane == HEAD_DIM, 1.0, 0.0).astype(BF16), jnp.where(lane == 0, 1.0, 0.0).astype(BF16)]
        qb = q_ref[...] * scale
        cmask = (lax.broadcasted_iota(jnp.int32, (blk, blk), 1) <= lax.broadcasted_iota(jnp.int32, (blk, blk), 0))

        def step(kb, carry, masked, nk=1):
            r0 = pl.multiple_of(kb * blk, blk)
            ks = k_ref[pl.ds(r0, nk * blk), :]
            vs = v_ref[pl.ds(r0, nk * blk), :]
            kk = jnp.concatenate([jnp.where(sel, ks, jnp.zeros_like(ks)) for sel in sels], axis=0)
            s_both = _dot(qb, kk, NT)
            out = []
            for j in range(2):
                m, acc = carry[2 * j], carry[2 * j + 1]
                ckr = jnp.concatenate([ck_ref[0, kb + t, j:j + 1, :] for t in range(nk)], axis=1)
                s = s_both[:, j * nk * blk:(j + 1) * nk * blk] - ckr
                if masked:
                    s = jnp.where(cmask, s, NEG)
                mn = jnp.maximum(m, jnp.max(s, axis=-1, keepdims=True))
                pr = jnp.exp(s - mn).astype(BF16)
                acc = jnp.exp(m - mn) * acc + _dot(pr, jnp.where(sels[j], vs, ones[j]))
                out += [mn, acc]
            return tuple(out)

        init = (jnp.full((blk, 1), NEG, F32), jnp.zeros((blk, LANES), F32)) * 2
        n4 = i // 4
        n2 = (i - 4 * n4) // 2
        carry = lax.fori_loop(0, n4, lambda t, c: step(4 * t, c, False, 4), init)
        carry = lax.fori_loop(0, n2, lambda t, c: step(4 * n4 + 2 * t, c, False, 2), carry)
        carry = lax.fori_loop(4 * n4 + 2 * n2, i, lambda kb, c: step(kb, c, False), carry)
        m0, a0, m1, a1 = step(i, carry, True)
        l0 = a0[:, HEAD_DIM:HEAD_DIM + 1]
        l1 = a1[:, 0:1]
        o_ref[...] = jnp.where(sels[0], a0 / l0, a1 / l1).astype(BF16)
        lse_ref[...] = jnp.where(sels[0], m0 + jnp.log(l0), m1 + jnp.log(l1))

    return pl.pallas_call(
        body, name="attn_fwd", grid=(npair, p // blk),
        in_specs=[pl.BlockSpec((blk, LANES), lambda h, i: (i, h)),
                  pl.BlockSpec((p, LANES), lambda h, i: (0, h)), pl.BlockSpec((p, LANES), lambda h, i: (0, h)),
                  pl.BlockSpec((1, nkb, 8, blk), lambda h, i: (h, 0, 0, 0))],
        out_specs=[pl.BlockSpec((blk, LANES), lambda h, i: (i, h)), pl.BlockSpec((blk, LANES), lambda h, i: (i, h))],
        out_shape=[jax.ShapeDtypeStruct((p, da), BF16), jax.ShapeDtypeStruct((p, da), F32)],
        compiler_params=_cparams(("parallel", "arbitrary")),
    )(q, k, v, ck)


def _attn_bwd(q, k, v, o, do, lse_rep, ck, blk):
    p, da = q.shape
    npair, nkb = ck.shape[0], ck.shape[1]
    nq = p // blk
    scale = 1.0 / math.sqrt(HEAD_DIM)

    def body(k_ref, v_ref, q_ref, do_ref, o_ref, lse_ref, ck_ref, dk_ref, dv_ref, dq_ref, dcs_ref, rsum_ref, dq_acc):
        jb = pl.program_id(1)

        @pl.when(jb == 0)
        def _():
            dq_acc[...] = jnp.zeros_like(dq_acc)

        ks = k_ref[...]
        vs = v_ref[...]
        lane = lax.broadcasted_iota(jnp.int32, (1, LANES), 1)
        sels = [lane < HEAD_DIM, lane >= HEAD_DIM]
        ones = [jnp.where(lane == HEAD_DIM, 1.0, 0.0).astype(BF16), jnp.where(lane == 0, 1.0, 0.0).astype(BF16)]
        kss = ks * scale
        kmo = [jnp.where(sels[j], kss, ones[j]) for j in range(2)]
        cmask = (lax.broadcasted_iota(jnp.int32, (blk, blk), 1) <= lax.broadcasted_iota(jnp.int32, (blk, blk), 0))

        def step(ib, carry, masked, nb=1):
            rows = nb * blk
            r0 = pl.multiple_of(ib * blk, blk)
            qb = q_ref[pl.ds(r0, rows), :] * scale
            dob = do_ref[pl.ds(r0, rows), :]
            prod = dob.astype(F32) * o_ref[pl.ds(r0, rows), :].astype(F32)
            out = []
            for j in range(2):
                dk, dv = carry[2 * j], carry[2 * j + 1]
                qm = jnp.where(sels[j], qb, jnp.zeros_like(qb))
                dom = jnp.where(sels[j], dob, jnp.zeros_like(dob))
                lse = lse_ref[pl.ds(r0, rows), HEAD_DIM * j:HEAD_DIM * j + 1]
                dlt = jnp.sum(jnp.where(sels[j], prod, 0.0), axis=-1, keepdims=True)
                s = _dot(qm, ks, NT) - ck_ref[0, 0, j:j + 1, :] - lse
                pm = jnp.exp(jnp.minimum(s, 0.0))
                if masked:
                    pm = jnp.where(cmask, pm, 0.0)
                ds_b = (pm * (_dot(dom, vs, NT) - dlt)).astype(BF16)
                dv = dv + _dot(pm.astype(BF16), dom, TN)
                dk = dk + _dot(ds_b, jnp.where(sels[j], qb, ones[j]), TN)
                dq_acc[pl.ds(r0, rows), LANES * j:LANES * (j + 1)] += _dot(ds_b, kmo[j])
                out += [dk, dv]
            return tuple(out)

        zero = jnp.zeros((blk, LANES), F32)
        carry = step(jb, (zero, zero, zero, zero), True)
        n4 = (nq - 1 - jb) // 4
        n2 = (nq - 1 - jb - 4 * n4) // 2
        carry = lax.fori_loop(0, n4, lambda t, c: step(jb + 1 + 4 * t, c, False, 4), carry)
        carry = lax.fori_loop(0, n2, lambda t, c: step(jb + 1 + 4 * n4 + 2 * t, c, False, 2), carry)
        dk0, dv0, dk1, dv1 = lax.fori_loop(jb + 1 + 4 * n4 + 2 * n2, nq, lambda ib, c: step(ib, c, False), carry)
        dk_ref[...] = jnp.where(sels[0], dk0, dk1).astype(BF16)
        dv_ref[...] = (dv0 + dv1).astype(BF16)
        pair8 = lambda c0, c1: jnp.where(lane == 0, c0, jnp.where(lane == 1, c1, 0.0)).T[0:8]
        dcs_ref[0] = pair8(dk0[:, HEAD_DIM:HEAD_DIM + 1], dk1[:, 0:1])

        @pl.when(jb == nkb - 1)
        def _():
            a0 = dq_acc[:, :LANES]
            a1 = dq_acc[:, LANES:]
            dq_ref[...] = jnp.where(sels[0], a0, a1).astype(BF16)
            rsum_ref[0] = pair8(a0[:, HEAD_DIM:HEAD_DIM + 1], a1[:, 0:1])

    colblk = pl.BlockSpec((blk, LANES), lambda h, j: (j, h))
    colfull = pl.BlockSpec((p, LANES), lambda h, j: (0, h))
    ckspec = pl.BlockSpec((1, 1, 8, blk), lambda h, j: (h, j, 0, 0))
    return pl.pallas_call(
        body, name="attn_bwd", grid=(npair, nkb),
        in_specs=[colblk, colblk, colfull, colfull, colfull, colfull, ckspec],
        out_specs=[colblk, colblk, colfull, pl.BlockSpec((1, 8, blk), lambda h, j: (h, 0, j)),
                   pl.BlockSpec((1, 8, p), lambda h, j: (h, 0, 0))],
        out_shape=[jax.ShapeDtypeStruct((p, da), BF16), jax.ShapeDtypeStruct((p, da), BF16),
                   jax.ShapeDtypeStruct((p, da), BF16), jax.ShapeDtypeStruct((npair, 8, p), F32),
                   jax.ShapeDtypeStruct((npair, 8, p), F32)],
        scratch_shapes=[pltpu.VMEM((p, 2 * LANES), F32)],
        compiler_params=_cparams(("parallel", "arbitrary")),
    )(k, v, q, do, o, lse_rep, ck)


def _rows3(i):
    return jnp.maximum(3 * i - 1, 0), 3 * i, 3 * i + 1


def _tail_fwd(yssd, o, zatt, graw, head, x2, tgt2, wps, wpa, wout, gate_bias, norm_post, tm):
    p, ds = yssd.shape
    da = o.shape[1]
    d = x2.shape[1]
    nsub = tm // CHUNK

    def body(yssd_ref, o_ref, zatt_ref, g_ref, head_ref, *rest):
        x_refs, t_refs = rest[:nsub], rest[nsub:2 * nsub]
        (wps_ref, wpa_ref, wout_ref, gb_ref, np_ref,
         yatt_ref, mrg_ref, a_ref, b_ref, dzo_ref, dout_ref, red_ref) = rest[2 * nsub:]
        i = pl.program_id(0)

        @pl.when(i == 0)
        def _():
            red_ref[...] = jnp.zeros_like(red_ref)

        first = jnp.where(i == 0, head_ref[...], x_refs[0][...])
        h = jnp.concatenate([first] + [r[...] for r in x_refs[1:]], axis=0)
        tgt = jnp.concatenate([r[...] for r in t_refs], axis=0)
        rows = lax.broadcasted_iota(jnp.int32, (tm, 1), 0)
        valid = jnp.where((i > 0) | (rows >= CHUNK), 1.0, 0.0)
        ob = o_ref[...].astype(F32)
        za = zatt_ref[...].astype(F32)
        yatt_b = (ob * za * _sigmoid(za)).astype(BF16)
        yatt_ref[...] = yatt_b
        a = _dot(yssd_ref[...], wps_ref[...])
        b = _dot(yatt_b, wpa_ref[...])
        a_ref[...] = a.astype(BF16)
        b_ref[...] = b.astype(BF16)
        gr = g_ref[...].astype(F32) + gb_ref[...]
        mrg_b = (_sigmoid(gr[:, :d]) * a + _sigmoid(gr[:, d:]) * b).astype(BF16)
        mrg_ref[...] = mrg_b
        zo = _dot(mrg_b, wout_ref[...])
        rstd = lax.rsqrt(jnp.mean(zo * zo, axis=-1, keepdims=True) + EPS)
        zh = zo * rstd
        npw = np_ref[...]
        err = (h + zh * npw - tgt) * valid
        dout = err * (1.0 / d)
        dout_ref[...] = dout
        dzh = dout * npw
        dzo_ref[...] = (rstd * (dzh - zh * jnp.mean(dzh * zh, axis=-1, keepdims=True))).astype(BF16)
        red_ref[0:1, :] += jnp.sum(dout * zh, axis=0, keepdims=True)
        red_ref[1:2, 0:1] += jnp.sum(jnp.sum(err * err, axis=1, keepdims=True), axis=0, keepdims=True) * (0.5 / d)

    row = lambda w: pl.BlockSpec((tm, w), lambda i: (i, 0))
    once = lambda shape: pl.BlockSpec(shape, lambda i: (0,) * len(shape), pipeline_mode=pl.Buffered(1))
    if nsub == 1:
        subs = [pl.BlockSpec((CHUNK, d), lambda i: (jnp.maximum(i - 1, 0), 0))]
    else:
        subs = [pl.BlockSpec((CHUNK, d), functools.partial(lambda i, k: (_rows3(i)[k], 0), k=k)) for k in range(3)]
    sd = jax.ShapeDtypeStruct
    return pl.pallas_call(
        body, name="tail_fwd", grid=(p // tm,),
        in_specs=[row(ds), row(da), row(da), row(2 * d), _full((CHUNK, d))] + subs + subs
                 + [once((ds, d)), once((da, d)), once((d, d)), _full((1, 2 * d)), _full((1, d))],
        out_specs=[row(da), row(d), row(d), row(d), row(d), row(d), _full((8, d))],
        out_shape=[sd((p, da), BF16), sd((p, d), BF16), sd((p, d), BF16), sd((p, d), BF16), sd((p, d), BF16),
                   sd((p, d), F32), sd((8, d), F32)],
        compiler_params=_cparams(("arbitrary",)),
    )(yssd, o, zatt, graw, head, *([x2] * nsub), *([tgt2] * nsub), wps, wpa, wout, gate_bias, norm_post)


def _tail_bwd(dzo, a_b, b_b, graw, o, zatt, wps, wpa, wout, gate_bias, tm):
    p, d = dzo.shape
    ds, da = wps.shape[0], wpa.shape[0]

    def body(dzo_ref, a_ref, b_ref, g_ref, o_ref, zatt_ref, wps_ref, wpa_ref, wout_ref, gb_ref,
             da_ref, db_ref, dg_ref, dyssd_ref, do_ref, dzatt_ref, red_ref):
        i = pl.program_id(0)

        @pl.when(i == 0)
        def _():
            red_ref[...] = jnp.zeros_like(red_ref)

        gr = g_ref[...].astype(F32) + gb_ref[...]
        gs = _sigmoid(gr[:, :d])
        ga = _sigmoid(gr[:, d:])
        dm = _dot(dzo_ref[...], wout_ref[...], NT)
        da_b = (gs * dm).astype(BF16)
        db_b = (ga * dm).astype(BF16)
        da_ref[...] = da_b
        db_ref[...] = db_b
        dgs = dm * a_ref[...].astype(F32) * gs * (1.0 - gs)
        dga = dm * b_ref[...].astype(F32) * ga * (1.0 - ga)
        dg_ref[:, :d] = dgs.astype(BF16)
        dg_ref[:, d:] = dga.astype(BF16)
        red_ref[0:1, :d] += jnp.sum(dgs, axis=0, keepdims=True)
        red_ref[0:1, d:] += jnp.sum(dga, axis=0, keepdims=True)
        dyssd_ref[...] = _dot(da_b, wps_ref[...], NT).astype(BF16)
        dya = _dot(db_b, wpa_ref[...], NT)
        ob = o_ref[...].astype(F32)
        za = zatt_ref[...].astype(F32)
        sza = _sigmoid(za)
        do_ref[...] = (dya * za * sza).astype(BF16)
        dzatt_ref[...] = (dya * ob * sza * (1.0 + za * (1.0 - sza))).astype(BF16)

    row = lambda w: pl.BlockSpec((tm, w), lambda i: (i, 0))
    once = lambda shape: pl.BlockSpec(shape, lambda i: (0,) * len(shape), pipeline_mode=pl.Buffered(1))
    sd = jax.ShapeDtypeStruct
    return pl.pallas_call(
        body, name="tail_bwd", grid=(p // tm,),
        in_specs=[row(d), row(d), row(d), row(2 * d), row(da), row(da),
                  once((ds, d)), once((da, d)), once((d, d)), _full((1, 2 * d))],
        out_specs=[row(d), row(d), row(2 * d), row(ds), row(da), row(da), _full((8, 2 * d))],
        out_shape=[sd((p, d), BF16), sd((p, d), BF16), sd((p, 2 * d), BF16), sd((p, ds), BF16), sd((p, da), BF16),
                   sd((p, da), BF16), sd((8, 2 * d), F32)],
        compiler_params=_cparams(("arbitrary",)),
    )(dzo, a_b, b_b, graw, o, zatt, wps, wpa, wout, gate_bias)


def _adamw_math(w, g, m, v):
    m2 = ADAM_B1 * m + (1.0 - ADAM_B1) * g
    v2 = ADAM_B2 * v + (1.0 - ADAM_B2) * (g * g)
    m_hat = m2 / (1.0 - ADAM_B1 ** ADAM_STEP)
    v_hat = v2 / (1.0 - ADAM_B2 ** ADAM_STEP)
    delta = -ADAM_LR * (m_hat / (jnp.sqrt(v_hat) + ADAM_EPS) + ADAM_WD * w)
    return delta, m2, v2


def _adamw_small(params, red, name):
    names = list(params)
    n = len(names)
    extra = [params[k][3] for k in names if not isinstance(params[k][3], tuple)]

    def body(*refs):
        w_refs, m_refs, v_refs = refs[:n], refs[n:2 * n], refs[2 * n:3 * n]
        red_ref = refs[3 * n]
        g_refs = iter(refs[3 * n + 1:3 * n + 1 + len(extra)])
        outs = refs[3 * n + 1 + len(extra):]
        for i, k in enumerate(names):
            where = params[k][3]
            rows, cols = w_refs[i].shape
            if isinstance(where, tuple):
                g = red_ref[where[0]:where[0] + rows, where[1]:where[1] + cols]
            else:
                g = next(g_refs)[...]
            delta, m2, v2 = _adamw_math(w_refs[i][...], g, m_refs[i][...], v_refs[i][...])
            for o, val in zip(outs[4 * i:4 * i + 4], (g, delta, m2, v2)):
                o[...] = val

    vm = pl.BlockSpec(memory_space=pltpu.VMEM)
    ws, ms, vs = ([params[k][j] for k in names] for j in range(3))
    out = pl.pallas_call(
        body, name=name,
        out_shape=[jax.ShapeDtypeStruct(w.shape, F32) for w in ws for _ in range(4)],
        in_specs=[vm] * (3 * n + 1 + len(extra)), out_specs=[vm] * (4 * n),
    )(*ws, *ms, *vs, red, *extra)
    return {k: tuple(out[4 * i:4 * i + 4]) for i, k in enumerate(names)}


def _adamw(w, g, m, v, name, parts=False, part_row0=0):
    r, cdim = w.shape
    tr, tc, by_rows = _tiles_2d(r, cdim)
    pick = (lambda i: (i, 0)) if by_rows else (lambda i: (0, i))
    assert part_row0 % tr == 0
    gpick = (lambda i: (i + part_row0 // tr, 0)) if by_rows else (lambda i: (part_row0 // tr, i))

    def body(w_ref, g_ref, m_ref, v_ref, go_ref, d_ref, mo_ref, vo_ref):
        if parts:
            g = g_ref[0].astype(F32)
            for s in range(1, g_ref.shape[0]):
                g = g + g_ref[s].astype(F32)
        else:
            g = g_ref[...]
        delta, m2, v2 = _adamw_math(w_ref[...], g, m_ref[...], v_ref[...])
        go_ref[...] = g
        d_ref[...] = delta
        mo_ref[...] = m2
        vo_ref[...] = v2

    blk = pl.BlockSpec((tr, tc), pick)
    gspec = pl.BlockSpec((g.shape[0], tr, tc), lambda i: (0,) + gpick(i)) if parts else blk
    return pl.pallas_call(
        body, name=name, grid=((r // tr) * (cdim // tc),),
        in_specs=[blk, gspec, blk, blk], out_specs=[blk] * 4,
        out_shape=[jax.ShapeDtypeStruct((r, cdim), F32)] * 4,
        compiler_params=_cparams(("parallel",)),
    )(w, g, m, v)


def _pad_cols(a, width):
    return jnp.pad(a, ((0, 0), (0, width - a.shape[1])))


def _pack_small_shard(conv_w_sh, meta_sh, width):
    return jnp.concatenate([_pad_cols(conv_w_sh, width), jnp.zeros((4, width), F32), _pad_cols(meta_sh, width)], axis=0)


def _pack_small_rep(norm_pre, norm_post, gate_bias, ssd_norm, conv_b, misc, width):
    rows = [norm_pre, norm_post, gate_bias, ssd_norm, conv_b, misc]
    return jnp.concatenate([_pad_cols(r, width) for r in rows] + [jnp.zeros((2, width), F32)], axis=0)


def kernel(x, meta_tokens, norm_pre, w_in, conv_w, conv_b, dt_bias, a_log, d_skip, ssd_norm, fgate_bias, gate_bias, w_proj_ssd, w_proj_att, w_out, norm_post, loss_target, m_meta_tokens, m_norm_pre, m_w_in, m_conv_w, m_conv_b, m_dt_bias, m_a_log, m_d_skip, m_ssd_norm, m_fgate_bias, m_gate_bias, m_w_proj_ssd, m_w_proj_att, m_w_out, m_norm_post, v_meta_tokens, v_norm_pre, v_w_in, v_conv_w, v_conv_b, v_dt_bias, v_a_log, v_d_skip, v_ssd_norm, v_fgate_bias, v_gate_bias, v_w_proj_ssd, v_w_proj_att, v_w_out, v_norm_post):
    seq, d = x.shape[1], x.shape[2]
    p = seq + CHUNK
    hs, ha = dt_bias.shape[1], fgate_bias.shape[1]
    ds, cd = ssd_norm.shape[1], conv_b.shape[1]
    da = ha * HEAD_DIM
    nc8 = w_in.shape[2]
    cws = cd // N_DEV
    msh = d // N_DEV
    r1, r2, r3 = ds // N_DEV, da // N_DEV, d // N_DEV
    me = _dev_index(*_my_pos())
    x2, tgt2 = x[0], loss_target[0]

    win_sh = jnp.transpose(w_in[0]).astype(BF16)
    rows_sh = jnp.concatenate([w_proj_ssd[0], w_proj_att[0], w_out[0]], axis=0).astype(BF16)
    small_sh = _pack_small_shard(conv_w[0], meta_tokens, cws)
    win_all, small_all = _all_gather([win_sh, small_sh], "gather_weights")
    rows_sh, win_all = lax.optimization_barrier((rows_sh, win_all))
    rows_sems, rows_thru, rows_land, rows_token = _bcast_start(rows_sh, "gather_rows_start")
    cuts = [0, ds, ds + cd, ds + cd + hs, ds + cd + hs + da, ds + cd + hs + 2 * da, ds + cd + hs + 3 * da,
            ds + cd + hs + 4 * da, ds + cd + hs + 4 * da + ha, ds + cd + hs + 4 * da + ha + 2 * d]

    def piece_rows(r0, r1):
        parts = [win_all[s, max(r0, s * nc8) - s * nc8:min(r1, (s + 1) * nc8) - s * nc8]
                 for s in range(N_DEV) if max(r0, s * nc8) < min(r1, (s + 1) * nc8)]
        return parts[0] if len(parts) == 1 else jnp.concatenate(parts, axis=0)

    w_z, w_xbc, w_dt, w_zatt, w_q, w_k, w_v, w_f, w_g = [piece_rows(cuts[i], cuts[i + 1]) for i in range(9)]
    w_dtf = jnp.concatenate([w_dt, w_f, jnp.zeros((LANES - hs - ha, d), BF16)], axis=0)
    conv_w_full = jnp.transpose(small_all[:, 0:CONV_K, :], (1, 0, 2)).reshape(CONV_K, cd)
    meta_full = jnp.transpose(small_all[:, 8:8 + N_META, :msh], (1, 0, 2)).reshape(N_META, d)
    head = jnp.concatenate([jnp.zeros((PADN, d), F32), meta_full + rows_token[0:1, 0:1]], axis=0)

    u = _prenorm_fwd(head, x2, norm_pre)
    tm = _att_block(p)
    seg_w = [w_z, w_xbc, w_zatt, w_q, w_k, w_v, w_g]
    zs, xbc, zatt, q, k, v, graw = [
        _mm(u, w, "nt", BF16, _tile(p, (1408, tm)), _tile(w.shape[0], (1024, 512, 256, 128)), "inproj_%d" % i)
        for i, w in enumerate(seg_w)]
    dtf = _mm(u, w_dtf, "nt", F32, _tile(p, (1408, tm)), LANES, "inproj_dtf")

    brow = jnp.concatenate([dt_bias, fgate_bias, jnp.zeros((1, LANES - hs - ha), F32)], axis=1)
    alog_row = _pad_cols(a_log, LANES)
    dskip_l = jnp.repeat(d_skip, HEAD_DIM, axis=1)
    sel_t = (lax.broadcasted_iota(jnp.int32, (LANES, ds), 1) // HEAD_DIM
             == lax.broadcasted_iota(jnp.int32, (LANES, ds), 0)).astype(BF16)
    sel = sel_t.T
    y, yssd, hin, cf, pre = _ssd_fwd(xbc, zs, dtf, conv_w_full, conv_b, brow, alog_row, dskip_l, ssd_norm, sel_t, hs, ha)

    blk = _att_block(p)
    nkb, npair = p // blk, ha // 2
    cum = jnp.where(lax.broadcasted_iota(jnp.int32, (p, 1), 0) < PADN, -NEG, cf[:, hs:hs + ha])
    ck = jnp.transpose(cum.T.reshape(npair, 2, nkb, blk), (0, 2, 1, 3))
    ck = jnp.pad(ck, ((0, 0), (0, 0), (0, 6), (0, 0)))
    o, lse_rep = _attn_fwd(q, k, v, ck, blk)

    rows_all = _bcast_wait(rows_sems, rows_thru, rows_land, lse_rep, "gather_rows_wait")
    wps = rows_all[:, :r1].reshape(ds, d)
    wpa = rows_all[:, r1:r1 + r2].reshape(da, d)
    wout = rows_all[:, r1 + r2:].reshape(d, d)

    yatt, mrg, a_b, b_b, dzo, dout, red_fwd = _tail_fwd(
        yssd, o, zatt, graw, head, x2, tgt2, wps, wpa, wout, gate_bias, norm_post, tm)
    da_, db_, dgraw, dyssd, d_o, dzatt, red_bwd = _tail_bwd(dzo, a_b, b_b, graw, o, zatt, wps, wpa, wout, gate_bias, tm)

    tw = _tile(d, (512, 256, 128))
    g_wout = _mm(mrg, dzo, "tn", BF16, tw, d, "wgrad_out")
    g_wps = _mm(yssd, da_, "tn", BF16, _tile(ds, (512, 256, 128)), d, "wgrad_ps")
    g_wpa = _mm(yatt, db_, "tn", BF16, _tile(da, (512, 256, 128)), d, "wgrad_pa")

    dk, dv, dq, dcs, rsum = _attn_bwd(q, k, v, o, d_o, lse_rep, ck, blk)
    dcum = (rsum - dcs)[:, 0:2, :].reshape(ha, p).T
    dcf = jnp.pad(dcum, ((0, 0), (hs, LANES - hs - ha)))
    dxbc, dzs, ddtf, gcw, gcb, gnrm, gsm = _ssd_bwd(
        dyssd, y, zs, xbc, pre, dtf, hin, dcf, conv_w_full, brow, alog_row, dskip_l, ssd_norm, sel_t, sel, hs, ha)
    ddtf_b = ddtf.astype(BF16)

    dsegs = [dzs, dxbc, dzatt, dq, dk, dv, dgraw, ddtf_b]
    gsegs = [_mm(dsg, u, "tn", BF16, _tile(dsg.shape[1], (512, 256, 128)), d, "wgrad_in_%d" % i)
             for i, dsg in enumerate(dsegs)]
    g_z, g_xbc, g_zatt, g_q, g_k, g_v, g_g, g_dtf = gsegs
    gw_full = jnp.concatenate([g_z, g_xbc, g_dtf[:hs], g_zatt, g_q, g_k, g_v, g_dtf[hs:hs + ha], g_g], axis=0)
    gwin_parts = gw_full.reshape(N_DEV, nc8, d)
    grows_parts = jnp.concatenate([g_wps.reshape(N_DEV, r1, d), g_wpa.reshape(N_DEV, r2, d),
                                   g_wout.reshape(N_DEV, r3, d)], axis=1)

    core = lax.axis_index("c").astype(jnp.int32).reshape(1)
    sib_win, sib_rows = _exchange_sibling([gwin_parts, grows_parts], "scatter_grads_sibling")
    chip_win = _pair_add(gwin_parts, sib_win, core, "pair_add_w_in")
    chip_rows = _pair_add(grows_parts, sib_rows, core, "pair_add_rows")
    sems, thru, lands, token = _exchange_chips_start([chip_win, chip_rows], "scatter_grads_start")
    dsegs_after = dsegs[:-1] + [ddtf_b + token[0:1, 0:1].astype(BF16)]
    du = _mm_sum_nn(dsegs_after, seg_w + [w_dtf], tm, _tile(d, (512, 256, 128)), "dgrad_in")
    gx, ghead, gnp = _prenorm_bwd(head, x2, norm_pre, du, dout)
    sent, got = _exchange_chips_wait(sems, thru, lands, gnp, "scatter_grads_wait")
    chip = me // 2
    recv_win, recv_rows = [lax.dynamic_update_slice_in_dim(g, lax.dynamic_slice_in_dim(s, chip, 1, axis=0), chip, axis=0)
                           for g, s in zip(got, sent)]
    gmisc = jnp.concatenate([gsm[0:1], gsm[1:2], gsm[2:3], _pad_cols(red_fwd[1:2, 0:1], LANES)], axis=1)
    small_g = jnp.concatenate([
        _pack_small_rep(gnp[0:1], red_fwd[0:1], red_bwd[0:1], gnrm[0:1], gcb[0:1], gmisc, cd),
        _pad_cols(gcw[0:CONV_K], cd), jnp.zeros((4, cd), F32), _pad_cols(ghead[PADN:], cd)], axis=0)
    sg_sems, sg_thru, sg_land, sg_token = _bcast_start(small_g, "reduce_small_start")

    upd_in = _adamw(jnp.transpose(w_in[0]) + sg_token[0:1, 0:1], recv_win, jnp.transpose(m_w_in[0]),
                    jnp.transpose(v_w_in[0]), "adamw_w_in", parts=True)
    upd_ps = _adamw(w_proj_ssd[0] + sg_token[0:1, 0:1], recv_rows, m_w_proj_ssd[0], v_w_proj_ssd[0],
                    "adamw_w_proj_ssd", parts=True, part_row0=0)
    upd_pa = _adamw(w_proj_att[0], recv_rows, m_w_proj_att[0], v_w_proj_att[0], "adamw_w_proj_att", parts=True,
                    part_row0=r1)
    upd_out = _adamw(w_out[0], recv_rows, m_w_out[0], v_w_out[0], "adamw_w_out", parts=True, part_row0=r1 + r2)
    all_done = upd_in[1][0:8, 0:LANES] + upd_ps[1][0:8, 0:LANES] + upd_pa[1][0:8, 0:LANES] + upd_out[1][0:8, 0:LANES]
    red = _sum_slots(_bcast_wait(sg_sems, sg_thru, sg_land, all_done, "reduce_small_wait"), "reduce_small_sum")
    loss = red[5, 3 * LANES]
    g_conv_w = lax.dynamic_slice_in_dim(red[8:8 + CONV_K], me * cws, cws, axis=1)
    g_meta = lax.dynamic_slice_in_dim(red[16:16 + N_META, :d], me * msh, msh, axis=1)
    small = {
        "meta_tokens": (meta_tokens, m_meta_tokens, v_meta_tokens, g_meta),
        "norm_pre": (norm_pre, m_norm_pre, v_norm_pre, (0, 0)),
        "conv_w": (conv_w[0], m_conv_w[0], v_conv_w[0], g_conv_w),
        "conv_b": (conv_b, m_conv_b, v_conv_b, (4, 0)),
        "dt_bias": (dt_bias, m_dt_bias, v_dt_bias, (5, 0)),
        "a_log": (a_log, m_a_log, v_a_log, (5, LANES)),
        "d_skip": (d_skip, m_d_skip, v_d_skip, (5, 2 * LANES)),
        "ssd_norm": (ssd_norm, m_ssd_norm, v_ssd_norm, (3, 0)),
        "fgate_bias": (fgate_bias, m_fgate_bias, v_fgate_bias, (5, hs)),
        "gate_bias": (gate_bias, m_gate_bias, v_gate_bias, (2, 0)),
        "norm_post": (norm_post, m_norm_post, v_norm_post, (1, 0)),
    }
    upd_small = _adamw_small(small, red, "adamw_small")

    def leaves(i):
        sm = {k: v[i] for k, v in upd_small.items()}
        return [sm["meta_tokens"], sm["norm_pre"], jnp.transpose(upd_in[i])[None], sm["conv_w"][None], sm["conv_b"],
                sm["dt_bias"], sm["a_log"], sm["d_skip"], sm["ssd_norm"], sm["fgate_bias"], sm["gate_bias"],
                upd_ps[i][None], upd_pa[i][None], upd_out[i][None], sm["norm_post"]]

    return tuple([loss, gx[None]] + leaves(0) + leaves(1) + leaves(2) + leaves(3))
```

```python
import functools
import math

import jax
import jax.numpy as jnp
from jax import lax
from jax.experimental import pallas as pl
from jax.experimental.pallas import tpu as pltpu

F32 = jnp.float32
BF16 = jnp.bfloat16

N_DEV = 8
N_META = 16
CHUNK = 128
PADN = CHUNK - N_META
HEAD_DIM = 64
SSD_GROUPS = 4
CONV_K = 4
EPS = 1e-6
NEG = -1e30
LANES = 128
HALO = 16

ADAM_LR = 0.001
ADAM_B1 = 0.9
ADAM_B2 = 0.999
ADAM_EPS = 1e-08
ADAM_WD = 0.01
ADAM_STEP = 10

VMEM_LIMIT = 56 * 1024 * 1024

NN = (((1,), (0,)), ((), ()))
NT = (((1,), (1,)), ((), ()))
TN = (((0,), (0,)), ((), ()))
MESH = pl.DeviceIdType.MESH


def _dot(a, b, dims=NN):
    return lax.dot_general(a, b, dims, preferred_element_type=F32)


def _split2(x):
    hi = x.astype(BF16)
    lo = (x - hi.astype(F32)).astype(BF16)
    return hi, lo


def _dot_sel(x, sel):
    hi, lo = _split2(x)
    return _dot(hi, sel) + _dot(lo, sel)


def _dot_tri(tri, x):
    h1 = x.astype(BF16)
    r1 = x - h1.astype(F32)
    h2 = r1.astype(BF16)
    h3 = (r1 - h2.astype(F32)).astype(BF16)
    return _dot(tri, h1) + _dot(tri, h2) + _dot(tri, h3)


def _sigmoid(x):
    return 0.5 * jnp.tanh(0.5 * x) + 0.5


def _softplus(x):
    return jnp.maximum(x, 0.0) + jnp.log(1.0 + jnp.exp(-jnp.abs(x)))


def _cparams(sem=None, vmem=VMEM_LIMIT):
    kw = {"vmem_limit_bytes": vmem}
    if sem is not None:
        kw["dimension_semantics"] = sem
    return pltpu.CompilerParams(**kw)


def _full(shape):
    nd = len(shape)
    return pl.BlockSpec(shape, lambda *_: (0,) * nd)


def _att_block(p):
    return 384 if p % 384 == 0 else CHUNK


def _my_pos():
    return lax.axis_index("x"), lax.axis_index("y"), lax.axis_index("c")


def _dev_index(x, y, c):
    return 4 * x + 2 * y + c


FLIPS = [(fx, fy, fc) for fx in (0, 1) for fy in (0, 1) for fc in (0, 1)][1:]


def _flip(pos, f):
    return tuple((1 - p) if fi else p for p, fi in zip(pos, f))


def _all_gather(bufs, name):
    nb = len(bufs)

    def body(*refs):
        ins, outs = refs[:nb], refs[nb:2 * nb]
        send_sems, recv_sems, local_sems = refs[2 * nb:]
        x, y, c = _my_pos()
        me = _dev_index(x, y, c)
        sibling = (x, y, 1 - c)
        near = [(1 - x, y), (x, 1 - y)]
        far = (1 - x, 1 - y)
        relay_from = (c * (1 - x) + (1 - c) * x, c * y + (1 - c) * (1 - y))
        relay_to = (c * x + (1 - c) * (1 - x), c * (1 - y) + (1 - c) * y)

        def copy(b, k, block_idx, to, src=None):
            dst = outs[b].at[block_idx]
            return pltpu.make_async_remote_copy(
                src_ref=dst if src is None else src, dst_ref=dst,
                send_sem=send_sems.at[b, k], recv_sem=recv_sems.at[b, k],
                device_id=to, device_id_type=MESH)

        started = []
        for b in range(nb):
            mine = pltpu.make_async_copy(ins[b], outs[b].at[me], local_sems.at[b])
            mine.start()
            started.append(mine)
        sent = []
        for b in range(nb):
            sent.append(copy(b, 0, me, sibling, src=ins[b]))
            for j, chip in enumerate(near):
                sent.append(copy(b, 1 + j, me, (chip[0], chip[1], c), src=ins[b]))
        for cp in sent:
            cp.start()
        for j, chip in enumerate(near):
            blk = _dev_index(chip[0], chip[1], c)
            for b in range(nb):
                copy(b, 1 + j, blk, (x, y, c)).wait_recv()
                sent.append(copy(b, 4 + j, blk, sibling))
                sent[-1].start()
        for b in range(nb):
            sent.append(copy(b, 3, _dev_index(relay_from[0], relay_from[1], c), (relay_to[0], relay_to[1], c)))
            sent[-1].start()
        blk = _dev_index(far[0], far[1], c)
        for b in range(nb):
            copy(b, 3, blk, (x, y, c)).wait_recv()
            sent.append(copy(b, 6, blk, sibling))
            sent[-1].start()
        for b in range(nb):
            copy(b, 0, _dev_index(x, y, 1 - c), (x, y, c)).wait_recv()
        for j, chip in enumerate(near + [far]):
            blk = _dev_index(chip[0], chip[1], 1 - c)
            for b in range(nb):
                copy(b, 4 + j, blk, (x, y, c)).wait_recv()
        for cp in sent:
            cp.wait_send()
        for mine in started:
            mine.wait()

    any_spec = pl.BlockSpec(memory_space=pl.ANY)
    return pl.pallas_call(
        body, name=name,
        out_shape=[jax.ShapeDtypeStruct((N_DEV,) + b.shape, b.dtype) for b in bufs],
        in_specs=[any_spec] * nb, out_specs=[any_spec] * nb,
        scratch_shapes=[pltpu.SemaphoreType.DMA((nb, 7)), pltpu.SemaphoreType.DMA((nb, 7)),
                        pltpu.SemaphoreType.DMA((nb,))],
    )(*bufs)


N_CHIP = 4
CHIP_FLIPS = [(1, 0), (0, 1), (1, 1)]


def _exchange_sibling(bufs, name):
    nb = len(bufs)

    def body(*refs):
        ins, outs = refs[:nb], refs[nb:2 * nb]
        send_sems, recv_sems = refs[2 * nb:]
        x, y, c = _my_pos()

        def copy(b, k):
            return pltpu.make_async_remote_copy(
                src_ref=ins[b].at[2 * k + (1 - c)], dst_ref=outs[b].at[k],
                send_sem=send_sems.at[b, k], recv_sem=recv_sems.at[b, k],
                device_id=(x, y, 1 - c), device_id_type=MESH)

        cps = [copy(b, k) for b in range(nb) for k in range(N_CHIP)]
        for cp in cps:
            cp.start()
        for cp in cps:
            cp.wait()

    any_spec = pl.BlockSpec(memory_space=pl.ANY)
    return pl.pallas_call(
        body, name=name,
        out_shape=[jax.ShapeDtypeStruct((N_CHIP,) + b.shape[1:], b.dtype) for b in bufs],
        in_specs=[any_spec] * nb, out_specs=[any_spec] * nb,
        scratch_shapes=[pltpu.SemaphoreType.DMA((nb, N_CHIP)), pltpu.SemaphoreType.DMA((nb, N_CHIP))],
    )(*bufs)


def _pair_add(mine, recv, core, name):
    _, r, cdim = mine.shape
    tr, tc, by_rows = _tiles_2d(r, cdim)
    pick = (lambda i: (i, 0)) if by_rows else (lambda i: (0, i))

    def body(core_ref, a_ref, b_ref, o_ref):
        o_ref[0] = (a_ref[0].astype(F32) + b_ref[0].astype(F32)).astype(o_ref.dtype)

    return pl.pallas_call(
        body, name=name,
        grid_spec=pltpu.PrefetchScalarGridSpec(
            num_scalar_prefetch=1, grid=(N_CHIP, (r // tr) * (cdim // tc)),
            in_specs=[pl.BlockSpec((1, tr, tc), lambda k, i, core_ref: (2 * k + core_ref[0],) + pick(i)),
                      pl.BlockSpec((1, tr, tc), lambda k, i, core_ref: (k,) + pick(i))],
            out_specs=pl.BlockSpec((1, tr, tc), lambda k, i, core_ref: (k,) + pick(i))),
        out_shape=jax.ShapeDtypeStruct((N_CHIP, r, cdim), mine.dtype),
        compiler_params=_cparams(("parallel", "parallel")),
    )(core, mine, recv)


def _chip_peer(x, y, f):
    return ((1 - x) if f[0] else x), ((1 - y) if f[1] else y)


def _exchange_chips_start(bufs, name):
    nb = len(bufs)
    nsem = 2 * 3 * nb

    def body(*refs):
        ins, lands = refs[:nb], refs[nb:2 * nb]
        sems = refs[2 * nb:2 * nb + nsem]
        token = refs[-1]
        x, y, c = _my_pos()
        for b in range(nb):
            for j, f in enumerate(CHIP_FLIPS):
                px, py = _chip_peer(x, y, f)
                pltpu.make_async_remote_copy(
                    src_ref=ins[b].at[2 * px + py], dst_ref=lands[b].at[2 * x + y],
                    send_sem=sems[2 * (3 * b + j)], recv_sem=sems[2 * (3 * b + j) + 1],
                    device_id=(px, py, c), device_id_type=MESH).start()
        token[...] = jnp.zeros_like(token)

    hbm = pl.BlockSpec(memory_space=pltpu.HBM)
    sem = pl.BlockSpec(memory_space=pltpu.SEMAPHORE)
    out = pl.pallas_call(
        body, name=name,
        out_shape=(*([pltpu.SemaphoreType.DMA(())] * nsem),
                   *[pltpu.HBM(b.shape, b.dtype) for b in bufs], *[pltpu.HBM(b.shape, b.dtype) for b in bufs],
                   jax.ShapeDtypeStruct((8, LANES), F32)),
        in_specs=[hbm] * (2 * nb),
        out_specs=(*([sem] * nsem), *([hbm] * (2 * nb)), pl.BlockSpec(memory_space=pltpu.VMEM)),
        input_output_aliases={i: nsem + i for i in range(2 * nb)},
        compiler_params=pltpu.CompilerParams(has_side_effects=pltpu.SideEffectType.DATAFLOW_SIDE_EFFECTING),
    )(*[pltpu.with_memory_space_constraint(b, pltpu.HBM) for b in bufs],
      *[pltpu.with_memory_space_constraint(lax.empty(b.shape, b.dtype), pltpu.HBM) for b in bufs])
    return out[:nsem], out[nsem:nsem + nb], out[nsem + nb:nsem + 2 * nb], out[-1]


def _exchange_chips_wait(sems, thru, lands, after, name):
    nb = len(thru)
    nsem = len(sems)

    def body(*refs):
        ins, lnd = refs[:nb], refs[nb:2 * nb]
        sem_refs = refs[2 * nb:2 * nb + nsem]
        x, y, c = _my_pos()
        for b in range(nb):
            for j, f in enumerate(CHIP_FLIPS):
                px, py = _chip_peer(x, y, f)
                cp = pltpu.make_async_remote_copy(
                    src_ref=ins[b].at[2 * px + py], dst_ref=lnd[b].at[2 * px + py],
                    send_sem=sem_refs[2 * (3 * b + j)], recv_sem=sem_refs[2 * (3 * b + j) + 1],
                    device_id=(px, py, c), device_id_type=MESH)
                cp.wait_send()
                cp.wait_recv()

    hbm = pl.BlockSpec(memory_space=pltpu.HBM)
    sem = pl.BlockSpec(memory_space=pltpu.SEMAPHORE)
    out = pl.pallas_call(
        body, name=name,
        out_shape=tuple([pltpu.HBM(b.shape, b.dtype) for b in thru] + [pltpu.HBM(b.shape, b.dtype) for b in lands]),
        in_specs=[hbm] * (2 * nb) + [sem] * nsem + [pl.BlockSpec(memory_space=pl.ANY)],
        out_specs=tuple([hbm] * (2 * nb)),
        input_output_aliases={i: i for i in range(2 * nb)},
        compiler_params=pltpu.CompilerParams(has_side_effects=pltpu.SideEffectType.DATAFLOW_SIDE_EFFECTING),
    )(*thru, *lands, *sems, after)
    return out[:nb], out[nb:]


def _bcast_start(buf, name):
    nsem = 2 * len(FLIPS)

    def body(src, land, *rest):
        sems, token = rest[:nsem], rest[-1]
        pos = _my_pos()
        for k, f in enumerate(FLIPS):
            pltpu.make_async_remote_copy(
                src_ref=src, dst_ref=land.at[_dev_index(*pos)], send_sem=sems[2 * k], recv_sem=sems[2 * k + 1],
                device_id=_flip(pos, f), device_id_type=MESH).start()
        token[...] = jnp.zeros_like(token)

    hbm = pl.BlockSpec(memory_space=pltpu.HBM)
    sem = pl.BlockSpec(memory_space=pltpu.SEMAPHORE)
    land_shape = (N_DEV,) + buf.shape
    out = pl.pallas_call(
        body, name=name,
        out_shape=(*([pltpu.SemaphoreType.DMA(())] * nsem), pltpu.HBM(buf.shape, buf.dtype),
                   pltpu.HBM(land_shape, buf.dtype), jax.ShapeDtypeStruct((8, LANES), F32)),
        in_specs=[hbm, hbm],
        out_specs=(*([sem] * nsem), hbm, hbm, pl.BlockSpec(memory_space=pltpu.VMEM)),
        input_output_aliases={0: nsem, 1: nsem + 1},
        compiler_params=pltpu.CompilerParams(has_side_effects=pltpu.SideEffectType.DATAFLOW_SIDE_EFFECTING),
    )(pltpu.with_memory_space_constraint(buf, pltpu.HBM),
      pltpu.with_memory_space_constraint(lax.empty(land_shape, buf.dtype), pltpu.HBM))
    return out[:nsem], out[nsem], out[nsem + 1], out[-1]


def _bcast_wait(sems, thru, land, after, name):
    nsem = len(sems)

    def body(src, lnd, *rest):
        sem_refs = rest[:nsem]
        pos = _my_pos()
        for k, f in enumerate(FLIPS):
            peer = _flip(pos, f)
            cp = pltpu.make_async_remote_copy(
                src_ref=src, dst_ref=lnd.at[_dev_index(*peer)], send_sem=sem_refs[2 * k],
                recv_sem=sem_refs[2 * k + 1], device_id=peer, device_id_type=MESH)
            cp.wait_send()
            cp.wait_recv()

    hbm = pl.BlockSpec(memory_space=pltpu.HBM)
    sem = pl.BlockSpec(memory_space=pltpu.SEMAPHORE)
    sent, got = pl.pallas_call(
        body, name=name,
        out_shape=(pltpu.HBM(thru.shape, thru.dtype), pltpu.HBM(land.shape, land.dtype)),
        in_specs=[hbm, hbm] + [sem] * nsem + [pl.BlockSpec(memory_space=pl.ANY)],
        out_specs=(hbm, hbm), input_output_aliases={0: 0, 1: 1},
        compiler_params=pltpu.CompilerParams(has_side_effects=pltpu.SideEffectType.DATAFLOW_SIDE_EFFECTING),
    )(thru, land, *sems, after)
    return lax.dynamic_update_slice_in_dim(got, sent[None], _dev_index(*_my_pos()), axis=0)


def _sum_slots(v, name):
    _, r, cdim = v.shape

    def body(v_ref, o_ref):
        acc = v_ref[0]
        for s in range(1, N_DEV):
            acc = acc + v_ref[s]
        o_ref[...] = acc

    return pl.pallas_call(
        body, name=name, out_shape=jax.ShapeDtypeStruct((r, cdim), F32),
        in_specs=[_full((N_DEV, r, cdim))], out_specs=_full((r, cdim)), grid=(1,),
        compiler_params=_cparams(("arbitrary",)),
    )(v)


def _mm(a, b, dims, out_dtype, tm, tn, name):
    if dims == "nn":
        (m, k), (_, n) = a.shape, b.shape
        a_spec = pl.BlockSpec((tm, k), lambda j, i: (i, 0))
        b_spec = pl.BlockSpec((k, tn), lambda j, i: (0, j))
        dn = NN
    elif dims == "nt":
        (m, k), (n, _) = a.shape, b.shape
        a_spec = pl.BlockSpec((tm, k), lambda j, i: (i, 0))
        b_spec = pl.BlockSpec((tn, k), lambda j, i: (j, 0))
        dn = NT
    else:
        (k, m), (_, n) = a.shape, b.shape
        a_spec = pl.BlockSpec((k, tm), lambda j, i: (0, i))
        b_spec = pl.BlockSpec((k, tn), lambda j, i: (0, j))
        dn = TN
    assert m % tm == 0 and n % tn == 0, (m, tm, n, tn)

    def body(a_ref, b_ref, o_ref):
        o_ref[...] = _dot(a_ref[...], b_ref[...], dn).astype(o_ref.dtype)

    return pl.pallas_call(
        body, name=name, grid=(n // tn, m // tm),
        in_specs=[a_spec, b_spec], out_specs=pl.BlockSpec((tm, tn), lambda j, i: (i, j)),
        out_shape=jax.ShapeDtypeStruct((m, n), out_dtype),
        compiler_params=_cparams(("parallel", "parallel")),
    )(a, b)


def _tiles_2d(r, cdim):
    if r % CHUNK == 0:
        return CHUNK, cdim, True
    return r, _tile(cdim, (256, 128)), False


def _mm_sum_nn(a_list, b_list, tm, tn, name):
    n_op = len(a_list)
    m, n = a_list[0].shape[0], b_list[0].shape[1]

    def body(*refs):
        acc = _dot(refs[0][...], refs[n_op][...])
        for i in range(1, n_op):
            acc = acc + _dot(refs[i][...], refs[n_op + i][...])
        refs[2 * n_op][...] = acc

    return pl.pallas_call(
        body, name=name, grid=(n // tn, m // tm),
        in_specs=([pl.BlockSpec((tm, a.shape[1]), lambda j, i: (i, 0)) for a in a_list]
                  + [pl.BlockSpec((b.shape[0], tn), lambda j, i: (0, j)) for b in b_list]),
        out_specs=pl.BlockSpec((tm, tn), lambda j, i: (i, j)),
        out_shape=jax.ShapeDtypeStruct((m, n), F32),
        compiler_params=_cparams(("parallel", "parallel")),
    )(*a_list, *b_list)


def _tile(n, prefs):
    for t in prefs:
        if n % t == 0:
            return t
    return n


def _rows3(i):
    return jnp.maximum(3 * i - 1, 0), 3 * i, 3 * i + 1


def _x_row_specs(tm, d):
    if tm == CHUNK:
        return [pl.BlockSpec((CHUNK, d), lambda i: (jnp.maximum(i - 1, 0), 0))]
    return [pl.BlockSpec((CHUNK, d), functools.partial(lambda i, k: (_rows3(i)[k], 0), k=k)) for k in range(3)]


def _prenorm_fwd(head, x2, w, tm):
    p, d = x2.shape[0] + CHUNK, x2.shape[1]
    subs = _x_row_specs(tm, d)

    def body(head_ref, *rest):
        x_refs, (w_ref, u_ref) = rest[:len(subs)], rest[len(subs):]
        i = pl.program_id(0)
        first = jnp.where(i == 0, head_ref[...], x_refs[0][...])
        h = jnp.concatenate([first] + [r[...] for r in x_refs[1:]], axis=0)
        ms = jnp.mean(h * h, axis=-1, keepdims=True)
        u_ref[...] = (h * lax.rsqrt(ms + EPS) * w_ref[...]).astype(BF16)

    return pl.pallas_call(
        body, name="prenorm_fwd", grid=(p // tm,),
        in_specs=[_full((CHUNK, d))] + subs + [_full((1, d))],
        out_specs=pl.BlockSpec((tm, d), lambda i: (i, 0)),
        out_shape=jax.ShapeDtypeStruct((p, d), BF16),
        compiler_params=_cparams(("arbitrary",)),
    )(head, *([x2] * len(subs)), w)


def _prenorm_bwd(head, x2, w, du, dout):
    p, d = x2.shape[0] + CHUNK, x2.shape[1]

    def body(head_ref, x_ref, w_ref, du_ref, dout_ref, gx_ref, ghead_ref, gw_ref):
        i = pl.program_id(0)
        h = jnp.where(i == 0, head_ref[...], x_ref[...])
        rstd = lax.rsqrt(jnp.mean(h * h, axis=-1, keepdims=True) + EPS)
        xhat = h * rstd
        dub = du_ref[...]
        dxh = dub * w_ref[...]
        dh = rstd * (dxh - xhat * jnp.mean(dxh * xhat, axis=-1, keepdims=True)) + dout_ref[...]

        @pl.when(i == 0)
        def _():
            ghead_ref[...] = dh
            gw_ref[...] = jnp.zeros_like(gw_ref)

        gx_ref[...] = dh
        gw_ref[0:1, :] += jnp.sum(dub * xhat, axis=0, keepdims=True)

    return pl.pallas_call(
        body, name="prenorm_bwd", grid=(p // CHUNK,),
        in_specs=[_full((CHUNK, d)), pl.BlockSpec((CHUNK, d), lambda i: (jnp.maximum(i - 1, 0), 0)), _full((1, d)),
                  pl.BlockSpec((CHUNK, d), lambda i: (i, 0)), pl.BlockSpec((CHUNK, d), lambda i: (i, 0))],
        out_specs=[pl.BlockSpec((CHUNK, d), lambda i: (jnp.maximum(i - 1, 0), 0)), _full((CHUNK, d)), _full((8, d))],
        out_shape=[jax.ShapeDtypeStruct(x2.shape, F32), jax.ShapeDtypeStruct((CHUNK, d), F32),
                   jax.ShapeDtypeStruct((8, d), F32)],
        compiler_params=_cparams(("arbitrary",)),
    )(head, x2, w, du, dout)


def _conv_pre(ext_ref, cw_ref, cb_ref):
    pre = cb_ref[...] + cw_ref[CONV_K - 1:CONV_K, :] * ext_ref[8:8 + CHUNK, :]
    for j in range(1, CONV_K):
        pre = pre + cw_ref[CONV_K - 1 - j:CONV_K - j, :] * ext_ref[8 - j:8 - j + CHUNK, :]
    return pre


def _ssd_scalars(dtf_ref, brow_ref, alog_ref, rowmask, hs, ha, tri):
    lane = lax.broadcasted_iota(jnp.int32, (1, LANES), 1)
    is_dt = lane < hs
    is_f = (lane >= hs) & (lane < hs + ha)
    dtr = dtf_ref[...] + brow_ref[...]
    sp = _softplus(dtr)
    dt = jnp.where(is_dt, sp, 0.0) * rowmask
    logf = jnp.where(is_f, jnp.minimum(dtr, 0.0) - jnp.log(1.0 + jnp.exp(-jnp.abs(dtr))), 0.0) * rowmask
    a_row = jnp.where(is_dt, -jnp.exp(alog_ref[...]), 0.0)
    run = _dot_tri(tri, dt * a_row + logf)
    return dtr, dt, a_row, run, is_dt, is_f


def _tri_mats():
    r = lax.broadcasted_iota(jnp.int32, (CHUNK, CHUNK), 0)
    c = lax.broadcasted_iota(jnp.int32, (CHUNK, CHUNK), 1)
    return r, c


def _ssd_fwd(xbc, z, dtf, conv_w, conv_b, brow, alog, dskip_l, ssd_norm, sel_t, hs, ha):
    p, cd = xbc.shape
    ds = z.shape[1]
    ns = (cd - ds) // (2 * SSD_GROUPS)
    gw = ds // SSD_GROUPS
    nch = p // CHUNK
    hpg = hs // SSD_GROUPS

    def body(xbc_ref, halo_ref, z_ref, dtf_ref, cw_ref, cb_ref, brow_ref, alog_ref, dsk_ref, nrm_ref, selt_ref,
             y_ref, yssd_ref, hin_ref, cf_ref, pre_ref, st_ref, carry_ref, yacc_ref, xc_s, ex_s, xdtb_s, xwb_s, ext_s):
        c = pl.program_id(0)

        @pl.when(c == 0)
        def _():
            st_ref[...] = jnp.zeros_like(st_ref)
            carry_ref[...] = jnp.zeros_like(carry_ref)

        rows = lax.broadcasted_iota(jnp.int32, (CHUNK, 1), 0)
        rowmask = jnp.where((rows >= PADN) | (c > 0), 1.0, 0.0)
        ri, ci = _tri_mats()
        causal = ri >= ci
        tri = jnp.where(causal, 1.0, 0.0).astype(BF16)

        ext_s[0:8, :] = halo_ref[...].astype(F32)[HALO - 8:, :] * jnp.where(c > 0, 1.0, 0.0)
        ext_s[8:, :] = xbc_ref[...].astype(F32)
        pre = _conv_pre(ext_s, cw_ref, cb_ref)
        pre_ref[...] = pre.astype(BF16)
        xc_s[...] = pre * _sigmoid(pre) * rowmask

        dtr, dt, a_row, run, is_dt, is_f = _ssd_scalars(dtf_ref, brow_ref, alog_ref, rowmask, hs, ha, tri)
        cf = run + carry_ref[...]
        cf_ref[...] = cf
        carry_ref[...] = jnp.where(is_f, cf[CHUNK - 1:CHUNK, :], 0.0)
        cs = jnp.where(is_dt, run, 0.0)
        cl = cs[CHUNK - 1:CHUNK, :]
        selt = selt_ref[...]
        ex_s[...] = _dot_sel(jnp.exp(cs), selt)
        cdec_x = _dot_sel(jnp.broadcast_to(jnp.exp(cl), (8, LANES)), selt)[0:1, :]
        cs_t = cs.T
        xdt = xc_s[:, :ds] * _dot_sel(dt, selt)
        xdtb_s[...] = xdt.astype(BF16)
        xwb_s[...] = (xdt * _dot_sel(jnp.exp(cl - cs), selt)).astype(BF16)

        lane = lax.broadcasted_iota(jnp.int32, (1, LANES), 1)
        half0 = lane < HEAD_DIM
        for g in range(SSD_GROUPS):
            bg = xc_s[:, ds + g * ns: ds + (g + 1) * ns].astype(BF16)
            cg = xc_s[:, ds + SSD_GROUPS * ns + g * ns: ds + SSD_GROUPS * ns + (g + 1) * ns].astype(BF16)
            gm = _dot(cg, bg, NT)
            gs = slice(g * gw, (g + 1) * gw)
            stg = st_ref[:, gs]
            stg_b = stg.astype(BF16)
            hin_ref[0, :, gs] = stg_b
            yoff = _dot(cg, stg_b) * ex_s[:, gs]
            for pr in range(gw // LANES):
                sl = slice(g * gw + pr * LANES, g * gw + (pr + 1) * LANES)
                xp = xdtb_s[:, sl]
                yd = jnp.zeros((CHUNK, LANES), F32)
                for j in range(2):
                    h = g * hpg + 2 * pr + j
                    seg = cs[:, h:h + 1] - cs_t[h:h + 1, :]
                    m = jnp.where(causal, gm * jnp.exp(jnp.minimum(seg, 0.0)), 0.0).astype(BF16)
                    sel = half0 if j == 0 else jnp.logical_not(half0)
                    yd = yd + _dot(m, jnp.where(sel, xp, jnp.zeros_like(xp)))
                yacc_ref[:, sl] = yd + yoff[:, pr * LANES:(pr + 1) * LANES] + dsk_ref[:, sl] * xc_s[:, sl]
            st_ref[:, gs] = stg * cdec_x[:, gs] + _dot(bg, xwb_s[:, gs], TN)

        y = yacc_ref[...]
        y_ref[...] = y.astype(BF16)
        zf = z_ref[...].astype(F32)
        u = y * zf * _sigmoid(zf)
        for g in range(SSD_GROUPS):
            gs = slice(g * gw, (g + 1) * gw)
            ug = u[:, gs]
            ms = jnp.mean(ug * ug, axis=-1, keepdims=True)
            yssd_ref[:, gs] = (ug * lax.rsqrt(ms + EPS) * nrm_ref[:, gs]).astype(BF16)

    rb = CHUNK // HALO
    return pl.pallas_call(
        body, name="ssd_fwd", grid=(nch,),
        in_specs=[pl.BlockSpec((CHUNK, cd), lambda c: (c, 0)),
                  pl.BlockSpec((HALO, cd), lambda c: (jnp.maximum(c * rb - 1, 0), 0)),
                  pl.BlockSpec((CHUNK, ds), lambda c: (c, 0)),
                  pl.BlockSpec((CHUNK, LANES), lambda c: (c, 0)),
                  _full((CONV_K, cd)), _full((1, cd)), _full((1, LANES)), _full((1, LANES)),
                  _full((1, ds)), _full((1, ds)), _full((LANES, ds))],
        out_specs=[pl.BlockSpec((CHUNK, ds), lambda c: (c, 0)), pl.BlockSpec((CHUNK, ds), lambda c: (c, 0)),
                   pl.BlockSpec((1, ns, ds), lambda c: (c, 0, 0)), pl.BlockSpec((CHUNK, LANES), lambda c: (c, 0)),
                   pl.BlockSpec((CHUNK, cd), lambda c: (c, 0))],
        out_shape=[jax.ShapeDtypeStruct((p, ds), BF16), jax.ShapeDtypeStruct((p, ds), BF16),
                   jax.ShapeDtypeStruct((nch, ns, ds), BF16), jax.ShapeDtypeStruct((p, LANES), F32),
                   jax.ShapeDtypeStruct((p, cd), BF16)],
        scratch_shapes=[pltpu.VMEM((ns, ds), F32), pltpu.VMEM((1, LANES), F32), pltpu.VMEM((CHUNK, ds), F32),
                        pltpu.VMEM((CHUNK, cd), F32), pltpu.VMEM((CHUNK, ds), F32),
                        pltpu.VMEM((CHUNK, ds), BF16), pltpu.VMEM((CHUNK, ds), BF16),
                        pltpu.VMEM((8 + CHUNK, cd), F32)],
        compiler_params=_cparams(("arbitrary",)),
    )(xbc, xbc, z, dtf, conv_w, conv_b, brow, alog, dskip_l, ssd_norm, sel_t)


def _ssd_bwd(dyssd, y, z, xbc, pre, dtf, hin, dcf, conv_w, brow, alog, dskip_l, ssd_norm, sel_t, sel, hs, ha):
    p, cd = xbc.shape
    ds = z.shape[1]
    ns = (cd - ds) // (2 * SSD_GROUPS)
    gw = ds // SSD_GROUPS
    nch = p // CHUNK
    hpg = hs // SSD_GROUPS

    def body(dyssd_ref, y_ref, z_ref, xbc_ref, pre_ref, dtf_ref, hin_ref, dcf_ref, cw_ref, brow_ref,
             alog_ref, dsk_ref, nrm_ref, selt_ref, sel_ref,
             dxbc_ref, dz_ref, ddtf_ref, gcw_ref, gcb_ref, gnrm_ref, gsm_ref,
             dst_ref, nxt_ref, fcar_ref, gdsk_ref, dxc_ref, xc_s, dsl_s, dtx_s, ex_s, wx_s, dy_s, xdtb_s, xwb_s,
             dyb_s, dyeb_s):
        step = pl.program_id(0)
        c = nch - 1 - step

        @pl.when(step == 0)
        def _():
            dst_ref[...] = jnp.zeros_like(dst_ref)
            nxt_ref[...] = jnp.zeros_like(nxt_ref)
            fcar_ref[...] = jnp.zeros_like(fcar_ref)
            gdsk_ref[...] = jnp.zeros_like(gdsk_ref)
            gcw_ref[...] = jnp.zeros_like(gcw_ref)
            gcb_ref[...] = jnp.zeros_like(gcb_ref)
            gnrm_ref[...] = jnp.zeros_like(gnrm_ref)
            gsm_ref[...] = jnp.zeros_like(gsm_ref)

        rows = lax.broadcasted_iota(jnp.int32, (CHUNK, 1), 0)
        rowmask = jnp.where((rows >= PADN) | (c > 0), 1.0, 0.0)
        ri, ci = _tri_mats()
        causal = ri >= ci
        anti = ci >= ri
        tri = jnp.where(causal, 1.0, 0.0).astype(BF16)
        rtri = jnp.where(anti, 1.0, 0.0).astype(BF16)

        pre = pre_ref[...].astype(F32)
        sg = _sigmoid(pre)
        xc_s[...] = pre * sg * rowmask
        dsl_s[...] = sg * (1.0 + pre * (1.0 - sg)) * rowmask

        dtr, dt, a_row, run, is_dt, is_f = _ssd_scalars(dtf_ref, brow_ref, alog_ref, rowmask, hs, ha, tri)
        cs = jnp.where(is_dt, run, 0.0)
        cl = cs[CHUNK - 1:CHUNK, :]
        selt = selt_ref[...]
        selm = sel_ref[...]
        dtx_s[...] = _dot_sel(dt, selt)
        ex_s[...] = _dot_sel(jnp.exp(cs), selt)
        wx_s[...] = _dot_sel(jnp.exp(cl - cs), selt)
        cdec = jnp.exp(cl)
        cdec_x = _dot_sel(jnp.broadcast_to(cdec, (8, LANES)), selt)[0:1, :]
        cs_t = cs.T
        xdt = xc_s[:, :ds] * dtx_s[...]
        xdtb_s[...] = xdt.astype(BF16)
        xwb_s[...] = (xdt * wx_s[...]).astype(BF16)

        yv = y_ref[...].astype(F32)
        zf = z_ref[...].astype(F32)
        sz = _sigmoid(zf)
        u = yv * zf * sz
        dyo = dyssd_ref[...].astype(F32)
        du_parts = []
        for g in range(SSD_GROUPS):
            gs = slice(g * gw, (g + 1) * gw)
            ug = u[:, gs]
            rstd = lax.rsqrt(jnp.mean(ug * ug, axis=-1, keepdims=True) + EPS)
            yhat = ug * rstd
            dyg = dyo[:, gs]
            gnrm_ref[0:1, gs] += jnp.sum(dyg * yhat, axis=0, keepdims=True)
            dyh = dyg * nrm_ref[:, gs]
            du_parts.append(rstd * (dyh - yhat * jnp.mean(dyh * yhat, axis=-1, keepdims=True)))
        du = jnp.concatenate(du_parts, axis=1)
        dy = du * zf * sz
        dz_ref[...] = (du * yv * sz * (1.0 + zf * (1.0 - sz))).astype(BF16)
        dy_s[...] = dy
        dyb_s[...] = dy.astype(BF16)
        dyeb_s[...] = (dy * ex_s[...]).astype(BF16)
        gdsk_ref[...] += jnp.sum(dy * xc_s[:, :ds], axis=0, keepdims=True)
        lane = lax.broadcasted_iota(jnp.int32, (1, LANES), 1)
        half0 = lane < HEAD_DIM
        x_parts, yo_parts, t4_parts = [], [], []
        dcs = jnp.zeros((CHUNK, LANES), F32)
        for g in range(SSD_GROUPS):
            gs = slice(g * gw, (g + 1) * gw)
            bsl = slice(ds + g * ns, ds + (g + 1) * ns)
            csl = slice(ds + SSD_GROUPS * ns + g * ns, ds + SSD_GROUPS * ns + (g + 1) * ns)
            bg = xc_s[:, bsl].astype(BF16)
            cg = xc_s[:, csl].astype(BF16)
            gm = _dot(cg, bg, NT)
            gm_t = _dot(bg, cg, NT)
            stg_b = hin_ref[0, :, gs]
            dstg = dst_ref[:, gs]
            dstg_b = dstg.astype(BF16)
            t4_parts.append(jnp.sum(dstg * stg_b.astype(F32), axis=0, keepdims=True))
            zst = _dot(bg, dstg_b) * wx_s[:, gs]
            x_parts.append(xc_s[:, gs] * dtx_s[:, gs] * zst)
            yo_parts.append(dy_s[:, gs] * (_dot(cg, stg_b) * ex_s[:, gs]))
            dgsum = jnp.zeros((CHUNK, CHUNK), F32)
            dgtsum = jnp.zeros((CHUNK, CHUNK), F32)
            for pr in range(gw // LANES):
                sl = slice(g * gw + pr * LANES, g * gw + (pr + 1) * LANES)
                xp = xdtb_s[:, sl]
                dyp = dyb_s[:, sl]
                dxd = zst[:, pr * LANES:(pr + 1) * LANES]
                for j in range(2):
                    h = g * hpg + 2 * pr + j
                    sel_l = half0 if j == 0 else jnp.logical_not(half0)
                    seg = cs[:, h:h + 1] - cs_t[h:h + 1, :]
                    lm = jnp.where(causal, jnp.exp(jnp.minimum(seg, 0.0)), 0.0)
                    lmt = jnp.where(anti, jnp.exp(jnp.minimum(-seg, 0.0)), 0.0)
                    dyp_m = jnp.where(sel_l, dyp, jnp.zeros_like(dyp))
                    xp_m = jnp.where(sel_l, xp, jnp.zeros_like(xp))
                    dxd = dxd + _dot((gm_t * lmt).astype(BF16), dyp_m)
                    dg = _dot(dyp_m, xp, NT) * lm
                    dgt = _dot(xp_m, dyp, NT) * lmt
                    dgsum = dgsum + dg
                    dgtsum = dgtsum + dgt
                    qrow = (jnp.sum(dg * gm, axis=1, keepdims=True) - jnp.sum(dgt * gm_t, axis=1, keepdims=True))
                    dcs = dcs + jnp.where(lane == h, qrow, 0.0)
                dxc_ref[:, sl] = dxd
            dxc_ref[:, csl] = _dot(dgsum.astype(BF16), bg) + _dot(dyeb_s[:, gs], stg_b, NT)
            dxc_ref[:, bsl] = _dot(dgtsum.astype(BF16), cg) + _dot(xwb_s[:, gs], dstg_b, NT)
            dst_ref[:, gs] = dstg * cdec_x[:, gs] + _dot(cg, dyeb_s[:, gs], TN)

        dxdt = dxc_ref[:, :ds]
        xst = _dot_sel(jnp.concatenate(x_parts, axis=1), selm)
        yo = _dot_sel(jnp.concatenate(yo_parts, axis=1), selm)
        t4 = _dot_sel(jnp.concatenate([jnp.concatenate(t4_parts, axis=1), jnp.zeros((7, ds), F32)], axis=0), selm)
        dcl = jnp.sum(xst, axis=0, keepdims=True) + cdec * t4[0:1, :]
        dcs = dcs + yo - xst + jnp.where(rows == CHUNK - 1, dcl, 0.0)
        da_ = _dot_tri(rtri, dcs)
        ddt = _dot_sel(dxdt * xc_s[:, :ds], selm) + da_ * a_row
        dcf_blk = dcf_ref[...]
        dlogf = _dot_tri(rtri, dcf_blk) + fcar_ref[...]
        fcar_ref[...] += jnp.sum(dcf_blk, axis=0, keepdims=True)
        sgd = _sigmoid(dtr)
        ddtf = (jnp.where(is_dt, ddt * sgd, 0.0) + jnp.where(is_f, dlogf * (1.0 - sgd), 0.0)) * rowmask
        ddtf_ref[...] = ddtf
        gsm_ref[0:1, :] += jnp.sum(ddtf, axis=0, keepdims=True)
        gsm_ref[1:2, :] += jnp.sum(da_ * dt, axis=0, keepdims=True) * a_row

        dxc_ref[:, :ds] = dxdt * dtx_s[...] + dsk_ref[...] * dy_s[...]
        dpre = dxc_ref[...] * dsl_s[...]
        nxt_ref[0:CHUNK, :] = dpre
        gcb_ref[0:1, :] += jnp.sum(dpre, axis=0, keepdims=True)
        xr = xbc_ref[...].astype(F32)
        gcw_ref[CONV_K - 1:CONV_K, :] += jnp.sum(dpre * xr, axis=0, keepdims=True)
        dxr = cw_ref[CONV_K - 1:CONV_K, :] * dpre
        for j in range(1, CONV_K):
            up = nxt_ref[j:j + CHUNK, :]
            gcw_ref[CONV_K - 1 - j:CONV_K - j, :] += jnp.sum(up * xr, axis=0, keepdims=True)
            dxr = dxr + cw_ref[CONV_K - 1 - j:CONV_K - j, :] * up
        nxt_ref[CHUNK:, :] = dpre[0:8, :]
        dxbc_ref[...] = dxr.astype(BF16)

        @pl.when(step == nch - 1)
        def _():
            gsm_ref[2:3, :] = _dot_sel(jnp.broadcast_to(gdsk_ref[...], (8, ds)), selm)[0:1, :]

    rev = lambda s: nch - 1 - s
    blk = lambda w: pl.BlockSpec((CHUNK, w), lambda s: (rev(s), 0))
    return pl.pallas_call(
        body, name="ssd_bwd", grid=(nch,),
        in_specs=[blk(ds), blk(ds), blk(ds), blk(cd), blk(cd),
                  blk(LANES), pl.BlockSpec((1, ns, ds), lambda s: (rev(s), 0, 0)), blk(LANES),
                  _full((CONV_K, cd)), _full((1, LANES)), _full((1, LANES)),
                  _full((1, ds)), _full((1, ds)), _full((LANES, ds)), _full((ds, LANES))],
        out_specs=[blk(cd), blk(ds), blk(LANES), _full((8, cd)), _full((8, cd)), _full((8, ds)), _full((8, LANES))],
        out_shape=[jax.ShapeDtypeStruct((p, cd), BF16), jax.ShapeDtypeStruct((p, ds), BF16),
                   jax.ShapeDtypeStruct((p, LANES), F32), jax.ShapeDtypeStruct((8, cd), F32),
                   jax.ShapeDtypeStruct((8, cd), F32), jax.ShapeDtypeStruct((8, ds), F32),
                   jax.ShapeDtypeStruct((8, LANES), F32)],
        scratch_shapes=[pltpu.VMEM((ns, ds), F32), pltpu.VMEM((CHUNK + 8, cd), F32), pltpu.VMEM((1, LANES), F32),
                        pltpu.VMEM((1, ds), F32), pltpu.VMEM((CHUNK, cd), F32),
                        pltpu.VMEM((CHUNK, cd), F32), pltpu.VMEM((CHUNK, cd), F32),
                        pltpu.VMEM((CHUNK, ds), F32), pltpu.VMEM((CHUNK, ds), F32), pltpu.VMEM((CHUNK, ds), F32),
                        pltpu.VMEM((CHUNK, ds), F32), pltpu.VMEM((CHUNK, ds), BF16), pltpu.VMEM((CHUNK, ds), BF16),
                        pltpu.VMEM((CHUNK, ds), BF16), pltpu.VMEM((CHUNK, ds), BF16)],
        compiler_params=_cparams(("arbitrary",)),
    )(dyssd, y, z, xbc, pre, dtf, hin, dcf, conv_w, brow, alog, dskip_l, ssd_norm, sel_t, sel)


def _attn_fwd(q, k, v, ck, blk):
    p, da = q.shape
    npair, nkb = ck.shape[0], ck.shape[1]
    scale = 1.0 / math.sqrt(HEAD_DIM)

    def body(q_ref, k_ref, v_ref, ck_ref, o_ref, lse_ref):
        i = pl.program_id(1)
        lane = lax.broadcasted_iota(jnp.int32, (1, LANES), 1)
        sels = [lane < HEAD_DIM, lane >= HEAD_DIM]
        ones = [jnp.where(lane == HEAD_DIM, 1.0, 0.0).astype(BF16), jnp.where(lane == 0, 1.0, 0.0).astype(BF16)]
        qb = q_ref[...] * scale
        cmask = (lax.broadcasted_iota(jnp.int32, (blk, blk), 1) <= lax.broadcasted_iota(jnp.int32, (blk, blk), 0))

        def step(kb, carry, masked, nk=1):
            r0 = pl.multiple_of(kb * blk, blk)
            ks = k_ref[pl.ds(r0, nk * blk), :]
            vs = v_ref[pl.ds(r0, nk * blk), :]
            kk = jnp.concatenate([jnp.where(sel, ks, jnp.zeros_like(ks)) for sel in sels], axis=0)
            s_both = _dot(qb, kk, NT)
            out = []
            for j in range(2):
                m, acc = carry[2 * j], carry[2 * j + 1]
                ckr = jnp.concatenate([ck_ref[0, kb + t, j:j + 1, :] for t in range(nk)], axis=1)
                s = s_both[:, j * nk * blk:(j + 1) * nk * blk] - ckr
                if masked:
                    s = jnp.where(cmask, s, NEG)
                mn = jnp.maximum(m, jnp.max(s, axis=-1, keepdims=True))
                pr = jnp.exp(s - mn).astype(BF16)
                acc = jnp.exp(m - mn) * acc + _dot(pr, jnp.where(sels[j], vs, ones[j]))
                out += [mn, acc]
            return tuple(out)

        init = (jnp.full((blk, 1), NEG, F32), jnp.zeros((blk, LANES), F32)) * 2
        n4 = i // 4
        n2 = (i - 4 * n4) // 2
        carry = lax.fori_loop(0, n4, lambda t, c: step(4 * t, c, False, 4), init)
        carry = lax.fori_loop(0, n2, lambda t, c: step(4 * n4 + 2 * t, c, False, 2), carry)
        carry = lax.fori_loop(4 * n4 + 2 * n2, i, lambda kb, c: step(kb, c, False), carry)
        m0, a0, m1, a1 = step(i, carry, True)
        l0 = a0[:, HEAD_DIM:HEAD_DIM + 1]
        l1 = a1[:, 0:1]
        o_ref[...] = jnp.where(sels[0], a0 / l0, a1 / l1).astype(BF16)
        lse_ref[...] = jnp.where(sels[0], m0 + jnp.log(l0), m1 + jnp.log(l1))

    return pl.pallas_call(
        body, name="attn_fwd", grid=(npair, p // blk),
        in_specs=[pl.BlockSpec((blk, LANES), lambda h, i: (i, h)),
                  pl.BlockSpec((p, LANES), lambda h, i: (0, h)), pl.BlockSpec((p, LANES), lambda h, i: (0, h)),
                  pl.BlockSpec((1, nkb, 8, blk), lambda h, i: (h, 0, 0, 0))],
        out_specs=[pl.BlockSpec((blk, LANES), lambda h, i: (i, h)), pl.BlockSpec((blk, LANES), lambda h, i: (i, h))],
        out_shape=[jax.ShapeDtypeStruct((p, da), BF16), jax.ShapeDtypeStruct((p, da), F32)],
        compiler_params=_cparams(("parallel", "arbitrary")),
    )(q, k, v, ck)


def _attn_bwd(q, k, v, o, do, lse_rep, ck, blk):
    p, da = q.shape
    npair, nkb = ck.shape[0], ck.shape[1]
    nq = p // blk
    scale = 1.0 / math.sqrt(HEAD_DIM)

    def body(k_ref, v_ref, q_ref, do_ref, o_ref, lse_ref, ck_ref, dk_ref, dv_ref, dq_ref, dcs_ref, rsum_ref, dq_acc):
        jb = pl.program_id(1)

        @pl.when(jb == 0)
        def _():
            dq_acc[...] = jnp.zeros_like(dq_acc)

        ks = k_ref[...]
        vs = v_ref[...]
        lane = lax.broadcasted_iota(jnp.int32, (1, LANES), 1)
        sels = [lane < HEAD_DIM, lane >= HEAD_DIM]
        ones = [jnp.where(lane == HEAD_DIM, 1.0, 0.0).astype(BF16), jnp.where(lane == 0, 1.0, 0.0).astype(BF16)]
        kss = ks * scale
        kmo = [jnp.where(sels[j], kss, ones[j]) for j in range(2)]
        cmask = (lax.broadcasted_iota(jnp.int32, (blk, blk), 1) <= lax.broadcasted_iota(jnp.int32, (blk, blk), 0))

        def step(ib, carry, masked, nb=1):
            rows = nb * blk
            r0 = pl.multiple_of(ib * blk, blk)
            qb = q_ref[pl.ds(r0, rows), :] * scale
            dob = do_ref[pl.ds(r0, rows), :]
            prod = dob.astype(F32) * o_ref[pl.ds(r0, rows), :].astype(F32)
            out = []
            for j in range(2):
                dk, dv = carry[2 * j], carry[2 * j + 1]
                qm = jnp.where(sels[j], qb, jnp.zeros_like(qb))
                dom = jnp.where(sels[j], dob, jnp.zeros_like(dob))
                lse = lse_ref[pl.ds(r0, rows), HEAD_DIM * j:HEAD_DIM * j + 1]
                dlt = jnp.sum(jnp.where(sels[j], prod, 0.0), axis=-1, keepdims=True)
                s = _dot(qm, ks, NT) - ck_ref[0, 0, j:j + 1, :] - lse
                pm = jnp.exp(jnp.minimum(s, 0.0))
                if masked:
                    pm = jnp.where(cmask, pm, 0.0)
                ds_b = (pm * (_dot(dom, vs, NT) - dlt)).astype(BF16)
                dv = dv + _dot(pm.astype(BF16), dom, TN)
                dk = dk + _dot(ds_b, jnp.where(sels[j], qb, ones[j]), TN)
                dq_acc[pl.ds(r0, rows), LANES * j:LANES * (j + 1)] += _dot(ds_b, kmo[j])
                out += [dk, dv]
            return tuple(out)

        zero = jnp.zeros((blk, LANES), F32)
        carry = step(jb, (zero, zero, zero, zero), True)
        n4 = (nq - 1 - jb) // 4
        n2 = (nq - 1 - jb - 4 * n4) // 2
        carry = lax.fori_loop(0, n4, lambda t, c: step(jb + 1 + 4 * t, c, False, 4), carry)
        carry = lax.fori_loop(0, n2, lambda t, c: step(jb + 1 + 4 * n4 + 2 * t, c, False, 2), carry)
        dk0, dv0, dk1, dv1 = lax.fori_loop(jb + 1 + 4 * n4 + 2 * n2, nq, lambda ib, c: step(ib, c, False), carry)
        dk_ref[...] = jnp.where(sels[0], dk0, dk1).astype(BF16)
        dv_ref[...] = (dv0 + dv1).astype(BF16)
        pair8 = lambda c0, c1: jnp.where(lane == 0, c0, jnp.where(lane == 1, c1, 0.0)).T[0:8]
        dcs_ref[0] = pair8(dk0[:, HEAD_DIM:HEAD_DIM + 1], dk1[:, 0:1])

        @pl.when(jb == nkb - 1)
        def _():
            a0 = dq_acc[:, :LANES]
            a1 = dq_acc[:, LANES:]
            dq_ref[...] = jnp.where(sels[0], a0, a1).astype(BF16)
            rsum_ref[0] = pair8(a0[:, HEAD_DIM:HEAD_DIM + 1], a1[:, 0:1])

    colblk = pl.BlockSpec((blk, LANES), lambda h, j: (j, h))
    colfull = pl.BlockSpec((p, LANES), lambda h, j: (0, h))
    ckspec = pl.BlockSpec((1, 1, 8, blk), lambda h, j: (h, j, 0, 0))
    return pl.pallas_call(
        body, name="attn_bwd", grid=(npair, nkb),
        in_specs=[colblk, colblk, colfull, colfull, colfull, colfull, ckspec],
        out_specs=[colblk, colblk, colfull, pl.BlockSpec((1, 8, blk), lambda h, j: (h, 0, j)),
                   pl.BlockSpec((1, 8, p), lambda h, j: (h, 0, 0))],
        out_shape=[jax.ShapeDtypeStruct((p, da), BF16), jax.ShapeDtypeStruct((p, da), BF16),
                   jax.ShapeDtypeStruct((p, da), BF16), jax.ShapeDtypeStruct((npair, 8, p), F32),
                   jax.ShapeDtypeStruct((npair, 8, p), F32)],
        scratch_shapes=[pltpu.VMEM((p, 2 * LANES), F32)],
        compiler_params=_cparams(("parallel", "arbitrary")),
    )(k, v, q, do, o, lse_rep, ck)


def _tail_fwd(yssd, o, zatt, graw, head, x2, tgt2, wps, wpa, wout, gate_bias, norm_post, tm):
    p, ds = yssd.shape
    da = o.shape[1]
    d = x2.shape[1]
    nsub = tm // CHUNK

    def body(yssd_ref, o_ref, zatt_ref, g_ref, head_ref, *rest):
        x_refs, t_refs = rest[:nsub], rest[nsub:2 * nsub]
        (wps_ref, wpa_ref, wout_ref, gb_ref, np_ref,
         yatt_ref, mrg_ref, a_ref, b_ref, dzo_ref, dout_ref, red_ref) = rest[2 * nsub:]
        i = pl.program_id(0)

        @pl.when(i == 0)
        def _():
            red_ref[...] = jnp.zeros_like(red_ref)

        first = jnp.where(i == 0, head_ref[...], x_refs[0][...])
        h = jnp.concatenate([first] + [r[...] for r in x_refs[1:]], axis=0)
        tgt = jnp.concatenate([r[...] for r in t_refs], axis=0)
        rows = lax.broadcasted_iota(jnp.int32, (tm, 1), 0)
        valid = jnp.where((i > 0) | (rows >= CHUNK), 1.0, 0.0)
        ob = o_ref[...].astype(F32)
        za = zatt_ref[...].astype(F32)
        yatt_b = (ob * za * _sigmoid(za)).astype(BF16)
        yatt_ref[...] = yatt_b
        a = _dot(yssd_ref[...], wps_ref[...])
        b = _dot(yatt_b, wpa_ref[...])
        a_ref[...] = a.astype(BF16)
        b_ref[...] = b.astype(BF16)
        gr = g_ref[...].astype(F32) + gb_ref[...]
        mrg_b = (_sigmoid(gr[:, :d]) * a + _sigmoid(gr[:, d:]) * b).astype(BF16)
        mrg_ref[...] = mrg_b
        zo = _dot(mrg_b, wout_ref[...])
        rstd = lax.rsqrt(jnp.mean(zo * zo, axis=-1, keepdims=True) + EPS)
        zh = zo * rstd
        npw = np_ref[...]
        err = (h + zh * npw - tgt) * valid
        dout = err * (1.0 / d)
        dout_ref[...] = dout
        dzh = dout * npw
        dzo_ref[...] = (rstd * (dzh - zh * jnp.mean(dzh * zh, axis=-1, keepdims=True))).astype(BF16)
        red_ref[0:1, :] += jnp.sum(dout * zh, axis=0, keepdims=True)
        red_ref[1:2, 0:1] += jnp.sum(jnp.sum(err * err, axis=1, keepdims=True), axis=0, keepdims=True) * (0.5 / d)

    row = lambda w: pl.BlockSpec((tm, w), lambda i: (i, 0))
    once = lambda shape: pl.BlockSpec(shape, lambda i: (0,) * len(shape), pipeline_mode=pl.Buffered(1))
    subs = _x_row_specs(tm, d)
    sd = jax.ShapeDtypeStruct
    return pl.pallas_call(
        body, name="tail_fwd", grid=(p // tm,),
        in_specs=[row(ds), row(da), row(da), row(2 * d), _full((CHUNK, d))] + subs + subs
                 + [once((ds, d)), once((da, d)), once((d, d)), _full((1, 2 * d)), _full((1, d))],
        out_specs=[row(da), row(d), row(d), row(d), row(d), row(d), _full((8, d))],
        out_shape=[sd((p, da), BF16), sd((p, d), BF16), sd((p, d), BF16), sd((p, d), BF16), sd((p, d), BF16),
                   sd((p, d), F32), sd((8, d), F32)],
        compiler_params=_cparams(("arbitrary",)),
    )(yssd, o, zatt, graw, head, *([x2] * nsub), *([tgt2] * nsub), wps, wpa, wout, gate_bias, norm_post)


def _tail_bwd(dzo, a_b, b_b, graw, o, zatt, wps, wpa, wout, gate_bias, tm):
    p, d = dzo.shape
    ds, da = wps.shape[0], wpa.shape[0]

    def body(dzo_ref, a_ref, b_ref, g_ref, o_ref, zatt_ref, wps_ref, wpa_ref, wout_ref, gb_ref,
             da_ref, db_ref, dg_ref, dyssd_ref, do_ref, dzatt_ref, red_ref):
        i = pl.program_id(0)

        @pl.when(i == 0)
        def _():
            red_ref[...] = jnp.zeros_like(red_ref)

        gr = g_ref[...].astype(F32) + gb_ref[...]
        gs = _sigmoid(gr[:, :d])
        ga = _sigmoid(gr[:, d:])
        dm = _dot(dzo_ref[...], wout_ref[...], NT)
        da_b = (gs * dm).astype(BF16)
        db_b = (ga * dm).astype(BF16)
        da_ref[...] = da_b
        db_ref[...] = db_b
        dgs = dm * a_ref[...].astype(F32) * gs * (1.0 - gs)
        dga = dm * b_ref[...].astype(F32) * ga * (1.0 - ga)
        dg_ref[:, :d] = dgs.astype(BF16)
        dg_ref[:, d:] = dga.astype(BF16)
        red_ref[0:1, :d] += jnp.sum(dgs, axis=0, keepdims=True)
        red_ref[0:1, d:] += jnp.sum(dga, axis=0, keepdims=True)
        dyssd_ref[...] = _dot(da_b, wps_ref[...], NT).astype(BF16)
        dya = _dot(db_b, wpa_ref[...], NT)
        ob = o_ref[...].astype(F32)
        za = zatt_ref[...].astype(F32)
        sza = _sigmoid(za)
        do_ref[...] = (dya * za * sza).astype(BF16)
        dzatt_ref[...] = (dya * ob * sza * (1.0 + za * (1.0 - sza))).astype(BF16)

    row = lambda w: pl.BlockSpec((tm, w), lambda i: (i, 0))
    once = lambda shape: pl.BlockSpec(shape, lambda i: (0,) * len(shape), pipeline_mode=pl.Buffered(1))
    sd = jax.ShapeDtypeStruct
    return pl.pallas_call(
        body, name="tail_bwd", grid=(p // tm,),
        in_specs=[row(d), row(d), row(d), row(2 * d), row(da), row(da),
                  once((ds, d)), once((da, d)), once((d, d)), _full((1, 2 * d))],
        out_specs=[row(d), row(d), row(2 * d), row(ds), row(da), row(da), _full((8, 2 * d))],
        out_shape=[sd((p, d), BF16), sd((p, d), BF16), sd((p, 2 * d), BF16), sd((p, ds), BF16), sd((p, da), BF16),
                   sd((p, da), BF16), sd((8, 2 * d), F32)],
        compiler_params=_cparams(("arbitrary",)),
    )(dzo, a_b, b_b, graw, o, zatt, wps, wpa, wout, gate_bias)


def _adamw_math(w, g, m, v):
    m2 = ADAM_B1 * m + (1.0 - ADAM_B1) * g
    v2 = ADAM_B2 * v + (1.0 - ADAM_B2) * (g * g)
    m_hat = m2 / (1.0 - ADAM_B1 ** ADAM_STEP)
    v_hat = v2 / (1.0 - ADAM_B2 ** ADAM_STEP)
    delta = -ADAM_LR * (m_hat / (jnp.sqrt(v_hat) + ADAM_EPS) + ADAM_WD * w)
    return delta, m2, v2


def _adamw_small(params, red, name):
    names = list(params)
    n = len(names)
    extra = [params[k][3] for k in names if not isinstance(params[k][3], tuple)]

    def body(*refs):
        w_refs, m_refs, v_refs = refs[:n], refs[n:2 * n], refs[2 * n:3 * n]
        red_ref = refs[3 * n]
        g_refs = iter(refs[3 * n + 1:3 * n + 1 + len(extra)])
        outs = refs[3 * n + 1 + len(extra):]
        for i, k in enumerate(names):
            where = params[k][3]
            rows, cols = w_refs[i].shape
            if isinstance(where, tuple):
                g = red_ref[where[0]:where[0] + rows, where[1]:where[1] + cols]
            else:
                g = next(g_refs)[...]
            delta, m2, v2 = _adamw_math(w_refs[i][...], g, m_refs[i][...], v_refs[i][...])
            for o, val in zip(outs[4 * i:4 * i + 4], (g, delta, m2, v2)):
                o[...] = val

    vm = pl.BlockSpec(memory_space=pltpu.VMEM)
    ws, ms, vs = ([params[k][j] for k in names] for j in range(3))
    out = pl.pallas_call(
        body, name=name,
        out_shape=[jax.ShapeDtypeStruct(w.shape, F32) for w in ws for _ in range(4)],
        in_specs=[vm] * (3 * n + 1 + len(extra)), out_specs=[vm] * (4 * n),
    )(*ws, *ms, *vs, red, *extra)
    return {k: tuple(out[4 * i:4 * i + 4]) for i, k in enumerate(names)}


def _adamw(w, g, m, v, name, parts=False, part_row0=0):
    r, cdim = w.shape
    tr, tc, by_rows = _tiles_2d(r, cdim)
    pick = (lambda i: (i, 0)) if by_rows else (lambda i: (0, i))
    assert part_row0 % tr == 0
    gpick = (lambda i: (i + part_row0 // tr, 0)) if by_rows else (lambda i: (part_row0 // tr, i))

    def body(w_ref, g_ref, m_ref, v_ref, go_ref, d_ref, mo_ref, vo_ref):
        if parts:
            g = g_ref[0].astype(F32)
            for s in range(1, g_ref.shape[0]):
                g = g + g_ref[s].astype(F32)
        else:
            g = g_ref[...]
        delta, m2, v2 = _adamw_math(w_ref[...], g, m_ref[...], v_ref[...])
        go_ref[...] = g
        d_ref[...] = delta
        mo_ref[...] = m2
        vo_ref[...] = v2

    blk = pl.BlockSpec((tr, tc), pick)
    gspec = pl.BlockSpec((g.shape[0], tr, tc), lambda i: (0,) + gpick(i)) if parts else blk
    return pl.pallas_call(
        body, name=name, grid=((r // tr) * (cdim // tc),),
        in_specs=[blk, gspec, blk, blk], out_specs=[blk] * 4,
        out_shape=[jax.ShapeDtypeStruct((r, cdim), F32)] * 4,
        compiler_params=_cparams(("parallel",)),
    )(w, g, m, v)


def _pad_cols(a, width):
    return jnp.pad(a, ((0, 0), (0, width - a.shape[1])))


def _pack_small_shard(conv_w_sh, meta_sh, width):
    return jnp.concatenate([_pad_cols(conv_w_sh, width), jnp.zeros((4, width), F32), _pad_cols(meta_sh, width)], axis=0)


def _pack_small_rep(norm_pre, norm_post, gate_bias, ssd_norm, conv_b, misc, width):
    rows = [norm_pre, norm_post, gate_bias, ssd_norm, conv_b, misc]
    return jnp.concatenate([_pad_cols(r, width) for r in rows] + [jnp.zeros((2, width), F32)], axis=0)


def kernel(x, meta_tokens, norm_pre, w_in, conv_w, conv_b, dt_bias, a_log, d_skip, ssd_norm, fgate_bias, gate_bias, w_proj_ssd, w_proj_att, w_out, norm_post, loss_target, m_meta_tokens, m_norm_pre, m_w_in, m_conv_w, m_conv_b, m_dt_bias, m_a_log, m_d_skip, m_ssd_norm, m_fgate_bias, m_gate_bias, m_w_proj_ssd, m_w_proj_att, m_w_out, m_norm_post, v_meta_tokens, v_norm_pre, v_w_in, v_conv_w, v_conv_b, v_dt_bias, v_a_log, v_d_skip, v_ssd_norm, v_fgate_bias, v_gate_bias, v_w_proj_ssd, v_w_proj_att, v_w_out, v_norm_post):
    seq, d = x.shape[1], x.shape[2]
    p = seq + CHUNK
    hs, ha = dt_bias.shape[1], fgate_bias.shape[1]
    ds, cd = ssd_norm.shape[1], conv_b.shape[1]
    da = ha * HEAD_DIM
    nc8 = w_in.shape[2]
    cws = cd // N_DEV
    msh = d // N_DEV
    r1, r2, r3 = ds // N_DEV, da // N_DEV, d // N_DEV
    me = _dev_index(*_my_pos())
    x2, tgt2 = x[0], loss_target[0]

    win_sh = jnp.transpose(w_in[0]).astype(BF16)
    rows_sh = jnp.concatenate([w_proj_ssd[0], w_proj_att[0], w_out[0]], axis=0).astype(BF16)
    small_sh = _pack_small_shard(conv_w[0], meta_tokens, cws)
    win_all, small_all = _all_gather([win_sh, small_sh], "gather_weights")
    rows_sh, win_all = lax.optimization_barrier((rows_sh, win_all))
    rows_sems, rows_thru, rows_land, rows_token = _bcast_start(rows_sh, "gather_rows_start")
    cuts = [0, ds, ds + cd, ds + cd + hs, ds + cd + hs + da, ds + cd + hs + 2 * da, ds + cd + hs + 3 * da,
            ds + cd + hs + 4 * da, ds + cd + hs + 4 * da + ha, ds + cd + hs + 4 * da + ha + 2 * d]

    def piece_rows(r0, r1):
        parts = [win_all[s, max(r0, s * nc8) - s * nc8:min(r1, (s + 1) * nc8) - s * nc8]
                 for s in range(N_DEV) if max(r0, s * nc8) < min(r1, (s + 1) * nc8)]
        return parts[0] if len(parts) == 1 else jnp.concatenate(parts, axis=0)

    w_z, w_xbc, w_dt, w_zatt, w_q, w_k, w_v, w_f, w_g = [piece_rows(cuts[i], cuts[i + 1]) for i in range(9)]
    w_dtf = jnp.concatenate([w_dt, w_f, jnp.zeros((LANES - hs - ha, d), BF16)], axis=0)
    conv_w_full = jnp.transpose(small_all[:, 0:CONV_K, :], (1, 0, 2)).reshape(CONV_K, cd)
    meta_full = jnp.transpose(small_all[:, 8:8 + N_META, :msh], (1, 0, 2)).reshape(N_META, d)
    head = jnp.concatenate([jnp.zeros((PADN, d), F32), meta_full + rows_token[0:1, 0:1]], axis=0)

    tm = _att_block(p)
    u = _prenorm_fwd(head, x2, norm_pre, tm)
    seg_w = [w_z, w_xbc, w_zatt, w_q, w_k, w_v, w_g]
    zs, xbc, zatt, q, k, v, graw = [
        _mm(u, w, "nt", BF16, _tile(p, (1408, tm)), _tile(w.shape[0], (1024, 512, 256, 128)), "inproj_%d" % i)
        for i, w in enumerate(seg_w)]
    dtf = _mm(u, w_dtf, "nt", F32, _tile(p, (1408, tm)), LANES, "inproj_dtf")

    brow = jnp.concatenate([dt_bias, fgate_bias, jnp.zeros((1, LANES - hs - ha), F32)], axis=1)
    alog_row = _pad_cols(a_log, LANES)
    dskip_l = jnp.repeat(d_skip, HEAD_DIM, axis=1)
    sel_t = (lax.broadcasted_iota(jnp.int32, (LANES, ds), 1) // HEAD_DIM
             == lax.broadcasted_iota(jnp.int32, (LANES, ds), 0)).astype(BF16)
    sel = sel_t.T
    y, yssd, hin, cf, pre = _ssd_fwd(xbc, zs, dtf, conv_w_full, conv_b, brow, alog_row, dskip_l, ssd_norm, sel_t, hs, ha)

    blk = _att_block(p)
    nkb, npair = p // blk, ha // 2
    cum = jnp.where(lax.broadcasted_iota(jnp.int32, (p, 1), 0) < PADN, -NEG, cf[:, hs:hs + ha])
    ck = jnp.transpose(cum.T.reshape(npair, 2, nkb, blk), (0, 2, 1, 3))
    ck = jnp.pad(ck, ((0, 0), (0, 0), (0, 6), (0, 0)))
    o, lse_rep = _attn_fwd(q, k, v, ck, blk)

    rows_all = _bcast_wait(rows_sems, rows_thru, rows_land, lse_rep, "gather_rows_wait")
    wps = rows_all[:, :r1].reshape(ds, d)
    wpa = rows_all[:, r1:r1 + r2].reshape(da, d)
    wout = rows_all[:, r1 + r2:].reshape(d, d)

    yatt, mrg, a_b, b_b, dzo, dout, red_fwd = _tail_fwd(
        yssd, o, zatt, graw, head, x2, tgt2, wps, wpa, wout, gate_bias, norm_post, tm)
    da_, db_, dgraw, dyssd, d_o, dzatt, red_bwd = _tail_bwd(dzo, a_b, b_b, graw, o, zatt, wps, wpa, wout, gate_bias, tm)

    tw = _tile(d, (512, 256, 128))
    g_wout = _mm(mrg, dzo, "tn", BF16, tw, d, "wgrad_out")
    g_wps = _mm(yssd, da_, "tn", BF16, _tile(ds, (512, 256, 128)), d, "wgrad_ps")
    g_wpa = _mm(yatt, db_, "tn", BF16, _tile(da, (512, 256, 128)), d, "wgrad_pa")

    dk, dv, dq, dcs, rsum = _attn_bwd(q, k, v, o, d_o, lse_rep, ck, blk)
    dcum = (rsum - dcs)[:, 0:2, :].reshape(ha, p).T
    dcf = jnp.pad(dcum, ((0, 0), (hs, LANES - hs - ha)))
    dxbc, dzs, ddtf, gcw, gcb, gnrm, gsm = _ssd_bwd(
        dyssd, y, zs, xbc, pre, dtf, hin, dcf, conv_w_full, brow, alog_row, dskip_l, ssd_norm, sel_t, sel, hs, ha)
    ddtf_b = ddtf.astype(BF16)

    dsegs = [dzs, dxbc, dzatt, dq, dk, dv, dgraw, ddtf_b]
    gsegs = [_mm(dsg, u, "tn", BF16, _tile(dsg.shape[1], (512, 256, 128)), d, "wgrad_in_%d" % i)
             for i, dsg in enumerate(dsegs)]
    g_z, g_xbc, g_zatt, g_q, g_k, g_v, g_g, g_dtf = gsegs
    gw_full = jnp.concatenate([g_z, g_xbc, g_dtf[:hs], g_zatt, g_q, g_k, g_v, g_dtf[hs:hs + ha], g_g], axis=0)
    gwin_parts = gw_full.reshape(N_DEV, nc8, d)
    grows_parts = jnp.concatenate([g_wps.reshape(N_DEV, r1, d), g_wpa.reshape(N_DEV, r2, d),
                                   g_wout.reshape(N_DEV, r3, d)], axis=1)

    core = lax.axis_index("c").astype(jnp.int32).reshape(1)
    sib_win, sib_rows = _exchange_sibling([gwin_parts, grows_parts], "scatter_grads_sibling")
    chip_win = _pair_add(gwin_parts, sib_win, core, "pair_add_w_in")
    chip_rows = _pair_add(grows_parts, sib_rows, core, "pair_add_rows")
    sems, thru, lands, token = _exchange_chips_start([chip_win, chip_rows], "scatter_grads_start")
    dsegs_after = dsegs[:-1] + [ddtf_b + token[0:1, 0:1].astype(BF16)]
    du = _mm_sum_nn(dsegs_after, seg_w + [w_dtf], tm, _tile(d, (512, 256, 128)), "dgrad_in")
    gx, ghead, gnp = _prenorm_bwd(head, x2, norm_pre, du, dout)
    sent, got = _exchange_chips_wait(sems, thru, lands, gnp, "scatter_grads_wait")
    chip = me // 2
    recv_win, recv_rows = [lax.dynamic_update_slice_in_dim(g, lax.dynamic_slice_in_dim(s, chip, 1, axis=0), chip, axis=0)
                           for g, s in zip(got, sent)]
    gmisc = jnp.concatenate([gsm[0:1], gsm[1:2], gsm[2:3], _pad_cols(red_fwd[1:2, 0:1], LANES)], axis=1)
    small_g = jnp.concatenate([
        _pack_small_rep(gnp[0:1], red_fwd[0:1], red_bwd[0:1], gnrm[0:1], gcb[0:1], gmisc, cd),
        _pad_cols(gcw[0:CONV_K], cd), jnp.zeros((4, cd), F32), _pad_cols(ghead[PADN:], cd)], axis=0)
    sg_sems, sg_thru, sg_land, sg_token = _bcast_start(small_g, "reduce_small_start")

    upd_in = _adamw(jnp.transpose(w_in[0]) + sg_token[0:1, 0:1], recv_win, jnp.transpose(m_w_in[0]),
                    jnp.transpose(v_w_in[0]), "adamw_w_in", parts=True)
    upd_ps = _adamw(w_proj_ssd[0] + sg_token[0:1, 0:1], recv_rows, m_w_proj_ssd[0], v_w_proj_ssd[0],
                    "adamw_w_proj_ssd", parts=True, part_row0=0)
    upd_pa = _adamw(w_proj_att[0], recv_rows, m_w_proj_att[0], v_w_proj_att[0], "adamw_w_proj_att", parts=True,
                    part_row0=r1)
    upd_out = _adamw(w_out[0], recv_rows, m_w_out[0], v_w_out[0], "adamw_w_out", parts=True, part_row0=r1 + r2)
    all_done = upd_in[1][0:8, 0:LANES] + upd_ps[1][0:8, 0:LANES] + upd_pa[1][0:8, 0:LANES] + upd_out[1][0:8, 0:LANES]
    red = _sum_slots(_bcast_wait(sg_sems, sg_thru, sg_land, all_done, "reduce_small_wait"), "reduce_small_sum")
    loss = red[5, 3 * LANES]
    g_conv_w = lax.dynamic_slice_in_dim(red[8:8 + CONV_K], me * cws, cws, axis=1)
    g_meta = lax.dynamic_slice_in_dim(red[16:16 + N_META, :d], me * msh, msh, axis=1)
    small = {
        "meta_tokens": (meta_tokens, m_meta_tokens, v_meta_tokens, g_meta),
        "norm_pre": (norm_pre, m_norm_pre, v_norm_pre, (0, 0)),
        "conv_w": (conv_w[0], m_conv_w[0], v_conv_w[0], g_conv_w),
        "conv_b": (conv_b, m_conv_b, v_conv_b, (4, 0)),
        "dt_bias": (dt_bias, m_dt_bias, v_dt_bias, (5, 0)),
        "a_log": (a_log, m_a_log, v_a_log, (5, LANES)),
        "d_skip": (d_skip, m_d_skip, v_d_skip, (5, 2 * LANES)),
        "ssd_norm": (ssd_norm, m_ssd_norm, v_ssd_norm, (3, 0)),
        "fgate_bias": (fgate_bias, m_fgate_bias, v_fgate_bias, (5, hs)),
        "gate_bias": (gate_bias, m_gate_bias, v_gate_bias, (2, 0)),
        "norm_post": (norm_post, m_norm_post, v_norm_post, (1, 0)),
    }
    upd_small = _adamw_small(small, red, "adamw_small")

    def leaves(i):
        sm = {k: v[i] for k, v in upd_small.items()}
        return [sm["meta_tokens"], sm["norm_pre"], jnp.transpose(upd_in[i])[None], sm["conv_w"][None], sm["conv_b"],
                sm["dt_bias"], sm["a_log"], sm["d_skip"], sm["ssd_norm"], sm["fgate_bias"], sm["gate_bias"],
                upd_ps[i][None], upd_pa[i][None], upd_out[i][None], sm["norm_post"]]

    return tuple([loss, gx[None]] + leaves(0) + leaves(1) + leaves(2) + leaves(3))
```

```python
import functools
import math

import jax
import jax.numpy as jnp
from jax import lax
from jax.experimental import pallas as pl
from jax.experimental.pallas import tpu as pltpu

F32 = jnp.float32
BF16 = jnp.bfloat16

N_DEV = 8
N_META = 16
CHUNK = 128
PADN = CHUNK - N_META
HEAD_DIM = 64
SSD_GROUPS = 4
CONV_K = 4
EPS = 1e-6
NEG = -1e30
LANES = 128
HALO = 16

ADAM_LR = 0.001
ADAM_B1 = 0.9
ADAM_B2 = 0.999
ADAM_EPS = 1e-08
ADAM_WD = 0.01
ADAM_STEP = 10

VMEM_LIMIT = 56 * 1024 * 1024

NN = (((1,), (0,)), ((), ()))
NT = (((1,), (1,)), ((), ()))
TN = (((0,), (0,)), ((), ()))
MESH = pl.DeviceIdType.MESH


def _dot(a, b, dims=NN):
    return lax.dot_general(a, b, dims, preferred_element_type=F32)


def _split2(x):
    hi = x.astype(BF16)
    lo = (x - hi.astype(F32)).astype(BF16)
    return hi, lo


def _dot_sel(x, sel):
    hi, lo = _split2(x)
    return _dot(hi, sel) + _dot(lo, sel)


def _dot_tri(tri, x):
    h1 = x.astype(BF16)
    r1 = x - h1.astype(F32)
    h2 = r1.astype(BF16)
    h3 = (r1 - h2.astype(F32)).astype(BF16)
    return _dot(tri, h1) + _dot(tri, h2) + _dot(tri, h3)


def _sigmoid(x):
    return 0.5 * jnp.tanh(0.5 * x) + 0.5


def _softplus(x):
    return jnp.maximum(x, 0.0) + jnp.log(1.0 + jnp.exp(-jnp.abs(x)))


def _cparams(sem=None, vmem=VMEM_LIMIT):
    kw = {"vmem_limit_bytes": vmem}
    if sem is not None:
        kw["dimension_semantics"] = sem
    return pltpu.CompilerParams(**kw)


def _full(shape):
    nd = len(shape)
    return pl.BlockSpec(shape, lambda *_: (0,) * nd)


def _att_block(p):
    return 384 if p % 384 == 0 else CHUNK


def _my_pos():
    return lax.axis_index("x"), lax.axis_index("y"), lax.axis_index("c")


def _dev_index(x, y, c):
    return 4 * x + 2 * y + c


FLIPS = [(fx, fy, fc) for fx in (0, 1) for fy in (0, 1) for fc in (0, 1)][1:]


def _flip(pos, f):
    return tuple((1 - p) if fi else p for p, fi in zip(pos, f))


def _all_gather(bufs, name):
    nb = len(bufs)

    def body(*refs):
        ins, outs = refs[:nb], refs[nb:2 * nb]
        send_sems, recv_sems, local_sems = refs[2 * nb:]
        x, y, c = _my_pos()
        me = _dev_index(x, y, c)
        sibling = (x, y, 1 - c)
        near = [(1 - x, y), (x, 1 - y)]
        far = (1 - x, 1 - y)
        relay_from = (c * (1 - x) + (1 - c) * x, c * y + (1 - c) * (1 - y))
        relay_to = (c * x + (1 - c) * (1 - x), c * (1 - y) + (1 - c) * y)

        def copy(b, k, block_idx, to, src=None):
            dst = outs[b].at[block_idx]
            return pltpu.make_async_remote_copy(
                src_ref=dst if src is None else src, dst_ref=dst,
                send_sem=send_sems.at[b, k], recv_sem=recv_sems.at[b, k],
                device_id=to, device_id_type=MESH)

        started = []
        for b in range(nb):
            mine = pltpu.make_async_copy(ins[b], outs[b].at[me], local_sems.at[b])
            mine.start()
            started.append(mine)
        sent = []
        for b in range(nb):
            sent.append(copy(b, 0, me, sibling, src=ins[b]))
            for j, chip in enumerate(near):
                sent.append(copy(b, 1 + j, me, (chip[0], chip[1], c), src=ins[b]))
        for cp in sent:
            cp.start()
        for j, chip in enumerate(near):
            blk = _dev_index(chip[0], chip[1], c)
            for b in range(nb):
                copy(b, 1 + j, blk, (x, y, c)).wait_recv()
                sent.append(copy(b, 4 + j, blk, sibling))
                sent[-1].start()
        for b in range(nb):
            sent.append(copy(b, 3, _dev_index(relay_from[0], relay_from[1], c), (relay_to[0], relay_to[1], c)))
            sent[-1].start()
        blk = _dev_index(far[0], far[1], c)
        for b in range(nb):
            copy(b, 3, blk, (x, y, c)).wait_recv()
            sent.append(copy(b, 6, blk, sibling))
            sent[-1].start()
        for b in range(nb):
            copy(b, 0, _dev_index(x, y, 1 - c), (x, y, c)).wait_recv()
        for j, chip in enumerate(near + [far]):
            blk = _dev_index(chip[0], chip[1], 1 - c)
            for b in range(nb):
                copy(b, 4 + j, blk, (x, y, c)).wait_recv()
        for cp in sent:
            cp.wait_send()
        for mine in started:
            mine.wait()

    any_spec = pl.BlockSpec(memory_space=pl.ANY)
    return pl.pallas_call(
        body, name=name,
        out_shape=[jax.ShapeDtypeStruct((N_DEV,) + b.shape, b.dtype) for b in bufs],
        in_specs=[any_spec] * nb, out_specs=[any_spec] * nb,
        scratch_shapes=[pltpu.SemaphoreType.DMA((nb, 7)), pltpu.SemaphoreType.DMA((nb, 7)),
                        pltpu.SemaphoreType.DMA((nb,))],
    )(*bufs)


N_CHIP = 4
CHIP_FLIPS = [(1, 0), (0, 1), (1, 1)]


def _exchange_sibling(bufs, name):
    nb = len(bufs)

    def body(*refs):
        ins, outs = refs[:nb], refs[nb:2 * nb]
        send_sems, recv_sems = refs[2 * nb:]
        x, y, c = _my_pos()

        def copy(b, k):
            return pltpu.make_async_remote_copy(
                src_ref=ins[b].at[2 * k + (1 - c)], dst_ref=outs[b].at[k],
                send_sem=send_sems.at[b, k], recv_sem=recv_sems.at[b, k],
                device_id=(x, y, 1 - c), device_id_type=MESH)

        cps = [copy(b, k) for b in range(nb) for k in range(N_CHIP)]
        for cp in cps:
            cp.start()
        for cp in cps:
            cp.wait()

    any_spec = pl.BlockSpec(memory_space=pl.ANY)
    return pl.pallas_call(
        body, name=name,
        out_shape=[jax.ShapeDtypeStruct((N_CHIP,) + b.shape[1:], b.dtype) for b in bufs],
        in_specs=[any_spec] * nb, out_specs=[any_spec] * nb,
        scratch_shapes=[pltpu.SemaphoreType.DMA((nb, N_CHIP)), pltpu.SemaphoreType.DMA((nb, N_CHIP))],
    )(*bufs)


def _pair_add(mine, recv, core, name):
    _, r, cdim = mine.shape
    tr, tc = r, cdim
    pick = lambda i: (i, 0)

    def body(core_ref, a_ref, b_ref, o_ref):
        o_ref[0] = (a_ref[0].astype(F32) + b_ref[0].astype(F32)).astype(o_ref.dtype)

    return pl.pallas_call(
        body, name=name,
        grid_spec=pltpu.PrefetchScalarGridSpec(
            num_scalar_prefetch=1, grid=(N_CHIP, (r // tr) * (cdim // tc)),
            in_specs=[pl.BlockSpec((1, tr, tc), lambda k, i, core_ref: (2 * k + core_ref[0],) + pick(i)),
                      pl.BlockSpec((1, tr, tc), lambda k, i, core_ref: (k,) + pick(i))],
            out_specs=pl.BlockSpec((1, tr, tc), lambda k, i, core_ref: (k,) + pick(i))),
        out_shape=jax.ShapeDtypeStruct((N_CHIP, r, cdim), mine.dtype),
        compiler_params=_cparams(("parallel", "parallel")),
    )(core, mine, recv)


def _chip_peer(x, y, f):
    return ((1 - x) if f[0] else x), ((1 - y) if f[1] else y)


def _exchange_chips_start(bufs, name):
    nb = len(bufs)
    nsem = 2 * 3 * nb

    def body(*refs):
        ins, lands = refs[:nb], refs[nb:2 * nb]
        sems = refs[2 * nb:2 * nb + nsem]
        token = refs[-1]
        x, y, c = _my_pos()
        for b in range(nb):
            for j, f in enumerate(CHIP_FLIPS):
                px, py = _chip_peer(x, y, f)
                pltpu.make_async_remote_copy(
                    src_ref=ins[b].at[2 * px + py], dst_ref=lands[b].at[2 * x + y],
                    send_sem=sems[2 * (3 * b + j)], recv_sem=sems[2 * (3 * b + j) + 1],
                    device_id=(px, py, c), device_id_type=MESH).start()
        token[...] = jnp.zeros_like(token)

    hbm = pl.BlockSpec(memory_space=pltpu.HBM)
    sem = pl.BlockSpec(memory_space=pltpu.SEMAPHORE)
    out = pl.pallas_call(
        body, name=name,
        out_shape=(*([pltpu.SemaphoreType.DMA(())] * nsem),
                   *[pltpu.HBM(b.shape, b.dtype) for b in bufs], *[pltpu.HBM(b.shape, b.dtype) for b in bufs],
                   jax.ShapeDtypeStruct((8, LANES), F32)),
        in_specs=[hbm] * (2 * nb),
        out_specs=(*([sem] * nsem), *([hbm] * (2 * nb)), pl.BlockSpec(memory_space=pltpu.VMEM)),
        input_output_aliases={i: nsem + i for i in range(2 * nb)},
        compiler_params=pltpu.CompilerParams(has_side_effects=pltpu.SideEffectType.DATAFLOW_SIDE_EFFECTING),
    )(*[pltpu.with_memory_space_constraint(b, pltpu.HBM) for b in bufs],
      *[pltpu.with_memory_space_constraint(lax.empty(b.shape, b.dtype), pltpu.HBM) for b in bufs])
    return out[:nsem], out[nsem:nsem + nb], out[nsem + nb:nsem + 2 * nb], out[-1]


def _exchange_chips_wait(sems, thru, lands, after, name):
    nb = len(thru)
    nsem = len(sems)

    def body(*refs):
        ins, lnd = refs[:nb], refs[nb:2 * nb]
        sem_refs = refs[2 * nb:2 * nb + nsem]
        x, y, c = _my_pos()
        for b in range(nb):
            for j, f in enumerate(CHIP_FLIPS):
                px, py = _chip_peer(x, y, f)
                cp = pltpu.make_async_remote_copy(
                    src_ref=ins[b].at[2 * px + py], dst_ref=lnd[b].at[2 * px + py],
                    send_sem=sem_refs[2 * (3 * b + j)], recv_sem=sem_refs[2 * (3 * b + j) + 1],
                    device_id=(px, py, c), device_id_type=MESH)
                cp.wait_send()
                cp.wait_recv()

    hbm = pl.BlockSpec(memory_space=pltpu.HBM)
    sem = pl.BlockSpec(memory_space=pltpu.SEMAPHORE)
    out = pl.pallas_call(
        body, name=name,
        out_shape=tuple([pltpu.HBM(b.shape, b.dtype) for b in thru] + [pltpu.HBM(b.shape, b.dtype) for b in lands]),
        in_specs=[hbm] * (2 * nb) + [sem] * nsem + [pl.BlockSpec(memory_space=pl.ANY)],
        out_specs=tuple([hbm] * (2 * nb)),
        input_output_aliases={i: i for i in range(2 * nb)},
        compiler_params=pltpu.CompilerParams(has_side_effects=pltpu.SideEffectType.DATAFLOW_SIDE_EFFECTING),
    )(*thru, *lands, *sems, after)
    return out[:nb], out[nb:]


def _bcast_start(buf, name):
    nsem = 2 * len(FLIPS)

    def body(src, land, *rest):
        sems, token = rest[:nsem], rest[-1]
        pos = _my_pos()
        for k, f in enumerate(FLIPS):
            pltpu.make_async_remote_copy(
                src_ref=src, dst_ref=land.at[_dev_index(*pos)], send_sem=sems[2 * k], recv_sem=sems[2 * k + 1],
                device_id=_flip(pos, f), device_id_type=MESH).start()
        token[...] = jnp.zeros_like(token)

    hbm = pl.BlockSpec(memory_space=pltpu.HBM)
    sem = pl.BlockSpec(memory_space=pltpu.SEMAPHORE)
    land_shape = (N_DEV,) + buf.shape
    out = pl.pallas_call(
        body, name=name,
        out_shape=(*([pltpu.SemaphoreType.DMA(())] * nsem), pltpu.HBM(buf.shape, buf.dtype),
                   pltpu.HBM(land_shape, buf.dtype), jax.ShapeDtypeStruct((8, LANES), F32)),
        in_specs=[hbm, hbm],
        out_specs=(*([sem] * nsem), hbm, hbm, pl.BlockSpec(memory_space=pltpu.VMEM)),
        input_output_aliases={0: nsem, 1: nsem + 1},
        compiler_params=pltpu.CompilerParams(has_side_effects=pltpu.SideEffectType.DATAFLOW_SIDE_EFFECTING),
    )(pltpu.with_memory_space_constraint(buf, pltpu.HBM),
      pltpu.with_memory_space_constraint(lax.empty(land_shape, buf.dtype), pltpu.HBM))
    return out[:nsem], out[nsem], out[nsem + 1], out[-1]


def _bcast_wait(sems, thru, land, after, name):
    nsem = len(sems)

    def body(src, lnd, *rest):
        sem_refs = rest[:nsem]
        pos = _my_pos()
        for k, f in enumerate(FLIPS):
            peer = _flip(pos, f)
            cp = pltpu.make_async_remote_copy(
                src_ref=src, dst_ref=lnd.at[_dev_index(*peer)], send_sem=sem_refs[2 * k],
                recv_sem=sem_refs[2 * k + 1], device_id=peer, device_id_type=MESH)
            cp.wait_send()
            cp.wait_recv()

    hbm = pl.BlockSpec(memory_space=pltpu.HBM)
    sem = pl.BlockSpec(memory_space=pltpu.SEMAPHORE)
    sent, got = pl.pallas_call(
        body, name=name,
        out_shape=(pltpu.HBM(thru.shape, thru.dtype), pltpu.HBM(land.shape, land.dtype)),
        in_specs=[hbm, hbm] + [sem] * nsem + [pl.BlockSpec(memory_space=pl.ANY)],
        out_specs=(hbm, hbm), input_output_aliases={0: 0, 1: 1},
        compiler_params=pltpu.CompilerParams(has_side_effects=pltpu.SideEffectType.DATAFLOW_SIDE_EFFECTING),
    )(thru, land, *sems, after)
    return lax.dynamic_update_slice_in_dim(got, sent[None], _dev_index(*_my_pos()), axis=0)


def _sum_slots(v, name):
    _, r, cdim = v.shape

    def body(v_ref, o_ref):
        acc = v_ref[0]
        for s in range(1, N_DEV):
            acc = acc + v_ref[s]
        o_ref[...] = acc

    return pl.pallas_call(
        body, name=name, out_shape=jax.ShapeDtypeStruct((r, cdim), F32),
        in_specs=[_full((N_DEV, r, cdim))], out_specs=_full((r, cdim)), grid=(1,),
        compiler_params=_cparams(("arbitrary",)),
    )(v)


def _mm(a, b, dims, out_dtype, tm, tn, name):
    if dims == "nn":
        (m, k), (_, n) = a.shape, b.shape
        a_spec = pl.BlockSpec((tm, k), lambda j, i: (i, 0))
        b_spec = pl.BlockSpec((k, tn), lambda j, i: (0, j))
        dn = NN
    elif dims == "nt":
        (m, k), (n, _) = a.shape, b.shape
        a_spec = pl.BlockSpec((tm, k), lambda j, i: (i, 0))
        b_spec = pl.BlockSpec((tn, k), lambda j, i: (j, 0))
        dn = NT
    else:
        (k, m), (_, n) = a.shape, b.shape
        a_spec = pl.BlockSpec((k, tm), lambda j, i: (0, i))
        b_spec = pl.BlockSpec((k, tn), lambda j, i: (0, j))
        dn = TN
    assert m % tm == 0 and n % tn == 0, (m, tm, n, tn)

    def body(a_ref, b_ref, o_ref):
        o_ref[...] = _dot(a_ref[...], b_ref[...], dn).astype(o_ref.dtype)

    return pl.pallas_call(
        body, name=name, grid=(n // tn, m // tm),
        in_specs=[a_spec, b_spec], out_specs=pl.BlockSpec((tm, tn), lambda j, i: (i, j)),
        out_shape=jax.ShapeDtypeStruct((m, n), out_dtype),
        compiler_params=_cparams(("parallel", "parallel")),
    )(a, b)


def _tiles_2d(r, cdim):
    if r % CHUNK == 0:
        return CHUNK, cdim, True
    return r, _tile(cdim, (256, 128)), False


def _mm_sum_nn(a_list, b_list, tm, tn, name):
    n_op = len(a_list)
    m, n = a_list[0].shape[0], b_list[0].shape[1]

    def body(*refs):
        acc = _dot(refs[0][...], refs[n_op][...])
        for i in range(1, n_op):
            acc = acc + _dot(refs[i][...], refs[n_op + i][...])
        refs[2 * n_op][...] = acc

    return pl.pallas_call(
        body, name=name, grid=(n // tn, m // tm),
        in_specs=([pl.BlockSpec((tm, a.shape[1]), lambda j, i: (i, 0)) for a in a_list]
                  + [pl.BlockSpec((b.shape[0], tn), lambda j, i: (0, j)) for b in b_list]),
        out_specs=pl.BlockSpec((tm, tn), lambda j, i: (i, j)),
        out_shape=jax.ShapeDtypeStruct((m, n), F32),
        compiler_params=_cparams(("parallel", "parallel")),
    )(*a_list, *b_list)


def _tile(n, prefs):
    for t in prefs:
        if n % t == 0:
            return t
    return n


def _rows3(i):
    return jnp.maximum(3 * i - 1, 0), 3 * i, 3 * i + 1


def _x_row_specs(tm, d):
    if tm == CHUNK:
        return [pl.BlockSpec((CHUNK, d), lambda i: (jnp.maximum(i - 1, 0), 0))]
    return [pl.BlockSpec((CHUNK, d), functools.partial(lambda i, k: (_rows3(i)[k], 0), k=k)) for k in range(3)]


def _prenorm_fwd(head, x2, w, tm):
    p, d = x2.shape[0] + CHUNK, x2.shape[1]
    subs = _x_row_specs(tm, d)

    def body(head_ref, *rest):
        x_refs, (w_ref, u_ref) = rest[:len(subs)], rest[len(subs):]
        i = pl.program_id(0)
        first = jnp.where(i == 0, head_ref[...], x_refs[0][...])
        h = jnp.concatenate([first] + [r[...] for r in x_refs[1:]], axis=0)
        ms = jnp.mean(h * h, axis=-1, keepdims=True)
        u_ref[...] = (h * lax.rsqrt(ms + EPS) * w_ref[...]).astype(BF16)

    return pl.pallas_call(
        body, name="prenorm_fwd", grid=(p // tm,),
        in_specs=[_full((CHUNK, d))] + subs + [_full((1, d))],
        out_specs=pl.BlockSpec((tm, d), lambda i: (i, 0)),
        out_shape=jax.ShapeDtypeStruct((p, d), BF16),
        compiler_params=_cparams(("arbitrary",)),
    )(head, *([x2] * len(subs)), w)


def _prenorm_bwd(head, x2, w, du, dout):
    p, d = x2.shape[0] + CHUNK, x2.shape[1]

    def body(head_ref, x_ref, w_ref, du_ref, dout_ref, gx_ref, ghead_ref, gw_ref):
        i = pl.program_id(0)
        h = jnp.where(i == 0, head_ref[...], x_ref[...])
        rstd = lax.rsqrt(jnp.mean(h * h, axis=-1, keepdims=True) + EPS)
        xhat = h * rstd
        dub = du_ref[...]
        dxh = dub * w_ref[...]
        dh = rstd * (dxh - xhat * jnp.mean(dxh * xhat, axis=-1, keepdims=True)) + dout_ref[...]

        @pl.when(i == 0)
        def _():
            ghead_ref[...] = dh
            gw_ref[...] = jnp.zeros_like(gw_ref)

        gx_ref[...] = dh
        gw_ref[0:1, :] += jnp.sum(dub * xhat, axis=0, keepdims=True)

    return pl.pallas_call(
        body, name="prenorm_bwd", grid=(p // CHUNK,),
        in_specs=[_full((CHUNK, d)), pl.BlockSpec((CHUNK, d), lambda i: (jnp.maximum(i - 1, 0), 0)), _full((1, d)),
                  pl.BlockSpec((CHUNK, d), lambda i: (i, 0)), pl.BlockSpec((CHUNK, d), lambda i: (i, 0))],
        out_specs=[pl.BlockSpec((CHUNK, d), lambda i: (jnp.maximum(i - 1, 0), 0)), _full((CHUNK, d)), _full((8, d))],
        out_shape=[jax.ShapeDtypeStruct(x2.shape, F32), jax.ShapeDtypeStruct((CHUNK, d), F32),
                   jax.ShapeDtypeStruct((8, d), F32)],
        compiler_params=_cparams(("arbitrary",)),
    )(head, x2, w, du, dout)


def _conv_pre(ext_ref, cw_ref, cb_ref):
    pre = cb_ref[...] + cw_ref[CONV_K - 1:CONV_K, :] * ext_ref[8:8 + CHUNK, :]
    for j in range(1, CONV_K):
        pre = pre + cw_ref[CONV_K - 1 - j:CONV_K - j, :] * ext_ref[8 - j:8 - j + CHUNK, :]
    return pre


def _ssd_scalars(dtf_ref, brow_ref, alog_ref, rowmask, hs, ha, tri):
    lane = lax.broadcasted_iota(jnp.int32, (1, LANES), 1)
    is_dt = lane < hs
    is_f = (lane >= hs) & (lane < hs + ha)
    dtr = dtf_ref[...] + brow_ref[...]
    sp = _softplus(dtr)
    dt = jnp.where(is_dt, sp, 0.0) * rowmask
    logf = jnp.where(is_f, jnp.minimum(dtr, 0.0) - jnp.log(1.0 + jnp.exp(-jnp.abs(dtr))), 0.0) * rowmask
    a_row = jnp.where(is_dt, -jnp.exp(alog_ref[...]), 0.0)
    run = _dot_tri(tri, dt * a_row + logf)
    return dtr, dt, a_row, run, is_dt, is_f


def _tri_mats():
    r = lax.broadcasted_iota(jnp.int32, (CHUNK, CHUNK), 0)
    c = lax.broadcasted_iota(jnp.int32, (CHUNK, CHUNK), 1)
    return r, c


def _ssd_fwd(xbc, z, dtf, conv_w, conv_b, brow, alog, dskip_l, ssd_norm, sel_t, hs, ha):
    p, cd = xbc.shape
    ds = z.shape[1]
    ns = (cd - ds) // (2 * SSD_GROUPS)
    gw = ds // SSD_GROUPS
    nch = p // CHUNK
    hpg = hs // SSD_GROUPS

    def body(xbc_ref, halo_ref, z_ref, dtf_ref, cw_ref, cb_ref, brow_ref, alog_ref, dsk_ref, nrm_ref, selt_ref,
             y_ref, yssd_ref, hin_ref, cf_ref, pre_ref, st_ref, carry_ref, yacc_ref, xc_s, ex_s, xdtb_s, xwb_s, ext_s):
        c = pl.program_id(0)

        @pl.when(c == 0)
        def _():
            st_ref[...] = jnp.zeros_like(st_ref)
            carry_ref[...] = jnp.zeros_like(carry_ref)

        rows = lax.broadcasted_iota(jnp.int32, (CHUNK, 1), 0)
        rowmask = jnp.where((rows >= PADN) | (c > 0), 1.0, 0.0)
        ri, ci = _tri_mats()
        causal = ri >= ci
        tri = jnp.where(causal, 1.0, 0.0).astype(BF16)

        ext_s[0:8, :] = halo_ref[...].astype(F32)[HALO - 8:, :] * jnp.where(c > 0, 1.0, 0.0)
        ext_s[8:, :] = xbc_ref[...].astype(F32)
        pre = _conv_pre(ext_s, cw_ref, cb_ref)
        pre_ref[...] = pre.astype(BF16)
        xc_s[...] = pre * _sigmoid(pre) * rowmask

        dtr, dt, a_row, run, is_dt, is_f = _ssd_scalars(dtf_ref, brow_ref, alog_ref, rowmask, hs, ha, tri)
        cf = run + carry_ref[...]
        cf_ref[...] = cf
        carry_ref[...] = jnp.where(is_f, cf[CHUNK - 1:CHUNK, :], 0.0)
        cs = jnp.where(is_dt, run, 0.0)
        cl = cs[CHUNK - 1:CHUNK, :]
        selt = selt_ref[...]
        ex_s[...] = _dot_sel(jnp.exp(cs), selt)
        cdec_x = _dot_sel(jnp.broadcast_to(jnp.exp(cl), (8, LANES)), selt)[0:1, :]
        cs_t = cs.T
        xdt = xc_s[:, :ds] * _dot_sel(dt, selt)
        xdtb_s[...] = xdt.astype(BF16)
        xwb_s[...] = (xdt * _dot_sel(jnp.exp(cl - cs), selt)).astype(BF16)

        lane = lax.broadcasted_iota(jnp.int32, (1, LANES), 1)
        half0 = lane < HEAD_DIM
        for g in range(SSD_GROUPS):
            bg = xc_s[:, ds + g * ns: ds + (g + 1) * ns].astype(BF16)
            cg = xc_s[:, ds + SSD_GROUPS * ns + g * ns: ds + SSD_GROUPS * ns + (g + 1) * ns].astype(BF16)
            gm = _dot(cg, bg, NT)
            gs = slice(g * gw, (g + 1) * gw)
            stg = st_ref[:, gs]
            stg_b = stg.astype(BF16)
            hin_ref[0, :, gs] = stg_b
            yoff = _dot(cg, stg_b) * ex_s[:, gs]
            for pr in range(gw // LANES):
                sl = slice(g * gw + pr * LANES, g * gw + (pr + 1) * LANES)
                xp = xdtb_s[:, sl]
                yd = jnp.zeros((CHUNK, LANES), F32)
                for j in range(2):
                    h = g * hpg + 2 * pr + j
                    seg = cs[:, h:h + 1] - cs_t[h:h + 1, :]
                    m = jnp.where(causal, gm * jnp.exp(jnp.minimum(seg, 0.0)), 0.0).astype(BF16)
                    sel = half0 if j == 0 else jnp.logical_not(half0)
                    yd = yd + _dot(m, jnp.where(sel, xp, jnp.zeros_like(xp)))
                yacc_ref[:, sl] = yd + yoff[:, pr * LANES:(pr + 1) * LANES] + dsk_ref[:, sl] * xc_s[:, sl]
            st_ref[:, gs] = stg * cdec_x[:, gs] + _dot(bg, xwb_s[:, gs], TN)

        y = yacc_ref[...]
        y_ref[...] = y.astype(BF16)
        zf = z_ref[...].astype(F32)
        u = y * zf * _sigmoid(zf)
        for g in range(SSD_GROUPS):
            gs = slice(g * gw, (g + 1) * gw)
            ug = u[:, gs]
            ms = jnp.mean(ug * ug, axis=-1, keepdims=True)
            yssd_ref[:, gs] = (ug * lax.rsqrt(ms + EPS) * nrm_ref[:, gs]).astype(BF16)

    rb = CHUNK // HALO
    return pl.pallas_call(
        body, name="ssd_fwd", grid=(nch,),
        in_specs=[pl.BlockSpec((CHUNK, cd), lambda c: (c, 0)),
                  pl.BlockSpec((HALO, cd), lambda c: (jnp.maximum(c * rb - 1, 0), 0)),
                  pl.BlockSpec((CHUNK, ds), lambda c: (c, 0)),
                  pl.BlockSpec((CHUNK, LANES), lambda c: (c, 0)),
                  _full((CONV_K, cd)), _full((1, cd)), _full((1, LANES)), _full((1, LANES)),
                  _full((1, ds)), _full((1, ds)), _full((LANES, ds))],
        out_specs=[pl.BlockSpec((CHUNK, ds), lambda c: (c, 0)), pl.BlockSpec((CHUNK, ds), lambda c: (c, 0)),
                   pl.BlockSpec((1, ns, ds), lambda c: (c, 0, 0)), pl.BlockSpec((CHUNK, LANES), lambda c: (c, 0)),
                   pl.BlockSpec((CHUNK, cd), lambda c: (c, 0))],
        out_shape=[jax.ShapeDtypeStruct((p, ds), BF16), jax.ShapeDtypeStruct((p, ds), BF16),
                   jax.ShapeDtypeStruct((nch, ns, ds), BF16), jax.ShapeDtypeStruct((p, LANES), F32),
                   jax.ShapeDtypeStruct((p, cd), BF16)],
        scratch_shapes=[pltpu.VMEM((ns, ds), F32), pltpu.VMEM((1, LANES), F32), pltpu.VMEM((CHUNK, ds), F32),
                        pltpu.VMEM((CHUNK, cd), F32), pltpu.VMEM((CHUNK, ds), F32),
                        pltpu.VMEM((CHUNK, ds), BF16), pltpu.VMEM((CHUNK, ds), BF16),
                        pltpu.VMEM((8 + CHUNK, cd), F32)],
        compiler_params=_cparams(("arbitrary",)),
    )(xbc, xbc, z, dtf, conv_w, conv_b, brow, alog, dskip_l, ssd_norm, sel_t)


def _ssd_bwd(dyssd, y, z, xbc, pre, dtf, hin, dcf, conv_w, brow, alog, dskip_l, ssd_norm, sel_t, sel, hs, ha):
    p, cd = xbc.shape
    ds = z.shape[1]
    ns = (cd - ds) // (2 * SSD_GROUPS)
    gw = ds // SSD_GROUPS
    nch = p // CHUNK
    hpg = hs // SSD_GROUPS

    def body(dyssd_ref, y_ref, z_ref, xbc_ref, pre_ref, dtf_ref, hin_ref, dcf_ref, cw_ref, brow_ref,
             alog_ref, dsk_ref, nrm_ref, selt_ref, sel_ref,
             dxbc_ref, dz_ref, ddtf_ref, gcw_ref, gcb_ref, gnrm_ref, gsm_ref,
             dst_ref, nxt_ref, fcar_ref, gdsk_ref, dxc_ref, xc_s, dsl_s, dtx_s, ex_s, wx_s, dy_s, xdtb_s, xwb_s,
             dyb_s, dyeb_s):
        step = pl.program_id(0)
        c = nch - 1 - step

        @pl.when(step == 0)
        def _():
            dst_ref[...] = jnp.zeros_like(dst_ref)
            nxt_ref[...] = jnp.zeros_like(nxt_ref)
            fcar_ref[...] = jnp.zeros_like(fcar_ref)
            gdsk_ref[...] = jnp.zeros_like(gdsk_ref)
            gcw_ref[...] = jnp.zeros_like(gcw_ref)
            gcb_ref[...] = jnp.zeros_like(gcb_ref)
            gnrm_ref[...] = jnp.zeros_like(gnrm_ref)
            gsm_ref[...] = jnp.zeros_like(gsm_ref)

        rows = lax.broadcasted_iota(jnp.int32, (CHUNK, 1), 0)
        rowmask = jnp.where((rows >= PADN) | (c > 0), 1.0, 0.0)
        ri, ci = _tri_mats()
        causal = ri >= ci
        anti = ci >= ri
        tri = jnp.where(causal, 1.0, 0.0).astype(BF16)
        rtri = jnp.where(anti, 1.0, 0.0).astype(BF16)

        pre = pre_ref[...].astype(F32)
        sg = _sigmoid(pre)
        xc_s[...] = pre * sg * rowmask
        dsl_s[...] = sg * (1.0 + pre * (1.0 - sg)) * rowmask

        dtr, dt, a_row, run, is_dt, is_f = _ssd_scalars(dtf_ref, brow_ref, alog_ref, rowmask, hs, ha, tri)
        cs = jnp.where(is_dt, run, 0.0)
        cl = cs[CHUNK - 1:CHUNK, :]
        selt = selt_ref[...]
        selm = sel_ref[...]
        dtx_s[...] = _dot_sel(dt, selt)
        ex_s[...] = _dot_sel(jnp.exp(cs), selt)
        wx_s[...] = _dot_sel(jnp.exp(cl - cs), selt)
        cdec = jnp.exp(cl)
        cdec_x = _dot_sel(jnp.broadcast_to(cdec, (8, LANES)), selt)[0:1, :]
        cs_t = cs.T
        xdt = xc_s[:, :ds] * dtx_s[...]
        xdtb_s[...] = xdt.astype(BF16)
        xwb_s[...] = (xdt * wx_s[...]).astype(BF16)

        yv = y_ref[...].astype(F32)
        zf = z_ref[...].astype(F32)
        sz = _sigmoid(zf)
        u = yv * zf * sz
        dyo = dyssd_ref[...].astype(F32)
        du_parts = []
        for g in range(SSD_GROUPS):
            gs = slice(g * gw, (g + 1) * gw)
            ug = u[:, gs]
            rstd = lax.rsqrt(jnp.mean(ug * ug, axis=-1, keepdims=True) + EPS)
            yhat = ug * rstd
            dyg = dyo[:, gs]
            gnrm_ref[0:1, gs] += jnp.sum(dyg * yhat, axis=0, keepdims=True)
            dyh = dyg * nrm_ref[:, gs]
            du_parts.append(rstd * (dyh - yhat * jnp.mean(dyh * yhat, axis=-1, keepdims=True)))
        du = jnp.concatenate(du_parts, axis=1)
        dy = du * zf * sz
        dz_ref[...] = (du * yv * sz * (1.0 + zf * (1.0 - sz))).astype(BF16)
        dy_s[...] = dy
        dyb_s[...] = dy.astype(BF16)
        dyeb_s[...] = (dy * ex_s[...]).astype(BF16)
        gdsk_ref[...] += jnp.sum(dy * xc_s[:, :ds], axis=0, keepdims=True)
        lane = lax.broadcasted_iota(jnp.int32, (1, LANES), 1)
        half0 = lane < HEAD_DIM
        x_parts, yo_parts, t4_parts = [], [], []
        dcs = jnp.zeros((CHUNK, LANES), F32)
        for g in range(SSD_GROUPS):
            gs = slice(g * gw, (g + 1) * gw)
            bsl = slice(ds + g * ns, ds + (g + 1) * ns)
            csl = slice(ds + SSD_GROUPS * ns + g * ns, ds + SSD_GROUPS * ns + (g + 1) * ns)
            bg = xc_s[:, bsl].astype(BF16)
            cg = xc_s[:, csl].astype(BF16)
            gm = _dot(cg, bg, NT)
            gm_t = _dot(bg, cg, NT)
            stg_b = hin_ref[0, :, gs]
            dstg = dst_ref[:, gs]
            dstg_b = dstg.astype(BF16)
            t4_parts.append(jnp.sum(dstg * stg_b.astype(F32), axis=0, keepdims=True))
            zst = _dot(bg, dstg_b) * wx_s[:, gs]
            x_parts.append(xc_s[:, gs] * dtx_s[:, gs] * zst)
            yo_parts.append(dy_s[:, gs] * (_dot(cg, stg_b) * ex_s[:, gs]))
            dgsum = jnp.zeros((CHUNK, CHUNK), F32)
            dgtsum = jnp.zeros((CHUNK, CHUNK), F32)
            for pr in range(gw // LANES):
                sl = slice(g * gw + pr * LANES, g * gw + (pr + 1) * LANES)
                xp = xdtb_s[:, sl]
                dyp = dyb_s[:, sl]
                dxd = zst[:, pr * LANES:(pr + 1) * LANES]
                for j in range(2):
                    h = g * hpg + 2 * pr + j
                    sel_l = half0 if j == 0 else jnp.logical_not(half0)
                    seg = cs[:, h:h + 1] - cs_t[h:h + 1, :]
                    lm = jnp.where(causal, jnp.exp(jnp.minimum(seg, 0.0)), 0.0)
                    lmt = lm.T
                    dyp_m = jnp.where(sel_l, dyp, jnp.zeros_like(dyp))
                    xp_m = jnp.where(sel_l, xp, jnp.zeros_like(xp))
                    dxd = dxd + _dot((gm_t * lmt).astype(BF16), dyp_m)
                    dg = _dot(dyp_m, xp, NT) * lm
                    dgt = _dot(xp_m, dyp, NT) * lmt
                    dgsum = dgsum + dg
                    dgtsum = dgtsum + dgt
                    qrow = (jnp.sum(dg * gm, axis=1, keepdims=True) - jnp.sum(dgt * gm_t, axis=1, keepdims=True))
                    dcs = dcs + jnp.where(lane == h, qrow, 0.0)
                dxc_ref[:, sl] = dxd
            dxc_ref[:, csl] = _dot(dgsum.astype(BF16), bg) + _dot(dyeb_s[:, gs], stg_b, NT)
            dxc_ref[:, bsl] = _dot(dgtsum.astype(BF16), cg) + _dot(xwb_s[:, gs], dstg_b, NT)
            dst_ref[:, gs] = dstg * cdec_x[:, gs] + _dot(cg, dyeb_s[:, gs], TN)

        dxdt = dxc_ref[:, :ds]
        xst = _dot_sel(jnp.concatenate(x_parts, axis=1), selm)
        yo = _dot_sel(jnp.concatenate(yo_parts, axis=1), selm)
        t4 = _dot_sel(jnp.concatenate([jnp.concatenate(t4_parts, axis=1), jnp.zeros((7, ds), F32)], axis=0), selm)
        dcl = jnp.sum(xst, axis=0, keepdims=True) + cdec * t4[0:1, :]
        dcs = dcs + yo - xst + jnp.where(rows == CHUNK - 1, dcl, 0.0)
        da_ = _dot_tri(rtri, dcs)
        ddt = _dot_sel(dxdt * xc_s[:, :ds], selm) + da_ * a_row
        dcf_blk = dcf_ref[...]
        dlogf = _dot_tri(rtri, dcf_blk) + fcar_ref[...]
        fcar_ref[...] += jnp.sum(dcf_blk, axis=0, keepdims=True)
        sgd = _sigmoid(dtr)
        ddtf = (jnp.where(is_dt, ddt * sgd, 0.0) + jnp.where(is_f, dlogf * (1.0 - sgd), 0.0)) * rowmask
        ddtf_ref[...] = ddtf
        gsm_ref[0:1, :] += jnp.sum(ddtf, axis=0, keepdims=True)
        gsm_ref[1:2, :] += jnp.sum(da_ * dt, axis=0, keepdims=True) * a_row

        dxc_ref[:, :ds] = dxdt * dtx_s[...] + dsk_ref[...] * dy_s[...]
        dpre = dxc_ref[...] * dsl_s[...]
        nxt_ref[0:CHUNK, :] = dpre
        gcb_ref[0:1, :] += jnp.sum(dpre, axis=0, keepdims=True)
        xr = xbc_ref[...].astype(F32)
        gcw_ref[CONV_K - 1:CONV_K, :] += jnp.sum(dpre * xr, axis=0, keepdims=True)
        dxr = cw_ref[CONV_K - 1:CONV_K, :] * dpre
        for j in range(1, CONV_K):
            up = nxt_ref[j:j + CHUNK, :]
            gcw_ref[CONV_K - 1 - j:CONV_K - j, :] += jnp.sum(up * xr, axis=0, keepdims=True)
            dxr = dxr + cw_ref[CONV_K - 1 - j:CONV_K - j, :] * up
        nxt_ref[CHUNK:, :] = dpre[0:8, :]
        dxbc_ref[...] = dxr.astype(BF16)

        @pl.when(step == nch - 1)
        def _():
            gsm_ref[2:3, :] = _dot_sel(jnp.broadcast_to(gdsk_ref[...], (8, ds)), selm)[0:1, :]

    rev = lambda s: nch - 1 - s
    blk = lambda w: pl.BlockSpec((CHUNK, w), lambda s: (rev(s), 0))
    return pl.pallas_call(
        body, name="ssd_bwd", grid=(nch,),
        in_specs=[blk(ds), blk(ds), blk(ds), blk(cd), blk(cd),
                  blk(LANES), pl.BlockSpec((1, ns, ds), lambda s: (rev(s), 0, 0)), blk(LANES),
                  _full((CONV_K, cd)), _full((1, LANES)), _full((1, LANES)),
                  _full((1, ds)), _full((1, ds)), _full((LANES, ds)), _full((ds, LANES))],
        out_specs=[blk(cd), blk(ds), blk(LANES), _full((8, cd)), _full((8, cd)), _full((8, ds)), _full((8, LANES))],
        out_shape=[jax.ShapeDtypeStruct((p, cd), BF16), jax.ShapeDtypeStruct((p, ds), BF16),
                   jax.ShapeDtypeStruct((p, LANES), F32), jax.ShapeDtypeStruct((8, cd), F32),
                   jax.ShapeDtypeStruct((8, cd), F32), jax.ShapeDtypeStruct((8, ds), F32),
                   jax.ShapeDtypeStruct((8, LANES), F32)],
        scratch_shapes=[pltpu.VMEM((ns, ds), F32), pltpu.VMEM((CHUNK + 8, cd), F32), pltpu.VMEM((1, LANES), F32),
                        pltpu.VMEM((1, ds), F32), pltpu.VMEM((CHUNK, cd), F32),
                        pltpu.VMEM((CHUNK, cd), F32), pltpu.VMEM((CHUNK, cd), F32),
                        pltpu.VMEM((CHUNK, ds), F32), pltpu.VMEM((CHUNK, ds), F32), pltpu.VMEM((CHUNK, ds), F32),
                        pltpu.VMEM((CHUNK, ds), F32), pltpu.VMEM((CHUNK, ds), BF16), pltpu.VMEM((CHUNK, ds), BF16),
                        pltpu.VMEM((CHUNK, ds), BF16), pltpu.VMEM((CHUNK, ds), BF16)],
        compiler_params=_cparams(("arbitrary",)),
    )(dyssd, y, z, xbc, pre, dtf, hin, dcf, conv_w, brow, alog, dskip_l, ssd_norm, sel_t, sel)


def _attn_fwd(q, k, v, ck, blk):
    p, da = q.shape
    npair, nkb = ck.shape[0], ck.shape[1]
    scale = 1.0 / math.sqrt(HEAD_DIM)

    def body(q_ref, k_ref, v_ref, ck_ref, o_ref, lse_ref):
        i = pl.program_id(1)
        lane = lax.broadcasted_iota(jnp.int32, (1, LANES), 1)
        sels = [lane < HEAD_DIM, lane >= HEAD_DIM]
        ones = [jnp.where(lane == HEAD_DIM, 1.0, 0.0).astype(BF16), jnp.where(lane == 0, 1.0, 0.0).astype(BF16)]
        qb = q_ref[...] * scale
        cmask = (lax.broadcasted_iota(jnp.int32, (blk, blk), 1) <= lax.broadcasted_iota(jnp.int32, (blk, blk), 0))

        def step(kb, carry, masked, nk=1):
            r0 = pl.multiple_of(kb * blk, blk)
            ks = k_ref[pl.ds(r0, nk * blk), :]
            vs = v_ref[pl.ds(r0, nk * blk), :]
            kk = jnp.concatenate([jnp.where(sel, ks, jnp.zeros_like(ks)) for sel in sels], axis=0)
            s_both = _dot(qb, kk, NT)
            out = []
            for j in range(2):
                m, acc = carry[2 * j], carry[2 * j + 1]
                ckr = jnp.concatenate([ck_ref[0, kb + t, j:j + 1, :] for t in range(nk)], axis=1)
                s = s_both[:, j * nk * blk:(j + 1) * nk * blk] - ckr
                if masked:
                    s = jnp.where(cmask, s, NEG)
                mn = jnp.maximum(m, jnp.max(s, axis=-1, keepdims=True))
                pr = jnp.exp(s - mn).astype(BF16)
                acc = jnp.exp(m - mn) * acc + _dot(pr, jnp.where(sels[j], vs, ones[j]))
                out += [mn, acc]
            return tuple(out)

        init = (jnp.full((blk, 1), NEG, F32), jnp.zeros((blk, LANES), F32)) * 2
        n4 = i // 4
        n2 = (i - 4 * n4) // 2
        carry = lax.fori_loop(0, n4, lambda t, c: step(4 * t, c, False, 4), init)
        carry = lax.fori_loop(0, n2, lambda t, c: step(4 * n4 + 2 * t, c, False, 2), carry)
        carry = lax.fori_loop(4 * n4 + 2 * n2, i, lambda kb, c: step(kb, c, False), carry)
        m0, a0, m1, a1 = step(i, carry, True)
        l0 = a0[:, HEAD_DIM:HEAD_DIM + 1]
        l1 = a1[:, 0:1]
        o_ref[...] = jnp.where(sels[0], a0 / l0, a1 / l1).astype(BF16)
        lse_ref[...] = jnp.where(sels[0], m0 + jnp.log(l0), m1 + jnp.log(l1))

    return pl.pallas_call(
        body, name="attn_fwd", grid=(npair, p // blk),
        in_specs=[pl.BlockSpec((blk, LANES), lambda h, i: (i, h)),
                  pl.BlockSpec((p, LANES), lambda h, i: (0, h)), pl.BlockSpec((p, LANES), lambda h, i: (0, h)),
                  pl.BlockSpec((1, nkb, 8, blk), lambda h, i: (h, 0, 0, 0))],
        out_specs=[pl.BlockSpec((blk, LANES), lambda h, i: (i, h)), pl.BlockSpec((blk, LANES), lambda h, i: (i, h))],
        out_shape=[jax.ShapeDtypeStruct((p, da), BF16), jax.ShapeDtypeStruct((p, da), F32)],
        compiler_params=_cparams(("parallel", "arbitrary")),
    )(q, k, v, ck)


def _attn_bwd(q, k, v, o, do, lse_rep, ck, blk):
    p, da = q.shape
    npair, nkb = ck.shape[0], ck.shape[1]
    nq = p // blk
    scale = 1.0 / math.sqrt(HEAD_DIM)

    def body(k_ref, v_ref, q_ref, do_ref, o_ref, lse_ref, ck_ref, dk_ref, dv_ref, dq_ref, dcs_ref, rsum_ref, dq_acc):
        jb = pl.program_id(1)

        @pl.when(jb == 0)
        def _():
            dq_acc[...] = jnp.zeros_like(dq_acc)

        ks = k_ref[...]
        vs = v_ref[...]
        lane = lax.broadcasted_iota(jnp.int32, (1, LANES), 1)
        sels = [lane < HEAD_DIM, lane >= HEAD_DIM]
        ones = [jnp.where(lane == HEAD_DIM, 1.0, 0.0).astype(BF16), jnp.where(lane == 0, 1.0, 0.0).astype(BF16)]
        kss = ks * scale
        kmo = [jnp.where(sels[j], kss, ones[j]) for j in range(2)]
        cmask = (lax.broadcasted_iota(jnp.int32, (blk, blk), 1) <= lax.broadcasted_iota(jnp.int32, (blk, blk), 0))

        def step(ib, carry, masked, nb=1):
            rows = nb * blk
            r0 = pl.multiple_of(ib * blk, blk)
            qb = q_ref[pl.ds(r0, rows), :] * scale
            dob = do_ref[pl.ds(r0, rows), :]
            prod = dob.astype(F32) * o_ref[pl.ds(r0, rows), :].astype(F32)
            out = []
            for j in range(2):
                dk, dv = carry[2 * j], carry[2 * j + 1]
                qm = jnp.where(sels[j], qb, jnp.zeros_like(qb))
                dom = jnp.where(sels[j], dob, jnp.zeros_like(dob))
                lse = lse_ref[pl.ds(r0, rows), HEAD_DIM * j:HEAD_DIM * j + 1]
                dlt = jnp.sum(jnp.where(sels[j], prod, 0.0), axis=-1, keepdims=True)
                s = _dot(qm, ks, NT) - ck_ref[0, 0, j:j + 1, :] - lse
                pm = jnp.exp(jnp.minimum(s, 0.0))
                if masked:
                    pm = jnp.where(cmask, pm, 0.0)
                ds_b = (pm * (_dot(dom, vs, NT) - dlt)).astype(BF16)
                dv = dv + _dot(pm.astype(BF16), dom, TN)
                dk = dk + _dot(ds_b, jnp.where(sels[j], qb, ones[j]), TN)
                dq_acc[pl.ds(r0, rows), LANES * j:LANES * (j + 1)] += _dot(ds_b, kmo[j])
                out += [dk, dv]
            return tuple(out)

        zero = jnp.zeros((blk, LANES), F32)
        carry = step(jb, (zero, zero, zero, zero), True)
        n4 = (nq - 1 - jb) // 4
        n2 = (nq - 1 - jb - 4 * n4) // 2
        carry = lax.fori_loop(0, n4, lambda t, c: step(jb + 1 + 4 * t, c, False, 4), carry)
        carry = lax.fori_loop(0, n2, lambda t, c: step(jb + 1 + 4 * n4 + 2 * t, c, False, 2), carry)
        dk0, dv0, dk1, dv1 = lax.fori_loop(jb + 1 + 4 * n4 + 2 * n2, nq, lambda ib, c: step(ib, c, False), carry)
        dk_ref[...] = jnp.where(sels[0], dk0, dk1).astype(BF16)
        dv_ref[...] = (dv0 + dv1).astype(BF16)
        pair8 = lambda c0, c1: jnp.where(lane == 0, c0, jnp.where(lane == 1, c1, 0.0)).T[0:8]
        dcs_ref[0] = pair8(dk0[:, HEAD_DIM:HEAD_DIM + 1], dk1[:, 0:1])

        @pl.when(jb == nkb - 1)
        def _():
            a0 = dq_acc[:, :LANES]
            a1 = dq_acc[:, LANES:]
            dq_ref[...] = jnp.where(sels[0], a0, a1).astype(BF16)
            rsum_ref[0] = pair8(a0[:, HEAD_DIM:HEAD_DIM + 1], a1[:, 0:1])

    colblk = pl.BlockSpec((blk, LANES), lambda h, j: (j, h))
    colfull = pl.BlockSpec((p, LANES), lambda h, j: (0, h))
    ckspec = pl.BlockSpec((1, 1, 8, blk), lambda h, j: (h, j, 0, 0))
    return pl.pallas_call(
        body, name="attn_bwd", grid=(npair, nkb),
        in_specs=[colblk, colblk, colfull, colfull, colfull, colfull, ckspec],
        out_specs=[colblk, colblk, colfull, pl.BlockSpec((1, 8, blk), lambda h, j: (h, 0, j)),
                   pl.BlockSpec((1, 8, p), lambda h, j: (h, 0, 0))],
        out_shape=[jax.ShapeDtypeStruct((p, da), BF16), jax.ShapeDtypeStruct((p, da), BF16),
                   jax.ShapeDtypeStruct((p, da), BF16), jax.ShapeDtypeStruct((npair, 8, p), F32),
                   jax.ShapeDtypeStruct((npair, 8, p), F32)],
        scratch_shapes=[pltpu.VMEM((p, 2 * LANES), F32)],
        compiler_params=_cparams(("parallel", "arbitrary")),
    )(k, v, q, do, o, lse_rep, ck)


def _tail_fwd(yssd, o, zatt, graw, head, x2, tgt2, wps, wpa, wout, gate_bias, norm_post, tm):
    p, ds = yssd.shape
    da = o.shape[1]
    d = x2.shape[1]
    nsub = tm // CHUNK

    def body(yssd_ref, o_ref, zatt_ref, g_ref, head_ref, *rest):
        x_refs, t_refs = rest[:nsub], rest[nsub:2 * nsub]
        (wps_ref, wpa_ref, wout_ref, gb_ref, np_ref,
         yatt_ref, mrg_ref, a_ref, b_ref, dzo_ref, dout_ref, red_ref) = rest[2 * nsub:]
        i = pl.program_id(0)

        @pl.when(i == 0)
        def _():
            red_ref[...] = jnp.zeros_like(red_ref)

        first = jnp.where(i == 0, head_ref[...], x_refs[0][...])
        h = jnp.concatenate([first] + [r[...] for r in x_refs[1:]], axis=0)
        tgt = jnp.concatenate([r[...] for r in t_refs], axis=0)
        rows = lax.broadcasted_iota(jnp.int32, (tm, 1), 0)
        valid = jnp.where((i > 0) | (rows >= CHUNK), 1.0, 0.0)
        ob = o_ref[...].astype(F32)
        za = zatt_ref[...].astype(F32)
        yatt_b = (ob * za * _sigmoid(za)).astype(BF16)
        yatt_ref[...] = yatt_b
        a = _dot(yssd_ref[...], wps_ref[...])
        b = _dot(yatt_b, wpa_ref[...])
        a_ref[...] = a.astype(BF16)
        b_ref[...] = b.astype(BF16)
        gr = g_ref[...].astype(F32) + gb_ref[...]
        mrg_b = (_sigmoid(gr[:, :d]) * a + _sigmoid(gr[:, d:]) * b).astype(BF16)
        mrg_ref[...] = mrg_b
        zo = _dot(mrg_b, wout_ref[...])
        rstd = lax.rsqrt(jnp.mean(zo * zo, axis=-1, keepdims=True) + EPS)
        zh = zo * rstd
        npw = np_ref[...]
        err = (h + zh * npw - tgt) * valid
        dout = err * (1.0 / d)
        dout_ref[...] = dout
        dzh = dout * npw
        dzo_ref[...] = (rstd * (dzh - zh * jnp.mean(dzh * zh, axis=-1, keepdims=True))).astype(BF16)
        red_ref[0:1, :] += jnp.sum(dout * zh, axis=0, keepdims=True)
        red_ref[1:2, 0:1] += jnp.sum(jnp.sum(err * err, axis=1, keepdims=True), axis=0, keepdims=True) * (0.5 / d)

    row = lambda w: pl.BlockSpec((tm, w), lambda i: (i, 0))
    once = lambda shape: pl.BlockSpec(shape, lambda i: (0,) * len(shape), pipeline_mode=pl.Buffered(1))
    subs = _x_row_specs(tm, d)
    sd = jax.ShapeDtypeStruct
    return pl.pallas_call(
        body, name="tail_fwd", grid=(p // tm,),
        in_specs=[row(ds), row(da), row(da), row(2 * d), _full((CHUNK, d))] + subs + subs
                 + [once((ds, d)), once((da, d)), once((d, d)), _full((1, 2 * d)), _full((1, d))],
        out_specs=[row(da), row(d), row(d), row(d), row(d), row(d), _full((8, d))],
        out_shape=[sd((p, da), BF16), sd((p, d), BF16), sd((p, d), BF16), sd((p, d), BF16), sd((p, d), BF16),
                   sd((p, d), F32), sd((8, d), F32)],
        compiler_params=_cparams(("arbitrary",)),
    )(yssd, o, zatt, graw, head, *([x2] * nsub), *([tgt2] * nsub), wps, wpa, wout, gate_bias, norm_post)


def _tail_bwd(dzo, a_b, b_b, graw, o, zatt, wps, wpa, wout, gate_bias, tm):
    p, d = dzo.shape
    ds, da = wps.shape[0], wpa.shape[0]

    def body(dzo_ref, a_ref, b_ref, g_ref, o_ref, zatt_ref, wps_ref, wpa_ref, wout_ref, gb_ref,
             da_ref, db_ref, dg_ref, dyssd_ref, do_ref, dzatt_ref, red_ref):
        i = pl.program_id(0)

        @pl.when(i == 0)
        def _():
            red_ref[...] = jnp.zeros_like(red_ref)

        gr = g_ref[...].astype(F32) + gb_ref[...]
        gs = _sigmoid(gr[:, :d])
        ga = _sigmoid(gr[:, d:])
        dm = _dot(dzo_ref[...], wout_ref[...], NT)
        da_b = (gs * dm).astype(BF16)
        db_b = (ga * dm).astype(BF16)
        da_ref[...] = da_b
        db_ref[...] = db_b
        dgs = dm * a_ref[...].astype(F32) * gs * (1.0 - gs)
        dga = dm * b_ref[...].astype(F32) * ga * (1.0 - ga)
        dg_ref[:, :d] = dgs.astype(BF16)
        dg_ref[:, d:] = dga.astype(BF16)
        red_ref[0:1, :d] += jnp.sum(dgs, axis=0, keepdims=True)
        red_ref[0:1, d:] += jnp.sum(dga, axis=0, keepdims=True)
        dyssd_ref[...] = _dot(da_b, wps_ref[...], NT).astype(BF16)
        dya = _dot(db_b, wpa_ref[...], NT)
        ob = o_ref[...].astype(F32)
        za = zatt_ref[...].astype(F32)
        sza = _sigmoid(za)
        do_ref[...] = (dya * za * sza).astype(BF16)
        dzatt_ref[...] = (dya * ob * sza * (1.0 + za * (1.0 - sza))).astype(BF16)

    row = lambda w: pl.BlockSpec((tm, w), lambda i: (i, 0))
    once = lambda shape: pl.BlockSpec(shape, lambda i: (0,) * len(shape), pipeline_mode=pl.Buffered(1))
    sd = jax.ShapeDtypeStruct
    return pl.pallas_call(
        body, name="tail_bwd", grid=(p // tm,),
        in_specs=[row(d), row(d), row(d), row(2 * d), row(da), row(da),
                  once((ds, d)), once((da, d)), once((d, d)), _full((1, 2 * d))],
        out_specs=[row(d), row(d), row(2 * d), row(ds), row(da), row(da), _full((8, 2 * d))],
        out_shape=[sd((p, d), BF16), sd((p, d), BF16), sd((p, 2 * d), BF16), sd((p, ds), BF16), sd((p, da), BF16),
                   sd((p, da), BF16), sd((8, 2 * d), F32)],
        compiler_params=_cparams(("arbitrary",)),
    )(dzo, a_b, b_b, graw, o, zatt, wps, wpa, wout, gate_bias)


def _adamw_math(w, g, m, v):
    m2 = ADAM_B1 * m + (1.0 - ADAM_B1) * g
    v2 = ADAM_B2 * v + (1.0 - ADAM_B2) * (g * g)
    m_hat = m2 / (1.0 - ADAM_B1 ** ADAM_STEP)
    v_hat = v2 / (1.0 - ADAM_B2 ** ADAM_STEP)
    delta = -ADAM_LR * (m_hat / (jnp.sqrt(v_hat) + ADAM_EPS) + ADAM_WD * w)
    return delta, m2, v2


def _adamw_small(params, red, name):
    names = list(params)
    n = len(names)
    extra = [params[k][3] for k in names if not isinstance(params[k][3], tuple)]

    def body(*refs):
        w_refs, m_refs, v_refs = refs[:n], refs[n:2 * n], refs[2 * n:3 * n]
        red_ref = refs[3 * n]
        g_refs = iter(refs[3 * n + 1:3 * n + 1 + len(extra)])
        outs = refs[3 * n + 1 + len(extra):]
        for i, k in enumerate(names):
            where = params[k][3]
            rows, cols = w_refs[i].shape
            if isinstance(where, tuple):
                g = red_ref[where[0]:where[0] + rows, where[1]:where[1] + cols]
            else:
                g = next(g_refs)[...]
            delta, m2, v2 = _adamw_math(w_refs[i][...], g, m_refs[i][...], v_refs[i][...])
            for o, val in zip(outs[4 * i:4 * i + 4], (g, delta, m2, v2)):
                o[...] = val

    vm = pl.BlockSpec(memory_space=pltpu.VMEM)
    ws, ms, vs = ([params[k][j] for k in names] for j in range(3))
    out = pl.pallas_call(
        body, name=name,
        out_shape=[jax.ShapeDtypeStruct(w.shape, F32) for w in ws for _ in range(4)],
        in_specs=[vm] * (3 * n + 1 + len(extra)), out_specs=[vm] * (4 * n),
    )(*ws, *ms, *vs, red, *extra)
    return {k: tuple(out[4 * i:4 * i + 4]) for i, k in enumerate(names)}


def _adamw(w, g, m, v, name, parts=False, part_row0=0):
    r, cdim = w.shape
    tr, tc, by_rows = _tiles_2d(r, cdim)
    pick = (lambda i: (i, 0)) if by_rows else (lambda i: (0, i))
    assert part_row0 % tr == 0
    gpick = (lambda i: (i + part_row0 // tr, 0)) if by_rows else (lambda i: (part_row0 // tr, i))

    def body(w_ref, g_ref, m_ref, v_ref, go_ref, d_ref, mo_ref, vo_ref):
        if parts:
            g = g_ref[0].astype(F32)
            for s in range(1, g_ref.shape[0]):
                g = g + g_ref[s].astype(F32)
        else:
            g = g_ref[...]
        delta, m2, v2 = _adamw_math(w_ref[...], g, m_ref[...], v_ref[...])
        go_ref[...] = g
        d_ref[...] = delta
        mo_ref[...] = m2
        vo_ref[...] = v2

    blk = pl.BlockSpec((tr, tc), pick)
    gspec = pl.BlockSpec((g.shape[0], tr, tc), lambda i: (0,) + gpick(i)) if parts else blk
    return pl.pallas_call(
        body, name=name, grid=((r // tr) * (cdim // tc),),
        in_specs=[blk, gspec, blk, blk], out_specs=[blk] * 4,
        out_shape=[jax.ShapeDtypeStruct((r, cdim), F32)] * 4,
        compiler_params=_cparams(("parallel",)),
    )(w, g, m, v)


def _pad_cols(a, width):
    return jnp.pad(a, ((0, 0), (0, width - a.shape[1])))


def _pack_small_shard(conv_w_sh, meta_sh, width):
    return jnp.concatenate([_pad_cols(conv_w_sh, width), jnp.zeros((4, width), F32), _pad_cols(meta_sh, width)], axis=0)


def _pack_small_rep(norm_pre, norm_post, gate_bias, ssd_norm, conv_b, misc, width):
    rows = [norm_pre, norm_post, gate_bias, ssd_norm, conv_b, misc]
    return jnp.concatenate([_pad_cols(r, width) for r in rows] + [jnp.zeros((2, width), F32)], axis=0)


def kernel(x, meta_tokens, norm_pre, w_in, conv_w, conv_b, dt_bias, a_log, d_skip, ssd_norm, fgate_bias, gate_bias, w_proj_ssd, w_proj_att, w_out, norm_post, loss_target, m_meta_tokens, m_norm_pre, m_w_in, m_conv_w, m_conv_b, m_dt_bias, m_a_log, m_d_skip, m_ssd_norm, m_fgate_bias, m_gate_bias, m_w_proj_ssd, m_w_proj_att, m_w_out, m_norm_post, v_meta_tokens, v_norm_pre, v_w_in, v_conv_w, v_conv_b, v_dt_bias, v_a_log, v_d_skip, v_ssd_norm, v_fgate_bias, v_gate_bias, v_w_proj_ssd, v_w_proj_att, v_w_out, v_norm_post):
    seq, d = x.shape[1], x.shape[2]
    p = seq + CHUNK
    hs, ha = dt_bias.shape[1], fgate_bias.shape[1]
    ds, cd = ssd_norm.shape[1], conv_b.shape[1]
    da = ha * HEAD_DIM
    nc8 = w_in.shape[2]
    cws = cd // N_DEV
    msh = d // N_DEV
    r1, r2, r3 = ds // N_DEV, da // N_DEV, d // N_DEV
    me = _dev_index(*_my_pos())
    x2, tgt2 = x[0], loss_target[0]

    win_sh = jnp.transpose(w_in[0]).astype(BF16)
    rows_sh = jnp.concatenate([w_proj_ssd[0], w_proj_att[0], w_out[0]], axis=0).astype(BF16)
    small_sh = _pack_small_shard(conv_w[0], meta_tokens, cws)
    win_all, small_all = _all_gather([win_sh, small_sh], "gather_weights")
    rows_sh, win_all = lax.optimization_barrier((rows_sh, win_all))
    rows_sems, rows_thru, rows_land, rows_token = _bcast_start(rows_sh, "gather_rows_start")
    cuts = [0, ds, ds + cd, ds + cd + hs, ds + cd + hs + da, ds + cd + hs + 2 * da, ds + cd + hs + 3 * da,
            ds + cd + hs + 4 * da, ds + cd + hs + 4 * da + ha, ds + cd + hs + 4 * da + ha + 2 * d]

    def piece_rows(r0, r1):
        parts = [win_all[s, max(r0, s * nc8) - s * nc8:min(r1, (s + 1) * nc8) - s * nc8]
                 for s in range(N_DEV) if max(r0, s * nc8) < min(r1, (s + 1) * nc8)]
        return parts[0] if len(parts) == 1 else jnp.concatenate(parts, axis=0)

    w_z, w_xbc, w_dt, w_zatt, w_q, w_k, w_v, w_f, w_g = [piece_rows(cuts[i], cuts[i + 1]) for i in range(9)]
    w_dtf = jnp.concatenate([w_dt, w_f, jnp.zeros((LANES - hs - ha, d), BF16)], axis=0)
    conv_w_full = jnp.transpose(small_all[:, 0:CONV_K, :], (1, 0, 2)).reshape(CONV_K, cd)
    meta_full = jnp.transpose(small_all[:, 8:8 + N_META, :msh], (1, 0, 2)).reshape(N_META, d)
    head = jnp.concatenate([jnp.zeros((PADN, d), F32), meta_full + rows_token[0:1, 0:1]], axis=0)

    tm = _att_block(p)
    u = _prenorm_fwd(head, x2, norm_pre, tm)
    seg_w = [w_z, w_xbc, w_zatt, w_q, w_k, w_v, w_g]
    zs, xbc, zatt, q, k, v, graw = [
        _mm(u, w, "nt", BF16, _tile(p, (1408, tm)), _tile(w.shape[0], (1024, 512, 256, 128)), "inproj_%d" % i)
        for i, w in enumerate(seg_w)]
    dtf = _mm(u, w_dtf, "nt", F32, _tile(p, (1408, tm)), LANES, "inproj_dtf")

    brow = jnp.concatenate([dt_bias, fgate_bias, jnp.zeros((1, LANES - hs - ha), F32)], axis=1)
    alog_row = _pad_cols(a_log, LANES)
    dskip_l = jnp.repeat(d_skip, HEAD_DIM, axis=1)
    sel_t = (lax.broadcasted_iota(jnp.int32, (LANES, ds), 1) // HEAD_DIM
             == lax.broadcasted_iota(jnp.int32, (LANES, ds), 0)).astype(BF16)
    sel = sel_t.T
    y, yssd, hin, cf, pre = _ssd_fwd(xbc, zs, dtf, conv_w_full, conv_b, brow, alog_row, dskip_l, ssd_norm, sel_t, hs, ha)

    blk = _att_block(p)
    nkb, npair = p // blk, ha // 2
    cum = jnp.where(lax.broadcasted_iota(jnp.int32, (p, 1), 0) < PADN, -NEG, cf[:, hs:hs + ha])
    ck = jnp.transpose(cum.T.reshape(npair, 2, nkb, blk), (0, 2, 1, 3))
    ck = jnp.pad(ck, ((0, 0), (0, 0), (0, 6), (0, 0)))
    o, lse_rep = _attn_fwd(q, k, v, ck, blk)

    rows_all = _bcast_wait(rows_sems, rows_thru, rows_land, lse_rep, "gather_rows_wait")
    wps = rows_all[:, :r1].reshape(ds, d)
    wpa = rows_all[:, r1:r1 + r2].reshape(da, d)
    wout = rows_all[:, r1 + r2:].reshape(d, d)

    yatt, mrg, a_b, b_b, dzo, dout, red_fwd = _tail_fwd(
        yssd, o, zatt, graw, head, x2, tgt2, wps, wpa, wout, gate_bias, norm_post, tm)
    da_, db_, dgraw, dyssd, d_o, dzatt, red_bwd = _tail_bwd(dzo, a_b, b_b, graw, o, zatt, wps, wpa, wout, gate_bias, tm)

    tw = _tile(d, (512, 256, 128))
    g_wout = _mm(mrg, dzo, "tn", BF16, tw, d, "wgrad_out")
    g_wps = _mm(yssd, da_, "tn", BF16, _tile(ds, (512, 256, 128)), d, "wgrad_ps")
    g_wpa = _mm(yatt, db_, "tn", BF16, _tile(da, (512, 256, 128)), d, "wgrad_pa")

    dk, dv, dq, dcs, rsum = _attn_bwd(q, k, v, o, d_o, lse_rep, ck, blk)
    dcum = (rsum - dcs)[:, 0:2, :].reshape(ha, p).T
    dcf = jnp.pad(dcum, ((0, 0), (hs, LANES - hs - ha)))
    dxbc, dzs, ddtf, gcw, gcb, gnrm, gsm = _ssd_bwd(
        dyssd, y, zs, xbc, pre, dtf, hin, dcf, conv_w_full, brow, alog_row, dskip_l, ssd_norm, sel_t, sel, hs, ha)
    ddtf_b = ddtf.astype(BF16)

    dsegs = [dzs, dxbc, dzatt, dq, dk, dv, dgraw, ddtf_b]
    gsegs = [_mm(dsg, u, "tn", BF16, _tile(dsg.shape[1], (512, 256, 128)), d, "wgrad_in_%d" % i)
             for i, dsg in enumerate(dsegs)]
    g_z, g_xbc, g_zatt, g_q, g_k, g_v, g_g, g_dtf = gsegs
    gw_full = jnp.concatenate([g_z, g_xbc, g_dtf[:hs], g_zatt, g_q, g_k, g_v, g_dtf[hs:hs + ha], g_g], axis=0)
    gwin_parts = gw_full.reshape(N_DEV, nc8, d)
    grows_parts = jnp.concatenate([g_wps.reshape(N_DEV, r1, d), g_wpa.reshape(N_DEV, r2, d),
                                   g_wout.reshape(N_DEV, r3, d)], axis=1)

    core = lax.axis_index("c").astype(jnp.int32).reshape(1)
    sib_win, sib_rows = _exchange_sibling([gwin_parts, grows_parts], "scatter_grads_sibling")
    chip_win = _pair_add(gwin_parts, sib_win, core, "pair_add_w_in")
    chip_rows = _pair_add(grows_parts, sib_rows, core, "pair_add_rows")
    sems, thru, lands, token = _exchange_chips_start([chip_win, chip_rows], "scatter_grads_start")
    dsegs_after = dsegs[:-1] + [ddtf_b + token[0:1, 0:1].astype(BF16)]
    du = _mm_sum_nn(dsegs_after, seg_w + [w_dtf], tm, _tile(d, (512, 256, 128)), "dgrad_in")
    gx, ghead, gnp = _prenorm_bwd(head, x2, norm_pre, du, dout)
    sent, got = _exchange_chips_wait(sems, thru, lands, gnp, "scatter_grads_wait")
    chip = me // 2
    recv_win, recv_rows = [lax.dynamic_update_slice_in_dim(g, lax.dynamic_slice_in_dim(s, chip, 1, axis=0), chip, axis=0)
                           for g, s in zip(got, sent)]
    gmisc = jnp.concatenate([gsm[0:1], gsm[1:2], gsm[2:3], _pad_cols(red_fwd[1:2, 0:1], LANES)], axis=1)
    small_g = jnp.concatenate([
        _pack_small_rep(gnp[0:1], red_fwd[0:1], red_bwd[0:1], gnrm[0:1], gcb[0:1], gmisc, cd),
        _pad_cols(gcw[0:CONV_K], cd), jnp.zeros((4, cd), F32), _pad_cols(ghead[PADN:], cd)], axis=0)
    sg_sems, sg_thru, sg_land, sg_token = _bcast_start(small_g, "reduce_small_start")

    upd_in = _adamw(jnp.transpose(w_in[0]) + sg_token[0:1, 0:1], recv_win, jnp.transpose(m_w_in[0]),
                    jnp.transpose(v_w_in[0]), "adamw_w_in", parts=True)
    upd_ps = _adamw(w_proj_ssd[0] + sg_token[0:1, 0:1], recv_rows, m_w_proj_ssd[0], v_w_proj_ssd[0],
                    "adamw_w_proj_ssd", parts=True, part_row0=0)
    upd_pa = _adamw(w_proj_att[0], recv_rows, m_w_proj_att[0], v_w_proj_att[0], "adamw_w_proj_att", parts=True,
                    part_row0=r1)
    upd_out = _adamw(w_out[0], recv_rows, m_w_out[0], v_w_out[0], "adamw_w_out", parts=True, part_row0=r1 + r2)
    all_done = upd_in[1][0:8, 0:LANES] + upd_ps[1][0:8, 0:LANES] + upd_pa[1][0:8, 0:LANES] + upd_out[1][0:8, 0:LANES]
    red = _sum_slots(_bcast_wait(sg_sems, sg_thru, sg_land, all_done, "reduce_small_wait"), "reduce_small_sum")
    loss = red[5, 3 * LANES]
    g_conv_w = lax.dynamic_slice_in_dim(red[8:8 + CONV_K], me * cws, cws, axis=1)
    g_meta = lax.dynamic_slice_in_dim(red[16:16 + N_META, :d], me * msh, msh, axis=1)
    small = {
        "meta_tokens": (meta_tokens, m_meta_tokens, v_meta_tokens, g_meta),
        "norm_pre": (norm_pre, m_norm_pre, v_norm_pre, (0, 0)),
        "conv_w": (conv_w[0], m_conv_w[0], v_conv_w[0], g_conv_w),
        "conv_b": (conv_b, m_conv_b, v_conv_b, (4, 0)),
        "dt_bias": (dt_bias, m_dt_bias, v_dt_bias, (5, 0)),
        "a_log": (a_log, m_a_log, v_a_log, (5, LANES)),
        "d_skip": (d_skip, m_d_skip, v_d_skip, (5, 2 * LANES)),
        "ssd_norm": (ssd_norm, m_ssd_norm, v_ssd_norm, (3, 0)),
        "fgate_bias": (fgate_bias, m_fgate_bias, v_fgate_bias, (5, hs)),
        "gate_bias": (gate_bias, m_gate_bias, v_gate_bias, (2, 0)),
        "norm_post": (norm_post, m_norm_post, v_norm_post, (1, 0)),
    }
    upd_small = _adamw_small(small, red, "adamw_small")

    def leaves(i):
        sm = {k: v[i] for k, v in upd_small.items()}
        return [sm["meta_tokens"], sm["norm_pre"], jnp.transpose(upd_in[i])[None], sm["conv_w"][None], sm["conv_b"],
                sm["dt_bias"], sm["a_log"], sm["d_skip"], sm["ssd_norm"], sm["fgate_bias"], sm["gate_bias"],
                upd_ps[i][None], upd_pa[i][None], upd_out[i][None], sm["norm_post"]]

    return tuple([loss, gx[None]] + leaves(0) + leaves(1) + leaves(2) + leaves(3))
```

```python
import functools
import math

import jax
import jax.numpy as jnp
from jax import lax
from jax.experimental import pallas as pl
from jax.experimental.pallas import tpu as pltpu

F32 = jnp.float32
BF16 = jnp.bfloat16

N_DEV = 8
N_META = 16
CHUNK = 128
PADN = CHUNK - N_META
HEAD_DIM = 64
SSD_GROUPS = 4
CONV_K = 4
EPS = 1e-6
NEG = -1e30
LANES = 128
HALO = 16

ADAM_LR = 0.001
ADAM_B1 = 0.9
ADAM_B2 = 0.999
ADAM_EPS = 1e-08
ADAM_WD = 0.01
ADAM_STEP = 10

VMEM_LIMIT = 56 * 1024 * 1024

NN = (((1,), (0,)), ((), ()))
NT = (((1,), (1,)), ((), ()))
TN = (((0,), (0,)), ((), ()))
MESH = pl.DeviceIdType.MESH


def _dot(a, b, dims=NN):
    return lax.dot_general(a, b, dims, preferred_element_type=F32)


def _split2(x):
    hi = x.astype(BF16)
    lo = (x - hi.astype(F32)).astype(BF16)
    return hi, lo


def _dot_sel(x, sel):
    hi, lo = _split2(x)
    return _dot(hi, sel) + _dot(lo, sel)


def _dot_tri(tri, x):
    h1 = x.astype(BF16)
    r1 = x - h1.astype(F32)
    h2 = r1.astype(BF16)
    h3 = (r1 - h2.astype(F32)).astype(BF16)
    return _dot(tri, h1) + _dot(tri, h2) + _dot(tri, h3)


def _sigmoid(x):
    return 0.5 * jnp.tanh(0.5 * x) + 0.5


def _softplus(x):
    return jnp.maximum(x, 0.0) + jnp.log(1.0 + jnp.exp(-jnp.abs(x)))


def _cparams(sem=None, vmem=VMEM_LIMIT):
    kw = {"vmem_limit_bytes": vmem}
    if sem is not None:
        kw["dimension_semantics"] = sem
    return pltpu.CompilerParams(**kw)


def _full(shape):
    nd = len(shape)
    return pl.BlockSpec(shape, lambda *_: (0,) * nd)


def _att_block(p):
    return 384 if p % 384 == 0 else CHUNK


def _my_pos():
    return lax.axis_index("x"), lax.axis_index("y"), lax.axis_index("c")


def _dev_index(x, y, c):
    return 4 * x + 2 * y + c


FLIPS = [(fx, fy, fc) for fx in (0, 1) for fy in (0, 1) for fc in (0, 1)][1:]


def _flip(pos, f):
    return tuple((1 - p) if fi else p for p, fi in zip(pos, f))


def _all_gather(bufs, name):
    nb = len(bufs)

    def body(*refs):
        ins, outs = refs[:nb], refs[nb:2 * nb]
        send_sems, recv_sems, local_sems = refs[2 * nb:]
        x, y, c = _my_pos()
        me = _dev_index(x, y, c)
        sibling = (x, y, 1 - c)
        near = [(1 - x, y), (x, 1 - y)]
        far = (1 - x, 1 - y)
        relay_from = (c * (1 - x) + (1 - c) * x, c * y + (1 - c) * (1 - y))
        relay_to = (c * x + (1 - c) * (1 - x), c * (1 - y) + (1 - c) * y)

        def copy(b, k, block_idx, to, src=None):
            dst = outs[b].at[block_idx]
            return pltpu.make_async_remote_copy(
                src_ref=dst if src is None else src, dst_ref=dst,
                send_sem=send_sems.at[b, k], recv_sem=recv_sems.at[b, k],
                device_id=to, device_id_type=MESH)

        started = []
        for b in range(nb):
            mine = pltpu.make_async_copy(ins[b], outs[b].at[me], local_sems.at[b])
            mine.start()
            started.append(mine)
        sent = []
        for b in range(nb):
            sent.append(copy(b, 0, me, sibling, src=ins[b]))
            for j, chip in enumerate(near):
                sent.append(copy(b, 1 + j, me, (chip[0], chip[1], c), src=ins[b]))
        for cp in sent:
            cp.start()
        for j, chip in enumerate(near):
            blk = _dev_index(chip[0], chip[1], c)
            for b in range(nb):
                copy(b, 1 + j, blk, (x, y, c)).wait_recv()
                sent.append(copy(b, 4 + j, blk, sibling))
                sent[-1].start()
        for b in range(nb):
            sent.append(copy(b, 3, _dev_index(relay_from[0], relay_from[1], c), (relay_to[0], relay_to[1], c)))
            sent[-1].start()
        blk = _dev_index(far[0], far[1], c)
        for b in range(nb):
            copy(b, 3, blk, (x, y, c)).wait_recv()
            sent.append(copy(b, 6, blk, sibling))
            sent[-1].start()
        for b in range(nb):
            copy(b, 0, _dev_index(x, y, 1 - c), (x, y, c)).wait_recv()
        for j, chip in enumerate(near + [far]):
            blk = _dev_index(chip[0], chip[1], 1 - c)
            for b in range(nb):
                copy(b, 4 + j, blk, (x, y, c)).wait_recv()
        for cp in sent:
            cp.wait_send()
        for mine in started:
            mine.wait()

    any_spec = pl.BlockSpec(memory_space=pl.ANY)
    return pl.pallas_call(
        body, name=name,
        out_shape=[jax.ShapeDtypeStruct((N_DEV,) + b.shape, b.dtype) for b in bufs],
        in_specs=[any_spec] * nb, out_specs=[any_spec] * nb,
        scratch_shapes=[pltpu.SemaphoreType.DMA((nb, 7)), pltpu.SemaphoreType.DMA((nb, 7)),
                        pltpu.SemaphoreType.DMA((nb,))],
    )(*bufs)


N_CHIP = 4
CHIP_FLIPS = [(1, 0), (0, 1), (1, 1)]


def _exchange_sibling(bufs, name):
    nb = len(bufs)

    def body(*refs):
        ins, outs = refs[:nb], refs[nb:2 * nb]
        send_sems, recv_sems = refs[2 * nb:]
        x, y, c = _my_pos()

        def copy(b, k):
            return pltpu.make_async_remote_copy(
                src_ref=ins[b].at[2 * k + (1 - c)], dst_ref=outs[b].at[k],
                send_sem=send_sems.at[b, k], recv_sem=recv_sems.at[b, k],
                device_id=(x, y, 1 - c), device_id_type=MESH)

        cps = [copy(b, k) for b in range(nb) for k in range(N_CHIP)]
        for cp in cps:
            cp.start()
        for cp in cps:
            cp.wait()

    any_spec = pl.BlockSpec(memory_space=pl.ANY)
    return pl.pallas_call(
        body, name=name,
        out_shape=[jax.ShapeDtypeStruct((N_CHIP,) + b.shape[1:], b.dtype) for b in bufs],
        in_specs=[any_spec] * nb, out_specs=[any_spec] * nb,
        scratch_shapes=[pltpu.SemaphoreType.DMA((nb, N_CHIP)), pltpu.SemaphoreType.DMA((nb, N_CHIP))],
    )(*bufs)


def _pair_add(mine, recv, core, name):
    _, r, cdim = mine.shape
    tr, tc = r, cdim
    pick = lambda i: (i, 0)

    def body(core_ref, a_ref, b_ref, o_ref):
        o_ref[0] = (a_ref[0].astype(F32) + b_ref[0].astype(F32)).astype(o_ref.dtype)

    return pl.pallas_call(
        body, name=name,
        grid_spec=pltpu.PrefetchScalarGridSpec(
            num_scalar_prefetch=1, grid=(N_CHIP, (r // tr) * (cdim // tc)),
            in_specs=[pl.BlockSpec((1, tr, tc), lambda k, i, core_ref: (2 * k + core_ref[0],) + pick(i)),
                      pl.BlockSpec((1, tr, tc), lambda k, i, core_ref: (k,) + pick(i))],
            out_specs=pl.BlockSpec((1, tr, tc), lambda k, i, core_ref: (k,) + pick(i))),
        out_shape=jax.ShapeDtypeStruct((N_CHIP, r, cdim), mine.dtype),
        compiler_params=_cparams(("parallel", "parallel")),
    )(core, mine, recv)


def _chip_peer(x, y, f):
    return ((1 - x) if f[0] else x), ((1 - y) if f[1] else y)


def _exchange_chips_start(bufs, name):
    nb = len(bufs)
    nsem = 2 * 3 * nb

    def body(*refs):
        ins, lands = refs[:nb], refs[nb:2 * nb]
        sems = refs[2 * nb:2 * nb + nsem]
        token = refs[-1]
        x, y, c = _my_pos()
        for b in range(nb):
            for j, f in enumerate(CHIP_FLIPS):
                px, py = _chip_peer(x, y, f)
                pltpu.make_async_remote_copy(
                    src_ref=ins[b].at[2 * px + py], dst_ref=lands[b].at[2 * x + y],
                    send_sem=sems[2 * (3 * b + j)], recv_sem=sems[2 * (3 * b + j) + 1],
                    device_id=(px, py, c), device_id_type=MESH).start()
        token[...] = jnp.zeros_like(token)

    hbm = pl.BlockSpec(memory_space=pltpu.HBM)
    sem = pl.BlockSpec(memory_space=pltpu.SEMAPHORE)
    out = pl.pallas_call(
        body, name=name,
        out_shape=(*([pltpu.SemaphoreType.DMA(())] * nsem),
                   *[pltpu.HBM(b.shape, b.dtype) for b in bufs], *[pltpu.HBM(b.shape, b.dtype) for b in bufs],
                   jax.ShapeDtypeStruct((8, LANES), F32)),
        in_specs=[hbm] * (2 * nb),
        out_specs=(*([sem] * nsem), *([hbm] * (2 * nb)), pl.BlockSpec(memory_space=pltpu.VMEM)),
        input_output_aliases={i: nsem + i for i in range(2 * nb)},
        compiler_params=pltpu.CompilerParams(has_side_effects=pltpu.SideEffectType.DATAFLOW_SIDE_EFFECTING),
    )(*[pltpu.with_memory_space_constraint(b, pltpu.HBM) for b in bufs],
      *[pltpu.with_memory_space_constraint(lax.empty(b.shape, b.dtype), pltpu.HBM) for b in bufs])
    return out[:nsem], out[nsem:nsem + nb], out[nsem + nb:nsem + 2 * nb], out[-1]


def _exchange_chips_wait(sems, thru, lands, after, name):
    nb = len(thru)
    nsem = len(sems)

    def body(*refs):
        ins, lnd = refs[:nb], refs[nb:2 * nb]
        sem_refs = refs[2 * nb:2 * nb + nsem]
        x, y, c = _my_pos()
        for b in range(nb):
            for j, f in enumerate(CHIP_FLIPS):
                px, py = _chip_peer(x, y, f)
                cp = pltpu.make_async_remote_copy(
                    src_ref=ins[b].at[2 * px + py], dst_ref=lnd[b].at[2 * px + py],
                    send_sem=sem_refs[2 * (3 * b + j)], recv_sem=sem_refs[2 * (3 * b + j) + 1],
                    device_id=(px, py, c), device_id_type=MESH)
                cp.wait_send()
                cp.wait_recv()

    hbm = pl.BlockSpec(memory_space=pltpu.HBM)
    sem = pl.BlockSpec(memory_space=pltpu.SEMAPHORE)
    out = pl.pallas_call(
        body, name=name,
        out_shape=tuple([pltpu.HBM(b.shape, b.dtype) for b in thru] + [pltpu.HBM(b.shape, b.dtype) for b in lands]),
        in_specs=[hbm] * (2 * nb) + [sem] * nsem + [pl.BlockSpec(memory_space=pl.ANY)],
        out_specs=tuple([hbm] * (2 * nb)),
        input_output_aliases={i: i for i in range(2 * nb)},
        compiler_params=pltpu.CompilerParams(has_side_effects=pltpu.SideEffectType.DATAFLOW_SIDE_EFFECTING),
    )(*thru, *lands, *sems, after)
    return out[:nb], out[nb:]


def _bcast_start(buf, name):
    nsem = 2 * len(FLIPS)

    def body(src, land, *rest):
        sems, token = rest[:nsem], rest[-1]
        pos = _my_pos()
        for k, f in enumerate(FLIPS):
            pltpu.make_async_remote_copy(
                src_ref=src, dst_ref=land.at[_dev_index(*pos)], send_sem=sems[2 * k], recv_sem=sems[2 * k + 1],
                device_id=_flip(pos, f), device_id_type=MESH).start()
        token[...] = jnp.zeros_like(token)

    hbm = pl.BlockSpec(memory_space=pltpu.HBM)
    sem = pl.BlockSpec(memory_space=pltpu.SEMAPHORE)
    land_shape = (N_DEV,) + buf.shape
    out = pl.pallas_call(
        body, name=name,
        out_shape=(*([pltpu.SemaphoreType.DMA(())] * nsem), pltpu.HBM(buf.shape, buf.dtype),
                   pltpu.HBM(land_shape, buf.dtype), jax.ShapeDtypeStruct((8, LANES), F32)),
        in_specs=[hbm, hbm],
        out_specs=(*([sem] * nsem), hbm, hbm, pl.BlockSpec(memory_space=pltpu.VMEM)),
        input_output_aliases={0: nsem, 1: nsem + 1},
        compiler_params=pltpu.CompilerParams(has_side_effects=pltpu.SideEffectType.DATAFLOW_SIDE_EFFECTING),
    )(pltpu.with_memory_space_constraint(buf, pltpu.HBM),
      pltpu.with_memory_space_constraint(lax.empty(land_shape, buf.dtype), pltpu.HBM))
    return out[:nsem], out[nsem], out[nsem + 1], out[-1]


def _bcast_wait(sems, thru, land, after, name):
    nsem = len(sems)

    def body(src, lnd, *rest):
        sem_refs = rest[:nsem]
        pos = _my_pos()
        for k, f in enumerate(FLIPS):
            peer = _flip(pos, f)
            cp = pltpu.make_async_remote_copy(
                src_ref=src, dst_ref=lnd.at[_dev_index(*peer)], send_sem=sem_refs[2 * k],
                recv_sem=sem_refs[2 * k + 1], device_id=peer, device_id_type=MESH)
            cp.wait_send()
            cp.wait_recv()

    hbm = pl.BlockSpec(memory_space=pltpu.HBM)
    sem = pl.BlockSpec(memory_space=pltpu.SEMAPHORE)
    sent, got = pl.pallas_call(
        body, name=name,
        out_shape=(pltpu.HBM(thru.shape, thru.dtype), pltpu.HBM(land.shape, land.dtype)),
        in_specs=[hbm, hbm] + [sem] * nsem + [pl.BlockSpec(memory_space=pl.ANY)],
        out_specs=(hbm, hbm), input_output_aliases={0: 0, 1: 1},
        compiler_params=pltpu.CompilerParams(has_side_effects=pltpu.SideEffectType.DATAFLOW_SIDE_EFFECTING),
    )(thru, land, *sems, after)
    return lax.dynamic_update_slice_in_dim(got, sent[None], _dev_index(*_my_pos()), axis=0)


def _sum_slots(v, name):
    _, r, cdim = v.shape

    def body(v_ref, o_ref):
        acc = v_ref[0]
        for s in range(1, N_DEV):
            acc = acc + v_ref[s]
        o_ref[...] = acc

    return pl.pallas_call(
        body, name=name, out_shape=jax.ShapeDtypeStruct((r, cdim), F32),
        in_specs=[_full((N_DEV, r, cdim))], out_specs=_full((r, cdim)), grid=(1,),
        compiler_params=_cparams(("arbitrary",)),
    )(v)


def _mm(a, b, dims, out_dtype, tm, tn, name):
    if dims == "nn":
        (m, k), (_, n) = a.shape, b.shape
        a_spec = pl.BlockSpec((tm, k), lambda j, i: (i, 0))
        b_spec = pl.BlockSpec((k, tn), lambda j, i: (0, j))
        dn = NN
    elif dims == "nt":
        (m, k), (n, _) = a.shape, b.shape
        a_spec = pl.BlockSpec((tm, k), lambda j, i: (i, 0))
        b_spec = pl.BlockSpec((tn, k), lambda j, i: (j, 0))
        dn = NT
    else:
        (k, m), (_, n) = a.shape, b.shape
        a_spec = pl.BlockSpec((k, tm), lambda j, i: (0, i))
        b_spec = pl.BlockSpec((k, tn), lambda j, i: (0, j))
        dn = TN
    assert m % tm == 0 and n % tn == 0, (m, tm, n, tn)

    def body(a_ref, b_ref, o_ref):
        o_ref[...] = _dot(a_ref[...], b_ref[...], dn).astype(o_ref.dtype)

    return pl.pallas_call(
        body, name=name, grid=(n // tn, m // tm),
        in_specs=[a_spec, b_spec], out_specs=pl.BlockSpec((tm, tn), lambda j, i: (i, j)),
        out_shape=jax.ShapeDtypeStruct((m, n), out_dtype),
        compiler_params=_cparams(("parallel", "parallel")),
    )(a, b)


def _tiles_2d(r, cdim):
    if r % CHUNK == 0:
        return CHUNK, cdim, True
    return r, _tile(cdim, (256, 128)), False


def _mm_sum_nn(a_list, b_list, tm, tn, name):
    n_op = len(a_list)
    m, n = a_list[0].shape[0], b_list[0].shape[1]

    def body(*refs):
        acc = _dot(refs[0][...], refs[n_op][...])
        for i in range(1, n_op):
            acc = acc + _dot(refs[i][...], refs[n_op + i][...])
        refs[2 * n_op][...] = acc

    mode = {"pipeline_mode": pl.Buffered(1)} if n == tn else {}
    return pl.pallas_call(
        body, name=name, grid=(n // tn, m // tm),
        in_specs=([pl.BlockSpec((tm, a.shape[1]), lambda j, i: (i, 0)) for a in a_list]
                  + [pl.BlockSpec((b.shape[0], tn), lambda j, i: (0, j), **mode) for b in b_list]),
        out_specs=pl.BlockSpec((tm, tn), lambda j, i: (i, j)),
        out_shape=jax.ShapeDtypeStruct((m, n), F32),
        compiler_params=_cparams(("parallel", "parallel")),
    )(*a_list, *b_list)


def _tile(n, prefs):
    for t in prefs:
        if n % t == 0:
            return t
    return n


def _rows3(i):
    return jnp.maximum(3 * i - 1, 0), 3 * i, 3 * i + 1


def _x_row_specs(tm, d):
    if tm == CHUNK:
        return [pl.BlockSpec((CHUNK, d), lambda i: (jnp.maximum(i - 1, 0), 0))]
    return [pl.BlockSpec((CHUNK, d), functools.partial(lambda i, k: (_rows3(i)[k], 0), k=k)) for k in range(3)]


def _prenorm_fwd(head, x2, w, tm):
    p, d = x2.shape[0] + CHUNK, x2.shape[1]
    subs = _x_row_specs(tm, d)

    def body(head_ref, *rest):
        x_refs, (w_ref, u_ref) = rest[:len(subs)], rest[len(subs):]
        i = pl.program_id(0)
        first = jnp.where(i == 0, head_ref[...], x_refs[0][...])
        h = jnp.concatenate([first] + [r[...] for r in x_refs[1:]], axis=0)
        ms = jnp.mean(h * h, axis=-1, keepdims=True)
        u_ref[...] = (h * lax.rsqrt(ms + EPS) * w_ref[...]).astype(BF16)

    return pl.pallas_call(
        body, name="prenorm_fwd", grid=(p // tm,),
        in_specs=[_full((CHUNK, d))] + subs + [_full((1, d))],
        out_specs=pl.BlockSpec((tm, d), lambda i: (i, 0)),
        out_shape=jax.ShapeDtypeStruct((p, d), BF16),
        compiler_params=_cparams(("arbitrary",)),
    )(head, *([x2] * len(subs)), w)


def _prenorm_bwd(head, x2, w, du, dout):
    p, d = x2.shape[0] + CHUNK, x2.shape[1]

    def body(head_ref, x_ref, w_ref, du_ref, dout_ref, gx_ref, ghead_ref, gw_ref):
        i = pl.program_id(0)
        h = jnp.where(i == 0, head_ref[...], x_ref[...])
        rstd = lax.rsqrt(jnp.mean(h * h, axis=-1, keepdims=True) + EPS)
        xhat = h * rstd
        dub = du_ref[...]
        dxh = dub * w_ref[...]
        dh = rstd * (dxh - xhat * jnp.mean(dxh * xhat, axis=-1, keepdims=True)) + dout_ref[...]

        @pl.when(i == 0)
        def _():
            ghead_ref[...] = dh
            gw_ref[...] = jnp.zeros_like(gw_ref)

        gx_ref[...] = dh
        gw_ref[0:1, :] += jnp.sum(dub * xhat, axis=0, keepdims=True)

    return pl.pallas_call(
        body, name="prenorm_bwd", grid=(p // CHUNK,),
        in_specs=[_full((CHUNK, d)), pl.BlockSpec((CHUNK, d), lambda i: (jnp.maximum(i - 1, 0), 0)), _full((1, d)),
                  pl.BlockSpec((CHUNK, d), lambda i: (i, 0)), pl.BlockSpec((CHUNK, d), lambda i: (i, 0))],
        out_specs=[pl.BlockSpec((CHUNK, d), lambda i: (jnp.maximum(i - 1, 0), 0)), _full((CHUNK, d)), _full((8, d))],
        out_shape=[jax.ShapeDtypeStruct(x2.shape, F32), jax.ShapeDtypeStruct((CHUNK, d), F32),
                   jax.ShapeDtypeStruct((8, d), F32)],
        compiler_params=_cparams(("arbitrary",)),
    )(head, x2, w, du, dout)


def _conv_pre(ext_ref, cw_ref, cb_ref):
    pre = cb_ref[...] + cw_ref[CONV_K - 1:CONV_K, :] * ext_ref[8:8 + CHUNK, :]
    for j in range(1, CONV_K):
        pre = pre + cw_ref[CONV_K - 1 - j:CONV_K - j, :] * ext_ref[8 - j:8 - j + CHUNK, :]
    return pre


def _ssd_scalars(dtf_ref, brow_ref, alog_ref, rowmask, hs, ha, tri):
    lane = lax.broadcasted_iota(jnp.int32, (1, LANES), 1)
    is_dt = lane < hs
    is_f = (lane >= hs) & (lane < hs + ha)
    dtr = dtf_ref[...] + brow_ref[...]
    sp = _softplus(dtr)
    dt = jnp.where(is_dt, sp, 0.0) * rowmask
    logf = jnp.where(is_f, jnp.minimum(dtr, 0.0) - jnp.log(1.0 + jnp.exp(-jnp.abs(dtr))), 0.0) * rowmask
    a_row = jnp.where(is_dt, -jnp.exp(alog_ref[...]), 0.0)
    run = _dot_tri(tri, dt * a_row + logf)
    return dtr, dt, a_row, run, is_dt, is_f


def _tri_mats():
    r = lax.broadcasted_iota(jnp.int32, (CHUNK, CHUNK), 0)
    c = lax.broadcasted_iota(jnp.int32, (CHUNK, CHUNK), 1)
    return r, c


def _ssd_fwd(xbc, z, dtf, conv_w, conv_b, brow, alog, dskip_l, ssd_norm, sel_t, hs, ha):
    p, cd = xbc.shape
    ds = z.shape[1]
    ns = (cd - ds) // (2 * SSD_GROUPS)
    gw = ds // SSD_GROUPS
    nch = p // CHUNK
    hpg = hs // SSD_GROUPS

    def body(xbc_ref, halo_ref, z_ref, dtf_ref, cw_ref, cb_ref, brow_ref, alog_ref, dsk_ref, nrm_ref, selt_ref,
             y_ref, yssd_ref, hin_ref, cf_ref, pre_ref, st_ref, carry_ref, yacc_ref, xc_s, ex_s, xdtb_s, xwb_s, ext_s):
        c = pl.program_id(0)

        @pl.when(c == 0)
        def _():
            st_ref[...] = jnp.zeros_like(st_ref)
            carry_ref[...] = jnp.zeros_like(carry_ref)

        rows = lax.broadcasted_iota(jnp.int32, (CHUNK, 1), 0)
        rowmask = jnp.where((rows >= PADN) | (c > 0), 1.0, 0.0)
        ri, ci = _tri_mats()
        causal = ri >= ci
        tri = jnp.where(causal, 1.0, 0.0).astype(BF16)

        ext_s[0:8, :] = halo_ref[...].astype(F32)[HALO - 8:, :] * jnp.where(c > 0, 1.0, 0.0)
        ext_s[8:, :] = xbc_ref[...].astype(F32)
        pre = _conv_pre(ext_s, cw_ref, cb_ref)
        pre_ref[...] = pre.astype(BF16)
        xc_s[...] = pre * _sigmoid(pre) * rowmask

        dtr, dt, a_row, run, is_dt, is_f = _ssd_scalars(dtf_ref, brow_ref, alog_ref, rowmask, hs, ha, tri)
        cf = run + carry_ref[...]
        cf_ref[...] = cf
        carry_ref[...] = jnp.where(is_f, cf[CHUNK - 1:CHUNK, :], 0.0)
        cs = jnp.where(is_dt, run, 0.0)
        cl = cs[CHUNK - 1:CHUNK, :]
        selt = selt_ref[...]
        ex_s[...] = _dot_sel(jnp.exp(cs), selt)
        cdec_x = _dot_sel(jnp.broadcast_to(jnp.exp(cl), (8, LANES)), selt)[0:1, :]
        cs_t = cs.T
        xdt = xc_s[:, :ds] * _dot_sel(dt, selt)
        xdtb_s[...] = xdt.astype(BF16)
        xwb_s[...] = (xdt * _dot_sel(jnp.exp(cl - cs), selt)).astype(BF16)

        lane = lax.broadcasted_iota(jnp.int32, (1, LANES), 1)
        half0 = lane < HEAD_DIM
        for g in range(SSD_GROUPS):
            bg = xc_s[:, ds + g * ns: ds + (g + 1) * ns].astype(BF16)
            cg = xc_s[:, ds + SSD_GROUPS * ns + g * ns: ds + SSD_GROUPS * ns + (g + 1) * ns].astype(BF16)
            gm = _dot(cg, bg, NT)
            gs = slice(g * gw, (g + 1) * gw)
            stg = st_ref[:, gs]
            stg_b = stg.astype(BF16)
            hin_ref[0, :, gs] = stg_b
            yoff = _dot(cg, stg_b) * ex_s[:, gs]
            for pr in range(gw // LANES):
                sl = slice(g * gw + pr * LANES, g * gw + (pr + 1) * LANES)
                xp = xdtb_s[:, sl]
                yd = jnp.zeros((CHUNK, LANES), F32)
                for j in range(2):
                    h = g * hpg + 2 * pr + j
                    seg = cs[:, h:h + 1] - cs_t[h:h + 1, :]
                    m = jnp.where(causal, gm * jnp.exp(jnp.minimum(seg, 0.0)), 0.0).astype(BF16)
                    sel = half0 if j == 0 else jnp.logical_not(half0)
                    yd = yd + _dot(m, jnp.where(sel, xp, jnp.zeros_like(xp)))
                yacc_ref[:, sl] = yd + yoff[:, pr * LANES:(pr + 1) * LANES] + dsk_ref[:, sl] * xc_s[:, sl]
            st_ref[:, gs] = stg * cdec_x[:, gs] + _dot(bg, xwb_s[:, gs], TN)

        y = yacc_ref[...]
        y_ref[...] = y.astype(BF16)
        zf = z_ref[...].astype(F32)
        u = y * zf * _sigmoid(zf)
        for g in range(SSD_GROUPS):
            gs = slice(g * gw, (g + 1) * gw)
            ug = u[:, gs]
            ms = jnp.mean(ug * ug, axis=-1, keepdims=True)
            yssd_ref[:, gs] = (ug * lax.rsqrt(ms + EPS) * nrm_ref[:, gs]).astype(BF16)

    rb = CHUNK // HALO
    return pl.pallas_call(
        body, name="ssd_fwd", grid=(nch,),
        in_specs=[pl.BlockSpec((CHUNK, cd), lambda c: (c, 0)),
                  pl.BlockSpec((HALO, cd), lambda c: (jnp.maximum(c * rb - 1, 0), 0)),
                  pl.BlockSpec((CHUNK, ds), lambda c: (c, 0)),
                  pl.BlockSpec((CHUNK, LANES), lambda c: (c, 0)),
                  _full((CONV_K, cd)), _full((1, cd)), _full((1, LANES)), _full((1, LANES)),
                  _full((1, ds)), _full((1, ds)), _full((LANES, ds))],
        out_specs=[pl.BlockSpec((CHUNK, ds), lambda c: (c, 0)), pl.BlockSpec((CHUNK, ds), lambda c: (c, 0)),
                   pl.BlockSpec((1, ns, ds), lambda c: (c, 0, 0)), pl.BlockSpec((CHUNK, LANES), lambda c: (c, 0)),
                   pl.BlockSpec((CHUNK, cd), lambda c: (c, 0))],
        out_shape=[jax.ShapeDtypeStruct((p, ds), BF16), jax.ShapeDtypeStruct((p, ds), BF16),
                   jax.ShapeDtypeStruct((nch, ns, ds), BF16), jax.ShapeDtypeStruct((p, LANES), F32),
                   jax.ShapeDtypeStruct((p, cd), BF16)],
        scratch_shapes=[pltpu.VMEM((ns, ds), F32), pltpu.VMEM((1, LANES), F32), pltpu.VMEM((CHUNK, ds), F32),
                        pltpu.VMEM((CHUNK, cd), F32), pltpu.VMEM((CHUNK, ds), F32),
                        pltpu.VMEM((CHUNK, ds), BF16), pltpu.VMEM((CHUNK, ds), BF16),
                        pltpu.VMEM((8 + CHUNK, cd), F32)],
        compiler_params=_cparams(("arbitrary",)),
    )(xbc, xbc, z, dtf, conv_w, conv_b, brow, alog, dskip_l, ssd_norm, sel_t)


def _ssd_bwd(dyssd, y, z, xbc, pre, dtf, hin, dcf, conv_w, brow, alog, dskip_l, ssd_norm, sel_t, sel, hs, ha):
    p, cd = xbc.shape
    ds = z.shape[1]
    ns = (cd - ds) // (2 * SSD_GROUPS)
    gw = ds // SSD_GROUPS
    nch = p // CHUNK
    hpg = hs // SSD_GROUPS

    def body(dyssd_ref, y_ref, z_ref, xbc_ref, pre_ref, dtf_ref, hin_ref, dcf_ref, cw_ref, brow_ref,
             alog_ref, dsk_ref, nrm_ref, selt_ref, sel_ref,
             dxbc_ref, dz_ref, ddtf_ref, gcw_ref, gcb_ref, gnrm_ref, gsm_ref,
             dst_ref, nxt_ref, fcar_ref, gdsk_ref, dxc_ref, xc_s, dsl_s, dtx_s, ex_s, wx_s, dy_s, xdtb_s, xwb_s,
             dyb_s, dyeb_s):
        step = pl.program_id(0)
        c = nch - 1 - step

        @pl.when(step == 0)
        def _():
            dst_ref[...] = jnp.zeros_like(dst_ref)
            nxt_ref[...] = jnp.zeros_like(nxt_ref)
            fcar_ref[...] = jnp.zeros_like(fcar_ref)
            gdsk_ref[...] = jnp.zeros_like(gdsk_ref)
            gcw_ref[...] = jnp.zeros_like(gcw_ref)
            gcb_ref[...] = jnp.zeros_like(gcb_ref)
            gnrm_ref[...] = jnp.zeros_like(gnrm_ref)
            gsm_ref[...] = jnp.zeros_like(gsm_ref)

        rows = lax.broadcasted_iota(jnp.int32, (CHUNK, 1), 0)
        rowmask = jnp.where((rows >= PADN) | (c > 0), 1.0, 0.0)
        ri, ci = _tri_mats()
        causal = ri >= ci
        anti = ci >= ri
        tri = jnp.where(causal, 1.0, 0.0).astype(BF16)
        rtri = jnp.where(anti, 1.0, 0.0).astype(BF16)

        pre = pre_ref[...].astype(F32)
        sg = _sigmoid(pre)
        xc_s[...] = pre * sg * rowmask
        dsl_s[...] = sg * (1.0 + pre * (1.0 - sg)) * rowmask

        dtr, dt, a_row, run, is_dt, is_f = _ssd_scalars(dtf_ref, brow_ref, alog_ref, rowmask, hs, ha, tri)
        cs = jnp.where(is_dt, run, 0.0)
        cl = cs[CHUNK - 1:CHUNK, :]
        selt = selt_ref[...]
        selm = sel_ref[...]
        dtx_s[...] = _dot_sel(dt, selt)
        ex_s[...] = _dot_sel(jnp.exp(cs), selt)
        wx_s[...] = _dot_sel(jnp.exp(cl - cs), selt)
        cdec = jnp.exp(cl)
        cdec_x = _dot_sel(jnp.broadcast_to(cdec, (8, LANES)), selt)[0:1, :]
        cs_t = cs.T
        xdt = xc_s[:, :ds] * dtx_s[...]
        xdtb_s[...] = xdt.astype(BF16)
        xwb_s[...] = (xdt * wx_s[...]).astype(BF16)

        yv = y_ref[...].astype(F32)
        zf = z_ref[...].astype(F32)
        sz = _sigmoid(zf)
        u = yv * zf * sz
        dyo = dyssd_ref[...].astype(F32)
        du_parts = []
        for g in range(SSD_GROUPS):
            gs = slice(g * gw, (g + 1) * gw)
            ug = u[:, gs]
            rstd = lax.rsqrt(jnp.mean(ug * ug, axis=-1, keepdims=True) + EPS)
            yhat = ug * rstd
            dyg = dyo[:, gs]
            gnrm_ref[0:1, gs] += jnp.sum(dyg * yhat, axis=0, keepdims=True)
            dyh = dyg * nrm_ref[:, gs]
            du_parts.append(rstd * (dyh - yhat * jnp.mean(dyh * yhat, axis=-1, keepdims=True)))
        du = jnp.concatenate(du_parts, axis=1)
        dy = du * zf * sz
        dz_ref[...] = (du * yv * sz * (1.0 + zf * (1.0 - sz))).astype(BF16)
        dy_s[...] = dy
        dyb_s[...] = dy.astype(BF16)
        dyeb_s[...] = (dy * ex_s[...]).astype(BF16)
        gdsk_ref[...] += jnp.sum(dy * xc_s[:, :ds], axis=0, keepdims=True)
        lane = lax.broadcasted_iota(jnp.int32, (1, LANES), 1)
        half0 = lane < HEAD_DIM
        x_parts, yo_parts, t4_parts = [], [], []
        dcs = jnp.zeros((CHUNK, LANES), F32)
        for g in range(SSD_GROUPS):
            gs = slice(g * gw, (g + 1) * gw)
            bsl = slice(ds + g * ns, ds + (g + 1) * ns)
            csl = slice(ds + SSD_GROUPS * ns + g * ns, ds + SSD_GROUPS * ns + (g + 1) * ns)
            bg = xc_s[:, bsl].astype(BF16)
            cg = xc_s[:, csl].astype(BF16)
            gm = _dot(cg, bg, NT)
            gm_t = _dot(bg, cg, NT)
            stg_b = hin_ref[0, :, gs]
            dstg = dst_ref[:, gs]
            dstg_b = dstg.astype(BF16)
            t4_parts.append(jnp.sum(dstg * stg_b.astype(F32), axis=0, keepdims=True))
            zst = _dot(bg, dstg_b) * wx_s[:, gs]
            x_parts.append(xc_s[:, gs] * dtx_s[:, gs] * zst)
            yo_parts.append(dy_s[:, gs] * (_dot(cg, stg_b) * ex_s[:, gs]))
            dgsum = jnp.zeros((CHUNK, CHUNK), F32)
            dgtsum = jnp.zeros((CHUNK, CHUNK), F32)
            for pr in range(gw // LANES):
                sl = slice(g * gw + pr * LANES, g * gw + (pr + 1) * LANES)
                xp = xdtb_s[:, sl]
                dyp = dyb_s[:, sl]
                dxd = zst[:, pr * LANES:(pr + 1) * LANES]
                for j in range(2):
                    h = g * hpg + 2 * pr + j
                    sel_l = half0 if j == 0 else jnp.logical_not(half0)
                    seg = cs[:, h:h + 1] - cs_t[h:h + 1, :]
                    lm = jnp.where(causal, jnp.exp(jnp.minimum(seg, 0.0)), 0.0)
                    lmt = lm.T
                    dyp_m = jnp.where(sel_l, dyp, jnp.zeros_like(dyp))
                    xp_m = jnp.where(sel_l, xp, jnp.zeros_like(xp))
                    dxd = dxd + _dot((gm_t * lmt).astype(BF16), dyp_m)
                    dg = _dot(dyp_m, xp, NT) * lm
                    dgt = _dot(xp_m, dyp, NT) * lmt
                    dgsum = dgsum + dg
                    dgtsum = dgtsum + dgt
                    qrow = (jnp.sum(dg * gm, axis=1, keepdims=True) - jnp.sum(dgt * gm_t, axis=1, keepdims=True))
                    dcs = dcs + jnp.where(lane == h, qrow, 0.0)
                dxc_ref[:, sl] = dxd
            dxc_ref[:, csl] = _dot(dgsum.astype(BF16), bg) + _dot(dyeb_s[:, gs], stg_b, NT)
            dxc_ref[:, bsl] = _dot(dgtsum.astype(BF16), cg) + _dot(xwb_s[:, gs], dstg_b, NT)
            dst_ref[:, gs] = dstg * cdec_x[:, gs] + _dot(cg, dyeb_s[:, gs], TN)

        dxdt = dxc_ref[:, :ds]
        xst = _dot_sel(jnp.concatenate(x_parts, axis=1), selm)
        yo = _dot_sel(jnp.concatenate(yo_parts, axis=1), selm)
        t4 = _dot_sel(jnp.concatenate([jnp.concatenate(t4_parts, axis=1), jnp.zeros((7, ds), F32)], axis=0), selm)
        dcl = jnp.sum(xst, axis=0, keepdims=True) + cdec * t4[0:1, :]
        dcs = dcs + yo - xst + jnp.where(rows == CHUNK - 1, dcl, 0.0)
        da_ = _dot_tri(rtri, dcs)
        ddt = _dot_sel(dxdt * xc_s[:, :ds], selm) + da_ * a_row
        dcf_blk = dcf_ref[...]
        dlogf = _dot_tri(rtri, dcf_blk) + fcar_ref[...]
        fcar_ref[...] += jnp.sum(dcf_blk, axis=0, keepdims=True)
        sgd = _sigmoid(dtr)
        ddtf = (jnp.where(is_dt, ddt * sgd, 0.0) + jnp.where(is_f, dlogf * (1.0 - sgd), 0.0)) * rowmask
        ddtf_ref[...] = ddtf
        gsm_ref[0:1, :] += jnp.sum(ddtf, axis=0, keepdims=True)
        gsm_ref[1:2, :] += jnp.sum(da_ * dt, axis=0, keepdims=True) * a_row

        dxc_ref[:, :ds] = dxdt * dtx_s[...] + dsk_ref[...] * dy_s[...]
        dpre = dxc_ref[...] * dsl_s[...]
        nxt_ref[0:CHUNK, :] = dpre
        gcb_ref[0:1, :] += jnp.sum(dpre, axis=0, keepdims=True)
        xr = xbc_ref[...].astype(F32)
        gcw_ref[CONV_K - 1:CONV_K, :] += jnp.sum(dpre * xr, axis=0, keepdims=True)
        dxr = cw_ref[CONV_K - 1:CONV_K, :] * dpre
        for j in range(1, CONV_K):
            up = nxt_ref[j:j + CHUNK, :]
            gcw_ref[CONV_K - 1 - j:CONV_K - j, :] += jnp.sum(up * xr, axis=0, keepdims=True)
            dxr = dxr + cw_ref[CONV_K - 1 - j:CONV_K - j, :] * up
        nxt_ref[CHUNK:, :] = dpre[0:8, :]
        dxbc_ref[...] = dxr.astype(BF16)

        @pl.when(step == nch - 1)
        def _():
            gsm_ref[2:3, :] = _dot_sel(jnp.broadcast_to(gdsk_ref[...], (8, ds)), selm)[0:1, :]

    rev = lambda s: nch - 1 - s
    blk = lambda w: pl.BlockSpec((CHUNK, w), lambda s: (rev(s), 0))
    return pl.pallas_call(
        body, name="ssd_bwd", grid=(nch,),
        in_specs=[blk(ds), blk(ds), blk(ds), blk(cd), blk(cd),
                  blk(LANES), pl.BlockSpec((1, ns, ds), lambda s: (rev(s), 0, 0)), blk(LANES),
                  _full((CONV_K, cd)), _full((1, LANES)), _full((1, LANES)),
                  _full((1, ds)), _full((1, ds)), _full((LANES, ds)), _full((ds, LANES))],
        out_specs=[blk(cd), blk(ds), blk(LANES), _full((8, cd)), _full((8, cd)), _full((8, ds)), _full((8, LANES))],
        out_shape=[jax.ShapeDtypeStruct((p, cd), BF16), jax.ShapeDtypeStruct((p, ds), BF16),
                   jax.ShapeDtypeStruct((p, LANES), F32), jax.ShapeDtypeStruct((8, cd), F32),
                   jax.ShapeDtypeStruct((8, cd), F32), jax.ShapeDtypeStruct((8, ds), F32),
                   jax.ShapeDtypeStruct((8, LANES), F32)],
        scratch_shapes=[pltpu.VMEM((ns, ds), F32), pltpu.VMEM((CHUNK + 8, cd), F32), pltpu.VMEM((1, LANES), F32),
                        pltpu.VMEM((1, ds), F32), pltpu.VMEM((CHUNK, cd), F32),
                        pltpu.VMEM((CHUNK, cd), F32), pltpu.VMEM((CHUNK, cd), F32),
                        pltpu.VMEM((CHUNK, ds), F32), pltpu.VMEM((CHUNK, ds), F32), pltpu.VMEM((CHUNK, ds), F32),
                        pltpu.VMEM((CHUNK, ds), F32), pltpu.VMEM((CHUNK, ds), BF16), pltpu.VMEM((CHUNK, ds), BF16),
                        pltpu.VMEM((CHUNK, ds), BF16), pltpu.VMEM((CHUNK, ds), BF16)],
        compiler_params=_cparams(("arbitrary",)),
    )(dyssd, y, z, xbc, pre, dtf, hin, dcf, conv_w, brow, alog, dskip_l, ssd_norm, sel_t, sel)


def _attn_fwd(q, k, v, ck, blk):
    p, da = q.shape
    npair, nkb = ck.shape[0], ck.shape[1]
    scale = 1.0 / math.sqrt(HEAD_DIM)

    def body(q_ref, k_ref, v_ref, ck_ref, o_ref, lse_ref):
        i = pl.program_id(1)
        lane = lax.broadcasted_iota(jnp.int32, (1, LANES), 1)
        sels = [lane < HEAD_DIM, lane >= HEAD_DIM]
        ones = [jnp.where(lane == HEAD_DIM, 1.0, 0.0).astype(BF16), jnp.where(lane == 0, 1.0, 0.0).astype(BF16)]
        qb = q_ref[...] * scale
        cmask = (lax.broadcasted_iota(jnp.int32, (blk, blk), 1) <= lax.broadcasted_iota(jnp.int32, (blk, blk), 0))

        def step(kb, carry, masked, nk=1):
            r0 = pl.multiple_of(kb * blk, blk)
            ks = k_ref[pl.ds(r0, nk * blk), :]
            vs = v_ref[pl.ds(r0, nk * blk), :]
            kk = jnp.concatenate([jnp.where(sel, ks, jnp.zeros_like(ks)) for sel in sels], axis=0)
            s_both = _dot(qb, kk, NT)
            out = []
            for j in range(2):
                m, acc = carry[2 * j], carry[2 * j + 1]
                ckr = jnp.concatenate([ck_ref[0, kb + t, j:j + 1, :] for t in range(nk)], axis=1)
                s = s_both[:, j * nk * blk:(j + 1) * nk * blk] - ckr
                if masked:
                    s = jnp.where(cmask, s, NEG)
                mn = jnp.maximum(m, jnp.max(s, axis=-1, keepdims=True))
                pr = jnp.exp(s - mn).astype(BF16)
                acc = jnp.exp(m - mn) * acc + _dot(pr, jnp.where(sels[j], vs, ones[j]))
                out += [mn, acc]
            return tuple(out)

        init = (jnp.full((blk, 1), NEG, F32), jnp.zeros((blk, LANES), F32)) * 2
        n4 = i // 4
        n2 = (i - 4 * n4) // 2
        carry = lax.fori_loop(0, n4, lambda t, c: step(4 * t, c, False, 4), init)
        carry = lax.fori_loop(0, n2, lambda t, c: step(4 * n4 + 2 * t, c, False, 2), carry)
        carry = lax.fori_loop(4 * n4 + 2 * n2, i, lambda kb, c: step(kb, c, False), carry)
        m0, a0, m1, a1 = step(i, carry, True)
        l0 = a0[:, HEAD_DIM:HEAD_DIM + 1]
        l1 = a1[:, 0:1]
        o_ref[...] = jnp.where(sels[0], a0 / l0, a1 / l1).astype(BF16)
        lse_ref[...] = jnp.where(sels[0], m0 + jnp.log(l0), m1 + jnp.log(l1))

    return pl.pallas_call(
        body, name="attn_fwd", grid=(npair, p // blk),
        in_specs=[pl.BlockSpec((blk, LANES), lambda h, i: (i, h)),
                  pl.BlockSpec((p, LANES), lambda h, i: (0, h)), pl.BlockSpec((p, LANES), lambda h, i: (0, h)),
                  pl.BlockSpec((1, nkb, 8, blk), lambda h, i: (h, 0, 0, 0))],
        out_specs=[pl.BlockSpec((blk, LANES), lambda h, i: (i, h)), pl.BlockSpec((blk, LANES), lambda h, i: (i, h))],
        out_shape=[jax.ShapeDtypeStruct((p, da), BF16), jax.ShapeDtypeStruct((p, da), F32)],
        compiler_params=_cparams(("parallel", "arbitrary")),
    )(q, k, v, ck)


def _attn_bwd(q, k, v, o, do, lse_rep, ck, blk):
    p, da = q.shape
    npair, nkb = ck.shape[0], ck.shape[1]
    nq = p // blk
    scale = 1.0 / math.sqrt(HEAD_DIM)

    def body(k_ref, v_ref, q_ref, do_ref, o_ref, lse_ref, ck_ref, dk_ref, dv_ref, dq_ref, dcs_ref, rsum_ref, dq_acc):
        jb = pl.program_id(1)

        @pl.when(jb == 0)
        def _():
            dq_acc[...] = jnp.zeros_like(dq_acc)

        ks = k_ref[...]
        vs = v_ref[...]
        lane = lax.broadcasted_iota(jnp.int32, (1, LANES), 1)
        sels = [lane < HEAD_DIM, lane >= HEAD_DIM]
        ones = [jnp.where(lane == HEAD_DIM, 1.0, 0.0).astype(BF16), jnp.where(lane == 0, 1.0, 0.0).astype(BF16)]
        kss = ks * scale
        kmo = [jnp.where(sels[j], kss, ones[j]) for j in range(2)]
        cmask = (lax.broadcasted_iota(jnp.int32, (blk, blk), 1) <= lax.broadcasted_iota(jnp.int32, (blk, blk), 0))

        def step(ib, carry, masked, nb=1):
            rows = nb * blk
            r0 = pl.multiple_of(ib * blk, blk)
            qb = q_ref[pl.ds(r0, rows), :] * scale
            dob = do_ref[pl.ds(r0, rows), :]
            prod = dob.astype(F32) * o_ref[pl.ds(r0, rows), :].astype(F32)
            out = []
            for j in range(2):
                dk, dv = carry[2 * j], carry[2 * j + 1]
                qm = jnp.where(sels[j], qb, jnp.zeros_like(qb))
                dom = jnp.where(sels[j], dob, jnp.zeros_like(dob))
                lse = lse_ref[pl.ds(r0, rows), HEAD_DIM * j:HEAD_DIM * j + 1]
                dlt = jnp.sum(jnp.where(sels[j], prod, 0.0), axis=-1, keepdims=True)
                s = _dot(qm, ks, NT) - ck_ref[0, 0, j:j + 1, :] - lse
                pm = jnp.exp(jnp.minimum(s, 0.0))
                if masked:
                    pm = jnp.where(cmask, pm, 0.0)
                ds_b = (pm * (_dot(dom, vs, NT) - dlt)).astype(BF16)
                dv = dv + _dot(pm.astype(BF16), dom, TN)
                dk = dk + _dot(ds_b, jnp.where(sels[j], qb, ones[j]), TN)
                dq_acc[pl.ds(r0, rows), LANES * j:LANES * (j + 1)] += _dot(ds_b, kmo[j])
                out += [dk, dv]
            return tuple(out)

        zero = jnp.zeros((blk, LANES), F32)
        carry = step(jb, (zero, zero, zero, zero), True)
        n4 = (nq - 1 - jb) // 4
        n2 = (nq - 1 - jb - 4 * n4) // 2
        carry = lax.fori_loop(0, n4, lambda t, c: step(jb + 1 + 4 * t, c, False, 4), carry)
        carry = lax.fori_loop(0, n2, lambda t, c: step(jb + 1 + 4 * n4 + 2 * t, c, False, 2), carry)
        dk0, dv0, dk1, dv1 = lax.fori_loop(jb + 1 + 4 * n4 + 2 * n2, nq, lambda ib, c: step(ib, c, False), carry)
        dk_ref[...] = jnp.where(sels[0], dk0, dk1).astype(BF16)
        dv_ref[...] = (dv0 + dv1).astype(BF16)
        pair8 = lambda c0, c1: jnp.where(lane == 0, c0, jnp.where(lane == 1, c1, 0.0)).T[0:8]
        dcs_ref[0] = pair8(dk0[:, HEAD_DIM:HEAD_DIM + 1], dk1[:, 0:1])

        @pl.when(jb == nkb - 1)
        def _():
            a0 = dq_acc[:, :LANES]
            a1 = dq_acc[:, LANES:]
            dq_ref[...] = jnp.where(sels[0], a0, a1).astype(BF16)
            rsum_ref[0] = pair8(a0[:, HEAD_DIM:HEAD_DIM + 1], a1[:, 0:1])

    colblk = pl.BlockSpec((blk, LANES), lambda h, j: (j, h))
    colfull = pl.BlockSpec((p, LANES), lambda h, j: (0, h))
    ckspec = pl.BlockSpec((1, 1, 8, blk), lambda h, j: (h, j, 0, 0))
    return pl.pallas_call(
        body, name="attn_bwd", grid=(npair, nkb),
        in_specs=[colblk, colblk, colfull, colfull, colfull, colfull, ckspec],
        out_specs=[colblk, colblk, colfull, pl.BlockSpec((1, 8, blk), lambda h, j: (h, 0, j)),
                   pl.BlockSpec((1, 8, p), lambda h, j: (h, 0, 0))],
        out_shape=[jax.ShapeDtypeStruct((p, da), BF16), jax.ShapeDtypeStruct((p, da), BF16),
                   jax.ShapeDtypeStruct((p, da), BF16), jax.ShapeDtypeStruct((npair, 8, p), F32),
                   jax.ShapeDtypeStruct((npair, 8, p), F32)],
        scratch_shapes=[pltpu.VMEM((p, 2 * LANES), F32)],
        compiler_params=_cparams(("parallel", "arbitrary")),
    )(k, v, q, do, o, lse_rep, ck)


def _tail_fwd(yssd, o, zatt, graw, head, x2, tgt2, wps, wpa, wout, gate_bias, norm_post, tm):
    p, ds = yssd.shape
    da = o.shape[1]
    d = x2.shape[1]
    nsub = tm // CHUNK

    def body(yssd_ref, o_ref, zatt_ref, g_ref, head_ref, *rest):
        x_refs, t_refs = rest[:nsub], rest[nsub:2 * nsub]
        (wps_ref, wpa_ref, wout_ref, gb_ref, np_ref,
         yatt_ref, mrg_ref, a_ref, b_ref, dzo_ref, dout_ref, red_ref) = rest[2 * nsub:]
        i = pl.program_id(0)

        @pl.when(i == 0)
        def _():
            red_ref[...] = jnp.zeros_like(red_ref)

        first = jnp.where(i == 0, head_ref[...], x_refs[0][...])
        h = jnp.concatenate([first] + [r[...] for r in x_refs[1:]], axis=0)
        tgt = jnp.concatenate([r[...] for r in t_refs], axis=0)
        rows = lax.broadcasted_iota(jnp.int32, (tm, 1), 0)
        valid = jnp.where((i > 0) | (rows >= CHUNK), 1.0, 0.0)
        ob = o_ref[...].astype(F32)
        za = zatt_ref[...].astype(F32)
        yatt_b = (ob * za * _sigmoid(za)).astype(BF16)
        yatt_ref[...] = yatt_b
        a = _dot(yssd_ref[...], wps_ref[...])
        b = _dot(yatt_b, wpa_ref[...])
        a_ref[...] = a.astype(BF16)
        b_ref[...] = b.astype(BF16)
        gr = g_ref[...].astype(F32) + gb_ref[...]
        mrg_b = (_sigmoid(gr[:, :d]) * a + _sigmoid(gr[:, d:]) * b).astype(BF16)
        mrg_ref[...] = mrg_b
        zo = _dot(mrg_b, wout_ref[...])
        rstd = lax.rsqrt(jnp.mean(zo * zo, axis=-1, keepdims=True) + EPS)
        zh = zo * rstd
        npw = np_ref[...]
        err = (h + zh * npw - tgt) * valid
        dout = err * (1.0 / d)
        dout_ref[...] = dout
        dzh = dout * npw
        dzo_ref[...] = (rstd * (dzh - zh * jnp.mean(dzh * zh, axis=-1, keepdims=True))).astype(BF16)
        red_ref[0:1, :] += jnp.sum(dout * zh, axis=0, keepdims=True)
        red_ref[1:2, 0:1] += jnp.sum(jnp.sum(err * err, axis=1, keepdims=True), axis=0, keepdims=True) * (0.5 / d)

    row = lambda w: pl.BlockSpec((tm, w), lambda i: (i, 0))
    once = lambda shape: pl.BlockSpec(shape, lambda i: (0,) * len(shape), pipeline_mode=pl.Buffered(1))
    subs = _x_row_specs(tm, d)
    sd = jax.ShapeDtypeStruct
    return pl.pallas_call(
        body, name="tail_fwd", grid=(p // tm,),
        in_specs=[row(ds), row(da), row(da), row(2 * d), _full((CHUNK, d))] + subs + subs
                 + [once((ds, d)), once((da, d)), once((d, d)), _full((1, 2 * d)), _full((1, d))],
        out_specs=[row(da), row(d), row(d), row(d), row(d), row(d), _full((8, d))],
        out_shape=[sd((p, da), BF16), sd((p, d), BF16), sd((p, d), BF16), sd((p, d), BF16), sd((p, d), BF16),
                   sd((p, d), F32), sd((8, d), F32)],
        compiler_params=_cparams(("arbitrary",)),
    )(yssd, o, zatt, graw, head, *([x2] * nsub), *([tgt2] * nsub), wps, wpa, wout, gate_bias, norm_post)


def _tail_bwd(dzo, a_b, b_b, graw, o, zatt, wps, wpa, wout, gate_bias, tm):
    p, d = dzo.shape
    ds, da = wps.shape[0], wpa.shape[0]

    def body(dzo_ref, a_ref, b_ref, g_ref, o_ref, zatt_ref, wps_ref, wpa_ref, wout_ref, gb_ref,
             da_ref, db_ref, dg_ref, dyssd_ref, do_ref, dzatt_ref, red_ref):
        i = pl.program_id(0)

        @pl.when(i == 0)
        def _():
            red_ref[...] = jnp.zeros_like(red_ref)

        gr = g_ref[...].astype(F32) + gb_ref[...]
        gs = _sigmoid(gr[:, :d])
        ga = _sigmoid(gr[:, d:])
        dm = _dot(dzo_ref[...], wout_ref[...], NT)
        da_b = (gs * dm).astype(BF16)
        db_b = (ga * dm).astype(BF16)
        da_ref[...] = da_b
        db_ref[...] = db_b
        dgs = dm * a_ref[...].astype(F32) * gs * (1.0 - gs)
        dga = dm * b_ref[...].astype(F32) * ga * (1.0 - ga)
        dg_ref[:, :d] = dgs.astype(BF16)
        dg_ref[:, d:] = dga.astype(BF16)
        red_ref[0:1, :d] += jnp.sum(dgs, axis=0, keepdims=True)
        red_ref[0:1, d:] += jnp.sum(dga, axis=0, keepdims=True)
        dyssd_ref[...] = _dot(da_b, wps_ref[...], NT).astype(BF16)
        dya = _dot(db_b, wpa_ref[...], NT)
        ob = o_ref[...].astype(F32)
        za = zatt_ref[...].astype(F32)
        sza = _sigmoid(za)
        do_ref[...] = (dya * za * sza).astype(BF16)
        dzatt_ref[...] = (dya * ob * sza * (1.0 + za * (1.0 - sza))).astype(BF16)

    row = lambda w: pl.BlockSpec((tm, w), lambda i: (i, 0))
    once = lambda shape: pl.BlockSpec(shape, lambda i: (0,) * len(shape), pipeline_mode=pl.Buffered(1))
    sd = jax.ShapeDtypeStruct
    return pl.pallas_call(
        body, name="tail_bwd", grid=(p // tm,),
        in_specs=[row(d), row(d), row(d), row(2 * d), row(da), row(da),
                  once((ds, d)), once((da, d)), once((d, d)), _full((1, 2 * d))],
        out_specs=[row(d), row(d), row(2 * d), row(ds), row(da), row(da), _full((8, 2 * d))],
        out_shape=[sd((p, d), BF16), sd((p, d), BF16), sd((p, 2 * d), BF16), sd((p, ds), BF16), sd((p, da), BF16),
                   sd((p, da), BF16), sd((8, 2 * d), F32)],
        compiler_params=_cparams(("arbitrary",)),
    )(dzo, a_b, b_b, graw, o, zatt, wps, wpa, wout, gate_bias)


def _adamw_math(w, g, m, v):
    m2 = ADAM_B1 * m + (1.0 - ADAM_B1) * g
    v2 = ADAM_B2 * v + (1.0 - ADAM_B2) * (g * g)
    m_hat = m2 / (1.0 - ADAM_B1 ** ADAM_STEP)
    v_hat = v2 / (1.0 - ADAM_B2 ** ADAM_STEP)
    delta = -ADAM_LR * (m_hat / (jnp.sqrt(v_hat) + ADAM_EPS) + ADAM_WD * w)
    return delta, m2, v2


def _adamw_small(params, red, name):
    names = list(params)
    n = len(names)
    extra = [params[k][3] for k in names if not isinstance(params[k][3], tuple)]

    def body(*refs):
        w_refs, m_refs, v_refs = refs[:n], refs[n:2 * n], refs[2 * n:3 * n]
        red_ref = refs[3 * n]
        g_refs = iter(refs[3 * n + 1:3 * n + 1 + len(extra)])
        outs = refs[3 * n + 1 + len(extra):]
        for i, k in enumerate(names):
            where = params[k][3]
            rows, cols = w_refs[i].shape
            if isinstance(where, tuple):
                g = red_ref[where[0]:where[0] + rows, where[1]:where[1] + cols]
            else:
                g = next(g_refs)[...]
            delta, m2, v2 = _adamw_math(w_refs[i][...], g, m_refs[i][...], v_refs[i][...])
            for o, val in zip(outs[4 * i:4 * i + 4], (g, delta, m2, v2)):
                o[...] = val

    vm = pl.BlockSpec(memory_space=pltpu.VMEM)
    ws, ms, vs = ([params[k][j] for k in names] for j in range(3))
    out = pl.pallas_call(
        body, name=name,
        out_shape=[jax.ShapeDtypeStruct(w.shape, F32) for w in ws for _ in range(4)],
        in_specs=[vm] * (3 * n + 1 + len(extra)), out_specs=[vm] * (4 * n),
    )(*ws, *ms, *vs, red, *extra)
    return {k: tuple(out[4 * i:4 * i + 4]) for i, k in enumerate(names)}


def _adamw(w, g, m, v, name, parts=False, part_row0=0):
    r, cdim = w.shape
    tr, tc, by_rows = _tiles_2d(r, cdim)
    pick = (lambda i: (i, 0)) if by_rows else (lambda i: (0, i))
    assert part_row0 % tr == 0
    gpick = (lambda i: (i + part_row0 // tr, 0)) if by_rows else (lambda i: (part_row0 // tr, i))

    def body(w_ref, g_ref, m_ref, v_ref, go_ref, d_ref, mo_ref, vo_ref):
        if parts:
            g = g_ref[0].astype(F32)
            for s in range(1, g_ref.shape[0]):
                g = g + g_ref[s].astype(F32)
        else:
            g = g_ref[...]
        delta, m2, v2 = _adamw_math(w_ref[...], g, m_ref[...], v_ref[...])
        go_ref[...] = g
        d_ref[...] = delta
        mo_ref[...] = m2
        vo_ref[...] = v2

    blk = pl.BlockSpec((tr, tc), pick)
    gspec = pl.BlockSpec((g.shape[0], tr, tc), lambda i: (0,) + gpick(i)) if parts else blk
    return pl.pallas_call(
        body, name=name, grid=((r // tr) * (cdim // tc),),
        in_specs=[blk, gspec, blk, blk], out_specs=[blk] * 4,
        out_shape=[jax.ShapeDtypeStruct((r, cdim), F32)] * 4,
        compiler_params=_cparams(("parallel",)),
    )(w, g, m, v)


def _pad_cols(a, width):
    return jnp.pad(a, ((0, 0), (0, width - a.shape[1])))


def _pack_small_shard(conv_w_sh, meta_sh, width):
    return jnp.concatenate([_pad_cols(conv_w_sh, width), jnp.zeros((4, width), F32), _pad_cols(meta_sh, width)], axis=0)


def _pack_small_rep(norm_pre, norm_post, gate_bias, ssd_norm, conv_b, misc, width):
    rows = [norm_pre, norm_post, gate_bias, ssd_norm, conv_b, misc]
    return jnp.concatenate([_pad_cols(r, width) for r in rows] + [jnp.zeros((2, width), F32)], axis=0)


def kernel(x, meta_tokens, norm_pre, w_in, conv_w, conv_b, dt_bias, a_log, d_skip, ssd_norm, fgate_bias, gate_bias, w_proj_ssd, w_proj_att, w_out, norm_post, loss_target, m_meta_tokens, m_norm_pre, m_w_in, m_conv_w, m_conv_b, m_dt_bias, m_a_log, m_d_skip, m_ssd_norm, m_fgate_bias, m_gate_bias, m_w_proj_ssd, m_w_proj_att, m_w_out, m_norm_post, v_meta_tokens, v_norm_pre, v_w_in, v_conv_w, v_conv_b, v_dt_bias, v_a_log, v_d_skip, v_ssd_norm, v_fgate_bias, v_gate_bias, v_w_proj_ssd, v_w_proj_att, v_w_out, v_norm_post):
    seq, d = x.shape[1], x.shape[2]
    p = seq + CHUNK
    hs, ha = dt_bias.shape[1], fgate_bias.shape[1]
    ds, cd = ssd_norm.shape[1], conv_b.shape[1]
    da = ha * HEAD_DIM
    nc8 = w_in.shape[2]
    cws = cd // N_DEV
    msh = d // N_DEV
    r1, r2, r3 = ds // N_DEV, da // N_DEV, d // N_DEV
    me = _dev_index(*_my_pos())
    x2, tgt2 = x[0], loss_target[0]

    win_sh = jnp.transpose(w_in[0]).astype(BF16)
    rows_sh = jnp.concatenate([w_proj_ssd[0], w_proj_att[0], w_out[0]], axis=0).astype(BF16)
    small_sh = _pack_small_shard(conv_w[0], meta_tokens, cws)
    win_all, small_all = _all_gather([win_sh, small_sh], "gather_weights")
    rows_sh, win_all = lax.optimization_barrier((rows_sh, win_all))
    rows_sems, rows_thru, rows_land, rows_token = _bcast_start(rows_sh, "gather_rows_start")
    cuts = [0, ds, ds + cd, ds + cd + hs, ds + cd + hs + da, ds + cd + hs + 2 * da, ds + cd + hs + 3 * da,
            ds + cd + hs + 4 * da, ds + cd + hs + 4 * da + ha, ds + cd + hs + 4 * da + ha + 2 * d]

    def piece_rows(r0, r1):
        parts = [win_all[s, max(r0, s * nc8) - s * nc8:min(r1, (s + 1) * nc8) - s * nc8]
                 for s in range(N_DEV) if max(r0, s * nc8) < min(r1, (s + 1) * nc8)]
        return parts[0] if len(parts) == 1 else jnp.concatenate(parts, axis=0)

    w_z, w_xbc, w_dt, w_zatt, w_q, w_k, w_v, w_f, w_g = [piece_rows(cuts[i], cuts[i + 1]) for i in range(9)]
    w_dtf = jnp.concatenate([w_dt, w_f, jnp.zeros((LANES - hs - ha, d), BF16)], axis=0)
    conv_w_full = jnp.transpose(small_all[:, 0:CONV_K, :], (1, 0, 2)).reshape(CONV_K, cd)
    meta_full = jnp.transpose(small_all[:, 8:8 + N_META, :msh], (1, 0, 2)).reshape(N_META, d)
    head = jnp.concatenate([jnp.zeros((PADN, d), F32), meta_full + rows_token[0:1, 0:1]], axis=0)

    tm = _att_block(p)
    u = _prenorm_fwd(head, x2, norm_pre, tm)
    seg_w = [w_z, w_xbc, w_zatt, w_q, w_k, w_v, w_g]
    zs, xbc, zatt, q, k, v, graw = [
        _mm(u, w, "nt", BF16, _tile(p, (1408, tm)), _tile(w.shape[0], (1024, 512, 256, 128)), "inproj_%d" % i)
        for i, w in enumerate(seg_w)]
    dtf = _mm(u, w_dtf, "nt", F32, _tile(p, (1408, tm)), LANES, "inproj_dtf")

    brow = jnp.concatenate([dt_bias, fgate_bias, jnp.zeros((1, LANES - hs - ha), F32)], axis=1)
    alog_row = _pad_cols(a_log, LANES)
    dskip_l = jnp.repeat(d_skip, HEAD_DIM, axis=1)
    sel_t = (lax.broadcasted_iota(jnp.int32, (LANES, ds), 1) // HEAD_DIM
             == lax.broadcasted_iota(jnp.int32, (LANES, ds), 0)).astype(BF16)
    sel = sel_t.T
    y, yssd, hin, cf, pre = _ssd_fwd(xbc, zs, dtf, conv_w_full, conv_b, brow, alog_row, dskip_l, ssd_norm, sel_t, hs, ha)

    blk = _att_block(p)
    nkb, npair = p // blk, ha // 2
    cum = jnp.where(lax.broadcasted_iota(jnp.int32, (p, 1), 0) < PADN, -NEG, cf[:, hs:hs + ha])
    ck = jnp.transpose(cum.T.reshape(npair, 2, nkb, blk), (0, 2, 1, 3))
    ck = jnp.pad(ck, ((0, 0), (0, 0), (0, 6), (0, 0)))
    o, lse_rep = _attn_fwd(q, k, v, ck, blk)

    rows_all = _bcast_wait(rows_sems, rows_thru, rows_land, lse_rep, "gather_rows_wait")
    wps = rows_all[:, :r1].reshape(ds, d)
    wpa = rows_all[:, r1:r1 + r2].reshape(da, d)
    wout = rows_all[:, r1 + r2:].reshape(d, d)

    yatt, mrg, a_b, b_b, dzo, dout, red_fwd = _tail_fwd(
        yssd, o, zatt, graw, head, x2, tgt2, wps, wpa, wout, gate_bias, norm_post, tm)
    da_, db_, dgraw, dyssd, d_o, dzatt, red_bwd = _tail_bwd(dzo, a_b, b_b, graw, o, zatt, wps, wpa, wout, gate_bias, tm)

    tw = _tile(d, (512, 256, 128))
    g_wout = _mm(mrg, dzo, "tn", BF16, tw, d, "wgrad_out")
    g_wps = _mm(yssd, da_, "tn", BF16, _tile(ds, (512, 256, 128)), d, "wgrad_ps")
    g_wpa = _mm(yatt, db_, "tn", BF16, _tile(da, (512, 256, 128)), d, "wgrad_pa")

    dk, dv, dq, dcs, rsum = _attn_bwd(q, k, v, o, d_o, lse_rep, ck, blk)
    dcum = (rsum - dcs)[:, 0:2, :].reshape(ha, p).T
    dcf = jnp.pad(dcum, ((0, 0), (hs, LANES - hs - ha)))
    dxbc, dzs, ddtf, gcw, gcb, gnrm, gsm = _ssd_bwd(
        dyssd, y, zs, xbc, pre, dtf, hin, dcf, conv_w_full, brow, alog_row, dskip_l, ssd_norm, sel_t, sel, hs, ha)
    ddtf_b = ddtf.astype(BF16)

    dsegs = [dzs, dxbc, dzatt, dq, dk, dv, dgraw, ddtf_b]
    gsegs = [_mm(dsg, u, "tn", BF16, _tile(dsg.shape[1], (512, 256, 128)), d, "wgrad_in_%d" % i)
             for i, dsg in enumerate(dsegs)]
    g_z, g_xbc, g_zatt, g_q, g_k, g_v, g_g, g_dtf = gsegs
    gw_full = jnp.concatenate([g_z, g_xbc, g_dtf[:hs], g_zatt, g_q, g_k, g_v, g_dtf[hs:hs + ha], g_g], axis=0)
    gwin_parts = gw_full.reshape(N_DEV, nc8, d)
    grows_parts = jnp.concatenate([g_wps.reshape(N_DEV, r1, d), g_wpa.reshape(N_DEV, r2, d),
                                   g_wout.reshape(N_DEV, r3, d)], axis=1)

    core = lax.axis_index("c").astype(jnp.int32).reshape(1)
    sib_win, sib_rows = _exchange_sibling([gwin_parts, grows_parts], "scatter_grads_sibling")
    chip_win = _pair_add(gwin_parts, sib_win, core, "pair_add_w_in")
    chip_rows = _pair_add(grows_parts, sib_rows, core, "pair_add_rows")
    sems, thru, lands, token = _exchange_chips_start([chip_win, chip_rows], "scatter_grads_start")
    dsegs_after = dsegs[:-1] + [ddtf_b + token[0:1, 0:1].astype(BF16)]
    du = _mm_sum_nn(dsegs_after, seg_w + [w_dtf], tm, d, "dgrad_in")
    gx, ghead, gnp = _prenorm_bwd(head, x2, norm_pre, du, dout)
    sent, got = _exchange_chips_wait(sems, thru, lands, gnp, "scatter_grads_wait")
    chip = me // 2
    recv_win, recv_rows = [lax.dynamic_update_slice_in_dim(g, lax.dynamic_slice_in_dim(s, chip, 1, axis=0), chip, axis=0)
                           for g, s in zip(got, sent)]
    gmisc = jnp.concatenate([gsm[0:1], gsm[1:2], gsm[2:3], _pad_cols(red_fwd[1:2, 0:1], LANES)], axis=1)
    small_g = jnp.concatenate([
        _pack_small_rep(gnp[0:1], red_fwd[0:1], red_bwd[0:1], gnrm[0:1], gcb[0:1], gmisc, cd),
        _pad_cols(gcw[0:CONV_K], cd), jnp.zeros((4, cd), F32), _pad_cols(ghead[PADN:], cd)], axis=0)
    sg_sems, sg_thru, sg_land, sg_token = _bcast_start(small_g, "reduce_small_start")

    upd_in = _adamw(jnp.transpose(w_in[0]) + sg_token[0:1, 0:1], recv_win, jnp.transpose(m_w_in[0]),
                    jnp.transpose(v_w_in[0]), "adamw_w_in", parts=True)
    upd_ps = _adamw(w_proj_ssd[0] + sg_token[0:1, 0:1], recv_rows, m_w_proj_ssd[0], v_w_proj_ssd[0],
                    "adamw_w_proj_ssd", parts=True, part_row0=0)
    upd_pa = _adamw(w_proj_att[0], recv_rows, m_w_proj_att[0], v_w_proj_att[0], "adamw_w_proj_att", parts=True,
                    part_row0=r1)
    upd_out = _adamw(w_out[0], recv_rows, m_w_out[0], v_w_out[0], "adamw_w_out", parts=True, part_row0=r1 + r2)
    all_done = upd_in[1][0:8, 0:LANES] + upd_ps[1][0:8, 0:LANES] + upd_pa[1][0:8, 0:LANES] + upd_out[1][0:8, 0:LANES]
    red = _sum_slots(_bcast_wait(sg_sems, sg_thru, sg_land, all_done, "reduce_small_wait"), "reduce_small_sum")
    loss = red[5, 3 * LANES]
    g_conv_w = lax.dynamic_slice_in_dim(red[8:8 + CONV_K], me * cws, cws, axis=1)
    g_meta = lax.dynamic_slice_in_dim(red[16:16 + N_META, :d], me * msh, msh, axis=1)
    small = {
        "meta_tokens": (meta_tokens, m_meta_tokens, v_meta_tokens, g_meta),
        "norm_pre": (norm_pre, m_norm_pre, v_norm_pre, (0, 0)),
        "conv_w": (conv_w[0], m_conv_w[0], v_conv_w[0], g_conv_w),
        "conv_b": (conv_b, m_conv_b, v_conv_b, (4, 0)),
        "dt_bias": (dt_bias, m_dt_bias, v_dt_bias, (5, 0)),
        "a_log": (a_log, m_a_log, v_a_log, (5, LANES)),
        "d_skip": (d_skip, m_d_skip, v_d_skip, (5, 2 * LANES)),
        "ssd_norm": (ssd_norm, m_ssd_norm, v_ssd_norm, (3, 0)),
        "fgate_bias": (fgate_bias, m_fgate_bias, v_fgate_bias, (5, hs)),
        "gate_bias": (gate_bias, m_gate_bias, v_gate_bias, (2, 0)),
        "norm_post": (norm_post, m_norm_post, v_norm_post, (1, 0)),
    }
    upd_small = _adamw_small(small, red, "adamw_small")

    def leaves(i):
        sm = {k: v[i] for k, v in upd_small.items()}
        return [sm["meta_tokens"], sm["norm_pre"], jnp.transpose(upd_in[i])[None], sm["conv_w"][None], sm["conv_b"],
                sm["dt_bias"], sm["a_log"], sm["d_skip"], sm["ssd_norm"], sm["fgate_bias"], sm["gate_bias"],
                upd_ps[i][None], upd_pa[i][None], upd_out[i][None], sm["norm_post"]]

    return tuple([loss, gx[None]] + leaves(0) + leaves(1) + leaves(2) + leaves(3))
```

```python
import functools
import math

import jax
import jax.numpy as jnp
from jax import lax
from jax.experimental import pallas as pl
from jax.experimental.pallas import tpu as pltpu

F32 = jnp.float32
BF16 = jnp.bfloat16

N_DEV = 8
N_META = 16
CHUNK = 128
PADN = CHUNK - N_META
HEAD_DIM = 64
SSD_GROUPS = 4
CONV_K = 4
EPS = 1e-6
NEG = -1e30
LANES = 128
HALO = 16

ADAM_LR = 0.001
ADAM_B1 = 0.9
ADAM_B2 = 0.999
ADAM_EPS = 1e-08
ADAM_WD = 0.01
ADAM_STEP = 10

VMEM_LIMIT = 56 * 1024 * 1024

NN = (((1,), (0,)), ((), ()))
NT = (((1,), (1,)), ((), ()))
TN = (((0,), (0,)), ((), ()))
MESH = pl.DeviceIdType.MESH


def _dot(a, b, dims=NN):
    return lax.dot_general(a, b, dims, preferred_element_type=F32)


def _split2(x):
    hi = x.astype(BF16)
    lo = (x - hi.astype(F32)).astype(BF16)
    return hi, lo


def _dot_sel(x, sel):
    hi, lo = _split2(x)
    return _dot(hi, sel) + _dot(lo, sel)


def _dot_tri(tri, x):
    h1 = x.astype(BF16)
    r1 = x - h1.astype(F32)
    h2 = r1.astype(BF16)
    h3 = (r1 - h2.astype(F32)).astype(BF16)
    return _dot(tri, h1) + _dot(tri, h2) + _dot(tri, h3)


def _sigmoid(x):
    return 0.5 * jnp.tanh(0.5 * x) + 0.5


def _softplus(x):
    return jnp.maximum(x, 0.0) + jnp.log(1.0 + jnp.exp(-jnp.abs(x)))


def _cparams(sem=None, vmem=VMEM_LIMIT):
    kw = {"vmem_limit_bytes": vmem}
    if sem is not None:
        kw["dimension_semantics"] = sem
    return pltpu.CompilerParams(**kw)


def _full(shape):
    nd = len(shape)
    return pl.BlockSpec(shape, lambda *_: (0,) * nd)


def _att_block(p):
    return 384 if p % 384 == 0 else CHUNK


def _my_pos():
    return lax.axis_index("x"), lax.axis_index("y"), lax.axis_index("c")


def _dev_index(x, y, c):
    return 4 * x + 2 * y + c


FLIPS = [(fx, fy, fc) for fx in (0, 1) for fy in (0, 1) for fc in (0, 1)][1:]


def _flip(pos, f):
    return tuple((1 - p) if fi else p for p, fi in zip(pos, f))


def _all_gather(bufs, name):
    nb = len(bufs)

    def body(*refs):
        ins, outs = refs[:nb], refs[nb:2 * nb]
        send_sems, recv_sems, local_sems = refs[2 * nb:]
        x, y, c = _my_pos()
        me = _dev_index(x, y, c)
        sibling = (x, y, 1 - c)
        near = [(1 - x, y), (x, 1 - y)]
        far = (1 - x, 1 - y)
        relay_from = (c * (1 - x) + (1 - c) * x, c * y + (1 - c) * (1 - y))
        relay_to = (c * x + (1 - c) * (1 - x), c * (1 - y) + (1 - c) * y)

        def copy(b, k, block_idx, to, src=None):
            dst = outs[b].at[block_idx]
            return pltpu.make_async_remote_copy(
                src_ref=dst if src is None else src, dst_ref=dst,
                send_sem=send_sems.at[b, k], recv_sem=recv_sems.at[b, k],
                device_id=to, device_id_type=MESH)

        started = []
        for b in range(nb):
            mine = pltpu.make_async_copy(ins[b], outs[b].at[me], local_sems.at[b])
            mine.start()
            started.append(mine)
        sent = []
        for b in range(nb):
            sent.append(copy(b, 0, me, sibling, src=ins[b]))
            for j, chip in enumerate(near):
                sent.append(copy(b, 1 + j, me, (chip[0], chip[1], c), src=ins[b]))
        for cp in sent:
            cp.start()
        for j, chip in enumerate(near):
            blk = _dev_index(chip[0], chip[1], c)
            for b in range(nb):
                copy(b, 1 + j, blk, (x, y, c)).wait_recv()
                sent.append(copy(b, 4 + j, blk, sibling))
                sent[-1].start()
        for b in range(nb):
            sent.append(copy(b, 3, _dev_index(relay_from[0], relay_from[1], c), (relay_to[0], relay_to[1], c)))
            sent[-1].start()
        blk = _dev_index(far[0], far[1], c)
        for b in range(nb):
            copy(b, 3, blk, (x, y, c)).wait_recv()
            sent.append(copy(b, 6, blk, sibling))
            sent[-1].start()
        for b in range(nb):
            copy(b, 0, _dev_index(x, y, 1 - c), (x, y, c)).wait_recv()
        for j, chip in enumerate(near + [far]):
            blk = _dev_index(chip[0], chip[1], 1 - c)
            for b in range(nb):
                copy(b, 4 + j, blk, (x, y, c)).wait_recv()
        for cp in sent:
            cp.wait_send()
        for mine in started:
            mine.wait()

    any_spec = pl.BlockSpec(memory_space=pl.ANY)
    return pl.pallas_call(
        body, name=name,
        out_shape=[jax.ShapeDtypeStruct((N_DEV,) + b.shape, b.dtype) for b in bufs],
        in_specs=[any_spec] * nb, out_specs=[any_spec] * nb,
        scratch_shapes=[pltpu.SemaphoreType.DMA((nb, 7)), pltpu.SemaphoreType.DMA((nb, 7)),
                        pltpu.SemaphoreType.DMA((nb,))],
    )(*bufs)


N_CHIP = 4
CHIP_FLIPS = [(1, 0), (0, 1), (1, 1)]


def _exchange_sibling(bufs, name):
    nb = len(bufs)

    def body(*refs):
        ins, outs = refs[:nb], refs[nb:2 * nb]
        send_sems, recv_sems = refs[2 * nb:]
        x, y, c = _my_pos()

        def copy(b, k):
            return pltpu.make_async_remote_copy(
                src_ref=ins[b].at[2 * k + (1 - c)], dst_ref=outs[b].at[k],
                send_sem=send_sems.at[b, k], recv_sem=recv_sems.at[b, k],
                device_id=(x, y, 1 - c), device_id_type=MESH)

        cps = [copy(b, k) for b in range(nb) for k in range(N_CHIP)]
        for cp in cps:
            cp.start()
        for cp in cps:
            cp.wait()

    any_spec = pl.BlockSpec(memory_space=pl.ANY)
    return pl.pallas_call(
        body, name=name,
        out_shape=[jax.ShapeDtypeStruct((N_CHIP,) + b.shape[1:], b.dtype) for b in bufs],
        in_specs=[any_spec] * nb, out_specs=[any_spec] * nb,
        scratch_shapes=[pltpu.SemaphoreType.DMA((nb, N_CHIP)), pltpu.SemaphoreType.DMA((nb, N_CHIP))],
    )(*bufs)


def _pair_add(mine, recv, core, name):
    _, r, cdim = mine.shape
    tr, tc = r, cdim
    pick = lambda i: (i, 0)

    def body(core_ref, a_ref, b_ref, o_ref):
        o_ref[0] = (a_ref[0].astype(F32) + b_ref[0].astype(F32)).astype(o_ref.dtype)

    return pl.pallas_call(
        body, name=name,
        grid_spec=pltpu.PrefetchScalarGridSpec(
            num_scalar_prefetch=1, grid=(N_CHIP, (r // tr) * (cdim // tc)),
            in_specs=[pl.BlockSpec((1, tr, tc), lambda k, i, core_ref: (2 * k + core_ref[0],) + pick(i)),
                      pl.BlockSpec((1, tr, tc), lambda k, i, core_ref: (k,) + pick(i))],
            out_specs=pl.BlockSpec((1, tr, tc), lambda k, i, core_ref: (k,) + pick(i))),
        out_shape=jax.ShapeDtypeStruct((N_CHIP, r, cdim), mine.dtype),
        compiler_params=_cparams(("parallel", "parallel")),
    )(core, mine, recv)


def _chip_peer(x, y, f):
    return ((1 - x) if f[0] else x), ((1 - y) if f[1] else y)


def _exchange_chips_start(bufs, name):
    nb = len(bufs)
    nsem = 2 * 3 * nb

    def body(*refs):
        ins, lands = refs[:nb], refs[nb:2 * nb]
        sems = refs[2 * nb:2 * nb + nsem]
        token = refs[-1]
        x, y, c = _my_pos()
        for b in range(nb):
            for j, f in enumerate(CHIP_FLIPS):
                px, py = _chip_peer(x, y, f)
                pltpu.make_async_remote_copy(
                    src_ref=ins[b].at[2 * px + py], dst_ref=lands[b].at[2 * x + y],
                    send_sem=sems[2 * (3 * b + j)], recv_sem=sems[2 * (3 * b + j) + 1],
                    device_id=(px, py, c), device_id_type=MESH).start()
        token[...] = jnp.zeros_like(token)

    hbm = pl.BlockSpec(memory_space=pltpu.HBM)
    sem = pl.BlockSpec(memory_space=pltpu.SEMAPHORE)
    out = pl.pallas_call(
        body, name=name,
        out_shape=(*([pltpu.SemaphoreType.DMA(())] * nsem),
                   *[pltpu.HBM(b.shape, b.dtype) for b in bufs], *[pltpu.HBM(b.shape, b.dtype) for b in bufs],
                   jax.ShapeDtypeStruct((8, LANES), F32)),
        in_specs=[hbm] * (2 * nb),
        out_specs=(*([sem] * nsem), *([hbm] * (2 * nb)), pl.BlockSpec(memory_space=pltpu.VMEM)),
        input_output_aliases={i: nsem + i for i in range(2 * nb)},
        compiler_params=pltpu.CompilerParams(has_side_effects=pltpu.SideEffectType.DATAFLOW_SIDE_EFFECTING),
    )(*[pltpu.with_memory_space_constraint(b, pltpu.HBM) for b in bufs],
      *[pltpu.with_memory_space_constraint(lax.empty(b.shape, b.dtype), pltpu.HBM) for b in bufs])
    return out[:nsem], out[nsem:nsem + nb], out[nsem + nb:nsem + 2 * nb], out[-1]


def _exchange_chips_wait(sems, thru, lands, after, name):
    nb = len(thru)
    nsem = len(sems)

    def body(*refs):
        ins, lnd = refs[:nb], refs[nb:2 * nb]
        sem_refs = refs[2 * nb:2 * nb + nsem]
        x, y, c = _my_pos()
        for b in range(nb):
            for j, f in enumerate(CHIP_FLIPS):
                px, py = _chip_peer(x, y, f)
                cp = pltpu.make_async_remote_copy(
                    src_ref=ins[b].at[2 * px + py], dst_ref=lnd[b].at[2 * px + py],
                    send_sem=sem_refs[2 * (3 * b + j)], recv_sem=sem_refs[2 * (3 * b + j) + 1],
                    device_id=(px, py, c), device_id_type=MESH)
                cp.wait_send()
                cp.wait_recv()

    hbm = pl.BlockSpec(memory_space=pltpu.HBM)
    sem = pl.BlockSpec(memory_space=pltpu.SEMAPHORE)
    out = pl.pallas_call(
        body, name=name,
        out_shape=tuple([pltpu.HBM(b.shape, b.dtype) for b in thru] + [pltpu.HBM(b.shape, b.dtype) for b in lands]),
        in_specs=[hbm] * (2 * nb) + [sem] * nsem + [pl.BlockSpec(memory_space=pl.ANY)],
        out_specs=tuple([hbm] * (2 * nb)),
        input_output_aliases={i: i for i in range(2 * nb)},
        compiler_params=pltpu.CompilerParams(has_side_effects=pltpu.SideEffectType.DATAFLOW_SIDE_EFFECTING),
    )(*thru, *lands, *sems, after)
    return out[:nb], out[nb:]


def _bcast_start(buf, name):
    nsem = 2 * len(FLIPS)

    def body(src, land, *rest):
        sems, token = rest[:nsem], rest[-1]
        pos = _my_pos()
        for k, f in enumerate(FLIPS):
            pltpu.make_async_remote_copy(
                src_ref=src, dst_ref=land.at[_dev_index(*pos)], send_sem=sems[2 * k], recv_sem=sems[2 * k + 1],
                device_id=_flip(pos, f), device_id_type=MESH).start()
        token[...] = jnp.zeros_like(token)

    hbm = pl.BlockSpec(memory_space=pltpu.HBM)
    sem = pl.BlockSpec(memory_space=pltpu.SEMAPHORE)
    land_shape = (N_DEV,) + buf.shape
    out = pl.pallas_call(
        body, name=name,
        out_shape=(*([pltpu.SemaphoreType.DMA(())] * nsem), pltpu.HBM(buf.shape, buf.dtype),
                   pltpu.HBM(land_shape, buf.dtype), jax.ShapeDtypeStruct((8, LANES), F32)),
        in_specs=[hbm, hbm],
        out_specs=(*([sem] * nsem), hbm, hbm, pl.BlockSpec(memory_space=pltpu.VMEM)),
        input_output_aliases={0: nsem, 1: nsem + 1},
        compiler_params=pltpu.CompilerParams(has_side_effects=pltpu.SideEffectType.DATAFLOW_SIDE_EFFECTING),
    )(pltpu.with_memory_space_constraint(buf, pltpu.HBM),
      pltpu.with_memory_space_constraint(lax.empty(land_shape, buf.dtype), pltpu.HBM))
    return out[:nsem], out[nsem], out[nsem + 1], out[-1]


def _bcast_wait(sems, thru, land, after, name):
    nsem = len(sems)

    def body(src, lnd, *rest):
        sem_refs = rest[:nsem]
        pos = _my_pos()
        for k, f in enumerate(FLIPS):
            peer = _flip(pos, f)
            cp = pltpu.make_async_remote_copy(
                src_ref=src, dst_ref=lnd.at[_dev_index(*peer)], send_sem=sem_refs[2 * k],
                recv_sem=sem_refs[2 * k + 1], device_id=peer, device_id_type=MESH)
            cp.wait_send()
            cp.wait_recv()

    hbm = pl.BlockSpec(memory_space=pltpu.HBM)
    sem = pl.BlockSpec(memory_space=pltpu.SEMAPHORE)
    sent, got = pl.pallas_call(
        body, name=name,
        out_shape=(pltpu.HBM(thru.shape, thru.dtype), pltpu.HBM(land.shape, land.dtype)),
        in_specs=[hbm, hbm] + [sem] * nsem + [pl.BlockSpec(memory_space=pl.ANY)],
        out_specs=(hbm, hbm), input_output_aliases={0: 0, 1: 1},
        compiler_params=pltpu.CompilerParams(has_side_effects=pltpu.SideEffectType.DATAFLOW_SIDE_EFFECTING),
    )(thru, land, *sems, after)
    return lax.dynamic_update_slice_in_dim(got, sent[None], _dev_index(*_my_pos()), axis=0)


def _sum_slots(v, name):
    _, r, cdim = v.shape

    def body(v_ref, o_ref):
        acc = v_ref[0]
        for s in range(1, N_DEV):
            acc = acc + v_ref[s]
        o_ref[...] = acc

    return pl.pallas_call(
        body, name=name, out_shape=jax.ShapeDtypeStruct((r, cdim), F32),
        in_specs=[_full((N_DEV, r, cdim))], out_specs=_full((r, cdim)), grid=(1,),
        compiler_params=_cparams(("arbitrary",)),
    )(v)


def _mm(a, b, dims, out_dtype, tm, tn, name):
    if dims == "nn":
        (m, k), (_, n) = a.shape, b.shape
        a_spec = pl.BlockSpec((tm, k), lambda j, i: (i, 0))
        b_spec = pl.BlockSpec((k, tn), lambda j, i: (0, j))
        dn = NN
    elif dims == "nt":
        (m, k), (n, _) = a.shape, b.shape
        a_spec = pl.BlockSpec((tm, k), lambda j, i: (i, 0))
        b_spec = pl.BlockSpec((tn, k), lambda j, i: (j, 0))
        dn = NT
    else:
        (k, m), (_, n) = a.shape, b.shape
        a_spec = pl.BlockSpec((k, tm), lambda j, i: (0, i))
        b_spec = pl.BlockSpec((k, tn), lambda j, i: (0, j))
        dn = TN
    assert m % tm == 0 and n % tn == 0, (m, tm, n, tn)

    def body(a_ref, b_ref, o_ref):
        o_ref[...] = _dot(a_ref[...], b_ref[...], dn).astype(o_ref.dtype)

    return pl.pallas_call(
        body, name=name, grid=(n // tn, m // tm),
        in_specs=[a_spec, b_spec], out_specs=pl.BlockSpec((tm, tn), lambda j, i: (i, j)),
        out_shape=jax.ShapeDtypeStruct((m, n), out_dtype),
        compiler_params=_cparams(("parallel", "parallel")),
    )(a, b)


def _tiles_2d(r, cdim):
    if r % CHUNK == 0:
        return CHUNK, cdim, True
    return r, _tile(cdim, (256, 128)), False


def _dgrad_prenorm(a_list, b_list, head, x2, w, dout, tm, name):
    n_op = len(a_list)
    m, d = a_list[0].shape[0], b_list[0].shape[1]
    subs = _x_row_specs(tm, d)

    def body(*refs):
        a_refs, b_refs = refs[:n_op], refs[n_op:2 * n_op]
        head_ref = refs[2 * n_op]
        x_refs = refs[2 * n_op + 1:2 * n_op + 1 + len(subs)]
        w_ref, dout_ref, dh_ref, gw_ref = refs[2 * n_op + 1 + len(subs):]
        i = pl.program_id(0)

        @pl.when(i == 0)
        def _():
            gw_ref[...] = jnp.zeros_like(gw_ref)

        du = _dot(a_refs[0][...], b_refs[0][...])
        for k in range(1, n_op):
            du = du + _dot(a_refs[k][...], b_refs[k][...])
        first = jnp.where(i == 0, head_ref[...], x_refs[0][...])
        h = jnp.concatenate([first] + [r[...] for r in x_refs[1:]], axis=0)
        rstd = lax.rsqrt(jnp.mean(h * h, axis=-1, keepdims=True) + EPS)
        xhat = h * rstd
        dxh = du * w_ref[...]
        dh_ref[...] = rstd * (dxh - xhat * jnp.mean(dxh * xhat, axis=-1, keepdims=True)) + dout_ref[...]
        gw_ref[0:1, :] += jnp.sum(du * xhat, axis=0, keepdims=True)

    once = lambda b: pl.BlockSpec(b.shape, lambda i: (0, 0), pipeline_mode=pl.Buffered(1))
    row = lambda width: pl.BlockSpec((tm, width), lambda i: (i, 0))
    return pl.pallas_call(
        body, name=name, grid=(m // tm,),
        in_specs=([row(a.shape[1]) for a in a_list] + [once(b) for b in b_list]
                  + [_full((CHUNK, d))] + subs + [_full((1, d)), row(d)]),
        out_specs=[row(d), _full((8, d))],
        out_shape=[jax.ShapeDtypeStruct((m, d), F32), jax.ShapeDtypeStruct((8, d), F32)],
        compiler_params=_cparams(("arbitrary",)),
    )(*a_list, *b_list, head, *([x2] * len(subs)), w, dout)


def _tile(n, prefs):
    for t in prefs:
        if n % t == 0:
            return t
    return n


def _rows3(i):
    return jnp.maximum(3 * i - 1, 0), 3 * i, 3 * i + 1


def _x_row_specs(tm, d):
    if tm == CHUNK:
        return [pl.BlockSpec((CHUNK, d), lambda i: (jnp.maximum(i - 1, 0), 0))]
    return [pl.BlockSpec((CHUNK, d), functools.partial(lambda i, k: (_rows3(i)[k], 0), k=k)) for k in range(3)]


def _prenorm_fwd(head, x2, w, tm):
    p, d = x2.shape[0] + CHUNK, x2.shape[1]
    subs = _x_row_specs(tm, d)

    def body(head_ref, *rest):
        x_refs, (w_ref, u_ref) = rest[:len(subs)], rest[len(subs):]
        i = pl.program_id(0)
        first = jnp.where(i == 0, head_ref[...], x_refs[0][...])
        h = jnp.concatenate([first] + [r[...] for r in x_refs[1:]], axis=0)
        ms = jnp.mean(h * h, axis=-1, keepdims=True)
        u_ref[...] = (h * lax.rsqrt(ms + EPS) * w_ref[...]).astype(BF16)

    return pl.pallas_call(
        body, name="prenorm_fwd", grid=(p // tm,),
        in_specs=[_full((CHUNK, d))] + subs + [_full((1, d))],
        out_specs=pl.BlockSpec((tm, d), lambda i: (i, 0)),
        out_shape=jax.ShapeDtypeStruct((p, d), BF16),
        compiler_params=_cparams(("arbitrary",)),
    )(head, *([x2] * len(subs)), w)


def _conv_pre(ext_ref, cw_ref, cb_ref):
    pre = cb_ref[...] + cw_ref[CONV_K - 1:CONV_K, :] * ext_ref[8:8 + CHUNK, :]
    for j in range(1, CONV_K):
        pre = pre + cw_ref[CONV_K - 1 - j:CONV_K - j, :] * ext_ref[8 - j:8 - j + CHUNK, :]
    return pre


def _ssd_scalars(dtf_ref, brow_ref, alog_ref, rowmask, hs, ha, tri):
    lane = lax.broadcasted_iota(jnp.int32, (1, LANES), 1)
    is_dt = lane < hs
    is_f = (lane >= hs) & (lane < hs + ha)
    dtr = dtf_ref[...] + brow_ref[...]
    sp = _softplus(dtr)
    dt = jnp.where(is_dt, sp, 0.0) * rowmask
    logf = jnp.where(is_f, jnp.minimum(dtr, 0.0) - jnp.log(1.0 + jnp.exp(-jnp.abs(dtr))), 0.0) * rowmask
    a_row = jnp.where(is_dt, -jnp.exp(alog_ref[...]), 0.0)
    run = _dot_tri(tri, dt * a_row + logf)
    return dtr, dt, a_row, run, is_dt, is_f


def _tri_mats():
    r = lax.broadcasted_iota(jnp.int32, (CHUNK, CHUNK), 0)
    c = lax.broadcasted_iota(jnp.int32, (CHUNK, CHUNK), 1)
    return r, c


def _ssd_fwd(xbc, z, dtf, conv_w, conv_b, brow, alog, dskip_l, ssd_norm, sel_t, hs, ha):
    p, cd = xbc.shape
    ds = z.shape[1]
    ns = (cd - ds) // (2 * SSD_GROUPS)
    gw = ds // SSD_GROUPS
    nch = p // CHUNK
    hpg = hs // SSD_GROUPS

    def body(xbc_ref, halo_ref, z_ref, dtf_ref, cw_ref, cb_ref, brow_ref, alog_ref, dsk_ref, nrm_ref, selt_ref,
             y_ref, yssd_ref, hin_ref, cf_ref, pre_ref, st_ref, carry_ref, yacc_ref, xc_s, ex_s, xdtb_s, xwb_s, ext_s):
        c = pl.program_id(0)

        @pl.when(c == 0)
        def _():
            st_ref[...] = jnp.zeros_like(st_ref)
            carry_ref[...] = jnp.zeros_like(carry_ref)

        rows = lax.broadcasted_iota(jnp.int32, (CHUNK, 1), 0)
        rowmask = jnp.where((rows >= PADN) | (c > 0), 1.0, 0.0)
        ri, ci = _tri_mats()
        causal = ri >= ci
        tri = jnp.where(causal, 1.0, 0.0).astype(BF16)

        ext_s[0:8, :] = halo_ref[...].astype(F32)[HALO - 8:, :] * jnp.where(c > 0, 1.0, 0.0)
        ext_s[8:, :] = xbc_ref[...].astype(F32)
        pre = _conv_pre(ext_s, cw_ref, cb_ref)
        pre_ref[...] = pre.astype(BF16)
        xc_s[...] = pre * _sigmoid(pre) * rowmask

        dtr, dt, a_row, run, is_dt, is_f = _ssd_scalars(dtf_ref, brow_ref, alog_ref, rowmask, hs, ha, tri)
        cf = run + carry_ref[...]
        cf_ref[...] = cf
        carry_ref[...] = jnp.where(is_f, cf[CHUNK - 1:CHUNK, :], 0.0)
        cs = jnp.where(is_dt, run, 0.0)
        cl = cs[CHUNK - 1:CHUNK, :]
        selt = selt_ref[...]
        ex_s[...] = _dot_sel(jnp.exp(cs), selt)
        cdec_x = _dot_sel(jnp.broadcast_to(jnp.exp(cl), (8, LANES)), selt)[0:1, :]
        cs_t = cs.T
        xdt = xc_s[:, :ds] * _dot_sel(dt, selt)
        xdtb_s[...] = xdt.astype(BF16)
        xwb_s[...] = (xdt * _dot_sel(jnp.exp(cl - cs), selt)).astype(BF16)

        lane = lax.broadcasted_iota(jnp.int32, (1, LANES), 1)
        half0 = lane < HEAD_DIM
        for g in range(SSD_GROUPS):
            bg = xc_s[:, ds + g * ns: ds + (g + 1) * ns].astype(BF16)
            cg = xc_s[:, ds + SSD_GROUPS * ns + g * ns: ds + SSD_GROUPS * ns + (g + 1) * ns].astype(BF16)
            gm = _dot(cg, bg, NT)
            gs = slice(g * gw, (g + 1) * gw)
            stg = st_ref[:, gs]
            stg_b = stg.astype(BF16)
            hin_ref[0, :, gs] = stg_b
            yoff = _dot(cg, stg_b) * ex_s[:, gs]
            for pr in range(gw // LANES):
                sl = slice(g * gw + pr * LANES, g * gw + (pr + 1) * LANES)
                xp = xdtb_s[:, sl]
                yd = jnp.zeros((CHUNK, LANES), F32)
                for j in range(2):
                    h = g * hpg + 2 * pr + j
                    seg = cs[:, h:h + 1] - cs_t[h:h + 1, :]
                    m = jnp.where(causal, gm * jnp.exp(jnp.minimum(seg, 0.0)), 0.0).astype(BF16)
                    sel = half0 if j == 0 else jnp.logical_not(half0)
                    yd = yd + _dot(m, jnp.where(sel, xp, jnp.zeros_like(xp)))
                yacc_ref[:, sl] = yd + yoff[:, pr * LANES:(pr + 1) * LANES] + dsk_ref[:, sl] * xc_s[:, sl]
            st_ref[:, gs] = stg * cdec_x[:, gs] + _dot(bg, xwb_s[:, gs], TN)

        y = yacc_ref[...]
        y_ref[...] = y.astype(BF16)
        zf = z_ref[...].astype(F32)
        u = y * zf * _sigmoid(zf)
        for g in range(SSD_GROUPS):
            gs = slice(g * gw, (g + 1) * gw)
            ug = u[:, gs]
            ms = jnp.mean(ug * ug, axis=-1, keepdims=True)
            yssd_ref[:, gs] = (ug * lax.rsqrt(ms + EPS) * nrm_ref[:, gs]).astype(BF16)

    rb = CHUNK // HALO
    return pl.pallas_call(
        body, name="ssd_fwd", grid=(nch,),
        in_specs=[pl.BlockSpec((CHUNK, cd), lambda c: (c, 0)),
                  pl.BlockSpec((HALO, cd), lambda c: (jnp.maximum(c * rb - 1, 0), 0)),
                  pl.BlockSpec((CHUNK, ds), lambda c: (c, 0)),
                  pl.BlockSpec((CHUNK, LANES), lambda c: (c, 0)),
                  _full((CONV_K, cd)), _full((1, cd)), _full((1, LANES)), _full((1, LANES)),
                  _full((1, ds)), _full((1, ds)), _full((LANES, ds))],
        out_specs=[pl.BlockSpec((CHUNK, ds), lambda c: (c, 0)), pl.BlockSpec((CHUNK, ds), lambda c: (c, 0)),
                   pl.BlockSpec((1, ns, ds), lambda c: (c, 0, 0)), pl.BlockSpec((CHUNK, LANES), lambda c: (c, 0)),
                   pl.BlockSpec((CHUNK, cd), lambda c: (c, 0))],
        out_shape=[jax.ShapeDtypeStruct((p, ds), BF16), jax.ShapeDtypeStruct((p, ds), BF16),
                   jax.ShapeDtypeStruct((nch, ns, ds), BF16), jax.ShapeDtypeStruct((p, LANES), F32),
                   jax.ShapeDtypeStruct((p, cd), BF16)],
        scratch_shapes=[pltpu.VMEM((ns, ds), F32), pltpu.VMEM((1, LANES), F32), pltpu.VMEM((CHUNK, ds), F32),
                        pltpu.VMEM((CHUNK, cd), F32), pltpu.VMEM((CHUNK, ds), F32),
                        pltpu.VMEM((CHUNK, ds), BF16), pltpu.VMEM((CHUNK, ds), BF16),
                        pltpu.VMEM((8 + CHUNK, cd), F32)],
        compiler_params=_cparams(("arbitrary",)),
    )(xbc, xbc, z, dtf, conv_w, conv_b, brow, alog, dskip_l, ssd_norm, sel_t)


def _ssd_bwd(dyssd, y, z, xbc, pre, dtf, hin, dcf, conv_w, brow, alog, dskip_l, ssd_norm, sel_t, sel, hs, ha):
    p, cd = xbc.shape
    ds = z.shape[1]
    ns = (cd - ds) // (2 * SSD_GROUPS)
    gw = ds // SSD_GROUPS
    nch = p // CHUNK
    hpg = hs // SSD_GROUPS

    def body(dyssd_ref, y_ref, z_ref, xbc_ref, pre_ref, dtf_ref, hin_ref, dcf_ref, cw_ref, brow_ref,
             alog_ref, dsk_ref, nrm_ref, selt_ref, sel_ref,
             dxbc_ref, dz_ref, ddtf_ref, gcw_ref, gcb_ref, gnrm_ref, gsm_ref,
             dst_ref, nxt_ref, fcar_ref, gdsk_ref, dxc_ref, xc_s, dsl_s, dtx_s, ex_s, wx_s, dy_s, xdtb_s, xwb_s,
             dyb_s, dyeb_s):
        step = pl.program_id(0)
        c = nch - 1 - step

        @pl.when(step == 0)
        def _():
            dst_ref[...] = jnp.zeros_like(dst_ref)
            nxt_ref[...] = jnp.zeros_like(nxt_ref)
            fcar_ref[...] = jnp.zeros_like(fcar_ref)
            gdsk_ref[...] = jnp.zeros_like(gdsk_ref)
            gcw_ref[...] = jnp.zeros_like(gcw_ref)
            gcb_ref[...] = jnp.zeros_like(gcb_ref)
            gnrm_ref[...] = jnp.zeros_like(gnrm_ref)
            gsm_ref[...] = jnp.zeros_like(gsm_ref)

        rows = lax.broadcasted_iota(jnp.int32, (CHUNK, 1), 0)
        rowmask = jnp.where((rows >= PADN) | (c > 0), 1.0, 0.0)
        ri, ci = _tri_mats()
        causal = ri >= ci
        anti = ci >= ri
        tri = jnp.where(causal, 1.0, 0.0).astype(BF16)
        rtri = jnp.where(anti, 1.0, 0.0).astype(BF16)

        pre = pre_ref[...].astype(F32)
        sg = _sigmoid(pre)
        xc_s[...] = pre * sg * rowmask
        dsl_s[...] = sg * (1.0 + pre * (1.0 - sg)) * rowmask

        dtr, dt, a_row, run, is_dt, is_f = _ssd_scalars(dtf_ref, brow_ref, alog_ref, rowmask, hs, ha, tri)
        cs = jnp.where(is_dt, run, 0.0)
        cl = cs[CHUNK - 1:CHUNK, :]
        selt = selt_ref[...]
        selm = sel_ref[...]
        dtx_s[...] = _dot_sel(dt, selt)
        ex_s[...] = _dot_sel(jnp.exp(cs), selt)
        wx_s[...] = _dot_sel(jnp.exp(cl - cs), selt)
        cdec = jnp.exp(cl)
        cdec_x = _dot_sel(jnp.broadcast_to(cdec, (8, LANES)), selt)[0:1, :]
        cs_t = cs.T
        xdt = xc_s[:, :ds] * dtx_s[...]
        xdtb_s[...] = xdt.astype(BF16)
        xwb_s[...] = (xdt * wx_s[...]).astype(BF16)

        yv = y_ref[...].astype(F32)
        zf = z_ref[...].astype(F32)
        sz = _sigmoid(zf)
        u = yv * zf * sz
        dyo = dyssd_ref[...].astype(F32)
        du_parts = []
        for g in range(SSD_GROUPS):
            gs = slice(g * gw, (g + 1) * gw)
            ug = u[:, gs]
            rstd = lax.rsqrt(jnp.mean(ug * ug, axis=-1, keepdims=True) + EPS)
            yhat = ug * rstd
            dyg = dyo[:, gs]
            gnrm_ref[0:1, gs] += jnp.sum(dyg * yhat, axis=0, keepdims=True)
            dyh = dyg * nrm_ref[:, gs]
            du_parts.append(rstd * (dyh - yhat * jnp.mean(dyh * yhat, axis=-1, keepdims=True)))
        du = jnp.concatenate(du_parts, axis=1)
        dy = du * zf * sz
        dz_ref[...] = (du * yv * sz * (1.0 + zf * (1.0 - sz))).astype(BF16)
        dy_s[...] = dy
        dyb_s[...] = dy.astype(BF16)
        dyeb_s[...] = (dy * ex_s[...]).astype(BF16)
        gdsk_ref[...] += jnp.sum(dy * xc_s[:, :ds], axis=0, keepdims=True)
        lane = lax.broadcasted_iota(jnp.int32, (1, LANES), 1)
        half0 = lane < HEAD_DIM
        x_parts, yo_parts, t4_parts = [], [], []
        dcs = jnp.zeros((CHUNK, LANES), F32)
        for g in range(SSD_GROUPS):
            gs = slice(g * gw, (g + 1) * gw)
            bsl = slice(ds + g * ns, ds + (g + 1) * ns)
            csl = slice(ds + SSD_GROUPS * ns + g * ns, ds + SSD_GROUPS * ns + (g + 1) * ns)
            bg = xc_s[:, bsl].astype(BF16)
            cg = xc_s[:, csl].astype(BF16)
            gm = _dot(cg, bg, NT)
            gm_t = _dot(bg, cg, NT)
            stg_b = hin_ref[0, :, gs]
            dstg = dst_ref[:, gs]
            dstg_b = dstg.astype(BF16)
            t4_parts.append(jnp.sum(dstg * stg_b.astype(F32), axis=0, keepdims=True))
            zst = _dot(bg, dstg_b) * wx_s[:, gs]
            x_parts.append(xc_s[:, gs] * dtx_s[:, gs] * zst)
            yo_parts.append(dy_s[:, gs] * (_dot(cg, stg_b) * ex_s[:, gs]))
            dgsum = jnp.zeros((CHUNK, CHUNK), F32)
            dgtsum = jnp.zeros((CHUNK, CHUNK), F32)
            for pr in range(gw // LANES):
                sl = slice(g * gw + pr * LANES, g * gw + (pr + 1) * LANES)
                xp = xdtb_s[:, sl]
                dyp = dyb_s[:, sl]
                dxd = zst[:, pr * LANES:(pr + 1) * LANES]
                for j in range(2):
                    h = g * hpg + 2 * pr + j
                    sel_l = half0 if j == 0 else jnp.logical_not(half0)
                    seg = cs[:, h:h + 1] - cs_t[h:h + 1, :]
                    lm = jnp.where(causal, jnp.exp(jnp.minimum(seg, 0.0)), 0.0)
                    lmt = lm.T
                    dyp_m = jnp.where(sel_l, dyp, jnp.zeros_like(dyp))
                    xp_m = jnp.where(sel_l, xp, jnp.zeros_like(xp))
                    dxd = dxd + _dot((gm_t * lmt).astype(BF16), dyp_m)
                    dg = _dot(dyp_m, xp, NT) * lm
                    dgt = _dot(xp_m, dyp, NT) * lmt
                    dgsum = dgsum + dg
                    dgtsum = dgtsum + dgt
                    qrow = (jnp.sum(dg * gm, axis=1, keepdims=True) - jnp.sum(dgt * gm_t, axis=1, keepdims=True))
                    dcs = dcs + jnp.where(lane == h, qrow, 0.0)
                dxc_ref[:, sl] = dxd
            dxc_ref[:, csl] = _dot(dgsum.astype(BF16), bg) + _dot(dyeb_s[:, gs], stg_b, NT)
            dxc_ref[:, bsl] = _dot(dgtsum.astype(BF16), cg) + _dot(xwb_s[:, gs], dstg_b, NT)
            dst_ref[:, gs] = dstg * cdec_x[:, gs] + _dot(cg, dyeb_s[:, gs], TN)

        dxdt = dxc_ref[:, :ds]
        xst = _dot_sel(jnp.concatenate(x_parts, axis=1), selm)
        yo = _dot_sel(jnp.concatenate(yo_parts, axis=1), selm)
        t4 = _dot_sel(jnp.concatenate([jnp.concatenate(t4_parts, axis=1), jnp.zeros((7, ds), F32)], axis=0), selm)
        dcl = jnp.sum(xst, axis=0, keepdims=True) + cdec * t4[0:1, :]
        dcs = dcs + yo - xst + jnp.where(rows == CHUNK - 1, dcl, 0.0)
        da_ = _dot_tri(rtri, dcs)
        ddt = _dot_sel(dxdt * xc_s[:, :ds], selm) + da_ * a_row
        dcf_blk = dcf_ref[...]
        dlogf = _dot_tri(rtri, dcf_blk) + fcar_ref[...]
        fcar_ref[...] += jnp.sum(dcf_blk, axis=0, keepdims=True)
        sgd = _sigmoid(dtr)
        ddtf = (jnp.where(is_dt, ddt * sgd, 0.0) + jnp.where(is_f, dlogf * (1.0 - sgd), 0.0)) * rowmask
        ddtf_ref[...] = ddtf
        gsm_ref[0:1, :] += jnp.sum(ddtf, axis=0, keepdims=True)
        gsm_ref[1:2, :] += jnp.sum(da_ * dt, axis=0, keepdims=True) * a_row

        dxc_ref[:, :ds] = dxdt * dtx_s[...] + dsk_ref[...] * dy_s[...]
        dpre = dxc_ref[...] * dsl_s[...]
        nxt_ref[0:CHUNK, :] = dpre
        gcb_ref[0:1, :] += jnp.sum(dpre, axis=0, keepdims=True)
        xr = xbc_ref[...].astype(F32)
        gcw_ref[CONV_K - 1:CONV_K, :] += jnp.sum(dpre * xr, axis=0, keepdims=True)
        dxr = cw_ref[CONV_K - 1:CONV_K, :] * dpre
        for j in range(1, CONV_K):
            up = nxt_ref[j:j + CHUNK, :]
            gcw_ref[CONV_K - 1 - j:CONV_K - j, :] += jnp.sum(up * xr, axis=0, keepdims=True)
            dxr = dxr + cw_ref[CONV_K - 1 - j:CONV_K - j, :] * up
        nxt_ref[CHUNK:, :] = dpre[0:8, :]
        dxbc_ref[...] = dxr.astype(BF16)

        @pl.when(step == nch - 1)
        def _():
            gsm_ref[2:3, :] = _dot_sel(jnp.broadcast_to(gdsk_ref[...], (8, ds)), selm)[0:1, :]

    rev = lambda s: nch - 1 - s
    blk = lambda w: pl.BlockSpec((CHUNK, w), lambda s: (rev(s), 0))
    return pl.pallas_call(
        body, name="ssd_bwd", grid=(nch,),
        in_specs=[blk(ds), blk(ds), blk(ds), blk(cd), blk(cd),
                  blk(LANES), pl.BlockSpec((1, ns, ds), lambda s: (rev(s), 0, 0)), blk(LANES),
                  _full((CONV_K, cd)), _full((1, LANES)), _full((1, LANES)),
                  _full((1, ds)), _full((1, ds)), _full((LANES, ds)), _full((ds, LANES))],
        out_specs=[blk(cd), blk(ds), blk(LANES), _full((8, cd)), _full((8, cd)), _full((8, ds)), _full((8, LANES))],
        out_shape=[jax.ShapeDtypeStruct((p, cd), BF16), jax.ShapeDtypeStruct((p, ds), BF16),
                   jax.ShapeDtypeStruct((p, LANES), F32), jax.ShapeDtypeStruct((8, cd), F32),
                   jax.ShapeDtypeStruct((8, cd), F32), jax.ShapeDtypeStruct((8, ds), F32),
                   jax.ShapeDtypeStruct((8, LANES), F32)],
        scratch_shapes=[pltpu.VMEM((ns, ds), F32), pltpu.VMEM((CHUNK + 8, cd), F32), pltpu.VMEM((1, LANES), F32),
                        pltpu.VMEM((1, ds), F32), pltpu.VMEM((CHUNK, cd), F32),
                        pltpu.VMEM((CHUNK, cd), F32), pltpu.VMEM((CHUNK, cd), F32),
                        pltpu.VMEM((CHUNK, ds), F32), pltpu.VMEM((CHUNK, ds), F32), pltpu.VMEM((CHUNK, ds), F32),
                        pltpu.VMEM((CHUNK, ds), F32), pltpu.VMEM((CHUNK, ds), BF16), pltpu.VMEM((CHUNK, ds), BF16),
                        pltpu.VMEM((CHUNK, ds), BF16), pltpu.VMEM((CHUNK, ds), BF16)],
        compiler_params=_cparams(("arbitrary",)),
    )(dyssd, y, z, xbc, pre, dtf, hin, dcf, conv_w, brow, alog, dskip_l, ssd_norm, sel_t, sel)


def _attn_fwd(q, k, v, ck, blk):
    p, da = q.shape
    npair, nkb = ck.shape[0], ck.shape[1]
    scale = 1.0 / math.sqrt(HEAD_DIM)

    def body(q_ref, k_ref, v_ref, ck_ref, o_ref, lse_ref):
        i = pl.program_id(1)
        lane = lax.broadcasted_iota(jnp.int32, (1, LANES), 1)
        sels = [lane < HEAD_DIM, lane >= HEAD_DIM]
        ones = [jnp.where(lane == HEAD_DIM, 1.0, 0.0).astype(BF16), jnp.where(lane == 0, 1.0, 0.0).astype(BF16)]
        qb = q_ref[...] * scale
        cmask = (lax.broadcasted_iota(jnp.int32, (blk, blk), 1) <= lax.broadcasted_iota(jnp.int32, (blk, blk), 0))

        def step(kb, carry, masked, nk=1):
            r0 = pl.multiple_of(kb * blk, blk)
            ks = k_ref[pl.ds(r0, nk * blk), :]
            vs = v_ref[pl.ds(r0, nk * blk), :]
            kk = jnp.concatenate([jnp.where(sel, ks, jnp.zeros_like(ks)) for sel in sels], axis=0)
            s_both = _dot(qb, kk, NT)
            out = []
            for j in range(2):
                m, acc = carry[2 * j], carry[2 * j + 1]
                ckr = jnp.concatenate([ck_ref[0, kb + t, j:j + 1, :] for t in range(nk)], axis=1)
                s = s_both[:, j * nk * blk:(j + 1) * nk * blk] - ckr
                if masked:
                    s = jnp.where(cmask, s, NEG)
                mn = jnp.maximum(m, jnp.max(s, axis=-1, keepdims=True))
                pr = jnp.exp(s - mn).astype(BF16)
                acc = jnp.exp(m - mn) * acc + _dot(pr, jnp.where(sels[j], vs, ones[j]))
                out += [mn, acc]
            return tuple(out)

        init = (jnp.full((blk, 1), NEG, F32), jnp.zeros((blk, LANES), F32)) * 2
        n4 = i // 4
        n2 = (i - 4 * n4) // 2
        carry = lax.fori_loop(0, n4, lambda t, c: step(4 * t, c, False, 4), init)
        carry = lax.fori_loop(0, n2, lambda t, c: step(4 * n4 + 2 * t, c, False, 2), carry)
        carry = lax.fori_loop(4 * n4 + 2 * n2, i, lambda kb, c: step(kb, c, False), carry)
        m0, a0, m1, a1 = step(i, carry, True)
        l0 = a0[:, HEAD_DIM:HEAD_DIM + 1]
        l1 = a1[:, 0:1]
        o_ref[...] = jnp.where(sels[0], a0 / l0, a1 / l1).astype(BF16)
        lse_ref[...] = jnp.where(sels[0], m0 + jnp.log(l0), m1 + jnp.log(l1))

    return pl.pallas_call(
        body, name="attn_fwd", grid=(npair, p // blk),
        in_specs=[pl.BlockSpec((blk, LANES), lambda h, i: (i, h)),
                  pl.BlockSpec((p, LANES), lambda h, i: (0, h)), pl.BlockSpec((p, LANES), lambda h, i: (0, h)),
                  pl.BlockSpec((1, nkb, 8, blk), lambda h, i: (h, 0, 0, 0))],
        out_specs=[pl.BlockSpec((blk, LANES), lambda h, i: (i, h)), pl.BlockSpec((blk, LANES), lambda h, i: (i, h))],
        out_shape=[jax.ShapeDtypeStruct((p, da), BF16), jax.ShapeDtypeStruct((p, da), F32)],
        compiler_params=_cparams(("parallel", "arbitrary")),
    )(q, k, v, ck)


def _attn_bwd(q, k, v, o, do, lse_rep, ck, blk):
    p, da = q.shape
    npair, nkb = ck.shape[0], ck.shape[1]
    nq = p // blk
    scale = 1.0 / math.sqrt(HEAD_DIM)

    def body(k_ref, v_ref, q_ref, do_ref, o_ref, lse_ref, ck_ref, dk_ref, dv_ref, dq_ref, dcs_ref, rsum_ref, dq_acc):
        jb = pl.program_id(1)

        @pl.when(jb == 0)
        def _():
            dq_acc[...] = jnp.zeros_like(dq_acc)

        ks = k_ref[...]
        vs = v_ref[...]
        lane = lax.broadcasted_iota(jnp.int32, (1, LANES), 1)
        sels = [lane < HEAD_DIM, lane >= HEAD_DIM]
        ones = [jnp.where(lane == HEAD_DIM, 1.0, 0.0).astype(BF16), jnp.where(lane == 0, 1.0, 0.0).astype(BF16)]
        kss = ks * scale
        kmo = [jnp.where(sels[j], kss, ones[j]) for j in range(2)]
        cmask = (lax.broadcasted_iota(jnp.int32, (blk, blk), 1) <= lax.broadcasted_iota(jnp.int32, (blk, blk), 0))

        def step(ib, carry, masked, nb=1):
            rows = nb * blk
            r0 = pl.multiple_of(ib * blk, blk)
            qb = q_ref[pl.ds(r0, rows), :] * scale
            dob = do_ref[pl.ds(r0, rows), :]
            prod = dob.astype(F32) * o_ref[pl.ds(r0, rows), :].astype(F32)
            out = []
            for j in range(2):
                dk, dv = carry[2 * j], carry[2 * j + 1]
                qm = jnp.where(sels[j], qb, jnp.zeros_like(qb))
                dom = jnp.where(sels[j], dob, jnp.zeros_like(dob))
                lse = lse_ref[pl.ds(r0, rows), HEAD_DIM * j:HEAD_DIM * j + 1]
                dlt = jnp.sum(jnp.where(sels[j], prod, 0.0), axis=-1, keepdims=True)
                s = _dot(qm, ks, NT) - ck_ref[0, 0, j:j + 1, :] - lse
                pm = jnp.exp(jnp.minimum(s, 0.0))
                if masked:
                    pm = jnp.where(cmask, pm, 0.0)
                ds_b = (pm * (_dot(dom, vs, NT) - dlt)).astype(BF16)
                dv = dv + _dot(pm.astype(BF16), dom, TN)
                dk = dk + _dot(ds_b, jnp.where(sels[j], qb, ones[j]), TN)
                dq_acc[pl.ds(r0, rows), LANES * j:LANES * (j + 1)] += _dot(ds_b, kmo[j])
                out += [dk, dv]
            return tuple(out)

        zero = jnp.zeros((blk, LANES), F32)
        carry = step(jb, (zero, zero, zero, zero), True)
        n4 = (nq - 1 - jb) // 4
        n2 = (nq - 1 - jb - 4 * n4) // 2
        carry = lax.fori_loop(0, n4, lambda t, c: step(jb + 1 + 4 * t, c, False, 4), carry)
        carry = lax.fori_loop(0, n2, lambda t, c: step(jb + 1 + 4 * n4 + 2 * t, c, False, 2), carry)
        dk0, dv0, dk1, dv1 = lax.fori_loop(jb + 1 + 4 * n4 + 2 * n2, nq, lambda ib, c: step(ib, c, False), carry)
        dk_ref[...] = jnp.where(sels[0], dk0, dk1).astype(BF16)
        dv_ref[...] = (dv0 + dv1).astype(BF16)
        pair8 = lambda c0, c1: jnp.where(lane == 0, c0, jnp.where(lane == 1, c1, 0.0)).T[0:8]
        dcs_ref[0] = pair8(dk0[:, HEAD_DIM:HEAD_DIM + 1], dk1[:, 0:1])

        @pl.when(jb == nkb - 1)
        def _():
            a0 = dq_acc[:, :LANES]
            a1 = dq_acc[:, LANES:]
            dq_ref[...] = jnp.where(sels[0], a0, a1).astype(BF16)
            rsum_ref[0] = pair8(a0[:, HEAD_DIM:HEAD_DIM + 1], a1[:, 0:1])

    colblk = pl.BlockSpec((blk, LANES), lambda h, j: (j, h))
    colfull = pl.BlockSpec((p, LANES), lambda h, j: (0, h))
    ckspec = pl.BlockSpec((1, 1, 8, blk), lambda h, j: (h, j, 0, 0))
    return pl.pallas_call(
        body, name="attn_bwd", grid=(npair, nkb),
        in_specs=[colblk, colblk, colfull, colfull, colfull, colfull, ckspec],
        out_specs=[colblk, colblk, colfull, pl.BlockSpec((1, 8, blk), lambda h, j: (h, 0, j)),
                   pl.BlockSpec((1, 8, p), lambda h, j: (h, 0, 0))],
        out_shape=[jax.ShapeDtypeStruct((p, da), BF16), jax.ShapeDtypeStruct((p, da), BF16),
                   jax.ShapeDtypeStruct((p, da), BF16), jax.ShapeDtypeStruct((npair, 8, p), F32),
                   jax.ShapeDtypeStruct((npair, 8, p), F32)],
        scratch_shapes=[pltpu.VMEM((p, 2 * LANES), F32)],
        compiler_params=_cparams(("parallel", "arbitrary")),
    )(k, v, q, do, o, lse_rep, ck)


def _tail_fwd(yssd, o, zatt, graw, head, x2, tgt2, wps, wpa, wout, gate_bias, norm_post, tm):
    p, ds = yssd.shape
    da = o.shape[1]
    d = x2.shape[1]
    nsub = tm // CHUNK

    def body(yssd_ref, o_ref, zatt_ref, g_ref, head_ref, *rest):
        x_refs, t_refs = rest[:nsub], rest[nsub:2 * nsub]
        (wps_ref, wpa_ref, wout_ref, gb_ref, np_ref,
         yatt_ref, mrg_ref, a_ref, b_ref, dzo_ref, dout_ref, red_ref) = rest[2 * nsub:]
        i = pl.program_id(0)

        @pl.when(i == 0)
        def _():
            red_ref[...] = jnp.zeros_like(red_ref)

        first = jnp.where(i == 0, head_ref[...], x_refs[0][...])
        h = jnp.concatenate([first] + [r[...] for r in x_refs[1:]], axis=0)
        tgt = jnp.concatenate([r[...] for r in t_refs], axis=0)
        rows = lax.broadcasted_iota(jnp.int32, (tm, 1), 0)
        valid = jnp.where((i > 0) | (rows >= CHUNK), 1.0, 0.0)
        ob = o_ref[...].astype(F32)
        za = zatt_ref[...].astype(F32)
        yatt_b = (ob * za * _sigmoid(za)).astype(BF16)
        yatt_ref[...] = yatt_b
        a = _dot(yssd_ref[...], wps_ref[...])
        b = _dot(yatt_b, wpa_ref[...])
        a_ref[...] = a.astype(BF16)
        b_ref[...] = b.astype(BF16)
        gr = g_ref[...].astype(F32) + gb_ref[...]
        mrg_b = (_sigmoid(gr[:, :d]) * a + _sigmoid(gr[:, d:]) * b).astype(BF16)
        mrg_ref[...] = mrg_b
        zo = _dot(mrg_b, wout_ref[...])
        rstd = lax.rsqrt(jnp.mean(zo * zo, axis=-1, keepdims=True) + EPS)
        zh = zo * rstd
        npw = np_ref[...]
        err = (h + zh * npw - tgt) * valid
        dout = err * (1.0 / d)
        dout_ref[...] = dout
        dzh = dout * npw
        dzo_ref[...] = (rstd * (dzh - zh * jnp.mean(dzh * zh, axis=-1, keepdims=True))).astype(BF16)
        red_ref[0:1, :] += jnp.sum(dout * zh, axis=0, keepdims=True)
        red_ref[1:2, 0:1] += jnp.sum(jnp.sum(err * err, axis=1, keepdims=True), axis=0, keepdims=True) * (0.5 / d)

    row = lambda w: pl.BlockSpec((tm, w), lambda i: (i, 0))
    once = lambda shape: pl.BlockSpec(shape, lambda i: (0,) * len(shape), pipeline_mode=pl.Buffered(1))
    subs = _x_row_specs(tm, d)
    sd = jax.ShapeDtypeStruct
    return pl.pallas_call(
        body, name="tail_fwd", grid=(p // tm,),
        in_specs=[row(ds), row(da), row(da), row(2 * d), _full((CHUNK, d))] + subs + subs
                 + [once((ds, d)), once((da, d)), once((d, d)), _full((1, 2 * d)), _full((1, d))],
        out_specs=[row(da), row(d), row(d), row(d), row(d), row(d), _full((8, d))],
        out_shape=[sd((p, da), BF16), sd((p, d), BF16), sd((p, d), BF16), sd((p, d), BF16), sd((p, d), BF16),
                   sd((p, d), F32), sd((8, d), F32)],
        compiler_params=_cparams(("arbitrary",)),
    )(yssd, o, zatt, graw, head, *([x2] * nsub), *([tgt2] * nsub), wps, wpa, wout, gate_bias, norm_post)


def _tail_bwd(dzo, a_b, b_b, graw, o, zatt, wps, wpa, wout, gate_bias, tm):
    p, d = dzo.shape
    ds, da = wps.shape[0], wpa.shape[0]

    def body(dzo_ref, a_ref, b_ref, g_ref, o_ref, zatt_ref, wps_ref, wpa_ref, wout_ref, gb_ref,
             da_ref, db_ref, dg_ref, dyssd_ref, do_ref, dzatt_ref, red_ref):
        i = pl.program_id(0)

        @pl.when(i == 0)
        def _():
            red_ref[...] = jnp.zeros_like(red_ref)

        gr = g_ref[...].astype(F32) + gb_ref[...]
        gs = _sigmoid(gr[:, :d])
        ga = _sigmoid(gr[:, d:])
        dm = _dot(dzo_ref[...], wout_ref[...], NT)
        da_b = (gs * dm).astype(BF16)
        db_b = (ga * dm).astype(BF16)
        da_ref[...] = da_b
        db_ref[...] = db_b
        dgs = dm * a_ref[...].astype(F32) * gs * (1.0 - gs)
        dga = dm * b_ref[...].astype(F32) * ga * (1.0 - ga)
        dg_ref[:, :d] = dgs.astype(BF16)
        dg_ref[:, d:] = dga.astype(BF16)
        red_ref[0:1, :d] += jnp.sum(dgs, axis=0, keepdims=True)
        red_ref[0:1, d:] += jnp.sum(dga, axis=0, keepdims=True)
        dyssd_ref[...] = _dot(da_b, wps_ref[...], NT).astype(BF16)
        dya = _dot(db_b, wpa_ref[...], NT)
        ob = o_ref[...].astype(F32)
        za = zatt_ref[...].astype(F32)
        sza = _sigmoid(za)
        do_ref[...] = (dya * za * sza).astype(BF16)
        dzatt_ref[...] = (dya * ob * sza * (1.0 + za * (1.0 - sza))).astype(BF16)

    row = lambda w: pl.BlockSpec((tm, w), lambda i: (i, 0))
    once = lambda shape: pl.BlockSpec(shape, lambda i: (0,) * len(shape), pipeline_mode=pl.Buffered(1))
    sd = jax.ShapeDtypeStruct
    return pl.pallas_call(
        body, name="tail_bwd", grid=(p // tm,),
        in_specs=[row(d), row(d), row(d), row(2 * d), row(da), row(da),
                  once((ds, d)), once((da, d)), once((d, d)), _full((1, 2 * d))],
        out_specs=[row(d), row(d), row(2 * d), row(ds), row(da), row(da), _full((8, 2 * d))],
        out_shape=[sd((p, d), BF16), sd((p, d), BF16), sd((p, 2 * d), BF16), sd((p, ds), BF16), sd((p, da), BF16),
                   sd((p, da), BF16), sd((8, 2 * d), F32)],
        compiler_params=_cparams(("arbitrary",)),
    )(dzo, a_b, b_b, graw, o, zatt, wps, wpa, wout, gate_bias)


def _adamw_math(w, g, m, v):
    m2 = ADAM_B1 * m + (1.0 - ADAM_B1) * g
    v2 = ADAM_B2 * v + (1.0 - ADAM_B2) * (g * g)
    m_hat = m2 / (1.0 - ADAM_B1 ** ADAM_STEP)
    v_hat = v2 / (1.0 - ADAM_B2 ** ADAM_STEP)
    delta = -ADAM_LR * (m_hat / (jnp.sqrt(v_hat) + ADAM_EPS) + ADAM_WD * w)
    return delta, m2, v2


def _adamw_small(params, red, name):
    names = list(params)
    n = len(names)
    extra = [params[k][3] for k in names if not isinstance(params[k][3], tuple)]

    def body(*refs):
        w_refs, m_refs, v_refs = refs[:n], refs[n:2 * n], refs[2 * n:3 * n]
        red_ref = refs[3 * n]
        g_refs = iter(refs[3 * n + 1:3 * n + 1 + len(extra)])
        outs = refs[3 * n + 1 + len(extra):]
        for i, k in enumerate(names):
            where = params[k][3]
            rows, cols = w_refs[i].shape
            if isinstance(where, tuple):
                g = red_ref[where[0]:where[0] + rows, where[1]:where[1] + cols]
            else:
                g = next(g_refs)[...]
            delta, m2, v2 = _adamw_math(w_refs[i][...], g, m_refs[i][...], v_refs[i][...])
            for o, val in zip(outs[4 * i:4 * i + 4], (g, delta, m2, v2)):
                o[...] = val

    vm = pl.BlockSpec(memory_space=pltpu.VMEM)
    ws, ms, vs = ([params[k][j] for k in names] for j in range(3))
    out = pl.pallas_call(
        body, name=name,
        out_shape=[jax.ShapeDtypeStruct(w.shape, F32) for w in ws for _ in range(4)],
        in_specs=[vm] * (3 * n + 1 + len(extra)), out_specs=[vm] * (4 * n),
    )(*ws, *ms, *vs, red, *extra)
    return {k: tuple(out[4 * i:4 * i + 4]) for i, k in enumerate(names)}


def _adamw(w, g, m, v, name, parts=False, part_row0=0):
    r, cdim = w.shape
    tr, tc, by_rows = _tiles_2d(r, cdim)
    pick = (lambda i: (i, 0)) if by_rows else (lambda i: (0, i))
    assert part_row0 % tr == 0
    gpick = (lambda i: (i + part_row0 // tr, 0)) if by_rows else (lambda i: (part_row0 // tr, i))

    def body(w_ref, g_ref, m_ref, v_ref, go_ref, d_ref, mo_ref, vo_ref):
        if parts:
            g = g_ref[0].astype(F32)
            for s in range(1, g_ref.shape[0]):
                g = g + g_ref[s].astype(F32)
        else:
            g = g_ref[...]
        delta, m2, v2 = _adamw_math(w_ref[...], g, m_ref[...], v_ref[...])
        go_ref[...] = g
        d_ref[...] = delta
        mo_ref[...] = m2
        vo_ref[...] = v2

    blk = pl.BlockSpec((tr, tc), pick)
    gspec = pl.BlockSpec((g.shape[0], tr, tc), lambda i: (0,) + gpick(i)) if parts else blk
    return pl.pallas_call(
        body, name=name, grid=((r // tr) * (cdim // tc),),
        in_specs=[blk, gspec, blk, blk], out_specs=[blk] * 4,
        out_shape=[jax.ShapeDtypeStruct((r, cdim), F32)] * 4,
        compiler_params=_cparams(("parallel",)),
    )(w, g, m, v)


def _pad_cols(a, width):
    return jnp.pad(a, ((0, 0), (0, width - a.shape[1])))


def _pack_small_shard(conv_w_sh, meta_sh, width):
    return jnp.concatenate([_pad_cols(conv_w_sh, width), jnp.zeros((4, width), F32), _pad_cols(meta_sh, width)], axis=0)


def _pack_small_rep(norm_pre, norm_post, gate_bias, ssd_norm, conv_b, misc, width):
    rows = [norm_pre, norm_post, gate_bias, ssd_norm, conv_b, misc]
    return jnp.concatenate([_pad_cols(r, width) for r in rows] + [jnp.zeros((2, width), F32)], axis=0)


def kernel(x, meta_tokens, norm_pre, w_in, conv_w, conv_b, dt_bias, a_log, d_skip, ssd_norm, fgate_bias, gate_bias, w_proj_ssd, w_proj_att, w_out, norm_post, loss_target, m_meta_tokens, m_norm_pre, m_w_in, m_conv_w, m_conv_b, m_dt_bias, m_a_log, m_d_skip, m_ssd_norm, m_fgate_bias, m_gate_bias, m_w_proj_ssd, m_w_proj_att, m_w_out, m_norm_post, v_meta_tokens, v_norm_pre, v_w_in, v_conv_w, v_conv_b, v_dt_bias, v_a_log, v_d_skip, v_ssd_norm, v_fgate_bias, v_gate_bias, v_w_proj_ssd, v_w_proj_att, v_w_out, v_norm_post):
    seq, d = x.shape[1], x.shape[2]
    p = seq + CHUNK
    hs, ha = dt_bias.shape[1], fgate_bias.shape[1]
    ds, cd = ssd_norm.shape[1], conv_b.shape[1]
    da = ha * HEAD_DIM
    nc8 = w_in.shape[2]
    cws = cd // N_DEV
    msh = d // N_DEV
    r1, r2, r3 = ds // N_DEV, da // N_DEV, d // N_DEV
    me = _dev_index(*_my_pos())
    x2, tgt2 = x[0], loss_target[0]

    win_sh = jnp.transpose(w_in[0]).astype(BF16)
    rows_sh = jnp.concatenate([w_proj_ssd[0], w_proj_att[0], w_out[0]], axis=0).astype(BF16)
    small_sh = _pack_small_shard(conv_w[0], meta_tokens, cws)
    win_all, small_all = _all_gather([win_sh, small_sh], "gather_weights")
    rows_sh, win_all = lax.optimization_barrier((rows_sh, win_all))
    rows_sems, rows_thru, rows_land, rows_token = _bcast_start(rows_sh, "gather_rows_start")
    cuts = [0, ds, ds + cd, ds + cd + hs, ds + cd + hs + da, ds + cd + hs + 2 * da, ds + cd + hs + 3 * da,
            ds + cd + hs + 4 * da, ds + cd + hs + 4 * da + ha, ds + cd + hs + 4 * da + ha + 2 * d]

    def piece_rows(r0, r1):
        parts = [win_all[s, max(r0, s * nc8) - s * nc8:min(r1, (s + 1) * nc8) - s * nc8]
                 for s in range(N_DEV) if max(r0, s * nc8) < min(r1, (s + 1) * nc8)]
        return parts[0] if len(parts) == 1 else jnp.concatenate(parts, axis=0)

    w_z, w_xbc, w_dt, w_zatt, w_q, w_k, w_v, w_f, w_g = [piece_rows(cuts[i], cuts[i + 1]) for i in range(9)]
    w_dtf = jnp.concatenate([w_dt, w_f, jnp.zeros((LANES - hs - ha, d), BF16)], axis=0)
    conv_w_full = jnp.transpose(small_all[:, 0:CONV_K, :], (1, 0, 2)).reshape(CONV_K, cd)
    meta_full = jnp.transpose(small_all[:, 8:8 + N_META, :msh], (1, 0, 2)).reshape(N_META, d)
    head = jnp.concatenate([jnp.zeros((PADN, d), F32), meta_full + rows_token[0:1, 0:1]], axis=0)

    tm = _att_block(p)
    u = _prenorm_fwd(head, x2, norm_pre, tm)
    seg_w = [w_z, w_xbc, w_zatt, w_q, w_k, w_v, w_g]
    zs, xbc, zatt, q, k, v, graw = [
        _mm(u, w, "nt", BF16, _tile(p, (1408, tm)), _tile(w.shape[0], (1024, 512, 256, 128)), "inproj_%d" % i)
        for i, w in enumerate(seg_w)]
    dtf = _mm(u, w_dtf, "nt", F32, _tile(p, (1408, tm)), LANES, "inproj_dtf")

    brow = jnp.concatenate([dt_bias, fgate_bias, jnp.zeros((1, LANES - hs - ha), F32)], axis=1)
    alog_row = _pad_cols(a_log, LANES)
    dskip_l = jnp.repeat(d_skip, HEAD_DIM, axis=1)
    sel_t = (lax.broadcasted_iota(jnp.int32, (LANES, ds), 1) // HEAD_DIM
             == lax.broadcasted_iota(jnp.int32, (LANES, ds), 0)).astype(BF16)
    sel = sel_t.T
    y, yssd, hin, cf, pre = _ssd_fwd(xbc, zs, dtf, conv_w_full, conv_b, brow, alog_row, dskip_l, ssd_norm, sel_t, hs, ha)

    blk = _att_block(p)
    nkb, npair = p // blk, ha // 2
    cum = jnp.where(lax.broadcasted_iota(jnp.int32, (p, 1), 0) < PADN, -NEG, cf[:, hs:hs + ha])
    ck = jnp.transpose(cum.T.reshape(npair, 2, nkb, blk), (0, 2, 1, 3))
    ck = jnp.pad(ck, ((0, 0), (0, 0), (0, 6), (0, 0)))
    o, lse_rep = _attn_fwd(q, k, v, ck, blk)

    rows_all = _bcast_wait(rows_sems, rows_thru, rows_land, lse_rep, "gather_rows_wait")
    wps = rows_all[:, :r1].reshape(ds, d)
    wpa = rows_all[:, r1:r1 + r2].reshape(da, d)
    wout = rows_all[:, r1 + r2:].reshape(d, d)

    yatt, mrg, a_b, b_b, dzo, dout, red_fwd = _tail_fwd(
        yssd, o, zatt, graw, head, x2, tgt2, wps, wpa, wout, gate_bias, norm_post, tm)
    da_, db_, dgraw, dyssd, d_o, dzatt, red_bwd = _tail_bwd(dzo, a_b, b_b, graw, o, zatt, wps, wpa, wout, gate_bias, tm)

    tw = _tile(d, (512, 256, 128))
    g_wout = _mm(mrg, dzo, "tn", BF16, tw, d, "wgrad_out")
    g_wps = _mm(yssd, da_, "tn", BF16, _tile(ds, (512, 256, 128)), d, "wgrad_ps")
    g_wpa = _mm(yatt, db_, "tn", BF16, _tile(da, (512, 256, 128)), d, "wgrad_pa")

    core = lax.axis_index("c").astype(jnp.int32).reshape(1)
    chip = me // 2
    grows_parts = jnp.concatenate([g_wps.reshape(N_DEV, r1, d), g_wpa.reshape(N_DEV, r2, d),
                                   g_wout.reshape(N_DEV, r3, d)], axis=1)
    (sib_rows,) = _exchange_sibling([grows_parts], "scatter_rows_sibling")
    chip_rows = _pair_add(grows_parts, sib_rows, core, "pair_add_rows")
    r_sems, r_thru, r_lands, r_token = _exchange_chips_start([chip_rows], "scatter_rows_start")

    dk, dv, dq, dcs, rsum = _attn_bwd(q, k, v, o, d_o, lse_rep, ck + r_token[0:1, 0:1], blk)
    dcum = (rsum - dcs)[:, 0:2, :].reshape(ha, p).T
    dcf = jnp.pad(dcum, ((0, 0), (hs, LANES - hs - ha)))
    dxbc, dzs, ddtf, gcw, gcb, gnrm, gsm = _ssd_bwd(
        dyssd, y, zs, xbc, pre, dtf, hin, dcf, conv_w_full, brow, alog_row, dskip_l, ssd_norm, sel_t, sel, hs, ha)
    ddtf_b = ddtf.astype(BF16)

    dsegs = [dzs, dxbc, dzatt, dq, dk, dv, dgraw, ddtf_b]
    gsegs = [_mm(dsg, u, "tn", BF16, _tile(dsg.shape[1], (512, 256, 128)), d, "wgrad_in_%d" % i)
             for i, dsg in enumerate(dsegs)]
    g_z, g_xbc, g_zatt, g_q, g_k, g_v, g_g, g_dtf = gsegs
    gw_full = jnp.concatenate([g_z, g_xbc, g_dtf[:hs], g_zatt, g_q, g_k, g_v, g_dtf[hs:hs + ha], g_g], axis=0)
    gwin_parts = gw_full.reshape(N_DEV, nc8, d)

    (sib_win,) = _exchange_sibling([gwin_parts], "scatter_grads_sibling")
    chip_win = _pair_add(gwin_parts, sib_win, core, "pair_add_w_in")
    sems, thru, lands, token = _exchange_chips_start([chip_win], "scatter_grads_start")
    dsegs_after = dsegs[:-1] + [ddtf_b + token[0:1, 0:1].astype(BF16)]
    dhp, gnp = _dgrad_prenorm(dsegs_after, seg_w + [w_dtf], head, x2, norm_pre, dout, tm, "dgrad_in")
    gx, ghead = dhp[CHUNK:], dhp[:CHUNK]
    own_slot = lambda got, sent: lax.dynamic_update_slice_in_dim(
        got, lax.dynamic_slice_in_dim(sent, chip, 1, axis=0), chip, axis=0)
    (sent,), (got,) = _exchange_chips_wait(sems, thru, lands, gnp, "scatter_grads_wait")
    recv_win = own_slot(got, sent)
    (r_sent,), (r_got,) = _exchange_chips_wait(r_sems, r_thru, r_lands, gnp, "scatter_rows_wait")
    recv_rows = own_slot(r_got, r_sent)
    gmisc = jnp.concatenate([gsm[0:1], gsm[1:2], gsm[2:3], _pad_cols(red_fwd[1:2, 0:1], LANES)], axis=1)
    small_g = jnp.concatenate([
        _pack_small_rep(gnp[0:1], red_fwd[0:1], red_bwd[0:1], gnrm[0:1], gcb[0:1], gmisc, cd),
        _pad_cols(gcw[0:CONV_K], cd), jnp.zeros((4, cd), F32), _pad_cols(ghead[PADN:], cd)], axis=0)
    sg_sems, sg_thru, sg_land, sg_token = _bcast_start(small_g, "reduce_small_start")

    upd_in = _adamw(jnp.transpose(w_in[0]) + sg_token[0:1, 0:1], recv_win, jnp.transpose(m_w_in[0]),
                    jnp.transpose(v_w_in[0]), "adamw_w_in", parts=True)
    upd_ps = _adamw(w_proj_ssd[0] + sg_token[0:1, 0:1], recv_rows, m_w_proj_ssd[0], v_w_proj_ssd[0],
                    "adamw_w_proj_ssd", parts=True, part_row0=0)
    upd_pa = _adamw(w_proj_att[0], recv_rows, m_w_proj_att[0], v_w_proj_att[0], "adamw_w_proj_att", parts=True,
                    part_row0=r1)
    upd_out = _adamw(w_out[0], recv_rows, m_w_out[0], v_w_out[0], "adamw_w_out", parts=True, part_row0=r1 + r2)
    all_done = upd_in[1][0:8, 0:LANES] + upd_ps[1][0:8, 0:LANES] + upd_pa[1][0:8, 0:LANES] + upd_out[1][0:8, 0:LANES]
    red = _sum_slots(_bcast_wait(sg_sems, sg_thru, sg_land, all_done, "reduce_small_wait"), "reduce_small_sum")
    loss = red[5, 3 * LANES]
    g_conv_w = lax.dynamic_slice_in_dim(red[8:8 + CONV_K], me * cws, cws, axis=1)
    g_meta = lax.dynamic_slice_in_dim(red[16:16 + N_META, :d], me * msh, msh, axis=1)
    small = {
        "meta_tokens": (meta_tokens, m_meta_tokens, v_meta_tokens, g_meta),
        "norm_pre": (norm_pre, m_norm_pre, v_norm_pre, (0, 0)),
        "conv_w": (conv_w[0], m_conv_w[0], v_conv_w[0], g_conv_w),
        "conv_b": (conv_b, m_conv_b, v_conv_b, (4, 0)),
        "dt_bias": (dt_bias, m_dt_bias, v_dt_bias, (5, 0)),
        "a_log": (a_log, m_a_log, v_a_log, (5, LANES)),
        "d_skip": (d_skip, m_d_skip, v_d_skip, (5, 2 * LANES)),
        "ssd_norm": (ssd_norm, m_ssd_norm, v_ssd_norm, (3, 0)),
        "fgate_bias": (fgate_bias, m_fgate_bias, v_fgate_bias, (5, hs)),
        "gate_bias": (gate_bias, m_gate_bias, v_gate_bias, (2, 0)),
        "norm_post": (norm_post, m_norm_post, v_norm_post, (1, 0)),
    }
    upd_small = _adamw_small(small, red, "adamw_small")

    def leaves(i):
        sm = {k: v[i] for k, v in upd_small.items()}
        return [sm["meta_tokens"], sm["norm_pre"], jnp.transpose(upd_in[i])[None], sm["conv_w"][None], sm["conv_b"],
                sm["dt_bias"], sm["a_log"], sm["d_skip"], sm["ssd_norm"], sm["fgate_bias"], sm["gate_bias"],
                upd_ps[i][None], upd_pa[i][None], upd_out[i][None], sm["norm_post"]]

    return tuple([loss, gx[None]] + leaves(0) + leaves(1) + leaves(2) + leaves(3))
```

```python
import functools
import math

import jax
import jax.numpy as jnp
from jax import lax
from jax.experimental import pallas as pl
from jax.experimental.pallas import tpu as pltpu

F32 = jnp.float32
BF16 = jnp.bfloat16

N_DEV = 8
N_META = 16
CHUNK = 128
PADN = CHUNK - N_META
HEAD_DIM = 64
SSD_GROUPS = 4
CONV_K = 4
EPS = 1e-6
NEG = -1e30
LANES = 128
HALO = 16

ADAM_LR = 0.001
ADAM_B1 = 0.9
ADAM_B2 = 0.999
ADAM_EPS = 1e-08
ADAM_WD = 0.01
ADAM_STEP = 10

VMEM_LIMIT = 56 * 1024 * 1024

NN = (((1,), (0,)), ((), ()))
NT = (((1,), (1,)), ((), ()))
TN = (((0,), (0,)), ((), ()))
MESH = pl.DeviceIdType.MESH


def _dot(a, b, dims=NN):
    return lax.dot_general(a, b, dims, preferred_element_type=F32)


def _split2(x):
    hi = x.astype(BF16)
    lo = (x - hi.astype(F32)).astype(BF16)
    return hi, lo


def _dot_sel(x, sel):
    hi, lo = _split2(x)
    return _dot(hi, sel) + _dot(lo, sel)


def _dot_tri(tri, x):
    h1 = x.astype(BF16)
    r1 = x - h1.astype(F32)
    h2 = r1.astype(BF16)
    h3 = (r1 - h2.astype(F32)).astype(BF16)
    return _dot(tri, h1) + _dot(tri, h2) + _dot(tri, h3)


def _sigmoid(x):
    return 0.5 * jnp.tanh(0.5 * x) + 0.5


def _softplus(x):
    return jnp.maximum(x, 0.0) + jnp.log(1.0 + jnp.exp(-jnp.abs(x)))


def _cparams(sem=None, vmem=VMEM_LIMIT):
    kw = {"vmem_limit_bytes": vmem}
    if sem is not None:
        kw["dimension_semantics"] = sem
    return pltpu.CompilerParams(**kw)


def _full(shape):
    nd = len(shape)
    return pl.BlockSpec(shape, lambda *_: (0,) * nd)


def _att_block(p):
    return 384 if p % 384 == 0 else CHUNK


def _my_pos():
    return lax.axis_index("x"), lax.axis_index("y"), lax.axis_index("c")


def _dev_index(x, y, c):
    return 4 * x + 2 * y + c


FLIPS = [(fx, fy, fc) for fx in (0, 1) for fy in (0, 1) for fc in (0, 1)][1:]


def _flip(pos, f):
    return tuple((1 - p) if fi else p for p, fi in zip(pos, f))


def _all_gather(bufs, name):
    nb = len(bufs)

    def body(*refs):
        ins, outs = refs[:nb], refs[nb:2 * nb]
        send_sems, recv_sems, local_sems = refs[2 * nb:]
        x, y, c = _my_pos()
        me = _dev_index(x, y, c)
        sibling = (x, y, 1 - c)
        near = [(1 - x, y), (x, 1 - y)]
        far = (1 - x, 1 - y)
        relay_from = (c * (1 - x) + (1 - c) * x, c * y + (1 - c) * (1 - y))
        relay_to = (c * x + (1 - c) * (1 - x), c * (1 - y) + (1 - c) * y)

        def copy(b, k, block_idx, to, src=None):
            dst = outs[b].at[block_idx]
            return pltpu.make_async_remote_copy(
                src_ref=dst if src is None else src, dst_ref=dst,
                send_sem=send_sems.at[b, k], recv_sem=recv_sems.at[b, k],
                device_id=to, device_id_type=MESH)

        started = []
        for b in range(nb):
            mine = pltpu.make_async_copy(ins[b], outs[b].at[me], local_sems.at[b])
            mine.start()
            started.append(mine)
        sent = []
        for b in range(nb):
            sent.append(copy(b, 0, me, sibling, src=ins[b]))
            for j, chip in enumerate(near):
                sent.append(copy(b, 1 + j, me, (chip[0], chip[1], c), src=ins[b]))
        for cp in sent:
            cp.start()
        for j, chip in enumerate(near):
            blk = _dev_index(chip[0], chip[1], c)
            for b in range(nb):
                copy(b, 1 + j, blk, (x, y, c)).wait_recv()
                sent.append(copy(b, 4 + j, blk, sibling))
                sent[-1].start()
        for b in range(nb):
            sent.append(copy(b, 3, _dev_index(relay_from[0], relay_from[1], c), (relay_to[0], relay_to[1], c)))
            sent[-1].start()
        blk = _dev_index(far[0], far[1], c)
        for b in range(nb):
            copy(b, 3, blk, (x, y, c)).wait_recv()
            sent.append(copy(b, 6, blk, sibling))
            sent[-1].start()
        for b in range(nb):
            copy(b, 0, _dev_index(x, y, 1 - c), (x, y, c)).wait_recv()
        for j, chip in enumerate(near + [far]):
            blk = _dev_index(chip[0], chip[1], 1 - c)
            for b in range(nb):
                copy(b, 4 + j, blk, (x, y, c)).wait_recv()
        for cp in sent:
            cp.wait_send()
        for mine in started:
            mine.wait()

    any_spec = pl.BlockSpec(memory_space=pl.ANY)
    return pl.pallas_call(
        body, name=name,
        out_shape=[jax.ShapeDtypeStruct((N_DEV,) + b.shape, b.dtype) for b in bufs],
        in_specs=[any_spec] * nb, out_specs=[any_spec] * nb,
        scratch_shapes=[pltpu.SemaphoreType.DMA((nb, 7)), pltpu.SemaphoreType.DMA((nb, 7)),
                        pltpu.SemaphoreType.DMA((nb,))],
    )(*bufs)


N_CHIP = 4
CHIP_FLIPS = [(1, 0), (0, 1), (1, 1)]


def _exchange_sibling(bufs, name):
    nb = len(bufs)

    def body(*refs):
        ins, outs = refs[:nb], refs[nb:2 * nb]
        send_sems, recv_sems = refs[2 * nb:]
        x, y, c = _my_pos()

        def copy(b, k):
            return pltpu.make_async_remote_copy(
                src_ref=ins[b].at[2 * k + (1 - c)], dst_ref=outs[b].at[k],
                send_sem=send_sems.at[b, k], recv_sem=recv_sems.at[b, k],
                device_id=(x, y, 1 - c), device_id_type=MESH)

        cps = [copy(b, k) for b in range(nb) for k in range(N_CHIP)]
        for cp in cps:
            cp.start()
        for cp in cps:
            cp.wait()

    any_spec = pl.BlockSpec(memory_space=pl.ANY)
    return pl.pallas_call(
        body, name=name,
        out_shape=[jax.ShapeDtypeStruct((N_CHIP,) + b.shape[1:], b.dtype) for b in bufs],
        in_specs=[any_spec] * nb, out_specs=[any_spec] * nb,
        scratch_shapes=[pltpu.SemaphoreType.DMA((nb, N_CHIP)), pltpu.SemaphoreType.DMA((nb, N_CHIP))],
    )(*bufs)


def _pair_add(mine, recv, core, name):
    _, r, cdim = mine.shape
    tr, tc = r, cdim
    pick = lambda i: (i, 0)

    def body(core_ref, a_ref, b_ref, o_ref):
        o_ref[0] = (a_ref[0].astype(F32) + b_ref[0].astype(F32)).astype(o_ref.dtype)

    return pl.pallas_call(
        body, name=name,
        grid_spec=pltpu.PrefetchScalarGridSpec(
            num_scalar_prefetch=1, grid=(N_CHIP, (r // tr) * (cdim // tc)),
            in_specs=[pl.BlockSpec((1, tr, tc), lambda k, i, core_ref: (2 * k + core_ref[0],) + pick(i)),
                      pl.BlockSpec((1, tr, tc), lambda k, i, core_ref: (k,) + pick(i))],
            out_specs=pl.BlockSpec((1, tr, tc), lambda k, i, core_ref: (k,) + pick(i))),
        out_shape=jax.ShapeDtypeStruct((N_CHIP, r, cdim), mine.dtype),
        compiler_params=_cparams(("parallel", "parallel")),
    )(core, mine, recv)


def _chip_peer(x, y, f):
    return ((1 - x) if f[0] else x), ((1 - y) if f[1] else y)


def _exchange_chips_start(bufs, name):
    nb = len(bufs)
    nsem = 2 * 3 * nb

    def body(*refs):
        ins, lands = refs[:nb], refs[nb:2 * nb]
        sems = refs[2 * nb:2 * nb + nsem]
        token = refs[-1]
        x, y, c = _my_pos()
        for b in range(nb):
            for j, f in enumerate(CHIP_FLIPS):
                px, py = _chip_peer(x, y, f)
                pltpu.make_async_remote_copy(
                    src_ref=ins[b].at[2 * px + py], dst_ref=lands[b].at[2 * x + y],
                    send_sem=sems[2 * (3 * b + j)], recv_sem=sems[2 * (3 * b + j) + 1],
                    device_id=(px, py, c), device_id_type=MESH).start()
        token[...] = jnp.zeros_like(token)

    hbm = pl.BlockSpec(memory_space=pltpu.HBM)
    sem = pl.BlockSpec(memory_space=pltpu.SEMAPHORE)
    out = pl.pallas_call(
        body, name=name,
        out_shape=(*([pltpu.SemaphoreType.DMA(())] * nsem),
                   *[pltpu.HBM(b.shape, b.dtype) for b in bufs], *[pltpu.HBM(b.shape, b.dtype) for b in bufs],
                   jax.ShapeDtypeStruct((8, LANES), F32)),
        in_specs=[hbm] * (2 * nb),
        out_specs=(*([sem] * nsem), *([hbm] * (2 * nb)), pl.BlockSpec(memory_space=pltpu.VMEM)),
        input_output_aliases={i: nsem + i for i in range(2 * nb)},
        compiler_params=pltpu.CompilerParams(has_side_effects=pltpu.SideEffectType.DATAFLOW_SIDE_EFFECTING),
    )(*[pltpu.with_memory_space_constraint(b, pltpu.HBM) for b in bufs],
      *[pltpu.with_memory_space_constraint(lax.empty(b.shape, b.dtype), pltpu.HBM) for b in bufs])
    return out[:nsem], out[nsem:nsem + nb], out[nsem + nb:nsem + 2 * nb], out[-1]


def _exchange_chips_wait(sems, thru, lands, after, name):
    nb = len(thru)
    nsem = len(sems)

    def body(*refs):
        ins, lnd = refs[:nb], refs[nb:2 * nb]
        sem_refs = refs[2 * nb:2 * nb + nsem]
        x, y, c = _my_pos()
        for b in range(nb):
            for j, f in enumerate(CHIP_FLIPS):
                px, py = _chip_peer(x, y, f)
                cp = pltpu.make_async_remote_copy(
                    src_ref=ins[b].at[2 * px + py], dst_ref=lnd[b].at[2 * px + py],
                    send_sem=sem_refs[2 * (3 * b + j)], recv_sem=sem_refs[2 * (3 * b + j) + 1],
                    device_id=(px, py, c), device_id_type=MESH)
                cp.wait_send()
                cp.wait_recv()

    hbm = pl.BlockSpec(memory_space=pltpu.HBM)
    sem = pl.BlockSpec(memory_space=pltpu.SEMAPHORE)
    out = pl.pallas_call(
        body, name=name,
        out_shape=tuple([pltpu.HBM(b.shape, b.dtype) for b in thru] + [pltpu.HBM(b.shape, b.dtype) for b in lands]),
        in_specs=[hbm] * (2 * nb) + [sem] * nsem + [pl.BlockSpec(memory_space=pl.ANY)],
        out_specs=tuple([hbm] * (2 * nb)),
        input_output_aliases={i: i for i in range(2 * nb)},
        compiler_params=pltpu.CompilerParams(has_side_effects=pltpu.SideEffectType.DATAFLOW_SIDE_EFFECTING),
    )(*thru, *lands, *sems, after)
    return out[:nb], out[nb:]


def _bcast_start(buf, name):
    nsem = 2 * len(FLIPS)

    def body(src, land, *rest):
        sems, token = rest[:nsem], rest[-1]
        pos = _my_pos()
        for k, f in enumerate(FLIPS):
            pltpu.make_async_remote_copy(
                src_ref=src, dst_ref=land.at[_dev_index(*pos)], send_sem=sems[2 * k], recv_sem=sems[2 * k + 1],
                device_id=_flip(pos, f), device_id_type=MESH).start()
        token[...] = jnp.zeros_like(token)

    hbm = pl.BlockSpec(memory_space=pltpu.HBM)
    sem = pl.BlockSpec(memory_space=pltpu.SEMAPHORE)
    land_shape = (N_DEV,) + buf.shape
    out = pl.pallas_call(
        body, name=name,
        out_shape=(*([pltpu.SemaphoreType.DMA(())] * nsem), pltpu.HBM(buf.shape, buf.dtype),
                   pltpu.HBM(land_shape, buf.dtype), jax.ShapeDtypeStruct((8, LANES), F32)),
        in_specs=[hbm, hbm],
        out_specs=(*([sem] * nsem), hbm, hbm, pl.BlockSpec(memory_space=pltpu.VMEM)),
        input_output_aliases={0: nsem, 1: nsem + 1},
        compiler_params=pltpu.CompilerParams(has_side_effects=pltpu.SideEffectType.DATAFLOW_SIDE_EFFECTING),
    )(pltpu.with_memory_space_constraint(buf, pltpu.HBM),
      pltpu.with_memory_space_constraint(lax.empty(land_shape, buf.dtype), pltpu.HBM))
    return out[:nsem], out[nsem], out[nsem + 1], out[-1]


def _bcast_wait(sems, thru, land, after, name):
    nsem = len(sems)

    def body(src, lnd, *rest):
        sem_refs = rest[:nsem]
        pos = _my_pos()
        for k, f in enumerate(FLIPS):
            peer = _flip(pos, f)
            cp = pltpu.make_async_remote_copy(
                src_ref=src, dst_ref=lnd.at[_dev_index(*peer)], send_sem=sem_refs[2 * k],
                recv_sem=sem_refs[2 * k + 1], device_id=peer, device_id_type=MESH)
            cp.wait_send()
            cp.wait_recv()

    hbm = pl.BlockSpec(memory_space=pltpu.HBM)
    sem = pl.BlockSpec(memory_space=pltpu.SEMAPHORE)
    sent, got = pl.pallas_call(
        body, name=name,
        out_shape=(pltpu.HBM(thru.shape, thru.dtype), pltpu.HBM(land.shape, land.dtype)),
        in_specs=[hbm, hbm] + [sem] * nsem + [pl.BlockSpec(memory_space=pl.ANY)],
        out_specs=(hbm, hbm), input_output_aliases={0: 0, 1: 1},
        compiler_params=pltpu.CompilerParams(has_side_effects=pltpu.SideEffectType.DATAFLOW_SIDE_EFFECTING),
    )(thru, land, *sems, after)
    return lax.dynamic_update_slice_in_dim(got, sent[None], _dev_index(*_my_pos()), axis=0)


def _sum_slots(v, name):
    _, r, cdim = v.shape

    def body(v_ref, o_ref):
        acc = v_ref[0]
        for s in range(1, N_DEV):
            acc = acc + v_ref[s]
        o_ref[...] = acc

    return pl.pallas_call(
        body, name=name, out_shape=jax.ShapeDtypeStruct((r, cdim), F32),
        in_specs=[_full((N_DEV, r, cdim))], out_specs=_full((r, cdim)), grid=(1,),
        compiler_params=_cparams(("arbitrary",)),
    )(v)


def _mm(a, b, dims, out_dtype, tm, tn, name):
    if dims == "nn":
        (m, k), (_, n) = a.shape, b.shape
        a_spec = pl.BlockSpec((tm, k), lambda j, i: (i, 0))
        b_spec = pl.BlockSpec((k, tn), lambda j, i: (0, j))
        dn = NN
    elif dims == "nt":
        (m, k), (n, _) = a.shape, b.shape
        a_spec = pl.BlockSpec((tm, k), lambda j, i: (i, 0))
        b_spec = pl.BlockSpec((tn, k), lambda j, i: (j, 0))
        dn = NT
    else:
        (k, m), (_, n) = a.shape, b.shape
        a_spec = pl.BlockSpec((k, tm), lambda j, i: (0, i))
        b_spec = pl.BlockSpec((k, tn), lambda j, i: (0, j))
        dn = TN
    assert m % tm == 0 and n % tn == 0, (m, tm, n, tn)

    def body(a_ref, b_ref, o_ref):
        o_ref[...] = _dot(a_ref[...], b_ref[...], dn).astype(o_ref.dtype)

    return pl.pallas_call(
        body, name=name, grid=(n // tn, m // tm),
        in_specs=[a_spec, b_spec], out_specs=pl.BlockSpec((tm, tn), lambda j, i: (i, j)),
        out_shape=jax.ShapeDtypeStruct((m, n), out_dtype),
        compiler_params=_cparams(("parallel", "parallel")),
    )(a, b)


def _tiles_2d(r, cdim):
    if r % CHUNK == 0:
        return CHUNK, cdim, True
    return r, _tile(cdim, (256, 128)), False


def _dgrad_prenorm(a_list, b_list, head, x2, w, dout, tm, name):
    n_op = len(a_list)
    m, d = a_list[0].shape[0], b_list[0].shape[1]
    subs = _x_row_specs(tm, d)
    last = m // tm - 1
    rest = tm - CHUNK

    def body(*refs):
        a_refs, b_refs = refs[:n_op], refs[n_op:2 * n_op]
        head_ref = refs[2 * n_op]
        x_refs = refs[2 * n_op + 1:2 * n_op + 1 + len(subs)]
        w_ref, dout_ref, gx_ref, ghead_ref, gw_ref, dh_buf, sem = refs[2 * n_op + 1 + len(subs):]
        i = pl.program_id(0)

        def first_copy():
            return pltpu.make_async_copy(dh_buf.at[pl.ds(CHUNK, rest)], gx_ref.at[pl.ds(0, rest)], sem)

        def later_copy(step):
            return pltpu.make_async_copy(dh_buf, gx_ref.at[pl.ds(pl.multiple_of(step * tm - CHUNK, CHUNK), tm)], sem)

        @pl.when(i == 0)
        def _():
            gw_ref[...] = jnp.zeros_like(gw_ref)

        if rest and last >= 1:
            @pl.when(i == 1)
            def _():
                first_copy().wait()

        @pl.when(i >= (2 if rest else 1))
        def _():
            later_copy(i - 1).wait()

        du = _dot(a_refs[0][...], b_refs[0][...])
        for k in range(1, n_op):
            du = du + _dot(a_refs[k][...], b_refs[k][...])
        first = jnp.where(i == 0, head_ref[...], x_refs[0][...])
        h = jnp.concatenate([first] + [r[...] for r in x_refs[1:]], axis=0)
        rstd = lax.rsqrt(jnp.mean(h * h, axis=-1, keepdims=True) + EPS)
        xhat = h * rstd
        dxh = du * w_ref[...]
        dh_buf[...] = rstd * (dxh - xhat * jnp.mean(dxh * xhat, axis=-1, keepdims=True)) + dout_ref[...]
        gw_ref[0:1, :] += jnp.sum(du * xhat, axis=0, keepdims=True)

        @pl.when(i == 0)
        def _():
            ghead_ref[...] = dh_buf[0:CHUNK, :]
            if rest:
                first_copy().start()
                if last == 0:
                    first_copy().wait()

        @pl.when(i >= 1)
        def _():
            later_copy(i).start()

        if last >= 1:
            @pl.when(i == last)
            def _():
                later_copy(i).wait()

    once = lambda b: pl.BlockSpec(b.shape, lambda i: (0, 0), pipeline_mode=pl.Buffered(1))
    row = lambda width: pl.BlockSpec((tm, width), lambda i: (i, 0))
    return pl.pallas_call(
        body, name=name, grid=(m // tm,),
        in_specs=([row(a.shape[1]) for a in a_list] + [once(b) for b in b_list]
                  + [_full((CHUNK, d))] + subs + [_full((1, d)), row(d)]),
        out_specs=[pl.BlockSpec(memory_space=pl.ANY), _full((CHUNK, d)), _full((8, d))],
        out_shape=[jax.ShapeDtypeStruct((m - CHUNK, d), F32), jax.ShapeDtypeStruct((CHUNK, d), F32),
                   jax.ShapeDtypeStruct((8, d), F32)],
        scratch_shapes=[pltpu.VMEM((tm, d), F32), pltpu.SemaphoreType.DMA],
        compiler_params=_cparams(("arbitrary",)),
    )(*a_list, *b_list, head, *([x2] * len(subs)), w, dout)


def _tile(n, prefs):
    for t in prefs:
        if n % t == 0:
            return t
    return n


def _rows3(i):
    return jnp.maximum(3 * i - 1, 0), 3 * i, 3 * i + 1


def _x_row_specs(tm, d):
    if tm == CHUNK:
        return [pl.BlockSpec((CHUNK, d), lambda i: (jnp.maximum(i - 1, 0), 0))]
    return [pl.BlockSpec((CHUNK, d), functools.partial(lambda i, k: (_rows3(i)[k], 0), k=k)) for k in range(3)]


def _prenorm_fwd(head, x2, w, tm):
    p, d = x2.shape[0] + CHUNK, x2.shape[1]
    subs = _x_row_specs(tm, d)

    def body(head_ref, *rest):
        x_refs, (w_ref, u_ref) = rest[:len(subs)], rest[len(subs):]
        i = pl.program_id(0)
        first = jnp.where(i == 0, head_ref[...], x_refs[0][...])
        h = jnp.concatenate([first] + [r[...] for r in x_refs[1:]], axis=0)
        ms = jnp.mean(h * h, axis=-1, keepdims=True)
        u_ref[...] = (h * lax.rsqrt(ms + EPS) * w_ref[...]).astype(BF16)

    return pl.pallas_call(
        body, name="prenorm_fwd", grid=(p // tm,),
        in_specs=[_full((CHUNK, d))] + subs + [_full((1, d))],
        out_specs=pl.BlockSpec((tm, d), lambda i: (i, 0)),
        out_shape=jax.ShapeDtypeStruct((p, d), BF16),
        compiler_params=_cparams(("arbitrary",)),
    )(head, *([x2] * len(subs)), w)


def _conv_pre(ext_ref, cw_ref, cb_ref):
    pre = cb_ref[...] + cw_ref[CONV_K - 1:CONV_K, :] * ext_ref[8:8 + CHUNK, :]
    for j in range(1, CONV_K):
        pre = pre + cw_ref[CONV_K - 1 - j:CONV_K - j, :] * ext_ref[8 - j:8 - j + CHUNK, :]
    return pre


def _ssd_scalars(dtf_ref, brow_ref, alog_ref, rowmask, hs, ha, tri):
    lane = lax.broadcasted_iota(jnp.int32, (1, LANES), 1)
    is_dt = lane < hs
    is_f = (lane >= hs) & (lane < hs + ha)
    dtr = dtf_ref[...] + brow_ref[...]
    sp = _softplus(dtr)
    dt = jnp.where(is_dt, sp, 0.0) * rowmask
    logf = jnp.where(is_f, jnp.minimum(dtr, 0.0) - jnp.log(1.0 + jnp.exp(-jnp.abs(dtr))), 0.0) * rowmask
    a_row = jnp.where(is_dt, -jnp.exp(alog_ref[...]), 0.0)
    run = _dot_tri(tri, dt * a_row + logf)
    return dtr, dt, a_row, run, is_dt, is_f


def _tri_mats():
    r = lax.broadcasted_iota(jnp.int32, (CHUNK, CHUNK), 0)
    c = lax.broadcasted_iota(jnp.int32, (CHUNK, CHUNK), 1)
    return r, c


def _ssd_fwd(xbc, z, dtf, conv_w, conv_b, brow, alog, dskip_l, ssd_norm, sel_t, hs, ha):
    p, cd = xbc.shape
    ds = z.shape[1]
    ns = (cd - ds) // (2 * SSD_GROUPS)
    gw = ds // SSD_GROUPS
    nch = p // CHUNK
    hpg = hs // SSD_GROUPS

    def body(xbc_ref, halo_ref, z_ref, dtf_ref, cw_ref, cb_ref, brow_ref, alog_ref, dsk_ref, nrm_ref, selt_ref,
             y_ref, yssd_ref, hin_ref, cf_ref, pre_ref, st_ref, carry_ref, yacc_ref, xc_s, ex_s, xdtb_s, xwb_s, ext_s):
        c = pl.program_id(0)

        @pl.when(c == 0)
        def _():
            st_ref[...] = jnp.zeros_like(st_ref)
            carry_ref[...] = jnp.zeros_like(carry_ref)

        rows = lax.broadcasted_iota(jnp.int32, (CHUNK, 1), 0)
        rowmask = jnp.where((rows >= PADN) | (c > 0), 1.0, 0.0)
        ri, ci = _tri_mats()
        causal = ri >= ci
        tri = jnp.where(causal, 1.0, 0.0).astype(BF16)

        ext_s[0:8, :] = halo_ref[...].astype(F32)[HALO - 8:, :] * jnp.where(c > 0, 1.0, 0.0)
        ext_s[8:, :] = xbc_ref[...].astype(F32)
        pre = _conv_pre(ext_s, cw_ref, cb_ref)
        pre_ref[...] = pre.astype(BF16)
        xc_s[...] = pre * _sigmoid(pre) * rowmask

        dtr, dt, a_row, run, is_dt, is_f = _ssd_scalars(dtf_ref, brow_ref, alog_ref, rowmask, hs, ha, tri)
        cf = run + carry_ref[...]
        cf_ref[...] = cf
        carry_ref[...] = jnp.where(is_f, cf[CHUNK - 1:CHUNK, :], 0.0)
        cs = jnp.where(is_dt, run, 0.0)
        cl = cs[CHUNK - 1:CHUNK, :]
        selt = selt_ref[...]
        ex_s[...] = _dot_sel(jnp.exp(cs), selt)
        cdec_x = _dot_sel(jnp.broadcast_to(jnp.exp(cl), (8, LANES)), selt)[0:1, :]
        cs_t = cs.T
        xdt = xc_s[:, :ds] * _dot_sel(dt, selt)
        xdtb_s[...] = xdt.astype(BF16)
        xwb_s[...] = (xdt * _dot_sel(jnp.exp(cl - cs), selt)).astype(BF16)

        lane = lax.broadcasted_iota(jnp.int32, (1, LANES), 1)
        half0 = lane < HEAD_DIM
        for g in range(SSD_GROUPS):
            bg = xc_s[:, ds + g * ns: ds + (g + 1) * ns].astype(BF16)
            cg = xc_s[:, ds + SSD_GROUPS * ns + g * ns: ds + SSD_GROUPS * ns + (g + 1) * ns].astype(BF16)
            gm = _dot(cg, bg, NT)
            gs = slice(g * gw, (g + 1) * gw)
            stg = st_ref[:, gs]
            stg_b = stg.astype(BF16)
            hin_ref[0, :, gs] = stg_b
            yoff = _dot(cg, stg_b) * ex_s[:, gs]
            for pr in range(gw // LANES):
                sl = slice(g * gw + pr * LANES, g * gw + (pr + 1) * LANES)
                xp = xdtb_s[:, sl]
                yd = jnp.zeros((CHUNK, LANES), F32)
                for j in range(2):
                    h = g * hpg + 2 * pr + j
                    seg = cs[:, h:h + 1] - cs_t[h:h + 1, :]
                    m = jnp.where(causal, gm * jnp.exp(jnp.minimum(seg, 0.0)), 0.0).astype(BF16)
                    sel = half0 if j == 0 else jnp.logical_not(half0)
                    yd = yd + _dot(m, jnp.where(sel, xp, jnp.zeros_like(xp)))
                yacc_ref[:, sl] = yd + yoff[:, pr * LANES:(pr + 1) * LANES] + dsk_ref[:, sl] * xc_s[:, sl]
            st_ref[:, gs] = stg * cdec_x[:, gs] + _dot(bg, xwb_s[:, gs], TN)

        y = yacc_ref[...]
        y_ref[...] = y.astype(BF16)
        zf = z_ref[...].astype(F32)
        u = y * zf * _sigmoid(zf)
        for g in range(SSD_GROUPS):
            gs = slice(g * gw, (g + 1) * gw)
            ug = u[:, gs]
            ms = jnp.mean(ug * ug, axis=-1, keepdims=True)
            yssd_ref[:, gs] = (ug * lax.rsqrt(ms + EPS) * nrm_ref[:, gs]).astype(BF16)

    rb = CHUNK // HALO
    return pl.pallas_call(
        body, name="ssd_fwd", grid=(nch,),
        in_specs=[pl.BlockSpec((CHUNK, cd), lambda c: (c, 0)),
                  pl.BlockSpec((HALO, cd), lambda c: (jnp.maximum(c * rb - 1, 0), 0)),
                  pl.BlockSpec((CHUNK, ds), lambda c: (c, 0)),
                  pl.BlockSpec((CHUNK, LANES), lambda c: (c, 0)),
                  _full((CONV_K, cd)), _full((1, cd)), _full((1, LANES)), _full((1, LANES)),
                  _full((1, ds)), _full((1, ds)), _full((LANES, ds))],
        out_specs=[pl.BlockSpec((CHUNK, ds), lambda c: (c, 0)), pl.BlockSpec((CHUNK, ds), lambda c: (c, 0)),
                   pl.BlockSpec((1, ns, ds), lambda c: (c, 0, 0)), pl.BlockSpec((CHUNK, LANES), lambda c: (c, 0)),
                   pl.BlockSpec((CHUNK, cd), lambda c: (c, 0))],
        out_shape=[jax.ShapeDtypeStruct((p, ds), BF16), jax.ShapeDtypeStruct((p, ds), BF16),
                   jax.ShapeDtypeStruct((nch, ns, ds), BF16), jax.ShapeDtypeStruct((p, LANES), F32),
                   jax.ShapeDtypeStruct((p, cd), BF16)],
        scratch_shapes=[pltpu.VMEM((ns, ds), F32), pltpu.VMEM((1, LANES), F32), pltpu.VMEM((CHUNK, ds), F32),
                        pltpu.VMEM((CHUNK, cd), F32), pltpu.VMEM((CHUNK, ds), F32),
                        pltpu.VMEM((CHUNK, ds), BF16), pltpu.VMEM((CHUNK, ds), BF16),
                        pltpu.VMEM((8 + CHUNK, cd), F32)],
        compiler_params=_cparams(("arbitrary",)),
    )(xbc, xbc, z, dtf, conv_w, conv_b, brow, alog, dskip_l, ssd_norm, sel_t)


def _ssd_bwd(dyssd, y, z, xbc, pre, dtf, hin, dcf, conv_w, brow, alog, dskip_l, ssd_norm, sel_t, sel, hs, ha):
    p, cd = xbc.shape
    ds = z.shape[1]
    ns = (cd - ds) // (2 * SSD_GROUPS)
    gw = ds // SSD_GROUPS
    nch = p // CHUNK
    hpg = hs // SSD_GROUPS

    def body(dyssd_ref, y_ref, z_ref, xbc_ref, pre_ref, dtf_ref, hin_ref, dcf_ref, cw_ref, brow_ref,
             alog_ref, dsk_ref, nrm_ref, selt_ref, sel_ref,
             dxbc_ref, dz_ref, ddtf_ref, gcw_ref, gcb_ref, gnrm_ref, gsm_ref,
             dst_ref, nxt_ref, fcar_ref, gdsk_ref, dxc_ref, xc_s, dsl_s, dtx_s, ex_s, wx_s, dy_s, xdtb_s, xwb_s,
             dyb_s, dyeb_s):
        step = pl.program_id(0)
        c = nch - 1 - step

        @pl.when(step == 0)
        def _():
            dst_ref[...] = jnp.zeros_like(dst_ref)
            nxt_ref[...] = jnp.zeros_like(nxt_ref)
            fcar_ref[...] = jnp.zeros_like(fcar_ref)
            gdsk_ref[...] = jnp.zeros_like(gdsk_ref)
            gcw_ref[...] = jnp.zeros_like(gcw_ref)
            gcb_ref[...] = jnp.zeros_like(gcb_ref)
            gnrm_ref[...] = jnp.zeros_like(gnrm_ref)
            gsm_ref[...] = jnp.zeros_like(gsm_ref)

        rows = lax.broadcasted_iota(jnp.int32, (CHUNK, 1), 0)
        rowmask = jnp.where((rows >= PADN) | (c > 0), 1.0, 0.0)
        ri, ci = _tri_mats()
        causal = ri >= ci
        anti = ci >= ri
        tri = jnp.where(causal, 1.0, 0.0).astype(BF16)
        rtri = jnp.where(anti, 1.0, 0.0).astype(BF16)

        pre = pre_ref[...].astype(F32)
        sg = _sigmoid(pre)
        xc_s[...] = pre * sg * rowmask
        dsl_s[...] = sg * (1.0 + pre * (1.0 - sg)) * rowmask

        dtr, dt, a_row, run, is_dt, is_f = _ssd_scalars(dtf_ref, brow_ref, alog_ref, rowmask, hs, ha, tri)
        cs = jnp.where(is_dt, run, 0.0)
        cl = cs[CHUNK - 1:CHUNK, :]
        selt = selt_ref[...]
        selm = sel_ref[...]
        dtx_s[...] = _dot_sel(dt, selt)
        ex_s[...] = _dot_sel(jnp.exp(cs), selt)
        wx_s[...] = _dot_sel(jnp.exp(cl - cs), selt)
        cdec = jnp.exp(cl)
        cdec_x = _dot_sel(jnp.broadcast_to(cdec, (8, LANES)), selt)[0:1, :]
        cs_t = cs.T
        xdt = xc_s[:, :ds] * dtx_s[...]
        xdtb_s[...] = xdt.astype(BF16)
        xwb_s[...] = (xdt * wx_s[...]).astype(BF16)

        yv = y_ref[...].astype(F32)
        zf = z_ref[...].astype(F32)
        sz = _sigmoid(zf)
        u = yv * zf * sz
        dyo = dyssd_ref[...].astype(F32)
        du_parts = []
        for g in range(SSD_GROUPS):
            gs = slice(g * gw, (g + 1) * gw)
            ug = u[:, gs]
            rstd = lax.rsqrt(jnp.mean(ug * ug, axis=-1, keepdims=True) + EPS)
            yhat = ug * rstd
            dyg = dyo[:, gs]
            gnrm_ref[0:1, gs] += jnp.sum(dyg * yhat, axis=0, keepdims=True)
            dyh = dyg * nrm_ref[:, gs]
            du_parts.append(rstd * (dyh - yhat * jnp.mean(dyh * yhat, axis=-1, keepdims=True)))
        du = jnp.concatenate(du_parts, axis=1)
        dy = du * zf * sz
        dz_ref[...] = (du * yv * sz * (1.0 + zf * (1.0 - sz))).astype(BF16)
        dy_s[...] = dy
        dyb_s[...] = dy.astype(BF16)
        dyeb_s[...] = (dy * ex_s[...]).astype(BF16)
        gdsk_ref[...] += jnp.sum(dy * xc_s[:, :ds], axis=0, keepdims=True)
        lane = lax.broadcasted_iota(jnp.int32, (1, LANES), 1)
        half0 = lane < HEAD_DIM
        x_parts, yo_parts, t4_parts = [], [], []
        dcs = jnp.zeros((CHUNK, LANES), F32)
        for g in range(SSD_GROUPS):
            gs = slice(g * gw, (g + 1) * gw)
            bsl = slice(ds + g * ns, ds + (g + 1) * ns)
            csl = slice(ds + SSD_GROUPS * ns + g * ns, ds + SSD_GROUPS * ns + (g + 1) * ns)
            bg = xc_s[:, bsl].astype(BF16)
            cg = xc_s[:, csl].astype(BF16)
            gm = _dot(cg, bg, NT)
            gm_t = _dot(bg, cg, NT)
            stg_b = hin_ref[0, :, gs]
            dstg = dst_ref[:, gs]
            dstg_b = dstg.astype(BF16)
            t4_parts.append(jnp.sum(dstg * stg_b.astype(F32), axis=0, keepdims=True))
            zst = _dot(bg, dstg_b) * wx_s[:, gs]
            x_parts.append(xc_s[:, gs] * dtx_s[:, gs] * zst)
            yo_parts.append(dy_s[:, gs] * (_dot(cg, stg_b) * ex_s[:, gs]))
            dgsum = jnp.zeros((CHUNK, CHUNK), F32)
            dgtsum = jnp.zeros((CHUNK, CHUNK), F32)
            for pr in range(gw // LANES):
                sl = slice(g * gw + pr * LANES, g * gw + (pr + 1) * LANES)
                xp = xdtb_s[:, sl]
                dyp = dyb_s[:, sl]
                dxd = zst[:, pr * LANES:(pr + 1) * LANES]
                for j in range(2):
                    h = g * hpg + 2 * pr + j
                    sel_l = half0 if j == 0 else jnp.logical_not(half0)
                    seg = cs[:, h:h + 1] - cs_t[h:h + 1, :]
                    lm = jnp.where(causal, jnp.exp(jnp.minimum(seg, 0.0)), 0.0)
                    lmt = lm.T
                    dyp_m = jnp.where(sel_l, dyp, jnp.zeros_like(dyp))
                    xp_m = jnp.where(sel_l, xp, jnp.zeros_like(xp))
                    dxd = dxd + _dot((gm_t * lmt).astype(BF16), dyp_m)
                    dg = _dot(dyp_m, xp, NT) * lm
                    dgt = _dot(xp_m, dyp, NT) * lmt
                    dgsum = dgsum + dg
                    dgtsum = dgtsum + dgt
                    qrow = (jnp.sum(dg * gm, axis=1, keepdims=True) - jnp.sum(dgt * gm_t, axis=1, keepdims=True))
                    dcs = dcs + jnp.where(lane == h, qrow, 0.0)
                dxc_ref[:, sl] = dxd
            dxc_ref[:, csl] = _dot(dgsum.astype(BF16), bg) + _dot(dyeb_s[:, gs], stg_b, NT)
            dxc_ref[:, bsl] = _dot(dgtsum.astype(BF16), cg) + _dot(xwb_s[:, gs], dstg_b, NT)
            dst_ref[:, gs] = dstg * cdec_x[:, gs] + _dot(cg, dyeb_s[:, gs], TN)

        dxdt = dxc_ref[:, :ds]
        xst = _dot_sel(jnp.concatenate(x_parts, axis=1), selm)
        yo = _dot_sel(jnp.concatenate(yo_parts, axis=1), selm)
        t4 = _dot_sel(jnp.concatenate([jnp.concatenate(t4_parts, axis=1), jnp.zeros((7, ds), F32)], axis=0), selm)
        dcl = jnp.sum(xst, axis=0, keepdims=True) + cdec * t4[0:1, :]
        dcs = dcs + yo - xst + jnp.where(rows == CHUNK - 1, dcl, 0.0)
        da_ = _dot_tri(rtri, dcs)
        ddt = _dot_sel(dxdt * xc_s[:, :ds], selm) + da_ * a_row
        dcf_blk = dcf_ref[...]
        dlogf = _dot_tri(rtri, dcf_blk) + fcar_ref[...]
        fcar_ref[...] += jnp.sum(dcf_blk, axis=0, keepdims=True)
        sgd = _sigmoid(dtr)
        ddtf = (jnp.where(is_dt, ddt * sgd, 0.0) + jnp.where(is_f, dlogf * (1.0 - sgd), 0.0)) * rowmask
        ddtf_ref[...] = ddtf
        gsm_ref[0:1, :] += jnp.sum(ddtf, axis=0, keepdims=True)
        gsm_ref[1:2, :] += jnp.sum(da_ * dt, axis=0, keepdims=True) * a_row

        dxc_ref[:, :ds] = dxdt * dtx_s[...] + dsk_ref[...] * dy_s[...]
        dpre = dxc_ref[...] * dsl_s[...]
        nxt_ref[0:CHUNK, :] = dpre
        gcb_ref[0:1, :] += jnp.sum(dpre, axis=0, keepdims=True)
        xr = xbc_ref[...].astype(F32)
        gcw_ref[CONV_K - 1:CONV_K, :] += jnp.sum(dpre * xr, axis=0, keepdims=True)
        dxr = cw_ref[CONV_K - 1:CONV_K, :] * dpre
        for j in range(1, CONV_K):
            up = nxt_ref[j:j + CHUNK, :]
            gcw_ref[CONV_K - 1 - j:CONV_K - j, :] += jnp.sum(up * xr, axis=0, keepdims=True)
            dxr = dxr + cw_ref[CONV_K - 1 - j:CONV_K - j, :] * up
        nxt_ref[CHUNK:, :] = dpre[0:8, :]
        dxbc_ref[...] = dxr.astype(BF16)

        @pl.when(step == nch - 1)
        def _():
            gsm_ref[2:3, :] = _dot_sel(jnp.broadcast_to(gdsk_ref[...], (8, ds)), selm)[0:1, :]

    rev = lambda s: nch - 1 - s
    blk = lambda w: pl.BlockSpec((CHUNK, w), lambda s: (rev(s), 0))
    return pl.pallas_call(
        body, name="ssd_bwd", grid=(nch,),
        in_specs=[blk(ds), blk(ds), blk(ds), blk(cd), blk(cd),
                  blk(LANES), pl.BlockSpec((1, ns, ds), lambda s: (rev(s), 0, 0)), blk(LANES),
                  _full((CONV_K, cd)), _full((1, LANES)), _full((1, LANES)),
                  _full((1, ds)), _full((1, ds)), _full((LANES, ds)), _full((ds, LANES))],
        out_specs=[blk(cd), blk(ds), blk(LANES), _full((8, cd)), _full((8, cd)), _full((8, ds)), _full((8, LANES))],
        out_shape=[jax.ShapeDtypeStruct((p, cd), BF16), jax.ShapeDtypeStruct((p, ds), BF16),
                   jax.ShapeDtypeStruct((p, LANES), F32), jax.ShapeDtypeStruct((8, cd), F32),
                   jax.ShapeDtypeStruct((8, cd), F32), jax.ShapeDtypeStruct((8, ds), F32),
                   jax.ShapeDtypeStruct((8, LANES), F32)],
        scratch_shapes=[pltpu.VMEM((ns, ds), F32), pltpu.VMEM((CHUNK + 8, cd), F32), pltpu.VMEM((1, LANES), F32),
                        pltpu.VMEM((1, ds), F32), pltpu.VMEM((CHUNK, cd), F32),
                        pltpu.VMEM((CHUNK, cd), F32), pltpu.VMEM((CHUNK, cd), F32),
                        pltpu.VMEM((CHUNK, ds), F32), pltpu.VMEM((CHUNK, ds), F32), pltpu.VMEM((CHUNK, ds), F32),
                        pltpu.VMEM((CHUNK, ds), F32), pltpu.VMEM((CHUNK, ds), BF16), pltpu.VMEM((CHUNK, ds), BF16),
                        pltpu.VMEM((CHUNK, ds), BF16), pltpu.VMEM((CHUNK, ds), BF16)],
        compiler_params=_cparams(("arbitrary",)),
    )(dyssd, y, z, xbc, pre, dtf, hin, dcf, conv_w, brow, alog, dskip_l, ssd_norm, sel_t, sel)


def _attn_fwd(q, k, v, ck, blk):
    p, da = q.shape
    npair, nkb = ck.shape[0], ck.shape[1]
    scale = 1.0 / math.sqrt(HEAD_DIM)

    def body(q_ref, k_ref, v_ref, ck_ref, o_ref, lse_ref):
        i = pl.program_id(1)
        lane = lax.broadcasted_iota(jnp.int32, (1, LANES), 1)
        sels = [lane < HEAD_DIM, lane >= HEAD_DIM]
        ones = [jnp.where(lane == HEAD_DIM, 1.0, 0.0).astype(BF16), jnp.where(lane == 0, 1.0, 0.0).astype(BF16)]
        qb = q_ref[...] * scale
        cmask = (lax.broadcasted_iota(jnp.int32, (blk, blk), 1) <= lax.broadcasted_iota(jnp.int32, (blk, blk), 0))

        def step(kb, carry, masked, nk=1):
            r0 = pl.multiple_of(kb * blk, blk)
            ks = k_ref[pl.ds(r0, nk * blk), :]
            vs = v_ref[pl.ds(r0, nk * blk), :]
            kk = jnp.concatenate([jnp.where(sel, ks, jnp.zeros_like(ks)) for sel in sels], axis=0)
            s_both = _dot(qb, kk, NT)
            out = []
            for j in range(2):
                m, acc = carry[2 * j], carry[2 * j + 1]
                ckr = jnp.concatenate([ck_ref[0, kb + t, j:j + 1, :] for t in range(nk)], axis=1)
                s = s_both[:, j * nk * blk:(j + 1) * nk * blk] - ckr
                if masked:
                    s = jnp.where(cmask, s, NEG)
                mn = jnp.maximum(m, jnp.max(s, axis=-1, keepdims=True))
                pr = jnp.exp(s - mn).astype(BF16)
                acc = jnp.exp(m - mn) * acc + _dot(pr, jnp.where(sels[j], vs, ones[j]))
                out += [mn, acc]
            return tuple(out)

        init = (jnp.full((blk, 1), NEG, F32), jnp.zeros((blk, LANES), F32)) * 2
        n4 = i // 4
        n2 = (i - 4 * n4) // 2
        carry = lax.fori_loop(0, n4, lambda t, c: step(4 * t, c, False, 4), init)
        carry = lax.fori_loop(0, n2, lambda t, c: step(4 * n4 + 2 * t, c, False, 2), carry)
        carry = lax.fori_loop(4 * n4 + 2 * n2, i, lambda kb, c: step(kb, c, False), carry)
        m0, a0, m1, a1 = step(i, carry, True)
        l0 = a0[:, HEAD_DIM:HEAD_DIM + 1]
        l1 = a1[:, 0:1]
        o_ref[...] = jnp.where(sels[0], a0 / l0, a1 / l1).astype(BF16)
        lse_ref[...] = jnp.where(sels[0], m0 + jnp.log(l0), m1 + jnp.log(l1))

    return pl.pallas_call(
        body, name="attn_fwd", grid=(npair, p // blk),
        in_specs=[pl.BlockSpec((blk, LANES), lambda h, i: (i, h)),
                  pl.BlockSpec((p, LANES), lambda h, i: (0, h)), pl.BlockSpec((p, LANES), lambda h, i: (0, h)),
                  pl.BlockSpec((1, nkb, 8, blk), lambda h, i: (h, 0, 0, 0))],
        out_specs=[pl.BlockSpec((blk, LANES), lambda h, i: (i, h)), pl.BlockSpec((blk, LANES), lambda h, i: (i, h))],
        out_shape=[jax.ShapeDtypeStruct((p, da), BF16), jax.ShapeDtypeStruct((p, da), F32)],
        compiler_params=_cparams(("parallel", "arbitrary")),
    )(q, k, v, ck)


def _attn_bwd(q, k, v, o, do, lse_rep, ck, blk):
    p, da = q.shape
    npair, nkb = ck.shape[0], ck.shape[1]
    nq = p // blk
    scale = 1.0 / math.sqrt(HEAD_DIM)

    def body(k_ref, v_ref, q_ref, do_ref, o_ref, lse_ref, ck_ref, dk_ref, dv_ref, dq_ref, dcs_ref, rsum_ref, dq_acc):
        jb = pl.program_id(1)

        @pl.when(jb == 0)
        def _():
            dq_acc[...] = jnp.zeros_like(dq_acc)

        ks = k_ref[...]
        vs = v_ref[...]
        lane = lax.broadcasted_iota(jnp.int32, (1, LANES), 1)
        sels = [lane < HEAD_DIM, lane >= HEAD_DIM]
        ones = [jnp.where(lane == HEAD_DIM, 1.0, 0.0).astype(BF16), jnp.where(lane == 0, 1.0, 0.0).astype(BF16)]
        kss = ks * scale
        kmo = [jnp.where(sels[j], kss, ones[j]) for j in range(2)]
        cmask = (lax.broadcasted_iota(jnp.int32, (blk, blk), 1) <= lax.broadcasted_iota(jnp.int32, (blk, blk), 0))

        def step(ib, carry, masked, nb=1):
            rows = nb * blk
            r0 = pl.multiple_of(ib * blk, blk)
            qb = q_ref[pl.ds(r0, rows), :] * scale
            dob = do_ref[pl.ds(r0, rows), :]
            prod = dob.astype(F32) * o_ref[pl.ds(r0, rows), :].astype(F32)
            out = []
            for j in range(2):
                dk, dv = carry[2 * j], carry[2 * j + 1]
                qm = jnp.where(sels[j], qb, jnp.zeros_like(qb))
                dom = jnp.where(sels[j], dob, jnp.zeros_like(dob))
                lse = lse_ref[pl.ds(r0, rows), HEAD_DIM * j:HEAD_DIM * j + 1]
                dlt = jnp.sum(jnp.where(sels[j], prod, 0.0), axis=-1, keepdims=True)
                s = _dot(qm, ks, NT) - ck_ref[0, 0, j:j + 1, :] - lse
                pm = jnp.exp(jnp.minimum(s, 0.0))
                if masked:
                    pm = jnp.where(cmask, pm, 0.0)
                ds_b = (pm * (_dot(dom, vs, NT) - dlt)).astype(BF16)
                dv = dv + _dot(pm.astype(BF16), dom, TN)
                dk = dk + _dot(ds_b, jnp.where(sels[j], qb, ones[j]), TN)
                dq_acc[pl.ds(r0, rows), LANES * j:LANES * (j + 1)] += _dot(ds_b, kmo[j])
                out += [dk, dv]
            return tuple(out)

        zero = jnp.zeros((blk, LANES), F32)
        carry = step(jb, (zero, zero, zero, zero), True)
        n4 = (nq - 1 - jb) // 4
        n2 = (nq - 1 - jb - 4 * n4) // 2
        carry = lax.fori_loop(0, n4, lambda t, c: step(jb + 1 + 4 * t, c, False, 4), carry)
        carry = lax.fori_loop(0, n2, lambda t, c: step(jb + 1 + 4 * n4 + 2 * t, c, False, 2), carry)
        dk0, dv0, dk1, dv1 = lax.fori_loop(jb + 1 + 4 * n4 + 2 * n2, nq, lambda ib, c: step(ib, c, False), carry)
        dk_ref[...] = jnp.where(sels[0], dk0, dk1).astype(BF16)
        dv_ref[...] = (dv0 + dv1).astype(BF16)
        pair8 = lambda c0, c1: jnp.where(lane == 0, c0, jnp.where(lane == 1, c1, 0.0)).T[0:8]
        dcs_ref[0] = pair8(dk0[:, HEAD_DIM:HEAD_DIM + 1], dk1[:, 0:1])

        @pl.when(jb == nkb - 1)
        def _():
            a0 = dq_acc[:, :LANES]
            a1 = dq_acc[:, LANES:]
            dq_ref[...] = jnp.where(sels[0], a0, a1).astype(BF16)
            rsum_ref[0] = pair8(a0[:, HEAD_DIM:HEAD_DIM + 1], a1[:, 0:1])

    colblk = pl.BlockSpec((blk, LANES), lambda h, j: (j, h))
    colfull = pl.BlockSpec((p, LANES), lambda h, j: (0, h))
    ckspec = pl.BlockSpec((1, 1, 8, blk), lambda h, j: (h, j, 0, 0))
    return pl.pallas_call(
        body, name="attn_bwd", grid=(npair, nkb),
        in_specs=[colblk, colblk, colfull, colfull, colfull, colfull, ckspec],
        out_specs=[colblk, colblk, colfull, pl.BlockSpec((1, 8, blk), lambda h, j: (h, 0, j)),
                   pl.BlockSpec((1, 8, p), lambda h, j: (h, 0, 0))],
        out_shape=[jax.ShapeDtypeStruct((p, da), BF16), jax.ShapeDtypeStruct((p, da), BF16),
                   jax.ShapeDtypeStruct((p, da), BF16), jax.ShapeDtypeStruct((npair, 8, p), F32),
                   jax.ShapeDtypeStruct((npair, 8, p), F32)],
        scratch_shapes=[pltpu.VMEM((p, 2 * LANES), F32)],
        compiler_params=_cparams(("parallel", "arbitrary")),
    )(k, v, q, do, o, lse_rep, ck)


def _tail_fwd(yssd, o, zatt, graw, head, x2, tgt2, wps, wpa, wout, gate_bias, norm_post, tm):
    p, ds = yssd.shape
    da = o.shape[1]
    d = x2.shape[1]
    nsub = tm // CHUNK

    def body(yssd_ref, o_ref, zatt_ref, g_ref, head_ref, *rest):
        x_refs, t_refs = rest[:nsub], rest[nsub:2 * nsub]
        (wps_ref, wpa_ref, wout_ref, gb_ref, np_ref,
         yatt_ref, mrg_ref, a_ref, b_ref, dzo_ref, dout_ref, red_ref) = rest[2 * nsub:]
        i = pl.program_id(0)

        @pl.when(i == 0)
        def _():
            red_ref[...] = jnp.zeros_like(red_ref)

        first = jnp.where(i == 0, head_ref[...], x_refs[0][...])
        h = jnp.concatenate([first] + [r[...] for r in x_refs[1:]], axis=0)
        tgt = jnp.concatenate([r[...] for r in t_refs], axis=0)
        rows = lax.broadcasted_iota(jnp.int32, (tm, 1), 0)
        valid = jnp.where((i > 0) | (rows >= CHUNK), 1.0, 0.0)
        ob = o_ref[...].astype(F32)
        za = zatt_ref[...].astype(F32)
        yatt_b = (ob * za * _sigmoid(za)).astype(BF16)
        yatt_ref[...] = yatt_b
        a = _dot(yssd_ref[...], wps_ref[...])
        b = _dot(yatt_b, wpa_ref[...])
        a_ref[...] = a.astype(BF16)
        b_ref[...] = b.astype(BF16)
        gr = g_ref[...].astype(F32) + gb_ref[...]
        mrg_b = (_sigmoid(gr[:, :d]) * a + _sigmoid(gr[:, d:]) * b).astype(BF16)
        mrg_ref[...] = mrg_b
        zo = _dot(mrg_b, wout_ref[...])
        rstd = lax.rsqrt(jnp.mean(zo * zo, axis=-1, keepdims=True) + EPS)
        zh = zo * rstd
        npw = np_ref[...]
        err = (h + zh * npw - tgt) * valid
        dout = err * (1.0 / d)
        dout_ref[...] = dout
        dzh = dout * npw
        dzo_ref[...] = (rstd * (dzh - zh * jnp.mean(dzh * zh, axis=-1, keepdims=True))).astype(BF16)
        red_ref[0:1, :] += jnp.sum(dout * zh, axis=0, keepdims=True)
        red_ref[1:2, 0:1] += jnp.sum(jnp.sum(err * err, axis=1, keepdims=True), axis=0, keepdims=True) * (0.5 / d)

    row = lambda w: pl.BlockSpec((tm, w), lambda i: (i, 0))
    once = lambda shape: pl.BlockSpec(shape, lambda i: (0,) * len(shape), pipeline_mode=pl.Buffered(1))
    subs = _x_row_specs(tm, d)
    sd = jax.ShapeDtypeStruct
    return pl.pallas_call(
        body, name="tail_fwd", grid=(p // tm,),
        in_specs=[row(ds), row(da), row(da), row(2 * d), _full((CHUNK, d))] + subs + subs
                 + [once((ds, d)), once((da, d)), once((d, d)), _full((1, 2 * d)), _full((1, d))],
        out_specs=[row(da), row(d), row(d), row(d), row(d), row(d), _full((8, d))],
        out_shape=[sd((p, da), BF16), sd((p, d), BF16), sd((p, d), BF16), sd((p, d), BF16), sd((p, d), BF16),
                   sd((p, d), F32), sd((8, d), F32)],
        compiler_params=_cparams(("arbitrary",)),
    )(yssd, o, zatt, graw, head, *([x2] * nsub), *([tgt2] * nsub), wps, wpa, wout, gate_bias, norm_post)


def _tail_bwd(dzo, a_b, b_b, graw, o, zatt, wps, wpa, wout, gate_bias, tm):
    p, d = dzo.shape
    ds, da = wps.shape[0], wpa.shape[0]

    def body(dzo_ref, a_ref, b_ref, g_ref, o_ref, zatt_ref, wps_ref, wpa_ref, wout_ref, gb_ref,
             da_ref, db_ref, dg_ref, dyssd_ref, do_ref, dzatt_ref, red_ref):
        i = pl.program_id(0)

        @pl.when(i == 0)
        def _():
            red_ref[...] = jnp.zeros_like(red_ref)

        gr = g_ref[...].astype(F32) + gb_ref[...]
        gs = _sigmoid(gr[:, :d])
        ga = _sigmoid(gr[:, d:])
        dm = _dot(dzo_ref[...], wout_ref[...], NT)
        da_b = (gs * dm).astype(BF16)
        db_b = (ga * dm).astype(BF16)
        da_ref[...] = da_b
        db_ref[...] = db_b
        dgs = dm * a_ref[...].astype(F32) * gs * (1.0 - gs)
        dga = dm * b_ref[...].astype(F32) * ga * (1.0 - ga)
        dg_ref[:, :d] = dgs.astype(BF16)
        dg_ref[:, d:] = dga.astype(BF16)
        red_ref[0:1, :d] += jnp.sum(dgs, axis=0, keepdims=True)
        red_ref[0:1, d:] += jnp.sum(dga, axis=0, keepdims=True)
        dyssd_ref[...] = _dot(da_b, wps_ref[...], NT).astype(BF16)
        dya = _dot(db_b, wpa_ref[...], NT)
        ob = o_ref[...].astype(F32)
        za = zatt_ref[...].astype(F32)
        sza = _sigmoid(za)
        do_ref[...] = (dya * za * sza).astype(BF16)
        dzatt_ref[...] = (dya * ob * sza * (1.0 + za * (1.0 - sza))).astype(BF16)

    row = lambda w: pl.BlockSpec((tm, w), lambda i: (i, 0))
    once = lambda shape: pl.BlockSpec(shape, lambda i: (0,) * len(shape), pipeline_mode=pl.Buffered(1))
    sd = jax.ShapeDtypeStruct
    return pl.pallas_call(
        body, name="tail_bwd", grid=(p // tm,),
        in_specs=[row(d), row(d), row(d), row(2 * d), row(da), row(da),
                  once((ds, d)), once((da, d)), once((d, d)), _full((1, 2 * d))],
        out_specs=[row(d), row(d), row(2 * d), row(ds), row(da), row(da), _full((8, 2 * d))],
        out_shape=[sd((p, d), BF16), sd((p, d), BF16), sd((p, 2 * d), BF16), sd((p, ds), BF16), sd((p, da), BF16),
                   sd((p, da), BF16), sd((8, 2 * d), F32)],
        compiler_params=_cparams(("arbitrary",)),
    )(dzo, a_b, b_b, graw, o, zatt, wps, wpa, wout, gate_bias)


def _adamw_math(w, g, m, v):
    m2 = ADAM_B1 * m + (1.0 - ADAM_B1) * g
    v2 = ADAM_B2 * v + (1.0 - ADAM_B2) * (g * g)
    m_hat = m2 / (1.0 - ADAM_B1 ** ADAM_STEP)
    v_hat = v2 / (1.0 - ADAM_B2 ** ADAM_STEP)
    delta = -ADAM_LR * (m_hat / (jnp.sqrt(v_hat) + ADAM_EPS) + ADAM_WD * w)
    return delta, m2, v2


def _adamw_small(params, red, name):
    names = list(params)
    n = len(names)
    extra = [params[k][3] for k in names if not isinstance(params[k][3], tuple)]

    def body(*refs):
        w_refs, m_refs, v_refs = refs[:n], refs[n:2 * n], refs[2 * n:3 * n]
        red_ref = refs[3 * n]
        g_refs = iter(refs[3 * n + 1:3 * n + 1 + len(extra)])
        outs = refs[3 * n + 1 + len(extra):]
        for i, k in enumerate(names):
            where = params[k][3]
            rows, cols = w_refs[i].shape
            if isinstance(where, tuple):
                g = red_ref[where[0]:where[0] + rows, where[1]:where[1] + cols]
            else:
                g = next(g_refs)[...]
            delta, m2, v2 = _adamw_math(w_refs[i][...], g, m_refs[i][...], v_refs[i][...])
            for o, val in zip(outs[4 * i:4 * i + 4], (g, delta, m2, v2)):
                o[...] = val

    vm = pl.BlockSpec(memory_space=pltpu.VMEM)
    ws, ms, vs = ([params[k][j] for k in names] for j in range(3))
    out = pl.pallas_call(
        body, name=name,
        out_shape=[jax.ShapeDtypeStruct(w.shape, F32) for w in ws for _ in range(4)],
        in_specs=[vm] * (3 * n + 1 + len(extra)), out_specs=[vm] * (4 * n),
    )(*ws, *ms, *vs, red, *extra)
    return {k: tuple(out[4 * i:4 * i + 4]) for i, k in enumerate(names)}


def _adamw(w, g, m, v, name, parts=False, part_row0=0):
    r, cdim = w.shape
    tr, tc, by_rows = _tiles_2d(r, cdim)
    pick = (lambda i: (i, 0)) if by_rows else (lambda i: (0, i))
    assert part_row0 % tr == 0
    gpick = (lambda i: (i + part_row0 // tr, 0)) if by_rows else (lambda i: (part_row0 // tr, i))

    def body(w_ref, g_ref, m_ref, v_ref, go_ref, d_ref, mo_ref, vo_ref):
        if parts:
            g = g_ref[0].astype(F32)
            for s in range(1, g_ref.shape[0]):
                g = g + g_ref[s].astype(F32)
        else:
            g = g_ref[...]
        delta, m2, v2 = _adamw_math(w_ref[...], g, m_ref[...], v_ref[...])
        go_ref[...] = g
        d_ref[...] = delta
        mo_ref[...] = m2
        vo_ref[...] = v2

    blk = pl.BlockSpec((tr, tc), pick)
    gspec = pl.BlockSpec((g.shape[0], tr, tc), lambda i: (0,) + gpick(i)) if parts else blk
    return pl.pallas_call(
        body, name=name, grid=((r // tr) * (cdim // tc),),
        in_specs=[blk, gspec, blk, blk], out_specs=[blk] * 4,
        out_shape=[jax.ShapeDtypeStruct((r, cdim), F32)] * 4,
        compiler_params=_cparams(("parallel",)),
    )(w, g, m, v)


def _pad_cols(a, width):
    return jnp.pad(a, ((0, 0), (0, width - a.shape[1])))


def _pack_small_shard(conv_w_sh, meta_sh, width):
    return jnp.concatenate([_pad_cols(conv_w_sh, width), jnp.zeros((4, width), F32), _pad_cols(meta_sh, width)], axis=0)


def _pack_small_rep(norm_pre, norm_post, gate_bias, ssd_norm, conv_b, misc, width):
    rows = [norm_pre, norm_post, gate_bias, ssd_norm, conv_b, misc]
    return jnp.concatenate([_pad_cols(r, width) for r in rows] + [jnp.zeros((2, width), F32)], axis=0)


def kernel(x, meta_tokens, norm_pre, w_in, conv_w, conv_b, dt_bias, a_log, d_skip, ssd_norm, fgate_bias, gate_bias, w_proj_ssd, w_proj_att, w_out, norm_post, loss_target, m_meta_tokens, m_norm_pre, m_w_in, m_conv_w, m_conv_b, m_dt_bias, m_a_log, m_d_skip, m_ssd_norm, m_fgate_bias, m_gate_bias, m_w_proj_ssd, m_w_proj_att, m_w_out, m_norm_post, v_meta_tokens, v_norm_pre, v_w_in, v_conv_w, v_conv_b, v_dt_bias, v_a_log, v_d_skip, v_ssd_norm, v_fgate_bias, v_gate_bias, v_w_proj_ssd, v_w_proj_att, v_w_out, v_norm_post):
    seq, d = x.shape[1], x.shape[2]
    p = seq + CHUNK
    hs, ha = dt_bias.shape[1], fgate_bias.shape[1]
    ds, cd = ssd_norm.shape[1], conv_b.shape[1]
    da = ha * HEAD_DIM
    nc8 = w_in.shape[2]
    cws = cd // N_DEV
    msh = d // N_DEV
    r1, r2, r3 = ds // N_DEV, da // N_DEV, d // N_DEV
    me = _dev_index(*_my_pos())
    x2, tgt2 = x[0], loss_target[0]

    win_sh = jnp.transpose(w_in[0]).astype(BF16)
    rows_sh = jnp.concatenate([w_proj_ssd[0], w_proj_att[0], w_out[0]], axis=0).astype(BF16)
    small_sh = _pack_small_shard(conv_w[0], meta_tokens, cws)
    win_all, small_all = _all_gather([win_sh, small_sh], "gather_weights")
    rows_sh, win_all = lax.optimization_barrier((rows_sh, win_all))
    rows_sems, rows_thru, rows_land, rows_token = _bcast_start(rows_sh, "gather_rows_start")
    cuts = [0, ds, ds + cd, ds + cd + hs, ds + cd + hs + da, ds + cd + hs + 2 * da, ds + cd + hs + 3 * da,
            ds + cd + hs + 4 * da, ds + cd + hs + 4 * da + ha, ds + cd + hs + 4 * da + ha + 2 * d]

    def piece_rows(r0, r1):
        parts = [win_all[s, max(r0, s * nc8) - s * nc8:min(r1, (s + 1) * nc8) - s * nc8]
                 for s in range(N_DEV) if max(r0, s * nc8) < min(r1, (s + 1) * nc8)]
        return parts[0] if len(parts) == 1 else jnp.concatenate(parts, axis=0)

    w_z, w_xbc, w_dt, w_zatt, w_q, w_k, w_v, w_f, w_g = [piece_rows(cuts[i], cuts[i + 1]) for i in range(9)]
    w_dtf = jnp.concatenate([w_dt, w_f, jnp.zeros((LANES - hs - ha, d), BF16)], axis=0)
    conv_w_full = jnp.transpose(small_all[:, 0:CONV_K, :], (1, 0, 2)).reshape(CONV_K, cd)
    meta_full = jnp.transpose(small_all[:, 8:8 + N_META, :msh], (1, 0, 2)).reshape(N_META, d)
    head = jnp.concatenate([jnp.zeros((PADN, d), F32), meta_full + rows_token[0:1, 0:1]], axis=0)

    tm = _att_block(p)
    u = _prenorm_fwd(head, x2, norm_pre, tm)
    seg_w = [w_z, w_xbc, w_zatt, w_q, w_k, w_v, w_g]
    zs, xbc, zatt, q, k, v, graw = [
        _mm(u, w, "nt", BF16, _tile(p, (1408, tm)), _tile(w.shape[0], (1024, 512, 256, 128)), "inproj_%d" % i)
        for i, w in enumerate(seg_w)]
    dtf = _mm(u, w_dtf, "nt", F32, _tile(p, (1408, tm)), LANES, "inproj_dtf")

    brow = jnp.concatenate([dt_bias, fgate_bias, jnp.zeros((1, LANES - hs - ha), F32)], axis=1)
    alog_row = _pad_cols(a_log, LANES)
    dskip_l = jnp.repeat(d_skip, HEAD_DIM, axis=1)
    sel_t = (lax.broadcasted_iota(jnp.int32, (LANES, ds), 1) // HEAD_DIM
             == lax.broadcasted_iota(jnp.int32, (LANES, ds), 0)).astype(BF16)
    sel = sel_t.T
    y, yssd, hin, cf, pre = _ssd_fwd(xbc, zs, dtf, conv_w_full, conv_b, brow, alog_row, dskip_l, ssd_norm, sel_t, hs, ha)

    blk = _att_block(p)
    nkb, npair = p // blk, ha // 2
    cum = jnp.where(lax.broadcasted_iota(jnp.int32, (p, 1), 0) < PADN, -NEG, cf[:, hs:hs + ha])
    ck = jnp.transpose(cum.T.reshape(npair, 2, nkb, blk), (0, 2, 1, 3))
    ck = jnp.pad(ck, ((0, 0), (0, 0), (0, 6), (0, 0)))
    o, lse_rep = _attn_fwd(q, k, v, ck, blk)

    rows_all = _bcast_wait(rows_sems, rows_thru, rows_land, lse_rep, "gather_rows_wait")
    wps = rows_all[:, :r1].reshape(ds, d)
    wpa = rows_all[:, r1:r1 + r2].reshape(da, d)
    wout = rows_all[:, r1 + r2:].reshape(d, d)

    yatt, mrg, a_b, b_b, dzo, dout, red_fwd = _tail_fwd(
        yssd, o, zatt, graw, head, x2, tgt2, wps, wpa, wout, gate_bias, norm_post, tm)
    da_, db_, dgraw, dyssd, d_o, dzatt, red_bwd = _tail_bwd(dzo, a_b, b_b, graw, o, zatt, wps, wpa, wout, gate_bias, tm)

    tw = _tile(d, (512, 256, 128))
    g_wout = _mm(mrg, dzo, "tn", BF16, tw, d, "wgrad_out")
    g_wps = _mm(yssd, da_, "tn", BF16, _tile(ds, (512, 256, 128)), d, "wgrad_ps")
    g_wpa = _mm(yatt, db_, "tn", BF16, _tile(da, (512, 256, 128)), d, "wgrad_pa")

    core = lax.axis_index("c").astype(jnp.int32).reshape(1)
    chip = me // 2
    grows_parts = jnp.concatenate([g_wps.reshape(N_DEV, r1, d), g_wpa.reshape(N_DEV, r2, d),
                                   g_wout.reshape(N_DEV, r3, d)], axis=1)
    (sib_rows,) = _exchange_sibling([grows_parts], "scatter_rows_sibling")
    chip_rows = _pair_add(grows_parts, sib_rows, core, "pair_add_rows")
    r_sems, r_thru, r_lands, r_token = _exchange_chips_start([chip_rows], "scatter_rows_start")

    dk, dv, dq, dcs, rsum = _attn_bwd(q, k, v, o, d_o, lse_rep, ck + r_token[0:1, 0:1], blk)
    dcum = (rsum - dcs)[:, 0:2, :].reshape(ha, p).T
    dcf = jnp.pad(dcum, ((0, 0), (hs, LANES - hs - ha)))
    dxbc, dzs, ddtf, gcw, gcb, gnrm, gsm = _ssd_bwd(
        dyssd, y, zs, xbc, pre, dtf, hin, dcf, conv_w_full, brow, alog_row, dskip_l, ssd_norm, sel_t, sel, hs, ha)
    ddtf_b = ddtf.astype(BF16)

    dsegs = [dzs, dxbc, dzatt, dq, dk, dv, dgraw, ddtf_b]
    gsegs = [_mm(dsg, u, "tn", BF16, _tile(dsg.shape[1], (512, 256, 128)), d, "wgrad_in_%d" % i)
             for i, dsg in enumerate(dsegs)]
    g_z, g_xbc, g_zatt, g_q, g_k, g_v, g_g, g_dtf = gsegs
    gw_full = jnp.concatenate([g_z, g_xbc, g_dtf[:hs], g_zatt, g_q, g_k, g_v, g_dtf[hs:hs + ha], g_g], axis=0)
    gwin_parts = gw_full.reshape(N_DEV, nc8, d)

    (sib_win,) = _exchange_sibling([gwin_parts], "scatter_grads_sibling")
    chip_win = _pair_add(gwin_parts, sib_win, core, "pair_add_w_in")
    sems, thru, lands, token = _exchange_chips_start([chip_win], "scatter_grads_start")
    dsegs_after = dsegs[:-1] + [ddtf_b + token[0:1, 0:1].astype(BF16)]
    gx, ghead, gnp = _dgrad_prenorm(dsegs_after, seg_w + [w_dtf], head, x2, norm_pre, dout, tm, "dgrad_in")
    own_slot = lambda got, sent: lax.dynamic_update_slice_in_dim(
        got, lax.dynamic_slice_in_dim(sent, chip, 1, axis=0), chip, axis=0)
    (sent,), (got,) = _exchange_chips_wait(sems, thru, lands, gnp, "scatter_grads_wait")
    recv_win = own_slot(got, sent)
    (r_sent,), (r_got,) = _exchange_chips_wait(r_sems, r_thru, r_lands, gnp, "scatter_rows_wait")
    recv_rows = own_slot(r_got, r_sent)
    gmisc = jnp.concatenate([gsm[0:1], gsm[1:2], gsm[2:3], _pad_cols(red_fwd[1:2, 0:1], LANES)], axis=1)
    small_g = jnp.concatenate([
        _pack_small_rep(gnp[0:1], red_fwd[0:1], red_bwd[0:1], gnrm[0:1], gcb[0:1], gmisc, cd),
        _pad_cols(gcw[0:CONV_K], cd), jnp.zeros((4, cd), F32), _pad_cols(ghead[PADN:], cd)], axis=0)
    sg_sems, sg_thru, sg_land, sg_token = _bcast_start(small_g, "reduce_small_start")

    upd_in = _adamw(jnp.transpose(w_in[0]) + sg_token[0:1, 0:1], recv_win, jnp.transpose(m_w_in[0]),
                    jnp.transpose(v_w_in[0]), "adamw_w_in", parts=True)
    upd_ps = _adamw(w_proj_ssd[0] + sg_token[0:1, 0:1], recv_rows, m_w_proj_ssd[0], v_w_proj_ssd[0],
                    "adamw_w_proj_ssd", parts=True, part_row0=0)
    upd_pa = _adamw(w_proj_att[0], recv_rows, m_w_proj_att[0], v_w_proj_att[0], "adamw_w_proj_att", parts=True,
                    part_row0=r1)
    upd_out = _adamw(w_out[0], recv_rows, m_w_out[0], v_w_out[0], "adamw_w_out", parts=True, part_row0=r1 + r2)
    all_done = upd_in[1][0:8, 0:LANES] + upd_ps[1][0:8, 0:LANES] + upd_pa[1][0:8, 0:LANES] + upd_out[1][0:8, 0:LANES]
    red = _sum_slots(_bcast_wait(sg_sems, sg_thru, sg_land, all_done, "reduce_small_wait"), "reduce_small_sum")
    loss = red[5, 3 * LANES]
    g_conv_w = lax.dynamic_slice_in_dim(red[8:8 + CONV_K], me * cws, cws, axis=1)
    g_meta = lax.dynamic_slice_in_dim(red[16:16 + N_META, :d], me * msh, msh, axis=1)
    small = {
        "meta_tokens": (meta_tokens, m_meta_tokens, v_meta_tokens, g_meta),
        "norm_pre": (norm_pre, m_norm_pre, v_norm_pre, (0, 0)),
        "conv_w": (conv_w[0], m_conv_w[0], v_conv_w[0], g_conv_w),
        "conv_b": (conv_b, m_conv_b, v_conv_b, (4, 0)),
        "dt_bias": (dt_bias, m_dt_bias, v_dt_bias, (5, 0)),
        "a_log": (a_log, m_a_log, v_a_log, (5, LANES)),
        "d_skip": (d_skip, m_d_skip, v_d_skip, (5, 2 * LANES)),
        "ssd_norm": (ssd_norm, m_ssd_norm, v_ssd_norm, (3, 0)),
        "fgate_bias": (fgate_bias, m_fgate_bias, v_fgate_bias, (5, hs)),
        "gate_bias": (gate_bias, m_gate_bias, v_gate_bias, (2, 0)),
        "norm_post": (norm_post, m_norm_post, v_norm_post, (1, 0)),
    }
    upd_small = _adamw_small(small, red, "adamw_small")

    def leaves(i):
        sm = {k: v[i] for k, v in upd_small.items()}
        return [sm["meta_tokens"], sm["norm_pre"], jnp.transpose(upd_in[i])[None], sm["conv_w"][None], sm["conv_b"],
                sm["dt_bias"], sm["a_log"], sm["d_skip"], sm["ssd_norm"], sm["fgate_bias"], sm["gate_bias"],
                upd_ps[i][None], upd_pa[i][None], upd_out[i][None], sm["norm_post"]]

    return tuple([loss, gx[None]] + leaves(0) + leaves(1) + leaves(2) + leaves(3))
```

```python
import functools
import math

import jax
import jax.numpy as jnp
from jax import lax
from jax.experimental import pallas as pl
from jax.experimental.pallas import tpu as pltpu

F32 = jnp.float32
BF16 = jnp.bfloat16

N_DEV = 8
N_META = 16
CHUNK = 128
PADN = CHUNK - N_META
HEAD_DIM = 64
SSD_GROUPS = 4
CONV_K = 4
EPS = 1e-6
NEG = -1e30
LANES = 128
HALO = 16

ADAM_LR = 0.001
ADAM_B1 = 0.9
ADAM_B2 = 0.999
ADAM_EPS = 1e-08
ADAM_WD = 0.01
ADAM_STEP = 10

VMEM_LIMIT = 56 * 1024 * 1024

NN = (((1,), (0,)), ((), ()))
NT = (((1,), (1,)), ((), ()))
TN = (((0,), (0,)), ((), ()))
MESH = pl.DeviceIdType.MESH


def _dot(a, b, dims=NN):
    return lax.dot_general(a, b, dims, preferred_element_type=F32)


def _split2(x):
    hi = x.astype(BF16)
    lo = (x - hi.astype(F32)).astype(BF16)
    return hi, lo


def _dot_sel(x, sel):
    hi, lo = _split2(x)
    return _dot(hi, sel) + _dot(lo, sel)


def _dot_tri(tri, x):
    h1 = x.astype(BF16)
    r1 = x - h1.astype(F32)
    h2 = r1.astype(BF16)
    h3 = (r1 - h2.astype(F32)).astype(BF16)
    return _dot(tri, h1) + _dot(tri, h2) + _dot(tri, h3)


def _sigmoid(x):
    return 0.5 * jnp.tanh(0.5 * x) + 0.5


def _softplus(x):
    return jnp.maximum(x, 0.0) + jnp.log(1.0 + jnp.exp(-jnp.abs(x)))


def _cparams(sem=None, vmem=VMEM_LIMIT):
    kw = {"vmem_limit_bytes": vmem}
    if sem is not None:
        kw["dimension_semantics"] = sem
    return pltpu.CompilerParams(**kw)


def _full(shape):
    nd = len(shape)
    return pl.BlockSpec(shape, lambda *_: (0,) * nd)


def _att_block(p):
    return 384 if p % 384 == 0 else CHUNK


def _my_pos():
    return lax.axis_index("x"), lax.axis_index("y"), lax.axis_index("c")


def _dev_index(x, y, c):
    return 4 * x + 2 * y + c


FLIPS = [(fx, fy, fc) for fx in (0, 1) for fy in (0, 1) for fc in (0, 1)][1:]


def _flip(pos, f):
    return tuple((1 - p) if fi else p for p, fi in zip(pos, f))


def _all_gather(bufs, name):
    nb = len(bufs)

    def body(*refs):
        ins, outs = refs[:nb], refs[nb:2 * nb]
        send_sems, recv_sems, local_sems = refs[2 * nb:]
        x, y, c = _my_pos()
        me = _dev_index(x, y, c)
        sibling = (x, y, 1 - c)
        near = [(1 - x, y), (x, 1 - y)]
        far = (1 - x, 1 - y)
        relay_from = (c * (1 - x) + (1 - c) * x, c * y + (1 - c) * (1 - y))
        relay_to = (c * x + (1 - c) * (1 - x), c * (1 - y) + (1 - c) * y)

        def copy(b, k, block_idx, to, src=None):
            dst = outs[b].at[block_idx]
            return pltpu.make_async_remote_copy(
                src_ref=dst if src is None else src, dst_ref=dst,
                send_sem=send_sems.at[b, k], recv_sem=recv_sems.at[b, k],
                device_id=to, device_id_type=MESH)

        started = []
        for b in range(nb):
            mine = pltpu.make_async_copy(ins[b], outs[b].at[me], local_sems.at[b])
            mine.start()
            started.append(mine)
        sent = []
        for b in range(nb):
            sent.append(copy(b, 0, me, sibling, src=ins[b]))
            for j, chip in enumerate(near):
                sent.append(copy(b, 1 + j, me, (chip[0], chip[1], c), src=ins[b]))
        for cp in sent:
            cp.start()
        for j, chip in enumerate(near):
            blk = _dev_index(chip[0], chip[1], c)
            for b in range(nb):
                copy(b, 1 + j, blk, (x, y, c)).wait_recv()
                sent.append(copy(b, 4 + j, blk, sibling))
                sent[-1].start()
        for b in range(nb):
            sent.append(copy(b, 3, _dev_index(relay_from[0], relay_from[1], c), (relay_to[0], relay_to[1], c)))
            sent[-1].start()
        blk = _dev_index(far[0], far[1], c)
        for b in range(nb):
            copy(b, 3, blk, (x, y, c)).wait_recv()
            sent.append(copy(b, 6, blk, sibling))
            sent[-1].start()
        for b in range(nb):
            copy(b, 0, _dev_index(x, y, 1 - c), (x, y, c)).wait_recv()
        for j, chip in enumerate(near + [far]):
            blk = _dev_index(chip[0], chip[1], 1 - c)
            for b in range(nb):
                copy(b, 4 + j, blk, (x, y, c)).wait_recv()
        for cp in sent:
            cp.wait_send()
        for mine in started:
            mine.wait()

    any_spec = pl.BlockSpec(memory_space=pl.ANY)
    return pl.pallas_call(
        body, name=name,
        out_shape=[jax.ShapeDtypeStruct((N_DEV,) + b.shape, b.dtype) for b in bufs],
        in_specs=[any_spec] * nb, out_specs=[any_spec] * nb,
        scratch_shapes=[pltpu.SemaphoreType.DMA((nb, 7)), pltpu.SemaphoreType.DMA((nb, 7)),
                        pltpu.SemaphoreType.DMA((nb,))],
    )(*bufs)


N_CHIP = 4
CHIP_FLIPS = [(1, 0), (0, 1), (1, 1)]


def _exchange_sibling(bufs, name):
    nb = len(bufs)

    def body(*refs):
        ins, outs = refs[:nb], refs[nb:2 * nb]
        send_sems, recv_sems = refs[2 * nb:]
        x, y, c = _my_pos()

        def copy(b, k):
            return pltpu.make_async_remote_copy(
                src_ref=ins[b].at[2 * k + (1 - c)], dst_ref=outs[b].at[k],
                send_sem=send_sems.at[b, k], recv_sem=recv_sems.at[b, k],
                device_id=(x, y, 1 - c), device_id_type=MESH)

        cps = [copy(b, k) for b in range(nb) for k in range(N_CHIP)]
        for cp in cps:
            cp.start()
        for cp in cps:
            cp.wait()

    any_spec = pl.BlockSpec(memory_space=pl.ANY)
    return pl.pallas_call(
        body, name=name,
        out_shape=[jax.ShapeDtypeStruct((N_CHIP,) + b.shape[1:], b.dtype) for b in bufs],
        in_specs=[any_spec] * nb, out_specs=[any_spec] * nb,
        scratch_shapes=[pltpu.SemaphoreType.DMA((nb, N_CHIP)), pltpu.SemaphoreType.DMA((nb, N_CHIP))],
    )(*bufs)


def _pair_add(mine, recv, core, name):
    _, r, cdim = mine.shape
    tr, tc = r, cdim
    pick = lambda i: (i, 0)

    def body(core_ref, a_ref, b_ref, o_ref):
        o_ref[0] = (a_ref[0].astype(F32) + b_ref[0].astype(F32)).astype(o_ref.dtype)

    return pl.pallas_call(
        body, name=name,
        grid_spec=pltpu.PrefetchScalarGridSpec(
            num_scalar_prefetch=1, grid=(N_CHIP, (r // tr) * (cdim // tc)),
            in_specs=[pl.BlockSpec((1, tr, tc), lambda k, i, core_ref: (2 * k + core_ref[0],) + pick(i)),
                      pl.BlockSpec((1, tr, tc), lambda k, i, core_ref: (k,) + pick(i))],
            out_specs=pl.BlockSpec((1, tr, tc), lambda k, i, core_ref: (k,) + pick(i))),
        out_shape=jax.ShapeDtypeStruct((N_CHIP, r, cdim), mine.dtype),
        compiler_params=_cparams(("parallel", "parallel")),
    )(core, mine, recv)


def _chip_peer(x, y, f):
    return ((1 - x) if f[0] else x), ((1 - y) if f[1] else y)


def _exchange_chips_start(bufs, name):
    nb = len(bufs)
    nsem = 2 * 3 * nb

    def body(*refs):
        ins, lands = refs[:nb], refs[nb:2 * nb]
        sems = refs[2 * nb:2 * nb + nsem]
        token = refs[-1]
        x, y, c = _my_pos()
        for b in range(nb):
            for j, f in enumerate(CHIP_FLIPS):
                px, py = _chip_peer(x, y, f)
                pltpu.make_async_remote_copy(
                    src_ref=ins[b].at[2 * px + py], dst_ref=lands[b].at[2 * x + y],
                    send_sem=sems[2 * (3 * b + j)], recv_sem=sems[2 * (3 * b + j) + 1],
                    device_id=(px, py, c), device_id_type=MESH).start()
        token[...] = jnp.zeros_like(token)

    hbm = pl.BlockSpec(memory_space=pltpu.HBM)
    sem = pl.BlockSpec(memory_space=pltpu.SEMAPHORE)
    out = pl.pallas_call(
        body, name=name,
        out_shape=(*([pltpu.SemaphoreType.DMA(())] * nsem),
                   *[pltpu.HBM(b.shape, b.dtype) for b in bufs], *[pltpu.HBM(b.shape, b.dtype) for b in bufs],
                   jax.ShapeDtypeStruct((8, LANES), F32)),
        in_specs=[hbm] * (2 * nb),
        out_specs=(*([sem] * nsem), *([hbm] * (2 * nb)), pl.BlockSpec(memory_space=pltpu.VMEM)),
        input_output_aliases={i: nsem + i for i in range(2 * nb)},
        compiler_params=pltpu.CompilerParams(has_side_effects=pltpu.SideEffectType.DATAFLOW_SIDE_EFFECTING),
    )(*[pltpu.with_memory_space_constraint(b, pltpu.HBM) for b in bufs],
      *[pltpu.with_memory_space_constraint(lax.empty(b.shape, b.dtype), pltpu.HBM) for b in bufs])
    return out[:nsem], out[nsem:nsem + nb], out[nsem + nb:nsem + 2 * nb], out[-1]


def _exchange_chips_wait(sems, thru, lands, after, name):
    nb = len(thru)
    nsem = len(sems)

    def body(*refs):
        ins, lnd = refs[:nb], refs[nb:2 * nb]
        sem_refs = refs[2 * nb:2 * nb + nsem]
        x, y, c = _my_pos()
        for b in range(nb):
            for j, f in enumerate(CHIP_FLIPS):
                px, py = _chip_peer(x, y, f)
                cp = pltpu.make_async_remote_copy(
                    src_ref=ins[b].at[2 * px + py], dst_ref=lnd[b].at[2 * px + py],
                    send_sem=sem_refs[2 * (3 * b + j)], recv_sem=sem_refs[2 * (3 * b + j) + 1],
                    device_id=(px, py, c), device_id_type=MESH)
                cp.wait_send()
                cp.wait_recv()

    hbm = pl.BlockSpec(memory_space=pltpu.HBM)
    sem = pl.BlockSpec(memory_space=pltpu.SEMAPHORE)
    out = pl.pallas_call(
        body, name=name,
        out_shape=tuple([pltpu.HBM(b.shape, b.dtype) for b in thru] + [pltpu.HBM(b.shape, b.dtype) for b in lands]),
        in_specs=[hbm] * (2 * nb) + [sem] * nsem + [pl.BlockSpec(memory_space=pl.ANY)],
        out_specs=tuple([hbm] * (2 * nb)),
        input_output_aliases={i: i for i in range(2 * nb)},
        compiler_params=pltpu.CompilerParams(has_side_effects=pltpu.SideEffectType.DATAFLOW_SIDE_EFFECTING),
    )(*thru, *lands, *sems, after)
    return out[:nb], out[nb:]


def _bcast_start(buf, name):
    nsem = 2 * len(FLIPS)

    def body(src, land, *rest):
        sems, token = rest[:nsem], rest[-1]
        pos = _my_pos()
        for k, f in enumerate(FLIPS):
            pltpu.make_async_remote_copy(
                src_ref=src, dst_ref=land.at[_dev_index(*pos)], send_sem=sems[2 * k], recv_sem=sems[2 * k + 1],
                device_id=_flip(pos, f), device_id_type=MESH).start()
        token[...] = jnp.zeros_like(token)

    hbm = pl.BlockSpec(memory_space=pltpu.HBM)
    sem = pl.BlockSpec(memory_space=pltpu.SEMAPHORE)
    land_shape = (N_DEV,) + buf.shape
    out = pl.pallas_call(
        body, name=name,
        out_shape=(*([pltpu.SemaphoreType.DMA(())] * nsem), pltpu.HBM(buf.shape, buf.dtype),
                   pltpu.HBM(land_shape, buf.dtype), jax.ShapeDtypeStruct((8, LANES), F32)),
        in_specs=[hbm, hbm],
        out_specs=(*([sem] * nsem), hbm, hbm, pl.BlockSpec(memory_space=pltpu.VMEM)),
        input_output_aliases={0: nsem, 1: nsem + 1},
        compiler_params=pltpu.CompilerParams(has_side_effects=pltpu.SideEffectType.DATAFLOW_SIDE_EFFECTING),
    )(pltpu.with_memory_space_constraint(buf, pltpu.HBM),
      pltpu.with_memory_space_constraint(lax.empty(land_shape, buf.dtype), pltpu.HBM))
    return out[:nsem], out[nsem], out[nsem + 1], out[-1]


def _bcast_wait(sems, thru, land, after, name):
    nsem = len(sems)

    def body(src, lnd, *rest):
        sem_refs = rest[:nsem]
        pos = _my_pos()
        for k, f in enumerate(FLIPS):
            peer = _flip(pos, f)
            cp = pltpu.make_async_remote_copy(
                src_ref=src, dst_ref=lnd.at[_dev_index(*peer)], send_sem=sem_refs[2 * k],
                recv_sem=sem_refs[2 * k + 1], device_id=peer, device_id_type=MESH)
            cp.wait_send()
            cp.wait_recv()

    hbm = pl.BlockSpec(memory_space=pltpu.HBM)
    sem = pl.BlockSpec(memory_space=pltpu.SEMAPHORE)
    sent, got = pl.pallas_call(
        body, name=name,
        out_shape=(pltpu.HBM(thru.shape, thru.dtype), pltpu.HBM(land.shape, land.dtype)),
        in_specs=[hbm, hbm] + [sem] * nsem + [pl.BlockSpec(memory_space=pl.ANY)],
        out_specs=(hbm, hbm), input_output_aliases={0: 0, 1: 1},
        compiler_params=pltpu.CompilerParams(has_side_effects=pltpu.SideEffectType.DATAFLOW_SIDE_EFFECTING),
    )(thru, land, *sems, after)
    return lax.dynamic_update_slice_in_dim(got, sent[None], _dev_index(*_my_pos()), axis=0)


def _sum_slots(v, name):
    _, r, cdim = v.shape

    def body(v_ref, o_ref):
        acc = v_ref[0]
        for s in range(1, N_DEV):
            acc = acc + v_ref[s]
        o_ref[...] = acc

    return pl.pallas_call(
        body, name=name, out_shape=jax.ShapeDtypeStruct((r, cdim), F32),
        in_specs=[_full((N_DEV, r, cdim))], out_specs=_full((r, cdim)), grid=(1,),
        compiler_params=_cparams(("arbitrary",)),
    )(v)


def _mm(a, b, dims, out_dtype, tm, tn, name):
    if dims == "nn":
        (m, k), (_, n) = a.shape, b.shape
        a_spec = pl.BlockSpec((tm, k), lambda j, i: (i, 0))
        b_spec = pl.BlockSpec((k, tn), lambda j, i: (0, j))
        dn = NN
    elif dims == "nt":
        (m, k), (n, _) = a.shape, b.shape
        a_spec = pl.BlockSpec((tm, k), lambda j, i: (i, 0))
        b_spec = pl.BlockSpec((tn, k), lambda j, i: (j, 0))
        dn = NT
    else:
        (k, m), (_, n) = a.shape, b.shape
        a_spec = pl.BlockSpec((k, tm), lambda j, i: (0, i))
        b_spec = pl.BlockSpec((k, tn), lambda j, i: (0, j))
        dn = TN
    assert m % tm == 0 and n % tn == 0, (m, tm, n, tn)

    def body(a_ref, b_ref, o_ref):
        o_ref[...] = _dot(a_ref[...], b_ref[...], dn).astype(o_ref.dtype)

    return pl.pallas_call(
        body, name=name, grid=(n // tn, m // tm),
        in_specs=[a_spec, b_spec], out_specs=pl.BlockSpec((tm, tn), lambda j, i: (i, j)),
        out_shape=jax.ShapeDtypeStruct((m, n), out_dtype),
        compiler_params=_cparams(("parallel", "parallel")),
    )(a, b)


def _tiles_2d(r, cdim):
    if r % CHUNK == 0:
        return CHUNK, cdim, True
    return r, _tile(cdim, (256, 128)), False


def _dgrad_prenorm(a_list, b_list, head, x2, w, dout, tm, name):
    n_op = len(a_list)
    m, d = a_list[0].shape[0], b_list[0].shape[1]
    subs = _x_row_specs(tm, d)
    last = m // tm - 1
    rest = tm - CHUNK

    def body(*refs):
        a_refs, b_refs = refs[:n_op], refs[n_op:2 * n_op]
        head_ref = refs[2 * n_op]
        x_refs = refs[2 * n_op + 1:2 * n_op + 1 + len(subs)]
        w_ref, dout_ref, gx_ref, ghead_ref, gw_ref, dh_buf, sem = refs[2 * n_op + 1 + len(subs):]
        i = pl.program_id(0)

        def first_copy():
            return pltpu.make_async_copy(dh_buf.at[pl.ds(CHUNK, rest)], gx_ref.at[pl.ds(0, rest)], sem)

        def later_copy(step):
            return pltpu.make_async_copy(dh_buf, gx_ref.at[pl.ds(pl.multiple_of(step * tm - CHUNK, CHUNK), tm)], sem)

        @pl.when(i == 0)
        def _():
            gw_ref[...] = jnp.zeros_like(gw_ref)

        du = _dot(a_refs[0][...], b_refs[0][...])
        for k in range(1, n_op):
            du = du + _dot(a_refs[k][...], b_refs[k][...])
        first = jnp.where(i == 0, head_ref[...], x_refs[0][...])
        h = jnp.concatenate([first] + [r[...] for r in x_refs[1:]], axis=0)
        rstd = lax.rsqrt(jnp.mean(h * h, axis=-1, keepdims=True) + EPS)
        xhat = h * rstd
        dxh = du * w_ref[...]
        dh = rstd * (dxh - xhat * jnp.mean(dxh * xhat, axis=-1, keepdims=True)) + dout_ref[...]
        gw_ref[0:1, :] += jnp.sum(du * xhat, axis=0, keepdims=True)

        if rest and last >= 1:
            @pl.when(i == 1)
            def _():
                first_copy().wait()

        @pl.when(i >= (2 if rest else 1))
        def _():
            later_copy(i - 1).wait()

        dh_buf[...] = dh

        @pl.when(i == 0)
        def _():
            ghead_ref[...] = dh_buf[0:CHUNK, :]
            if rest:
                first_copy().start()
                if last == 0:
                    first_copy().wait()

        @pl.when(i >= 1)
        def _():
            later_copy(i).start()

        if last >= 1:
            @pl.when(i == last)
            def _():
                later_copy(i).wait()

    once = lambda b: pl.BlockSpec(b.shape, lambda i: (0, 0), pipeline_mode=pl.Buffered(1))
    row = lambda width: pl.BlockSpec((tm, width), lambda i: (i, 0))
    return pl.pallas_call(
        body, name=name, grid=(m // tm,),
        in_specs=([row(a.shape[1]) for a in a_list] + [once(b) for b in b_list]
                  + [_full((CHUNK, d))] + subs + [_full((1, d)), row(d)]),
        out_specs=[pl.BlockSpec(memory_space=pl.ANY), _full((CHUNK, d)), _full((8, d))],
        out_shape=[jax.ShapeDtypeStruct((m - CHUNK, d), F32), jax.ShapeDtypeStruct((CHUNK, d), F32),
                   jax.ShapeDtypeStruct((8, d), F32)],
        scratch_shapes=[pltpu.VMEM((tm, d), F32), pltpu.SemaphoreType.DMA],
        compiler_params=_cparams(("arbitrary",)),
    )(*a_list, *b_list, head, *([x2] * len(subs)), w, dout)


def _tile(n, prefs):
    for t in prefs:
        if n % t == 0:
            return t
    return n


def _rows3(i):
    return jnp.maximum(3 * i - 1, 0), 3 * i, 3 * i + 1


def _x_row_specs(tm, d):
    if tm == CHUNK:
        return [pl.BlockSpec((CHUNK, d), lambda i: (jnp.maximum(i - 1, 0), 0))]
    return [pl.BlockSpec((CHUNK, d), functools.partial(lambda i, k: (_rows3(i)[k], 0), k=k)) for k in range(3)]


def _prenorm_fwd(head, x2, w, tm):
    p, d = x2.shape[0] + CHUNK, x2.shape[1]
    subs = _x_row_specs(tm, d)

    def body(head_ref, *rest):
        x_refs, (w_ref, u_ref) = rest[:len(subs)], rest[len(subs):]
        i = pl.program_id(0)
        first = jnp.where(i == 0, head_ref[...], x_refs[0][...])
        h = jnp.concatenate([first] + [r[...] for r in x_refs[1:]], axis=0)
        ms = jnp.mean(h * h, axis=-1, keepdims=True)
        u_ref[...] = (h * lax.rsqrt(ms + EPS) * w_ref[...]).astype(BF16)

    return pl.pallas_call(
        body, name="prenorm_fwd", grid=(p // tm,),
        in_specs=[_full((CHUNK, d))] + subs + [_full((1, d))],
        out_specs=pl.BlockSpec((tm, d), lambda i: (i, 0)),
        out_shape=jax.ShapeDtypeStruct((p, d), BF16),
        compiler_params=_cparams(("arbitrary",)),
    )(head, *([x2] * len(subs)), w)


def _conv_pre(ext_ref, cw_ref, cb_ref):
    pre = cb_ref[...] + cw_ref[CONV_K - 1:CONV_K, :] * ext_ref[8:8 + CHUNK, :]
    for j in range(1, CONV_K):
        pre = pre + cw_ref[CONV_K - 1 - j:CONV_K - j, :] * ext_ref[8 - j:8 - j + CHUNK, :]
    return pre


def _ssd_scalars(dtf_ref, brow_ref, alog_ref, rowmask, hs, ha, tri):
    lane = lax.broadcasted_iota(jnp.int32, (1, LANES), 1)
    is_dt = lane < hs
    is_f = (lane >= hs) & (lane < hs + ha)
    dtr = dtf_ref[...] + brow_ref[...]
    sp = _softplus(dtr)
    dt = jnp.where(is_dt, sp, 0.0) * rowmask
    logf = jnp.where(is_f, jnp.minimum(dtr, 0.0) - jnp.log(1.0 + jnp.exp(-jnp.abs(dtr))), 0.0) * rowmask
    a_row = jnp.where(is_dt, -jnp.exp(alog_ref[...]), 0.0)
    run = _dot_tri(tri, dt * a_row + logf)
    return dtr, dt, a_row, run, is_dt, is_f


def _tri_mats():
    r = lax.broadcasted_iota(jnp.int32, (CHUNK, CHUNK), 0)
    c = lax.broadcasted_iota(jnp.int32, (CHUNK, CHUNK), 1)
    return r, c


def _ssd_fwd(xbc, z, dtf, conv_w, conv_b, brow, alog, dskip_l, ssd_norm, sel_t, hs, ha):
    p, cd = xbc.shape
    ds = z.shape[1]
    ns = (cd - ds) // (2 * SSD_GROUPS)
    gw = ds // SSD_GROUPS
    nch = p // CHUNK
    hpg = hs // SSD_GROUPS

    def body(xbc_ref, halo_ref, z_ref, dtf_ref, cw_ref, cb_ref, brow_ref, alog_ref, dsk_ref, nrm_ref, selt_ref,
             y_ref, yssd_ref, hin_ref, cf_ref, pre_ref, st_ref, carry_ref, yacc_ref, xc_s, ex_s, xdtb_s, xwb_s, ext_s):
        c = pl.program_id(0)

        @pl.when(c == 0)
        def _():
            st_ref[...] = jnp.zeros_like(st_ref)
            carry_ref[...] = jnp.zeros_like(carry_ref)

        rows = lax.broadcasted_iota(jnp.int32, (CHUNK, 1), 0)
        rowmask = jnp.where((rows >= PADN) | (c > 0), 1.0, 0.0)
        ri, ci = _tri_mats()
        causal = ri >= ci
        tri = jnp.where(causal, 1.0, 0.0).astype(BF16)

        ext_s[0:8, :] = halo_ref[...].astype(F32)[HALO - 8:, :] * jnp.where(c > 0, 1.0, 0.0)
        ext_s[8:, :] = xbc_ref[...].astype(F32)
        pre = _conv_pre(ext_s, cw_ref, cb_ref)
        pre_ref[...] = pre.astype(BF16)
        xc_s[...] = pre * _sigmoid(pre) * rowmask

        dtr, dt, a_row, run, is_dt, is_f = _ssd_scalars(dtf_ref, brow_ref, alog_ref, rowmask, hs, ha, tri)
        cf = run + carry_ref[...]
        cf_ref[...] = cf
        carry_ref[...] = jnp.where(is_f, cf[CHUNK - 1:CHUNK, :], 0.0)
        cs = jnp.where(is_dt, run, 0.0)
        cl = cs[CHUNK - 1:CHUNK, :]
        selt = selt_ref[...]
        ex_s[...] = _dot_sel(jnp.exp(cs), selt)
        cdec_x = _dot_sel(jnp.broadcast_to(jnp.exp(cl), (8, LANES)), selt)[0:1, :]
        cs_t = cs.T
        xdt = xc_s[:, :ds] * _dot_sel(dt, selt)
        xdtb_s[...] = xdt.astype(BF16)
        xwb_s[...] = (xdt * _dot_sel(jnp.exp(cl - cs), selt)).astype(BF16)

        lane = lax.broadcasted_iota(jnp.int32, (1, LANES), 1)
        half0 = lane < HEAD_DIM
        for g in range(SSD_GROUPS):
            bg = xc_s[:, ds + g * ns: ds + (g + 1) * ns].astype(BF16)
            cg = xc_s[:, ds + SSD_GROUPS * ns + g * ns: ds + SSD_GROUPS * ns + (g + 1) * ns].astype(BF16)
            gm = _dot(cg, bg, NT)
            gs = slice(g * gw, (g + 1) * gw)
            stg = st_ref[:, gs]
            stg_b = stg.astype(BF16)
            hin_ref[0, :, gs] = stg_b
            yoff = _dot(cg, stg_b) * ex_s[:, gs]
            for pr in range(gw // LANES):
                sl = slice(g * gw + pr * LANES, g * gw + (pr + 1) * LANES)
                xp = xdtb_s[:, sl]
                yd = jnp.zeros((CHUNK, LANES), F32)
                for j in range(2):
                    h = g * hpg + 2 * pr + j
                    seg = cs[:, h:h + 1] - cs_t[h:h + 1, :]
                    m = jnp.where(causal, gm * jnp.exp(jnp.minimum(seg, 0.0)), 0.0).astype(BF16)
                    sel = half0 if j == 0 else jnp.logical_not(half0)
                    yd = yd + _dot(m, jnp.where(sel, xp, jnp.zeros_like(xp)))
                yacc_ref[:, sl] = yd + yoff[:, pr * LANES:(pr + 1) * LANES] + dsk_ref[:, sl] * xc_s[:, sl]
            st_ref[:, gs] = stg * cdec_x[:, gs] + _dot(bg, xwb_s[:, gs], TN)

        y = yacc_ref[...]
        y_ref[...] = y.astype(BF16)
        zf = z_ref[...].astype(F32)
        u = y * zf * _sigmoid(zf)
        for g in range(SSD_GROUPS):
            gs = slice(g * gw, (g + 1) * gw)
            ug = u[:, gs]
            ms = jnp.mean(ug * ug, axis=-1, keepdims=True)
            yssd_ref[:, gs] = (ug * lax.rsqrt(ms + EPS) * nrm_ref[:, gs]).astype(BF16)

    rb = CHUNK // HALO
    return pl.pallas_call(
        body, name="ssd_fwd", grid=(nch,),
        in_specs=[pl.BlockSpec((CHUNK, cd), lambda c: (c, 0)),
                  pl.BlockSpec((HALO, cd), lambda c: (jnp.maximum(c * rb - 1, 0), 0)),
                  pl.BlockSpec((CHUNK, ds), lambda c: (c, 0)),
                  pl.BlockSpec((CHUNK, LANES), lambda c: (c, 0)),
                  _full((CONV_K, cd)), _full((1, cd)), _full((1, LANES)), _full((1, LANES)),
                  _full((1, ds)), _full((1, ds)), _full((LANES, ds))],
        out_specs=[pl.BlockSpec((CHUNK, ds), lambda c: (c, 0)), pl.BlockSpec((CHUNK, ds), lambda c: (c, 0)),
                   pl.BlockSpec((1, ns, ds), lambda c: (c, 0, 0)), pl.BlockSpec((CHUNK, LANES), lambda c: (c, 0)),
                   pl.BlockSpec((CHUNK, cd), lambda c: (c, 0))],
        out_shape=[jax.ShapeDtypeStruct((p, ds), BF16), jax.ShapeDtypeStruct((p, ds), BF16),
                   jax.ShapeDtypeStruct((nch, ns, ds), BF16), jax.ShapeDtypeStruct((p, LANES), F32),
                   jax.ShapeDtypeStruct((p, cd), BF16)],
        scratch_shapes=[pltpu.VMEM((ns, ds), F32), pltpu.VMEM((1, LANES), F32), pltpu.VMEM((CHUNK, ds), F32),
                        pltpu.VMEM((CHUNK, cd), F32), pltpu.VMEM((CHUNK, ds), F32),
                        pltpu.VMEM((CHUNK, ds), BF16), pltpu.VMEM((CHUNK, ds), BF16),
                        pltpu.VMEM((8 + CHUNK, cd), F32)],
        compiler_params=_cparams(("arbitrary",)),
    )(xbc, xbc, z, dtf, conv_w, conv_b, brow, alog, dskip_l, ssd_norm, sel_t)


def _ssd_bwd(dyssd, y, z, xbc, pre, dtf, hin, dcf, conv_w, brow, alog, dskip_l, ssd_norm, sel_t, sel, hs, ha):
    p, cd = xbc.shape
    ds = z.shape[1]
    ns = (cd - ds) // (2 * SSD_GROUPS)
    gw = ds // SSD_GROUPS
    nch = p // CHUNK
    hpg = hs // SSD_GROUPS

    def body(dyssd_ref, y_ref, z_ref, xbc_ref, pre_ref, dtf_ref, hin_ref, dcf_ref, cw_ref, brow_ref,
             alog_ref, dsk_ref, nrm_ref, selt_ref, sel_ref,
             dxbc_ref, dz_ref, ddtf_ref, gcw_ref, gcb_ref, gnrm_ref, gsm_ref,
             dst_ref, nxt_ref, fcar_ref, gdsk_ref, dxc_ref, xc_s, dsl_s, dtx_s, ex_s, wx_s, dy_s, xdtb_s, xwb_s,
             dyb_s, dyeb_s):
        step = pl.program_id(0)
        c = nch - 1 - step

        @pl.when(step == 0)
        def _():
            dst_ref[...] = jnp.zeros_like(dst_ref)
            nxt_ref[...] = jnp.zeros_like(nxt_ref)
            fcar_ref[...] = jnp.zeros_like(fcar_ref)
            gdsk_ref[...] = jnp.zeros_like(gdsk_ref)
            gcw_ref[...] = jnp.zeros_like(gcw_ref)
            gcb_ref[...] = jnp.zeros_like(gcb_ref)
            gnrm_ref[...] = jnp.zeros_like(gnrm_ref)
            gsm_ref[...] = jnp.zeros_like(gsm_ref)

        rows = lax.broadcasted_iota(jnp.int32, (CHUNK, 1), 0)
        rowmask = jnp.where((rows >= PADN) | (c > 0), 1.0, 0.0)
        ri, ci = _tri_mats()
        causal = ri >= ci
        anti = ci >= ri
        tri = jnp.where(causal, 1.0, 0.0).astype(BF16)
        rtri = jnp.where(anti, 1.0, 0.0).astype(BF16)

        pre = pre_ref[...].astype(F32)
        sg = _sigmoid(pre)
        xc_s[...] = pre * sg * rowmask
        dsl_s[...] = sg * (1.0 + pre * (1.0 - sg)) * rowmask

        dtr, dt, a_row, run, is_dt, is_f = _ssd_scalars(dtf_ref, brow_ref, alog_ref, rowmask, hs, ha, tri)
        cs = jnp.where(is_dt, run, 0.0)
        cl = cs[CHUNK - 1:CHUNK, :]
        selt = selt_ref[...]
        selm = sel_ref[...]
        dtx_s[...] = _dot_sel(dt, selt)
        ex_s[...] = _dot_sel(jnp.exp(cs), selt)
        wx_s[...] = _dot_sel(jnp.exp(cl - cs), selt)
        cdec = jnp.exp(cl)
        cdec_x = _dot_sel(jnp.broadcast_to(cdec, (8, LANES)), selt)[0:1, :]
        cs_t = cs.T
        xdt = xc_s[:, :ds] * dtx_s[...]
        xdtb_s[...] = xdt.astype(BF16)
        xwb_s[...] = (xdt * wx_s[...]).astype(BF16)

        yv = y_ref[...].astype(F32)
        zf = z_ref[...].astype(F32)
        sz = _sigmoid(zf)
        u = yv * zf * sz
        dyo = dyssd_ref[...].astype(F32)
        du_parts = []
        for g in range(SSD_GROUPS):
            gs = slice(g * gw, (g + 1) * gw)
            ug = u[:, gs]
            rstd = lax.rsqrt(jnp.mean(ug * ug, axis=-1, keepdims=True) + EPS)
            yhat = ug * rstd
            dyg = dyo[:, gs]
            gnrm_ref[0:1, gs] += jnp.sum(dyg * yhat, axis=0, keepdims=True)
            dyh = dyg * nrm_ref[:, gs]
            du_parts.append(rstd * (dyh - yhat * jnp.mean(dyh * yhat, axis=-1, keepdims=True)))
        du = jnp.concatenate(du_parts, axis=1)
        dy = du * zf * sz
        dz_ref[...] = (du * yv * sz * (1.0 + zf * (1.0 - sz))).astype(BF16)
        dy_s[...] = dy
        dyb_s[...] = dy.astype(BF16)
        dyeb_s[...] = (dy * ex_s[...]).astype(BF16)
        gdsk_ref[...] += jnp.sum(dy * xc_s[:, :ds], axis=0, keepdims=True)
        lane = lax.broadcasted_iota(jnp.int32, (1, LANES), 1)
        half0 = lane < HEAD_DIM
        x_parts, yo_parts, t4_parts = [], [], []
        dcs = jnp.zeros((CHUNK, LANES), F32)
        for g in range(SSD_GROUPS):
            gs = slice(g * gw, (g + 1) * gw)
            bsl = slice(ds + g * ns, ds + (g + 1) * ns)
            csl = slice(ds + SSD_GROUPS * ns + g * ns, ds + SSD_GROUPS * ns + (g + 1) * ns)
            bg = xc_s[:, bsl].astype(BF16)
            cg = xc_s[:, csl].astype(BF16)
            gm = _dot(cg, bg, NT)
            gm_t = _dot(bg, cg, NT)
            stg_b = hin_ref[0, :, gs]
            dstg = dst_ref[:, gs]
            dstg_b = dstg.astype(BF16)
            t4_parts.append(jnp.sum(dstg * stg_b.astype(F32), axis=0, keepdims=True))
            zst = _dot(bg, dstg_b) * wx_s[:, gs]
            x_parts.append(xc_s[:, gs] * dtx_s[:, gs] * zst)
            yo_parts.append(dy_s[:, gs] * (_dot(cg, stg_b) * ex_s[:, gs]))
            dgsum = jnp.zeros((CHUNK, CHUNK), F32)
            dgtsum = jnp.zeros((CHUNK, CHUNK), F32)
            for pr in range(gw // LANES):
                sl = slice(g * gw + pr * LANES, g * gw + (pr + 1) * LANES)
                xp = xdtb_s[:, sl]
                dyp = dyb_s[:, sl]
                dxd = zst[:, pr * LANES:(pr + 1) * LANES]
                for j in range(2):
                    h = g * hpg + 2 * pr + j
                    sel_l = half0 if j == 0 else jnp.logical_not(half0)
                    seg = cs[:, h:h + 1] - cs_t[h:h + 1, :]
                    lm = jnp.where(causal, jnp.exp(jnp.minimum(seg, 0.0)), 0.0)
                    lmt = lm.T
                    dyp_m = jnp.where(sel_l, dyp, jnp.zeros_like(dyp))
                    xp_m = jnp.where(sel_l, xp, jnp.zeros_like(xp))
                    dxd = dxd + _dot((gm_t * lmt).astype(BF16), dyp_m)
                    dg = _dot(dyp_m, xp, NT) * lm
                    dgt = _dot(xp_m, dyp, NT) * lmt
                    dgsum = dgsum + dg
                    dgtsum = dgtsum + dgt
                    qrow = (jnp.sum(dg * gm, axis=1, keepdims=True) - jnp.sum(dgt * gm_t, axis=1, keepdims=True))
                    dcs = dcs + jnp.where(lane == h, qrow, 0.0)
                dxc_ref[:, sl] = dxd
            dxc_ref[:, csl] = _dot(dgsum.astype(BF16), bg) + _dot(dyeb_s[:, gs], stg_b, NT)
            dxc_ref[:, bsl] = _dot(dgtsum.astype(BF16), cg) + _dot(xwb_s[:, gs], dstg_b, NT)
            dst_ref[:, gs] = dstg * cdec_x[:, gs] + _dot(cg, dyeb_s[:, gs], TN)

        dxdt = dxc_ref[:, :ds]
        xst = _dot_sel(jnp.concatenate(x_parts, axis=1), selm)
        yo = _dot_sel(jnp.concatenate(yo_parts, axis=1), selm)
        t4 = _dot_sel(jnp.concatenate([jnp.concatenate(t4_parts, axis=1), jnp.zeros((7, ds), F32)], axis=0), selm)
        dcl = jnp.sum(xst, axis=0, keepdims=True) + cdec * t4[0:1, :]
        dcs = dcs + yo - xst + jnp.where(rows == CHUNK - 1, dcl, 0.0)
        da_ = _dot_tri(rtri, dcs)
        ddt = _dot_sel(dxdt * xc_s[:, :ds], selm) + da_ * a_row
        dcf_blk = dcf_ref[...]
        dlogf = _dot_tri(rtri, dcf_blk) + fcar_ref[...]
        fcar_ref[...] += jnp.sum(dcf_blk, axis=0, keepdims=True)
        sgd = _sigmoid(dtr)
        ddtf = (jnp.where(is_dt, ddt * sgd, 0.0) + jnp.where(is_f, dlogf * (1.0 - sgd), 0.0)) * rowmask
        ddtf_ref[...] = ddtf
        gsm_ref[0:1, :] += jnp.sum(ddtf, axis=0, keepdims=True)
        gsm_ref[1:2, :] += jnp.sum(da_ * dt, axis=0, keepdims=True) * a_row

        dxc_ref[:, :ds] = dxdt * dtx_s[...] + dsk_ref[...] * dy_s[...]
        dpre = dxc_ref[...] * dsl_s[...]
        nxt_ref[0:CHUNK, :] = dpre
        gcb_ref[0:1, :] += jnp.sum(dpre, axis=0, keepdims=True)
        xr = xbc_ref[...].astype(F32)
        gcw_ref[CONV_K - 1:CONV_K, :] += jnp.sum(dpre * xr, axis=0, keepdims=True)
        dxr = cw_ref[CONV_K - 1:CONV_K, :] * dpre
        for j in range(1, CONV_K):
            up = nxt_ref[j:j + CHUNK, :]
            gcw_ref[CONV_K - 1 - j:CONV_K - j, :] += jnp.sum(up * xr, axis=0, keepdims=True)
            dxr = dxr + cw_ref[CONV_K - 1 - j:CONV_K - j, :] * up
        nxt_ref[CHUNK:, :] = dpre[0:8, :]
        dxbc_ref[...] = dxr.astype(BF16)

        @pl.when(step == nch - 1)
        def _():
            gsm_ref[2:3, :] = _dot_sel(jnp.broadcast_to(gdsk_ref[...], (8, ds)), selm)[0:1, :]

    rev = lambda s: nch - 1 - s
    blk = lambda w: pl.BlockSpec((CHUNK, w), lambda s: (rev(s), 0))
    return pl.pallas_call(
        body, name="ssd_bwd", grid=(nch,),
        in_specs=[blk(ds), blk(ds), blk(ds), blk(cd), blk(cd),
                  blk(LANES), pl.BlockSpec((1, ns, ds), lambda s: (rev(s), 0, 0)), blk(LANES),
                  _full((CONV_K, cd)), _full((1, LANES)), _full((1, LANES)),
                  _full((1, ds)), _full((1, ds)), _full((LANES, ds)), _full((ds, LANES))],
        out_specs=[blk(cd), blk(ds), blk(LANES), _full((8, cd)), _full((8, cd)), _full((8, ds)), _full((8, LANES))],
        out_shape=[jax.ShapeDtypeStruct((p, cd), BF16), jax.ShapeDtypeStruct((p, ds), BF16),
                   jax.ShapeDtypeStruct((p, LANES), F32), jax.ShapeDtypeStruct((8, cd), F32),
                   jax.ShapeDtypeStruct((8, cd), F32), jax.ShapeDtypeStruct((8, ds), F32),
                   jax.ShapeDtypeStruct((8, LANES), F32)],
        scratch_shapes=[pltpu.VMEM((ns, ds), F32), pltpu.VMEM((CHUNK + 8, cd), F32), pltpu.VMEM((1, LANES), F32),
                        pltpu.VMEM((1, ds), F32), pltpu.VMEM((CHUNK, cd), F32),
                        pltpu.VMEM((CHUNK, cd), F32), pltpu.VMEM((CHUNK, cd), F32),
                        pltpu.VMEM((CHUNK, ds), F32), pltpu.VMEM((CHUNK, ds), F32), pltpu.VMEM((CHUNK, ds), F32),
                        pltpu.VMEM((CHUNK, ds), F32), pltpu.VMEM((CHUNK, ds), BF16), pltpu.VMEM((CHUNK, ds), BF16),
                        pltpu.VMEM((CHUNK, ds), BF16), pltpu.VMEM((CHUNK, ds), BF16)],
        compiler_params=_cparams(("arbitrary",)),
    )(dyssd, y, z, xbc, pre, dtf, hin, dcf, conv_w, brow, alog, dskip_l, ssd_norm, sel_t, sel)


def _attn_fwd(q, k, v, ck, blk):
    p, da = q.shape
    npair, nkb = ck.shape[0], ck.shape[1]
    scale = 1.0 / math.sqrt(HEAD_DIM)

    def body(q_ref, k_ref, v_ref, ck_ref, o_ref, lse_ref):
        i = pl.program_id(1)
        lane = lax.broadcasted_iota(jnp.int32, (1, LANES), 1)
        sels = [lane < HEAD_DIM, lane >= HEAD_DIM]
        ones = [jnp.where(lane == HEAD_DIM, 1.0, 0.0).astype(BF16), jnp.where(lane == 0, 1.0, 0.0).astype(BF16)]
        qb = q_ref[...] * scale
        cmask = (lax.broadcasted_iota(jnp.int32, (blk, blk), 1) <= lax.broadcasted_iota(jnp.int32, (blk, blk), 0))

        def step(kb, carry, masked, nk=1):
            r0 = pl.multiple_of(kb * blk, blk)
            ks = k_ref[pl.ds(r0, nk * blk), :]
            vs = v_ref[pl.ds(r0, nk * blk), :]
            kk = jnp.concatenate([jnp.where(sel, ks, jnp.zeros_like(ks)) for sel in sels], axis=0)
            s_both = _dot(qb, kk, NT)
            out = []
            for j in range(2):
                m, acc = carry[2 * j], carry[2 * j + 1]
                ckr = jnp.concatenate([ck_ref[0, kb + t, j:j + 1, :] for t in range(nk)], axis=1)
                s = s_both[:, j * nk * blk:(j + 1) * nk * blk] - ckr
                if masked:
                    s = jnp.where(cmask, s, NEG)
                mn = jnp.maximum(m, jnp.max(s, axis=-1, keepdims=True))
                pr = jnp.exp(s - mn).astype(BF16)
                acc = jnp.exp(m - mn) * acc + _dot(pr, jnp.where(sels[j], vs, ones[j]))
                out += [mn, acc]
            return tuple(out)

        init = (jnp.full((blk, 1), NEG, F32), jnp.zeros((blk, LANES), F32)) * 2
        n4 = i // 4
        n2 = (i - 4 * n4) // 2
        carry = lax.fori_loop(0, n4, lambda t, c: step(4 * t, c, False, 4), init)
        carry = lax.fori_loop(0, n2, lambda t, c: step(4 * n4 + 2 * t, c, False, 2), carry)
        carry = lax.fori_loop(4 * n4 + 2 * n2, i, lambda kb, c: step(kb, c, False), carry)
        m0, a0, m1, a1 = step(i, carry, True)
        l0 = a0[:, HEAD_DIM:HEAD_DIM + 1]
        l1 = a1[:, 0:1]
        o_ref[...] = jnp.where(sels[0], a0 / l0, a1 / l1).astype(BF16)
        lse_ref[...] = jnp.where(sels[0], m0 + jnp.log(l0), m1 + jnp.log(l1))

    return pl.pallas_call(
        body, name="attn_fwd", grid=(npair, p // blk),
        in_specs=[pl.BlockSpec((blk, LANES), lambda h, i: (i, h)),
                  pl.BlockSpec((p, LANES), lambda h, i: (0, h)), pl.BlockSpec((p, LANES), lambda h, i: (0, h)),
                  pl.BlockSpec((1, nkb, 8, blk), lambda h, i: (h, 0, 0, 0))],
        out_specs=[pl.BlockSpec((blk, LANES), lambda h, i: (i, h)), pl.BlockSpec((blk, LANES), lambda h, i: (i, h))],
        out_shape=[jax.ShapeDtypeStruct((p, da), BF16), jax.ShapeDtypeStruct((p, da), F32)],
        compiler_params=_cparams(("parallel", "arbitrary")),
    )(q, k, v, ck)


def _attn_bwd(q, k, v, o, do, lse_rep, ck, blk):
    p, da = q.shape
    npair, nkb = ck.shape[0], ck.shape[1]
    nq = p // blk
    scale = 1.0 / math.sqrt(HEAD_DIM)

    def body(k_ref, v_ref, q_ref, do_ref, o_ref, lse_ref, ck_ref, dk_ref, dv_ref, dq_ref, dcs_ref, rsum_ref, dq_acc):
        jb = pl.program_id(1)

        @pl.when(jb == 0)
        def _():
            dq_acc[...] = jnp.zeros_like(dq_acc)

        ks = k_ref[...]
        vs = v_ref[...]
        lane = lax.broadcasted_iota(jnp.int32, (1, LANES), 1)
        sels = [lane < HEAD_DIM, lane >= HEAD_DIM]
        ones = [jnp.where(lane == HEAD_DIM, 1.0, 0.0).astype(BF16), jnp.where(lane == 0, 1.0, 0.0).astype(BF16)]
        kss = ks * scale
        kmo = [jnp.where(sels[j], kss, ones[j]) for j in range(2)]
        cmask = (lax.broadcasted_iota(jnp.int32, (blk, blk), 1) <= lax.broadcasted_iota(jnp.int32, (blk, blk), 0))

        def step(ib, carry, masked, nb=1):
            rows = nb * blk
            r0 = pl.multiple_of(ib * blk, blk)
            qb = q_ref[pl.ds(r0, rows), :] * scale
            dob = do_ref[pl.ds(r0, rows), :]
            prod = dob.astype(F32) * o_ref[pl.ds(r0, rows), :].astype(F32)
            out = []
            for j in range(2):
                dk, dv = carry[2 * j], carry[2 * j + 1]
                qm = jnp.where(sels[j], qb, jnp.zeros_like(qb))
                dom = jnp.where(sels[j], dob, jnp.zeros_like(dob))
                lse = lse_ref[pl.ds(r0, rows), HEAD_DIM * j:HEAD_DIM * j + 1]
                dlt = jnp.sum(jnp.where(sels[j], prod, 0.0), axis=-1, keepdims=True)
                s = _dot(qm, ks, NT) - ck_ref[0, 0, j:j + 1, :] - lse
                pm = jnp.exp(jnp.minimum(s, 0.0))
                if masked:
                    pm = jnp.where(cmask, pm, 0.0)
                ds_b = (pm * (_dot(dom, vs, NT) - dlt)).astype(BF16)
                dv = dv + _dot(pm.astype(BF16), dom, TN)
                dk = dk + _dot(ds_b, jnp.where(sels[j], qb, ones[j]), TN)
                dq_acc[pl.ds(r0, rows), LANES * j:LANES * (j + 1)] += _dot(ds_b, kmo[j])
                out += [dk, dv]
            return tuple(out)

        zero = jnp.zeros((blk, LANES), F32)
        carry = step(jb, (zero, zero, zero, zero), True)
        n4 = (nq - 1 - jb) // 4
        n2 = (nq - 1 - jb - 4 * n4) // 2
        carry = lax.fori_loop(0, n4, lambda t, c: step(jb + 1 + 4 * t, c, False, 4), carry)
        carry = lax.fori_loop(0, n2, lambda t, c: step(jb + 1 + 4 * n4 + 2 * t, c, False, 2), carry)
        dk0, dv0, dk1, dv1 = lax.fori_loop(jb + 1 + 4 * n4 + 2 * n2, nq, lambda ib, c: step(ib, c, False), carry)
        dk_ref[...] = jnp.where(sels[0], dk0, dk1).astype(BF16)
        dv_ref[...] = (dv0 + dv1).astype(BF16)
        pair8 = lambda c0, c1: jnp.where(lane == 0, c0, jnp.where(lane == 1, c1, 0.0)).T[0:8]
        dcs_ref[0] = pair8(dk0[:, HEAD_DIM:HEAD_DIM + 1], dk1[:, 0:1])

        @pl.when(jb == nkb - 1)
        def _():
            a0 = dq_acc[:, :LANES]
            a1 = dq_acc[:, LANES:]
            dq_ref[...] = jnp.where(sels[0], a0, a1).astype(BF16)
            rsum_ref[0] = pair8(a0[:, HEAD_DIM:HEAD_DIM + 1], a1[:, 0:1])

    colblk = pl.BlockSpec((blk, LANES), lambda h, j: (j, h))
    colfull = pl.BlockSpec((p, LANES), lambda h, j: (0, h))
    ckspec = pl.BlockSpec((1, 1, 8, blk), lambda h, j: (h, j, 0, 0))
    return pl.pallas_call(
        body, name="attn_bwd", grid=(npair, nkb),
        in_specs=[colblk, colblk, colfull, colfull, colfull, colfull, ckspec],
        out_specs=[colblk, colblk, colfull, pl.BlockSpec((1, 8, blk), lambda h, j: (h, 0, j)),
                   pl.BlockSpec((1, 8, p), lambda h, j: (h, 0, 0))],
        out_shape=[jax.ShapeDtypeStruct((p, da), BF16), jax.ShapeDtypeStruct((p, da), BF16),
                   jax.ShapeDtypeStruct((p, da), BF16), jax.ShapeDtypeStruct((npair, 8, p), F32),
                   jax.ShapeDtypeStruct((npair, 8, p), F32)],
        scratch_shapes=[pltpu.VMEM((p, 2 * LANES), F32)],
        compiler_params=_cparams(("parallel", "arbitrary")),
    )(k, v, q, do, o, lse_rep, ck)


def _tail_fwd(yssd, o, zatt, graw, head, x2, tgt2, wps, wpa, wout, gate_bias, norm_post, tm):
    p, ds = yssd.shape
    da = o.shape[1]
    d = x2.shape[1]
    nsub = tm // CHUNK

    def body(yssd_ref, o_ref, zatt_ref, g_ref, head_ref, *rest):
        x_refs, t_refs = rest[:nsub], rest[nsub:2 * nsub]
        (wps_ref, wpa_ref, wout_ref, gb_ref, np_ref,
         yatt_ref, mrg_ref, a_ref, b_ref, dzo_ref, dout_ref, red_ref) = rest[2 * nsub:]
        i = pl.program_id(0)

        @pl.when(i == 0)
        def _():
            red_ref[...] = jnp.zeros_like(red_ref)

        first = jnp.where(i == 0, head_ref[...], x_refs[0][...])
        h = jnp.concatenate([first] + [r[...] for r in x_refs[1:]], axis=0)
        tgt = jnp.concatenate([r[...] for r in t_refs], axis=0)
        rows = lax.broadcasted_iota(jnp.int32, (tm, 1), 0)
        valid = jnp.where((i > 0) | (rows >= CHUNK), 1.0, 0.0)
        ob = o_ref[...].astype(F32)
        za = zatt_ref[...].astype(F32)
        yatt_b = (ob * za * _sigmoid(za)).astype(BF16)
        yatt_ref[...] = yatt_b
        a = _dot(yssd_ref[...], wps_ref[...])
        b = _dot(yatt_b, wpa_ref[...])
        a_ref[...] = a.astype(BF16)
        b_ref[...] = b.astype(BF16)
        gr = g_ref[...].astype(F32) + gb_ref[...]
        mrg_b = (_sigmoid(gr[:, :d]) * a + _sigmoid(gr[:, d:]) * b).astype(BF16)
        mrg_ref[...] = mrg_b
        zo = _dot(mrg_b, wout_ref[...])
        rstd = lax.rsqrt(jnp.mean(zo * zo, axis=-1, keepdims=True) + EPS)
        zh = zo * rstd
        npw = np_ref[...]
        err = (h + zh * npw - tgt) * valid
        dout = err * (1.0 / d)
        dout_ref[...] = dout
        dzh = dout * npw
        dzo_ref[...] = (rstd * (dzh - zh * jnp.mean(dzh * zh, axis=-1, keepdims=True))).astype(BF16)
        red_ref[0:1, :] += jnp.sum(dout * zh, axis=0, keepdims=True)
        red_ref[1:2, 0:1] += jnp.sum(jnp.sum(err * err, axis=1, keepdims=True), axis=0, keepdims=True) * (0.5 / d)

    row = lambda w: pl.BlockSpec((tm, w), lambda i: (i, 0))
    once = lambda shape: pl.BlockSpec(shape, lambda i: (0,) * len(shape), pipeline_mode=pl.Buffered(1))
    subs = _x_row_specs(tm, d)
    sd = jax.ShapeDtypeStruct
    return pl.pallas_call(
        body, name="tail_fwd", grid=(p // tm,),
        in_specs=[row(ds), row(da), row(da), row(2 * d), _full((CHUNK, d))] + subs + subs
                 + [once((ds, d)), once((da, d)), once((d, d)), _full((1, 2 * d)), _full((1, d))],
        out_specs=[row(da), row(d), row(d), row(d), row(d), row(d), _full((8, d))],
        out_shape=[sd((p, da), BF16), sd((p, d), BF16), sd((p, d), BF16), sd((p, d), BF16), sd((p, d), BF16),
                   sd((p, d), F32), sd((8, d), F32)],
        compiler_params=_cparams(("arbitrary",)),
    )(yssd, o, zatt, graw, head, *([x2] * nsub), *([tgt2] * nsub), wps, wpa, wout, gate_bias, norm_post)


def _tail_bwd(dzo, a_b, b_b, graw, o, zatt, wps, wpa, wout, gate_bias, tm):
    p, d = dzo.shape
    ds, da = wps.shape[0], wpa.shape[0]

    def body(dzo_ref, a_ref, b_ref, g_ref, o_ref, zatt_ref, wps_ref, wpa_ref, wout_ref, gb_ref,
             da_ref, db_ref, dg_ref, dyssd_ref, do_ref, dzatt_ref, red_ref):
        i = pl.program_id(0)

        @pl.when(i == 0)
        def _():
            red_ref[...] = jnp.zeros_like(red_ref)

        gr = g_ref[...].astype(F32) + gb_ref[...]
        gs = _sigmoid(gr[:, :d])
        ga = _sigmoid(gr[:, d:])
        dm = _dot(dzo_ref[...], wout_ref[...], NT)
        da_b = (gs * dm).astype(BF16)
        db_b = (ga * dm).astype(BF16)
        da_ref[...] = da_b
        db_ref[...] = db_b
        dgs = dm * a_ref[...].astype(F32) * gs * (1.0 - gs)
        dga = dm * b_ref[...].astype(F32) * ga * (1.0 - ga)
        dg_ref[:, :d] = dgs.astype(BF16)
        dg_ref[:, d:] = dga.astype(BF16)
        red_ref[0:1, :d] += jnp.sum(dgs, axis=0, keepdims=True)
        red_ref[0:1, d:] += jnp.sum(dga, axis=0, keepdims=True)
        dyssd_ref[...] = _dot(da_b, wps_ref[...], NT).astype(BF16)
        dya = _dot(db_b, wpa_ref[...], NT)
        ob = o_ref[...].astype(F32)
        za = zatt_ref[...].astype(F32)
        sza = _sigmoid(za)
        do_ref[...] = (dya * za * sza).astype(BF16)
        dzatt_ref[...] = (dya * ob * sza * (1.0 + za * (1.0 - sza))).astype(BF16)

    row = lambda w: pl.BlockSpec((tm, w), lambda i: (i, 0))
    once = lambda shape: pl.BlockSpec(shape, lambda i: (0,) * len(shape), pipeline_mode=pl.Buffered(1))
    sd = jax.ShapeDtypeStruct
    return pl.pallas_call(
        body, name="tail_bwd", grid=(p // tm,),
        in_specs=[row(d), row(d), row(d), row(2 * d), row(da), row(da),
                  once((ds, d)), once((da, d)), once((d, d)), _full((1, 2 * d))],
        out_specs=[row(d), row(d), row(2 * d), row(ds), row(da), row(da), _full((8, 2 * d))],
        out_shape=[sd((p, d), BF16), sd((p, d), BF16), sd((p, 2 * d), BF16), sd((p, ds), BF16), sd((p, da), BF16),
                   sd((p, da), BF16), sd((8, 2 * d), F32)],
        compiler_params=_cparams(("arbitrary",)),
    )(dzo, a_b, b_b, graw, o, zatt, wps, wpa, wout, gate_bias)


def _adamw_math(w, g, m, v):
    m2 = ADAM_B1 * m + (1.0 - ADAM_B1) * g
    v2 = ADAM_B2 * v + (1.0 - ADAM_B2) * (g * g)
    m_hat = m2 / (1.0 - ADAM_B1 ** ADAM_STEP)
    v_hat = v2 / (1.0 - ADAM_B2 ** ADAM_STEP)
    delta = -ADAM_LR * (m_hat / (jnp.sqrt(v_hat) + ADAM_EPS) + ADAM_WD * w)
    return delta, m2, v2


def _adamw_small(params, red, name):
    names = list(params)
    n = len(names)
    extra = [params[k][3] for k in names if not isinstance(params[k][3], tuple)]

    def body(*refs):
        w_refs, m_refs, v_refs = refs[:n], refs[n:2 * n], refs[2 * n:3 * n]
        red_ref = refs[3 * n]
        g_refs = iter(refs[3 * n + 1:3 * n + 1 + len(extra)])
        outs = refs[3 * n + 1 + len(extra):]
        for i, k in enumerate(names):
            where = params[k][3]
            rows, cols = w_refs[i].shape
            if isinstance(where, tuple):
                g = red_ref[where[0]:where[0] + rows, where[1]:where[1] + cols]
            else:
                g = next(g_refs)[...]
            delta, m2, v2 = _adamw_math(w_refs[i][...], g, m_refs[i][...], v_refs[i][...])
            for o, val in zip(outs[4 * i:4 * i + 4], (g, delta, m2, v2)):
                o[...] = val

    vm = pl.BlockSpec(memory_space=pltpu.VMEM)
    ws, ms, vs = ([params[k][j] for k in names] for j in range(3))
    out = pl.pallas_call(
        body, name=name,
        out_shape=[jax.ShapeDtypeStruct(w.shape, F32) for w in ws for _ in range(4)],
        in_specs=[vm] * (3 * n + 1 + len(extra)), out_specs=[vm] * (4 * n),
    )(*ws, *ms, *vs, red, *extra)
    return {k: tuple(out[4 * i:4 * i + 4]) for i, k in enumerate(names)}


def _adamw(w, g, m, v, name, parts=False, part_row0=0):
    r, cdim = w.shape
    tr, tc, by_rows = _tiles_2d(r, cdim)
    pick = (lambda i: (i, 0)) if by_rows else (lambda i: (0, i))
    assert part_row0 % tr == 0
    gpick = (lambda i: (i + part_row0 // tr, 0)) if by_rows else (lambda i: (part_row0 // tr, i))

    def body(w_ref, g_ref, m_ref, v_ref, go_ref, d_ref, mo_ref, vo_ref):
        if parts:
            g = g_ref[0].astype(F32)
            for s in range(1, g_ref.shape[0]):
                g = g + g_ref[s].astype(F32)
        else:
            g = g_ref[...]
        delta, m2, v2 = _adamw_math(w_ref[...], g, m_ref[...], v_ref[...])
        go_ref[...] = g
        d_ref[...] = delta
        mo_ref[...] = m2
        vo_ref[...] = v2

    blk = pl.BlockSpec((tr, tc), pick)
    gspec = pl.BlockSpec((g.shape[0], tr, tc), lambda i: (0,) + gpick(i)) if parts else blk
    return pl.pallas_call(
        body, name=name, grid=((r // tr) * (cdim // tc),),
        in_specs=[blk, gspec, blk, blk], out_specs=[blk] * 4,
        out_shape=[jax.ShapeDtypeStruct((r, cdim), F32)] * 4,
        compiler_params=_cparams(("parallel",)),
    )(w, g, m, v)


def _pad_cols(a, width):
    return jnp.pad(a, ((0, 0), (0, width - a.shape[1])))


def _pack_small_shard(conv_w_sh, meta_sh, width):
    return jnp.concatenate([_pad_cols(conv_w_sh, width), jnp.zeros((4, width), F32), _pad_cols(meta_sh, width)], axis=0)


def _pack_small_rep(norm_pre, norm_post, gate_bias, ssd_norm, conv_b, misc, width):
    rows = [norm_pre, norm_post, gate_bias, ssd_norm, conv_b, misc]
    return jnp.concatenate([_pad_cols(r, width) for r in rows] + [jnp.zeros((2, width), F32)], axis=0)


def kernel(x, meta_tokens, norm_pre, w_in, conv_w, conv_b, dt_bias, a_log, d_skip, ssd_norm, fgate_bias, gate_bias, w_proj_ssd, w_proj_att, w_out, norm_post, loss_target, m_meta_tokens, m_norm_pre, m_w_in, m_conv_w, m_conv_b, m_dt_bias, m_a_log, m_d_skip, m_ssd_norm, m_fgate_bias, m_gate_bias, m_w_proj_ssd, m_w_proj_att, m_w_out, m_norm_post, v_meta_tokens, v_norm_pre, v_w_in, v_conv_w, v_conv_b, v_dt_bias, v_a_log, v_d_skip, v_ssd_norm, v_fgate_bias, v_gate_bias, v_w_proj_ssd, v_w_proj_att, v_w_out, v_norm_post):
    seq, d = x.shape[1], x.shape[2]
    p = seq + CHUNK
    hs, ha = dt_bias.shape[1], fgate_bias.shape[1]
    ds, cd = ssd_norm.shape[1], conv_b.shape[1]
    da = ha * HEAD_DIM
    nc8 = w_in.shape[2]
    cws = cd // N_DEV
    msh = d // N_DEV
    r1, r2, r3 = ds // N_DEV, da // N_DEV, d // N_DEV
    me = _dev_index(*_my_pos())
    x2, tgt2 = x[0], loss_target[0]

    win_sh = jnp.transpose(w_in[0]).astype(BF16)
    rows_sh = jnp.concatenate([w_proj_ssd[0], w_proj_att[0], w_out[0]], axis=0).astype(BF16)
    small_sh = _pack_small_shard(conv_w[0], meta_tokens, cws)
    win_all, small_all = _all_gather([win_sh, small_sh], "gather_weights")
    rows_sh, win_all = lax.optimization_barrier((rows_sh, win_all))
    rows_sems, rows_thru, rows_land, rows_token = _bcast_start(rows_sh, "gather_rows_start")
    cuts = [0, ds, ds + cd, ds + cd + hs, ds + cd + hs + da, ds + cd + hs + 2 * da, ds + cd + hs + 3 * da,
            ds + cd + hs + 4 * da, ds + cd + hs + 4 * da + ha, ds + cd + hs + 4 * da + ha + 2 * d]

    def piece_rows(r0, r1):
        parts = [win_all[s, max(r0, s * nc8) - s * nc8:min(r1, (s + 1) * nc8) - s * nc8]
                 for s in range(N_DEV) if max(r0, s * nc8) < min(r1, (s + 1) * nc8)]
        return parts[0] if len(parts) == 1 else jnp.concatenate(parts, axis=0)

    w_z, w_xbc, w_dt, w_zatt, w_q, w_k, w_v, w_f, w_g = [piece_rows(cuts[i], cuts[i + 1]) for i in range(9)]
    w_dtf = jnp.concatenate([w_dt, w_f, jnp.zeros((LANES - hs - ha, d), BF16)], axis=0)
    conv_w_full = jnp.transpose(small_all[:, 0:CONV_K, :], (1, 0, 2)).reshape(CONV_K, cd)
    meta_full = jnp.transpose(small_all[:, 8:8 + N_META, :msh], (1, 0, 2)).reshape(N_META, d)
    head = jnp.concatenate([jnp.zeros((PADN, d), F32), meta_full + rows_token[0:1, 0:1]], axis=0)

    tm = _att_block(p)
    u = _prenorm_fwd(head, x2, norm_pre, tm)
    seg_w = [w_z, w_xbc, w_zatt, w_q, w_k, w_v, w_g]
    zs, xbc, zatt, q, k, v, graw = [
        _mm(u, w, "nt", BF16, _tile(p, (1408, tm)), _tile(w.shape[0], (1024, 512, 256, 128)), "inproj_%d" % i)
        for i, w in enumerate(seg_w)]
    dtf = _mm(u, w_dtf, "nt", F32, _tile(p, (1408, tm)), LANES, "inproj_dtf")

    brow = jnp.concatenate([dt_bias, fgate_bias, jnp.zeros((1, LANES - hs - ha), F32)], axis=1)
    alog_row = _pad_cols(a_log, LANES)
    dskip_l = jnp.repeat(d_skip, HEAD_DIM, axis=1)
    sel_t = (lax.broadcasted_iota(jnp.int32, (LANES, ds), 1) // HEAD_DIM
             == lax.broadcasted_iota(jnp.int32, (LANES, ds), 0)).astype(BF16)
    sel = sel_t.T
    y, yssd, hin, cf, pre = _ssd_fwd(xbc, zs, dtf, conv_w_full, conv_b, brow, alog_row, dskip_l, ssd_norm, sel_t, hs, ha)

    blk = _att_block(p)
    nkb, npair = p // blk, ha // 2
    cum = jnp.where(lax.broadcasted_iota(jnp.int32, (p, 1), 0) < PADN, -NEG, cf[:, hs:hs + ha])
    ck = jnp.transpose(cum.T.reshape(npair, 2, nkb, blk), (0, 2, 1, 3))
    ck = jnp.pad(ck, ((0, 0), (0, 0), (0, 6), (0, 0)))
    o, lse_rep = _attn_fwd(q, k, v, ck, blk)

    rows_all = _bcast_wait(rows_sems, rows_thru, rows_land, lse_rep, "gather_rows_wait")
    wps = rows_all[:, :r1].reshape(ds, d)
    wpa = rows_all[:, r1:r1 + r2].reshape(da, d)
    wout = rows_all[:, r1 + r2:].reshape(d, d)

    yatt, mrg, a_b, b_b, dzo, dout, red_fwd = _tail_fwd(
        yssd, o, zatt, graw, head, x2, tgt2, wps, wpa, wout, gate_bias, norm_post, tm)
    da_, db_, dgraw, dyssd, d_o, dzatt, red_bwd = _tail_bwd(dzo, a_b, b_b, graw, o, zatt, wps, wpa, wout, gate_bias, tm)

    tw = _tile(d, (512, 256, 128))
    g_wout = _mm(mrg, dzo, "tn", BF16, tw, d, "wgrad_out")
    g_wps = _mm(yssd, da_, "tn", BF16, _tile(ds, (512, 256, 128)), d, "wgrad_ps")
    g_wpa = _mm(yatt, db_, "tn", BF16, _tile(da, (512, 256, 128)), d, "wgrad_pa")

    core = lax.axis_index("c").astype(jnp.int32).reshape(1)
    chip = me // 2
    grows_parts = jnp.concatenate([g_wps.reshape(N_DEV, r1, d), g_wpa.reshape(N_DEV, r2, d),
                                   g_wout.reshape(N_DEV, r3, d)], axis=1)
    (sib_rows,) = _exchange_sibling([grows_parts], "scatter_rows_sibling")
    chip_rows = _pair_add(grows_parts, sib_rows, core, "pair_add_rows")
    r_sems, r_thru, r_lands, r_token = _exchange_chips_start([chip_rows], "scatter_rows_start")

    dk, dv, dq, dcs, rsum = _attn_bwd(q, k, v, o, d_o, lse_rep, ck + r_token[0:1, 0:1], blk)
    dcum = (rsum - dcs)[:, 0:2, :].reshape(ha, p).T
    dcf = jnp.pad(dcum, ((0, 0), (hs, LANES - hs - ha)))
    dxbc, dzs, ddtf, gcw, gcb, gnrm, gsm = _ssd_bwd(
        dyssd, y, zs, xbc, pre, dtf, hin, dcf, conv_w_full, brow, alog_row, dskip_l, ssd_norm, sel_t, sel, hs, ha)
    ddtf_b = ddtf.astype(BF16)

    dsegs = [dzs, dxbc, dzatt, dq, dk, dv, dgraw, ddtf_b]
    gsegs = [_mm(dsg, u, "tn", BF16, _tile(dsg.shape[1], (512, 256, 128)), d, "wgrad_in_%d" % i)
             for i, dsg in enumerate(dsegs)]
    g_z, g_xbc, g_zatt, g_q, g_k, g_v, g_g, g_dtf = gsegs
    gw_full = jnp.concatenate([g_z, g_xbc, g_dtf[:hs], g_zatt, g_q, g_k, g_v, g_dtf[hs:hs + ha], g_g], axis=0)
    gwin_parts = gw_full.reshape(N_DEV, nc8, d)

    (sib_win,) = _exchange_sibling([gwin_parts], "scatter_grads_sibling")
    chip_win = _pair_add(gwin_parts, sib_win, core, "pair_add_w_in")
    sems, thru, lands, token = _exchange_chips_start([chip_win], "scatter_grads_start")
    dsegs_after = dsegs[:-1] + [ddtf_b + token[0:1, 0:1].astype(BF16)]
    gx, ghead, gnp = _dgrad_prenorm(dsegs_after, seg_w + [w_dtf], head, x2, norm_pre, dout, tm, "dgrad_in")
    own_slot = lambda got, sent: lax.dynamic_update_slice_in_dim(
        got, lax.dynamic_slice_in_dim(sent, chip, 1, axis=0), chip, axis=0)
    (sent,), (got,) = _exchange_chips_wait(sems, thru, lands, gnp, "scatter_grads_wait")
    recv_win = own_slot(got, sent)
    (r_sent,), (r_got,) = _exchange_chips_wait(r_sems, r_thru, r_lands, gnp, "scatter_rows_wait")
    recv_rows = own_slot(r_got, r_sent)
    gmisc = jnp.concatenate([gsm[0:1], gsm[1:2], gsm[2:3], _pad_cols(red_fwd[1:2, 0:1], LANES)], axis=1)
    small_g = jnp.concatenate([
        _pack_small_rep(gnp[0:1], red_fwd[0:1], red_bwd[0:1], gnrm[0:1], gcb[0:1], gmisc, cd),
        _pad_cols(gcw[0:CONV_K], cd), jnp.zeros((4, cd), F32), _pad_cols(ghead[PADN:], cd)], axis=0)
    sg_sems, sg_thru, sg_land, sg_token = _bcast_start(small_g, "reduce_small_start")

    upd_in = _adamw(jnp.transpose(w_in[0]) + sg_token[0:1, 0:1], recv_win, jnp.transpose(m_w_in[0]),
                    jnp.transpose(v_w_in[0]), "adamw_w_in", parts=True)
    upd_ps = _adamw(w_proj_ssd[0] + sg_token[0:1, 0:1], recv_rows, m_w_proj_ssd[0], v_w_proj_ssd[0],
                    "adamw_w_proj_ssd", parts=True, part_row0=0)
    upd_pa = _adamw(w_proj_att[0], recv_rows, m_w_proj_att[0], v_w_proj_att[0], "adamw_w_proj_att", parts=True,
                    part_row0=r1)
    upd_out = _adamw(w_out[0], recv_rows, m_w_out[0], v_w_out[0], "adamw_w_out", parts=True, part_row0=r1 + r2)
    all_done = upd_in[1][0:8, 0:LANES] + upd_ps[1][0:8, 0:LANES] + upd_pa[1][0:8, 0:LANES] + upd_out[1][0:8, 0:LANES]
    red = _sum_slots(_bcast_wait(sg_sems, sg_thru, sg_land, all_done, "reduce_small_wait"), "reduce_small_sum")
    loss = red[5, 3 * LANES]
    g_conv_w = lax.dynamic_slice_in_dim(red[8:8 + CONV_K], me * cws, cws, axis=1)
    g_meta = lax.dynamic_slice_in_dim(red[16:16 + N_META, :d], me * msh, msh, axis=1)
    small = {
        "meta_tokens": (meta_tokens, m_meta_tokens, v_meta_tokens, g_meta),
        "norm_pre": (norm_pre, m_norm_pre, v_norm_pre, (0, 0)),
        "conv_w": (conv_w[0], m_conv_w[0], v_conv_w[0], g_conv_w),
        "conv_b": (conv_b, m_conv_b, v_conv_b, (4, 0)),
        "dt_bias": (dt_bias, m_dt_bias, v_dt_bias, (5, 0)),
        "a_log": (a_log, m_a_log, v_a_log, (5, LANES)),
        "d_skip": (d_skip, m_d_skip, v_d_skip, (5, 2 * LANES)),
        "ssd_norm": (ssd_norm, m_ssd_norm, v_ssd_norm, (3, 0)),
        "fgate_bias": (fgate_bias, m_fgate_bias, v_fgate_bias, (5, hs)),
        "gate_bias": (gate_bias, m_gate_bias, v_gate_bias, (2, 0)),
        "norm_post": (norm_post, m_norm_post, v_norm_post, (1, 0)),
    }
    upd_small = _adamw_small(small, red, "adamw_small")

    def leaves(i):
        sm = {k: v[i] for k, v in upd_small.items()}
        return [sm["meta_tokens"], sm["norm_pre"], jnp.transpose(upd_in[i])[None], sm["conv_w"][None], sm["conv_b"],
                sm["dt_bias"], sm["a_log"], sm["d_skip"], sm["ssd_norm"], sm["fgate_bias"], sm["gate_bias"],
                upd_ps[i][None], upd_pa[i][None], upd_out[i][None], sm["norm_post"]]

    return tuple([loss, gx[None]] + leaves(0) + leaves(1) + leaves(2) + leaves(3))
```

```python
import functools
import math

import jax
import jax.numpy as jnp
from jax import lax
from jax.experimental import pallas as pl
from jax.experimental.pallas import tpu as pltpu

F32 = jnp.float32
BF16 = jnp.bfloat16

N_DEV = 8
N_META = 16
CHUNK = 128
PADN = CHUNK - N_META
HEAD_DIM = 64
SSD_GROUPS = 4
CONV_K = 4
EPS = 1e-6
NEG = -1e30
LANES = 128
HALO = 16

ADAM_LR = 0.001
ADAM_B1 = 0.9
ADAM_B2 = 0.999
ADAM_EPS = 1e-08
ADAM_WD = 0.01
ADAM_STEP = 10

VMEM_LIMIT = 56 * 1024 * 1024

NN = (((1,), (0,)), ((), ()))
NT = (((1,), (1,)), ((), ()))
TN = (((0,), (0,)), ((), ()))
MESH = pl.DeviceIdType.MESH


def _dot(a, b, dims=NN):
    return lax.dot_general(a, b, dims, preferred_element_type=F32)


def _split2(x):
    hi = x.astype(BF16)
    lo = (x - hi.astype(F32)).astype(BF16)
    return hi, lo


def _dot_sel(x, sel):
    hi, lo = _split2(x)
    return _dot(hi, sel) + _dot(lo, sel)


def _dot_tri(tri, x):
    h1 = x.astype(BF16)
    r1 = x - h1.astype(F32)
    h2 = r1.astype(BF16)
    h3 = (r1 - h2.astype(F32)).astype(BF16)
    return _dot(tri, h1) + _dot(tri, h2) + _dot(tri, h3)


def _sigmoid(x):
    return 0.5 * jnp.tanh(0.5 * x) + 0.5


def _softplus(x):
    return jnp.maximum(x, 0.0) + jnp.log(1.0 + jnp.exp(-jnp.abs(x)))


def _cparams(sem=None, vmem=VMEM_LIMIT):
    kw = {"vmem_limit_bytes": vmem}
    if sem is not None:
        kw["dimension_semantics"] = sem
    return pltpu.CompilerParams(**kw)


def _full(shape):
    nd = len(shape)
    return pl.BlockSpec(shape, lambda *_: (0,) * nd)


def _att_block(p):
    return 384 if p % 384 == 0 else CHUNK


def _my_pos():
    return lax.axis_index("x"), lax.axis_index("y"), lax.axis_index("c")


def _dev_index(x, y, c):
    return 4 * x + 2 * y + c


FLIPS = [(fx, fy, fc) for fx in (0, 1) for fy in (0, 1) for fc in (0, 1)][1:]


def _flip(pos, f):
    return tuple((1 - p) if fi else p for p, fi in zip(pos, f))


def _all_gather(bufs, name):
    nb = len(bufs)

    def body(*refs):
        ins, outs = refs[:nb], refs[nb:2 * nb]
        send_sems, recv_sems, local_sems = refs[2 * nb:]
        x, y, c = _my_pos()
        me = _dev_index(x, y, c)
        sibling = (x, y, 1 - c)
        near = [(1 - x, y), (x, 1 - y)]
        far = (1 - x, 1 - y)
        relay_from = (c * (1 - x) + (1 - c) * x, c * y + (1 - c) * (1 - y))
        relay_to = (c * x + (1 - c) * (1 - x), c * (1 - y) + (1 - c) * y)

        def copy(b, k, block_idx, to, src=None):
            dst = outs[b].at[block_idx]
            return pltpu.make_async_remote_copy(
                src_ref=dst if src is None else src, dst_ref=dst,
                send_sem=send_sems.at[b, k], recv_sem=recv_sems.at[b, k],
                device_id=to, device_id_type=MESH)

        started = []
        for b in range(nb):
            mine = pltpu.make_async_copy(ins[b], outs[b].at[me], local_sems.at[b])
            mine.start()
            started.append(mine)
        sent = []
        for b in range(nb):
            sent.append(copy(b, 0, me, sibling, src=ins[b]))
            for j, chip in enumerate(near):
                sent.append(copy(b, 1 + j, me, (chip[0], chip[1], c), src=ins[b]))
        for cp in sent:
            cp.start()
        for j, chip in enumerate(near):
            blk = _dev_index(chip[0], chip[1], c)
            for b in range(nb):
                copy(b, 1 + j, blk, (x, y, c)).wait_recv()
                sent.append(copy(b, 4 + j, blk, sibling))
                sent[-1].start()
        for b in range(nb):
            sent.append(copy(b, 3, _dev_index(relay_from[0], relay_from[1], c), (relay_to[0], relay_to[1], c)))
            sent[-1].start()
        blk = _dev_index(far[0], far[1], c)
        for b in range(nb):
            copy(b, 3, blk, (x, y, c)).wait_recv()
            sent.append(copy(b, 6, blk, sibling))
            sent[-1].start()
        for b in range(nb):
            copy(b, 0, _dev_index(x, y, 1 - c), (x, y, c)).wait_recv()
        for j, chip in enumerate(near + [far]):
            blk = _dev_index(chip[0], chip[1], 1 - c)
            for b in range(nb):
                copy(b, 4 + j, blk, (x, y, c)).wait_recv()
        for cp in sent:
            cp.wait_send()
        for mine in started:
            mine.wait()

    any_spec = pl.BlockSpec(memory_space=pl.ANY)
    return pl.pallas_call(
        body, name=name,
        out_shape=[jax.ShapeDtypeStruct((N_DEV,) + b.shape, b.dtype) for b in bufs],
        in_specs=[any_spec] * nb, out_specs=[any_spec] * nb,
        scratch_shapes=[pltpu.SemaphoreType.DMA((nb, 7)), pltpu.SemaphoreType.DMA((nb, 7)),
                        pltpu.SemaphoreType.DMA((nb,))],
    )(*bufs)


N_CHIP = 4
CHIP_FLIPS = [(1, 0), (0, 1), (1, 1)]


def _exchange_sibling(bufs, name):
    nb = len(bufs)

    def body(*refs):
        ins, outs = refs[:nb], refs[nb:2 * nb]
        send_sems, recv_sems = refs[2 * nb:]
        x, y, c = _my_pos()

        def copy(b, k):
            return pltpu.make_async_remote_copy(
                src_ref=ins[b].at[2 * k + (1 - c)], dst_ref=outs[b].at[k],
                send_sem=send_sems.at[b, k], recv_sem=recv_sems.at[b, k],
                device_id=(x, y, 1 - c), device_id_type=MESH)

        cps = [copy(b, k) for b in range(nb) for k in range(N_CHIP)]
        for cp in cps:
            cp.start()
        for cp in cps:
            cp.wait()

    any_spec = pl.BlockSpec(memory_space=pl.ANY)
    return pl.pallas_call(
        body, name=name,
        out_shape=[jax.ShapeDtypeStruct((N_CHIP,) + b.shape[1:], b.dtype) for b in bufs],
        in_specs=[any_spec] * nb, out_specs=[any_spec] * nb,
        scratch_shapes=[pltpu.SemaphoreType.DMA((nb, N_CHIP)), pltpu.SemaphoreType.DMA((nb, N_CHIP))],
    )(*bufs)


def _pair_add(mine, recv, core, name):
    _, r, cdim = mine.shape
    tr, tc = r, cdim
    pick = lambda i: (i, 0)

    def body(core_ref, a_ref, b_ref, o_ref):
        o_ref[0] = (a_ref[0].astype(F32) + b_ref[0].astype(F32)).astype(o_ref.dtype)

    return pl.pallas_call(
        body, name=name,
        grid_spec=pltpu.PrefetchScalarGridSpec(
            num_scalar_prefetch=1, grid=(N_CHIP, (r // tr) * (cdim // tc)),
            in_specs=[pl.BlockSpec((1, tr, tc), lambda k, i, core_ref: (2 * k + core_ref[0],) + pick(i)),
                      pl.BlockSpec((1, tr, tc), lambda k, i, core_ref: (k,) + pick(i))],
            out_specs=pl.BlockSpec((1, tr, tc), lambda k, i, core_ref: (k,) + pick(i))),
        out_shape=jax.ShapeDtypeStruct((N_CHIP, r, cdim), mine.dtype),
        compiler_params=_cparams(("parallel", "parallel")),
    )(core, mine, recv)


def _chip_peer(x, y, f):
    return ((1 - x) if f[0] else x), ((1 - y) if f[1] else y)


def _exchange_chips_start(bufs, name):
    nb = len(bufs)
    nsem = 2 * 3 * nb

    def body(*refs):
        ins, lands = refs[:nb], refs[nb:2 * nb]
        sems = refs[2 * nb:2 * nb + nsem]
        token = refs[-1]
        x, y, c = _my_pos()
        for b in range(nb):
            for j, f in enumerate(CHIP_FLIPS):
                px, py = _chip_peer(x, y, f)
                pltpu.make_async_remote_copy(
                    src_ref=ins[b].at[2 * px + py], dst_ref=lands[b].at[2 * x + y],
                    send_sem=sems[2 * (3 * b + j)], recv_sem=sems[2 * (3 * b + j) + 1],
                    device_id=(px, py, c), device_id_type=MESH).start()
        token[...] = jnp.zeros_like(token)

    hbm = pl.BlockSpec(memory_space=pltpu.HBM)
    sem = pl.BlockSpec(memory_space=pltpu.SEMAPHORE)
    out = pl.pallas_call(
        body, name=name,
        out_shape=(*([pltpu.SemaphoreType.DMA(())] * nsem),
                   *[pltpu.HBM(b.shape, b.dtype) for b in bufs], *[pltpu.HBM(b.shape, b.dtype) for b in bufs],
                   jax.ShapeDtypeStruct((8, LANES), F32)),
        in_specs=[hbm] * (2 * nb),
        out_specs=(*([sem] * nsem), *([hbm] * (2 * nb)), pl.BlockSpec(memory_space=pltpu.VMEM)),
        input_output_aliases={i: nsem + i for i in range(2 * nb)},
        compiler_params=pltpu.CompilerParams(has_side_effects=pltpu.SideEffectType.DATAFLOW_SIDE_EFFECTING),
    )(*[pltpu.with_memory_space_constraint(b, pltpu.HBM) for b in bufs],
      *[pltpu.with_memory_space_constraint(lax.empty(b.shape, b.dtype), pltpu.HBM) for b in bufs])
    return out[:nsem], out[nsem:nsem + nb], out[nsem + nb:nsem + 2 * nb], out[-1]


def _exchange_chips_wait(sems, thru, lands, after, name):
    nb = len(thru)
    nsem = len(sems)

    def body(*refs):
        ins, lnd = refs[:nb], refs[nb:2 * nb]
        sem_refs = refs[2 * nb:2 * nb + nsem]
        x, y, c = _my_pos()
        for b in range(nb):
            for j, f in enumerate(CHIP_FLIPS):
                px, py = _chip_peer(x, y, f)
                cp = pltpu.make_async_remote_copy(
                    src_ref=ins[b].at[2 * px + py], dst_ref=lnd[b].at[2 * px + py],
                    send_sem=sem_refs[2 * (3 * b + j)], recv_sem=sem_refs[2 * (3 * b + j) + 1],
                    device_id=(px, py, c), device_id_type=MESH)
                cp.wait_send()
                cp.wait_recv()

    hbm = pl.BlockSpec(memory_space=pltpu.HBM)
    sem = pl.BlockSpec(memory_space=pltpu.SEMAPHORE)
    out = pl.pallas_call(
        body, name=name,
        out_shape=tuple([pltpu.HBM(b.shape, b.dtype) for b in thru] + [pltpu.HBM(b.shape, b.dtype) for b in lands]),
        in_specs=[hbm] * (2 * nb) + [sem] * nsem + [pl.BlockSpec(memory_space=pl.ANY)],
        out_specs=tuple([hbm] * (2 * nb)),
        input_output_aliases={i: i for i in range(2 * nb)},
        compiler_params=pltpu.CompilerParams(has_side_effects=pltpu.SideEffectType.DATAFLOW_SIDE_EFFECTING),
    )(*thru, *lands, *sems, after)
    return out[:nb], out[nb:]


def _bcast_start(buf, name):
    nsem = 2 * len(FLIPS)

    def body(src, land, *rest):
        sems, token = rest[:nsem], rest[-1]
        pos = _my_pos()
        for k, f in enumerate(FLIPS):
            pltpu.make_async_remote_copy(
                src_ref=src, dst_ref=land.at[_dev_index(*pos)], send_sem=sems[2 * k], recv_sem=sems[2 * k + 1],
                device_id=_flip(pos, f), device_id_type=MESH).start()
        token[...] = jnp.zeros_like(token)

    hbm = pl.BlockSpec(memory_space=pltpu.HBM)
    sem = pl.BlockSpec(memory_space=pltpu.SEMAPHORE)
    land_shape = (N_DEV,) + buf.shape
    out = pl.pallas_call(
        body, name=name,
        out_shape=(*([pltpu.SemaphoreType.DMA(())] * nsem), pltpu.HBM(buf.shape, buf.dtype),
                   pltpu.HBM(land_shape, buf.dtype), jax.ShapeDtypeStruct((8, LANES), F32)),
        in_specs=[hbm, hbm],
        out_specs=(*([sem] * nsem), hbm, hbm, pl.BlockSpec(memory_space=pltpu.VMEM)),
        input_output_aliases={0: nsem, 1: nsem + 1},
        compiler_params=pltpu.CompilerParams(has_side_effects=pltpu.SideEffectType.DATAFLOW_SIDE_EFFECTING),
    )(pltpu.with_memory_space_constraint(buf, pltpu.HBM),
      pltpu.with_memory_space_constraint(lax.empty(land_shape, buf.dtype), pltpu.HBM))
    return out[:nsem], out[nsem], out[nsem + 1], out[-1]


def _bcast_wait(sems, thru, land, after, name):
    nsem = len(sems)

    def body(src, lnd, *rest):
        sem_refs = rest[:nsem]
        pos = _my_pos()
        for k, f in enumerate(FLIPS):
            peer = _flip(pos, f)
            cp = pltpu.make_async_remote_copy(
                src_ref=src, dst_ref=lnd.at[_dev_index(*peer)], send_sem=sem_refs[2 * k],
                recv_sem=sem_refs[2 * k + 1], device_id=peer, device_id_type=MESH)
            cp.wait_send()
            cp.wait_recv()

    hbm = pl.BlockSpec(memory_space=pltpu.HBM)
    sem = pl.BlockSpec(memory_space=pltpu.SEMAPHORE)
    sent, got = pl.pallas_call(
        body, name=name,
        out_shape=(pltpu.HBM(thru.shape, thru.dtype), pltpu.HBM(land.shape, land.dtype)),
        in_specs=[hbm, hbm] + [sem] * nsem + [pl.BlockSpec(memory_space=pl.ANY)],
        out_specs=(hbm, hbm), input_output_aliases={0: 0, 1: 1},
        compiler_params=pltpu.CompilerParams(has_side_effects=pltpu.SideEffectType.DATAFLOW_SIDE_EFFECTING),
    )(thru, land, *sems, after)
    return lax.dynamic_update_slice_in_dim(got, sent[None], _dev_index(*_my_pos()), axis=0)


def _sum_slots(v, name):
    _, r, cdim = v.shape

    def body(v_ref, o_ref):
        acc = v_ref[0]
        for s in range(1, N_DEV):
            acc = acc + v_ref[s]
        o_ref[...] = acc

    return pl.pallas_call(
        body, name=name, out_shape=jax.ShapeDtypeStruct((r, cdim), F32),
        in_specs=[_full((N_DEV, r, cdim))], out_specs=_full((r, cdim)), grid=(1,),
        compiler_params=_cparams(("arbitrary",)),
    )(v)


def _mm(a, b, dims, out_dtype, tm, tn, name):
    if dims == "nn":
        (m, k), (_, n) = a.shape, b.shape
        a_spec = pl.BlockSpec((tm, k), lambda j, i: (i, 0))
        b_spec = pl.BlockSpec((k, tn), lambda j, i: (0, j))
        dn = NN
    elif dims == "nt":
        (m, k), (n, _) = a.shape, b.shape
        a_spec = pl.BlockSpec((tm, k), lambda j, i: (i, 0))
        b_spec = pl.BlockSpec((tn, k), lambda j, i: (j, 0))
        dn = NT
    else:
        (k, m), (_, n) = a.shape, b.shape
        a_spec = pl.BlockSpec((k, tm), lambda j, i: (0, i))
        b_spec = pl.BlockSpec((k, tn), lambda j, i: (0, j))
        dn = TN
    assert m % tm == 0 and n % tn == 0, (m, tm, n, tn)

    def body(a_ref, b_ref, o_ref):
        o_ref[...] = _dot(a_ref[...], b_ref[...], dn).astype(o_ref.dtype)

    return pl.pallas_call(
        body, name=name, grid=(n // tn, m // tm),
        in_specs=[a_spec, b_spec], out_specs=pl.BlockSpec((tm, tn), lambda j, i: (i, j)),
        out_shape=jax.ShapeDtypeStruct((m, n), out_dtype),
        compiler_params=_cparams(("parallel", "parallel")),
    )(a, b)


def _tiles_2d(r, cdim):
    if r % CHUNK == 0:
        return CHUNK, cdim, True
    return r, _tile(cdim, (256, 128)), False


def _dgrad_prenorm(a_list, b_list, head, x2, w, dout, tm, name):
    n_op = len(a_list)
    m, d = a_list[0].shape[0], b_list[0].shape[1]
    subs = _x_row_specs(tm, d)
    last = m // tm - 1
    rest = tm - CHUNK

    def body(*refs):
        a_refs, b_refs = refs[:n_op], refs[n_op:2 * n_op]
        head_ref = refs[2 * n_op]
        x_refs = refs[2 * n_op + 1:2 * n_op + 1 + len(subs)]
        w_ref, dout_ref, gx_ref, ghead_ref, gw_ref, dh_buf, sem = refs[2 * n_op + 1 + len(subs):]
        i = pl.program_id(0)

        def first_copy():
            return pltpu.make_async_copy(dh_buf.at[pl.ds(CHUNK, rest)], gx_ref.at[pl.ds(0, rest)], sem)

        def later_copy(step):
            return pltpu.make_async_copy(dh_buf, gx_ref.at[pl.ds(pl.multiple_of(step * tm - CHUNK, CHUNK), tm)], sem)

        @pl.when(i == 0)
        def _():
            gw_ref[...] = jnp.zeros_like(gw_ref)

        du = _dot(a_refs[0][...], b_refs[0][...])
        for k in range(1, n_op):
            du = du + _dot(a_refs[k][...], b_refs[k][...])
        first = jnp.where(i == 0, head_ref[...], x_refs[0][...])
        h = jnp.concatenate([first] + [r[...] for r in x_refs[1:]], axis=0)
        rstd = lax.rsqrt(jnp.mean(h * h, axis=-1, keepdims=True) + EPS)
        xhat = h * rstd
        dxh = du * w_ref[...]
        dh = rstd * (dxh - xhat * jnp.mean(dxh * xhat, axis=-1, keepdims=True)) + dout_ref[...]
        gw_ref[0:1, :] += jnp.sum(du * xhat, axis=0, keepdims=True)

        if rest and last >= 1:
            @pl.when(i == 1)
            def _():
                first_copy().wait()

        @pl.when(i >= (2 if rest else 1))
        def _():
            later_copy(i - 1).wait()

        dh_buf[...] = dh

        @pl.when(i == 0)
        def _():
            ghead_ref[...] = dh_buf[0:CHUNK, :]
            if rest:
                first_copy().start()
                if last == 0:
                    first_copy().wait()

        @pl.when(i >= 1)
        def _():
            later_copy(i).start()

        if last >= 1:
            @pl.when(i == last)
            def _():
                later_copy(i).wait()

    once = lambda b: pl.BlockSpec(b.shape, lambda i: (0, 0), pipeline_mode=pl.Buffered(1))
    row = lambda width: pl.BlockSpec((tm, width), lambda i: (i, 0))
    return pl.pallas_call(
        body, name=name, grid=(m // tm,),
        in_specs=([row(a.shape[1]) for a in a_list] + [once(b) for b in b_list]
                  + [_full((CHUNK, d))] + subs + [_full((1, d)), row(d)]),
        out_specs=[pl.BlockSpec(memory_space=pl.ANY), _full((CHUNK, d)), _full((8, d))],
        out_shape=[jax.ShapeDtypeStruct((m - CHUNK, d), F32), jax.ShapeDtypeStruct((CHUNK, d), F32),
                   jax.ShapeDtypeStruct((8, d), F32)],
        scratch_shapes=[pltpu.VMEM((tm, d), F32), pltpu.SemaphoreType.DMA],
        compiler_params=_cparams(("arbitrary",)),
    )(*a_list, *b_list, head, *([x2] * len(subs)), w, dout)


def _tile(n, prefs):
    for t in prefs:
        if n % t == 0:
            return t
    return n


def _rows3(i):
    return jnp.maximum(3 * i - 1, 0), 3 * i, 3 * i + 1


def _x_row_specs(tm, d):
    if tm == CHUNK:
        return [pl.BlockSpec((CHUNK, d), lambda i: (jnp.maximum(i - 1, 0), 0))]
    return [pl.BlockSpec((CHUNK, d), functools.partial(lambda i, k: (_rows3(i)[k], 0), k=k)) for k in range(3)]


def _prenorm_fwd(head, x2, w, tm):
    p, d = x2.shape[0] + CHUNK, x2.shape[1]
    subs = _x_row_specs(tm, d)

    def body(head_ref, *rest):
        x_refs, (w_ref, u_ref) = rest[:len(subs)], rest[len(subs):]
        i = pl.program_id(0)
        first = jnp.where(i == 0, head_ref[...], x_refs[0][...])
        h = jnp.concatenate([first] + [r[...] for r in x_refs[1:]], axis=0)
        ms = jnp.mean(h * h, axis=-1, keepdims=True)
        u_ref[...] = (h * lax.rsqrt(ms + EPS) * w_ref[...]).astype(BF16)

    return pl.pallas_call(
        body, name="prenorm_fwd", grid=(p // tm,),
        in_specs=[_full((CHUNK, d))] + subs + [_full((1, d))],
        out_specs=pl.BlockSpec((tm, d), lambda i: (i, 0)),
        out_shape=jax.ShapeDtypeStruct((p, d), BF16),
        compiler_params=_cparams(("arbitrary",)),
    )(head, *([x2] * len(subs)), w)


def _conv_pre(ext_ref, cw_ref, cb_ref):
    pre = cb_ref[...] + cw_ref[CONV_K - 1:CONV_K, :] * ext_ref[8:8 + CHUNK, :]
    for j in range(1, CONV_K):
        pre = pre + cw_ref[CONV_K - 1 - j:CONV_K - j, :] * ext_ref[8 - j:8 - j + CHUNK, :]
    return pre


def _ssd_scalars(dtf_ref, brow_ref, alog_ref, rowmask, hs, ha, tri):
    lane = lax.broadcasted_iota(jnp.int32, (1, LANES), 1)
    is_dt = lane < hs
    is_f = (lane >= hs) & (lane < hs + ha)
    dtr = dtf_ref[...] + brow_ref[...]
    sp = _softplus(dtr)
    dt = jnp.where(is_dt, sp, 0.0) * rowmask
    logf = jnp.where(is_f, jnp.minimum(dtr, 0.0) - jnp.log(1.0 + jnp.exp(-jnp.abs(dtr))), 0.0) * rowmask
    a_row = jnp.where(is_dt, -jnp.exp(alog_ref[...]), 0.0)
    run = _dot_tri(tri, dt * a_row + logf)
    return dtr, dt, a_row, run, is_dt, is_f


def _tri_mats():
    r = lax.broadcasted_iota(jnp.int32, (CHUNK, CHUNK), 0)
    c = lax.broadcasted_iota(jnp.int32, (CHUNK, CHUNK), 1)
    return r, c


def _ssd_fwd(xbc, z, dtf, conv_w, conv_b, brow, alog, dskip_l, ssd_norm, sel_t, hs, ha):
    p, cd = xbc.shape
    ds = z.shape[1]
    ns = (cd - ds) // (2 * SSD_GROUPS)
    gw = ds // SSD_GROUPS
    nch = p // CHUNK
    hpg = hs // SSD_GROUPS

    def body(xbc_ref, halo_ref, z_ref, dtf_ref, cw_ref, cb_ref, brow_ref, alog_ref, dsk_ref, nrm_ref, selt_ref,
             y_ref, yssd_ref, hin_ref, cf_ref, pre_ref, st_ref, carry_ref, yacc_ref, xc_s, ex_s, xdtb_s, xwb_s, ext_s):
        c = pl.program_id(0)

        @pl.when(c == 0)
        def _():
            st_ref[...] = jnp.zeros_like(st_ref)
            carry_ref[...] = jnp.zeros_like(carry_ref)

        rows = lax.broadcasted_iota(jnp.int32, (CHUNK, 1), 0)
        rowmask = jnp.where((rows >= PADN) | (c > 0), 1.0, 0.0)
        ri, ci = _tri_mats()
        causal = ri >= ci
        tri = jnp.where(causal, 1.0, 0.0).astype(BF16)

        ext_s[0:8, :] = halo_ref[...].astype(F32)[HALO - 8:, :] * jnp.where(c > 0, 1.0, 0.0)
        ext_s[8:, :] = xbc_ref[...].astype(F32)
        pre = _conv_pre(ext_s, cw_ref, cb_ref)
        pre_ref[...] = pre.astype(BF16)
        xc_s[...] = pre * _sigmoid(pre) * rowmask

        dtr, dt, a_row, run, is_dt, is_f = _ssd_scalars(dtf_ref, brow_ref, alog_ref, rowmask, hs, ha, tri)
        cf = run + carry_ref[...]
        cf_ref[...] = cf
        carry_ref[...] = jnp.where(is_f, cf[CHUNK - 1:CHUNK, :], 0.0)
        cs = jnp.where(is_dt, run, 0.0)
        cl = cs[CHUNK - 1:CHUNK, :]
        selt = selt_ref[...]
        ex_s[...] = _dot_sel(jnp.exp(cs), selt)
        cdec_x = _dot_sel(jnp.broadcast_to(jnp.exp(cl), (8, LANES)), selt)[0:1, :]
        cs_t = cs.T
        xdt = xc_s[:, :ds] * _dot_sel(dt, selt)
        xdtb_s[...] = xdt.astype(BF16)
        xwb_s[...] = (xdt * _dot_sel(jnp.exp(cl - cs), selt)).astype(BF16)

        lane = lax.broadcasted_iota(jnp.int32, (1, LANES), 1)
        half0 = lane < HEAD_DIM
        for g in range(SSD_GROUPS):
            bg = xc_s[:, ds + g * ns: ds + (g + 1) * ns].astype(BF16)
            cg = xc_s[:, ds + SSD_GROUPS * ns + g * ns: ds + SSD_GROUPS * ns + (g + 1) * ns].astype(BF16)
            gm = _dot(cg, bg, NT)
            gs = slice(g * gw, (g + 1) * gw)
            stg = st_ref[:, gs]
            stg_b = stg.astype(BF16)
            hin_ref[0, :, gs] = stg_b
            yoff = _dot(cg, stg_b) * ex_s[:, gs]
            for pr in range(gw // LANES):
                sl = slice(g * gw + pr * LANES, g * gw + (pr + 1) * LANES)
                xp = xdtb_s[:, sl]
                yd = jnp.zeros((CHUNK, LANES), F32)
                for j in range(2):
                    h = g * hpg + 2 * pr + j
                    seg = cs[:, h:h + 1] - cs_t[h:h + 1, :]
                    m = jnp.where(causal, gm * jnp.exp(jnp.minimum(seg, 0.0)), 0.0).astype(BF16)
                    sel = half0 if j == 0 else jnp.logical_not(half0)
                    yd = yd + _dot(m, jnp.where(sel, xp, jnp.zeros_like(xp)))
                yacc_ref[:, sl] = yd + yoff[:, pr * LANES:(pr + 1) * LANES] + dsk_ref[:, sl] * xc_s[:, sl]
            st_ref[:, gs] = stg * cdec_x[:, gs] + _dot(bg, xwb_s[:, gs], TN)

        y = yacc_ref[...]
        y_ref[...] = y.astype(BF16)
        zf = z_ref[...].astype(F32)
        u = y * zf * _sigmoid(zf)
        for g in range(SSD_GROUPS):
            gs = slice(g * gw, (g + 1) * gw)
            ug = u[:, gs]
            ms = jnp.mean(ug * ug, axis=-1, keepdims=True)
            yssd_ref[:, gs] = (ug * lax.rsqrt(ms + EPS) * nrm_ref[:, gs]).astype(BF16)

    rb = CHUNK // HALO
    return pl.pallas_call(
        body, name="ssd_fwd", grid=(nch,),
        in_specs=[pl.BlockSpec((CHUNK, cd), lambda c: (c, 0)),
                  pl.BlockSpec((HALO, cd), lambda c: (jnp.maximum(c * rb - 1, 0), 0)),
                  pl.BlockSpec((CHUNK, ds), lambda c: (c, 0)),
                  pl.BlockSpec((CHUNK, LANES), lambda c: (c, 0)),
                  _full((CONV_K, cd)), _full((1, cd)), _full((1, LANES)), _full((1, LANES)),
                  _full((1, ds)), _full((1, ds)), _full((LANES, ds))],
        out_specs=[pl.BlockSpec((CHUNK, ds), lambda c: (c, 0)), pl.BlockSpec((CHUNK, ds), lambda c: (c, 0)),
                   pl.BlockSpec((1, ns, ds), lambda c: (c, 0, 0)), pl.BlockSpec((CHUNK, LANES), lambda c: (c, 0)),
                   pl.BlockSpec((CHUNK, cd), lambda c: (c, 0))],
        out_shape=[jax.ShapeDtypeStruct((p, ds), BF16), jax.ShapeDtypeStruct((p, ds), BF16),
                   jax.ShapeDtypeStruct((nch, ns, ds), BF16), jax.ShapeDtypeStruct((p, LANES), F32),
                   jax.ShapeDtypeStruct((p, cd), BF16)],
        scratch_shapes=[pltpu.VMEM((ns, ds), F32), pltpu.VMEM((1, LANES), F32), pltpu.VMEM((CHUNK, ds), F32),
                        pltpu.VMEM((CHUNK, cd), F32), pltpu.VMEM((CHUNK, ds), F32),
                        pltpu.VMEM((CHUNK, ds), BF16), pltpu.VMEM((CHUNK, ds), BF16),
                        pltpu.VMEM((8 + CHUNK, cd), F32)],
        compiler_params=_cparams(("arbitrary",)),
    )(xbc, xbc, z, dtf, conv_w, conv_b, brow, alog, dskip_l, ssd_norm, sel_t)


def _ssd_bwd(dyssd, y, z, xbc, pre, dtf, hin, dcf, conv_w, brow, alog, dskip_l, ssd_norm, sel_t, sel, hs, ha):
    p, cd = xbc.shape
    ds = z.shape[1]
    ns = (cd - ds) // (2 * SSD_GROUPS)
    gw = ds // SSD_GROUPS
    nch = p // CHUNK
    hpg = hs // SSD_GROUPS

    def body(dyssd_ref, y_ref, z_ref, xbc_ref, pre_ref, dtf_ref, hin_ref, dcf_ref, cw_ref, brow_ref,
             alog_ref, dsk_ref, nrm_ref, selt_ref, sel_ref,
             dxbc_ref, dz_ref, ddtf_ref, gcw_ref, gcb_ref, gnrm_ref, gsm_ref,
             dst_ref, nxt_ref, fcar_ref, gdsk_ref, dxc_ref, xc_s, dsl_s, dtx_s, ex_s, wx_s, dy_s, xdtb_s, xwb_s,
             dyb_s, dyeb_s):
        step = pl.program_id(0)
        c = nch - 1 - step

        @pl.when(step == 0)
        def _():
            dst_ref[...] = jnp.zeros_like(dst_ref)
            nxt_ref[...] = jnp.zeros_like(nxt_ref)
            fcar_ref[...] = jnp.zeros_like(fcar_ref)
            gdsk_ref[...] = jnp.zeros_like(gdsk_ref)
            gcw_ref[...] = jnp.zeros_like(gcw_ref)
            gcb_ref[...] = jnp.zeros_like(gcb_ref)
            gnrm_ref[...] = jnp.zeros_like(gnrm_ref)
            gsm_ref[...] = jnp.zeros_like(gsm_ref)

        rows = lax.broadcasted_iota(jnp.int32, (CHUNK, 1), 0)
        rowmask = jnp.where((rows >= PADN) | (c > 0), 1.0, 0.0)
        ri, ci = _tri_mats()
        causal = ri >= ci
        anti = ci >= ri
        tri = jnp.where(causal, 1.0, 0.0).astype(BF16)
        rtri = jnp.where(anti, 1.0, 0.0).astype(BF16)

        pre = pre_ref[...].astype(F32)
        sg = _sigmoid(pre)
        xc_s[...] = pre * sg * rowmask
        dsl_s[...] = sg * (1.0 + pre * (1.0 - sg)) * rowmask

        dtr, dt, a_row, run, is_dt, is_f = _ssd_scalars(dtf_ref, brow_ref, alog_ref, rowmask, hs, ha, tri)
        cs = jnp.where(is_dt, run, 0.0)
        cl = cs[CHUNK - 1:CHUNK, :]
        selt = selt_ref[...]
        selm = sel_ref[...]
        dtx_s[...] = _dot_sel(dt, selt)
        ex_s[...] = _dot_sel(jnp.exp(cs), selt)
        wx_s[...] = _dot_sel(jnp.exp(cl - cs), selt)
        cdec = jnp.exp(cl)
        cdec_x = _dot_sel(jnp.broadcast_to(cdec, (8, LANES)), selt)[0:1, :]
        cs_t = cs.T
        xdt = xc_s[:, :ds] * dtx_s[...]
        xdtb_s[...] = xdt.astype(BF16)
        xwb_s[...] = (xdt * wx_s[...]).astype(BF16)

        yv = y_ref[...].astype(F32)
        zf = z_ref[...].astype(F32)
        sz = _sigmoid(zf)
        u = yv * zf * sz
        dyo = dyssd_ref[...].astype(F32)
        du_parts = []
        for g in range(SSD_GROUPS):
            gs = slice(g * gw, (g + 1) * gw)
            ug = u[:, gs]
            rstd = lax.rsqrt(jnp.mean(ug * ug, axis=-1, keepdims=True) + EPS)
            yhat = ug * rstd
            dyg = dyo[:, gs]
            gnrm_ref[0:1, gs] += jnp.sum(dyg * yhat, axis=0, keepdims=True)
            dyh = dyg * nrm_ref[:, gs]
            du_parts.append(rstd * (dyh - yhat * jnp.mean(dyh * yhat, axis=-1, keepdims=True)))
        du = jnp.concatenate(du_parts, axis=1)
        dy = du * zf * sz
        dz_ref[...] = (du * yv * sz * (1.0 + zf * (1.0 - sz))).astype(BF16)
        dy_s[...] = dy
        dyb_s[...] = dy.astype(BF16)
        dyeb_s[...] = (dy * ex_s[...]).astype(BF16)
        gdsk_ref[...] += jnp.sum(dy * xc_s[:, :ds], axis=0, keepdims=True)
        lane = lax.broadcasted_iota(jnp.int32, (1, LANES), 1)
        half0 = lane < HEAD_DIM
        x_parts, yo_parts, t4_parts = [], [], []
        dcs = jnp.zeros((CHUNK, LANES), F32)
        for g in range(SSD_GROUPS):
            gs = slice(g * gw, (g + 1) * gw)
            bsl = slice(ds + g * ns, ds + (g + 1) * ns)
            csl = slice(ds + SSD_GROUPS * ns + g * ns, ds + SSD_GROUPS * ns + (g + 1) * ns)
            bg = xc_s[:, bsl].astype(BF16)
            cg = xc_s[:, csl].astype(BF16)
            gm = _dot(cg, bg, NT)
            gm_t = _dot(bg, cg, NT)
            stg_b = hin_ref[0, :, gs]
            dstg = dst_ref[:, gs]
            dstg_b = dstg.astype(BF16)
            t4_parts.append(jnp.sum(dstg * stg_b.astype(F32), axis=0, keepdims=True))
            zst = _dot(bg, dstg_b) * wx_s[:, gs]
            x_parts.append(xc_s[:, gs] * dtx_s[:, gs] * zst)
            yo_parts.append(dy_s[:, gs] * (_dot(cg, stg_b) * ex_s[:, gs]))
            dgsum = jnp.zeros((CHUNK, CHUNK), F32)
            dgtsum = jnp.zeros((CHUNK, CHUNK), F32)
            for pr in range(gw // LANES):
                sl = slice(g * gw + pr * LANES, g * gw + (pr + 1) * LANES)
                xp = xdtb_s[:, sl]
                dyp = dyb_s[:, sl]
                dxd = zst[:, pr * LANES:(pr + 1) * LANES]
                for j in range(2):
                    h = g * hpg + 2 * pr + j
                    sel_l = half0 if j == 0 else jnp.logical_not(half0)
                    seg = cs[:, h:h + 1] - cs_t[h:h + 1, :]
                    lm = jnp.where(causal, jnp.exp(jnp.minimum(seg, 0.0)), 0.0)
                    lmt = lm.T
                    dyp_m = jnp.where(sel_l, dyp, jnp.zeros_like(dyp))
                    xp_m = jnp.where(sel_l, xp, jnp.zeros_like(xp))
                    dxd = dxd + _dot((gm_t * lmt).astype(BF16), dyp_m)
                    dg = _dot(dyp_m, xp, NT) * lm
                    dgt = _dot(xp_m, dyp, NT) * lmt
                    dgsum = dgsum + dg
                    dgtsum = dgtsum + dgt
                    qrow = (jnp.sum(dg * gm, axis=1, keepdims=True) - jnp.sum(dgt * gm_t, axis=1, keepdims=True))
                    dcs = dcs + jnp.where(lane == h, qrow, 0.0)
                dxc_ref[:, sl] = dxd
            dxc_ref[:, csl] = _dot(dgsum.astype(BF16), bg) + _dot(dyeb_s[:, gs], stg_b, NT)
            dxc_ref[:, bsl] = _dot(dgtsum.astype(BF16), cg) + _dot(xwb_s[:, gs], dstg_b, NT)
            dst_ref[:, gs] = dstg * cdec_x[:, gs] + _dot(cg, dyeb_s[:, gs], TN)

        dxdt = dxc_ref[:, :ds]
        xst = _dot_sel(jnp.concatenate(x_parts, axis=1), selm)
        yo = _dot_sel(jnp.concatenate(yo_parts, axis=1), selm)
        t4 = _dot_sel(jnp.concatenate([jnp.concatenate(t4_parts, axis=1), jnp.zeros((7, ds), F32)], axis=0), selm)
        dcl = jnp.sum(xst, axis=0, keepdims=True) + cdec * t4[0:1, :]
        dcs = dcs + yo - xst + jnp.where(rows == CHUNK - 1, dcl, 0.0)
        da_ = _dot_tri(rtri, dcs)
        ddt = _dot_sel(dxdt * xc_s[:, :ds], selm) + da_ * a_row
        dcf_blk = dcf_ref[...]
        dlogf = _dot_tri(rtri, dcf_blk) + fcar_ref[...]
        fcar_ref[...] += jnp.sum(dcf_blk, axis=0, keepdims=True)
        sgd = _sigmoid(dtr)
        ddtf = (jnp.where(is_dt, ddt * sgd, 0.0) + jnp.where(is_f, dlogf * (1.0 - sgd), 0.0)) * rowmask
        ddtf_ref[...] = ddtf
        gsm_ref[0:1, :] += jnp.sum(ddtf, axis=0, keepdims=True)
        gsm_ref[1:2, :] += jnp.sum(da_ * dt, axis=0, keepdims=True) * a_row

        dxc_ref[:, :ds] = dxdt * dtx_s[...] + dsk_ref[...] * dy_s[...]
        dpre = dxc_ref[...] * dsl_s[...]
        nxt_ref[0:CHUNK, :] = dpre
        gcb_ref[0:1, :] += jnp.sum(dpre, axis=0, keepdims=True)
        xr = xbc_ref[...].astype(F32)
        gcw_ref[CONV_K - 1:CONV_K, :] += jnp.sum(dpre * xr, axis=0, keepdims=True)
        dxr = cw_ref[CONV_K - 1:CONV_K, :] * dpre
        for j in range(1, CONV_K):
            up = nxt_ref[j:j + CHUNK, :]
            gcw_ref[CONV_K - 1 - j:CONV_K - j, :] += jnp.sum(up * xr, axis=0, keepdims=True)
            dxr = dxr + cw_ref[CONV_K - 1 - j:CONV_K - j, :] * up
        nxt_ref[CHUNK:, :] = dpre[0:8, :]
        dxbc_ref[...] = dxr.astype(BF16)

        @pl.when(step == nch - 1)
        def _():
            gsm_ref[2:3, :] = _dot_sel(jnp.broadcast_to(gdsk_ref[...], (8, ds)), selm)[0:1, :]

    rev = lambda s: nch - 1 - s
    blk = lambda w: pl.BlockSpec((CHUNK, w), lambda s: (rev(s), 0))
    return pl.pallas_call(
        body, name="ssd_bwd", grid=(nch,),
        in_specs=[blk(ds), blk(ds), blk(ds), blk(cd), blk(cd),
                  blk(LANES), pl.BlockSpec((1, ns, ds), lambda s: (rev(s), 0, 0)), blk(LANES),
                  _full((CONV_K, cd)), _full((1, LANES)), _full((1, LANES)),
                  _full((1, ds)), _full((1, ds)), _full((LANES, ds)), _full((ds, LANES))],
        out_specs=[blk(cd), blk(ds), blk(LANES), _full((8, cd)), _full((8, cd)), _full((8, ds)), _full((8, LANES))],
        out_shape=[jax.ShapeDtypeStruct((p, cd), BF16), jax.ShapeDtypeStruct((p, ds), BF16),
                   jax.ShapeDtypeStruct((p, LANES), F32), jax.ShapeDtypeStruct((8, cd), F32),
                   jax.ShapeDtypeStruct((8, cd), F32), jax.ShapeDtypeStruct((8, ds), F32),
                   jax.ShapeDtypeStruct((8, LANES), F32)],
        scratch_shapes=[pltpu.VMEM((ns, ds), F32), pltpu.VMEM((CHUNK + 8, cd), F32), pltpu.VMEM((1, LANES), F32),
                        pltpu.VMEM((1, ds), F32), pltpu.VMEM((CHUNK, cd), F32),
                        pltpu.VMEM((CHUNK, cd), F32), pltpu.VMEM((CHUNK, cd), F32),
                        pltpu.VMEM((CHUNK, ds), F32), pltpu.VMEM((CHUNK, ds), F32), pltpu.VMEM((CHUNK, ds), F32),
                        pltpu.VMEM((CHUNK, ds), F32), pltpu.VMEM((CHUNK, ds), BF16), pltpu.VMEM((CHUNK, ds), BF16),
                        pltpu.VMEM((CHUNK, ds), BF16), pltpu.VMEM((CHUNK, ds), BF16)],
        compiler_params=_cparams(("arbitrary",)),
    )(dyssd, y, z, xbc, pre, dtf, hin, dcf, conv_w, brow, alog, dskip_l, ssd_norm, sel_t, sel)


def _attn_fwd(q, k, v, ck, blk):
    p, da = q.shape
    npair, nkb = ck.shape[0], ck.shape[1]
    scale = 1.0 / math.sqrt(HEAD_DIM)

    def body(q_ref, k_ref, v_ref, ck_ref, o_ref, lse_ref):
        i = pl.program_id(1)
        lane = lax.broadcasted_iota(jnp.int32, (1, LANES), 1)
        sels = [lane < HEAD_DIM, lane >= HEAD_DIM]
        ones = [jnp.where(lane == HEAD_DIM, 1.0, 0.0).astype(BF16), jnp.where(lane == 0, 1.0, 0.0).astype(BF16)]
        qb = q_ref[...] * scale

        def step(kb, carry, masked, nk=1):
            r0 = pl.multiple_of(kb * blk, blk)
            ks = k_ref[pl.ds(r0, nk * blk), :]
            vs = v_ref[pl.ds(r0, nk * blk), :]
            kk = jnp.concatenate([jnp.where(sel, ks, jnp.zeros_like(ks)) for sel in sels], axis=0)
            s_both = _dot(qb, kk, NT)
            out = []
            for j in range(2):
                m, acc = carry[2 * j], carry[2 * j + 1]
                ckr = jnp.concatenate([ck_ref[0, kb + t, j:j + 1, :] for t in range(nk)], axis=1)
                s = s_both[:, j * nk * blk:(j + 1) * nk * blk] - ckr
                if masked:
                    col = lax.broadcasted_iota(jnp.int32, (blk, nk * blk), 1) - (nk - 1) * blk
                    s = jnp.where(col <= lax.broadcasted_iota(jnp.int32, (blk, nk * blk), 0), s, NEG)
                mn = jnp.maximum(m, jnp.max(s, axis=-1, keepdims=True))
                pr = jnp.exp(s - mn).astype(BF16)
                acc = jnp.exp(m - mn) * acc + _dot(pr, jnp.where(sels[j], vs, ones[j]))
                out += [mn, acc]
            return tuple(out)

        init = (jnp.full((blk, 1), NEG, F32), jnp.zeros((blk, LANES), F32)) * 2

        def finish(carry):
            m0, a0, m1, a1 = carry
            l0 = a0[:, HEAD_DIM:HEAD_DIM + 1]
            l1 = a1[:, 0:1]
            o_ref[...] = jnp.where(sels[0], a0 / l0, a1 / l1).astype(BF16)
            lse_ref[...] = jnp.where(sels[0], m0 + jnp.log(l0), m1 + jnp.log(l1))

        @pl.when(i == 0)
        def _():
            finish(step(0, init, True))

        @pl.when(i > 0)
        def _():
            below = i - 1
            n4 = below // 4
            n2 = (below - 4 * n4) // 2
            carry = lax.fori_loop(0, n4, lambda t, c: step(4 * t, c, False, 4), init)
            carry = lax.fori_loop(0, n2, lambda t, c: step(4 * n4 + 2 * t, c, False, 2), carry)
            carry = lax.fori_loop(4 * n4 + 2 * n2, below, lambda kb, c: step(kb, c, False), carry)
            finish(step(below, carry, True, 2))

    return pl.pallas_call(
        body, name="attn_fwd", grid=(npair, p // blk),
        in_specs=[pl.BlockSpec((blk, LANES), lambda h, i: (i, h)),
                  pl.BlockSpec((p, LANES), lambda h, i: (0, h)), pl.BlockSpec((p, LANES), lambda h, i: (0, h)),
                  pl.BlockSpec((1, nkb, 8, blk), lambda h, i: (h, 0, 0, 0))],
        out_specs=[pl.BlockSpec((blk, LANES), lambda h, i: (i, h)), pl.BlockSpec((blk, LANES), lambda h, i: (i, h))],
        out_shape=[jax.ShapeDtypeStruct((p, da), BF16), jax.ShapeDtypeStruct((p, da), F32)],
        compiler_params=_cparams(("parallel", "arbitrary")),
    )(q, k, v, ck)


def _attn_bwd(q, k, v, o, do, lse_rep, ck, blk):
    p, da = q.shape
    npair, nkb = ck.shape[0], ck.shape[1]
    nq = p // blk
    scale = 1.0 / math.sqrt(HEAD_DIM)

    def body(k_ref, v_ref, q_ref, do_ref, o_ref, lse_ref, ck_ref, dk_ref, dv_ref, dq_ref, dcs_ref, rsum_ref, dq_acc):
        jb = pl.program_id(1)

        @pl.when(jb == 0)
        def _():
            dq_acc[...] = jnp.zeros_like(dq_acc)

        ks = k_ref[...]
        vs = v_ref[...]
        lane = lax.broadcasted_iota(jnp.int32, (1, LANES), 1)
        sels = [lane < HEAD_DIM, lane >= HEAD_DIM]
        ones = [jnp.where(lane == HEAD_DIM, 1.0, 0.0).astype(BF16), jnp.where(lane == 0, 1.0, 0.0).astype(BF16)]
        kss = ks * scale
        kmo = [jnp.where(sels[j], kss, ones[j]) for j in range(2)]
        cmask = (lax.broadcasted_iota(jnp.int32, (blk, blk), 1) <= lax.broadcasted_iota(jnp.int32, (blk, blk), 0))

        def step(ib, carry, masked, nb=1):
            rows = nb * blk
            r0 = pl.multiple_of(ib * blk, blk)
            qb = q_ref[pl.ds(r0, rows), :] * scale
            dob = do_ref[pl.ds(r0, rows), :]
            prod = dob.astype(F32) * o_ref[pl.ds(r0, rows), :].astype(F32)
            out = []
            for j in range(2):
                dk, dv = carry[2 * j], carry[2 * j + 1]
                qm = jnp.where(sels[j], qb, jnp.zeros_like(qb))
                dom = jnp.where(sels[j], dob, jnp.zeros_like(dob))
                lse = lse_ref[pl.ds(r0, rows), HEAD_DIM * j:HEAD_DIM * j + 1]
                dlt = jnp.sum(jnp.where(sels[j], prod, 0.0), axis=-1, keepdims=True)
                s = _dot(qm, ks, NT) - ck_ref[0, 0, j:j + 1, :] - lse
                pm = jnp.exp(jnp.minimum(s, 0.0))
                if masked:
                    pm = jnp.where(cmask, pm, 0.0)
                ds_b = (pm * (_dot(dom, vs, NT) - dlt)).astype(BF16)
                dv = dv + _dot(pm.astype(BF16), dom, TN)
                dk = dk + _dot(ds_b, jnp.where(sels[j], qb, ones[j]), TN)
                dq_acc[pl.ds(r0, rows), LANES * j:LANES * (j + 1)] += _dot(ds_b, kmo[j])
                out += [dk, dv]
            return tuple(out)

        zero = jnp.zeros((blk, LANES), F32)
        carry = step(jb, (zero, zero, zero, zero), True)
        n4 = (nq - 1 - jb) // 4
        n2 = (nq - 1 - jb - 4 * n4) // 2
        carry = lax.fori_loop(0, n4, lambda t, c: step(jb + 1 + 4 * t, c, False, 4), carry)
        carry = lax.fori_loop(0, n2, lambda t, c: step(jb + 1 + 4 * n4 + 2 * t, c, False, 2), carry)
        dk0, dv0, dk1, dv1 = lax.fori_loop(jb + 1 + 4 * n4 + 2 * n2, nq, lambda ib, c: step(ib, c, False), carry)
        dk_ref[...] = jnp.where(sels[0], dk0, dk1).astype(BF16)
        dv_ref[...] = (dv0 + dv1).astype(BF16)
        pair8 = lambda c0, c1: jnp.where(lane == 0, c0, jnp.where(lane == 1, c1, 0.0)).T[0:8]
        dcs_ref[0] = pair8(dk0[:, HEAD_DIM:HEAD_DIM + 1], dk1[:, 0:1])

        @pl.when(jb == nkb - 1)
        def _():
            a0 = dq_acc[:, :LANES]
            a1 = dq_acc[:, LANES:]
            dq_ref[...] = jnp.where(sels[0], a0, a1).astype(BF16)
            rsum_ref[0] = pair8(a0[:, HEAD_DIM:HEAD_DIM + 1], a1[:, 0:1])

    colblk = pl.BlockSpec((blk, LANES), lambda h, j: (j, h))
    colfull = pl.BlockSpec((p, LANES), lambda h, j: (0, h))
    ckspec = pl.BlockSpec((1, 1, 8, blk), lambda h, j: (h, j, 0, 0))
    return pl.pallas_call(
        body, name="attn_bwd", grid=(npair, nkb),
        in_specs=[colblk, colblk, colfull, colfull, colfull, colfull, ckspec],
        out_specs=[colblk, colblk, colfull, pl.BlockSpec((1, 8, blk), lambda h, j: (h, 0, j)),
                   pl.BlockSpec((1, 8, p), lambda h, j: (h, 0, 0))],
        out_shape=[jax.ShapeDtypeStruct((p, da), BF16), jax.ShapeDtypeStruct((p, da), BF16),
                   jax.ShapeDtypeStruct((p, da), BF16), jax.ShapeDtypeStruct((npair, 8, p), F32),
                   jax.ShapeDtypeStruct((npair, 8, p), F32)],
        scratch_shapes=[pltpu.VMEM((p, 2 * LANES), F32)],
        compiler_params=_cparams(("parallel", "arbitrary")),
    )(k, v, q, do, o, lse_rep, ck)


def _tail_fwd(yssd, o, zatt, graw, head, x2, tgt2, wps, wpa, wout, gate_bias, norm_post, tm):
    p, ds = yssd.shape
    da = o.shape[1]
    d = x2.shape[1]
    nsub = tm // CHUNK

    def body(yssd_ref, o_ref, zatt_ref, g_ref, head_ref, *rest):
        x_refs, t_refs = rest[:nsub], rest[nsub:2 * nsub]
        (wps_ref, wpa_ref, wout_ref, gb_ref, np_ref,
         yatt_ref, mrg_ref, a_ref, b_ref, dzo_ref, dout_ref, red_ref) = rest[2 * nsub:]
        i = pl.program_id(0)

        @pl.when(i == 0)
        def _():
            red_ref[...] = jnp.zeros_like(red_ref)

        first = jnp.where(i == 0, head_ref[...], x_refs[0][...])
        h = jnp.concatenate([first] + [r[...] for r in x_refs[1:]], axis=0)
        tgt = jnp.concatenate([r[...] for r in t_refs], axis=0)
        rows = lax.broadcasted_iota(jnp.int32, (tm, 1), 0)
        valid = jnp.where((i > 0) | (rows >= CHUNK), 1.0, 0.0)
        ob = o_ref[...].astype(F32)
        za = zatt_ref[...].astype(F32)
        yatt_b = (ob * za * _sigmoid(za)).astype(BF16)
        yatt_ref[...] = yatt_b
        a = _dot(yssd_ref[...], wps_ref[...])
        b = _dot(yatt_b, wpa_ref[...])
        a_ref[...] = a.astype(BF16)
        b_ref[...] = b.astype(BF16)
        gr = g_ref[...].astype(F32) + gb_ref[...]
        mrg_b = (_sigmoid(gr[:, :d]) * a + _sigmoid(gr[:, d:]) * b).astype(BF16)
        mrg_ref[...] = mrg_b
        zo = _dot(mrg_b, wout_ref[...])
        rstd = lax.rsqrt(jnp.mean(zo * zo, axis=-1, keepdims=True) + EPS)
        zh = zo * rstd
        npw = np_ref[...]
        err = (h + zh * npw - tgt) * valid
        dout = err * (1.0 / d)
        dout_ref[...] = dout
        dzh = dout * npw
        dzo_ref[...] = (rstd * (dzh - zh * jnp.mean(dzh * zh, axis=-1, keepdims=True))).astype(BF16)
        red_ref[0:1, :] += jnp.sum(dout * zh, axis=0, keepdims=True)
        red_ref[1:2, 0:1] += jnp.sum(jnp.sum(err * err, axis=1, keepdims=True), axis=0, keepdims=True) * (0.5 / d)

    row = lambda w: pl.BlockSpec((tm, w), lambda i: (i, 0))
    once = lambda shape: pl.BlockSpec(shape, lambda i: (0,) * len(shape), pipeline_mode=pl.Buffered(1))
    subs = _x_row_specs(tm, d)
    sd = jax.ShapeDtypeStruct
    return pl.pallas_call(
        body, name="tail_fwd", grid=(p // tm,),
        in_specs=[row(ds), row(da), row(da), row(2 * d), _full((CHUNK, d))] + subs + subs
                 + [once((ds, d)), once((da, d)), once((d, d)), _full((1, 2 * d)), _full((1, d))],
        out_specs=[row(da), row(d), row(d), row(d), row(d), row(d), _full((8, d))],
        out_shape=[sd((p, da), BF16), sd((p, d), BF16), sd((p, d), BF16), sd((p, d), BF16), sd((p, d), BF16),
                   sd((p, d), F32), sd((8, d), F32)],
        compiler_params=_cparams(("arbitrary",)),
    )(yssd, o, zatt, graw, head, *([x2] * nsub), *([tgt2] * nsub), wps, wpa, wout, gate_bias, norm_post)


def _tail_bwd(dzo, a_b, b_b, graw, o, zatt, wps, wpa, wout, gate_bias, tm):
    p, d = dzo.shape
    ds, da = wps.shape[0], wpa.shape[0]

    def body(dzo_ref, a_ref, b_ref, g_ref, o_ref, zatt_ref, wps_ref, wpa_ref, wout_ref, gb_ref,
             da_ref, db_ref, dg_ref, dyssd_ref, do_ref, dzatt_ref, red_ref):
        i = pl.program_id(0)

        @pl.when(i == 0)
        def _():
            red_ref[...] = jnp.zeros_like(red_ref)

        gr = g_ref[...].astype(F32) + gb_ref[...]
        gs = _sigmoid(gr[:, :d])
        ga = _sigmoid(gr[:, d:])
        dm = _dot(dzo_ref[...], wout_ref[...], NT)
        da_b = (gs * dm).astype(BF16)
        db_b = (ga * dm).astype(BF16)
        da_ref[...] = da_b
        db_ref[...] = db_b
        dgs = dm * a_ref[...].astype(F32) * gs * (1.0 - gs)
        dga = dm * b_ref[...].astype(F32) * ga * (1.0 - ga)
        dg_ref[:, :d] = dgs.astype(BF16)
        dg_ref[:, d:] = dga.astype(BF16)
        red_ref[0:1, :d] += jnp.sum(dgs, axis=0, keepdims=True)
        red_ref[0:1, d:] += jnp.sum(dga, axis=0, keepdims=True)
        dyssd_ref[...] = _dot(da_b, wps_ref[...], NT).astype(BF16)
        dya = _dot(db_b, wpa_ref[...], NT)
        ob = o_ref[...].astype(F32)
        za = zatt_ref[...].astype(F32)
        sza = _sigmoid(za)
        do_ref[...] = (dya * za * sza).astype(BF16)
        dzatt_ref[...] = (dya * ob * sza * (1.0 + za * (1.0 - sza))).astype(BF16)

    row = lambda w: pl.BlockSpec((tm, w), lambda i: (i, 0))
    once = lambda shape: pl.BlockSpec(shape, lambda i: (0,) * len(shape), pipeline_mode=pl.Buffered(1))
    sd = jax.ShapeDtypeStruct
    return pl.pallas_call(
        body, name="tail_bwd", grid=(p // tm,),
        in_specs=[row(d), row(d), row(d), row(2 * d), row(da), row(da),
                  once((ds, d)), once((da, d)), once((d, d)), _full((1, 2 * d))],
        out_specs=[row(d), row(d), row(2 * d), row(ds), row(da), row(da), _full((8, 2 * d))],
        out_shape=[sd((p, d), BF16), sd((p, d), BF16), sd((p, 2 * d), BF16), sd((p, ds), BF16), sd((p, da), BF16),
                   sd((p, da), BF16), sd((8, 2 * d), F32)],
        compiler_params=_cparams(("arbitrary",)),
    )(dzo, a_b, b_b, graw, o, zatt, wps, wpa, wout, gate_bias)


def _adamw_math(w, g, m, v):
    m2 = ADAM_B1 * m + (1.0 - ADAM_B1) * g
    v2 = ADAM_B2 * v + (1.0 - ADAM_B2) * (g * g)
    m_hat = m2 / (1.0 - ADAM_B1 ** ADAM_STEP)
    v_hat = v2 / (1.0 - ADAM_B2 ** ADAM_STEP)
    delta = -ADAM_LR * (m_hat / (jnp.sqrt(v_hat) + ADAM_EPS) + ADAM_WD * w)
    return delta, m2, v2


def _adamw_small(params, red, name):
    names = list(params)
    n = len(names)
    extra = [params[k][3] for k in names if not isinstance(params[k][3], tuple)]

    def body(*refs):
        w_refs, m_refs, v_refs = refs[:n], refs[n:2 * n], refs[2 * n:3 * n]
        red_ref = refs[3 * n]
        g_refs = iter(refs[3 * n + 1:3 * n + 1 + len(extra)])
        outs = refs[3 * n + 1 + len(extra):]
        for i, k in enumerate(names):
            where = params[k][3]
            rows, cols = w_refs[i].shape
            if isinstance(where, tuple):
                g = red_ref[where[0]:where[0] + rows, where[1]:where[1] + cols]
            else:
                g = next(g_refs)[...]
            delta, m2, v2 = _adamw_math(w_refs[i][...], g, m_refs[i][...], v_refs[i][...])
            for o, val in zip(outs[4 * i:4 * i + 4], (g, delta, m2, v2)):
                o[...] = val

    vm = pl.BlockSpec(memory_space=pltpu.VMEM)
    ws, ms, vs = ([params[k][j] for k in names] for j in range(3))
    out = pl.pallas_call(
        body, name=name,
        out_shape=[jax.ShapeDtypeStruct(w.shape, F32) for w in ws for _ in range(4)],
        in_specs=[vm] * (3 * n + 1 + len(extra)), out_specs=[vm] * (4 * n),
    )(*ws, *ms, *vs, red, *extra)
    return {k: tuple(out[4 * i:4 * i + 4]) for i, k in enumerate(names)}


def _adamw(w, g, m, v, name, parts=False, part_row0=0):
    r, cdim = w.shape
    tr, tc, by_rows = _tiles_2d(r, cdim)
    pick = (lambda i: (i, 0)) if by_rows else (lambda i: (0, i))
    assert part_row0 % tr == 0
    gpick = (lambda i: (i + part_row0 // tr, 0)) if by_rows else (lambda i: (part_row0 // tr, i))

    def body(w_ref, g_ref, m_ref, v_ref, go_ref, d_ref, mo_ref, vo_ref):
        if parts:
            g = g_ref[0].astype(F32)
            for s in range(1, g_ref.shape[0]):
                g = g + g_ref[s].astype(F32)
        else:
            g = g_ref[...]
        delta, m2, v2 = _adamw_math(w_ref[...], g, m_ref[...], v_ref[...])
        go_ref[...] = g
        d_ref[...] = delta
        mo_ref[...] = m2
        vo_ref[...] = v2

    blk = pl.BlockSpec((tr, tc), pick)
    gspec = pl.BlockSpec((g.shape[0], tr, tc), lambda i: (0,) + gpick(i)) if parts else blk
    return pl.pallas_call(
        body, name=name, grid=((r // tr) * (cdim // tc),),
        in_specs=[blk, gspec, blk, blk], out_specs=[blk] * 4,
        out_shape=[jax.ShapeDtypeStruct((r, cdim), F32)] * 4,
        compiler_params=_cparams(("parallel",)),
    )(w, g, m, v)


def _pad_cols(a, width):
    return jnp.pad(a, ((0, 0), (0, width - a.shape[1])))


def _pack_small_shard(conv_w_sh, meta_sh, width):
    return jnp.concatenate([_pad_cols(conv_w_sh, width), jnp.zeros((4, width), F32), _pad_cols(meta_sh, width)], axis=0)


def _pack_small_rep(norm_pre, norm_post, gate_bias, ssd_norm, conv_b, misc, width):
    rows = [norm_pre, norm_post, gate_bias, ssd_norm, conv_b, misc]
    return jnp.concatenate([_pad_cols(r, width) for r in rows] + [jnp.zeros((2, width), F32)], axis=0)


def kernel(x, meta_tokens, norm_pre, w_in, conv_w, conv_b, dt_bias, a_log, d_skip, ssd_norm, fgate_bias, gate_bias, w_proj_ssd, w_proj_att, w_out, norm_post, loss_target, m_meta_tokens, m_norm_pre, m_w_in, m_conv_w, m_conv_b, m_dt_bias, m_a_log, m_d_skip, m_ssd_norm, m_fgate_bias, m_gate_bias, m_w_proj_ssd, m_w_proj_att, m_w_out, m_norm_post, v_meta_tokens, v_norm_pre, v_w_in, v_conv_w, v_conv_b, v_dt_bias, v_a_log, v_d_skip, v_ssd_norm, v_fgate_bias, v_gate_bias, v_w_proj_ssd, v_w_proj_att, v_w_out, v_norm_post):
    seq, d = x.shape[1], x.shape[2]
    p = seq + CHUNK
    hs, ha = dt_bias.shape[1], fgate_bias.shape[1]
    ds, cd = ssd_norm.shape[1], conv_b.shape[1]
    da = ha * HEAD_DIM
    nc8 = w_in.shape[2]
    cws = cd // N_DEV
    msh = d // N_DEV
    r1, r2, r3 = ds // N_DEV, da // N_DEV, d // N_DEV
    me = _dev_index(*_my_pos())
    x2, tgt2 = x[0], loss_target[0]

    win_sh = jnp.transpose(w_in[0]).astype(BF16)
    rows_sh = jnp.concatenate([w_proj_ssd[0], w_proj_att[0], w_out[0]], axis=0).astype(BF16)
    small_sh = _pack_small_shard(conv_w[0], meta_tokens, cws)
    win_all, small_all = _all_gather([win_sh, small_sh], "gather_weights")
    rows_sh, win_all = lax.optimization_barrier((rows_sh, win_all))
    rows_sems, rows_thru, rows_land, rows_token = _bcast_start(rows_sh, "gather_rows_start")
    cuts = [0, ds, ds + cd, ds + cd + hs, ds + cd + hs + da, ds + cd + hs + 2 * da, ds + cd + hs + 3 * da,
            ds + cd + hs + 4 * da, ds + cd + hs + 4 * da + ha, ds + cd + hs + 4 * da + ha + 2 * d]

    def piece_rows(r0, r1):
        parts = [win_all[s, max(r0, s * nc8) - s * nc8:min(r1, (s + 1) * nc8) - s * nc8]
                 for s in range(N_DEV) if max(r0, s * nc8) < min(r1, (s + 1) * nc8)]
        return parts[0] if len(parts) == 1 else jnp.concatenate(parts, axis=0)

    w_z, w_xbc, w_dt, w_zatt, w_q, w_k, w_v, w_f, w_g = [piece_rows(cuts[i], cuts[i + 1]) for i in range(9)]
    w_dtf = jnp.concatenate([w_dt, w_f, jnp.zeros((LANES - hs - ha, d), BF16)], axis=0)
    conv_w_full = jnp.transpose(small_all[:, 0:CONV_K, :], (1, 0, 2)).reshape(CONV_K, cd)
    meta_full = jnp.transpose(small_all[:, 8:8 + N_META, :msh], (1, 0, 2)).reshape(N_META, d)
    head = jnp.concatenate([jnp.zeros((PADN, d), F32), meta_full + rows_token[0:1, 0:1]], axis=0)

    tm = _att_block(p)
    u = _prenorm_fwd(head, x2, norm_pre, tm)
    seg_w = [w_z, w_xbc, w_zatt, w_q, w_k, w_v, w_g]
    zs, xbc, zatt, q, k, v, graw = [
        _mm(u, w, "nt", BF16, _tile(p, (1408, tm)), _tile(w.shape[0], (1024, 512, 256, 128)), "inproj_%d" % i)
        for i, w in enumerate(seg_w)]
    dtf = _mm(u, w_dtf, "nt", F32, _tile(p, (1408, tm)), LANES, "inproj_dtf")

    brow = jnp.concatenate([dt_bias, fgate_bias, jnp.zeros((1, LANES - hs - ha), F32)], axis=1)
    alog_row = _pad_cols(a_log, LANES)
    dskip_l = jnp.repeat(d_skip, HEAD_DIM, axis=1)
    sel_t = (lax.broadcasted_iota(jnp.int32, (LANES, ds), 1) // HEAD_DIM
             == lax.broadcasted_iota(jnp.int32, (LANES, ds), 0)).astype(BF16)
    sel = sel_t.T
    y, yssd, hin, cf, pre = _ssd_fwd(xbc, zs, dtf, conv_w_full, conv_b, brow, alog_row, dskip_l, ssd_norm, sel_t, hs, ha)

    blk = _att_block(p)
    nkb, npair = p // blk, ha // 2
    cum = jnp.where(lax.broadcasted_iota(jnp.int32, (p, 1), 0) < PADN, -NEG, cf[:, hs:hs + ha])
    ck = jnp.transpose(cum.T.reshape(npair, 2, nkb, blk), (0, 2, 1, 3))
    ck = jnp.pad(ck, ((0, 0), (0, 0), (0, 6), (0, 0)))
    o, lse_rep = _attn_fwd(q, k, v, ck, blk)

    rows_all = _bcast_wait(rows_sems, rows_thru, rows_land, lse_rep, "gather_rows_wait")
    wps = rows_all[:, :r1].reshape(ds, d)
    wpa = rows_all[:, r1:r1 + r2].reshape(da, d)
    wout = rows_all[:, r1 + r2:].reshape(d, d)

    yatt, mrg, a_b, b_b, dzo, dout, red_fwd = _tail_fwd(
        yssd, o, zatt, graw, head, x2, tgt2, wps, wpa, wout, gate_bias, norm_post, tm)
    da_, db_, dgraw, dyssd, d_o, dzatt, red_bwd = _tail_bwd(dzo, a_b, b_b, graw, o, zatt, wps, wpa, wout, gate_bias, tm)

    tw = _tile(d, (512, 256, 128))
    g_wout = _mm(mrg, dzo, "tn", BF16, tw, d, "wgrad_out")
    g_wps = _mm(yssd, da_, "tn", BF16, _tile(ds, (512, 256, 128)), d, "wgrad_ps")
    g_wpa = _mm(yatt, db_, "tn", BF16, _tile(da, (512, 256, 128)), d, "wgrad_pa")

    core = lax.axis_index("c").astype(jnp.int32).reshape(1)
    chip = me // 2
    grows_parts = jnp.concatenate([g_wps.reshape(N_DEV, r1, d), g_wpa.reshape(N_DEV, r2, d),
                                   g_wout.reshape(N_DEV, r3, d)], axis=1)
    (sib_rows,) = _exchange_sibling([grows_parts], "scatter_rows_sibling")
    chip_rows = _pair_add(grows_parts, sib_rows, core, "pair_add_rows")
    r_sems, r_thru, r_lands, r_token = _exchange_chips_start([chip_rows], "scatter_rows_start")

    dk, dv, dq, dcs, rsum = _attn_bwd(q, k, v, o, d_o, lse_rep, ck + r_token[0:1, 0:1], blk)
    dcum = (rsum - dcs)[:, 0:2, :].reshape(ha, p).T
    dcf = jnp.pad(dcum, ((0, 0), (hs, LANES - hs - ha)))
    dxbc, dzs, ddtf, gcw, gcb, gnrm, gsm = _ssd_bwd(
        dyssd, y, zs, xbc, pre, dtf, hin, dcf, conv_w_full, brow, alog_row, dskip_l, ssd_norm, sel_t, sel, hs, ha)
    ddtf_b = ddtf.astype(BF16)

    dsegs = [dzs, dxbc, dzatt, dq, dk, dv, dgraw, ddtf_b]
    gsegs = [_mm(dsg, u, "tn", BF16, _tile(dsg.shape[1], (512, 256, 128)), d, "wgrad_in_%d" % i)
             for i, dsg in enumerate(dsegs)]
    g_z, g_xbc, g_zatt, g_q, g_k, g_v, g_g, g_dtf = gsegs
    gw_full = jnp.concatenate([g_z, g_xbc, g_dtf[:hs], g_zatt, g_q, g_k, g_v, g_dtf[hs:hs + ha], g_g], axis=0)
    gwin_parts = gw_full.reshape(N_DEV, nc8, d)

    (sib_win,) = _exchange_sibling([gwin_parts], "scatter_grads_sibling")
    chip_win = _pair_add(gwin_parts, sib_win, core, "pair_add_w_in")
    sems, thru, lands, token = _exchange_chips_start([chip_win], "scatter_grads_start")
    dsegs_after = dsegs[:-1] + [ddtf_b + token[0:1, 0:1].astype(BF16)]
    gx, ghead, gnp = _dgrad_prenorm(dsegs_after, seg_w + [w_dtf], head, x2, norm_pre, dout, tm, "dgrad_in")
    own_slot = lambda got, sent: lax.dynamic_update_slice_in_dim(
        got, lax.dynamic_slice_in_dim(sent, chip, 1, axis=0), chip, axis=0)
    (sent,), (got,) = _exchange_chips_wait(sems, thru, lands, gnp, "scatter_grads_wait")
    recv_win = own_slot(got, sent)
    (r_sent,), (r_got,) = _exchange_chips_wait(r_sems, r_thru, r_lands, gnp, "scatter_rows_wait")
    recv_rows = own_slot(r_got, r_sent)
    gmisc = jnp.concatenate([gsm[0:1], gsm[1:2], gsm[2:3], _pad_cols(red_fwd[1:2, 0:1], LANES)], axis=1)
    small_g = jnp.concatenate([
        _pack_small_rep(gnp[0:1], red_fwd[0:1], red_bwd[0:1], gnrm[0:1], gcb[0:1], gmisc, cd),
        _pad_cols(gcw[0:CONV_K], cd), jnp.zeros((4, cd), F32), _pad_cols(ghead[PADN:], cd)], axis=0)
    sg_sems, sg_thru, sg_land, sg_token = _bcast_start(small_g, "reduce_small_start")

    upd_in = _adamw(jnp.transpose(w_in[0]) + sg_token[0:1, 0:1], recv_win, jnp.transpose(m_w_in[0]),
                    jnp.transpose(v_w_in[0]), "adamw_w_in", parts=True)
    upd_ps = _adamw(w_proj_ssd[0] + sg_token[0:1, 0:1], recv_rows, m_w_proj_ssd[0], v_w_proj_ssd[0],
                    "adamw_w_proj_ssd", parts=True, part_row0=0)
    upd_pa = _adamw(w_proj_att[0], recv_rows, m_w_proj_att[0], v_w_proj_att[0], "adamw_w_proj_att", parts=True,
                    part_row0=r1)
    upd_out = _adamw(w_out[0], recv_rows, m_w_out[0], v_w_out[0], "adamw_w_out", parts=True, part_row0=r1 + r2)
    all_done = upd_in[1][0:8, 0:LANES] + upd_ps[1][0:8, 0:LANES] + upd_pa[1][0:8, 0:LANES] + upd_out[1][0:8, 0:LANES]
    red = _sum_slots(_bcast_wait(sg_sems, sg_thru, sg_land, all_done, "reduce_small_wait"), "reduce_small_sum")
    loss = red[5, 3 * LANES]
    g_conv_w = lax.dynamic_slice_in_dim(red[8:8 + CONV_K], me * cws, cws, axis=1)
    g_meta = lax.dynamic_slice_in_dim(red[16:16 + N_META, :d], me * msh, msh, axis=1)
    small = {
        "meta_tokens": (meta_tokens, m_meta_tokens, v_meta_tokens, g_meta),
        "norm_pre": (norm_pre, m_norm_pre, v_norm_pre, (0, 0)),
        "conv_w": (conv_w[0], m_conv_w[0], v_conv_w[0], g_conv_w),
        "conv_b": (conv_b, m_conv_b, v_conv_b, (4, 0)),
        "dt_bias": (dt_bias, m_dt_bias, v_dt_bias, (5, 0)),
        "a_log": (a_log, m_a_log, v_a_log, (5, LANES)),
        "d_skip": (d_skip, m_d_skip, v_d_skip, (5, 2 * LANES)),
        "ssd_norm": (ssd_norm, m_ssd_norm, v_ssd_norm, (3, 0)),
        "fgate_bias": (fgate_bias, m_fgate_bias, v_fgate_bias, (5, hs)),
        "gate_bias": (gate_bias, m_gate_bias, v_gate_bias, (2, 0)),
        "norm_post": (norm_post, m_norm_post, v_norm_post, (1, 0)),
    }
    upd_small = _adamw_small(small, red, "adamw_small")

    def leaves(i):
        sm = {k: v[i] for k, v in upd_small.items()}
        return [sm["meta_tokens"], sm["norm_pre"], jnp.transpose(upd_in[i])[None], sm["conv_w"][None], sm["conv_b"],
                sm["dt_bias"], sm["a_log"], sm["d_skip"], sm["ssd_norm"], sm["fgate_bias"], sm["gate_bias"],
                upd_ps[i][None], upd_pa[i][None], upd_out[i][None], sm["norm_post"]]

    return tuple([loss, gx[None]] + leaves(0) + leaves(1) + leaves(2) + leaves(3))
```

```python
import functools
import math

import jax
import jax.numpy as jnp
from jax import lax
from jax.experimental import pallas as pl
from jax.experimental.pallas import tpu as pltpu

F32 = jnp.float32
BF16 = jnp.bfloat16

N_DEV = 8
N_META = 16
CHUNK = 128
PADN = CHUNK - N_META
HEAD_DIM = 64
SSD_GROUPS = 4
CONV_K = 4
EPS = 1e-6
NEG = -1e30
LANES = 128
HALO = 16

ADAM_LR = 0.001
ADAM_B1 = 0.9
ADAM_B2 = 0.999
ADAM_EPS = 1e-08
ADAM_WD = 0.01
ADAM_STEP = 10

VMEM_LIMIT = 56 * 1024 * 1024

NN = (((1,), (0,)), ((), ()))
NT = (((1,), (1,)), ((), ()))
TN = (((0,), (0,)), ((), ()))
MESH = pl.DeviceIdType.MESH


def _dot(a, b, dims=NN):
    return lax.dot_general(a, b, dims, preferred_element_type=F32)


def _split2(x):
    hi = x.astype(BF16)
    lo = (x - hi.astype(F32)).astype(BF16)
    return hi, lo


def _dot_sel(x, sel):
    hi, lo = _split2(x)
    return _dot(hi, sel) + _dot(lo, sel)


def _dot_tri(tri, x):
    h1 = x.astype(BF16)
    r1 = x - h1.astype(F32)
    h2 = r1.astype(BF16)
    h3 = (r1 - h2.astype(F32)).astype(BF16)
    return _dot(tri, h1) + _dot(tri, h2) + _dot(tri, h3)


def _sigmoid(x):
    return 0.5 * jnp.tanh(0.5 * x) + 0.5


def _softplus(x):
    return jnp.maximum(x, 0.0) + jnp.log(1.0 + jnp.exp(-jnp.abs(x)))


def _cparams(sem=None, vmem=VMEM_LIMIT):
    kw = {"vmem_limit_bytes": vmem}
    if sem is not None:
        kw["dimension_semantics"] = sem
    return pltpu.CompilerParams(**kw)


def _full(shape):
    nd = len(shape)
    return pl.BlockSpec(shape, lambda *_: (0,) * nd)


def _att_block(p):
    return 384 if p % 384 == 0 else CHUNK


def _my_pos():
    return lax.axis_index("x"), lax.axis_index("y"), lax.axis_index("c")


def _dev_index(x, y, c):
    return 4 * x + 2 * y + c


FLIPS = [(fx, fy, fc) for fx in (0, 1) for fy in (0, 1) for fc in (0, 1)][1:]


def _flip(pos, f):
    return tuple((1 - p) if fi else p for p, fi in zip(pos, f))


def _all_gather(bufs, name):
    nb = len(bufs)

    def body(*refs):
        ins, outs = refs[:nb], refs[nb:2 * nb]
        send_sems, recv_sems, local_sems = refs[2 * nb:]
        x, y, c = _my_pos()
        me = _dev_index(x, y, c)
        sibling = (x, y, 1 - c)
        near = [(1 - x, y), (x, 1 - y)]
        far = (1 - x, 1 - y)
        relay_from = (c * (1 - x) + (1 - c) * x, c * y + (1 - c) * (1 - y))
        relay_to = (c * x + (1 - c) * (1 - x), c * (1 - y) + (1 - c) * y)

        def copy(b, k, block_idx, to, src=None):
            dst = outs[b].at[block_idx]
            return pltpu.make_async_remote_copy(
                src_ref=dst if src is None else src, dst_ref=dst,
                send_sem=send_sems.at[b, k], recv_sem=recv_sems.at[b, k],
                device_id=to, device_id_type=MESH)

        started = []
        for b in range(nb):
            mine = pltpu.make_async_copy(ins[b], outs[b].at[me], local_sems.at[b])
            mine.start()
            started.append(mine)
        sent = []
        for b in range(nb):
            sent.append(copy(b, 0, me, sibling, src=ins[b]))
            for j, chip in enumerate(near):
                sent.append(copy(b, 1 + j, me, (chip[0], chip[1], c), src=ins[b]))
        for cp in sent:
            cp.start()
        for j, chip in enumerate(near):
            blk = _dev_index(chip[0], chip[1], c)
            for b in range(nb):
                copy(b, 1 + j, blk, (x, y, c)).wait_recv()
                sent.append(copy(b, 4 + j, blk, sibling))
                sent[-1].start()
        for b in range(nb):
            sent.append(copy(b, 3, _dev_index(relay_from[0], relay_from[1], c), (relay_to[0], relay_to[1], c)))
            sent[-1].start()
        blk = _dev_index(far[0], far[1], c)
        for b in range(nb):
            copy(b, 3, blk, (x, y, c)).wait_recv()
            sent.append(copy(b, 6, blk, sibling))
            sent[-1].start()
        for b in range(nb):
            copy(b, 0, _dev_index(x, y, 1 - c), (x, y, c)).wait_recv()
        for j, chip in enumerate(near + [far]):
            blk = _dev_index(chip[0], chip[1], 1 - c)
            for b in range(nb):
                copy(b, 4 + j, blk, (x, y, c)).wait_recv()
        for cp in sent:
            cp.wait_send()
        for mine in started:
            mine.wait()

    any_spec = pl.BlockSpec(memory_space=pl.ANY)
    return pl.pallas_call(
        body, name=name,
        out_shape=[jax.ShapeDtypeStruct((N_DEV,) + b.shape, b.dtype) for b in bufs],
        in_specs=[any_spec] * nb, out_specs=[any_spec] * nb,
        scratch_shapes=[pltpu.SemaphoreType.DMA((nb, 7)), pltpu.SemaphoreType.DMA((nb, 7)),
                        pltpu.SemaphoreType.DMA((nb,))],
    )(*bufs)


N_CHIP = 4
CHIP_FLIPS = [(1, 0), (0, 1), (1, 1)]


def _exchange_sibling(bufs, name):
    nb = len(bufs)

    def body(*refs):
        ins, outs = refs[:nb], refs[nb:2 * nb]
        send_sems, recv_sems = refs[2 * nb:]
        x, y, c = _my_pos()

        def copy(b, k):
            return pltpu.make_async_remote_copy(
                src_ref=ins[b].at[2 * k + (1 - c)], dst_ref=outs[b].at[k],
                send_sem=send_sems.at[b, k], recv_sem=recv_sems.at[b, k],
                device_id=(x, y, 1 - c), device_id_type=MESH)

        cps = [copy(b, k) for b in range(nb) for k in range(N_CHIP)]
        for cp in cps:
            cp.start()
        for cp in cps:
            cp.wait()

    any_spec = pl.BlockSpec(memory_space=pl.ANY)
    return pl.pallas_call(
        body, name=name,
        out_shape=[jax.ShapeDtypeStruct((N_CHIP,) + b.shape[1:], b.dtype) for b in bufs],
        in_specs=[any_spec] * nb, out_specs=[any_spec] * nb,
        scratch_shapes=[pltpu.SemaphoreType.DMA((nb, N_CHIP)), pltpu.SemaphoreType.DMA((nb, N_CHIP))],
    )(*bufs)


def _pair_add(mine, recv, core, name):
    _, r, cdim = mine.shape
    tr, tc = r, cdim
    pick = lambda i: (i, 0)

    def body(core_ref, a_ref, b_ref, o_ref):
        o_ref[0] = (a_ref[0].astype(F32) + b_ref[0].astype(F32)).astype(o_ref.dtype)

    return pl.pallas_call(
        body, name=name,
        grid_spec=pltpu.PrefetchScalarGridSpec(
            num_scalar_prefetch=1, grid=(N_CHIP, (r // tr) * (cdim // tc)),
            in_specs=[pl.BlockSpec((1, tr, tc), lambda k, i, core_ref: (2 * k + core_ref[0],) + pick(i)),
                      pl.BlockSpec((1, tr, tc), lambda k, i, core_ref: (k,) + pick(i))],
            out_specs=pl.BlockSpec((1, tr, tc), lambda k, i, core_ref: (k,) + pick(i))),
        out_shape=jax.ShapeDtypeStruct((N_CHIP, r, cdim), mine.dtype),
        compiler_params=_cparams(("parallel", "parallel")),
    )(core, mine, recv)


def _chip_peer(x, y, f):
    return ((1 - x) if f[0] else x), ((1 - y) if f[1] else y)


def _exchange_chips_start(bufs, name):
    nb = len(bufs)
    nsem = 2 * 3 * nb

    def body(*refs):
        ins, lands = refs[:nb], refs[nb:2 * nb]
        sems = refs[2 * nb:2 * nb + nsem]
        token = refs[-1]
        x, y, c = _my_pos()
        for b in range(nb):
            for j, f in enumerate(CHIP_FLIPS):
                px, py = _chip_peer(x, y, f)
                pltpu.make_async_remote_copy(
                    src_ref=ins[b].at[2 * px + py], dst_ref=lands[b].at[2 * x + y],
                    send_sem=sems[2 * (3 * b + j)], recv_sem=sems[2 * (3 * b + j) + 1],
                    device_id=(px, py, c), device_id_type=MESH).start()
        token[...] = jnp.zeros_like(token)

    hbm = pl.BlockSpec(memory_space=pltpu.HBM)
    sem = pl.BlockSpec(memory_space=pltpu.SEMAPHORE)
    out = pl.pallas_call(
        body, name=name,
        out_shape=(*([pltpu.SemaphoreType.DMA(())] * nsem),
                   *[pltpu.HBM(b.shape, b.dtype) for b in bufs], *[pltpu.HBM(b.shape, b.dtype) for b in bufs],
                   jax.ShapeDtypeStruct((8, LANES), F32)),
        in_specs=[hbm] * (2 * nb),
        out_specs=(*([sem] * nsem), *([hbm] * (2 * nb)), pl.BlockSpec(memory_space=pltpu.VMEM)),
        input_output_aliases={i: nsem + i for i in range(2 * nb)},
        compiler_params=pltpu.CompilerParams(has_side_effects=pltpu.SideEffectType.DATAFLOW_SIDE_EFFECTING),
    )(*[pltpu.with_memory_space_constraint(b, pltpu.HBM) for b in bufs],
      *[pltpu.with_memory_space_constraint(lax.empty(b.shape, b.dtype), pltpu.HBM) for b in bufs])
    return out[:nsem], out[nsem:nsem + nb], out[nsem + nb:nsem + 2 * nb], out[-1]


def _exchange_chips_wait(sems, thru, lands, after, name):
    nb = len(thru)
    nsem = len(sems)

    def body(*refs):
        ins, lnd = refs[:nb], refs[nb:2 * nb]
        sem_refs = refs[2 * nb:2 * nb + nsem]
        x, y, c = _my_pos()
        for b in range(nb):
            for j, f in enumerate(CHIP_FLIPS):
                px, py = _chip_peer(x, y, f)
                cp = pltpu.make_async_remote_copy(
                    src_ref=ins[b].at[2 * px + py], dst_ref=lnd[b].at[2 * px + py],
                    send_sem=sem_refs[2 * (3 * b + j)], recv_sem=sem_refs[2 * (3 * b + j) + 1],
                    device_id=(px, py, c), device_id_type=MESH)
                cp.wait_send()
                cp.wait_recv()

    hbm = pl.BlockSpec(memory_space=pltpu.HBM)
    sem = pl.BlockSpec(memory_space=pltpu.SEMAPHORE)
    out = pl.pallas_call(
        body, name=name,
        out_shape=tuple([pltpu.HBM(b.shape, b.dtype) for b in thru] + [pltpu.HBM(b.shape, b.dtype) for b in lands]),
        in_specs=[hbm] * (2 * nb) + [sem] * nsem + [pl.BlockSpec(memory_space=pl.ANY)],
        out_specs=tuple([hbm] * (2 * nb)),
        input_output_aliases={i: i for i in range(2 * nb)},
        compiler_params=pltpu.CompilerParams(has_side_effects=pltpu.SideEffectType.DATAFLOW_SIDE_EFFECTING),
    )(*thru, *lands, *sems, after)
    return out[:nb], out[nb:]


def _bcast_start(buf, name):
    nsem = 2 * len(FLIPS)

    def body(src, land, *rest):
        sems, token = rest[:nsem], rest[-1]
        pos = _my_pos()
        for k, f in enumerate(FLIPS):
            pltpu.make_async_remote_copy(
                src_ref=src, dst_ref=land.at[_dev_index(*pos)], send_sem=sems[2 * k], recv_sem=sems[2 * k + 1],
                device_id=_flip(pos, f), device_id_type=MESH).start()
        token[...] = jnp.zeros_like(token)

    hbm = pl.BlockSpec(memory_space=pltpu.HBM)
    sem = pl.BlockSpec(memory_space=pltpu.SEMAPHORE)
    land_shape = (N_DEV,) + buf.shape
    out = pl.pallas_call(
        body, name=name,
        out_shape=(*([pltpu.SemaphoreType.DMA(())] * nsem), pltpu.HBM(buf.shape, buf.dtype),
                   pltpu.HBM(land_shape, buf.dtype), jax.ShapeDtypeStruct((8, LANES), F32)),
        in_specs=[hbm, hbm],
        out_specs=(*([sem] * nsem), hbm, hbm, pl.BlockSpec(memory_space=pltpu.VMEM)),
        input_output_aliases={0: nsem, 1: nsem + 1},
        compiler_params=pltpu.CompilerParams(has_side_effects=pltpu.SideEffectType.DATAFLOW_SIDE_EFFECTING),
    )(pltpu.with_memory_space_constraint(buf, pltpu.HBM),
      pltpu.with_memory_space_constraint(lax.empty(land_shape, buf.dtype), pltpu.HBM))
    return out[:nsem], out[nsem], out[nsem + 1], out[-1]


def _bcast_wait(sems, thru, land, after, name):
    nsem = len(sems)

    def body(src, lnd, *rest):
        sem_refs = rest[:nsem]
        pos = _my_pos()
        for k, f in enumerate(FLIPS):
            peer = _flip(pos, f)
            cp = pltpu.make_async_remote_copy(
                src_ref=src, dst_ref=lnd.at[_dev_index(*peer)], send_sem=sem_refs[2 * k],
                recv_sem=sem_refs[2 * k + 1], device_id=peer, device_id_type=MESH)
            cp.wait_send()
            cp.wait_recv()

    hbm = pl.BlockSpec(memory_space=pltpu.HBM)
    sem = pl.BlockSpec(memory_space=pltpu.SEMAPHORE)
    sent, got = pl.pallas_call(
        body, name=name,
        out_shape=(pltpu.HBM(thru.shape, thru.dtype), pltpu.HBM(land.shape, land.dtype)),
        in_specs=[hbm, hbm] + [sem] * nsem + [pl.BlockSpec(memory_space=pl.ANY)],
        out_specs=(hbm, hbm), input_output_aliases={0: 0, 1: 1},
        compiler_params=pltpu.CompilerParams(has_side_effects=pltpu.SideEffectType.DATAFLOW_SIDE_EFFECTING),
    )(thru, land, *sems, after)
    return lax.dynamic_update_slice_in_dim(got, sent[None], _dev_index(*_my_pos()), axis=0)


def _sum_slots(v, name):
    _, r, cdim = v.shape

    def body(v_ref, o_ref):
        acc = v_ref[0]
        for s in range(1, N_DEV):
            acc = acc + v_ref[s]
        o_ref[...] = acc

    return pl.pallas_call(
        body, name=name, out_shape=jax.ShapeDtypeStruct((r, cdim), F32),
        in_specs=[_full((N_DEV, r, cdim))], out_specs=_full((r, cdim)), grid=(1,),
        compiler_params=_cparams(("arbitrary",)),
    )(v)


def _mm(a, b, dims, out_dtype, tm, tn, name):
    if dims == "nn":
        (m, k), (_, n) = a.shape, b.shape
        a_spec = pl.BlockSpec((tm, k), lambda j, i: (i, 0))
        b_spec = pl.BlockSpec((k, tn), lambda j, i: (0, j))
        dn = NN
    elif dims == "nt":
        (m, k), (n, _) = a.shape, b.shape
        a_spec = pl.BlockSpec((tm, k), lambda j, i: (i, 0))
        b_spec = pl.BlockSpec((tn, k), lambda j, i: (j, 0))
        dn = NT
    else:
        (k, m), (_, n) = a.shape, b.shape
        a_spec = pl.BlockSpec((k, tm), lambda j, i: (0, i))
        b_spec = pl.BlockSpec((k, tn), lambda j, i: (0, j))
        dn = TN
    assert m % tm == 0 and n % tn == 0, (m, tm, n, tn)

    def body(a_ref, b_ref, o_ref):
        o_ref[...] = _dot(a_ref[...], b_ref[...], dn).astype(o_ref.dtype)

    return pl.pallas_call(
        body, name=name, grid=(n // tn, m // tm),
        in_specs=[a_spec, b_spec], out_specs=pl.BlockSpec((tm, tn), lambda j, i: (i, j)),
        out_shape=jax.ShapeDtypeStruct((m, n), out_dtype),
        compiler_params=_cparams(("parallel", "parallel")),
    )(a, b)


def _tiles_2d(r, cdim):
    if r % CHUNK == 0:
        return CHUNK, cdim, True
    return r, _tile(cdim, (256, 128)), False


def _dgrad_prenorm(a_list, b_list, head, x2, w, dout, tm, name):
    n_op = len(a_list)
    m, d = a_list[0].shape[0], b_list[0].shape[1]
    subs = _x_row_specs(tm, d)
    last = m // tm - 1
    rest = tm - CHUNK

    def body(*refs):
        a_refs, b_refs = refs[:n_op], refs[n_op:2 * n_op]
        head_ref = refs[2 * n_op]
        x_refs = refs[2 * n_op + 1:2 * n_op + 1 + len(subs)]
        w_ref, dout_ref, gx_ref, ghead_ref, gw_ref, dh_buf, sem = refs[2 * n_op + 1 + len(subs):]
        i = pl.program_id(0)

        def first_copy():
            return pltpu.make_async_copy(dh_buf.at[pl.ds(CHUNK, rest)], gx_ref.at[pl.ds(0, rest)], sem)

        def later_copy(step):
            return pltpu.make_async_copy(dh_buf, gx_ref.at[pl.ds(pl.multiple_of(step * tm - CHUNK, CHUNK), tm)], sem)

        @pl.when(i == 0)
        def _():
            gw_ref[...] = jnp.zeros_like(gw_ref)

        du = _dot(a_refs[0][...], b_refs[0][...])
        for k in range(1, n_op):
            du = du + _dot(a_refs[k][...], b_refs[k][...])
        first = jnp.where(i == 0, head_ref[...], x_refs[0][...])
        h = jnp.concatenate([first] + [r[...] for r in x_refs[1:]], axis=0)
        rstd = lax.rsqrt(jnp.mean(h * h, axis=-1, keepdims=True) + EPS)
        xhat = h * rstd
        dxh = du * w_ref[...]
        dh = rstd * (dxh - xhat * jnp.mean(dxh * xhat, axis=-1, keepdims=True)) + dout_ref[...]
        gw_ref[0:1, :] += jnp.sum(du * xhat, axis=0, keepdims=True)

        if rest and last >= 1:
            @pl.when(i == 1)
            def _():
                first_copy().wait()

        @pl.when(i >= (2 if rest else 1))
        def _():
            later_copy(i - 1).wait()

        dh_buf[...] = dh

        @pl.when(i == 0)
        def _():
            ghead_ref[...] = dh_buf[0:CHUNK, :]
            if rest:
                first_copy().start()
                if last == 0:
                    first_copy().wait()

        @pl.when(i >= 1)
        def _():
            later_copy(i).start()

        if last >= 1:
            @pl.when(i == last)
            def _():
                later_copy(i).wait()

    once = lambda b: pl.BlockSpec(b.shape, lambda i: (0, 0), pipeline_mode=pl.Buffered(1))
    row = lambda width: pl.BlockSpec((tm, width), lambda i: (i, 0))
    return pl.pallas_call(
        body, name=name, grid=(m // tm,),
        in_specs=([row(a.shape[1]) for a in a_list] + [once(b) for b in b_list]
                  + [_full((CHUNK, d))] + subs + [_full((1, d)), row(d)]),
        out_specs=[pl.BlockSpec(memory_space=pl.ANY), _full((CHUNK, d)), _full((8, d))],
        out_shape=[jax.ShapeDtypeStruct((m - CHUNK, d), F32), jax.ShapeDtypeStruct((CHUNK, d), F32),
                   jax.ShapeDtypeStruct((8, d), F32)],
        scratch_shapes=[pltpu.VMEM((tm, d), F32), pltpu.SemaphoreType.DMA],
        compiler_params=_cparams(("arbitrary",)),
    )(*a_list, *b_list, head, *([x2] * len(subs)), w, dout)


def _tile(n, prefs):
    for t in prefs:
        if n % t == 0:
            return t
    return n


def _rows3(i):
    return jnp.maximum(3 * i - 1, 0), 3 * i, 3 * i + 1


def _x_row_specs(tm, d):
    if tm == CHUNK:
        return [pl.BlockSpec((CHUNK, d), lambda i: (jnp.maximum(i - 1, 0), 0))]
    return [pl.BlockSpec((CHUNK, d), functools.partial(lambda i, k: (_rows3(i)[k], 0), k=k)) for k in range(3)]


def _prenorm_fwd(head, x2, w, tm):
    p, d = x2.shape[0] + CHUNK, x2.shape[1]
    subs = _x_row_specs(tm, d)

    def body(head_ref, *rest):
        x_refs, (w_ref, u_ref) = rest[:len(subs)], rest[len(subs):]
        i = pl.program_id(0)
        first = jnp.where(i == 0, head_ref[...], x_refs[0][...])
        h = jnp.concatenate([first] + [r[...] for r in x_refs[1:]], axis=0)
        ms = jnp.mean(h * h, axis=-1, keepdims=True)
        u_ref[...] = (h * lax.rsqrt(ms + EPS) * w_ref[...]).astype(BF16)

    return pl.pallas_call(
        body, name="prenorm_fwd", grid=(p // tm,),
        in_specs=[_full((CHUNK, d))] + subs + [_full((1, d))],
        out_specs=pl.BlockSpec((tm, d), lambda i: (i, 0)),
        out_shape=jax.ShapeDtypeStruct((p, d), BF16),
        compiler_params=_cparams(("arbitrary",)),
    )(head, *([x2] * len(subs)), w)


def _conv_pre(ext_ref, cw_ref, cb_ref):
    pre = cb_ref[...] + cw_ref[CONV_K - 1:CONV_K, :] * ext_ref[8:8 + CHUNK, :]
    for j in range(1, CONV_K):
        pre = pre + cw_ref[CONV_K - 1 - j:CONV_K - j, :] * ext_ref[8 - j:8 - j + CHUNK, :]
    return pre


def _ssd_scalars(dtf_ref, brow_ref, alog_ref, rowmask, hs, ha, tri):
    lane = lax.broadcasted_iota(jnp.int32, (1, LANES), 1)
    is_dt = lane < hs
    is_f = (lane >= hs) & (lane < hs + ha)
    dtr = dtf_ref[...] + brow_ref[...]
    sp = _softplus(dtr)
    dt = jnp.where(is_dt, sp, 0.0) * rowmask
    logf = jnp.where(is_f, jnp.minimum(dtr, 0.0) - jnp.log(1.0 + jnp.exp(-jnp.abs(dtr))), 0.0) * rowmask
    a_row = jnp.where(is_dt, -jnp.exp(alog_ref[...]), 0.0)
    run = _dot_tri(tri, dt * a_row + logf)
    return dtr, dt, a_row, run, is_dt, is_f


def _tri_mats():
    r = lax.broadcasted_iota(jnp.int32, (CHUNK, CHUNK), 0)
    c = lax.broadcasted_iota(jnp.int32, (CHUNK, CHUNK), 1)
    return r, c


def _ssd_fwd(xbc, z, dtf, conv_w, conv_b, brow, alog, dskip_l, ssd_norm, sel_t, hs, ha):
    p, cd = xbc.shape
    ds = z.shape[1]
    ns = (cd - ds) // (2 * SSD_GROUPS)
    gw = ds // SSD_GROUPS
    nch = p // CHUNK
    hpg = hs // SSD_GROUPS

    def body(xbc_ref, halo_ref, z_ref, dtf_ref, cw_ref, cb_ref, brow_ref, alog_ref, dsk_ref, nrm_ref, selt_ref,
             y_ref, yssd_ref, hin_ref, cf_ref, pre_ref, st_ref, carry_ref, yacc_ref, xc_s, ex_s, xdtb_s, xwb_s, ext_s):
        c = pl.program_id(0)

        @pl.when(c == 0)
        def _():
            st_ref[...] = jnp.zeros_like(st_ref)
            carry_ref[...] = jnp.zeros_like(carry_ref)

        rows = lax.broadcasted_iota(jnp.int32, (CHUNK, 1), 0)
        rowmask = jnp.where((rows >= PADN) | (c > 0), 1.0, 0.0)
        ri, ci = _tri_mats()
        causal = ri >= ci
        tri = jnp.where(causal, 1.0, 0.0).astype(BF16)

        ext_s[0:8, :] = halo_ref[...].astype(F32)[HALO - 8:, :] * jnp.where(c > 0, 1.0, 0.0)
        ext_s[8:, :] = xbc_ref[...].astype(F32)
        pre = _conv_pre(ext_s, cw_ref, cb_ref)
        pre_ref[...] = pre.astype(BF16)
        xc_s[...] = pre * _sigmoid(pre) * rowmask

        dtr, dt, a_row, run, is_dt, is_f = _ssd_scalars(dtf_ref, brow_ref, alog_ref, rowmask, hs, ha, tri)
        cf = run + carry_ref[...]
        cf_ref[...] = cf
        carry_ref[...] = jnp.where(is_f, cf[CHUNK - 1:CHUNK, :], 0.0)
        cs = jnp.where(is_dt, run, 0.0)
        cl = cs[CHUNK - 1:CHUNK, :]
        selt = selt_ref[...]
        ex_s[...] = _dot_sel(jnp.exp(cs), selt)
        cdec_x = _dot_sel(jnp.broadcast_to(jnp.exp(cl), (8, LANES)), selt)[0:1, :]
        cs_t = cs.T
        xdt = xc_s[:, :ds] * _dot_sel(dt, selt)
        xdtb_s[...] = xdt.astype(BF16)
        xwb_s[...] = (xdt * _dot_sel(jnp.exp(cl - cs), selt)).astype(BF16)

        lane = lax.broadcasted_iota(jnp.int32, (1, LANES), 1)
        half0 = lane < HEAD_DIM
        for g in range(SSD_GROUPS):
            bg = xc_s[:, ds + g * ns: ds + (g + 1) * ns].astype(BF16)
            cg = xc_s[:, ds + SSD_GROUPS * ns + g * ns: ds + SSD_GROUPS * ns + (g + 1) * ns].astype(BF16)
            gm = _dot(cg, bg, NT)
            gs = slice(g * gw, (g + 1) * gw)
            stg = st_ref[:, gs]
            stg_b = stg.astype(BF16)
            hin_ref[0, :, gs] = stg_b
            yoff = _dot(cg, stg_b) * ex_s[:, gs]
            for pr in range(gw // LANES):
                sl = slice(g * gw + pr * LANES, g * gw + (pr + 1) * LANES)
                xp = xdtb_s[:, sl]
                yd = jnp.zeros((CHUNK, LANES), F32)
                for j in range(2):
                    h = g * hpg + 2 * pr + j
                    seg = cs[:, h:h + 1] - cs_t[h:h + 1, :]
                    m = jnp.where(causal, gm * jnp.exp(jnp.minimum(seg, 0.0)), 0.0).astype(BF16)
                    sel = half0 if j == 0 else jnp.logical_not(half0)
                    yd = yd + _dot(m, jnp.where(sel, xp, jnp.zeros_like(xp)))
                yacc_ref[:, sl] = yd + yoff[:, pr * LANES:(pr + 1) * LANES] + dsk_ref[:, sl] * xc_s[:, sl]
            st_ref[:, gs] = stg * cdec_x[:, gs] + _dot(bg, xwb_s[:, gs], TN)

        y = yacc_ref[...]
        y_ref[...] = y.astype(BF16)
        zf = z_ref[...].astype(F32)
        u = y * zf * _sigmoid(zf)
        for g in range(SSD_GROUPS):
            gs = slice(g * gw, (g + 1) * gw)
            ug = u[:, gs]
            ms = jnp.mean(ug * ug, axis=-1, keepdims=True)
            yssd_ref[:, gs] = (ug * lax.rsqrt(ms + EPS) * nrm_ref[:, gs]).astype(BF16)

    rb = CHUNK // HALO
    return pl.pallas_call(
        body, name="ssd_fwd", grid=(nch,),
        in_specs=[pl.BlockSpec((CHUNK, cd), lambda c: (c, 0)),
                  pl.BlockSpec((HALO, cd), lambda c: (jnp.maximum(c * rb - 1, 0), 0)),
                  pl.BlockSpec((CHUNK, ds), lambda c: (c, 0)),
                  pl.BlockSpec((CHUNK, LANES), lambda c: (c, 0)),
                  _full((CONV_K, cd)), _full((1, cd)), _full((1, LANES)), _full((1, LANES)),
                  _full((1, ds)), _full((1, ds)), _full((LANES, ds))],
        out_specs=[pl.BlockSpec((CHUNK, ds), lambda c: (c, 0)), pl.BlockSpec((CHUNK, ds), lambda c: (c, 0)),
                   pl.BlockSpec((1, ns, ds), lambda c: (c, 0, 0)), pl.BlockSpec((CHUNK, LANES), lambda c: (c, 0)),
                   pl.BlockSpec((CHUNK, cd), lambda c: (c, 0))],
        out_shape=[jax.ShapeDtypeStruct((p, ds), BF16), jax.ShapeDtypeStruct((p, ds), BF16),
                   jax.ShapeDtypeStruct((nch, ns, ds), BF16), jax.ShapeDtypeStruct((p, LANES), F32),
                   jax.ShapeDtypeStruct((p, cd), BF16)],
        scratch_shapes=[pltpu.VMEM((ns, ds), F32), pltpu.VMEM((1, LANES), F32), pltpu.VMEM((CHUNK, ds), F32),
                        pltpu.VMEM((CHUNK, cd), F32), pltpu.VMEM((CHUNK, ds), F32),
                        pltpu.VMEM((CHUNK, ds), BF16), pltpu.VMEM((CHUNK, ds), BF16),
                        pltpu.VMEM((8 + CHUNK, cd), F32)],
        compiler_params=_cparams(("arbitrary",)),
    )(xbc, xbc, z, dtf, conv_w, conv_b, brow, alog, dskip_l, ssd_norm, sel_t)


def _ssd_bwd(dyssd, y, z, xbc, pre, dtf, hin, dcf, conv_w, brow, alog, dskip_l, ssd_norm, sel_t, sel, hs, ha):
    p, cd = xbc.shape
    ds = z.shape[1]
    ns = (cd - ds) // (2 * SSD_GROUPS)
    gw = ds // SSD_GROUPS
    nch = p // CHUNK
    hpg = hs // SSD_GROUPS

    def body(dyssd_ref, y_ref, z_ref, xbc_ref, pre_ref, dtf_ref, hin_ref, dcf_ref, cw_ref, brow_ref,
             alog_ref, dsk_ref, nrm_ref, selt_ref, sel_ref,
             dxbc_ref, dz_ref, ddtf_ref, gcw_ref, gcb_ref, gnrm_ref, gsm_ref,
             dst_ref, nxt_ref, fcar_ref, gdsk_ref, dxc_ref, xc_s, dsl_s, dtx_s, ex_s, wx_s, dy_s, xdtb_s, xwb_s,
             dyb_s, dyeb_s):
        step = pl.program_id(0)
        c = nch - 1 - step

        @pl.when(step == 0)
        def _():
            dst_ref[...] = jnp.zeros_like(dst_ref)
            nxt_ref[...] = jnp.zeros_like(nxt_ref)
            fcar_ref[...] = jnp.zeros_like(fcar_ref)
            gdsk_ref[...] = jnp.zeros_like(gdsk_ref)
            gcw_ref[...] = jnp.zeros_like(gcw_ref)
            gcb_ref[...] = jnp.zeros_like(gcb_ref)
            gnrm_ref[...] = jnp.zeros_like(gnrm_ref)
            gsm_ref[...] = jnp.zeros_like(gsm_ref)

        rows = lax.broadcasted_iota(jnp.int32, (CHUNK, 1), 0)
        rowmask = jnp.where((rows >= PADN) | (c > 0), 1.0, 0.0)
        ri, ci = _tri_mats()
        causal = ri >= ci
        anti = ci >= ri
        tri = jnp.where(causal, 1.0, 0.0).astype(BF16)
        rtri = jnp.where(anti, 1.0, 0.0).astype(BF16)

        pre = pre_ref[...].astype(F32)
        sg = _sigmoid(pre)
        xc_s[...] = pre * sg * rowmask
        dsl_s[...] = sg * (1.0 + pre * (1.0 - sg)) * rowmask

        dtr, dt, a_row, run, is_dt, is_f = _ssd_scalars(dtf_ref, brow_ref, alog_ref, rowmask, hs, ha, tri)
        cs = jnp.where(is_dt, run, 0.0)
        cl = cs[CHUNK - 1:CHUNK, :]
        selt = selt_ref[...]
        selm = sel_ref[...]
        dtx_s[...] = _dot_sel(dt, selt)
        ex_s[...] = _dot_sel(jnp.exp(cs), selt)
        wx_s[...] = _dot_sel(jnp.exp(cl - cs), selt)
        cdec = jnp.exp(cl)
        cdec_x = _dot_sel(jnp.broadcast_to(cdec, (8, LANES)), selt)[0:1, :]
        cs_t = cs.T
        xdt = xc_s[:, :ds] * dtx_s[...]
        xdtb_s[...] = xdt.astype(BF16)
        xwb_s[...] = (xdt * wx_s[...]).astype(BF16)

        yv = y_ref[...].astype(F32)
        zf = z_ref[...].astype(F32)
        sz = _sigmoid(zf)
        u = yv * zf * sz
        dyo = dyssd_ref[...].astype(F32)
        du_parts = []
        for g in range(SSD_GROUPS):
            gs = slice(g * gw, (g + 1) * gw)
            ug = u[:, gs]
            rstd = lax.rsqrt(jnp.mean(ug * ug, axis=-1, keepdims=True) + EPS)
            yhat = ug * rstd
            dyg = dyo[:, gs]
            gnrm_ref[0:1, gs] += jnp.sum(dyg * yhat, axis=0, keepdims=True)
            dyh = dyg * nrm_ref[:, gs]
            du_parts.append(rstd * (dyh - yhat * jnp.mean(dyh * yhat, axis=-1, keepdims=True)))
        du = jnp.concatenate(du_parts, axis=1)
        dy = du * zf * sz
        dz_ref[...] = (du * yv * sz * (1.0 + zf * (1.0 - sz))).astype(BF16)
        dy_s[...] = dy
        dyb_s[...] = dy.astype(BF16)
        dyeb_s[...] = (dy * ex_s[...]).astype(BF16)
        gdsk_ref[...] += jnp.sum(dy * xc_s[:, :ds], axis=0, keepdims=True)
        lane = lax.broadcasted_iota(jnp.int32, (1, LANES), 1)
        half0 = lane < HEAD_DIM
        x_parts, yo_parts, t4_parts = [], [], []
        dcs = jnp.zeros((CHUNK, LANES), F32)
        for g in range(SSD_GROUPS):
            gs = slice(g * gw, (g + 1) * gw)
            bsl = slice(ds + g * ns, ds + (g + 1) * ns)
            csl = slice(ds + SSD_GROUPS * ns + g * ns, ds + SSD_GROUPS * ns + (g + 1) * ns)
            bg = xc_s[:, bsl].astype(BF16)
            cg = xc_s[:, csl].astype(BF16)
            gm = _dot(cg, bg, NT)
            gm_t = _dot(bg, cg, NT)
            stg_b = hin_ref[0, :, gs]
            dstg = dst_ref[:, gs]
            dstg_b = dstg.astype(BF16)
            t4_parts.append(jnp.sum(dstg * stg_b.astype(F32), axis=0, keepdims=True))
            zst = _dot(bg, dstg_b) * wx_s[:, gs]
            x_parts.append(xc_s[:, gs] * dtx_s[:, gs] * zst)
            yo_parts.append(dy_s[:, gs] * (_dot(cg, stg_b) * ex_s[:, gs]))
            dgsum = jnp.zeros((CHUNK, CHUNK), F32)
            dgtsum = jnp.zeros((CHUNK, CHUNK), F32)
            for pr in range(gw // LANES):
                sl = slice(g * gw + pr * LANES, g * gw + (pr + 1) * LANES)
                xp = xdtb_s[:, sl]
                dyp = dyb_s[:, sl]
                dxd = zst[:, pr * LANES:(pr + 1) * LANES]
                for j in range(2):
                    h = g * hpg + 2 * pr + j
                    sel_l = half0 if j == 0 else jnp.logical_not(half0)
                    seg = cs[:, h:h + 1] - cs_t[h:h + 1, :]
                    lm = jnp.where(causal, jnp.exp(jnp.minimum(seg, 0.0)), 0.0)
                    lmt = lm.T
                    dyp_m = jnp.where(sel_l, dyp, jnp.zeros_like(dyp))
                    xp_m = jnp.where(sel_l, xp, jnp.zeros_like(xp))
                    dxd = dxd + _dot((gm_t * lmt).astype(BF16), dyp_m)
                    dg = _dot(dyp_m, xp, NT) * lm
                    dgt = _dot(xp_m, dyp, NT) * lmt
                    dgsum = dgsum + dg
                    dgtsum = dgtsum + dgt
                    qrow = (jnp.sum(dg * gm, axis=1, keepdims=True) - jnp.sum(dgt * gm_t, axis=1, keepdims=True))
                    dcs = dcs + jnp.where(lane == h, qrow, 0.0)
                dxc_ref[:, sl] = dxd
            dxc_ref[:, csl] = _dot(dgsum.astype(BF16), bg) + _dot(dyeb_s[:, gs], stg_b, NT)
            dxc_ref[:, bsl] = _dot(dgtsum.astype(BF16), cg) + _dot(xwb_s[:, gs], dstg_b, NT)
            dst_ref[:, gs] = dstg * cdec_x[:, gs] + _dot(cg, dyeb_s[:, gs], TN)

        dxdt = dxc_ref[:, :ds]
        xst = _dot_sel(jnp.concatenate(x_parts, axis=1), selm)
        yo = _dot_sel(jnp.concatenate(yo_parts, axis=1), selm)
        t4 = _dot_sel(jnp.concatenate([jnp.concatenate(t4_parts, axis=1), jnp.zeros((7, ds), F32)], axis=0), selm)
        dcl = jnp.sum(xst, axis=0, keepdims=True) + cdec * t4[0:1, :]
        dcs = dcs + yo - xst + jnp.where(rows == CHUNK - 1, dcl, 0.0)
        da_ = _dot_tri(rtri, dcs)
        ddt = _dot_sel(dxdt * xc_s[:, :ds], selm) + da_ * a_row
        dcf_blk = dcf_ref[...]
        dlogf = _dot_tri(rtri, dcf_blk) + fcar_ref[...]
        fcar_ref[...] += jnp.sum(dcf_blk, axis=0, keepdims=True)
        sgd = _sigmoid(dtr)
        ddtf = (jnp.where(is_dt, ddt * sgd, 0.0) + jnp.where(is_f, dlogf * (1.0 - sgd), 0.0)) * rowmask
        ddtf_ref[...] = ddtf
        gsm_ref[0:1, :] += jnp.sum(ddtf, axis=0, keepdims=True)
        gsm_ref[1:2, :] += jnp.sum(da_ * dt, axis=0, keepdims=True) * a_row

        dxc_ref[:, :ds] = dxdt * dtx_s[...] + dsk_ref[...] * dy_s[...]
        dpre = dxc_ref[...] * dsl_s[...]
        nxt_ref[0:CHUNK, :] = dpre
        gcb_ref[0:1, :] += jnp.sum(dpre, axis=0, keepdims=True)
        xr = xbc_ref[...].astype(F32)
        gcw_ref[CONV_K - 1:CONV_K, :] += jnp.sum(dpre * xr, axis=0, keepdims=True)
        dxr = cw_ref[CONV_K - 1:CONV_K, :] * dpre
        for j in range(1, CONV_K):
            up = nxt_ref[j:j + CHUNK, :]
            gcw_ref[CONV_K - 1 - j:CONV_K - j, :] += jnp.sum(up * xr, axis=0, keepdims=True)
            dxr = dxr + cw_ref[CONV_K - 1 - j:CONV_K - j, :] * up
        nxt_ref[CHUNK:, :] = dpre[0:8, :]
        dxbc_ref[...] = dxr.astype(BF16)

        @pl.when(step == nch - 1)
        def _():
            gsm_ref[2:3, :] = _dot_sel(jnp.broadcast_to(gdsk_ref[...], (8, ds)), selm)[0:1, :]

    rev = lambda s: nch - 1 - s
    blk = lambda w: pl.BlockSpec((CHUNK, w), lambda s: (rev(s), 0))
    return pl.pallas_call(
        body, name="ssd_bwd", grid=(nch,),
        in_specs=[blk(ds), blk(ds), blk(ds), blk(cd), blk(cd),
                  blk(LANES), pl.BlockSpec((1, ns, ds), lambda s: (rev(s), 0, 0)), blk(LANES),
                  _full((CONV_K, cd)), _full((1, LANES)), _full((1, LANES)),
                  _full((1, ds)), _full((1, ds)), _full((LANES, ds)), _full((ds, LANES))],
        out_specs=[blk(cd), blk(ds), blk(LANES), _full((8, cd)), _full((8, cd)), _full((8, ds)), _full((8, LANES))],
        out_shape=[jax.ShapeDtypeStruct((p, cd), BF16), jax.ShapeDtypeStruct((p, ds), BF16),
                   jax.ShapeDtypeStruct((p, LANES), F32), jax.ShapeDtypeStruct((8, cd), F32),
                   jax.ShapeDtypeStruct((8, cd), F32), jax.ShapeDtypeStruct((8, ds), F32),
                   jax.ShapeDtypeStruct((8, LANES), F32)],
        scratch_shapes=[pltpu.VMEM((ns, ds), F32), pltpu.VMEM((CHUNK + 8, cd), F32), pltpu.VMEM((1, LANES), F32),
                        pltpu.VMEM((1, ds), F32), pltpu.VMEM((CHUNK, cd), F32),
                        pltpu.VMEM((CHUNK, cd), F32), pltpu.VMEM((CHUNK, cd), F32),
                        pltpu.VMEM((CHUNK, ds), F32), pltpu.VMEM((CHUNK, ds), F32), pltpu.VMEM((CHUNK, ds), F32),
                        pltpu.VMEM((CHUNK, ds), F32), pltpu.VMEM((CHUNK, ds), BF16), pltpu.VMEM((CHUNK, ds), BF16),
                        pltpu.VMEM((CHUNK, ds), BF16), pltpu.VMEM((CHUNK, ds), BF16)],
        compiler_params=_cparams(("arbitrary",)),
    )(dyssd, y, z, xbc, pre, dtf, hin, dcf, conv_w, brow, alog, dskip_l, ssd_norm, sel_t, sel)


def _attn_fwd(q, k, v, ck, blk):
    p, da = q.shape
    npair, nkb = ck.shape[0], ck.shape[1]
    scale = 1.0 / math.sqrt(HEAD_DIM)

    def body(q_ref, k_ref, v_ref, ck_ref, o_ref, lse_ref):
        i = pl.program_id(1)
        lane = lax.broadcasted_iota(jnp.int32, (1, LANES), 1)
        sels = [lane < HEAD_DIM, lane >= HEAD_DIM]
        ones = [jnp.where(lane == HEAD_DIM, 1.0, 0.0).astype(BF16), jnp.where(lane == 0, 1.0, 0.0).astype(BF16)]
        qb = q_ref[...] * scale

        def step(kb, carry, masked, nk=1):
            r0 = pl.multiple_of(kb * blk, blk)
            ks = k_ref[pl.ds(r0, nk * blk), :]
            vs = v_ref[pl.ds(r0, nk * blk), :]
            kk = jnp.concatenate([jnp.where(sel, ks, jnp.zeros_like(ks)) for sel in sels], axis=0)
            s_both = _dot(qb, kk, NT)
            out = []
            for j in range(2):
                m, acc = carry[2 * j], carry[2 * j + 1]
                ckr = jnp.concatenate([ck_ref[0, kb + t, j:j + 1, :] for t in range(nk)], axis=1)
                s = s_both[:, j * nk * blk:(j + 1) * nk * blk] - ckr
                if masked:
                    col = lax.broadcasted_iota(jnp.int32, (blk, nk * blk), 1) - (nk - 1) * blk
                    s = jnp.where(col <= lax.broadcasted_iota(jnp.int32, (blk, nk * blk), 0), s, NEG)
                mn = jnp.maximum(m, jnp.max(s, axis=-1, keepdims=True))
                pr = jnp.exp(s - mn).astype(BF16)
                acc = jnp.exp(m - mn) * acc + _dot(pr, jnp.where(sels[j], vs, ones[j]))
                out += [mn, acc]
            return tuple(out)

        init = (jnp.full((blk, 1), NEG, F32), jnp.zeros((blk, LANES), F32)) * 2

        def finish(carry):
            m0, a0, m1, a1 = carry
            l0 = a0[:, HEAD_DIM:HEAD_DIM + 1]
            l1 = a1[:, 0:1]
            o_ref[...] = jnp.where(sels[0], a0 / l0, a1 / l1).astype(BF16)
            lse_ref[...] = jnp.where(sels[0], m0 + jnp.log(l0), m1 + jnp.log(l1))

        @pl.when(i == 0)
        def _():
            finish(step(0, init, True))

        @pl.when(i > 0)
        def _():
            below = i - 1
            n4 = below // 4
            n2 = (below - 4 * n4) // 2
            carry = lax.fori_loop(0, n4, lambda t, c: step(4 * t, c, False, 4), init)
            carry = lax.fori_loop(0, n2, lambda t, c: step(4 * n4 + 2 * t, c, False, 2), carry)
            carry = lax.fori_loop(4 * n4 + 2 * n2, below, lambda kb, c: step(kb, c, False), carry)
            finish(step(below, carry, True, 2))

    return pl.pallas_call(
        body, name="attn_fwd", grid=(npair, p // blk),
        in_specs=[pl.BlockSpec((blk, LANES), lambda h, i: (i, h)),
                  pl.BlockSpec((p, LANES), lambda h, i: (0, h)), pl.BlockSpec((p, LANES), lambda h, i: (0, h)),
                  pl.BlockSpec((1, nkb, 8, blk), lambda h, i: (h, 0, 0, 0))],
        out_specs=[pl.BlockSpec((blk, LANES), lambda h, i: (i, h)), pl.BlockSpec((blk, LANES), lambda h, i: (i, h))],
        out_shape=[jax.ShapeDtypeStruct((p, da), BF16), jax.ShapeDtypeStruct((p, da), F32)],
        compiler_params=_cparams(("parallel", "arbitrary")),
    )(q, k, v, ck)


def _attn_bwd(q, k, v, o, do, lse_rep, ck, blk):
    p, da = q.shape
    npair, nkb = ck.shape[0], ck.shape[1]
    nq = p // blk
    scale = 1.0 / math.sqrt(HEAD_DIM)

    def body(k_ref, v_ref, q_ref, do_ref, o_ref, lse_ref, ck_ref, dk_ref, dv_ref, dq_ref, dcs_ref, rsum_ref, dq_acc):
        jb = pl.program_id(1)

        @pl.when(jb == 0)
        def _():
            dq_acc[...] = jnp.zeros_like(dq_acc)

        ks = k_ref[...]
        vs = v_ref[...]
        lane = lax.broadcasted_iota(jnp.int32, (1, LANES), 1)
        sels = [lane < HEAD_DIM, lane >= HEAD_DIM]
        ones = [jnp.where(lane == HEAD_DIM, 1.0, 0.0).astype(BF16), jnp.where(lane == 0, 1.0, 0.0).astype(BF16)]
        kss = ks * scale
        kmo = [jnp.where(sels[j], kss, ones[j]) for j in range(2)]

        def step(ib, carry, masked, nb=1):
            rows = nb * blk
            r0 = pl.multiple_of(ib * blk, blk)
            qb = q_ref[pl.ds(r0, rows), :] * scale
            dob = do_ref[pl.ds(r0, rows), :]
            prod = dob.astype(F32) * o_ref[pl.ds(r0, rows), :].astype(F32)
            out = []
            for j in range(2):
                dk, dv = carry[2 * j], carry[2 * j + 1]
                qm = jnp.where(sels[j], qb, jnp.zeros_like(qb))
                dom = jnp.where(sels[j], dob, jnp.zeros_like(dob))
                lse = lse_ref[pl.ds(r0, rows), HEAD_DIM * j:HEAD_DIM * j + 1]
                dlt = jnp.sum(jnp.where(sels[j], prod, 0.0), axis=-1, keepdims=True)
                s = _dot(qm, ks, NT) - ck_ref[0, 0, j:j + 1, :] - lse
                pm = jnp.exp(jnp.minimum(s, 0.0))
                if masked:
                    causal = (lax.broadcasted_iota(jnp.int32, (rows, blk), 1)
                              <= lax.broadcasted_iota(jnp.int32, (rows, blk), 0))
                    pm = jnp.where(causal, pm, 0.0)
                ds_b = (pm * (_dot(dom, vs, NT) - dlt)).astype(BF16)
                dv = dv + _dot(pm.astype(BF16), dom, TN)
                dk = dk + _dot(ds_b, jnp.where(sels[j], qb, ones[j]), TN)
                dq_acc[pl.ds(r0, rows), LANES * j:LANES * (j + 1)] += _dot(ds_b, kmo[j])
                out += [dk, dv]
            return tuple(out)

        pair8 = lambda c0, c1: jnp.where(lane == 0, c0, jnp.where(lane == 1, c1, 0.0)).T[0:8]
        zero = jnp.zeros((blk, LANES), F32)
        init = (zero, zero, zero, zero)

        def finish(carry):
            dk0, dv0, dk1, dv1 = carry
            dk_ref[...] = jnp.where(sels[0], dk0, dk1).astype(BF16)
            dv_ref[...] = (dv0 + dv1).astype(BF16)
            dcs_ref[0] = pair8(dk0[:, HEAD_DIM:HEAD_DIM + 1], dk1[:, 0:1])

        @pl.when(jb == nq - 1)
        def _():
            finish(step(jb, init, True))

        @pl.when(jb < nq - 1)
        def _():
            carry = step(jb, init, True, 2)
            n4 = (nq - 2 - jb) // 4
            n2 = (nq - 2 - jb - 4 * n4) // 2
            carry = lax.fori_loop(0, n4, lambda t, c: step(jb + 2 + 4 * t, c, False, 4), carry)
            carry = lax.fori_loop(0, n2, lambda t, c: step(jb + 2 + 4 * n4 + 2 * t, c, False, 2), carry)
            finish(lax.fori_loop(jb + 2 + 4 * n4 + 2 * n2, nq, lambda ib, c: step(ib, c, False), carry))

        @pl.when(jb == nkb - 1)
        def _():
            a0 = dq_acc[:, :LANES]
            a1 = dq_acc[:, LANES:]
            dq_ref[...] = jnp.where(sels[0], a0, a1).astype(BF16)
            rsum_ref[0] = pair8(a0[:, HEAD_DIM:HEAD_DIM + 1], a1[:, 0:1])

    colblk = pl.BlockSpec((blk, LANES), lambda h, j: (j, h))
    colfull = pl.BlockSpec((p, LANES), lambda h, j: (0, h))
    ckspec = pl.BlockSpec((1, 1, 8, blk), lambda h, j: (h, j, 0, 0))
    return pl.pallas_call(
        body, name="attn_bwd", grid=(npair, nkb),
        in_specs=[colblk, colblk, colfull, colfull, colfull, colfull, ckspec],
        out_specs=[colblk, colblk, colfull, pl.BlockSpec((1, 8, blk), lambda h, j: (h, 0, j)),
                   pl.BlockSpec((1, 8, p), lambda h, j: (h, 0, 0))],
        out_shape=[jax.ShapeDtypeStruct((p, da), BF16), jax.ShapeDtypeStruct((p, da), BF16),
                   jax.ShapeDtypeStruct((p, da), BF16), jax.ShapeDtypeStruct((npair, 8, p), F32),
                   jax.ShapeDtypeStruct((npair, 8, p), F32)],
        scratch_shapes=[pltpu.VMEM((p, 2 * LANES), F32)],
        compiler_params=_cparams(("parallel", "arbitrary")),
    )(k, v, q, do, o, lse_rep, ck)


def _tail_fwd(yssd, o, zatt, graw, head, x2, tgt2, wps, wpa, wout, gate_bias, norm_post, tm):
    p, ds = yssd.shape
    da = o.shape[1]
    d = x2.shape[1]
    nsub = tm // CHUNK

    def body(yssd_ref, o_ref, zatt_ref, g_ref, head_ref, *rest):
        x_refs, t_refs = rest[:nsub], rest[nsub:2 * nsub]
        (wps_ref, wpa_ref, wout_ref, gb_ref, np_ref,
         yatt_ref, mrg_ref, a_ref, b_ref, dzo_ref, dout_ref, red_ref) = rest[2 * nsub:]
        i = pl.program_id(0)

        @pl.when(i == 0)
        def _():
            red_ref[...] = jnp.zeros_like(red_ref)

        first = jnp.where(i == 0, head_ref[...], x_refs[0][...])
        h = jnp.concatenate([first] + [r[...] for r in x_refs[1:]], axis=0)
        tgt = jnp.concatenate([r[...] for r in t_refs], axis=0)
        rows = lax.broadcasted_iota(jnp.int32, (tm, 1), 0)
        valid = jnp.where((i > 0) | (rows >= CHUNK), 1.0, 0.0)
        ob = o_ref[...].astype(F32)
        za = zatt_ref[...].astype(F32)
        yatt_b = (ob * za * _sigmoid(za)).astype(BF16)
        yatt_ref[...] = yatt_b
        a = _dot(yssd_ref[...], wps_ref[...])
        b = _dot(yatt_b, wpa_ref[...])
        a_ref[...] = a.astype(BF16)
        b_ref[...] = b.astype(BF16)
        gr = g_ref[...].astype(F32) + gb_ref[...]
        mrg_b = (_sigmoid(gr[:, :d]) * a + _sigmoid(gr[:, d:]) * b).astype(BF16)
        mrg_ref[...] = mrg_b
        zo = _dot(mrg_b, wout_ref[...])
        rstd = lax.rsqrt(jnp.mean(zo * zo, axis=-1, keepdims=True) + EPS)
        zh = zo * rstd
        npw = np_ref[...]
        err = (h + zh * npw - tgt) * valid
        dout = err * (1.0 / d)
        dout_ref[...] = dout
        dzh = dout * npw
        dzo_ref[...] = (rstd * (dzh - zh * jnp.mean(dzh * zh, axis=-1, keepdims=True))).astype(BF16)
        red_ref[0:1, :] += jnp.sum(dout * zh, axis=0, keepdims=True)
        red_ref[1:2, 0:1] += jnp.sum(jnp.sum(err * err, axis=1, keepdims=True), axis=0, keepdims=True) * (0.5 / d)

    row = lambda w: pl.BlockSpec((tm, w), lambda i: (i, 0))
    once = lambda shape: pl.BlockSpec(shape, lambda i: (0,) * len(shape), pipeline_mode=pl.Buffered(1))
    subs = _x_row_specs(tm, d)
    sd = jax.ShapeDtypeStruct
    return pl.pallas_call(
        body, name="tail_fwd", grid=(p // tm,),
        in_specs=[row(ds), row(da), row(da), row(2 * d), _full((CHUNK, d))] + subs + subs
                 + [once((ds, d)), once((da, d)), once((d, d)), _full((1, 2 * d)), _full((1, d))],
        out_specs=[row(da), row(d), row(d), row(d), row(d), row(d), _full((8, d))],
        out_shape=[sd((p, da), BF16), sd((p, d), BF16), sd((p, d), BF16), sd((p, d), BF16), sd((p, d), BF16),
                   sd((p, d), F32), sd((8, d), F32)],
        compiler_params=_cparams(("arbitrary",)),
    )(yssd, o, zatt, graw, head, *([x2] * nsub), *([tgt2] * nsub), wps, wpa, wout, gate_bias, norm_post)


def _tail_bwd(dzo, a_b, b_b, graw, o, zatt, wps, wpa, wout, gate_bias, tm):
    p, d = dzo.shape
    ds, da = wps.shape[0], wpa.shape[0]

    def body(dzo_ref, a_ref, b_ref, g_ref, o_ref, zatt_ref, wps_ref, wpa_ref, wout_ref, gb_ref,
             da_ref, db_ref, dg_ref, dyssd_ref, do_ref, dzatt_ref, red_ref):
        i = pl.program_id(0)

        @pl.when(i == 0)
        def _():
            red_ref[...] = jnp.zeros_like(red_ref)

        gr = g_ref[...].astype(F32) + gb_ref[...]
        gs = _sigmoid(gr[:, :d])
        ga = _sigmoid(gr[:, d:])
        dm = _dot(dzo_ref[...], wout_ref[...], NT)
        da_b = (gs * dm).astype(BF16)
        db_b = (ga * dm).astype(BF16)
        da_ref[...] = da_b
        db_ref[...] = db_b
        dgs = dm * a_ref[...].astype(F32) * gs * (1.0 - gs)
        dga = dm * b_ref[...].astype(F32) * ga * (1.0 - ga)
        dg_ref[:, :d] = dgs.astype(BF16)
        dg_ref[:, d:] = dga.astype(BF16)
        red_ref[0:1, :d] += jnp.sum(dgs, axis=0, keepdims=True)
        red_ref[0:1, d:] += jnp.sum(dga, axis=0, keepdims=True)
        dyssd_ref[...] = _dot(da_b, wps_ref[...], NT).astype(BF16)
        dya = _dot(db_b, wpa_ref[...], NT)
        ob = o_ref[...].astype(F32)
        za = zatt_ref[...].astype(F32)
        sza = _sigmoid(za)
        do_ref[...] = (dya * za * sza).astype(BF16)
        dzatt_ref[...] = (dya * ob * sza * (1.0 + za * (1.0 - sza))).astype(BF16)

    row = lambda w: pl.BlockSpec((tm, w), lambda i: (i, 0))
    once = lambda shape: pl.BlockSpec(shape, lambda i: (0,) * len(shape), pipeline_mode=pl.Buffered(1))
    sd = jax.ShapeDtypeStruct
    return pl.pallas_call(
        body, name="tail_bwd", grid=(p // tm,),
        in_specs=[row(d), row(d), row(d), row(2 * d), row(da), row(da),
                  once((ds, d)), once((da, d)), once((d, d)), _full((1, 2 * d))],
        out_specs=[row(d), row(d), row(2 * d), row(ds), row(da), row(da), _full((8, 2 * d))],
        out_shape=[sd((p, d), BF16), sd((p, d), BF16), sd((p, 2 * d), BF16), sd((p, ds), BF16), sd((p, da), BF16),
                   sd((p, da), BF16), sd((8, 2 * d), F32)],
        compiler_params=_cparams(("arbitrary",)),
    )(dzo, a_b, b_b, graw, o, zatt, wps, wpa, wout, gate_bias)


def _adamw_math(w, g, m, v):
    m2 = ADAM_B1 * m + (1.0 - ADAM_B1) * g
    v2 = ADAM_B2 * v + (1.0 - ADAM_B2) * (g * g)
    m_hat = m2 / (1.0 - ADAM_B1 ** ADAM_STEP)
    v_hat = v2 / (1.0 - ADAM_B2 ** ADAM_STEP)
    delta = -ADAM_LR * (m_hat / (jnp.sqrt(v_hat) + ADAM_EPS) + ADAM_WD * w)
    return delta, m2, v2


def _adamw_small(params, red, name):
    names = list(params)
    n = len(names)
    extra = [params[k][3] for k in names if not isinstance(params[k][3], tuple)]

    def body(*refs):
        w_refs, m_refs, v_refs = refs[:n], refs[n:2 * n], refs[2 * n:3 * n]
        red_ref = refs[3 * n]
        g_refs = iter(refs[3 * n + 1:3 * n + 1 + len(extra)])
        outs = refs[3 * n + 1 + len(extra):]
        for i, k in enumerate(names):
            where = params[k][3]
            rows, cols = w_refs[i].shape
            if isinstance(where, tuple):
                g = red_ref[where[0]:where[0] + rows, where[1]:where[1] + cols]
            else:
                g = next(g_refs)[...]
            delta, m2, v2 = _adamw_math(w_refs[i][...], g, m_refs[i][...], v_refs[i][...])
            for o, val in zip(outs[4 * i:4 * i + 4], (g, delta, m2, v2)):
                o[...] = val

    vm = pl.BlockSpec(memory_space=pltpu.VMEM)
    ws, ms, vs = ([params[k][j] for k in names] for j in range(3))
    out = pl.pallas_call(
        body, name=name,
        out_shape=[jax.ShapeDtypeStruct(w.shape, F32) for w in ws for _ in range(4)],
        in_specs=[vm] * (3 * n + 1 + len(extra)), out_specs=[vm] * (4 * n),
    )(*ws, *ms, *vs, red, *extra)
    return {k: tuple(out[4 * i:4 * i + 4]) for i, k in enumerate(names)}


def _adamw(w, g, m, v, name, parts=False, part_row0=0):
    r, cdim = w.shape
    tr, tc, by_rows = _tiles_2d(r, cdim)
    pick = (lambda i: (i, 0)) if by_rows else (lambda i: (0, i))
    assert part_row0 % tr == 0
    gpick = (lambda i: (i + part_row0 // tr, 0)) if by_rows else (lambda i: (part_row0 // tr, i))

    def body(w_ref, g_ref, m_ref, v_ref, go_ref, d_ref, mo_ref, vo_ref):
        if parts:
            g = g_ref[0].astype(F32)
            for s in range(1, g_ref.shape[0]):
                g = g + g_ref[s].astype(F32)
        else:
            g = g_ref[...]
        delta, m2, v2 = _adamw_math(w_ref[...], g, m_ref[...], v_ref[...])
        go_ref[...] = g
        d_ref[...] = delta
        mo_ref[...] = m2
        vo_ref[...] = v2

    blk = pl.BlockSpec((tr, tc), pick)
    gspec = pl.BlockSpec((g.shape[0], tr, tc), lambda i: (0,) + gpick(i)) if parts else blk
    return pl.pallas_call(
        body, name=name, grid=((r // tr) * (cdim // tc),),
        in_specs=[blk, gspec, blk, blk], out_specs=[blk] * 4,
        out_shape=[jax.ShapeDtypeStruct((r, cdim), F32)] * 4,
        compiler_params=_cparams(("parallel",)),
    )(w, g, m, v)


def _pad_cols(a, width):
    return jnp.pad(a, ((0, 0), (0, width - a.shape[1])))


def _pack_small_shard(conv_w_sh, meta_sh, width):
    return jnp.concatenate([_pad_cols(conv_w_sh, width), jnp.zeros((4, width), F32), _pad_cols(meta_sh, width)], axis=0)


def _pack_small_rep(norm_pre, norm_post, gate_bias, ssd_norm, conv_b, misc, width):
    rows = [norm_pre, norm_post, gate_bias, ssd_norm, conv_b, misc]
    return jnp.concatenate([_pad_cols(r, width) for r in rows] + [jnp.zeros((2, width), F32)], axis=0)


def kernel(x, meta_tokens, norm_pre, w_in, conv_w, conv_b, dt_bias, a_log, d_skip, ssd_norm, fgate_bias, gate_bias, w_proj_ssd, w_proj_att, w_out, norm_post, loss_target, m_meta_tokens, m_norm_pre, m_w_in, m_conv_w, m_conv_b, m_dt_bias, m_a_log, m_d_skip, m_ssd_norm, m_fgate_bias, m_gate_bias, m_w_proj_ssd, m_w_proj_att, m_w_out, m_norm_post, v_meta_tokens, v_norm_pre, v_w_in, v_conv_w, v_conv_b, v_dt_bias, v_a_log, v_d_skip, v_ssd_norm, v_fgate_bias, v_gate_bias, v_w_proj_ssd, v_w_proj_att, v_w_out, v_norm_post):
    seq, d = x.shape[1], x.shape[2]
    p = seq + CHUNK
    hs, ha = dt_bias.shape[1], fgate_bias.shape[1]
    ds, cd = ssd_norm.shape[1], conv_b.shape[1]
    da = ha * HEAD_DIM
    nc8 = w_in.shape[2]
    cws = cd // N_DEV
    msh = d // N_DEV
    r1, r2, r3 = ds // N_DEV, da // N_DEV, d // N_DEV
    me = _dev_index(*_my_pos())
    x2, tgt2 = x[0], loss_target[0]

    win_sh = jnp.transpose(w_in[0]).astype(BF16)
    rows_sh = jnp.concatenate([w_proj_ssd[0], w_proj_att[0], w_out[0]], axis=0).astype(BF16)
    small_sh = _pack_small_shard(conv_w[0], meta_tokens, cws)
    win_all, small_all = _all_gather([win_sh, small_sh], "gather_weights")
    rows_sh, win_all = lax.optimization_barrier((rows_sh, win_all))
    rows_sems, rows_thru, rows_land, rows_token = _bcast_start(rows_sh, "gather_rows_start")
    cuts = [0, ds, ds + cd, ds + cd + hs, ds + cd + hs + da, ds + cd + hs + 2 * da, ds + cd + hs + 3 * da,
            ds + cd + hs + 4 * da, ds + cd + hs + 4 * da + ha, ds + cd + hs + 4 * da + ha + 2 * d]

    def piece_rows(r0, r1):
        parts = [win_all[s, max(r0, s * nc8) - s * nc8:min(r1, (s + 1) * nc8) - s * nc8]
                 for s in range(N_DEV) if max(r0, s * nc8) < min(r1, (s + 1) * nc8)]
        return parts[0] if len(parts) == 1 else jnp.concatenate(parts, axis=0)

    w_z, w_xbc, w_dt, w_zatt, w_q, w_k, w_v, w_f, w_g = [piece_rows(cuts[i], cuts[i + 1]) for i in range(9)]
    w_dtf = jnp.concatenate([w_dt, w_f, jnp.zeros((LANES - hs - ha, d), BF16)], axis=0)
    conv_w_full = jnp.transpose(small_all[:, 0:CONV_K, :], (1, 0, 2)).reshape(CONV_K, cd)
    meta_full = jnp.transpose(small_all[:, 8:8 + N_META, :msh], (1, 0, 2)).reshape(N_META, d)
    head = jnp.concatenate([jnp.zeros((PADN, d), F32), meta_full + rows_token[0:1, 0:1]], axis=0)

    tm = _att_block(p)
    u = _prenorm_fwd(head, x2, norm_pre, tm)
    seg_w = [w_z, w_xbc, w_zatt, w_q, w_k, w_v, w_g]
    zs, xbc, zatt, q, k, v, graw = [
        _mm(u, w, "nt", BF16, _tile(p, (1408, tm)), _tile(w.shape[0], (1024, 512, 256, 128)), "inproj_%d" % i)
        for i, w in enumerate(seg_w)]
    dtf = _mm(u, w_dtf, "nt", F32, _tile(p, (1408, tm)), LANES, "inproj_dtf")

    brow = jnp.concatenate([dt_bias, fgate_bias, jnp.zeros((1, LANES - hs - ha), F32)], axis=1)
    alog_row = _pad_cols(a_log, LANES)
    dskip_l = jnp.repeat(d_skip, HEAD_DIM, axis=1)
    sel_t = (lax.broadcasted_iota(jnp.int32, (LANES, ds), 1) // HEAD_DIM
             == lax.broadcasted_iota(jnp.int32, (LANES, ds), 0)).astype(BF16)
    sel = sel_t.T
    y, yssd, hin, cf, pre = _ssd_fwd(xbc, zs, dtf, conv_w_full, conv_b, brow, alog_row, dskip_l, ssd_norm, sel_t, hs, ha)

    blk = _att_block(p)
    nkb, npair = p // blk, ha // 2
    cum = jnp.where(lax.broadcasted_iota(jnp.int32, (p, 1), 0) < PADN, -NEG, cf[:, hs:hs + ha])
    ck = jnp.transpose(cum.T.reshape(npair, 2, nkb, blk), (0, 2, 1, 3))
    ck = jnp.pad(ck, ((0, 0), (0, 0), (0, 6), (0, 0)))
    o, lse_rep = _attn_fwd(q, k, v, ck, blk)

    rows_all = _bcast_wait(rows_sems, rows_thru, rows_land, lse_rep, "gather_rows_wait")
    wps = rows_all[:, :r1].reshape(ds, d)
    wpa = rows_all[:, r1:r1 + r2].reshape(da, d)
    wout = rows_all[:, r1 + r2:].reshape(d, d)

    yatt, mrg, a_b, b_b, dzo, dout, red_fwd = _tail_fwd(
        yssd, o, zatt, graw, head, x2, tgt2, wps, wpa, wout, gate_bias, norm_post, tm)
    da_, db_, dgraw, dyssd, d_o, dzatt, red_bwd = _tail_bwd(dzo, a_b, b_b, graw, o, zatt, wps, wpa, wout, gate_bias, tm)

    tw = _tile(d, (512, 256, 128))
    g_wout = _mm(mrg, dzo, "tn", BF16, tw, d, "wgrad_out")
    g_wps = _mm(yssd, da_, "tn", BF16, _tile(ds, (512, 256, 128)), d, "wgrad_ps")
    g_wpa = _mm(yatt, db_, "tn", BF16, _tile(da, (512, 256, 128)), d, "wgrad_pa")

    core = lax.axis_index("c").astype(jnp.int32).reshape(1)
    chip = me // 2
    grows_parts = jnp.concatenate([g_wps.reshape(N_DEV, r1, d), g_wpa.reshape(N_DEV, r2, d),
                                   g_wout.reshape(N_DEV, r3, d)], axis=1)
    (sib_rows,) = _exchange_sibling([grows_parts], "scatter_rows_sibling")
    chip_rows = _pair_add(grows_parts, sib_rows, core, "pair_add_rows")
    r_sems, r_thru, r_lands, r_token = _exchange_chips_start([chip_rows], "scatter_rows_start")

    dk, dv, dq, dcs, rsum = _attn_bwd(q, k, v, o, d_o, lse_rep, ck + r_token[0:1, 0:1], blk)
    dcum = (rsum - dcs)[:, 0:2, :].reshape(ha, p).T
    dcf = jnp.pad(dcum, ((0, 0), (hs, LANES - hs - ha)))
    dxbc, dzs, ddtf, gcw, gcb, gnrm, gsm = _ssd_bwd(
        dyssd, y, zs, xbc, pre, dtf, hin, dcf, conv_w_full, brow, alog_row, dskip_l, ssd_norm, sel_t, sel, hs, ha)
    ddtf_b = ddtf.astype(BF16)

    dsegs = [dzs, dxbc, dzatt, dq, dk, dv, dgraw, ddtf_b]
    gsegs = [_mm(dsg, u, "tn", BF16, _tile(dsg.shape[1], (512, 256, 128)), d, "wgrad_in_%d" % i)
             for i, dsg in enumerate(dsegs)]
    g_z, g_xbc, g_zatt, g_q, g_k, g_v, g_g, g_dtf = gsegs
    gw_full = jnp.concatenate([g_z, g_xbc, g_dtf[:hs], g_zatt, g_q, g_k, g_v, g_dtf[hs:hs + ha], g_g], axis=0)
    gwin_parts = gw_full.reshape(N_DEV, nc8, d)

    (sib_win,) = _exchange_sibling([gwin_parts], "scatter_grads_sibling")
    chip_win = _pair_add(gwin_parts, sib_win, core, "pair_add_w_in")
    sems, thru, lands, token = _exchange_chips_start([chip_win], "scatter_grads_start")
    dsegs_after = dsegs[:-1] + [ddtf_b + token[0:1, 0:1].astype(BF16)]
    gx, ghead, gnp = _dgrad_prenorm(dsegs_after, seg_w + [w_dtf], head, x2, norm_pre, dout, tm, "dgrad_in")
    own_slot = lambda got, sent: lax.dynamic_update_slice_in_dim(
        got, lax.dynamic_slice_in_dim(sent, chip, 1, axis=0), chip, axis=0)
    (sent,), (got,) = _exchange_chips_wait(sems, thru, lands, gnp, "scatter_grads_wait")
    recv_win = own_slot(got, sent)
    (r_sent,), (r_got,) = _exchange_chips_wait(r_sems, r_thru, r_lands, gnp, "scatter_rows_wait")
    recv_rows = own_slot(r_got, r_sent)
    gmisc = jnp.concatenate([gsm[0:1], gsm[1:2], gsm[2:3], _pad_cols(red_fwd[1:2, 0:1], LANES)], axis=1)
    small_g = jnp.concatenate([
        _pack_small_rep(gnp[0:1], red_fwd[0:1], red_bwd[0:1], gnrm[0:1], gcb[0:1], gmisc, cd),
        _pad_cols(gcw[0:CONV_K], cd), jnp.zeros((4, cd), F32), _pad_cols(ghead[PADN:], cd)], axis=0)
    sg_sems, sg_thru, sg_land, sg_token = _bcast_start(small_g, "reduce_small_start")

    upd_in = _adamw(jnp.transpose(w_in[0]) + sg_token[0:1, 0:1], recv_win, jnp.transpose(m_w_in[0]),
                    jnp.transpose(v_w_in[0]), "adamw_w_in", parts=True)
    upd_ps = _adamw(w_proj_ssd[0] + sg_token[0:1, 0:1], recv_rows, m_w_proj_ssd[0], v_w_proj_ssd[0],
                    "adamw_w_proj_ssd", parts=True, part_row0=0)
    upd_pa = _adamw(w_proj_att[0], recv_rows, m_w_proj_att[0], v_w_proj_att[0], "adamw_w_proj_att", parts=True,
                    part_row0=r1)
    upd_out = _adamw(w_out[0], recv_rows, m_w_out[0], v_w_out[0], "adamw_w_out", parts=True, part_row0=r1 + r2)
    all_done = upd_in[1][0:8, 0:LANES] + upd_ps[1][0:8, 0:LANES] + upd_pa[1][0:8, 0:LANES] + upd_out[1][0:8, 0:LANES]
    red = _sum_slots(_bcast_wait(sg_sems, sg_thru, sg_land, all_done, "reduce_small_wait"), "reduce_small_sum")
    loss = red[5, 3 * LANES]
    g_conv_w = lax.dynamic_slice_in_dim(red[8:8 + CONV_K], me * cws, cws, axis=1)
    g_meta = lax.dynamic_slice_in_dim(red[16:16 + N_META, :d], me * msh, msh, axis=1)
    small = {
        "meta_tokens": (meta_tokens, m_meta_tokens, v_meta_tokens, g_meta),
        "norm_pre": (norm_pre, m_norm_pre, v_norm_pre, (0, 0)),
        "conv_w": (conv_w[0], m_conv_w[0], v_conv_w[0], g_conv_w),
        "conv_b": (conv_b, m_conv_b, v_conv_b, (4, 0)),
        "dt_bias": (dt_bias, m_dt_bias, v_dt_bias, (5, 0)),
        "a_log": (a_log, m_a_log, v_a_log, (5, LANES)),
        "d_skip": (d_skip, m_d_skip, v_d_skip, (5, 2 * LANES)),
        "ssd_norm": (ssd_norm, m_ssd_norm, v_ssd_norm, (3, 0)),
        "fgate_bias": (fgate_bias, m_fgate_bias, v_fgate_bias, (5, hs)),
        "gate_bias": (gate_bias, m_gate_bias, v_gate_bias, (2, 0)),
        "norm_post": (norm_post, m_norm_post, v_norm_post, (1, 0)),
    }
    upd_small = _adamw_small(small, red, "adamw_small")

    def leaves(i):
        sm = {k: v[i] for k, v in upd_small.items()}
        return [sm["meta_tokens"], sm["norm_pre"], jnp.transpose(upd_in[i])[None], sm["conv_w"][None], sm["conv_b"],
                sm["dt_bias"], sm["a_log"], sm["d_skip"], sm["ssd_norm"], sm["fgate_bias"], sm["gate_bias"],
                upd_ps[i][None], upd_pa[i][None], upd_out[i][None], sm["norm_post"]]

    return tuple([loss, gx[None]] + leaves(0) + leaves(1) + leaves(2) + leaves(3))
```

```python
import functools
import math

import jax
import jax.numpy as jnp
from jax import lax
from jax.experimental import pallas as pl
from jax.experimental.pallas import tpu as pltpu

F32 = jnp.float32
BF16 = jnp.bfloat16

N_DEV = 8
N_META = 16
CHUNK = 128
PADN = CHUNK - N_META
HEAD_DIM = 64
SSD_GROUPS = 4
CONV_K = 4
EPS = 1e-6
NEG = -1e30
LANES = 128
HALO = 16

ADAM_LR = 0.001
ADAM_B1 = 0.9
ADAM_B2 = 0.999
ADAM_EPS = 1e-08
ADAM_WD = 0.01
ADAM_STEP = 10

VMEM_LIMIT = 56 * 1024 * 1024

NN = (((1,), (0,)), ((), ()))
NT = (((1,), (1,)), ((), ()))
TN = (((0,), (0,)), ((), ()))
MESH = pl.DeviceIdType.MESH


def _dot(a, b, dims=NN):
    return lax.dot_general(a, b, dims, preferred_element_type=F32)


def _split2(x):
    hi = x.astype(BF16)
    lo = (x - hi.astype(F32)).astype(BF16)
    return hi, lo


def _dot_sel(x, sel):
    hi, lo = _split2(x)
    return _dot(hi, sel) + _dot(lo, sel)


def _dot_tri(tri, x):
    h1 = x.astype(BF16)
    r1 = x - h1.astype(F32)
    h2 = r1.astype(BF16)
    h3 = (r1 - h2.astype(F32)).astype(BF16)
    return _dot(tri, h1) + _dot(tri, h2) + _dot(tri, h3)


def _sigmoid(x):
    return 0.5 * jnp.tanh(0.5 * x) + 0.5


def _softplus(x):
    return jnp.maximum(x, 0.0) + jnp.log(1.0 + jnp.exp(-jnp.abs(x)))


def _cparams(sem=None, vmem=VMEM_LIMIT):
    kw = {"vmem_limit_bytes": vmem}
    if sem is not None:
        kw["dimension_semantics"] = sem
    return pltpu.CompilerParams(**kw)


def _full(shape):
    nd = len(shape)
    return pl.BlockSpec(shape, lambda *_: (0,) * nd)


def _att_block(p):
    return 384 if p % 384 == 0 else CHUNK


def _my_pos():
    return lax.axis_index("x"), lax.axis_index("y"), lax.axis_index("c")


def _dev_index(x, y, c):
    return 4 * x + 2 * y + c


FLIPS = [(fx, fy, fc) for fx in (0, 1) for fy in (0, 1) for fc in (0, 1)][1:]


def _flip(pos, f):
    return tuple((1 - p) if fi else p for p, fi in zip(pos, f))


def _all_gather(bufs, name):
    nb = len(bufs)

    def body(*refs):
        ins, outs = refs[:nb], refs[nb:2 * nb]
        send_sems, recv_sems, local_sems = refs[2 * nb:]
        x, y, c = _my_pos()
        me = _dev_index(x, y, c)
        sibling = (x, y, 1 - c)
        near = [(1 - x, y), (x, 1 - y)]
        far = (1 - x, 1 - y)
        relay_from = (c * (1 - x) + (1 - c) * x, c * y + (1 - c) * (1 - y))
        relay_to = (c * x + (1 - c) * (1 - x), c * (1 - y) + (1 - c) * y)

        def copy(b, k, block_idx, to, src=None):
            dst = outs[b].at[block_idx]
            return pltpu.make_async_remote_copy(
                src_ref=dst if src is None else src, dst_ref=dst,
                send_sem=send_sems.at[b, k], recv_sem=recv_sems.at[b, k],
                device_id=to, device_id_type=MESH)

        started = []
        for b in range(nb):
            mine = pltpu.make_async_copy(ins[b], outs[b].at[me], local_sems.at[b])
            mine.start()
            started.append(mine)
        sent = []
        for b in range(nb):
            sent.append(copy(b, 0, me, sibling, src=ins[b]))
            for j, chip in enumerate(near):
                sent.append(copy(b, 1 + j, me, (chip[0], chip[1], c), src=ins[b]))
        for cp in sent:
            cp.start()
        for j, chip in enumerate(near):
            blk = _dev_index(chip[0], chip[1], c)
            for b in range(nb):
                copy(b, 1 + j, blk, (x, y, c)).wait_recv()
                sent.append(copy(b, 4 + j, blk, sibling))
                sent[-1].start()
        for b in range(nb):
            sent.append(copy(b, 3, _dev_index(relay_from[0], relay_from[1], c), (relay_to[0], relay_to[1], c)))
            sent[-1].start()
        blk = _dev_index(far[0], far[1], c)
        for b in range(nb):
            copy(b, 3, blk, (x, y, c)).wait_recv()
            sent.append(copy(b, 6, blk, sibling))
            sent[-1].start()
        for b in range(nb):
            copy(b, 0, _dev_index(x, y, 1 - c), (x, y, c)).wait_recv()
        for j, chip in enumerate(near + [far]):
            blk = _dev_index(chip[0], chip[1], 1 - c)
            for b in range(nb):
                copy(b, 4 + j, blk, (x, y, c)).wait_recv()
        for cp in sent:
            cp.wait_send()
        for mine in started:
            mine.wait()

    any_spec = pl.BlockSpec(memory_space=pl.ANY)
    return pl.pallas_call(
        body, name=name,
        out_shape=[jax.ShapeDtypeStruct((N_DEV,) + b.shape, b.dtype) for b in bufs],
        in_specs=[any_spec] * nb, out_specs=[any_spec] * nb,
        scratch_shapes=[pltpu.SemaphoreType.DMA((nb, 7)), pltpu.SemaphoreType.DMA((nb, 7)),
                        pltpu.SemaphoreType.DMA((nb,))],
    )(*bufs)


N_CHIP = 4
CHIP_FLIPS = [(1, 0), (0, 1), (1, 1)]


def _exchange_sibling(bufs, name):
    nb = len(bufs)

    def body(*refs):
        ins, outs = refs[:nb], refs[nb:2 * nb]
        send_sems, recv_sems = refs[2 * nb:]
        x, y, c = _my_pos()

        def copy(b, k):
            return pltpu.make_async_remote_copy(
                src_ref=ins[b].at[2 * k + (1 - c)], dst_ref=outs[b].at[k],
                send_sem=send_sems.at[b, k], recv_sem=recv_sems.at[b, k],
                device_id=(x, y, 1 - c), device_id_type=MESH)

        cps = [copy(b, k) for b in range(nb) for k in range(N_CHIP)]
        for cp in cps:
            cp.start()
        for cp in cps:
            cp.wait()

    any_spec = pl.BlockSpec(memory_space=pl.ANY)
    return pl.pallas_call(
        body, name=name,
        out_shape=[jax.ShapeDtypeStruct((N_CHIP,) + b.shape[1:], b.dtype) for b in bufs],
        in_specs=[any_spec] * nb, out_specs=[any_spec] * nb,
        scratch_shapes=[pltpu.SemaphoreType.DMA((nb, N_CHIP)), pltpu.SemaphoreType.DMA((nb, N_CHIP))],
    )(*bufs)


def _pair_add(mine, recv, core, name):
    _, r, cdim = mine.shape
    tr, tc = r, cdim
    pick = lambda i: (i, 0)

    def body(core_ref, a_ref, b_ref, o_ref):
        o_ref[0] = (a_ref[0].astype(F32) + b_ref[0].astype(F32)).astype(o_ref.dtype)

    return pl.pallas_call(
        body, name=name,
        grid_spec=pltpu.PrefetchScalarGridSpec(
            num_scalar_prefetch=1, grid=(N_CHIP, (r // tr) * (cdim // tc)),
            in_specs=[pl.BlockSpec((1, tr, tc), lambda k, i, core_ref: (2 * k + core_ref[0],) + pick(i)),
                      pl.BlockSpec((1, tr, tc), lambda k, i, core_ref: (k,) + pick(i))],
            out_specs=pl.BlockSpec((1, tr, tc), lambda k, i, core_ref: (k,) + pick(i))),
        out_shape=jax.ShapeDtypeStruct((N_CHIP, r, cdim), mine.dtype),
        compiler_params=_cparams(("parallel", "parallel")),
    )(core, mine, recv)


def _chip_peer(x, y, f):
    return ((1 - x) if f[0] else x), ((1 - y) if f[1] else y)


def _exchange_chips_start(bufs, name):
    nb = len(bufs)
    nsem = 2 * 3 * nb

    def body(*refs):
        ins, lands = refs[:nb], refs[nb:2 * nb]
        sems = refs[2 * nb:2 * nb + nsem]
        token = refs[-1]
        x, y, c = _my_pos()
        for b in range(nb):
            for j, f in enumerate(CHIP_FLIPS):
                px, py = _chip_peer(x, y, f)
                pltpu.make_async_remote_copy(
                    src_ref=ins[b].at[2 * px + py], dst_ref=lands[b].at[2 * x + y],
                    send_sem=sems[2 * (3 * b + j)], recv_sem=sems[2 * (3 * b + j) + 1],
                    device_id=(px, py, c), device_id_type=MESH).start()
        token[...] = jnp.zeros_like(token)

    hbm = pl.BlockSpec(memory_space=pltpu.HBM)
    sem = pl.BlockSpec(memory_space=pltpu.SEMAPHORE)
    out = pl.pallas_call(
        body, name=name,
        out_shape=(*([pltpu.SemaphoreType.DMA(())] * nsem),
                   *[pltpu.HBM(b.shape, b.dtype) for b in bufs], *[pltpu.HBM(b.shape, b.dtype) for b in bufs],
                   jax.ShapeDtypeStruct((8, LANES), F32)),
        in_specs=[hbm] * (2 * nb),
        out_specs=(*([sem] * nsem), *([hbm] * (2 * nb)), pl.BlockSpec(memory_space=pltpu.VMEM)),
        input_output_aliases={i: nsem + i for i in range(2 * nb)},
        compiler_params=pltpu.CompilerParams(has_side_effects=pltpu.SideEffectType.DATAFLOW_SIDE_EFFECTING),
    )(*[pltpu.with_memory_space_constraint(b, pltpu.HBM) for b in bufs],
      *[pltpu.with_memory_space_constraint(lax.empty(b.shape, b.dtype), pltpu.HBM) for b in bufs])
    return out[:nsem], out[nsem:nsem + nb], out[nsem + nb:nsem + 2 * nb], out[-1]


def _exchange_chips_wait(sems, thru, lands, after, name):
    nb = len(thru)
    nsem = len(sems)

    def body(*refs):
        ins, lnd = refs[:nb], refs[nb:2 * nb]
        sem_refs = refs[2 * nb:2 * nb + nsem]
        x, y, c = _my_pos()
        for b in range(nb):
            for j, f in enumerate(CHIP_FLIPS):
                px, py = _chip_peer(x, y, f)
                cp = pltpu.make_async_remote_copy(
                    src_ref=ins[b].at[2 * px + py], dst_ref=lnd[b].at[2 * px + py],
                    send_sem=sem_refs[2 * (3 * b + j)], recv_sem=sem_refs[2 * (3 * b + j) + 1],
                    device_id=(px, py, c), device_id_type=MESH)
                cp.wait_send()
                cp.wait_recv()

    hbm = pl.BlockSpec(memory_space=pltpu.HBM)
    sem = pl.BlockSpec(memory_space=pltpu.SEMAPHORE)
    out = pl.pallas_call(
        body, name=name,
        out_shape=tuple([pltpu.HBM(b.shape, b.dtype) for b in thru] + [pltpu.HBM(b.shape, b.dtype) for b in lands]),
        in_specs=[hbm] * (2 * nb) + [sem] * nsem + [pl.BlockSpec(memory_space=pl.ANY)],
        out_specs=tuple([hbm] * (2 * nb)),
        input_output_aliases={i: i for i in range(2 * nb)},
        compiler_params=pltpu.CompilerParams(has_side_effects=pltpu.SideEffectType.DATAFLOW_SIDE_EFFECTING),
    )(*thru, *lands, *sems, after)
    return out[:nb], out[nb:]


def _bcast_start(buf, name):
    nsem = 2 * len(FLIPS)

    def body(src, land, *rest):
        sems, token = rest[:nsem], rest[-1]
        pos = _my_pos()
        for k, f in enumerate(FLIPS):
            pltpu.make_async_remote_copy(
                src_ref=src, dst_ref=land.at[_dev_index(*pos)], send_sem=sems[2 * k], recv_sem=sems[2 * k + 1],
                device_id=_flip(pos, f), device_id_type=MESH).start()
        token[...] = jnp.zeros_like(token)

    hbm = pl.BlockSpec(memory_space=pltpu.HBM)
    sem = pl.BlockSpec(memory_space=pltpu.SEMAPHORE)
    land_shape = (N_DEV,) + buf.shape
    out = pl.pallas_call(
        body, name=name,
        out_shape=(*([pltpu.SemaphoreType.DMA(())] * nsem), pltpu.HBM(buf.shape, buf.dtype),
                   pltpu.HBM(land_shape, buf.dtype), jax.ShapeDtypeStruct((8, LANES), F32)),
        in_specs=[hbm, hbm],
        out_specs=(*([sem] * nsem), hbm, hbm, pl.BlockSpec(memory_space=pltpu.VMEM)),
        input_output_aliases={0: nsem, 1: nsem + 1},
        compiler_params=pltpu.CompilerParams(has_side_effects=pltpu.SideEffectType.DATAFLOW_SIDE_EFFECTING),
    )(pltpu.with_memory_space_constraint(buf, pltpu.HBM),
      pltpu.with_memory_space_constraint(lax.empty(land_shape, buf.dtype), pltpu.HBM))
    return out[:nsem], out[nsem], out[nsem + 1], out[-1]


def _bcast_wait(sems, thru, land, after, name):
    nsem = len(sems)

    def body(src, lnd, *rest):
        sem_refs = rest[:nsem]
        pos = _my_pos()
        for k, f in enumerate(FLIPS):
            peer = _flip(pos, f)
            cp = pltpu.make_async_remote_copy(
                src_ref=src, dst_ref=lnd.at[_dev_index(*peer)], send_sem=sem_refs[2 * k],
                recv_sem=sem_refs[2 * k + 1], device_id=peer, device_id_type=MESH)
            cp.wait_send()
            cp.wait_recv()

    hbm = pl.BlockSpec(memory_space=pltpu.HBM)
    sem = pl.BlockSpec(memory_space=pltpu.SEMAPHORE)
    sent, got = pl.pallas_call(
        body, name=name,
        out_shape=(pltpu.HBM(thru.shape, thru.dtype), pltpu.HBM(land.shape, land.dtype)),
        in_specs=[hbm, hbm] + [sem] * nsem + [pl.BlockSpec(memory_space=pl.ANY)],
        out_specs=(hbm, hbm), input_output_aliases={0: 0, 1: 1},
        compiler_params=pltpu.CompilerParams(has_side_effects=pltpu.SideEffectType.DATAFLOW_SIDE_EFFECTING),
    )(thru, land, *sems, after)
    return lax.dynamic_update_slice_in_dim(got, sent[None], _dev_index(*_my_pos()), axis=0)


def _sum_slots(v, name):
    _, r, cdim = v.shape

    def body(v_ref, o_ref):
        acc = v_ref[0]
        for s in range(1, N_DEV):
            acc = acc + v_ref[s]
        o_ref[...] = acc

    return pl.pallas_call(
        body, name=name, out_shape=jax.ShapeDtypeStruct((r, cdim), F32),
        in_specs=[_full((N_DEV, r, cdim))], out_specs=_full((r, cdim)), grid=(1,),
        compiler_params=_cparams(("arbitrary",)),
    )(v)


def _mm(a, b, dims, out_dtype, tm, tn, name):
    if dims == "nn":
        (m, k), (_, n) = a.shape, b.shape
        a_spec = pl.BlockSpec((tm, k), lambda j, i: (i, 0))
        b_spec = pl.BlockSpec((k, tn), lambda j, i: (0, j))
        dn = NN
    elif dims == "nt":
        (m, k), (n, _) = a.shape, b.shape
        a_spec = pl.BlockSpec((tm, k), lambda j, i: (i, 0))
        b_spec = pl.BlockSpec((tn, k), lambda j, i: (j, 0))
        dn = NT
    else:
        (k, m), (_, n) = a.shape, b.shape
        a_spec = pl.BlockSpec((k, tm), lambda j, i: (0, i))
        b_spec = pl.BlockSpec((k, tn), lambda j, i: (0, j))
        dn = TN
    assert m % tm == 0 and n % tn == 0, (m, tm, n, tn)

    def body(a_ref, b_ref, o_ref):
        o_ref[...] = _dot(a_ref[...], b_ref[...], dn).astype(o_ref.dtype)

    return pl.pallas_call(
        body, name=name, grid=(n // tn, m // tm),
        in_specs=[a_spec, b_spec], out_specs=pl.BlockSpec((tm, tn), lambda j, i: (i, j)),
        out_shape=jax.ShapeDtypeStruct((m, n), out_dtype),
        compiler_params=_cparams(("parallel", "parallel")),
    )(a, b)


def _mm_side_by_side(a, bs, out_dtype, tm, name):
    (m, k), (n, _) = a.shape, bs[0].shape
    assert m % tm == 0 and all(b.shape == (n, k) for b in bs)

    def body(a_ref, *refs):
        o_ref = refs[-1]
        for t, b_ref in enumerate(refs[:-1]):
            @pl.when(pl.program_id(0) == t)
            def _():
                o_ref[...] = _dot(a_ref[...], b_ref[...], NT).astype(o_ref.dtype)

    return pl.pallas_call(
        body, name=name, grid=(len(bs), m // tm),
        in_specs=[pl.BlockSpec((tm, k), lambda j, i: (i, 0))]
        + [pl.BlockSpec((n, k), lambda j, i: (0, 0), pipeline_mode=pl.Buffered(1)) for _ in bs],
        out_specs=pl.BlockSpec((tm, n), lambda j, i: (i, j)),
        out_shape=jax.ShapeDtypeStruct((m, len(bs) * n), out_dtype),
        compiler_params=_cparams(("parallel", "parallel")),
    )(a, *bs)


def _tiles_2d(r, cdim):
    if r % CHUNK == 0:
        return CHUNK, cdim, True
    return r, _tile(cdim, (256, 128)), False


def _dgrad_prenorm(a_list, b_list, head, x2, w, dout, tm, name):
    n_op = len(a_list)
    m, d = a_list[0].shape[0], b_list[0].shape[1]
    subs = _x_row_specs(tm, d)
    last = m // tm - 1
    rest = tm - CHUNK

    def body(*refs):
        a_refs, b_refs = refs[:n_op], refs[n_op:2 * n_op]
        head_ref = refs[2 * n_op]
        x_refs = refs[2 * n_op + 1:2 * n_op + 1 + len(subs)]
        w_ref, dout_ref, gx_ref, ghead_ref, gw_ref, dh_buf, sem = refs[2 * n_op + 1 + len(subs):]
        i = pl.program_id(0)

        def first_copy():
            return pltpu.make_async_copy(dh_buf.at[pl.ds(CHUNK, rest)], gx_ref.at[pl.ds(0, rest)], sem)

        def later_copy(step):
            return pltpu.make_async_copy(dh_buf, gx_ref.at[pl.ds(pl.multiple_of(step * tm - CHUNK, CHUNK), tm)], sem)

        @pl.when(i == 0)
        def _():
            gw_ref[...] = jnp.zeros_like(gw_ref)

        du = _dot(a_refs[0][...], b_refs[0][...])
        for k in range(1, n_op):
            du = du + _dot(a_refs[k][...], b_refs[k][...])
        first = jnp.where(i == 0, head_ref[...], x_refs[0][...])
        h = jnp.concatenate([first] + [r[...] for r in x_refs[1:]], axis=0)
        rstd = lax.rsqrt(jnp.mean(h * h, axis=-1, keepdims=True) + EPS)
        xhat = h * rstd
        dxh = du * w_ref[...]
        dh = rstd * (dxh - xhat * jnp.mean(dxh * xhat, axis=-1, keepdims=True)) + dout_ref[...]
        gw_ref[0:1, :] += jnp.sum(du * xhat, axis=0, keepdims=True)

        if rest and last >= 1:
            @pl.when(i == 1)
            def _():
                first_copy().wait()

        @pl.when(i >= (2 if rest else 1))
        def _():
            later_copy(i - 1).wait()

        dh_buf[...] = dh

        @pl.when(i == 0)
        def _():
            ghead_ref[...] = dh_buf[0:CHUNK, :]
            if rest:
                first_copy().start()
                if last == 0:
                    first_copy().wait()

        @pl.when(i >= 1)
        def _():
            later_copy(i).start()

        if last >= 1:
            @pl.when(i == last)
            def _():
                later_copy(i).wait()

    once = lambda b: pl.BlockSpec(b.shape, lambda i: (0, 0), pipeline_mode=pl.Buffered(1))
    row = lambda width: pl.BlockSpec((tm, width), lambda i: (i, 0))
    return pl.pallas_call(
        body, name=name, grid=(m // tm,),
        in_specs=([row(a.shape[1]) for a in a_list] + [once(b) for b in b_list]
                  + [_full((CHUNK, d))] + subs + [_full((1, d)), row(d)]),
        out_specs=[pl.BlockSpec(memory_space=pl.ANY), _full((CHUNK, d)), _full((8, d))],
        out_shape=[jax.ShapeDtypeStruct((m - CHUNK, d), F32), jax.ShapeDtypeStruct((CHUNK, d), F32),
                   jax.ShapeDtypeStruct((8, d), F32)],
        scratch_shapes=[pltpu.VMEM((tm, d), F32), pltpu.SemaphoreType.DMA],
        compiler_params=_cparams(("arbitrary",)),
    )(*a_list, *b_list, head, *([x2] * len(subs)), w, dout)


def _tile(n, prefs):
    for t in prefs:
        if n % t == 0:
            return t
    return n


def _rows3(i):
    return jnp.maximum(3 * i - 1, 0), 3 * i, 3 * i + 1


def _x_row_specs(tm, d):
    if tm == CHUNK:
        return [pl.BlockSpec((CHUNK, d), lambda i: (jnp.maximum(i - 1, 0), 0))]
    return [pl.BlockSpec((CHUNK, d), functools.partial(lambda i, k: (_rows3(i)[k], 0), k=k)) for k in range(3)]


def _prenorm_fwd(head, x2, w, tm):
    p, d = x2.shape[0] + CHUNK, x2.shape[1]
    subs = _x_row_specs(tm, d)

    def body(head_ref, *rest):
        x_refs, (w_ref, u_ref) = rest[:len(subs)], rest[len(subs):]
        i = pl.program_id(0)
        first = jnp.where(i == 0, head_ref[...], x_refs[0][...])
        h = jnp.concatenate([first] + [r[...] for r in x_refs[1:]], axis=0)
        ms = jnp.mean(h * h, axis=-1, keepdims=True)
        u_ref[...] = (h * lax.rsqrt(ms + EPS) * w_ref[...]).astype(BF16)

    return pl.pallas_call(
        body, name="prenorm_fwd", grid=(p // tm,),
        in_specs=[_full((CHUNK, d))] + subs + [_full((1, d))],
        out_specs=pl.BlockSpec((tm, d), lambda i: (i, 0)),
        out_shape=jax.ShapeDtypeStruct((p, d), BF16),
        compiler_params=_cparams(("arbitrary",)),
    )(head, *([x2] * len(subs)), w)


def _conv_pre(ext_ref, cw_ref, cb_ref):
    pre = cb_ref[...] + cw_ref[CONV_K - 1:CONV_K, :] * ext_ref[8:8 + CHUNK, :]
    for j in range(1, CONV_K):
        pre = pre + cw_ref[CONV_K - 1 - j:CONV_K - j, :] * ext_ref[8 - j:8 - j + CHUNK, :]
    return pre


def _ssd_scalars(dtf_ref, brow_ref, alog_ref, rowmask, hs, ha, tri):
    lane = lax.broadcasted_iota(jnp.int32, (1, LANES), 1)
    is_dt = lane < hs
    is_f = (lane >= hs) & (lane < hs + ha)
    dtr = dtf_ref[...] + brow_ref[...]
    sp = _softplus(dtr)
    dt = jnp.where(is_dt, sp, 0.0) * rowmask
    logf = jnp.where(is_f, jnp.minimum(dtr, 0.0) - jnp.log(1.0 + jnp.exp(-jnp.abs(dtr))), 0.0) * rowmask
    a_row = jnp.where(is_dt, -jnp.exp(alog_ref[...]), 0.0)
    run = _dot_tri(tri, dt * a_row + logf)
    return dtr, dt, a_row, run, is_dt, is_f


def _tri_mats():
    r = lax.broadcasted_iota(jnp.int32, (CHUNK, CHUNK), 0)
    c = lax.broadcasted_iota(jnp.int32, (CHUNK, CHUNK), 1)
    return r, c


def _ssd_fwd(xbc, z, dtf, conv_w, conv_b, brow, alog, dskip_l, ssd_norm, sel_t, hs, ha):
    p, cd = xbc.shape
    ds = z.shape[1]
    ns = (cd - ds) // (2 * SSD_GROUPS)
    gw = ds // SSD_GROUPS
    nch = p // CHUNK
    hpg = hs // SSD_GROUPS

    def body(xbc_ref, halo_ref, z_ref, dtf_ref, cw_ref, cb_ref, brow_ref, alog_ref, dsk_ref, nrm_ref, selt_ref,
             y_ref, yssd_ref, hin_ref, cf_ref, pre_ref, st_ref, carry_ref, yacc_ref, xc_s, ex_s, xdtb_s, xwb_s, ext_s):
        c = pl.program_id(0)

        @pl.when(c == 0)
        def _():
            st_ref[...] = jnp.zeros_like(st_ref)
            carry_ref[...] = jnp.zeros_like(carry_ref)

        rows = lax.broadcasted_iota(jnp.int32, (CHUNK, 1), 0)
        rowmask = jnp.where((rows >= PADN) | (c > 0), 1.0, 0.0)
        ri, ci = _tri_mats()
        causal = ri >= ci
        tri = jnp.where(causal, 1.0, 0.0).astype(BF16)

        ext_s[0:8, :] = halo_ref[...].astype(F32)[HALO - 8:, :] * jnp.where(c > 0, 1.0, 0.0)
        ext_s[8:, :] = xbc_ref[...].astype(F32)
        pre = _conv_pre(ext_s, cw_ref, cb_ref)
        pre_ref[...] = pre.astype(BF16)
        xc_s[...] = pre * _sigmoid(pre) * rowmask

        dtr, dt, a_row, run, is_dt, is_f = _ssd_scalars(dtf_ref, brow_ref, alog_ref, rowmask, hs, ha, tri)
        cf = run + carry_ref[...]
        cf_ref[...] = cf
        carry_ref[...] = jnp.where(is_f, cf[CHUNK - 1:CHUNK, :], 0.0)
        cs = jnp.where(is_dt, run, 0.0)
        cl = cs[CHUNK - 1:CHUNK, :]
        selt = selt_ref[...]
        ex_s[...] = _dot_sel(jnp.exp(cs), selt)
        cdec_x = _dot_sel(jnp.broadcast_to(jnp.exp(cl), (8, LANES)), selt)[0:1, :]
        cs_t = cs.T
        xdt = xc_s[:, :ds] * _dot_sel(dt, selt)
        xdtb_s[...] = xdt.astype(BF16)
        xwb_s[...] = (xdt * _dot_sel(jnp.exp(cl - cs), selt)).astype(BF16)

        lane = lax.broadcasted_iota(jnp.int32, (1, LANES), 1)
        half0 = lane < HEAD_DIM
        for g in range(SSD_GROUPS):
            bg = xc_s[:, ds + g * ns: ds + (g + 1) * ns].astype(BF16)
            cg = xc_s[:, ds + SSD_GROUPS * ns + g * ns: ds + SSD_GROUPS * ns + (g + 1) * ns].astype(BF16)
            gm = _dot(cg, bg, NT)
            gs = slice(g * gw, (g + 1) * gw)
            stg = st_ref[:, gs]
            stg_b = stg.astype(BF16)
            hin_ref[0, :, gs] = stg_b
            yoff = _dot(cg, stg_b) * ex_s[:, gs]
            for pr in range(gw // LANES):
                sl = slice(g * gw + pr * LANES, g * gw + (pr + 1) * LANES)
                xp = xdtb_s[:, sl]
                yd = jnp.zeros((CHUNK, LANES), F32)
                for j in range(2):
                    h = g * hpg + 2 * pr + j
                    seg = cs[:, h:h + 1] - cs_t[h:h + 1, :]
                    m = jnp.where(causal, gm * jnp.exp(jnp.minimum(seg, 0.0)), 0.0).astype(BF16)
                    sel = half0 if j == 0 else jnp.logical_not(half0)
                    yd = yd + _dot(m, jnp.where(sel, xp, jnp.zeros_like(xp)))
                yacc_ref[:, sl] = yd + yoff[:, pr * LANES:(pr + 1) * LANES] + dsk_ref[:, sl] * xc_s[:, sl]
            st_ref[:, gs] = stg * cdec_x[:, gs] + _dot(bg, xwb_s[:, gs], TN)

        y = yacc_ref[...]
        y_ref[...] = y.astype(BF16)
        zf = z_ref[...].astype(F32)
        u = y * zf * _sigmoid(zf)
        for g in range(SSD_GROUPS):
            gs = slice(g * gw, (g + 1) * gw)
            ug = u[:, gs]
            ms = jnp.mean(ug * ug, axis=-1, keepdims=True)
            yssd_ref[:, gs] = (ug * lax.rsqrt(ms + EPS) * nrm_ref[:, gs]).astype(BF16)

    rb = CHUNK // HALO
    return pl.pallas_call(
        body, name="ssd_fwd", grid=(nch,),
        in_specs=[pl.BlockSpec((CHUNK, cd), lambda c: (c, 0)),
                  pl.BlockSpec((HALO, cd), lambda c: (jnp.maximum(c * rb - 1, 0), 0)),
                  pl.BlockSpec((CHUNK, ds), lambda c: (c, 0)),
                  pl.BlockSpec((CHUNK, LANES), lambda c: (c, 0)),
                  _full((CONV_K, cd)), _full((1, cd)), _full((1, LANES)), _full((1, LANES)),
                  _full((1, ds)), _full((1, ds)), _full((LANES, ds))],
        out_specs=[pl.BlockSpec((CHUNK, ds), lambda c: (c, 0)), pl.BlockSpec((CHUNK, ds), lambda c: (c, 0)),
                   pl.BlockSpec((1, ns, ds), lambda c: (c, 0, 0)), pl.BlockSpec((CHUNK, LANES), lambda c: (c, 0)),
                   pl.BlockSpec((CHUNK, cd), lambda c: (c, 0))],
        out_shape=[jax.ShapeDtypeStruct((p, ds), BF16), jax.ShapeDtypeStruct((p, ds), BF16),
                   jax.ShapeDtypeStruct((nch, ns, ds), BF16), jax.ShapeDtypeStruct((p, LANES), F32),
                   jax.ShapeDtypeStruct((p, cd), BF16)],
        scratch_shapes=[pltpu.VMEM((ns, ds), F32), pltpu.VMEM((1, LANES), F32), pltpu.VMEM((CHUNK, ds), F32),
                        pltpu.VMEM((CHUNK, cd), F32), pltpu.VMEM((CHUNK, ds), F32),
                        pltpu.VMEM((CHUNK, ds), BF16), pltpu.VMEM((CHUNK, ds), BF16),
                        pltpu.VMEM((8 + CHUNK, cd), F32)],
        compiler_params=_cparams(("arbitrary",)),
    )(xbc, xbc, z, dtf, conv_w, conv_b, brow, alog, dskip_l, ssd_norm, sel_t)


def _ssd_bwd(dyssd, y, z, xbc, pre, dtf, hin, dcf, conv_w, brow, alog, dskip_l, ssd_norm, sel_t, sel, hs, ha):
    p, cd = xbc.shape
    ds = z.shape[1]
    ns = (cd - ds) // (2 * SSD_GROUPS)
    gw = ds // SSD_GROUPS
    nch = p // CHUNK
    hpg = hs // SSD_GROUPS

    def body(dyssd_ref, y_ref, z_ref, xbc_ref, pre_ref, dtf_ref, hin_ref, dcf_ref, cw_ref, brow_ref,
             alog_ref, dsk_ref, nrm_ref, selt_ref, sel_ref,
             dxbc_ref, dz_ref, ddtf_ref, gcw_ref, gcb_ref, gnrm_ref, gsm_ref,
             dst_ref, nxt_ref, fcar_ref, gdsk_ref, dxc_ref, xc_s, dsl_s, dtx_s, ex_s, wx_s, dy_s, xdtb_s, xwb_s,
             dyb_s, dyeb_s):
        step = pl.program_id(0)
        c = nch - 1 - step

        @pl.when(step == 0)
        def _():
            dst_ref[...] = jnp.zeros_like(dst_ref)
            nxt_ref[...] = jnp.zeros_like(nxt_ref)
            fcar_ref[...] = jnp.zeros_like(fcar_ref)
            gdsk_ref[...] = jnp.zeros_like(gdsk_ref)
            gcw_ref[...] = jnp.zeros_like(gcw_ref)
            gcb_ref[...] = jnp.zeros_like(gcb_ref)
            gnrm_ref[...] = jnp.zeros_like(gnrm_ref)
            gsm_ref[...] = jnp.zeros_like(gsm_ref)

        rows = lax.broadcasted_iota(jnp.int32, (CHUNK, 1), 0)
        rowmask = jnp.where((rows >= PADN) | (c > 0), 1.0, 0.0)
        ri, ci = _tri_mats()
        causal = ri >= ci
        anti = ci >= ri
        tri = jnp.where(causal, 1.0, 0.0).astype(BF16)
        rtri = jnp.where(anti, 1.0, 0.0).astype(BF16)

        pre = pre_ref[...].astype(F32)
        sg = _sigmoid(pre)
        xc_s[...] = pre * sg * rowmask
        dsl_s[...] = sg * (1.0 + pre * (1.0 - sg)) * rowmask

        dtr, dt, a_row, run, is_dt, is_f = _ssd_scalars(dtf_ref, brow_ref, alog_ref, rowmask, hs, ha, tri)
        cs = jnp.where(is_dt, run, 0.0)
        cl = cs[CHUNK - 1:CHUNK, :]
        selt = selt_ref[...]
        selm = sel_ref[...]
        dtx_s[...] = _dot_sel(dt, selt)
        ex_s[...] = _dot_sel(jnp.exp(cs), selt)
        wx_s[...] = _dot_sel(jnp.exp(cl - cs), selt)
        cdec = jnp.exp(cl)
        cdec_x = _dot_sel(jnp.broadcast_to(cdec, (8, LANES)), selt)[0:1, :]
        cs_t = cs.T
        xdt = xc_s[:, :ds] * dtx_s[...]
        xdtb_s[...] = xdt.astype(BF16)
        xwb_s[...] = (xdt * wx_s[...]).astype(BF16)

        yv = y_ref[...].astype(F32)
        zf = z_ref[...].astype(F32)
        sz = _sigmoid(zf)
        u = yv * zf * sz
        dyo = dyssd_ref[...].astype(F32)
        du_parts = []
        for g in range(SSD_GROUPS):
            gs = slice(g * gw, (g + 1) * gw)
            ug = u[:, gs]
            rstd = lax.rsqrt(jnp.mean(ug * ug, axis=-1, keepdims=True) + EPS)
            yhat = ug * rstd
            dyg = dyo[:, gs]
            gnrm_ref[0:1, gs] += jnp.sum(dyg * yhat, axis=0, keepdims=True)
            dyh = dyg * nrm_ref[:, gs]
            du_parts.append(rstd * (dyh - yhat * jnp.mean(dyh * yhat, axis=-1, keepdims=True)))
        du = jnp.concatenate(du_parts, axis=1)
        dy = du * zf * sz
        dz_ref[...] = (du * yv * sz * (1.0 + zf * (1.0 - sz))).astype(BF16)
        dy_s[...] = dy
        dyb_s[...] = dy.astype(BF16)
        dyeb_s[...] = (dy * ex_s[...]).astype(BF16)
        gdsk_ref[...] += jnp.sum(dy * xc_s[:, :ds], axis=0, keepdims=True)
        lane = lax.broadcasted_iota(jnp.int32, (1, LANES), 1)
        half0 = lane < HEAD_DIM
        x_parts, yo_parts, t4_parts = [], [], []
        dcs = jnp.zeros((CHUNK, LANES), F32)
        for g in range(SSD_GROUPS):
            gs = slice(g * gw, (g + 1) * gw)
            bsl = slice(ds + g * ns, ds + (g + 1) * ns)
            csl = slice(ds + SSD_GROUPS * ns + g * ns, ds + SSD_GROUPS * ns + (g + 1) * ns)
            bg = xc_s[:, bsl].astype(BF16)
            cg = xc_s[:, csl].astype(BF16)
            gm = _dot(cg, bg, NT)
            gm_t = _dot(bg, cg, NT)
            stg_b = hin_ref[0, :, gs]
            dstg = dst_ref[:, gs]
            dstg_b = dstg.astype(BF16)
            t4_parts.append(jnp.sum(dstg * stg_b.astype(F32), axis=0, keepdims=True))
            zst = _dot(bg, dstg_b) * wx_s[:, gs]
            x_parts.append(xc_s[:, gs] * dtx_s[:, gs] * zst)
            yo_parts.append(dy_s[:, gs] * (_dot(cg, stg_b) * ex_s[:, gs]))
            dgsum = jnp.zeros((CHUNK, CHUNK), F32)
            dgtsum = jnp.zeros((CHUNK, CHUNK), F32)
            for pr in range(gw // LANES):
                sl = slice(g * gw + pr * LANES, g * gw + (pr + 1) * LANES)
                xp = xdtb_s[:, sl]
                dyp = dyb_s[:, sl]
                dxd = zst[:, pr * LANES:(pr + 1) * LANES]
                for j in range(2):
                    h = g * hpg + 2 * pr + j
                    sel_l = half0 if j == 0 else jnp.logical_not(half0)
                    seg = cs[:, h:h + 1] - cs_t[h:h + 1, :]
                    lm = jnp.where(causal, jnp.exp(jnp.minimum(seg, 0.0)), 0.0)
                    lmt = lm.T
                    dyp_m = jnp.where(sel_l, dyp, jnp.zeros_like(dyp))
                    xp_m = jnp.where(sel_l, xp, jnp.zeros_like(xp))
                    dxd = dxd + _dot((gm_t * lmt).astype(BF16), dyp_m)
                    dg = _dot(dyp_m, xp, NT) * lm
                    dgt = _dot(xp_m, dyp, NT) * lmt
                    dgsum = dgsum + dg
                    dgtsum = dgtsum + dgt
                    qrow = (jnp.sum(dg * gm, axis=1, keepdims=True) - jnp.sum(dgt * gm_t, axis=1, keepdims=True))
                    dcs = dcs + jnp.where(lane == h, qrow, 0.0)
                dxc_ref[:, sl] = dxd
            dxc_ref[:, csl] = _dot(dgsum.astype(BF16), bg) + _dot(dyeb_s[:, gs], stg_b, NT)
            dxc_ref[:, bsl] = _dot(dgtsum.astype(BF16), cg) + _dot(xwb_s[:, gs], dstg_b, NT)
            dst_ref[:, gs] = dstg * cdec_x[:, gs] + _dot(cg, dyeb_s[:, gs], TN)

        dxdt = dxc_ref[:, :ds]
        xst = _dot_sel(jnp.concatenate(x_parts, axis=1), selm)
        yo = _dot_sel(jnp.concatenate(yo_parts, axis=1), selm)
        t4 = _dot_sel(jnp.concatenate([jnp.concatenate(t4_parts, axis=1), jnp.zeros((7, ds), F32)], axis=0), selm)
        dcl = jnp.sum(xst, axis=0, keepdims=True) + cdec * t4[0:1, :]
        dcs = dcs + yo - xst + jnp.where(rows == CHUNK - 1, dcl, 0.0)
        da_ = _dot_tri(rtri, dcs)
        ddt = _dot_sel(dxdt * xc_s[:, :ds], selm) + da_ * a_row
        dcf_blk = dcf_ref[...]
        dlogf = _dot_tri(rtri, dcf_blk) + fcar_ref[...]
        fcar_ref[...] += jnp.sum(dcf_blk, axis=0, keepdims=True)
        sgd = _sigmoid(dtr)
        ddtf = (jnp.where(is_dt, ddt * sgd, 0.0) + jnp.where(is_f, dlogf * (1.0 - sgd), 0.0)) * rowmask
        ddtf_ref[...] = ddtf
        gsm_ref[0:1, :] += jnp.sum(ddtf, axis=0, keepdims=True)
        gsm_ref[1:2, :] += jnp.sum(da_ * dt, axis=0, keepdims=True) * a_row

        dxc_ref[:, :ds] = dxdt * dtx_s[...] + dsk_ref[...] * dy_s[...]
        dpre = dxc_ref[...] * dsl_s[...]
        nxt_ref[0:CHUNK, :] = dpre
        gcb_ref[0:1, :] += jnp.sum(dpre, axis=0, keepdims=True)
        xr = xbc_ref[...].astype(F32)
        gcw_ref[CONV_K - 1:CONV_K, :] += jnp.sum(dpre * xr, axis=0, keepdims=True)
        dxr = cw_ref[CONV_K - 1:CONV_K, :] * dpre
        for j in range(1, CONV_K):
            up = nxt_ref[j:j + CHUNK, :]
            gcw_ref[CONV_K - 1 - j:CONV_K - j, :] += jnp.sum(up * xr, axis=0, keepdims=True)
            dxr = dxr + cw_ref[CONV_K - 1 - j:CONV_K - j, :] * up
        nxt_ref[CHUNK:, :] = dpre[0:8, :]
        dxbc_ref[...] = dxr.astype(BF16)

        @pl.when(step == nch - 1)
        def _():
            gsm_ref[2:3, :] = _dot_sel(jnp.broadcast_to(gdsk_ref[...], (8, ds)), selm)[0:1, :]

    rev = lambda s: nch - 1 - s
    blk = lambda w: pl.BlockSpec((CHUNK, w), lambda s: (rev(s), 0))
    return pl.pallas_call(
        body, name="ssd_bwd", grid=(nch,),
        in_specs=[blk(ds), blk(ds), blk(ds), blk(cd), blk(cd),
                  blk(LANES), pl.BlockSpec((1, ns, ds), lambda s: (rev(s), 0, 0)), blk(LANES),
                  _full((CONV_K, cd)), _full((1, LANES)), _full((1, LANES)),
                  _full((1, ds)), _full((1, ds)), _full((LANES, ds)), _full((ds, LANES))],
        out_specs=[blk(cd), blk(ds), blk(LANES), _full((8, cd)), _full((8, cd)), _full((8, ds)), _full((8, LANES))],
        out_shape=[jax.ShapeDtypeStruct((p, cd), BF16), jax.ShapeDtypeStruct((p, ds), BF16),
                   jax.ShapeDtypeStruct((p, LANES), F32), jax.ShapeDtypeStruct((8, cd), F32),
                   jax.ShapeDtypeStruct((8, cd), F32), jax.ShapeDtypeStruct((8, ds), F32),
                   jax.ShapeDtypeStruct((8, LANES), F32)],
        scratch_shapes=[pltpu.VMEM((ns, ds), F32), pltpu.VMEM((CHUNK + 8, cd), F32), pltpu.VMEM((1, LANES), F32),
                        pltpu.VMEM((1, ds), F32), pltpu.VMEM((CHUNK, cd), F32),
                        pltpu.VMEM((CHUNK, cd), F32), pltpu.VMEM((CHUNK, cd), F32),
                        pltpu.VMEM((CHUNK, ds), F32), pltpu.VMEM((CHUNK, ds), F32), pltpu.VMEM((CHUNK, ds), F32),
                        pltpu.VMEM((CHUNK, ds), F32), pltpu.VMEM((CHUNK, ds), BF16), pltpu.VMEM((CHUNK, ds), BF16),
                        pltpu.VMEM((CHUNK, ds), BF16), pltpu.VMEM((CHUNK, ds), BF16)],
        compiler_params=_cparams(("arbitrary",)),
    )(dyssd, y, z, xbc, pre, dtf, hin, dcf, conv_w, brow, alog, dskip_l, ssd_norm, sel_t, sel)


def _attn_fwd(qkv, ck, blk):
    p, da = qkv.shape[0], qkv.shape[1] // 3
    npair, nkb = ck.shape[0], ck.shape[1]
    scale = 1.0 / math.sqrt(HEAD_DIM)

    def body(q_ref, k_ref, v_ref, ck_ref, o_ref, lse_ref):
        i = pl.program_id(1)
        lane = lax.broadcasted_iota(jnp.int32, (1, LANES), 1)
        sels = [lane < HEAD_DIM, lane >= HEAD_DIM]
        ones = [jnp.where(lane == HEAD_DIM, 1.0, 0.0).astype(BF16), jnp.where(lane == 0, 1.0, 0.0).astype(BF16)]
        qb = q_ref[...] * scale

        def step(kb, carry, masked, nk=1):
            r0 = pl.multiple_of(kb * blk, blk)
            ks = k_ref[pl.ds(r0, nk * blk), :]
            vs = v_ref[pl.ds(r0, nk * blk), :]
            kk = jnp.concatenate([jnp.where(sel, ks, jnp.zeros_like(ks)) for sel in sels], axis=0)
            s_both = _dot(qb, kk, NT)
            out = []
            for j in range(2):
                m, acc = carry[2 * j], carry[2 * j + 1]
                ckr = jnp.concatenate([ck_ref[0, kb + t, j:j + 1, :] for t in range(nk)], axis=1)
                s = s_both[:, j * nk * blk:(j + 1) * nk * blk] - ckr
                if masked:
                    col = lax.broadcasted_iota(jnp.int32, (blk, nk * blk), 1) - (nk - 1) * blk
                    s = jnp.where(col <= lax.broadcasted_iota(jnp.int32, (blk, nk * blk), 0), s, NEG)
                mn = jnp.maximum(m, jnp.max(s, axis=-1, keepdims=True))
                pr = jnp.exp(s - mn).astype(BF16)
                acc = jnp.exp(m - mn) * acc + _dot(pr, jnp.where(sels[j], vs, ones[j]))
                out += [mn, acc]
            return tuple(out)

        init = (jnp.full((blk, 1), NEG, F32), jnp.zeros((blk, LANES), F32)) * 2

        def finish(carry):
            m0, a0, m1, a1 = carry
            l0 = a0[:, HEAD_DIM:HEAD_DIM + 1]
            l1 = a1[:, 0:1]
            o_ref[...] = jnp.where(sels[0], a0 / l0, a1 / l1).astype(BF16)
            lse_ref[...] = jnp.where(sels[0], m0 + jnp.log(l0), m1 + jnp.log(l1))

        @pl.when(i == 0)
        def _():
            finish(step(0, init, True))

        @pl.when(i > 0)
        def _():
            below = i - 1
            n4 = below // 4
            n2 = (below - 4 * n4) // 2
            carry = lax.fori_loop(0, n4, lambda t, c: step(4 * t, c, False, 4), init)
            carry = lax.fori_loop(0, n2, lambda t, c: step(4 * n4 + 2 * t, c, False, 2), carry)
            carry = lax.fori_loop(4 * n4 + 2 * n2, below, lambda kb, c: step(kb, c, False), carry)
            finish(step(below, carry, True, 2))

    return pl.pallas_call(
        body, name="attn_fwd", grid=(npair, p // blk),
        in_specs=[pl.BlockSpec((blk, LANES), lambda h, i: (i, h)),
                  pl.BlockSpec((p, LANES), lambda h, i: (0, npair + h)),
                  pl.BlockSpec((p, LANES), lambda h, i: (0, 2 * npair + h)),
                  pl.BlockSpec((1, nkb, 8, blk), lambda h, i: (h, 0, 0, 0))],
        out_specs=[pl.BlockSpec((blk, LANES), lambda h, i: (i, h)), pl.BlockSpec((blk, LANES), lambda h, i: (i, h))],
        out_shape=[jax.ShapeDtypeStruct((p, da), BF16), jax.ShapeDtypeStruct((p, da), F32)],
        compiler_params=_cparams(("parallel", "arbitrary")),
    )(qkv, qkv, qkv, ck)


def _attn_bwd(qkv, o, do, lse_rep, ck, blk):
    p, da = qkv.shape[0], qkv.shape[1] // 3
    npair, nkb = ck.shape[0], ck.shape[1]
    nq = p // blk
    scale = 1.0 / math.sqrt(HEAD_DIM)

    def body(k_ref, v_ref, q_ref, do_ref, o_ref, lse_ref, ck_ref, dk_ref, dv_ref, dq_ref, dcs_ref, rsum_ref, dq_acc):
        jb = pl.program_id(1)

        @pl.when(jb == 0)
        def _():
            dq_acc[...] = jnp.zeros_like(dq_acc)

        ks = k_ref[...]
        vs = v_ref[...]
        lane = lax.broadcasted_iota(jnp.int32, (1, LANES), 1)
        sels = [lane < HEAD_DIM, lane >= HEAD_DIM]
        ones = [jnp.where(lane == HEAD_DIM, 1.0, 0.0).astype(BF16), jnp.where(lane == 0, 1.0, 0.0).astype(BF16)]
        kss = ks * scale
        kmo = [jnp.where(sels[j], kss, ones[j]) for j in range(2)]

        def step(ib, carry, masked, nb=1):
            rows = nb * blk
            r0 = pl.multiple_of(ib * blk, blk)
            qb = q_ref[pl.ds(r0, rows), :] * scale
            dob = do_ref[pl.ds(r0, rows), :]
            prod = dob.astype(F32) * o_ref[pl.ds(r0, rows), :].astype(F32)
            out = []
            for j in range(2):
                dk, dv = carry[2 * j], carry[2 * j + 1]
                qm = jnp.where(sels[j], qb, jnp.zeros_like(qb))
                dom = jnp.where(sels[j], dob, jnp.zeros_like(dob))
                lse = lse_ref[pl.ds(r0, rows), HEAD_DIM * j:HEAD_DIM * j + 1]
                dlt = jnp.sum(jnp.where(sels[j], prod, 0.0), axis=-1, keepdims=True)
                s = _dot(qm, ks, NT) - ck_ref[0, 0, j:j + 1, :] - lse
                pm = jnp.exp(jnp.minimum(s, 0.0))
                if masked:
                    causal = (lax.broadcasted_iota(jnp.int32, (rows, blk), 1)
                              <= lax.broadcasted_iota(jnp.int32, (rows, blk), 0))
                    pm = jnp.where(causal, pm, 0.0)
                ds_b = (pm * (_dot(dom, vs, NT) - dlt)).astype(BF16)
                dv = dv + _dot(pm.astype(BF16), dom, TN)
                dk = dk + _dot(ds_b, jnp.where(sels[j], qb, ones[j]), TN)
                dq_acc[pl.ds(r0, rows), LANES * j:LANES * (j + 1)] += _dot(ds_b, kmo[j])
                out += [dk, dv]
            return tuple(out)

        pair8 = lambda c0, c1: jnp.where(lane == 0, c0, jnp.where(lane == 1, c1, 0.0)).T[0:8]
        zero = jnp.zeros((blk, LANES), F32)
        init = (zero, zero, zero, zero)

        def finish(carry):
            dk0, dv0, dk1, dv1 = carry
            dk_ref[...] = jnp.where(sels[0], dk0, dk1).astype(BF16)
            dv_ref[...] = (dv0 + dv1).astype(BF16)
            dcs_ref[0] = pair8(dk0[:, HEAD_DIM:HEAD_DIM + 1], dk1[:, 0:1])

        @pl.when(jb == nq - 1)
        def _():
            finish(step(jb, init, True))

        @pl.when(jb < nq - 1)
        def _():
            carry = step(jb, init, True, 2)
            n4 = (nq - 2 - jb) // 4
            n2 = (nq - 2 - jb - 4 * n4) // 2
            carry = lax.fori_loop(0, n4, lambda t, c: step(jb + 2 + 4 * t, c, False, 4), carry)
            carry = lax.fori_loop(0, n2, lambda t, c: step(jb + 2 + 4 * n4 + 2 * t, c, False, 2), carry)
            finish(lax.fori_loop(jb + 2 + 4 * n4 + 2 * n2, nq, lambda ib, c: step(ib, c, False), carry))

        @pl.when(jb == nkb - 1)
        def _():
            a0 = dq_acc[:, :LANES]
            a1 = dq_acc[:, LANES:]
            dq_ref[...] = jnp.where(sels[0], a0, a1).astype(BF16)
            rsum_ref[0] = pair8(a0[:, HEAD_DIM:HEAD_DIM + 1], a1[:, 0:1])

    colblk = pl.BlockSpec((blk, LANES), lambda h, j: (j, h))
    colfull = pl.BlockSpec((p, LANES), lambda h, j: (0, h))
    ckspec = pl.BlockSpec((1, 1, 8, blk), lambda h, j: (h, j, 0, 0))
    return pl.pallas_call(
        body, name="attn_bwd", grid=(npair, nkb),
        in_specs=[pl.BlockSpec((blk, LANES), lambda h, j: (j, npair + h)),
                  pl.BlockSpec((blk, LANES), lambda h, j: (j, 2 * npair + h)),
                  colfull, colfull, colfull, colfull, ckspec],
        out_specs=[colblk, colblk, colfull, pl.BlockSpec((1, 8, blk), lambda h, j: (h, 0, j)),
                   pl.BlockSpec((1, 8, p), lambda h, j: (h, 0, 0))],
        out_shape=[jax.ShapeDtypeStruct((p, da), BF16), jax.ShapeDtypeStruct((p, da), BF16),
                   jax.ShapeDtypeStruct((p, da), BF16), jax.ShapeDtypeStruct((npair, 8, p), F32),
                   jax.ShapeDtypeStruct((npair, 8, p), F32)],
        scratch_shapes=[pltpu.VMEM((p, 2 * LANES), F32)],
        compiler_params=_cparams(("parallel", "arbitrary")),
    )(qkv, qkv, qkv, do, o, lse_rep, ck)


def _tail_fwd(yssd, o, zatt, graw, head, x2, tgt2, wps, wpa, wout, gate_bias, norm_post, tm):
    p, ds = yssd.shape
    da = o.shape[1]
    d = x2.shape[1]
    nsub = tm // CHUNK

    def body(yssd_ref, o_ref, zatt_ref, g_ref, head_ref, *rest):
        x_refs, t_refs = rest[:nsub], rest[nsub:2 * nsub]
        (wps_ref, wpa_ref, wout_ref, gb_ref, np_ref,
         yatt_ref, mrg_ref, a_ref, b_ref, dzo_ref, dout_ref, red_ref) = rest[2 * nsub:]
        i = pl.program_id(0)

        @pl.when(i == 0)
        def _():
            red_ref[...] = jnp.zeros_like(red_ref)

        first = jnp.where(i == 0, head_ref[...], x_refs[0][...])
        h = jnp.concatenate([first] + [r[...] for r in x_refs[1:]], axis=0)
        tgt = jnp.concatenate([r[...] for r in t_refs], axis=0)
        rows = lax.broadcasted_iota(jnp.int32, (tm, 1), 0)
        valid = jnp.where((i > 0) | (rows >= CHUNK), 1.0, 0.0)
        ob = o_ref[...].astype(F32)
        za = zatt_ref[...].astype(F32)
        yatt_b = (ob * za * _sigmoid(za)).astype(BF16)
        yatt_ref[...] = yatt_b
        a = _dot(yssd_ref[...], wps_ref[...])
        b = _dot(yatt_b, wpa_ref[...])
        a_ref[...] = a.astype(BF16)
        b_ref[...] = b.astype(BF16)
        gr = g_ref[...].astype(F32) + gb_ref[...]
        mrg_b = (_sigmoid(gr[:, :d]) * a + _sigmoid(gr[:, d:]) * b).astype(BF16)
        mrg_ref[...] = mrg_b
        zo = _dot(mrg_b, wout_ref[...])
        rstd = lax.rsqrt(jnp.mean(zo * zo, axis=-1, keepdims=True) + EPS)
        zh = zo * rstd
        npw = np_ref[...]
        err = (h + zh * npw - tgt) * valid
        dout = err * (1.0 / d)
        dout_ref[...] = dout
        dzh = dout * npw
        dzo_ref[...] = (rstd * (dzh - zh * jnp.mean(dzh * zh, axis=-1, keepdims=True))).astype(BF16)
        red_ref[0:1, :] += jnp.sum(dout * zh, axis=0, keepdims=True)
        red_ref[1:2, 0:1] += jnp.sum(jnp.sum(err * err, axis=1, keepdims=True), axis=0, keepdims=True) * (0.5 / d)

    row = lambda w: pl.BlockSpec((tm, w), lambda i: (i, 0))
    once = lambda shape: pl.BlockSpec(shape, lambda i: (0,) * len(shape), pipeline_mode=pl.Buffered(1))
    subs = _x_row_specs(tm, d)
    sd = jax.ShapeDtypeStruct
    return pl.pallas_call(
        body, name="tail_fwd", grid=(p // tm,),
        in_specs=[row(ds), row(da), row(da), row(2 * d), _full((CHUNK, d))] + subs + subs
                 + [once((ds, d)), once((da, d)), once((d, d)), _full((1, 2 * d)), _full((1, d))],
        out_specs=[row(da), row(d), row(d), row(d), row(d), row(d), _full((8, d))],
        out_shape=[sd((p, da), BF16), sd((p, d), BF16), sd((p, d), BF16), sd((p, d), BF16), sd((p, d), BF16),
                   sd((p, d), F32), sd((8, d), F32)],
        compiler_params=_cparams(("arbitrary",)),
    )(yssd, o, zatt, graw, head, *([x2] * nsub), *([tgt2] * nsub), wps, wpa, wout, gate_bias, norm_post)


def _tail_bwd(dzo, a_b, b_b, graw, o, zatt, wps, wpa, wout, gate_bias, tm):
    p, d = dzo.shape
    ds, da = wps.shape[0], wpa.shape[0]

    def body(dzo_ref, a_ref, b_ref, g_ref, o_ref, zatt_ref, wps_ref, wpa_ref, wout_ref, gb_ref,
             da_ref, db_ref, dg_ref, dyssd_ref, do_ref, dzatt_ref, red_ref):
        i = pl.program_id(0)

        @pl.when(i == 0)
        def _():
            red_ref[...] = jnp.zeros_like(red_ref)

        gr = g_ref[...].astype(F32) + gb_ref[...]
        gs = _sigmoid(gr[:, :d])
        ga = _sigmoid(gr[:, d:])
        dm = _dot(dzo_ref[...], wout_ref[...], NT)
        da_b = (gs * dm).astype(BF16)
        db_b = (ga * dm).astype(BF16)
        da_ref[...] = da_b
        db_ref[...] = db_b
        dgs = dm * a_ref[...].astype(F32) * gs * (1.0 - gs)
        dga = dm * b_ref[...].astype(F32) * ga * (1.0 - ga)
        dg_ref[:, :d] = dgs.astype(BF16)
        dg_ref[:, d:] = dga.astype(BF16)
        red_ref[0:1, :d] += jnp.sum(dgs, axis=0, keepdims=True)
        red_ref[0:1, d:] += jnp.sum(dga, axis=0, keepdims=True)
        dyssd_ref[...] = _dot(da_b, wps_ref[...], NT).astype(BF16)
        dya = _dot(db_b, wpa_ref[...], NT)
        ob = o_ref[...].astype(F32)
        za = zatt_ref[...].astype(F32)
        sza = _sigmoid(za)
        do_ref[...] = (dya * za * sza).astype(BF16)
        dzatt_ref[...] = (dya * ob * sza * (1.0 + za * (1.0 - sza))).astype(BF16)

    row = lambda w: pl.BlockSpec((tm, w), lambda i: (i, 0))
    once = lambda shape: pl.BlockSpec(shape, lambda i: (0,) * len(shape), pipeline_mode=pl.Buffered(1))
    sd = jax.ShapeDtypeStruct
    return pl.pallas_call(
        body, name="tail_bwd", grid=(p // tm,),
        in_specs=[row(d), row(d), row(d), row(2 * d), row(da), row(da),
                  once((ds, d)), once((da, d)), once((d, d)), _full((1, 2 * d))],
        out_specs=[row(d), row(d), row(2 * d), row(ds), row(da), row(da), _full((8, 2 * d))],
        out_shape=[sd((p, d), BF16), sd((p, d), BF16), sd((p, 2 * d), BF16), sd((p, ds), BF16), sd((p, da), BF16),
                   sd((p, da), BF16), sd((8, 2 * d), F32)],
        compiler_params=_cparams(("arbitrary",)),
    )(dzo, a_b, b_b, graw, o, zatt, wps, wpa, wout, gate_bias)


def _adamw_math(w, g, m, v):
    m2 = ADAM_B1 * m + (1.0 - ADAM_B1) * g
    v2 = ADAM_B2 * v + (1.0 - ADAM_B2) * (g * g)
    m_hat = m2 / (1.0 - ADAM_B1 ** ADAM_STEP)
    v_hat = v2 / (1.0 - ADAM_B2 ** ADAM_STEP)
    delta = -ADAM_LR * (m_hat / (jnp.sqrt(v_hat) + ADAM_EPS) + ADAM_WD * w)
    return delta, m2, v2


def _adamw_small(params, red, name):
    names = list(params)
    n = len(names)
    extra = [params[k][3] for k in names if not isinstance(params[k][3], tuple)]

    def body(*refs):
        w_refs, m_refs, v_refs = refs[:n], refs[n:2 * n], refs[2 * n:3 * n]
        red_ref = refs[3 * n]
        g_refs = iter(refs[3 * n + 1:3 * n + 1 + len(extra)])
        outs = refs[3 * n + 1 + len(extra):]
        for i, k in enumerate(names):
            where = params[k][3]
            rows, cols = w_refs[i].shape
            if isinstance(where, tuple):
                g = red_ref[where[0]:where[0] + rows, where[1]:where[1] + cols]
            else:
                g = next(g_refs)[...]
            delta, m2, v2 = _adamw_math(w_refs[i][...], g, m_refs[i][...], v_refs[i][...])
            for o, val in zip(outs[4 * i:4 * i + 4], (g, delta, m2, v2)):
                o[...] = val

    vm = pl.BlockSpec(memory_space=pltpu.VMEM)
    ws, ms, vs = ([params[k][j] for k in names] for j in range(3))
    out = pl.pallas_call(
        body, name=name,
        out_shape=[jax.ShapeDtypeStruct(w.shape, F32) for w in ws for _ in range(4)],
        in_specs=[vm] * (3 * n + 1 + len(extra)), out_specs=[vm] * (4 * n),
    )(*ws, *ms, *vs, red, *extra)
    return {k: tuple(out[4 * i:4 * i + 4]) for i, k in enumerate(names)}


def _adamw(w, g, m, v, name, parts=False, part_row0=0):
    r, cdim = w.shape
    tr, tc, by_rows = _tiles_2d(r, cdim)
    pick = (lambda i: (i, 0)) if by_rows else (lambda i: (0, i))
    assert part_row0 % tr == 0
    gpick = (lambda i: (i + part_row0 // tr, 0)) if by_rows else (lambda i: (part_row0 // tr, i))

    def body(w_ref, g_ref, m_ref, v_ref, go_ref, d_ref, mo_ref, vo_ref):
        if parts:
            g = g_ref[0].astype(F32)
            for s in range(1, g_ref.shape[0]):
                g = g + g_ref[s].astype(F32)
        else:
            g = g_ref[...]
        delta, m2, v2 = _adamw_math(w_ref[...], g, m_ref[...], v_ref[...])
        go_ref[...] = g
        d_ref[...] = delta
        mo_ref[...] = m2
        vo_ref[...] = v2

    blk = pl.BlockSpec((tr, tc), pick)
    gspec = pl.BlockSpec((g.shape[0], tr, tc), lambda i: (0,) + gpick(i)) if parts else blk
    return pl.pallas_call(
        body, name=name, grid=((r // tr) * (cdim // tc),),
        in_specs=[blk, gspec, blk, blk], out_specs=[blk] * 4,
        out_shape=[jax.ShapeDtypeStruct((r, cdim), F32)] * 4,
        compiler_params=_cparams(("parallel",)),
    )(w, g, m, v)


def _pad_cols(a, width):
    return jnp.pad(a, ((0, 0), (0, width - a.shape[1])))


def _pack_small_shard(conv_w_sh, meta_sh, width):
    return jnp.concatenate([_pad_cols(conv_w_sh, width), jnp.zeros((4, width), F32), _pad_cols(meta_sh, width)], axis=0)


def _pack_small_rep(norm_pre, norm_post, gate_bias, ssd_norm, conv_b, misc, width):
    rows = [norm_pre, norm_post, gate_bias, ssd_norm, conv_b, misc]
    return jnp.concatenate([_pad_cols(r, width) for r in rows] + [jnp.zeros((2, width), F32)], axis=0)


def kernel(x, meta_tokens, norm_pre, w_in, conv_w, conv_b, dt_bias, a_log, d_skip, ssd_norm, fgate_bias, gate_bias, w_proj_ssd, w_proj_att, w_out, norm_post, loss_target, m_meta_tokens, m_norm_pre, m_w_in, m_conv_w, m_conv_b, m_dt_bias, m_a_log, m_d_skip, m_ssd_norm, m_fgate_bias, m_gate_bias, m_w_proj_ssd, m_w_proj_att, m_w_out, m_norm_post, v_meta_tokens, v_norm_pre, v_w_in, v_conv_w, v_conv_b, v_dt_bias, v_a_log, v_d_skip, v_ssd_norm, v_fgate_bias, v_gate_bias, v_w_proj_ssd, v_w_proj_att, v_w_out, v_norm_post):
    seq, d = x.shape[1], x.shape[2]
    p = seq + CHUNK
    hs, ha = dt_bias.shape[1], fgate_bias.shape[1]
    ds, cd = ssd_norm.shape[1], conv_b.shape[1]
    da = ha * HEAD_DIM
    nc8 = w_in.shape[2]
    cws = cd // N_DEV
    msh = d // N_DEV
    r1, r2, r3 = ds // N_DEV, da // N_DEV, d // N_DEV
    me = _dev_index(*_my_pos())
    x2, tgt2 = x[0], loss_target[0]

    win_sh = jnp.transpose(w_in[0]).astype(BF16)
    rows_sh = jnp.concatenate([w_proj_ssd[0], w_proj_att[0], w_out[0]], axis=0).astype(BF16)
    small_sh = _pack_small_shard(conv_w[0], meta_tokens, cws)
    win_all, small_all = _all_gather([win_sh, small_sh], "gather_weights")
    rows_sh, win_all = lax.optimization_barrier((rows_sh, win_all))
    rows_sems, rows_thru, rows_land, rows_token = _bcast_start(rows_sh, "gather_rows_start")
    cuts = [0, ds, ds + cd, ds + cd + hs, ds + cd + hs + da, ds + cd + hs + 2 * da, ds + cd + hs + 3 * da,
            ds + cd + hs + 4 * da, ds + cd + hs + 4 * da + ha, ds + cd + hs + 4 * da + ha + 2 * d]

    def piece_rows(r0, r1):
        parts = [win_all[s, max(r0, s * nc8) - s * nc8:min(r1, (s + 1) * nc8) - s * nc8]
                 for s in range(N_DEV) if max(r0, s * nc8) < min(r1, (s + 1) * nc8)]
        return parts[0] if len(parts) == 1 else jnp.concatenate(parts, axis=0)

    w_z, w_xbc, w_dt, w_zatt, w_q, w_k, w_v, w_f, w_g = [piece_rows(cuts[i], cuts[i + 1]) for i in range(9)]
    w_dtf = jnp.concatenate([w_dt, w_f, jnp.zeros((LANES - hs - ha, d), BF16)], axis=0)
    conv_w_full = jnp.transpose(small_all[:, 0:CONV_K, :], (1, 0, 2)).reshape(CONV_K, cd)
    meta_full = jnp.transpose(small_all[:, 8:8 + N_META, :msh], (1, 0, 2)).reshape(N_META, d)
    head = jnp.concatenate([jnp.zeros((PADN, d), F32), meta_full + rows_token[0:1, 0:1]], axis=0)

    tm = _att_block(p)
    u = _prenorm_fwd(head, x2, norm_pre, tm)
    seg_w = [w_z, w_xbc, w_zatt, w_q, w_k, w_v, w_g]
    zs, xbc, zatt, graw = [
        _mm(u, w, "nt", BF16, _tile(p, (1408, tm)), _tile(w.shape[0], (1024, 512, 256, 128)), "inproj_%d" % i)
        for i, w in enumerate([w_z, w_xbc, w_zatt, w_g])]
    qkv = _mm_side_by_side(u, [w_q, w_k, w_v], BF16, _tile(p, (1408, tm)), "inproj_qkv")
    dtf = _mm(u, w_dtf, "nt", F32, _tile(p, (1408, tm)), LANES, "inproj_dtf")

    brow = jnp.concatenate([dt_bias, fgate_bias, jnp.zeros((1, LANES - hs - ha), F32)], axis=1)
    alog_row = _pad_cols(a_log, LANES)
    dskip_l = jnp.repeat(d_skip, HEAD_DIM, axis=1)
    sel_t = (lax.broadcasted_iota(jnp.int32, (LANES, ds), 1) // HEAD_DIM
             == lax.broadcasted_iota(jnp.int32, (LANES, ds), 0)).astype(BF16)
    sel = sel_t.T
    y, yssd, hin, cf, pre = _ssd_fwd(xbc, zs, dtf, conv_w_full, conv_b, brow, alog_row, dskip_l, ssd_norm, sel_t, hs, ha)

    blk = _att_block(p)
    nkb, npair = p // blk, ha // 2
    cum = jnp.where(lax.broadcasted_iota(jnp.int32, (p, 1), 0) < PADN, -NEG, cf[:, hs:hs + ha])
    ck = jnp.transpose(cum.T.reshape(npair, 2, nkb, blk), (0, 2, 1, 3))
    ck = jnp.pad(ck, ((0, 0), (0, 0), (0, 6), (0, 0)))
    o, lse_rep = _attn_fwd(qkv, ck, blk)

    rows_all = _bcast_wait(rows_sems, rows_thru, rows_land, lse_rep, "gather_rows_wait")
    wps = rows_all[:, :r1].reshape(ds, d)
    wpa = rows_all[:, r1:r1 + r2].reshape(da, d)
    wout = rows_all[:, r1 + r2:].reshape(d, d)

    yatt, mrg, a_b, b_b, dzo, dout, red_fwd = _tail_fwd(
        yssd, o, zatt, graw, head, x2, tgt2, wps, wpa, wout, gate_bias, norm_post, tm)
    da_, db_, dgraw, dyssd, d_o, dzatt, red_bwd = _tail_bwd(dzo, a_b, b_b, graw, o, zatt, wps, wpa, wout, gate_bias, tm)

    tw = _tile(d, (512, 256, 128))
    g_wout = _mm(mrg, dzo, "tn", BF16, tw, d, "wgrad_out")
    g_wps = _mm(yssd, da_, "tn", BF16, _tile(ds, (512, 256, 128)), d, "wgrad_ps")
    g_wpa = _mm(yatt, db_, "tn", BF16, _tile(da, (512, 256, 128)), d, "wgrad_pa")

    core = lax.axis_index("c").astype(jnp.int32).reshape(1)
    chip = me // 2
    grows_parts = jnp.concatenate([g_wps.reshape(N_DEV, r1, d), g_wpa.reshape(N_DEV, r2, d),
                                   g_wout.reshape(N_DEV, r3, d)], axis=1)
    (sib_rows,) = _exchange_sibling([grows_parts], "scatter_rows_sibling")
    chip_rows = _pair_add(grows_parts, sib_rows, core, "pair_add_rows")
    r_sems, r_thru, r_lands, r_token = _exchange_chips_start([chip_rows], "scatter_rows_start")

    dk, dv, dq, dcs, rsum = _attn_bwd(qkv, o, d_o, lse_rep, ck + r_token[0:1, 0:1], blk)
    dcum = (rsum - dcs)[:, 0:2, :].reshape(ha, p).T
    dcf = jnp.pad(dcum, ((0, 0), (hs, LANES - hs - ha)))
    dxbc, dzs, ddtf, gcw, gcb, gnrm, gsm = _ssd_bwd(
        dyssd, y, zs, xbc, pre, dtf, hin, dcf, conv_w_full, brow, alog_row, dskip_l, ssd_norm, sel_t, sel, hs, ha)
    ddtf_b = ddtf.astype(BF16)

    dsegs = [dzs, dxbc, dzatt, dq, dk, dv, dgraw, ddtf_b]
    gsegs = [_mm(dsg, u, "tn", BF16, _tile(dsg.shape[1], (512, 256, 128)), d, "wgrad_in_%d" % i)
             for i, dsg in enumerate(dsegs)]
    g_z, g_xbc, g_zatt, g_q, g_k, g_v, g_g, g_dtf = gsegs
    gw_full = jnp.concatenate([g_z, g_xbc, g_dtf[:hs], g_zatt, g_q, g_k, g_v, g_dtf[hs:hs + ha], g_g], axis=0)
    gwin_parts = gw_full.reshape(N_DEV, nc8, d)

    (sib_win,) = _exchange_sibling([gwin_parts], "scatter_grads_sibling")
    chip_win = _pair_add(gwin_parts, sib_win, core, "pair_add_w_in")
    sems, thru, lands, token = _exchange_chips_start([chip_win], "scatter_grads_start")
    dsegs_after = dsegs[:-1] + [ddtf_b + token[0:1, 0:1].astype(BF16)]
    gx, ghead, gnp = _dgrad_prenorm(dsegs_after, seg_w + [w_dtf], head, x2, norm_pre, dout, tm, "dgrad_in")
    own_slot = lambda got, sent: lax.dynamic_update_slice_in_dim(
        got, lax.dynamic_slice_in_dim(sent, chip, 1, axis=0), chip, axis=0)
    (sent,), (got,) = _exchange_chips_wait(sems, thru, lands, gnp, "scatter_grads_wait")
    recv_win = own_slot(got, sent)
    (r_sent,), (r_got,) = _exchange_chips_wait(r_sems, r_thru, r_lands, gnp, "scatter_rows_wait")
    recv_rows = own_slot(r_got, r_sent)
    gmisc = jnp.concatenate([gsm[0:1], gsm[1:2], gsm[2:3], _pad_cols(red_fwd[1:2, 0:1], LANES)], axis=1)
    small_g = jnp.concatenate([
        _pack_small_rep(gnp[0:1], red_fwd[0:1], red_bwd[0:1], gnrm[0:1], gcb[0:1], gmisc, cd),
        _pad_cols(gcw[0:CONV_K], cd), jnp.zeros((4, cd), F32), _pad_cols(ghead[PADN:], cd)], axis=0)
    sg_sems, sg_thru, sg_land, sg_token = _bcast_start(small_g, "reduce_small_start")

    upd_in = _adamw(jnp.transpose(w_in[0]) + sg_token[0:1, 0:1], recv_win, jnp.transpose(m_w_in[0]),
                    jnp.transpose(v_w_in[0]), "adamw_w_in", parts=True)
    upd_ps = _adamw(w_proj_ssd[0] + sg_token[0:1, 0:1], recv_rows, m_w_proj_ssd[0], v_w_proj_ssd[0],
                    "adamw_w_proj_ssd", parts=True, part_row0=0)
    upd_pa = _adamw(w_proj_att[0], recv_rows, m_w_proj_att[0], v_w_proj_att[0], "adamw_w_proj_att", parts=True,
                    part_row0=r1)
    upd_out = _adamw(w_out[0], recv_rows, m_w_out[0], v_w_out[0], "adamw_w_out", parts=True, part_row0=r1 + r2)
    all_done = upd_in[1][0:8, 0:LANES] + upd_ps[1][0:8, 0:LANES] + upd_pa[1][0:8, 0:LANES] + upd_out[1][0:8, 0:LANES]
    red = _sum_slots(_bcast_wait(sg_sems, sg_thru, sg_land, all_done, "reduce_small_wait"), "reduce_small_sum")
    loss = red[5, 3 * LANES]
    g_conv_w = lax.dynamic_slice_in_dim(red[8:8 + CONV_K], me * cws, cws, axis=1)
    g_meta = lax.dynamic_slice_in_dim(red[16:16 + N_META, :d], me * msh, msh, axis=1)
    small = {
        "meta_tokens": (meta_tokens, m_meta_tokens, v_meta_tokens, g_meta),
        "norm_pre": (norm_pre, m_norm_pre, v_norm_pre, (0, 0)),
        "conv_w": (conv_w[0], m_conv_w[0], v_conv_w[0], g_conv_w),
        "conv_b": (conv_b, m_conv_b, v_conv_b, (4, 0)),
        "dt_bias": (dt_bias, m_dt_bias, v_dt_bias, (5, 0)),
        "a_log": (a_log, m_a_log, v_a_log, (5, LANES)),
        "d_skip": (d_skip, m_d_skip, v_d_skip, (5, 2 * LANES)),
        "ssd_norm": (ssd_norm, m_ssd_norm, v_ssd_norm, (3, 0)),
        "fgate_bias": (fgate_bias, m_fgate_bias, v_fgate_bias, (5, hs)),
        "gate_bias": (gate_bias, m_gate_bias, v_gate_bias, (2, 0)),
        "norm_post": (norm_post, m_norm_post, v_norm_post, (1, 0)),
    }
    upd_small = _adamw_small(small, red, "adamw_small")

    def leaves(i):
        sm = {k: v[i] for k, v in upd_small.items()}
        return [sm["meta_tokens"], sm["norm_pre"], jnp.transpose(upd_in[i])[None], sm["conv_w"][None], sm["conv_b"],
                sm["dt_bias"], sm["a_log"], sm["d_skip"], sm["ssd_norm"], sm["fgate_bias"], sm["gate_bias"],
                upd_ps[i][None], upd_pa[i][None], upd_out[i][None], sm["norm_post"]]

    return tuple([loss, gx[None]] + leaves(0) + leaves(1) + leaves(2) + leaves(3))
```

```python
import functools
import math

import jax
import jax.numpy as jnp
from jax import lax
from jax.experimental import pallas as pl
from jax.experimental.pallas import tpu as pltpu

F32 = jnp.float32
BF16 = jnp.bfloat16

N_DEV = 8
N_META = 16
CHUNK = 128
PADN = CHUNK - N_META
HEAD_DIM = 64
SSD_GROUPS = 4
CONV_K = 4
EPS = 1e-6
NEG = -1e30
LANES = 128
HALO = 16

ADAM_LR = 0.001
ADAM_B1 = 0.9
ADAM_B2 = 0.999
ADAM_EPS = 1e-08
ADAM_WD = 0.01
ADAM_STEP = 10

VMEM_LIMIT = 56 * 1024 * 1024

NN = (((1,), (0,)), ((), ()))
NT = (((1,), (1,)), ((), ()))
TN = (((0,), (0,)), ((), ()))
MESH = pl.DeviceIdType.MESH


def _dot(a, b, dims=NN):
    return lax.dot_general(a, b, dims, preferred_element_type=F32)


def _split2(x):
    hi = x.astype(BF16)
    lo = (x - hi.astype(F32)).astype(BF16)
    return hi, lo


def _dot_sel(x, sel):
    hi, lo = _split2(x)
    return _dot(hi, sel) + _dot(lo, sel)


def _dot_tri(tri, x):
    h1 = x.astype(BF16)
    r1 = x - h1.astype(F32)
    h2 = r1.astype(BF16)
    h3 = (r1 - h2.astype(F32)).astype(BF16)
    return _dot(tri, h1) + _dot(tri, h2) + _dot(tri, h3)


def _sigmoid(x):
    return 0.5 * jnp.tanh(0.5 * x) + 0.5


def _softplus(x):
    return jnp.maximum(x, 0.0) + jnp.log(1.0 + jnp.exp(-jnp.abs(x)))


def _cparams(sem=None, vmem=VMEM_LIMIT):
    kw = {"vmem_limit_bytes": vmem}
    if sem is not None:
        kw["dimension_semantics"] = sem
    return pltpu.CompilerParams(**kw)


def _full(shape):
    nd = len(shape)
    return pl.BlockSpec(shape, lambda *_: (0,) * nd)


def _att_block(p):
    return 384 if p % 384 == 0 else CHUNK


def _my_pos():
    return lax.axis_index("x"), lax.axis_index("y"), lax.axis_index("c")


def _dev_index(x, y, c):
    return 4 * x + 2 * y + c


FLIPS = [(fx, fy, fc) for fx in (0, 1) for fy in (0, 1) for fc in (0, 1)][1:]


def _flip(pos, f):
    return tuple((1 - p) if fi else p for p, fi in zip(pos, f))


def _all_gather(bufs, name):
    nb = len(bufs)

    def body(*refs):
        ins, outs = refs[:nb], refs[nb:2 * nb]
        send_sems, recv_sems, local_sems = refs[2 * nb:]
        x, y, c = _my_pos()
        me = _dev_index(x, y, c)
        sibling = (x, y, 1 - c)
        near = [(1 - x, y), (x, 1 - y)]
        far = (1 - x, 1 - y)
        relay_from = (c * (1 - x) + (1 - c) * x, c * y + (1 - c) * (1 - y))
        relay_to = (c * x + (1 - c) * (1 - x), c * (1 - y) + (1 - c) * y)

        def copy(b, k, block_idx, to, src=None):
            dst = outs[b].at[block_idx]
            return pltpu.make_async_remote_copy(
                src_ref=dst if src is None else src, dst_ref=dst,
                send_sem=send_sems.at[b, k], recv_sem=recv_sems.at[b, k],
                device_id=to, device_id_type=MESH)

        started = []
        for b in range(nb):
            mine = pltpu.make_async_copy(ins[b], outs[b].at[me], local_sems.at[b])
            mine.start()
            started.append(mine)
        sent = []
        for b in range(nb):
            sent.append(copy(b, 0, me, sibling, src=ins[b]))
            for j, chip in enumerate(near):
                sent.append(copy(b, 1 + j, me, (chip[0], chip[1], c), src=ins[b]))
        for cp in sent:
            cp.start()
        for j, chip in enumerate(near):
            blk = _dev_index(chip[0], chip[1], c)
            for b in range(nb):
                copy(b, 1 + j, blk, (x, y, c)).wait_recv()
                sent.append(copy(b, 4 + j, blk, sibling))
                sent[-1].start()
        for b in range(nb):
            sent.append(copy(b, 3, _dev_index(relay_from[0], relay_from[1], c), (relay_to[0], relay_to[1], c)))
            sent[-1].start()
        blk = _dev_index(far[0], far[1], c)
        for b in range(nb):
            copy(b, 3, blk, (x, y, c)).wait_recv()
            sent.append(copy(b, 6, blk, sibling))
            sent[-1].start()
        for b in range(nb):
            copy(b, 0, _dev_index(x, y, 1 - c), (x, y, c)).wait_recv()
        for j, chip in enumerate(near + [far]):
            blk = _dev_index(chip[0], chip[1], 1 - c)
            for b in range(nb):
                copy(b, 4 + j, blk, (x, y, c)).wait_recv()
        for cp in sent:
            cp.wait_send()
        for mine in started:
            mine.wait()

    any_spec = pl.BlockSpec(memory_space=pl.ANY)
    return pl.pallas_call(
        body, name=name,
        out_shape=[jax.ShapeDtypeStruct((N_DEV,) + b.shape, b.dtype) for b in bufs],
        in_specs=[any_spec] * nb, out_specs=[any_spec] * nb,
        scratch_shapes=[pltpu.SemaphoreType.DMA((nb, 7)), pltpu.SemaphoreType.DMA((nb, 7)),
                        pltpu.SemaphoreType.DMA((nb,))],
    )(*bufs)


N_CHIP = 4
CHIP_FLIPS = [(1, 0), (0, 1), (1, 1)]


def _exchange_sibling(bufs, name):
    nb = len(bufs)

    def body(*refs):
        ins, outs = refs[:nb], refs[nb:2 * nb]
        send_sems, recv_sems = refs[2 * nb:]
        x, y, c = _my_pos()

        def copy(b, k):
            return pltpu.make_async_remote_copy(
                src_ref=ins[b].at[2 * k + (1 - c)], dst_ref=outs[b].at[k],
                send_sem=send_sems.at[b, k], recv_sem=recv_sems.at[b, k],
                device_id=(x, y, 1 - c), device_id_type=MESH)

        cps = [copy(b, k) for b in range(nb) for k in range(N_CHIP)]
        for cp in cps:
            cp.start()
        for cp in cps:
            cp.wait()

    any_spec = pl.BlockSpec(memory_space=pl.ANY)
    return pl.pallas_call(
        body, name=name,
        out_shape=[jax.ShapeDtypeStruct((N_CHIP,) + b.shape[1:], b.dtype) for b in bufs],
        in_specs=[any_spec] * nb, out_specs=[any_spec] * nb,
        scratch_shapes=[pltpu.SemaphoreType.DMA((nb, N_CHIP)), pltpu.SemaphoreType.DMA((nb, N_CHIP))],
    )(*bufs)


def _pair_add(mine, recv, core, name):
    _, r, cdim = mine.shape
    tr, tc = r, cdim
    pick = lambda i: (i, 0)

    def body(core_ref, a_ref, b_ref, o_ref):
        o_ref[0] = (a_ref[0].astype(F32) + b_ref[0].astype(F32)).astype(o_ref.dtype)

    return pl.pallas_call(
        body, name=name,
        grid_spec=pltpu.PrefetchScalarGridSpec(
            num_scalar_prefetch=1, grid=(N_CHIP, (r // tr) * (cdim // tc)),
            in_specs=[pl.BlockSpec((1, tr, tc), lambda k, i, core_ref: (2 * k + core_ref[0],) + pick(i)),
                      pl.BlockSpec((1, tr, tc), lambda k, i, core_ref: (k,) + pick(i))],
            out_specs=pl.BlockSpec((1, tr, tc), lambda k, i, core_ref: (k,) + pick(i))),
        out_shape=jax.ShapeDtypeStruct((N_CHIP, r, cdim), mine.dtype),
        compiler_params=_cparams(("parallel", "parallel")),
    )(core, mine, recv)


def _chip_peer(x, y, f):
    return ((1 - x) if f[0] else x), ((1 - y) if f[1] else y)


def _exchange_chips_start(bufs, name):
    nb = len(bufs)
    nsem = 2 * 3 * nb

    def body(*refs):
        ins, lands = refs[:nb], refs[nb:2 * nb]
        sems = refs[2 * nb:2 * nb + nsem]
        token = refs[-1]
        x, y, c = _my_pos()
        for b in range(nb):
            for j, f in enumerate(CHIP_FLIPS):
                px, py = _chip_peer(x, y, f)
                pltpu.make_async_remote_copy(
                    src_ref=ins[b].at[2 * px + py], dst_ref=lands[b].at[2 * x + y],
                    send_sem=sems[2 * (3 * b + j)], recv_sem=sems[2 * (3 * b + j) + 1],
                    device_id=(px, py, c), device_id_type=MESH).start()
        token[...] = jnp.zeros_like(token)

    hbm = pl.BlockSpec(memory_space=pltpu.HBM)
    sem = pl.BlockSpec(memory_space=pltpu.SEMAPHORE)
    out = pl.pallas_call(
        body, name=name,
        out_shape=(*([pltpu.SemaphoreType.DMA(())] * nsem),
                   *[pltpu.HBM(b.shape, b.dtype) for b in bufs], *[pltpu.HBM(b.shape, b.dtype) for b in bufs],
                   jax.ShapeDtypeStruct((8, LANES), F32)),
        in_specs=[hbm] * (2 * nb),
        out_specs=(*([sem] * nsem), *([hbm] * (2 * nb)), pl.BlockSpec(memory_space=pltpu.VMEM)),
        input_output_aliases={i: nsem + i for i in range(2 * nb)},
        compiler_params=pltpu.CompilerParams(has_side_effects=pltpu.SideEffectType.DATAFLOW_SIDE_EFFECTING),
    )(*[pltpu.with_memory_space_constraint(b, pltpu.HBM) for b in bufs],
      *[pltpu.with_memory_space_constraint(lax.empty(b.shape, b.dtype), pltpu.HBM) for b in bufs])
    return out[:nsem], out[nsem:nsem + nb], out[nsem + nb:nsem + 2 * nb], out[-1]


def _exchange_chips_wait(sems, thru, lands, after, name):
    nb = len(thru)
    nsem = len(sems)

    def body(*refs):
        ins, lnd = refs[:nb], refs[nb:2 * nb]
        sem_refs = refs[2 * nb:2 * nb + nsem]
        x, y, c = _my_pos()
        for b in range(nb):
            for j, f in enumerate(CHIP_FLIPS):
                px, py = _chip_peer(x, y, f)
                cp = pltpu.make_async_remote_copy(
                    src_ref=ins[b].at[2 * px + py], dst_ref=lnd[b].at[2 * px + py],
                    send_sem=sem_refs[2 * (3 * b + j)], recv_sem=sem_refs[2 * (3 * b + j) + 1],
                    device_id=(px, py, c), device_id_type=MESH)
                cp.wait_send()
                cp.wait_recv()

    hbm = pl.BlockSpec(memory_space=pltpu.HBM)
    sem = pl.BlockSpec(memory_space=pltpu.SEMAPHORE)
    out = pl.pallas_call(
        body, name=name,
        out_shape=tuple([pltpu.HBM(b.shape, b.dtype) for b in thru] + [pltpu.HBM(b.shape, b.dtype) for b in lands]),
        in_specs=[hbm] * (2 * nb) + [sem] * nsem + [pl.BlockSpec(memory_space=pl.ANY)],
        out_specs=tuple([hbm] * (2 * nb)),
        input_output_aliases={i: i for i in range(2 * nb)},
        compiler_params=pltpu.CompilerParams(has_side_effects=pltpu.SideEffectType.DATAFLOW_SIDE_EFFECTING),
    )(*thru, *lands, *sems, after)
    return out[:nb], out[nb:]


def _bcast_start(buf, name):
    nsem = 2 * len(FLIPS)

    def body(src, land, *rest):
        sems, token = rest[:nsem], rest[-1]
        pos = _my_pos()
        for k, f in enumerate(FLIPS):
            pltpu.make_async_remote_copy(
                src_ref=src, dst_ref=land.at[_dev_index(*pos)], send_sem=sems[2 * k], recv_sem=sems[2 * k + 1],
                device_id=_flip(pos, f), device_id_type=MESH).start()
        token[...] = jnp.zeros_like(token)

    hbm = pl.BlockSpec(memory_space=pltpu.HBM)
    sem = pl.BlockSpec(memory_space=pltpu.SEMAPHORE)
    land_shape = (N_DEV,) + buf.shape
    out = pl.pallas_call(
        body, name=name,
        out_shape=(*([pltpu.SemaphoreType.DMA(())] * nsem), pltpu.HBM(buf.shape, buf.dtype),
                   pltpu.HBM(land_shape, buf.dtype), jax.ShapeDtypeStruct((8, LANES), F32)),
        in_specs=[hbm, hbm],
        out_specs=(*([sem] * nsem), hbm, hbm, pl.BlockSpec(memory_space=pltpu.VMEM)),
        input_output_aliases={0: nsem, 1: nsem + 1},
        compiler_params=pltpu.CompilerParams(has_side_effects=pltpu.SideEffectType.DATAFLOW_SIDE_EFFECTING),
    )(pltpu.with_memory_space_constraint(buf, pltpu.HBM),
      pltpu.with_memory_space_constraint(lax.empty(land_shape, buf.dtype), pltpu.HBM))
    return out[:nsem], out[nsem], out[nsem + 1], out[-1]


def _bcast_wait(sems, thru, land, after, name):
    nsem = len(sems)

    def body(src, lnd, *rest):
        sem_refs = rest[:nsem]
        pos = _my_pos()
        for k, f in enumerate(FLIPS):
            peer = _flip(pos, f)
            cp = pltpu.make_async_remote_copy(
                src_ref=src, dst_ref=lnd.at[_dev_index(*peer)], send_sem=sem_refs[2 * k],
                recv_sem=sem_refs[2 * k + 1], device_id=peer, device_id_type=MESH)
            cp.wait_send()
            cp.wait_recv()

    hbm = pl.BlockSpec(memory_space=pltpu.HBM)
    sem = pl.BlockSpec(memory_space=pltpu.SEMAPHORE)
    sent, got = pl.pallas_call(
        body, name=name,
        out_shape=(pltpu.HBM(thru.shape, thru.dtype), pltpu.HBM(land.shape, land.dtype)),
        in_specs=[hbm, hbm] + [sem] * nsem + [pl.BlockSpec(memory_space=pl.ANY)],
        out_specs=(hbm, hbm), input_output_aliases={0: 0, 1: 1},
        compiler_params=pltpu.CompilerParams(has_side_effects=pltpu.SideEffectType.DATAFLOW_SIDE_EFFECTING),
    )(thru, land, *sems, after)
    return lax.dynamic_update_slice_in_dim(got, sent[None], _dev_index(*_my_pos()), axis=0)


def _sum_slots(v, name):
    _, r, cdim = v.shape

    def body(v_ref, o_ref):
        acc = v_ref[0]
        for s in range(1, N_DEV):
            acc = acc + v_ref[s]
        o_ref[...] = acc

    return pl.pallas_call(
        body, name=name, out_shape=jax.ShapeDtypeStruct((r, cdim), F32),
        in_specs=[_full((N_DEV, r, cdim))], out_specs=_full((r, cdim)), grid=(1,),
        compiler_params=_cparams(("arbitrary",)),
    )(v)


def _mm(a, b, dims, out_dtype, tm, tn, name):
    if dims == "nn":
        (m, k), (_, n) = a.shape, b.shape
        a_spec = pl.BlockSpec((tm, k), lambda j, i: (i, 0))
        b_spec = pl.BlockSpec((k, tn), lambda j, i: (0, j))
        dn = NN
    elif dims == "nt":
        (m, k), (n, _) = a.shape, b.shape
        a_spec = pl.BlockSpec((tm, k), lambda j, i: (i, 0))
        b_spec = pl.BlockSpec((tn, k), lambda j, i: (j, 0))
        dn = NT
    else:
        (k, m), (_, n) = a.shape, b.shape
        a_spec = pl.BlockSpec((k, tm), lambda j, i: (0, i))
        b_spec = pl.BlockSpec((k, tn), lambda j, i: (0, j))
        dn = TN
    assert m % tm == 0 and n % tn == 0, (m, tm, n, tn)

    def body(a_ref, b_ref, o_ref):
        o_ref[...] = _dot(a_ref[...], b_ref[...], dn).astype(o_ref.dtype)

    return pl.pallas_call(
        body, name=name, grid=(n // tn, m // tm),
        in_specs=[a_spec, b_spec], out_specs=pl.BlockSpec((tm, tn), lambda j, i: (i, j)),
        out_shape=jax.ShapeDtypeStruct((m, n), out_dtype),
        compiler_params=_cparams(("parallel", "parallel")),
    )(a, b)


def _tiles_2d(r, cdim):
    if r % CHUNK == 0:
        return CHUNK, cdim, True
    return r, _tile(cdim, (256, 128)), False


def _dgrad_prenorm(a_list, b_list, head, x2, w, dout, tm, name):
    n_op = len(a_list)
    m, d = a_list[0].shape[0], b_list[0].shape[1]
    subs = _x_row_specs(tm, d)
    last = m // tm - 1
    rest = tm - CHUNK

    def body(*refs):
        a_refs, b_refs = refs[:n_op], refs[n_op:2 * n_op]
        head_ref = refs[2 * n_op]
        x_refs = refs[2 * n_op + 1:2 * n_op + 1 + len(subs)]
        w_ref, dout_ref, gx_ref, ghead_ref, gw_ref, dh_buf, sem = refs[2 * n_op + 1 + len(subs):]
        i = pl.program_id(0)

        def first_copy():
            return pltpu.make_async_copy(dh_buf.at[pl.ds(CHUNK, rest)], gx_ref.at[pl.ds(0, rest)], sem)

        def later_copy(step):
            return pltpu.make_async_copy(dh_buf, gx_ref.at[pl.ds(pl.multiple_of(step * tm - CHUNK, CHUNK), tm)], sem)

        @pl.when(i == 0)
        def _():
            gw_ref[...] = jnp.zeros_like(gw_ref)

        du = _dot(a_refs[0][...], b_refs[0][...])
        for k in range(1, n_op):
            du = du + _dot(a_refs[k][...], b_refs[k][...])
        first = jnp.where(i == 0, head_ref[...], x_refs[0][...])
        h = jnp.concatenate([first] + [r[...] for r in x_refs[1:]], axis=0)
        rstd = lax.rsqrt(jnp.mean(h * h, axis=-1, keepdims=True) + EPS)
        xhat = h * rstd
        dxh = du * w_ref[...]
        dh = rstd * (dxh - xhat * jnp.mean(dxh * xhat, axis=-1, keepdims=True)) + dout_ref[...]
        gw_ref[0:1, :] += jnp.sum(du * xhat, axis=0, keepdims=True)

        if rest and last >= 1:
            @pl.when(i == 1)
            def _():
                first_copy().wait()

        @pl.when(i >= (2 if rest else 1))
        def _():
            later_copy(i - 1).wait()

        dh_buf[...] = dh

        @pl.when(i == 0)
        def _():
            ghead_ref[...] = dh_buf[0:CHUNK, :]
            if rest:
                first_copy().start()
                if last == 0:
                    first_copy().wait()

        @pl.when(i >= 1)
        def _():
            later_copy(i).start()

        if last >= 1:
            @pl.when(i == last)
            def _():
                later_copy(i).wait()

    once = lambda b: pl.BlockSpec(b.shape, lambda i: (0, 0), pipeline_mode=pl.Buffered(1))
    row = lambda width: pl.BlockSpec((tm, width), lambda i: (i, 0))
    return pl.pallas_call(
        body, name=name, grid=(m // tm,),
        in_specs=([row(a.shape[1]) for a in a_list] + [once(b) for b in b_list]
                  + [_full((CHUNK, d))] + subs + [_full((1, d)), row(d)]),
        out_specs=[pl.BlockSpec(memory_space=pl.ANY), _full((CHUNK, d)), _full((8, d))],
        out_shape=[jax.ShapeDtypeStruct((m - CHUNK, d), F32), jax.ShapeDtypeStruct((CHUNK, d), F32),
                   jax.ShapeDtypeStruct((8, d), F32)],
        scratch_shapes=[pltpu.VMEM((tm, d), F32), pltpu.SemaphoreType.DMA],
        compiler_params=_cparams(("arbitrary",)),
    )(*a_list, *b_list, head, *([x2] * len(subs)), w, dout)


def _tile(n, prefs):
    for t in prefs:
        if n % t == 0:
            return t
    return n


def _rows3(i):
    return jnp.maximum(3 * i - 1, 0), 3 * i, 3 * i + 1


def _x_row_specs(tm, d):
    if tm == CHUNK:
        return [pl.BlockSpec((CHUNK, d), lambda i: (jnp.maximum(i - 1, 0), 0))]
    return [pl.BlockSpec((CHUNK, d), functools.partial(lambda i, k: (_rows3(i)[k], 0), k=k)) for k in range(3)]


def _prenorm_fwd(head, x2, w, tm):
    p, d = x2.shape[0] + CHUNK, x2.shape[1]
    subs = _x_row_specs(tm, d)

    def body(head_ref, *rest):
        x_refs, (w_ref, u_ref) = rest[:len(subs)], rest[len(subs):]
        i = pl.program_id(0)
        first = jnp.where(i == 0, head_ref[...], x_refs[0][...])
        h = jnp.concatenate([first] + [r[...] for r in x_refs[1:]], axis=0)
        ms = jnp.mean(h * h, axis=-1, keepdims=True)
        u_ref[...] = (h * lax.rsqrt(ms + EPS) * w_ref[...]).astype(BF16)

    return pl.pallas_call(
        body, name="prenorm_fwd", grid=(p // tm,),
        in_specs=[_full((CHUNK, d))] + subs + [_full((1, d))],
        out_specs=pl.BlockSpec((tm, d), lambda i: (i, 0)),
        out_shape=jax.ShapeDtypeStruct((p, d), BF16),
        compiler_params=_cparams(("arbitrary",)),
    )(head, *([x2] * len(subs)), w)


def _conv_pre(ext_ref, cw_ref, cb_ref):
    pre = cb_ref[...] + cw_ref[CONV_K - 1:CONV_K, :] * ext_ref[8:8 + CHUNK, :]
    for j in range(1, CONV_K):
        pre = pre + cw_ref[CONV_K - 1 - j:CONV_K - j, :] * ext_ref[8 - j:8 - j + CHUNK, :]
    return pre


def _ssd_scalars(dtf_ref, brow_ref, alog_ref, rowmask, hs, ha, tri):
    lane = lax.broadcasted_iota(jnp.int32, (1, LANES), 1)
    is_dt = lane < hs
    is_f = (lane >= hs) & (lane < hs + ha)
    dtr = dtf_ref[...] + brow_ref[...]
    sp = _softplus(dtr)
    dt = jnp.where(is_dt, sp, 0.0) * rowmask
    logf = jnp.where(is_f, jnp.minimum(dtr, 0.0) - jnp.log(1.0 + jnp.exp(-jnp.abs(dtr))), 0.0) * rowmask
    a_row = jnp.where(is_dt, -jnp.exp(alog_ref[...]), 0.0)
    run = _dot_tri(tri, dt * a_row + logf)
    return dtr, dt, a_row, run, is_dt, is_f


def _tri_mats():
    r = lax.broadcasted_iota(jnp.int32, (CHUNK, CHUNK), 0)
    c = lax.broadcasted_iota(jnp.int32, (CHUNK, CHUNK), 1)
    return r, c


def _ssd_fwd(xbc, z, dtf, conv_w, conv_b, brow, alog, dskip_l, ssd_norm, sel_t, hs, ha):
    p, cd = xbc.shape
    ds = z.shape[1]
    ns = (cd - ds) // (2 * SSD_GROUPS)
    gw = ds // SSD_GROUPS
    nch = p // CHUNK
    hpg = hs // SSD_GROUPS

    def body(xbc_ref, halo_ref, z_ref, dtf_ref, cw_ref, cb_ref, brow_ref, alog_ref, dsk_ref, nrm_ref, selt_ref,
             y_ref, yssd_ref, hin_ref, cf_ref, pre_ref, st_ref, carry_ref, yacc_ref, xc_s, ex_s, xdtb_s, xwb_s, ext_s):
        c = pl.program_id(0)

        @pl.when(c == 0)
        def _():
            st_ref[...] = jnp.zeros_like(st_ref)
            carry_ref[...] = jnp.zeros_like(carry_ref)

        rows = lax.broadcasted_iota(jnp.int32, (CHUNK, 1), 0)
        rowmask = jnp.where((rows >= PADN) | (c > 0), 1.0, 0.0)
        ri, ci = _tri_mats()
        causal = ri >= ci
        tri = jnp.where(causal, 1.0, 0.0).astype(BF16)

        ext_s[0:8, :] = halo_ref[...].astype(F32)[HALO - 8:, :] * jnp.where(c > 0, 1.0, 0.0)
        ext_s[8:, :] = xbc_ref[...].astype(F32)
        pre = _conv_pre(ext_s, cw_ref, cb_ref)
        pre_ref[...] = pre.astype(BF16)
        xc_s[...] = pre * _sigmoid(pre) * rowmask

        dtr, dt, a_row, run, is_dt, is_f = _ssd_scalars(dtf_ref, brow_ref, alog_ref, rowmask, hs, ha, tri)
        cf = run + carry_ref[...]
        cf_ref[...] = cf
        carry_ref[...] = jnp.where(is_f, cf[CHUNK - 1:CHUNK, :], 0.0)
        cs = jnp.where(is_dt, run, 0.0)
        cl = cs[CHUNK - 1:CHUNK, :]
        selt = selt_ref[...]
        ex_s[...] = _dot_sel(jnp.exp(cs), selt)
        cdec_x = _dot_sel(jnp.broadcast_to(jnp.exp(cl), (8, LANES)), selt)[0:1, :]
        cs_t = cs.T
        xdt = xc_s[:, :ds] * _dot_sel(dt, selt)
        xdtb_s[...] = xdt.astype(BF16)
        xwb_s[...] = (xdt * _dot_sel(jnp.exp(cl - cs), selt)).astype(BF16)

        lane = lax.broadcasted_iota(jnp.int32, (1, LANES), 1)
        half0 = lane < HEAD_DIM
        for g in range(SSD_GROUPS):
            bg = xc_s[:, ds + g * ns: ds + (g + 1) * ns].astype(BF16)
            cg = xc_s[:, ds + SSD_GROUPS * ns + g * ns: ds + SSD_GROUPS * ns + (g + 1) * ns].astype(BF16)
            gm = _dot(cg, bg, NT)
            gs = slice(g * gw, (g + 1) * gw)
            stg = st_ref[:, gs]
            stg_b = stg.astype(BF16)
            hin_ref[0, :, gs] = stg_b
            yoff = _dot(cg, stg_b) * ex_s[:, gs]
            for pr in range(gw // LANES):
                sl = slice(g * gw + pr * LANES, g * gw + (pr + 1) * LANES)
                xp = xdtb_s[:, sl]
                yd = jnp.zeros((CHUNK, LANES), F32)
                for j in range(2):
                    h = g * hpg + 2 * pr + j
                    seg = cs[:, h:h + 1] - cs_t[h:h + 1, :]
                    m = jnp.where(causal, gm * jnp.exp(jnp.minimum(seg, 0.0)), 0.0).astype(BF16)
                    sel = half0 if j == 0 else jnp.logical_not(half0)
                    yd = yd + _dot(m, jnp.where(sel, xp, jnp.zeros_like(xp)))
                yacc_ref[:, sl] = yd + yoff[:, pr * LANES:(pr + 1) * LANES] + dsk_ref[:, sl] * xc_s[:, sl]
            st_ref[:, gs] = stg * cdec_x[:, gs] + _dot(bg, xwb_s[:, gs], TN)

        y = yacc_ref[...]
        y_ref[...] = y.astype(BF16)
        zf = z_ref[...].astype(F32)
        u = y * zf * _sigmoid(zf)
        for g in range(SSD_GROUPS):
            gs = slice(g * gw, (g + 1) * gw)
            ug = u[:, gs]
            ms = jnp.mean(ug * ug, axis=-1, keepdims=True)
            yssd_ref[:, gs] = (ug * lax.rsqrt(ms + EPS) * nrm_ref[:, gs]).astype(BF16)

    rb = CHUNK // HALO
    return pl.pallas_call(
        body, name="ssd_fwd", grid=(nch,),
        in_specs=[pl.BlockSpec((CHUNK, cd), lambda c: (c, 0)),
                  pl.BlockSpec((HALO, cd), lambda c: (jnp.maximum(c * rb - 1, 0), 0)),
                  pl.BlockSpec((CHUNK, ds), lambda c: (c, 0)),
                  pl.BlockSpec((CHUNK, LANES), lambda c: (c, 0)),
                  _full((CONV_K, cd)), _full((1, cd)), _full((1, LANES)), _full((1, LANES)),
                  _full((1, ds)), _full((1, ds)), _full((LANES, ds))],
        out_specs=[pl.BlockSpec((CHUNK, ds), lambda c: (c, 0)), pl.BlockSpec((CHUNK, ds), lambda c: (c, 0)),
                   pl.BlockSpec((1, ns, ds), lambda c: (c, 0, 0)), pl.BlockSpec((CHUNK, LANES), lambda c: (c, 0)),
                   pl.BlockSpec((CHUNK, cd), lambda c: (c, 0))],
        out_shape=[jax.ShapeDtypeStruct((p, ds), BF16), jax.ShapeDtypeStruct((p, ds), BF16),
                   jax.ShapeDtypeStruct((nch, ns, ds), BF16), jax.ShapeDtypeStruct((p, LANES), F32),
                   jax.ShapeDtypeStruct((p, cd), BF16)],
        scratch_shapes=[pltpu.VMEM((ns, ds), F32), pltpu.VMEM((1, LANES), F32), pltpu.VMEM((CHUNK, ds), F32),
                        pltpu.VMEM((CHUNK, cd), F32), pltpu.VMEM((CHUNK, ds), F32),
                        pltpu.VMEM((CHUNK, ds), BF16), pltpu.VMEM((CHUNK, ds), BF16),
                        pltpu.VMEM((8 + CHUNK, cd), F32)],
        compiler_params=_cparams(("arbitrary",)),
    )(xbc, xbc, z, dtf, conv_w, conv_b, brow, alog, dskip_l, ssd_norm, sel_t)


def _ssd_bwd(dyssd, y, z, xbc, pre, dtf, hin, dcf, conv_w, brow, alog, dskip_l, ssd_norm, sel_t, sel, hs, ha):
    p, cd = xbc.shape
    ds = z.shape[1]
    ns = (cd - ds) // (2 * SSD_GROUPS)
    gw = ds // SSD_GROUPS
    nch = p // CHUNK
    hpg = hs // SSD_GROUPS

    def body(dyssd_ref, y_ref, z_ref, xbc_ref, pre_ref, dtf_ref, hin_ref, dcf_ref, cw_ref, brow_ref,
             alog_ref, dsk_ref, nrm_ref, selt_ref, sel_ref,
             dxbc_ref, dz_ref, ddtf_ref, gcw_ref, gcb_ref, gnrm_ref, gsm_ref,
             dst_ref, nxt_ref, fcar_ref, gdsk_ref, dxc_ref, xc_s, dsl_s, dtx_s, ex_s, wx_s, dy_s, xdtb_s, xwb_s,
             dyb_s, dyeb_s):
        step = pl.program_id(0)
        c = nch - 1 - step

        @pl.when(step == 0)
        def _():
            dst_ref[...] = jnp.zeros_like(dst_ref)
            nxt_ref[...] = jnp.zeros_like(nxt_ref)
            fcar_ref[...] = jnp.zeros_like(fcar_ref)
            gdsk_ref[...] = jnp.zeros_like(gdsk_ref)
            gcw_ref[...] = jnp.zeros_like(gcw_ref)
            gcb_ref[...] = jnp.zeros_like(gcb_ref)
            gnrm_ref[...] = jnp.zeros_like(gnrm_ref)
            gsm_ref[...] = jnp.zeros_like(gsm_ref)

        rows = lax.broadcasted_iota(jnp.int32, (CHUNK, 1), 0)
        rowmask = jnp.where((rows >= PADN) | (c > 0), 1.0, 0.0)
        ri, ci = _tri_mats()
        causal = ri >= ci
        anti = ci >= ri
        tri = jnp.where(causal, 1.0, 0.0).astype(BF16)
        rtri = jnp.where(anti, 1.0, 0.0).astype(BF16)

        pre = pre_ref[...].astype(F32)
        sg = _sigmoid(pre)
        xc_s[...] = pre * sg * rowmask
        dsl_s[...] = sg * (1.0 + pre * (1.0 - sg)) * rowmask

        dtr, dt, a_row, run, is_dt, is_f = _ssd_scalars(dtf_ref, brow_ref, alog_ref, rowmask, hs, ha, tri)
        cs = jnp.where(is_dt, run, 0.0)
        cl = cs[CHUNK - 1:CHUNK, :]
        selt = selt_ref[...]
        selm = sel_ref[...]
        dtx_s[...] = _dot_sel(dt, selt)
        ex_s[...] = _dot_sel(jnp.exp(cs), selt)
        wx_s[...] = _dot_sel(jnp.exp(cl - cs), selt)
        cdec = jnp.exp(cl)
        cdec_x = _dot_sel(jnp.broadcast_to(cdec, (8, LANES)), selt)[0:1, :]
        cs_t = cs.T
        xdt = xc_s[:, :ds] * dtx_s[...]
        xdtb_s[...] = xdt.astype(BF16)
        xwb_s[...] = (xdt * wx_s[...]).astype(BF16)

        yv = y_ref[...].astype(F32)
        zf = z_ref[...].astype(F32)
        sz = _sigmoid(zf)
        u = yv * zf * sz
        dyo = dyssd_ref[...].astype(F32)
        du_parts = []
        for g in range(SSD_GROUPS):
            gs = slice(g * gw, (g + 1) * gw)
            ug = u[:, gs]
            rstd = lax.rsqrt(jnp.mean(ug * ug, axis=-1, keepdims=True) + EPS)
            yhat = ug * rstd
            dyg = dyo[:, gs]
            gnrm_ref[0:1, gs] += jnp.sum(dyg * yhat, axis=0, keepdims=True)
            dyh = dyg * nrm_ref[:, gs]
            du_parts.append(rstd * (dyh - yhat * jnp.mean(dyh * yhat, axis=-1, keepdims=True)))
        du = jnp.concatenate(du_parts, axis=1)
        dy = du * zf * sz
        dz_ref[...] = (du * yv * sz * (1.0 + zf * (1.0 - sz))).astype(BF16)
        dy_s[...] = dy
        dyb_s[...] = dy.astype(BF16)
        dyeb_s[...] = (dy * ex_s[...]).astype(BF16)
        gdsk_ref[...] += jnp.sum(dy * xc_s[:, :ds], axis=0, keepdims=True)
        lane = lax.broadcasted_iota(jnp.int32, (1, LANES), 1)
        half0 = lane < HEAD_DIM
        x_parts, yo_parts, t4_parts = [], [], []
        dcs = jnp.zeros((CHUNK, LANES), F32)
        for g in range(SSD_GROUPS):
            gs = slice(g * gw, (g + 1) * gw)
            bsl = slice(ds + g * ns, ds + (g + 1) * ns)
            csl = slice(ds + SSD_GROUPS * ns + g * ns, ds + SSD_GROUPS * ns + (g + 1) * ns)
            bg = xc_s[:, bsl].astype(BF16)
            cg = xc_s[:, csl].astype(BF16)
            gm = _dot(cg, bg, NT)
            gm_t = _dot(bg, cg, NT)
            stg_b = hin_ref[0, :, gs]
            dstg = dst_ref[:, gs]
            dstg_b = dstg.astype(BF16)
            t4_parts.append(jnp.sum(dstg * stg_b.astype(F32), axis=0, keepdims=True))
            zst = _dot(bg, dstg_b) * wx_s[:, gs]
            x_parts.append(xc_s[:, gs] * dtx_s[:, gs] * zst)
            yo_parts.append(dy_s[:, gs] * (_dot(cg, stg_b) * ex_s[:, gs]))
            dgsum = jnp.zeros((CHUNK, CHUNK), F32)
            dgtsum = jnp.zeros((CHUNK, CHUNK), F32)
            for pr in range(gw // LANES):
                sl = slice(g * gw + pr * LANES, g * gw + (pr + 1) * LANES)
                xp = xdtb_s[:, sl]
                dyp = dyb_s[:, sl]
                dxd = zst[:, pr * LANES:(pr + 1) * LANES]
                for j in range(2):
                    h = g * hpg + 2 * pr + j
                    sel_l = half0 if j == 0 else jnp.logical_not(half0)
                    seg = cs[:, h:h + 1] - cs_t[h:h + 1, :]
                    lm = jnp.where(causal, jnp.exp(jnp.minimum(seg, 0.0)), 0.0)
                    lmt = lm.T
                    dyp_m = jnp.where(sel_l, dyp, jnp.zeros_like(dyp))
                    xp_m = jnp.where(sel_l, xp, jnp.zeros_like(xp))
                    dxd = dxd + _dot((gm_t * lmt).astype(BF16), dyp_m)
                    dg = _dot(dyp_m, xp, NT) * lm
                    dgt = _dot(xp_m, dyp, NT) * lmt
                    dgsum = dgsum + dg
                    dgtsum = dgtsum + dgt
                    qrow = (jnp.sum(dg * gm, axis=1, keepdims=True) - jnp.sum(dgt * gm_t, axis=1, keepdims=True))
                    dcs = dcs + jnp.where(lane == h, qrow, 0.0)
                dxc_ref[:, sl] = dxd
            dxc_ref[:, csl] = _dot(dgsum.astype(BF16), bg) + _dot(dyeb_s[:, gs], stg_b, NT)
            dxc_ref[:, bsl] = _dot(dgtsum.astype(BF16), cg) + _dot(xwb_s[:, gs], dstg_b, NT)
            dst_ref[:, gs] = dstg * cdec_x[:, gs] + _dot(cg, dyeb_s[:, gs], TN)

        dxdt = dxc_ref[:, :ds]
        xst = _dot_sel(jnp.concatenate(x_parts, axis=1), selm)
        yo = _dot_sel(jnp.concatenate(yo_parts, axis=1), selm)
        t4 = _dot_sel(jnp.concatenate([jnp.concatenate(t4_parts, axis=1), jnp.zeros((7, ds), F32)], axis=0), selm)
        dcl = jnp.sum(xst, axis=0, keepdims=True) + cdec * t4[0:1, :]
        dcs = dcs + yo - xst + jnp.where(rows == CHUNK - 1, dcl, 0.0)
        da_ = _dot_tri(rtri, dcs)
        ddt = _dot_sel(dxdt * xc_s[:, :ds], selm) + da_ * a_row
        dcf_blk = dcf_ref[...]
        dlogf = _dot_tri(rtri, dcf_blk) + fcar_ref[...]
        fcar_ref[...] += jnp.sum(dcf_blk, axis=0, keepdims=True)
        sgd = _sigmoid(dtr)
        ddtf = (jnp.where(is_dt, ddt * sgd, 0.0) + jnp.where(is_f, dlogf * (1.0 - sgd), 0.0)) * rowmask
        ddtf_ref[...] = ddtf
        gsm_ref[0:1, :] += jnp.sum(ddtf, axis=0, keepdims=True)
        gsm_ref[1:2, :] += jnp.sum(da_ * dt, axis=0, keepdims=True) * a_row

        dxc_ref[:, :ds] = dxdt * dtx_s[...] + dsk_ref[...] * dy_s[...]
        dpre = dxc_ref[...] * dsl_s[...]
        nxt_ref[0:CHUNK, :] = dpre
        gcb_ref[0:1, :] += jnp.sum(dpre, axis=0, keepdims=True)
        xr = xbc_ref[...].astype(F32)
        gcw_ref[CONV_K - 1:CONV_K, :] += jnp.sum(dpre * xr, axis=0, keepdims=True)
        dxr = cw_ref[CONV_K - 1:CONV_K, :] * dpre
        for j in range(1, CONV_K):
            up = nxt_ref[j:j + CHUNK, :]
            gcw_ref[CONV_K - 1 - j:CONV_K - j, :] += jnp.sum(up * xr, axis=0, keepdims=True)
            dxr = dxr + cw_ref[CONV_K - 1 - j:CONV_K - j, :] * up
        nxt_ref[CHUNK:, :] = dpre[0:8, :]
        dxbc_ref[...] = dxr.astype(BF16)

        @pl.when(step == nch - 1)
        def _():
            gsm_ref[2:3, :] = _dot_sel(jnp.broadcast_to(gdsk_ref[...], (8, ds)), selm)[0:1, :]

    rev = lambda s: nch - 1 - s
    blk = lambda w: pl.BlockSpec((CHUNK, w), lambda s: (rev(s), 0))
    return pl.pallas_call(
        body, name="ssd_bwd", grid=(nch,),
        in_specs=[blk(ds), blk(ds), blk(ds), blk(cd), blk(cd),
                  blk(LANES), pl.BlockSpec((1, ns, ds), lambda s: (rev(s), 0, 0)), blk(LANES),
                  _full((CONV_K, cd)), _full((1, LANES)), _full((1, LANES)),
                  _full((1, ds)), _full((1, ds)), _full((LANES, ds)), _full((ds, LANES))],
        out_specs=[blk(cd), blk(ds), blk(LANES), _full((8, cd)), _full((8, cd)), _full((8, ds)), _full((8, LANES))],
        out_shape=[jax.ShapeDtypeStruct((p, cd), BF16), jax.ShapeDtypeStruct((p, ds), BF16),
                   jax.ShapeDtypeStruct((p, LANES), F32), jax.ShapeDtypeStruct((8, cd), F32),
                   jax.ShapeDtypeStruct((8, cd), F32), jax.ShapeDtypeStruct((8, ds), F32),
                   jax.ShapeDtypeStruct((8, LANES), F32)],
        scratch_shapes=[pltpu.VMEM((ns, ds), F32), pltpu.VMEM((CHUNK + 8, cd), F32), pltpu.VMEM((1, LANES), F32),
                        pltpu.VMEM((1, ds), F32), pltpu.VMEM((CHUNK, cd), F32),
                        pltpu.VMEM((CHUNK, cd), F32), pltpu.VMEM((CHUNK, cd), F32),
                        pltpu.VMEM((CHUNK, ds), F32), pltpu.VMEM((CHUNK, ds), F32), pltpu.VMEM((CHUNK, ds), F32),
                        pltpu.VMEM((CHUNK, ds), F32), pltpu.VMEM((CHUNK, ds), BF16), pltpu.VMEM((CHUNK, ds), BF16),
                        pltpu.VMEM((CHUNK, ds), BF16), pltpu.VMEM((CHUNK, ds), BF16)],
        compiler_params=_cparams(("arbitrary",)),
    )(dyssd, y, z, xbc, pre, dtf, hin, dcf, conv_w, brow, alog, dskip_l, ssd_norm, sel_t, sel)


def _attn_fwd(q, k, v, ck, blk):
    p, da = q.shape
    npair, nkb = ck.shape[0], ck.shape[1]
    scale = 1.0 / math.sqrt(HEAD_DIM)

    def body(q_ref, k_ref, v_ref, ck_ref, o_ref, lse_ref):
        i = pl.program_id(1)
        lane = lax.broadcasted_iota(jnp.int32, (1, LANES), 1)
        sels = [lane < HEAD_DIM, lane >= HEAD_DIM]
        ones = [jnp.where(lane == HEAD_DIM, 1.0, 0.0).astype(BF16), jnp.where(lane == 0, 1.0, 0.0).astype(BF16)]
        qb = q_ref[...] * scale

        def step(kb, carry, masked, nk=1):
            r0 = pl.multiple_of(kb * blk, blk)
            ks = k_ref[pl.ds(r0, nk * blk), :]
            vs = v_ref[pl.ds(r0, nk * blk), :]
            kk = jnp.concatenate([jnp.where(sel, ks, jnp.zeros_like(ks)) for sel in sels], axis=0)
            s_both = _dot(qb, kk, NT)
            out = []
            for j in range(2):
                m, acc = carry[2 * j], carry[2 * j + 1]
                ckr = jnp.concatenate([ck_ref[0, kb + t, j:j + 1, :] for t in range(nk)], axis=1)
                s = s_both[:, j * nk * blk:(j + 1) * nk * blk] - ckr
                if masked:
                    col = lax.broadcasted_iota(jnp.int32, (blk, nk * blk), 1) - (nk - 1) * blk
                    s = jnp.where(col <= lax.broadcasted_iota(jnp.int32, (blk, nk * blk), 0), s, NEG)
                mn = jnp.maximum(m, jnp.max(s, axis=-1, keepdims=True))
                pr = jnp.exp(s - mn).astype(BF16)
                acc = jnp.exp(m - mn) * acc + _dot(pr, jnp.where(sels[j], vs, ones[j]))
                out += [mn, acc]
            return tuple(out)

        init = (jnp.full((blk, 1), NEG, F32), jnp.zeros((blk, LANES), F32)) * 2

        def finish(carry):
            m0, a0, m1, a1 = carry
            l0 = a0[:, HEAD_DIM:HEAD_DIM + 1]
            l1 = a1[:, 0:1]
            o_ref[...] = jnp.where(sels[0], a0 / l0, a1 / l1).astype(BF16)
            lse_ref[...] = jnp.where(sels[0], m0 + jnp.log(l0), m1 + jnp.log(l1))

        @pl.when(i == 0)
        def _():
            finish(step(0, init, True))

        def sweep(last):
            below = i + 1 - last
            n4 = below // 4
            n2 = (below - 4 * n4) // 2
            carry = lax.fori_loop(0, n4, lambda t, c: step(4 * t, c, False, 4), init)
            carry = lax.fori_loop(0, n2, lambda t, c: step(4 * n4 + 2 * t, c, False, 2), carry)
            carry = lax.fori_loop(4 * n4 + 2 * n2, below, lambda kb, c: step(kb, c, False), carry)
            finish(step(below, carry, True, last))

        @pl.when((i > 0) & (i < 3))
        def _():
            sweep(2)

        @pl.when(i >= 3)
        def _():
            sweep(4)

    return pl.pallas_call(
        body, name="attn_fwd", grid=(npair, p // blk),
        in_specs=[pl.BlockSpec((blk, LANES), lambda h, i: (i, h)),
                  pl.BlockSpec((p, LANES), lambda h, i: (0, h)), pl.BlockSpec((p, LANES), lambda h, i: (0, h)),
                  pl.BlockSpec((1, nkb, 8, blk), lambda h, i: (h, 0, 0, 0))],
        out_specs=[pl.BlockSpec((blk, LANES), lambda h, i: (i, h)), pl.BlockSpec((blk, LANES), lambda h, i: (i, h))],
        out_shape=[jax.ShapeDtypeStruct((p, da), BF16), jax.ShapeDtypeStruct((p, da), F32)],
        compiler_params=_cparams(("parallel", "arbitrary")),
    )(q, k, v, ck)


def _attn_bwd(q, k, v, o, do, lse_rep, ck, blk):
    p, da = q.shape
    npair, nkb = ck.shape[0], ck.shape[1]
    nq = p // blk
    scale = 1.0 / math.sqrt(HEAD_DIM)

    def body(k_ref, v_ref, q_ref, do_ref, o_ref, lse_ref, ck_ref, dk_ref, dv_ref, dq_ref, dcs_ref, rsum_ref, dq_acc):
        jb = pl.program_id(1)

        @pl.when(jb == 0)
        def _():
            dq_acc[...] = jnp.zeros_like(dq_acc)

        ks = k_ref[...]
        vs = v_ref[...]
        lane = lax.broadcasted_iota(jnp.int32, (1, LANES), 1)
        sels = [lane < HEAD_DIM, lane >= HEAD_DIM]
        ones = [jnp.where(lane == HEAD_DIM, 1.0, 0.0).astype(BF16), jnp.where(lane == 0, 1.0, 0.0).astype(BF16)]
        kss = ks * scale
        kmo = [jnp.where(sels[j], kss, ones[j]) for j in range(2)]

        def step(ib, carry, masked, nb=1):
            rows = nb * blk
            r0 = pl.multiple_of(ib * blk, blk)
            qb = q_ref[pl.ds(r0, rows), :] * scale
            dob = do_ref[pl.ds(r0, rows), :]
            prod = dob.astype(F32) * o_ref[pl.ds(r0, rows), :].astype(F32)
            out = []
            for j in range(2):
                dk, dv = carry[2 * j], carry[2 * j + 1]
                qm = jnp.where(sels[j], qb, jnp.zeros_like(qb))
                dom = jnp.where(sels[j], dob, jnp.zeros_like(dob))
                lse = lse_ref[pl.ds(r0, rows), HEAD_DIM * j:HEAD_DIM * j + 1]
                dlt = jnp.sum(jnp.where(sels[j], prod, 0.0), axis=-1, keepdims=True)
                s = _dot(qm, ks, NT) - ck_ref[0, 0, j:j + 1, :] - lse
                pm = jnp.exp(jnp.minimum(s, 0.0))
                if masked:
                    causal = (lax.broadcasted_iota(jnp.int32, (rows, blk), 1)
                              <= lax.broadcasted_iota(jnp.int32, (rows, blk), 0))
                    pm = jnp.where(causal, pm, 0.0)
                ds_b = (pm * (_dot(dom, vs, NT) - dlt)).astype(BF16)
                dv = dv + _dot(pm.astype(BF16), dom, TN)
                dk = dk + _dot(ds_b, jnp.where(sels[j], qb, ones[j]), TN)
                dq_acc[pl.ds(r0, rows), LANES * j:LANES * (j + 1)] += _dot(ds_b, kmo[j])
                out += [dk, dv]
            return tuple(out)

        pair8 = lambda c0, c1: jnp.where(lane == 0, c0, jnp.where(lane == 1, c1, 0.0)).T[0:8]
        zero = jnp.zeros((blk, LANES), F32)
        init = (zero, zero, zero, zero)

        def finish(carry):
            dk0, dv0, dk1, dv1 = carry
            dk_ref[...] = jnp.where(sels[0], dk0, dk1).astype(BF16)
            dv_ref[...] = (dv0 + dv1).astype(BF16)
            dcs_ref[0] = pair8(dk0[:, HEAD_DIM:HEAD_DIM + 1], dk1[:, 0:1])

        @pl.when(jb == nq - 1)
        def _():
            finish(step(jb, init, True))

        @pl.when(jb < nq - 1)
        def _():
            carry = step(jb, init, True, 2)
            n4 = (nq - 2 - jb) // 4
            n2 = (nq - 2 - jb - 4 * n4) // 2
            carry = lax.fori_loop(0, n4, lambda t, c: step(jb + 2 + 4 * t, c, False, 4), carry)
            carry = lax.fori_loop(0, n2, lambda t, c: step(jb + 2 + 4 * n4 + 2 * t, c, False, 2), carry)
            finish(lax.fori_loop(jb + 2 + 4 * n4 + 2 * n2, nq, lambda ib, c: step(ib, c, False), carry))

        @pl.when(jb == nkb - 1)
        def _():
            a0 = dq_acc[:, :LANES]
            a1 = dq_acc[:, LANES:]
            dq_ref[...] = jnp.where(sels[0], a0, a1).astype(BF16)
            rsum_ref[0] = pair8(a0[:, HEAD_DIM:HEAD_DIM + 1], a1[:, 0:1])

    colblk = pl.BlockSpec((blk, LANES), lambda h, j: (j, h))
    colfull = pl.BlockSpec((p, LANES), lambda h, j: (0, h))
    ckspec = pl.BlockSpec((1, 1, 8, blk), lambda h, j: (h, j, 0, 0))
    return pl.pallas_call(
        body, name="attn_bwd", grid=(npair, nkb),
        in_specs=[colblk, colblk, colfull, colfull, colfull, colfull, ckspec],
        out_specs=[colblk, colblk, colfull, pl.BlockSpec((1, 8, blk), lambda h, j: (h, 0, j)),
                   pl.BlockSpec((1, 8, p), lambda h, j: (h, 0, 0))],
        out_shape=[jax.ShapeDtypeStruct((p, da), BF16), jax.ShapeDtypeStruct((p, da), BF16),
                   jax.ShapeDtypeStruct((p, da), BF16), jax.ShapeDtypeStruct((npair, 8, p), F32),
                   jax.ShapeDtypeStruct((npair, 8, p), F32)],
        scratch_shapes=[pltpu.VMEM((p, 2 * LANES), F32)],
        compiler_params=_cparams(("parallel", "arbitrary")),
    )(k, v, q, do, o, lse_rep, ck)


def _tail_fwd(yssd, o, zatt, graw, head, x2, tgt2, wps, wpa, wout, gate_bias, norm_post, tm):
    p, ds = yssd.shape
    da = o.shape[1]
    d = x2.shape[1]
    nsub = tm // CHUNK

    def body(yssd_ref, o_ref, zatt_ref, g_ref, head_ref, *rest):
        x_refs, t_refs = rest[:nsub], rest[nsub:2 * nsub]
        (wps_ref, wpa_ref, wout_ref, gb_ref, np_ref,
         yatt_ref, mrg_ref, a_ref, b_ref, dzo_ref, dout_ref, red_ref) = rest[2 * nsub:]
        i = pl.program_id(0)

        @pl.when(i == 0)
        def _():
            red_ref[...] = jnp.zeros_like(red_ref)

        first = jnp.where(i == 0, head_ref[...], x_refs[0][...])
        h = jnp.concatenate([first] + [r[...] for r in x_refs[1:]], axis=0)
        tgt = jnp.concatenate([r[...] for r in t_refs], axis=0)
        rows = lax.broadcasted_iota(jnp.int32, (tm, 1), 0)
        valid = jnp.where((i > 0) | (rows >= CHUNK), 1.0, 0.0)
        ob = o_ref[...].astype(F32)
        za = zatt_ref[...].astype(F32)
        yatt_b = (ob * za * _sigmoid(za)).astype(BF16)
        yatt_ref[...] = yatt_b
        a = _dot(yssd_ref[...], wps_ref[...])
        b = _dot(yatt_b, wpa_ref[...])
        a_ref[...] = a.astype(BF16)
        b_ref[...] = b.astype(BF16)
        gr = g_ref[...].astype(F32) + gb_ref[...]
        mrg_b = (_sigmoid(gr[:, :d]) * a + _sigmoid(gr[:, d:]) * b).astype(BF16)
        mrg_ref[...] = mrg_b
        zo = _dot(mrg_b, wout_ref[...])
        rstd = lax.rsqrt(jnp.mean(zo * zo, axis=-1, keepdims=True) + EPS)
        zh = zo * rstd
        npw = np_ref[...]
        err = (h + zh * npw - tgt) * valid
        dout = err * (1.0 / d)
        dout_ref[...] = dout
        dzh = dout * npw
        dzo_ref[...] = (rstd * (dzh - zh * jnp.mean(dzh * zh, axis=-1, keepdims=True))).astype(BF16)
        red_ref[0:1, :] += jnp.sum(dout * zh, axis=0, keepdims=True)
        red_ref[1:2, 0:1] += jnp.sum(jnp.sum(err * err, axis=1, keepdims=True), axis=0, keepdims=True) * (0.5 / d)

    row = lambda w: pl.BlockSpec((tm, w), lambda i: (i, 0))
    once = lambda shape: pl.BlockSpec(shape, lambda i: (0,) * len(shape), pipeline_mode=pl.Buffered(1))
    subs = _x_row_specs(tm, d)
    sd = jax.ShapeDtypeStruct
    return pl.pallas_call(
        body, name="tail_fwd", grid=(p // tm,),
        in_specs=[row(ds), row(da), row(da), row(2 * d), _full((CHUNK, d))] + subs + subs
                 + [once((ds, d)), once((da, d)), once((d, d)), _full((1, 2 * d)), _full((1, d))],
        out_specs=[row(da), row(d), row(d), row(d), row(d), row(d), _full((8, d))],
        out_shape=[sd((p, da), BF16), sd((p, d), BF16), sd((p, d), BF16), sd((p, d), BF16), sd((p, d), BF16),
                   sd((p, d), F32), sd((8, d), F32)],
        compiler_params=_cparams(("arbitrary",)),
    )(yssd, o, zatt, graw, head, *([x2] * nsub), *([tgt2] * nsub), wps, wpa, wout, gate_bias, norm_post)


def _tail_bwd(dzo, a_b, b_b, graw, o, zatt, wps, wpa, wout, gate_bias, tm):
    p, d = dzo.shape
    ds, da = wps.shape[0], wpa.shape[0]

    def body(dzo_ref, a_ref, b_ref, g_ref, o_ref, zatt_ref, wps_ref, wpa_ref, wout_ref, gb_ref,
             da_ref, db_ref, dg_ref, dyssd_ref, do_ref, dzatt_ref, red_ref):
        i = pl.program_id(0)

        @pl.when(i == 0)
        def _():
            red_ref[...] = jnp.zeros_like(red_ref)

        gr = g_ref[...].astype(F32) + gb_ref[...]
        gs = _sigmoid(gr[:, :d])
        ga = _sigmoid(gr[:, d:])
        dm = _dot(dzo_ref[...], wout_ref[...], NT)
        da_b = (gs * dm).astype(BF16)
        db_b = (ga * dm).astype(BF16)
        da_ref[...] = da_b
        db_ref[...] = db_b
        dgs = dm * a_ref[...].astype(F32) * gs * (1.0 - gs)
        dga = dm * b_ref[...].astype(F32) * ga * (1.0 - ga)
        dg_ref[:, :d] = dgs.astype(BF16)
        dg_ref[:, d:] = dga.astype(BF16)
        red_ref[0:1, :d] += jnp.sum(dgs, axis=0, keepdims=True)
        red_ref[0:1, d:] += jnp.sum(dga, axis=0, keepdims=True)
        dyssd_ref[...] = _dot(da_b, wps_ref[...], NT).astype(BF16)
        dya = _dot(db_b, wpa_ref[...], NT)
        ob = o_ref[...].astype(F32)
        za = zatt_ref[...].astype(F32)
        sza = _sigmoid(za)
        do_ref[...] = (dya * za * sza).astype(BF16)
        dzatt_ref[...] = (dya * ob * sza * (1.0 + za * (1.0 - sza))).astype(BF16)

    row = lambda w: pl.BlockSpec((tm, w), lambda i: (i, 0))
    once = lambda shape: pl.BlockSpec(shape, lambda i: (0,) * len(shape), pipeline_mode=pl.Buffered(1))
    sd = jax.ShapeDtypeStruct
    return pl.pallas_call(
        body, name="tail_bwd", grid=(p // tm,),
        in_specs=[row(d), row(d), row(d), row(2 * d), row(da), row(da),
                  once((ds, d)), once((da, d)), once((d, d)), _full((1, 2 * d))],
        out_specs=[row(d), row(d), row(2 * d), row(ds), row(da), row(da), _full((8, 2 * d))],
        out_shape=[sd((p, d), BF16), sd((p, d), BF16), sd((p, 2 * d), BF16), sd((p, ds), BF16), sd((p, da), BF16),
                   sd((p, da), BF16), sd((8, 2 * d), F32)],
        compiler_params=_cparams(("arbitrary",)),
    )(dzo, a_b, b_b, graw, o, zatt, wps, wpa, wout, gate_bias)


def _adamw_math(w, g, m, v):
    m2 = ADAM_B1 * m + (1.0 - ADAM_B1) * g
    v2 = ADAM_B2 * v + (1.0 - ADAM_B2) * (g * g)
    m_hat = m2 / (1.0 - ADAM_B1 ** ADAM_STEP)
    v_hat = v2 / (1.0 - ADAM_B2 ** ADAM_STEP)
    delta = -ADAM_LR * (m_hat / (jnp.sqrt(v_hat) + ADAM_EPS) + ADAM_WD * w)
    return delta, m2, v2


def _adamw_small(params, red, name):
    names = list(params)
    n = len(names)
    extra = [params[k][3] for k in names if not isinstance(params[k][3], tuple)]

    def body(*refs):
        w_refs, m_refs, v_refs = refs[:n], refs[n:2 * n], refs[2 * n:3 * n]
        red_ref = refs[3 * n]
        g_refs = iter(refs[3 * n + 1:3 * n + 1 + len(extra)])
        outs = refs[3 * n + 1 + len(extra):]
        for i, k in enumerate(names):
            where = params[k][3]
            rows, cols = w_refs[i].shape
            if isinstance(where, tuple):
                g = red_ref[where[0]:where[0] + rows, where[1]:where[1] + cols]
            else:
                g = next(g_refs)[...]
            delta, m2, v2 = _adamw_math(w_refs[i][...], g, m_refs[i][...], v_refs[i][...])
            for o, val in zip(outs[4 * i:4 * i + 4], (g, delta, m2, v2)):
                o[...] = val

    vm = pl.BlockSpec(memory_space=pltpu.VMEM)
    ws, ms, vs = ([params[k][j] for k in names] for j in range(3))
    out = pl.pallas_call(
        body, name=name,
        out_shape=[jax.ShapeDtypeStruct(w.shape, F32) for w in ws for _ in range(4)],
        in_specs=[vm] * (3 * n + 1 + len(extra)), out_specs=[vm] * (4 * n),
    )(*ws, *ms, *vs, red, *extra)
    return {k: tuple(out[4 * i:4 * i + 4]) for i, k in enumerate(names)}


def _adamw(w, g, m, v, name, parts=False, part_row0=0):
    r, cdim = w.shape
    tr, tc, by_rows = _tiles_2d(r, cdim)
    pick = (lambda i: (i, 0)) if by_rows else (lambda i: (0, i))
    assert part_row0 % tr == 0
    gpick = (lambda i: (i + part_row0 // tr, 0)) if by_rows else (lambda i: (part_row0 // tr, i))

    def body(w_ref, g_ref, m_ref, v_ref, go_ref, d_ref, mo_ref, vo_ref):
        if parts:
            g = g_ref[0].astype(F32)
            for s in range(1, g_ref.shape[0]):
                g = g + g_ref[s].astype(F32)
        else:
            g = g_ref[...]
        delta, m2, v2 = _adamw_math(w_ref[...], g, m_ref[...], v_ref[...])
        go_ref[...] = g
        d_ref[...] = delta
        mo_ref[...] = m2
        vo_ref[...] = v2

    blk = pl.BlockSpec((tr, tc), pick)
    gspec = pl.BlockSpec((g.shape[0], tr, tc), lambda i: (0,) + gpick(i)) if parts else blk
    return pl.pallas_call(
        body, name=name, grid=((r // tr) * (cdim // tc),),
        in_specs=[blk, gspec, blk, blk], out_specs=[blk] * 4,
        out_shape=[jax.ShapeDtypeStruct((r, cdim), F32)] * 4,
        compiler_params=_cparams(("parallel",)),
    )(w, g, m, v)


def _pad_cols(a, width):
    return jnp.pad(a, ((0, 0), (0, width - a.shape[1])))


def _pack_small_shard(conv_w_sh, meta_sh, width):
    return jnp.concatenate([_pad_cols(conv_w_sh, width), jnp.zeros((4, width), F32), _pad_cols(meta_sh, width)], axis=0)


def _pack_small_rep(norm_pre, norm_post, gate_bias, ssd_norm, conv_b, misc, width):
    rows = [norm_pre, norm_post, gate_bias, ssd_norm, conv_b, misc]
    return jnp.concatenate([_pad_cols(r, width) for r in rows] + [jnp.zeros((2, width), F32)], axis=0)


def kernel(x, meta_tokens, norm_pre, w_in, conv_w, conv_b, dt_bias, a_log, d_skip, ssd_norm, fgate_bias, gate_bias, w_proj_ssd, w_proj_att, w_out, norm_post, loss_target, m_meta_tokens, m_norm_pre, m_w_in, m_conv_w, m_conv_b, m_dt_bias, m_a_log, m_d_skip, m_ssd_norm, m_fgate_bias, m_gate_bias, m_w_proj_ssd, m_w_proj_att, m_w_out, m_norm_post, v_meta_tokens, v_norm_pre, v_w_in, v_conv_w, v_conv_b, v_dt_bias, v_a_log, v_d_skip, v_ssd_norm, v_fgate_bias, v_gate_bias, v_w_proj_ssd, v_w_proj_att, v_w_out, v_norm_post):
    seq, d = x.shape[1], x.shape[2]
    p = seq + CHUNK
    hs, ha = dt_bias.shape[1], fgate_bias.shape[1]
    ds, cd = ssd_norm.shape[1], conv_b.shape[1]
    da = ha * HEAD_DIM
    nc8 = w_in.shape[2]
    cws = cd // N_DEV
    msh = d // N_DEV
    r1, r2, r3 = ds // N_DEV, da // N_DEV, d // N_DEV
    me = _dev_index(*_my_pos())
    x2, tgt2 = x[0], loss_target[0]

    win_sh = jnp.transpose(w_in[0]).astype(BF16)
    rows_sh = jnp.concatenate([w_proj_ssd[0], w_proj_att[0], w_out[0]], axis=0).astype(BF16)
    small_sh = _pack_small_shard(conv_w[0], meta_tokens, cws)
    win_all, small_all = _all_gather([win_sh, small_sh], "gather_weights")
    rows_sh, win_all = lax.optimization_barrier((rows_sh, win_all))
    rows_sems, rows_thru, rows_land, rows_token = _bcast_start(rows_sh, "gather_rows_start")
    cuts = [0, ds, ds + cd, ds + cd + hs, ds + cd + hs + da, ds + cd + hs + 2 * da, ds + cd + hs + 3 * da,
            ds + cd + hs + 4 * da, ds + cd + hs + 4 * da + ha, ds + cd + hs + 4 * da + ha + 2 * d]

    def piece_rows(r0, r1):
        parts = [win_all[s, max(r0, s * nc8) - s * nc8:min(r1, (s + 1) * nc8) - s * nc8]
                 for s in range(N_DEV) if max(r0, s * nc8) < min(r1, (s + 1) * nc8)]
        return parts[0] if len(parts) == 1 else jnp.concatenate(parts, axis=0)

    w_z, w_xbc, w_dt, w_zatt, w_q, w_k, w_v, w_f, w_g = [piece_rows(cuts[i], cuts[i + 1]) for i in range(9)]
    w_dtf = jnp.concatenate([w_dt, w_f, jnp.zeros((LANES - hs - ha, d), BF16)], axis=0)
    conv_w_full = jnp.transpose(small_all[:, 0:CONV_K, :], (1, 0, 2)).reshape(CONV_K, cd)
    meta_full = jnp.transpose(small_all[:, 8:8 + N_META, :msh], (1, 0, 2)).reshape(N_META, d)
    head = jnp.concatenate([jnp.zeros((PADN, d), F32), meta_full + rows_token[0:1, 0:1]], axis=0)

    tm = _att_block(p)
    u = _prenorm_fwd(head, x2, norm_pre, tm)
    seg_w = [w_z, w_xbc, w_zatt, w_q, w_k, w_v, w_g]
    zs, xbc, zatt, q, k, v, graw = [
        _mm(u, w, "nt", BF16, _tile(p, (1408, tm)), _tile(w.shape[0], (1024, 512, 256, 128)), "inproj_%d" % i)
        for i, w in enumerate(seg_w)]
    dtf = _mm(u, w_dtf, "nt", F32, _tile(p, (1408, tm)), LANES, "inproj_dtf")

    brow = jnp.concatenate([dt_bias, fgate_bias, jnp.zeros((1, LANES - hs - ha), F32)], axis=1)
    alog_row = _pad_cols(a_log, LANES)
    dskip_l = jnp.repeat(d_skip, HEAD_DIM, axis=1)
    sel_t = (lax.broadcasted_iota(jnp.int32, (LANES, ds), 1) // HEAD_DIM
             == lax.broadcasted_iota(jnp.int32, (LANES, ds), 0)).astype(BF16)
    sel = sel_t.T
    y, yssd, hin, cf, pre = _ssd_fwd(xbc, zs, dtf, conv_w_full, conv_b, brow, alog_row, dskip_l, ssd_norm, sel_t, hs, ha)

    blk = _att_block(p)
    nkb, npair = p // blk, ha // 2
    cum = jnp.where(lax.broadcasted_iota(jnp.int32, (p, 1), 0) < PADN, -NEG, cf[:, hs:hs + ha])
    ck = jnp.transpose(cum.T.reshape(npair, 2, nkb, blk), (0, 2, 1, 3))
    ck = jnp.pad(ck, ((0, 0), (0, 0), (0, 6), (0, 0)))
    o, lse_rep = _attn_fwd(q, k, v, ck, blk)

    rows_all = _bcast_wait(rows_sems, rows_thru, rows_land, lse_rep, "gather_rows_wait")
    wps = rows_all[:, :r1].reshape(ds, d)
    wpa = rows_all[:, r1:r1 + r2].reshape(da, d)
    wout = rows_all[:, r1 + r2:].reshape(d, d)

    yatt, mrg, a_b, b_b, dzo, dout, red_fwd = _tail_fwd(
        yssd, o, zatt, graw, head, x2, tgt2, wps, wpa, wout, gate_bias, norm_post, tm)
    da_, db_, dgraw, dyssd, d_o, dzatt, red_bwd = _tail_bwd(dzo, a_b, b_b, graw, o, zatt, wps, wpa, wout, gate_bias, tm)

    tw = _tile(d, (512, 256, 128))
    g_wout = _mm(mrg, dzo, "tn", BF16, tw, d, "wgrad_out")
    g_wps = _mm(yssd, da_, "tn", BF16, _tile(ds, (512, 256, 128)), d, "wgrad_ps")
    g_wpa = _mm(yatt, db_, "tn", BF16, _tile(da, (512, 256, 128)), d, "wgrad_pa")

    core = lax.axis_index("c").astype(jnp.int32).reshape(1)
    chip = me // 2
    grows_parts = jnp.concatenate([g_wps.reshape(N_DEV, r1, d), g_wpa.reshape(N_DEV, r2, d),
                                   g_wout.reshape(N_DEV, r3, d)], axis=1)
    (sib_rows,) = _exchange_sibling([grows_parts], "scatter_rows_sibling")
    chip_rows = _pair_add(grows_parts, sib_rows, core, "pair_add_rows")
    r_sems, r_thru, r_lands, r_token = _exchange_chips_start([chip_rows], "scatter_rows_start")

    dk, dv, dq, dcs, rsum = _attn_bwd(q, k, v, o, d_o, lse_rep, ck + r_token[0:1, 0:1], blk)
    dcum = (rsum - dcs)[:, 0:2, :].reshape(ha, p).T
    dcf = jnp.pad(dcum, ((0, 0), (hs, LANES - hs - ha)))
    dxbc, dzs, ddtf, gcw, gcb, gnrm, gsm = _ssd_bwd(
        dyssd, y, zs, xbc, pre, dtf, hin, dcf, conv_w_full, brow, alog_row, dskip_l, ssd_norm, sel_t, sel, hs, ha)
    ddtf_b = ddtf.astype(BF16)

    dsegs = [dzs, dxbc, dzatt, dq, dk, dv, dgraw, ddtf_b]
    gsegs = [_mm(dsg, u, "tn", BF16, _tile(dsg.shape[1], (512, 256, 128)), d, "wgrad_in_%d" % i)
             for i, dsg in enumerate(dsegs)]
    g_z, g_xbc, g_zatt, g_q, g_k, g_v, g_g, g_dtf = gsegs
    gw_full = jnp.concatenate([g_z, g_xbc, g_dtf[:hs], g_zatt, g_q, g_k, g_v, g_dtf[hs:hs + ha], g_g], axis=0)
    gwin_parts = gw_full.reshape(N_DEV, nc8, d)

    (sib_win,) = _exchange_sibling([gwin_parts], "scatter_grads_sibling")
    chip_win = _pair_add(gwin_parts, sib_win, core, "pair_add_w_in")
    sems, thru, lands, token = _exchange_chips_start([chip_win], "scatter_grads_start")
    dsegs_after = dsegs[:-1] + [ddtf_b + token[0:1, 0:1].astype(BF16)]
    gx, ghead, gnp = _dgrad_prenorm(dsegs_after, seg_w + [w_dtf], head, x2, norm_pre, dout, tm, "dgrad_in")
    own_slot = lambda got, sent: lax.dynamic_update_slice_in_dim(
        got, lax.dynamic_slice_in_dim(sent, chip, 1, axis=0), chip, axis=0)
    (sent,), (got,) = _exchange_chips_wait(sems, thru, lands, gnp, "scatter_grads_wait")
    recv_win = own_slot(got, sent)
    (r_sent,), (r_got,) = _exchange_chips_wait(r_sems, r_thru, r_lands, gnp, "scatter_rows_wait")
    recv_rows = own_slot(r_got, r_sent)
    gmisc = jnp.concatenate([gsm[0:1], gsm[1:2], gsm[2:3], _pad_cols(red_fwd[1:2, 0:1], LANES)], axis=1)
    small_g = jnp.concatenate([
        _pack_small_rep(gnp[0:1], red_fwd[0:1], red_bwd[0:1], gnrm[0:1], gcb[0:1], gmisc, cd),
        _pad_cols(gcw[0:CONV_K], cd), jnp.zeros((4, cd), F32), _pad_cols(ghead[PADN:], cd)], axis=0)
    sg_sems, sg_thru, sg_land, sg_token = _bcast_start(small_g, "reduce_small_start")

    upd_in = _adamw(jnp.transpose(w_in[0]) + sg_token[0:1, 0:1], recv_win, jnp.transpose(m_w_in[0]),
                    jnp.transpose(v_w_in[0]), "adamw_w_in", parts=True)
    upd_ps = _adamw(w_proj_ssd[0] + sg_token[0:1, 0:1], recv_rows, m_w_proj_ssd[0], v_w_proj_ssd[0],
                    "adamw_w_proj_ssd", parts=True, part_row0=0)
    upd_pa = _adamw(w_proj_att[0], recv_rows, m_w_proj_att[0], v_w_proj_att[0], "adamw_w_proj_att", parts=True,
                    part_row0=r1)
    upd_out = _adamw(w_out[0], recv_rows, m_w_out[0], v_w_out[0], "adamw_w_out", parts=True, part_row0=r1 + r2)
    all_done = upd_in[1][0:8, 0:LANES] + upd_ps[1][0:8, 0:LANES] + upd_pa[1][0:8, 0:LANES] + upd_out[1][0:8, 0:LANES]
    red = _sum_slots(_bcast_wait(sg_sems, sg_thru, sg_land, all_done, "reduce_small_wait"), "reduce_small_sum")
    loss = red[5, 3 * LANES]
    g_conv_w = lax.dynamic_slice_in_dim(red[8:8 + CONV_K], me * cws, cws, axis=1)
    g_meta = lax.dynamic_slice_in_dim(red[16:16 + N_META, :d], me * msh, msh, axis=1)
    small = {
        "meta_tokens": (meta_tokens, m_meta_tokens, v_meta_tokens, g_meta),
        "norm_pre": (norm_pre, m_norm_pre, v_norm_pre, (0, 0)),
        "conv_w": (conv_w[0], m_conv_w[0], v_conv_w[0], g_conv_w),
        "conv_b": (conv_b, m_conv_b, v_conv_b, (4, 0)),
        "dt_bias": (dt_bias, m_dt_bias, v_dt_bias, (5, 0)),
        "a_log": (a_log, m_a_log, v_a_log, (5, LANES)),
        "d_skip": (d_skip, m_d_skip, v_d_skip, (5, 2 * LANES)),
        "ssd_norm": (ssd_norm, m_ssd_norm, v_ssd_norm, (3, 0)),
        "fgate_bias": (fgate_bias, m_fgate_bias, v_fgate_bias, (5, hs)),
        "gate_bias": (gate_bias, m_gate_bias, v_gate_bias, (2, 0)),
        "norm_post": (norm_post, m_norm_post, v_norm_post, (1, 0)),
    }
    upd_small = _adamw_small(small, red, "adamw_small")

    def leaves(i):
        sm = {k: v[i] for k, v in upd_small.items()}
        return [sm["meta_tokens"], sm["norm_pre"], jnp.transpose(upd_in[i])[None], sm["conv_w"][None], sm["conv_b"],
                sm["dt_bias"], sm["a_log"], sm["d_skip"], sm["ssd_norm"], sm["fgate_bias"], sm["gate_bias"],
                upd_ps[i][None], upd_pa[i][None], upd_out[i][None], sm["norm_post"]]

    return tuple([loss, gx[None]] + leaves(0) + leaves(1) + leaves(2) + leaves(3))
```

```python
import functools
import math

import jax
import jax.numpy as jnp
from jax import lax
from jax.experimental import pallas as pl
from jax.experimental.pallas import tpu as pltpu

F32 = jnp.float32
BF16 = jnp.bfloat16

N_DEV = 8
N_META = 16
CHUNK = 128
PADN = CHUNK - N_META
HEAD_DIM = 64
SSD_GROUPS = 4
CONV_K = 4
EPS = 1e-6
NEG = -1e30
LANES = 128
HALO = 16

ADAM_LR = 0.001
ADAM_B1 = 0.9
ADAM_B2 = 0.999
ADAM_EPS = 1e-08
ADAM_WD = 0.01
ADAM_STEP = 10

VMEM_LIMIT = 56 * 1024 * 1024

NN = (((1,), (0,)), ((), ()))
NT = (((1,), (1,)), ((), ()))
TN = (((0,), (0,)), ((), ()))
MESH = pl.DeviceIdType.MESH


def _dot(a, b, dims=NN):
    return lax.dot_general(a, b, dims, preferred_element_type=F32)


def _split2(x):
    hi = x.astype(BF16)
    lo = (x - hi.astype(F32)).astype(BF16)
    return hi, lo


def _dot_sel(x, sel):
    hi, lo = _split2(x)
    return _dot(hi, sel) + _dot(lo, sel)


def _dot_tri(tri, x):
    h1 = x.astype(BF16)
    r1 = x - h1.astype(F32)
    h2 = r1.astype(BF16)
    h3 = (r1 - h2.astype(F32)).astype(BF16)
    return _dot(tri, h1) + _dot(tri, h2) + _dot(tri, h3)


def _sigmoid(x):
    return 0.5 * jnp.tanh(0.5 * x) + 0.5


def _softplus(x):
    return jnp.maximum(x, 0.0) + jnp.log(1.0 + jnp.exp(-jnp.abs(x)))


def _cparams(sem=None, vmem=VMEM_LIMIT):
    kw = {"vmem_limit_bytes": vmem}
    if sem is not None:
        kw["dimension_semantics"] = sem
    return pltpu.CompilerParams(**kw)


def _full(shape):
    nd = len(shape)
    return pl.BlockSpec(shape, lambda *_: (0,) * nd)


def _att_block(p):
    return 384 if p % 384 == 0 else CHUNK


def _my_pos():
    return lax.axis_index("x"), lax.axis_index("y"), lax.axis_index("c")


def _dev_index(x, y, c):
    return 4 * x + 2 * y + c


FLIPS = [(fx, fy, fc) for fx in (0, 1) for fy in (0, 1) for fc in (0, 1)][1:]


def _flip(pos, f):
    return tuple((1 - p) if fi else p for p, fi in zip(pos, f))


def _all_gather(bufs, name):
    nb = len(bufs)

    def body(*refs):
        ins, outs = refs[:nb], refs[nb:2 * nb]
        send_sems, recv_sems, local_sems = refs[2 * nb:]
        x, y, c = _my_pos()
        me = _dev_index(x, y, c)
        sibling = (x, y, 1 - c)
        near = [(1 - x, y), (x, 1 - y)]
        far = (1 - x, 1 - y)
        relay_from = (c * (1 - x) + (1 - c) * x, c * y + (1 - c) * (1 - y))
        relay_to = (c * x + (1 - c) * (1 - x), c * (1 - y) + (1 - c) * y)

        def copy(b, k, block_idx, to, src=None):
            dst = outs[b].at[block_idx]
            return pltpu.make_async_remote_copy(
                src_ref=dst if src is None else src, dst_ref=dst,
                send_sem=send_sems.at[b, k], recv_sem=recv_sems.at[b, k],
                device_id=to, device_id_type=MESH)

        started = []
        for b in range(nb):
            mine = pltpu.make_async_copy(ins[b], outs[b].at[me], local_sems.at[b])
            mine.start()
            started.append(mine)
        sent = []
        for b in range(nb):
            sent.append(copy(b, 0, me, sibling, src=ins[b]))
            for j, chip in enumerate(near):
                sent.append(copy(b, 1 + j, me, (chip[0], chip[1], c), src=ins[b]))
        for cp in sent:
            cp.start()
        for j, chip in enumerate(near):
            blk = _dev_index(chip[0], chip[1], c)
            for b in range(nb):
                copy(b, 1 + j, blk, (x, y, c)).wait_recv()
                sent.append(copy(b, 4 + j, blk, sibling))
                sent[-1].start()
        for b in range(nb):
            sent.append(copy(b, 3, _dev_index(relay_from[0], relay_from[1], c), (relay_to[0], relay_to[1], c)))
            sent[-1].start()
        blk = _dev_index(far[0], far[1], c)
        for b in range(nb):
            copy(b, 3, blk, (x, y, c)).wait_recv()
            sent.append(copy(b, 6, blk, sibling))
            sent[-1].start()
        for b in range(nb):
            copy(b, 0, _dev_index(x, y, 1 - c), (x, y, c)).wait_recv()
        for j, chip in enumerate(near + [far]):
            blk = _dev_index(chip[0], chip[1], 1 - c)
            for b in range(nb):
                copy(b, 4 + j, blk, (x, y, c)).wait_recv()
        for cp in sent:
            cp.wait_send()
        for mine in started:
            mine.wait()

    any_spec = pl.BlockSpec(memory_space=pl.ANY)
    return pl.pallas_call(
        body, name=name,
        out_shape=[jax.ShapeDtypeStruct((N_DEV,) + b.shape, b.dtype) for b in bufs],
        in_specs=[any_spec] * nb, out_specs=[any_spec] * nb,
        scratch_shapes=[pltpu.SemaphoreType.DMA((nb, 7)), pltpu.SemaphoreType.DMA((nb, 7)),
                        pltpu.SemaphoreType.DMA((nb,))],
    )(*bufs)


N_CHIP = 4
CHIP_FLIPS = [(1, 0), (0, 1), (1, 1)]


def _exchange_sibling(bufs, name):
    nb = len(bufs)

    def body(*refs):
        ins, outs = refs[:nb], refs[nb:2 * nb]
        send_sems, recv_sems = refs[2 * nb:]
        x, y, c = _my_pos()

        def copy(b, k):
            return pltpu.make_async_remote_copy(
                src_ref=ins[b].at[2 * k + (1 - c)], dst_ref=outs[b].at[k],
                send_sem=send_sems.at[b, k], recv_sem=recv_sems.at[b, k],
                device_id=(x, y, 1 - c), device_id_type=MESH)

        cps = [copy(b, k) for b in range(nb) for k in range(N_CHIP)]
        for cp in cps:
            cp.start()
        for cp in cps:
            cp.wait()

    any_spec = pl.BlockSpec(memory_space=pl.ANY)
    return pl.pallas_call(
        body, name=name,
        out_shape=[jax.ShapeDtypeStruct((N_CHIP,) + b.shape[1:], b.dtype) for b in bufs],
        in_specs=[any_spec] * nb, out_specs=[any_spec] * nb,
        scratch_shapes=[pltpu.SemaphoreType.DMA((nb, N_CHIP)), pltpu.SemaphoreType.DMA((nb, N_CHIP))],
    )(*bufs)


def _pair_add(mine, recv, core, name):
    _, r, cdim = mine.shape
    tr, tc = r, cdim
    pick = lambda i: (i, 0)

    def body(core_ref, a_ref, b_ref, o_ref):
        o_ref[0] = (a_ref[0].astype(F32) + b_ref[0].astype(F32)).astype(o_ref.dtype)

    return pl.pallas_call(
        body, name=name,
        grid_spec=pltpu.PrefetchScalarGridSpec(
            num_scalar_prefetch=1, grid=(N_CHIP, (r // tr) * (cdim // tc)),
            in_specs=[pl.BlockSpec((1, tr, tc), lambda k, i, core_ref: (2 * k + core_ref[0],) + pick(i)),
                      pl.BlockSpec((1, tr, tc), lambda k, i, core_ref: (k,) + pick(i))],
            out_specs=pl.BlockSpec((1, tr, tc), lambda k, i, core_ref: (k,) + pick(i))),
        out_shape=jax.ShapeDtypeStruct((N_CHIP, r, cdim), mine.dtype),
        compiler_params=_cparams(("parallel", "parallel")),
    )(core, mine, recv)


def _chip_peer(x, y, f):
    return ((1 - x) if f[0] else x), ((1 - y) if f[1] else y)


def _exchange_chips_start(bufs, name):
    nb = len(bufs)
    nsem = 2 * 3 * nb

    def body(*refs):
        ins, lands = refs[:nb], refs[nb:2 * nb]
        sems = refs[2 * nb:2 * nb + nsem]
        token = refs[-1]
        x, y, c = _my_pos()
        for b in range(nb):
            for j, f in enumerate(CHIP_FLIPS):
                px, py = _chip_peer(x, y, f)
                pltpu.make_async_remote_copy(
                    src_ref=ins[b].at[2 * px + py], dst_ref=lands[b].at[2 * x + y],
                    send_sem=sems[2 * (3 * b + j)], recv_sem=sems[2 * (3 * b + j) + 1],
                    device_id=(px, py, c), device_id_type=MESH).start()
        token[...] = jnp.zeros_like(token)

    hbm = pl.BlockSpec(memory_space=pltpu.HBM)
    sem = pl.BlockSpec(memory_space=pltpu.SEMAPHORE)
    out = pl.pallas_call(
        body, name=name,
        out_shape=(*([pltpu.SemaphoreType.DMA(())] * nsem),
                   *[pltpu.HBM(b.shape, b.dtype) for b in bufs], *[pltpu.HBM(b.shape, b.dtype) for b in bufs],
                   jax.ShapeDtypeStruct((8, LANES), F32)),
        in_specs=[hbm] * (2 * nb),
        out_specs=(*([sem] * nsem), *([hbm] * (2 * nb)), pl.BlockSpec(memory_space=pltpu.VMEM)),
        input_output_aliases={i: nsem + i for i in range(2 * nb)},
        compiler_params=pltpu.CompilerParams(has_side_effects=pltpu.SideEffectType.DATAFLOW_SIDE_EFFECTING),
    )(*[pltpu.with_memory_space_constraint(b, pltpu.HBM) for b in bufs],
      *[pltpu.with_memory_space_constraint(lax.empty(b.shape, b.dtype), pltpu.HBM) for b in bufs])
    return out[:nsem], out[nsem:nsem + nb], out[nsem + nb:nsem + 2 * nb], out[-1]


def _exchange_chips_wait(sems, thru, lands, after, name):
    nb = len(thru)
    nsem = len(sems)

    def body(*refs):
        ins, lnd = refs[:nb], refs[nb:2 * nb]
        sem_refs = refs[2 * nb:2 * nb + nsem]
        x, y, c = _my_pos()
        for b in range(nb):
            for j, f in enumerate(CHIP_FLIPS):
                px, py = _chip_peer(x, y, f)
                cp = pltpu.make_async_remote_copy(
                    src_ref=ins[b].at[2 * px + py], dst_ref=lnd[b].at[2 * px + py],
                    send_sem=sem_refs[2 * (3 * b + j)], recv_sem=sem_refs[2 * (3 * b + j) + 1],
                    device_id=(px, py, c), device_id_type=MESH)
                cp.wait_send()
                cp.wait_recv()

    hbm = pl.BlockSpec(memory_space=pltpu.HBM)
    sem = pl.BlockSpec(memory_space=pltpu.SEMAPHORE)
    out = pl.pallas_call(
        body, name=name,
        out_shape=tuple([pltpu.HBM(b.shape, b.dtype) for b in thru] + [pltpu.HBM(b.shape, b.dtype) for b in lands]),
        in_specs=[hbm] * (2 * nb) + [sem] * nsem + [pl.BlockSpec(memory_space=pl.ANY)],
        out_specs=tuple([hbm] * (2 * nb)),
        input_output_aliases={i: i for i in range(2 * nb)},
        compiler_params=pltpu.CompilerParams(has_side_effects=pltpu.SideEffectType.DATAFLOW_SIDE_EFFECTING),
    )(*thru, *lands, *sems, after)
    return out[:nb], out[nb:]


def _bcast_start(buf, name):
    nsem = 2 * len(FLIPS)

    def body(src, land, *rest):
        sems, token = rest[:nsem], rest[-1]
        pos = _my_pos()
        for k, f in enumerate(FLIPS):
            pltpu.make_async_remote_copy(
                src_ref=src, dst_ref=land.at[_dev_index(*pos)], send_sem=sems[2 * k], recv_sem=sems[2 * k + 1],
                device_id=_flip(pos, f), device_id_type=MESH).start()
        token[...] = jnp.zeros_like(token)

    hbm = pl.BlockSpec(memory_space=pltpu.HBM)
    sem = pl.BlockSpec(memory_space=pltpu.SEMAPHORE)
    land_shape = (N_DEV,) + buf.shape
    out = pl.pallas_call(
        body, name=name,
        out_shape=(*([pltpu.SemaphoreType.DMA(())] * nsem), pltpu.HBM(buf.shape, buf.dtype),
                   pltpu.HBM(land_shape, buf.dtype), jax.ShapeDtypeStruct((8, LANES), F32)),
        in_specs=[hbm, hbm],
        out_specs=(*([sem] * nsem), hbm, hbm, pl.BlockSpec(memory_space=pltpu.VMEM)),
        input_output_aliases={0: nsem, 1: nsem + 1},
        compiler_params=pltpu.CompilerParams(has_side_effects=pltpu.SideEffectType.DATAFLOW_SIDE_EFFECTING),
    )(pltpu.with_memory_space_constraint(buf, pltpu.HBM),
      pltpu.with_memory_space_constraint(lax.empty(land_shape, buf.dtype), pltpu.HBM))
    return out[:nsem], out[nsem], out[nsem + 1], out[-1]


def _bcast_wait(sems, thru, land, after, name):
    nsem = len(sems)

    def body(src, lnd, *rest):
        sem_refs = rest[:nsem]
        pos = _my_pos()
        for k, f in enumerate(FLIPS):
            peer = _flip(pos, f)
            cp = pltpu.make_async_remote_copy(
                src_ref=src, dst_ref=lnd.at[_dev_index(*peer)], send_sem=sem_refs[2 * k],
                recv_sem=sem_refs[2 * k + 1], device_id=peer, device_id_type=MESH)
            cp.wait_send()
            cp.wait_recv()

    hbm = pl.BlockSpec(memory_space=pltpu.HBM)
    sem = pl.BlockSpec(memory_space=pltpu.SEMAPHORE)
    sent, got = pl.pallas_call(
        body, name=name,
        out_shape=(pltpu.HBM(thru.shape, thru.dtype), pltpu.HBM(land.shape, land.dtype)),
        in_specs=[hbm, hbm] + [sem] * nsem + [pl.BlockSpec(memory_space=pl.ANY)],
        out_specs=(hbm, hbm), input_output_aliases={0: 0, 1: 1},
        compiler_params=pltpu.CompilerParams(has_side_effects=pltpu.SideEffectType.DATAFLOW_SIDE_EFFECTING),
    )(thru, land, *sems, after)
    return lax.dynamic_update_slice_in_dim(got, sent[None], _dev_index(*_my_pos()), axis=0)


def _sum_slots(v, name):
    _, r, cdim = v.shape

    def body(v_ref, o_ref):
        acc = v_ref[0]
        for s in range(1, N_DEV):
            acc = acc + v_ref[s]
        o_ref[...] = acc

    return pl.pallas_call(
        body, name=name, out_shape=jax.ShapeDtypeStruct((r, cdim), F32),
        in_specs=[_full((N_DEV, r, cdim))], out_specs=_full((r, cdim)), grid=(1,),
        compiler_params=_cparams(("arbitrary",)),
    )(v)


def _mm(a, b, dims, out_dtype, tm, tn, name):
    if dims == "nn":
        (m, k), (_, n) = a.shape, b.shape
        a_spec = pl.BlockSpec((tm, k), lambda j, i: (i, 0))
        b_spec = pl.BlockSpec((k, tn), lambda j, i: (0, j))
        dn = NN
    elif dims == "nt":
        (m, k), (n, _) = a.shape, b.shape
        a_spec = pl.BlockSpec((tm, k), lambda j, i: (i, 0))
        b_spec = pl.BlockSpec((tn, k), lambda j, i: (j, 0))
        dn = NT
    else:
        (k, m), (_, n) = a.shape, b.shape
        a_spec = pl.BlockSpec((k, tm), lambda j, i: (0, i))
        b_spec = pl.BlockSpec((k, tn), lambda j, i: (0, j))
        dn = TN
    assert m % tm == 0 and n % tn == 0, (m, tm, n, tn)

    def body(a_ref, b_ref, o_ref):
        o_ref[...] = _dot(a_ref[...], b_ref[...], dn).astype(o_ref.dtype)

    return pl.pallas_call(
        body, name=name, grid=(n // tn, m // tm),
        in_specs=[a_spec, b_spec], out_specs=pl.BlockSpec((tm, tn), lambda j, i: (i, j)),
        out_shape=jax.ShapeDtypeStruct((m, n), out_dtype),
        compiler_params=_cparams(("parallel", "parallel")),
    )(a, b)


def _tiles_2d(r, cdim):
    if r % CHUNK == 0:
        return CHUNK, cdim, True
    return r, _tile(cdim, (256, 128)), False


def _dgrad_prenorm(a_list, b_list, head, x2, w, dout, tm, name):
    n_op = len(a_list)
    m, d = a_list[0].shape[0], b_list[0].shape[1]
    subs = _x_row_specs(tm, d)
    last = m // tm - 1
    rest = tm - CHUNK

    def body(*refs):
        a_refs, b_refs = refs[:n_op], refs[n_op:2 * n_op]
        head_ref = refs[2 * n_op]
        x_refs = refs[2 * n_op + 1:2 * n_op + 1 + len(subs)]
        w_ref, dout_ref, gx_ref, ghead_ref, gw_ref, dh_buf, sem = refs[2 * n_op + 1 + len(subs):]
        i = pl.program_id(0)

        def first_copy():
            return pltpu.make_async_copy(dh_buf.at[pl.ds(CHUNK, rest)], gx_ref.at[pl.ds(0, rest)], sem)

        def later_copy(step):
            return pltpu.make_async_copy(dh_buf, gx_ref.at[pl.ds(pl.multiple_of(step * tm - CHUNK, CHUNK), tm)], sem)

        @pl.when(i == 0)
        def _():
            gw_ref[...] = jnp.zeros_like(gw_ref)

        du = _dot(a_refs[0][...], b_refs[0][...])
        for k in range(1, n_op):
            du = du + _dot(a_refs[k][...], b_refs[k][...])
        first = jnp.where(i == 0, head_ref[...], x_refs[0][...])
        h = jnp.concatenate([first] + [r[...] for r in x_refs[1:]], axis=0)
        rstd = lax.rsqrt(jnp.mean(h * h, axis=-1, keepdims=True) + EPS)
        xhat = h * rstd
        dxh = du * w_ref[...]
        dh = rstd * (dxh - xhat * jnp.mean(dxh * xhat, axis=-1, keepdims=True)) + dout_ref[...]
        gw_ref[0:1, :] += jnp.sum(du * xhat, axis=0, keepdims=True)

        if rest and last >= 1:
            @pl.when(i == 1)
            def _():
                first_copy().wait()

        @pl.when(i >= (2 if rest else 1))
        def _():
            later_copy(i - 1).wait()

        dh_buf[...] = dh

        @pl.when(i == 0)
        def _():
            ghead_ref[...] = dh_buf[0:CHUNK, :]
            if rest:
                first_copy().start()
                if last == 0:
                    first_copy().wait()

        @pl.when(i >= 1)
        def _():
            later_copy(i).start()

        if last >= 1:
            @pl.when(i == last)
            def _():
                later_copy(i).wait()

    once = lambda b: pl.BlockSpec(b.shape, lambda i: (0, 0), pipeline_mode=pl.Buffered(1))
    row = lambda width: pl.BlockSpec((tm, width), lambda i: (i, 0))
    return pl.pallas_call(
        body, name=name, grid=(m // tm,),
        in_specs=([row(a.shape[1]) for a in a_list] + [once(b) for b in b_list]
                  + [_full((CHUNK, d))] + subs + [_full((1, d)), row(d)]),
        out_specs=[pl.BlockSpec(memory_space=pl.ANY), _full((CHUNK, d)), _full((8, d))],
        out_shape=[jax.ShapeDtypeStruct((m - CHUNK, d), F32), jax.ShapeDtypeStruct((CHUNK, d), F32),
                   jax.ShapeDtypeStruct((8, d), F32)],
        scratch_shapes=[pltpu.VMEM((tm, d), F32), pltpu.SemaphoreType.DMA],
        compiler_params=_cparams(("arbitrary",)),
    )(*a_list, *b_list, head, *([x2] * len(subs)), w, dout)


def _tile(n, prefs):
    for t in prefs:
        if n % t == 0:
            return t
    return n


def _rows3(i):
    return jnp.maximum(3 * i - 1, 0), 3 * i, 3 * i + 1


def _x_row_specs(tm, d):
    if tm == CHUNK:
        return [pl.BlockSpec((CHUNK, d), lambda i: (jnp.maximum(i - 1, 0), 0))]
    return [pl.BlockSpec((CHUNK, d), functools.partial(lambda i, k: (_rows3(i)[k], 0), k=k)) for k in range(3)]


def _prenorm_fwd(head, x2, w, tm):
    p, d = x2.shape[0] + CHUNK, x2.shape[1]
    subs = _x_row_specs(tm, d)

    def body(head_ref, *rest):
        x_refs, (w_ref, u_ref) = rest[:len(subs)], rest[len(subs):]
        i = pl.program_id(0)
        first = jnp.where(i == 0, head_ref[...], x_refs[0][...])
        h = jnp.concatenate([first] + [r[...] for r in x_refs[1:]], axis=0)
        ms = jnp.mean(h * h, axis=-1, keepdims=True)
        u_ref[...] = (h * lax.rsqrt(ms + EPS) * w_ref[...]).astype(BF16)

    return pl.pallas_call(
        body, name="prenorm_fwd", grid=(p // tm,),
        in_specs=[_full((CHUNK, d))] + subs + [_full((1, d))],
        out_specs=pl.BlockSpec((tm, d), lambda i: (i, 0)),
        out_shape=jax.ShapeDtypeStruct((p, d), BF16),
        compiler_params=_cparams(("arbitrary",)),
    )(head, *([x2] * len(subs)), w)


def _conv_pre(ext_ref, cw_ref, cb_ref):
    pre = cb_ref[...] + cw_ref[CONV_K - 1:CONV_K, :] * ext_ref[8:8 + CHUNK, :]
    for j in range(1, CONV_K):
        pre = pre + cw_ref[CONV_K - 1 - j:CONV_K - j, :] * ext_ref[8 - j:8 - j + CHUNK, :]
    return pre


def _ssd_scalars(dtf_ref, brow_ref, alog_ref, rowmask, hs, ha, tri):
    lane = lax.broadcasted_iota(jnp.int32, (1, LANES), 1)
    is_dt = lane < hs
    is_f = (lane >= hs) & (lane < hs + ha)
    dtr = dtf_ref[...] + brow_ref[...]
    sp = _softplus(dtr)
    dt = jnp.where(is_dt, sp, 0.0) * rowmask
    logf = jnp.where(is_f, jnp.minimum(dtr, 0.0) - jnp.log(1.0 + jnp.exp(-jnp.abs(dtr))), 0.0) * rowmask
    a_row = jnp.where(is_dt, -jnp.exp(alog_ref[...]), 0.0)
    run = _dot_tri(tri, dt * a_row + logf)
    return dtr, dt, a_row, run, is_dt, is_f


def _tri_mats():
    r = lax.broadcasted_iota(jnp.int32, (CHUNK, CHUNK), 0)
    c = lax.broadcasted_iota(jnp.int32, (CHUNK, CHUNK), 1)
    return r, c


def _ssd_fwd(xbc, z, dtf, conv_w, conv_b, brow, alog, dskip_l, ssd_norm, sel_t, hs, ha):
    p, cd = xbc.shape
    ds = z.shape[1]
    ns = (cd - ds) // (2 * SSD_GROUPS)
    gw = ds // SSD_GROUPS
    nch = p // CHUNK
    hpg = hs // SSD_GROUPS

    def body(xbc_ref, halo_ref, z_ref, dtf_ref, cw_ref, cb_ref, brow_ref, alog_ref, dsk_ref, nrm_ref, selt_ref,
             y_ref, yssd_ref, hin_ref, cf_ref, pre_ref, st_ref, carry_ref, yacc_ref, xc_s, ex_s, xdtb_s, xwb_s, ext_s):
        c = pl.program_id(0)

        @pl.when(c == 0)
        def _():
            st_ref[...] = jnp.zeros_like(st_ref)
            carry_ref[...] = jnp.zeros_like(carry_ref)

        rows = lax.broadcasted_iota(jnp.int32, (CHUNK, 1), 0)
        rowmask = jnp.where((rows >= PADN) | (c > 0), 1.0, 0.0)
        ri, ci = _tri_mats()
        causal = ri >= ci
        tri = jnp.where(causal, 1.0, 0.0).astype(BF16)

        ext_s[0:8, :] = halo_ref[...].astype(F32)[HALO - 8:, :] * jnp.where(c > 0, 1.0, 0.0)
        ext_s[8:, :] = xbc_ref[...].astype(F32)
        pre = _conv_pre(ext_s, cw_ref, cb_ref)
        pre_ref[...] = pre.astype(BF16)
        xc_s[...] = pre * _sigmoid(pre) * rowmask

        dtr, dt, a_row, run, is_dt, is_f = _ssd_scalars(dtf_ref, brow_ref, alog_ref, rowmask, hs, ha, tri)
        cf = run + carry_ref[...]
        cf_ref[...] = cf
        carry_ref[...] = jnp.where(is_f, cf[CHUNK - 1:CHUNK, :], 0.0)
        cs = jnp.where(is_dt, run, 0.0)
        cl = cs[CHUNK - 1:CHUNK, :]
        selt = selt_ref[...]
        ex_s[...] = _dot_sel(jnp.exp(cs), selt)
        cdec_x = _dot_sel(jnp.broadcast_to(jnp.exp(cl), (8, LANES)), selt)[0:1, :]
        cs_t = cs.T
        xdt = xc_s[:, :ds] * _dot_sel(dt, selt)
        xdtb_s[...] = xdt.astype(BF16)
        xwb_s[...] = (xdt * _dot_sel(jnp.exp(cl - cs), selt)).astype(BF16)

        lane = lax.broadcasted_iota(jnp.int32, (1, LANES), 1)
        half0 = lane < HEAD_DIM
        for g in range(SSD_GROUPS):
            bg = xc_s[:, ds + g * ns: ds + (g + 1) * ns].astype(BF16)
            cg = xc_s[:, ds + SSD_GROUPS * ns + g * ns: ds + SSD_GROUPS * ns + (g + 1) * ns].astype(BF16)
            gm = _dot(cg, bg, NT)
            gs = slice(g * gw, (g + 1) * gw)
            stg = st_ref[:, gs]
            stg_b = stg.astype(BF16)
            hin_ref[0, :, gs] = stg_b
            yoff = _dot(cg, stg_b) * ex_s[:, gs]
            for pr in range(gw // LANES):
                sl = slice(g * gw + pr * LANES, g * gw + (pr + 1) * LANES)
                xp = xdtb_s[:, sl]
                yd = jnp.zeros((CHUNK, LANES), F32)
                for j in range(2):
                    h = g * hpg + 2 * pr + j
                    seg = cs[:, h:h + 1] - cs_t[h:h + 1, :]
                    m = jnp.where(causal, gm * jnp.exp(jnp.minimum(seg, 0.0)), 0.0).astype(BF16)
                    sel = half0 if j == 0 else jnp.logical_not(half0)
                    yd = yd + _dot(m, jnp.where(sel, xp, jnp.zeros_like(xp)))
                yacc_ref[:, sl] = yd + yoff[:, pr * LANES:(pr + 1) * LANES] + dsk_ref[:, sl] * xc_s[:, sl]
            st_ref[:, gs] = stg * cdec_x[:, gs] + _dot(bg, xwb_s[:, gs], TN)

        y = yacc_ref[...]
        y_ref[...] = y.astype(BF16)
        zf = z_ref[...].astype(F32)
        u = y * zf * _sigmoid(zf)
        for g in range(SSD_GROUPS):
            gs = slice(g * gw, (g + 1) * gw)
            ug = u[:, gs]
            ms = jnp.mean(ug * ug, axis=-1, keepdims=True)
            yssd_ref[:, gs] = (ug * lax.rsqrt(ms + EPS) * nrm_ref[:, gs]).astype(BF16)

    rb = CHUNK // HALO
    return pl.pallas_call(
        body, name="ssd_fwd", grid=(nch,),
        in_specs=[pl.BlockSpec((CHUNK, cd), lambda c: (c, 0)),
                  pl.BlockSpec((HALO, cd), lambda c: (jnp.maximum(c * rb - 1, 0), 0)),
                  pl.BlockSpec((CHUNK, ds), lambda c: (c, 0)),
                  pl.BlockSpec((CHUNK, LANES), lambda c: (c, 0)),
                  _full((CONV_K, cd)), _full((1, cd)), _full((1, LANES)), _full((1, LANES)),
                  _full((1, ds)), _full((1, ds)), _full((LANES, ds))],
        out_specs=[pl.BlockSpec((CHUNK, ds), lambda c: (c, 0)), pl.BlockSpec((CHUNK, ds), lambda c: (c, 0)),
                   pl.BlockSpec((1, ns, ds), lambda c: (c, 0, 0)), pl.BlockSpec((CHUNK, LANES), lambda c: (c, 0)),
                   pl.BlockSpec((CHUNK, cd), lambda c: (c, 0))],
        out_shape=[jax.ShapeDtypeStruct((p, ds), BF16), jax.ShapeDtypeStruct((p, ds), BF16),
                   jax.ShapeDtypeStruct((nch, ns, ds), BF16), jax.ShapeDtypeStruct((p, LANES), F32),
                   jax.ShapeDtypeStruct((p, cd), BF16)],
        scratch_shapes=[pltpu.VMEM((ns, ds), F32), pltpu.VMEM((1, LANES), F32), pltpu.VMEM((CHUNK, ds), F32),
                        pltpu.VMEM((CHUNK, cd), F32), pltpu.VMEM((CHUNK, ds), F32),
                        pltpu.VMEM((CHUNK, ds), BF16), pltpu.VMEM((CHUNK, ds), BF16),
                        pltpu.VMEM((8 + CHUNK, cd), F32)],
        compiler_params=_cparams(("arbitrary",)),
    )(xbc, xbc, z, dtf, conv_w, conv_b, brow, alog, dskip_l, ssd_norm, sel_t)


def _ssd_bwd(dyssd, y, z, xbc, pre, dtf, hin, dcf, conv_w, brow, alog, dskip_l, ssd_norm, sel_t, sel, hs, ha):
    p, cd = xbc.shape
    ds = z.shape[1]
    ns = (cd - ds) // (2 * SSD_GROUPS)
    gw = ds // SSD_GROUPS
    nch = p // CHUNK
    hpg = hs // SSD_GROUPS

    def body(dyssd_ref, y_ref, z_ref, xbc_ref, pre_ref, dtf_ref, hin_ref, dcf_ref, cw_ref, brow_ref,
             alog_ref, dsk_ref, nrm_ref, selt_ref, sel_ref,
             dxbc_ref, dz_ref, ddtf_ref, gcw_ref, gcb_ref, gnrm_ref, gsm_ref,
             dst_ref, nxt_ref, fcar_ref, gdsk_ref, dxc_ref, xc_s, dsl_s, dtx_s, ex_s, wx_s, dy_s, xdtb_s, xwb_s,
             dyb_s, dyeb_s):
        step = pl.program_id(0)
        c = nch - 1 - step

        @pl.when(step == 0)
        def _():
            dst_ref[...] = jnp.zeros_like(dst_ref)
            nxt_ref[...] = jnp.zeros_like(nxt_ref)
            fcar_ref[...] = jnp.zeros_like(fcar_ref)
            gdsk_ref[...] = jnp.zeros_like(gdsk_ref)
            gcw_ref[...] = jnp.zeros_like(gcw_ref)
            gcb_ref[...] = jnp.zeros_like(gcb_ref)
            gnrm_ref[...] = jnp.zeros_like(gnrm_ref)
            gsm_ref[...] = jnp.zeros_like(gsm_ref)

        rows = lax.broadcasted_iota(jnp.int32, (CHUNK, 1), 0)
        rowmask = jnp.where((rows >= PADN) | (c > 0), 1.0, 0.0)
        ri, ci = _tri_mats()
        causal = ri >= ci
        anti = ci >= ri
        tri = jnp.where(causal, 1.0, 0.0).astype(BF16)
        rtri = jnp.where(anti, 1.0, 0.0).astype(BF16)

        pre = pre_ref[...].astype(F32)
        sg = _sigmoid(pre)
        xc_s[...] = pre * sg * rowmask
        dsl_s[...] = sg * (1.0 + pre * (1.0 - sg)) * rowmask

        dtr, dt, a_row, run, is_dt, is_f = _ssd_scalars(dtf_ref, brow_ref, alog_ref, rowmask, hs, ha, tri)
        cs = jnp.where(is_dt, run, 0.0)
        cl = cs[CHUNK - 1:CHUNK, :]
        selt = selt_ref[...]
        selm = sel_ref[...]
        dtx_s[...] = _dot_sel(dt, selt)
        ex_s[...] = _dot_sel(jnp.exp(cs), selt)
        wx_s[...] = _dot_sel(jnp.exp(cl - cs), selt)
        cdec = jnp.exp(cl)
        cdec_x = _dot_sel(jnp.broadcast_to(cdec, (8, LANES)), selt)[0:1, :]
        cs_t = cs.T
        xdt = xc_s[:, :ds] * dtx_s[...]
        xdtb_s[...] = xdt.astype(BF16)
        xwb_s[...] = (xdt * wx_s[...]).astype(BF16)

        yv = y_ref[...].astype(F32)
        zf = z_ref[...].astype(F32)
        sz = _sigmoid(zf)
        u = yv * zf * sz
        dyo = dyssd_ref[...].astype(F32)
        du_parts = []
        for g in range(SSD_GROUPS):
            gs = slice(g * gw, (g + 1) * gw)
            ug = u[:, gs]
            rstd = lax.rsqrt(jnp.mean(ug * ug, axis=-1, keepdims=True) + EPS)
            yhat = ug * rstd
            dyg = dyo[:, gs]
            gnrm_ref[0:1, gs] += jnp.sum(dyg * yhat, axis=0, keepdims=True)
            dyh = dyg * nrm_ref[:, gs]
            du_parts.append(rstd * (dyh - yhat * jnp.mean(dyh * yhat, axis=-1, keepdims=True)))
        du = jnp.concatenate(du_parts, axis=1)
        dy = du * zf * sz
        dz_ref[...] = (du * yv * sz * (1.0 + zf * (1.0 - sz))).astype(BF16)
        dy_s[...] = dy
        dyb_s[...] = dy.astype(BF16)
        dyeb_s[...] = (dy * ex_s[...]).astype(BF16)
        gdsk_ref[...] += jnp.sum(dy * xc_s[:, :ds], axis=0, keepdims=True)
        lane = lax.broadcasted_iota(jnp.int32, (1, LANES), 1)
        half0 = lane < HEAD_DIM
        x_parts, yo_parts, t4_parts = [], [], []
        dcs = jnp.zeros((CHUNK, LANES), F32)
        for g in range(SSD_GROUPS):
            gs = slice(g * gw, (g + 1) * gw)
            bsl = slice(ds + g * ns, ds + (g + 1) * ns)
            csl = slice(ds + SSD_GROUPS * ns + g * ns, ds + SSD_GROUPS * ns + (g + 1) * ns)
            bg = xc_s[:, bsl].astype(BF16)
            cg = xc_s[:, csl].astype(BF16)
            gm = _dot(cg, bg, NT)
            gm_t = _dot(bg, cg, NT)
            stg_b = hin_ref[0, :, gs]
            dstg = dst_ref[:, gs]
            dstg_b = dstg.astype(BF16)
            t4_parts.append(jnp.sum(dstg * stg_b.astype(F32), axis=0, keepdims=True))
            zst = _dot(bg, dstg_b) * wx_s[:, gs]
            x_parts.append(xc_s[:, gs] * dtx_s[:, gs] * zst)
            yo_parts.append(dy_s[:, gs] * (_dot(cg, stg_b) * ex_s[:, gs]))
            dgsum = jnp.zeros((CHUNK, CHUNK), F32)
            dgtsum = jnp.zeros((CHUNK, CHUNK), F32)
            for pr in range(gw // LANES):
                sl = slice(g * gw + pr * LANES, g * gw + (pr + 1) * LANES)
                xp = xdtb_s[:, sl]
                dyp = dyb_s[:, sl]
                dxd = zst[:, pr * LANES:(pr + 1) * LANES]
                for j in range(2):
                    h = g * hpg + 2 * pr + j
                    sel_l = half0 if j == 0 else jnp.logical_not(half0)
                    seg = cs[:, h:h + 1] - cs_t[h:h + 1, :]
                    lm = jnp.where(causal, jnp.exp(jnp.minimum(seg, 0.0)), 0.0)
                    lmt = lm.T
                    dyp_m = jnp.where(sel_l, dyp, jnp.zeros_like(dyp))
                    xp_m = jnp.where(sel_l, xp, jnp.zeros_like(xp))
                    dxd = dxd + _dot((gm_t * lmt).astype(BF16), dyp_m)
                    dg = _dot(dyp_m, xp, NT) * lm
                    dgt = _dot(xp_m, dyp, NT) * lmt
                    dgsum = dgsum + dg
                    dgtsum = dgtsum + dgt
                    qrow = (jnp.sum(dg * gm, axis=1, keepdims=True) - jnp.sum(dgt * gm_t, axis=1, keepdims=True))
                    dcs = dcs + jnp.where(lane == h, qrow, 0.0)
                dxc_ref[:, sl] = dxd
            dxc_ref[:, csl] = _dot(dgsum.astype(BF16), bg) + _dot(dyeb_s[:, gs], stg_b, NT)
            dxc_ref[:, bsl] = _dot(dgtsum.astype(BF16), cg) + _dot(xwb_s[:, gs], dstg_b, NT)
            dst_ref[:, gs] = dstg * cdec_x[:, gs] + _dot(cg, dyeb_s[:, gs], TN)

        dxdt = dxc_ref[:, :ds]
        xst = _dot_sel(jnp.concatenate(x_parts, axis=1), selm)
        yo = _dot_sel(jnp.concatenate(yo_parts, axis=1), selm)
        t4 = _dot_sel(jnp.concatenate([jnp.concatenate(t4_parts, axis=1), jnp.zeros((7, ds), F32)], axis=0), selm)
        dcl = jnp.sum(xst, axis=0, keepdims=True) + cdec * t4[0:1, :]
        dcs = dcs + yo - xst + jnp.where(rows == CHUNK - 1, dcl, 0.0)
        da_ = _dot_tri(rtri, dcs)
        ddt = _dot_sel(dxdt * xc_s[:, :ds], selm) + da_ * a_row
        dcf_blk = dcf_ref[...]
        dlogf = _dot_tri(rtri, dcf_blk) + fcar_ref[...]
        fcar_ref[...] += jnp.sum(dcf_blk, axis=0, keepdims=True)
        sgd = _sigmoid(dtr)
        ddtf = (jnp.where(is_dt, ddt * sgd, 0.0) + jnp.where(is_f, dlogf * (1.0 - sgd), 0.0)) * rowmask
        ddtf_ref[...] = ddtf
        gsm_ref[0:1, :] += jnp.sum(ddtf, axis=0, keepdims=True)
        gsm_ref[1:2, :] += jnp.sum(da_ * dt, axis=0, keepdims=True) * a_row

        dxc_ref[:, :ds] = dxdt * dtx_s[...] + dsk_ref[...] * dy_s[...]
        dpre = dxc_ref[...] * dsl_s[...]
        nxt_ref[0:CHUNK, :] = dpre
        gcb_ref[0:1, :] += jnp.sum(dpre, axis=0, keepdims=True)
        xr = xbc_ref[...].astype(F32)
        gcw_ref[CONV_K - 1:CONV_K, :] += jnp.sum(dpre * xr, axis=0, keepdims=True)
        dxr = cw_ref[CONV_K - 1:CONV_K, :] * dpre
        for j in range(1, CONV_K):
            up = nxt_ref[j:j + CHUNK, :]
            gcw_ref[CONV_K - 1 - j:CONV_K - j, :] += jnp.sum(up * xr, axis=0, keepdims=True)
            dxr = dxr + cw_ref[CONV_K - 1 - j:CONV_K - j, :] * up
        nxt_ref[CHUNK:, :] = dpre[0:8, :]
        dxbc_ref[...] = dxr.astype(BF16)

        @pl.when(step == nch - 1)
        def _():
            gsm_ref[2:3, :] = _dot_sel(jnp.broadcast_to(gdsk_ref[...], (8, ds)), selm)[0:1, :]

    rev = lambda s: nch - 1 - s
    blk = lambda w: pl.BlockSpec((CHUNK, w), lambda s: (rev(s), 0))
    return pl.pallas_call(
        body, name="ssd_bwd", grid=(nch,),
        in_specs=[blk(ds), blk(ds), blk(ds), blk(cd), blk(cd),
                  blk(LANES), pl.BlockSpec((1, ns, ds), lambda s: (rev(s), 0, 0)), blk(LANES),
                  _full((CONV_K, cd)), _full((1, LANES)), _full((1, LANES)),
                  _full((1, ds)), _full((1, ds)), _full((LANES, ds)), _full((ds, LANES))],
        out_specs=[blk(cd), blk(ds), blk(LANES), _full((8, cd)), _full((8, cd)), _full((8, ds)), _full((8, LANES))],
        out_shape=[jax.ShapeDtypeStruct((p, cd), BF16), jax.ShapeDtypeStruct((p, ds), BF16),
                   jax.ShapeDtypeStruct((p, LANES), F32), jax.ShapeDtypeStruct((8, cd), F32),
                   jax.ShapeDtypeStruct((8, cd), F32), jax.ShapeDtypeStruct((8, ds), F32),
                   jax.ShapeDtypeStruct((8, LANES), F32)],
        scratch_shapes=[pltpu.VMEM((ns, ds), F32), pltpu.VMEM((CHUNK + 8, cd), F32), pltpu.VMEM((1, LANES), F32),
                        pltpu.VMEM((1, ds), F32), pltpu.VMEM((CHUNK, cd), F32),
                        pltpu.VMEM((CHUNK, cd), F32), pltpu.VMEM((CHUNK, cd), F32),
                        pltpu.VMEM((CHUNK, ds), F32), pltpu.VMEM((CHUNK, ds), F32), pltpu.VMEM((CHUNK, ds), F32),
                        pltpu.VMEM((CHUNK, ds), F32), pltpu.VMEM((CHUNK, ds), BF16), pltpu.VMEM((CHUNK, ds), BF16),
                        pltpu.VMEM((CHUNK, ds), BF16), pltpu.VMEM((CHUNK, ds), BF16)],
        compiler_params=_cparams(("arbitrary",)),
    )(dyssd, y, z, xbc, pre, dtf, hin, dcf, conv_w, brow, alog, dskip_l, ssd_norm, sel_t, sel)


def _attn_fwd(q, k, v, ck, blk):
    p, da = q.shape
    npair, nkb = ck.shape[0], ck.shape[1]
    scale = 1.0 / math.sqrt(HEAD_DIM)

    def body(q_ref, k_ref, v_ref, ck_ref, o_ref, lse_ref):
        i = pl.program_id(1)
        lane = lax.broadcasted_iota(jnp.int32, (1, LANES), 1)
        sels = [lane < HEAD_DIM, lane >= HEAD_DIM]
        ones = [jnp.where(lane == HEAD_DIM, 1.0, 0.0).astype(BF16), jnp.where(lane == 0, 1.0, 0.0).astype(BF16)]
        qb = q_ref[...] * scale

        def step(kb, carry, masked, nk=1):
            r0 = pl.multiple_of(kb * blk, blk)
            ks = k_ref[pl.ds(r0, nk * blk), :]
            vs = v_ref[pl.ds(r0, nk * blk), :]
            kk = jnp.concatenate([jnp.where(sel, ks, jnp.zeros_like(ks)) for sel in sels], axis=0)
            s_both = _dot(qb, kk, NT)
            out = []
            for j in range(2):
                m, acc = carry[2 * j], carry[2 * j + 1]
                ckr = jnp.concatenate([ck_ref[0, kb + t, j:j + 1, :] for t in range(nk)], axis=1)
                s = s_both[:, j * nk * blk:(j + 1) * nk * blk] - ckr
                if masked:
                    col = lax.broadcasted_iota(jnp.int32, (blk, nk * blk), 1) - (nk - 1) * blk
                    s = jnp.where(col <= lax.broadcasted_iota(jnp.int32, (blk, nk * blk), 0), s, NEG)
                mn = jnp.maximum(m, jnp.max(s, axis=-1, keepdims=True))
                pr = jnp.exp(s - mn).astype(BF16)
                acc = jnp.exp(m - mn) * acc + _dot(pr, jnp.where(sels[j], vs, ones[j]))
                out += [mn, acc]
            return tuple(out)

        init = (jnp.full((blk, 1), NEG, F32), jnp.zeros((blk, LANES), F32)) * 2

        def finish(carry):
            m0, a0, m1, a1 = carry
            l0 = a0[:, HEAD_DIM:HEAD_DIM + 1]
            l1 = a1[:, 0:1]
            o_ref[...] = jnp.where(sels[0], a0 / l0, a1 / l1).astype(BF16)
            lse_ref[...] = jnp.where(sels[0], m0 + jnp.log(l0), m1 + jnp.log(l1))

        @pl.when(i == 0)
        def _():
            finish(step(0, init, True))

        def sweep(last):
            below = i + 1 - last
            n4 = below // 4
            n2 = (below - 4 * n4) // 2
            carry = lax.fori_loop(0, n4, lambda t, c: step(4 * t, c, False, 4), init)
            carry = lax.fori_loop(0, n2, lambda t, c: step(4 * n4 + 2 * t, c, False, 2), carry)
            carry = lax.fori_loop(4 * n4 + 2 * n2, below, lambda kb, c: step(kb, c, False), carry)
            finish(step(below, carry, True, last))

        @pl.when((i > 0) & (i < 3))
        def _():
            sweep(2)

        @pl.when(i >= 3)
        def _():
            sweep(4)

    return pl.pallas_call(
        body, name="attn_fwd", grid=(npair, p // blk),
        in_specs=[pl.BlockSpec((blk, LANES), lambda h, i: (i, h)),
                  pl.BlockSpec((p, LANES), lambda h, i: (0, h)), pl.BlockSpec((p, LANES), lambda h, i: (0, h)),
                  pl.BlockSpec((1, nkb, 8, blk), lambda h, i: (h, 0, 0, 0))],
        out_specs=[pl.BlockSpec((blk, LANES), lambda h, i: (i, h)), pl.BlockSpec((blk, LANES), lambda h, i: (i, h))],
        out_shape=[jax.ShapeDtypeStruct((p, da), BF16), jax.ShapeDtypeStruct((p, da), F32)],
        compiler_params=_cparams(("parallel", "arbitrary")),
    )(q, k, v, ck)


def _attn_bwd(q, k, v, o, do, lse_rep, ck, blk):
    p, da = q.shape
    npair, nkb = ck.shape[0], ck.shape[1]
    nq = p // blk
    scale = 1.0 / math.sqrt(HEAD_DIM)

    def body(k_ref, v_ref, q_ref, do_ref, o_ref, lse_ref, ck_ref, dk_ref, dv_ref, dq_ref, dcs_ref, rsum_ref, dq_acc):
        jb = pl.program_id(1)

        @pl.when(jb == 0)
        def _():
            dq_acc[...] = jnp.zeros_like(dq_acc)

        ks = k_ref[...]
        vs = v_ref[...]
        lane = lax.broadcasted_iota(jnp.int32, (1, LANES), 1)
        sels = [lane < HEAD_DIM, lane >= HEAD_DIM]
        ones = [jnp.where(lane == HEAD_DIM, 1.0, 0.0).astype(BF16), jnp.where(lane == 0, 1.0, 0.0).astype(BF16)]
        kss = ks * scale
        kmo = [jnp.where(sels[j], kss, ones[j]) for j in range(2)]

        def step(ib, carry, masked, nb=1):
            rows = nb * blk
            r0 = pl.multiple_of(ib * blk, blk)
            qb = q_ref[pl.ds(r0, rows), :] * scale
            dob = do_ref[pl.ds(r0, rows), :]
            prod = dob.astype(F32) * o_ref[pl.ds(r0, rows), :].astype(F32)
            out = []
            for j in range(2):
                dk, dv = carry[2 * j], carry[2 * j + 1]
                qm = jnp.where(sels[j], qb, jnp.zeros_like(qb))
                dom = jnp.where(sels[j], dob, jnp.zeros_like(dob))
                lse = lse_ref[pl.ds(r0, rows), HEAD_DIM * j:HEAD_DIM * j + 1]
                dlt = jnp.sum(jnp.where(sels[j], prod, 0.0), axis=-1, keepdims=True)
                s = _dot(qm, ks, NT) - ck_ref[0, 0, j:j + 1, :] - lse
                pm = jnp.exp(jnp.minimum(s, 0.0))
                if masked:
                    causal = (lax.broadcasted_iota(jnp.int32, (rows, blk), 1)
                              <= lax.broadcasted_iota(jnp.int32, (rows, blk), 0))
                    pm = jnp.where(causal, pm, 0.0)
                ds_b = (pm * (_dot(dom, vs, NT) - dlt)).astype(BF16)
                dv = dv + _dot(pm.astype(BF16), dom, TN)
                dk = dk + _dot(ds_b, jnp.where(sels[j], qb, ones[j]), TN)
                dq_acc[pl.ds(r0, rows), LANES * j:LANES * (j + 1)] += _dot(ds_b, kmo[j])
                out += [dk, dv]
            return tuple(out)

        pair8 = lambda c0, c1: jnp.where(lane == 0, c0, jnp.where(lane == 1, c1, 0.0)).T[0:8]
        zero = jnp.zeros((blk, LANES), F32)
        init = (zero, zero, zero, zero)

        def finish(carry):
            dk0, dv0, dk1, dv1 = carry
            dk_ref[...] = jnp.where(sels[0], dk0, dk1).astype(BF16)
            dv_ref[...] = (dv0 + dv1).astype(BF16)
            dcs_ref[0] = pair8(dk0[:, HEAD_DIM:HEAD_DIM + 1], dk1[:, 0:1])

        @pl.when(jb == nq - 1)
        def _():
            finish(step(jb, init, True))

        @pl.when(jb < nq - 1)
        def _():
            carry = step(jb, init, True, 2)
            n4 = (nq - 2 - jb) // 4
            n2 = (nq - 2 - jb - 4 * n4) // 2
            carry = lax.fori_loop(0, n4, lambda t, c: step(jb + 2 + 4 * t, c, False, 4), carry)
            carry = lax.fori_loop(0, n2, lambda t, c: step(jb + 2 + 4 * n4 + 2 * t, c, False, 2), carry)
            finish(lax.fori_loop(jb + 2 + 4 * n4 + 2 * n2, nq, lambda ib, c: step(ib, c, False), carry))

        @pl.when(jb == nkb - 1)
        def _():
            a0 = dq_acc[:, :LANES]
            a1 = dq_acc[:, LANES:]
            dq_ref[...] = jnp.where(sels[0], a0, a1).astype(BF16)
            rsum_ref[0] = pair8(a0[:, HEAD_DIM:HEAD_DIM + 1], a1[:, 0:1])

    colblk = pl.BlockSpec((blk, LANES), lambda h, j: (j, h))
    colfull = pl.BlockSpec((p, LANES), lambda h, j: (0, h))
    ckspec = pl.BlockSpec((1, 1, 8, blk), lambda h, j: (h, j, 0, 0))
    return pl.pallas_call(
        body, name="attn_bwd", grid=(npair, nkb),
        in_specs=[colblk, colblk, colfull, colfull, colfull, colfull, ckspec],
        out_specs=[colblk, colblk, colfull, pl.BlockSpec((1, 8, blk), lambda h, j: (h, 0, j)),
                   pl.BlockSpec((1, 8, p), lambda h, j: (h, 0, 0))],
        out_shape=[jax.ShapeDtypeStruct((p, da), BF16), jax.ShapeDtypeStruct((p, da), BF16),
                   jax.ShapeDtypeStruct((p, da), BF16), jax.ShapeDtypeStruct((npair, 8, p), F32),
                   jax.ShapeDtypeStruct((npair, 8, p), F32)],
        scratch_shapes=[pltpu.VMEM((p, 2 * LANES), F32)],
        compiler_params=_cparams(("parallel", "arbitrary")),
    )(k, v, q, do, o, lse_rep, ck)


def _tail_fwd(yssd, o, zatt, graw, head, x2, tgt2, wps, wpa, wout, gate_bias, norm_post, tm):
    p, ds = yssd.shape
    da = o.shape[1]
    d = x2.shape[1]
    nsub = tm // CHUNK

    def body(yssd_ref, o_ref, zatt_ref, g_ref, head_ref, *rest):
        x_refs, t_refs = rest[:nsub], rest[nsub:2 * nsub]
        (wps_ref, wpa_ref, wout_ref, gb_ref, np_ref,
         yatt_ref, mrg_ref, a_ref, b_ref, dzo_ref, dout_ref, red_ref) = rest[2 * nsub:]
        i = pl.program_id(0)

        @pl.when(i == 0)
        def _():
            red_ref[...] = jnp.zeros_like(red_ref)

        first = jnp.where(i == 0, head_ref[...], x_refs[0][...])
        h = jnp.concatenate([first] + [r[...] for r in x_refs[1:]], axis=0)
        tgt = jnp.concatenate([r[...] for r in t_refs], axis=0)
        rows = lax.broadcasted_iota(jnp.int32, (tm, 1), 0)
        valid = jnp.where((i > 0) | (rows >= CHUNK), 1.0, 0.0)
        ob = o_ref[...].astype(F32)
        za = zatt_ref[...].astype(F32)
        yatt_b = (ob * za * _sigmoid(za)).astype(BF16)
        yatt_ref[...] = yatt_b
        a = _dot(yssd_ref[...], wps_ref[...])
        b = _dot(yatt_b, wpa_ref[...])
        a_ref[...] = a.astype(BF16)
        b_ref[...] = b.astype(BF16)
        gr = g_ref[...].astype(F32) + gb_ref[...]
        mrg_b = (_sigmoid(gr[:, :d]) * a + _sigmoid(gr[:, d:]) * b).astype(BF16)
        mrg_ref[...] = mrg_b
        zo = _dot(mrg_b, wout_ref[...])
        rstd = lax.rsqrt(jnp.mean(zo * zo, axis=-1, keepdims=True) + EPS)
        zh = zo * rstd
        npw = np_ref[...]
        err = (h + zh * npw - tgt) * valid
        dout = err * (1.0 / d)
        dout_ref[...] = dout
        dzh = dout * npw
        dzo_ref[...] = (rstd * (dzh - zh * jnp.mean(dzh * zh, axis=-1, keepdims=True))).astype(BF16)
        red_ref[0:1, :] += jnp.sum(dout * zh, axis=0, keepdims=True)
        red_ref[1:2, 0:1] += jnp.sum(jnp.sum(err * err, axis=1, keepdims=True), axis=0, keepdims=True) * (0.5 / d)

    row = lambda w: pl.BlockSpec((tm, w), lambda i: (i, 0))
    once = lambda shape: pl.BlockSpec(shape, lambda i: (0,) * len(shape), pipeline_mode=pl.Buffered(1))
    subs = _x_row_specs(tm, d)
    sd = jax.ShapeDtypeStruct
    return pl.pallas_call(
        body, name="tail_fwd", grid=(p // tm,),
        in_specs=[row(ds), row(da), row(da), row(2 * d), _full((CHUNK, d))] + subs + subs
                 + [once((ds, d)), once((da, d)), once((d, d)), _full((1, 2 * d)), _full((1, d))],
        out_specs=[row(da), row(d), row(d), row(d), row(d), row(d), _full((8, d))],
        out_shape=[sd((p, da), BF16), sd((p, d), BF16), sd((p, d), BF16), sd((p, d), BF16), sd((p, d), BF16),
                   sd((p, d), F32), sd((8, d), F32)],
        compiler_params=_cparams(("arbitrary",)),
    )(yssd, o, zatt, graw, head, *([x2] * nsub), *([tgt2] * nsub), wps, wpa, wout, gate_bias, norm_post)


def _tail_bwd(dzo, a_b, b_b, graw, o, zatt, wps, wpa, wout, gate_bias, tm):
    p, d = dzo.shape
    ds, da = wps.shape[0], wpa.shape[0]

    def body(dzo_ref, a_ref, b_ref, g_ref, o_ref, zatt_ref, wps_ref, wpa_ref, wout_ref, gb_ref,
             da_ref, db_ref, dg_ref, dyssd_ref, do_ref, dzatt_ref, red_ref):
        i = pl.program_id(0)

        @pl.when(i == 0)
        def _():
            red_ref[...] = jnp.zeros_like(red_ref)

        gr = g_ref[...].astype(F32) + gb_ref[...]
        gs = _sigmoid(gr[:, :d])
        ga = _sigmoid(gr[:, d:])
        dm = _dot(dzo_ref[...], wout_ref[...], NT)
        da_b = (gs * dm).astype(BF16)
        db_b = (ga * dm).astype(BF16)
        da_ref[...] = da_b
        db_ref[...] = db_b
        dgs = dm * a_ref[...].astype(F32) * gs * (1.0 - gs)
        dga = dm * b_ref[...].astype(F32) * ga * (1.0 - ga)
        dg_ref[:, :d] = dgs.astype(BF16)
        dg_ref[:, d:] = dga.astype(BF16)
        red_ref[0:1, :d] += jnp.sum(dgs, axis=0, keepdims=True)
        red_ref[0:1, d:] += jnp.sum(dga, axis=0, keepdims=True)
        dyssd_ref[...] = _dot(da_b, wps_ref[...], NT).astype(BF16)
        dya = _dot(db_b, wpa_ref[...], NT)
        ob = o_ref[...].astype(F32)
        za = zatt_ref[...].astype(F32)
        sza = _sigmoid(za)
        do_ref[...] = (dya * za * sza).astype(BF16)
        dzatt_ref[...] = (dya * ob * sza * (1.0 + za * (1.0 - sza))).astype(BF16)

    row = lambda w: pl.BlockSpec((tm, w), lambda i: (i, 0))
    once = lambda shape: pl.BlockSpec(shape, lambda i: (0,) * len(shape), pipeline_mode=pl.Buffered(1))
    sd = jax.ShapeDtypeStruct
    return pl.pallas_call(
        body, name="tail_bwd", grid=(p // tm,),
        in_specs=[row(d), row(d), row(d), row(2 * d), row(da), row(da),
                  once((ds, d)), once((da, d)), once((d, d)), _full((1, 2 * d))],
        out_specs=[row(d), row(d), row(2 * d), row(ds), row(da), row(da), _full((8, 2 * d))],
        out_shape=[sd((p, d), BF16), sd((p, d), BF16), sd((p, 2 * d), BF16), sd((p, ds), BF16), sd((p, da), BF16),
                   sd((p, da), BF16), sd((8, 2 * d), F32)],
        compiler_params=_cparams(("arbitrary",)),
    )(dzo, a_b, b_b, graw, o, zatt, wps, wpa, wout, gate_bias)


def _adamw_math(w, g, m, v):
    m2 = ADAM_B1 * m + (1.0 - ADAM_B1) * g
    v2 = ADAM_B2 * v + (1.0 - ADAM_B2) * (g * g)
    m_hat = m2 / (1.0 - ADAM_B1 ** ADAM_STEP)
    v_hat = v2 / (1.0 - ADAM_B2 ** ADAM_STEP)
    delta = -ADAM_LR * (m_hat / (jnp.sqrt(v_hat) + ADAM_EPS) + ADAM_WD * w)
    return delta, m2, v2


def _adamw_small(params, red, name):
    names = list(params)
    n = len(names)
    extra = [params[k][3] for k in names if not isinstance(params[k][3], tuple)]

    def body(*refs):
        w_refs, m_refs, v_refs = refs[:n], refs[n:2 * n], refs[2 * n:3 * n]
        red_ref = refs[3 * n]
        g_refs = iter(refs[3 * n + 1:3 * n + 1 + len(extra)])
        outs = refs[3 * n + 1 + len(extra):]
        for i, k in enumerate(names):
            where = params[k][3]
            rows, cols = w_refs[i].shape
            if isinstance(where, tuple):
                g = red_ref[where[0]:where[0] + rows, where[1]:where[1] + cols]
            else:
                g = next(g_refs)[...]
            delta, m2, v2 = _adamw_math(w_refs[i][...], g, m_refs[i][...], v_refs[i][...])
            for o, val in zip(outs[4 * i:4 * i + 4], (g, delta, m2, v2)):
                o[...] = val

    vm = pl.BlockSpec(memory_space=pltpu.VMEM)
    ws, ms, vs = ([params[k][j] for k in names] for j in range(3))
    out = pl.pallas_call(
        body, name=name,
        out_shape=[jax.ShapeDtypeStruct(w.shape, F32) for w in ws for _ in range(4)],
        in_specs=[vm] * (3 * n + 1 + len(extra)), out_specs=[vm] * (4 * n),
    )(*ws, *ms, *vs, red, *extra)
    return {k: tuple(out[4 * i:4 * i + 4]) for i, k in enumerate(names)}


def _adamw(w, g, m, v, name, parts=False, part_row0=0):
    r, cdim = w.shape
    tr, tc, by_rows = _tiles_2d(r, cdim)
    pick = (lambda i: (i, 0)) if by_rows else (lambda i: (0, i))
    assert part_row0 % tr == 0
    gpick = (lambda i: (i + part_row0 // tr, 0)) if by_rows else (lambda i: (part_row0 // tr, i))

    def body(w_ref, g_ref, m_ref, v_ref, go_ref, d_ref, mo_ref, vo_ref):
        if parts:
            g = g_ref[0].astype(F32)
            for s in range(1, g_ref.shape[0]):
                g = g + g_ref[s].astype(F32)
        else:
            g = g_ref[...]
        delta, m2, v2 = _adamw_math(w_ref[...], g, m_ref[...], v_ref[...])
        go_ref[...] = g
        d_ref[...] = delta
        mo_ref[...] = m2
        vo_ref[...] = v2

    blk = pl.BlockSpec((tr, tc), pick)
    gspec = pl.BlockSpec((g.shape[0], tr, tc), lambda i: (0,) + gpick(i)) if parts else blk
    return pl.pallas_call(
        body, name=name, grid=((r // tr) * (cdim // tc),),
        in_specs=[blk, gspec, blk, blk], out_specs=[blk] * 4,
        out_shape=[jax.ShapeDtypeStruct((r, cdim), F32)] * 4,
        compiler_params=_cparams(("parallel",)),
    )(w, g, m, v)


def _pad_cols(a, width):
    return jnp.pad(a, ((0, 0), (0, width - a.shape[1])))


def _pack_small_shard(conv_w_sh, meta_sh, width):
    return jnp.concatenate([_pad_cols(conv_w_sh, width), jnp.zeros((4, width), F32), _pad_cols(meta_sh, width)], axis=0)


def _pack_small_rep(norm_pre, norm_post, gate_bias, ssd_norm, conv_b, misc, width):
    rows = [norm_pre, norm_post, gate_bias, ssd_norm, conv_b, misc]
    return jnp.concatenate([_pad_cols(r, width) for r in rows] + [jnp.zeros((2, width), F32)], axis=0)


def kernel(x, meta_tokens, norm_pre, w_in, conv_w, conv_b, dt_bias, a_log, d_skip, ssd_norm, fgate_bias, gate_bias, w_proj_ssd, w_proj_att, w_out, norm_post, loss_target, m_meta_tokens, m_norm_pre, m_w_in, m_conv_w, m_conv_b, m_dt_bias, m_a_log, m_d_skip, m_ssd_norm, m_fgate_bias, m_gate_bias, m_w_proj_ssd, m_w_proj_att, m_w_out, m_norm_post, v_meta_tokens, v_norm_pre, v_w_in, v_conv_w, v_conv_b, v_dt_bias, v_a_log, v_d_skip, v_ssd_norm, v_fgate_bias, v_gate_bias, v_w_proj_ssd, v_w_proj_att, v_w_out, v_norm_post):
    seq, d = x.shape[1], x.shape[2]
    p = seq + CHUNK
    hs, ha = dt_bias.shape[1], fgate_bias.shape[1]
    ds, cd = ssd_norm.shape[1], conv_b.shape[1]
    da = ha * HEAD_DIM
    nc8 = w_in.shape[2]
    cws = cd // N_DEV
    msh = d // N_DEV
    r1, r2, r3 = ds // N_DEV, da // N_DEV, d // N_DEV
    me = _dev_index(*_my_pos())
    x2, tgt2 = x[0], loss_target[0]

    win_sh = jnp.transpose(w_in[0]).astype(BF16)
    rows_sh = jnp.concatenate([w_proj_ssd[0], w_proj_att[0], w_out[0]], axis=0).astype(BF16)
    small_sh = _pack_small_shard(conv_w[0], meta_tokens, cws)
    win_all, small_all = _all_gather([win_sh, small_sh], "gather_weights")
    rows_sh, win_all = lax.optimization_barrier((rows_sh, win_all))
    rows_sems, rows_thru, rows_land, rows_token = _bcast_start(rows_sh, "gather_rows_start")
    cuts = [0, ds, ds + cd, ds + cd + hs, ds + cd + hs + da, ds + cd + hs + 2 * da, ds + cd + hs + 3 * da,
            ds + cd + hs + 4 * da, ds + cd + hs + 4 * da + ha, ds + cd + hs + 4 * da + ha + 2 * d]

    def piece_rows(r0, r1):
        parts = [win_all[s, max(r0, s * nc8) - s * nc8:min(r1, (s + 1) * nc8) - s * nc8]
                 for s in range(N_DEV) if max(r0, s * nc8) < min(r1, (s + 1) * nc8)]
        return parts[0] if len(parts) == 1 else jnp.concatenate(parts, axis=0)

    w_z, w_xbc, w_dt, w_zatt, w_q, w_k, w_v, w_f, w_g = [piece_rows(cuts[i], cuts[i + 1]) for i in range(9)]
    w_dtf = jnp.concatenate([w_dt, w_f, jnp.zeros((LANES - hs - ha, d), BF16)], axis=0)
    conv_w_full = jnp.transpose(small_all[:, 0:CONV_K, :], (1, 0, 2)).reshape(CONV_K, cd)
    meta_full = jnp.transpose(small_all[:, 8:8 + N_META, :msh], (1, 0, 2)).reshape(N_META, d)
    head = jnp.concatenate([jnp.zeros((PADN, d), F32), meta_full + rows_token[0:1, 0:1]], axis=0)

    tm = _att_block(p)
    u = _prenorm_fwd(head, x2, norm_pre, tm)
    seg_w = [w_z, w_xbc, w_zatt, w_q, w_k, w_v, w_g]
    zs, xbc, zatt, q, k, v, graw = [
        _mm(u, w, "nt", BF16, _tile(p, (2112, 1408, tm)), _tile(w.shape[0], (1024, 512, 256, 128)), "inproj_%d" % i)
        for i, w in enumerate(seg_w)]
    dtf = _mm(u, w_dtf, "nt", F32, _tile(p, (1408, tm)), LANES, "inproj_dtf")

    brow = jnp.concatenate([dt_bias, fgate_bias, jnp.zeros((1, LANES - hs - ha), F32)], axis=1)
    alog_row = _pad_cols(a_log, LANES)
    dskip_l = jnp.repeat(d_skip, HEAD_DIM, axis=1)
    sel_t = (lax.broadcasted_iota(jnp.int32, (LANES, ds), 1) // HEAD_DIM
             == lax.broadcasted_iota(jnp.int32, (LANES, ds), 0)).astype(BF16)
    sel = sel_t.T
    y, yssd, hin, cf, pre = _ssd_fwd(xbc, zs, dtf, conv_w_full, conv_b, brow, alog_row, dskip_l, ssd_norm, sel_t, hs, ha)

    blk = _att_block(p)
    nkb, npair = p // blk, ha // 2
    cum = jnp.where(lax.broadcasted_iota(jnp.int32, (p, 1), 0) < PADN, -NEG, cf[:, hs:hs + ha])
    ck = jnp.transpose(cum.T.reshape(npair, 2, nkb, blk), (0, 2, 1, 3))
    ck = jnp.pad(ck, ((0, 0), (0, 0), (0, 6), (0, 0)))
    o, lse_rep = _attn_fwd(q, k, v, ck, blk)

    rows_all = _bcast_wait(rows_sems, rows_thru, rows_land, lse_rep, "gather_rows_wait")
    wps = rows_all[:, :r1].reshape(ds, d)
    wpa = rows_all[:, r1:r1 + r2].reshape(da, d)
    wout = rows_all[:, r1 + r2:].reshape(d, d)

    yatt, mrg, a_b, b_b, dzo, dout, red_fwd = _tail_fwd(
        yssd, o, zatt, graw, head, x2, tgt2, wps, wpa, wout, gate_bias, norm_post, tm)
    da_, db_, dgraw, dyssd, d_o, dzatt, red_bwd = _tail_bwd(dzo, a_b, b_b, graw, o, zatt, wps, wpa, wout, gate_bias, tm)

    tw = _tile(d, (512, 256, 128))
    g_wout = _mm(mrg, dzo, "tn", BF16, tw, d, "wgrad_out")
    g_wps = _mm(yssd, da_, "tn", BF16, _tile(ds, (512, 256, 128)), d, "wgrad_ps")
    g_wpa = _mm(yatt, db_, "tn", BF16, _tile(da, (512, 256, 128)), d, "wgrad_pa")

    core = lax.axis_index("c").astype(jnp.int32).reshape(1)
    chip = me // 2
    grows_parts = jnp.concatenate([g_wps.reshape(N_DEV, r1, d), g_wpa.reshape(N_DEV, r2, d),
                                   g_wout.reshape(N_DEV, r3, d)], axis=1)
    (sib_rows,) = _exchange_sibling([grows_parts], "scatter_rows_sibling")
    chip_rows = _pair_add(grows_parts, sib_rows, core, "pair_add_rows")
    r_sems, r_thru, r_lands, r_token = _exchange_chips_start([chip_rows], "scatter_rows_start")

    dk, dv, dq, dcs, rsum = _attn_bwd(q, k, v, o, d_o, lse_rep, ck + r_token[0:1, 0:1], blk)
    dcum = (rsum - dcs)[:, 0:2, :].reshape(ha, p).T
    dcf = jnp.pad(dcum, ((0, 0), (hs, LANES - hs - ha)))
    dxbc, dzs, ddtf, gcw, gcb, gnrm, gsm = _ssd_bwd(
        dyssd, y, zs, xbc, pre, dtf, hin, dcf, conv_w_full, brow, alog_row, dskip_l, ssd_norm, sel_t, sel, hs, ha)
    ddtf_b = ddtf.astype(BF16)

    dsegs = [dzs, dxbc, dzatt, dq, dk, dv, dgraw, ddtf_b]
    gsegs = [_mm(dsg, u, "tn", BF16, _tile(dsg.shape[1], (512, 256, 128)), d, "wgrad_in_%d" % i)
             for i, dsg in enumerate(dsegs)]
    g_z, g_xbc, g_zatt, g_q, g_k, g_v, g_g, g_dtf = gsegs
    gw_full = jnp.concatenate([g_z, g_xbc, g_dtf[:hs], g_zatt, g_q, g_k, g_v, g_dtf[hs:hs + ha], g_g], axis=0)
    gwin_parts = gw_full.reshape(N_DEV, nc8, d)

    (sib_win,) = _exchange_sibling([gwin_parts], "scatter_grads_sibling")
    chip_win = _pair_add(gwin_parts, sib_win, core, "pair_add_w_in")
    sems, thru, lands, token = _exchange_chips_start([chip_win], "scatter_grads_start")
    dsegs_after = dsegs[:-1] + [ddtf_b + token[0:1, 0:1].astype(BF16)]
    gx, ghead, gnp = _dgrad_prenorm(dsegs_after, seg_w + [w_dtf], head, x2, norm_pre, dout, tm, "dgrad_in")
    own_slot = lambda got, sent: lax.dynamic_update_slice_in_dim(
        got, lax.dynamic_slice_in_dim(sent, chip, 1, axis=0), chip, axis=0)
    (sent,), (got,) = _exchange_chips_wait(sems, thru, lands, gnp, "scatter_grads_wait")
    recv_win = own_slot(got, sent)
    (r_sent,), (r_got,) = _exchange_chips_wait(r_sems, r_thru, r_lands, gnp, "scatter_rows_wait")
    recv_rows = own_slot(r_got, r_sent)
    gmisc = jnp.concatenate([gsm[0:1], gsm[1:2], gsm[2:3], _pad_cols(red_fwd[1:2, 0:1], LANES)], axis=1)
    small_g = jnp.concatenate([
        _pack_small_rep(gnp[0:1], red_fwd[0:1], red_bwd[0:1], gnrm[0:1], gcb[0:1], gmisc, cd),
        _pad_cols(gcw[0:CONV_K], cd), jnp.zeros((4, cd), F32), _pad_cols(ghead[PADN:], cd)], axis=0)
    sg_sems, sg_thru, sg_land, sg_token = _bcast_start(small_g, "reduce_small_start")

    upd_in = _adamw(jnp.transpose(w_in[0]) + sg_token[0:1, 0:1], recv_win, jnp.transpose(m_w_in[0]),
                    jnp.transpose(v_w_in[0]), "adamw_w_in", parts=True)
    upd_ps = _adamw(w_proj_ssd[0] + sg_token[0:1, 0:1], recv_rows, m_w_proj_ssd[0], v_w_proj_ssd[0],
                    "adamw_w_proj_ssd", parts=True, part_row0=0)
    upd_pa = _adamw(w_proj_att[0], recv_rows, m_w_proj_att[0], v_w_proj_att[0], "adamw_w_proj_att", parts=True,
                    part_row0=r1)
    upd_out = _adamw(w_out[0], recv_rows, m_w_out[0], v_w_out[0], "adamw_w_out", parts=True, part_row0=r1 + r2)
    all_done = upd_in[1][0:8, 0:LANES] + upd_ps[1][0:8, 0:LANES] + upd_pa[1][0:8, 0:LANES] + upd_out[1][0:8, 0:LANES]
    red = _sum_slots(_bcast_wait(sg_sems, sg_thru, sg_land, all_done, "reduce_small_wait"), "reduce_small_sum")
    loss = red[5, 3 * LANES]
    g_conv_w = lax.dynamic_slice_in_dim(red[8:8 + CONV_K], me * cws, cws, axis=1)
    g_meta = lax.dynamic_slice_in_dim(red[16:16 + N_META, :d], me * msh, msh, axis=1)
    small = {
        "meta_tokens": (meta_tokens, m_meta_tokens, v_meta_tokens, g_meta),
        "norm_pre": (norm_pre, m_norm_pre, v_norm_pre, (0, 0)),
        "conv_w": (conv_w[0], m_conv_w[0], v_conv_w[0], g_conv_w),
        "conv_b": (conv_b, m_conv_b, v_conv_b, (4, 0)),
        "dt_bias": (dt_bias, m_dt_bias, v_dt_bias, (5, 0)),
        "a_log": (a_log, m_a_log, v_a_log, (5, LANES)),
        "d_skip": (d_skip, m_d_skip, v_d_skip, (5, 2 * LANES)),
        "ssd_norm": (ssd_norm, m_ssd_norm, v_ssd_norm, (3, 0)),
        "fgate_bias": (fgate_bias, m_fgate_bias, v_fgate_bias, (5, hs)),
        "gate_bias": (gate_bias, m_gate_bias, v_gate_bias, (2, 0)),
        "norm_post": (norm_post, m_norm_post, v_norm_post, (1, 0)),
    }
    upd_small = _adamw_small(small, red, "adamw_small")

    def leaves(i):
        sm = {k: v[i] for k, v in upd_small.items()}
        return [sm["meta_tokens"], sm["norm_pre"], jnp.transpose(upd_in[i])[None], sm["conv_w"][None], sm["conv_b"],
                sm["dt_bias"], sm["a_log"], sm["d_skip"], sm["ssd_norm"], sm["fgate_bias"], sm["gate_bias"],
                upd_ps[i][None], upd_pa[i][None], upd_out[i][None], sm["norm_post"]]

    return tuple([loss, gx[None]] + leaves(0) + leaves(1) + leaves(2) + leaves(3))
```

```python
import functools
import math

import jax
import jax.numpy as jnp
from jax import lax
from jax.experimental import pallas as pl
from jax.experimental.pallas import tpu as pltpu

F32 = jnp.float32
BF16 = jnp.bfloat16

N_DEV = 8
N_META = 16
CHUNK = 128
PADN = CHUNK - N_META
HEAD_DIM = 64
SSD_GROUPS = 4
CONV_K = 4
EPS = 1e-6
NEG = -1e30
LANES = 128
HALO = 16

ADAM_LR = 0.001
ADAM_B1 = 0.9
ADAM_B2 = 0.999
ADAM_EPS = 1e-08
ADAM_WD = 0.01
ADAM_STEP = 10

VMEM_LIMIT = 56 * 1024 * 1024

NN = (((1,), (0,)), ((), ()))
NT = (((1,), (1,)), ((), ()))
TN = (((0,), (0,)), ((), ()))
MESH = pl.DeviceIdType.MESH


def _dot(a, b, dims=NN):
    return lax.dot_general(a, b, dims, preferred_element_type=F32)


def _split2(x):
    hi = x.astype(BF16)
    lo = (x - hi.astype(F32)).astype(BF16)
    return hi, lo


def _dot_sel(x, sel):
    hi, lo = _split2(x)
    return _dot(hi, sel) + _dot(lo, sel)


def _dot_tri(tri, x):
    h1 = x.astype(BF16)
    r1 = x - h1.astype(F32)
    h2 = r1.astype(BF16)
    h3 = (r1 - h2.astype(F32)).astype(BF16)
    return _dot(tri, h1) + _dot(tri, h2) + _dot(tri, h3)


def _sigmoid(x):
    return 0.5 * jnp.tanh(0.5 * x) + 0.5


def _softplus(x):
    return jnp.maximum(x, 0.0) + jnp.log(1.0 + jnp.exp(-jnp.abs(x)))


def _cparams(sem=None, vmem=VMEM_LIMIT):
    kw = {"vmem_limit_bytes": vmem}
    if sem is not None:
        kw["dimension_semantics"] = sem
    return pltpu.CompilerParams(**kw)


def _full(shape):
    nd = len(shape)
    return pl.BlockSpec(shape, lambda *_: (0,) * nd)


def _att_block(p):
    return 384 if p % 384 == 0 else CHUNK


def _my_pos():
    return lax.axis_index("x"), lax.axis_index("y"), lax.axis_index("c")


def _dev_index(x, y, c):
    return 4 * x + 2 * y + c


FLIPS = [(fx, fy, fc) for fx in (0, 1) for fy in (0, 1) for fc in (0, 1)][1:]


def _flip(pos, f):
    return tuple((1 - p) if fi else p for p, fi in zip(pos, f))


def _all_gather(bufs, name):
    nb = len(bufs)

    def body(*refs):
        ins, outs = refs[:nb], refs[nb:2 * nb]
        send_sems, recv_sems, local_sems = refs[2 * nb:]
        x, y, c = _my_pos()
        me = _dev_index(x, y, c)
        sibling = (x, y, 1 - c)
        near = [(1 - x, y), (x, 1 - y)]
        far = (1 - x, 1 - y)
        relay_from = (c * (1 - x) + (1 - c) * x, c * y + (1 - c) * (1 - y))
        relay_to = (c * x + (1 - c) * (1 - x), c * (1 - y) + (1 - c) * y)

        def copy(b, k, block_idx, to, src=None):
            dst = outs[b].at[block_idx]
            return pltpu.make_async_remote_copy(
                src_ref=dst if src is None else src, dst_ref=dst,
                send_sem=send_sems.at[b, k], recv_sem=recv_sems.at[b, k],
                device_id=to, device_id_type=MESH)

        started = []
        for b in range(nb):
            mine = pltpu.make_async_copy(ins[b], outs[b].at[me], local_sems.at[b])
            mine.start()
            started.append(mine)
        sent = []
        for b in range(nb):
            sent.append(copy(b, 0, me, sibling, src=ins[b]))
            for j, chip in enumerate(near):
                sent.append(copy(b, 1 + j, me, (chip[0], chip[1], c), src=ins[b]))
        for cp in sent:
            cp.start()
        for j, chip in enumerate(near):
            blk = _dev_index(chip[0], chip[1], c)
            for b in range(nb):
                copy(b, 1 + j, blk, (x, y, c)).wait_recv()
                sent.append(copy(b, 4 + j, blk, sibling))
                sent[-1].start()
        for b in range(nb):
            sent.append(copy(b, 3, _dev_index(relay_from[0], relay_from[1], c), (relay_to[0], relay_to[1], c)))
            sent[-1].start()
        blk = _dev_index(far[0], far[1], c)
        for b in range(nb):
            copy(b, 3, blk, (x, y, c)).wait_recv()
            sent.append(copy(b, 6, blk, sibling))
            sent[-1].start()
        for b in range(nb):
            copy(b, 0, _dev_index(x, y, 1 - c), (x, y, c)).wait_recv()
        for j, chip in enumerate(near + [far]):
            blk = _dev_index(chip[0], chip[1], 1 - c)
            for b in range(nb):
                copy(b, 4 + j, blk, (x, y, c)).wait_recv()
        for cp in sent:
            cp.wait_send()
        for mine in started:
            mine.wait()

    any_spec = pl.BlockSpec(memory_space=pl.ANY)
    return pl.pallas_call(
        body, name=name,
        out_shape=[jax.ShapeDtypeStruct((N_DEV,) + b.shape, b.dtype) for b in bufs],
        in_specs=[any_spec] * nb, out_specs=[any_spec] * nb,
        scratch_shapes=[pltpu.SemaphoreType.DMA((nb, 7)), pltpu.SemaphoreType.DMA((nb, 7)),
                        pltpu.SemaphoreType.DMA((nb,))],
    )(*bufs)


N_CHIP = 4
CHIP_FLIPS = [(1, 0), (0, 1), (1, 1)]


def _exchange_sibling(bufs, name):
    nb = len(bufs)

    def body(*refs):
        ins, outs = refs[:nb], refs[nb:2 * nb]
        send_sems, recv_sems = refs[2 * nb:]
        x, y, c = _my_pos()

        def copy(b, k):
            return pltpu.make_async_remote_copy(
                src_ref=ins[b].at[2 * k + (1 - c)], dst_ref=outs[b].at[k],
                send_sem=send_sems.at[b, k], recv_sem=recv_sems.at[b, k],
                device_id=(x, y, 1 - c), device_id_type=MESH)

        cps = [copy(b, k) for b in range(nb) for k in range(N_CHIP)]
        for cp in cps:
            cp.start()
        for cp in cps:
            cp.wait()

    any_spec = pl.BlockSpec(memory_space=pl.ANY)
    return pl.pallas_call(
        body, name=name,
        out_shape=[jax.ShapeDtypeStruct((N_CHIP,) + b.shape[1:], b.dtype) for b in bufs],
        in_specs=[any_spec] * nb, out_specs=[any_spec] * nb,
        scratch_shapes=[pltpu.SemaphoreType.DMA((nb, N_CHIP)), pltpu.SemaphoreType.DMA((nb, N_CHIP))],
    )(*bufs)


def _pair_add(mine, recv, core, name):
    _, r, cdim = mine.shape
    tr, tc = r, cdim
    pick = lambda i: (i, 0)

    def body(core_ref, a_ref, b_ref, o_ref):
        o_ref[0] = (a_ref[0].astype(F32) + b_ref[0].astype(F32)).astype(o_ref.dtype)

    return pl.pallas_call(
        body, name=name,
        grid_spec=pltpu.PrefetchScalarGridSpec(
            num_scalar_prefetch=1, grid=(N_CHIP, (r // tr) * (cdim // tc)),
            in_specs=[pl.BlockSpec((1, tr, tc), lambda k, i, core_ref: (2 * k + core_ref[0],) + pick(i)),
                      pl.BlockSpec((1, tr, tc), lambda k, i, core_ref: (k,) + pick(i))],
            out_specs=pl.BlockSpec((1, tr, tc), lambda k, i, core_ref: (k,) + pick(i))),
        out_shape=jax.ShapeDtypeStruct((N_CHIP, r, cdim), mine.dtype),
        compiler_params=_cparams(("parallel", "parallel")),
    )(core, mine, recv)


def _chip_peer(x, y, f):
    return ((1 - x) if f[0] else x), ((1 - y) if f[1] else y)


def _exchange_chips_start(bufs, name):
    nb = len(bufs)
    nsem = 2 * 3 * nb

    def body(*refs):
        ins, lands = refs[:nb], refs[nb:2 * nb]
        sems = refs[2 * nb:2 * nb + nsem]
        token = refs[-1]
        x, y, c = _my_pos()
        for b in range(nb):
            for j, f in enumerate(CHIP_FLIPS):
                px, py = _chip_peer(x, y, f)
                pltpu.make_async_remote_copy(
                    src_ref=ins[b].at[2 * px + py], dst_ref=lands[b].at[2 * x + y],
                    send_sem=sems[2 * (3 * b + j)], recv_sem=sems[2 * (3 * b + j) + 1],
                    device_id=(px, py, c), device_id_type=MESH).start()
        token[...] = jnp.zeros_like(token)

    hbm = pl.BlockSpec(memory_space=pltpu.HBM)
    sem = pl.BlockSpec(memory_space=pltpu.SEMAPHORE)
    out = pl.pallas_call(
        body, name=name,
        out_shape=(*([pltpu.SemaphoreType.DMA(())] * nsem),
                   *[pltpu.HBM(b.shape, b.dtype) for b in bufs], *[pltpu.HBM(b.shape, b.dtype) for b in bufs],
                   jax.ShapeDtypeStruct((8, LANES), F32)),
        in_specs=[hbm] * (2 * nb),
        out_specs=(*([sem] * nsem), *([hbm] * (2 * nb)), pl.BlockSpec(memory_space=pltpu.VMEM)),
        input_output_aliases={i: nsem + i for i in range(2 * nb)},
        compiler_params=pltpu.CompilerParams(has_side_effects=pltpu.SideEffectType.DATAFLOW_SIDE_EFFECTING),
    )(*[pltpu.with_memory_space_constraint(b, pltpu.HBM) for b in bufs],
      *[pltpu.with_memory_space_constraint(lax.empty(b.shape, b.dtype), pltpu.HBM) for b in bufs])
    return out[:nsem], out[nsem:nsem + nb], out[nsem + nb:nsem + 2 * nb], out[-1]


def _exchange_chips_wait(sems, thru, lands, after, name):
    nb = len(thru)
    nsem = len(sems)

    def body(*refs):
        ins, lnd = refs[:nb], refs[nb:2 * nb]
        sem_refs = refs[2 * nb:2 * nb + nsem]
        x, y, c = _my_pos()
        for b in range(nb):
            for j, f in enumerate(CHIP_FLIPS):
                px, py = _chip_peer(x, y, f)
                cp = pltpu.make_async_remote_copy(
                    src_ref=ins[b].at[2 * px + py], dst_ref=lnd[b].at[2 * px + py],
                    send_sem=sem_refs[2 * (3 * b + j)], recv_sem=sem_refs[2 * (3 * b + j) + 1],
                    device_id=(px, py, c), device_id_type=MESH)
                cp.wait_send()
                cp.wait_recv()

    hbm = pl.BlockSpec(memory_space=pltpu.HBM)
    sem = pl.BlockSpec(memory_space=pltpu.SEMAPHORE)
    out = pl.pallas_call(
        body, name=name,
        out_shape=tuple([pltpu.HBM(b.shape, b.dtype) for b in thru] + [pltpu.HBM(b.shape, b.dtype) for b in lands]),
        in_specs=[hbm] * (2 * nb) + [sem] * nsem + [pl.BlockSpec(memory_space=pl.ANY)],
        out_specs=tuple([hbm] * (2 * nb)),
        input_output_aliases={i: i for i in range(2 * nb)},
        compiler_params=pltpu.CompilerParams(has_side_effects=pltpu.SideEffectType.DATAFLOW_SIDE_EFFECTING),
    )(*thru, *lands, *sems, after)
    return out[:nb], out[nb:]


def _bcast_start(buf, name):
    nsem = 2 * len(FLIPS)

    def body(src, land, *rest):
        sems, token = rest[:nsem], rest[-1]
        pos = _my_pos()
        for k, f in enumerate(FLIPS):
            pltpu.make_async_remote_copy(
                src_ref=src, dst_ref=land.at[_dev_index(*pos)], send_sem=sems[2 * k], recv_sem=sems[2 * k + 1],
                device_id=_flip(pos, f), device_id_type=MESH).start()
        token[...] = jnp.zeros_like(token)

    hbm = pl.BlockSpec(memory_space=pltpu.HBM)
    sem = pl.BlockSpec(memory_space=pltpu.SEMAPHORE)
    land_shape = (N_DEV,) + buf.shape
    out = pl.pallas_call(
        body, name=name,
        out_shape=(*([pltpu.SemaphoreType.DMA(())] * nsem), pltpu.HBM(buf.shape, buf.dtype),
                   pltpu.HBM(land_shape, buf.dtype), jax.ShapeDtypeStruct((8, LANES), F32)),
        in_specs=[hbm, hbm],
        out_specs=(*([sem] * nsem), hbm, hbm, pl.BlockSpec(memory_space=pltpu.VMEM)),
        input_output_aliases={0: nsem, 1: nsem + 1},
        compiler_params=pltpu.CompilerParams(has_side_effects=pltpu.SideEffectType.DATAFLOW_SIDE_EFFECTING),
    )(pltpu.with_memory_space_constraint(buf, pltpu.HBM),
      pltpu.with_memory_space_constraint(lax.empty(land_shape, buf.dtype), pltpu.HBM))
    return out[:nsem], out[nsem], out[nsem + 1], out[-1]


def _bcast_wait(sems, thru, land, after, name):
    nsem = len(sems)

    def body(src, lnd, *rest):
        sem_refs = rest[:nsem]
        pos = _my_pos()
        for k, f in enumerate(FLIPS):
            peer = _flip(pos, f)
            cp = pltpu.make_async_remote_copy(
                src_ref=src, dst_ref=lnd.at[_dev_index(*peer)], send_sem=sem_refs[2 * k],
                recv_sem=sem_refs[2 * k + 1], device_id=peer, device_id_type=MESH)
            cp.wait_send()
            cp.wait_recv()

    hbm = pl.BlockSpec(memory_space=pltpu.HBM)
    sem = pl.BlockSpec(memory_space=pltpu.SEMAPHORE)
    sent, got = pl.pallas_call(
        body, name=name,
        out_shape=(pltpu.HBM(thru.shape, thru.dtype), pltpu.HBM(land.shape, land.dtype)),
        in_specs=[hbm, hbm] + [sem] * nsem + [pl.BlockSpec(memory_space=pl.ANY)],
        out_specs=(hbm, hbm), input_output_aliases={0: 0, 1: 1},
        compiler_params=pltpu.CompilerParams(has_side_effects=pltpu.SideEffectType.DATAFLOW_SIDE_EFFECTING),
    )(thru, land, *sems, after)
    return lax.dynamic_update_slice_in_dim(got, sent[None], _dev_index(*_my_pos()), axis=0)


def _sum_slots(v, name):
    _, r, cdim = v.shape

    def body(v_ref, o_ref):
        acc = v_ref[0]
        for s in range(1, N_DEV):
            acc = acc + v_ref[s]
        o_ref[...] = acc

    return pl.pallas_call(
        body, name=name, out_shape=jax.ShapeDtypeStruct((r, cdim), F32),
        in_specs=[_full((N_DEV, r, cdim))], out_specs=_full((r, cdim)), grid=(1,),
        compiler_params=_cparams(("arbitrary",)),
    )(v)


def _mm(a, b, dims, out_dtype, tm, tn, name):
    if dims == "nn":
        (m, k), (_, n) = a.shape, b.shape
        a_spec = pl.BlockSpec((tm, k), lambda j, i: (i, 0))
        b_spec = pl.BlockSpec((k, tn), lambda j, i: (0, j))
        dn = NN
    elif dims == "nt":
        (m, k), (n, _) = a.shape, b.shape
        a_spec = pl.BlockSpec((tm, k), lambda j, i: (i, 0))
        b_spec = pl.BlockSpec((tn, k), lambda j, i: (j, 0))
        dn = NT
    else:
        (k, m), (_, n) = a.shape, b.shape
        a_spec = pl.BlockSpec((k, tm), lambda j, i: (0, i))
        b_spec = pl.BlockSpec((k, tn), lambda j, i: (0, j))
        dn = TN
    assert m % tm == 0 and n % tn == 0, (m, tm, n, tn)

    def body(a_ref, b_ref, o_ref):
        o_ref[...] = _dot(a_ref[...], b_ref[...], dn).astype(o_ref.dtype)

    return pl.pallas_call(
        body, name=name, grid=(n // tn, m // tm),
        in_specs=[a_spec, b_spec], out_specs=pl.BlockSpec((tm, tn), lambda j, i: (i, j)),
        out_shape=jax.ShapeDtypeStruct((m, n), out_dtype),
        compiler_params=_cparams(("parallel", "parallel")),
    )(a, b)


def _mm_tn_rows(a, b, out_dtype, tm, tk, name):
    (k, m), (_, n) = a.shape, b.shape
    assert m % tm == 0 and k % tk == 0, (m, tm, k, tk)
    nk = k // tk

    def body(a_ref, b_ref, o_ref, acc):
        kk = pl.program_id(1)
        part = _dot(a_ref[...], b_ref[...], TN)

        @pl.when(kk == 0)
        def _():
            acc[...] = part

        @pl.when(kk > 0)
        def _():
            acc[...] += part

        @pl.when(kk == nk - 1)
        def _():
            o_ref[...] = acc[...].astype(o_ref.dtype)

    return pl.pallas_call(
        body, name=name, grid=(m // tm, nk),
        in_specs=[pl.BlockSpec((tk, tm), lambda i, kk: (kk, i)), pl.BlockSpec((tk, n), lambda i, kk: (kk, 0))],
        out_specs=pl.BlockSpec((tm, n), lambda i, kk: (i, 0)),
        out_shape=jax.ShapeDtypeStruct((m, n), out_dtype),
        scratch_shapes=[pltpu.VMEM((tm, n), F32)],
        compiler_params=_cparams(("parallel", "arbitrary")),
    )(a, b)


def _tiles_2d(r, cdim):
    if r % CHUNK == 0:
        return CHUNK, cdim, True
    return r, _tile(cdim, (256, 128)), False


def _dgrad_prenorm(a_list, b_list, head, x2, w, dout, tm, name):
    n_op = len(a_list)
    m, d = a_list[0].shape[0], b_list[0].shape[1]
    subs = _x_row_specs(tm, d)
    last = m // tm - 1
    rest = tm - CHUNK

    def body(*refs):
        a_refs, b_refs = refs[:n_op], refs[n_op:2 * n_op]
        head_ref = refs[2 * n_op]
        x_refs = refs[2 * n_op + 1:2 * n_op + 1 + len(subs)]
        w_ref, dout_ref, gx_ref, ghead_ref, gw_ref, dh_buf, sem = refs[2 * n_op + 1 + len(subs):]
        i = pl.program_id(0)

        def first_copy():
            return pltpu.make_async_copy(dh_buf.at[pl.ds(CHUNK, rest)], gx_ref.at[pl.ds(0, rest)], sem)

        def later_copy(step):
            return pltpu.make_async_copy(dh_buf, gx_ref.at[pl.ds(pl.multiple_of(step * tm - CHUNK, CHUNK), tm)], sem)

        @pl.when(i == 0)
        def _():
            gw_ref[...] = jnp.zeros_like(gw_ref)

        du = _dot(a_refs[0][...], b_refs[0][...])
        for k in range(1, n_op):
            du = du + _dot(a_refs[k][...], b_refs[k][...])
        first = jnp.where(i == 0, head_ref[...], x_refs[0][...])
        h = jnp.concatenate([first] + [r[...] for r in x_refs[1:]], axis=0)
        rstd = lax.rsqrt(jnp.mean(h * h, axis=-1, keepdims=True) + EPS)
        xhat = h * rstd
        dxh = du * w_ref[...]
        dh = rstd * (dxh - xhat * jnp.mean(dxh * xhat, axis=-1, keepdims=True)) + dout_ref[...]
        gw_ref[0:1, :] += jnp.sum(du * xhat, axis=0, keepdims=True)

        if rest and last >= 1:
            @pl.when(i == 1)
            def _():
                first_copy().wait()

        @pl.when(i >= (2 if rest else 1))
        def _():
            later_copy(i - 1).wait()

        dh_buf[...] = dh

        @pl.when(i == 0)
        def _():
            ghead_ref[...] = dh_buf[0:CHUNK, :]
            if rest:
                first_copy().start()
                if last == 0:
                    first_copy().wait()

        @pl.when(i >= 1)
        def _():
            later_copy(i).start()

        if last >= 1:
            @pl.when(i == last)
            def _():
                later_copy(i).wait()

    once = lambda b: pl.BlockSpec(b.shape, lambda i: (0, 0), pipeline_mode=pl.Buffered(1))
    row = lambda width: pl.BlockSpec((tm, width), lambda i: (i, 0))
    return pl.pallas_call(
        body, name=name, grid=(m // tm,),
        in_specs=([row(a.shape[1]) for a in a_list] + [once(b) for b in b_list]
                  + [_full((CHUNK, d))] + subs + [_full((1, d)), row(d)]),
        out_specs=[pl.BlockSpec(memory_space=pl.ANY), _full((CHUNK, d)), _full((8, d))],
        out_shape=[jax.ShapeDtypeStruct((m - CHUNK, d), F32), jax.ShapeDtypeStruct((CHUNK, d), F32),
                   jax.ShapeDtypeStruct((8, d), F32)],
        scratch_shapes=[pltpu.VMEM((tm, d), F32), pltpu.SemaphoreType.DMA],
        compiler_params=_cparams(("arbitrary",)),
    )(*a_list, *b_list, head, *([x2] * len(subs)), w, dout)


def _tile(n, prefs):
    for t in prefs:
        if n % t == 0:
            return t
    return n


def _rows3(i):
    return jnp.maximum(3 * i - 1, 0), 3 * i, 3 * i + 1


def _x_row_specs(tm, d):
    if tm == CHUNK:
        return [pl.BlockSpec((CHUNK, d), lambda i: (jnp.maximum(i - 1, 0), 0))]
    return [pl.BlockSpec((CHUNK, d), functools.partial(lambda i, k: (_rows3(i)[k], 0), k=k)) for k in range(3)]


def _prenorm_fwd(head, x2, w, tm):
    p, d = x2.shape[0] + CHUNK, x2.shape[1]
    subs = _x_row_specs(tm, d)

    def body(head_ref, *rest):
        x_refs, (w_ref, u_ref) = rest[:len(subs)], rest[len(subs):]
        i = pl.program_id(0)
        first = jnp.where(i == 0, head_ref[...], x_refs[0][...])
        h = jnp.concatenate([first] + [r[...] for r in x_refs[1:]], axis=0)
        ms = jnp.mean(h * h, axis=-1, keepdims=True)
        u_ref[...] = (h * lax.rsqrt(ms + EPS) * w_ref[...]).astype(BF16)

    return pl.pallas_call(
        body, name="prenorm_fwd", grid=(p // tm,),
        in_specs=[_full((CHUNK, d))] + subs + [_full((1, d))],
        out_specs=pl.BlockSpec((tm, d), lambda i: (i, 0)),
        out_shape=jax.ShapeDtypeStruct((p, d), BF16),
        compiler_params=_cparams(("arbitrary",)),
    )(head, *([x2] * len(subs)), w)


def _conv_pre(ext_ref, cw_ref, cb_ref):
    pre = cb_ref[...] + cw_ref[CONV_K - 1:CONV_K, :] * ext_ref[8:8 + CHUNK, :]
    for j in range(1, CONV_K):
        pre = pre + cw_ref[CONV_K - 1 - j:CONV_K - j, :] * ext_ref[8 - j:8 - j + CHUNK, :]
    return pre


def _ssd_scalars(dtf_ref, brow_ref, alog_ref, rowmask, hs, ha, tri):
    lane = lax.broadcasted_iota(jnp.int32, (1, LANES), 1)
    is_dt = lane < hs
    is_f = (lane >= hs) & (lane < hs + ha)
    dtr = dtf_ref[...] + brow_ref[...]
    sp = _softplus(dtr)
    dt = jnp.where(is_dt, sp, 0.0) * rowmask
    logf = jnp.where(is_f, jnp.minimum(dtr, 0.0) - jnp.log(1.0 + jnp.exp(-jnp.abs(dtr))), 0.0) * rowmask
    a_row = jnp.where(is_dt, -jnp.exp(alog_ref[...]), 0.0)
    run = _dot_tri(tri, dt * a_row + logf)
    return dtr, dt, a_row, run, is_dt, is_f


def _tri_mats():
    r = lax.broadcasted_iota(jnp.int32, (CHUNK, CHUNK), 0)
    c = lax.broadcasted_iota(jnp.int32, (CHUNK, CHUNK), 1)
    return r, c


def _ssd_fwd(xbc, z, dtf, conv_w, conv_b, brow, alog, dskip_l, ssd_norm, sel_t, hs, ha):
    p, cd = xbc.shape
    ds = z.shape[1]
    ns = (cd - ds) // (2 * SSD_GROUPS)
    gw = ds // SSD_GROUPS
    nch = p // CHUNK
    hpg = hs // SSD_GROUPS

    def body(xbc_ref, halo_ref, z_ref, dtf_ref, cw_ref, cb_ref, brow_ref, alog_ref, dsk_ref, nrm_ref, selt_ref,
             y_ref, yssd_ref, hin_ref, cf_ref, pre_ref, st_ref, carry_ref, yacc_ref, xc_s, ex_s, xdtb_s, xwb_s, ext_s):
        c = pl.program_id(0)

        @pl.when(c == 0)
        def _():
            st_ref[...] = jnp.zeros_like(st_ref)
            carry_ref[...] = jnp.zeros_like(carry_ref)

        rows = lax.broadcasted_iota(jnp.int32, (CHUNK, 1), 0)
        rowmask = jnp.where((rows >= PADN) | (c > 0), 1.0, 0.0)
        ri, ci = _tri_mats()
        causal = ri >= ci
        tri = jnp.where(causal, 1.0, 0.0).astype(BF16)

        ext_s[0:8, :] = halo_ref[...].astype(F32)[HALO - 8:, :] * jnp.where(c > 0, 1.0, 0.0)
        ext_s[8:, :] = xbc_ref[...].astype(F32)
        pre = _conv_pre(ext_s, cw_ref, cb_ref)
        pre_ref[...] = pre.astype(BF16)
        xc_s[...] = pre * _sigmoid(pre) * rowmask

        dtr, dt, a_row, run, is_dt, is_f = _ssd_scalars(dtf_ref, brow_ref, alog_ref, rowmask, hs, ha, tri)
        cf = run + carry_ref[...]
        cf_ref[...] = cf
        carry_ref[...] = jnp.where(is_f, cf[CHUNK - 1:CHUNK, :], 0.0)
        cs = jnp.where(is_dt, run, 0.0)
        cl = cs[CHUNK - 1:CHUNK, :]
        selt = selt_ref[...]
        ex_s[...] = _dot_sel(jnp.exp(cs), selt)
        cdec_x = _dot_sel(jnp.broadcast_to(jnp.exp(cl), (8, LANES)), selt)[0:1, :]
        cs_t = cs.T
        xdt = xc_s[:, :ds] * _dot_sel(dt, selt)
        xdtb_s[...] = xdt.astype(BF16)
        xwb_s[...] = (xdt * _dot_sel(jnp.exp(cl - cs), selt)).astype(BF16)

        lane = lax.broadcasted_iota(jnp.int32, (1, LANES), 1)
        half0 = lane < HEAD_DIM
        for g in range(SSD_GROUPS):
            bg = xc_s[:, ds + g * ns: ds + (g + 1) * ns].astype(BF16)
            cg = xc_s[:, ds + SSD_GROUPS * ns + g * ns: ds + SSD_GROUPS * ns + (g + 1) * ns].astype(BF16)
            gm = _dot(cg, bg, NT)
            gs = slice(g * gw, (g + 1) * gw)
            stg = st_ref[:, gs]
            stg_b = stg.astype(BF16)
            hin_ref[0, :, gs] = stg_b
            yoff = _dot(cg, stg_b) * ex_s[:, gs]
            for pr in range(gw // LANES):
                sl = slice(g * gw + pr * LANES, g * gw + (pr + 1) * LANES)
                xp = xdtb_s[:, sl]
                yd = jnp.zeros((CHUNK, LANES), F32)
                for j in range(2):
                    h = g * hpg + 2 * pr + j
                    seg = cs[:, h:h + 1] - cs_t[h:h + 1, :]
                    m = jnp.where(causal, gm * jnp.exp(jnp.minimum(seg, 0.0)), 0.0).astype(BF16)
                    sel = half0 if j == 0 else jnp.logical_not(half0)
                    yd = yd + _dot(m, jnp.where(sel, xp, jnp.zeros_like(xp)))
                yacc_ref[:, sl] = yd + yoff[:, pr * LANES:(pr + 1) * LANES] + dsk_ref[:, sl] * xc_s[:, sl]
            st_ref[:, gs] = stg * cdec_x[:, gs] + _dot(bg, xwb_s[:, gs], TN)

        y = yacc_ref[...]
        y_ref[...] = y.astype(BF16)
        zf = z_ref[...].astype(F32)
        u = y * zf * _sigmoid(zf)
        for g in range(SSD_GROUPS):
            gs = slice(g * gw, (g + 1) * gw)
            ug = u[:, gs]
            ms = jnp.mean(ug * ug, axis=-1, keepdims=True)
            yssd_ref[:, gs] = (ug * lax.rsqrt(ms + EPS) * nrm_ref[:, gs]).astype(BF16)

    rb = CHUNK // HALO
    return pl.pallas_call(
        body, name="ssd_fwd", grid=(nch,),
        in_specs=[pl.BlockSpec((CHUNK, cd), lambda c: (c, 0)),
                  pl.BlockSpec((HALO, cd), lambda c: (jnp.maximum(c * rb - 1, 0), 0)),
                  pl.BlockSpec((CHUNK, ds), lambda c: (c, 0)),
                  pl.BlockSpec((CHUNK, LANES), lambda c: (c, 0)),
                  _full((CONV_K, cd)), _full((1, cd)), _full((1, LANES)), _full((1, LANES)),
                  _full((1, ds)), _full((1, ds)), _full((LANES, ds))],
        out_specs=[pl.BlockSpec((CHUNK, ds), lambda c: (c, 0)), pl.BlockSpec((CHUNK, ds), lambda c: (c, 0)),
                   pl.BlockSpec((1, ns, ds), lambda c: (c, 0, 0)), pl.BlockSpec((CHUNK, LANES), lambda c: (c, 0)),
                   pl.BlockSpec((CHUNK, cd), lambda c: (c, 0))],
        out_shape=[jax.ShapeDtypeStruct((p, ds), BF16), jax.ShapeDtypeStruct((p, ds), BF16),
                   jax.ShapeDtypeStruct((nch, ns, ds), BF16), jax.ShapeDtypeStruct((p, LANES), F32),
                   jax.ShapeDtypeStruct((p, cd), BF16)],
        scratch_shapes=[pltpu.VMEM((ns, ds), F32), pltpu.VMEM((1, LANES), F32), pltpu.VMEM((CHUNK, ds), F32),
                        pltpu.VMEM((CHUNK, cd), F32), pltpu.VMEM((CHUNK, ds), F32),
                        pltpu.VMEM((CHUNK, ds), BF16), pltpu.VMEM((CHUNK, ds), BF16),
                        pltpu.VMEM((8 + CHUNK, cd), F32)],
        compiler_params=_cparams(("arbitrary",)),
    )(xbc, xbc, z, dtf, conv_w, conv_b, brow, alog, dskip_l, ssd_norm, sel_t)


def _ssd_bwd(dyssd, y, z, xbc, pre, dtf, hin, dcf, conv_w, brow, alog, dskip_l, ssd_norm, sel_t, sel, hs, ha):
    p, cd = xbc.shape
    ds = z.shape[1]
    ns = (cd - ds) // (2 * SSD_GROUPS)
    gw = ds // SSD_GROUPS
    nch = p // CHUNK
    hpg = hs // SSD_GROUPS

    def body(dyssd_ref, y_ref, z_ref, xbc_ref, pre_ref, dtf_ref, hin_ref, dcf_ref, cw_ref, brow_ref,
             alog_ref, dsk_ref, nrm_ref, selt_ref, sel_ref,
             dxbc_ref, dz_ref, ddtf_ref, gcw_ref, gcb_ref, gnrm_ref, gsm_ref,
             dst_ref, nxt_ref, fcar_ref, gdsk_ref, dxc_ref, xc_s, dsl_s, dtx_s, ex_s, wx_s, dy_s, xdtb_s, xwb_s,
             dyb_s, dyeb_s):
        step = pl.program_id(0)
        c = nch - 1 - step

        @pl.when(step == 0)
        def _():
            dst_ref[...] = jnp.zeros_like(dst_ref)
            nxt_ref[...] = jnp.zeros_like(nxt_ref)
            fcar_ref[...] = jnp.zeros_like(fcar_ref)
            gdsk_ref[...] = jnp.zeros_like(gdsk_ref)
            gcw_ref[...] = jnp.zeros_like(gcw_ref)
            gcb_ref[...] = jnp.zeros_like(gcb_ref)
            gnrm_ref[...] = jnp.zeros_like(gnrm_ref)
            gsm_ref[...] = jnp.zeros_like(gsm_ref)

        rows = lax.broadcasted_iota(jnp.int32, (CHUNK, 1), 0)
        rowmask = jnp.where((rows >= PADN) | (c > 0), 1.0, 0.0)
        ri, ci = _tri_mats()
        causal = ri >= ci
        anti = ci >= ri
        tri = jnp.where(causal, 1.0, 0.0).astype(BF16)
        rtri = jnp.where(anti, 1.0, 0.0).astype(BF16)

        pre = pre_ref[...].astype(F32)
        sg = _sigmoid(pre)
        xc_s[...] = pre * sg * rowmask
        dsl_s[...] = sg * (1.0 + pre * (1.0 - sg)) * rowmask

        dtr, dt, a_row, run, is_dt, is_f = _ssd_scalars(dtf_ref, brow_ref, alog_ref, rowmask, hs, ha, tri)
        cs = jnp.where(is_dt, run, 0.0)
        cl = cs[CHUNK - 1:CHUNK, :]
        selt = selt_ref[...]
        selm = sel_ref[...]
        dtx_s[...] = _dot_sel(dt, selt)
        ex_s[...] = _dot_sel(jnp.exp(cs), selt)
        wx_s[...] = _dot_sel(jnp.exp(cl - cs), selt)
        cdec = jnp.exp(cl)
        cdec_x = _dot_sel(jnp.broadcast_to(cdec, (8, LANES)), selt)[0:1, :]
        cs_t = cs.T
        xdt = xc_s[:, :ds] * dtx_s[...]
        xdtb_s[...] = xdt.astype(BF16)
        xwb_s[...] = (xdt * wx_s[...]).astype(BF16)

        yv = y_ref[...].astype(F32)
        zf = z_ref[...].astype(F32)
        sz = _sigmoid(zf)
        u = yv * zf * sz
        dyo = dyssd_ref[...].astype(F32)
        du_parts = []
        for g in range(SSD_GROUPS):
            gs = slice(g * gw, (g + 1) * gw)
            ug = u[:, gs]
            rstd = lax.rsqrt(jnp.mean(ug * ug, axis=-1, keepdims=True) + EPS)
            yhat = ug * rstd
            dyg = dyo[:, gs]
            gnrm_ref[0:1, gs] += jnp.sum(dyg * yhat, axis=0, keepdims=True)
            dyh = dyg * nrm_ref[:, gs]
            du_parts.append(rstd * (dyh - yhat * jnp.mean(dyh * yhat, axis=-1, keepdims=True)))
        du = jnp.concatenate(du_parts, axis=1)
        dy = du * zf * sz
        dz_ref[...] = (du * yv * sz * (1.0 + zf * (1.0 - sz))).astype(BF16)
        dy_s[...] = dy
        dyb_s[...] = dy.astype(BF16)
        dyeb_s[...] = (dy * ex_s[...]).astype(BF16)
        gdsk_ref[...] += jnp.sum(dy * xc_s[:, :ds], axis=0, keepdims=True)
        lane = lax.broadcasted_iota(jnp.int32, (1, LANES), 1)
        half0 = lane < HEAD_DIM
        x_parts, yo_parts, t4_parts = [], [], []
        dcs = jnp.zeros((CHUNK, LANES), F32)
        for g in range(SSD_GROUPS):
            gs = slice(g * gw, (g + 1) * gw)
            bsl = slice(ds + g * ns, ds + (g + 1) * ns)
            csl = slice(ds + SSD_GROUPS * ns + g * ns, ds + SSD_GROUPS * ns + (g + 1) * ns)
            bg = xc_s[:, bsl].astype(BF16)
            cg = xc_s[:, csl].astype(BF16)
            gm = _dot(cg, bg, NT)
            gm_t = _dot(bg, cg, NT)
            stg_b = hin_ref[0, :, gs]
            dstg = dst_ref[:, gs]
            dstg_b = dstg.astype(BF16)
            t4_parts.append(jnp.sum(dstg * stg_b.astype(F32), axis=0, keepdims=True))
            zst = _dot(bg, dstg_b) * wx_s[:, gs]
            x_parts.append(xc_s[:, gs] * dtx_s[:, gs] * zst)
            yo_parts.append(dy_s[:, gs] * (_dot(cg, stg_b) * ex_s[:, gs]))
            dgsum = jnp.zeros((CHUNK, CHUNK), F32)
            dgtsum = jnp.zeros((CHUNK, CHUNK), F32)
            for pr in range(gw // LANES):
                sl = slice(g * gw + pr * LANES, g * gw + (pr + 1) * LANES)
                xp = xdtb_s[:, sl]
                dyp = dyb_s[:, sl]
                dxd = zst[:, pr * LANES:(pr + 1) * LANES]
                for j in range(2):
                    h = g * hpg + 2 * pr + j
                    sel_l = half0 if j == 0 else jnp.logical_not(half0)
                    seg = cs[:, h:h + 1] - cs_t[h:h + 1, :]
                    lm = jnp.where(causal, jnp.exp(jnp.minimum(seg, 0.0)), 0.0)
                    lmt = lm.T
                    dyp_m = jnp.where(sel_l, dyp, jnp.zeros_like(dyp))
                    xp_m = jnp.where(sel_l, xp, jnp.zeros_like(xp))
                    dxd = dxd + _dot((gm_t * lmt).astype(BF16), dyp_m)
                    dg = _dot(dyp_m, xp, NT) * lm
                    dgt = _dot(xp_m, dyp, NT) * lmt
                    dgsum = dgsum + dg
                    dgtsum = dgtsum + dgt
                    qrow = (jnp.sum(dg * gm, axis=1, keepdims=True) - jnp.sum(dgt * gm_t, axis=1, keepdims=True))
                    dcs = dcs + jnp.where(lane == h, qrow, 0.0)
                dxc_ref[:, sl] = dxd
            dxc_ref[:, csl] = _dot(dgsum.astype(BF16), bg) + _dot(dyeb_s[:, gs], stg_b, NT)
            dxc_ref[:, bsl] = _dot(dgtsum.astype(BF16), cg) + _dot(xwb_s[:, gs], dstg_b, NT)
            dst_ref[:, gs] = dstg * cdec_x[:, gs] + _dot(cg, dyeb_s[:, gs], TN)

        dxdt = dxc_ref[:, :ds]
        xst = _dot_sel(jnp.concatenate(x_parts, axis=1), selm)
        yo = _dot_sel(jnp.concatenate(yo_parts, axis=1), selm)
        t4 = _dot_sel(jnp.concatenate([jnp.concatenate(t4_parts, axis=1), jnp.zeros((7, ds), F32)], axis=0), selm)
        dcl = jnp.sum(xst, axis=0, keepdims=True) + cdec * t4[0:1, :]
        dcs = dcs + yo - xst + jnp.where(rows == CHUNK - 1, dcl, 0.0)
        da_ = _dot_tri(rtri, dcs)
        ddt = _dot_sel(dxdt * xc_s[:, :ds], selm) + da_ * a_row
        dcf_blk = dcf_ref[...]
        dlogf = _dot_tri(rtri, dcf_blk) + fcar_ref[...]
        fcar_ref[...] += jnp.sum(dcf_blk, axis=0, keepdims=True)
        sgd = _sigmoid(dtr)
        ddtf = (jnp.where(is_dt, ddt * sgd, 0.0) + jnp.where(is_f, dlogf * (1.0 - sgd), 0.0)) * rowmask
        ddtf_ref[...] = ddtf
        gsm_ref[0:1, :] += jnp.sum(ddtf, axis=0, keepdims=True)
        gsm_ref[1:2, :] += jnp.sum(da_ * dt, axis=0, keepdims=True) * a_row

        dxc_ref[:, :ds] = dxdt * dtx_s[...] + dsk_ref[...] * dy_s[...]
        dpre = dxc_ref[...] * dsl_s[...]
        nxt_ref[0:CHUNK, :] = dpre
        gcb_ref[0:1, :] += jnp.sum(dpre, axis=0, keepdims=True)
        xr = xbc_ref[...].astype(F32)
        gcw_ref[CONV_K - 1:CONV_K, :] += jnp.sum(dpre * xr, axis=0, keepdims=True)
        dxr = cw_ref[CONV_K - 1:CONV_K, :] * dpre
        for j in range(1, CONV_K):
            up = nxt_ref[j:j + CHUNK, :]
            gcw_ref[CONV_K - 1 - j:CONV_K - j, :] += jnp.sum(up * xr, axis=0, keepdims=True)
            dxr = dxr + cw_ref[CONV_K - 1 - j:CONV_K - j, :] * up
        nxt_ref[CHUNK:, :] = dpre[0:8, :]
        dxbc_ref[...] = dxr.astype(BF16)

        @pl.when(step == nch - 1)
        def _():
            gsm_ref[2:3, :] = _dot_sel(jnp.broadcast_to(gdsk_ref[...], (8, ds)), selm)[0:1, :]

    rev = lambda s: nch - 1 - s
    blk = lambda w: pl.BlockSpec((CHUNK, w), lambda s: (rev(s), 0))
    return pl.pallas_call(
        body, name="ssd_bwd", grid=(nch,),
        in_specs=[blk(ds), blk(ds), blk(ds), blk(cd), blk(cd),
                  blk(LANES), pl.BlockSpec((1, ns, ds), lambda s: (rev(s), 0, 0)), blk(LANES),
                  _full((CONV_K, cd)), _full((1, LANES)), _full((1, LANES)),
                  _full((1, ds)), _full((1, ds)), _full((LANES, ds)), _full((ds, LANES))],
        out_specs=[blk(cd), blk(ds), blk(LANES), _full((8, cd)), _full((8, cd)), _full((8, ds)), _full((8, LANES))],
        out_shape=[jax.ShapeDtypeStruct((p, cd), BF16), jax.ShapeDtypeStruct((p, ds), BF16),
                   jax.ShapeDtypeStruct((p, LANES), F32), jax.ShapeDtypeStruct((8, cd), F32),
                   jax.ShapeDtypeStruct((8, cd), F32), jax.ShapeDtypeStruct((8, ds), F32),
                   jax.ShapeDtypeStruct((8, LANES), F32)],
        scratch_shapes=[pltpu.VMEM((ns, ds), F32), pltpu.VMEM((CHUNK + 8, cd), F32), pltpu.VMEM((1, LANES), F32),
                        pltpu.VMEM((1, ds), F32), pltpu.VMEM((CHUNK, cd), F32),
                        pltpu.VMEM((CHUNK, cd), F32), pltpu.VMEM((CHUNK, cd), F32),
                        pltpu.VMEM((CHUNK, ds), F32), pltpu.VMEM((CHUNK, ds), F32), pltpu.VMEM((CHUNK, ds), F32),
                        pltpu.VMEM((CHUNK, ds), F32), pltpu.VMEM((CHUNK, ds), BF16), pltpu.VMEM((CHUNK, ds), BF16),
                        pltpu.VMEM((CHUNK, ds), BF16), pltpu.VMEM((CHUNK, ds), BF16)],
        compiler_params=_cparams(("arbitrary",)),
    )(dyssd, y, z, xbc, pre, dtf, hin, dcf, conv_w, brow, alog, dskip_l, ssd_norm, sel_t, sel)


def _attn_fwd(q, k, v, ck, blk):
    p, da = q.shape
    npair, nkb = ck.shape[0], ck.shape[1]
    scale = 1.0 / math.sqrt(HEAD_DIM)

    def body(q_ref, k_ref, v_ref, ck_ref, o_ref, lse_ref):
        i = pl.program_id(1)
        lane = lax.broadcasted_iota(jnp.int32, (1, LANES), 1)
        sels = [lane < HEAD_DIM, lane >= HEAD_DIM]
        ones = [jnp.where(lane == HEAD_DIM, 1.0, 0.0).astype(BF16), jnp.where(lane == 0, 1.0, 0.0).astype(BF16)]
        qb = q_ref[...] * scale

        def step(kb, carry, masked, nk=1):
            r0 = pl.multiple_of(kb * blk, blk)
            ks = k_ref[pl.ds(r0, nk * blk), :]
            vs = v_ref[pl.ds(r0, nk * blk), :]
            kk = jnp.concatenate([jnp.where(sel, ks, jnp.zeros_like(ks)) for sel in sels], axis=0)
            s_both = _dot(qb, kk, NT)
            out = []
            for j in range(2):
                m, acc = carry[2 * j], carry[2 * j + 1]
                ckr = jnp.concatenate([ck_ref[0, kb + t, j:j + 1, :] for t in range(nk)], axis=1)
                s = s_both[:, j * nk * blk:(j + 1) * nk * blk] - ckr
                if masked:
                    col = lax.broadcasted_iota(jnp.int32, (blk, nk * blk), 1) - (nk - 1) * blk
                    s = jnp.where(col <= lax.broadcasted_iota(jnp.int32, (blk, nk * blk), 0), s, NEG)
                mn = jnp.maximum(m, jnp.max(s, axis=-1, keepdims=True))
                pr = jnp.exp(s - mn).astype(BF16)
                acc = jnp.exp(m - mn) * acc + _dot(pr, jnp.where(sels[j], vs, ones[j]))
                out += [mn, acc]
            return tuple(out)

        init = (jnp.full((blk, 1), NEG, F32), jnp.zeros((blk, LANES), F32)) * 2

        def finish(carry):
            m0, a0, m1, a1 = carry
            l0 = a0[:, HEAD_DIM:HEAD_DIM + 1]
            l1 = a1[:, 0:1]
            o_ref[...] = jnp.where(sels[0], a0 / l0, a1 / l1).astype(BF16)
            lse_ref[...] = jnp.where(sels[0], m0 + jnp.log(l0), m1 + jnp.log(l1))

        @pl.when(i == 0)
        def _():
            finish(step(0, init, True))

        def sweep(last):
            below = i + 1 - last
            n4 = below // 4
            n2 = (below - 4 * n4) // 2
            carry = lax.fori_loop(0, n4, lambda t, c: step(4 * t, c, False, 4), init)
            carry = lax.fori_loop(0, n2, lambda t, c: step(4 * n4 + 2 * t, c, False, 2), carry)
            carry = lax.fori_loop(4 * n4 + 2 * n2, below, lambda kb, c: step(kb, c, False), carry)
            finish(step(below, carry, True, last))

        @pl.when((i > 0) & (i < 3))
        def _():
            sweep(2)

        @pl.when(i >= 3)
        def _():
            sweep(4)

    return pl.pallas_call(
        body, name="attn_fwd", grid=(npair, p // blk),
        in_specs=[pl.BlockSpec((blk, LANES), lambda h, i: (i, h)),
                  pl.BlockSpec((p, LANES), lambda h, i: (0, h)), pl.BlockSpec((p, LANES), lambda h, i: (0, h)),
                  pl.BlockSpec((1, nkb, 8, blk), lambda h, i: (h, 0, 0, 0))],
        out_specs=[pl.BlockSpec((blk, LANES), lambda h, i: (i, h)), pl.BlockSpec((blk, LANES), lambda h, i: (i, h))],
        out_shape=[jax.ShapeDtypeStruct((p, da), BF16), jax.ShapeDtypeStruct((p, da), F32)],
        compiler_params=_cparams(("parallel", "arbitrary")),
    )(q, k, v, ck)


def _attn_bwd(q, k, v, o, do, lse_rep, ck, blk):
    p, da = q.shape
    npair, nkb = ck.shape[0], ck.shape[1]
    nq = p // blk
    scale = 1.0 / math.sqrt(HEAD_DIM)

    def body(k_ref, v_ref, q_ref, do_ref, o_ref, lse_ref, ck_ref, dk_ref, dv_ref, dq_ref, dcs_ref, rsum_ref, dq_acc):
        jb = pl.program_id(1)

        @pl.when(jb == 0)
        def _():
            dq_acc[...] = jnp.zeros_like(dq_acc)

        ks = k_ref[...]
        vs = v_ref[...]
        lane = lax.broadcasted_iota(jnp.int32, (1, LANES), 1)
        sels = [lane < HEAD_DIM, lane >= HEAD_DIM]
        ones = [jnp.where(lane == HEAD_DIM, 1.0, 0.0).astype(BF16), jnp.where(lane == 0, 1.0, 0.0).astype(BF16)]
        kss = ks * scale
        kmo = [jnp.where(sels[j], kss, ones[j]) for j in range(2)]

        def step(ib, carry, masked, nb=1):
            rows = nb * blk
            r0 = pl.multiple_of(ib * blk, blk)
            qb = q_ref[pl.ds(r0, rows), :] * scale
            dob = do_ref[pl.ds(r0, rows), :]
            prod = dob.astype(F32) * o_ref[pl.ds(r0, rows), :].astype(F32)
            out = []
            for j in range(2):
                dk, dv = carry[2 * j], carry[2 * j + 1]
                qm = jnp.where(sels[j], qb, jnp.zeros_like(qb))
                dom = jnp.where(sels[j], dob, jnp.zeros_like(dob))
                lse = lse_ref[pl.ds(r0, rows), HEAD_DIM * j:HEAD_DIM * j + 1]
                dlt = jnp.sum(jnp.where(sels[j], prod, 0.0), axis=-1, keepdims=True)
                s = _dot(qm, ks, NT) - ck_ref[0, 0, j:j + 1, :] - lse
                pm = jnp.exp(jnp.minimum(s, 0.0))
                if masked:
                    causal = (lax.broadcasted_iota(jnp.int32, (rows, blk), 1)
                              <= lax.broadcasted_iota(jnp.int32, (rows, blk), 0))
                    pm = jnp.where(causal, pm, 0.0)
                ds_b = (pm * (_dot(dom, vs, NT) - dlt)).astype(BF16)
                dv = dv + _dot(pm.astype(BF16), dom, TN)
                dk = dk + _dot(ds_b, jnp.where(sels[j], qb, ones[j]), TN)
                dq_acc[pl.ds(r0, rows), LANES * j:LANES * (j + 1)] += _dot(ds_b, kmo[j])
                out += [dk, dv]
            return tuple(out)

        pair8 = lambda c0, c1: jnp.where(lane == 0, c0, jnp.where(lane == 1, c1, 0.0)).T[0:8]
        zero = jnp.zeros((blk, LANES), F32)
        init = (zero, zero, zero, zero)

        def finish(carry):
            dk0, dv0, dk1, dv1 = carry
            dk_ref[...] = jnp.where(sels[0], dk0, dk1).astype(BF16)
            dv_ref[...] = (dv0 + dv1).astype(BF16)
            dcs_ref[0] = pair8(dk0[:, HEAD_DIM:HEAD_DIM + 1], dk1[:, 0:1])

        @pl.when(jb == nq - 1)
        def _():
            finish(step(jb, init, True))

        @pl.when(jb < nq - 1)
        def _():
            carry = step(jb, init, True, 2)
            n4 = (nq - 2 - jb) // 4
            n2 = (nq - 2 - jb - 4 * n4) // 2
            carry = lax.fori_loop(0, n4, lambda t, c: step(jb + 2 + 4 * t, c, False, 4), carry)
            carry = lax.fori_loop(0, n2, lambda t, c: step(jb + 2 + 4 * n4 + 2 * t, c, False, 2), carry)
            finish(lax.fori_loop(jb + 2 + 4 * n4 + 2 * n2, nq, lambda ib, c: step(ib, c, False), carry))

        @pl.when(jb == nkb - 1)
        def _():
            a0 = dq_acc[:, :LANES]
            a1 = dq_acc[:, LANES:]
            dq_ref[...] = jnp.where(sels[0], a0, a1).astype(BF16)
            rsum_ref[0] = pair8(a0[:, HEAD_DIM:HEAD_DIM + 1], a1[:, 0:1])

    colblk = pl.BlockSpec((blk, LANES), lambda h, j: (j, h))
    colfull = pl.BlockSpec((p, LANES), lambda h, j: (0, h))
    ckspec = pl.BlockSpec((1, 1, 8, blk), lambda h, j: (h, j, 0, 0))
    return pl.pallas_call(
        body, name="attn_bwd", grid=(npair, nkb),
        in_specs=[colblk, colblk, colfull, colfull, colfull, colfull, ckspec],
        out_specs=[colblk, colblk, colfull, pl.BlockSpec((1, 8, blk), lambda h, j: (h, 0, j)),
                   pl.BlockSpec((1, 8, p), lambda h, j: (h, 0, 0))],
        out_shape=[jax.ShapeDtypeStruct((p, da), BF16), jax.ShapeDtypeStruct((p, da), BF16),
                   jax.ShapeDtypeStruct((p, da), BF16), jax.ShapeDtypeStruct((npair, 8, p), F32),
                   jax.ShapeDtypeStruct((npair, 8, p), F32)],
        scratch_shapes=[pltpu.VMEM((p, 2 * LANES), F32)],
        compiler_params=_cparams(("parallel", "arbitrary")),
    )(k, v, q, do, o, lse_rep, ck)


def _tail_fwd(yssd, o, zatt, graw, head, x2, tgt2, wps, wpa, wout, gate_bias, norm_post, tm):
    p, ds = yssd.shape
    da = o.shape[1]
    d = x2.shape[1]
    nsub = tm // CHUNK

    def body(yssd_ref, o_ref, zatt_ref, g_ref, head_ref, *rest):
        x_refs, t_refs = rest[:nsub], rest[nsub:2 * nsub]
        (wps_ref, wpa_ref, wout_ref, gb_ref, np_ref,
         yatt_ref, mrg_ref, a_ref, b_ref, dzo_ref, dout_ref, red_ref) = rest[2 * nsub:]
        i = pl.program_id(0)

        @pl.when(i == 0)
        def _():
            red_ref[...] = jnp.zeros_like(red_ref)

        first = jnp.where(i == 0, head_ref[...], x_refs[0][...])
        h = jnp.concatenate([first] + [r[...] for r in x_refs[1:]], axis=0)
        tgt = jnp.concatenate([r[...] for r in t_refs], axis=0)
        rows = lax.broadcasted_iota(jnp.int32, (tm, 1), 0)
        valid = jnp.where((i > 0) | (rows >= CHUNK), 1.0, 0.0)
        ob = o_ref[...].astype(F32)
        za = zatt_ref[...].astype(F32)
        yatt_b = (ob * za * _sigmoid(za)).astype(BF16)
        yatt_ref[...] = yatt_b
        a = _dot(yssd_ref[...], wps_ref[...])
        b = _dot(yatt_b, wpa_ref[...])
        a_ref[...] = a.astype(BF16)
        b_ref[...] = b.astype(BF16)
        gr = g_ref[...].astype(F32) + gb_ref[...]
        mrg_b = (_sigmoid(gr[:, :d]) * a + _sigmoid(gr[:, d:]) * b).astype(BF16)
        mrg_ref[...] = mrg_b
        zo = _dot(mrg_b, wout_ref[...])
        rstd = lax.rsqrt(jnp.mean(zo * zo, axis=-1, keepdims=True) + EPS)
        zh = zo * rstd
        npw = np_ref[...]
        err = (h + zh * npw - tgt) * valid
        dout = err * (1.0 / d)
        dout_ref[...] = dout
        dzh = dout * npw
        dzo_ref[...] = (rstd * (dzh - zh * jnp.mean(dzh * zh, axis=-1, keepdims=True))).astype(BF16)
        red_ref[0:1, :] += jnp.sum(dout * zh, axis=0, keepdims=True)
        red_ref[1:2, 0:1] += jnp.sum(jnp.sum(err * err, axis=1, keepdims=True), axis=0, keepdims=True) * (0.5 / d)

    row = lambda w: pl.BlockSpec((tm, w), lambda i: (i, 0))
    once = lambda shape: pl.BlockSpec(shape, lambda i: (0,) * len(shape), pipeline_mode=pl.Buffered(1))
    subs = _x_row_specs(tm, d)
    sd = jax.ShapeDtypeStruct
    return pl.pallas_call(
        body, name="tail_fwd", grid=(p // tm,),
        in_specs=[row(ds), row(da), row(da), row(2 * d), _full((CHUNK, d))] + subs + subs
                 + [once((ds, d)), once((da, d)), once((d, d)), _full((1, 2 * d)), _full((1, d))],
        out_specs=[row(da), row(d), row(d), row(d), row(d), row(d), _full((8, d))],
        out_shape=[sd((p, da), BF16), sd((p, d), BF16), sd((p, d), BF16), sd((p, d), BF16), sd((p, d), BF16),
                   sd((p, d), F32), sd((8, d), F32)],
        compiler_params=_cparams(("arbitrary",)),
    )(yssd, o, zatt, graw, head, *([x2] * nsub), *([tgt2] * nsub), wps, wpa, wout, gate_bias, norm_post)


def _tail_bwd(dzo, a_b, b_b, graw, o, zatt, wps, wpa, wout, gate_bias, tm):
    p, d = dzo.shape
    ds, da = wps.shape[0], wpa.shape[0]

    def body(dzo_ref, a_ref, b_ref, g_ref, o_ref, zatt_ref, wps_ref, wpa_ref, wout_ref, gb_ref,
             da_ref, db_ref, dg_ref, dyssd_ref, do_ref, dzatt_ref, red_ref):
        i = pl.program_id(0)

        @pl.when(i == 0)
        def _():
            red_ref[...] = jnp.zeros_like(red_ref)

        gr = g_ref[...].astype(F32) + gb_ref[...]
        gs = _sigmoid(gr[:, :d])
        ga = _sigmoid(gr[:, d:])
        dm = _dot(dzo_ref[...], wout_ref[...], NT)
        da_b = (gs * dm).astype(BF16)
        db_b = (ga * dm).astype(BF16)
        da_ref[...] = da_b
        db_ref[...] = db_b
        dgs = dm * a_ref[...].astype(F32) * gs * (1.0 - gs)
        dga = dm * b_ref[...].astype(F32) * ga * (1.0 - ga)
        dg_ref[:, :d] = dgs.astype(BF16)
        dg_ref[:, d:] = dga.astype(BF16)
        red_ref[0:1, :d] += jnp.sum(dgs, axis=0, keepdims=True)
        red_ref[0:1, d:] += jnp.sum(dga, axis=0, keepdims=True)
        dyssd_ref[...] = _dot(da_b, wps_ref[...], NT).astype(BF16)
        dya = _dot(db_b, wpa_ref[...], NT)
        ob = o_ref[...].astype(F32)
        za = zatt_ref[...].astype(F32)
        sza = _sigmoid(za)
        do_ref[...] = (dya * za * sza).astype(BF16)
        dzatt_ref[...] = (dya * ob * sza * (1.0 + za * (1.0 - sza))).astype(BF16)

    row = lambda w: pl.BlockSpec((tm, w), lambda i: (i, 0))
    once = lambda shape: pl.BlockSpec(shape, lambda i: (0,) * len(shape), pipeline_mode=pl.Buffered(1))
    sd = jax.ShapeDtypeStruct
    return pl.pallas_call(
        body, name="tail_bwd", grid=(p // tm,),
        in_specs=[row(d), row(d), row(d), row(2 * d), row(da), row(da),
                  once((ds, d)), once((da, d)), once((d, d)), _full((1, 2 * d))],
        out_specs=[row(d), row(d), row(2 * d), row(ds), row(da), row(da), _full((8, 2 * d))],
        out_shape=[sd((p, d), BF16), sd((p, d), BF16), sd((p, 2 * d), BF16), sd((p, ds), BF16), sd((p, da), BF16),
                   sd((p, da), BF16), sd((8, 2 * d), F32)],
        compiler_params=_cparams(("arbitrary",)),
    )(dzo, a_b, b_b, graw, o, zatt, wps, wpa, wout, gate_bias)


def _adamw_math(w, g, m, v):
    m2 = ADAM_B1 * m + (1.0 - ADAM_B1) * g
    v2 = ADAM_B2 * v + (1.0 - ADAM_B2) * (g * g)
    m_hat = m2 / (1.0 - ADAM_B1 ** ADAM_STEP)
    v_hat = v2 / (1.0 - ADAM_B2 ** ADAM_STEP)
    delta = -ADAM_LR * (m_hat / (jnp.sqrt(v_hat) + ADAM_EPS) + ADAM_WD * w)
    return delta, m2, v2


def _adamw_small(params, red, name):
    names = list(params)
    n = len(names)
    extra = [params[k][3] for k in names if not isinstance(params[k][3], tuple)]

    def body(*refs):
        w_refs, m_refs, v_refs = refs[:n], refs[n:2 * n], refs[2 * n:3 * n]
        red_ref = refs[3 * n]
        g_refs = iter(refs[3 * n + 1:3 * n + 1 + len(extra)])
        outs = refs[3 * n + 1 + len(extra):]
        for i, k in enumerate(names):
            where = params[k][3]
            rows, cols = w_refs[i].shape
            if isinstance(where, tuple):
                g = red_ref[where[0]:where[0] + rows, where[1]:where[1] + cols]
            else:
                g = next(g_refs)[...]
            delta, m2, v2 = _adamw_math(w_refs[i][...], g, m_refs[i][...], v_refs[i][...])
            for o, val in zip(outs[4 * i:4 * i + 4], (g, delta, m2, v2)):
                o[...] = val

    vm = pl.BlockSpec(memory_space=pltpu.VMEM)
    ws, ms, vs = ([params[k][j] for k in names] for j in range(3))
    out = pl.pallas_call(
        body, name=name,
        out_shape=[jax.ShapeDtypeStruct(w.shape, F32) for w in ws for _ in range(4)],
        in_specs=[vm] * (3 * n + 1 + len(extra)), out_specs=[vm] * (4 * n),
    )(*ws, *ms, *vs, red, *extra)
    return {k: tuple(out[4 * i:4 * i + 4]) for i, k in enumerate(names)}


def _adamw(w, g, m, v, name, parts=False, part_row0=0):
    r, cdim = w.shape
    tr, tc, by_rows = _tiles_2d(r, cdim)
    pick = (lambda i: (i, 0)) if by_rows else (lambda i: (0, i))
    assert part_row0 % tr == 0
    gpick = (lambda i: (i + part_row0 // tr, 0)) if by_rows else (lambda i: (part_row0 // tr, i))

    def body(w_ref, g_ref, m_ref, v_ref, go_ref, d_ref, mo_ref, vo_ref):
        if parts:
            g = g_ref[0].astype(F32)
            for s in range(1, g_ref.shape[0]):
                g = g + g_ref[s].astype(F32)
        else:
            g = g_ref[...]
        delta, m2, v2 = _adamw_math(w_ref[...], g, m_ref[...], v_ref[...])
        go_ref[...] = g
        d_ref[...] = delta
        mo_ref[...] = m2
        vo_ref[...] = v2

    blk = pl.BlockSpec((tr, tc), pick)
    gspec = pl.BlockSpec((g.shape[0], tr, tc), lambda i: (0,) + gpick(i)) if parts else blk
    return pl.pallas_call(
        body, name=name, grid=((r // tr) * (cdim // tc),),
        in_specs=[blk, gspec, blk, blk], out_specs=[blk] * 4,
        out_shape=[jax.ShapeDtypeStruct((r, cdim), F32)] * 4,
        compiler_params=_cparams(("parallel",)),
    )(w, g, m, v)


def _pad_cols(a, width):
    return jnp.pad(a, ((0, 0), (0, width - a.shape[1])))


def _pack_small_shard(conv_w_sh, meta_sh, width):
    return jnp.concatenate([_pad_cols(conv_w_sh, width), jnp.zeros((4, width), F32), _pad_cols(meta_sh, width)], axis=0)


def _pack_small_rep(norm_pre, norm_post, gate_bias, ssd_norm, conv_b, misc, width):
    rows = [norm_pre, norm_post, gate_bias, ssd_norm, conv_b, misc]
    return jnp.concatenate([_pad_cols(r, width) for r in rows] + [jnp.zeros((2, width), F32)], axis=0)


def kernel(x, meta_tokens, norm_pre, w_in, conv_w, conv_b, dt_bias, a_log, d_skip, ssd_norm, fgate_bias, gate_bias, w_proj_ssd, w_proj_att, w_out, norm_post, loss_target, m_meta_tokens, m_norm_pre, m_w_in, m_conv_w, m_conv_b, m_dt_bias, m_a_log, m_d_skip, m_ssd_norm, m_fgate_bias, m_gate_bias, m_w_proj_ssd, m_w_proj_att, m_w_out, m_norm_post, v_meta_tokens, v_norm_pre, v_w_in, v_conv_w, v_conv_b, v_dt_bias, v_a_log, v_d_skip, v_ssd_norm, v_fgate_bias, v_gate_bias, v_w_proj_ssd, v_w_proj_att, v_w_out, v_norm_post):
    seq, d = x.shape[1], x.shape[2]
    p = seq + CHUNK
    hs, ha = dt_bias.shape[1], fgate_bias.shape[1]
    ds, cd = ssd_norm.shape[1], conv_b.shape[1]
    da = ha * HEAD_DIM
    nc8 = w_in.shape[2]
    cws = cd // N_DEV
    msh = d // N_DEV
    r1, r2, r3 = ds // N_DEV, da // N_DEV, d // N_DEV
    me = _dev_index(*_my_pos())
    x2, tgt2 = x[0], loss_target[0]

    win_sh = jnp.transpose(w_in[0]).astype(BF16)
    rows_sh = jnp.concatenate([w_proj_ssd[0], w_proj_att[0], w_out[0]], axis=0).astype(BF16)
    small_sh = _pack_small_shard(conv_w[0], meta_tokens, cws)
    win_all, small_all = _all_gather([win_sh, small_sh], "gather_weights")
    rows_sh, win_all = lax.optimization_barrier((rows_sh, win_all))
    rows_sems, rows_thru, rows_land, rows_token = _bcast_start(rows_sh, "gather_rows_start")
    cuts = [0, ds, ds + cd, ds + cd + hs, ds + cd + hs + da, ds + cd + hs + 2 * da, ds + cd + hs + 3 * da,
            ds + cd + hs + 4 * da, ds + cd + hs + 4 * da + ha, ds + cd + hs + 4 * da + ha + 2 * d]

    def piece_rows(r0, r1):
        parts = [win_all[s, max(r0, s * nc8) - s * nc8:min(r1, (s + 1) * nc8) - s * nc8]
                 for s in range(N_DEV) if max(r0, s * nc8) < min(r1, (s + 1) * nc8)]
        return parts[0] if len(parts) == 1 else jnp.concatenate(parts, axis=0)

    w_z, w_xbc, w_dt, w_zatt, w_q, w_k, w_v, w_f, w_g = [piece_rows(cuts[i], cuts[i + 1]) for i in range(9)]
    w_dtf = jnp.concatenate([w_dt, w_f, jnp.zeros((LANES - hs - ha, d), BF16)], axis=0)
    conv_w_full = jnp.transpose(small_all[:, 0:CONV_K, :], (1, 0, 2)).reshape(CONV_K, cd)
    meta_full = jnp.transpose(small_all[:, 8:8 + N_META, :msh], (1, 0, 2)).reshape(N_META, d)
    head = jnp.concatenate([jnp.zeros((PADN, d), F32), meta_full + rows_token[0:1, 0:1]], axis=0)

    tm = _att_block(p)
    u = _prenorm_fwd(head, x2, norm_pre, tm)
    seg_w = [w_z, w_xbc, w_zatt, w_q, w_k, w_v, w_g]
    zs, xbc, zatt, q, k, v, graw = [
        _mm(u, w, "nt", BF16, _tile(p, (2112, 1408, tm)), _tile(w.shape[0], (1024, 512, 256, 128)), "inproj_%d" % i)
        for i, w in enumerate(seg_w)]
    dtf = _mm(u, w_dtf, "nt", F32, _tile(p, (1408, tm)), LANES, "inproj_dtf")

    brow = jnp.concatenate([dt_bias, fgate_bias, jnp.zeros((1, LANES - hs - ha), F32)], axis=1)
    alog_row = _pad_cols(a_log, LANES)
    dskip_l = jnp.repeat(d_skip, HEAD_DIM, axis=1)
    sel_t = (lax.broadcasted_iota(jnp.int32, (LANES, ds), 1) // HEAD_DIM
             == lax.broadcasted_iota(jnp.int32, (LANES, ds), 0)).astype(BF16)
    sel = sel_t.T
    y, yssd, hin, cf, pre = _ssd_fwd(xbc, zs, dtf, conv_w_full, conv_b, brow, alog_row, dskip_l, ssd_norm, sel_t, hs, ha)

    blk = _att_block(p)
    nkb, npair = p // blk, ha // 2
    cum = jnp.where(lax.broadcasted_iota(jnp.int32, (p, 1), 0) < PADN, -NEG, cf[:, hs:hs + ha])
    ck = jnp.transpose(cum.T.reshape(npair, 2, nkb, blk), (0, 2, 1, 3))
    ck = jnp.pad(ck, ((0, 0), (0, 0), (0, 6), (0, 0)))
    o, lse_rep = _attn_fwd(q, k, v, ck, blk)

    rows_all = _bcast_wait(rows_sems, rows_thru, rows_land, lse_rep, "gather_rows_wait")
    wps = rows_all[:, :r1].reshape(ds, d)
    wpa = rows_all[:, r1:r1 + r2].reshape(da, d)
    wout = rows_all[:, r1 + r2:].reshape(d, d)

    yatt, mrg, a_b, b_b, dzo, dout, red_fwd = _tail_fwd(
        yssd, o, zatt, graw, head, x2, tgt2, wps, wpa, wout, gate_bias, norm_post, tm)
    da_, db_, dgraw, dyssd, d_o, dzatt, red_bwd = _tail_bwd(dzo, a_b, b_b, graw, o, zatt, wps, wpa, wout, gate_bias, tm)

    tw = _tile(d, (512, 256, 128))
    g_wout = _mm(mrg, dzo, "tn", BF16, tw, d, "wgrad_out")
    g_wps = _mm(yssd, da_, "tn", BF16, _tile(ds, (512, 256, 128)), d, "wgrad_ps")
    g_wpa = _mm(yatt, db_, "tn", BF16, _tile(da, (512, 256, 128)), d, "wgrad_pa")

    core = lax.axis_index("c").astype(jnp.int32).reshape(1)
    chip = me // 2
    grows_parts = jnp.concatenate([g_wps.reshape(N_DEV, r1, d), g_wpa.reshape(N_DEV, r2, d),
                                   g_wout.reshape(N_DEV, r3, d)], axis=1)
    (sib_rows,) = _exchange_sibling([grows_parts], "scatter_rows_sibling")
    chip_rows = _pair_add(grows_parts, sib_rows, core, "pair_add_rows")
    r_sems, r_thru, r_lands, r_token = _exchange_chips_start([chip_rows], "scatter_rows_start")

    dk, dv, dq, dcs, rsum = _attn_bwd(q, k, v, o, d_o, lse_rep, ck + r_token[0:1, 0:1], blk)
    dcum = (rsum - dcs)[:, 0:2, :].reshape(ha, p).T
    dcf = jnp.pad(dcum, ((0, 0), (hs, LANES - hs - ha)))
    dxbc, dzs, ddtf, gcw, gcb, gnrm, gsm = _ssd_bwd(
        dyssd, y, zs, xbc, pre, dtf, hin, dcf, conv_w_full, brow, alog_row, dskip_l, ssd_norm, sel_t, sel, hs, ha)
    ddtf_b = ddtf.astype(BF16)

    dsegs = [dzs, dxbc, dzatt, dq, dk, dv, dgraw, ddtf_b]
    gsegs = [_mm_tn_rows(dsg, u, BF16, _tile(dsg.shape[1], (1024, 512, 256, 128)), _tile(p, (1408, tm)),
                         "wgrad_in_%d" % i) for i, dsg in enumerate(dsegs)]
    g_z, g_xbc, g_zatt, g_q, g_k, g_v, g_g, g_dtf = gsegs
    gw_full = jnp.concatenate([g_z, g_xbc, g_dtf[:hs], g_zatt, g_q, g_k, g_v, g_dtf[hs:hs + ha], g_g], axis=0)
    gwin_parts = gw_full.reshape(N_DEV, nc8, d)

    (sib_win,) = _exchange_sibling([gwin_parts], "scatter_grads_sibling")
    chip_win = _pair_add(gwin_parts, sib_win, core, "pair_add_w_in")
    sems, thru, lands, token = _exchange_chips_start([chip_win], "scatter_grads_start")
    dsegs_after = dsegs[:-1] + [ddtf_b + token[0:1, 0:1].astype(BF16)]
    gx, ghead, gnp = _dgrad_prenorm(dsegs_after, seg_w + [w_dtf], head, x2, norm_pre, dout, tm, "dgrad_in")
    own_slot = lambda got, sent: lax.dynamic_update_slice_in_dim(
        got, lax.dynamic_slice_in_dim(sent, chip, 1, axis=0), chip, axis=0)
    (sent,), (got,) = _exchange_chips_wait(sems, thru, lands, gnp, "scatter_grads_wait")
    recv_win = own_slot(got, sent)
    (r_sent,), (r_got,) = _exchange_chips_wait(r_sems, r_thru, r_lands, gnp, "scatter_rows_wait")
    recv_rows = own_slot(r_got, r_sent)
    gmisc = jnp.concatenate([gsm[0:1], gsm[1:2], gsm[2:3], _pad_cols(red_fwd[1:2, 0:1], LANES)], axis=1)
    small_g = jnp.concatenate([
        _pack_small_rep(gnp[0:1], red_fwd[0:1], red_bwd[0:1], gnrm[0:1], gcb[0:1], gmisc, cd),
        _pad_cols(gcw[0:CONV_K], cd), jnp.zeros((4, cd), F32), _pad_cols(ghead[PADN:], cd)], axis=0)
    sg_sems, sg_thru, sg_land, sg_token = _bcast_start(small_g, "reduce_small_start")

    upd_in = _adamw(jnp.transpose(w_in[0]) + sg_token[0:1, 0:1], recv_win, jnp.transpose(m_w_in[0]),
                    jnp.transpose(v_w_in[0]), "adamw_w_in", parts=True)
    upd_ps = _adamw(w_proj_ssd[0] + sg_token[0:1, 0:1], recv_rows, m_w_proj_ssd[0], v_w_proj_ssd[0],
                    "adamw_w_proj_ssd", parts=True, part_row0=0)
    upd_pa = _adamw(w_proj_att[0], recv_rows, m_w_proj_att[0], v_w_proj_att[0], "adamw_w_proj_att", parts=True,
                    part_row0=r1)
    upd_out = _adamw(w_out[0], recv_rows, m_w_out[0], v_w_out[0], "adamw_w_out", parts=True, part_row0=r1 + r2)
    all_done = upd_in[1][0:8, 0:LANES] + upd_ps[1][0:8, 0:LANES] + upd_pa[1][0:8, 0:LANES] + upd_out[1][0:8, 0:LANES]
    red = _sum_slots(_bcast_wait(sg_sems, sg_thru, sg_land, all_done, "reduce_small_wait"), "reduce_small_sum")
    loss = red[5, 3 * LANES]
    g_conv_w = lax.dynamic_slice_in_dim(red[8:8 + CONV_K], me * cws, cws, axis=1)
    g_meta = lax.dynamic_slice_in_dim(red[16:16 + N_META, :d], me * msh, msh, axis=1)
    small = {
        "meta_tokens": (meta_tokens, m_meta_tokens, v_meta_tokens, g_meta),
        "norm_pre": (norm_pre, m_norm_pre, v_norm_pre, (0, 0)),
        "conv_w": (conv_w[0], m_conv_w[0], v_conv_w[0], g_conv_w),
        "conv_b": (conv_b, m_conv_b, v_conv_b, (4, 0)),
        "dt_bias": (dt_bias, m_dt_bias, v_dt_bias, (5, 0)),
        "a_log": (a_log, m_a_log, v_a_log, (5, LANES)),
        "d_skip": (d_skip, m_d_skip, v_d_skip, (5, 2 * LANES)),
        "ssd_norm": (ssd_norm, m_ssd_norm, v_ssd_norm, (3, 0)),
        "fgate_bias": (fgate_bias, m_fgate_bias, v_fgate_bias, (5, hs)),
        "gate_bias": (gate_bias, m_gate_bias, v_gate_bias, (2, 0)),
        "norm_post": (norm_post, m_norm_post, v_norm_post, (1, 0)),
    }
    upd_small = _adamw_small(small, red, "adamw_small")

    def leaves(i):
        sm = {k: v[i] for k, v in upd_small.items()}
        return [sm["meta_tokens"], sm["norm_pre"], jnp.transpose(upd_in[i])[None], sm["conv_w"][None], sm["conv_b"],
                sm["dt_bias"], sm["a_log"], sm["d_skip"], sm["ssd_norm"], sm["fgate_bias"], sm["gate_bias"],
                upd_ps[i][None], upd_pa[i][None], upd_out[i][None], sm["norm_post"]]

    return tuple([loss, gx[None]] + leaves(0) + leaves(1) + leaves(2) + leaves(3))
```
